```python
import jax
import jax.numpy as jnp
from jax import lax
import numpy as np

D_MODEL = 1024
BATCH = 8
SEQ = 2048
DEPTH = 2

CTX_LEN = 256
GRID_W = 64
N_MIXERS = 2
N_REC_LAYERS = (DEPTH + 1) // 2
N_CONV_LAYERS = DEPTH // 2
D_RNN = (4 * D_MODEL // 3) // 128 * 128
N_RNN_BLOCKS = 16
RNN_BLOCK = D_RNN // N_RNN_BLOCKS
REC_CONV_W = 4
REC_CONV_PAD = (1, 2)
RG_C = 8.0
CONF_KW = 31
CONF_PAD = (CONF_KW // 2, CONF_KW // 2)
D_FF = 4 * D_MODEL
N_MOD = 6
EPS = 1e-6
POS_BASE = 10000.0

kernel_name = 'hybrid_rglru_conformer_dit_block'


def rmsnorm(x, g):
    xf = x.astype(jnp.float32)
    y = xf * lax.rsqrt(jnp.mean(xf * xf, axis=-1, keepdims=True) + EPS)
    return (y * g.astype(jnp.float32)).astype(x.dtype)


def layernorm(x, g, b):
    xf = x.astype(jnp.float32)
    mu = jnp.mean(xf, axis=-1, keepdims=True)
    var = jnp.mean(jnp.square(xf - mu), axis=-1, keepdims=True)
    y = (xf - mu) * lax.rsqrt(var + EPS)
    return (y * g.astype(jnp.float32) + b.astype(jnp.float32)).astype(x.dtype)


def modulate(h, shift, scale):
    return h * (1 + scale) + shift


def grid_pos_embed(rows, d, dtype):
    t = jnp.arange(rows * GRID_W, dtype=jnp.int32)
    row = (t // GRID_W).astype(jnp.float32)
    col = (t % GRID_W).astype(jnp.float32)
    q = d // 4
    omega = 1.0 / (POS_BASE ** (jnp.arange(q, dtype=jnp.float32) / q))
    er = row[:, None] * omega[None, :]
    ec = col[:, None] * omega[None, :]
    return jnp.concatenate([jnp.sin(er), jnp.cos(er), jnp.sin(ec), jnp.cos(ec)], axis=-1).astype(dtype)


def dwconv(x, w, b, pad):
    y = lax.conv_general_dilated(x, w[:, None, :].astype(x.dtype), window_strides=(1,), padding=[pad],
                                 dimension_numbers=('NWC', 'WIO', 'NWC'), feature_group_count=x.shape[-1])
    return y + b.astype(x.dtype)


def sq_relu_mlp(h, w_in, w_out):
    return jnp.square(jax.nn.relu(h @ w_in)) @ w_out


def block_diag(u, w, b):
    ub = u.reshape(u.shape[:-1] + (N_RNN_BLOCKS, RNN_BLOCK))
    y = jnp.einsum('bthi,hij->bthj', ub, w.astype(jnp.float32)) + b.astype(jnp.float32)
    return y.reshape(u.shape)


def rglru_coeffs(u, lam, w_a, b_a, w_x, b_x):
    uf = u.astype(jnp.float32)
    r = jax.nn.sigmoid(block_diag(uf, w_a, b_a))
    i = jax.nn.sigmoid(block_diag(uf, w_x, b_x))
    log_a = -RG_C * r * jax.nn.softplus(-lam.astype(jnp.float32))
    a = jnp.exp(log_a)
    return a, jnp.sqrt(-jnp.expm1(2.0 * log_a)) * (i * uf)


def linear_scan(a, b, h0, reverse):
    def combine(l, r):
        al, bl = l
        ar, br = r
        return al * ar, ar * bl + br
    a_cum, b_cum = lax.associative_scan(combine, (a, b), axis=1, reverse=reverse)
    return a_cum * h0[:, None, :] + b_cum


def recurrent_block(h_lat, h_ctx, w_in, conv_w, conv_b, lam, w_a, b_a, w_x, b_x, w_out, ctx_out):
    w_gate, w_rec = w_in[:, :D_RNN], w_in[:, D_RNN:]
    u_lat = dwconv(h_lat @ w_rec, conv_w, conv_b, REC_CONV_PAD)
    u_ctx = dwconv(h_ctx @ w_rec, conv_w, conv_b, REC_CONV_PAD)
    zeros = jnp.zeros((h_lat.shape[0], D_RNN), jnp.float32)
    ys_lat, ys_ctx = [], []
    for d, rev in enumerate((False, True)):
        a_c, b_c = rglru_coeffs(u_ctx, lam[d], w_a[d], b_a[d], w_x[d], b_x[d])
        s_ctx = linear_scan(a_c, b_c, zeros, rev)
        h0 = s_ctx[:, 0] if rev else s_ctx[:, -1]
        a_l, b_l = rglru_coeffs(u_lat, lam[d], w_a[d], b_a[d], w_x[d], b_x[d])
        ys_lat.append(linear_scan(a_l, b_l, h0, rev))
        ys_ctx.append(s_ctx)
    y_lat = (ys_lat[0] + ys_lat[1]).astype(h_lat.dtype)
    out_lat = (jax.nn.gelu(h_lat @ w_gate) * y_lat) @ w_out
    if not ctx_out:
        return out_lat, None
    y_ctx = (ys_ctx[0] + ys_ctx[1]).astype(h_ctx.dtype)
    out_ctx = (jax.nn.gelu(h_ctx @ w_gate) * y_ctx) @ w_out
    return out_lat, out_ctx


def conformer_conv(h, w_pw1, b_pw1, conv_w, conv_b, ln_g, ln_b, w_pw2, b_pw2):
    z = jax.nn.glu(h @ w_pw1 + b_pw1, axis=-1)
    z = dwconv(z, conv_w, conv_b, CONF_PAD)
    z = jax.nn.silu(layernorm(z, ln_g, ln_b))
    return z @ w_pw2 + b_pw2


def _fwd_setup_inputs(seed: int = 0) -> dict:
    key = jax.random.key(seed)
    ks = jax.random.split(key, 32)
    f32 = jnp.float32

    def nrm(k, shape, scale):
        return jax.random.normal(k, shape, f32) * scale

    x = nrm(ks[0], (BATCH, SEQ, D_MODEL), 1.0)
    c = nrm(ks[1], (BATCH, D_MODEL), 1.0)
    ctx = nrm(ks[2], (BATCH, CTX_LEN, D_MODEL), 1.0)
    c_ctx = nrm(ks[3], (D_MODEL,), 1.0)
    w_ada = nrm(ks[4], (DEPTH, D_MODEL, N_MOD * D_MODEL), 0.5 * D_MODEL ** -0.5)
    b_ada = nrm(ks[5], (DEPTH, N_MOD * D_MODEL), 0.02)
    norm_g = 1.0 + nrm(ks[6], (DEPTH, 2, D_MODEL), 0.05)
    rec_w_in = nrm(ks[7], (N_REC_LAYERS, D_MODEL, 2 * D_RNN), D_MODEL ** -0.5)
    rec_conv_w = nrm(ks[8], (N_REC_LAYERS, REC_CONV_W, D_RNN), REC_CONV_W ** -0.5)
    rec_conv_b = nrm(ks[9], (N_REC_LAYERS, D_RNN), 0.02)
    u = jax.random.uniform(ks[10], (N_REC_LAYERS, 2, D_RNN), f32, 0.9, 0.999)
    a_base = u ** (1.0 / RG_C)
    rec_lambda = jnp.log(a_base) - jnp.log1p(-a_base)
    rec_w_a = nrm(ks[11], (N_REC_LAYERS, 2, N_RNN_BLOCKS, RNN_BLOCK, RNN_BLOCK), RNN_BLOCK ** -0.5)
    rec_b_a = nrm(ks[12], (N_REC_LAYERS, 2, N_RNN_BLOCKS, RNN_BLOCK), 0.02)
    rec_w_x = nrm(ks[13], (N_REC_LAYERS, 2, N_RNN_BLOCKS, RNN_BLOCK, RNN_BLOCK), RNN_BLOCK ** -0.5)
    rec_b_x = nrm(ks[14], (N_REC_LAYERS, 2, N_RNN_BLOCKS, RNN_BLOCK), 0.02)
    rec_w_out = nrm(ks[15], (N_REC_LAYERS, D_RNN, D_MODEL), D_RNN ** -0.5)
    conf_w_pw1 = nrm(ks[16], (N_CONV_LAYERS, D_MODEL, 2 * D_MODEL), D_MODEL ** -0.5)
    conf_b_pw1 = nrm(ks[17], (N_CONV_LAYERS, 2 * D_MODEL), 0.02)
    conf_conv_w = nrm(ks[18], (N_CONV_LAYERS, CONF_KW, D_MODEL), CONF_KW ** -0.5)
    conf_conv_b = nrm(ks[19], (N_CONV_LAYERS, D_MODEL), 0.02)
    conf_ln_g = 1.0 + nrm(ks[20], (N_CONV_LAYERS, D_MODEL), 0.05)
    conf_ln_b = nrm(ks[21], (N_CONV_LAYERS, D_MODEL), 0.02)
    conf_w_pw2 = nrm(ks[22], (N_CONV_LAYERS, D_MODEL, D_MODEL), D_MODEL ** -0.5)
    conf_b_pw2 = nrm(ks[23], (N_CONV_LAYERS, D_MODEL), 0.02)
    mlp_w_in = nrm(ks[24], (DEPTH, D_MODEL, D_FF), D_MODEL ** -0.5)
    mlp_w_out = nrm(ks[25], (DEPTH, D_FF, D_MODEL), D_FF ** -0.5)
    final_g = 1.0 + nrm(ks[26], (D_MODEL,), 0.05)
    return {'x': x, 'c': c, 'ctx': ctx, 'c_ctx': c_ctx, 'w_ada': w_ada, 'b_ada': b_ada, 'norm_g': norm_g,
            'rec_w_in': rec_w_in, 'rec_conv_w': rec_conv_w, 'rec_conv_b': rec_conv_b, 'rec_lambda': rec_lambda,
            'rec_w_a': rec_w_a, 'rec_b_a': rec_b_a, 'rec_w_x': rec_w_x, 'rec_b_x': rec_b_x, 'rec_w_out': rec_w_out,
            'conf_w_pw1': conf_w_pw1, 'conf_b_pw1': conf_b_pw1, 'conf_conv_w': conf_conv_w, 'conf_conv_b': conf_conv_b,
            'conf_ln_g': conf_ln_g, 'conf_ln_b': conf_ln_b, 'conf_w_pw2': conf_w_pw2, 'conf_b_pw2': conf_b_pw2,
            'mlp_w_in': mlp_w_in, 'mlp_w_out': mlp_w_out, 'final_g': final_g}


def _fwd_reference(x, c, ctx, c_ctx, w_ada, b_ada, norm_g, rec_w_in, rec_conv_w, rec_conv_b, rec_lambda,
              rec_w_a, rec_b_a, rec_w_x, rec_b_x, rec_w_out, conf_w_pw1, conf_b_pw1, conf_conv_w, conf_conv_b,
              conf_ln_g, conf_ln_b, conf_w_pw2, conf_b_pw2, mlp_w_in, mlp_w_out, final_g):
    rows = x.shape[1] // GRID_W
    x = x + grid_pos_embed(rows, x.shape[-1], x.dtype)[None]
    xc = ctx
    last_ctx_layer = ((DEPTH - 1) // N_MIXERS) * N_MIXERS
    s_c = jax.nn.silu(c)
    s_cc = jax.nn.silu(c_ctx)
    for i in range(DEPTH):
        sh1, sc1, g1, sh2, sc2, g2 = jnp.split((s_c @ w_ada[i] + b_ada[i])[:, None, :], N_MOD, axis=-1)
        use_ctx = i <= last_ctx_layer
        ctx_out = i < last_ctx_layer
        h = modulate(rmsnorm(x, norm_g[i, 0]), sh1, sc1)
        hc = None
        if use_ctx:
            csh1, csc1, cg1, csh2, csc2, cg2 = jnp.split((s_cc @ w_ada[i] + b_ada[i])[None, None, :], N_MOD, axis=-1)
            hc = modulate(rmsnorm(xc, norm_g[i, 0]), csh1, csc1)
        j = i // N_MIXERS
        if i % N_MIXERS == 0:
            y, yc = recurrent_block(h, hc, rec_w_in[j], rec_conv_w[j], rec_conv_b[j], rec_lambda[j],
                                    rec_w_a[j], rec_b_a[j], rec_w_x[j], rec_b_x[j], rec_w_out[j], ctx_out)
        else:
            conf_p = (conf_w_pw1[j], conf_b_pw1[j], conf_conv_w[j], conf_conv_b[j],
                      conf_ln_g[j], conf_ln_b[j], conf_w_pw2[j], conf_b_pw2[j])
            y = conformer_conv(h, *conf_p)
            yc = conformer_conv(hc, *conf_p) if ctx_out else None
        x = x + g1 * y
        x = x + g2 * sq_relu_mlp(modulate(rmsnorm(x, norm_g[i, 1]), sh2, sc2), mlp_w_in[i], mlp_w_out[i])
        if ctx_out:
            xc = xc + cg1 * yc
            xc = xc + cg2 * sq_relu_mlp(modulate(rmsnorm(xc, norm_g[i, 1]), csh2, csc2), mlp_w_in[i], mlp_w_out[i])
    return rmsnorm(x, final_g)


import jax as _jax
import jax.numpy as _jnp

TWIN_FORMAT = 'train_step'
FWD_PARAMS = ['x', 'c', 'ctx', 'c_ctx', 'w_ada', 'b_ada', 'norm_g', 'rec_w_in', 'rec_conv_w', 'rec_conv_b', 'rec_lambda', 'rec_w_a', 'rec_b_a', 'rec_w_x', 'rec_b_x', 'rec_w_out', 'conf_w_pw1', 'conf_b_pw1', 'conf_conv_w', 'conf_conv_b', 'conf_ln_g', 'conf_ln_b', 'conf_w_pw2', 'conf_b_pw2', 'mlp_w_in', 'mlp_w_out', 'final_g']
TWIN_WEIGHTS = ['c_ctx', 'w_ada', 'b_ada', 'norm_g', 'rec_w_in', 'rec_conv_w', 'rec_conv_b', 'rec_lambda', 'rec_w_a', 'rec_b_a', 'rec_w_x', 'rec_b_x', 'rec_w_out', 'conf_w_pw1', 'conf_b_pw1', 'conf_conv_w', 'conf_conv_b', 'conf_ln_g', 'conf_ln_b', 'conf_w_pw2', 'conf_b_pw2', 'mlp_w_in', 'mlp_w_out', 'final_g']
TWIN_DIFF_INPUT = 'x'
TWIN_INPUTS = ['x', 'c', 'ctx', 'c_ctx', 'w_ada', 'b_ada', 'norm_g', 'rec_w_in', 'rec_conv_w', 'rec_conv_b', 'rec_lambda', 'rec_w_a', 'rec_b_a', 'rec_w_x', 'rec_b_x', 'rec_w_out', 'conf_w_pw1', 'conf_b_pw1', 'conf_conv_w', 'conf_conv_b', 'conf_ln_g', 'conf_ln_b', 'conf_w_pw2', 'conf_b_pw2', 'mlp_w_in', 'mlp_w_out', 'final_g', 'loss_target', 'm_c_ctx', 'm_w_ada', 'm_b_ada', 'm_norm_g', 'm_rec_w_in', 'm_rec_conv_w', 'm_rec_conv_b', 'm_rec_lambda', 'm_rec_w_a', 'm_rec_b_a', 'm_rec_w_x', 'm_rec_b_x', 'm_rec_w_out', 'm_conf_w_pw1', 'm_conf_b_pw1', 'm_conf_conv_w', 'm_conf_conv_b', 'm_conf_ln_g', 'm_conf_ln_b', 'm_conf_w_pw2', 'm_conf_b_pw2', 'm_mlp_w_in', 'm_mlp_w_out', 'm_final_g', 'v_c_ctx', 'v_w_ada', 'v_b_ada', 'v_norm_g', 'v_rec_w_in', 'v_rec_conv_w', 'v_rec_conv_b', 'v_rec_lambda', 'v_rec_w_a', 'v_rec_b_a', 'v_rec_w_x', 'v_rec_b_x', 'v_rec_w_out', 'v_conf_w_pw1', 'v_conf_b_pw1', 'v_conf_conv_w', 'v_conf_conv_b', 'v_conf_ln_g', 'v_conf_ln_b', 'v_conf_w_pw2', 'v_conf_b_pw2', 'v_mlp_w_in', 'v_mlp_w_out', 'v_final_g']
TWIN_OUTPUTS = ['loss', 'grad_x', 'grad_c_ctx', 'grad_w_ada', 'grad_b_ada', 'grad_norm_g', 'grad_rec_w_in', 'grad_rec_conv_w', 'grad_rec_conv_b', 'grad_rec_lambda', 'grad_rec_w_a', 'grad_rec_b_a', 'grad_rec_w_x', 'grad_rec_b_x', 'grad_rec_w_out', 'grad_conf_w_pw1', 'grad_conf_b_pw1', 'grad_conf_conv_w', 'grad_conf_conv_b', 'grad_conf_ln_g', 'grad_conf_ln_b', 'grad_conf_w_pw2', 'grad_conf_b_pw2', 'grad_mlp_w_in', 'grad_mlp_w_out', 'grad_final_g', 'delta_c_ctx', 'delta_w_ada', 'delta_b_ada', 'delta_norm_g', 'delta_rec_w_in', 'delta_rec_conv_w', 'delta_rec_conv_b', 'delta_rec_lambda', 'delta_rec_w_a', 'delta_rec_b_a', 'delta_rec_w_x', 'delta_rec_b_x', 'delta_rec_w_out', 'delta_conf_w_pw1', 'delta_conf_b_pw1', 'delta_conf_conv_w', 'delta_conf_conv_b', 'delta_conf_ln_g', 'delta_conf_ln_b', 'delta_conf_w_pw2', 'delta_conf_b_pw2', 'delta_mlp_w_in', 'delta_mlp_w_out', 'delta_final_g', 'new_m_c_ctx', 'new_m_w_ada', 'new_m_b_ada', 'new_m_norm_g', 'new_m_rec_w_in', 'new_m_rec_conv_w', 'new_m_rec_conv_b', 'new_m_rec_lambda', 'new_m_rec_w_a', 'new_m_rec_b_a', 'new_m_rec_w_x', 'new_m_rec_b_x', 'new_m_rec_w_out', 'new_m_conf_w_pw1', 'new_m_conf_b_pw1', 'new_m_conf_conv_w', 'new_m_conf_conv_b', 'new_m_conf_ln_g', 'new_m_conf_ln_b', 'new_m_conf_w_pw2', 'new_m_conf_b_pw2', 'new_m_mlp_w_in', 'new_m_mlp_w_out', 'new_m_final_g', 'new_v_c_ctx', 'new_v_w_ada', 'new_v_b_ada', 'new_v_norm_g', 'new_v_rec_w_in', 'new_v_rec_conv_w', 'new_v_rec_conv_b', 'new_v_rec_lambda', 'new_v_rec_w_a', 'new_v_rec_b_a', 'new_v_rec_w_x', 'new_v_rec_b_x', 'new_v_rec_w_out', 'new_v_conf_w_pw1', 'new_v_conf_b_pw1', 'new_v_conf_conv_w', 'new_v_conf_conv_b', 'new_v_conf_ln_g', 'new_v_conf_ln_b', 'new_v_conf_w_pw2', 'new_v_conf_b_pw2', 'new_v_mlp_w_in', 'new_v_mlp_w_out', 'new_v_final_g']
TWIN_LEAF_KINDS = {'loss': 'loss', 'grad_x': 'grad_x', 'grad_c_ctx': 'grad_w', 'grad_w_ada': 'grad_w', 'grad_b_ada': 'grad_w', 'grad_norm_g': 'grad_w', 'grad_rec_w_in': 'grad_w', 'grad_rec_conv_w': 'grad_w', 'grad_rec_conv_b': 'grad_w', 'grad_rec_lambda': 'grad_w', 'grad_rec_w_a': 'grad_w', 'grad_rec_b_a': 'grad_w', 'grad_rec_w_x': 'grad_w', 'grad_rec_b_x': 'grad_w', 'grad_rec_w_out': 'grad_w', 'grad_conf_w_pw1': 'grad_w', 'grad_conf_b_pw1': 'grad_w', 'grad_conf_conv_w': 'grad_w', 'grad_conf_conv_b': 'grad_w', 'grad_conf_ln_g': 'grad_w', 'grad_conf_ln_b': 'grad_w', 'grad_conf_w_pw2': 'grad_w', 'grad_conf_b_pw2': 'grad_w', 'grad_mlp_w_in': 'grad_w', 'grad_mlp_w_out': 'grad_w', 'grad_final_g': 'grad_w', 'delta_c_ctx': 'delta_w', 'delta_w_ada': 'delta_w', 'delta_b_ada': 'delta_w', 'delta_norm_g': 'delta_w', 'delta_rec_w_in': 'delta_w', 'delta_rec_conv_w': 'delta_w', 'delta_rec_conv_b': 'delta_w', 'delta_rec_lambda': 'delta_w', 'delta_rec_w_a': 'delta_w', 'delta_rec_b_a': 'delta_w', 'delta_rec_w_x': 'delta_w', 'delta_rec_b_x': 'delta_w', 'delta_rec_w_out': 'delta_w', 'delta_conf_w_pw1': 'delta_w', 'delta_conf_b_pw1': 'delta_w', 'delta_conf_conv_w': 'delta_w', 'delta_conf_conv_b': 'delta_w', 'delta_conf_ln_g': 'delta_w', 'delta_conf_ln_b': 'delta_w', 'delta_conf_w_pw2': 'delta_w', 'delta_conf_b_pw2': 'delta_w', 'delta_mlp_w_in': 'delta_w', 'delta_mlp_w_out': 'delta_w', 'delta_final_g': 'delta_w', 'new_m_c_ctx': 'new_m', 'new_m_w_ada': 'new_m', 'new_m_b_ada': 'new_m', 'new_m_norm_g': 'new_m', 'new_m_rec_w_in': 'new_m', 'new_m_rec_conv_w': 'new_m', 'new_m_rec_conv_b': 'new_m', 'new_m_rec_lambda': 'new_m', 'new_m_rec_w_a': 'new_m', 'new_m_rec_b_a': 'new_m', 'new_m_rec_w_x': 'new_m', 'new_m_rec_b_x': 'new_m', 'new_m_rec_w_out': 'new_m', 'new_m_conf_w_pw1': 'new_m', 'new_m_conf_b_pw1': 'new_m', 'new_m_conf_conv_w': 'new_m', 'new_m_conf_conv_b': 'new_m', 'new_m_conf_ln_g': 'new_m', 'new_m_conf_ln_b': 'new_m', 'new_m_conf_w_pw2': 'new_m', 'new_m_conf_b_pw2': 'new_m', 'new_m_mlp_w_in': 'new_m', 'new_m_mlp_w_out': 'new_m', 'new_m_final_g': 'new_m', 'new_v_c_ctx': 'new_v', 'new_v_w_ada': 'new_v', 'new_v_b_ada': 'new_v', 'new_v_norm_g': 'new_v', 'new_v_rec_w_in': 'new_v', 'new_v_rec_conv_w': 'new_v', 'new_v_rec_conv_b': 'new_v', 'new_v_rec_lambda': 'new_v', 'new_v_rec_w_a': 'new_v', 'new_v_rec_b_a': 'new_v', 'new_v_rec_w_x': 'new_v', 'new_v_rec_b_x': 'new_v', 'new_v_rec_w_out': 'new_v', 'new_v_conf_w_pw1': 'new_v', 'new_v_conf_b_pw1': 'new_v', 'new_v_conf_conv_w': 'new_v', 'new_v_conf_conv_b': 'new_v', 'new_v_conf_ln_g': 'new_v', 'new_v_conf_ln_b': 'new_v', 'new_v_conf_w_pw2': 'new_v', 'new_v_conf_b_pw2': 'new_v', 'new_v_mlp_w_in': 'new_v', 'new_v_mlp_w_out': 'new_v', 'new_v_final_g': 'new_v'}


def _forward(args):
    return _fwd_reference(*[args[k] for k in FWD_PARAMS])


def _output_shape():
    out = _jax.eval_shape(lambda: _forward(_fwd_setup_inputs(0)))
    return out.shape, out.dtype

N_MICROBATCH = 1
ADAM_LR = 0.001
ADAM_B1 = 0.9
ADAM_B2 = 0.999
ADAM_EPS = 1e-08
ADAM_WD = 0.01
ADAM_STEP = 10
PER_EXAMPLE_BATCH_AXIS = {'x': 0, 'c': 0, 'ctx': 0, 'loss_target': 0}
SHARED_INPUTS = []
_WEIGHT_DTYPES = {'c_ctx': _jnp.float32, 'w_ada': _jnp.float32, 'b_ada': _jnp.float32, 'norm_g': _jnp.float32, 'rec_w_in': _jnp.float32, 'rec_conv_w': _jnp.float32, 'rec_conv_b': _jnp.float32, 'rec_lambda': _jnp.float32, 'rec_w_a': _jnp.float32, 'rec_b_a': _jnp.float32, 'rec_w_x': _jnp.float32, 'rec_b_x': _jnp.float32, 'rec_w_out': _jnp.float32, 'conf_w_pw1': _jnp.float32, 'conf_b_pw1': _jnp.float32, 'conf_conv_w': _jnp.float32, 'conf_conv_b': _jnp.float32, 'conf_ln_g': _jnp.float32, 'conf_ln_b': _jnp.float32, 'conf_w_pw2': _jnp.float32, 'conf_b_pw2': _jnp.float32, 'mlp_w_in': _jnp.float32, 'mlp_w_out': _jnp.float32, 'final_g': _jnp.float32}
MOMENT_SCALE = {'c_ctx': 2.638651e-02, 'w_ada': 2.902293e-01, 'b_ada': 5.234988e-01, 'norm_g': 1.568003e-01, 'rec_w_in': 2.120652e-01, 'rec_conv_w': 2.791063e-01, 'rec_conv_b': 4.559965e-01, 'rec_lambda': 5.955811e-02, 'rec_w_a': 1.230285e-02, 'rec_b_a': 2.139212e-02, 'rec_w_x': 3.108979e-02, 'rec_b_x': 5.134852e-02, 'rec_w_out': 2.754968e-01, 'conf_w_pw1': 1.294847e-02, 'conf_b_pw1': 1.712547e-02, 'conf_conv_w': 1.733071e-02, 'conf_conv_b': 4.501149e-02, 'conf_ln_g': 2.229966e-02, 'conf_ln_b': 2.670478e-02, 'conf_w_pw2': 1.953712e-02, 'conf_b_pw2': 5.342014e-02, 'mlp_w_in': 2.369708e-02, 'mlp_w_out': 5.351174e-02, 'final_g': 1.723146e+01}


def _to_microbatches(a, axis):
    t = _jnp.moveaxis(a, axis, 0)
    t = t.reshape((N_MICROBATCH, t.shape[0] // N_MICROBATCH) + t.shape[1:])
    return _jnp.moveaxis(t, 1, axis + 1)


def setup_inputs(seed: int = 0) -> dict:
    inp = _fwd_setup_inputs(seed)
    key = _jax.random.fold_in(_jax.random.key(seed), 7919)
    shape, _ = _output_shape()
    out = dict(inp)
    out["loss_target"] = _jax.random.normal(_jax.random.fold_in(key, 0), shape, _jnp.float32)
    for i, name in enumerate(TWIN_WEIGHTS):
        w = inp[name].astype(_jnp.float32)
        if MOMENT_SCALE is None:
            s = _jnp.sqrt(_jnp.mean(_jnp.square(w)) + 1e-30)
        else:
            s = MOMENT_SCALE[name]
        km, kv = _jax.random.split(_jax.random.fold_in(key, i + 1))
        out[name] = w
        out["m_" + name] = s * _jax.random.normal(km, w.shape, _jnp.float32)
        out["v_" + name] = (s * s) * _jax.random.uniform(kv, w.shape, _jnp.float32, 0.5, 1.5)
    if N_MICROBATCH > 1:
        for name, axis in PER_EXAMPLE_BATCH_AXIS.items():
            out[name] = _to_microbatches(out[name], axis)
    return {'x': out['x'], 'c': out['c'], 'ctx': out['ctx'], 'c_ctx': out['c_ctx'], 'w_ada': out['w_ada'], 'b_ada': out['b_ada'], 'norm_g': out['norm_g'], 'rec_w_in': out['rec_w_in'], 'rec_conv_w': out['rec_conv_w'], 'rec_conv_b': out['rec_conv_b'], 'rec_lambda': out['rec_lambda'], 'rec_w_a': out['rec_w_a'], 'rec_b_a': out['rec_b_a'], 'rec_w_x': out['rec_w_x'], 'rec_b_x': out['rec_b_x'], 'rec_w_out': out['rec_w_out'], 'conf_w_pw1': out['conf_w_pw1'], 'conf_b_pw1': out['conf_b_pw1'], 'conf_conv_w': out['conf_conv_w'], 'conf_conv_b': out['conf_conv_b'], 'conf_ln_g': out['conf_ln_g'], 'conf_ln_b': out['conf_ln_b'], 'conf_w_pw2': out['conf_w_pw2'], 'conf_b_pw2': out['conf_b_pw2'], 'mlp_w_in': out['mlp_w_in'], 'mlp_w_out': out['mlp_w_out'], 'final_g': out['final_g'], 'loss_target': out['loss_target'], 'm_c_ctx': out['m_c_ctx'], 'm_w_ada': out['m_w_ada'], 'm_b_ada': out['m_b_ada'], 'm_norm_g': out['m_norm_g'], 'm_rec_w_in': out['m_rec_w_in'], 'm_rec_conv_w': out['m_rec_conv_w'], 'm_rec_conv_b': out['m_rec_conv_b'], 'm_rec_lambda': out['m_rec_lambda'], 'm_rec_w_a': out['m_rec_w_a'], 'm_rec_b_a': out['m_rec_b_a'], 'm_rec_w_x': out['m_rec_w_x'], 'm_rec_b_x': out['m_rec_b_x'], 'm_rec_w_out': out['m_rec_w_out'], 'm_conf_w_pw1': out['m_conf_w_pw1'], 'm_conf_b_pw1': out['m_conf_b_pw1'], 'm_conf_conv_w': out['m_conf_conv_w'], 'm_conf_conv_b': out['m_conf_conv_b'], 'm_conf_ln_g': out['m_conf_ln_g'], 'm_conf_ln_b': out['m_conf_ln_b'], 'm_conf_w_pw2': out['m_conf_w_pw2'], 'm_conf_b_pw2': out['m_conf_b_pw2'], 'm_mlp_w_in': out['m_mlp_w_in'], 'm_mlp_w_out': out['m_mlp_w_out'], 'm_final_g': out['m_final_g'], 'v_c_ctx': out['v_c_ctx'], 'v_w_ada': out['v_w_ada'], 'v_b_ada': out['v_b_ada'], 'v_norm_g': out['v_norm_g'], 'v_rec_w_in': out['v_rec_w_in'], 'v_rec_conv_w': out['v_rec_conv_w'], 'v_rec_conv_b': out['v_rec_conv_b'], 'v_rec_lambda': out['v_rec_lambda'], 'v_rec_w_a': out['v_rec_w_a'], 'v_rec_b_a': out['v_rec_b_a'], 'v_rec_w_x': out['v_rec_w_x'], 'v_rec_b_x': out['v_rec_b_x'], 'v_rec_w_out': out['v_rec_w_out'], 'v_conf_w_pw1': out['v_conf_w_pw1'], 'v_conf_b_pw1': out['v_conf_b_pw1'], 'v_conf_conv_w': out['v_conf_conv_w'], 'v_conf_conv_b': out['v_conf_conv_b'], 'v_conf_ln_g': out['v_conf_ln_g'], 'v_conf_ln_b': out['v_conf_ln_b'], 'v_conf_w_pw2': out['v_conf_w_pw2'], 'v_conf_b_pw2': out['v_conf_b_pw2'], 'v_mlp_w_in': out['v_mlp_w_in'], 'v_mlp_w_out': out['v_mlp_w_out'], 'v_final_g': out['v_final_g']}


def _loss(weights, diff, rest, loss_target):
    with _jax.named_scope("forward"):
        args = {**rest, TWIN_DIFF_INPUT: diff, **{k: w.astype(_WEIGHT_DTYPES[k]) for k, w in weights.items()}}
        y = _forward(args)
    with _jax.named_scope("loss_head"):
        err = _jnp.square(y.astype(_jnp.float32) - loss_target)
        return 0.5 * _jnp.sum(_jnp.mean(err, axis=-1)) if err.ndim else 0.5 * err


def _adamw(w, g, m, v):
    m = ADAM_B1 * m + (1.0 - ADAM_B1) * g
    v = ADAM_B2 * v + (1.0 - ADAM_B2) * _jnp.square(g)
    m_hat = m / (1.0 - ADAM_B1 ** ADAM_STEP)
    v_hat = v / (1.0 - ADAM_B2 ** ADAM_STEP)
    delta = -ADAM_LR * (m_hat / (_jnp.sqrt(v_hat) + ADAM_EPS) + ADAM_WD * w)
    return delta, m, v


def reference(x, c, ctx, c_ctx, w_ada, b_ada, norm_g, rec_w_in, rec_conv_w, rec_conv_b, rec_lambda, rec_w_a, rec_b_a, rec_w_x, rec_b_x, rec_w_out, conf_w_pw1, conf_b_pw1, conf_conv_w, conf_conv_b, conf_ln_g, conf_ln_b, conf_w_pw2, conf_b_pw2, mlp_w_in, mlp_w_out, final_g, loss_target, m_c_ctx, m_w_ada, m_b_ada, m_norm_g, m_rec_w_in, m_rec_conv_w, m_rec_conv_b, m_rec_lambda, m_rec_w_a, m_rec_b_a, m_rec_w_x, m_rec_b_x, m_rec_w_out, m_conf_w_pw1, m_conf_b_pw1, m_conf_conv_w, m_conf_conv_b, m_conf_ln_g, m_conf_ln_b, m_conf_w_pw2, m_conf_b_pw2, m_mlp_w_in, m_mlp_w_out, m_final_g, v_c_ctx, v_w_ada, v_b_ada, v_norm_g, v_rec_w_in, v_rec_conv_w, v_rec_conv_b, v_rec_lambda, v_rec_w_a, v_rec_b_a, v_rec_w_x, v_rec_b_x, v_rec_w_out, v_conf_w_pw1, v_conf_b_pw1, v_conf_conv_w, v_conf_conv_b, v_conf_ln_g, v_conf_ln_b, v_conf_w_pw2, v_conf_b_pw2, v_mlp_w_in, v_mlp_w_out, v_final_g):
    given = dict(x=x, c=c, ctx=ctx, c_ctx=c_ctx, w_ada=w_ada, b_ada=b_ada, norm_g=norm_g, rec_w_in=rec_w_in, rec_conv_w=rec_conv_w, rec_conv_b=rec_conv_b, rec_lambda=rec_lambda, rec_w_a=rec_w_a, rec_b_a=rec_b_a, rec_w_x=rec_w_x, rec_b_x=rec_b_x, rec_w_out=rec_w_out, conf_w_pw1=conf_w_pw1, conf_b_pw1=conf_b_pw1, conf_conv_w=conf_conv_w, conf_conv_b=conf_conv_b, conf_ln_g=conf_ln_g, conf_ln_b=conf_ln_b, conf_w_pw2=conf_w_pw2, conf_b_pw2=conf_b_pw2, mlp_w_in=mlp_w_in, mlp_w_out=mlp_w_out, final_g=final_g, loss_target=loss_target, m_c_ctx=m_c_ctx, m_w_ada=m_w_ada, m_b_ada=m_b_ada, m_norm_g=m_norm_g, m_rec_w_in=m_rec_w_in, m_rec_conv_w=m_rec_conv_w, m_rec_conv_b=m_rec_conv_b, m_rec_lambda=m_rec_lambda, m_rec_w_a=m_rec_w_a, m_rec_b_a=m_rec_b_a, m_rec_w_x=m_rec_w_x, m_rec_b_x=m_rec_b_x, m_rec_w_out=m_rec_w_out, m_conf_w_pw1=m_conf_w_pw1, m_conf_b_pw1=m_conf_b_pw1, m_conf_conv_w=m_conf_conv_w, m_conf_conv_b=m_conf_conv_b, m_conf_ln_g=m_conf_ln_g, m_conf_ln_b=m_conf_ln_b, m_conf_w_pw2=m_conf_w_pw2, m_conf_b_pw2=m_conf_b_pw2, m_mlp_w_in=m_mlp_w_in, m_mlp_w_out=m_mlp_w_out, m_final_g=m_final_g, v_c_ctx=v_c_ctx, v_w_ada=v_w_ada, v_b_ada=v_b_ada, v_norm_g=v_norm_g, v_rec_w_in=v_rec_w_in, v_rec_conv_w=v_rec_conv_w, v_rec_conv_b=v_rec_conv_b, v_rec_lambda=v_rec_lambda, v_rec_w_a=v_rec_w_a, v_rec_b_a=v_rec_b_a, v_rec_w_x=v_rec_w_x, v_rec_b_x=v_rec_b_x, v_rec_w_out=v_rec_w_out, v_conf_w_pw1=v_conf_w_pw1, v_conf_b_pw1=v_conf_b_pw1, v_conf_conv_w=v_conf_conv_w, v_conf_conv_b=v_conf_conv_b, v_conf_ln_g=v_conf_ln_g, v_conf_ln_b=v_conf_ln_b, v_conf_w_pw2=v_conf_w_pw2, v_conf_b_pw2=v_conf_b_pw2, v_mlp_w_in=v_mlp_w_in, v_mlp_w_out=v_mlp_w_out, v_final_g=v_final_g)
    weights = {n: given[n] for n in TWIN_WEIGHTS}
    shared = {n: given[n] for n in SHARED_INPUTS}
    per_example = {n: given[n] for n in ['x', 'c', 'ctx']}
    grad_fn = _jax.value_and_grad(_loss, argnums=(0, 1))

    def one_microbatch(ex, loss_target):
        ex = dict(ex)
        diff = ex.pop(TWIN_DIFF_INPUT)
        return grad_fn(weights, diff, {**shared, **ex}, loss_target)

    if N_MICROBATCH == 1:
        loss, (grad_w, grad_x) = one_microbatch(per_example, given["loss_target"])
    else:
        def body(carry, xs):
            loss_sum, grad_sum = carry
            l_k, (gw_k, gx_k) = one_microbatch(xs[0], xs[1])
            with _jax.named_scope("update"):
                return (loss_sum + l_k, _jax.tree.map(_jnp.add, grad_sum, gw_k)), gx_k

        init = (_jnp.zeros((), _jnp.float32), _jax.tree.map(_jnp.zeros_like, weights))
        (loss, grad_w), grad_x = _jax.lax.scan(body, init, (per_example, given["loss_target"]))
    with _jax.named_scope("update"):
        delta_w, new_m, new_v = {}, {}, {}
        for n in TWIN_WEIGHTS:
            delta_w[n], new_m[n], new_v[n] = _adamw(weights[n], grad_w[n], given["m_" + n], given["v_" + n])
    return (loss, grad_x, *[grad_w[n] for n in TWIN_WEIGHTS], *[delta_w[n] for n in TWIN_WEIGHTS],
            *[new_m[n] for n in TWIN_WEIGHTS], *[new_v[n] for n in TWIN_WEIGHTS])
```

```python
import functools

import jax
import jax.numpy as jnp
from jax import lax
from jax.experimental import pallas as pl
from jax.experimental.pallas import tpu as pltpu

f32 = jnp.float32
bf16 = jnp.bfloat16

N_DEV = 8
D = 1024
T_LAT = 2048
T_CTX = 256
T_ALL = T_CTX + T_LAT
R = 1280
N_BLK = 16
BLK = R // N_BLK
F = 4096
GRID_W = 64
RG_C = 8.0
EPS = 1e-6
POS_BASE = 10000.0
N_MOD = 6
ADA_SHARD = N_MOD * D // N_DEV

ADAM_LR = 0.001
ADAM_B1 = 0.9
ADAM_B2 = 0.999
ADAM_EPS = 1e-08
ADAM_WD = 0.01
ADAM_STEP = 10

VMEM_LIMIT_V7X = 56 * 1024 * 1024
HALO = 16
MESH = pl.DeviceIdType.MESH


def _cparams(*sem):
    return pltpu.CompilerParams(dimension_semantics=sem, vmem_limit_bytes=VMEM_LIMIT_V7X)


def _pick(n, cands):
    for c in cands:
        if n % c == 0:
            return c
    raise ValueError(f"no block size for {n}")


def _position():
    x, y, c = lax.axis_index("x"), lax.axis_index("y"), lax.axis_index("c")
    return x, y, c, 4 * x + 2 * y + c


def _peer(x, y, c, k):
    px = (1 - x) if (k >> 2) & 1 else x
    py = (1 - y) if (k >> 1) & 1 else y
    pc = (1 - c) if k & 1 else c
    return (px, py, pc), 4 * px + 2 * py + pc


def _exchange(arrs, name, scatter):
    n = len(arrs)

    def body(*refs):
        ins, outs = refs[:n], refs[n:2 * n]
        send_sems, recv_sems, local_sems = refs[2 * n:]
        x, y, c, me = _position()
        local = []
        for a in range(n):
            src = ins[a].at[me] if scatter else ins[a]
            cp = pltpu.make_async_copy(src, outs[a].at[me], local_sems.at[a])
            cp.start()
            local.append(cp)
        sends, recvs = [], []
        for a in range(n):
            for k in range(1, N_DEV):
                peer, peer_lin = _peer(x, y, c, k)
                src = ins[a].at[peer_lin] if scatter else ins[a]
                cp = pltpu.make_async_remote_copy(
                    src_ref=src, dst_ref=outs[a].at[me], send_sem=send_sems.at[a, k - 1],
                    recv_sem=recv_sems.at[a, k - 1], device_id=peer, device_id_type=MESH)
                cp.start()
                sends.append(cp)
                recvs.append(pltpu.make_async_remote_copy(
                    src_ref=src, dst_ref=outs[a].at[peer_lin], send_sem=send_sems.at[a, k - 1],
                    recv_sem=recv_sems.at[a, k - 1], device_id=peer, device_id_type=MESH))
        for cp in recvs:
            cp.wait_recv()
        for cp in sends:
            cp.wait_send()
        for cp in local:
            cp.wait()

    if scatter:
        out_shape = [jax.ShapeDtypeStruct(a.shape, a.dtype) for a in arrs]
    else:
        out_shape = [jax.ShapeDtypeStruct((N_DEV,) + a.shape, a.dtype) for a in arrs]
    any_spec = pl.BlockSpec(memory_space=pl.ANY)
    return pl.pallas_call(
        body, name=name, out_shape=out_shape,
        in_specs=[any_spec] * n, out_specs=[any_spec] * n,
        scratch_shapes=[pltpu.SemaphoreType.DMA((n, N_DEV - 1)), pltpu.SemaphoreType.DMA((n, N_DEV - 1)),
                        pltpu.SemaphoreType.DMA((n,))],
    )(*arrs)


def _all_gather(arrs, name):
    return _exchange(arrs, name, scatter=False)


def _all_to_all(arrs, name):
    return _exchange(arrs, name, scatter=True)


def _mm(a, b, name, ta=False, tb=False, out_dtype=f32):
    if ta:
        k_dim, m_dim = a.shape
    else:
        m_dim, k_dim = a.shape
    if tb:
        n_dim, k2 = b.shape
    else:
        k2, n_dim = b.shape
    assert k_dim == k2, (a.shape, b.shape)
    assert a.dtype == bf16 and b.dtype == bf16
    bm = _pick(m_dim, (512, 768, 640, 256, 128))
    bn = _pick(n_dim, (512, 640, 256, 128))
    bk = _pick(k_dim, (1024, 1280, 768, 512))
    nk = k_dim // bk
    a_spec = (pl.BlockSpec((bk, bm), lambda i, j, k: (k, i)) if ta
              else pl.BlockSpec((bm, bk), lambda i, j, k: (i, k)))
    b_spec = (pl.BlockSpec((bn, bk), lambda i, j, k: (j, k)) if tb
              else pl.BlockSpec((bk, bn), lambda i, j, k: (k, j)))
    dims = (((0 if ta else 1,), (1 if tb else 0,)), ((), ()))

    def body(a_ref, b_ref, o_ref, acc_ref):
        k = pl.program_id(2)

        @pl.when(k == 0)
        def _():
            acc_ref[...] = jnp.zeros_like(acc_ref)

        acc_ref[...] += lax.dot_general(a_ref[...], b_ref[...], dims, preferred_element_type=f32)

        @pl.when(k == nk - 1)
        def _():
            o_ref[...] = acc_ref[...].astype(o_ref.dtype)

    return pl.pallas_call(
        body, name=name, out_shape=jax.ShapeDtypeStruct((m_dim, n_dim), out_dtype),
        grid=(m_dim // bm, n_dim // bn, nk), in_specs=[a_spec, b_spec],
        out_specs=pl.BlockSpec((bm, bn), lambda i, j, k: (i, j)),
        scratch_shapes=[pltpu.VMEM((bm, bn), f32)],
        compiler_params=_cparams("parallel", "parallel", "arbitrary"),
    )(a, b)


def _rin(arr, width=None, cb=0, roff=0):
    return (arr, arr.shape[1] if width is None else width, cb, roff)


def _rowcall(fn, name, rows, tm, row_ins, par_ins, row_outs, acc_outs=()):
    nr, npar, nro = len(row_ins), len(par_ins), len(row_outs)
    in_specs, args = [], []
    for arr, width, cb, roff in row_ins:
        if roff >= 0:
            imap = lambda i, cb=cb, roff=roff: (i + roff, cb)
        else:
            imap = lambda i, cb=cb, roff=roff: (jnp.maximum(i + roff, 0), cb)
        in_specs.append(pl.BlockSpec((tm, width), imap))
        args.append(arr)
    for p in par_ins:
        in_specs.append(pl.BlockSpec(p.shape, lambda i: (0, 0)))
        args.append(p)
    out_shape, out_specs = [], []
    for width, dt in row_outs:
        out_shape.append(jax.ShapeDtypeStruct((rows, width), dt))
        out_specs.append(pl.BlockSpec((tm, width), lambda i: (i, 0)))
    for p, width in acc_outs:
        out_shape.append(jax.ShapeDtypeStruct((p, width), f32))
        out_specs.append(pl.BlockSpec((p, width), lambda i: (0, 0)))

    def body(*refs):
        i = pl.program_id(0)
        res = fn(i, *[r[...] for r in refs[:nr + npar]])
        outs = refs[nr + npar:]
        for o, v in zip(outs[:nro], res[:nro]):
            o[...] = v.astype(o.dtype)
        if acc_outs:
            @pl.when(i == 0)
            def _():
                for o in outs[nro:]:
                    o[...] = jnp.zeros_like(o)

            for o, v in zip(outs[nro:], res[nro:]):
                o[...] += v

    return pl.pallas_call(
        body, name=name, out_shape=out_shape, grid=(rows // tm,), in_specs=in_specs, out_specs=out_specs,
        compiler_params=_cparams("arbitrary"),
    )(*args)


def _rms(x, g):
    return x * lax.rsqrt(jnp.mean(x * x, axis=-1, keepdims=True) + EPS) * g


def _normmod(x, g, sc, sh):
    return _rms(x, g) * (1.0 + sc) + sh


def _rows2(v0, v1):
    rid = lax.broadcasted_iota(jnp.int32, (2, v0.shape[1]), 0)
    return jnp.where(rid == 0, v0, v1)


def _gelu(x):
    return 0.5 * x * (1.0 + jnp.tanh(0.7978845608028654 * (x + 0.044715 * (x * x * x))))


def _sigmoid(x):
    return 1.0 / (1.0 + jnp.exp(-x))


def _coeff(pre_a, pre_x, u, ba, bx, lam):
    r = _sigmoid(pre_a + ba)
    ig = _sigmoid(pre_x + bx)
    nl = -lam
    sp = jnp.maximum(nl, 0.0) + jnp.log(1.0 + jnp.exp(-jnp.abs(nl)))
    la = -RG_C * r * sp
    a = jnp.exp(la)
    one_minus_a2 = -jnp.tanh(la) * (a * a + 1.0)
    return a, jnp.sqrt(one_minus_a2) * (ig * u)


def _scan_call(a, v, segments, name, backward):
    rows, width = a.shape
    cb = 256
    n_out = 1 if backward else 2

    def body(a_ref, v_ref, *outs):
        rid = lax.broadcasted_iota(jnp.int32, (8, cb), 0)
        state = jnp.zeros((1, cb), f32)
        for start, n, rev in segments:
            nt = n // 8

            def tile(j, st, start=start, nt=nt, rev=rev):
                t0 = pl.multiple_of(start + (nt - 1 - j if rev else j) * 8, 8)
                at = a_ref[pl.ds(t0, 8), :]
                vt = v_ref[pl.ds(t0, 8), :]
                out = jnp.zeros((8, cb), f32)
                prev = jnp.zeros((8, cb), f32)
                for i in (range(7, -1, -1) if rev else range(8)):
                    if backward:
                        g = vt[i:i + 1] + st
                        st = at[i:i + 1] * g
                        out = jnp.where(rid == i, g, out)
                    else:
                        prev = jnp.where(rid == i, st, prev)
                        st = at[i:i + 1] * st + vt[i:i + 1]
                        out = jnp.where(rid == i, st, out)
                outs[0][pl.ds(t0, 8), :] = out
                if not backward:
                    outs[1][pl.ds(t0, 8), :] = prev
                return st

            state = lax.fori_loop(0, nt, tile, state)

    spec = pl.BlockSpec((rows, cb), lambda j: (0, j))
    return pl.pallas_call(
        body, name=name, out_shape=[jax.ShapeDtypeStruct((rows, width), f32)] * n_out,
        grid=(width // cb,), in_specs=[spec, spec], out_specs=[spec] * n_out,
        compiler_params=_cparams("parallel"),
    )(a, v)


CONV_CHUNK = 256


def _fill_padded(pad_ref, src_ref, start, n):
    cb = pad_ref.shape[1]
    pad_ref[pl.ds(0, HALO), :] = jnp.zeros((HALO, cb), f32)
    pad_ref[pl.ds(HALO, n), :] = src_ref[pl.ds(start, n), :].astype(f32)
    pad_ref[pl.ds(HALO + n, HALO), :] = jnp.zeros((HALO, cb), f32)


def _dwconv_fwd(x, x_cb0, w, b, taps, pad_left, segments, cb, name, emit_bf16):
    rows = x.shape[0]
    width = w.shape[1]

    def body(x_ref, w_ref, b_ref, *rest):
        outs, xp = rest[:-1], rest[-1]
        for start, n in segments:
            _fill_padded(xp, x_ref, start, n)
            for c0 in range(0, n, CONV_CHUNK):
                acc = jnp.zeros((CONV_CHUNK, cb), f32) + b_ref[...]
                for k in range(taps):
                    acc = acc + w_ref[k:k + 1, :] * xp[pl.ds(HALO + c0 + k - pad_left, CONV_CHUNK), :]
                for o in outs:
                    o[pl.ds(start + c0, CONV_CHUNK), :] = acc.astype(o.dtype)

    out_dtypes = [f32, bf16] if emit_bf16 else [f32]
    return pl.pallas_call(
        body, name=name, out_shape=[jax.ShapeDtypeStruct((rows, width), dt) for dt in out_dtypes],
        grid=(width // cb,),
        in_specs=[pl.BlockSpec((rows, cb), lambda j: (0, j + x_cb0)), pl.BlockSpec((taps, cb), lambda j: (0, j)),
                  pl.BlockSpec((1, cb), lambda j: (0, j))],
        out_specs=[pl.BlockSpec((rows, cb), lambda j: (0, j))] * len(out_dtypes),
        scratch_shapes=[pltpu.VMEM((rows + 2 * HALO, cb), f32)],
        compiler_params=_cparams("parallel"),
    )(x, w, b)


def _dwconv_bwd(douts, x, x_cb0, w, taps, pad_left, segments, cb, name, dx_dtype):
    rows = x.shape[0]
    width = w.shape[1]
    nd = len(douts)

    def body(*refs):
        d_refs, x_ref, w_ref = refs[:nd], refs[nd], refs[nd + 1]
        dx_ref, dw_ref, db_ref, xp, dp, dsum = refs[nd + 2:]
        dw_ref[...] = jnp.zeros_like(dw_ref)
        db_ref[...] = jnp.zeros_like(db_ref)
        if nd > 1:
            total = d_refs[0][...]
            for r in d_refs[1:]:
                total = total + r[...]
            dsum[...] = total
            d_ref = dsum
        else:
            d_ref = d_refs[0]
        for start, n in segments:
            _fill_padded(xp, x_ref, start, n)
            _fill_padded(dp, d_ref, start, n)
            for c0 in range(0, n, CONV_CHUNK):
                dchunk = dp[pl.ds(HALO + c0, CONV_CHUNK), :]
                db_ref[...] += jnp.sum(dchunk, axis=0, keepdims=True)
                acc = jnp.zeros((CONV_CHUNK, cb), f32)
                for k in range(taps):
                    acc = acc + w_ref[k:k + 1, :] * dp[pl.ds(HALO + c0 + pad_left - k, CONV_CHUNK), :]
                    xs = xp[pl.ds(HALO + c0 + k - pad_left, CONV_CHUNK), :]
                    dw_ref[k:k + 1, :] += jnp.sum(dchunk * xs, axis=0, keepdims=True)
                dx_ref[pl.ds(start + c0, CONV_CHUNK), :] = acc.astype(dx_ref.dtype)

    dspec = pl.BlockSpec((rows, cb), lambda j: (0, j))
    return pl.pallas_call(
        body, name=name,
        out_shape=[jax.ShapeDtypeStruct((rows, width), dx_dtype), jax.ShapeDtypeStruct((taps, width), f32),
                   jax.ShapeDtypeStruct((1, width), f32)],
        grid=(width // cb,),
        in_specs=[dspec] * nd + [pl.BlockSpec((rows, cb), lambda j: (0, j + x_cb0)),
                                 pl.BlockSpec((taps, cb), lambda j: (0, j))],
        out_specs=[dspec, pl.BlockSpec((taps, cb), lambda j: (0, j)), pl.BlockSpec((1, cb), lambda j: (0, j))],
        scratch_shapes=[pltpu.VMEM((rows + 2 * HALO, cb), f32), pltpu.VMEM((rows + 2 * HALO, cb), f32),
                        pltpu.VMEM((rows, cb), f32)],
        compiler_params=_cparams("parallel"),
    )(*douts, x, w)


def _ada_forward(c16, w_ada, b_loc):
    def body(c_ref, w_ref, b_ref, o_ref):
        cv = c_ref[...]
        s = (cv * _sigmoid(cv)).astype(bf16)
        o_ref[0] = jnp.dot(s, w_ref[0].astype(bf16), preferred_element_type=f32) + b_ref[0]

    return pl.pallas_call(
        body, name="ada_forward", out_shape=jax.ShapeDtypeStruct((2, 16, ADA_SHARD), f32), grid=(2,),
        in_specs=[pl.BlockSpec((16, D), lambda l: (0, 0)), pl.BlockSpec((1, D, ADA_SHARD), lambda l: (l, 0, 0)),
                  pl.BlockSpec((1, 1, ADA_SHARD), lambda l: (l, 0, 0))],
        out_specs=pl.BlockSpec((1, 16, ADA_SHARD), lambda l: (l, 0, 0)),
        compiler_params=_cparams("parallel"),
    )(c16, w_ada, b_loc)


def _ada_backward(c16, g16, w_ada):
    def body(c_ref, g_ref, w_ref, dw_ref, ds_ref):
        cv = c_ref[...]
        s = (cv * _sigmoid(cv)).astype(bf16)
        g = g_ref[0].astype(bf16)
        dw_ref[0] = lax.dot_general(s, g, (((0,), (0,)), ((), ())), preferred_element_type=f32)
        ds = lax.dot_general(g, w_ref[0].astype(bf16), (((1,), (1,)), ((), ())), preferred_element_type=f32)
        cc = cv[8:9]
        sg = _sigmoid(cc)
        dsilu = sg * (1.0 + cc * (1.0 - sg))
        ds_ref[0] = jnp.zeros((8, D), f32) + jnp.sum(ds[8:16], axis=0, keepdims=True) * dsilu

    return pl.pallas_call(
        body, name="ada_backward",
        out_shape=[jax.ShapeDtypeStruct((2, D, ADA_SHARD), f32), jax.ShapeDtypeStruct((2, 8, D), f32)], grid=(2,),
        in_specs=[pl.BlockSpec((16, D), lambda l: (0, 0)), pl.BlockSpec((1, 16, ADA_SHARD), lambda l: (l, 0, 0)),
                  pl.BlockSpec((1, D, ADA_SHARD), lambda l: (l, 0, 0))],
        out_specs=[pl.BlockSpec((1, D, ADA_SHARD), lambda l: (l, 0, 0)), pl.BlockSpec((1, 8, D), lambda l: (l, 0, 0))],
        compiler_params=_cparams("parallel"),
    )(c16, g16, w_ada)


def _adamw(pieces, w, m, v, name):
    n_pieces, rows, cols = pieces.shape
    tm = 256 if (rows % 256 == 0 and rows > 256) else rows

    def body(p_ref, w_ref, m_ref, v_ref, g_ref, d_ref, nm_ref, nv_ref):
        g = p_ref[0]
        for j in range(1, n_pieces):
            g = g + p_ref[j]
        m2 = ADAM_B1 * m_ref[...] + (1.0 - ADAM_B1) * g
        v2 = ADAM_B2 * v_ref[...] + (1.0 - ADAM_B2) * (g * g)
        m_hat = m2 / (1.0 - ADAM_B1 ** ADAM_STEP)
        v_hat = v2 / (1.0 - ADAM_B2 ** ADAM_STEP)
        g_ref[...] = g
        d_ref[...] = -ADAM_LR * (m_hat / (jnp.sqrt(v_hat) + ADAM_EPS) + ADAM_WD * w_ref[...])
        nm_ref[...] = m2
        nv_ref[...] = v2

    spec = pl.BlockSpec((tm, cols), lambda i: (i, 0))
    return pl.pallas_call(
        body, name=name, out_shape=[jax.ShapeDtypeStruct((rows, cols), f32)] * 4, grid=(rows // tm,),
        in_specs=[pl.BlockSpec((n_pieces, tm, cols), lambda i: (0, i, 0)), spec, spec, spec],
        out_specs=[spec] * 4, compiler_params=_cparams("parallel"),
    )(pieces, w, m, v)


def _pos_embed():
    t = jnp.arange(T_LAT, dtype=jnp.int32)
    row = (t // GRID_W).astype(f32)
    col = (t % GRID_W).astype(f32)
    q = D // 4
    omega = 1.0 / (POS_BASE ** (jnp.arange(q, dtype=f32) / q))
    er = row[:, None] * omega[None, :]
    ec = col[:, None] * omega[None, :]
    return jnp.concatenate([jnp.sin(er), jnp.cos(er), jnp.sin(ec), jnp.cos(ec)], axis=-1).astype(f32)


def _dense_gates(w_a, w_x):
    eye = jnp.eye(N_BLK, dtype=f32)
    parts = []
    for d in range(2):
        for w in (w_a, w_x):
            parts.append(jnp.einsum("hij,hg->higj", w[d], eye).reshape(R, R))
    return jnp.concatenate(parts, axis=1).astype(bf16)


def _gate_blocks(dwg, part):
    blk = dwg[:, part * R:(part + 1) * R].reshape(N_BLK, BLK, N_BLK, BLK)
    return jnp.moveaxis(jnp.diagonal(blk, axis1=0, axis2=2), -1, 0)


FWD_SEGMENTS = ((0, T_ALL, False),)
REV_SEGMENTS = ((0, T_CTX, True), (T_CTX, T_LAT, True))
FWD_SEGMENTS_BWD = ((0, T_ALL, True),)
REV_SEGMENTS_BWD = ((T_CTX, T_LAT, False), (0, T_CTX, False))
CONV_SEGMENTS = ((0, T_CTX), (T_CTX, T_LAT))
TM = 128
N_CTX_TILES = T_CTX // TM


def _local_step(x, ctx, target, mods, cmod, wts):
    sh1, sc1, g1, sh2, sc2, g2 = [[mods[l, i][None] for l in range(2)] for i in range(N_MOD)]
    ng = wts["norm_g"]
    xcat = jnp.concatenate([ctx, x], axis=0)
    poscat = jnp.concatenate([jnp.zeros((T_CTX, D), f32), _pos_embed()], axis=0)
    scp = jnp.concatenate([cmod[1][None], sc1[0]], axis=0)
    shp = jnp.concatenate([cmod[0][None], sh1[0]], axis=0)

    def blend(i, p):
        sel = jnp.where(i < N_CTX_TILES, 1.0, 0.0)
        return sel * p[0:1] + (1.0 - sel) * p[1:2]

    def f_pre0(i, xc, pos, g, scp_, shp_):
        x0 = xc + pos
        return x0, _normmod(x0, g, blend(i, scp_), blend(i, shp_))

    x0cat, h0 = _rowcall(f_pre0, "l0_prenorm", T_ALL, TM, [_rin(xcat), _rin(poscat)], [ng[0, 0][None], scp, shp],
                         [(D, f32), (D, bf16)])
    gr = _mm(h0, wts["rec_w_in"], "l0_in_proj")
    u, ub = _dwconv_fwd(gr, R // 256, wts["rec_conv_w"], wts["rec_conv_b"], 4, 1, CONV_SEGMENTS, 256,
                        "l0_conv", True)
    pre = _mm(ub, wts["gates"], "l0_gates")

    def f_coeff(i, pre_, u_, ba, bx, lam):
        outs = []
        for d in range(2):
            a, b = _coeff(pre_[:, 2 * d * R:(2 * d + 1) * R], pre_[:, (2 * d + 1) * R:(2 * d + 2) * R], u_,
                          ba[d:d + 1], bx[d:d + 1], lam[d:d + 1])
            outs += [a, b]
        return tuple(outs)

    a0, b0, a1, b1 = _rowcall(f_coeff, "l0_coeff", T_ALL, TM, [_rin(pre), _rin(u)],
                              [wts["rec_b_a"], wts["rec_b_x"], wts["rec_lambda"]], [(R, f32)] * 4)
    y0, yp0 = _scan_call(a0, b0, FWD_SEGMENTS, "l0_scan_fwd", False)
    y1, yp1 = _scan_call(a1, b1, REV_SEGMENTS, "l0_scan_rev", False)

    def f_gate(i, gp, y0_, y1_):
        return (_gelu(gp) * (y0_ + y1_),)

    (zb,) = _rowcall(f_gate, "l0_gate", T_LAT, TM,
                     [_rin(gr, R, 0, N_CTX_TILES), _rin(y0, None, 0, N_CTX_TILES), _rin(y1, None, 0, N_CTX_TILES)],
                     [], [(R, bf16)])
    out0 = _mm(zb, wts["rec_w_out"], "l0_out_proj")

    def f_resid_pre(i, xin, out, gate, bias, g, sc, sh):
        xn = xin + gate * (out + bias)
        return xn, _normmod(xn, g, sc, sh)

    zero_d = jnp.zeros((1, D), f32)
    x1, h1 = _rowcall(f_resid_pre, "l0_resid_a", T_LAT, TM, [_rin(x0cat, None, 0, N_CTX_TILES), _rin(out0)],
                      [g1[0], zero_d, ng[0, 1][None], sc2[0], sh2[0]], [(D, f32), (D, bf16)])

    def f_sqrelu(i, hid):
        return (jnp.square(jnp.maximum(hid, 0.0)),)

    def mlp_fwd(h, l, tag):
        hid = _mm(h, wts["mlp_w_in"][l], tag + "_mlp_in")
        (act,) = _rowcall(f_sqrelu, tag + "_sqrelu", T_LAT, TM, [_rin(hid)], [], [(F, bf16)])
        return hid, act, _mm(act, wts["mlp_w_out"][l], tag + "_mlp_out")

    hid0, act0, mo0 = mlp_fwd(h1, 0, "l0")
    x2, h2 = _rowcall(f_resid_pre, "l0_resid_b", T_LAT, TM, [_rin(x1), _rin(mo0)],
                      [g2[0], zero_d, ng[1, 0][None], sc1[1], sh1[1]], [(D, f32), (D, bf16)])

    pw = _mm(h2, wts["conf_w_pw1"], "l1_pw1")

    def f_glu(i, pa, pb, b1):
        return ((pa + b1[:, :D]) * _sigmoid(pb + b1[:, D:]),)

    (zg,) = _rowcall(f_glu, "l1_glu", T_LAT, TM, [_rin(pw, D, 0), _rin(pw, D, 1)], [wts["conf_b_pw1"]], [(D, f32)])
    (zc,) = _dwconv_fwd(zg, 0, wts["conf_conv_w"], wts["conf_conv_b"], 31, 15, ((0, T_LAT),), 128, "l1_conv", False)

    def ln_silu(z, lg, lb):
        mu = jnp.mean(z, axis=-1, keepdims=True)
        zc_ = z - mu
        var = jnp.mean(zc_ * zc_, axis=-1, keepdims=True)
        yv = zc_ * lax.rsqrt(var + EPS) * lg + lb
        return yv * _sigmoid(yv)

    def f_lnsilu(i, z, lg, lb):
        return (ln_silu(z, lg, lb),)

    (sb,) = _rowcall(f_lnsilu, "l1_ln_silu", T_LAT, TM, [_rin(zc)], [wts["conf_ln_g"], wts["conf_ln_b"]], [(D, bf16)])
    out1 = _mm(sb, wts["conf_w_pw2"], "l1_pw2")
    x3, h3 = _rowcall(f_resid_pre, "l1_resid_a", T_LAT, TM, [_rin(x2), _rin(out1)],
                      [g1[1], wts["conf_b_pw2"], ng[1, 1][None], sc2[1], sh2[1]], [(D, f32), (D, bf16)])
    hid1, act1, mo1 = mlp_fwd(h3, 1, "l1")

    def loss_fn(x3_, mo_, gate, fg, tgt):
        x4 = x3_ + gate * mo_
        err = _rms(x4, fg) - tgt
        per_row = jnp.mean(err * err, axis=-1, keepdims=True)
        return 0.5 * jnp.sum(per_row, axis=0, keepdims=True)

    def f_head(i, x3_, mo_, tgt, gate, fg):
        loss, vjp = jax.vjp(lambda a, b, c, e: loss_fn(a, b, c, e, tgt), x3_, mo_, gate, fg)
        dx, dmo, dgate, dfg = vjp(jnp.ones((1, 1), f32))
        return dx, dmo, jnp.broadcast_to(loss, (1, 128)), dgate, dfg

    dx3, dmo1, loss_acc, dg2_1, dfinal_g = _rowcall(
        f_head, "head", T_LAT, TM, [_rin(x3), _rin(mo1), _rin(target)], [g2[1], wts["final_g"]],
        [(D, f32), (D, bf16)], [(1, 128), (1, D), (1, D)])

    grads = {"final_g": dfinal_g}

    def f_sqrelu_bwd(i, hid, dact):
        return (dact * (2.0 * jnp.maximum(hid, 0.0)),)

    def normmod_bwd(xin, dh, dx_skip, g, sc, sh, tag):
        def fb(i, x_, dh_, dxs, g_, sc_, sh_):
            _, vjp = jax.vjp(_normmod, x_, g_, sc_, sh_)
            dx, dg, dsc, dsh = vjp(dh_)
            return dx + dxs, dg, dsc, dsh

        return _rowcall(fb, tag + "_normmod_bwd", T_LAT, TM, [_rin(xin), _rin(dh), _rin(dx_skip)], [g, sc, sh],
                        [(D, f32)], [(1, D)] * 3)

    def mlp_bwd(dmo, hid, act, h, l, tag):
        dact = _mm(dmo, wts["mlp_w_out"][l], tag + "_mlp_out_dx", tb=True)
        dw_out = _mm(act, dmo, tag + "_mlp_out_dw", ta=True)
        (dhid,) = _rowcall(f_sqrelu_bwd, tag + "_sqrelu_bwd", T_LAT, TM, [_rin(hid), _rin(dact)], [], [(F, bf16)])
        dh = _mm(dhid, wts["mlp_w_in"][l], tag + "_mlp_in_dx", tb=True)
        dw_in = _mm(h, dhid, tag + "_mlp_in_dw", ta=True)
        return dh, dw_in, dw_out

    def f_resid_bwd(i, dx, out, gate, bias):
        return dx * gate, jnp.sum(dx * (out + bias), axis=0, keepdims=True), gate * jnp.sum(dx, axis=0, keepdims=True)

    def resid_bwd(dx, out, gate, bias, tag):
        return _rowcall(f_resid_bwd, tag + "_resid_bwd", T_LAT, TM, [_rin(dx), _rin(out)], [gate, bias],
                        [(D, bf16)], [(1, D)] * 2)

    dh3, dw_in1, dw_out1 = mlp_bwd(dmo1, hid1, act1, h3, 1, "l1")
    dx3, dng11, dsc2_1, dsh2_1 = normmod_bwd(x3, dh3, dx3, ng[1, 1][None], sc2[1], sh2[1], "l1b")

    dout1, dg1_1, db_pw2 = resid_bwd(dx3, out1, g1[1], wts["conf_b_pw2"], "l1a")
    ds = _mm(dout1, wts["conf_w_pw2"], "l1_pw2_dx", tb=True)
    grads["conf_w_pw2"] = _mm(sb, dout1, "l1_pw2_dw", ta=True)
    grads["conf_b_pw2"] = db_pw2

    def f_lnsilu_bwd(i, z, ds_, lg, lb):
        _, vjp = jax.vjp(ln_silu, z, lg, lb)
        return vjp(ds_)

    dzc, dln_g, dln_b = _rowcall(f_lnsilu_bwd, "l1_ln_silu_bwd", T_LAT, TM, [_rin(zc), _rin(ds)],
                                 [wts["conf_ln_g"], wts["conf_ln_b"]], [(D, f32)], [(1, D)] * 2)
    grads["conf_ln_g"], grads["conf_ln_b"] = dln_g, dln_b
    dzg, dconv_w, dconv_b = _dwconv_bwd([dzc], zg, 0, wts["conf_conv_w"], 31, 15, ((0, T_LAT),), 128,
                                        "l1_conv_bwd", f32)
    grads["conf_conv_w"], grads["conf_conv_b"] = dconv_w, dconv_b

    def f_glu_bwd(i, pa, pb, dz, b1):
        _, vjp = jax.vjp(lambda a, b, c: (a + c[:, :D]) * _sigmoid(b + c[:, D:]), pa, pb, b1)
        da, db, dc = vjp(dz)
        return jnp.concatenate([da, db], axis=1), dc

    dpw, db_pw1 = _rowcall(f_glu_bwd, "l1_glu_bwd", T_LAT, TM, [_rin(pw, D, 0), _rin(pw, D, 1), _rin(dzg)],
                           [wts["conf_b_pw1"]], [(2 * D, bf16)], [(1, 2 * D)])
    grads["conf_b_pw1"] = db_pw1
    dh2 = _mm(dpw, wts["conf_w_pw1"], "l1_pw1_dx", tb=True)
    grads["conf_w_pw1"] = _mm(h2, dpw, "l1_pw1_dw", ta=True)
    dx2, dng10, dsc1_1, dsh1_1 = normmod_bwd(x2, dh2, dx3, ng[1, 0][None], sc1[1], sh1[1], "l1a")

    dmo0, dg2_0, _ = resid_bwd(dx2, mo0, g2[0], zero_d, "l0b")
    dh1, dw_in0, dw_out0 = mlp_bwd(dmo0, hid0, act0, h1, 0, "l0")
    grads["mlp_w_in"] = (dw_in0, dw_in1)
    grads["mlp_w_out"] = (dw_out0, dw_out1)
    dx1, dng01, dsc2_0, dsh2_0 = normmod_bwd(x1, dh1, dx2, ng[0, 1][None], sc2[0], sh2[0], "l0b")

    dout0, dg1_0, _ = resid_bwd(dx1, out0, g1[0], zero_d, "l0a")
    dz = _mm(dout0, wts["rec_w_out"], "l0_out_proj_dx", tb=True)
    grads["rec_w_out"] = _mm(zb, dout0, "l0_out_proj_dw", ta=True)

    def f_gate_bwd(i, gp, y0_, y1_, dz_):
        lat = jnp.where(i < N_CTX_TILES, 0.0, 1.0)
        _, vjp = jax.vjp(lambda a, b: _gelu(a) * b, gp, y0_ + y1_)
        dgp, dy = vjp(dz_)
        return dgp * lat, dy * lat

    dgp, dy = _rowcall(f_gate_bwd, "l0_gate_bwd", T_ALL, TM,
                       [_rin(gr, R, 0), _rin(y0), _rin(y1), _rin(dz, None, 0, -N_CTX_TILES)], [],
                       [(R, bf16), (R, f32)])
    (dh_f,) = _scan_call(a0, dy, FWD_SEGMENTS_BWD, "l0_scan_fwd_bwd", True)
    (dh_r,) = _scan_call(a1, dy, REV_SEGMENTS_BWD, "l0_scan_rev_bwd", True)

    def f_coeff_bwd(i, pre_, u_, dhf, dhr, ypf, ypr, ba, bx, lam):
        dpre, dba, dbx, dlam = [], [], [], []
        du = jnp.zeros_like(u_)
        for d, (dh_, yp_) in enumerate(((dhf, ypf), (dhr, ypr))):
            _, vjp = jax.vjp(_coeff, pre_[:, 2 * d * R:(2 * d + 1) * R], pre_[:, (2 * d + 1) * R:(2 * d + 2) * R], u_,
                             ba[d:d + 1], bx[d:d + 1], lam[d:d + 1])
            dpa, dpx, du_d, dba_d, dbx_d, dlam_d = vjp((dh_ * yp_, dh_))
            dpre += [dpa, dpx]
            du = du + du_d
            dba.append(dba_d)
            dbx.append(dbx_d)
            dlam.append(dlam_d)
        return jnp.concatenate(dpre, axis=1), du, _rows2(*dba), _rows2(*dbx), _rows2(*dlam)

    dpre, du_direct, db_a, db_x, dlam = _rowcall(
        f_coeff_bwd, "l0_coeff_bwd", T_ALL, 64,
        [_rin(pre), _rin(u), _rin(dh_f), _rin(dh_r), _rin(yp0), _rin(yp1)],
        [wts["rec_b_a"], wts["rec_b_x"], wts["rec_lambda"]], [(4 * R, bf16), (R, f32)], [(2, R)] * 3)
    grads["rec_b_a"], grads["rec_b_x"], grads["rec_lambda"] = db_a, db_x, dlam
    du_gates = _mm(dpre, wts["gates"], "l0_gates_dx", tb=True)
    grads["gates"] = _mm(ub, dpre, "l0_gates_dw", ta=True)
    drec, dconv4_w, dconv4_b = _dwconv_bwd([du_direct, du_gates], gr, R // 256, wts["rec_conv_w"], 4, 1,
                                           CONV_SEGMENTS, 256, "l0_conv_bwd", bf16)
    grads["rec_conv_w"], grads["rec_conv_b"] = dconv4_w, dconv4_b
    dgr = jnp.concatenate([dgp, drec], axis=1)
    dh0 = _mm(dgr, wts["rec_w_in"], "l0_in_proj_dx", tb=True)
    grads["rec_w_in"] = _mm(h0, dgr, "l0_in_proj_dw", ta=True)

    def f_pre0_bwd(i, x0, dh_, dxs, g, scp_, shp_):
        lat = jnp.where(i < N_CTX_TILES, 0.0, 1.0)
        _, vjp = jax.vjp(lambda a, b, c, e: _normmod(a, b, blend(i, c), blend(i, e)), x0, g, scp_, shp_)
        dx, dg, dscp, dshp = vjp(dh_)
        return dx + lat * dxs, dg, dscp, dshp

    dx0cat, dng00, dscp, dshp = _rowcall(
        f_pre0_bwd, "l0_prenorm_bwd", T_ALL, TM, [_rin(x0cat), _rin(dh0), _rin(dx1, None, 0, -N_CTX_TILES)],
        [ng[0, 0][None], scp, shp], [(D, f32)], [(1, D), (2, D), (2, D)])

    grads["norm_g"] = jnp.stack([jnp.concatenate([dng00, dng01], 0), jnp.concatenate([dng10, dng11], 0)])
    dmods = jnp.stack([
        jnp.concatenate([dshp[1:2], dscp[1:2], dg1_0, dsh2_0, dsc2_0, dg2_0], axis=0),
        jnp.concatenate([dsh1_1, dsc1_1, dg1_1, dsh2_1, dsc2_1, dg2_1], axis=0)])
    dcmod = jnp.concatenate([dshp[0:1], dscp[0:1]], axis=0)
    return loss_acc[0, 0], dx0cat[T_CTX:], dmods, dcmod, grads


def _unshard_cols(g):
    g = jnp.moveaxis(g, 0, -2)
    return g.reshape(g.shape[:-2] + (g.shape[-2] * g.shape[-1],))


def _shard_cols(w):
    w = w.reshape(w.shape[:-1] + (N_DEV, w.shape[-1] // N_DEV))
    return jnp.moveaxis(w, -2, 0)


def _shard_rows(w):
    return w.reshape((N_DEV, w.shape[0] // N_DEV) + w.shape[1:])


SMALL_PACK_ROWS = 64


def kernel(x, c, ctx, c_ctx, w_ada, b_ada, norm_g, rec_w_in, rec_conv_w, rec_conv_b, rec_lambda, rec_w_a, rec_b_a, rec_w_x, rec_b_x, rec_w_out, conf_w_pw1, conf_b_pw1, conf_conv_w, conf_conv_b, conf_ln_g, conf_ln_b, conf_w_pw2, conf_b_pw2, mlp_w_in, mlp_w_out, final_g, loss_target, m_c_ctx, m_w_ada, m_b_ada, m_norm_g, m_rec_w_in, m_rec_conv_w, m_rec_conv_b, m_rec_lambda, m_rec_w_a, m_rec_b_a, m_rec_w_x, m_rec_b_x, m_rec_w_out, m_conf_w_pw1, m_conf_b_pw1, m_conf_conv_w, m_conf_conv_b, m_conf_ln_g, m_conf_ln_b, m_conf_w_pw2, m_conf_b_pw2, m_mlp_w_in, m_mlp_w_out, m_final_g, v_c_ctx, v_w_ada, v_b_ada, v_norm_g, v_rec_w_in, v_rec_conv_w, v_rec_conv_b, v_rec_lambda, v_rec_w_a, v_rec_b_a, v_rec_w_x, v_rec_b_x, v_rec_w_out, v_conf_w_pw1, v_conf_b_pw1, v_conf_conv_w, v_conf_conv_b, v_conf_ln_g, v_conf_ln_b, v_conf_w_pw2, v_conf_b_pw2, v_mlp_w_in, v_mlp_w_out, v_final_g):
    me = 4 * lax.axis_index("x") + 2 * lax.axis_index("y") + lax.axis_index("c")
    weights = dict(c_ctx=c_ctx, w_ada=w_ada, b_ada=b_ada, norm_g=norm_g, rec_w_in=rec_w_in, rec_conv_w=rec_conv_w,
                   rec_conv_b=rec_conv_b, rec_lambda=rec_lambda, rec_w_a=rec_w_a, rec_b_a=rec_b_a, rec_w_x=rec_w_x,
                   rec_b_x=rec_b_x, rec_w_out=rec_w_out, conf_w_pw1=conf_w_pw1, conf_b_pw1=conf_b_pw1,
                   conf_conv_w=conf_conv_w, conf_conv_b=conf_conv_b, conf_ln_g=conf_ln_g, conf_ln_b=conf_ln_b,
                   conf_w_pw2=conf_w_pw2, conf_b_pw2=conf_b_pw2, mlp_w_in=mlp_w_in, mlp_w_out=mlp_w_out, final_g=final_g)
    m_in = dict(c_ctx=m_c_ctx, w_ada=m_w_ada, b_ada=m_b_ada, norm_g=m_norm_g, rec_w_in=m_rec_w_in, rec_conv_w=m_rec_conv_w,
                rec_conv_b=m_rec_conv_b, rec_lambda=m_rec_lambda, rec_w_a=m_rec_w_a, rec_b_a=m_rec_b_a, rec_w_x=m_rec_w_x,
                rec_b_x=m_rec_b_x, rec_w_out=m_rec_w_out, conf_w_pw1=m_conf_w_pw1, conf_b_pw1=m_conf_b_pw1,
                conf_conv_w=m_conf_conv_w, conf_conv_b=m_conf_conv_b, conf_ln_g=m_conf_ln_g, conf_ln_b=m_conf_ln_b,
                conf_w_pw2=m_conf_w_pw2, conf_b_pw2=m_conf_b_pw2, mlp_w_in=m_mlp_w_in, mlp_w_out=m_mlp_w_out,
                final_g=m_final_g)
    v_in = dict(c_ctx=v_c_ctx, w_ada=v_w_ada, b_ada=v_b_ada, norm_g=v_norm_g, rec_w_in=v_rec_w_in, rec_conv_w=v_rec_conv_w,
                rec_conv_b=v_rec_conv_b, rec_lambda=v_rec_lambda, rec_w_a=v_rec_w_a, rec_b_a=v_rec_b_a, rec_w_x=v_rec_w_x,
                rec_b_x=v_rec_b_x, rec_w_out=v_rec_w_out, conf_w_pw1=v_conf_w_pw1, conf_b_pw1=v_conf_b_pw1,
                conf_conv_w=v_conf_conv_w, conf_conv_b=v_conf_conv_b, conf_ln_g=v_conf_ln_g, conf_ln_b=v_conf_ln_b,
                conf_w_pw2=v_conf_w_pw2, conf_b_pw2=v_conf_b_pw2, mlp_w_in=v_mlp_w_in, mlp_w_out=v_mlp_w_out,
                final_g=v_final_g)
    names = list(weights)

    small_items = [c, norm_g, rec_conv_w, rec_lambda, conf_b_pw1, conf_conv_w, conf_conv_b, conf_ln_g, conf_ln_b,
                   conf_b_pw2]
    flat = jnp.concatenate([a.reshape(-1) for a in small_items])
    flat = jnp.pad(flat, (0, SMALL_PACK_ROWS * 128 - flat.shape[0])).reshape(SMALL_PACK_ROWS, 128)
    big_items = [rec_w_in[0], rec_w_out[0], conf_w_pw1[0], conf_w_pw2[0], mlp_w_in, mlp_w_out]
    (small_all,) = _all_gather([flat], "gather_small")
    big_all = _all_gather([a.astype(bf16) for a in big_items], "gather_weights")

    small_all = small_all.reshape(N_DEV, -1)
    off = 0
    small = []
    for a in small_items:
        small.append(small_all[:, off:off + a.size].reshape((N_DEV,) + a.shape))
        off += a.size
    c_all, ng_all, rcw_all, lam_all, bpw1_all, ccw_all, ccb_all, lng_all, lnb_all, bpw2_all = small
    wts = {
        "norm_g": _unshard_cols(ng_all),
        "rec_conv_w": _unshard_cols(rcw_all)[0],
        "rec_lambda": _unshard_cols(lam_all)[0],
        "conf_b_pw1": _unshard_cols(bpw1_all),
        "conf_conv_w": _unshard_cols(ccw_all)[0],
        "conf_conv_b": _unshard_cols(ccb_all),
        "conf_ln_g": _unshard_cols(lng_all),
        "conf_ln_b": _unshard_cols(lnb_all),
        "conf_b_pw2": _unshard_cols(bpw2_all),
        "rec_conv_b": rec_conv_b,
        "rec_b_a": rec_b_a[0].reshape(2, R),
        "rec_b_x": rec_b_x[0].reshape(2, R),
        "final_g": final_g[None],
        "gates": _dense_gates(rec_w_a[0], rec_w_x[0]),
        "rec_w_in": _unshard_cols(big_all[0]),
        "rec_w_out": big_all[1].reshape(R, D),
        "conf_w_pw1": _unshard_cols(big_all[2]),
        "conf_w_pw2": big_all[3].reshape(D, D),
        "mlp_w_in": _unshard_cols(big_all[4]),
        "mlp_w_out": jnp.moveaxis(big_all[5], 0, 1).reshape(2, F, D),
    }

    c16 = jnp.concatenate([c_all[:, 0], jnp.broadcast_to(c_ctx[None], (8, D))], axis=0)
    b_loc = lax.dynamic_slice_in_dim(b_ada, me * ADA_SHARD, ADA_SHARD, axis=1)[:, None]
    (mods_all,) = _all_gather([_ada_forward(c16, w_ada, b_loc)], "gather_mods")
    mods_all = _unshard_cols(mods_all)
    mods = lax.dynamic_index_in_dim(mods_all, me, axis=1, keepdims=False).reshape(2, N_MOD, D)
    cmod = mods_all[0, 8, :2 * D].reshape(2, D)

    loss_part, grad_x, dmods, dcmod, grads = _local_step(x[0], ctx[0], loss_target[0], mods, cmod, wts)
    loss = lax.psum(loss_part, ("x", "y", "c"))

    dm_flat = jnp.concatenate([dmods.reshape(-1), dcmod.reshape(-1)]).reshape(-1, 128)
    (dm_all,) = _all_gather([dm_flat], "gather_dmods")
    dm_all = dm_all.reshape(N_DEV, -1)
    dmods_all = dm_all[:, :2 * N_MOD * D].reshape(N_DEV, 2, N_MOD * D)
    dcmod_all = jnp.pad(dm_all[:, 2 * N_MOD * D:], ((0, 0), (0, (N_MOD - 2) * D)))
    g16_full = jnp.stack([jnp.concatenate([dmods_all[:, 0], dcmod_all], axis=0),
                          jnp.concatenate([dmods_all[:, 1], jnp.zeros_like(dcmod_all)], axis=0)])
    g16 = lax.dynamic_slice_in_dim(g16_full, me * ADA_SHARD, ADA_SHARD, axis=2)
    dw_ada, ds_part = _ada_backward(c16, g16, w_ada)
    (ds_all,) = _all_gather([ds_part[0]], "gather_dsilu")

    big_grads = [_shard_cols(grads["rec_w_in"]), _shard_rows(grads["rec_w_out"]), _shard_cols(grads["conf_w_pw1"]),
                 _shard_rows(grads["conf_w_pw2"]),
                 jnp.stack([_shard_cols(g) for g in grads["mlp_w_in"]], axis=1),
                 jnp.stack([_shard_rows(g) for g in grads["mlp_w_out"]], axis=1)]
    small_sharded = ["norm_g", "rec_conv_w", "rec_lambda", "conf_b_pw1", "conf_conv_w", "conf_conv_b", "conf_ln_g",
                     "conf_ln_b", "conf_b_pw2"]
    pack = jnp.concatenate([_shard_cols(grads[n]).reshape(N_DEV, -1) for n in small_sharded], axis=1)
    pack_len = pack.shape[1]
    pack = jnp.pad(pack, ((0, 0), (0, SMALL_PACK_ROWS * 128 - pack_len))).reshape(N_DEV, SMALL_PACK_ROWS, 128)
    big_recv = _all_to_all(big_grads, "scatter_weight_grads")
    (pack_recv,) = _all_to_all([pack], "scatter_small_grads")
    pack_recv = pack_recv.reshape(N_DEV, -1)

    dwg = grads["gates"]
    repl = {"rec_conv_b": grads["rec_conv_b"],
            "rec_w_a": jnp.stack([_gate_blocks(dwg, 0), _gate_blocks(dwg, 2)])[None],
            "rec_w_x": jnp.stack([_gate_blocks(dwg, 1), _gate_blocks(dwg, 3)])[None],
            "rec_b_a": grads["rec_b_a"].reshape(1, 2, N_BLK, BLK),
            "rec_b_x": grads["rec_b_x"].reshape(1, 2, N_BLK, BLK),
            "final_g": grads["final_g"][0]}
    repl_names = list(repl)
    repl_flat = jnp.concatenate([repl[n].reshape(-1) for n in repl_names])
    repl_len = repl_flat.shape[0]
    repl_rows = -(-repl_len // 1024) * 8
    repl_flat = jnp.pad(repl_flat, (0, repl_rows * 128 - repl_len)).reshape(repl_rows, 128)
    (repl_all,) = _all_gather([repl_flat], "gather_replicated_grads")
    repl_all = repl_all.reshape(N_DEV, -1)

    pieces = {}
    shard_shapes = {n: weights[n].shape for n in names}
    for n, recv in zip(["rec_w_in", "rec_w_out", "conf_w_pw1", "conf_w_pw2", "mlp_w_in", "mlp_w_out"], big_recv):
        pieces[n] = recv
    off = 0
    for n in small_sharded:
        size = weights[n].size
        pieces[n] = pack_recv[:, off:off + size]
        off += size
    off = 0
    for n in repl_names:
        size = weights[n].size
        pieces[n] = repl_all[:, off:off + size]
        off += size
    pieces["w_ada"] = dw_ada[None]
    db_terms = jnp.concatenate([dmods_all, jnp.stack([dcmod_all, jnp.zeros_like(dcmod_all)], axis=1)], axis=0)
    pieces["b_ada"] = db_terms
    pieces["c_ctx"] = ds_all[:, 0]

    def as2d(shape):
        if len(shape) == 1:
            return (1, shape[0])
        if len(shape) == 5:
            return (shape[0] * shape[1] * shape[2], shape[3] * shape[4])
        rows = 1
        for s in shape[:-1]:
            rows *= s
        return (rows, shape[-1])

    g_out, d_out, m_out, v_out = {}, {}, {}, {}
    for n in names:
        shape = shard_shapes[n]
        r2, c2 = as2d(shape)
        p = pieces[n].reshape(-1, r2, c2)
        g, dl, nm, nv = _adamw(p, weights[n].reshape(r2, c2), m_in[n].reshape(r2, c2), v_in[n].reshape(r2, c2),
                               "adamw_" + n)
        g_out[n], d_out[n], m_out[n], v_out[n] = (t.reshape(shape) for t in (g, dl, nm, nv))

    return (loss, grad_x[None], *[g_out[n] for n in names], *[d_out[n] for n in names],
            *[m_out[n] for n in names], *[v_out[n] for n in names])
```

```python
import functools

import jax
import jax.numpy as jnp
from jax import lax
from jax.experimental import pallas as pl
from jax.experimental.pallas import tpu as pltpu

f32 = jnp.float32
bf16 = jnp.bfloat16

N_DEV = 8
D = 1024
T_LAT = 2048
T_CTX = 256
T_ALL = T_CTX + T_LAT
R = 1280
N_BLK = 16
BLK = R // N_BLK
F = 4096
GRID_W = 64
RG_C = 8.0
EPS = 1e-6
POS_BASE = 10000.0
N_MOD = 6
ADA_SHARD = N_MOD * D // N_DEV

ADAM_LR = 0.001
ADAM_B1 = 0.9
ADAM_B2 = 0.999
ADAM_EPS = 1e-08
ADAM_WD = 0.01
ADAM_STEP = 10

VMEM_LIMIT_V7X = 56 * 1024 * 1024
HALO = 16
MESH = pl.DeviceIdType.MESH


def _cparams(*sem):
    return pltpu.CompilerParams(dimension_semantics=sem, vmem_limit_bytes=VMEM_LIMIT_V7X)


def _pick(n, cands):
    for c in cands:
        if n % c == 0:
            return c
    raise ValueError(f"no block size for {n}")


def _position():
    x, y, c = lax.axis_index("x"), lax.axis_index("y"), lax.axis_index("c")
    return x, y, c, 4 * x + 2 * y + c


def _peer(x, y, c, k):
    px = (1 - x) if (k >> 2) & 1 else x
    py = (1 - y) if (k >> 1) & 1 else y
    pc = (1 - c) if k & 1 else c
    return (px, py, pc), 4 * px + 2 * py + pc


def _exchange(arrs, name, scatter):
    n = len(arrs)

    def body(*refs):
        ins, outs = refs[:n], refs[n:2 * n]
        send_sems, recv_sems, local_sems = refs[2 * n:]
        x, y, c, me = _position()
        local = []
        for a in range(n):
            src = ins[a].at[me] if scatter else ins[a]
            cp = pltpu.make_async_copy(src, outs[a].at[me], local_sems.at[a])
            cp.start()
            local.append(cp)
        sends, recvs = [], []
        for a in range(n):
            for k in range(1, N_DEV):
                peer, peer_lin = _peer(x, y, c, k)
                src = ins[a].at[peer_lin] if scatter else ins[a]
                cp = pltpu.make_async_remote_copy(
                    src_ref=src, dst_ref=outs[a].at[me], send_sem=send_sems.at[a, k - 1],
                    recv_sem=recv_sems.at[a, k - 1], device_id=peer, device_id_type=MESH)
                cp.start()
                sends.append(cp)
                recvs.append(pltpu.make_async_remote_copy(
                    src_ref=src, dst_ref=outs[a].at[peer_lin], send_sem=send_sems.at[a, k - 1],
                    recv_sem=recv_sems.at[a, k - 1], device_id=peer, device_id_type=MESH))
        for cp in recvs:
            cp.wait_recv()
        for cp in sends:
            cp.wait_send()
        for cp in local:
            cp.wait()

    if scatter:
        out_shape = [jax.ShapeDtypeStruct(a.shape, a.dtype) for a in arrs]
    else:
        out_shape = [jax.ShapeDtypeStruct((N_DEV,) + a.shape, a.dtype) for a in arrs]
    any_spec = pl.BlockSpec(memory_space=pl.ANY)
    return pl.pallas_call(
        body, name=name, out_shape=out_shape,
        in_specs=[any_spec] * n, out_specs=[any_spec] * n,
        scratch_shapes=[pltpu.SemaphoreType.DMA((n, N_DEV - 1)), pltpu.SemaphoreType.DMA((n, N_DEV - 1)),
                        pltpu.SemaphoreType.DMA((n,))],
    )(*arrs)


def _all_gather(arrs, name):
    return _exchange(arrs, name, scatter=False)


def _all_to_all(arrs, name):
    return _exchange(arrs, name, scatter=True)


def _lin(p):
    return 4 * p[0] + 2 * p[1] + p[2]


def _comm_call(body, name, ins, out_shape, n_sems):
    any_spec = pl.BlockSpec(memory_space=pl.ANY)
    return pl.pallas_call(
        body, name=name, out_shape=out_shape, in_specs=[any_spec] * len(ins), out_specs=[any_spec] * len(out_shape),
        scratch_shapes=[pltpu.SemaphoreType.DMA((n_sems,)), pltpu.SemaphoreType.DMA((n_sems,)),
                        pltpu.SemaphoreType.DMA((4,))],
    )(*ins)


def _all_gather_2level(shard, name):
    def body(in_ref, out_ref, send_sems, recv_sems, local_sems):
        x, y, c, me = _position()
        sib, xn, yn, dg = (x, y, 1 - c), (1 - x, y, c), (x, 1 - y, c), (1 - x, 1 - y, c)

        def cp(k, src, slot, to):
            return pltpu.make_async_remote_copy(src_ref=src, dst_ref=out_ref.at[slot], send_sem=send_sems.at[k],
                                                recv_sem=recv_sems.at[k], device_id=to, device_id_type=MESH)

        mine = pltpu.make_async_copy(in_ref, out_ref.at[me], local_sems.at[0])
        mine.start()
        for k, to in ((0, sib), (1, xn), (2, yn)):
            cp(k, in_ref, me, to).start()
        cp(1, in_ref, _lin(xn), xn).wait_recv()
        cp(3, out_ref.at[_lin(xn)], _lin(xn), sib).start()

        @pl.when(c == 0)
        def _():
            cp(5, out_ref.at[_lin(xn)], _lin(xn), yn).start()

        cp(2, in_ref, _lin(yn), yn).wait_recv()
        cp(4, out_ref.at[_lin(yn)], _lin(yn), sib).start()

        @pl.when(c == 1)
        def _():
            cp(5, out_ref.at[_lin(yn)], _lin(yn), xn).start()

        cp(5, in_ref, _lin(dg), xn).wait_recv()
        cp(6, out_ref.at[_lin(dg)], _lin(dg), sib).start()
        for k, origin in ((0, sib), (3, (1 - x, y, 1 - c)), (4, (x, 1 - y, 1 - c)), (6, (1 - x, 1 - y, 1 - c))):
            cp(k, in_ref, _lin(origin), sib).wait_recv()
        for k in range(7):
            cp(k, in_ref, me, sib).wait_send()
        mine.wait()

    (out,) = _comm_call(body, name, [shard], [jax.ShapeDtypeStruct((N_DEV,) + shard.shape, shard.dtype)], 7)
    return out


def _plane_pos(x, y, q):
    return ((1 - x) if q & 2 else x, (1 - y) if q & 1 else y)


def _scatter_d2d(g, name):
    def body(g_ref, own_ref, recv_ref, send_sems, recv_sems, local_sems):
        x, y, c, me = _position()
        sib = (x, y, 1 - c)
        local, sends = [], []
        for q in range(4):
            px, py = _plane_pos(x, y, q)
            cp = pltpu.make_async_copy(g_ref.at[_lin((px, py, c))], own_ref.at[q], local_sems.at[q])
            cp.start()
            local.append(cp)
            cp = pltpu.make_async_remote_copy(src_ref=g_ref.at[_lin((px, py, 1 - c))], dst_ref=recv_ref.at[q],
                                              send_sem=send_sems.at[q], recv_sem=recv_sems.at[q], device_id=sib,
                                              device_id_type=MESH)
            cp.start()
            sends.append(cp)
        for cp in sends:
            cp.wait_recv()
        for cp in sends:
            cp.wait_send()
        for cp in local:
            cp.wait()

    quarter = jax.ShapeDtypeStruct((4,) + g.shape[1:], g.dtype)
    return _comm_call(body, name, [g], [quarter, quarter], 4)


def _scatter_ici_first(h, name):
    def body(h_ref, keep_ref, recv_ref, send_sems, recv_sems, local_sems):
        x, y, c, me = _position()
        xn, yn = (1 - x, y, c), (x, 1 - y, c)

        def cp(k, q, to):
            return pltpu.make_async_remote_copy(src_ref=h_ref.at[q], dst_ref=recv_ref.at[k], send_sem=send_sems.at[k],
                                                recv_sem=recv_sems.at[k], device_id=to, device_id_type=MESH)

        mine = pltpu.make_async_copy(h_ref.at[0], keep_ref.at[0], local_sems.at[0])
        mine.start()

        @pl.when(c == 0)
        def _():
            pltpu.make_async_copy(h_ref.at[1], keep_ref.at[1], local_sems.at[1]).start()
            cp(0, 2, xn).start()
            cp(1, 3, xn).start()

        @pl.when(c == 1)
        def _():
            pltpu.make_async_copy(h_ref.at[2], keep_ref.at[1], local_sems.at[1]).start()
            cp(0, 1, yn).start()
            cp(1, 3, yn).start()

        for k in range(2):
            cp(k, 0, xn).wait_recv()
        for k in range(2):
            cp(k, 0, xn).wait_send()
        mine.wait()
        pltpu.make_async_copy(h_ref.at[1], keep_ref.at[1], local_sems.at[1]).wait()

    half = jax.ShapeDtypeStruct((2,) + h.shape[1:], h.dtype)
    return _comm_call(body, name, [h], [half, half], 2)


def _scatter_ici_second(k1, name):
    def body(k_ref, recv_ref, send_sems, recv_sems, local_sems):
        x, y, c, me = _position()
        xn, yn = (1 - x, y, c), (x, 1 - y, c)

        def cp(to):
            return pltpu.make_async_remote_copy(src_ref=k_ref, dst_ref=recv_ref, send_sem=send_sems.at[0],
                                                recv_sem=recv_sems.at[0], device_id=to, device_id_type=MESH)

        @pl.when(c == 0)
        def _():
            cp(yn).start()

        @pl.when(c == 1)
        def _():
            cp(xn).start()

        cp(xn).wait_recv()
        cp(xn).wait_send()

    (out,) = _comm_call(body, name, [k1], [jax.ShapeDtypeStruct(k1.shape, k1.dtype)], 1)
    return out


def _add_call(a, b, out_dtypes, name):
    nb, rows, cols = a.shape
    tm = _pick(rows, (416, 512, 256, 128, 8))
    per_block = isinstance(out_dtypes, (list, tuple))

    def body(a_ref, b_ref, *outs):
        if per_block:
            for j, o in enumerate(outs):
                o[...] = (a_ref[j].astype(f32) + b_ref[j].astype(f32)).astype(o.dtype)
        else:
            outs[0][...] = (a_ref[...].astype(f32) + b_ref[...].astype(f32)).astype(outs[0].dtype)

    in_spec = pl.BlockSpec((nb, tm, cols), lambda i: (0, i, 0))
    if per_block:
        out_shape = [jax.ShapeDtypeStruct((rows, cols), dt) for dt in out_dtypes]
        out_specs = [pl.BlockSpec((tm, cols), lambda i: (i, 0))] * nb
    else:
        out_shape = [jax.ShapeDtypeStruct(a.shape, out_dtypes)]
        out_specs = [in_spec]
    return pl.pallas_call(body, name=name, out_shape=out_shape, grid=(rows // tm,), in_specs=[in_spec, in_spec],
                          out_specs=out_specs, compiler_params=_cparams("parallel"))(a, b)


def _reduce_scatter(g, tag):
    own, recv = _scatter_d2d(g, tag + "_d2d")
    (h,) = _add_call(own, recv, bf16, tag + "_add_chip")
    keep, recv2 = _scatter_ici_first(h, tag + "_ici_first")
    k0, k1 = _add_call(keep, recv2, [f32, bf16], tag + "_add_pair")
    return k0, _scatter_ici_second(k1, tag + "_ici_second")


def _mm(a, b, name, ta=False, tb=False, out_dtype=f32):
    if ta:
        k_dim, m_dim = a.shape
    else:
        m_dim, k_dim = a.shape
    if tb:
        n_dim, k2 = b.shape
    else:
        k2, n_dim = b.shape
    assert k_dim == k2, (a.shape, b.shape)
    assert a.dtype == bf16 and b.dtype == bf16
    bm = _pick(m_dim, (512, 768, 640, 256, 128))
    bn = _pick(n_dim, (512, 640, 256, 128))
    bk = _pick(k_dim, (1024, 1280, 768, 512))
    nk = k_dim // bk
    a_spec = (pl.BlockSpec((bk, bm), lambda i, j, k: (k, i)) if ta
              else pl.BlockSpec((bm, bk), lambda i, j, k: (i, k)))
    b_spec = (pl.BlockSpec((bn, bk), lambda i, j, k: (j, k)) if tb
              else pl.BlockSpec((bk, bn), lambda i, j, k: (k, j)))
    dims = (((0 if ta else 1,), (1 if tb else 0,)), ((), ()))

    def body_single(a_ref, b_ref, o_ref):
        o_ref[...] = lax.dot_general(a_ref[...], b_ref[...], dims, preferred_element_type=f32).astype(o_ref.dtype)

    def body(a_ref, b_ref, o_ref, acc_ref):
        k = pl.program_id(2)

        @pl.when(k == 0)
        def _():
            acc_ref[...] = jnp.zeros_like(acc_ref)

        acc_ref[...] += lax.dot_general(a_ref[...], b_ref[...], dims, preferred_element_type=f32)

        @pl.when(k == nk - 1)
        def _():
            o_ref[...] = acc_ref[...].astype(o_ref.dtype)

    return pl.pallas_call(
        body_single if nk == 1 else body, name=name, out_shape=jax.ShapeDtypeStruct((m_dim, n_dim), out_dtype),
        grid=(m_dim // bm, n_dim // bn, nk), in_specs=[a_spec, b_spec],
        out_specs=pl.BlockSpec((bm, bn), lambda i, j, k: (i, j)),
        scratch_shapes=[] if nk == 1 else [pltpu.VMEM((bm, bn), f32)],
        compiler_params=_cparams("parallel", "parallel", "arbitrary"),
    )(a, b)


def _rin(arr, width=None, cb=0, roff=0):
    return (arr, arr.shape[1] if width is None else width, cb, roff)


def _rowcall(fn, name, rows, tm, row_ins, par_ins, row_outs, acc_outs=()):
    nr, npar, nro = len(row_ins), len(par_ins), len(row_outs)
    in_specs, args = [], []
    for arr, width, cb, roff in row_ins:
        if roff >= 0:
            imap = lambda i, cb=cb, roff=roff: (i + roff, cb)
        else:
            imap = lambda i, cb=cb, roff=roff: (jnp.maximum(i + roff, 0), cb)
        in_specs.append(pl.BlockSpec((tm, width), imap))
        args.append(arr)
    for p in par_ins:
        in_specs.append(pl.BlockSpec(p.shape, lambda i: (0, 0)))
        args.append(p)
    out_shape, out_specs = [], []
    for width, dt in row_outs:
        out_shape.append(jax.ShapeDtypeStruct((rows, width), dt))
        out_specs.append(pl.BlockSpec((tm, width), lambda i: (i, 0)))
    for p, width in acc_outs:
        out_shape.append(jax.ShapeDtypeStruct((p, width), f32))
        out_specs.append(pl.BlockSpec((p, width), lambda i: (0, 0)))

    def body(*refs):
        i = pl.program_id(0)
        res = fn(i, *[r[...] for r in refs[:nr + npar]])
        outs = refs[nr + npar:]
        for o, v in zip(outs[:nro], res[:nro]):
            o[...] = v.astype(o.dtype)
        if acc_outs:
            @pl.when(i == 0)
            def _():
                for o in outs[nro:]:
                    o[...] = jnp.zeros_like(o)

            for o, v in zip(outs[nro:], res[nro:]):
                o[...] += v

    return pl.pallas_call(
        body, name=name, out_shape=out_shape, grid=(rows // tm,), in_specs=in_specs, out_specs=out_specs,
        compiler_params=_cparams("arbitrary"),
    )(*args)


def _rms(x, g):
    return x * lax.rsqrt(jnp.mean(x * x, axis=-1, keepdims=True) + EPS) * g


def _normmod(x, g, sc, sh):
    return _rms(x, g) * (1.0 + sc) + sh


def _rows2(v0, v1):
    rid = lax.broadcasted_iota(jnp.int32, (2, v0.shape[1]), 0)
    return jnp.where(rid == 0, v0, v1)


def _gelu(x):
    return 0.5 * x * (1.0 + jnp.tanh(0.7978845608028654 * (x + 0.044715 * (x * x * x))))


def _sigmoid(x):
    return 1.0 / (1.0 + jnp.exp(-x))


def _coeff(pre_a, pre_x, u, ba, bx, lam):
    r = _sigmoid(pre_a + ba)
    ig = _sigmoid(pre_x + bx)
    nl = -lam
    sp = jnp.maximum(nl, 0.0) + jnp.log(1.0 + jnp.exp(-jnp.abs(nl)))
    la = -RG_C * r * sp
    a = jnp.exp(la)
    one_minus_a2 = -jnp.tanh(la) * (a * a + 1.0)
    return a, jnp.sqrt(one_minus_a2) * (ig * u)


def _scan_call(a, v, segments, name, backward):
    rows, width = a.shape
    cb = 256
    n_out = 1 if backward else 2

    def body(a_ref, v_ref, *outs):
        rid = lax.broadcasted_iota(jnp.int32, (8, cb), 0)
        state = jnp.zeros((1, cb), f32)
        for start, n, rev in segments:
            nt = n // 8

            def tile(j, st, start=start, nt=nt, rev=rev):
                t0 = pl.multiple_of(start + (nt - 1 - j if rev else j) * 8, 8)
                at = a_ref[pl.ds(t0, 8), :]
                vt = v_ref[pl.ds(t0, 8), :]
                out = jnp.zeros((8, cb), f32)
                prev = jnp.zeros((8, cb), f32)
                for i in (range(7, -1, -1) if rev else range(8)):
                    if backward:
                        g = vt[i:i + 1] + st
                        st = at[i:i + 1] * g
                        out = jnp.where(rid == i, g, out)
                    else:
                        prev = jnp.where(rid == i, st, prev)
                        st = at[i:i + 1] * st + vt[i:i + 1]
                        out = jnp.where(rid == i, st, out)
                outs[0][pl.ds(t0, 8), :] = out
                if not backward:
                    outs[1][pl.ds(t0, 8), :] = prev
                return st

            state = lax.fori_loop(0, nt, tile, state)

    spec = pl.BlockSpec((rows, cb), lambda j: (0, j))
    return pl.pallas_call(
        body, name=name, out_shape=[jax.ShapeDtypeStruct((rows, width), f32)] * n_out,
        grid=(width // cb,), in_specs=[spec, spec], out_specs=[spec] * n_out,
        compiler_params=_cparams("parallel"),
    )(a, v)


CONV_CHUNK = 256


def _fill_padded(pad_ref, src_ref, start, n):
    cb = pad_ref.shape[1]
    pad_ref[pl.ds(0, HALO), :] = jnp.zeros((HALO, cb), f32)
    pad_ref[pl.ds(HALO, n), :] = src_ref[pl.ds(start, n), :].astype(f32)
    pad_ref[pl.ds(HALO + n, HALO), :] = jnp.zeros((HALO, cb), f32)


def _dwconv_fwd(x, x_cb0, w, b, taps, pad_left, segments, cb, name, emit_bf16):
    rows = x.shape[0]
    width = w.shape[1]

    def body(x_ref, w_ref, b_ref, *rest):
        outs, xp = rest[:-1], rest[-1]
        for start, n in segments:
            _fill_padded(xp, x_ref, start, n)
            for c0 in range(0, n, CONV_CHUNK):
                acc = jnp.zeros((CONV_CHUNK, cb), f32) + b_ref[...]
                for k in range(taps):
                    acc = acc + w_ref[k:k + 1, :] * xp[pl.ds(HALO + c0 + k - pad_left, CONV_CHUNK), :]
                for o in outs:
                    o[pl.ds(start + c0, CONV_CHUNK), :] = acc.astype(o.dtype)

    out_dtypes = [f32, bf16] if emit_bf16 else [f32]
    return pl.pallas_call(
        body, name=name, out_shape=[jax.ShapeDtypeStruct((rows, width), dt) for dt in out_dtypes],
        grid=(width // cb,),
        in_specs=[pl.BlockSpec((rows, cb), lambda j: (0, j + x_cb0)), pl.BlockSpec((taps, cb), lambda j: (0, j)),
                  pl.BlockSpec((1, cb), lambda j: (0, j))],
        out_specs=[pl.BlockSpec((rows, cb), lambda j: (0, j))] * len(out_dtypes),
        scratch_shapes=[pltpu.VMEM((rows + 2 * HALO, cb), f32)],
        compiler_params=_cparams("parallel"),
    )(x, w, b)


def _dwconv_bwd(douts, x, x_cb0, w, taps, pad_left, segments, cb, name, dx_dtype):
    rows = x.shape[0]
    width = w.shape[1]
    nd = len(douts)

    def body(*refs):
        d_refs, x_ref, w_ref = refs[:nd], refs[nd], refs[nd + 1]
        dx_ref, dw_ref, db_ref, xp, dp, dsum = refs[nd + 2:]
        dw_ref[...] = jnp.zeros_like(dw_ref)
        db_ref[...] = jnp.zeros_like(db_ref)
        if nd > 1:
            total = d_refs[0][...]
            for r in d_refs[1:]:
                total = total + r[...]
            dsum[...] = total
            d_ref = dsum
        else:
            d_ref = d_refs[0]
        for start, n in segments:
            _fill_padded(xp, x_ref, start, n)
            _fill_padded(dp, d_ref, start, n)
            for c0 in range(0, n, CONV_CHUNK):
                dchunk = dp[pl.ds(HALO + c0, CONV_CHUNK), :]
                db_ref[...] += jnp.sum(dchunk, axis=0, keepdims=True)
                acc = jnp.zeros((CONV_CHUNK, cb), f32)
                for k in range(taps):
                    acc = acc + w_ref[k:k + 1, :] * dp[pl.ds(HALO + c0 + pad_left - k, CONV_CHUNK), :]
                    xs = xp[pl.ds(HALO + c0 + k - pad_left, CONV_CHUNK), :]
                    dw_ref[k:k + 1, :] += jnp.sum(dchunk * xs, axis=0, keepdims=True)
                dx_ref[pl.ds(start + c0, CONV_CHUNK), :] = acc.astype(dx_ref.dtype)

    dspec = pl.BlockSpec((rows, cb), lambda j: (0, j))
    return pl.pallas_call(
        body, name=name,
        out_shape=[jax.ShapeDtypeStruct((rows, width), dx_dtype), jax.ShapeDtypeStruct((taps, width), f32),
                   jax.ShapeDtypeStruct((1, width), f32)],
        grid=(width // cb,),
        in_specs=[dspec] * nd + [pl.BlockSpec((rows, cb), lambda j: (0, j + x_cb0)),
                                 pl.BlockSpec((taps, cb), lambda j: (0, j))],
        out_specs=[dspec, pl.BlockSpec((taps, cb), lambda j: (0, j)), pl.BlockSpec((1, cb), lambda j: (0, j))],
        scratch_shapes=[pltpu.VMEM((rows + 2 * HALO, cb), f32), pltpu.VMEM((rows + 2 * HALO, cb), f32),
                        pltpu.VMEM((rows, cb), f32)],
        compiler_params=_cparams("parallel"),
    )(*douts, x, w)


def _ada_forward(c16, w_ada, b_loc):
    def body(c_ref, w_ref, b_ref, o_ref):
        cv = c_ref[...]
        s = (cv * _sigmoid(cv)).astype(bf16)
        o_ref[0] = jnp.dot(s, w_ref[0].astype(bf16), preferred_element_type=f32) + b_ref[0]

    return pl.pallas_call(
        body, name="ada_forward", out_shape=jax.ShapeDtypeStruct((2, 16, ADA_SHARD), f32), grid=(2,),
        in_specs=[pl.BlockSpec((16, D), lambda l: (0, 0)), pl.BlockSpec((1, D, ADA_SHARD), lambda l: (l, 0, 0)),
                  pl.BlockSpec((1, 1, ADA_SHARD), lambda l: (l, 0, 0))],
        out_specs=pl.BlockSpec((1, 16, ADA_SHARD), lambda l: (l, 0, 0)),
        compiler_params=_cparams("parallel"),
    )(c16, w_ada, b_loc)


def _ada_backward(c16, g16, w_ada):
    def body(c_ref, g_ref, w_ref, dw_ref, ds_ref):
        cv = c_ref[...]
        s = (cv * _sigmoid(cv)).astype(bf16)
        g = g_ref[0].astype(bf16)
        dw_ref[0] = lax.dot_general(s, g, (((0,), (0,)), ((), ())), preferred_element_type=f32)
        ds = lax.dot_general(g, w_ref[0].astype(bf16), (((1,), (1,)), ((), ())), preferred_element_type=f32)
        cc = cv[8:9]
        sg = _sigmoid(cc)
        dsilu = sg * (1.0 + cc * (1.0 - sg))
        ds_ref[0] = jnp.zeros((8, D), f32) + jnp.sum(ds[8:16], axis=0, keepdims=True) * dsilu

    return pl.pallas_call(
        body, name="ada_backward",
        out_shape=[jax.ShapeDtypeStruct((2, D, ADA_SHARD), f32), jax.ShapeDtypeStruct((2, 8, D), f32)], grid=(2,),
        in_specs=[pl.BlockSpec((16, D), lambda l: (0, 0)), pl.BlockSpec((1, 16, ADA_SHARD), lambda l: (l, 0, 0)),
                  pl.BlockSpec((1, D, ADA_SHARD), lambda l: (l, 0, 0))],
        out_specs=[pl.BlockSpec((1, D, ADA_SHARD), lambda l: (l, 0, 0)), pl.BlockSpec((1, 8, D), lambda l: (l, 0, 0))],
        compiler_params=_cparams("parallel"),
    )(c16, g16, w_ada)


def _adamw(pieces, w, m, v, name):
    rows, cols = w.shape
    n_arr = len(pieces)
    tm = 256 if (rows % 256 == 0 and rows > 256) else rows

    def body(*refs):
        p_refs = refs[:n_arr]
        w_ref, m_ref, v_ref, g_ref, d_ref, nm_ref, nv_ref = refs[n_arr:]
        g = None
        for p_ref in p_refs:
            for j in range(p_ref.shape[0]):
                term = p_ref[j].astype(f32)
                g = term if g is None else g + term
        m2 = ADAM_B1 * m_ref[...] + (1.0 - ADAM_B1) * g
        v2 = ADAM_B2 * v_ref[...] + (1.0 - ADAM_B2) * (g * g)
        m_hat = m2 / (1.0 - ADAM_B1 ** ADAM_STEP)
        v_hat = v2 / (1.0 - ADAM_B2 ** ADAM_STEP)
        g_ref[...] = g
        d_ref[...] = -ADAM_LR * (m_hat / (jnp.sqrt(v_hat) + ADAM_EPS) + ADAM_WD * w_ref[...])
        nm_ref[...] = m2
        nv_ref[...] = v2

    spec = pl.BlockSpec((tm, cols), lambda i: (i, 0))
    return pl.pallas_call(
        body, name=name, out_shape=[jax.ShapeDtypeStruct((rows, cols), f32)] * 4, grid=(rows // tm,),
        in_specs=[pl.BlockSpec((p.shape[0], tm, cols), lambda i: (0, i, 0)) for p in pieces] + [spec, spec, spec],
        out_specs=[spec] * 4, compiler_params=_cparams("parallel"),
    )(*pieces, w, m, v)


def _pos_embed():
    t = jnp.arange(T_LAT, dtype=jnp.int32)
    row = (t // GRID_W).astype(f32)
    col = (t % GRID_W).astype(f32)
    q = D // 4
    omega = 1.0 / (POS_BASE ** (jnp.arange(q, dtype=f32) / q))
    er = row[:, None] * omega[None, :]
    ec = col[:, None] * omega[None, :]
    return jnp.concatenate([jnp.sin(er), jnp.cos(er), jnp.sin(ec), jnp.cos(ec)], axis=-1).astype(f32)


def _dense_gates(w_a, w_x):
    eye = jnp.eye(N_BLK, dtype=f32)
    parts = []
    for d in range(2):
        for w in (w_a, w_x):
            parts.append(jnp.einsum("hij,hg->higj", w[d], eye).reshape(R, R))
    return jnp.concatenate(parts, axis=1).astype(bf16)


def _gate_blocks(dwg, part):
    blk = dwg[:, part * R:(part + 1) * R].reshape(N_BLK, BLK, N_BLK, BLK)
    return jnp.moveaxis(jnp.diagonal(blk, axis1=0, axis2=2), -1, 0)


FWD_SEGMENTS = ((0, T_ALL, False),)
REV_SEGMENTS = ((0, T_CTX, True), (T_CTX, T_LAT, True))
FWD_SEGMENTS_BWD = ((0, T_ALL, True),)
REV_SEGMENTS_BWD = ((T_CTX, T_LAT, False), (0, T_CTX, False))
CONV_SEGMENTS = ((0, T_CTX), (T_CTX, T_LAT))
TM = 128
N_CTX_TILES = T_CTX // TM


def _local_step(x, ctx, target, mods, cmod, wts):
    sh1, sc1, g1, sh2, sc2, g2 = [[mods[l, i][None] for l in range(2)] for i in range(N_MOD)]
    ng = wts["norm_g"]
    xcat = jnp.concatenate([ctx, x], axis=0)
    poscat = jnp.concatenate([jnp.zeros((T_CTX, D), f32), _pos_embed()], axis=0)
    scp = jnp.concatenate([cmod[1][None], sc1[0]], axis=0)
    shp = jnp.concatenate([cmod[0][None], sh1[0]], axis=0)

    def blend(i, p):
        sel = jnp.where(i < N_CTX_TILES, 1.0, 0.0)
        return sel * p[0:1] + (1.0 - sel) * p[1:2]

    def f_pre0(i, xc, pos, g, scp_, shp_):
        x0 = xc + pos
        return x0, _normmod(x0, g, blend(i, scp_), blend(i, shp_))

    x0cat, h0 = _rowcall(f_pre0, "l0_prenorm", T_ALL, TM, [_rin(xcat), _rin(poscat)], [ng[0, 0][None], scp, shp],
                         [(D, f32), (D, bf16)])
    gr = _mm(h0, wts["rec_w_in"], "l0_in_proj")
    u, ub = _dwconv_fwd(gr, R // 256, wts["rec_conv_w"], wts["rec_conv_b"], 4, 1, CONV_SEGMENTS, 256,
                        "l0_conv", True)
    pre = _mm(ub, wts["gates"], "l0_gates")

    def f_coeff(i, pre_, u_, ba, bx, lam):
        outs = []
        for d in range(2):
            a, b = _coeff(pre_[:, 2 * d * R:(2 * d + 1) * R], pre_[:, (2 * d + 1) * R:(2 * d + 2) * R], u_,
                          ba[d:d + 1], bx[d:d + 1], lam[d:d + 1])
            outs += [a, b]
        return tuple(outs)

    a0, b0, a1, b1 = _rowcall(f_coeff, "l0_coeff", T_ALL, TM, [_rin(pre), _rin(u)],
                              [wts["rec_b_a"], wts["rec_b_x"], wts["rec_lambda"]], [(R, f32)] * 4)
    y0, yp0 = _scan_call(a0, b0, FWD_SEGMENTS, "l0_scan_fwd", False)
    y1, yp1 = _scan_call(a1, b1, REV_SEGMENTS, "l0_scan_rev", False)

    def f_gate(i, gp, y0_, y1_):
        return (_gelu(gp) * (y0_ + y1_),)

    (zb,) = _rowcall(f_gate, "l0_gate", T_LAT, TM,
                     [_rin(gr, R, 0, N_CTX_TILES), _rin(y0, None, 0, N_CTX_TILES), _rin(y1, None, 0, N_CTX_TILES)],
                     [], [(R, bf16)])
    out0 = _mm(zb, wts["rec_w_out"], "l0_out_proj")

    def f_resid_pre(i, xin, out, gate, bias, g, sc, sh):
        xn = xin + gate * (out + bias)
        return xn, _normmod(xn, g, sc, sh)

    zero_d = jnp.zeros((1, D), f32)
    x1, h1 = _rowcall(f_resid_pre, "l0_resid_a", T_LAT, TM, [_rin(x0cat, None, 0, N_CTX_TILES), _rin(out0)],
                      [g1[0], zero_d, ng[0, 1][None], sc2[0], sh2[0]], [(D, f32), (D, bf16)])

    def f_sqrelu(i, hid):
        return (jnp.square(jnp.maximum(hid, 0.0)),)

    def mlp_fwd(h, l, tag):
        hid = _mm(h, wts["mlp_w_in"][l], tag + "_mlp_in")
        (act,) = _rowcall(f_sqrelu, tag + "_sqrelu", T_LAT, TM, [_rin(hid)], [], [(F, bf16)])
        return hid, act, _mm(act, wts["mlp_w_out"][l], tag + "_mlp_out")

    hid0, act0, mo0 = mlp_fwd(h1, 0, "l0")
    x2, h2 = _rowcall(f_resid_pre, "l0_resid_b", T_LAT, TM, [_rin(x1), _rin(mo0)],
                      [g2[0], zero_d, ng[1, 0][None], sc1[1], sh1[1]], [(D, f32), (D, bf16)])

    pw = _mm(h2, wts["conf_w_pw1"], "l1_pw1")

    def f_glu(i, pa, pb, b1):
        return ((pa + b1[:, :D]) * _sigmoid(pb + b1[:, D:]),)

    (zg,) = _rowcall(f_glu, "l1_glu", T_LAT, TM, [_rin(pw, D, 0), _rin(pw, D, 1)], [wts["conf_b_pw1"]], [(D, f32)])
    (zc,) = _dwconv_fwd(zg, 0, wts["conf_conv_w"], wts["conf_conv_b"], 31, 15, ((0, T_LAT),), 128, "l1_conv", False)

    def ln_silu(z, lg, lb):
        mu = jnp.mean(z, axis=-1, keepdims=True)
        zc_ = z - mu
        var = jnp.mean(zc_ * zc_, axis=-1, keepdims=True)
        yv = zc_ * lax.rsqrt(var + EPS) * lg + lb
        return yv * _sigmoid(yv)

    def f_lnsilu(i, z, lg, lb):
        return (ln_silu(z, lg, lb),)

    (sb,) = _rowcall(f_lnsilu, "l1_ln_silu", T_LAT, TM, [_rin(zc)], [wts["conf_ln_g"], wts["conf_ln_b"]], [(D, bf16)])
    out1 = _mm(sb, wts["conf_w_pw2"], "l1_pw2")
    x3, h3 = _rowcall(f_resid_pre, "l1_resid_a", T_LAT, TM, [_rin(x2), _rin(out1)],
                      [g1[1], wts["conf_b_pw2"], ng[1, 1][None], sc2[1], sh2[1]], [(D, f32), (D, bf16)])
    hid1, act1, mo1 = mlp_fwd(h3, 1, "l1")

    def loss_fn(x3_, mo_, gate, fg, tgt):
        x4 = x3_ + gate * mo_
        err = _rms(x4, fg) - tgt
        per_row = jnp.mean(err * err, axis=-1, keepdims=True)
        return 0.5 * jnp.sum(per_row, axis=0, keepdims=True)

    def f_head(i, x3_, mo_, tgt, gate, fg):
        loss, vjp = jax.vjp(lambda a, b, c, e: loss_fn(a, b, c, e, tgt), x3_, mo_, gate, fg)
        dx, dmo, dgate, dfg = vjp(jnp.ones((1, 1), f32))
        return dx, dmo, jnp.broadcast_to(loss, (1, 128)), dgate, dfg

    dx3, dmo1, loss_acc, dg2_1, dfinal_g = _rowcall(
        f_head, "head", T_LAT, TM, [_rin(x3), _rin(mo1), _rin(target)], [g2[1], wts["final_g"]],
        [(D, f32), (D, bf16)], [(1, 128), (1, D), (1, D)])

    grads = {"final_g": dfinal_g}

    def f_sqrelu_bwd(i, hid, dact):
        return (dact * (2.0 * jnp.maximum(hid, 0.0)),)

    def normmod_bwd(xin, dh, dx_skip, g, sc, sh, tag):
        def fb(i, x_, dh_, dxs, g_, sc_, sh_):
            _, vjp = jax.vjp(_normmod, x_, g_, sc_, sh_)
            dx, dg, dsc, dsh = vjp(dh_)
            return dx + dxs, dg, dsc, dsh

        return _rowcall(fb, tag + "_normmod_bwd", T_LAT, TM, [_rin(xin), _rin(dh), _rin(dx_skip)], [g, sc, sh],
                        [(D, f32)], [(1, D)] * 3)

    def mlp_bwd(dmo, hid, act, h, l, tag):
        dact = _mm(dmo, wts["mlp_w_out"][l], tag + "_mlp_out_dx", tb=True)
        dw_out = _mm(act, dmo, tag + "_mlp_out_dw", ta=True, out_dtype=bf16)
        (dhid,) = _rowcall(f_sqrelu_bwd, tag + "_sqrelu_bwd", T_LAT, TM, [_rin(hid), _rin(dact)], [], [(F, bf16)])
        dh = _mm(dhid, wts["mlp_w_in"][l], tag + "_mlp_in_dx", tb=True)
        dw_in = _mm(h, dhid, tag + "_mlp_in_dw", ta=True, out_dtype=bf16)
        return dh, dw_in, dw_out

    def f_resid_bwd(i, dx, out, gate, bias):
        return dx * gate, jnp.sum(dx * (out + bias), axis=0, keepdims=True), gate * jnp.sum(dx, axis=0, keepdims=True)

    def resid_bwd(dx, out, gate, bias, tag):
        return _rowcall(f_resid_bwd, tag + "_resid_bwd", T_LAT, TM, [_rin(dx), _rin(out)], [gate, bias],
                        [(D, bf16)], [(1, D)] * 2)

    dh3, dw_in1, dw_out1 = mlp_bwd(dmo1, hid1, act1, h3, 1, "l1")
    dx3, dng11, dsc2_1, dsh2_1 = normmod_bwd(x3, dh3, dx3, ng[1, 1][None], sc2[1], sh2[1], "l1b")

    dout1, dg1_1, db_pw2 = resid_bwd(dx3, out1, g1[1], wts["conf_b_pw2"], "l1a")
    ds = _mm(dout1, wts["conf_w_pw2"], "l1_pw2_dx", tb=True)
    grads["conf_w_pw2"] = _mm(sb, dout1, "l1_pw2_dw", ta=True, out_dtype=bf16)
    grads["conf_b_pw2"] = db_pw2

    def f_lnsilu_bwd(i, z, ds_, lg, lb):
        _, vjp = jax.vjp(ln_silu, z, lg, lb)
        return vjp(ds_)

    dzc, dln_g, dln_b = _rowcall(f_lnsilu_bwd, "l1_ln_silu_bwd", T_LAT, TM, [_rin(zc), _rin(ds)],
                                 [wts["conf_ln_g"], wts["conf_ln_b"]], [(D, f32)], [(1, D)] * 2)
    grads["conf_ln_g"], grads["conf_ln_b"] = dln_g, dln_b
    dzg, dconv_w, dconv_b = _dwconv_bwd([dzc], zg, 0, wts["conf_conv_w"], 31, 15, ((0, T_LAT),), 128,
                                        "l1_conv_bwd", f32)
    grads["conf_conv_w"], grads["conf_conv_b"] = dconv_w, dconv_b

    def f_glu_bwd(i, pa, pb, dz, b1):
        _, vjp = jax.vjp(lambda a, b, c: (a + c[:, :D]) * _sigmoid(b + c[:, D:]), pa, pb, b1)
        da, db, dc = vjp(dz)
        return jnp.concatenate([da, db], axis=1), dc

    dpw, db_pw1 = _rowcall(f_glu_bwd, "l1_glu_bwd", T_LAT, TM, [_rin(pw, D, 0), _rin(pw, D, 1), _rin(dzg)],
                           [wts["conf_b_pw1"]], [(2 * D, bf16)], [(1, 2 * D)])
    grads["conf_b_pw1"] = db_pw1
    dh2 = _mm(dpw, wts["conf_w_pw1"], "l1_pw1_dx", tb=True)
    grads["conf_w_pw1"] = _mm(h2, dpw, "l1_pw1_dw", ta=True, out_dtype=bf16)
    dx2, dng10, dsc1_1, dsh1_1 = normmod_bwd(x2, dh2, dx3, ng[1, 0][None], sc1[1], sh1[1], "l1a")

    dmo0, dg2_0, _ = resid_bwd(dx2, mo0, g2[0], zero_d, "l0b")
    dh1, dw_in0, dw_out0 = mlp_bwd(dmo0, hid0, act0, h1, 0, "l0")
    grads["mlp_w_in"] = (dw_in0, dw_in1)
    grads["mlp_w_out"] = (dw_out0, dw_out1)
    dx1, dng01, dsc2_0, dsh2_0 = normmod_bwd(x1, dh1, dx2, ng[0, 1][None], sc2[0], sh2[0], "l0b")

    dout0, dg1_0, _ = resid_bwd(dx1, out0, g1[0], zero_d, "l0a")
    dz = _mm(dout0, wts["rec_w_out"], "l0_out_proj_dx", tb=True)
    grads["rec_w_out"] = _mm(zb, dout0, "l0_out_proj_dw", ta=True, out_dtype=bf16)

    def f_gate_bwd(i, gp, y0_, y1_, dz_):
        lat = jnp.where(i < N_CTX_TILES, 0.0, 1.0)
        _, vjp = jax.vjp(lambda a, b: _gelu(a) * b, gp, y0_ + y1_)
        dgp, dy = vjp(dz_)
        return dgp * lat, dy * lat

    dgp, dy = _rowcall(f_gate_bwd, "l0_gate_bwd", T_ALL, TM,
                       [_rin(gr, R, 0), _rin(y0), _rin(y1), _rin(dz, None, 0, -N_CTX_TILES)], [],
                       [(R, bf16), (R, f32)])
    (dh_f,) = _scan_call(a0, dy, FWD_SEGMENTS_BWD, "l0_scan_fwd_bwd", True)
    (dh_r,) = _scan_call(a1, dy, REV_SEGMENTS_BWD, "l0_scan_rev_bwd", True)

    def f_coeff_bwd(i, pre_, u_, dhf, dhr, ypf, ypr, ba, bx, lam):
        dpre, dba, dbx, dlam = [], [], [], []
        du = jnp.zeros_like(u_)
        for d, (dh_, yp_) in enumerate(((dhf, ypf), (dhr, ypr))):
            _, vjp = jax.vjp(_coeff, pre_[:, 2 * d * R:(2 * d + 1) * R], pre_[:, (2 * d + 1) * R:(2 * d + 2) * R], u_,
                             ba[d:d + 1], bx[d:d + 1], lam[d:d + 1])
            dpa, dpx, du_d, dba_d, dbx_d, dlam_d = vjp((dh_ * yp_, dh_))
            dpre += [dpa, dpx]
            du = du + du_d
            dba.append(dba_d)
            dbx.append(dbx_d)
            dlam.append(dlam_d)
        return jnp.concatenate(dpre, axis=1), du, _rows2(*dba), _rows2(*dbx), _rows2(*dlam)

    dpre, du_direct, db_a, db_x, dlam = _rowcall(
        f_coeff_bwd, "l0_coeff_bwd", T_ALL, 64,
        [_rin(pre), _rin(u), _rin(dh_f), _rin(dh_r), _rin(yp0), _rin(yp1)],
        [wts["rec_b_a"], wts["rec_b_x"], wts["rec_lambda"]], [(4 * R, bf16), (R, f32)], [(2, R)] * 3)
    grads["rec_b_a"], grads["rec_b_x"], grads["rec_lambda"] = db_a, db_x, dlam
    du_gates = _mm(dpre, wts["gates"], "l0_gates_dx", tb=True)
    grads["gates"] = _mm(ub, dpre, "l0_gates_dw", ta=True)
    drec, dconv4_w, dconv4_b = _dwconv_bwd([du_direct, du_gates], gr, R // 256, wts["rec_conv_w"], 4, 1,
                                           CONV_SEGMENTS, 256, "l0_conv_bwd", bf16)
    grads["rec_conv_w"], grads["rec_conv_b"] = dconv4_w, dconv4_b
    dgr = jnp.concatenate([dgp, drec], axis=1)
    dh0 = _mm(dgr, wts["rec_w_in"], "l0_in_proj_dx", tb=True)
    grads["rec_w_in"] = _mm(h0, dgr, "l0_in_proj_dw", ta=True, out_dtype=bf16)

    def f_pre0_bwd(i, x0, dh_, dxs, g, scp_, shp_):
        lat = jnp.where(i < N_CTX_TILES, 0.0, 1.0)
        _, vjp = jax.vjp(lambda a, b, c, e: _normmod(a, b, blend(i, c), blend(i, e)), x0, g, scp_, shp_)
        dx, dg, dscp, dshp = vjp(dh_)
        return dx + lat * dxs, dg, dscp, dshp

    dx0cat, dng00, dscp, dshp = _rowcall(
        f_pre0_bwd, "l0_prenorm_bwd", T_ALL, TM, [_rin(x0cat), _rin(dh0), _rin(dx1, None, 0, -N_CTX_TILES)],
        [ng[0, 0][None], scp, shp], [(D, f32)], [(1, D), (2, D), (2, D)])

    grads["norm_g"] = jnp.stack([jnp.concatenate([dng00, dng01], 0), jnp.concatenate([dng10, dng11], 0)])
    dmods = jnp.stack([
        jnp.concatenate([dshp[1:2], dscp[1:2], dg1_0, dsh2_0, dsc2_0, dg2_0], axis=0),
        jnp.concatenate([dsh1_1, dsc1_1, dg1_1, dsh2_1, dsc2_1, dg2_1], axis=0)])
    dcmod = jnp.concatenate([dshp[0:1], dscp[0:1]], axis=0)
    return loss_acc[0, 0], dx0cat[T_CTX:], dmods, dcmod, grads


def _unshard_cols(g):
    g = jnp.moveaxis(g, 0, -2)
    return g.reshape(g.shape[:-2] + (g.shape[-2] * g.shape[-1],))


def _shard_cols(w):
    w = w.reshape(w.shape[:-1] + (N_DEV, w.shape[-1] // N_DEV))
    return jnp.moveaxis(w, -2, 0)


def _shard_rows(w):
    return w.reshape((N_DEV, w.shape[0] // N_DEV) + w.shape[1:])


SMALL_PACK_ROWS = 64


def kernel(x, c, ctx, c_ctx, w_ada, b_ada, norm_g, rec_w_in, rec_conv_w, rec_conv_b, rec_lambda, rec_w_a, rec_b_a, rec_w_x, rec_b_x, rec_w_out, conf_w_pw1, conf_b_pw1, conf_conv_w, conf_conv_b, conf_ln_g, conf_ln_b, conf_w_pw2, conf_b_pw2, mlp_w_in, mlp_w_out, final_g, loss_target, m_c_ctx, m_w_ada, m_b_ada, m_norm_g, m_rec_w_in, m_rec_conv_w, m_rec_conv_b, m_rec_lambda, m_rec_w_a, m_rec_b_a, m_rec_w_x, m_rec_b_x, m_rec_w_out, m_conf_w_pw1, m_conf_b_pw1, m_conf_conv_w, m_conf_conv_b, m_conf_ln_g, m_conf_ln_b, m_conf_w_pw2, m_conf_b_pw2, m_mlp_w_in, m_mlp_w_out, m_final_g, v_c_ctx, v_w_ada, v_b_ada, v_norm_g, v_rec_w_in, v_rec_conv_w, v_rec_conv_b, v_rec_lambda, v_rec_w_a, v_rec_b_a, v_rec_w_x, v_rec_b_x, v_rec_w_out, v_conf_w_pw1, v_conf_b_pw1, v_conf_conv_w, v_conf_conv_b, v_conf_ln_g, v_conf_ln_b, v_conf_w_pw2, v_conf_b_pw2, v_mlp_w_in, v_mlp_w_out, v_final_g):
    me = 4 * lax.axis_index("x") + 2 * lax.axis_index("y") + lax.axis_index("c")
    weights = dict(c_ctx=c_ctx, w_ada=w_ada, b_ada=b_ada, norm_g=norm_g, rec_w_in=rec_w_in, rec_conv_w=rec_conv_w,
                   rec_conv_b=rec_conv_b, rec_lambda=rec_lambda, rec_w_a=rec_w_a, rec_b_a=rec_b_a, rec_w_x=rec_w_x,
                   rec_b_x=rec_b_x, rec_w_out=rec_w_out, conf_w_pw1=conf_w_pw1, conf_b_pw1=conf_b_pw1,
                   conf_conv_w=conf_conv_w, conf_conv_b=conf_conv_b, conf_ln_g=conf_ln_g, conf_ln_b=conf_ln_b,
                   conf_w_pw2=conf_w_pw2, conf_b_pw2=conf_b_pw2, mlp_w_in=mlp_w_in, mlp_w_out=mlp_w_out, final_g=final_g)
    m_in = dict(c_ctx=m_c_ctx, w_ada=m_w_ada, b_ada=m_b_ada, norm_g=m_norm_g, rec_w_in=m_rec_w_in, rec_conv_w=m_rec_conv_w,
                rec_conv_b=m_rec_conv_b, rec_lambda=m_rec_lambda, rec_w_a=m_rec_w_a, rec_b_a=m_rec_b_a, rec_w_x=m_rec_w_x,
                rec_b_x=m_rec_b_x, rec_w_out=m_rec_w_out, conf_w_pw1=m_conf_w_pw1, conf_b_pw1=m_conf_b_pw1,
                conf_conv_w=m_conf_conv_w, conf_conv_b=m_conf_conv_b, conf_ln_g=m_conf_ln_g, conf_ln_b=m_conf_ln_b,
                conf_w_pw2=m_conf_w_pw2, conf_b_pw2=m_conf_b_pw2, mlp_w_in=m_mlp_w_in, mlp_w_out=m_mlp_w_out,
                final_g=m_final_g)
    v_in = dict(c_ctx=v_c_ctx, w_ada=v_w_ada, b_ada=v_b_ada, norm_g=v_norm_g, rec_w_in=v_rec_w_in, rec_conv_w=v_rec_conv_w,
                rec_conv_b=v_rec_conv_b, rec_lambda=v_rec_lambda, rec_w_a=v_rec_w_a, rec_b_a=v_rec_b_a, rec_w_x=v_rec_w_x,
                rec_b_x=v_rec_b_x, rec_w_out=v_rec_w_out, conf_w_pw1=v_conf_w_pw1, conf_b_pw1=v_conf_b_pw1,
                conf_conv_w=v_conf_conv_w, conf_conv_b=v_conf_conv_b, conf_ln_g=v_conf_ln_g, conf_ln_b=v_conf_ln_b,
                conf_w_pw2=v_conf_w_pw2, conf_b_pw2=v_conf_b_pw2, mlp_w_in=v_mlp_w_in, mlp_w_out=v_mlp_w_out,
                final_g=v_final_g)
    names = list(weights)

    small_items = [c, norm_g, rec_conv_w, rec_lambda, conf_b_pw1, conf_conv_w, conf_conv_b, conf_ln_g, conf_ln_b,
                   conf_b_pw2]
    flat = jnp.concatenate([a.reshape(-1) for a in small_items])
    flat = jnp.pad(flat, (0, SMALL_PACK_ROWS * 128 - flat.shape[0])).reshape(SMALL_PACK_ROWS, 128)
    big_items = [rec_w_in[0], rec_w_out[0], conf_w_pw1[0], conf_w_pw2[0], mlp_w_in, mlp_w_out]
    (small_all,) = _all_gather([flat], "gather_small")
    big_flat = jnp.concatenate([a.astype(bf16).reshape(-1) for a in big_items]).reshape(-1, D)
    big_gathered = _all_gather_2level(big_flat, "gather_weights").reshape(N_DEV, -1)
    big_all, off = [], 0
    for a in big_items:
        big_all.append(big_gathered[:, off:off + a.size].reshape((N_DEV,) + a.shape))
        off += a.size

    small_all = small_all.reshape(N_DEV, -1)
    off = 0
    small = []
    for a in small_items:
        small.append(small_all[:, off:off + a.size].reshape((N_DEV,) + a.shape))
        off += a.size
    c_all, ng_all, rcw_all, lam_all, bpw1_all, ccw_all, ccb_all, lng_all, lnb_all, bpw2_all = small
    wts = {
        "norm_g": _unshard_cols(ng_all),
        "rec_conv_w": _unshard_cols(rcw_all)[0],
        "rec_lambda": _unshard_cols(lam_all)[0],
        "conf_b_pw1": _unshard_cols(bpw1_all),
        "conf_conv_w": _unshard_cols(ccw_all)[0],
        "conf_conv_b": _unshard_cols(ccb_all),
        "conf_ln_g": _unshard_cols(lng_all),
        "conf_ln_b": _unshard_cols(lnb_all),
        "conf_b_pw2": _unshard_cols(bpw2_all),
        "rec_conv_b": rec_conv_b,
        "rec_b_a": rec_b_a[0].reshape(2, R),
        "rec_b_x": rec_b_x[0].reshape(2, R),
        "final_g": final_g[None],
        "gates": _dense_gates(rec_w_a[0], rec_w_x[0]),
        "rec_w_in": _unshard_cols(big_all[0]),
        "rec_w_out": big_all[1].reshape(R, D),
        "conf_w_pw1": _unshard_cols(big_all[2]),
        "conf_w_pw2": big_all[3].reshape(D, D),
        "mlp_w_in": _unshard_cols(big_all[4]),
        "mlp_w_out": jnp.moveaxis(big_all[5], 0, 1).reshape(2, F, D),
    }

    c16 = jnp.concatenate([c_all[:, 0], jnp.broadcast_to(c_ctx[None], (8, D))], axis=0)
    b_loc = lax.dynamic_slice_in_dim(b_ada, me * ADA_SHARD, ADA_SHARD, axis=1)[:, None]
    (mods_all,) = _all_gather([_ada_forward(c16, w_ada, b_loc)], "gather_mods")
    mods_all = _unshard_cols(mods_all)
    mods = lax.dynamic_index_in_dim(mods_all, me, axis=1, keepdims=False).reshape(2, N_MOD, D)
    cmod = mods_all[0, 8, :2 * D].reshape(2, D)

    loss_part, grad_x, dmods, dcmod, grads = _local_step(x[0], ctx[0], loss_target[0], mods, cmod, wts)
    loss = lax.psum(loss_part, ("x", "y", "c"))

    dm_flat = jnp.concatenate([dmods.reshape(-1), dcmod.reshape(-1)]).reshape(-1, 128)
    (dm_all,) = _all_gather([dm_flat], "gather_dmods")
    dm_all = dm_all.reshape(N_DEV, -1)
    dmods_all = dm_all[:, :2 * N_MOD * D].reshape(N_DEV, 2, N_MOD * D)
    dcmod_all = jnp.pad(dm_all[:, 2 * N_MOD * D:], ((0, 0), (0, (N_MOD - 2) * D)))
    g16_full = jnp.stack([jnp.concatenate([dmods_all[:, 0], dcmod_all], axis=0),
                          jnp.concatenate([dmods_all[:, 1], jnp.zeros_like(dcmod_all)], axis=0)])
    g16 = lax.dynamic_slice_in_dim(g16_full, me * ADA_SHARD, ADA_SHARD, axis=2)
    dw_ada, ds_part = _ada_backward(c16, g16, w_ada)
    (ds_all,) = _all_gather([ds_part[0]], "gather_dsilu")

    big_grads = [_shard_cols(grads["rec_w_in"]), _shard_rows(grads["rec_w_out"]), _shard_cols(grads["conf_w_pw1"]),
                 _shard_rows(grads["conf_w_pw2"]),
                 jnp.stack([_shard_cols(g) for g in grads["mlp_w_in"]], axis=1),
                 jnp.stack([_shard_rows(g) for g in grads["mlp_w_out"]], axis=1)]
    small_sharded = ["norm_g", "rec_conv_w", "rec_lambda", "conf_b_pw1", "conf_conv_w", "conf_conv_b", "conf_ln_g",
                     "conf_ln_b", "conf_b_pw2"]
    pack = jnp.concatenate([_shard_cols(grads[n]).reshape(N_DEV, -1) for n in small_sharded], axis=1)
    pack_len = pack.shape[1]
    pack = jnp.pad(pack, ((0, 0), (0, SMALL_PACK_ROWS * 128 - pack_len))).reshape(N_DEV, SMALL_PACK_ROWS, 128)
    big_names = ["rec_w_in", "rec_w_out", "conf_w_pw1", "conf_w_pw2", "mlp_w_in", "mlp_w_out"]
    big_pack = jnp.concatenate([g.reshape(N_DEV, -1) for g in big_grads], axis=1).reshape(N_DEV, -1, D)
    big_f32, big_last = _reduce_scatter(big_pack, "scatter_weight_grads")
    big_f32, big_last = big_f32.reshape(-1), big_last.reshape(-1)
    (pack_recv,) = _all_to_all([pack], "scatter_small_grads")
    pack_recv = pack_recv.reshape(N_DEV, -1)

    dwg = grads["gates"]
    repl = {"rec_conv_b": grads["rec_conv_b"],
            "rec_w_a": jnp.stack([_gate_blocks(dwg, 0), _gate_blocks(dwg, 2)])[None],
            "rec_w_x": jnp.stack([_gate_blocks(dwg, 1), _gate_blocks(dwg, 3)])[None],
            "rec_b_a": grads["rec_b_a"].reshape(1, 2, N_BLK, BLK),
            "rec_b_x": grads["rec_b_x"].reshape(1, 2, N_BLK, BLK),
            "final_g": grads["final_g"][0]}
    repl_names = list(repl)
    repl_flat = jnp.concatenate([repl[n].reshape(-1) for n in repl_names])
    repl_len = repl_flat.shape[0]
    repl_rows = -(-repl_len // (16 * D)) * 16
    repl_flat = jnp.pad(repl_flat, (0, repl_rows * D - repl_len)).reshape(repl_rows, D).astype(bf16)
    repl_all = _all_gather_2level(repl_flat, "gather_replicated_grads").reshape(N_DEV, -1)

    pieces = {}
    shard_shapes = {n: weights[n].shape for n in names}
    off = 0
    for n in big_names:
        size = weights[n].size
        pieces[n] = [big_f32[off:off + size], big_last[off:off + size]]
        off += size
    off = 0
    for n in small_sharded:
        size = weights[n].size
        pieces[n] = [pack_recv[:, off:off + size]]
        off += size
    off = 0
    for n in repl_names:
        size = weights[n].size
        pieces[n] = [repl_all[:, off:off + size]]
        off += size
    pieces["w_ada"] = [dw_ada]
    db_terms = jnp.concatenate([dmods_all, jnp.stack([dcmod_all, jnp.zeros_like(dcmod_all)], axis=1)], axis=0)
    pieces["b_ada"] = [db_terms]
    pieces["c_ctx"] = [ds_all[:, 0]]

    def as2d(shape):
        if len(shape) == 1:
            return (1, shape[0])
        if len(shape) == 5:
            return (shape[0] * shape[1] * shape[2], shape[3] * shape[4])
        rows = 1
        for s in shape[:-1]:
            rows *= s
        return (rows, shape[-1])

    g_out, d_out, m_out, v_out = {}, {}, {}, {}
    for n in names:
        shape = shard_shapes[n]
        r2, c2 = as2d(shape)
        p = [piece.reshape(-1, r2, c2) for piece in pieces[n]]
        g, dl, nm, nv = _adamw(p, weights[n].reshape(r2, c2), m_in[n].reshape(r2, c2), v_in[n].reshape(r2, c2),
                               "adamw_" + n)
        g_out[n], d_out[n], m_out[n], v_out[n] = (t.reshape(shape) for t in (g, dl, nm, nv))

    return (loss, grad_x[None], *[g_out[n] for n in names], *[d_out[n] for n in names],
            *[m_out[n] for n in names], *[v_out[n] for n in names])
```

```python
import functools

import jax
import jax.numpy as jnp
from jax import lax
from jax.experimental import pallas as pl
from jax.experimental.pallas import tpu as pltpu

f32 = jnp.float32
bf16 = jnp.bfloat16

N_DEV = 8
D = 1024
T_LAT = 2048
T_CTX = 256
T_ALL = T_CTX + T_LAT
R = 1280
N_BLK = 16
BLK = R // N_BLK
F = 4096
GRID_W = 64
RG_C = 8.0
EPS = 1e-6
POS_BASE = 10000.0
N_MOD = 6
ADA_SHARD = N_MOD * D // N_DEV

ADAM_LR = 0.001
ADAM_B1 = 0.9
ADAM_B2 = 0.999
ADAM_EPS = 1e-08
ADAM_WD = 0.01
ADAM_STEP = 10

VMEM_LIMIT_V7X = 56 * 1024 * 1024
HALO = 16
MESH = pl.DeviceIdType.MESH


def _cparams(*sem):
    return pltpu.CompilerParams(dimension_semantics=sem, vmem_limit_bytes=VMEM_LIMIT_V7X)


def _pick(n, cands):
    for c in cands:
        if n % c == 0:
            return c
    raise ValueError(f"no block size for {n}")


def _position():
    x, y, c = lax.axis_index("x"), lax.axis_index("y"), lax.axis_index("c")
    return x, y, c, 4 * x + 2 * y + c


def _peer(x, y, c, k):
    px = (1 - x) if (k >> 2) & 1 else x
    py = (1 - y) if (k >> 1) & 1 else y
    pc = (1 - c) if k & 1 else c
    return (px, py, pc), 4 * px + 2 * py + pc


def _exchange(arrs, name, scatter):
    n = len(arrs)

    def body(*refs):
        ins, outs = refs[:n], refs[n:2 * n]
        send_sems, recv_sems, local_sems = refs[2 * n:]
        x, y, c, me = _position()
        local = []
        for a in range(n):
            src = ins[a].at[me] if scatter else ins[a]
            cp = pltpu.make_async_copy(src, outs[a].at[me], local_sems.at[a])
            cp.start()
            local.append(cp)
        sends, recvs = [], []
        for a in range(n):
            for k in range(1, N_DEV):
                peer, peer_lin = _peer(x, y, c, k)
                src = ins[a].at[peer_lin] if scatter else ins[a]
                cp = pltpu.make_async_remote_copy(
                    src_ref=src, dst_ref=outs[a].at[me], send_sem=send_sems.at[a, k - 1],
                    recv_sem=recv_sems.at[a, k - 1], device_id=peer, device_id_type=MESH)
                cp.start()
                sends.append(cp)
                recvs.append(pltpu.make_async_remote_copy(
                    src_ref=src, dst_ref=outs[a].at[peer_lin], send_sem=send_sems.at[a, k - 1],
                    recv_sem=recv_sems.at[a, k - 1], device_id=peer, device_id_type=MESH))
        for cp in recvs:
            cp.wait_recv()
        for cp in sends:
            cp.wait_send()
        for cp in local:
            cp.wait()

    if scatter:
        out_shape = [jax.ShapeDtypeStruct(a.shape, a.dtype) for a in arrs]
    else:
        out_shape = [jax.ShapeDtypeStruct((N_DEV,) + a.shape, a.dtype) for a in arrs]
    any_spec = pl.BlockSpec(memory_space=pl.ANY)
    return pl.pallas_call(
        body, name=name, out_shape=out_shape,
        in_specs=[any_spec] * n, out_specs=[any_spec] * n,
        scratch_shapes=[pltpu.SemaphoreType.DMA((n, N_DEV - 1)), pltpu.SemaphoreType.DMA((n, N_DEV - 1)),
                        pltpu.SemaphoreType.DMA((n,))],
    )(*arrs)


def _all_gather(arrs, name):
    return _exchange(arrs, name, scatter=False)


def _all_to_all(arrs, name):
    return _exchange(arrs, name, scatter=True)


def _lin(p):
    return 4 * p[0] + 2 * p[1] + p[2]


def _comm_call(body, name, ins, out_shape, n_sems, extra_scratch=()):
    any_spec = pl.BlockSpec(memory_space=pl.ANY)
    return pl.pallas_call(
        body, name=name, out_shape=out_shape, in_specs=[any_spec] * len(ins), out_specs=[any_spec] * len(out_shape),
        scratch_shapes=[pltpu.SemaphoreType.DMA((n_sems,)), pltpu.SemaphoreType.DMA((n_sems,))] + list(extra_scratch),
    )(*ins)


def _staged_copy(src, dst, buf, in_sems, out_sems, rows, chunk):
    n = rows // chunk

    def rd(i):
        return pltpu.make_async_copy(src.at[pl.ds(i * chunk, chunk)], buf.at[i % 2], in_sems.at[i % 2])

    def wr(i):
        return pltpu.make_async_copy(buf.at[i % 2], dst.at[pl.ds(i * chunk, chunk)], out_sems.at[i % 2])

    rd(0).start()
    for i in range(n):
        if i + 1 < n:
            if i >= 1:
                wr(i - 1).wait()
            rd(i + 1).start()
        rd(i).wait()
        wr(i).start()
    for i in range(max(n - 2, 0), n):
        wr(i).wait()


def _all_gather_2level(shard, name):
    rows, cols = shard.shape
    chunk = _pick(rows, (416, 512, 256, 128, 64, 16))

    def body(in_ref, out_ref, send_sems, recv_sems, buf, in_sems, out_sems):
        x, y, c, me = _position()
        sib, xn, yn, dg = (x, y, 1 - c), (1 - x, y, c), (x, 1 - y, c), (1 - x, 1 - y, c)

        def cp(k, src, slot, to):
            return pltpu.make_async_remote_copy(src_ref=src, dst_ref=out_ref.at[slot], send_sem=send_sems.at[k],
                                                recv_sem=recv_sems.at[k], device_id=to, device_id_type=MESH)

        for k, to in ((0, sib), (1, xn), (2, yn)):
            cp(k, in_ref, me, to).start()
        cp(1, in_ref, _lin(xn), xn).wait_recv()
        cp(3, out_ref.at[_lin(xn)], _lin(xn), sib).start()

        @pl.when(c == 0)
        def _():
            cp(5, out_ref.at[_lin(xn)], _lin(xn), yn).start()

        cp(2, in_ref, _lin(yn), yn).wait_recv()
        cp(4, out_ref.at[_lin(yn)], _lin(yn), sib).start()

        @pl.when(c == 1)
        def _():
            cp(5, out_ref.at[_lin(yn)], _lin(yn), xn).start()

        cp(5, in_ref, _lin(dg), xn).wait_recv()
        cp(6, out_ref.at[_lin(dg)], _lin(dg), sib).start()
        _staged_copy(in_ref, out_ref.at[me], buf, in_sems, out_sems, rows, chunk)
        for k, origin in ((0, sib), (3, (1 - x, y, 1 - c)), (4, (x, 1 - y, 1 - c)), (6, (1 - x, 1 - y, 1 - c))):
            cp(k, in_ref, _lin(origin), sib).wait_recv()
        for k in range(7):
            cp(k, in_ref, me, sib).wait_send()

    (out,) = _comm_call(body, name, [shard], [jax.ShapeDtypeStruct((N_DEV,) + shard.shape, shard.dtype)], 7,
                        [pltpu.VMEM((2, chunk, cols), shard.dtype), pltpu.SemaphoreType.DMA((2,)),
                         pltpu.SemaphoreType.DMA((2,))])
    return out


def _plane_pos(x, y, q):
    return ((1 - x) if q & 2 else x, (1 - y) if q & 1 else y)


def _scatter_d2d(g, name):
    def body(g_ref, recv_ref, send_sems, recv_sems):
        x, y, c, me = _position()
        sib = (x, y, 1 - c)
        sends = []
        for q in range(4):
            px, py = _plane_pos(x, y, q)
            cp = pltpu.make_async_remote_copy(src_ref=g_ref.at[_lin((px, py, 1 - c))], dst_ref=recv_ref.at[q],
                                              send_sem=send_sems.at[q], recv_sem=recv_sems.at[q], device_id=sib,
                                              device_id_type=MESH)
            cp.start()
            sends.append(cp)
        for cp in sends:
            cp.wait_recv()
        for cp in sends:
            cp.wait_send()

    (recv,) = _comm_call(body, name, [g], [jax.ShapeDtypeStruct((4,) + g.shape[1:], g.dtype)], 4)
    return recv


def _scatter_ici_first(h, name):
    def body(h_ref, recv_ref, send_sems, recv_sems):
        x, y, c, me = _position()
        xn, yn = (1 - x, y, c), (x, 1 - y, c)

        def cp(k, q, to):
            return pltpu.make_async_remote_copy(src_ref=h_ref.at[q], dst_ref=recv_ref.at[k], send_sem=send_sems.at[k],
                                                recv_sem=recv_sems.at[k], device_id=to, device_id_type=MESH)

        @pl.when(c == 0)
        def _():
            cp(0, 2, xn).start()
            cp(1, 3, xn).start()

        @pl.when(c == 1)
        def _():
            cp(0, 1, yn).start()
            cp(1, 3, yn).start()

        for k in range(2):
            cp(k, 0, xn).wait_recv()
        for k in range(2):
            cp(k, 0, xn).wait_send()

    (recv,) = _comm_call(body, name, [h], [jax.ShapeDtypeStruct((2,) + h.shape[1:], h.dtype)], 2)
    return recv


def _scatter_ici_second(k1, name):
    def body(k_ref, recv_ref, send_sems, recv_sems):
        x, y, c, me = _position()
        xn, yn = (1 - x, y, c), (x, 1 - y, c)

        def cp(to):
            return pltpu.make_async_remote_copy(src_ref=k_ref, dst_ref=recv_ref, send_sem=send_sems.at[0],
                                                recv_sem=recv_sems.at[0], device_id=to, device_id_type=MESH)

        @pl.when(c == 0)
        def _():
            cp(yn).start()

        @pl.when(c == 1)
        def _():
            cp(xn).start()

        cp(xn).wait_recv()
        cp(xn).wait_send()

    (out,) = _comm_call(body, name, [k1], [jax.ShapeDtypeStruct(k1.shape, k1.dtype)], 1)
    return out


def _add_blocks(a, a_idx, b, b_idx, out_dtype, name):
    rows, cols = a.shape[1:]
    n = a_idx.shape[0]
    tm = _pick(rows, (416, 512, 256, 128, 8))

    def body(ia_ref, ib_ref, a_ref, b_ref, o_ref):
        o_ref[...] = (a_ref[...].astype(f32) + b_ref[...].astype(f32)).astype(o_ref.dtype)

    grid_spec = pltpu.PrefetchScalarGridSpec(
        num_scalar_prefetch=2, grid=(n, rows // tm),
        in_specs=[pl.BlockSpec((None, tm, cols), lambda j, i, ia, ib: (ia[j], i, 0)),
                  pl.BlockSpec((None, tm, cols), lambda j, i, ia, ib: (ib[j], i, 0))],
        out_specs=pl.BlockSpec((None, tm, cols), lambda j, i, ia, ib: (j, i, 0)))
    return pl.pallas_call(body, name=name, out_shape=jax.ShapeDtypeStruct((n, rows, cols), out_dtype),
                          grid_spec=grid_spec, compiler_params=_cparams("parallel", "parallel"))(a_idx, b_idx, a, b)


def _reduce_scatter(g, tag):
    x, y, c, me = _position()
    i32 = lambda *v: jnp.stack([jnp.asarray(t, jnp.int32) for t in v])
    recv = _scatter_d2d(g, tag + "_d2d")
    own_idx = i32(*[_lin(_plane_pos(x, y, q) + (c,)) for q in range(4)])
    h = _add_blocks(g, own_idx, recv, i32(0, 1, 2, 3), bf16, tag + "_add_chip")
    recv2 = _scatter_ici_first(h, tag + "_ici_first")
    k0 = _add_blocks(h, i32(0), recv2, i32(0), f32, tag + "_add_mine")[0]
    k1 = _add_blocks(h, i32(1 + c), recv2, i32(1), bf16, tag + "_add_onward")[0]
    return k0, _scatter_ici_second(k1, tag + "_ici_second")


def _mm(a, b, name, ta=False, tb=False, out_dtype=f32):
    if ta:
        k_dim, m_dim = a.shape
    else:
        m_dim, k_dim = a.shape
    if tb:
        n_dim, k2 = b.shape
    else:
        k2, n_dim = b.shape
    assert k_dim == k2, (a.shape, b.shape)
    assert a.dtype == bf16 and b.dtype == bf16
    bm = _pick(m_dim, (512, 768, 640, 256, 128))
    bn = _pick(n_dim, (512, 640, 256, 128))
    bk = _pick(k_dim, (1024, 1280, 768, 512))
    nk = k_dim // bk
    a_spec = (pl.BlockSpec((bk, bm), lambda i, j, k: (k, i)) if ta
              else pl.BlockSpec((bm, bk), lambda i, j, k: (i, k)))
    b_spec = (pl.BlockSpec((bn, bk), lambda i, j, k: (j, k)) if tb
              else pl.BlockSpec((bk, bn), lambda i, j, k: (k, j)))
    dims = (((0 if ta else 1,), (1 if tb else 0,)), ((), ()))

    def body_single(a_ref, b_ref, o_ref):
        o_ref[...] = lax.dot_general(a_ref[...], b_ref[...], dims, preferred_element_type=f32).astype(o_ref.dtype)

    def body(a_ref, b_ref, o_ref, acc_ref):
        k = pl.program_id(2)

        @pl.when(k == 0)
        def _():
            acc_ref[...] = jnp.zeros_like(acc_ref)

        acc_ref[...] += lax.dot_general(a_ref[...], b_ref[...], dims, preferred_element_type=f32)

        @pl.when(k == nk - 1)
        def _():
            o_ref[...] = acc_ref[...].astype(o_ref.dtype)

    return pl.pallas_call(
        body_single if nk == 1 else body, name=name, out_shape=jax.ShapeDtypeStruct((m_dim, n_dim), out_dtype),
        grid=(m_dim // bm, n_dim // bn, nk), in_specs=[a_spec, b_spec],
        out_specs=pl.BlockSpec((bm, bn), lambda i, j, k: (i, j)),
        scratch_shapes=[] if nk == 1 else [pltpu.VMEM((bm, bn), f32)],
        compiler_params=_cparams("parallel", "parallel", "arbitrary"),
    )(a, b)


def _rin(arr, width=None, cb=0, roff=0):
    return (arr, arr.shape[1] if width is None else width, cb, roff)


def _rowcall(fn, name, rows, tm, row_ins, par_ins, row_outs, acc_outs=()):
    nr, npar, nro = len(row_ins), len(par_ins), len(row_outs)
    in_specs, args = [], []
    for arr, width, cb, roff in row_ins:
        if roff >= 0:
            imap = lambda i, cb=cb, roff=roff: (i + roff, cb)
        else:
            imap = lambda i, cb=cb, roff=roff: (jnp.maximum(i + roff, 0), cb)
        in_specs.append(pl.BlockSpec((tm, width), imap))
        args.append(arr)
    for p in par_ins:
        in_specs.append(pl.BlockSpec(p.shape, lambda i: (0, 0)))
        args.append(p)
    out_shape, out_specs = [], []
    for width, dt in row_outs:
        out_shape.append(jax.ShapeDtypeStruct((rows, width), dt))
        out_specs.append(pl.BlockSpec((tm, width), lambda i: (i, 0)))
    for p, width in acc_outs:
        out_shape.append(jax.ShapeDtypeStruct((p, width), f32))
        out_specs.append(pl.BlockSpec((p, width), lambda i: (0, 0)))

    def body(*refs):
        i = pl.program_id(0)
        res = fn(i, *[r[...] for r in refs[:nr + npar]])
        outs = refs[nr + npar:]
        for o, v in zip(outs[:nro], res[:nro]):
            o[...] = v.astype(o.dtype)
        if acc_outs:
            @pl.when(i == 0)
            def _():
                for o in outs[nro:]:
                    o[...] = jnp.zeros_like(o)

            for o, v in zip(outs[nro:], res[nro:]):
                o[...] += v

    return pl.pallas_call(
        body, name=name, out_shape=out_shape, grid=(rows // tm,), in_specs=in_specs, out_specs=out_specs,
        compiler_params=_cparams("arbitrary"),
    )(*args)


def _rms(x, g):
    return x * lax.rsqrt(jnp.mean(x * x, axis=-1, keepdims=True) + EPS) * g


def _normmod(x, g, sc, sh):
    return _rms(x, g) * (1.0 + sc) + sh


def _rows2(v0, v1):
    rid = lax.broadcasted_iota(jnp.int32, (2, v0.shape[1]), 0)
    return jnp.where(rid == 0, v0, v1)


def _gelu(x):
    return 0.5 * x * (1.0 + jnp.tanh(0.7978845608028654 * (x + 0.044715 * (x * x * x))))


def _sigmoid(x):
    return 1.0 / (1.0 + jnp.exp(-x))


def _coeff(pre_a, pre_x, u, ba, bx, lam):
    r = _sigmoid(pre_a + ba)
    ig = _sigmoid(pre_x + bx)
    nl = -lam
    sp = jnp.maximum(nl, 0.0) + jnp.log(1.0 + jnp.exp(-jnp.abs(nl)))
    la = -RG_C * r * sp
    a = jnp.exp(la)
    one_minus_a2 = -jnp.tanh(la) * (a * a + 1.0)
    return a, jnp.sqrt(one_minus_a2) * (ig * u)


def _scan_call(a, v, segments, name, backward):
    rows, width = a.shape
    cb = 256
    n_out = 1 if backward else 2

    def body(a_ref, v_ref, *outs):
        rid = lax.broadcasted_iota(jnp.int32, (8, cb), 0)
        state = jnp.zeros((1, cb), f32)
        for start, n, rev in segments:
            nt = n // 8

            def tile(j, st, start=start, nt=nt, rev=rev):
                t0 = pl.multiple_of(start + (nt - 1 - j if rev else j) * 8, 8)
                at = a_ref[pl.ds(t0, 8), :]
                vt = v_ref[pl.ds(t0, 8), :]
                out = jnp.zeros((8, cb), f32)
                prev = jnp.zeros((8, cb), f32)
                for i in (range(7, -1, -1) if rev else range(8)):
                    if backward:
                        g = vt[i:i + 1] + st
                        st = at[i:i + 1] * g
                        out = jnp.where(rid == i, g, out)
                    else:
                        prev = jnp.where(rid == i, st, prev)
                        st = at[i:i + 1] * st + vt[i:i + 1]
                        out = jnp.where(rid == i, st, out)
                outs[0][pl.ds(t0, 8), :] = out
                if not backward:
                    outs[1][pl.ds(t0, 8), :] = prev
                return st

            state = lax.fori_loop(0, nt, tile, state)

    spec = pl.BlockSpec((rows, cb), lambda j: (0, j))
    return pl.pallas_call(
        body, name=name, out_shape=[jax.ShapeDtypeStruct((rows, width), f32)] * n_out,
        grid=(width // cb,), in_specs=[spec, spec], out_specs=[spec] * n_out,
        compiler_params=_cparams("parallel"),
    )(a, v)


CONV_CHUNK = 256


def _fill_padded(pad_ref, src_ref, start, n):
    cb = pad_ref.shape[1]
    pad_ref[pl.ds(0, HALO), :] = jnp.zeros((HALO, cb), f32)
    pad_ref[pl.ds(HALO, n), :] = src_ref[pl.ds(start, n), :].astype(f32)
    pad_ref[pl.ds(HALO + n, HALO), :] = jnp.zeros((HALO, cb), f32)


def _dwconv_fwd(x, x_cb0, w, b, taps, pad_left, segments, cb, name, emit_bf16):
    rows = x.shape[0]
    width = w.shape[1]

    def body(x_ref, w_ref, b_ref, *rest):
        outs, xp = rest[:-1], rest[-1]
        for start, n in segments:
            _fill_padded(xp, x_ref, start, n)
            for c0 in range(0, n, CONV_CHUNK):
                acc = jnp.zeros((CONV_CHUNK, cb), f32) + b_ref[...]
                for k in range(taps):
                    acc = acc + w_ref[k:k + 1, :] * xp[pl.ds(HALO + c0 + k - pad_left, CONV_CHUNK), :]
                for o in outs:
                    o[pl.ds(start + c0, CONV_CHUNK), :] = acc.astype(o.dtype)

    out_dtypes = [f32, bf16] if emit_bf16 else [f32]
    return pl.pallas_call(
        body, name=name, out_shape=[jax.ShapeDtypeStruct((rows, width), dt) for dt in out_dtypes],
        grid=(width // cb,),
        in_specs=[pl.BlockSpec((rows, cb), lambda j: (0, j + x_cb0)), pl.BlockSpec((taps, cb), lambda j: (0, j)),
                  pl.BlockSpec((1, cb), lambda j: (0, j))],
        out_specs=[pl.BlockSpec((rows, cb), lambda j: (0, j))] * len(out_dtypes),
        scratch_shapes=[pltpu.VMEM((rows + 2 * HALO, cb), f32)],
        compiler_params=_cparams("parallel"),
    )(x, w, b)


def _dwconv_bwd(douts, x, x_cb0, w, taps, pad_left, segments, cb, name, dx_dtype):
    rows = x.shape[0]
    width = w.shape[1]
    nd = len(douts)

    def body(*refs):
        d_refs, x_ref, w_ref = refs[:nd], refs[nd], refs[nd + 1]
        dx_ref, dw_ref, db_ref, xp, dp, dsum = refs[nd + 2:]
        dw_ref[...] = jnp.zeros_like(dw_ref)
        db_ref[...] = jnp.zeros_like(db_ref)
        if nd > 1:
            total = d_refs[0][...]
            for r in d_refs[1:]:
                total = total + r[...]
            dsum[...] = total
            d_ref = dsum
        else:
            d_ref = d_refs[0]
        for start, n in segments:
            _fill_padded(xp, x_ref, start, n)
            _fill_padded(dp, d_ref, start, n)
            for c0 in range(0, n, CONV_CHUNK):
                dchunk = dp[pl.ds(HALO + c0, CONV_CHUNK), :]
                db_ref[...] += jnp.sum(dchunk, axis=0, keepdims=True)
                acc = jnp.zeros((CONV_CHUNK, cb), f32)
                for k in range(taps):
                    acc = acc + w_ref[k:k + 1, :] * dp[pl.ds(HALO + c0 + pad_left - k, CONV_CHUNK), :]
                    xs = xp[pl.ds(HALO + c0 + k - pad_left, CONV_CHUNK), :]
                    dw_ref[k:k + 1, :] += jnp.sum(dchunk * xs, axis=0, keepdims=True)
                dx_ref[pl.ds(start + c0, CONV_CHUNK), :] = acc.astype(dx_ref.dtype)

    dspec = pl.BlockSpec((rows, cb), lambda j: (0, j))
    return pl.pallas_call(
        body, name=name,
        out_shape=[jax.ShapeDtypeStruct((rows, width), dx_dtype), jax.ShapeDtypeStruct((taps, width), f32),
                   jax.ShapeDtypeStruct((1, width), f32)],
        grid=(width // cb,),
        in_specs=[dspec] * nd + [pl.BlockSpec((rows, cb), lambda j: (0, j + x_cb0)),
                                 pl.BlockSpec((taps, cb), lambda j: (0, j))],
        out_specs=[dspec, pl.BlockSpec((taps, cb), lambda j: (0, j)), pl.BlockSpec((1, cb), lambda j: (0, j))],
        scratch_shapes=[pltpu.VMEM((rows + 2 * HALO, cb), f32), pltpu.VMEM((rows + 2 * HALO, cb), f32),
                        pltpu.VMEM((rows, cb), f32)],
        compiler_params=_cparams("parallel"),
    )(*douts, x, w)


def _ada_forward(c16, w_ada, b_loc):
    def body(c_ref, w_ref, b_ref, o_ref):
        cv = c_ref[...]
        s = (cv * _sigmoid(cv)).astype(bf16)
        o_ref[0] = jnp.dot(s, w_ref[0].astype(bf16), preferred_element_type=f32) + b_ref[0]

    return pl.pallas_call(
        body, name="ada_forward", out_shape=jax.ShapeDtypeStruct((2, 16, ADA_SHARD), f32), grid=(2,),
        in_specs=[pl.BlockSpec((16, D), lambda l: (0, 0)), pl.BlockSpec((1, D, ADA_SHARD), lambda l: (l, 0, 0)),
                  pl.BlockSpec((1, 1, ADA_SHARD), lambda l: (l, 0, 0))],
        out_specs=pl.BlockSpec((1, 16, ADA_SHARD), lambda l: (l, 0, 0)),
        compiler_params=_cparams("parallel"),
    )(c16, w_ada, b_loc)


def _ada_backward(c16, g16, w_ada):
    def body(c_ref, g_ref, w_ref, dw_ref, ds_ref):
        cv = c_ref[...]
        s = (cv * _sigmoid(cv)).astype(bf16)
        g = g_ref[0].astype(bf16)
        dw_ref[0] = lax.dot_general(s, g, (((0,), (0,)), ((), ())), preferred_element_type=f32)
        ds = lax.dot_general(g, w_ref[0].astype(bf16), (((1,), (1,)), ((), ())), preferred_element_type=f32)
        cc = cv[8:9]
        sg = _sigmoid(cc)
        dsilu = sg * (1.0 + cc * (1.0 - sg))
        ds_ref[0] = jnp.zeros((8, D), f32) + jnp.sum(ds[8:16], axis=0, keepdims=True) * dsilu

    return pl.pallas_call(
        body, name="ada_backward",
        out_shape=[jax.ShapeDtypeStruct((2, D, ADA_SHARD), f32), jax.ShapeDtypeStruct((2, 8, D), f32)], grid=(2,),
        in_specs=[pl.BlockSpec((16, D), lambda l: (0, 0)), pl.BlockSpec((1, 16, ADA_SHARD), lambda l: (l, 0, 0)),
                  pl.BlockSpec((1, D, ADA_SHARD), lambda l: (l, 0, 0))],
        out_specs=[pl.BlockSpec((1, D, ADA_SHARD), lambda l: (l, 0, 0)), pl.BlockSpec((1, 8, D), lambda l: (l, 0, 0))],
        compiler_params=_cparams("parallel"),
    )(c16, g16, w_ada)


def _adamw(pieces, w, m, v, name):
    rows, cols = w.shape
    n_arr = len(pieces)
    tm = 256 if (rows % 256 == 0 and rows > 256) else rows

    def body(*refs):
        p_refs = refs[:n_arr]
        w_ref, m_ref, v_ref, g_ref, d_ref, nm_ref, nv_ref = refs[n_arr:]
        g = None
        for p_ref in p_refs:
            for j in range(p_ref.shape[0]):
                term = p_ref[j].astype(f32)
                g = term if g is None else g + term
        m2 = ADAM_B1 * m_ref[...] + (1.0 - ADAM_B1) * g
        v2 = ADAM_B2 * v_ref[...] + (1.0 - ADAM_B2) * (g * g)
        m_hat = m2 / (1.0 - ADAM_B1 ** ADAM_STEP)
        v_hat = v2 / (1.0 - ADAM_B2 ** ADAM_STEP)
        g_ref[...] = g
        d_ref[...] = -ADAM_LR * (m_hat / (jnp.sqrt(v_hat) + ADAM_EPS) + ADAM_WD * w_ref[...])
        nm_ref[...] = m2
        nv_ref[...] = v2

    spec = pl.BlockSpec((tm, cols), lambda i: (i, 0))
    return pl.pallas_call(
        body, name=name, out_shape=[jax.ShapeDtypeStruct((rows, cols), f32)] * 4, grid=(rows // tm,),
        in_specs=[pl.BlockSpec((p.shape[0], tm, cols), lambda i: (0, i, 0)) for p in pieces] + [spec, spec, spec],
        out_specs=[spec] * 4, compiler_params=_cparams("parallel"),
    )(*pieces, w, m, v)


def _pos_embed():
    t = jnp.arange(T_LAT, dtype=jnp.int32)
    row = (t // GRID_W).astype(f32)
    col = (t % GRID_W).astype(f32)
    q = D // 4
    omega = 1.0 / (POS_BASE ** (jnp.arange(q, dtype=f32) / q))
    er = row[:, None] * omega[None, :]
    ec = col[:, None] * omega[None, :]
    return jnp.concatenate([jnp.sin(er), jnp.cos(er), jnp.sin(ec), jnp.cos(ec)], axis=-1).astype(f32)


def _dense_gates(w_a, w_x):
    eye = jnp.eye(N_BLK, dtype=f32)
    parts = []
    for d in range(2):
        for w in (w_a, w_x):
            parts.append(jnp.einsum("hij,hg->higj", w[d], eye).reshape(R, R))
    return jnp.concatenate(parts, axis=1).astype(bf16)


def _gate_blocks(dwg, part):
    blk = dwg[:, part * R:(part + 1) * R].reshape(N_BLK, BLK, N_BLK, BLK)
    return jnp.moveaxis(jnp.diagonal(blk, axis1=0, axis2=2), -1, 0)


FWD_SEGMENTS = ((0, T_ALL, False),)
REV_SEGMENTS = ((0, T_CTX, True), (T_CTX, T_LAT, True))
FWD_SEGMENTS_BWD = ((0, T_ALL, True),)
REV_SEGMENTS_BWD = ((T_CTX, T_LAT, False), (0, T_CTX, False))
CONV_SEGMENTS = ((0, T_CTX), (T_CTX, T_LAT))
TM = 128
N_CTX_TILES = T_CTX // TM


def _local_step(x, ctx, target, mods, cmod, wts):
    sh1, sc1, g1, sh2, sc2, g2 = [[mods[l, i][None] for l in range(2)] for i in range(N_MOD)]
    ng = wts["norm_g"]
    xcat = jnp.concatenate([ctx, x], axis=0)
    poscat = jnp.concatenate([jnp.zeros((T_CTX, D), f32), _pos_embed()], axis=0)
    scp = jnp.concatenate([cmod[1][None], sc1[0]], axis=0)
    shp = jnp.concatenate([cmod[0][None], sh1[0]], axis=0)

    def blend(i, p):
        sel = jnp.where(i < N_CTX_TILES, 1.0, 0.0)
        return sel * p[0:1] + (1.0 - sel) * p[1:2]

    def f_pre0(i, xc, pos, g, scp_, shp_):
        x0 = xc + pos
        return x0, _normmod(x0, g, blend(i, scp_), blend(i, shp_))

    x0cat, h0 = _rowcall(f_pre0, "l0_prenorm", T_ALL, TM, [_rin(xcat), _rin(poscat)], [ng[0, 0][None], scp, shp],
                         [(D, f32), (D, bf16)])
    gr = _mm(h0, wts["rec_w_in"], "l0_in_proj")
    u, ub = _dwconv_fwd(gr, R // 256, wts["rec_conv_w"], wts["rec_conv_b"], 4, 1, CONV_SEGMENTS, 256,
                        "l0_conv", True)
    pre = _mm(ub, wts["gates"], "l0_gates")

    def f_coeff(i, pre_, u_, ba, bx, lam):
        outs = []
        for d in range(2):
            a, b = _coeff(pre_[:, 2 * d * R:(2 * d + 1) * R], pre_[:, (2 * d + 1) * R:(2 * d + 2) * R], u_,
                          ba[d:d + 1], bx[d:d + 1], lam[d:d + 1])
            outs += [a, b]
        return tuple(outs)

    a0, b0, a1, b1 = _rowcall(f_coeff, "l0_coeff", T_ALL, TM, [_rin(pre), _rin(u)],
                              [wts["rec_b_a"], wts["rec_b_x"], wts["rec_lambda"]], [(R, f32)] * 4)
    y0, yp0 = _scan_call(a0, b0, FWD_SEGMENTS, "l0_scan_fwd", False)
    y1, yp1 = _scan_call(a1, b1, REV_SEGMENTS, "l0_scan_rev", False)

    def f_gate(i, gp, y0_, y1_):
        return (_gelu(gp) * (y0_ + y1_),)

    (zb,) = _rowcall(f_gate, "l0_gate", T_LAT, TM,
                     [_rin(gr, R, 0, N_CTX_TILES), _rin(y0, None, 0, N_CTX_TILES), _rin(y1, None, 0, N_CTX_TILES)],
                     [], [(R, bf16)])
    out0 = _mm(zb, wts["rec_w_out"], "l0_out_proj")

    def f_resid_pre(i, xin, out, gate, bias, g, sc, sh):
        xn = xin + gate * (out + bias)
        return xn, _normmod(xn, g, sc, sh)

    zero_d = jnp.zeros((1, D), f32)
    x1, h1 = _rowcall(f_resid_pre, "l0_resid_a", T_LAT, TM, [_rin(x0cat, None, 0, N_CTX_TILES), _rin(out0)],
                      [g1[0], zero_d, ng[0, 1][None], sc2[0], sh2[0]], [(D, f32), (D, bf16)])

    def f_sqrelu(i, hid):
        return (jnp.square(jnp.maximum(hid, 0.0)),)

    def mlp_fwd(h, l, tag):
        hid = _mm(h, wts["mlp_w_in"][l], tag + "_mlp_in")
        (act,) = _rowcall(f_sqrelu, tag + "_sqrelu", T_LAT, TM, [_rin(hid)], [], [(F, bf16)])
        return hid, act, _mm(act, wts["mlp_w_out"][l], tag + "_mlp_out")

    hid0, act0, mo0 = mlp_fwd(h1, 0, "l0")
    x2, h2 = _rowcall(f_resid_pre, "l0_resid_b", T_LAT, TM, [_rin(x1), _rin(mo0)],
                      [g2[0], zero_d, ng[1, 0][None], sc1[1], sh1[1]], [(D, f32), (D, bf16)])

    pw = _mm(h2, wts["conf_w_pw1"], "l1_pw1")

    def f_glu(i, pa, pb, b1):
        return ((pa + b1[:, :D]) * _sigmoid(pb + b1[:, D:]),)

    (zg,) = _rowcall(f_glu, "l1_glu", T_LAT, TM, [_rin(pw, D, 0), _rin(pw, D, 1)], [wts["conf_b_pw1"]], [(D, f32)])
    (zc,) = _dwconv_fwd(zg, 0, wts["conf_conv_w"], wts["conf_conv_b"], 31, 15, ((0, T_LAT),), 128, "l1_conv", False)

    def ln_silu(z, lg, lb):
        mu = jnp.mean(z, axis=-1, keepdims=True)
        zc_ = z - mu
        var = jnp.mean(zc_ * zc_, axis=-1, keepdims=True)
        yv = zc_ * lax.rsqrt(var + EPS) * lg + lb
        return yv * _sigmoid(yv)

    def f_lnsilu(i, z, lg, lb):
        return (ln_silu(z, lg, lb),)

    (sb,) = _rowcall(f_lnsilu, "l1_ln_silu", T_LAT, TM, [_rin(zc)], [wts["conf_ln_g"], wts["conf_ln_b"]], [(D, bf16)])
    out1 = _mm(sb, wts["conf_w_pw2"], "l1_pw2")
    x3, h3 = _rowcall(f_resid_pre, "l1_resid_a", T_LAT, TM, [_rin(x2), _rin(out1)],
                      [g1[1], wts["conf_b_pw2"], ng[1, 1][None], sc2[1], sh2[1]], [(D, f32), (D, bf16)])
    hid1, act1, mo1 = mlp_fwd(h3, 1, "l1")

    def loss_fn(x3_, mo_, gate, fg, tgt):
        x4 = x3_ + gate * mo_
        err = _rms(x4, fg) - tgt
        per_row = jnp.mean(err * err, axis=-1, keepdims=True)
        return 0.5 * jnp.sum(per_row, axis=0, keepdims=True)

    def f_head(i, x3_, mo_, tgt, gate, fg):
        loss, vjp = jax.vjp(lambda a, b, c, e: loss_fn(a, b, c, e, tgt), x3_, mo_, gate, fg)
        dx, dmo, dgate, dfg = vjp(jnp.ones((1, 1), f32))
        return dx, dmo, jnp.broadcast_to(loss, (1, 128)), dgate, dfg

    dx3, dmo1, loss_acc, dg2_1, dfinal_g = _rowcall(
        f_head, "head", T_LAT, TM, [_rin(x3), _rin(mo1), _rin(target)], [g2[1], wts["final_g"]],
        [(D, f32), (D, bf16)], [(1, 128), (1, D), (1, D)])

    grads = {"final_g": dfinal_g}

    def f_sqrelu_bwd(i, hid, dact):
        return (dact * (2.0 * jnp.maximum(hid, 0.0)),)

    def normmod_bwd(xin, dh, dx_skip, g, sc, sh, tag):
        def fb(i, x_, dh_, dxs, g_, sc_, sh_):
            _, vjp = jax.vjp(_normmod, x_, g_, sc_, sh_)
            dx, dg, dsc, dsh = vjp(dh_)
            return dx + dxs, dg, dsc, dsh

        return _rowcall(fb, tag + "_normmod_bwd", T_LAT, TM, [_rin(xin), _rin(dh), _rin(dx_skip)], [g, sc, sh],
                        [(D, f32)], [(1, D)] * 3)

    def mlp_bwd(dmo, hid, act, h, l, tag):
        dact = _mm(dmo, wts["mlp_w_out"][l], tag + "_mlp_out_dx", tb=True)
        dw_out = _mm(act, dmo, tag + "_mlp_out_dw", ta=True, out_dtype=bf16)
        (dhid,) = _rowcall(f_sqrelu_bwd, tag + "_sqrelu_bwd", T_LAT, TM, [_rin(hid), _rin(dact)], [], [(F, bf16)])
        dh = _mm(dhid, wts["mlp_w_in"][l], tag + "_mlp_in_dx", tb=True)
        dw_in = _mm(h, dhid, tag + "_mlp_in_dw", ta=True, out_dtype=bf16)
        return dh, dw_in, dw_out

    def f_resid_bwd(i, dx, out, gate, bias):
        return dx * gate, jnp.sum(dx * (out + bias), axis=0, keepdims=True), gate * jnp.sum(dx, axis=0, keepdims=True)

    def resid_bwd(dx, out, gate, bias, tag):
        return _rowcall(f_resid_bwd, tag + "_resid_bwd", T_LAT, TM, [_rin(dx), _rin(out)], [gate, bias],
                        [(D, bf16)], [(1, D)] * 2)

    dh3, dw_in1, dw_out1 = mlp_bwd(dmo1, hid1, act1, h3, 1, "l1")
    dx3, dng11, dsc2_1, dsh2_1 = normmod_bwd(x3, dh3, dx3, ng[1, 1][None], sc2[1], sh2[1], "l1b")

    dout1, dg1_1, db_pw2 = resid_bwd(dx3, out1, g1[1], wts["conf_b_pw2"], "l1a")
    ds = _mm(dout1, wts["conf_w_pw2"], "l1_pw2_dx", tb=True)
    grads["conf_w_pw2"] = _mm(sb, dout1, "l1_pw2_dw", ta=True, out_dtype=bf16)
    grads["conf_b_pw2"] = db_pw2

    def f_lnsilu_bwd(i, z, ds_, lg, lb):
        _, vjp = jax.vjp(ln_silu, z, lg, lb)
        return vjp(ds_)

    dzc, dln_g, dln_b = _rowcall(f_lnsilu_bwd, "l1_ln_silu_bwd", T_LAT, TM, [_rin(zc), _rin(ds)],
                                 [wts["conf_ln_g"], wts["conf_ln_b"]], [(D, f32)], [(1, D)] * 2)
    grads["conf_ln_g"], grads["conf_ln_b"] = dln_g, dln_b
    dzg, dconv_w, dconv_b = _dwconv_bwd([dzc], zg, 0, wts["conf_conv_w"], 31, 15, ((0, T_LAT),), 128,
                                        "l1_conv_bwd", f32)
    grads["conf_conv_w"], grads["conf_conv_b"] = dconv_w, dconv_b

    def f_glu_bwd(i, pa, pb, dz, b1):
        _, vjp = jax.vjp(lambda a, b, c: (a + c[:, :D]) * _sigmoid(b + c[:, D:]), pa, pb, b1)
        da, db, dc = vjp(dz)
        return jnp.concatenate([da, db], axis=1), dc

    dpw, db_pw1 = _rowcall(f_glu_bwd, "l1_glu_bwd", T_LAT, TM, [_rin(pw, D, 0), _rin(pw, D, 1), _rin(dzg)],
                           [wts["conf_b_pw1"]], [(2 * D, bf16)], [(1, 2 * D)])
    grads["conf_b_pw1"] = db_pw1
    dh2 = _mm(dpw, wts["conf_w_pw1"], "l1_pw1_dx", tb=True)
    grads["conf_w_pw1"] = _mm(h2, dpw, "l1_pw1_dw", ta=True, out_dtype=bf16)
    dx2, dng10, dsc1_1, dsh1_1 = normmod_bwd(x2, dh2, dx3, ng[1, 0][None], sc1[1], sh1[1], "l1a")

    dmo0, dg2_0, _ = resid_bwd(dx2, mo0, g2[0], zero_d, "l0b")
    dh1, dw_in0, dw_out0 = mlp_bwd(dmo0, hid0, act0, h1, 0, "l0")
    grads["mlp_w_in"] = (dw_in0, dw_in1)
    grads["mlp_w_out"] = (dw_out0, dw_out1)
    dx1, dng01, dsc2_0, dsh2_0 = normmod_bwd(x1, dh1, dx2, ng[0, 1][None], sc2[0], sh2[0], "l0b")

    dout0, dg1_0, _ = resid_bwd(dx1, out0, g1[0], zero_d, "l0a")
    dz = _mm(dout0, wts["rec_w_out"], "l0_out_proj_dx", tb=True)
    grads["rec_w_out"] = _mm(zb, dout0, "l0_out_proj_dw", ta=True, out_dtype=bf16)

    def f_gate_bwd(i, gp, y0_, y1_, dz_):
        lat = jnp.where(i < N_CTX_TILES, 0.0, 1.0)
        _, vjp = jax.vjp(lambda a, b: _gelu(a) * b, gp, y0_ + y1_)
        dgp, dy = vjp(dz_)
        return dgp * lat, dy * lat

    dgp, dy = _rowcall(f_gate_bwd, "l0_gate_bwd", T_ALL, TM,
                       [_rin(gr, R, 0), _rin(y0), _rin(y1), _rin(dz, None, 0, -N_CTX_TILES)], [],
                       [(R, bf16), (R, f32)])
    (dh_f,) = _scan_call(a0, dy, FWD_SEGMENTS_BWD, "l0_scan_fwd_bwd", True)
    (dh_r,) = _scan_call(a1, dy, REV_SEGMENTS_BWD, "l0_scan_rev_bwd", True)

    def f_coeff_bwd(i, pre_, u_, dhf, dhr, ypf, ypr, ba, bx, lam):
        dpre, dba, dbx, dlam = [], [], [], []
        du = jnp.zeros_like(u_)
        for d, (dh_, yp_) in enumerate(((dhf, ypf), (dhr, ypr))):
            _, vjp = jax.vjp(_coeff, pre_[:, 2 * d * R:(2 * d + 1) * R], pre_[:, (2 * d + 1) * R:(2 * d + 2) * R], u_,
                             ba[d:d + 1], bx[d:d + 1], lam[d:d + 1])
            dpa, dpx, du_d, dba_d, dbx_d, dlam_d = vjp((dh_ * yp_, dh_))
            dpre += [dpa, dpx]
            du = du + du_d
            dba.append(dba_d)
            dbx.append(dbx_d)
            dlam.append(dlam_d)
        return jnp.concatenate(dpre, axis=1), du, _rows2(*dba), _rows2(*dbx), _rows2(*dlam)

    dpre, du_direct, db_a, db_x, dlam = _rowcall(
        f_coeff_bwd, "l0_coeff_bwd", T_ALL, 64,
        [_rin(pre), _rin(u), _rin(dh_f), _rin(dh_r), _rin(yp0), _rin(yp1)],
        [wts["rec_b_a"], wts["rec_b_x"], wts["rec_lambda"]], [(4 * R, bf16), (R, f32)], [(2, R)] * 3)
    grads["rec_b_a"], grads["rec_b_x"], grads["rec_lambda"] = db_a, db_x, dlam
    du_gates = _mm(dpre, wts["gates"], "l0_gates_dx", tb=True)
    grads["gates"] = _mm(ub, dpre, "l0_gates_dw", ta=True)
    drec, dconv4_w, dconv4_b = _dwconv_bwd([du_direct, du_gates], gr, R // 256, wts["rec_conv_w"], 4, 1,
                                           CONV_SEGMENTS, 256, "l0_conv_bwd", bf16)
    grads["rec_conv_w"], grads["rec_conv_b"] = dconv4_w, dconv4_b
    dgr = jnp.concatenate([dgp, drec], axis=1)
    dh0 = _mm(dgr, wts["rec_w_in"], "l0_in_proj_dx", tb=True)
    grads["rec_w_in"] = _mm(h0, dgr, "l0_in_proj_dw", ta=True, out_dtype=bf16)

    def f_pre0_bwd(i, x0, dh_, dxs, g, scp_, shp_):
        lat = jnp.where(i < N_CTX_TILES, 0.0, 1.0)
        _, vjp = jax.vjp(lambda a, b, c, e: _normmod(a, b, blend(i, c), blend(i, e)), x0, g, scp_, shp_)
        dx, dg, dscp, dshp = vjp(dh_)
        return dx + lat * dxs, dg, dscp, dshp

    dx0cat, dng00, dscp, dshp = _rowcall(
        f_pre0_bwd, "l0_prenorm_bwd", T_ALL, TM, [_rin(x0cat), _rin(dh0), _rin(dx1, None, 0, -N_CTX_TILES)],
        [ng[0, 0][None], scp, shp], [(D, f32)], [(1, D), (2, D), (2, D)])

    grads["norm_g"] = jnp.stack([jnp.concatenate([dng00, dng01], 0), jnp.concatenate([dng10, dng11], 0)])
    dmods = jnp.stack([
        jnp.concatenate([dshp[1:2], dscp[1:2], dg1_0, dsh2_0, dsc2_0, dg2_0], axis=0),
        jnp.concatenate([dsh1_1, dsc1_1, dg1_1, dsh2_1, dsc2_1, dg2_1], axis=0)])
    dcmod = jnp.concatenate([dshp[0:1], dscp[0:1]], axis=0)
    return loss_acc[0, 0], dx0cat[T_CTX:], dmods, dcmod, grads


def _unshard_cols(g):
    g = jnp.moveaxis(g, 0, -2)
    return g.reshape(g.shape[:-2] + (g.shape[-2] * g.shape[-1],))


def _shard_cols(w):
    w = w.reshape(w.shape[:-1] + (N_DEV, w.shape[-1] // N_DEV))
    return jnp.moveaxis(w, -2, 0)


def _shard_rows(w):
    return w.reshape((N_DEV, w.shape[0] // N_DEV) + w.shape[1:])


SMALL_PACK_ROWS = 64


def kernel(x, c, ctx, c_ctx, w_ada, b_ada, norm_g, rec_w_in, rec_conv_w, rec_conv_b, rec_lambda, rec_w_a, rec_b_a, rec_w_x, rec_b_x, rec_w_out, conf_w_pw1, conf_b_pw1, conf_conv_w, conf_conv_b, conf_ln_g, conf_ln_b, conf_w_pw2, conf_b_pw2, mlp_w_in, mlp_w_out, final_g, loss_target, m_c_ctx, m_w_ada, m_b_ada, m_norm_g, m_rec_w_in, m_rec_conv_w, m_rec_conv_b, m_rec_lambda, m_rec_w_a, m_rec_b_a, m_rec_w_x, m_rec_b_x, m_rec_w_out, m_conf_w_pw1, m_conf_b_pw1, m_conf_conv_w, m_conf_conv_b, m_conf_ln_g, m_conf_ln_b, m_conf_w_pw2, m_conf_b_pw2, m_mlp_w_in, m_mlp_w_out, m_final_g, v_c_ctx, v_w_ada, v_b_ada, v_norm_g, v_rec_w_in, v_rec_conv_w, v_rec_conv_b, v_rec_lambda, v_rec_w_a, v_rec_b_a, v_rec_w_x, v_rec_b_x, v_rec_w_out, v_conf_w_pw1, v_conf_b_pw1, v_conf_conv_w, v_conf_conv_b, v_conf_ln_g, v_conf_ln_b, v_conf_w_pw2, v_conf_b_pw2, v_mlp_w_in, v_mlp_w_out, v_final_g):
    me = 4 * lax.axis_index("x") + 2 * lax.axis_index("y") + lax.axis_index("c")
    weights = dict(c_ctx=c_ctx, w_ada=w_ada, b_ada=b_ada, norm_g=norm_g, rec_w_in=rec_w_in, rec_conv_w=rec_conv_w,
                   rec_conv_b=rec_conv_b, rec_lambda=rec_lambda, rec_w_a=rec_w_a, rec_b_a=rec_b_a, rec_w_x=rec_w_x,
                   rec_b_x=rec_b_x, rec_w_out=rec_w_out, conf_w_pw1=conf_w_pw1, conf_b_pw1=conf_b_pw1,
                   conf_conv_w=conf_conv_w, conf_conv_b=conf_conv_b, conf_ln_g=conf_ln_g, conf_ln_b=conf_ln_b,
                   conf_w_pw2=conf_w_pw2, conf_b_pw2=conf_b_pw2, mlp_w_in=mlp_w_in, mlp_w_out=mlp_w_out, final_g=final_g)
    m_in = dict(c_ctx=m_c_ctx, w_ada=m_w_ada, b_ada=m_b_ada, norm_g=m_norm_g, rec_w_in=m_rec_w_in, rec_conv_w=m_rec_conv_w,
                rec_conv_b=m_rec_conv_b, rec_lambda=m_rec_lambda, rec_w_a=m_rec_w_a, rec_b_a=m_rec_b_a, rec_w_x=m_rec_w_x,
                rec_b_x=m_rec_b_x, rec_w_out=m_rec_w_out, conf_w_pw1=m_conf_w_pw1, conf_b_pw1=m_conf_b_pw1,
                conf_conv_w=m_conf_conv_w, conf_conv_b=m_conf_conv_b, conf_ln_g=m_conf_ln_g, conf_ln_b=m_conf_ln_b,
                conf_w_pw2=m_conf_w_pw2, conf_b_pw2=m_conf_b_pw2, mlp_w_in=m_mlp_w_in, mlp_w_out=m_mlp_w_out,
                final_g=m_final_g)
    v_in = dict(c_ctx=v_c_ctx, w_ada=v_w_ada, b_ada=v_b_ada, norm_g=v_norm_g, rec_w_in=v_rec_w_in, rec_conv_w=v_rec_conv_w,
                rec_conv_b=v_rec_conv_b, rec_lambda=v_rec_lambda, rec_w_a=v_rec_w_a, rec_b_a=v_rec_b_a, rec_w_x=v_rec_w_x,
                rec_b_x=v_rec_b_x, rec_w_out=v_rec_w_out, conf_w_pw1=v_conf_w_pw1, conf_b_pw1=v_conf_b_pw1,
                conf_conv_w=v_conf_conv_w, conf_conv_b=v_conf_conv_b, conf_ln_g=v_conf_ln_g, conf_ln_b=v_conf_ln_b,
                conf_w_pw2=v_conf_w_pw2, conf_b_pw2=v_conf_b_pw2, mlp_w_in=v_mlp_w_in, mlp_w_out=v_mlp_w_out,
                final_g=v_final_g)
    names = list(weights)

    small_items = [c, norm_g, rec_conv_w, rec_lambda, conf_b_pw1, conf_conv_w, conf_conv_b, conf_ln_g, conf_ln_b,
                   conf_b_pw2]
    flat = jnp.concatenate([a.reshape(-1) for a in small_items])
    flat = jnp.pad(flat, (0, SMALL_PACK_ROWS * 128 - flat.shape[0])).reshape(SMALL_PACK_ROWS, 128)
    big_items = [rec_w_in[0], rec_w_out[0], conf_w_pw1[0], conf_w_pw2[0], mlp_w_in, mlp_w_out]
    (small_all,) = _all_gather([flat], "gather_small")
    big_flat = jnp.concatenate([a.astype(bf16).reshape(-1) for a in big_items]).reshape(-1, D)
    big_gathered = _all_gather_2level(big_flat, "gather_weights").reshape(N_DEV, -1)
    big_all, off = [], 0
    for a in big_items:
        big_all.append(big_gathered[:, off:off + a.size].reshape((N_DEV,) + a.shape))
        off += a.size

    small_all = small_all.reshape(N_DEV, -1)
    off = 0
    small = []
    for a in small_items:
        small.append(small_all[:, off:off + a.size].reshape((N_DEV,) + a.shape))
        off += a.size
    c_all, ng_all, rcw_all, lam_all, bpw1_all, ccw_all, ccb_all, lng_all, lnb_all, bpw2_all = small
    wts = {
        "norm_g": _unshard_cols(ng_all),
        "rec_conv_w": _unshard_cols(rcw_all)[0],
        "rec_lambda": _unshard_cols(lam_all)[0],
        "conf_b_pw1": _unshard_cols(bpw1_all),
        "conf_conv_w": _unshard_cols(ccw_all)[0],
        "conf_conv_b": _unshard_cols(ccb_all),
        "conf_ln_g": _unshard_cols(lng_all),
        "conf_ln_b": _unshard_cols(lnb_all),
        "conf_b_pw2": _unshard_cols(bpw2_all),
        "rec_conv_b": rec_conv_b,
        "rec_b_a": rec_b_a[0].reshape(2, R),
        "rec_b_x": rec_b_x[0].reshape(2, R),
        "final_g": final_g[None],
        "gates": _dense_gates(rec_w_a[0], rec_w_x[0]),
        "rec_w_in": _unshard_cols(big_all[0]),
        "rec_w_out": big_all[1].reshape(R, D),
        "conf_w_pw1": _unshard_cols(big_all[2]),
        "conf_w_pw2": big_all[3].reshape(D, D),
        "mlp_w_in": _unshard_cols(big_all[4]),
        "mlp_w_out": jnp.moveaxis(big_all[5], 0, 1).reshape(2, F, D),
    }

    c16 = jnp.concatenate([c_all[:, 0], jnp.broadcast_to(c_ctx[None], (8, D))], axis=0)
    b_loc = lax.dynamic_slice_in_dim(b_ada, me * ADA_SHARD, ADA_SHARD, axis=1)[:, None]
    (mods_all,) = _all_gather([_ada_forward(c16, w_ada, b_loc)], "gather_mods")
    mods_all = _unshard_cols(mods_all)
    mods = lax.dynamic_index_in_dim(mods_all, me, axis=1, keepdims=False).reshape(2, N_MOD, D)
    cmod = mods_all[0, 8, :2 * D].reshape(2, D)

    loss_part, grad_x, dmods, dcmod, grads = _local_step(x[0], ctx[0], loss_target[0], mods, cmod, wts)
    loss = lax.psum(loss_part, ("x", "y", "c"))

    dm_flat = jnp.concatenate([dmods.reshape(-1), dcmod.reshape(-1)]).reshape(-1, 128)
    (dm_all,) = _all_gather([dm_flat], "gather_dmods")
    dm_all = dm_all.reshape(N_DEV, -1)
    dmods_all = dm_all[:, :2 * N_MOD * D].reshape(N_DEV, 2, N_MOD * D)
    dcmod_all = jnp.pad(dm_all[:, 2 * N_MOD * D:], ((0, 0), (0, (N_MOD - 2) * D)))
    g16_full = jnp.stack([jnp.concatenate([dmods_all[:, 0], dcmod_all], axis=0),
                          jnp.concatenate([dmods_all[:, 1], jnp.zeros_like(dcmod_all)], axis=0)])
    g16 = lax.dynamic_slice_in_dim(g16_full, me * ADA_SHARD, ADA_SHARD, axis=2)
    dw_ada, ds_part = _ada_backward(c16, g16, w_ada)
    (ds_all,) = _all_gather([ds_part[0]], "gather_dsilu")

    big_grads = [_shard_cols(grads["rec_w_in"]), _shard_rows(grads["rec_w_out"]), _shard_cols(grads["conf_w_pw1"]),
                 _shard_rows(grads["conf_w_pw2"]),
                 jnp.stack([_shard_cols(g) for g in grads["mlp_w_in"]], axis=1),
                 jnp.stack([_shard_rows(g) for g in grads["mlp_w_out"]], axis=1)]
    small_sharded = ["norm_g", "rec_conv_w", "rec_lambda", "conf_b_pw1", "conf_conv_w", "conf_conv_b", "conf_ln_g",
                     "conf_ln_b", "conf_b_pw2"]
    pack = jnp.concatenate([_shard_cols(grads[n]).reshape(N_DEV, -1) for n in small_sharded], axis=1)
    pack_len = pack.shape[1]
    pack = jnp.pad(pack, ((0, 0), (0, SMALL_PACK_ROWS * 128 - pack_len))).reshape(N_DEV, SMALL_PACK_ROWS, 128)
    big_names = ["rec_w_in", "rec_w_out", "conf_w_pw1", "conf_w_pw2", "mlp_w_in", "mlp_w_out"]
    big_pack = jnp.concatenate([g.reshape(N_DEV, -1) for g in big_grads], axis=1).reshape(N_DEV, -1, D)
    big_f32, big_last = _reduce_scatter(big_pack, "scatter_weight_grads")
    big_f32, big_last = big_f32.reshape(-1), big_last.reshape(-1)
    (pack_recv,) = _all_to_all([pack], "scatter_small_grads")
    pack_recv = pack_recv.reshape(N_DEV, -1)

    dwg = grads["gates"]
    repl = {"rec_conv_b": grads["rec_conv_b"],
            "rec_w_a": jnp.stack([_gate_blocks(dwg, 0), _gate_blocks(dwg, 2)])[None],
            "rec_w_x": jnp.stack([_gate_blocks(dwg, 1), _gate_blocks(dwg, 3)])[None],
            "rec_b_a": grads["rec_b_a"].reshape(1, 2, N_BLK, BLK),
            "rec_b_x": grads["rec_b_x"].reshape(1, 2, N_BLK, BLK),
            "final_g": grads["final_g"][0]}
    repl_names = list(repl)
    repl_flat = jnp.concatenate([repl[n].reshape(-1) for n in repl_names])
    repl_len = repl_flat.shape[0]
    repl_rows = -(-repl_len // (16 * D)) * 16
    repl_flat = jnp.pad(repl_flat, (0, repl_rows * D - repl_len)).reshape(repl_rows, D).astype(bf16)
    repl_all = _all_gather_2level(repl_flat, "gather_replicated_grads").reshape(N_DEV, -1)

    pieces = {}
    shard_shapes = {n: weights[n].shape for n in names}
    off = 0
    for n in big_names:
        size = weights[n].size
        pieces[n] = [big_f32[off:off + size], big_last[off:off + size]]
        off += size
    off = 0
    for n in small_sharded:
        size = weights[n].size
        pieces[n] = [pack_recv[:, off:off + size]]
        off += size
    off = 0
    for n in repl_names:
        size = weights[n].size
        pieces[n] = [repl_all[:, off:off + size]]
        off += size
    pieces["w_ada"] = [dw_ada]
    db_terms = jnp.concatenate([dmods_all, jnp.stack([dcmod_all, jnp.zeros_like(dcmod_all)], axis=1)], axis=0)
    pieces["b_ada"] = [db_terms]
    pieces["c_ctx"] = [ds_all[:, 0]]

    def as2d(shape):
        if len(shape) == 1:
            return (1, shape[0])
        if len(shape) == 5:
            return (shape[0] * shape[1] * shape[2], shape[3] * shape[4])
        rows = 1
        for s in shape[:-1]:
            rows *= s
        return (rows, shape[-1])

    g_out, d_out, m_out, v_out = {}, {}, {}, {}
    for n in names:
        shape = shard_shapes[n]
        r2, c2 = as2d(shape)
        p = [piece.reshape(-1, r2, c2) for piece in pieces[n]]
        g, dl, nm, nv = _adamw(p, weights[n].reshape(r2, c2), m_in[n].reshape(r2, c2), v_in[n].reshape(r2, c2),
                               "adamw_" + n)
        g_out[n], d_out[n], m_out[n], v_out[n] = (t.reshape(shape) for t in (g, dl, nm, nv))

    return (loss, grad_x[None], *[g_out[n] for n in names], *[d_out[n] for n in names],
            *[m_out[n] for n in names], *[v_out[n] for n in names])
```

```python
import functools

import jax
import jax.numpy as jnp
from jax import lax
from jax.experimental import pallas as pl
from jax.experimental.pallas import tpu as pltpu

f32 = jnp.float32
bf16 = jnp.bfloat16

N_DEV = 8
D = 1024
T_LAT = 2048
T_CTX = 256
T_ALL = T_CTX + T_LAT
R = 1280
N_BLK = 16
BLK = R // N_BLK
F = 4096
GRID_W = 64
RG_C = 8.0
EPS = 1e-6
POS_BASE = 10000.0
N_MOD = 6
ADA_SHARD = N_MOD * D // N_DEV

ADAM_LR = 0.001
ADAM_B1 = 0.9
ADAM_B2 = 0.999
ADAM_EPS = 1e-08
ADAM_WD = 0.01
ADAM_STEP = 10

VMEM_LIMIT_V7X = 56 * 1024 * 1024
HALO = 16
MESH = pl.DeviceIdType.MESH


def _cparams(*sem):
    return pltpu.CompilerParams(dimension_semantics=sem, vmem_limit_bytes=VMEM_LIMIT_V7X)


def _pick(n, cands):
    for c in cands:
        if n % c == 0:
            return c
    raise ValueError(f"no block size for {n}")


def _position():
    x, y, c = lax.axis_index("x"), lax.axis_index("y"), lax.axis_index("c")
    return x, y, c, 4 * x + 2 * y + c


def _peer(x, y, c, k):
    px = (1 - x) if (k >> 2) & 1 else x
    py = (1 - y) if (k >> 1) & 1 else y
    pc = (1 - c) if k & 1 else c
    return (px, py, pc), 4 * px + 2 * py + pc


def _exchange(arrs, name, scatter):
    n = len(arrs)

    def body(*refs):
        ins, outs = refs[:n], refs[n:2 * n]
        send_sems, recv_sems, local_sems = refs[2 * n:]
        x, y, c, me = _position()
        local = []
        for a in range(n):
            src = ins[a].at[me] if scatter else ins[a]
            cp = pltpu.make_async_copy(src, outs[a].at[me], local_sems.at[a])
            cp.start()
            local.append(cp)
        sends, recvs = [], []
        for a in range(n):
            for k in range(1, N_DEV):
                peer, peer_lin = _peer(x, y, c, k)
                src = ins[a].at[peer_lin] if scatter else ins[a]
                cp = pltpu.make_async_remote_copy(
                    src_ref=src, dst_ref=outs[a].at[me], send_sem=send_sems.at[a, k - 1],
                    recv_sem=recv_sems.at[a, k - 1], device_id=peer, device_id_type=MESH)
                cp.start()
                sends.append(cp)
                recvs.append(pltpu.make_async_remote_copy(
                    src_ref=src, dst_ref=outs[a].at[peer_lin], send_sem=send_sems.at[a, k - 1],
                    recv_sem=recv_sems.at[a, k - 1], device_id=peer, device_id_type=MESH))
        for cp in recvs:
            cp.wait_recv()
        for cp in sends:
            cp.wait_send()
        for cp in local:
            cp.wait()

    if scatter:
        out_shape = [jax.ShapeDtypeStruct(a.shape, a.dtype) for a in arrs]
    else:
        out_shape = [jax.ShapeDtypeStruct((N_DEV,) + a.shape, a.dtype) for a in arrs]
    any_spec = pl.BlockSpec(memory_space=pl.ANY)
    return pl.pallas_call(
        body, name=name, out_shape=out_shape,
        in_specs=[any_spec] * n, out_specs=[any_spec] * n,
        scratch_shapes=[pltpu.SemaphoreType.DMA((n, N_DEV - 1)), pltpu.SemaphoreType.DMA((n, N_DEV - 1)),
                        pltpu.SemaphoreType.DMA((n,))],
    )(*arrs)


def _all_gather(arrs, name):
    return _exchange(arrs, name, scatter=False)


def _all_to_all(arrs, name):
    return _exchange(arrs, name, scatter=True)


def _lin(p):
    return 4 * p[0] + 2 * p[1] + p[2]


def _comm_call(body, name, ins, out_shape, n_sems, extra_scratch=()):
    any_spec = pl.BlockSpec(memory_space=pl.ANY)
    return pl.pallas_call(
        body, name=name, out_shape=out_shape, in_specs=[any_spec] * len(ins), out_specs=[any_spec] * len(out_shape),
        scratch_shapes=[pltpu.SemaphoreType.DMA((n_sems,)), pltpu.SemaphoreType.DMA((n_sems,))] + list(extra_scratch),
    )(*ins)


def _staged_copy(src, dst, buf, in_sems, out_sems, rows, chunk):
    n = rows // chunk

    def rd(i):
        return pltpu.make_async_copy(src.at[pl.ds(i * chunk, chunk)], buf.at[i % 2], in_sems.at[i % 2])

    def wr(i):
        return pltpu.make_async_copy(buf.at[i % 2], dst.at[pl.ds(i * chunk, chunk)], out_sems.at[i % 2])

    rd(0).start()
    for i in range(n):
        if i + 1 < n:
            if i >= 1:
                wr(i - 1).wait()
            rd(i + 1).start()
        rd(i).wait()
        wr(i).start()
    for i in range(max(n - 2, 0), n):
        wr(i).wait()


def _all_gather_2level(shard, name):
    rows, cols = shard.shape
    chunk = _pick(rows, (416, 512, 256, 128, 64, 16))

    def body(in_ref, out_ref, send_sems, recv_sems, buf, in_sems, out_sems):
        x, y, c, me = _position()
        sib, xn, yn, dg = (x, y, 1 - c), (1 - x, y, c), (x, 1 - y, c), (1 - x, 1 - y, c)

        def cp(k, src, slot, to):
            return pltpu.make_async_remote_copy(src_ref=src, dst_ref=out_ref.at[slot], send_sem=send_sems.at[k],
                                                recv_sem=recv_sems.at[k], device_id=to, device_id_type=MESH)

        for k, to in ((0, sib), (1, xn), (2, yn)):
            cp(k, in_ref, me, to).start()
        cp(1, in_ref, _lin(xn), xn).wait_recv()
        cp(3, out_ref.at[_lin(xn)], _lin(xn), sib).start()

        @pl.when(c == 0)
        def _():
            cp(5, out_ref.at[_lin(xn)], _lin(xn), yn).start()

        cp(2, in_ref, _lin(yn), yn).wait_recv()
        cp(4, out_ref.at[_lin(yn)], _lin(yn), sib).start()

        @pl.when(c == 1)
        def _():
            cp(5, out_ref.at[_lin(yn)], _lin(yn), xn).start()

        cp(5, in_ref, _lin(dg), xn).wait_recv()
        cp(6, out_ref.at[_lin(dg)], _lin(dg), sib).start()
        _staged_copy(in_ref, out_ref.at[me], buf, in_sems, out_sems, rows, chunk)
        for k, origin in ((0, sib), (3, (1 - x, y, 1 - c)), (4, (x, 1 - y, 1 - c)), (6, (1 - x, 1 - y, 1 - c))):
            cp(k, in_ref, _lin(origin), sib).wait_recv()
        for k in range(7):
            cp(k, in_ref, me, sib).wait_send()

    (out,) = _comm_call(body, name, [shard], [jax.ShapeDtypeStruct((N_DEV,) + shard.shape, shard.dtype)], 7,
                        [pltpu.VMEM((2, chunk, cols), shard.dtype), pltpu.SemaphoreType.DMA((2,)),
                         pltpu.SemaphoreType.DMA((2,))])
    return out


def _plane_pos(x, y, q):
    return ((1 - x) if q & 2 else x, (1 - y) if q & 1 else y)


def _scatter_d2d(g, name):
    def body(g_ref, recv_ref, send_sems, recv_sems):
        x, y, c, me = _position()
        sib = (x, y, 1 - c)
        sends = []
        for q in range(4):
            px, py = _plane_pos(x, y, q)
            cp = pltpu.make_async_remote_copy(src_ref=g_ref.at[_lin((px, py, 1 - c))], dst_ref=recv_ref.at[q],
                                              send_sem=send_sems.at[q], recv_sem=recv_sems.at[q], device_id=sib,
                                              device_id_type=MESH)
            cp.start()
            sends.append(cp)
        for cp in sends:
            cp.wait_recv()
        for cp in sends:
            cp.wait_send()

    (recv,) = _comm_call(body, name, [g], [jax.ShapeDtypeStruct((4,) + g.shape[1:], g.dtype)], 4)
    return recv


def _scatter_ici_first(h, name):
    def body(h_ref, recv_ref, send_sems, recv_sems):
        x, y, c, me = _position()
        xn, yn = (1 - x, y, c), (x, 1 - y, c)

        def cp(k, q, to):
            return pltpu.make_async_remote_copy(src_ref=h_ref.at[q], dst_ref=recv_ref.at[k], send_sem=send_sems.at[k],
                                                recv_sem=recv_sems.at[k], device_id=to, device_id_type=MESH)

        @pl.when(c == 0)
        def _():
            cp(0, 2, xn).start()
            cp(1, 3, xn).start()

        @pl.when(c == 1)
        def _():
            cp(0, 1, yn).start()
            cp(1, 3, yn).start()

        for k in range(2):
            cp(k, 0, xn).wait_recv()
        for k in range(2):
            cp(k, 0, xn).wait_send()

    (recv,) = _comm_call(body, name, [h], [jax.ShapeDtypeStruct((2,) + h.shape[1:], h.dtype)], 2)
    return recv


def _scatter_ici_second(k1, name):
    def body(k_ref, recv_ref, send_sems, recv_sems):
        x, y, c, me = _position()
        xn, yn = (1 - x, y, c), (x, 1 - y, c)

        def cp(to):
            return pltpu.make_async_remote_copy(src_ref=k_ref, dst_ref=recv_ref, send_sem=send_sems.at[0],
                                                recv_sem=recv_sems.at[0], device_id=to, device_id_type=MESH)

        @pl.when(c == 0)
        def _():
            cp(yn).start()

        @pl.when(c == 1)
        def _():
            cp(xn).start()

        cp(xn).wait_recv()
        cp(xn).wait_send()

    (out,) = _comm_call(body, name, [k1], [jax.ShapeDtypeStruct(k1.shape, k1.dtype)], 1)
    return out


def _add_blocks(a, a_idx, b, b_idx, out_dtype, name):
    rows, cols = a.shape[1:]
    n = a_idx.shape[0]
    tm = _pick(rows, (416, 512, 256, 128, 8))

    def body(ia_ref, ib_ref, a_ref, b_ref, o_ref):
        o_ref[...] = (a_ref[...].astype(f32) + b_ref[...].astype(f32)).astype(o_ref.dtype)

    grid_spec = pltpu.PrefetchScalarGridSpec(
        num_scalar_prefetch=2, grid=(n, rows // tm),
        in_specs=[pl.BlockSpec((None, tm, cols), lambda j, i, ia, ib: (ia[j], i, 0)),
                  pl.BlockSpec((None, tm, cols), lambda j, i, ia, ib: (ib[j], i, 0))],
        out_specs=pl.BlockSpec((None, tm, cols), lambda j, i, ia, ib: (j, i, 0)))
    return pl.pallas_call(body, name=name, out_shape=jax.ShapeDtypeStruct((n, rows, cols), out_dtype),
                          grid_spec=grid_spec, compiler_params=_cparams("parallel", "parallel"))(a_idx, b_idx, a, b)


def _reduce_scatter(g, tag):
    x, y, c, me = _position()
    i32 = lambda *v: jnp.stack([jnp.asarray(t, jnp.int32) for t in v])
    recv = _scatter_d2d(g, tag + "_d2d")
    own_idx = i32(*[_lin(_plane_pos(x, y, q) + (c,)) for q in range(4)])
    h = _add_blocks(g, own_idx, recv, i32(0, 1, 2, 3), bf16, tag + "_add_chip")
    recv2 = _scatter_ici_first(h, tag + "_ici_first")
    k0 = _add_blocks(h, i32(0), recv2, i32(0), f32, tag + "_add_mine")[0]
    k1 = _add_blocks(h, i32(1 + c), recv2, i32(1), bf16, tag + "_add_onward")[0]
    return k0, _scatter_ici_second(k1, tag + "_ici_second")


def _mm(a, b, name, ta=False, tb=False, out_dtype=f32):
    if ta:
        k_dim, m_dim = a.shape
    else:
        m_dim, k_dim = a.shape
    if tb:
        n_dim, k2 = b.shape
    else:
        k2, n_dim = b.shape
    assert k_dim == k2, (a.shape, b.shape)
    assert a.dtype == bf16 and b.dtype == bf16
    bm = _pick(m_dim, (512, 768, 640, 256, 128))
    bn = _pick(n_dim, (512, 640, 256, 128))
    bk = _pick(k_dim, (1024, 1280, 768, 512))
    nk = k_dim // bk
    a_spec = (pl.BlockSpec((bk, bm), lambda i, j, k: (k, i)) if ta
              else pl.BlockSpec((bm, bk), lambda i, j, k: (i, k)))
    b_spec = (pl.BlockSpec((bn, bk), lambda i, j, k: (j, k)) if tb
              else pl.BlockSpec((bk, bn), lambda i, j, k: (k, j)))
    dims = (((0 if ta else 1,), (1 if tb else 0,)), ((), ()))

    def body_single(a_ref, b_ref, o_ref):
        o_ref[...] = lax.dot_general(a_ref[...], b_ref[...], dims, preferred_element_type=f32).astype(o_ref.dtype)

    def body(a_ref, b_ref, o_ref, acc_ref):
        k = pl.program_id(2)

        @pl.when(k == 0)
        def _():
            acc_ref[...] = jnp.zeros_like(acc_ref)

        acc_ref[...] += lax.dot_general(a_ref[...], b_ref[...], dims, preferred_element_type=f32)

        @pl.when(k == nk - 1)
        def _():
            o_ref[...] = acc_ref[...].astype(o_ref.dtype)

    return pl.pallas_call(
        body_single if nk == 1 else body, name=name, out_shape=jax.ShapeDtypeStruct((m_dim, n_dim), out_dtype),
        grid=(m_dim // bm, n_dim // bn, nk), in_specs=[a_spec, b_spec],
        out_specs=pl.BlockSpec((bm, bn), lambda i, j, k: (i, j)),
        scratch_shapes=[] if nk == 1 else [pltpu.VMEM((bm, bn), f32)],
        compiler_params=_cparams("parallel", "parallel", "arbitrary"),
    )(a, b)


def _rin(arr, width=None, cb=0, roff=0):
    return (arr, arr.shape[1] if width is None else width, cb, roff)


def _rowcall(fn, name, rows, tm, row_ins, par_ins, row_outs, acc_outs=()):
    nr, npar, nro = len(row_ins), len(par_ins), len(row_outs)
    in_specs, args = [], []
    for arr, width, cb, roff in row_ins:
        if roff >= 0:
            imap = lambda i, cb=cb, roff=roff: (i + roff, cb)
        else:
            imap = lambda i, cb=cb, roff=roff: (jnp.maximum(i + roff, 0), cb)
        in_specs.append(pl.BlockSpec((tm, width), imap))
        args.append(arr)
    for p in par_ins:
        in_specs.append(pl.BlockSpec(p.shape, lambda i: (0, 0)))
        args.append(p)
    out_shape, out_specs = [], []
    for width, dt in row_outs:
        out_shape.append(jax.ShapeDtypeStruct((rows, width), dt))
        out_specs.append(pl.BlockSpec((tm, width), lambda i: (i, 0)))
    for p, width in acc_outs:
        out_shape.append(jax.ShapeDtypeStruct((p, width), f32))
        out_specs.append(pl.BlockSpec((p, width), lambda i: (0, 0)))

    def body(*refs):
        i = pl.program_id(0)
        res = fn(i, *[r[...] for r in refs[:nr + npar]])
        outs = refs[nr + npar:]
        for o, v in zip(outs[:nro], res[:nro]):
            o[...] = v.astype(o.dtype)
        if acc_outs:
            @pl.when(i == 0)
            def _():
                for o in outs[nro:]:
                    o[...] = jnp.zeros_like(o)

            for o, v in zip(outs[nro:], res[nro:]):
                o[...] += v

    return pl.pallas_call(
        body, name=name, out_shape=out_shape, grid=(rows // tm,), in_specs=in_specs, out_specs=out_specs,
        compiler_params=_cparams("arbitrary"),
    )(*args)


def _rms(x, g):
    return x * lax.rsqrt(jnp.mean(x * x, axis=-1, keepdims=True) + EPS) * g


def _normmod(x, g, sc, sh):
    return _rms(x, g) * (1.0 + sc) + sh


def _rows2(v0, v1):
    rid = lax.broadcasted_iota(jnp.int32, (2, v0.shape[1]), 0)
    return jnp.where(rid == 0, v0, v1)


def _gelu(x):
    return 0.5 * x * (1.0 + jnp.tanh(0.7978845608028654 * (x + 0.044715 * (x * x * x))))


def _sigmoid(x):
    return 1.0 / (1.0 + jnp.exp(-x))


def _coeff(pre_a, pre_x, u, ba, bx, lam):
    r = _sigmoid(pre_a + ba)
    ig = _sigmoid(pre_x + bx)
    nl = -lam
    sp = jnp.maximum(nl, 0.0) + jnp.log(1.0 + jnp.exp(-jnp.abs(nl)))
    la = -RG_C * r * sp
    a = jnp.exp(la)
    one_minus_a2 = -jnp.tanh(la) * (a * a + 1.0)
    return a, jnp.sqrt(one_minus_a2) * (ig * u)


SCAN_CHUNK = 256


def _scan_call(a, v, chunk_of, reverse, name, backward):
    rows, width = a.shape
    n_out = 1 if backward else 2
    nt = SCAN_CHUNK // 8

    def body(a_ref, v_ref, *rest):
        outs, state_ref = rest[:-1], rest[-1]

        @pl.when(pl.program_id(0) == 0)
        def _():
            state_ref[...] = jnp.zeros_like(state_ref)

        rid = lax.broadcasted_iota(jnp.int32, (8, width), 0)

        def tile(j, st):
            t0 = pl.multiple_of((nt - 1 - j if reverse else j) * 8, 8)
            at = a_ref[pl.ds(t0, 8), :]
            vt = v_ref[pl.ds(t0, 8), :]
            out = jnp.zeros((8, width), f32)
            prev = jnp.zeros((8, width), f32)
            for i in (range(7, -1, -1) if reverse else range(8)):
                if backward:
                    g = vt[i:i + 1] + st
                    st = at[i:i + 1] * g
                    out = jnp.where(rid == i, g, out)
                else:
                    prev = jnp.where(rid == i, st, prev)
                    st = at[i:i + 1] * st + vt[i:i + 1]
                    out = jnp.where(rid == i, st, out)
            outs[0][pl.ds(t0, 8), :] = out
            if not backward:
                outs[1][pl.ds(t0, 8), :] = prev
            return st

        state_ref[0:1, :] = lax.fori_loop(0, nt, tile, state_ref[0:1, :])

    spec = pl.BlockSpec((SCAN_CHUNK, width), lambda t: (chunk_of(t), 0))
    return pl.pallas_call(
        body, name=name, out_shape=[jax.ShapeDtypeStruct((rows, width), f32)] * n_out,
        grid=(rows // SCAN_CHUNK,), in_specs=[spec, spec], out_specs=[spec] * n_out,
        scratch_shapes=[pltpu.VMEM((8, width), f32)],
        compiler_params=_cparams("arbitrary"),
    )(a, v)


CONV_CHUNK = 256


def _fill_padded(pad_ref, src_ref, start, n):
    cb = pad_ref.shape[1]
    pad_ref[pl.ds(0, HALO), :] = jnp.zeros((HALO, cb), f32)
    pad_ref[pl.ds(HALO, n), :] = src_ref[pl.ds(start, n), :].astype(f32)
    pad_ref[pl.ds(HALO + n, HALO), :] = jnp.zeros((HALO, cb), f32)


def _dwconv_fwd(x, x_cb0, w, b, taps, pad_left, segments, cb, name, emit_bf16):
    rows = x.shape[0]
    width = w.shape[1]

    def body(x_ref, w_ref, b_ref, *rest):
        outs, xp = rest[:-1], rest[-1]
        for start, n in segments:
            _fill_padded(xp, x_ref, start, n)
            for c0 in range(0, n, CONV_CHUNK):
                acc = jnp.zeros((CONV_CHUNK, cb), f32) + b_ref[...]
                for k in range(taps):
                    acc = acc + w_ref[k:k + 1, :] * xp[pl.ds(HALO + c0 + k - pad_left, CONV_CHUNK), :]
                for o in outs:
                    o[pl.ds(start + c0, CONV_CHUNK), :] = acc.astype(o.dtype)

    out_dtypes = [f32, bf16] if emit_bf16 else [f32]
    return pl.pallas_call(
        body, name=name, out_shape=[jax.ShapeDtypeStruct((rows, width), dt) for dt in out_dtypes],
        grid=(width // cb,),
        in_specs=[pl.BlockSpec((rows, cb), lambda j: (0, j + x_cb0)), pl.BlockSpec((taps, cb), lambda j: (0, j)),
                  pl.BlockSpec((1, cb), lambda j: (0, j))],
        out_specs=[pl.BlockSpec((rows, cb), lambda j: (0, j))] * len(out_dtypes),
        scratch_shapes=[pltpu.VMEM((rows + 2 * HALO, cb), f32)],
        compiler_params=_cparams("parallel"),
    )(x, w, b)


def _dwconv_bwd(douts, x, x_cb0, w, taps, pad_left, segments, cb, name, dx_dtype):
    rows = x.shape[0]
    width = w.shape[1]
    nd = len(douts)

    def body(*refs):
        d_refs, x_ref, w_ref = refs[:nd], refs[nd], refs[nd + 1]
        dx_ref, dw_ref, db_ref, xp, dp, dsum = refs[nd + 2:]
        dw_ref[...] = jnp.zeros_like(dw_ref)
        db_ref[...] = jnp.zeros_like(db_ref)
        if nd > 1:
            total = d_refs[0][...]
            for r in d_refs[1:]:
                total = total + r[...]
            dsum[...] = total
            d_ref = dsum
        else:
            d_ref = d_refs[0]
        for start, n in segments:
            _fill_padded(xp, x_ref, start, n)
            _fill_padded(dp, d_ref, start, n)
            for c0 in range(0, n, CONV_CHUNK):
                dchunk = dp[pl.ds(HALO + c0, CONV_CHUNK), :]
                db_ref[...] += jnp.sum(dchunk, axis=0, keepdims=True)
                acc = jnp.zeros((CONV_CHUNK, cb), f32)
                for k in range(taps):
                    acc = acc + w_ref[k:k + 1, :] * dp[pl.ds(HALO + c0 + pad_left - k, CONV_CHUNK), :]
                    xs = xp[pl.ds(HALO + c0 + k - pad_left, CONV_CHUNK), :]
                    dw_ref[k:k + 1, :] += jnp.sum(dchunk * xs, axis=0, keepdims=True)
                dx_ref[pl.ds(start + c0, CONV_CHUNK), :] = acc.astype(dx_ref.dtype)

    dspec = pl.BlockSpec((rows, cb), lambda j: (0, j))
    return pl.pallas_call(
        body, name=name,
        out_shape=[jax.ShapeDtypeStruct((rows, width), dx_dtype), jax.ShapeDtypeStruct((taps, width), f32),
                   jax.ShapeDtypeStruct((1, width), f32)],
        grid=(width // cb,),
        in_specs=[dspec] * nd + [pl.BlockSpec((rows, cb), lambda j: (0, j + x_cb0)),
                                 pl.BlockSpec((taps, cb), lambda j: (0, j))],
        out_specs=[dspec, pl.BlockSpec((taps, cb), lambda j: (0, j)), pl.BlockSpec((1, cb), lambda j: (0, j))],
        scratch_shapes=[pltpu.VMEM((rows + 2 * HALO, cb), f32), pltpu.VMEM((rows + 2 * HALO, cb), f32),
                        pltpu.VMEM((rows, cb), f32)],
        compiler_params=_cparams("parallel"),
    )(*douts, x, w)


def _ada_forward(c16, w_ada, b_loc):
    def body(c_ref, w_ref, b_ref, o_ref):
        cv = c_ref[...]
        s = (cv * _sigmoid(cv)).astype(bf16)
        o_ref[0] = jnp.dot(s, w_ref[0].astype(bf16), preferred_element_type=f32) + b_ref[0]

    return pl.pallas_call(
        body, name="ada_forward", out_shape=jax.ShapeDtypeStruct((2, 16, ADA_SHARD), f32), grid=(2,),
        in_specs=[pl.BlockSpec((16, D), lambda l: (0, 0)), pl.BlockSpec((1, D, ADA_SHARD), lambda l: (l, 0, 0)),
                  pl.BlockSpec((1, 1, ADA_SHARD), lambda l: (l, 0, 0))],
        out_specs=pl.BlockSpec((1, 16, ADA_SHARD), lambda l: (l, 0, 0)),
        compiler_params=_cparams("parallel"),
    )(c16, w_ada, b_loc)


def _ada_backward(c16, g16, w_ada):
    def body(c_ref, g_ref, w_ref, dw_ref, ds_ref):
        cv = c_ref[...]
        s = (cv * _sigmoid(cv)).astype(bf16)
        g = g_ref[0].astype(bf16)
        dw_ref[0] = lax.dot_general(s, g, (((0,), (0,)), ((), ())), preferred_element_type=f32)
        ds = lax.dot_general(g, w_ref[0].astype(bf16), (((1,), (1,)), ((), ())), preferred_element_type=f32)
        cc = cv[8:9]
        sg = _sigmoid(cc)
        dsilu = sg * (1.0 + cc * (1.0 - sg))
        ds_ref[0] = jnp.zeros((8, D), f32) + jnp.sum(ds[8:16], axis=0, keepdims=True) * dsilu

    return pl.pallas_call(
        body, name="ada_backward",
        out_shape=[jax.ShapeDtypeStruct((2, D, ADA_SHARD), f32), jax.ShapeDtypeStruct((2, 8, D), f32)], grid=(2,),
        in_specs=[pl.BlockSpec((16, D), lambda l: (0, 0)), pl.BlockSpec((1, 16, ADA_SHARD), lambda l: (l, 0, 0)),
                  pl.BlockSpec((1, D, ADA_SHARD), lambda l: (l, 0, 0))],
        out_specs=[pl.BlockSpec((1, D, ADA_SHARD), lambda l: (l, 0, 0)), pl.BlockSpec((1, 8, D), lambda l: (l, 0, 0))],
        compiler_params=_cparams("parallel"),
    )(c16, g16, w_ada)


def _adamw(pieces, w, m, v, name):
    rows, cols = w.shape
    n_arr = len(pieces)
    tm = 256 if (rows % 256 == 0 and rows > 256) else rows

    def body(*refs):
        p_refs = refs[:n_arr]
        w_ref, m_ref, v_ref, g_ref, d_ref, nm_ref, nv_ref = refs[n_arr:]
        g = None
        for p_ref in p_refs:
            for j in range(p_ref.shape[0]):
                term = p_ref[j].astype(f32)
                g = term if g is None else g + term
        m2 = ADAM_B1 * m_ref[...] + (1.0 - ADAM_B1) * g
        v2 = ADAM_B2 * v_ref[...] + (1.0 - ADAM_B2) * (g * g)
        m_hat = m2 / (1.0 - ADAM_B1 ** ADAM_STEP)
        v_hat = v2 / (1.0 - ADAM_B2 ** ADAM_STEP)
        g_ref[...] = g
        d_ref[...] = -ADAM_LR * (m_hat / (jnp.sqrt(v_hat) + ADAM_EPS) + ADAM_WD * w_ref[...])
        nm_ref[...] = m2
        nv_ref[...] = v2

    spec = pl.BlockSpec((tm, cols), lambda i: (i, 0))
    return pl.pallas_call(
        body, name=name, out_shape=[jax.ShapeDtypeStruct((rows, cols), f32)] * 4, grid=(rows // tm,),
        in_specs=[pl.BlockSpec((p.shape[0], tm, cols), lambda i: (0, i, 0)) for p in pieces] + [spec, spec, spec],
        out_specs=[spec] * 4, compiler_params=_cparams("parallel"),
    )(*pieces, w, m, v)


MLP_TM = 256
FB = F // N_DEV


def _stack_rows(vals, n):
    cols = vals[0].shape[1]
    rid = lax.broadcasted_iota(jnp.int32, (n, cols), 0)
    out = jnp.zeros((n, cols), f32)
    for k, v in enumerate(vals):
        out = jnp.where(rid == k, v, out)
    return out


N_MLP_PARAMS = 9


class _ParamRows:
    def __init__(self, ref):
        self.ref = ref

    def __getitem__(self, sl):
        return self.ref[8 * sl.start:8 * sl.start + 1, :]


def _resident(shape, imap):
    return pl.BlockSpec(shape, imap, pipeline_mode=pl.Buffered(1))


def _mlp_forward(xa, xa_roff, out_prev, par, w_in, w_out, layer, name):
    def body(xa_ref, op_ref, par_ref, win_ref, wout_ref, x1_ref, h_ref, r_ref, mo_ref, x2_ref, hn_ref):
        p = _ParamRows(par_ref)
        x1 = xa_ref[...] + p[0:1] * (op_ref[...] + p[1:2])
        h = _normmod(x1, p[2:3], p[3:4], p[4:5]).astype(bf16)
        x1_ref[...] = x1
        h_ref[...] = h
        mo = jnp.zeros((MLP_TM, D), f32)
        for j in range(N_DEV):
            r = jnp.maximum(jnp.dot(h, win_ref[j], preferred_element_type=f32), 0.0)
            r_ref[:, j * FB:(j + 1) * FB] = r.astype(bf16)
            mo = mo + jnp.dot((r * r).astype(bf16), wout_ref[j], preferred_element_type=f32)
        mo_ref[...] = mo.astype(bf16)
        x2 = x1 + p[5:6] * mo
        x2_ref[...] = x2
        hn_ref[...] = _normmod(x2, p[6:7], p[7:8], p[8:9]).astype(bf16)

    row = lambda width: pl.BlockSpec((MLP_TM, width), lambda i: (i, 0))
    return pl.pallas_call(
        body, name=name, grid=(T_LAT // MLP_TM,),
        out_shape=[jax.ShapeDtypeStruct((T_LAT, D), f32), jax.ShapeDtypeStruct((T_LAT, D), bf16),
                   jax.ShapeDtypeStruct((T_LAT, F), bf16), jax.ShapeDtypeStruct((T_LAT, D), bf16),
                   jax.ShapeDtypeStruct((T_LAT, D), f32), jax.ShapeDtypeStruct((T_LAT, D), bf16)],
        in_specs=[pl.BlockSpec((MLP_TM, D), lambda i: (i + xa_roff, 0)), row(D), pl.BlockSpec((8 * N_MLP_PARAMS, D), lambda i: (0, 0)),
                  _resident((N_DEV, None, D, FB), lambda i: (0, layer, 0, 0)),
                  _resident((N_DEV, None, FB, D), lambda i: (0, layer, 0, 0))],
        out_specs=[row(D), row(D), row(F), row(D), row(D), row(D)],
        compiler_params=_cparams("parallel"),
    )(xa, out_prev, par, w_in, w_out)


def _mlp_backward(dx2, x1, r, mo, out_prev, par, w_in, w_out, layer, name):
    nt = (((1,), (1,)), ((), ()))

    def body(dx2_ref, x1_ref, r_ref, mo_ref, op_ref, par_ref, win_ref, wout_ref, dx1_ref, dop_ref, dmo_ref, dhid_ref,
             acc_ref):
        p = _ParamRows(par_ref)
        dx2v = dx2_ref[...]
        dmo = (p[5:6] * dx2v).astype(bf16)
        dmo_ref[...] = dmo
        dh = jnp.zeros((MLP_TM, D), f32)
        mo = mo_ref[...].astype(f32)
        for j in range(N_DEV):
            rf = r_ref[:, j * FB:(j + 1) * FB].astype(f32)
            dact = lax.dot_general(dmo, wout_ref[j], nt, preferred_element_type=f32)
            dhid = (dact * (2.0 * rf)).astype(bf16)
            dhid_ref[:, j * FB:(j + 1) * FB] = dhid
            dh = dh + lax.dot_general(dhid, win_ref[j], nt, preferred_element_type=f32)
        x1 = x1_ref[...]
        _, vjp = jax.vjp(_normmod, x1, p[2:3], p[3:4], p[4:5])
        dx, dng, dsc, dsh = vjp(dh)
        dx1 = dx2v + dx
        dx1_ref[...] = dx1
        dop_ref[...] = (p[0:1] * dx1).astype(bf16)
        sums = _stack_rows([jnp.sum(dx1 * (op_ref[...] + p[1:2]), axis=0, keepdims=True),
                            p[0:1] * jnp.sum(dx1, axis=0, keepdims=True), dng, dsc, dsh,
                            jnp.sum(dx2v * mo, axis=0, keepdims=True)], 8)

        @pl.when(pl.program_id(0) == 0)
        def _():
            acc_ref[...] = jnp.zeros_like(acc_ref)

        acc_ref[...] += sums

    row = lambda width: pl.BlockSpec((MLP_TM, width), lambda i: (i, 0))
    return pl.pallas_call(
        body, name=name, grid=(T_LAT // MLP_TM,),
        out_shape=[jax.ShapeDtypeStruct((T_LAT, D), f32), jax.ShapeDtypeStruct((T_LAT, D), bf16),
                   jax.ShapeDtypeStruct((T_LAT, D), bf16), jax.ShapeDtypeStruct((T_LAT, F), bf16),
                   jax.ShapeDtypeStruct((8, D), f32)],
        in_specs=[row(D), row(D), row(F), row(D), row(D), pl.BlockSpec((8 * N_MLP_PARAMS, D), lambda i: (0, 0)),
                  _resident((N_DEV, None, D, FB), lambda i: (0, layer, 0, 0)),
                  _resident((N_DEV, None, FB, D), lambda i: (0, layer, 0, 0))],
        out_specs=[row(D), row(D), row(D), row(F), pl.BlockSpec((8, D), lambda i: (0, 0))],
        compiler_params=_cparams("arbitrary"),
    )(dx2, x1, r, mo, out_prev, par, w_in, w_out)


def _mlp_weight_grads(h, dhid, r, dmo, tag):
    tn = (((0,), (0,)), ((), ()))

    def body_in(h_ref, dhid_ref, o_ref):
        o_ref[...] = lax.dot_general(h_ref[...], dhid_ref[...], tn, preferred_element_type=f32).astype(bf16)

    def body_out(r_ref, dmo_ref, o_ref):
        rf = r_ref[...].astype(f32)
        o_ref[...] = lax.dot_general((rf * rf).astype(bf16), dmo_ref[...], tn, preferred_element_type=f32).astype(bf16)

    dw_in = pl.pallas_call(
        body_in, name=tag + "_mlp_in_dw", grid=(N_DEV,), out_shape=jax.ShapeDtypeStruct((N_DEV, D, FB), bf16),
        in_specs=[_resident((T_LAT, D), lambda j: (0, 0)), pl.BlockSpec((T_LAT, FB), lambda j: (0, j))],
        out_specs=pl.BlockSpec((None, D, FB), lambda j: (j, 0, 0)), compiler_params=_cparams("parallel"),
    )(h, dhid)
    dw_out = pl.pallas_call(
        body_out, name=tag + "_mlp_out_dw", grid=(N_DEV,), out_shape=jax.ShapeDtypeStruct((N_DEV, FB, D), bf16),
        in_specs=[pl.BlockSpec((T_LAT, FB), lambda j: (0, j)), _resident((T_LAT, D), lambda j: (0, 0))],
        out_specs=pl.BlockSpec((None, FB, D), lambda j: (j, 0, 0)), compiler_params=_cparams("parallel"),
    )(r, dmo)
    return dw_in, dw_out


def _pos_embed():
    t = jnp.arange(T_LAT, dtype=jnp.int32)
    row = (t // GRID_W).astype(f32)
    col = (t % GRID_W).astype(f32)
    q = D // 4
    omega = 1.0 / (POS_BASE ** (jnp.arange(q, dtype=f32) / q))
    er = row[:, None] * omega[None, :]
    ec = col[:, None] * omega[None, :]
    return jnp.concatenate([jnp.sin(er), jnp.cos(er), jnp.sin(ec), jnp.cos(ec)], axis=-1).astype(f32)


def _dense_gates(w_a, w_x):
    eye = jnp.eye(N_BLK, dtype=f32)
    parts = []
    for d in range(2):
        for w in (w_a, w_x):
            parts.append(jnp.einsum("hij,hg->higj", w[d], eye).reshape(R, R))
    return jnp.concatenate(parts, axis=1).astype(bf16)


def _gate_blocks(dwg, part):
    blk = dwg[:, part * R:(part + 1) * R].reshape(N_BLK, BLK, N_BLK, BLK)
    return jnp.moveaxis(jnp.diagonal(blk, axis1=0, axis2=2), -1, 0)


N_SCAN_CHUNKS = T_ALL // SCAN_CHUNK
SCAN_FWD = lambda t: t
SCAN_FWD_BWD = lambda t: N_SCAN_CHUNKS - 1 - t
SCAN_REV = lambda t: jnp.where(t == 0, 0, N_SCAN_CHUNKS - t)
SCAN_REV_BWD = lambda t: jnp.where(t == N_SCAN_CHUNKS - 1, 0, t + 1)
CONV_SEGMENTS = ((0, T_CTX), (T_CTX, T_LAT))
TM = 128
N_CTX_TILES = T_CTX // TM


def _local_step(x, ctx, target, mods, cmod, wts):
    sh1, sc1, g1, sh2, sc2, g2 = [[mods[l, i][None] for l in range(2)] for i in range(N_MOD)]
    ng = wts["norm_g"]
    xcat = jnp.concatenate([ctx, x], axis=0)
    poscat = jnp.concatenate([jnp.zeros((T_CTX, D), f32), _pos_embed()], axis=0)
    scp = jnp.concatenate([cmod[1][None], sc1[0]], axis=0)
    shp = jnp.concatenate([cmod[0][None], sh1[0]], axis=0)

    def blend(i, p):
        sel = jnp.where(i < N_CTX_TILES, 1.0, 0.0)
        return sel * p[0:1] + (1.0 - sel) * p[1:2]

    def f_pre0(i, xc, pos, g, scp_, shp_):
        x0 = xc + pos
        return x0, _normmod(x0, g, blend(i, scp_), blend(i, shp_))

    x0cat, h0 = _rowcall(f_pre0, "l0_prenorm", T_ALL, TM, [_rin(xcat), _rin(poscat)], [ng[0, 0][None], scp, shp],
                         [(D, f32), (D, bf16)])
    gr = _mm(h0, wts["rec_w_in"], "l0_in_proj")
    u, ub = _dwconv_fwd(gr, R // 256, wts["rec_conv_w"], wts["rec_conv_b"], 4, 1, CONV_SEGMENTS, 256,
                        "l0_conv", True)
    pre = _mm(ub, wts["gates"], "l0_gates")

    def f_coeff(i, pre_, u_, ba, bx, lam):
        outs = []
        for d in range(2):
            a, b = _coeff(pre_[:, 2 * d * R:(2 * d + 1) * R], pre_[:, (2 * d + 1) * R:(2 * d + 2) * R], u_,
                          ba[d:d + 1], bx[d:d + 1], lam[d:d + 1])
            outs += [a, b]
        return tuple(outs)

    a0, b0, a1, b1 = _rowcall(f_coeff, "l0_coeff", T_ALL, TM, [_rin(pre), _rin(u)],
                              [wts["rec_b_a"], wts["rec_b_x"], wts["rec_lambda"]], [(R, f32)] * 4)
    y0, yp0 = _scan_call(a0, b0, SCAN_FWD, False, "l0_scan_fwd", False)
    y1, yp1 = _scan_call(a1, b1, SCAN_REV, True, "l0_scan_rev", False)

    def f_gate(i, gp, y0_, y1_):
        return (_gelu(gp) * (y0_ + y1_),)

    (zb,) = _rowcall(f_gate, "l0_gate", T_LAT, TM,
                     [_rin(gr, R, 0, N_CTX_TILES), _rin(y0, None, 0, N_CTX_TILES), _rin(y1, None, 0, N_CTX_TILES)],
                     [], [(R, bf16)])
    out0 = _mm(zb, wts["rec_w_out"], "l0_out_proj")

    zero_d = jnp.zeros((1, D), f32)

    def mlp_params(rows):
        rows = rows + [zero_d] * (N_MLP_PARAMS - len(rows))
        return jnp.concatenate([jnp.broadcast_to(r, (8, D)) for r in rows], axis=0)

    par0 = mlp_params([g1[0], zero_d, ng[0, 1][None], sc2[0], sh2[0], g2[0], ng[1, 0][None], sc1[1], sh1[1]])
    x1, h1, r0, mo0, x2, h2 = _mlp_forward(x0cat, T_CTX // MLP_TM, out0, par0, wts["mlp_w_in"], wts["mlp_w_out"], 0,
                                           "l0_mlp")

    pw = _mm(h2, wts["conf_w_pw1"], "l1_pw1")

    def f_glu(i, pa, pb, b1):
        return ((pa + b1[:, :D]) * _sigmoid(pb + b1[:, D:]),)

    (zg,) = _rowcall(f_glu, "l1_glu", T_LAT, TM, [_rin(pw, D, 0), _rin(pw, D, 1)], [wts["conf_b_pw1"]], [(D, f32)])
    (zc,) = _dwconv_fwd(zg, 0, wts["conf_conv_w"], wts["conf_conv_b"], 31, 15, ((0, T_LAT),), 128, "l1_conv", False)

    def ln_silu(z, lg, lb):
        mu = jnp.mean(z, axis=-1, keepdims=True)
        zc_ = z - mu
        var = jnp.mean(zc_ * zc_, axis=-1, keepdims=True)
        yv = zc_ * lax.rsqrt(var + EPS) * lg + lb
        return yv * _sigmoid(yv)

    def f_lnsilu(i, z, lg, lb):
        return (ln_silu(z, lg, lb),)

    (sb,) = _rowcall(f_lnsilu, "l1_ln_silu", T_LAT, TM, [_rin(zc)], [wts["conf_ln_g"], wts["conf_ln_b"]], [(D, bf16)])
    out1 = _mm(sb, wts["conf_w_pw2"], "l1_pw2")
    par1 = mlp_params([g1[1], wts["conf_b_pw2"], ng[1, 1][None], sc2[1], sh2[1], g2[1]])
    x3, h3, r1, mo1, x4, _ = _mlp_forward(x2, 0, out1, par1, wts["mlp_w_in"], wts["mlp_w_out"], 1, "l1_mlp")

    def loss_fn(x4_, fg, tgt):
        err = _rms(x4_, fg) - tgt
        per_row = jnp.mean(err * err, axis=-1, keepdims=True)
        return 0.5 * jnp.sum(per_row, axis=0, keepdims=True)

    def f_head(i, x4_, tgt, fg):
        loss, vjp = jax.vjp(lambda a, e: loss_fn(a, e, tgt), x4_, fg)
        dx, dfg = vjp(jnp.ones((1, 1), f32))
        return dx, jnp.broadcast_to(loss, (1, 128)), dfg

    dx4, loss_acc, dfinal_g = _rowcall(f_head, "head", T_LAT, TM, [_rin(x4), _rin(target)], [wts["final_g"]],
                                       [(D, f32)], [(1, 128), (1, D)])

    grads = {"final_g": dfinal_g}

    def normmod_bwd(xin, dh, dx_skip, g, sc, sh, tag):
        def fb(i, x_, dh_, dxs, g_, sc_, sh_):
            _, vjp = jax.vjp(_normmod, x_, g_, sc_, sh_)
            dx, dg, dsc, dsh = vjp(dh_)
            return dx + dxs, dg, dsc, dsh

        return _rowcall(fb, tag + "_normmod_bwd", T_LAT, TM, [_rin(xin), _rin(dh), _rin(dx_skip)], [g, sc, sh],
                        [(D, f32)], [(1, D)] * 3)

    dx3, dout1, dmo1, dhid1, acc1 = _mlp_backward(dx4, x3, r1, mo1, out1, par1, wts["mlp_w_in"], wts["mlp_w_out"], 1,
                                                  "l1_mlp_bwd")
    dw_in1, dw_out1 = _mlp_weight_grads(h3, dhid1, r1, dmo1, "l1")
    dg1_1, db_pw2, dng11, dsc2_1, dsh2_1, dg2_1 = [acc1[k:k + 1] for k in range(6)]

    ds = _mm(dout1, wts["conf_w_pw2"], "l1_pw2_dx", tb=True)
    grads["conf_w_pw2"] = _mm(sb, dout1, "l1_pw2_dw", ta=True, out_dtype=bf16)
    grads["conf_b_pw2"] = db_pw2

    def f_lnsilu_bwd(i, z, ds_, lg, lb):
        _, vjp = jax.vjp(ln_silu, z, lg, lb)
        return vjp(ds_)

    dzc, dln_g, dln_b = _rowcall(f_lnsilu_bwd, "l1_ln_silu_bwd", T_LAT, TM, [_rin(zc), _rin(ds)],
                                 [wts["conf_ln_g"], wts["conf_ln_b"]], [(D, f32)], [(1, D)] * 2)
    grads["conf_ln_g"], grads["conf_ln_b"] = dln_g, dln_b
    dzg, dconv_w, dconv_b = _dwconv_bwd([dzc], zg, 0, wts["conf_conv_w"], 31, 15, ((0, T_LAT),), 128,
                                        "l1_conv_bwd", f32)
    grads["conf_conv_w"], grads["conf_conv_b"] = dconv_w, dconv_b

    def f_glu_bwd(i, pa, pb, dz, b1):
        _, vjp = jax.vjp(lambda a, b, c: (a + c[:, :D]) * _sigmoid(b + c[:, D:]), pa, pb, b1)
        da, db, dc = vjp(dz)
        return jnp.concatenate([da, db], axis=1), dc

    dpw, db_pw1 = _rowcall(f_glu_bwd, "l1_glu_bwd", T_LAT, TM, [_rin(pw, D, 0), _rin(pw, D, 1), _rin(dzg)],
                           [wts["conf_b_pw1"]], [(2 * D, bf16)], [(1, 2 * D)])
    grads["conf_b_pw1"] = db_pw1
    dh2 = _mm(dpw, wts["conf_w_pw1"], "l1_pw1_dx", tb=True)
    grads["conf_w_pw1"] = _mm(h2, dpw, "l1_pw1_dw", ta=True, out_dtype=bf16)
    dx2, dng10, dsc1_1, dsh1_1 = normmod_bwd(x2, dh2, dx3, ng[1, 0][None], sc1[1], sh1[1], "l1a")

    dx1, dout0, dmo0, dhid0, acc0 = _mlp_backward(dx2, x1, r0, mo0, out0, par0, wts["mlp_w_in"], wts["mlp_w_out"], 0,
                                                  "l0_mlp_bwd")
    dw_in0, dw_out0 = _mlp_weight_grads(h1, dhid0, r0, dmo0, "l0")
    dg1_0, _, dng01, dsc2_0, dsh2_0, dg2_0 = [acc0[k:k + 1] for k in range(6)]
    grads["mlp_w_in"] = (dw_in0, dw_in1)
    grads["mlp_w_out"] = (dw_out0, dw_out1)

    dz = _mm(dout0, wts["rec_w_out"], "l0_out_proj_dx", tb=True)
    grads["rec_w_out"] = _mm(zb, dout0, "l0_out_proj_dw", ta=True, out_dtype=bf16)

    def f_gate_bwd(i, gp, y0_, y1_, dz_):
        lat = jnp.where(i < N_CTX_TILES, 0.0, 1.0)
        _, vjp = jax.vjp(lambda a, b: _gelu(a) * b, gp, y0_ + y1_)
        dgp, dy = vjp(dz_)
        return dgp * lat, dy * lat

    dgp, dy = _rowcall(f_gate_bwd, "l0_gate_bwd", T_ALL, TM,
                       [_rin(gr, R, 0), _rin(y0), _rin(y1), _rin(dz, None, 0, -N_CTX_TILES)], [],
                       [(R, bf16), (R, f32)])
    (dh_f,) = _scan_call(a0, dy, SCAN_FWD_BWD, True, "l0_scan_fwd_bwd", True)
    (dh_r,) = _scan_call(a1, dy, SCAN_REV_BWD, False, "l0_scan_rev_bwd", True)

    def f_coeff_bwd(i, pre_, u_, dhf, dhr, ypf, ypr, ba, bx, lam):
        dpre, dba, dbx, dlam = [], [], [], []
        du = jnp.zeros_like(u_)
        for d, (dh_, yp_) in enumerate(((dhf, ypf), (dhr, ypr))):
            _, vjp = jax.vjp(_coeff, pre_[:, 2 * d * R:(2 * d + 1) * R], pre_[:, (2 * d + 1) * R:(2 * d + 2) * R], u_,
                             ba[d:d + 1], bx[d:d + 1], lam[d:d + 1])
            dpa, dpx, du_d, dba_d, dbx_d, dlam_d = vjp((dh_ * yp_, dh_))
            dpre += [dpa, dpx]
            du = du + du_d
            dba.append(dba_d)
            dbx.append(dbx_d)
            dlam.append(dlam_d)
        return jnp.concatenate(dpre, axis=1), du, _rows2(*dba), _rows2(*dbx), _rows2(*dlam)

    dpre, du_direct, db_a, db_x, dlam = _rowcall(
        f_coeff_bwd, "l0_coeff_bwd", T_ALL, 64,
        [_rin(pre), _rin(u), _rin(dh_f), _rin(dh_r), _rin(yp0), _rin(yp1)],
        [wts["rec_b_a"], wts["rec_b_x"], wts["rec_lambda"]], [(4 * R, bf16), (R, f32)], [(2, R)] * 3)
    grads["rec_b_a"], grads["rec_b_x"], grads["rec_lambda"] = db_a, db_x, dlam
    du_gates = _mm(dpre, wts["gates"], "l0_gates_dx", tb=True)
    grads["gates"] = _mm(ub, dpre, "l0_gates_dw", ta=True)
    drec, dconv4_w, dconv4_b = _dwconv_bwd([du_direct, du_gates], gr, R // 256, wts["rec_conv_w"], 4, 1,
                                           CONV_SEGMENTS, 256, "l0_conv_bwd", bf16)
    grads["rec_conv_w"], grads["rec_conv_b"] = dconv4_w, dconv4_b
    dgr = jnp.concatenate([dgp, drec], axis=1)
    dh0 = _mm(dgr, wts["rec_w_in"], "l0_in_proj_dx", tb=True)
    grads["rec_w_in"] = _mm(h0, dgr, "l0_in_proj_dw", ta=True, out_dtype=bf16)

    def f_pre0_bwd(i, x0, dh_, dxs, g, scp_, shp_):
        lat = jnp.where(i < N_CTX_TILES, 0.0, 1.0)
        _, vjp = jax.vjp(lambda a, b, c, e: _normmod(a, b, blend(i, c), blend(i, e)), x0, g, scp_, shp_)
        dx, dg, dscp, dshp = vjp(dh_)
        return dx + lat * dxs, dg, dscp, dshp

    dx0cat, dng00, dscp, dshp = _rowcall(
        f_pre0_bwd, "l0_prenorm_bwd", T_ALL, TM, [_rin(x0cat), _rin(dh0), _rin(dx1, None, 0, -N_CTX_TILES)],
        [ng[0, 0][None], scp, shp], [(D, f32)], [(1, D), (2, D), (2, D)])

    grads["norm_g"] = jnp.stack([jnp.concatenate([dng00, dng01], 0), jnp.concatenate([dng10, dng11], 0)])
    dmods = jnp.stack([
        jnp.concatenate([dshp[1:2], dscp[1:2], dg1_0, dsh2_0, dsc2_0, dg2_0], axis=0),
        jnp.concatenate([dsh1_1, dsc1_1, dg1_1, dsh2_1, dsc2_1, dg2_1], axis=0)])
    dcmod = jnp.concatenate([dshp[0:1], dscp[0:1]], axis=0)
    return loss_acc[0, 0], dx0cat[T_CTX:], dmods, dcmod, grads


def _unshard_cols(g):
    g = jnp.moveaxis(g, 0, -2)
    return g.reshape(g.shape[:-2] + (g.shape[-2] * g.shape[-1],))


def _shard_cols(w):
    w = w.reshape(w.shape[:-1] + (N_DEV, w.shape[-1] // N_DEV))
    return jnp.moveaxis(w, -2, 0)


def _shard_rows(w):
    return w.reshape((N_DEV, w.shape[0] // N_DEV) + w.shape[1:])


SMALL_PACK_ROWS = 64


def kernel(x, c, ctx, c_ctx, w_ada, b_ada, norm_g, rec_w_in, rec_conv_w, rec_conv_b, rec_lambda, rec_w_a, rec_b_a, rec_w_x, rec_b_x, rec_w_out, conf_w_pw1, conf_b_pw1, conf_conv_w, conf_conv_b, conf_ln_g, conf_ln_b, conf_w_pw2, conf_b_pw2, mlp_w_in, mlp_w_out, final_g, loss_target, m_c_ctx, m_w_ada, m_b_ada, m_norm_g, m_rec_w_in, m_rec_conv_w, m_rec_conv_b, m_rec_lambda, m_rec_w_a, m_rec_b_a, m_rec_w_x, m_rec_b_x, m_rec_w_out, m_conf_w_pw1, m_conf_b_pw1, m_conf_conv_w, m_conf_conv_b, m_conf_ln_g, m_conf_ln_b, m_conf_w_pw2, m_conf_b_pw2, m_mlp_w_in, m_mlp_w_out, m_final_g, v_c_ctx, v_w_ada, v_b_ada, v_norm_g, v_rec_w_in, v_rec_conv_w, v_rec_conv_b, v_rec_lambda, v_rec_w_a, v_rec_b_a, v_rec_w_x, v_rec_b_x, v_rec_w_out, v_conf_w_pw1, v_conf_b_pw1, v_conf_conv_w, v_conf_conv_b, v_conf_ln_g, v_conf_ln_b, v_conf_w_pw2, v_conf_b_pw2, v_mlp_w_in, v_mlp_w_out, v_final_g):
    me = 4 * lax.axis_index("x") + 2 * lax.axis_index("y") + lax.axis_index("c")
    weights = dict(c_ctx=c_ctx, w_ada=w_ada, b_ada=b_ada, norm_g=norm_g, rec_w_in=rec_w_in, rec_conv_w=rec_conv_w,
                   rec_conv_b=rec_conv_b, rec_lambda=rec_lambda, rec_w_a=rec_w_a, rec_b_a=rec_b_a, rec_w_x=rec_w_x,
                   rec_b_x=rec_b_x, rec_w_out=rec_w_out, conf_w_pw1=conf_w_pw1, conf_b_pw1=conf_b_pw1,
                   conf_conv_w=conf_conv_w, conf_conv_b=conf_conv_b, conf_ln_g=conf_ln_g, conf_ln_b=conf_ln_b,
                   conf_w_pw2=conf_w_pw2, conf_b_pw2=conf_b_pw2, mlp_w_in=mlp_w_in, mlp_w_out=mlp_w_out, final_g=final_g)
    m_in = dict(c_ctx=m_c_ctx, w_ada=m_w_ada, b_ada=m_b_ada, norm_g=m_norm_g, rec_w_in=m_rec_w_in, rec_conv_w=m_rec_conv_w,
                rec_conv_b=m_rec_conv_b, rec_lambda=m_rec_lambda, rec_w_a=m_rec_w_a, rec_b_a=m_rec_b_a, rec_w_x=m_rec_w_x,
                rec_b_x=m_rec_b_x, rec_w_out=m_rec_w_out, conf_w_pw1=m_conf_w_pw1, conf_b_pw1=m_conf_b_pw1,
                conf_conv_w=m_conf_conv_w, conf_conv_b=m_conf_conv_b, conf_ln_g=m_conf_ln_g, conf_ln_b=m_conf_ln_b,
                conf_w_pw2=m_conf_w_pw2, conf_b_pw2=m_conf_b_pw2, mlp_w_in=m_mlp_w_in, mlp_w_out=m_mlp_w_out,
                final_g=m_final_g)
    v_in = dict(c_ctx=v_c_ctx, w_ada=v_w_ada, b_ada=v_b_ada, norm_g=v_norm_g, rec_w_in=v_rec_w_in, rec_conv_w=v_rec_conv_w,
                rec_conv_b=v_rec_conv_b, rec_lambda=v_rec_lambda, rec_w_a=v_rec_w_a, rec_b_a=v_rec_b_a, rec_w_x=v_rec_w_x,
                rec_b_x=v_rec_b_x, rec_w_out=v_rec_w_out, conf_w_pw1=v_conf_w_pw1, conf_b_pw1=v_conf_b_pw1,
                conf_conv_w=v_conf_conv_w, conf_conv_b=v_conf_conv_b, conf_ln_g=v_conf_ln_g, conf_ln_b=v_conf_ln_b,
                conf_w_pw2=v_conf_w_pw2, conf_b_pw2=v_conf_b_pw2, mlp_w_in=v_mlp_w_in, mlp_w_out=v_mlp_w_out,
                final_g=v_final_g)
    names = list(weights)

    small_items = [c, norm_g, rec_conv_w, rec_lambda, conf_b_pw1, conf_conv_w, conf_conv_b, conf_ln_g, conf_ln_b,
                   conf_b_pw2]
    flat = jnp.concatenate([a.reshape(-1) for a in small_items])
    flat = jnp.pad(flat, (0, SMALL_PACK_ROWS * 128 - flat.shape[0])).reshape(SMALL_PACK_ROWS, 128)
    big_items = [rec_w_in[0], rec_w_out[0], conf_w_pw1[0], conf_w_pw2[0], mlp_w_in, mlp_w_out]
    (small_all,) = _all_gather([flat], "gather_small")
    big_flat = jnp.concatenate([a.astype(bf16).reshape(-1) for a in big_items]).reshape(-1, D)
    big_gathered = _all_gather_2level(big_flat, "gather_weights").reshape(N_DEV, -1)
    big_all, off = [], 0
    for a in big_items:
        big_all.append(big_gathered[:, off:off + a.size].reshape((N_DEV,) + a.shape))
        off += a.size

    small_all = small_all.reshape(N_DEV, -1)
    off = 0
    small = []
    for a in small_items:
        small.append(small_all[:, off:off + a.size].reshape((N_DEV,) + a.shape))
        off += a.size
    c_all, ng_all, rcw_all, lam_all, bpw1_all, ccw_all, ccb_all, lng_all, lnb_all, bpw2_all = small
    wts = {
        "norm_g": _unshard_cols(ng_all),
        "rec_conv_w": _unshard_cols(rcw_all)[0],
        "rec_lambda": _unshard_cols(lam_all)[0],
        "conf_b_pw1": _unshard_cols(bpw1_all),
        "conf_conv_w": _unshard_cols(ccw_all)[0],
        "conf_conv_b": _unshard_cols(ccb_all),
        "conf_ln_g": _unshard_cols(lng_all),
        "conf_ln_b": _unshard_cols(lnb_all),
        "conf_b_pw2": _unshard_cols(bpw2_all),
        "rec_conv_b": rec_conv_b,
        "rec_b_a": rec_b_a[0].reshape(2, R),
        "rec_b_x": rec_b_x[0].reshape(2, R),
        "final_g": final_g[None],
        "gates": _dense_gates(rec_w_a[0], rec_w_x[0]),
        "rec_w_in": _unshard_cols(big_all[0]),
        "rec_w_out": big_all[1].reshape(R, D),
        "conf_w_pw1": _unshard_cols(big_all[2]),
        "conf_w_pw2": big_all[3].reshape(D, D),
        "mlp_w_in": big_all[4],
        "mlp_w_out": big_all[5],
    }

    c16 = jnp.concatenate([c_all[:, 0], jnp.broadcast_to(c_ctx[None], (8, D))], axis=0)
    b_loc = lax.dynamic_slice_in_dim(b_ada, me * ADA_SHARD, ADA_SHARD, axis=1)[:, None]
    (mods_all,) = _all_gather([_ada_forward(c16, w_ada, b_loc)], "gather_mods")
    mods_all = _unshard_cols(mods_all)
    mods = lax.dynamic_index_in_dim(mods_all, me, axis=1, keepdims=False).reshape(2, N_MOD, D)
    cmod = mods_all[0, 8, :2 * D].reshape(2, D)

    loss_part, grad_x, dmods, dcmod, grads = _local_step(x[0], ctx[0], loss_target[0], mods, cmod, wts)
    loss = lax.psum(loss_part, ("x", "y", "c"))

    dm_flat = jnp.concatenate([dmods.reshape(-1), dcmod.reshape(-1)]).reshape(-1, 128)
    (dm_all,) = _all_gather([dm_flat], "gather_dmods")
    dm_all = dm_all.reshape(N_DEV, -1)
    dmods_all = dm_all[:, :2 * N_MOD * D].reshape(N_DEV, 2, N_MOD * D)
    dcmod_all = jnp.pad(dm_all[:, 2 * N_MOD * D:], ((0, 0), (0, (N_MOD - 2) * D)))
    g16_full = jnp.stack([jnp.concatenate([dmods_all[:, 0], dcmod_all], axis=0),
                          jnp.concatenate([dmods_all[:, 1], jnp.zeros_like(dcmod_all)], axis=0)])
    g16 = lax.dynamic_slice_in_dim(g16_full, me * ADA_SHARD, ADA_SHARD, axis=2)
    dw_ada, ds_part = _ada_backward(c16, g16, w_ada)
    (ds_all,) = _all_gather([ds_part[0]], "gather_dsilu")

    big_grads = [_shard_cols(grads["rec_w_in"]), _shard_rows(grads["rec_w_out"]), _shard_cols(grads["conf_w_pw1"]),
                 _shard_rows(grads["conf_w_pw2"]),
                 jnp.stack(grads["mlp_w_in"], axis=1), jnp.stack(grads["mlp_w_out"], axis=1)]
    small_sharded = ["norm_g", "rec_conv_w", "rec_lambda", "conf_b_pw1", "conf_conv_w", "conf_conv_b", "conf_ln_g",
                     "conf_ln_b", "conf_b_pw2"]
    pack = jnp.concatenate([_shard_cols(grads[n]).reshape(N_DEV, -1) for n in small_sharded], axis=1)
    pack_len = pack.shape[1]
    pack = jnp.pad(pack, ((0, 0), (0, SMALL_PACK_ROWS * 128 - pack_len))).reshape(N_DEV, SMALL_PACK_ROWS, 128)
    big_names = ["rec_w_in", "rec_w_out", "conf_w_pw1", "conf_w_pw2", "mlp_w_in", "mlp_w_out"]
    big_pack = jnp.concatenate([g.reshape(N_DEV, -1) for g in big_grads], axis=1).reshape(N_DEV, -1, D)
    big_f32, big_last = _reduce_scatter(big_pack, "scatter_weight_grads")
    big_f32, big_last = big_f32.reshape(-1), big_last.reshape(-1)
    (pack_recv,) = _all_to_all([pack], "scatter_small_grads")
    pack_recv = pack_recv.reshape(N_DEV, -1)

    dwg = grads["gates"]
    repl = {"rec_conv_b": grads["rec_conv_b"],
            "rec_w_a": jnp.stack([_gate_blocks(dwg, 0), _gate_blocks(dwg, 2)])[None],
            "rec_w_x": jnp.stack([_gate_blocks(dwg, 1), _gate_blocks(dwg, 3)])[None],
            "rec_b_a": grads["rec_b_a"].reshape(1, 2, N_BLK, BLK),
            "rec_b_x": grads["rec_b_x"].reshape(1, 2, N_BLK, BLK),
            "final_g": grads["final_g"][0]}
    repl_names = list(repl)
    repl_flat = jnp.concatenate([repl[n].reshape(-1) for n in repl_names])
    repl_len = repl_flat.shape[0]
    repl_rows = -(-repl_len // (16 * D)) * 16
    repl_flat = jnp.pad(repl_flat, (0, repl_rows * D - repl_len)).reshape(repl_rows, D).astype(bf16)
    repl_all = _all_gather_2level(repl_flat, "gather_replicated_grads").reshape(N_DEV, -1)

    pieces = {}
    shard_shapes = {n: weights[n].shape for n in names}
    off = 0
    for n in big_names:
        size = weights[n].size
        pieces[n] = [big_f32[off:off + size], big_last[off:off + size]]
        off += size
    off = 0
    for n in small_sharded:
        size = weights[n].size
        pieces[n] = [pack_recv[:, off:off + size]]
        off += size
    off = 0
    for n in repl_names:
        size = weights[n].size
        pieces[n] = [repl_all[:, off:off + size]]
        off += size
    pieces["w_ada"] = [dw_ada]
    db_terms = jnp.concatenate([dmods_all, jnp.stack([dcmod_all, jnp.zeros_like(dcmod_all)], axis=1)], axis=0)
    pieces["b_ada"] = [db_terms]
    pieces["c_ctx"] = [ds_all[:, 0]]

    def as2d(shape):
        if len(shape) == 1:
            return (1, shape[0])
        if len(shape) == 5:
            return (shape[0] * shape[1] * shape[2], shape[3] * shape[4])
        rows = 1
        for s in shape[:-1]:
            rows *= s
        return (rows, shape[-1])

    g_out, d_out, m_out, v_out = {}, {}, {}, {}
    for n in names:
        shape = shard_shapes[n]
        r2, c2 = as2d(shape)
        p = [piece.reshape(-1, r2, c2) for piece in pieces[n]]
        g, dl, nm, nv = _adamw(p, weights[n].reshape(r2, c2), m_in[n].reshape(r2, c2), v_in[n].reshape(r2, c2),
                               "adamw_" + n)
        g_out[n], d_out[n], m_out[n], v_out[n] = (t.reshape(shape) for t in (g, dl, nm, nv))

    return (loss, grad_x[None], *[g_out[n] for n in names], *[d_out[n] for n in names],
            *[m_out[n] for n in names], *[v_out[n] for n in names])
```

```python
import functools

import jax
import jax.numpy as jnp
from jax import lax
from jax.experimental import pallas as pl
from jax.experimental.pallas import tpu as pltpu

f32 = jnp.float32
bf16 = jnp.bfloat16

N_DEV = 8
D = 1024
T_LAT = 2048
T_CTX = 256
T_ALL = T_CTX + T_LAT
R = 1280
N_BLK = 16
BLK = R // N_BLK
F = 4096
GRID_W = 64
RG_C = 8.0
EPS = 1e-6
POS_BASE = 10000.0
N_MOD = 6
ADA_SHARD = N_MOD * D // N_DEV

ADAM_LR = 0.001
ADAM_B1 = 0.9
ADAM_B2 = 0.999
ADAM_EPS = 1e-08
ADAM_WD = 0.01
ADAM_STEP = 10

VMEM_LIMIT_V7X = 56 * 1024 * 1024
HALO = 16
MESH = pl.DeviceIdType.MESH


def _cparams(*sem):
    return pltpu.CompilerParams(dimension_semantics=sem, vmem_limit_bytes=VMEM_LIMIT_V7X)


def _pick(n, cands):
    for c in cands:
        if n % c == 0:
            return c
    raise ValueError(f"no block size for {n}")


def _position():
    x, y, c = lax.axis_index("x"), lax.axis_index("y"), lax.axis_index("c")
    return x, y, c, 4 * x + 2 * y + c


def _peer(x, y, c, k):
    px = (1 - x) if (k >> 2) & 1 else x
    py = (1 - y) if (k >> 1) & 1 else y
    pc = (1 - c) if k & 1 else c
    return (px, py, pc), 4 * px + 2 * py + pc


def _exchange(arrs, name, scatter):
    n = len(arrs)

    def body(*refs):
        ins, outs = refs[:n], refs[n:2 * n]
        send_sems, recv_sems, local_sems = refs[2 * n:]
        x, y, c, me = _position()
        local = []
        for a in range(n):
            src = ins[a].at[me] if scatter else ins[a]
            cp = pltpu.make_async_copy(src, outs[a].at[me], local_sems.at[a])
            cp.start()
            local.append(cp)
        sends, recvs = [], []
        for a in range(n):
            for k in range(1, N_DEV):
                peer, peer_lin = _peer(x, y, c, k)
                src = ins[a].at[peer_lin] if scatter else ins[a]
                cp = pltpu.make_async_remote_copy(
                    src_ref=src, dst_ref=outs[a].at[me], send_sem=send_sems.at[a, k - 1],
                    recv_sem=recv_sems.at[a, k - 1], device_id=peer, device_id_type=MESH)
                cp.start()
                sends.append(cp)
                recvs.append(pltpu.make_async_remote_copy(
                    src_ref=src, dst_ref=outs[a].at[peer_lin], send_sem=send_sems.at[a, k - 1],
                    recv_sem=recv_sems.at[a, k - 1], device_id=peer, device_id_type=MESH))
        for cp in recvs:
            cp.wait_recv()
        for cp in sends:
            cp.wait_send()
        for cp in local:
            cp.wait()

    if scatter:
        out_shape = [jax.ShapeDtypeStruct(a.shape, a.dtype) for a in arrs]
    else:
        out_shape = [jax.ShapeDtypeStruct((N_DEV,) + a.shape, a.dtype) for a in arrs]
    any_spec = pl.BlockSpec(memory_space=pl.ANY)
    return pl.pallas_call(
        body, name=name, out_shape=out_shape,
        in_specs=[any_spec] * n, out_specs=[any_spec] * n,
        scratch_shapes=[pltpu.SemaphoreType.DMA((n, N_DEV - 1)), pltpu.SemaphoreType.DMA((n, N_DEV - 1)),
                        pltpu.SemaphoreType.DMA((n,))],
    )(*arrs)


def _all_gather(arrs, name):
    return _exchange(arrs, name, scatter=False)


def _all_to_all(arrs, name):
    return _exchange(arrs, name, scatter=True)


def _lin(p):
    return 4 * p[0] + 2 * p[1] + p[2]


def _staged_copy(src, dst, buf, in_sems, out_sems, rows, chunk):
    n = rows // chunk

    def rd(i):
        return pltpu.make_async_copy(src.at[pl.ds(i * chunk, chunk)], buf.at[i % 2], in_sems.at[i % 2])

    def wr(i):
        return pltpu.make_async_copy(buf.at[i % 2], dst.at[pl.ds(i * chunk, chunk)], out_sems.at[i % 2])

    rd(0).start()
    for i in range(n):
        if i + 1 < n:
            if i >= 1:
                wr(i - 1).wait()
            rd(i + 1).start()
        rd(i).wait()
        wr(i).start()
    for i in range(max(n - 2, 0), n):
        wr(i).wait()


def _all_gather_2level(shards, name):
    n = len(shards)
    chunks = [_pick(s.shape[0], (416, 512, 256, 160, 128, 64, 16)) for s in shards]

    def body(*refs):
        ins, outs = refs[:n], refs[n:2 * n]
        send_sems, recv_sems, in_sems, out_sems = refs[2 * n:2 * n + 4]
        bufs = refs[2 * n + 4:]
        x, y, c, me = _position()
        sib, xn, yn, dg = (x, y, 1 - c), (1 - x, y, c), (x, 1 - y, c), (1 - x, 1 - y, c)

        def cp(a, k, src, slot, to):
            return pltpu.make_async_remote_copy(src_ref=src, dst_ref=outs[a].at[slot], send_sem=send_sems.at[a, k],
                                                recv_sem=recv_sems.at[a, k], device_id=to, device_id_type=MESH)

        for a in range(n):
            for k, to in ((0, sib), (1, xn), (2, yn)):
                cp(a, k, ins[a], me, to).start()
        for a in range(n):
            cp(a, 1, ins[a], _lin(xn), xn).wait_recv()
            cp(a, 3, outs[a].at[_lin(xn)], _lin(xn), sib).start()

            @pl.when(c == 0)
            def _():
                cp(a, 5, outs[a].at[_lin(xn)], _lin(xn), yn).start()

            cp(a, 2, ins[a], _lin(yn), yn).wait_recv()
            cp(a, 4, outs[a].at[_lin(yn)], _lin(yn), sib).start()

            @pl.when(c == 1)
            def _():
                cp(a, 5, outs[a].at[_lin(yn)], _lin(yn), xn).start()

        for a in range(n):
            cp(a, 5, ins[a], _lin(dg), xn).wait_recv()
            cp(a, 6, outs[a].at[_lin(dg)], _lin(dg), sib).start()
        for a in range(n):
            _staged_copy(ins[a], outs[a].at[me], bufs[a], in_sems.at[a], out_sems.at[a], shards[a].shape[0], chunks[a])
        for a in range(n):
            for k, origin in ((0, sib), (3, (1 - x, y, 1 - c)), (4, (x, 1 - y, 1 - c)), (6, (1 - x, 1 - y, 1 - c))):
                cp(a, k, ins[a], _lin(origin), sib).wait_recv()
            for k in range(7):
                cp(a, k, ins[a], me, sib).wait_send()

    any_spec = pl.BlockSpec(memory_space=pl.ANY)
    return pl.pallas_call(
        body, name=name, out_shape=[jax.ShapeDtypeStruct((N_DEV,) + s.shape, s.dtype) for s in shards],
        in_specs=[any_spec] * n, out_specs=[any_spec] * n,
        scratch_shapes=[pltpu.SemaphoreType.DMA((n, 7)), pltpu.SemaphoreType.DMA((n, 7)),
                        pltpu.SemaphoreType.DMA((n, 2)), pltpu.SemaphoreType.DMA((n, 2))]
        + [pltpu.VMEM((2, ch, s.shape[1]), s.dtype) for ch, s in zip(chunks, shards)],
    )(*shards)


def _plane_pos(x, y, q):
    return ((1 - x) if q & 2 else x, (1 - y) if q & 1 else y)


def _scatter_call(body, name, ins, out_shape, sems_per_array):
    n = len(ins)
    any_spec = pl.BlockSpec(memory_space=pl.ANY)
    return pl.pallas_call(
        body, name=name, out_shape=out_shape, in_specs=[any_spec] * n, out_specs=[any_spec] * n,
        scratch_shapes=[pltpu.SemaphoreType.DMA((n, sems_per_array)), pltpu.SemaphoreType.DMA((n, sems_per_array))],
    )(*ins)


def _scatter_d2d(gs, name):
    n = len(gs)

    def body(*refs):
        g_refs, recv_refs, send_sems, recv_sems = refs[:n], refs[n:2 * n], refs[2 * n], refs[2 * n + 1]
        x, y, c, me = _position()
        sib = (x, y, 1 - c)
        sends = []
        for a in range(n):
            for q in range(4):
                px, py = _plane_pos(x, y, q)
                cp = pltpu.make_async_remote_copy(
                    src_ref=g_refs[a].at[_lin((px, py, 1 - c))], dst_ref=recv_refs[a].at[q],
                    send_sem=send_sems.at[a, q], recv_sem=recv_sems.at[a, q], device_id=sib, device_id_type=MESH)
                cp.start()
                sends.append(cp)
        for cp in sends:
            cp.wait_recv()
        for cp in sends:
            cp.wait_send()

    return _scatter_call(body, name, gs, [jax.ShapeDtypeStruct((4,) + g.shape[1:], g.dtype) for g in gs], 4)


def _scatter_ici_first(hs, name):
    n = len(hs)

    def body(*refs):
        h_refs, recv_refs, send_sems, recv_sems = refs[:n], refs[n:2 * n], refs[2 * n], refs[2 * n + 1]
        x, y, c, me = _position()
        xn, yn = (1 - x, y, c), (x, 1 - y, c)

        def cp(a, k, q, to):
            return pltpu.make_async_remote_copy(
                src_ref=h_refs[a].at[q], dst_ref=recv_refs[a].at[k], send_sem=send_sems.at[a, k],
                recv_sem=recv_sems.at[a, k], device_id=to, device_id_type=MESH)

        @pl.when(c == 0)
        def _():
            for a in range(n):
                cp(a, 0, 2, xn).start()
                cp(a, 1, 3, xn).start()

        @pl.when(c == 1)
        def _():
            for a in range(n):
                cp(a, 0, 1, yn).start()
                cp(a, 1, 3, yn).start()

        for a in range(n):
            for k in range(2):
                cp(a, k, 0, xn).wait_recv()
        for a in range(n):
            for k in range(2):
                cp(a, k, 0, xn).wait_send()

    return _scatter_call(body, name, hs, [jax.ShapeDtypeStruct((2,) + h.shape[1:], h.dtype) for h in hs], 2)


def _scatter_ici_second(k1s, name):
    n = len(k1s)

    def body(*refs):
        k_refs, recv_refs, send_sems, recv_sems = refs[:n], refs[n:2 * n], refs[2 * n], refs[2 * n + 1]
        x, y, c, me = _position()
        xn, yn = (1 - x, y, c), (x, 1 - y, c)

        def cp(a, to):
            return pltpu.make_async_remote_copy(src_ref=k_refs[a], dst_ref=recv_refs[a], send_sem=send_sems.at[a, 0],
                                                recv_sem=recv_sems.at[a, 0], device_id=to, device_id_type=MESH)

        @pl.when(c == 0)
        def _():
            for a in range(n):
                cp(a, yn).start()

        @pl.when(c == 1)
        def _():
            for a in range(n):
                cp(a, xn).start()

        for a in range(n):
            cp(a, xn).wait_recv()
        for a in range(n):
            cp(a, xn).wait_send()

    return _scatter_call(body, name, k1s, [jax.ShapeDtypeStruct(k.shape, k.dtype) for k in k1s], 1)


def _add_blocks(a, a_idx, b, b_idx, out_dtype, name):
    rows, cols = a.shape[1:]
    n = a_idx.shape[0]
    tm = _pick(rows, (512, 256, 160, 128, 32, 16))

    def body(ia_ref, ib_ref, a_ref, b_ref, o_ref):
        o_ref[...] = (a_ref[...].astype(f32) + b_ref[...].astype(f32)).astype(o_ref.dtype)

    grid_spec = pltpu.PrefetchScalarGridSpec(
        num_scalar_prefetch=2, grid=(n, rows // tm),
        in_specs=[pl.BlockSpec((None, tm, cols), lambda j, i, ia, ib: (ia[j], i, 0)),
                  pl.BlockSpec((None, tm, cols), lambda j, i, ia, ib: (ib[j], i, 0))],
        out_specs=pl.BlockSpec((None, tm, cols), lambda j, i, ia, ib: (j, i, 0)))
    return pl.pallas_call(body, name=name, out_shape=jax.ShapeDtypeStruct((n, rows, cols), out_dtype),
                          grid_spec=grid_spec, compiler_params=_cparams("parallel", "parallel"))(a_idx, b_idx, a, b)


def _reduce_scatter(gs, names, tag):
    x, y, c, me = _position()
    i32 = lambda *v: jnp.stack([jnp.asarray(t, jnp.int32) for t in v])
    recvs = _scatter_d2d(gs, tag + "_d2d")
    own_idx = i32(*[_lin(_plane_pos(x, y, q) + (c,)) for q in range(4)])
    hs = [_add_blocks(g, own_idx, r, i32(0, 1, 2, 3), bf16, f"{tag}_add_chip_{nm}")
          for g, r, nm in zip(gs, recvs, names)]
    recv2s = _scatter_ici_first(hs, tag + "_ici_first")
    k1s = [_add_blocks(h, i32(1 + c), r2, i32(1), bf16, f"{tag}_add_onward_{nm}")[0]
           for h, r2, nm in zip(hs, recv2s, names)]
    lasts = _scatter_ici_second(k1s, tag + "_ici_second")
    return [[(h, 1), (r2, 1), (last[None], 1)] for h, r2, last in zip(hs, recv2s, lasts)]


def _mm(a, b, name, ta=False, tb=False, out_dtype=f32):
    if ta:
        k_dim, m_dim = a.shape
    else:
        m_dim, k_dim = a.shape
    if tb:
        n_dim, k2 = b.shape
    else:
        k2, n_dim = b.shape
    assert k_dim == k2, (a.shape, b.shape)
    assert a.dtype == bf16 and b.dtype == bf16
    bm = _pick(m_dim, (512, 768, 640, 256, 128))
    bn = _pick(n_dim, (512, 640, 256, 128))
    bk = _pick(k_dim, (1024, 1280, 768, 512))
    nk = k_dim // bk
    a_spec = (pl.BlockSpec((bk, bm), lambda i, j, k: (k, i)) if ta
              else pl.BlockSpec((bm, bk), lambda i, j, k: (i, k)))
    b_spec = (pl.BlockSpec((bn, bk), lambda i, j, k: (j, k)) if tb
              else pl.BlockSpec((bk, bn), lambda i, j, k: (k, j)))
    dims = (((0 if ta else 1,), (1 if tb else 0,)), ((), ()))

    def body_single(a_ref, b_ref, o_ref):
        o_ref[...] = lax.dot_general(a_ref[...], b_ref[...], dims, preferred_element_type=f32).astype(o_ref.dtype)

    def body(a_ref, b_ref, o_ref, acc_ref):
        k = pl.program_id(2)

        @pl.when(k == 0)
        def _():
            acc_ref[...] = jnp.zeros_like(acc_ref)

        acc_ref[...] += lax.dot_general(a_ref[...], b_ref[...], dims, preferred_element_type=f32)

        @pl.when(k == nk - 1)
        def _():
            o_ref[...] = acc_ref[...].astype(o_ref.dtype)

    return pl.pallas_call(
        body_single if nk == 1 else body, name=name, out_shape=jax.ShapeDtypeStruct((m_dim, n_dim), out_dtype),
        grid=(m_dim // bm, n_dim // bn, nk), in_specs=[a_spec, b_spec],
        out_specs=pl.BlockSpec((bm, bn), lambda i, j, k: (i, j)),
        scratch_shapes=[] if nk == 1 else [pltpu.VMEM((bm, bn), f32)],
        compiler_params=_cparams("parallel", "parallel", "arbitrary"),
    )(a, b)


def _rin(arr, width=None, cb=0, roff=0):
    return (arr, arr.shape[1] if width is None else width, cb, roff)


def _rowcall(fn, name, rows, tm, row_ins, par_ins, row_outs, acc_outs=()):
    nr, npar, nro = len(row_ins), len(par_ins), len(row_outs)
    in_specs, args = [], []
    for arr, width, cb, roff in row_ins:
        if roff >= 0:
            imap = lambda i, cb=cb, roff=roff: (i + roff, cb)
        else:
            imap = lambda i, cb=cb, roff=roff: (jnp.maximum(i + roff, 0), cb)
        in_specs.append(pl.BlockSpec((tm, width), imap))
        args.append(arr)
    for p in par_ins:
        in_specs.append(pl.BlockSpec(p.shape, lambda i: (0, 0)))
        args.append(p)
    out_shape, out_specs = [], []
    for width, dt in row_outs:
        out_shape.append(jax.ShapeDtypeStruct((rows, width), dt))
        out_specs.append(pl.BlockSpec((tm, width), lambda i: (i, 0)))
    for p, width in acc_outs:
        out_shape.append(jax.ShapeDtypeStruct((p, width), f32))
        out_specs.append(pl.BlockSpec((p, width), lambda i: (0, 0)))

    def body(*refs):
        i = pl.program_id(0)
        res = fn(i, *[r[...] for r in refs[:nr + npar]])
        outs = refs[nr + npar:]
        for o, v in zip(outs[:nro], res[:nro]):
            o[...] = v.astype(o.dtype)
        if acc_outs:
            @pl.when(i == 0)
            def _():
                for o in outs[nro:]:
                    o[...] = jnp.zeros_like(o)

            for o, v in zip(outs[nro:], res[nro:]):
                o[...] += v

    return pl.pallas_call(
        body, name=name, out_shape=out_shape, grid=(rows // tm,), in_specs=in_specs, out_specs=out_specs,
        compiler_params=_cparams("arbitrary"),
    )(*args)


def _rms(x, g):
    return x * lax.rsqrt(jnp.mean(x * x, axis=-1, keepdims=True) + EPS) * g


def _normmod(x, g, sc, sh):
    return _rms(x, g) * (1.0 + sc) + sh


def _rows2(v0, v1):
    rid = lax.broadcasted_iota(jnp.int32, (2, v0.shape[1]), 0)
    return jnp.where(rid == 0, v0, v1)


def _gelu(x):
    return 0.5 * x * (1.0 + jnp.tanh(0.7978845608028654 * (x + 0.044715 * (x * x * x))))


def _sigmoid(x):
    return 1.0 / (1.0 + jnp.exp(-x))


def _coeff(pre_a, pre_x, u, ba, bx, lam):
    r = _sigmoid(pre_a + ba)
    ig = _sigmoid(pre_x + bx)
    nl = -lam
    sp = jnp.maximum(nl, 0.0) + jnp.log(1.0 + jnp.exp(-jnp.abs(nl)))
    la = -RG_C * r * sp
    a = jnp.exp(la)
    one_minus_a2 = -jnp.tanh(la) * (a * a + 1.0)
    return a, jnp.sqrt(one_minus_a2) * (ig * u)


SCAN_CHUNK = 256


def _scan_call(a, v, chunk_of, reverse, name, backward):
    rows, width = a.shape
    n_out = 1 if backward else 2
    nt = SCAN_CHUNK // 8

    def body(a_ref, v_ref, *rest):
        outs, state_ref = rest[:-1], rest[-1]

        @pl.when(pl.program_id(0) == 0)
        def _():
            state_ref[...] = jnp.zeros_like(state_ref)

        rid = lax.broadcasted_iota(jnp.int32, (8, width), 0)

        def tile(j, st):
            t0 = pl.multiple_of((nt - 1 - j if reverse else j) * 8, 8)
            at = a_ref[pl.ds(t0, 8), :]
            vt = v_ref[pl.ds(t0, 8), :]
            out = jnp.zeros((8, width), f32)
            prev = jnp.zeros((8, width), f32)
            for i in (range(7, -1, -1) if reverse else range(8)):
                if backward:
                    g = vt[i:i + 1] + st
                    st = at[i:i + 1] * g
                    out = jnp.where(rid == i, g, out)
                else:
                    prev = jnp.where(rid == i, st, prev)
                    st = at[i:i + 1] * st + vt[i:i + 1]
                    out = jnp.where(rid == i, st, out)
            outs[0][pl.ds(t0, 8), :] = out
            if not backward:
                outs[1][pl.ds(t0, 8), :] = prev
            return st

        state_ref[0:1, :] = lax.fori_loop(0, nt, tile, state_ref[0:1, :])

    spec = pl.BlockSpec((SCAN_CHUNK, width), lambda t: (chunk_of(t), 0))
    return pl.pallas_call(
        body, name=name, out_shape=[jax.ShapeDtypeStruct((rows, width), f32)] * n_out,
        grid=(rows // SCAN_CHUNK,), in_specs=[spec, spec], out_specs=[spec] * n_out,
        scratch_shapes=[pltpu.VMEM((8, width), f32)],
        compiler_params=_cparams("arbitrary"),
    )(a, v)


CONV_CHUNK = 256


def _fill_padded(pad_ref, src_ref, start, n):
    cb = pad_ref.shape[1]
    pad_ref[pl.ds(0, HALO), :] = jnp.zeros((HALO, cb), f32)
    pad_ref[pl.ds(HALO, n), :] = src_ref[pl.ds(start, n), :].astype(f32)
    pad_ref[pl.ds(HALO + n, HALO), :] = jnp.zeros((HALO, cb), f32)


def _dwconv_fwd(x, x_cb0, w, b, taps, pad_left, segments, cb, name, emit_bf16):
    rows = x.shape[0]
    width = w.shape[1]

    def body(x_ref, w_ref, b_ref, *rest):
        outs, xp = rest[:-1], rest[-1]
        for start, n in segments:
            _fill_padded(xp, x_ref, start, n)
            for c0 in range(0, n, CONV_CHUNK):
                acc = jnp.zeros((CONV_CHUNK, cb), f32) + b_ref[...]
                for k in range(taps):
                    acc = acc + w_ref[k:k + 1, :] * xp[pl.ds(HALO + c0 + k - pad_left, CONV_CHUNK), :]
                for o in outs:
                    o[pl.ds(start + c0, CONV_CHUNK), :] = acc.astype(o.dtype)

    out_dtypes = [f32, bf16] if emit_bf16 else [f32]
    return pl.pallas_call(
        body, name=name, out_shape=[jax.ShapeDtypeStruct((rows, width), dt) for dt in out_dtypes],
        grid=(width // cb,),
        in_specs=[pl.BlockSpec((rows, cb), lambda j: (0, j + x_cb0)), pl.BlockSpec((taps, cb), lambda j: (0, j)),
                  pl.BlockSpec((1, cb), lambda j: (0, j))],
        out_specs=[pl.BlockSpec((rows, cb), lambda j: (0, j))] * len(out_dtypes),
        scratch_shapes=[pltpu.VMEM((rows + 2 * HALO, cb), f32)],
        compiler_params=_cparams("parallel"),
    )(x, w, b)


def _dwconv_bwd(douts, x, x_cb0, w, taps, pad_left, segments, cb, name, dx_dtype):
    rows = x.shape[0]
    width = w.shape[1]
    nd = len(douts)

    def body(*refs):
        d_refs, x_ref, w_ref = refs[:nd], refs[nd], refs[nd + 1]
        dx_ref, dw_ref, db_ref, xp, dp, dsum = refs[nd + 2:]
        dw_ref[...] = jnp.zeros_like(dw_ref)
        db_ref[...] = jnp.zeros_like(db_ref)
        if nd > 1:
            total = d_refs[0][...]
            for r in d_refs[1:]:
                total = total + r[...]
            dsum[...] = total
            d_ref = dsum
        else:
            d_ref = d_refs[0]
        for start, n in segments:
            _fill_padded(xp, x_ref, start, n)
            _fill_padded(dp, d_ref, start, n)
            for c0 in range(0, n, CONV_CHUNK):
                dchunk = dp[pl.ds(HALO + c0, CONV_CHUNK), :]
                db_ref[...] += jnp.sum(dchunk, axis=0, keepdims=True)
                acc = jnp.zeros((CONV_CHUNK, cb), f32)
                for k in range(taps):
                    acc = acc + w_ref[k:k + 1, :] * dp[pl.ds(HALO + c0 + pad_left - k, CONV_CHUNK), :]
                    xs = xp[pl.ds(HALO + c0 + k - pad_left, CONV_CHUNK), :]
                    dw_ref[k:k + 1, :] += jnp.sum(dchunk * xs, axis=0, keepdims=True)
                dx_ref[pl.ds(start + c0, CONV_CHUNK), :] = acc.astype(dx_ref.dtype)

    dspec = pl.BlockSpec((rows, cb), lambda j: (0, j))
    return pl.pallas_call(
        body, name=name,
        out_shape=[jax.ShapeDtypeStruct((rows, width), dx_dtype), jax.ShapeDtypeStruct((taps, width), f32),
                   jax.ShapeDtypeStruct((1, width), f32)],
        grid=(width // cb,),
        in_specs=[dspec] * nd + [pl.BlockSpec((rows, cb), lambda j: (0, j + x_cb0)),
                                 pl.BlockSpec((taps, cb), lambda j: (0, j))],
        out_specs=[dspec, pl.BlockSpec((taps, cb), lambda j: (0, j)), pl.BlockSpec((1, cb), lambda j: (0, j))],
        scratch_shapes=[pltpu.VMEM((rows + 2 * HALO, cb), f32), pltpu.VMEM((rows + 2 * HALO, cb), f32),
                        pltpu.VMEM((rows, cb), f32)],
        compiler_params=_cparams("parallel"),
    )(*douts, x, w)


def _ada_forward(c16, w_ada, b_loc):
    def body(c_ref, w_ref, b_ref, o_ref):
        cv = c_ref[...]
        s = (cv * _sigmoid(cv)).astype(bf16)
        o_ref[0] = jnp.dot(s, w_ref[0].astype(bf16), preferred_element_type=f32) + b_ref[0]

    return pl.pallas_call(
        body, name="ada_forward", out_shape=jax.ShapeDtypeStruct((2, 16, ADA_SHARD), f32), grid=(2,),
        in_specs=[pl.BlockSpec((16, D), lambda l: (0, 0)), pl.BlockSpec((1, D, ADA_SHARD), lambda l: (l, 0, 0)),
                  pl.BlockSpec((1, 1, ADA_SHARD), lambda l: (l, 0, 0))],
        out_specs=pl.BlockSpec((1, 16, ADA_SHARD), lambda l: (l, 0, 0)),
        compiler_params=_cparams("parallel"),
    )(c16, w_ada, b_loc)


def _ada_backward(c16, g16, w_ada):
    def body(c_ref, g_ref, w_ref, dw_ref, ds_ref):
        cv = c_ref[...]
        s = (cv * _sigmoid(cv)).astype(bf16)
        g = g_ref[0].astype(bf16)
        dw_ref[0] = lax.dot_general(s, g, (((0,), (0,)), ((), ())), preferred_element_type=f32)
        ds = lax.dot_general(g, w_ref[0].astype(bf16), (((1,), (1,)), ((), ())), preferred_element_type=f32)
        cc = cv[8:9]
        sg = _sigmoid(cc)
        dsilu = sg * (1.0 + cc * (1.0 - sg))
        ds_ref[0] = jnp.zeros((8, D), f32) + jnp.sum(ds[8:16], axis=0, keepdims=True) * dsilu

    return pl.pallas_call(
        body, name="ada_backward",
        out_shape=[jax.ShapeDtypeStruct((2, D, ADA_SHARD), f32), jax.ShapeDtypeStruct((2, 8, D), f32)], grid=(2,),
        in_specs=[pl.BlockSpec((16, D), lambda l: (0, 0)), pl.BlockSpec((1, 16, ADA_SHARD), lambda l: (l, 0, 0)),
                  pl.BlockSpec((1, D, ADA_SHARD), lambda l: (l, 0, 0))],
        out_specs=[pl.BlockSpec((1, D, ADA_SHARD), lambda l: (l, 0, 0)), pl.BlockSpec((1, 8, D), lambda l: (l, 0, 0))],
        compiler_params=_cparams("parallel"),
    )(c16, g16, w_ada)


def _adamw(pieces, w, m, v, name):
    rows, cols = w.shape
    n_arr = len(pieces)
    counts = [cnt for _, cnt in pieces]
    pieces = [p for p, _ in pieces]
    tm = 256 if (rows % 256 == 0 and rows > 256) else rows

    def body(*refs):
        p_refs = refs[:n_arr]
        w_ref, m_ref, v_ref, g_ref, d_ref, nm_ref, nv_ref = refs[n_arr:]
        g = None
        for p_ref in p_refs:
            for j in range(p_ref.shape[0]):
                term = p_ref[j].astype(f32)
                g = term if g is None else g + term
        m2 = ADAM_B1 * m_ref[...] + (1.0 - ADAM_B1) * g
        v2 = ADAM_B2 * v_ref[...] + (1.0 - ADAM_B2) * (g * g)
        m_hat = m2 / (1.0 - ADAM_B1 ** ADAM_STEP)
        v_hat = v2 / (1.0 - ADAM_B2 ** ADAM_STEP)
        g_ref[...] = g
        d_ref[...] = -ADAM_LR * (m_hat / (jnp.sqrt(v_hat) + ADAM_EPS) + ADAM_WD * w_ref[...])
        nm_ref[...] = m2
        nv_ref[...] = v2

    spec = pl.BlockSpec((tm, cols), lambda i: (i, 0))
    return pl.pallas_call(
        body, name=name, out_shape=[jax.ShapeDtypeStruct((rows, cols), f32)] * 4, grid=(rows // tm,),
        in_specs=[pl.BlockSpec((cnt, tm, cols), lambda i: (0, i, 0)) for cnt in counts] + [spec, spec, spec],
        out_specs=[spec] * 4, compiler_params=_cparams("parallel"),
    )(*pieces, w, m, v)


MLP_TM = 256
FB = F // N_DEV


def _stack_rows(vals, n):
    cols = vals[0].shape[1]
    rid = lax.broadcasted_iota(jnp.int32, (n, cols), 0)
    out = jnp.zeros((n, cols), f32)
    for k, v in enumerate(vals):
        out = jnp.where(rid == k, v, out)
    return out


N_MLP_PARAMS = 9


class _ParamRows:
    def __init__(self, ref):
        self.ref = ref

    def __getitem__(self, sl):
        return self.ref[8 * sl.start:8 * sl.start + 1, :]


def _resident(shape, imap):
    return pl.BlockSpec(shape, imap, pipeline_mode=pl.Buffered(1))


def _mlp_forward(xa, xa_roff, out_prev, par, w_in, w_out, layer, name):
    def body(xa_ref, op_ref, par_ref, win_ref, wout_ref, x1_ref, h_ref, r_ref, mo_ref, x2_ref, hn_ref):
        p = _ParamRows(par_ref)
        x1 = xa_ref[...] + p[0:1] * (op_ref[...] + p[1:2])
        h = _normmod(x1, p[2:3], p[3:4], p[4:5]).astype(bf16)
        x1_ref[...] = x1
        h_ref[...] = h
        mo = jnp.zeros((MLP_TM, D), f32)
        for j in range(N_DEV):
            r = jnp.maximum(jnp.dot(h, win_ref[j], preferred_element_type=f32), 0.0)
            r_ref[:, j * FB:(j + 1) * FB] = r.astype(bf16)
            mo = mo + jnp.dot((r * r).astype(bf16), wout_ref[j], preferred_element_type=f32)
        mo_ref[...] = mo.astype(bf16)
        x2 = x1 + p[5:6] * mo
        x2_ref[...] = x2
        hn_ref[...] = _normmod(x2, p[6:7], p[7:8], p[8:9]).astype(bf16)

    row = lambda width: pl.BlockSpec((MLP_TM, width), lambda i: (i, 0))
    return pl.pallas_call(
        body, name=name, grid=(T_LAT // MLP_TM,),
        out_shape=[jax.ShapeDtypeStruct((T_LAT, D), f32), jax.ShapeDtypeStruct((T_LAT, D), bf16),
                   jax.ShapeDtypeStruct((T_LAT, F), bf16), jax.ShapeDtypeStruct((T_LAT, D), bf16),
                   jax.ShapeDtypeStruct((T_LAT, D), f32), jax.ShapeDtypeStruct((T_LAT, D), bf16)],
        in_specs=[pl.BlockSpec((MLP_TM, D), lambda i: (i + xa_roff, 0)), row(D), pl.BlockSpec((8 * N_MLP_PARAMS, D), lambda i: (0, 0)),
                  _resident((N_DEV, None, D, FB), lambda i: (0, layer, 0, 0)),
                  _resident((N_DEV, None, FB, D), lambda i: (0, layer, 0, 0))],
        out_specs=[row(D), row(D), row(F), row(D), row(D), row(D)],
        compiler_params=_cparams("parallel"),
    )(xa, out_prev, par, w_in, w_out)


def _mlp_backward(dx2, x1, r, mo, out_prev, par, w_in, w_out, layer, name):
    nt = (((1,), (1,)), ((), ()))

    def body(dx2_ref, x1_ref, r_ref, mo_ref, op_ref, par_ref, win_ref, wout_ref, dx1_ref, dop_ref, dmo_ref, dhid_ref,
             acc_ref):
        p = _ParamRows(par_ref)
        dx2v = dx2_ref[...]
        dmo = (p[5:6] * dx2v).astype(bf16)
        dmo_ref[...] = dmo
        dh = jnp.zeros((MLP_TM, D), f32)
        mo = mo_ref[...].astype(f32)
        for j in range(N_DEV):
            rf = r_ref[:, j * FB:(j + 1) * FB].astype(f32)
            dact = lax.dot_general(dmo, wout_ref[j], nt, preferred_element_type=f32)
            dhid = (dact * (2.0 * rf)).astype(bf16)
            dhid_ref[:, j * FB:(j + 1) * FB] = dhid
            dh = dh + lax.dot_general(dhid, win_ref[j], nt, preferred_element_type=f32)
        x1 = x1_ref[...]
        _, vjp = jax.vjp(_normmod, x1, p[2:3], p[3:4], p[4:5])
        dx, dng, dsc, dsh = vjp(dh)
        dx1 = dx2v + dx
        dx1_ref[...] = dx1
        dop_ref[...] = (p[0:1] * dx1).astype(bf16)
        sums = _stack_rows([jnp.sum(dx1 * (op_ref[...] + p[1:2]), axis=0, keepdims=True),
                            p[0:1] * jnp.sum(dx1, axis=0, keepdims=True), dng, dsc, dsh,
                            jnp.sum(dx2v * mo, axis=0, keepdims=True)], 8)

        @pl.when(pl.program_id(0) == 0)
        def _():
            acc_ref[...] = jnp.zeros_like(acc_ref)

        acc_ref[...] += sums

    row = lambda width: pl.BlockSpec((MLP_TM, width), lambda i: (i, 0))
    return pl.pallas_call(
        body, name=name, grid=(T_LAT // MLP_TM,),
        out_shape=[jax.ShapeDtypeStruct((T_LAT, D), f32), jax.ShapeDtypeStruct((T_LAT, D), bf16),
                   jax.ShapeDtypeStruct((T_LAT, D), bf16), jax.ShapeDtypeStruct((T_LAT, F), bf16),
                   jax.ShapeDtypeStruct((8, D), f32)],
        in_specs=[row(D), row(D), row(F), row(D), row(D), pl.BlockSpec((8 * N_MLP_PARAMS, D), lambda i: (0, 0)),
                  _resident((N_DEV, None, D, FB), lambda i: (0, layer, 0, 0)),
                  _resident((N_DEV, None, FB, D), lambda i: (0, layer, 0, 0))],
        out_specs=[row(D), row(D), row(D), row(F), pl.BlockSpec((8, D), lambda i: (0, 0))],
        compiler_params=_cparams("arbitrary"),
    )(dx2, x1, r, mo, out_prev, par, w_in, w_out)


def _mlp_weight_grads(h, dhid, r, dmo, layer, other, tag):
    tn = (((0,), (0,)), ((), ()))

    def body_in(h_ref, dhid_ref, *rest):
        rest[-1][...] = lax.dot_general(h_ref[...], dhid_ref[...], tn, preferred_element_type=f32).astype(bf16)

    def body_out(r_ref, dmo_ref, *rest):
        rf = r_ref[...].astype(f32)
        rest[-1][...] = lax.dot_general((rf * rf).astype(bf16), dmo_ref[...], tn,
                                        preferred_element_type=f32).astype(bf16)

    def call(body, name, operands, specs, block, prev):
        extra = [] if prev is None else [prev]
        return pl.pallas_call(
            body, name=name, grid=(N_DEV,), out_shape=jax.ShapeDtypeStruct((N_DEV, 2) + block, bf16),
            in_specs=specs + [pl.BlockSpec(memory_space=pl.ANY)] * len(extra),
            out_specs=pl.BlockSpec((None, None) + block, lambda j: (j, layer, 0, 0)),
            input_output_aliases={} if prev is None else {2: 0}, compiler_params=_cparams("parallel"),
        )(*operands, *extra)

    dw_in = call(body_in, tag + "_mlp_in_dw", [h, dhid],
                 [_resident((T_LAT, D), lambda j: (0, 0)), pl.BlockSpec((T_LAT, FB), lambda j: (0, j))], (D, FB),
                 None if other is None else other[0])
    dw_out = call(body_out, tag + "_mlp_out_dw", [r, dmo],
                  [pl.BlockSpec((T_LAT, FB), lambda j: (0, j)), _resident((T_LAT, D), lambda j: (0, 0))], (FB, D),
                  None if other is None else other[1])
    return dw_in, dw_out


def _pos_embed():
    n_rows = T_LAT // GRID_W
    q = D // 4
    omega = 1.0 / (POS_BASE ** (jnp.arange(q, dtype=f32) / q))
    er = jnp.arange(n_rows, dtype=jnp.int32).astype(f32)[:, None] * omega[None, :]
    ec = jnp.arange(GRID_W, dtype=jnp.int32).astype(f32)[:, None] * omega[None, :]
    by_row = jnp.concatenate([jnp.sin(er), jnp.cos(er)], axis=-1)[:, None, :]
    by_col = jnp.concatenate([jnp.sin(ec), jnp.cos(ec)], axis=-1)[None, :, :]
    full = jnp.concatenate([jnp.broadcast_to(by_row, (n_rows, GRID_W, D // 2)),
                            jnp.broadcast_to(by_col, (n_rows, GRID_W, D // 2))], axis=-1)
    return full.reshape(T_LAT, D)


HALF = R // 2
BLK_PER_HALF = N_BLK // 2
N_PARTS = 4


def _gate_matrix(w_a, w_x):
    eye = jnp.eye(BLK_PER_HALF, dtype=bf16)
    cols = []
    for h in range(2):
        for d in range(2):
            for w in (w_a, w_x):
                blocks = w[d, BLK_PER_HALF * h:BLK_PER_HALF * (h + 1)].astype(bf16)
                cols.append(jnp.einsum("hij,hg->higj", blocks, eye).reshape(HALF, HALF))
    return jnp.concatenate(cols, axis=1)


def _gate_blocks(dwg, part):
    out = []
    for h in range(2):
        blk = dwg[:, (N_PARTS * h + part) * HALF:(N_PARTS * h + part + 1) * HALF]
        blk = blk.reshape(BLK_PER_HALF, BLK, BLK_PER_HALF, BLK)
        out.append(jnp.moveaxis(jnp.diagonal(blk, axis1=0, axis2=2), -1, 0))
    return jnp.concatenate(out, axis=0)


def _gate_part(pre, part):
    return jnp.concatenate([pre[:, (N_PARTS * h + part) * HALF:(N_PARTS * h + part + 1) * HALF] for h in range(2)],
                           axis=1)


def _gate_unpart(parts):
    return jnp.concatenate([parts[p][:, h * HALF:(h + 1) * HALF] for h in range(2) for p in range(N_PARTS)], axis=1)


GATE_BM = 768


def _gates_fwd(u, wg):
    rows = u.shape[0]

    def body(u_ref, w_ref, o_ref):
        o_ref[...] = jnp.dot(u_ref[...], w_ref[...], preferred_element_type=f32)

    return pl.pallas_call(
        body, name="l0_gates", grid=(rows // GATE_BM, 2 * N_PARTS),
        out_shape=jax.ShapeDtypeStruct((rows, 2 * N_PARTS * HALF), f32),
        in_specs=[pl.BlockSpec((GATE_BM, HALF), lambda i, j: (i, j // N_PARTS)),
                  pl.BlockSpec((HALF, HALF), lambda i, j: (0, j))],
        out_specs=pl.BlockSpec((GATE_BM, HALF), lambda i, j: (i, j)),
        compiler_params=_cparams("parallel", "parallel"),
    )(u, wg)


def _gates_dx(dpre, wg):
    rows = dpre.shape[0]

    def body(d_ref, w_ref, o_ref, acc_ref):
        p = pl.program_id(2)

        @pl.when(p == 0)
        def _():
            acc_ref[...] = jnp.zeros_like(acc_ref)

        acc_ref[...] += lax.dot_general(d_ref[...], w_ref[...], (((1,), (1,)), ((), ())), preferred_element_type=f32)

        @pl.when(p == N_PARTS - 1)
        def _():
            o_ref[...] = acc_ref[...]

    return pl.pallas_call(
        body, name="l0_gates_dx", grid=(rows // GATE_BM, 2, N_PARTS), out_shape=jax.ShapeDtypeStruct((rows, R), f32),
        in_specs=[pl.BlockSpec((GATE_BM, HALF), lambda i, h, p: (i, N_PARTS * h + p)),
                  pl.BlockSpec((HALF, HALF), lambda i, h, p: (0, N_PARTS * h + p))],
        out_specs=pl.BlockSpec((GATE_BM, HALF), lambda i, h, p: (i, h)),
        scratch_shapes=[pltpu.VMEM((GATE_BM, HALF), f32)],
        compiler_params=_cparams("parallel", "parallel", "arbitrary"),
    )(dpre, wg)


def _gates_dw(u, dpre):
    rows = u.shape[0]

    def body(u_ref, d_ref, o_ref):
        o_ref[...] = lax.dot_general(u_ref[...], d_ref[...], (((0,), (0,)), ((), ())), preferred_element_type=f32)

    return pl.pallas_call(
        body, name="l0_gates_dw", grid=(2 * N_PARTS,), out_shape=jax.ShapeDtypeStruct((HALF, 2 * N_PARTS * HALF), f32),
        in_specs=[pl.BlockSpec((rows, HALF), lambda j: (0, j // N_PARTS)), pl.BlockSpec((rows, HALF), lambda j: (0, j))],
        out_specs=pl.BlockSpec((HALF, HALF), lambda j: (0, j)), compiler_params=_cparams("parallel"),
    )(u, dpre)


N_SCAN_CHUNKS = T_ALL // SCAN_CHUNK
SCAN_FWD = lambda t: t
SCAN_FWD_BWD = lambda t: N_SCAN_CHUNKS - 1 - t
SCAN_REV = lambda t: jnp.where(t == 0, 0, N_SCAN_CHUNKS - t)
SCAN_REV_BWD = lambda t: jnp.where(t == N_SCAN_CHUNKS - 1, 0, t + 1)
CONV_SEGMENTS = ((0, T_CTX), (T_CTX, T_LAT))
TM = 128
N_CTX_TILES = T_CTX // TM


def _local_step(x, ctx, target, mods, cmod, wts):
    sh1, sc1, g1, sh2, sc2, g2 = [[mods[l, i][None] for l in range(2)] for i in range(N_MOD)]
    ng = wts["norm_g"]
    xcat = jnp.concatenate([ctx, x], axis=0)
    poscat = jnp.concatenate([jnp.zeros((T_CTX, D), f32), _pos_embed()], axis=0)
    scp = jnp.concatenate([cmod[1][None], sc1[0]], axis=0)
    shp = jnp.concatenate([cmod[0][None], sh1[0]], axis=0)

    def blend(i, p):
        sel = jnp.where(i < N_CTX_TILES, 1.0, 0.0)
        return sel * p[0:1] + (1.0 - sel) * p[1:2]

    def f_pre0(i, xc, pos, g, scp_, shp_):
        x0 = xc + pos
        return x0, _normmod(x0, g, blend(i, scp_), blend(i, shp_))

    x0cat, h0 = _rowcall(f_pre0, "l0_prenorm", T_ALL, TM, [_rin(xcat), _rin(poscat)], [ng[0, 0][None], scp, shp],
                         [(D, f32), (D, bf16)])
    gr = _mm(h0, wts["rec_w_in"], "l0_in_proj")
    u, ub = _dwconv_fwd(gr, R // 256, wts["rec_conv_w"], wts["rec_conv_b"], 4, 1, CONV_SEGMENTS, 256,
                        "l0_conv", True)
    pre = _gates_fwd(ub, wts["gates"])

    def f_coeff(i, pre_, u_, ba, bx, lam):
        outs = []
        for d in range(2):
            a, b = _coeff(_gate_part(pre_, 2 * d), _gate_part(pre_, 2 * d + 1), u_,
                          ba[d:d + 1], bx[d:d + 1], lam[d:d + 1])
            outs += [a, b]
        return tuple(outs)

    a0, b0, a1, b1 = _rowcall(f_coeff, "l0_coeff", T_ALL, TM, [_rin(pre), _rin(u)],
                              [wts["rec_b_a"], wts["rec_b_x"], wts["rec_lambda"]], [(R, f32)] * 4)
    y0, yp0 = _scan_call(a0, b0, SCAN_FWD, False, "l0_scan_fwd", False)
    y1, yp1 = _scan_call(a1, b1, SCAN_REV, True, "l0_scan_rev", False)

    def f_gate(i, gp, y0_, y1_):
        return (_gelu(gp) * (y0_ + y1_),)

    (zb,) = _rowcall(f_gate, "l0_gate", T_LAT, TM,
                     [_rin(gr, R, 0, N_CTX_TILES), _rin(y0, None, 0, N_CTX_TILES), _rin(y1, None, 0, N_CTX_TILES)],
                     [], [(R, bf16)])
    out0 = _mm(zb, wts["rec_w_out"], "l0_out_proj")

    zero_d = jnp.zeros((1, D), f32)

    def mlp_params(rows):
        rows = rows + [zero_d] * (N_MLP_PARAMS - len(rows))
        return jnp.concatenate([jnp.broadcast_to(r, (8, D)) for r in rows], axis=0)

    par0 = mlp_params([g1[0], zero_d, ng[0, 1][None], sc2[0], sh2[0], g2[0], ng[1, 0][None], sc1[1], sh1[1]])
    x1, h1, r0, mo0, x2, h2 = _mlp_forward(x0cat, T_CTX // MLP_TM, out0, par0, wts["mlp_w_in"], wts["mlp_w_out"], 0,
                                           "l0_mlp")

    pw = _mm(h2, wts["conf_w_pw1"], "l1_pw1")

    def f_glu(i, pa, pb, b1):
        return ((pa + b1[:, :D]) * _sigmoid(pb + b1[:, D:]),)

    (zg,) = _rowcall(f_glu, "l1_glu", T_LAT, TM, [_rin(pw, D, 0), _rin(pw, D, 1)], [wts["conf_b_pw1"]], [(D, f32)])
    (zc,) = _dwconv_fwd(zg, 0, wts["conf_conv_w"], wts["conf_conv_b"], 31, 15, ((0, T_LAT),), 128, "l1_conv", False)

    def ln_silu(z, lg, lb):
        mu = jnp.mean(z, axis=-1, keepdims=True)
        zc_ = z - mu
        var = jnp.mean(zc_ * zc_, axis=-1, keepdims=True)
        yv = zc_ * lax.rsqrt(var + EPS) * lg + lb
        return yv * _sigmoid(yv)

    def f_lnsilu(i, z, lg, lb):
        return (ln_silu(z, lg, lb),)

    (sb,) = _rowcall(f_lnsilu, "l1_ln_silu", T_LAT, TM, [_rin(zc)], [wts["conf_ln_g"], wts["conf_ln_b"]], [(D, bf16)])
    out1 = _mm(sb, wts["conf_w_pw2"], "l1_pw2")
    par1 = mlp_params([g1[1], wts["conf_b_pw2"], ng[1, 1][None], sc2[1], sh2[1], g2[1]])
    x3, h3, r1, mo1, x4, _ = _mlp_forward(x2, 0, out1, par1, wts["mlp_w_in"], wts["mlp_w_out"], 1, "l1_mlp")

    def loss_fn(x4_, fg, tgt):
        err = _rms(x4_, fg) - tgt
        per_row = jnp.mean(err * err, axis=-1, keepdims=True)
        return 0.5 * jnp.sum(per_row, axis=0, keepdims=True)

    def f_head(i, x4_, tgt, fg):
        loss, vjp = jax.vjp(lambda a, e: loss_fn(a, e, tgt), x4_, fg)
        dx, dfg = vjp(jnp.ones((1, 1), f32))
        return dx, jnp.broadcast_to(loss, (1, 128)), dfg

    dx4, loss_acc, dfinal_g = _rowcall(f_head, "head", T_LAT, TM, [_rin(x4), _rin(target)], [wts["final_g"]],
                                       [(D, f32)], [(1, 128), (1, D)])

    grads = {"final_g": dfinal_g}

    def normmod_bwd(xin, dh, dx_skip, g, sc, sh, tag):
        def fb(i, x_, dh_, dxs, g_, sc_, sh_):
            _, vjp = jax.vjp(_normmod, x_, g_, sc_, sh_)
            dx, dg, dsc, dsh = vjp(dh_)
            return dx + dxs, dg, dsc, dsh

        return _rowcall(fb, tag + "_normmod_bwd", T_LAT, TM, [_rin(xin), _rin(dh), _rin(dx_skip)], [g, sc, sh],
                        [(D, f32)], [(1, D)] * 3)

    dx3, dout1, dmo1, dhid1, acc1 = _mlp_backward(dx4, x3, r1, mo1, out1, par1, wts["mlp_w_in"], wts["mlp_w_out"], 1,
                                                  "l1_mlp_bwd")
    mlp_dw = _mlp_weight_grads(h3, dhid1, r1, dmo1, 1, None, "l1")
    dg1_1, db_pw2, dng11, dsc2_1, dsh2_1, dg2_1 = [acc1[k:k + 1] for k in range(6)]

    ds = _mm(dout1, wts["conf_w_pw2"], "l1_pw2_dx", tb=True)
    grads["conf_w_pw2"] = _mm(sb, dout1, "l1_pw2_dw", ta=True, out_dtype=bf16)
    grads["conf_b_pw2"] = db_pw2

    def f_lnsilu_bwd(i, z, ds_, lg, lb):
        _, vjp = jax.vjp(ln_silu, z, lg, lb)
        return vjp(ds_)

    dzc, dln_g, dln_b = _rowcall(f_lnsilu_bwd, "l1_ln_silu_bwd", T_LAT, TM, [_rin(zc), _rin(ds)],
                                 [wts["conf_ln_g"], wts["conf_ln_b"]], [(D, f32)], [(1, D)] * 2)
    grads["conf_ln_g"], grads["conf_ln_b"] = dln_g, dln_b
    dzg, dconv_w, dconv_b = _dwconv_bwd([dzc], zg, 0, wts["conf_conv_w"], 31, 15, ((0, T_LAT),), 128,
                                        "l1_conv_bwd", f32)
    grads["conf_conv_w"], grads["conf_conv_b"] = dconv_w, dconv_b

    def f_glu_bwd(i, pa, pb, dz, b1):
        _, vjp = jax.vjp(lambda a, b, c: (a + c[:, :D]) * _sigmoid(b + c[:, D:]), pa, pb, b1)
        da, db, dc = vjp(dz)
        return jnp.concatenate([da, db], axis=1), dc

    dpw, db_pw1 = _rowcall(f_glu_bwd, "l1_glu_bwd", T_LAT, TM, [_rin(pw, D, 0), _rin(pw, D, 1), _rin(dzg)],
                           [wts["conf_b_pw1"]], [(2 * D, bf16)], [(1, 2 * D)])
    grads["conf_b_pw1"] = db_pw1
    dh2 = _mm(dpw, wts["conf_w_pw1"], "l1_pw1_dx", tb=True)
    grads["conf_w_pw1"] = _mm(h2, dpw, "l1_pw1_dw", ta=True, out_dtype=bf16)
    dx2, dng10, dsc1_1, dsh1_1 = normmod_bwd(x2, dh2, dx3, ng[1, 0][None], sc1[1], sh1[1], "l1a")

    dx1, dout0, dmo0, dhid0, acc0 = _mlp_backward(dx2, x1, r0, mo0, out0, par0, wts["mlp_w_in"], wts["mlp_w_out"], 0,
                                                  "l0_mlp_bwd")
    grads["mlp_w_in"], grads["mlp_w_out"] = _mlp_weight_grads(h1, dhid0, r0, dmo0, 0, mlp_dw, "l0")
    dg1_0, _, dng01, dsc2_0, dsh2_0, dg2_0 = [acc0[k:k + 1] for k in range(6)]

    dz = _mm(dout0, wts["rec_w_out"], "l0_out_proj_dx", tb=True)
    grads["rec_w_out"] = _mm(zb, dout0, "l0_out_proj_dw", ta=True, out_dtype=bf16)

    def f_gate_bwd(i, gp, y0_, y1_, dz_):
        lat = jnp.where(i < N_CTX_TILES, 0.0, 1.0)
        _, vjp = jax.vjp(lambda a, b: _gelu(a) * b, gp, y0_ + y1_)
        dgp, dy = vjp(dz_)
        return dgp * lat, dy * lat

    dgp, dy = _rowcall(f_gate_bwd, "l0_gate_bwd", T_ALL, TM,
                       [_rin(gr, R, 0), _rin(y0), _rin(y1), _rin(dz, None, 0, -N_CTX_TILES)], [],
                       [(R, bf16), (R, f32)])
    (dh_f,) = _scan_call(a0, dy, SCAN_FWD_BWD, True, "l0_scan_fwd_bwd", True)
    (dh_r,) = _scan_call(a1, dy, SCAN_REV_BWD, False, "l0_scan_rev_bwd", True)

    def f_coeff_bwd(i, pre_, u_, dhf, dhr, ypf, ypr, ba, bx, lam):
        dpre, dba, dbx, dlam = [], [], [], []
        du = jnp.zeros_like(u_)
        for d, (dh_, yp_) in enumerate(((dhf, ypf), (dhr, ypr))):
            _, vjp = jax.vjp(_coeff, _gate_part(pre_, 2 * d), _gate_part(pre_, 2 * d + 1), u_,
                             ba[d:d + 1], bx[d:d + 1], lam[d:d + 1])
            dpa, dpx, du_d, dba_d, dbx_d, dlam_d = vjp((dh_ * yp_, dh_))
            dpre += [dpa, dpx]
            du = du + du_d
            dba.append(dba_d)
            dbx.append(dbx_d)
            dlam.append(dlam_d)
        return _gate_unpart(dpre), du, _rows2(*dba), _rows2(*dbx), _rows2(*dlam)

    dpre, du_direct, db_a, db_x, dlam = _rowcall(
        f_coeff_bwd, "l0_coeff_bwd", T_ALL, 64,
        [_rin(pre), _rin(u), _rin(dh_f), _rin(dh_r), _rin(yp0), _rin(yp1)],
        [wts["rec_b_a"], wts["rec_b_x"], wts["rec_lambda"]], [(4 * R, bf16), (R, f32)], [(2, R)] * 3)
    grads["rec_b_a"], grads["rec_b_x"], grads["rec_lambda"] = db_a, db_x, dlam
    du_gates = _gates_dx(dpre, wts["gates"])
    grads["gates"] = _gates_dw(ub, dpre)
    drec, dconv4_w, dconv4_b = _dwconv_bwd([du_direct, du_gates], gr, R // 256, wts["rec_conv_w"], 4, 1,
                                           CONV_SEGMENTS, 256, "l0_conv_bwd", bf16)
    grads["rec_conv_w"], grads["rec_conv_b"] = dconv4_w, dconv4_b
    dgr = jnp.concatenate([dgp, drec], axis=1)
    dh0 = _mm(dgr, wts["rec_w_in"], "l0_in_proj_dx", tb=True)
    grads["rec_w_in"] = _mm(h0, dgr, "l0_in_proj_dw", ta=True, out_dtype=bf16)

    def f_pre0_bwd(i, x0, dh_, dxs, g, scp_, shp_):
        lat = jnp.where(i < N_CTX_TILES, 0.0, 1.0)
        _, vjp = jax.vjp(lambda a, b, c, e: _normmod(a, b, blend(i, c), blend(i, e)), x0, g, scp_, shp_)
        dx, dg, dscp, dshp = vjp(dh_)
        return dx + lat * dxs, dg, dscp, dshp

    dx0cat, dng00, dscp, dshp = _rowcall(
        f_pre0_bwd, "l0_prenorm_bwd", T_ALL, TM, [_rin(x0cat), _rin(dh0), _rin(dx1, None, 0, -N_CTX_TILES)],
        [ng[0, 0][None], scp, shp], [(D, f32)], [(1, D), (2, D), (2, D)])

    grads["norm_g"] = jnp.stack([jnp.concatenate([dng00, dng01], 0), jnp.concatenate([dng10, dng11], 0)])
    dmods = jnp.stack([
        jnp.concatenate([dshp[1:2], dscp[1:2], dg1_0, dsh2_0, dsc2_0, dg2_0], axis=0),
        jnp.concatenate([dsh1_1, dsc1_1, dg1_1, dsh2_1, dsc2_1, dg2_1], axis=0)])
    dcmod = jnp.concatenate([dshp[0:1], dscp[0:1]], axis=0)
    return loss_acc[0, 0], dx0cat[T_CTX:], dmods, dcmod, grads


def _unshard_cols(g):
    g = jnp.moveaxis(g, 0, -2)
    return g.reshape(g.shape[:-2] + (g.shape[-2] * g.shape[-1],))


def _shard_cols(w):
    w = w.reshape(w.shape[:-1] + (N_DEV, w.shape[-1] // N_DEV))
    return jnp.moveaxis(w, -2, 0)


def _shard_rows(w):
    return w.reshape((N_DEV, w.shape[0] // N_DEV) + w.shape[1:])


SMALL_PACK_ROWS = 64


def kernel(x, c, ctx, c_ctx, w_ada, b_ada, norm_g, rec_w_in, rec_conv_w, rec_conv_b, rec_lambda, rec_w_a, rec_b_a, rec_w_x, rec_b_x, rec_w_out, conf_w_pw1, conf_b_pw1, conf_conv_w, conf_conv_b, conf_ln_g, conf_ln_b, conf_w_pw2, conf_b_pw2, mlp_w_in, mlp_w_out, final_g, loss_target, m_c_ctx, m_w_ada, m_b_ada, m_norm_g, m_rec_w_in, m_rec_conv_w, m_rec_conv_b, m_rec_lambda, m_rec_w_a, m_rec_b_a, m_rec_w_x, m_rec_b_x, m_rec_w_out, m_conf_w_pw1, m_conf_b_pw1, m_conf_conv_w, m_conf_conv_b, m_conf_ln_g, m_conf_ln_b, m_conf_w_pw2, m_conf_b_pw2, m_mlp_w_in, m_mlp_w_out, m_final_g, v_c_ctx, v_w_ada, v_b_ada, v_norm_g, v_rec_w_in, v_rec_conv_w, v_rec_conv_b, v_rec_lambda, v_rec_w_a, v_rec_b_a, v_rec_w_x, v_rec_b_x, v_rec_w_out, v_conf_w_pw1, v_conf_b_pw1, v_conf_conv_w, v_conf_conv_b, v_conf_ln_g, v_conf_ln_b, v_conf_w_pw2, v_conf_b_pw2, v_mlp_w_in, v_mlp_w_out, v_final_g):
    me = 4 * lax.axis_index("x") + 2 * lax.axis_index("y") + lax.axis_index("c")
    weights = dict(c_ctx=c_ctx, w_ada=w_ada, b_ada=b_ada, norm_g=norm_g, rec_w_in=rec_w_in, rec_conv_w=rec_conv_w,
                   rec_conv_b=rec_conv_b, rec_lambda=rec_lambda, rec_w_a=rec_w_a, rec_b_a=rec_b_a, rec_w_x=rec_w_x,
                   rec_b_x=rec_b_x, rec_w_out=rec_w_out, conf_w_pw1=conf_w_pw1, conf_b_pw1=conf_b_pw1,
                   conf_conv_w=conf_conv_w, conf_conv_b=conf_conv_b, conf_ln_g=conf_ln_g, conf_ln_b=conf_ln_b,
                   conf_w_pw2=conf_w_pw2, conf_b_pw2=conf_b_pw2, mlp_w_in=mlp_w_in, mlp_w_out=mlp_w_out, final_g=final_g)
    m_in = dict(c_ctx=m_c_ctx, w_ada=m_w_ada, b_ada=m_b_ada, norm_g=m_norm_g, rec_w_in=m_rec_w_in, rec_conv_w=m_rec_conv_w,
                rec_conv_b=m_rec_conv_b, rec_lambda=m_rec_lambda, rec_w_a=m_rec_w_a, rec_b_a=m_rec_b_a, rec_w_x=m_rec_w_x,
                rec_b_x=m_rec_b_x, rec_w_out=m_rec_w_out, conf_w_pw1=m_conf_w_pw1, conf_b_pw1=m_conf_b_pw1,
                conf_conv_w=m_conf_conv_w, conf_conv_b=m_conf_conv_b, conf_ln_g=m_conf_ln_g, conf_ln_b=m_conf_ln_b,
                conf_w_pw2=m_conf_w_pw2, conf_b_pw2=m_conf_b_pw2, mlp_w_in=m_mlp_w_in, mlp_w_out=m_mlp_w_out,
                final_g=m_final_g)
    v_in = dict(c_ctx=v_c_ctx, w_ada=v_w_ada, b_ada=v_b_ada, norm_g=v_norm_g, rec_w_in=v_rec_w_in, rec_conv_w=v_rec_conv_w,
                rec_conv_b=v_rec_conv_b, rec_lambda=v_rec_lambda, rec_w_a=v_rec_w_a, rec_b_a=v_rec_b_a, rec_w_x=v_rec_w_x,
                rec_b_x=v_rec_b_x, rec_w_out=v_rec_w_out, conf_w_pw1=v_conf_w_pw1, conf_b_pw1=v_conf_b_pw1,
                conf_conv_w=v_conf_conv_w, conf_conv_b=v_conf_conv_b, conf_ln_g=v_conf_ln_g, conf_ln_b=v_conf_ln_b,
                conf_w_pw2=v_conf_w_pw2, conf_b_pw2=v_conf_b_pw2, mlp_w_in=v_mlp_w_in, mlp_w_out=v_mlp_w_out,
                final_g=v_final_g)
    names = list(weights)

    small_items = [c, norm_g, rec_conv_w, rec_lambda, conf_b_pw1, conf_conv_w, conf_conv_b, conf_ln_g, conf_ln_b,
                   conf_b_pw2]
    flat = jnp.concatenate([a.reshape(-1) for a in small_items])
    flat = jnp.pad(flat, (0, SMALL_PACK_ROWS * 128 - flat.shape[0])).reshape(SMALL_PACK_ROWS, 128)
    big_items = [rec_w_in[0], rec_w_out[0], conf_w_pw1[0], conf_w_pw2[0], mlp_w_in, mlp_w_out]
    (small_all,) = _all_gather([flat], "gather_small")
    big_all = _all_gather_2level([a.astype(bf16).reshape(-1, a.shape[-1]) for a in big_items], "gather_weights")
    big_all = [g.reshape((N_DEV,) + a.shape) for g, a in zip(big_all, big_items)]

    small_all = small_all.reshape(N_DEV, -1)
    off = 0
    small = []
    for a in small_items:
        small.append(small_all[:, off:off + a.size].reshape((N_DEV,) + a.shape))
        off += a.size
    c_all, ng_all, rcw_all, lam_all, bpw1_all, ccw_all, ccb_all, lng_all, lnb_all, bpw2_all = small
    wts = {
        "norm_g": _unshard_cols(ng_all),
        "rec_conv_w": _unshard_cols(rcw_all)[0],
        "rec_lambda": _unshard_cols(lam_all)[0],
        "conf_b_pw1": _unshard_cols(bpw1_all),
        "conf_conv_w": _unshard_cols(ccw_all)[0],
        "conf_conv_b": _unshard_cols(ccb_all),
        "conf_ln_g": _unshard_cols(lng_all),
        "conf_ln_b": _unshard_cols(lnb_all),
        "conf_b_pw2": _unshard_cols(bpw2_all),
        "rec_conv_b": rec_conv_b,
        "rec_b_a": rec_b_a[0].reshape(2, R),
        "rec_b_x": rec_b_x[0].reshape(2, R),
        "final_g": final_g[None],
        "gates": _gate_matrix(rec_w_a[0], rec_w_x[0]),
        "rec_w_in": _unshard_cols(big_all[0]),
        "rec_w_out": big_all[1].reshape(R, D),
        "conf_w_pw1": _unshard_cols(big_all[2]),
        "conf_w_pw2": big_all[3].reshape(D, D),
        "mlp_w_in": big_all[4],
        "mlp_w_out": big_all[5],
    }

    c16 = jnp.concatenate([c_all[:, 0], jnp.broadcast_to(c_ctx[None], (8, D))], axis=0)
    b_loc = lax.dynamic_slice_in_dim(b_ada, me * ADA_SHARD, ADA_SHARD, axis=1)[:, None]
    (mods_all,) = _all_gather([_ada_forward(c16, w_ada, b_loc)], "gather_mods")
    mods_all = _unshard_cols(mods_all)
    mods = lax.dynamic_index_in_dim(mods_all, me, axis=1, keepdims=False).reshape(2, N_MOD, D)
    cmod = mods_all[0, 8, :2 * D].reshape(2, D)

    loss_part, grad_x, dmods, dcmod, grads = _local_step(x[0], ctx[0], loss_target[0], mods, cmod, wts)
    loss = lax.psum(loss_part, ("x", "y", "c"))

    dm_flat = jnp.concatenate([dmods.reshape(-1), dcmod.reshape(-1)]).reshape(-1, 128)
    (dm_all,) = _all_gather([dm_flat], "gather_dmods")
    dm_all = dm_all.reshape(N_DEV, -1)
    dmods_all = dm_all[:, :2 * N_MOD * D].reshape(N_DEV, 2, N_MOD * D)
    dcmod_all = jnp.pad(dm_all[:, 2 * N_MOD * D:], ((0, 0), (0, (N_MOD - 2) * D)))
    g16_full = jnp.stack([jnp.concatenate([dmods_all[:, 0], dcmod_all], axis=0),
                          jnp.concatenate([dmods_all[:, 1], jnp.zeros_like(dcmod_all)], axis=0)])
    g16 = lax.dynamic_slice_in_dim(g16_full, me * ADA_SHARD, ADA_SHARD, axis=2)
    dw_ada, ds_part = _ada_backward(c16, g16, w_ada)
    (ds_all,) = _all_gather([ds_part[0]], "gather_dsilu")

    big_names = ["rec_w_in", "rec_w_out", "conf_w_pw1", "conf_w_pw2", "mlp_w_in", "mlp_w_out"]
    big_grads = [_shard_cols(grads["rec_w_in"]), _shard_rows(grads["rec_w_out"]), _shard_cols(grads["conf_w_pw1"]),
                 _shard_rows(grads["conf_w_pw2"]), grads["mlp_w_in"], grads["mlp_w_out"]]
    big_grads = [g.reshape(N_DEV, -1, g.shape[-1]) for g in big_grads]
    big_pieces = _reduce_scatter(big_grads, big_names, "scatter_weight_grads")
    small_sharded = ["norm_g", "rec_conv_w", "rec_lambda", "conf_b_pw1", "conf_conv_w", "conf_conv_b", "conf_ln_g",
                     "conf_ln_b", "conf_b_pw2"]
    pack = jnp.concatenate([_shard_cols(grads[n]).reshape(N_DEV, -1) for n in small_sharded], axis=1)
    pack_len = pack.shape[1]
    pack = jnp.pad(pack, ((0, 0), (0, SMALL_PACK_ROWS * 128 - pack_len))).reshape(N_DEV, SMALL_PACK_ROWS, 128)
    (pack_recv,) = _all_to_all([pack], "scatter_small_grads")
    pack_recv = pack_recv.reshape(N_DEV, -1)

    dwg = grads["gates"]
    repl = {"rec_conv_b": grads["rec_conv_b"],
            "rec_w_a": jnp.stack([_gate_blocks(dwg, 0), _gate_blocks(dwg, 2)])[None],
            "rec_w_x": jnp.stack([_gate_blocks(dwg, 1), _gate_blocks(dwg, 3)])[None],
            "rec_b_a": grads["rec_b_a"].reshape(1, 2, N_BLK, BLK),
            "rec_b_x": grads["rec_b_x"].reshape(1, 2, N_BLK, BLK),
            "final_g": grads["final_g"][0]}
    repl_names = list(repl)
    repl_flat = jnp.concatenate([repl[n].reshape(-1) for n in repl_names])
    repl_len = repl_flat.shape[0]
    repl_rows = -(-repl_len // (16 * D)) * 16
    repl_flat = jnp.pad(repl_flat, (0, repl_rows * D - repl_len)).reshape(repl_rows, D).astype(bf16)
    (repl_all,) = _all_gather_2level([repl_flat], "gather_replicated_grads")
    repl_all = repl_all.reshape(N_DEV, -1)

    def as2d(shape):
        rows = 1
        for s in shape[:-1]:
            rows *= s
        return (rows, shape[-1])

    def whole(arr, shape):
        arr = arr.reshape((-1,) + as2d(shape))
        return (arr, arr.shape[0])

    pieces = {}
    shard_shapes = {n: weights[n].shape for n in names}
    for n, parts in zip(big_names, big_pieces):
        pieces[n] = parts
    off = 0
    for n in small_sharded:
        size = weights[n].size
        pieces[n] = [whole(pack_recv[:, off:off + size], shard_shapes[n])]
        off += size
    off = 0
    for n in repl_names:
        size = weights[n].size
        pieces[n] = [whole(repl_all[:, off:off + size], shard_shapes[n])]
        off += size
    pieces["w_ada"] = [whole(dw_ada, shard_shapes["w_ada"])]
    db_terms = jnp.concatenate([dmods_all, jnp.stack([dcmod_all, jnp.zeros_like(dcmod_all)], axis=1)], axis=0)
    pieces["b_ada"] = [whole(db_terms, shard_shapes["b_ada"])]
    pieces["c_ctx"] = [whole(ds_all[:, 0], shard_shapes["c_ctx"])]

    g_out, d_out, m_out, v_out = {}, {}, {}, {}
    for n in names:
        shape = shard_shapes[n]
        r2, c2 = as2d(shape)
        p = pieces[n]
        g, dl, nm, nv = _adamw(p, weights[n].reshape(r2, c2), m_in[n].reshape(r2, c2), v_in[n].reshape(r2, c2),
                               "adamw_" + n)
        g_out[n], d_out[n], m_out[n], v_out[n] = (t.reshape(shape) for t in (g, dl, nm, nv))

    return (loss, grad_x[None], *[g_out[n] for n in names], *[d_out[n] for n in names],
            *[m_out[n] for n in names], *[v_out[n] for n in names])
```

```python
import functools

import jax
import jax.numpy as jnp
from jax import lax
from jax.experimental import pallas as pl
from jax.experimental.pallas import tpu as pltpu

f32 = jnp.float32
bf16 = jnp.bfloat16

N_DEV = 8
D = 1024
T_LAT = 2048
T_CTX = 256
T_ALL = T_CTX + T_LAT
R = 1280
N_BLK = 16
BLK = R // N_BLK
F = 4096
GRID_W = 64
RG_C = 8.0
EPS = 1e-6
POS_BASE = 10000.0
N_MOD = 6
ADA_SHARD = N_MOD * D // N_DEV

ADAM_LR = 0.001
ADAM_B1 = 0.9
ADAM_B2 = 0.999
ADAM_EPS = 1e-08
ADAM_WD = 0.01
ADAM_STEP = 10

VMEM_LIMIT_V7X = 56 * 1024 * 1024
HALO = 16
MESH = pl.DeviceIdType.MESH


def _cparams(*sem):
    return pltpu.CompilerParams(dimension_semantics=sem, vmem_limit_bytes=VMEM_LIMIT_V7X)


def _pick(n, cands):
    for c in cands:
        if n % c == 0:
            return c
    raise ValueError(f"no block size for {n}")


def _position():
    x, y, c = lax.axis_index("x"), lax.axis_index("y"), lax.axis_index("c")
    return x, y, c, 4 * x + 2 * y + c


def _peer(x, y, c, k):
    px = (1 - x) if (k >> 2) & 1 else x
    py = (1 - y) if (k >> 1) & 1 else y
    pc = (1 - c) if k & 1 else c
    return (px, py, pc), 4 * px + 2 * py + pc


def _exchange(arrs, name, scatter):
    n = len(arrs)

    def body(*refs):
        ins, outs = refs[:n], refs[n:2 * n]
        send_sems, recv_sems, local_sems = refs[2 * n:]
        x, y, c, me = _position()
        local = []
        for a in range(n):
            src = ins[a].at[me] if scatter else ins[a]
            cp = pltpu.make_async_copy(src, outs[a].at[me], local_sems.at[a])
            cp.start()
            local.append(cp)
        sends, recvs = [], []
        for a in range(n):
            for k in range(1, N_DEV):
                peer, peer_lin = _peer(x, y, c, k)
                src = ins[a].at[peer_lin] if scatter else ins[a]
                cp = pltpu.make_async_remote_copy(
                    src_ref=src, dst_ref=outs[a].at[me], send_sem=send_sems.at[a, k - 1],
                    recv_sem=recv_sems.at[a, k - 1], device_id=peer, device_id_type=MESH)
                cp.start()
                sends.append(cp)
                recvs.append(pltpu.make_async_remote_copy(
                    src_ref=src, dst_ref=outs[a].at[peer_lin], send_sem=send_sems.at[a, k - 1],
                    recv_sem=recv_sems.at[a, k - 1], device_id=peer, device_id_type=MESH))
        for cp in recvs:
            cp.wait_recv()
        for cp in sends:
            cp.wait_send()
        for cp in local:
            cp.wait()

    if scatter:
        out_shape = [jax.ShapeDtypeStruct(a.shape, a.dtype) for a in arrs]
    else:
        out_shape = [jax.ShapeDtypeStruct((N_DEV,) + a.shape, a.dtype) for a in arrs]
    any_spec = pl.BlockSpec(memory_space=pl.ANY)
    return pl.pallas_call(
        body, name=name, out_shape=out_shape,
        in_specs=[any_spec] * n, out_specs=[any_spec] * n,
        scratch_shapes=[pltpu.SemaphoreType.DMA((n, N_DEV - 1)), pltpu.SemaphoreType.DMA((n, N_DEV - 1)),
                        pltpu.SemaphoreType.DMA((n,))],
    )(*arrs)


def _all_gather(arrs, name):
    return _exchange(arrs, name, scatter=False)


def _all_to_all(arrs, name):
    return _exchange(arrs, name, scatter=True)


def _lin(p):
    return 4 * p[0] + 2 * p[1] + p[2]


HBM_SPEC = pl.BlockSpec(memory_space=pltpu.HBM)
SEM_SPEC = pl.BlockSpec(memory_space=pltpu.SEMAPHORE)
DATAFLOW_EFFECT = pltpu.SideEffectType.DATAFLOW_SIDE_EFFECTING


def _split_copies(srcs, lands, send_sems, recv_sems, scatter):
    x, y, c, me = _position()
    out = []
    for a in range(len(srcs)):
        for k in range(1, N_DEV):
            peer, peer_lin = _peer(x, y, c, k)
            src = srcs[a].at[peer_lin] if scatter else srcs[a]
            mk = lambda slot: pltpu.make_async_remote_copy(
                src_ref=src, dst_ref=lands[a].at[slot], send_sem=send_sems.at[a * (N_DEV - 1) + k - 1],
                recv_sem=recv_sems.at[a * (N_DEV - 1) + k - 1], device_id=peer, device_id_type=MESH)
            out.append((mk(me), mk(peer_lin)))
    return out


def _exchange_start(srcs, lands, name, scatter):
    n = len(srcs)

    def body(*refs):
        srcs_r, lands_r = refs[:n], refs[n:2 * n]
        send_sems, recv_sems = refs[2 * n], refs[2 * n + 1]
        token = refs[-1]
        for outgoing, _ in _split_copies(srcs_r, lands_r, send_sems, recv_sems, scatter):
            outgoing.start()
        token[...] = jnp.zeros_like(token)

    hbm = lambda a: pltpu.HBM(a.shape, a.dtype)
    res = pl.pallas_call(
        body, name=name,
        out_shape=(pltpu.SemaphoreType.DMA((n * (N_DEV - 1),)), pltpu.SemaphoreType.DMA((n * (N_DEV - 1),)),
                   *[hbm(a) for a in srcs], *[hbm(a) for a in lands], jax.ShapeDtypeStruct((8, 128), f32)),
        in_specs=[HBM_SPEC] * (2 * n),
        out_specs=(SEM_SPEC, SEM_SPEC, *[HBM_SPEC] * (2 * n), pl.BlockSpec(memory_space=pltpu.VMEM)),
        input_output_aliases={i: 2 + i for i in range(2 * n)},
        compiler_params=pltpu.CompilerParams(has_side_effects=DATAFLOW_EFFECT),
    )(*[pltpu.with_memory_space_constraint(a, pltpu.HBM) for a in list(srcs) + list(lands)])
    return (res[0], res[1], list(res[2:2 + n]), list(res[2 + n:2 + 2 * n])), res[-1]


def _exchange_wait(handle, after, name, scatter):
    send_sems, recv_sems, srcs, lands = handle
    n = len(srcs)

    def body(*refs):
        srcs_r, lands_r = refs[:n], refs[n:2 * n]
        send_s, recv_s = refs[2 * n], refs[2 * n + 1]
        for outgoing, incoming in _split_copies(srcs_r, lands_r, send_s, recv_s, scatter):
            outgoing.wait_send()
            incoming.wait_recv()

    hbm = lambda a: pltpu.HBM(a.shape, a.dtype)
    res = pl.pallas_call(
        body, name=name, out_shape=tuple(hbm(a) for a in list(srcs) + list(lands)),
        in_specs=[HBM_SPEC] * (2 * n) + [SEM_SPEC, SEM_SPEC, pl.BlockSpec(memory_space=pl.ANY)],
        out_specs=tuple([HBM_SPEC] * (2 * n)),
        input_output_aliases={i: i for i in range(2 * n)},
        compiler_params=pltpu.CompilerParams(has_side_effects=DATAFLOW_EFFECT),
    )(*srcs, *lands, send_sems, recv_sems, after)
    return list(res[n:])


def _own_block_filled(block, me):
    land = lax.empty((N_DEV,) + block.shape, block.dtype)
    return lax.dynamic_update_index_in_dim(land, block, me, 0)


def _staged_copy(src, dst, buf, in_sems, out_sems, rows, chunk):
    n = rows // chunk

    def rd(i):
        return pltpu.make_async_copy(src.at[pl.ds(i * chunk, chunk)], buf.at[i % 2], in_sems.at[i % 2])

    def wr(i):
        return pltpu.make_async_copy(buf.at[i % 2], dst.at[pl.ds(i * chunk, chunk)], out_sems.at[i % 2])

    rd(0).start()
    for i in range(n):
        if i + 1 < n:
            if i >= 1:
                wr(i - 1).wait()
            rd(i + 1).start()
        rd(i).wait()
        wr(i).start()
    for i in range(max(n - 2, 0), n):
        wr(i).wait()


def _all_gather_2level(shards, name):
    n = len(shards)
    chunks = [_pick(s.shape[0], (416, 512, 256, 160, 128, 64, 16)) for s in shards]

    def body(*refs):
        ins, outs = refs[:n], refs[n:2 * n]
        send_sems, recv_sems, in_sems, out_sems = refs[2 * n:2 * n + 4]
        bufs = refs[2 * n + 4:]
        x, y, c, me = _position()
        sib, xn, yn, dg = (x, y, 1 - c), (1 - x, y, c), (x, 1 - y, c), (1 - x, 1 - y, c)

        def cp(a, k, src, slot, to):
            return pltpu.make_async_remote_copy(src_ref=src, dst_ref=outs[a].at[slot], send_sem=send_sems.at[a, k],
                                                recv_sem=recv_sems.at[a, k], device_id=to, device_id_type=MESH)

        for a in range(n):
            for k, to in ((0, sib), (1, xn), (2, yn)):
                cp(a, k, ins[a], me, to).start()
        for a in range(n):
            cp(a, 1, ins[a], _lin(xn), xn).wait_recv()
            cp(a, 3, outs[a].at[_lin(xn)], _lin(xn), sib).start()

            @pl.when(c == 0)
            def _():
                cp(a, 5, outs[a].at[_lin(xn)], _lin(xn), yn).start()

            cp(a, 2, ins[a], _lin(yn), yn).wait_recv()
            cp(a, 4, outs[a].at[_lin(yn)], _lin(yn), sib).start()

            @pl.when(c == 1)
            def _():
                cp(a, 5, outs[a].at[_lin(yn)], _lin(yn), xn).start()

        for a in range(n):
            cp(a, 5, ins[a], _lin(dg), xn).wait_recv()
            cp(a, 6, outs[a].at[_lin(dg)], _lin(dg), sib).start()
        for a in range(n):
            _staged_copy(ins[a], outs[a].at[me], bufs[a], in_sems.at[a], out_sems.at[a], shards[a].shape[0], chunks[a])
        for a in range(n):
            for k, origin in ((0, sib), (3, (1 - x, y, 1 - c)), (4, (x, 1 - y, 1 - c)), (6, (1 - x, 1 - y, 1 - c))):
                cp(a, k, ins[a], _lin(origin), sib).wait_recv()
            for k in range(7):
                cp(a, k, ins[a], me, sib).wait_send()

    any_spec = pl.BlockSpec(memory_space=pl.ANY)
    return pl.pallas_call(
        body, name=name, out_shape=[jax.ShapeDtypeStruct((N_DEV,) + s.shape, s.dtype) for s in shards],
        in_specs=[any_spec] * n, out_specs=[any_spec] * n,
        scratch_shapes=[pltpu.SemaphoreType.DMA((n, 7)), pltpu.SemaphoreType.DMA((n, 7)),
                        pltpu.SemaphoreType.DMA((n, 2)), pltpu.SemaphoreType.DMA((n, 2))]
        + [pltpu.VMEM((2, ch, s.shape[1]), s.dtype) for ch, s in zip(chunks, shards)],
    )(*shards)


def _plane_pos(x, y, q):
    return ((1 - x) if q & 2 else x, (1 - y) if q & 1 else y)


def _scatter_call(body, name, ins, out_shape, sems_per_array):
    n = len(ins)
    any_spec = pl.BlockSpec(memory_space=pl.ANY)
    return pl.pallas_call(
        body, name=name, out_shape=out_shape, in_specs=[any_spec] * n, out_specs=[any_spec] * n,
        scratch_shapes=[pltpu.SemaphoreType.DMA((n, sems_per_array)), pltpu.SemaphoreType.DMA((n, sems_per_array))],
    )(*ins)


def _scatter_d2d(gs, name):
    n = len(gs)

    def body(*refs):
        g_refs, recv_refs, send_sems, recv_sems = refs[:n], refs[n:2 * n], refs[2 * n], refs[2 * n + 1]
        x, y, c, me = _position()
        sib = (x, y, 1 - c)
        sends = []
        for a in range(n):
            for q in range(4):
                px, py = _plane_pos(x, y, q)
                cp = pltpu.make_async_remote_copy(
                    src_ref=g_refs[a].at[_lin((px, py, 1 - c))], dst_ref=recv_refs[a].at[q],
                    send_sem=send_sems.at[a, q], recv_sem=recv_sems.at[a, q], device_id=sib, device_id_type=MESH)
                cp.start()
                sends.append(cp)
        for cp in sends:
            cp.wait_recv()
        for cp in sends:
            cp.wait_send()

    return _scatter_call(body, name, gs, [jax.ShapeDtypeStruct((4,) + g.shape[1:], g.dtype) for g in gs], 4)


def _scatter_ici_first(hs, name):
    n = len(hs)

    def body(*refs):
        h_refs, recv_refs, send_sems, recv_sems = refs[:n], refs[n:2 * n], refs[2 * n], refs[2 * n + 1]
        x, y, c, me = _position()
        xn, yn = (1 - x, y, c), (x, 1 - y, c)

        def cp(a, k, q, to):
            return pltpu.make_async_remote_copy(
                src_ref=h_refs[a].at[q], dst_ref=recv_refs[a].at[k], send_sem=send_sems.at[a, k],
                recv_sem=recv_sems.at[a, k], device_id=to, device_id_type=MESH)

        @pl.when(c == 0)
        def _():
            for a in range(n):
                cp(a, 0, 2, xn).start()
                cp(a, 1, 3, xn).start()

        @pl.when(c == 1)
        def _():
            for a in range(n):
                cp(a, 0, 1, yn).start()
                cp(a, 1, 3, yn).start()

        for a in range(n):
            for k in range(2):
                cp(a, k, 0, xn).wait_recv()
        for a in range(n):
            for k in range(2):
                cp(a, k, 0, xn).wait_send()

    return _scatter_call(body, name, hs, [jax.ShapeDtypeStruct((2,) + h.shape[1:], h.dtype) for h in hs], 2)


def _scatter_ici_second(k1s, name):
    n = len(k1s)

    def body(*refs):
        k_refs, recv_refs, send_sems, recv_sems = refs[:n], refs[n:2 * n], refs[2 * n], refs[2 * n + 1]
        x, y, c, me = _position()
        xn, yn = (1 - x, y, c), (x, 1 - y, c)

        def cp(a, to):
            return pltpu.make_async_remote_copy(src_ref=k_refs[a], dst_ref=recv_refs[a], send_sem=send_sems.at[a, 0],
                                                recv_sem=recv_sems.at[a, 0], device_id=to, device_id_type=MESH)

        @pl.when(c == 0)
        def _():
            for a in range(n):
                cp(a, yn).start()

        @pl.when(c == 1)
        def _():
            for a in range(n):
                cp(a, xn).start()

        for a in range(n):
            cp(a, xn).wait_recv()
        for a in range(n):
            cp(a, xn).wait_send()

    return _scatter_call(body, name, k1s, [jax.ShapeDtypeStruct(k.shape, k.dtype) for k in k1s], 1)


def _add_blocks(a, a_idx, b, b_idx, out_dtype, name):
    rows, cols = a.shape[1:]
    n = a_idx.shape[0]
    tm = _pick(rows, (512, 256, 160, 128, 32, 16))

    def body(ia_ref, ib_ref, a_ref, b_ref, o_ref):
        o_ref[...] = (a_ref[...].astype(f32) + b_ref[...].astype(f32)).astype(o_ref.dtype)

    grid_spec = pltpu.PrefetchScalarGridSpec(
        num_scalar_prefetch=2, grid=(n, rows // tm),
        in_specs=[pl.BlockSpec((None, tm, cols), lambda j, i, ia, ib: (ia[j], i, 0)),
                  pl.BlockSpec((None, tm, cols), lambda j, i, ia, ib: (ib[j], i, 0))],
        out_specs=pl.BlockSpec((None, tm, cols), lambda j, i, ia, ib: (j, i, 0)))
    return pl.pallas_call(body, name=name, out_shape=jax.ShapeDtypeStruct((n, rows, cols), out_dtype),
                          grid_spec=grid_spec, compiler_params=_cparams("parallel", "parallel"))(a_idx, b_idx, a, b)


def _reduce_scatter(gs, names, tag):
    x, y, c, me = _position()
    i32 = lambda *v: jnp.stack([jnp.asarray(t, jnp.int32) for t in v])
    recvs = _scatter_d2d(gs, tag + "_d2d")
    own_idx = i32(*[_lin(_plane_pos(x, y, q) + (c,)) for q in range(4)])
    hs = [_add_blocks(g, own_idx, r, i32(0, 1, 2, 3), bf16, f"{tag}_add_chip_{nm}")
          for g, r, nm in zip(gs, recvs, names)]
    recv2s = _scatter_ici_first(hs, tag + "_ici_first")
    k1s = [_add_blocks(h, i32(1 + c), r2, i32(1), bf16, f"{tag}_add_onward_{nm}")[0]
           for h, r2, nm in zip(hs, recv2s, names)]
    lasts = _scatter_ici_second(k1s, tag + "_ici_second")
    return [[(h, 1), (r2, 1), (last[None], 1)] for h, r2, last in zip(hs, recv2s, lasts)]


def _mm(a, b, name, ta=False, tb=False, out_dtype=f32):
    if ta:
        k_dim, m_dim = a.shape
    else:
        m_dim, k_dim = a.shape
    if tb:
        n_dim, k2 = b.shape
    else:
        k2, n_dim = b.shape
    assert k_dim == k2, (a.shape, b.shape)
    assert a.dtype == bf16 and b.dtype == bf16
    bm = _pick(m_dim, (512, 768, 640, 256, 128))
    bn = _pick(n_dim, (512, 640, 256, 128))
    bk = _pick(k_dim, (1024, 1280, 768, 512))
    nk = k_dim // bk
    a_spec = (pl.BlockSpec((bk, bm), lambda i, j, k: (k, i)) if ta
              else pl.BlockSpec((bm, bk), lambda i, j, k: (i, k)))
    b_spec = (pl.BlockSpec((bn, bk), lambda i, j, k: (j, k)) if tb
              else pl.BlockSpec((bk, bn), lambda i, j, k: (k, j)))
    dims = (((0 if ta else 1,), (1 if tb else 0,)), ((), ()))

    def body_single(a_ref, b_ref, o_ref):
        o_ref[...] = lax.dot_general(a_ref[...], b_ref[...], dims, preferred_element_type=f32).astype(o_ref.dtype)

    def body(a_ref, b_ref, o_ref, acc_ref):
        k = pl.program_id(2)

        @pl.when(k == 0)
        def _():
            acc_ref[...] = jnp.zeros_like(acc_ref)

        acc_ref[...] += lax.dot_general(a_ref[...], b_ref[...], dims, preferred_element_type=f32)

        @pl.when(k == nk - 1)
        def _():
            o_ref[...] = acc_ref[...].astype(o_ref.dtype)

    return pl.pallas_call(
        body_single if nk == 1 else body, name=name, out_shape=jax.ShapeDtypeStruct((m_dim, n_dim), out_dtype),
        grid=(m_dim // bm, n_dim // bn, nk), in_specs=[a_spec, b_spec],
        out_specs=pl.BlockSpec((bm, bn), lambda i, j, k: (i, j)),
        scratch_shapes=[] if nk == 1 else [pltpu.VMEM((bm, bn), f32)],
        compiler_params=_cparams("parallel", "parallel", "arbitrary"),
    )(a, b)


def _rin(arr, width=None, cb=0, roff=0):
    return (arr, arr.shape[1] if width is None else width, cb, roff)


def _rowcall(fn, name, rows, tm, row_ins, par_ins, row_outs, acc_outs=()):
    nr, npar, nro = len(row_ins), len(par_ins), len(row_outs)
    in_specs, args = [], []
    for arr, width, cb, roff in row_ins:
        if roff >= 0:
            imap = lambda i, cb=cb, roff=roff: (i + roff, cb)
        else:
            imap = lambda i, cb=cb, roff=roff: (jnp.maximum(i + roff, 0), cb)
        in_specs.append(pl.BlockSpec((tm, width), imap))
        args.append(arr)
    for p in par_ins:
        in_specs.append(pl.BlockSpec(p.shape, lambda i: (0, 0)))
        args.append(p)
    out_shape, out_specs = [], []
    for width, dt in row_outs:
        out_shape.append(jax.ShapeDtypeStruct((rows, width), dt))
        out_specs.append(pl.BlockSpec((tm, width), lambda i: (i, 0)))
    for p, width in acc_outs:
        out_shape.append(jax.ShapeDtypeStruct((p, width), f32))
        out_specs.append(pl.BlockSpec((p, width), lambda i: (0, 0)))

    def body(*refs):
        i = pl.program_id(0)
        res = fn(i, *[r[...] for r in refs[:nr + npar]])
        outs = refs[nr + npar:]
        for o, v in zip(outs[:nro], res[:nro]):
            o[...] = v.astype(o.dtype)
        if acc_outs:
            @pl.when(i == 0)
            def _():
                for o in outs[nro:]:
                    o[...] = jnp.zeros_like(o)

            for o, v in zip(outs[nro:], res[nro:]):
                o[...] += v

    return pl.pallas_call(
        body, name=name, out_shape=out_shape, grid=(rows // tm,), in_specs=in_specs, out_specs=out_specs,
        compiler_params=_cparams("arbitrary"),
    )(*args)


def _rms(x, g):
    return x * lax.rsqrt(jnp.mean(x * x, axis=-1, keepdims=True) + EPS) * g


def _normmod(x, g, sc, sh):
    return _rms(x, g) * (1.0 + sc) + sh


def _rows2(v0, v1):
    rid = lax.broadcasted_iota(jnp.int32, (2, v0.shape[1]), 0)
    return jnp.where(rid == 0, v0, v1)


def _gelu(x):
    return 0.5 * x * (1.0 + jnp.tanh(0.7978845608028654 * (x + 0.044715 * (x * x * x))))


def _sigmoid(x):
    return 1.0 / (1.0 + jnp.exp(-x))


def _coeff(pre_a, pre_x, u, ba, bx, lam):
    r = _sigmoid(pre_a + ba)
    ig = _sigmoid(pre_x + bx)
    nl = -lam
    sp = jnp.maximum(nl, 0.0) + jnp.log(1.0 + jnp.exp(-jnp.abs(nl)))
    la = -RG_C * r * sp
    a = jnp.exp(la)
    one_minus_a2 = -jnp.tanh(la) * (a * a + 1.0)
    return a, jnp.sqrt(one_minus_a2) * (ig * u)


SCAN_CHUNK = 256


def _scan_call(a, v, chunk_of, reverse, name, backward):
    rows, width = a.shape
    n_out = 1 if backward else 2
    nt = SCAN_CHUNK // 8

    def body(a_ref, v_ref, *rest):
        outs, state_ref = rest[:-1], rest[-1]

        @pl.when(pl.program_id(0) == 0)
        def _():
            state_ref[...] = jnp.zeros_like(state_ref)

        rid = lax.broadcasted_iota(jnp.int32, (8, width), 0)

        def tile(j, st):
            t0 = pl.multiple_of((nt - 1 - j if reverse else j) * 8, 8)
            at = a_ref[pl.ds(t0, 8), :]
            vt = v_ref[pl.ds(t0, 8), :]
            out = jnp.zeros((8, width), f32)
            prev = jnp.zeros((8, width), f32)
            for i in (range(7, -1, -1) if reverse else range(8)):
                if backward:
                    g = vt[i:i + 1] + st
                    st = at[i:i + 1] * g
                    out = jnp.where(rid == i, g, out)
                else:
                    prev = jnp.where(rid == i, st, prev)
                    st = at[i:i + 1] * st + vt[i:i + 1]
                    out = jnp.where(rid == i, st, out)
            outs[0][pl.ds(t0, 8), :] = out
            if not backward:
                outs[1][pl.ds(t0, 8), :] = prev
            return st

        state_ref[0:1, :] = lax.fori_loop(0, nt, tile, state_ref[0:1, :])

    spec = pl.BlockSpec((SCAN_CHUNK, width), lambda t: (chunk_of(t), 0))
    return pl.pallas_call(
        body, name=name, out_shape=[jax.ShapeDtypeStruct((rows, width), f32)] * n_out,
        grid=(rows // SCAN_CHUNK,), in_specs=[spec, spec], out_specs=[spec] * n_out,
        scratch_shapes=[pltpu.VMEM((8, width), f32)],
        compiler_params=_cparams("arbitrary"),
    )(a, v)


CONV_CHUNK = 256


def _fill_padded(pad_ref, src_ref, start, n):
    cb = pad_ref.shape[1]
    pad_ref[pl.ds(0, HALO), :] = jnp.zeros((HALO, cb), f32)
    pad_ref[pl.ds(HALO, n), :] = src_ref[pl.ds(start, n), :].astype(f32)
    pad_ref[pl.ds(HALO + n, HALO), :] = jnp.zeros((HALO, cb), f32)


def _dwconv_fwd(x, x_cb0, w, b, taps, pad_left, segments, cb, name, emit_bf16):
    rows = x.shape[0]
    width = w.shape[1]

    def body(x_ref, w_ref, b_ref, *rest):
        outs, xp = rest[:-1], rest[-1]
        for start, n in segments:
            _fill_padded(xp, x_ref, start, n)
            for c0 in range(0, n, CONV_CHUNK):
                acc = jnp.zeros((CONV_CHUNK, cb), f32) + b_ref[...]
                for k in range(taps):
                    acc = acc + w_ref[k:k + 1, :] * xp[pl.ds(HALO + c0 + k - pad_left, CONV_CHUNK), :]
                for o in outs:
                    o[pl.ds(start + c0, CONV_CHUNK), :] = acc.astype(o.dtype)

    out_dtypes = [f32, bf16] if emit_bf16 else [f32]
    return pl.pallas_call(
        body, name=name, out_shape=[jax.ShapeDtypeStruct((rows, width), dt) for dt in out_dtypes],
        grid=(width // cb,),
        in_specs=[pl.BlockSpec((rows, cb), lambda j: (0, j + x_cb0)), pl.BlockSpec((taps, cb), lambda j: (0, j)),
                  pl.BlockSpec((1, cb), lambda j: (0, j))],
        out_specs=[pl.BlockSpec((rows, cb), lambda j: (0, j))] * len(out_dtypes),
        scratch_shapes=[pltpu.VMEM((rows + 2 * HALO, cb), f32)],
        compiler_params=_cparams("parallel"),
    )(x, w, b)


def _dwconv_bwd(douts, x, x_cb0, w, taps, pad_left, segments, cb, name, dx_dtype):
    rows = x.shape[0]
    width = w.shape[1]
    nd = len(douts)

    def body(*refs):
        d_refs, x_ref, w_ref = refs[:nd], refs[nd], refs[nd + 1]
        dx_ref, dw_ref, db_ref, xp, dp, dsum = refs[nd + 2:]
        dw_ref[...] = jnp.zeros_like(dw_ref)
        db_ref[...] = jnp.zeros_like(db_ref)
        if nd > 1:
            total = d_refs[0][...]
            for r in d_refs[1:]:
                total = total + r[...]
            dsum[...] = total
            d_ref = dsum
        else:
            d_ref = d_refs[0]
        for start, n in segments:
            _fill_padded(xp, x_ref, start, n)
            _fill_padded(dp, d_ref, start, n)
            for c0 in range(0, n, CONV_CHUNK):
                dchunk = dp[pl.ds(HALO + c0, CONV_CHUNK), :]
                db_ref[...] += jnp.sum(dchunk, axis=0, keepdims=True)
                acc = jnp.zeros((CONV_CHUNK, cb), f32)
                for k in range(taps):
                    acc = acc + w_ref[k:k + 1, :] * dp[pl.ds(HALO + c0 + pad_left - k, CONV_CHUNK), :]
                    xs = xp[pl.ds(HALO + c0 + k - pad_left, CONV_CHUNK), :]
                    dw_ref[k:k + 1, :] += jnp.sum(dchunk * xs, axis=0, keepdims=True)
                dx_ref[pl.ds(start + c0, CONV_CHUNK), :] = acc.astype(dx_ref.dtype)

    dspec = pl.BlockSpec((rows, cb), lambda j: (0, j))
    return pl.pallas_call(
        body, name=name,
        out_shape=[jax.ShapeDtypeStruct((rows, width), dx_dtype), jax.ShapeDtypeStruct((taps, width), f32),
                   jax.ShapeDtypeStruct((1, width), f32)],
        grid=(width // cb,),
        in_specs=[dspec] * nd + [pl.BlockSpec((rows, cb), lambda j: (0, j + x_cb0)),
                                 pl.BlockSpec((taps, cb), lambda j: (0, j))],
        out_specs=[dspec, pl.BlockSpec((taps, cb), lambda j: (0, j)), pl.BlockSpec((1, cb), lambda j: (0, j))],
        scratch_shapes=[pltpu.VMEM((rows + 2 * HALO, cb), f32), pltpu.VMEM((rows + 2 * HALO, cb), f32),
                        pltpu.VMEM((rows, cb), f32)],
        compiler_params=_cparams("parallel"),
    )(*douts, x, w)


def _ada_forward(c16, w_ada, b_loc):
    def body(c_ref, w_ref, b_ref, o_ref):
        cv = c_ref[...]
        s = (cv * _sigmoid(cv)).astype(bf16)
        o_ref[0] = jnp.dot(s, w_ref[0].astype(bf16), preferred_element_type=f32) + b_ref[0]

    return pl.pallas_call(
        body, name="ada_forward", out_shape=jax.ShapeDtypeStruct((2, 16, ADA_SHARD), f32), grid=(2,),
        in_specs=[pl.BlockSpec((16, D), lambda l: (0, 0)), pl.BlockSpec((1, D, ADA_SHARD), lambda l: (l, 0, 0)),
                  pl.BlockSpec((1, 1, ADA_SHARD), lambda l: (l, 0, 0))],
        out_specs=pl.BlockSpec((1, 16, ADA_SHARD), lambda l: (l, 0, 0)),
        compiler_params=_cparams("parallel"),
    )(c16, w_ada, b_loc)


def _ada_backward(c16, g16, w_ada):
    def body(c_ref, g_ref, w_ref, dw_ref, ds_ref):
        cv = c_ref[...]
        s = (cv * _sigmoid(cv)).astype(bf16)
        g = g_ref[0].astype(bf16)
        dw_ref[0] = lax.dot_general(s, g, (((0,), (0,)), ((), ())), preferred_element_type=f32)
        ds = lax.dot_general(g, w_ref[0].astype(bf16), (((1,), (1,)), ((), ())), preferred_element_type=f32)
        cc = cv[8:9]
        sg = _sigmoid(cc)
        dsilu = sg * (1.0 + cc * (1.0 - sg))
        ds_ref[0] = jnp.zeros((8, D), f32) + jnp.sum(ds[8:16], axis=0, keepdims=True) * dsilu

    return pl.pallas_call(
        body, name="ada_backward",
        out_shape=[jax.ShapeDtypeStruct((2, D, ADA_SHARD), f32), jax.ShapeDtypeStruct((2, 8, D), f32)], grid=(2,),
        in_specs=[pl.BlockSpec((16, D), lambda l: (0, 0)), pl.BlockSpec((1, 16, ADA_SHARD), lambda l: (l, 0, 0)),
                  pl.BlockSpec((1, D, ADA_SHARD), lambda l: (l, 0, 0))],
        out_specs=[pl.BlockSpec((1, D, ADA_SHARD), lambda l: (l, 0, 0)), pl.BlockSpec((1, 8, D), lambda l: (l, 0, 0))],
        compiler_params=_cparams("parallel"),
    )(c16, g16, w_ada)


def _adamw(pieces, w, m, v, name):
    rows, cols = w.shape
    n_arr = len(pieces)
    counts = [cnt for _, cnt in pieces]
    pieces = [p for p, _ in pieces]
    tm = 256 if (rows % 256 == 0 and rows > 256) else rows

    def body(*refs):
        p_refs = refs[:n_arr]
        w_ref, m_ref, v_ref, g_ref, d_ref, nm_ref, nv_ref = refs[n_arr:]
        g = None
        for p_ref in p_refs:
            for j in range(p_ref.shape[0]):
                term = p_ref[j].astype(f32)
                g = term if g is None else g + term
        m2 = ADAM_B1 * m_ref[...] + (1.0 - ADAM_B1) * g
        v2 = ADAM_B2 * v_ref[...] + (1.0 - ADAM_B2) * (g * g)
        m_hat = m2 / (1.0 - ADAM_B1 ** ADAM_STEP)
        v_hat = v2 / (1.0 - ADAM_B2 ** ADAM_STEP)
        g_ref[...] = g
        d_ref[...] = -ADAM_LR * (m_hat / (jnp.sqrt(v_hat) + ADAM_EPS) + ADAM_WD * w_ref[...])
        nm_ref[...] = m2
        nv_ref[...] = v2

    spec = pl.BlockSpec((tm, cols), lambda i: (i, 0))
    return pl.pallas_call(
        body, name=name, out_shape=[jax.ShapeDtypeStruct((rows, cols), f32)] * 4, grid=(rows // tm,),
        in_specs=[pl.BlockSpec((cnt, tm, cols), lambda i: (0, i, 0)) for cnt in counts] + [spec, spec, spec],
        out_specs=[spec] * 4, compiler_params=_cparams("parallel"),
    )(*pieces, w, m, v)


MLP_TM = 256
FB = F // N_DEV


def _stack_rows(vals, n):
    cols = vals[0].shape[1]
    rid = lax.broadcasted_iota(jnp.int32, (n, cols), 0)
    out = jnp.zeros((n, cols), f32)
    for k, v in enumerate(vals):
        out = jnp.where(rid == k, v, out)
    return out


N_MLP_PARAMS = 9


class _ParamRows:
    def __init__(self, ref):
        self.ref = ref

    def __getitem__(self, sl):
        return self.ref[8 * sl.start:8 * sl.start + 1, :]


def _resident(shape, imap):
    return pl.BlockSpec(shape, imap, pipeline_mode=pl.Buffered(1))


def _mlp_forward(xa, xa_roff, out_prev, par, w_in, w_out, layer, name):
    def body(xa_ref, op_ref, par_ref, win_ref, wout_ref, x1_ref, h_ref, r_ref, mo_ref, x2_ref, hn_ref):
        p = _ParamRows(par_ref)
        x1 = xa_ref[...] + p[0:1] * (op_ref[...] + p[1:2])
        h = _normmod(x1, p[2:3], p[3:4], p[4:5]).astype(bf16)
        x1_ref[...] = x1
        h_ref[...] = h
        mo = jnp.zeros((MLP_TM, D), f32)
        for j in range(N_DEV):
            r = jnp.maximum(jnp.dot(h, win_ref[j], preferred_element_type=f32), 0.0)
            r_ref[:, j * FB:(j + 1) * FB] = r.astype(bf16)
            mo = mo + jnp.dot((r * r).astype(bf16), wout_ref[j], preferred_element_type=f32)
        mo_ref[...] = mo.astype(bf16)
        x2 = x1 + p[5:6] * mo
        x2_ref[...] = x2
        hn_ref[...] = _normmod(x2, p[6:7], p[7:8], p[8:9]).astype(bf16)

    row = lambda width: pl.BlockSpec((MLP_TM, width), lambda i: (i, 0))
    return pl.pallas_call(
        body, name=name, grid=(T_LAT // MLP_TM,),
        out_shape=[jax.ShapeDtypeStruct((T_LAT, D), f32), jax.ShapeDtypeStruct((T_LAT, D), bf16),
                   jax.ShapeDtypeStruct((T_LAT, F), bf16), jax.ShapeDtypeStruct((T_LAT, D), bf16),
                   jax.ShapeDtypeStruct((T_LAT, D), f32), jax.ShapeDtypeStruct((T_LAT, D), bf16)],
        in_specs=[pl.BlockSpec((MLP_TM, D), lambda i: (i + xa_roff, 0)), row(D), pl.BlockSpec((8 * N_MLP_PARAMS, D), lambda i: (0, 0)),
                  _resident((N_DEV, None, D, FB), lambda i: (0, layer, 0, 0)),
                  _resident((N_DEV, None, FB, D), lambda i: (0, layer, 0, 0))],
        out_specs=[row(D), row(D), row(F), row(D), row(D), row(D)],
        compiler_params=_cparams("parallel"),
    )(xa, out_prev, par, w_in, w_out)


def _mlp_backward(dx2, x1, r, mo, out_prev, par, w_in, w_out, layer, name):
    nt = (((1,), (1,)), ((), ()))

    def body(dx2_ref, x1_ref, r_ref, mo_ref, op_ref, par_ref, win_ref, wout_ref, dx1_ref, dop_ref, dmo_ref, dhid_ref,
             acc_ref):
        p = _ParamRows(par_ref)
        dx2v = dx2_ref[...]
        dmo = (p[5:6] * dx2v).astype(bf16)
        dmo_ref[...] = dmo
        dh = jnp.zeros((MLP_TM, D), f32)
        mo = mo_ref[...].astype(f32)
        for j in range(N_DEV):
            rf = r_ref[:, j * FB:(j + 1) * FB].astype(f32)
            dact = lax.dot_general(dmo, wout_ref[j], nt, preferred_element_type=f32)
            dhid = (dact * (2.0 * rf)).astype(bf16)
            dhid_ref[:, j * FB:(j + 1) * FB] = dhid
            dh = dh + lax.dot_general(dhid, win_ref[j], nt, preferred_element_type=f32)
        x1 = x1_ref[...]
        _, vjp = jax.vjp(_normmod, x1, p[2:3], p[3:4], p[4:5])
        dx, dng, dsc, dsh = vjp(dh)
        dx1 = dx2v + dx
        dx1_ref[...] = dx1
        dop_ref[...] = (p[0:1] * dx1).astype(bf16)
        sums = _stack_rows([jnp.sum(dx1 * (op_ref[...] + p[1:2]), axis=0, keepdims=True),
                            p[0:1] * jnp.sum(dx1, axis=0, keepdims=True), dng, dsc, dsh,
                            jnp.sum(dx2v * mo, axis=0, keepdims=True)], 8)

        @pl.when(pl.program_id(0) == 0)
        def _():
            acc_ref[...] = jnp.zeros_like(acc_ref)

        acc_ref[...] += sums

    row = lambda width: pl.BlockSpec((MLP_TM, width), lambda i: (i, 0))
    return pl.pallas_call(
        body, name=name, grid=(T_LAT // MLP_TM,),
        out_shape=[jax.ShapeDtypeStruct((T_LAT, D), f32), jax.ShapeDtypeStruct((T_LAT, D), bf16),
                   jax.ShapeDtypeStruct((T_LAT, D), bf16), jax.ShapeDtypeStruct((T_LAT, F), bf16),
                   jax.ShapeDtypeStruct((8, D), f32)],
        in_specs=[row(D), row(D), row(F), row(D), row(D), pl.BlockSpec((8 * N_MLP_PARAMS, D), lambda i: (0, 0)),
                  _resident((N_DEV, None, D, FB), lambda i: (0, layer, 0, 0)),
                  _resident((N_DEV, None, FB, D), lambda i: (0, layer, 0, 0))],
        out_specs=[row(D), row(D), row(D), row(F), pl.BlockSpec((8, D), lambda i: (0, 0))],
        compiler_params=_cparams("arbitrary"),
    )(dx2, x1, r, mo, out_prev, par, w_in, w_out)


def _mlp_weight_grads(h, dhid, r, dmo, layer, other, tag):
    tn = (((0,), (0,)), ((), ()))

    def body_in(h_ref, dhid_ref, *rest):
        rest[-1][...] = lax.dot_general(h_ref[...], dhid_ref[...], tn, preferred_element_type=f32).astype(bf16)

    def body_out(r_ref, dmo_ref, *rest):
        rf = r_ref[...].astype(f32)
        rest[-1][...] = lax.dot_general((rf * rf).astype(bf16), dmo_ref[...], tn,
                                        preferred_element_type=f32).astype(bf16)

    def call(body, name, operands, specs, block, prev):
        extra = [] if prev is None else [prev]
        return pl.pallas_call(
            body, name=name, grid=(N_DEV,), out_shape=jax.ShapeDtypeStruct((N_DEV, 2) + block, bf16),
            in_specs=specs + [pl.BlockSpec(memory_space=pl.ANY)] * len(extra),
            out_specs=pl.BlockSpec((None, None) + block, lambda j: (j, layer, 0, 0)),
            input_output_aliases={} if prev is None else {2: 0}, compiler_params=_cparams("parallel"),
        )(*operands, *extra)

    dw_in = call(body_in, tag + "_mlp_in_dw", [h, dhid],
                 [_resident((T_LAT, D), lambda j: (0, 0)), pl.BlockSpec((T_LAT, FB), lambda j: (0, j))], (D, FB),
                 None if other is None else other[0])
    dw_out = call(body_out, tag + "_mlp_out_dw", [r, dmo],
                  [pl.BlockSpec((T_LAT, FB), lambda j: (0, j)), _resident((T_LAT, D), lambda j: (0, 0))], (FB, D),
                  None if other is None else other[1])
    return dw_in, dw_out


def _pos_embed():
    n_rows = T_LAT // GRID_W
    q = D // 4
    omega = 1.0 / (POS_BASE ** (jnp.arange(q, dtype=f32) / q))
    er = jnp.arange(n_rows, dtype=jnp.int32).astype(f32)[:, None] * omega[None, :]
    ec = jnp.arange(GRID_W, dtype=jnp.int32).astype(f32)[:, None] * omega[None, :]
    by_row = jnp.concatenate([jnp.sin(er), jnp.cos(er)], axis=-1)[:, None, :]
    by_col = jnp.concatenate([jnp.sin(ec), jnp.cos(ec)], axis=-1)[None, :, :]
    full = jnp.concatenate([jnp.broadcast_to(by_row, (n_rows, GRID_W, D // 2)),
                            jnp.broadcast_to(by_col, (n_rows, GRID_W, D // 2))], axis=-1)
    return full.reshape(T_LAT, D)


HALF = R // 2
BLK_PER_HALF = N_BLK // 2
N_PARTS = 4


def _gate_matrix(w_a, w_x):
    eye = jnp.eye(BLK_PER_HALF, dtype=bf16)
    cols = []
    for h in range(2):
        for d in range(2):
            for w in (w_a, w_x):
                blocks = w[d, BLK_PER_HALF * h:BLK_PER_HALF * (h + 1)].astype(bf16)
                cols.append(jnp.einsum("hij,hg->higj", blocks, eye).reshape(HALF, HALF))
    return jnp.concatenate(cols, axis=1)


def _gate_blocks(dwg, part):
    out = []
    for h in range(2):
        blk = dwg[:, (N_PARTS * h + part) * HALF:(N_PARTS * h + part + 1) * HALF]
        blk = blk.reshape(BLK_PER_HALF, BLK, BLK_PER_HALF, BLK)
        out.append(jnp.moveaxis(jnp.diagonal(blk, axis1=0, axis2=2), -1, 0))
    return jnp.concatenate(out, axis=0)


def _gate_part(pre, part):
    return jnp.concatenate([pre[:, (N_PARTS * h + part) * HALF:(N_PARTS * h + part + 1) * HALF] for h in range(2)],
                           axis=1)


def _gate_unpart(parts):
    return jnp.concatenate([parts[p][:, h * HALF:(h + 1) * HALF] for h in range(2) for p in range(N_PARTS)], axis=1)


GATE_BM = 768


def _gates_fwd(u, wg):
    rows = u.shape[0]

    def body(u_ref, w_ref, o_ref):
        o_ref[...] = jnp.dot(u_ref[...], w_ref[...], preferred_element_type=f32)

    return pl.pallas_call(
        body, name="l0_gates", grid=(rows // GATE_BM, 2 * N_PARTS),
        out_shape=jax.ShapeDtypeStruct((rows, 2 * N_PARTS * HALF), f32),
        in_specs=[pl.BlockSpec((GATE_BM, HALF), lambda i, j: (i, j // N_PARTS)),
                  pl.BlockSpec((HALF, HALF), lambda i, j: (0, j))],
        out_specs=pl.BlockSpec((GATE_BM, HALF), lambda i, j: (i, j)),
        compiler_params=_cparams("parallel", "parallel"),
    )(u, wg)


def _gates_dx(dpre, wg):
    rows = dpre.shape[0]

    def body(d_ref, w_ref, o_ref, acc_ref):
        p = pl.program_id(2)

        @pl.when(p == 0)
        def _():
            acc_ref[...] = jnp.zeros_like(acc_ref)

        acc_ref[...] += lax.dot_general(d_ref[...], w_ref[...], (((1,), (1,)), ((), ())), preferred_element_type=f32)

        @pl.when(p == N_PARTS - 1)
        def _():
            o_ref[...] = acc_ref[...]

    return pl.pallas_call(
        body, name="l0_gates_dx", grid=(rows // GATE_BM, 2, N_PARTS), out_shape=jax.ShapeDtypeStruct((rows, R), f32),
        in_specs=[pl.BlockSpec((GATE_BM, HALF), lambda i, h, p: (i, N_PARTS * h + p)),
                  pl.BlockSpec((HALF, HALF), lambda i, h, p: (0, N_PARTS * h + p))],
        out_specs=pl.BlockSpec((GATE_BM, HALF), lambda i, h, p: (i, h)),
        scratch_shapes=[pltpu.VMEM((GATE_BM, HALF), f32)],
        compiler_params=_cparams("parallel", "parallel", "arbitrary"),
    )(dpre, wg)


def _gates_dw(u, dpre):
    rows = u.shape[0]

    def body(u_ref, d_ref, o_ref):
        o_ref[...] = lax.dot_general(u_ref[...], d_ref[...], (((0,), (0,)), ((), ())), preferred_element_type=f32)

    return pl.pallas_call(
        body, name="l0_gates_dw", grid=(2 * N_PARTS,), out_shape=jax.ShapeDtypeStruct((HALF, 2 * N_PARTS * HALF), f32),
        in_specs=[pl.BlockSpec((rows, HALF), lambda j: (0, j // N_PARTS)), pl.BlockSpec((rows, HALF), lambda j: (0, j))],
        out_specs=pl.BlockSpec((HALF, HALF), lambda j: (0, j)), compiler_params=_cparams("parallel"),
    )(u, dpre)


N_SCAN_CHUNKS = T_ALL // SCAN_CHUNK
SCAN_FWD = lambda t: t
SCAN_FWD_BWD = lambda t: N_SCAN_CHUNKS - 1 - t
SCAN_REV = lambda t: jnp.where(t == 0, 0, N_SCAN_CHUNKS - t)
SCAN_REV_BWD = lambda t: jnp.where(t == N_SCAN_CHUNKS - 1, 0, t + 1)
CONV_SEGMENTS = ((0, T_CTX), (T_CTX, T_LAT))
TM = 128
N_CTX_TILES = T_CTX // TM


def _local_step(x, ctx, target, mods, cmod, wts, late_weights, send_grads):
    sh1, sc1, g1, sh2, sc2, g2 = [[mods[l, i][None] for l in range(2)] for i in range(N_MOD)]
    ng = wts["norm_g"]
    xcat = jnp.concatenate([ctx, x], axis=0)
    poscat = jnp.concatenate([jnp.zeros((T_CTX, D), f32), _pos_embed()], axis=0)
    scp = jnp.concatenate([cmod[1][None], sc1[0]], axis=0)
    shp = jnp.concatenate([cmod[0][None], sh1[0]], axis=0)

    def blend(i, p):
        sel = jnp.where(i < N_CTX_TILES, 1.0, 0.0)
        return sel * p[0:1] + (1.0 - sel) * p[1:2]

    def f_pre0(i, xc, pos, g, scp_, shp_):
        x0 = xc + pos
        return x0, _normmod(x0, g, blend(i, scp_), blend(i, shp_))

    x0cat, h0 = _rowcall(f_pre0, "l0_prenorm", T_ALL, TM, [_rin(xcat), _rin(poscat)], [ng[0, 0][None], scp, shp],
                         [(D, f32), (D, bf16)])
    gr = _mm(h0, wts["rec_w_in"], "l0_in_proj")
    u, ub = _dwconv_fwd(gr, R // 256, wts["rec_conv_w"], wts["rec_conv_b"], 4, 1, CONV_SEGMENTS, 256,
                        "l0_conv", True)
    pre = _gates_fwd(ub, wts["gates"])

    def f_coeff(i, pre_, u_, ba, bx, lam):
        outs = []
        for d in range(2):
            a, b = _coeff(_gate_part(pre_, 2 * d), _gate_part(pre_, 2 * d + 1), u_,
                          ba[d:d + 1], bx[d:d + 1], lam[d:d + 1])
            outs += [a, b]
        return tuple(outs)

    a0, b0, a1, b1 = _rowcall(f_coeff, "l0_coeff", T_ALL, TM, [_rin(pre), _rin(u)],
                              [wts["rec_b_a"], wts["rec_b_x"], wts["rec_lambda"]], [(R, f32)] * 4)
    y0, yp0 = _scan_call(a0, b0, SCAN_FWD, False, "l0_scan_fwd", False)
    y1, yp1 = _scan_call(a1, b1, SCAN_REV, True, "l0_scan_rev", False)

    def f_gate(i, gp, y0_, y1_):
        return (_gelu(gp) * (y0_ + y1_),)

    (zb,) = _rowcall(f_gate, "l0_gate", T_LAT, TM,
                     [_rin(gr, R, 0, N_CTX_TILES), _rin(y0, None, 0, N_CTX_TILES), _rin(y1, None, 0, N_CTX_TILES)],
                     [], [(R, bf16)])
    out0 = _mm(zb, wts["rec_w_out"], "l0_out_proj")

    zero_d = jnp.zeros((1, D), f32)

    def mlp_params(rows):
        rows = rows + [zero_d] * (N_MLP_PARAMS - len(rows))
        return jnp.concatenate([jnp.broadcast_to(r, (8, D)) for r in rows], axis=0)

    par0 = mlp_params([g1[0], zero_d, ng[0, 1][None], sc2[0], sh2[0], g2[0], ng[1, 0][None], sc1[1], sh1[1]])
    wts = dict(wts, **late_weights(out0))
    x1, h1, r0, mo0, x2, h2 = _mlp_forward(x0cat, T_CTX // MLP_TM, out0, par0, wts["mlp_w_in"], wts["mlp_w_out"], 0,
                                           "l0_mlp")

    pw = _mm(h2, wts["conf_w_pw1"], "l1_pw1")

    def f_glu(i, pa, pb, b1):
        return ((pa + b1[:, :D]) * _sigmoid(pb + b1[:, D:]),)

    (zg,) = _rowcall(f_glu, "l1_glu", T_LAT, TM, [_rin(pw, D, 0), _rin(pw, D, 1)], [wts["conf_b_pw1"]], [(D, f32)])
    (zc,) = _dwconv_fwd(zg, 0, wts["conf_conv_w"], wts["conf_conv_b"], 31, 15, ((0, T_LAT),), 128, "l1_conv", False)

    def ln_silu(z, lg, lb):
        mu = jnp.mean(z, axis=-1, keepdims=True)
        zc_ = z - mu
        var = jnp.mean(zc_ * zc_, axis=-1, keepdims=True)
        yv = zc_ * lax.rsqrt(var + EPS) * lg + lb
        return yv * _sigmoid(yv)

    def f_lnsilu(i, z, lg, lb):
        return (ln_silu(z, lg, lb),)

    (sb,) = _rowcall(f_lnsilu, "l1_ln_silu", T_LAT, TM, [_rin(zc)], [wts["conf_ln_g"], wts["conf_ln_b"]], [(D, bf16)])
    out1 = _mm(sb, wts["conf_w_pw2"], "l1_pw2")
    par1 = mlp_params([g1[1], wts["conf_b_pw2"], ng[1, 1][None], sc2[1], sh2[1], g2[1]])
    x3, h3, r1, mo1, x4, _ = _mlp_forward(x2, 0, out1, par1, wts["mlp_w_in"], wts["mlp_w_out"], 1, "l1_mlp")

    def loss_fn(x4_, fg, tgt):
        err = _rms(x4_, fg) - tgt
        per_row = jnp.mean(err * err, axis=-1, keepdims=True)
        return 0.5 * jnp.sum(per_row, axis=0, keepdims=True)

    def f_head(i, x4_, tgt, fg):
        loss, vjp = jax.vjp(lambda a, e: loss_fn(a, e, tgt), x4_, fg)
        dx, dfg = vjp(jnp.ones((1, 1), f32))
        return dx, jnp.broadcast_to(loss, (1, 128)), dfg

    dx4, loss_acc, dfinal_g = _rowcall(f_head, "head", T_LAT, TM, [_rin(x4), _rin(target)], [wts["final_g"]],
                                       [(D, f32)], [(1, 128), (1, D)])

    grads = {"final_g": dfinal_g}

    def normmod_bwd(xin, dh, dx_skip, g, sc, sh, tag):
        def fb(i, x_, dh_, dxs, g_, sc_, sh_):
            _, vjp = jax.vjp(_normmod, x_, g_, sc_, sh_)
            dx, dg, dsc, dsh = vjp(dh_)
            return dx + dxs, dg, dsc, dsh

        return _rowcall(fb, tag + "_normmod_bwd", T_LAT, TM, [_rin(xin), _rin(dh), _rin(dx_skip)], [g, sc, sh],
                        [(D, f32)], [(1, D)] * 3)

    dx3, dout1, dmo1, dhid1, acc1 = _mlp_backward(dx4, x3, r1, mo1, out1, par1, wts["mlp_w_in"], wts["mlp_w_out"], 1,
                                                  "l1_mlp_bwd")
    mlp_dw = _mlp_weight_grads(h3, dhid1, r1, dmo1, 1, None, "l1")
    dg1_1, db_pw2, dng11, dsc2_1, dsh2_1, dg2_1 = [acc1[k:k + 1] for k in range(6)]

    ds = _mm(dout1, wts["conf_w_pw2"], "l1_pw2_dx", tb=True)
    grads["conf_w_pw2"] = _mm(sb, dout1, "l1_pw2_dw", ta=True, out_dtype=bf16)
    grads["conf_b_pw2"] = db_pw2

    def f_lnsilu_bwd(i, z, ds_, lg, lb):
        _, vjp = jax.vjp(ln_silu, z, lg, lb)
        return vjp(ds_)

    dzc, dln_g, dln_b = _rowcall(f_lnsilu_bwd, "l1_ln_silu_bwd", T_LAT, TM, [_rin(zc), _rin(ds)],
                                 [wts["conf_ln_g"], wts["conf_ln_b"]], [(D, f32)], [(1, D)] * 2)
    grads["conf_ln_g"], grads["conf_ln_b"] = dln_g, dln_b
    dzg, dconv_w, dconv_b = _dwconv_bwd([dzc], zg, 0, wts["conf_conv_w"], 31, 15, ((0, T_LAT),), 128,
                                        "l1_conv_bwd", f32)
    grads["conf_conv_w"], grads["conf_conv_b"] = dconv_w, dconv_b

    def f_glu_bwd(i, pa, pb, dz, b1):
        _, vjp = jax.vjp(lambda a, b, c: (a + c[:, :D]) * _sigmoid(b + c[:, D:]), pa, pb, b1)
        da, db, dc = vjp(dz)
        return jnp.concatenate([da, db], axis=1), dc

    dpw, db_pw1 = _rowcall(f_glu_bwd, "l1_glu_bwd", T_LAT, TM, [_rin(pw, D, 0), _rin(pw, D, 1), _rin(dzg)],
                           [wts["conf_b_pw1"]], [(2 * D, bf16)], [(1, 2 * D)])
    grads["conf_b_pw1"] = db_pw1
    dh2 = _mm(dpw, wts["conf_w_pw1"], "l1_pw1_dx", tb=True)
    grads["conf_w_pw1"] = _mm(h2, dpw, "l1_pw1_dw", ta=True, out_dtype=bf16)
    send_grads(["conf_w_pw2", "conf_w_pw1"], grads)
    dx2, dng10, dsc1_1, dsh1_1 = normmod_bwd(x2, dh2, dx3, ng[1, 0][None], sc1[1], sh1[1], "l1a")

    dx1, dout0, dmo0, dhid0, acc0 = _mlp_backward(dx2, x1, r0, mo0, out0, par0, wts["mlp_w_in"], wts["mlp_w_out"], 0,
                                                  "l0_mlp_bwd")
    grads["mlp_w_in"], grads["mlp_w_out"] = _mlp_weight_grads(h1, dhid0, r0, dmo0, 0, mlp_dw, "l0")
    send_grads(["mlp_w_in", "mlp_w_out"], grads)
    dg1_0, _, dng01, dsc2_0, dsh2_0, dg2_0 = [acc0[k:k + 1] for k in range(6)]

    dz = _mm(dout0, wts["rec_w_out"], "l0_out_proj_dx", tb=True)
    grads["rec_w_out"] = _mm(zb, dout0, "l0_out_proj_dw", ta=True, out_dtype=bf16)
    send_grads(["rec_w_out"], grads)

    def f_gate_bwd(i, gp, y0_, y1_, dz_):
        lat = jnp.where(i < N_CTX_TILES, 0.0, 1.0)
        _, vjp = jax.vjp(lambda a, b: _gelu(a) * b, gp, y0_ + y1_)
        dgp, dy = vjp(dz_)
        return dgp * lat, dy * lat

    dgp, dy = _rowcall(f_gate_bwd, "l0_gate_bwd", T_ALL, TM,
                       [_rin(gr, R, 0), _rin(y0), _rin(y1), _rin(dz, None, 0, -N_CTX_TILES)], [],
                       [(R, bf16), (R, f32)])
    (dh_f,) = _scan_call(a0, dy, SCAN_FWD_BWD, True, "l0_scan_fwd_bwd", True)
    (dh_r,) = _scan_call(a1, dy, SCAN_REV_BWD, False, "l0_scan_rev_bwd", True)

    def f_coeff_bwd(i, pre_, u_, dhf, dhr, ypf, ypr, ba, bx, lam):
        dpre, dba, dbx, dlam = [], [], [], []
        du = jnp.zeros_like(u_)
        for d, (dh_, yp_) in enumerate(((dhf, ypf), (dhr, ypr))):
            _, vjp = jax.vjp(_coeff, _gate_part(pre_, 2 * d), _gate_part(pre_, 2 * d + 1), u_,
                             ba[d:d + 1], bx[d:d + 1], lam[d:d + 1])
            dpa, dpx, du_d, dba_d, dbx_d, dlam_d = vjp((dh_ * yp_, dh_))
            dpre += [dpa, dpx]
            du = du + du_d
            dba.append(dba_d)
            dbx.append(dbx_d)
            dlam.append(dlam_d)
        return _gate_unpart(dpre), du, _rows2(*dba), _rows2(*dbx), _rows2(*dlam)

    dpre, du_direct, db_a, db_x, dlam = _rowcall(
        f_coeff_bwd, "l0_coeff_bwd", T_ALL, 64,
        [_rin(pre), _rin(u), _rin(dh_f), _rin(dh_r), _rin(yp0), _rin(yp1)],
        [wts["rec_b_a"], wts["rec_b_x"], wts["rec_lambda"]], [(4 * R, bf16), (R, f32)], [(2, R)] * 3)
    grads["rec_b_a"], grads["rec_b_x"], grads["rec_lambda"] = db_a, db_x, dlam
    du_gates = _gates_dx(dpre, wts["gates"])
    grads["gates"] = _gates_dw(ub, dpre)
    drec, dconv4_w, dconv4_b = _dwconv_bwd([du_direct, du_gates], gr, R // 256, wts["rec_conv_w"], 4, 1,
                                           CONV_SEGMENTS, 256, "l0_conv_bwd", bf16)
    grads["rec_conv_w"], grads["rec_conv_b"] = dconv4_w, dconv4_b
    dgr = jnp.concatenate([dgp, drec], axis=1)
    dh0 = _mm(dgr, wts["rec_w_in"], "l0_in_proj_dx", tb=True)
    grads["rec_w_in"] = _mm(h0, dgr, "l0_in_proj_dw", ta=True, out_dtype=bf16)
    send_grads(["rec_w_in"], grads)

    def f_pre0_bwd(i, x0, dh_, dxs, g, scp_, shp_):
        lat = jnp.where(i < N_CTX_TILES, 0.0, 1.0)
        _, vjp = jax.vjp(lambda a, b, c, e: _normmod(a, b, blend(i, c), blend(i, e)), x0, g, scp_, shp_)
        dx, dg, dscp, dshp = vjp(dh_)
        return dx + lat * dxs, dg, dscp, dshp

    dx0cat, dng00, dscp, dshp = _rowcall(
        f_pre0_bwd, "l0_prenorm_bwd", T_ALL, TM, [_rin(x0cat), _rin(dh0), _rin(dx1, None, 0, -N_CTX_TILES)],
        [ng[0, 0][None], scp, shp], [(D, f32)], [(1, D), (2, D), (2, D)])

    grads["norm_g"] = jnp.stack([jnp.concatenate([dng00, dng01], 0), jnp.concatenate([dng10, dng11], 0)])
    dmods = jnp.stack([
        jnp.concatenate([dshp[1:2], dscp[1:2], dg1_0, dsh2_0, dsc2_0, dg2_0], axis=0),
        jnp.concatenate([dsh1_1, dsc1_1, dg1_1, dsh2_1, dsc2_1, dg2_1], axis=0)])
    dcmod = jnp.concatenate([dshp[0:1], dscp[0:1]], axis=0)
    return loss_acc[0, 0], dx0cat[T_CTX:], dmods, dcmod, grads


def _unshard_cols(g):
    g = jnp.moveaxis(g, 0, -2)
    return g.reshape(g.shape[:-2] + (g.shape[-2] * g.shape[-1],))


def _shard_cols(w):
    w = w.reshape(w.shape[:-1] + (N_DEV, w.shape[-1] // N_DEV))
    return jnp.moveaxis(w, -2, 0)


def _shard_rows(w):
    return w.reshape((N_DEV, w.shape[0] // N_DEV) + w.shape[1:])


SMALL_PACK_ROWS = 64


def kernel(x, c, ctx, c_ctx, w_ada, b_ada, norm_g, rec_w_in, rec_conv_w, rec_conv_b, rec_lambda, rec_w_a, rec_b_a, rec_w_x, rec_b_x, rec_w_out, conf_w_pw1, conf_b_pw1, conf_conv_w, conf_conv_b, conf_ln_g, conf_ln_b, conf_w_pw2, conf_b_pw2, mlp_w_in, mlp_w_out, final_g, loss_target, m_c_ctx, m_w_ada, m_b_ada, m_norm_g, m_rec_w_in, m_rec_conv_w, m_rec_conv_b, m_rec_lambda, m_rec_w_a, m_rec_b_a, m_rec_w_x, m_rec_b_x, m_rec_w_out, m_conf_w_pw1, m_conf_b_pw1, m_conf_conv_w, m_conf_conv_b, m_conf_ln_g, m_conf_ln_b, m_conf_w_pw2, m_conf_b_pw2, m_mlp_w_in, m_mlp_w_out, m_final_g, v_c_ctx, v_w_ada, v_b_ada, v_norm_g, v_rec_w_in, v_rec_conv_w, v_rec_conv_b, v_rec_lambda, v_rec_w_a, v_rec_b_a, v_rec_w_x, v_rec_b_x, v_rec_w_out, v_conf_w_pw1, v_conf_b_pw1, v_conf_conv_w, v_conf_conv_b, v_conf_ln_g, v_conf_ln_b, v_conf_w_pw2, v_conf_b_pw2, v_mlp_w_in, v_mlp_w_out, v_final_g):
    me = 4 * lax.axis_index("x") + 2 * lax.axis_index("y") + lax.axis_index("c")
    weights = dict(c_ctx=c_ctx, w_ada=w_ada, b_ada=b_ada, norm_g=norm_g, rec_w_in=rec_w_in, rec_conv_w=rec_conv_w,
                   rec_conv_b=rec_conv_b, rec_lambda=rec_lambda, rec_w_a=rec_w_a, rec_b_a=rec_b_a, rec_w_x=rec_w_x,
                   rec_b_x=rec_b_x, rec_w_out=rec_w_out, conf_w_pw1=conf_w_pw1, conf_b_pw1=conf_b_pw1,
                   conf_conv_w=conf_conv_w, conf_conv_b=conf_conv_b, conf_ln_g=conf_ln_g, conf_ln_b=conf_ln_b,
                   conf_w_pw2=conf_w_pw2, conf_b_pw2=conf_b_pw2, mlp_w_in=mlp_w_in, mlp_w_out=mlp_w_out, final_g=final_g)
    m_in = dict(c_ctx=m_c_ctx, w_ada=m_w_ada, b_ada=m_b_ada, norm_g=m_norm_g, rec_w_in=m_rec_w_in, rec_conv_w=m_rec_conv_w,
                rec_conv_b=m_rec_conv_b, rec_lambda=m_rec_lambda, rec_w_a=m_rec_w_a, rec_b_a=m_rec_b_a, rec_w_x=m_rec_w_x,
                rec_b_x=m_rec_b_x, rec_w_out=m_rec_w_out, conf_w_pw1=m_conf_w_pw1, conf_b_pw1=m_conf_b_pw1,
                conf_conv_w=m_conf_conv_w, conf_conv_b=m_conf_conv_b, conf_ln_g=m_conf_ln_g, conf_ln_b=m_conf_ln_b,
                conf_w_pw2=m_conf_w_pw2, conf_b_pw2=m_conf_b_pw2, mlp_w_in=m_mlp_w_in, mlp_w_out=m_mlp_w_out,
                final_g=m_final_g)
    v_in = dict(c_ctx=v_c_ctx, w_ada=v_w_ada, b_ada=v_b_ada, norm_g=v_norm_g, rec_w_in=v_rec_w_in, rec_conv_w=v_rec_conv_w,
                rec_conv_b=v_rec_conv_b, rec_lambda=v_rec_lambda, rec_w_a=v_rec_w_a, rec_b_a=v_rec_b_a, rec_w_x=v_rec_w_x,
                rec_b_x=v_rec_b_x, rec_w_out=v_rec_w_out, conf_w_pw1=v_conf_w_pw1, conf_b_pw1=v_conf_b_pw1,
                conf_conv_w=v_conf_conv_w, conf_conv_b=v_conf_conv_b, conf_ln_g=v_conf_ln_g, conf_ln_b=v_conf_ln_b,
                conf_w_pw2=v_conf_w_pw2, conf_b_pw2=v_conf_b_pw2, mlp_w_in=v_mlp_w_in, mlp_w_out=v_mlp_w_out,
                final_g=v_final_g)
    names = list(weights)

    small_items = [c, norm_g, rec_conv_w, rec_lambda, conf_b_pw1, conf_conv_w, conf_conv_b, conf_ln_g, conf_ln_b,
                   conf_b_pw2]
    flat = jnp.concatenate([a.reshape(-1) for a in small_items])
    flat = jnp.pad(flat, (0, SMALL_PACK_ROWS * 128 - flat.shape[0])).reshape(SMALL_PACK_ROWS, 128)
    (small_all,) = _all_gather([flat], "gather_small")

    small_all = small_all.reshape(N_DEV, -1)
    off = 0
    small = []
    for a in small_items:
        small.append(small_all[:, off:off + a.size].reshape((N_DEV,) + a.shape))
        off += a.size
    c_all, ng_all, rcw_all, lam_all, bpw1_all, ccw_all, ccb_all, lng_all, lnb_all, bpw2_all = small
    wts = {
        "norm_g": _unshard_cols(ng_all),
        "rec_conv_w": _unshard_cols(rcw_all)[0],
        "rec_lambda": _unshard_cols(lam_all)[0],
        "conf_b_pw1": _unshard_cols(bpw1_all),
        "conf_conv_w": _unshard_cols(ccw_all)[0],
        "conf_conv_b": _unshard_cols(ccb_all),
        "conf_ln_g": _unshard_cols(lng_all),
        "conf_ln_b": _unshard_cols(lnb_all),
        "conf_b_pw2": _unshard_cols(bpw2_all),
        "rec_conv_b": rec_conv_b,
        "rec_b_a": rec_b_a[0].reshape(2, R),
        "rec_b_x": rec_b_x[0].reshape(2, R),
        "final_g": final_g[None],
        "gates": _gate_matrix(rec_w_a[0], rec_w_x[0]),
    }

    c16 = jnp.concatenate([c_all[:, 0], jnp.broadcast_to(c_ctx[None], (8, D))], axis=0)
    b_loc = lax.dynamic_slice_in_dim(b_ada, me * ADA_SHARD, ADA_SHARD, axis=1)[:, None]
    (mods_all,) = _all_gather([_ada_forward(c16, w_ada, b_loc)], "gather_mods")
    mods_all = _unshard_cols(mods_all)
    mods = lax.dynamic_index_in_dim(mods_all, me, axis=1, keepdims=False).reshape(2, N_MOD, D)
    cmod = mods_all[0, 8, :2 * D].reshape(2, D)

    as_shard = lambda a: a.astype(bf16).reshape(-1, a.shape[-1])
    early = _all_gather_2level([as_shard(rec_w_in[0]), as_shard(rec_w_out[0])], "gather_weights_early")
    wts["rec_w_in"] = _unshard_cols(early[0])
    wts["rec_w_out"] = early[1].reshape(R, D)
    late_items = [conf_w_pw1[0], conf_w_pw2[0], mlp_w_in, mlp_w_out]
    late_shards = [as_shard(a) for a in late_items]
    late_handle, token = _exchange_start(late_shards, [_own_block_filled(s, me) for s in late_shards],
                                         "gather_weights_late_start", False)

    def late_weights(after):
        got = _exchange_wait(late_handle, after, "gather_weights_late_wait", False)
        got = [g.reshape((N_DEV,) + a.shape) for g, a in zip(got, late_items)]
        return {"conf_w_pw1": _unshard_cols(got[0]),
                "conf_w_pw2": got[1].reshape(D, D),
                "mlp_w_in": got[2],
                "mlp_w_out": got[3]}

    to_blocks = {"rec_w_in": _shard_cols, "conf_w_pw1": _shard_cols, "rec_w_out": _shard_rows, "conf_w_pw2": _shard_rows,
                 "mlp_w_in": lambda g: g, "mlp_w_out": lambda g: g}
    grad_handles = []

    def send_grads(group, grads):
        blocks = [to_blocks[n](grads[n]) for n in group]
        blocks = [g.reshape(N_DEV, -1, g.shape[-1]) for g in blocks]
        lands = [_own_block_filled(lax.dynamic_index_in_dim(g, me, 0, keepdims=False), me) for g in blocks]
        handle, _ = _exchange_start(blocks, lands, "scatter_start_" + group[0], True)
        grad_handles.append((group, handle))

    cmod = cmod + token[0:1, 0:1]
    loss_part, grad_x, dmods, dcmod, grads = _local_step(x[0], ctx[0], loss_target[0], mods, cmod, wts, late_weights,
                                                         send_grads)
    loss = lax.psum(loss_part, ("x", "y", "c"))

    dm_flat = jnp.concatenate([dmods.reshape(-1), dcmod.reshape(-1)]).reshape(-1, 128)
    (dm_all,) = _all_gather([dm_flat], "gather_dmods")
    dm_all = dm_all.reshape(N_DEV, -1)
    dmods_all = dm_all[:, :2 * N_MOD * D].reshape(N_DEV, 2, N_MOD * D)
    dcmod_all = jnp.pad(dm_all[:, 2 * N_MOD * D:], ((0, 0), (0, (N_MOD - 2) * D)))
    g16_full = jnp.stack([jnp.concatenate([dmods_all[:, 0], dcmod_all], axis=0),
                          jnp.concatenate([dmods_all[:, 1], jnp.zeros_like(dcmod_all)], axis=0)])
    g16 = lax.dynamic_slice_in_dim(g16_full, me * ADA_SHARD, ADA_SHARD, axis=2)
    dw_ada, ds_part = _ada_backward(c16, g16, w_ada)
    (ds_all,) = _all_gather([ds_part[0]], "gather_dsilu")

    big_names, big_pieces = [], []
    for group, handle in grad_handles:
        for n, got in zip(group, _exchange_wait(handle, grad_x, "scatter_wait_" + group[0], True)):
            big_names.append(n)
            big_pieces.append([(got, N_DEV)])
    small_sharded = ["norm_g", "rec_conv_w", "rec_lambda", "conf_b_pw1", "conf_conv_w", "conf_conv_b", "conf_ln_g",
                     "conf_ln_b", "conf_b_pw2"]
    pack = jnp.concatenate([_shard_cols(grads[n]).reshape(N_DEV, -1) for n in small_sharded], axis=1)
    pack_len = pack.shape[1]
    pack = jnp.pad(pack, ((0, 0), (0, SMALL_PACK_ROWS * 128 - pack_len))).reshape(N_DEV, SMALL_PACK_ROWS, 128)
    (pack_recv,) = _all_to_all([pack], "scatter_small_grads")
    pack_recv = pack_recv.reshape(N_DEV, -1)

    dwg = grads["gates"]
    repl = {"rec_conv_b": grads["rec_conv_b"],
            "rec_w_a": jnp.stack([_gate_blocks(dwg, 0), _gate_blocks(dwg, 2)])[None],
            "rec_w_x": jnp.stack([_gate_blocks(dwg, 1), _gate_blocks(dwg, 3)])[None],
            "rec_b_a": grads["rec_b_a"].reshape(1, 2, N_BLK, BLK),
            "rec_b_x": grads["rec_b_x"].reshape(1, 2, N_BLK, BLK),
            "final_g": grads["final_g"][0]}
    repl_names = list(repl)
    repl_flat = jnp.concatenate([repl[n].reshape(-1) for n in repl_names])
    repl_len = repl_flat.shape[0]
    repl_rows = -(-repl_len // (16 * D)) * 16
    repl_flat = jnp.pad(repl_flat, (0, repl_rows * D - repl_len)).reshape(repl_rows, D).astype(bf16)
    (repl_all,) = _all_gather_2level([repl_flat], "gather_replicated_grads")
    repl_all = repl_all.reshape(N_DEV, -1)

    def as2d(shape):
        rows = 1
        for s in shape[:-1]:
            rows *= s
        return (rows, shape[-1])

    def whole(arr, shape):
        arr = arr.reshape((-1,) + as2d(shape))
        return (arr, arr.shape[0])

    pieces = {}
    shard_shapes = {n: weights[n].shape for n in names}
    for n, parts in zip(big_names, big_pieces):
        pieces[n] = parts
    off = 0
    for n in small_sharded:
        size = weights[n].size
        pieces[n] = [whole(pack_recv[:, off:off + size], shard_shapes[n])]
        off += size
    off = 0
    for n in repl_names:
        size = weights[n].size
        pieces[n] = [whole(repl_all[:, off:off + size], shard_shapes[n])]
        off += size
    pieces["w_ada"] = [whole(dw_ada, shard_shapes["w_ada"])]
    db_terms = jnp.concatenate([dmods_all, jnp.stack([dcmod_all, jnp.zeros_like(dcmod_all)], axis=1)], axis=0)
    pieces["b_ada"] = [whole(db_terms, shard_shapes["b_ada"])]
    pieces["c_ctx"] = [whole(ds_all[:, 0], shard_shapes["c_ctx"])]

    g_out, d_out, m_out, v_out = {}, {}, {}, {}
    for n in names:
        shape = shard_shapes[n]
        r2, c2 = as2d(shape)
        p = pieces[n]
        g, dl, nm, nv = _adamw(p, weights[n].reshape(r2, c2), m_in[n].reshape(r2, c2), v_in[n].reshape(r2, c2),
                               "adamw_" + n)
        g_out[n], d_out[n], m_out[n], v_out[n] = (t.reshape(shape) for t in (g, dl, nm, nv))

    return (loss, grad_x[None], *[g_out[n] for n in names], *[d_out[n] for n in names],
            *[m_out[n] for n in names], *[v_out[n] for n in names])
```

```python
import functools

import jax
import jax.numpy as jnp
from jax import lax
from jax.experimental import pallas as pl
from jax.experimental.pallas import tpu as pltpu

f32 = jnp.float32
bf16 = jnp.bfloat16

N_DEV = 8
D = 1024
T_LAT = 2048
T_CTX = 256
T_ALL = T_CTX + T_LAT
R = 1280
N_BLK = 16
BLK = R // N_BLK
F = 4096
GRID_W = 64
RG_C = 8.0
EPS = 1e-6
POS_BASE = 10000.0
N_MOD = 6
ADA_SHARD = N_MOD * D // N_DEV

ADAM_LR = 0.001
ADAM_B1 = 0.9
ADAM_B2 = 0.999
ADAM_EPS = 1e-08
ADAM_WD = 0.01
ADAM_STEP = 10

VMEM_LIMIT_V7X = 56 * 1024 * 1024
HALO = 16
MESH = pl.DeviceIdType.MESH


def _cparams(*sem):
    return pltpu.CompilerParams(dimension_semantics=sem, vmem_limit_bytes=VMEM_LIMIT_V7X)


def _pick(n, cands):
    for c in cands:
        if n % c == 0:
            return c
    raise ValueError(f"no block size for {n}")


def _position():
    x, y, c = lax.axis_index("x"), lax.axis_index("y"), lax.axis_index("c")
    return x, y, c, 4 * x + 2 * y + c


def _peer(x, y, c, k):
    px = (1 - x) if (k >> 2) & 1 else x
    py = (1 - y) if (k >> 1) & 1 else y
    pc = (1 - c) if k & 1 else c
    return (px, py, pc), 4 * px + 2 * py + pc


def _exchange(arrs, name, scatter):
    n = len(arrs)

    def body(*refs):
        ins, outs = refs[:n], refs[n:2 * n]
        send_sems, recv_sems, local_sems = refs[2 * n:]
        x, y, c, me = _position()
        local = []
        for a in range(n):
            src = ins[a].at[me] if scatter else ins[a]
            cp = pltpu.make_async_copy(src, outs[a].at[me], local_sems.at[a])
            cp.start()
            local.append(cp)
        sends, recvs = [], []
        for a in range(n):
            for k in range(1, N_DEV):
                peer, peer_lin = _peer(x, y, c, k)
                src = ins[a].at[peer_lin] if scatter else ins[a]
                cp = pltpu.make_async_remote_copy(
                    src_ref=src, dst_ref=outs[a].at[me], send_sem=send_sems.at[a, k - 1],
                    recv_sem=recv_sems.at[a, k - 1], device_id=peer, device_id_type=MESH)
                cp.start()
                sends.append(cp)
                recvs.append(pltpu.make_async_remote_copy(
                    src_ref=src, dst_ref=outs[a].at[peer_lin], send_sem=send_sems.at[a, k - 1],
                    recv_sem=recv_sems.at[a, k - 1], device_id=peer, device_id_type=MESH))
        for cp in recvs:
            cp.wait_recv()
        for cp in sends:
            cp.wait_send()
        for cp in local:
            cp.wait()

    if scatter:
        out_shape = [jax.ShapeDtypeStruct(a.shape, a.dtype) for a in arrs]
    else:
        out_shape = [jax.ShapeDtypeStruct((N_DEV,) + a.shape, a.dtype) for a in arrs]
    any_spec = pl.BlockSpec(memory_space=pl.ANY)
    return pl.pallas_call(
        body, name=name, out_shape=out_shape,
        in_specs=[any_spec] * n, out_specs=[any_spec] * n,
        scratch_shapes=[pltpu.SemaphoreType.DMA((n, N_DEV - 1)), pltpu.SemaphoreType.DMA((n, N_DEV - 1)),
                        pltpu.SemaphoreType.DMA((n,))],
    )(*arrs)


def _all_gather(arrs, name):
    return _exchange(arrs, name, scatter=False)


def _all_to_all(arrs, name):
    return _exchange(arrs, name, scatter=True)


def _lin(p):
    return 4 * p[0] + 2 * p[1] + p[2]


HBM_SPEC = pl.BlockSpec(memory_space=pltpu.HBM)
SEM_SPEC = pl.BlockSpec(memory_space=pltpu.SEMAPHORE)
DATAFLOW_EFFECT = pltpu.SideEffectType.DATAFLOW_SIDE_EFFECTING


def _split_copies(srcs, lands, send_sems, recv_sems, scatter):
    x, y, c, me = _position()
    out = []
    for a in range(len(srcs)):
        for k in range(1, N_DEV):
            peer, peer_lin = _peer(x, y, c, k)
            src = srcs[a].at[peer_lin] if scatter else srcs[a]
            mk = lambda slot: pltpu.make_async_remote_copy(
                src_ref=src, dst_ref=lands[a].at[slot], send_sem=send_sems.at[a * (N_DEV - 1) + k - 1],
                recv_sem=recv_sems.at[a * (N_DEV - 1) + k - 1], device_id=peer, device_id_type=MESH)
            out.append((mk(me), mk(peer_lin)))
    return out


def _exchange_start(srcs, lands, name, scatter, after=()):
    n = len(srcs)
    n_after = len(after)

    def body(*refs):
        srcs_r, lands_r = refs[:n], refs[n:2 * n]
        send_sems, recv_sems = refs[2 * n + n_after], refs[2 * n + n_after + 1]
        token = refs[-1]
        for outgoing, _ in _split_copies(srcs_r, lands_r, send_sems, recv_sems, scatter):
            outgoing.start()
        token[...] = jnp.zeros_like(token)

    hbm = lambda a: pltpu.HBM(a.shape, a.dtype)
    res = pl.pallas_call(
        body, name=name,
        out_shape=(pltpu.SemaphoreType.DMA((n * (N_DEV - 1),)), pltpu.SemaphoreType.DMA((n * (N_DEV - 1),)),
                   *[hbm(a) for a in srcs], *[hbm(a) for a in lands], jax.ShapeDtypeStruct((8, 128), f32)),
        in_specs=[HBM_SPEC] * (2 * n) + [pl.BlockSpec(memory_space=pl.ANY)] * n_after,
        out_specs=(SEM_SPEC, SEM_SPEC, *[HBM_SPEC] * (2 * n), pl.BlockSpec(memory_space=pltpu.VMEM)),
        input_output_aliases={i: 2 + i for i in range(2 * n)},
        compiler_params=pltpu.CompilerParams(has_side_effects=DATAFLOW_EFFECT),
    )(*[pltpu.with_memory_space_constraint(a, pltpu.HBM) for a in list(srcs) + list(lands)], *after)
    return (res[0], res[1], list(res[2:2 + n]), list(res[2 + n:2 + 2 * n])), res[-1]


def _exchange_wait(handle, after, name, scatter):
    send_sems, recv_sems, srcs, lands = handle
    n = len(srcs)

    def body(*refs):
        srcs_r, lands_r = refs[:n], refs[n:2 * n]
        send_s, recv_s = refs[2 * n], refs[2 * n + 1]
        for outgoing, incoming in _split_copies(srcs_r, lands_r, send_s, recv_s, scatter):
            outgoing.wait_send()
            incoming.wait_recv()

    hbm = lambda a: pltpu.HBM(a.shape, a.dtype)
    res = pl.pallas_call(
        body, name=name, out_shape=tuple(hbm(a) for a in list(srcs) + list(lands)),
        in_specs=[HBM_SPEC] * (2 * n) + [SEM_SPEC, SEM_SPEC, pl.BlockSpec(memory_space=pl.ANY)],
        out_specs=tuple([HBM_SPEC] * (2 * n)),
        input_output_aliases={i: i for i in range(2 * n)},
        compiler_params=pltpu.CompilerParams(has_side_effects=DATAFLOW_EFFECT),
    )(*srcs, *lands, send_sems, recv_sems, after)
    return list(res[n:])


def _own_block_filled(block, me):
    land = lax.empty((N_DEV,) + block.shape, block.dtype)
    return lax.dynamic_update_index_in_dim(land, block, me, 0)


def _staged_copy(src, dst, buf, in_sems, out_sems, rows, chunk):
    n = rows // chunk

    def rd(i):
        return pltpu.make_async_copy(src.at[pl.ds(i * chunk, chunk)], buf.at[i % 2], in_sems.at[i % 2])

    def wr(i):
        return pltpu.make_async_copy(buf.at[i % 2], dst.at[pl.ds(i * chunk, chunk)], out_sems.at[i % 2])

    rd(0).start()
    for i in range(n):
        if i + 1 < n:
            if i >= 1:
                wr(i - 1).wait()
            rd(i + 1).start()
        rd(i).wait()
        wr(i).start()
    for i in range(max(n - 2, 0), n):
        wr(i).wait()


def _all_gather_2level(shards, name):
    n = len(shards)
    chunks = [_pick(s.shape[0], (416, 512, 256, 160, 128, 64, 16)) for s in shards]

    def body(*refs):
        ins, outs = refs[:n], refs[n:2 * n]
        send_sems, recv_sems, in_sems, out_sems = refs[2 * n:2 * n + 4]
        bufs = refs[2 * n + 4:]
        x, y, c, me = _position()
        sib, xn, yn, dg = (x, y, 1 - c), (1 - x, y, c), (x, 1 - y, c), (1 - x, 1 - y, c)

        def cp(a, k, src, slot, to):
            return pltpu.make_async_remote_copy(src_ref=src, dst_ref=outs[a].at[slot], send_sem=send_sems.at[a, k],
                                                recv_sem=recv_sems.at[a, k], device_id=to, device_id_type=MESH)

        for a in range(n):
            for k, to in ((0, sib), (1, xn), (2, yn)):
                cp(a, k, ins[a], me, to).start()
        for a in range(n):
            cp(a, 1, ins[a], _lin(xn), xn).wait_recv()
            cp(a, 3, outs[a].at[_lin(xn)], _lin(xn), sib).start()

            @pl.when(c == 0)
            def _():
                cp(a, 5, outs[a].at[_lin(xn)], _lin(xn), yn).start()

            cp(a, 2, ins[a], _lin(yn), yn).wait_recv()
            cp(a, 4, outs[a].at[_lin(yn)], _lin(yn), sib).start()

            @pl.when(c == 1)
            def _():
                cp(a, 5, outs[a].at[_lin(yn)], _lin(yn), xn).start()

        for a in range(n):
            cp(a, 5, ins[a], _lin(dg), xn).wait_recv()
            cp(a, 6, outs[a].at[_lin(dg)], _lin(dg), sib).start()
        for a in range(n):
            _staged_copy(ins[a], outs[a].at[me], bufs[a], in_sems.at[a], out_sems.at[a], shards[a].shape[0], chunks[a])
        for a in range(n):
            for k, origin in ((0, sib), (3, (1 - x, y, 1 - c)), (4, (x, 1 - y, 1 - c)), (6, (1 - x, 1 - y, 1 - c))):
                cp(a, k, ins[a], _lin(origin), sib).wait_recv()
            for k in range(7):
                cp(a, k, ins[a], me, sib).wait_send()

    any_spec = pl.BlockSpec(memory_space=pl.ANY)
    return pl.pallas_call(
        body, name=name, out_shape=[jax.ShapeDtypeStruct((N_DEV,) + s.shape, s.dtype) for s in shards],
        in_specs=[any_spec] * n, out_specs=[any_spec] * n,
        scratch_shapes=[pltpu.SemaphoreType.DMA((n, 7)), pltpu.SemaphoreType.DMA((n, 7)),
                        pltpu.SemaphoreType.DMA((n, 2)), pltpu.SemaphoreType.DMA((n, 2))]
        + [pltpu.VMEM((2, ch, s.shape[1]), s.dtype) for ch, s in zip(chunks, shards)],
    )(*shards)


def _plane_pos(x, y, q):
    return ((1 - x) if q & 2 else x, (1 - y) if q & 1 else y)


def _scatter_call(body, name, ins, out_shape, sems_per_array):
    n = len(ins)
    any_spec = pl.BlockSpec(memory_space=pl.ANY)
    return pl.pallas_call(
        body, name=name, out_shape=out_shape, in_specs=[any_spec] * n, out_specs=[any_spec] * n,
        scratch_shapes=[pltpu.SemaphoreType.DMA((n, sems_per_array)), pltpu.SemaphoreType.DMA((n, sems_per_array))],
    )(*ins)


def _scatter_d2d(gs, name):
    n = len(gs)

    def body(*refs):
        g_refs, recv_refs, send_sems, recv_sems = refs[:n], refs[n:2 * n], refs[2 * n], refs[2 * n + 1]
        x, y, c, me = _position()
        sib = (x, y, 1 - c)
        sends = []
        for a in range(n):
            for q in range(4):
                px, py = _plane_pos(x, y, q)
                cp = pltpu.make_async_remote_copy(
                    src_ref=g_refs[a].at[_lin((px, py, 1 - c))], dst_ref=recv_refs[a].at[q],
                    send_sem=send_sems.at[a, q], recv_sem=recv_sems.at[a, q], device_id=sib, device_id_type=MESH)
                cp.start()
                sends.append(cp)
        for cp in sends:
            cp.wait_recv()
        for cp in sends:
            cp.wait_send()

    return _scatter_call(body, name, gs, [jax.ShapeDtypeStruct((4,) + g.shape[1:], g.dtype) for g in gs], 4)


def _scatter_ici_first(hs, name):
    n = len(hs)

    def body(*refs):
        h_refs, recv_refs, send_sems, recv_sems = refs[:n], refs[n:2 * n], refs[2 * n], refs[2 * n + 1]
        x, y, c, me = _position()
        xn, yn = (1 - x, y, c), (x, 1 - y, c)

        def cp(a, k, q, to):
            return pltpu.make_async_remote_copy(
                src_ref=h_refs[a].at[q], dst_ref=recv_refs[a].at[k], send_sem=send_sems.at[a, k],
                recv_sem=recv_sems.at[a, k], device_id=to, device_id_type=MESH)

        @pl.when(c == 0)
        def _():
            for a in range(n):
                cp(a, 0, 2, xn).start()
                cp(a, 1, 3, xn).start()

        @pl.when(c == 1)
        def _():
            for a in range(n):
                cp(a, 0, 1, yn).start()
                cp(a, 1, 3, yn).start()

        for a in range(n):
            for k in range(2):
                cp(a, k, 0, xn).wait_recv()
        for a in range(n):
            for k in range(2):
                cp(a, k, 0, xn).wait_send()

    return _scatter_call(body, name, hs, [jax.ShapeDtypeStruct((2,) + h.shape[1:], h.dtype) for h in hs], 2)


def _scatter_ici_second(k1s, name):
    n = len(k1s)

    def body(*refs):
        k_refs, recv_refs, send_sems, recv_sems = refs[:n], refs[n:2 * n], refs[2 * n], refs[2 * n + 1]
        x, y, c, me = _position()
        xn, yn = (1 - x, y, c), (x, 1 - y, c)

        def cp(a, to):
            return pltpu.make_async_remote_copy(src_ref=k_refs[a], dst_ref=recv_refs[a], send_sem=send_sems.at[a, 0],
                                                recv_sem=recv_sems.at[a, 0], device_id=to, device_id_type=MESH)

        @pl.when(c == 0)
        def _():
            for a in range(n):
                cp(a, yn).start()

        @pl.when(c == 1)
        def _():
            for a in range(n):
                cp(a, xn).start()

        for a in range(n):
            cp(a, xn).wait_recv()
        for a in range(n):
            cp(a, xn).wait_send()

    return _scatter_call(body, name, k1s, [jax.ShapeDtypeStruct(k.shape, k.dtype) for k in k1s], 1)


def _add_blocks(a, a_idx, b, b_idx, out_dtype, name):
    rows, cols = a.shape[1:]
    n = a_idx.shape[0]
    tm = _pick(rows, (512, 256, 160, 128, 32, 16))

    def body(ia_ref, ib_ref, a_ref, b_ref, o_ref):
        o_ref[...] = (a_ref[...].astype(f32) + b_ref[...].astype(f32)).astype(o_ref.dtype)

    grid_spec = pltpu.PrefetchScalarGridSpec(
        num_scalar_prefetch=2, grid=(n, rows // tm),
        in_specs=[pl.BlockSpec((None, tm, cols), lambda j, i, ia, ib: (ia[j], i, 0)),
                  pl.BlockSpec((None, tm, cols), lambda j, i, ia, ib: (ib[j], i, 0))],
        out_specs=pl.BlockSpec((None, tm, cols), lambda j, i, ia, ib: (j, i, 0)))
    return pl.pallas_call(body, name=name, out_shape=jax.ShapeDtypeStruct((n, rows, cols), out_dtype),
                          grid_spec=grid_spec, compiler_params=_cparams("parallel", "parallel"))(a_idx, b_idx, a, b)


def _reduce_scatter(gs, names, tag):
    x, y, c, me = _position()
    i32 = lambda *v: jnp.stack([jnp.asarray(t, jnp.int32) for t in v])
    recvs = _scatter_d2d(gs, tag + "_d2d")
    own_idx = i32(*[_lin(_plane_pos(x, y, q) + (c,)) for q in range(4)])
    hs = [_add_blocks(g, own_idx, r, i32(0, 1, 2, 3), bf16, f"{tag}_add_chip_{nm}")
          for g, r, nm in zip(gs, recvs, names)]
    recv2s = _scatter_ici_first(hs, tag + "_ici_first")
    k1s = [_add_blocks(h, i32(1 + c), r2, i32(1), bf16, f"{tag}_add_onward_{nm}")[0]
           for h, r2, nm in zip(hs, recv2s, names)]
    lasts = _scatter_ici_second(k1s, tag + "_ici_second")
    return [[(h, 1), (r2, 1), (last[None], 1)] for h, r2, last in zip(hs, recv2s, lasts)]


ANY_SPEC = pl.BlockSpec(memory_space=pl.ANY)


def _mm(a, b, name, ta=False, tb=False, out_dtype=f32, after=()):
    if ta:
        k_dim, m_dim = a.shape
    else:
        m_dim, k_dim = a.shape
    if tb:
        n_dim, k2 = b.shape
    else:
        k2, n_dim = b.shape
    assert k_dim == k2, (a.shape, b.shape)
    assert a.dtype == bf16 and b.dtype == bf16
    bm = _pick(m_dim, (512, 768, 640, 256, 128))
    bn = _pick(n_dim, (512, 640, 256, 128))
    bk = _pick(k_dim, (1024, 1280, 768, 512))
    nk = k_dim // bk
    a_spec = (pl.BlockSpec((bk, bm), lambda i, j, k: (k, i)) if ta
              else pl.BlockSpec((bm, bk), lambda i, j, k: (i, k)))
    b_spec = (pl.BlockSpec((bn, bk), lambda i, j, k: (j, k)) if tb
              else pl.BlockSpec((bk, bn), lambda i, j, k: (k, j)))
    dims = (((0 if ta else 1,), (1 if tb else 0,)), ((), ()))

    n_after = len(after)

    def body_single(a_ref, b_ref, *rest):
        o_ref = rest[n_after]
        o_ref[...] = lax.dot_general(a_ref[...], b_ref[...], dims, preferred_element_type=f32).astype(o_ref.dtype)

    def body(a_ref, b_ref, *rest):
        o_ref, acc_ref = rest[n_after:]
        k = pl.program_id(2)

        @pl.when(k == 0)
        def _():
            acc_ref[...] = jnp.zeros_like(acc_ref)

        acc_ref[...] += lax.dot_general(a_ref[...], b_ref[...], dims, preferred_element_type=f32)

        @pl.when(k == nk - 1)
        def _():
            o_ref[...] = acc_ref[...].astype(o_ref.dtype)

    return pl.pallas_call(
        body_single if nk == 1 else body, name=name, out_shape=jax.ShapeDtypeStruct((m_dim, n_dim), out_dtype),
        grid=(m_dim // bm, n_dim // bn, nk), in_specs=[a_spec, b_spec] + [ANY_SPEC] * n_after,
        out_specs=pl.BlockSpec((bm, bn), lambda i, j, k: (i, j)),
        scratch_shapes=[] if nk == 1 else [pltpu.VMEM((bm, bn), f32)],
        compiler_params=_cparams("parallel", "parallel", "arbitrary"),
    )(a, b, *after)


def _rin(arr, width=None, cb=0, roff=0):
    return (arr, arr.shape[1] if width is None else width, cb, roff)


def _rowcall(fn, name, rows, tm, row_ins, par_ins, row_outs, acc_outs=(), after=()):
    nr, npar, nro, n_after = len(row_ins), len(par_ins), len(row_outs), len(after)
    in_specs, args = [], []
    for arr, width, cb, roff in row_ins:
        if roff >= 0:
            imap = lambda i, cb=cb, roff=roff: (i + roff, cb)
        else:
            imap = lambda i, cb=cb, roff=roff: (jnp.maximum(i + roff, 0), cb)
        in_specs.append(pl.BlockSpec((tm, width), imap))
        args.append(arr)
    for p in par_ins:
        in_specs.append(pl.BlockSpec(p.shape, lambda i: (0, 0)))
        args.append(p)
    out_shape, out_specs = [], []
    for width, dt in row_outs:
        out_shape.append(jax.ShapeDtypeStruct((rows, width), dt))
        out_specs.append(pl.BlockSpec((tm, width), lambda i: (i, 0)))
    for p, width in acc_outs:
        out_shape.append(jax.ShapeDtypeStruct((p, width), f32))
        out_specs.append(pl.BlockSpec((p, width), lambda i: (0, 0)))

    def body(*refs):
        i = pl.program_id(0)
        res = fn(i, *[r[...] for r in refs[:nr + npar]])
        outs = refs[nr + npar + n_after:]
        for o, v in zip(outs[:nro], res[:nro]):
            o[...] = v.astype(o.dtype)
        if acc_outs:
            @pl.when(i == 0)
            def _():
                for o in outs[nro:]:
                    o[...] = jnp.zeros_like(o)

            for o, v in zip(outs[nro:], res[nro:]):
                o[...] += v

    return pl.pallas_call(
        body, name=name, out_shape=out_shape, grid=(rows // tm,), in_specs=in_specs + [ANY_SPEC] * n_after,
        out_specs=out_specs, compiler_params=_cparams("arbitrary"),
    )(*args, *after)


def _rms(x, g):
    return x * lax.rsqrt(jnp.mean(x * x, axis=-1, keepdims=True) + EPS) * g


def _normmod(x, g, sc, sh):
    return _rms(x, g) * (1.0 + sc) + sh


def _rows2(v0, v1):
    rid = lax.broadcasted_iota(jnp.int32, (2, v0.shape[1]), 0)
    return jnp.where(rid == 0, v0, v1)


def _gelu(x):
    return 0.5 * x * (1.0 + jnp.tanh(0.7978845608028654 * (x + 0.044715 * (x * x * x))))


def _sigmoid(x):
    return 1.0 / (1.0 + jnp.exp(-x))


def _coeff(pre_a, pre_x, u, ba, bx, lam):
    r = _sigmoid(pre_a + ba)
    ig = _sigmoid(pre_x + bx)
    nl = -lam
    sp = jnp.maximum(nl, 0.0) + jnp.log(1.0 + jnp.exp(-jnp.abs(nl)))
    la = -RG_C * r * sp
    a = jnp.exp(la)
    one_minus_a2 = -jnp.tanh(la) * (a * a + 1.0)
    return a, jnp.sqrt(one_minus_a2) * (ig * u)


SCAN_CHUNK = 256


def _scan_call(a, v, chunk_of, reverse, name, backward):
    rows, width = a.shape
    n_out = 1 if backward else 2
    nt = SCAN_CHUNK // 8

    def body(a_ref, v_ref, *rest):
        outs, state_ref = rest[:-1], rest[-1]

        @pl.when(pl.program_id(0) == 0)
        def _():
            state_ref[...] = jnp.zeros_like(state_ref)

        rid = lax.broadcasted_iota(jnp.int32, (8, width), 0)

        def tile(j, st):
            t0 = pl.multiple_of((nt - 1 - j if reverse else j) * 8, 8)
            at = a_ref[pl.ds(t0, 8), :]
            vt = v_ref[pl.ds(t0, 8), :]
            out = jnp.zeros((8, width), f32)
            prev = jnp.zeros((8, width), f32)
            for i in (range(7, -1, -1) if reverse else range(8)):
                if backward:
                    g = vt[i:i + 1] + st
                    st = at[i:i + 1] * g
                    out = jnp.where(rid == i, g, out)
                else:
                    prev = jnp.where(rid == i, st, prev)
                    st = at[i:i + 1] * st + vt[i:i + 1]
                    out = jnp.where(rid == i, st, out)
            outs[0][pl.ds(t0, 8), :] = out
            if not backward:
                outs[1][pl.ds(t0, 8), :] = prev
            return st

        state_ref[0:1, :] = lax.fori_loop(0, nt, tile, state_ref[0:1, :])

    spec = pl.BlockSpec((SCAN_CHUNK, width), lambda t: (chunk_of(t), 0))
    return pl.pallas_call(
        body, name=name, out_shape=[jax.ShapeDtypeStruct((rows, width), f32)] * n_out,
        grid=(rows // SCAN_CHUNK,), in_specs=[spec, spec], out_specs=[spec] * n_out,
        scratch_shapes=[pltpu.VMEM((8, width), f32)],
        compiler_params=_cparams("arbitrary"),
    )(a, v)


CONV_CHUNK = 256


def _fill_padded(pad_ref, src_ref, start, n):
    cb = pad_ref.shape[1]
    pad_ref[pl.ds(0, HALO), :] = jnp.zeros((HALO, cb), f32)
    pad_ref[pl.ds(HALO, n), :] = src_ref[pl.ds(start, n), :].astype(f32)
    pad_ref[pl.ds(HALO + n, HALO), :] = jnp.zeros((HALO, cb), f32)


def _dwconv_fwd(x, x_cb0, w, b, taps, pad_left, segments, cb, name, emit_bf16):
    rows = x.shape[0]
    width = w.shape[1]

    def body(x_ref, w_ref, b_ref, *rest):
        outs, xp = rest[:-1], rest[-1]
        for start, n in segments:
            _fill_padded(xp, x_ref, start, n)
            for c0 in range(0, n, CONV_CHUNK):
                acc = jnp.zeros((CONV_CHUNK, cb), f32) + b_ref[...]
                for k in range(taps):
                    acc = acc + w_ref[k:k + 1, :] * xp[pl.ds(HALO + c0 + k - pad_left, CONV_CHUNK), :]
                for o in outs:
                    o[pl.ds(start + c0, CONV_CHUNK), :] = acc.astype(o.dtype)

    out_dtypes = [f32, bf16] if emit_bf16 else [f32]
    return pl.pallas_call(
        body, name=name, out_shape=[jax.ShapeDtypeStruct((rows, width), dt) for dt in out_dtypes],
        grid=(width // cb,),
        in_specs=[pl.BlockSpec((rows, cb), lambda j: (0, j + x_cb0)), pl.BlockSpec((taps, cb), lambda j: (0, j)),
                  pl.BlockSpec((1, cb), lambda j: (0, j))],
        out_specs=[pl.BlockSpec((rows, cb), lambda j: (0, j))] * len(out_dtypes),
        scratch_shapes=[pltpu.VMEM((rows + 2 * HALO, cb), f32)],
        compiler_params=_cparams("parallel"),
    )(x, w, b)


def _dwconv_bwd(douts, x, x_cb0, w, taps, pad_left, segments, cb, name, dx_dtype):
    rows = x.shape[0]
    width = w.shape[1]
    nd = len(douts)

    def body(*refs):
        d_refs, x_ref, w_ref = refs[:nd], refs[nd], refs[nd + 1]
        dx_ref, dw_ref, db_ref, xp, dp, dsum = refs[nd + 2:]
        dw_ref[...] = jnp.zeros_like(dw_ref)
        db_ref[...] = jnp.zeros_like(db_ref)
        if nd > 1:
            total = d_refs[0][...]
            for r in d_refs[1:]:
                total = total + r[...]
            dsum[...] = total
            d_ref = dsum
        else:
            d_ref = d_refs[0]
        for start, n in segments:
            _fill_padded(xp, x_ref, start, n)
            _fill_padded(dp, d_ref, start, n)
            for c0 in range(0, n, CONV_CHUNK):
                dchunk = dp[pl.ds(HALO + c0, CONV_CHUNK), :]
                db_ref[...] += jnp.sum(dchunk, axis=0, keepdims=True)
                acc = jnp.zeros((CONV_CHUNK, cb), f32)
                for k in range(taps):
                    acc = acc + w_ref[k:k + 1, :] * dp[pl.ds(HALO + c0 + pad_left - k, CONV_CHUNK), :]
                    xs = xp[pl.ds(HALO + c0 + k - pad_left, CONV_CHUNK), :]
                    dw_ref[k:k + 1, :] += jnp.sum(dchunk * xs, axis=0, keepdims=True)
                dx_ref[pl.ds(start + c0, CONV_CHUNK), :] = acc.astype(dx_ref.dtype)

    dspec = pl.BlockSpec((rows, cb), lambda j: (0, j))
    return pl.pallas_call(
        body, name=name,
        out_shape=[jax.ShapeDtypeStruct((rows, width), dx_dtype), jax.ShapeDtypeStruct((taps, width), f32),
                   jax.ShapeDtypeStruct((1, width), f32)],
        grid=(width // cb,),
        in_specs=[dspec] * nd + [pl.BlockSpec((rows, cb), lambda j: (0, j + x_cb0)),
                                 pl.BlockSpec((taps, cb), lambda j: (0, j))],
        out_specs=[dspec, pl.BlockSpec((taps, cb), lambda j: (0, j)), pl.BlockSpec((1, cb), lambda j: (0, j))],
        scratch_shapes=[pltpu.VMEM((rows + 2 * HALO, cb), f32), pltpu.VMEM((rows + 2 * HALO, cb), f32),
                        pltpu.VMEM((rows, cb), f32)],
        compiler_params=_cparams("parallel"),
    )(*douts, x, w)


def _ada_forward(c16, w_ada, b_loc):
    def body(c_ref, w_ref, b_ref, o_ref):
        cv = c_ref[...]
        s = (cv * _sigmoid(cv)).astype(bf16)
        o_ref[0] = jnp.dot(s, w_ref[0].astype(bf16), preferred_element_type=f32) + b_ref[0]

    return pl.pallas_call(
        body, name="ada_forward", out_shape=jax.ShapeDtypeStruct((2, 16, ADA_SHARD), f32), grid=(2,),
        in_specs=[pl.BlockSpec((16, D), lambda l: (0, 0)), pl.BlockSpec((1, D, ADA_SHARD), lambda l: (l, 0, 0)),
                  pl.BlockSpec((1, 1, ADA_SHARD), lambda l: (l, 0, 0))],
        out_specs=pl.BlockSpec((1, 16, ADA_SHARD), lambda l: (l, 0, 0)),
        compiler_params=_cparams("parallel"),
    )(c16, w_ada, b_loc)


def _ada_backward(c16, g16, w_ada):
    def body(c_ref, g_ref, w_ref, dw_ref, ds_ref):
        cv = c_ref[...]
        s = (cv * _sigmoid(cv)).astype(bf16)
        g = g_ref[0].astype(bf16)
        dw_ref[0] = lax.dot_general(s, g, (((0,), (0,)), ((), ())), preferred_element_type=f32)
        ds = lax.dot_general(g, w_ref[0].astype(bf16), (((1,), (1,)), ((), ())), preferred_element_type=f32)
        cc = cv[8:9]
        sg = _sigmoid(cc)
        dsilu = sg * (1.0 + cc * (1.0 - sg))
        ds_ref[0] = jnp.zeros((8, D), f32) + jnp.sum(ds[8:16], axis=0, keepdims=True) * dsilu

    return pl.pallas_call(
        body, name="ada_backward",
        out_shape=[jax.ShapeDtypeStruct((2, D, ADA_SHARD), f32), jax.ShapeDtypeStruct((2, 8, D), f32)], grid=(2,),
        in_specs=[pl.BlockSpec((16, D), lambda l: (0, 0)), pl.BlockSpec((1, 16, ADA_SHARD), lambda l: (l, 0, 0)),
                  pl.BlockSpec((1, D, ADA_SHARD), lambda l: (l, 0, 0))],
        out_specs=[pl.BlockSpec((1, D, ADA_SHARD), lambda l: (l, 0, 0)), pl.BlockSpec((1, 8, D), lambda l: (l, 0, 0))],
        compiler_params=_cparams("parallel"),
    )(c16, g16, w_ada)


def _adamw(pieces, w, m, v, name):
    rows, cols = w.shape
    n_arr = len(pieces)
    counts = [cnt for _, cnt in pieces]
    pieces = [p for p, _ in pieces]
    tm = 256 if (rows % 256 == 0 and rows > 256) else rows

    def body(*refs):
        p_refs = refs[:n_arr]
        w_ref, m_ref, v_ref, g_ref, d_ref, nm_ref, nv_ref = refs[n_arr:]
        g = None
        for p_ref in p_refs:
            for j in range(p_ref.shape[0]):
                term = p_ref[j].astype(f32)
                g = term if g is None else g + term
        m2 = ADAM_B1 * m_ref[...] + (1.0 - ADAM_B1) * g
        v2 = ADAM_B2 * v_ref[...] + (1.0 - ADAM_B2) * (g * g)
        m_hat = m2 / (1.0 - ADAM_B1 ** ADAM_STEP)
        v_hat = v2 / (1.0 - ADAM_B2 ** ADAM_STEP)
        g_ref[...] = g
        d_ref[...] = -ADAM_LR * (m_hat / (jnp.sqrt(v_hat) + ADAM_EPS) + ADAM_WD * w_ref[...])
        nm_ref[...] = m2
        nv_ref[...] = v2

    spec = pl.BlockSpec((tm, cols), lambda i: (i, 0))
    return pl.pallas_call(
        body, name=name, out_shape=[jax.ShapeDtypeStruct((rows, cols), f32)] * 4, grid=(rows // tm,),
        in_specs=[pl.BlockSpec((cnt, tm, cols), lambda i: (0, i, 0)) for cnt in counts] + [spec, spec, spec],
        out_specs=[spec] * 4, compiler_params=_cparams("parallel"),
    )(*pieces, w, m, v)


MLP_TM = 256
FB = F // N_DEV


def _stack_rows(vals, n):
    cols = vals[0].shape[1]
    rid = lax.broadcasted_iota(jnp.int32, (n, cols), 0)
    out = jnp.zeros((n, cols), f32)
    for k, v in enumerate(vals):
        out = jnp.where(rid == k, v, out)
    return out


N_MLP_PARAMS = 9


class _ParamRows:
    def __init__(self, ref):
        self.ref = ref

    def __getitem__(self, sl):
        return self.ref[8 * sl.start:8 * sl.start + 1, :]


def _resident(shape, imap):
    return pl.BlockSpec(shape, imap, pipeline_mode=pl.Buffered(1))


def _mlp_forward(xa, xa_roff, out_prev, par, w_in, w_out, layer, name):
    def body(xa_ref, op_ref, par_ref, win_ref, wout_ref, x1_ref, h_ref, r_ref, mo_ref, x2_ref, hn_ref):
        p = _ParamRows(par_ref)
        x1 = xa_ref[...] + p[0:1] * (op_ref[...] + p[1:2])
        h = _normmod(x1, p[2:3], p[3:4], p[4:5]).astype(bf16)
        x1_ref[...] = x1
        h_ref[...] = h
        mo = jnp.zeros((MLP_TM, D), f32)
        for j in range(N_DEV):
            r = jnp.maximum(jnp.dot(h, win_ref[j], preferred_element_type=f32), 0.0)
            r_ref[:, j * FB:(j + 1) * FB] = r.astype(bf16)
            mo = mo + jnp.dot((r * r).astype(bf16), wout_ref[j], preferred_element_type=f32)
        mo_ref[...] = mo.astype(bf16)
        x2 = x1 + p[5:6] * mo
        x2_ref[...] = x2
        hn_ref[...] = _normmod(x2, p[6:7], p[7:8], p[8:9]).astype(bf16)

    row = lambda width: pl.BlockSpec((MLP_TM, width), lambda i: (i, 0))
    return pl.pallas_call(
        body, name=name, grid=(T_LAT // MLP_TM,),
        out_shape=[jax.ShapeDtypeStruct((T_LAT, D), f32), jax.ShapeDtypeStruct((T_LAT, D), bf16),
                   jax.ShapeDtypeStruct((T_LAT, F), bf16), jax.ShapeDtypeStruct((T_LAT, D), bf16),
                   jax.ShapeDtypeStruct((T_LAT, D), f32), jax.ShapeDtypeStruct((T_LAT, D), bf16)],
        in_specs=[pl.BlockSpec((MLP_TM, D), lambda i: (i + xa_roff, 0)), row(D), pl.BlockSpec((8 * N_MLP_PARAMS, D), lambda i: (0, 0)),
                  _resident((N_DEV, None, D, FB), lambda i: (0, layer, 0, 0)),
                  _resident((N_DEV, None, FB, D), lambda i: (0, layer, 0, 0))],
        out_specs=[row(D), row(D), row(F), row(D), row(D), row(D)],
        compiler_params=_cparams("parallel"),
    )(xa, out_prev, par, w_in, w_out)


def _mlp_backward(dx2, x1, r, mo, out_prev, par, w_in, w_out, layer, name):
    nt = (((1,), (1,)), ((), ()))

    def body(dx2_ref, x1_ref, r_ref, mo_ref, op_ref, par_ref, win_ref, wout_ref, dx1_ref, dop_ref, dmo_ref, dhid_ref,
             acc_ref):
        p = _ParamRows(par_ref)
        dx2v = dx2_ref[...]
        dmo = (p[5:6] * dx2v).astype(bf16)
        dmo_ref[...] = dmo
        dh = jnp.zeros((MLP_TM, D), f32)
        mo = mo_ref[...].astype(f32)
        for j in range(N_DEV):
            rf = r_ref[:, j * FB:(j + 1) * FB].astype(f32)
            dact = lax.dot_general(dmo, wout_ref[j], nt, preferred_element_type=f32)
            dhid = (dact * (2.0 * rf)).astype(bf16)
            dhid_ref[:, j * FB:(j + 1) * FB] = dhid
            dh = dh + lax.dot_general(dhid, win_ref[j], nt, preferred_element_type=f32)
        x1 = x1_ref[...]
        _, vjp = jax.vjp(_normmod, x1, p[2:3], p[3:4], p[4:5])
        dx, dng, dsc, dsh = vjp(dh)
        dx1 = dx2v + dx
        dx1_ref[...] = dx1
        dop_ref[...] = (p[0:1] * dx1).astype(bf16)
        sums = _stack_rows([jnp.sum(dx1 * (op_ref[...] + p[1:2]), axis=0, keepdims=True),
                            p[0:1] * jnp.sum(dx1, axis=0, keepdims=True), dng, dsc, dsh,
                            jnp.sum(dx2v * mo, axis=0, keepdims=True)], 8)

        @pl.when(pl.program_id(0) == 0)
        def _():
            acc_ref[...] = jnp.zeros_like(acc_ref)

        acc_ref[...] += sums

    row = lambda width: pl.BlockSpec((MLP_TM, width), lambda i: (i, 0))
    return pl.pallas_call(
        body, name=name, grid=(T_LAT // MLP_TM,),
        out_shape=[jax.ShapeDtypeStruct((T_LAT, D), f32), jax.ShapeDtypeStruct((T_LAT, D), bf16),
                   jax.ShapeDtypeStruct((T_LAT, D), bf16), jax.ShapeDtypeStruct((T_LAT, F), bf16),
                   jax.ShapeDtypeStruct((8, D), f32)],
        in_specs=[row(D), row(D), row(F), row(D), row(D), pl.BlockSpec((8 * N_MLP_PARAMS, D), lambda i: (0, 0)),
                  _resident((N_DEV, None, D, FB), lambda i: (0, layer, 0, 0)),
                  _resident((N_DEV, None, FB, D), lambda i: (0, layer, 0, 0))],
        out_specs=[row(D), row(D), row(D), row(F), pl.BlockSpec((8, D), lambda i: (0, 0))],
        compiler_params=_cparams("arbitrary"),
    )(dx2, x1, r, mo, out_prev, par, w_in, w_out)


def _mlp_weight_grads(h, dhid, r, dmo, layer, other, tag):
    tn = (((0,), (0,)), ((), ()))

    def body_in(h_ref, dhid_ref, *rest):
        rest[-1][...] = lax.dot_general(h_ref[...], dhid_ref[...], tn, preferred_element_type=f32).astype(bf16)

    def body_out(r_ref, dmo_ref, *rest):
        rf = r_ref[...].astype(f32)
        rest[-1][...] = lax.dot_general((rf * rf).astype(bf16), dmo_ref[...], tn,
                                        preferred_element_type=f32).astype(bf16)

    def call(body, name, operands, specs, block, prev):
        extra = [] if prev is None else [prev]
        return pl.pallas_call(
            body, name=name, grid=(N_DEV,), out_shape=jax.ShapeDtypeStruct((N_DEV, 2) + block, bf16),
            in_specs=specs + [pl.BlockSpec(memory_space=pl.ANY)] * len(extra),
            out_specs=pl.BlockSpec((None, None) + block, lambda j: (j, layer, 0, 0)),
            input_output_aliases={} if prev is None else {2: 0}, compiler_params=_cparams("parallel"),
        )(*operands, *extra)

    dw_in = call(body_in, tag + "_mlp_in_dw", [h, dhid],
                 [_resident((T_LAT, D), lambda j: (0, 0)), pl.BlockSpec((T_LAT, FB), lambda j: (0, j))], (D, FB),
                 None if other is None else other[0])
    dw_out = call(body_out, tag + "_mlp_out_dw", [r, dmo],
                  [pl.BlockSpec((T_LAT, FB), lambda j: (0, j)), _resident((T_LAT, D), lambda j: (0, 0))], (FB, D),
                  None if other is None else other[1])
    return dw_in, dw_out


def _pos_embed():
    n_rows = T_LAT // GRID_W
    q = D // 4
    omega = 1.0 / (POS_BASE ** (jnp.arange(q, dtype=f32) / q))
    er = jnp.arange(n_rows, dtype=jnp.int32).astype(f32)[:, None] * omega[None, :]
    ec = jnp.arange(GRID_W, dtype=jnp.int32).astype(f32)[:, None] * omega[None, :]
    by_row = jnp.concatenate([jnp.sin(er), jnp.cos(er)], axis=-1)[:, None, :]
    by_col = jnp.concatenate([jnp.sin(ec), jnp.cos(ec)], axis=-1)[None, :, :]
    full = jnp.concatenate([jnp.broadcast_to(by_row, (n_rows, GRID_W, D // 2)),
                            jnp.broadcast_to(by_col, (n_rows, GRID_W, D // 2))], axis=-1)
    return full.reshape(T_LAT, D)


HALF = R // 2
BLK_PER_HALF = N_BLK // 2
N_PARTS = 4


def _gate_matrix(w_a, w_x):
    eye = jnp.eye(BLK_PER_HALF, dtype=bf16)
    cols = []
    for h in range(2):
        for d in range(2):
            for w in (w_a, w_x):
                blocks = w[d, BLK_PER_HALF * h:BLK_PER_HALF * (h + 1)].astype(bf16)
                cols.append(jnp.einsum("hij,hg->higj", blocks, eye).reshape(HALF, HALF))
    return jnp.concatenate(cols, axis=1)


def _gate_blocks(dwg, part):
    out = []
    for h in range(2):
        blk = dwg[:, (N_PARTS * h + part) * HALF:(N_PARTS * h + part + 1) * HALF]
        blk = blk.reshape(BLK_PER_HALF, BLK, BLK_PER_HALF, BLK)
        out.append(jnp.moveaxis(jnp.diagonal(blk, axis1=0, axis2=2), -1, 0))
    return jnp.concatenate(out, axis=0)


def _gate_part(pre, part):
    return jnp.concatenate([pre[:, (N_PARTS * h + part) * HALF:(N_PARTS * h + part + 1) * HALF] for h in range(2)],
                           axis=1)


def _gate_unpart(parts):
    return jnp.concatenate([parts[p][:, h * HALF:(h + 1) * HALF] for h in range(2) for p in range(N_PARTS)], axis=1)


GATE_BM = 768


def _gates_fwd(u, wg):
    rows = u.shape[0]

    def body(u_ref, w_ref, o_ref):
        o_ref[...] = jnp.dot(u_ref[...], w_ref[...], preferred_element_type=f32)

    return pl.pallas_call(
        body, name="l0_gates", grid=(rows // GATE_BM, 2 * N_PARTS),
        out_shape=jax.ShapeDtypeStruct((rows, 2 * N_PARTS * HALF), f32),
        in_specs=[pl.BlockSpec((GATE_BM, HALF), lambda i, j: (i, j // N_PARTS)),
                  pl.BlockSpec((HALF, HALF), lambda i, j: (0, j))],
        out_specs=pl.BlockSpec((GATE_BM, HALF), lambda i, j: (i, j)),
        compiler_params=_cparams("parallel", "parallel"),
    )(u, wg)


def _gates_dx(dpre, wg):
    rows = dpre.shape[0]

    def body(d_ref, w_ref, o_ref, acc_ref):
        p = pl.program_id(2)

        @pl.when(p == 0)
        def _():
            acc_ref[...] = jnp.zeros_like(acc_ref)

        acc_ref[...] += lax.dot_general(d_ref[...], w_ref[...], (((1,), (1,)), ((), ())), preferred_element_type=f32)

        @pl.when(p == N_PARTS - 1)
        def _():
            o_ref[...] = acc_ref[...]

    return pl.pallas_call(
        body, name="l0_gates_dx", grid=(rows // GATE_BM, 2, N_PARTS), out_shape=jax.ShapeDtypeStruct((rows, R), f32),
        in_specs=[pl.BlockSpec((GATE_BM, HALF), lambda i, h, p: (i, N_PARTS * h + p)),
                  pl.BlockSpec((HALF, HALF), lambda i, h, p: (0, N_PARTS * h + p))],
        out_specs=pl.BlockSpec((GATE_BM, HALF), lambda i, h, p: (i, h)),
        scratch_shapes=[pltpu.VMEM((GATE_BM, HALF), f32)],
        compiler_params=_cparams("parallel", "parallel", "arbitrary"),
    )(dpre, wg)


def _gates_dw(u, dpre):
    rows = u.shape[0]

    def body(u_ref, d_ref, o_ref):
        o_ref[...] = lax.dot_general(u_ref[...], d_ref[...], (((0,), (0,)), ((), ())), preferred_element_type=f32)

    return pl.pallas_call(
        body, name="l0_gates_dw", grid=(2 * N_PARTS,), out_shape=jax.ShapeDtypeStruct((HALF, 2 * N_PARTS * HALF), f32),
        in_specs=[pl.BlockSpec((rows, HALF), lambda j: (0, j // N_PARTS)), pl.BlockSpec((rows, HALF), lambda j: (0, j))],
        out_specs=pl.BlockSpec((HALF, HALF), lambda j: (0, j)), compiler_params=_cparams("parallel"),
    )(u, dpre)


N_SCAN_CHUNKS = T_ALL // SCAN_CHUNK
SCAN_FWD = lambda t: t
SCAN_FWD_BWD = lambda t: N_SCAN_CHUNKS - 1 - t
SCAN_REV = lambda t: jnp.where(t == 0, 0, N_SCAN_CHUNKS - t)
SCAN_REV_BWD = lambda t: jnp.where(t == N_SCAN_CHUNKS - 1, 0, t + 1)
CONV_SEGMENTS = ((0, T_CTX), (T_CTX, T_LAT))
TM = 128
N_CTX_TILES = T_CTX // TM


def _local_step(x, ctx, target, mods, cmod, wts, late_weights, send_grads, start_after=()):
    sh1, sc1, g1, sh2, sc2, g2 = [[mods[l, i][None] for l in range(2)] for i in range(N_MOD)]
    ng = wts["norm_g"]
    xcat = jnp.concatenate([ctx, x], axis=0)
    poscat = jnp.concatenate([jnp.zeros((T_CTX, D), f32), _pos_embed()], axis=0)
    scp = jnp.concatenate([cmod[1][None], sc1[0]], axis=0)
    shp = jnp.concatenate([cmod[0][None], sh1[0]], axis=0)

    def blend(i, p):
        sel = jnp.where(i < N_CTX_TILES, 1.0, 0.0)
        return sel * p[0:1] + (1.0 - sel) * p[1:2]

    def f_pre0(i, xc, pos, g, scp_, shp_):
        x0 = xc + pos
        return x0, _normmod(x0, g, blend(i, scp_), blend(i, shp_))

    x0cat, h0 = _rowcall(f_pre0, "l0_prenorm", T_ALL, TM, [_rin(xcat), _rin(poscat)], [ng[0, 0][None], scp, shp],
                         [(D, f32), (D, bf16)], after=start_after)
    gr = _mm(h0, wts["rec_w_in"], "l0_in_proj")
    u, ub = _dwconv_fwd(gr, R // 256, wts["rec_conv_w"], wts["rec_conv_b"], 4, 1, CONV_SEGMENTS, 256,
                        "l0_conv", True)
    pre = _gates_fwd(ub, wts["gates"])

    def f_coeff(i, pre_, u_, ba, bx, lam):
        outs = []
        for d in range(2):
            a, b = _coeff(_gate_part(pre_, 2 * d), _gate_part(pre_, 2 * d + 1), u_,
                          ba[d:d + 1], bx[d:d + 1], lam[d:d + 1])
            outs += [a, b]
        return tuple(outs)

    a0, b0, a1, b1 = _rowcall(f_coeff, "l0_coeff", T_ALL, TM, [_rin(pre), _rin(u)],
                              [wts["rec_b_a"], wts["rec_b_x"], wts["rec_lambda"]], [(R, f32)] * 4)
    y0, yp0 = _scan_call(a0, b0, SCAN_FWD, False, "l0_scan_fwd", False)
    y1, yp1 = _scan_call(a1, b1, SCAN_REV, True, "l0_scan_rev", False)

    def f_gate(i, gp, y0_, y1_):
        return (_gelu(gp) * (y0_ + y1_),)

    (zb,) = _rowcall(f_gate, "l0_gate", T_LAT, TM,
                     [_rin(gr, R, 0, N_CTX_TILES), _rin(y0, None, 0, N_CTX_TILES), _rin(y1, None, 0, N_CTX_TILES)],
                     [], [(R, bf16)])
    out0 = _mm(zb, wts["rec_w_out"], "l0_out_proj")

    zero_d = jnp.zeros((1, D), f32)

    def mlp_params(rows):
        rows = rows + [zero_d] * (N_MLP_PARAMS - len(rows))
        return jnp.concatenate([jnp.broadcast_to(r, (8, D)) for r in rows], axis=0)

    par0 = mlp_params([g1[0], zero_d, ng[0, 1][None], sc2[0], sh2[0], g2[0], ng[1, 0][None], sc1[1], sh1[1]])
    wts = dict(wts, **late_weights("mlp", out0))
    x1, h1, r0, mo0, x2, h2 = _mlp_forward(x0cat, T_CTX // MLP_TM, out0, par0, wts["mlp_w_in"], wts["mlp_w_out"], 0,
                                           "l0_mlp")

    wts = dict(wts, **late_weights("conf", x2))
    pw = _mm(h2, wts["conf_w_pw1"], "l1_pw1")

    def f_glu(i, pa, pb, b1):
        return ((pa + b1[:, :D]) * _sigmoid(pb + b1[:, D:]),)

    (zg,) = _rowcall(f_glu, "l1_glu", T_LAT, TM, [_rin(pw, D, 0), _rin(pw, D, 1)], [wts["conf_b_pw1"]], [(D, f32)])
    (zc,) = _dwconv_fwd(zg, 0, wts["conf_conv_w"], wts["conf_conv_b"], 31, 15, ((0, T_LAT),), 128, "l1_conv", False)

    def ln_silu(z, lg, lb):
        mu = jnp.mean(z, axis=-1, keepdims=True)
        zc_ = z - mu
        var = jnp.mean(zc_ * zc_, axis=-1, keepdims=True)
        yv = zc_ * lax.rsqrt(var + EPS) * lg + lb
        return yv * _sigmoid(yv)

    def f_lnsilu(i, z, lg, lb):
        return (ln_silu(z, lg, lb),)

    (sb,) = _rowcall(f_lnsilu, "l1_ln_silu", T_LAT, TM, [_rin(zc)], [wts["conf_ln_g"], wts["conf_ln_b"]], [(D, bf16)])
    out1 = _mm(sb, wts["conf_w_pw2"], "l1_pw2")
    par1 = mlp_params([g1[1], wts["conf_b_pw2"], ng[1, 1][None], sc2[1], sh2[1], g2[1]])
    x3, h3, r1, mo1, x4, _ = _mlp_forward(x2, 0, out1, par1, wts["mlp_w_in"], wts["mlp_w_out"], 1, "l1_mlp")

    def loss_fn(x4_, fg, tgt):
        err = _rms(x4_, fg) - tgt
        per_row = jnp.mean(err * err, axis=-1, keepdims=True)
        return 0.5 * jnp.sum(per_row, axis=0, keepdims=True)

    def f_head(i, x4_, tgt, fg):
        loss, vjp = jax.vjp(lambda a, e: loss_fn(a, e, tgt), x4_, fg)
        dx, dfg = vjp(jnp.ones((1, 1), f32))
        return dx, jnp.broadcast_to(loss, (1, 128)), dfg

    dx4, loss_acc, dfinal_g = _rowcall(f_head, "head", T_LAT, TM, [_rin(x4), _rin(target)], [wts["final_g"]],
                                       [(D, f32)], [(1, 128), (1, D)])

    grads = {"final_g": dfinal_g}

    def normmod_bwd(xin, dh, dx_skip, g, sc, sh, tag, after):
        def fb(i, x_, dh_, dxs, g_, sc_, sh_):
            _, vjp = jax.vjp(_normmod, x_, g_, sc_, sh_)
            dx, dg, dsc, dsh = vjp(dh_)
            return dx + dxs, dg, dsc, dsh

        return _rowcall(fb, tag + "_normmod_bwd", T_LAT, TM, [_rin(xin), _rin(dh), _rin(dx_skip)], [g, sc, sh],
                        [(D, f32)], [(1, D)] * 3, after=after)

    dx3, dout1, dmo1, dhid1, acc1 = _mlp_backward(dx4, x3, r1, mo1, out1, par1, wts["mlp_w_in"], wts["mlp_w_out"], 1,
                                                  "l1_mlp_bwd")
    mlp_dw = _mlp_weight_grads(h3, dhid1, r1, dmo1, 1, None, "l1")
    dg1_1, db_pw2, dng11, dsc2_1, dsh2_1, dg2_1 = [acc1[k:k + 1] for k in range(6)]

    ds = _mm(dout1, wts["conf_w_pw2"], "l1_pw2_dx", tb=True)
    grads["conf_w_pw2"] = _mm(sb, dout1, "l1_pw2_dw", ta=True, out_dtype=bf16)
    grads["conf_b_pw2"] = db_pw2

    def f_lnsilu_bwd(i, z, ds_, lg, lb):
        _, vjp = jax.vjp(ln_silu, z, lg, lb)
        return vjp(ds_)

    dzc, dln_g, dln_b = _rowcall(f_lnsilu_bwd, "l1_ln_silu_bwd", T_LAT, TM, [_rin(zc), _rin(ds)],
                                 [wts["conf_ln_g"], wts["conf_ln_b"]], [(D, f32)], [(1, D)] * 2)
    grads["conf_ln_g"], grads["conf_ln_b"] = dln_g, dln_b
    dzg, dconv_w, dconv_b = _dwconv_bwd([dzc], zg, 0, wts["conf_conv_w"], 31, 15, ((0, T_LAT),), 128,
                                        "l1_conv_bwd", f32)
    grads["conf_conv_w"], grads["conf_conv_b"] = dconv_w, dconv_b

    def f_glu_bwd(i, pa, pb, dz, b1):
        _, vjp = jax.vjp(lambda a, b, c: (a + c[:, :D]) * _sigmoid(b + c[:, D:]), pa, pb, b1)
        da, db, dc = vjp(dz)
        return jnp.concatenate([da, db], axis=1), dc

    dpw, db_pw1 = _rowcall(f_glu_bwd, "l1_glu_bwd", T_LAT, TM, [_rin(pw, D, 0), _rin(pw, D, 1), _rin(dzg)],
                           [wts["conf_b_pw1"]], [(2 * D, bf16)], [(1, 2 * D)])
    grads["conf_b_pw1"] = db_pw1
    dh2 = _mm(dpw, wts["conf_w_pw1"], "l1_pw1_dx", tb=True)
    grads["conf_w_pw1"] = _mm(h2, dpw, "l1_pw1_dw", ta=True, out_dtype=bf16)
    sent = send_grads(["conf_w_pw2", "conf_w_pw1"], grads)
    dx2, dng10, dsc1_1, dsh1_1 = normmod_bwd(x2, dh2, dx3, ng[1, 0][None], sc1[1], sh1[1], "l1a", [sent])

    dx1, dout0, dmo0, dhid0, acc0 = _mlp_backward(dx2, x1, r0, mo0, out0, par0, wts["mlp_w_in"], wts["mlp_w_out"], 0,
                                                  "l0_mlp_bwd")
    grads["mlp_w_in"], grads["mlp_w_out"] = _mlp_weight_grads(h1, dhid0, r0, dmo0, 0, mlp_dw, "l0")
    sent = send_grads(["mlp_w_in", "mlp_w_out"], grads)
    dg1_0, _, dng01, dsc2_0, dsh2_0, dg2_0 = [acc0[k:k + 1] for k in range(6)]

    dz = _mm(dout0, wts["rec_w_out"], "l0_out_proj_dx", tb=True, after=[sent])
    grads["rec_w_out"] = _mm(zb, dout0, "l0_out_proj_dw", ta=True, out_dtype=bf16)
    sent = send_grads(["rec_w_out"], grads)

    def f_gate_bwd(i, gp, y0_, y1_, dz_):
        lat = jnp.where(i < N_CTX_TILES, 0.0, 1.0)
        _, vjp = jax.vjp(lambda a, b: _gelu(a) * b, gp, y0_ + y1_)
        dgp, dy = vjp(dz_)
        return dgp * lat, dy * lat

    dgp, dy = _rowcall(f_gate_bwd, "l0_gate_bwd", T_ALL, TM,
                       [_rin(gr, R, 0), _rin(y0), _rin(y1), _rin(dz, None, 0, -N_CTX_TILES)], [],
                       [(R, bf16), (R, f32)], after=[sent])
    (dh_f,) = _scan_call(a0, dy, SCAN_FWD_BWD, True, "l0_scan_fwd_bwd", True)
    (dh_r,) = _scan_call(a1, dy, SCAN_REV_BWD, False, "l0_scan_rev_bwd", True)

    def f_coeff_bwd(i, pre_, u_, dhf, dhr, ypf, ypr, ba, bx, lam):
        dpre, dba, dbx, dlam = [], [], [], []
        du = jnp.zeros_like(u_)
        for d, (dh_, yp_) in enumerate(((dhf, ypf), (dhr, ypr))):
            _, vjp = jax.vjp(_coeff, _gate_part(pre_, 2 * d), _gate_part(pre_, 2 * d + 1), u_,
                             ba[d:d + 1], bx[d:d + 1], lam[d:d + 1])
            dpa, dpx, du_d, dba_d, dbx_d, dlam_d = vjp((dh_ * yp_, dh_))
            dpre += [dpa, dpx]
            du = du + du_d
            dba.append(dba_d)
            dbx.append(dbx_d)
            dlam.append(dlam_d)
        return _gate_unpart(dpre), du, _rows2(*dba), _rows2(*dbx), _rows2(*dlam)

    dpre, du_direct, db_a, db_x, dlam = _rowcall(
        f_coeff_bwd, "l0_coeff_bwd", T_ALL, 64,
        [_rin(pre), _rin(u), _rin(dh_f), _rin(dh_r), _rin(yp0), _rin(yp1)],
        [wts["rec_b_a"], wts["rec_b_x"], wts["rec_lambda"]], [(4 * R, bf16), (R, f32)], [(2, R)] * 3)
    grads["rec_b_a"], grads["rec_b_x"], grads["rec_lambda"] = db_a, db_x, dlam
    du_gates = _gates_dx(dpre, wts["gates"])
    grads["gates"] = _gates_dw(ub, dpre)
    drec, dconv4_w, dconv4_b = _dwconv_bwd([du_direct, du_gates], gr, R // 256, wts["rec_conv_w"], 4, 1,
                                           CONV_SEGMENTS, 256, "l0_conv_bwd", bf16)
    grads["rec_conv_w"], grads["rec_conv_b"] = dconv4_w, dconv4_b
    dgr = jnp.concatenate([dgp, drec], axis=1)
    grads["rec_w_in"] = _mm(h0, dgr, "l0_in_proj_dw", ta=True, out_dtype=bf16)
    dh0 = _mm(dgr, wts["rec_w_in"], "l0_in_proj_dx", tb=True, after=[send_grads(["rec_w_in"], grads)])

    def f_pre0_bwd(i, x0, dh_, dxs, g, scp_, shp_):
        lat = jnp.where(i < N_CTX_TILES, 0.0, 1.0)
        _, vjp = jax.vjp(lambda a, b, c, e: _normmod(a, b, blend(i, c), blend(i, e)), x0, g, scp_, shp_)
        dx, dg, dscp, dshp = vjp(dh_)
        return dx + lat * dxs, dg, dscp, dshp

    dx0cat, dng00, dscp, dshp = _rowcall(
        f_pre0_bwd, "l0_prenorm_bwd", T_ALL, TM, [_rin(x0cat), _rin(dh0), _rin(dx1, None, 0, -N_CTX_TILES)],
        [ng[0, 0][None], scp, shp], [(D, f32)], [(1, D), (2, D), (2, D)])

    grads["norm_g"] = jnp.stack([jnp.concatenate([dng00, dng01], 0), jnp.concatenate([dng10, dng11], 0)])
    dmods = jnp.stack([
        jnp.concatenate([dshp[1:2], dscp[1:2], dg1_0, dsh2_0, dsc2_0, dg2_0], axis=0),
        jnp.concatenate([dsh1_1, dsc1_1, dg1_1, dsh2_1, dsc2_1, dg2_1], axis=0)])
    dcmod = jnp.concatenate([dshp[0:1], dscp[0:1]], axis=0)
    return loss_acc[0, 0], dx0cat[T_CTX:], dmods, dcmod, grads


def _unshard_cols(g):
    g = jnp.moveaxis(g, 0, -2)
    return g.reshape(g.shape[:-2] + (g.shape[-2] * g.shape[-1],))


def _shard_cols(w):
    w = w.reshape(w.shape[:-1] + (N_DEV, w.shape[-1] // N_DEV))
    return jnp.moveaxis(w, -2, 0)


def _shard_rows(w):
    return w.reshape((N_DEV, w.shape[0] // N_DEV) + w.shape[1:])


SMALL_PACK_ROWS = 64


def kernel(x, c, ctx, c_ctx, w_ada, b_ada, norm_g, rec_w_in, rec_conv_w, rec_conv_b, rec_lambda, rec_w_a, rec_b_a, rec_w_x, rec_b_x, rec_w_out, conf_w_pw1, conf_b_pw1, conf_conv_w, conf_conv_b, conf_ln_g, conf_ln_b, conf_w_pw2, conf_b_pw2, mlp_w_in, mlp_w_out, final_g, loss_target, m_c_ctx, m_w_ada, m_b_ada, m_norm_g, m_rec_w_in, m_rec_conv_w, m_rec_conv_b, m_rec_lambda, m_rec_w_a, m_rec_b_a, m_rec_w_x, m_rec_b_x, m_rec_w_out, m_conf_w_pw1, m_conf_b_pw1, m_conf_conv_w, m_conf_conv_b, m_conf_ln_g, m_conf_ln_b, m_conf_w_pw2, m_conf_b_pw2, m_mlp_w_in, m_mlp_w_out, m_final_g, v_c_ctx, v_w_ada, v_b_ada, v_norm_g, v_rec_w_in, v_rec_conv_w, v_rec_conv_b, v_rec_lambda, v_rec_w_a, v_rec_b_a, v_rec_w_x, v_rec_b_x, v_rec_w_out, v_conf_w_pw1, v_conf_b_pw1, v_conf_conv_w, v_conf_conv_b, v_conf_ln_g, v_conf_ln_b, v_conf_w_pw2, v_conf_b_pw2, v_mlp_w_in, v_mlp_w_out, v_final_g):
    me = 4 * lax.axis_index("x") + 2 * lax.axis_index("y") + lax.axis_index("c")
    weights = dict(c_ctx=c_ctx, w_ada=w_ada, b_ada=b_ada, norm_g=norm_g, rec_w_in=rec_w_in, rec_conv_w=rec_conv_w,
                   rec_conv_b=rec_conv_b, rec_lambda=rec_lambda, rec_w_a=rec_w_a, rec_b_a=rec_b_a, rec_w_x=rec_w_x,
                   rec_b_x=rec_b_x, rec_w_out=rec_w_out, conf_w_pw1=conf_w_pw1, conf_b_pw1=conf_b_pw1,
                   conf_conv_w=conf_conv_w, conf_conv_b=conf_conv_b, conf_ln_g=conf_ln_g, conf_ln_b=conf_ln_b,
                   conf_w_pw2=conf_w_pw2, conf_b_pw2=conf_b_pw2, mlp_w_in=mlp_w_in, mlp_w_out=mlp_w_out, final_g=final_g)
    m_in = dict(c_ctx=m_c_ctx, w_ada=m_w_ada, b_ada=m_b_ada, norm_g=m_norm_g, rec_w_in=m_rec_w_in, rec_conv_w=m_rec_conv_w,
                rec_conv_b=m_rec_conv_b, rec_lambda=m_rec_lambda, rec_w_a=m_rec_w_a, rec_b_a=m_rec_b_a, rec_w_x=m_rec_w_x,
                rec_b_x=m_rec_b_x, rec_w_out=m_rec_w_out, conf_w_pw1=m_conf_w_pw1, conf_b_pw1=m_conf_b_pw1,
                conf_conv_w=m_conf_conv_w, conf_conv_b=m_conf_conv_b, conf_ln_g=m_conf_ln_g, conf_ln_b=m_conf_ln_b,
                conf_w_pw2=m_conf_w_pw2, conf_b_pw2=m_conf_b_pw2, mlp_w_in=m_mlp_w_in, mlp_w_out=m_mlp_w_out,
                final_g=m_final_g)
    v_in = dict(c_ctx=v_c_ctx, w_ada=v_w_ada, b_ada=v_b_ada, norm_g=v_norm_g, rec_w_in=v_rec_w_in, rec_conv_w=v_rec_conv_w,
                rec_conv_b=v_rec_conv_b, rec_lambda=v_rec_lambda, rec_w_a=v_rec_w_a, rec_b_a=v_rec_b_a, rec_w_x=v_rec_w_x,
                rec_b_x=v_rec_b_x, rec_w_out=v_rec_w_out, conf_w_pw1=v_conf_w_pw1, conf_b_pw1=v_conf_b_pw1,
                conf_conv_w=v_conf_conv_w, conf_conv_b=v_conf_conv_b, conf_ln_g=v_conf_ln_g, conf_ln_b=v_conf_ln_b,
                conf_w_pw2=v_conf_w_pw2, conf_b_pw2=v_conf_b_pw2, mlp_w_in=v_mlp_w_in, mlp_w_out=v_mlp_w_out,
                final_g=v_final_g)
    names = list(weights)

    small_items = [c, norm_g, rec_conv_w, rec_lambda, conf_b_pw1, conf_conv_w, conf_conv_b, conf_ln_g, conf_ln_b,
                   conf_b_pw2]
    flat = jnp.concatenate([a.reshape(-1) for a in small_items])
    flat = jnp.pad(flat, (0, SMALL_PACK_ROWS * 128 - flat.shape[0])).reshape(SMALL_PACK_ROWS, 128)
    (small_all,) = _all_gather([flat], "gather_small")

    small_all = small_all.reshape(N_DEV, -1)
    off = 0
    small = []
    for a in small_items:
        small.append(small_all[:, off:off + a.size].reshape((N_DEV,) + a.shape))
        off += a.size
    c_all, ng_all, rcw_all, lam_all, bpw1_all, ccw_all, ccb_all, lng_all, lnb_all, bpw2_all = small
    wts = {
        "norm_g": _unshard_cols(ng_all),
        "rec_conv_w": _unshard_cols(rcw_all)[0],
        "rec_lambda": _unshard_cols(lam_all)[0],
        "conf_b_pw1": _unshard_cols(bpw1_all),
        "conf_conv_w": _unshard_cols(ccw_all)[0],
        "conf_conv_b": _unshard_cols(ccb_all),
        "conf_ln_g": _unshard_cols(lng_all),
        "conf_ln_b": _unshard_cols(lnb_all),
        "conf_b_pw2": _unshard_cols(bpw2_all),
        "rec_conv_b": rec_conv_b,
        "rec_b_a": rec_b_a[0].reshape(2, R),
        "rec_b_x": rec_b_x[0].reshape(2, R),
        "final_g": final_g[None],
        "gates": _gate_matrix(rec_w_a[0], rec_w_x[0]),
    }

    c16 = jnp.concatenate([c_all[:, 0], jnp.broadcast_to(c_ctx[None], (8, D))], axis=0)
    b_loc = lax.dynamic_slice_in_dim(b_ada, me * ADA_SHARD, ADA_SHARD, axis=1)[:, None]
    (mods_all,) = _all_gather([_ada_forward(c16, w_ada, b_loc)], "gather_mods")
    mods_all = _unshard_cols(mods_all)
    mods = lax.dynamic_index_in_dim(mods_all, me, axis=1, keepdims=False).reshape(2, N_MOD, D)
    cmod = mods_all[0, 8, :2 * D].reshape(2, D)

    as_shard = lambda a: a.astype(bf16).reshape(-1, a.shape[-1])
    early = _all_gather_2level([as_shard(rec_w_in[0]), as_shard(rec_w_out[0])], "gather_weights_early")
    wts["rec_w_in"] = _unshard_cols(early[0])
    wts["rec_w_out"] = early[1].reshape(R, D)
    late_items = {"mlp": [mlp_w_in, mlp_w_out], "conf": [conf_w_pw1[0], conf_w_pw2[0]]}
    late_handles, order = {}, [early[0], mods]
    for group in ("mlp", "conf"):
        shards = [as_shard(a) for a in late_items[group]]
        late_handles[group], token = _exchange_start(shards, [_own_block_filled(s, me) for s in shards],
                                                     f"gather_{group}_start", False, after=order)
        order = [token]

    def late_weights(group, after):
        got = _exchange_wait(late_handles[group], after, f"gather_{group}_wait", False)
        got = [g.reshape((N_DEV,) + a.shape) for g, a in zip(got, late_items[group])]
        if group == "mlp":
            return {"mlp_w_in": got[0], "mlp_w_out": got[1]}
        return {"conf_w_pw1": _unshard_cols(got[0]), "conf_w_pw2": got[1].reshape(D, D)}

    to_blocks = {"rec_w_in": _shard_cols, "conf_w_pw1": _shard_cols, "rec_w_out": _shard_rows, "conf_w_pw2": _shard_rows,
                 "mlp_w_in": lambda g: g, "mlp_w_out": lambda g: g}
    grad_handles = []

    def send_grads(group, grads):
        blocks = [to_blocks[n](grads[n]) for n in group]
        blocks = [g.reshape(N_DEV, -1, g.shape[-1]) for g in blocks]
        lands = [_own_block_filled(lax.dynamic_index_in_dim(g, me, 0, keepdims=False), me) for g in blocks]
        handle, sent = _exchange_start(blocks, lands, "scatter_start_" + group[0], True)
        grad_handles.append((group, handle))
        return sent

    loss_part, grad_x, dmods, dcmod, grads = _local_step(x[0], ctx[0], loss_target[0], mods, cmod, wts, late_weights,
                                                         send_grads, start_after=order)
    loss = lax.psum(loss_part, ("x", "y", "c"))

    dm_flat = jnp.concatenate([dmods.reshape(-1), dcmod.reshape(-1)]).reshape(-1, 128)
    (dm_all,) = _all_gather([dm_flat], "gather_dmods")
    dm_all = dm_all.reshape(N_DEV, -1)
    dmods_all = dm_all[:, :2 * N_MOD * D].reshape(N_DEV, 2, N_MOD * D)
    dcmod_all = jnp.pad(dm_all[:, 2 * N_MOD * D:], ((0, 0), (0, (N_MOD - 2) * D)))
    g16_full = jnp.stack([jnp.concatenate([dmods_all[:, 0], dcmod_all], axis=0),
                          jnp.concatenate([dmods_all[:, 1], jnp.zeros_like(dcmod_all)], axis=0)])
    g16 = lax.dynamic_slice_in_dim(g16_full, me * ADA_SHARD, ADA_SHARD, axis=2)
    dw_ada, ds_part = _ada_backward(c16, g16, w_ada)
    (ds_all,) = _all_gather([ds_part[0]], "gather_dsilu")

    big_names, big_pieces = [], []
    for group, handle in grad_handles:
        for n, got in zip(group, _exchange_wait(handle, grad_x, "scatter_wait_" + group[0], True)):
            big_names.append(n)
            big_pieces.append([(got, N_DEV)])
    small_sharded = ["norm_g", "rec_conv_w", "rec_lambda", "conf_b_pw1", "conf_conv_w", "conf_conv_b", "conf_ln_g",
                     "conf_ln_b", "conf_b_pw2"]
    pack = jnp.concatenate([_shard_cols(grads[n]).reshape(N_DEV, -1) for n in small_sharded], axis=1)
    pack_len = pack.shape[1]
    pack = jnp.pad(pack, ((0, 0), (0, SMALL_PACK_ROWS * 128 - pack_len))).reshape(N_DEV, SMALL_PACK_ROWS, 128)
    (pack_recv,) = _all_to_all([pack], "scatter_small_grads")
    pack_recv = pack_recv.reshape(N_DEV, -1)

    dwg = grads["gates"]
    repl = {"rec_conv_b": grads["rec_conv_b"],
            "rec_w_a": jnp.stack([_gate_blocks(dwg, 0), _gate_blocks(dwg, 2)])[None],
            "rec_w_x": jnp.stack([_gate_blocks(dwg, 1), _gate_blocks(dwg, 3)])[None],
            "rec_b_a": grads["rec_b_a"].reshape(1, 2, N_BLK, BLK),
            "rec_b_x": grads["rec_b_x"].reshape(1, 2, N_BLK, BLK),
            "final_g": grads["final_g"][0]}
    repl_names = list(repl)
    repl_flat = jnp.concatenate([repl[n].reshape(-1) for n in repl_names])
    repl_len = repl_flat.shape[0]
    repl_rows = -(-repl_len // (16 * D)) * 16
    repl_flat = jnp.pad(repl_flat, (0, repl_rows * D - repl_len)).reshape(repl_rows, D).astype(bf16)
    (repl_all,) = _all_gather_2level([repl_flat], "gather_replicated_grads")
    repl_all = repl_all.reshape(N_DEV, -1)

    def as2d(shape):
        rows = 1
        for s in shape[:-1]:
            rows *= s
        return (rows, shape[-1])

    def whole(arr, shape):
        arr = arr.reshape((-1,) + as2d(shape))
        return (arr, arr.shape[0])

    pieces = {}
    shard_shapes = {n: weights[n].shape for n in names}
    for n, parts in zip(big_names, big_pieces):
        pieces[n] = parts
    off = 0
    for n in small_sharded:
        size = weights[n].size
        pieces[n] = [whole(pack_recv[:, off:off + size], shard_shapes[n])]
        off += size
    off = 0
    for n in repl_names:
        size = weights[n].size
        pieces[n] = [whole(repl_all[:, off:off + size], shard_shapes[n])]
        off += size
    pieces["w_ada"] = [whole(dw_ada, shard_shapes["w_ada"])]
    db_terms = jnp.concatenate([dmods_all, jnp.stack([dcmod_all, jnp.zeros_like(dcmod_all)], axis=1)], axis=0)
    pieces["b_ada"] = [whole(db_terms, shard_shapes["b_ada"])]
    pieces["c_ctx"] = [whole(ds_all[:, 0], shard_shapes["c_ctx"])]

    g_out, d_out, m_out, v_out = {}, {}, {}, {}
    for n in names:
        shape = shard_shapes[n]
        r2, c2 = as2d(shape)
        p = pieces[n]
        g, dl, nm, nv = _adamw(p, weights[n].reshape(r2, c2), m_in[n].reshape(r2, c2), v_in[n].reshape(r2, c2),
                               "adamw_" + n)
        g_out[n], d_out[n], m_out[n], v_out[n] = (t.reshape(shape) for t in (g, dl, nm, nv))

    return (loss, grad_x[None], *[g_out[n] for n in names], *[d_out[n] for n in names],
            *[m_out[n] for n in names], *[v_out[n] for n in names])
```

```python
import functools

import jax
import jax.numpy as jnp
from jax import lax
from jax.experimental import pallas as pl
from jax.experimental.pallas import tpu as pltpu

f32 = jnp.float32
bf16 = jnp.bfloat16

N_DEV = 8
D = 1024
T_LAT = 2048
T_CTX = 256
T_ALL = T_CTX + T_LAT
R = 1280
N_BLK = 16
BLK = R // N_BLK
F = 4096
GRID_W = 64
RG_C = 8.0
EPS = 1e-6
POS_BASE = 10000.0
N_MOD = 6
ADA_SHARD = N_MOD * D // N_DEV

ADAM_LR = 0.001
ADAM_B1 = 0.9
ADAM_B2 = 0.999
ADAM_EPS = 1e-08
ADAM_WD = 0.01
ADAM_STEP = 10

VMEM_LIMIT_V7X = 56 * 1024 * 1024
HALO = 16
MESH = pl.DeviceIdType.MESH


def _cparams(*sem):
    return pltpu.CompilerParams(dimension_semantics=sem, vmem_limit_bytes=VMEM_LIMIT_V7X)


def _pick(n, cands):
    for c in cands:
        if n % c == 0:
            return c
    raise ValueError(f"no block size for {n}")


def _position():
    x, y, c = lax.axis_index("x"), lax.axis_index("y"), lax.axis_index("c")
    return x, y, c, 4 * x + 2 * y + c


def _peer(x, y, c, k):
    px = (1 - x) if (k >> 2) & 1 else x
    py = (1 - y) if (k >> 1) & 1 else y
    pc = (1 - c) if k & 1 else c
    return (px, py, pc), 4 * px + 2 * py + pc


def _exchange(arrs, name, scatter):
    n = len(arrs)

    def body(*refs):
        ins, outs = refs[:n], refs[n:2 * n]
        send_sems, recv_sems, local_sems = refs[2 * n:]
        x, y, c, me = _position()
        local = []
        for a in range(n):
            src = ins[a].at[me] if scatter else ins[a]
            cp = pltpu.make_async_copy(src, outs[a].at[me], local_sems.at[a])
            cp.start()
            local.append(cp)
        sends, recvs = [], []
        for a in range(n):
            for k in range(1, N_DEV):
                peer, peer_lin = _peer(x, y, c, k)
                src = ins[a].at[peer_lin] if scatter else ins[a]
                cp = pltpu.make_async_remote_copy(
                    src_ref=src, dst_ref=outs[a].at[me], send_sem=send_sems.at[a, k - 1],
                    recv_sem=recv_sems.at[a, k - 1], device_id=peer, device_id_type=MESH)
                cp.start()
                sends.append(cp)
                recvs.append(pltpu.make_async_remote_copy(
                    src_ref=src, dst_ref=outs[a].at[peer_lin], send_sem=send_sems.at[a, k - 1],
                    recv_sem=recv_sems.at[a, k - 1], device_id=peer, device_id_type=MESH))
        for cp in recvs:
            cp.wait_recv()
        for cp in sends:
            cp.wait_send()
        for cp in local:
            cp.wait()

    if scatter:
        out_shape = [jax.ShapeDtypeStruct(a.shape, a.dtype) for a in arrs]
    else:
        out_shape = [jax.ShapeDtypeStruct((N_DEV,) + a.shape, a.dtype) for a in arrs]
    any_spec = pl.BlockSpec(memory_space=pl.ANY)
    return pl.pallas_call(
        body, name=name, out_shape=out_shape,
        in_specs=[any_spec] * n, out_specs=[any_spec] * n,
        scratch_shapes=[pltpu.SemaphoreType.DMA((n, N_DEV - 1)), pltpu.SemaphoreType.DMA((n, N_DEV - 1)),
                        pltpu.SemaphoreType.DMA((n,))],
    )(*arrs)


def _all_gather(arrs, name):
    return _exchange(arrs, name, scatter=False)


def _all_to_all(arrs, name):
    return _exchange(arrs, name, scatter=True)


def _lin(p):
    return 4 * p[0] + 2 * p[1] + p[2]


HBM_SPEC = pl.BlockSpec(memory_space=pltpu.HBM)
SEM_SPEC = pl.BlockSpec(memory_space=pltpu.SEMAPHORE)
DATAFLOW_EFFECT = pltpu.SideEffectType.DATAFLOW_SIDE_EFFECTING


def _split_copies(srcs, lands, send_sems, recv_sems, scatter):
    x, y, c, me = _position()
    out = []
    for a in range(len(srcs)):
        for k in range(1, N_DEV):
            peer, peer_lin = _peer(x, y, c, k)
            src = srcs[a].at[peer_lin] if scatter else srcs[a]
            mk = lambda slot: pltpu.make_async_remote_copy(
                src_ref=src, dst_ref=lands[a].at[slot], send_sem=send_sems.at[a * (N_DEV - 1) + k - 1],
                recv_sem=recv_sems.at[a * (N_DEV - 1) + k - 1], device_id=peer, device_id_type=MESH)
            out.append((mk(me), mk(peer_lin)))
    return out


def _exchange_start(srcs, lands, name, scatter, after=()):
    n = len(srcs)
    n_after = len(after)

    def body(*refs):
        srcs_r, lands_r = refs[:n], refs[n:2 * n]
        send_sems, recv_sems = refs[2 * n + n_after], refs[2 * n + n_after + 1]
        token = refs[-1]
        for outgoing, _ in _split_copies(srcs_r, lands_r, send_sems, recv_sems, scatter):
            outgoing.start()
        token[...] = jnp.zeros_like(token)

    hbm = lambda a: pltpu.HBM(a.shape, a.dtype)
    res = pl.pallas_call(
        body, name=name,
        out_shape=(pltpu.SemaphoreType.DMA((n * (N_DEV - 1),)), pltpu.SemaphoreType.DMA((n * (N_DEV - 1),)),
                   *[hbm(a) for a in srcs], *[hbm(a) for a in lands], jax.ShapeDtypeStruct((8, 128), f32)),
        in_specs=[HBM_SPEC] * (2 * n) + [pl.BlockSpec(memory_space=pl.ANY)] * n_after,
        out_specs=(SEM_SPEC, SEM_SPEC, *[HBM_SPEC] * (2 * n), pl.BlockSpec(memory_space=pltpu.VMEM)),
        input_output_aliases={i: 2 + i for i in range(2 * n)},
        compiler_params=pltpu.CompilerParams(has_side_effects=DATAFLOW_EFFECT),
    )(*[pltpu.with_memory_space_constraint(a, pltpu.HBM) for a in list(srcs) + list(lands)], *after)
    return (res[0], res[1], list(res[2:2 + n]), list(res[2 + n:2 + 2 * n])), res[-1]


def _exchange_wait(handle, after, name, scatter):
    send_sems, recv_sems, srcs, lands = handle
    n = len(srcs)

    def body(*refs):
        srcs_r, lands_r = refs[:n], refs[n:2 * n]
        send_s, recv_s = refs[2 * n], refs[2 * n + 1]
        for outgoing, incoming in _split_copies(srcs_r, lands_r, send_s, recv_s, scatter):
            outgoing.wait_send()
            incoming.wait_recv()

    hbm = lambda a: pltpu.HBM(a.shape, a.dtype)
    res = pl.pallas_call(
        body, name=name, out_shape=tuple(hbm(a) for a in list(srcs) + list(lands)),
        in_specs=[HBM_SPEC] * (2 * n) + [SEM_SPEC, SEM_SPEC, pl.BlockSpec(memory_space=pl.ANY)],
        out_specs=tuple([HBM_SPEC] * (2 * n)),
        input_output_aliases={i: i for i in range(2 * n)},
        compiler_params=pltpu.CompilerParams(has_side_effects=DATAFLOW_EFFECT),
    )(*srcs, *lands, send_sems, recv_sems, after)
    return list(res[n:])


def _chip_peers(x, y, c):
    return [(x, y, 1 - c)] + [_plane_pos(x, y, q) + (c,) for q in (2, 1, 3)]


def _chip_gather_start(shards, lands, name, after=()):
    n, n_after = len(shards), len(after)

    def body(*refs):
        srcs_r, lands_r = refs[:n], refs[n:2 * n]
        send_sems, recv_sems = refs[2 * n + n_after], refs[2 * n + n_after + 1]
        x, y, c, me = _position()
        for a in range(n):
            for k, peer in enumerate(_chip_peers(x, y, c)):
                pltpu.make_async_remote_copy(
                    src_ref=srcs_r[a], dst_ref=lands_r[a].at[me], send_sem=send_sems.at[4 * a + k],
                    recv_sem=recv_sems.at[4 * a + k], device_id=peer, device_id_type=MESH).start()
        refs[-1][...] = jnp.zeros_like(refs[-1])

    hbm = lambda a: pltpu.HBM(a.shape, a.dtype)
    res = pl.pallas_call(
        body, name=name,
        out_shape=(pltpu.SemaphoreType.DMA((4 * n,)), pltpu.SemaphoreType.DMA((4 * n,)),
                   *[hbm(a) for a in shards], *[hbm(a) for a in lands], jax.ShapeDtypeStruct((8, 128), f32)),
        in_specs=[HBM_SPEC] * (2 * n) + [ANY_SPEC] * n_after,
        out_specs=(SEM_SPEC, SEM_SPEC, *[HBM_SPEC] * (2 * n), pl.BlockSpec(memory_space=pltpu.VMEM)),
        input_output_aliases={i: 2 + i for i in range(2 * n)},
        compiler_params=pltpu.CompilerParams(has_side_effects=DATAFLOW_EFFECT),
    )(*[pltpu.with_memory_space_constraint(a, pltpu.HBM) for a in list(shards) + list(lands)], *after)
    return (res[0], res[1], list(res[2:2 + n]), list(res[2 + n:2 + 2 * n])), res[-1]


def _chip_gather_forward(handle, after, name):
    send_sems, recv_sems, srcs, lands = handle
    n = len(srcs)

    def body(*refs):
        srcs_r, lands_r = refs[:n], refs[n:2 * n]
        send1, recv1 = refs[2 * n], refs[2 * n + 1]
        send2, recv2 = refs[2 * n + 3], refs[2 * n + 4]
        x, y, c, me = _position()
        peers = _chip_peers(x, y, c)
        for a in range(n):
            for k, peer in enumerate(peers):
                mk = lambda slot: pltpu.make_async_remote_copy(
                    src_ref=srcs_r[a], dst_ref=lands_r[a].at[slot], send_sem=send1.at[4 * a + k],
                    recv_sem=recv1.at[4 * a + k], device_id=peer, device_id_type=MESH)
                mk(me).wait_send()
                mk(_lin(peer)).wait_recv()
        for a in range(n):
            for k, peer in enumerate(peers[1:]):
                slot = _lin(peer)
                pltpu.make_async_remote_copy(
                    src_ref=lands_r[a].at[slot], dst_ref=lands_r[a].at[slot], send_sem=send2.at[3 * a + k],
                    recv_sem=recv2.at[3 * a + k], device_id=peers[0], device_id_type=MESH).start()

    hbm = lambda a: pltpu.HBM(a.shape, a.dtype)
    res = pl.pallas_call(
        body, name=name,
        out_shape=(pltpu.SemaphoreType.DMA((3 * n,)), pltpu.SemaphoreType.DMA((3 * n,)), *[hbm(a) for a in lands]),
        in_specs=[HBM_SPEC] * (2 * n) + [SEM_SPEC, SEM_SPEC, ANY_SPEC],
        out_specs=(SEM_SPEC, SEM_SPEC, *[HBM_SPEC] * n),
        input_output_aliases={n + i: 2 + i for i in range(n)},
        compiler_params=pltpu.CompilerParams(has_side_effects=DATAFLOW_EFFECT),
    )(*srcs, *lands, send_sems, recv_sems, after)
    return (res[0], res[1], list(res[2:]))


def _chip_gather_wait(handle, after, name):
    send_sems, recv_sems, lands = handle
    n = len(lands)

    def body(*refs):
        lands_r, send2, recv2 = refs[:n], refs[n], refs[n + 1]
        x, y, c, me = _position()
        peers = _chip_peers(x, y, c)
        for a in range(n):
            for k, (px, py, pc) in enumerate(peers[1:]):
                mk = lambda slot: pltpu.make_async_remote_copy(
                    src_ref=lands_r[a].at[slot], dst_ref=lands_r[a].at[slot], send_sem=send2.at[3 * a + k],
                    recv_sem=recv2.at[3 * a + k], device_id=peers[0], device_id_type=MESH)
                mk(_lin((px, py, pc))).wait_send()
                mk(_lin((px, py, 1 - pc))).wait_recv()

    hbm = lambda a: pltpu.HBM(a.shape, a.dtype)
    res = pl.pallas_call(
        body, name=name, out_shape=tuple(hbm(a) for a in lands),
        in_specs=[HBM_SPEC] * n + [SEM_SPEC, SEM_SPEC, ANY_SPEC], out_specs=tuple([HBM_SPEC] * n),
        input_output_aliases={i: i for i in range(n)},
        compiler_params=pltpu.CompilerParams(has_side_effects=DATAFLOW_EFFECT),
    )(*lands, send_sems, recv_sems, after)
    return list(res)


def _own_block_filled(block, me):
    land = lax.empty((N_DEV,) + block.shape, block.dtype)
    return lax.dynamic_update_index_in_dim(land, block, me, 0)


def _staged_copy(src, dst, buf, in_sems, out_sems, rows, chunk):
    n = rows // chunk

    def rd(i):
        return pltpu.make_async_copy(src.at[pl.ds(i * chunk, chunk)], buf.at[i % 2], in_sems.at[i % 2])

    def wr(i):
        return pltpu.make_async_copy(buf.at[i % 2], dst.at[pl.ds(i * chunk, chunk)], out_sems.at[i % 2])

    rd(0).start()
    for i in range(n):
        if i + 1 < n:
            if i >= 1:
                wr(i - 1).wait()
            rd(i + 1).start()
        rd(i).wait()
        wr(i).start()
    for i in range(max(n - 2, 0), n):
        wr(i).wait()


def _all_gather_2level(shards, name):
    n = len(shards)
    chunks = [_pick(s.shape[0], (416, 512, 256, 160, 128, 64, 16)) for s in shards]

    def body(*refs):
        ins, outs = refs[:n], refs[n:2 * n]
        send_sems, recv_sems, in_sems, out_sems = refs[2 * n:2 * n + 4]
        bufs = refs[2 * n + 4:]
        x, y, c, me = _position()
        sib, xn, yn, dg = (x, y, 1 - c), (1 - x, y, c), (x, 1 - y, c), (1 - x, 1 - y, c)

        def cp(a, k, src, slot, to):
            return pltpu.make_async_remote_copy(src_ref=src, dst_ref=outs[a].at[slot], send_sem=send_sems.at[a, k],
                                                recv_sem=recv_sems.at[a, k], device_id=to, device_id_type=MESH)

        for a in range(n):
            for k, to in ((0, sib), (1, xn), (2, yn)):
                cp(a, k, ins[a], me, to).start()
        for a in range(n):
            cp(a, 1, ins[a], _lin(xn), xn).wait_recv()
            cp(a, 3, outs[a].at[_lin(xn)], _lin(xn), sib).start()

            @pl.when(c == 0)
            def _():
                cp(a, 5, outs[a].at[_lin(xn)], _lin(xn), yn).start()

            cp(a, 2, ins[a], _lin(yn), yn).wait_recv()
            cp(a, 4, outs[a].at[_lin(yn)], _lin(yn), sib).start()

            @pl.when(c == 1)
            def _():
                cp(a, 5, outs[a].at[_lin(yn)], _lin(yn), xn).start()

        for a in range(n):
            cp(a, 5, ins[a], _lin(dg), xn).wait_recv()
            cp(a, 6, outs[a].at[_lin(dg)], _lin(dg), sib).start()
        for a in range(n):
            _staged_copy(ins[a], outs[a].at[me], bufs[a], in_sems.at[a], out_sems.at[a], shards[a].shape[0], chunks[a])
        for a in range(n):
            for k, origin in ((0, sib), (3, (1 - x, y, 1 - c)), (4, (x, 1 - y, 1 - c)), (6, (1 - x, 1 - y, 1 - c))):
                cp(a, k, ins[a], _lin(origin), sib).wait_recv()
            for k in range(7):
                cp(a, k, ins[a], me, sib).wait_send()

    any_spec = pl.BlockSpec(memory_space=pl.ANY)
    return pl.pallas_call(
        body, name=name, out_shape=[jax.ShapeDtypeStruct((N_DEV,) + s.shape, s.dtype) for s in shards],
        in_specs=[any_spec] * n, out_specs=[any_spec] * n,
        scratch_shapes=[pltpu.SemaphoreType.DMA((n, 7)), pltpu.SemaphoreType.DMA((n, 7)),
                        pltpu.SemaphoreType.DMA((n, 2)), pltpu.SemaphoreType.DMA((n, 2))]
        + [pltpu.VMEM((2, ch, s.shape[1]), s.dtype) for ch, s in zip(chunks, shards)],
    )(*shards)


def _plane_pos(x, y, q):
    return ((1 - x) if q & 2 else x, (1 - y) if q & 1 else y)


def _scatter_call(body, name, ins, out_shape, sems_per_array):
    n = len(ins)
    any_spec = pl.BlockSpec(memory_space=pl.ANY)
    return pl.pallas_call(
        body, name=name, out_shape=out_shape, in_specs=[any_spec] * n, out_specs=[any_spec] * n,
        scratch_shapes=[pltpu.SemaphoreType.DMA((n, sems_per_array)), pltpu.SemaphoreType.DMA((n, sems_per_array))],
    )(*ins)


def _scatter_d2d(gs, name):
    n = len(gs)

    def body(*refs):
        g_refs, recv_refs, send_sems, recv_sems = refs[:n], refs[n:2 * n], refs[2 * n], refs[2 * n + 1]
        x, y, c, me = _position()
        sib = (x, y, 1 - c)
        sends = []
        for a in range(n):
            for q in range(4):
                px, py = _plane_pos(x, y, q)
                cp = pltpu.make_async_remote_copy(
                    src_ref=g_refs[a].at[_lin((px, py, 1 - c))], dst_ref=recv_refs[a].at[q],
                    send_sem=send_sems.at[a, q], recv_sem=recv_sems.at[a, q], device_id=sib, device_id_type=MESH)
                cp.start()
                sends.append(cp)
        for cp in sends:
            cp.wait_recv()
        for cp in sends:
            cp.wait_send()

    return _scatter_call(body, name, gs, [jax.ShapeDtypeStruct((4,) + g.shape[1:], g.dtype) for g in gs], 4)


def _scatter_ici_first(hs, name):
    n = len(hs)

    def body(*refs):
        h_refs, recv_refs, send_sems, recv_sems = refs[:n], refs[n:2 * n], refs[2 * n], refs[2 * n + 1]
        x, y, c, me = _position()
        xn, yn = (1 - x, y, c), (x, 1 - y, c)

        def cp(a, k, q, to):
            return pltpu.make_async_remote_copy(
                src_ref=h_refs[a].at[q], dst_ref=recv_refs[a].at[k], send_sem=send_sems.at[a, k],
                recv_sem=recv_sems.at[a, k], device_id=to, device_id_type=MESH)

        @pl.when(c == 0)
        def _():
            for a in range(n):
                cp(a, 0, 2, xn).start()
                cp(a, 1, 3, xn).start()

        @pl.when(c == 1)
        def _():
            for a in range(n):
                cp(a, 0, 1, yn).start()
                cp(a, 1, 3, yn).start()

        for a in range(n):
            for k in range(2):
                cp(a, k, 0, xn).wait_recv()
        for a in range(n):
            for k in range(2):
                cp(a, k, 0, xn).wait_send()

    return _scatter_call(body, name, hs, [jax.ShapeDtypeStruct((2,) + h.shape[1:], h.dtype) for h in hs], 2)


def _scatter_ici_second(k1s, name):
    n = len(k1s)

    def body(*refs):
        k_refs, recv_refs, send_sems, recv_sems = refs[:n], refs[n:2 * n], refs[2 * n], refs[2 * n + 1]
        x, y, c, me = _position()
        xn, yn = (1 - x, y, c), (x, 1 - y, c)

        def cp(a, to):
            return pltpu.make_async_remote_copy(src_ref=k_refs[a], dst_ref=recv_refs[a], send_sem=send_sems.at[a, 0],
                                                recv_sem=recv_sems.at[a, 0], device_id=to, device_id_type=MESH)

        @pl.when(c == 0)
        def _():
            for a in range(n):
                cp(a, yn).start()

        @pl.when(c == 1)
        def _():
            for a in range(n):
                cp(a, xn).start()

        for a in range(n):
            cp(a, xn).wait_recv()
        for a in range(n):
            cp(a, xn).wait_send()

    return _scatter_call(body, name, k1s, [jax.ShapeDtypeStruct(k.shape, k.dtype) for k in k1s], 1)


def _add_blocks(a, a_idx, b, b_idx, out_dtype, name):
    rows, cols = a.shape[1:]
    n = a_idx.shape[0]
    tm = _pick(rows, (512, 256, 160, 128, 32, 16))

    def body(ia_ref, ib_ref, a_ref, b_ref, o_ref):
        o_ref[...] = (a_ref[...].astype(f32) + b_ref[...].astype(f32)).astype(o_ref.dtype)

    grid_spec = pltpu.PrefetchScalarGridSpec(
        num_scalar_prefetch=2, grid=(n, rows // tm),
        in_specs=[pl.BlockSpec((None, tm, cols), lambda j, i, ia, ib: (ia[j], i, 0)),
                  pl.BlockSpec((None, tm, cols), lambda j, i, ia, ib: (ib[j], i, 0))],
        out_specs=pl.BlockSpec((None, tm, cols), lambda j, i, ia, ib: (j, i, 0)))
    return pl.pallas_call(body, name=name, out_shape=jax.ShapeDtypeStruct((n, rows, cols), out_dtype),
                          grid_spec=grid_spec, compiler_params=_cparams("parallel", "parallel"))(a_idx, b_idx, a, b)


def _reduce_scatter(gs, names, tag):
    x, y, c, me = _position()
    i32 = lambda *v: jnp.stack([jnp.asarray(t, jnp.int32) for t in v])
    recvs = _scatter_d2d(gs, tag + "_d2d")
    own_idx = i32(*[_lin(_plane_pos(x, y, q) + (c,)) for q in range(4)])
    hs = [_add_blocks(g, own_idx, r, i32(0, 1, 2, 3), bf16, f"{tag}_add_chip_{nm}")
          for g, r, nm in zip(gs, recvs, names)]
    recv2s = _scatter_ici_first(hs, tag + "_ici_first")
    k1s = [_add_blocks(h, i32(1 + c), r2, i32(1), bf16, f"{tag}_add_onward_{nm}")[0]
           for h, r2, nm in zip(hs, recv2s, names)]
    lasts = _scatter_ici_second(k1s, tag + "_ici_second")
    return [[(h, 1), (r2, 1), (last[None], 1)] for h, r2, last in zip(hs, recv2s, lasts)]


ANY_SPEC = pl.BlockSpec(memory_space=pl.ANY)


def _mm(a, b, name, ta=False, tb=False, out_dtype=f32, after=()):
    if ta:
        k_dim, m_dim = a.shape
    else:
        m_dim, k_dim = a.shape
    if tb:
        n_dim, k2 = b.shape
    else:
        k2, n_dim = b.shape
    assert k_dim == k2, (a.shape, b.shape)
    assert a.dtype == bf16 and b.dtype == bf16
    bm = _pick(m_dim, (512, 768, 640, 256, 128))
    bn = _pick(n_dim, (512, 640, 256, 128))
    bk = _pick(k_dim, (1024, 1280, 768, 512))
    nk = k_dim // bk
    a_spec = (pl.BlockSpec((bk, bm), lambda i, j, k: (k, i)) if ta
              else pl.BlockSpec((bm, bk), lambda i, j, k: (i, k)))
    b_spec = (pl.BlockSpec((bn, bk), lambda i, j, k: (j, k)) if tb
              else pl.BlockSpec((bk, bn), lambda i, j, k: (k, j)))
    dims = (((0 if ta else 1,), (1 if tb else 0,)), ((), ()))

    n_after = len(after)

    def body_single(a_ref, b_ref, *rest):
        o_ref = rest[n_after]
        o_ref[...] = lax.dot_general(a_ref[...], b_ref[...], dims, preferred_element_type=f32).astype(o_ref.dtype)

    def body(a_ref, b_ref, *rest):
        o_ref, acc_ref = rest[n_after:]
        k = pl.program_id(2)

        @pl.when(k == 0)
        def _():
            acc_ref[...] = jnp.zeros_like(acc_ref)

        acc_ref[...] += lax.dot_general(a_ref[...], b_ref[...], dims, preferred_element_type=f32)

        @pl.when(k == nk - 1)
        def _():
            o_ref[...] = acc_ref[...].astype(o_ref.dtype)

    return pl.pallas_call(
        body_single if nk == 1 else body, name=name, out_shape=jax.ShapeDtypeStruct((m_dim, n_dim), out_dtype),
        grid=(m_dim // bm, n_dim // bn, nk), in_specs=[a_spec, b_spec] + [ANY_SPEC] * n_after,
        out_specs=pl.BlockSpec((bm, bn), lambda i, j, k: (i, j)),
        scratch_shapes=[] if nk == 1 else [pltpu.VMEM((bm, bn), f32)],
        compiler_params=_cparams("parallel", "parallel", "arbitrary"),
    )(a, b, *after)


def _rin(arr, width=None, cb=0, roff=0):
    return (arr, arr.shape[1] if width is None else width, cb, roff)


def _rowcall(fn, name, rows, tm, row_ins, par_ins, row_outs, acc_outs=(), after=()):
    nr, npar, nro, n_after = len(row_ins), len(par_ins), len(row_outs), len(after)
    in_specs, args = [], []
    for arr, width, cb, roff in row_ins:
        if roff >= 0:
            imap = lambda i, cb=cb, roff=roff: (i + roff, cb)
        else:
            imap = lambda i, cb=cb, roff=roff: (jnp.maximum(i + roff, 0), cb)
        in_specs.append(pl.BlockSpec((tm, width), imap))
        args.append(arr)
    for p in par_ins:
        in_specs.append(pl.BlockSpec(p.shape, lambda i: (0, 0)))
        args.append(p)
    out_shape, out_specs = [], []
    for width, dt in row_outs:
        out_shape.append(jax.ShapeDtypeStruct((rows, width), dt))
        out_specs.append(pl.BlockSpec((tm, width), lambda i: (i, 0)))
    for p, width in acc_outs:
        out_shape.append(jax.ShapeDtypeStruct((p, width), f32))
        out_specs.append(pl.BlockSpec((p, width), lambda i: (0, 0)))

    def body(*refs):
        i = pl.program_id(0)
        res = fn(i, *[r[...] for r in refs[:nr + npar]])
        outs = refs[nr + npar + n_after:]
        for o, v in zip(outs[:nro], res[:nro]):
            o[...] = v.astype(o.dtype)
        if acc_outs:
            @pl.when(i == 0)
            def _():
                for o in outs[nro:]:
                    o[...] = jnp.zeros_like(o)

            for o, v in zip(outs[nro:], res[nro:]):
                o[...] += v

    return pl.pallas_call(
        body, name=name, out_shape=out_shape, grid=(rows // tm,), in_specs=in_specs + [ANY_SPEC] * n_after,
        out_specs=out_specs, compiler_params=_cparams("arbitrary"),
    )(*args, *after)


def _rms(x, g):
    return x * lax.rsqrt(jnp.mean(x * x, axis=-1, keepdims=True) + EPS) * g


def _normmod(x, g, sc, sh):
    return _rms(x, g) * (1.0 + sc) + sh


def _rows2(v0, v1):
    rid = lax.broadcasted_iota(jnp.int32, (2, v0.shape[1]), 0)
    return jnp.where(rid == 0, v0, v1)


def _gelu(x):
    return 0.5 * x * (1.0 + jnp.tanh(0.7978845608028654 * (x + 0.044715 * (x * x * x))))


def _sigmoid(x):
    return 0.5 * (jnp.tanh(0.5 * x) + 1.0)


def _coeff_parts(pre_a, pre_x, ba, bx, lam):
    r = _sigmoid(pre_a + ba)
    ig = _sigmoid(pre_x + bx)
    nl = -lam
    sp = jnp.maximum(nl, 0.0) + jnp.log(1.0 + jnp.exp(-jnp.abs(nl)))
    la = -RG_C * r * sp
    a = jnp.exp(la)
    one_minus_a2 = -jnp.tanh(la) * (a * a + 1.0)
    inv_m = lax.rsqrt(one_minus_a2)
    return r, ig, sp, a, one_minus_a2 * inv_m, inv_m


def _coeff(pre_a, pre_x, u, ba, bx, lam):
    _, ig, _, a, m, _ = _coeff_parts(pre_a, pre_x, ba, bx, lam)
    return a, m * (ig * u)


def _coeff_bwd(pre_a, pre_x, u, ba, bx, lam, da, db):
    r, ig, sp, a, m, inv_m = _coeff_parts(pre_a, pre_x, ba, bx, lam)
    dbu = db * u
    dig = dbu * m
    dm = dbu * ig
    dla = a * (da - dm * a * inv_m)
    dpa = dla * (-RG_C * sp) * (r * (1.0 - r))
    dpx = dig * (ig * (1.0 - ig))
    dsp = jnp.sum(dla * (-RG_C * r), axis=0, keepdims=True)
    dlam = -dsp * _sigmoid(-lam)
    return (dpa, dpx, db * m * ig, jnp.sum(dpa, axis=0, keepdims=True), jnp.sum(dpx, axis=0, keepdims=True), dlam)


SCAN_CHUNK = 256


def _scan_call(a, v, chunk_of, reverse, name, backward):
    rows, width = a.shape
    n_out = 1 if backward else 2
    nt = SCAN_CHUNK // 8

    def body(a_ref, v_ref, *rest):
        outs, state_ref = rest[:-1], rest[-1]

        @pl.when(pl.program_id(0) == 0)
        def _():
            state_ref[...] = jnp.zeros_like(state_ref)

        rid = lax.broadcasted_iota(jnp.int32, (8, width), 0)

        def tile(j, st):
            t0 = pl.multiple_of((nt - 1 - j if reverse else j) * 8, 8)
            at = a_ref[pl.ds(t0, 8), :]
            vt = v_ref[pl.ds(t0, 8), :]
            out = jnp.zeros((8, width), f32)
            prev = jnp.zeros((8, width), f32)
            for i in (range(7, -1, -1) if reverse else range(8)):
                if backward:
                    g = vt[i:i + 1] + st
                    st = at[i:i + 1] * g
                    out = jnp.where(rid == i, g, out)
                else:
                    prev = jnp.where(rid == i, st, prev)
                    st = at[i:i + 1] * st + vt[i:i + 1]
                    out = jnp.where(rid == i, st, out)
            outs[0][pl.ds(t0, 8), :] = out
            if not backward:
                outs[1][pl.ds(t0, 8), :] = prev
            return st

        state_ref[0:1, :] = lax.fori_loop(0, nt, tile, state_ref[0:1, :])

    spec = pl.BlockSpec((SCAN_CHUNK, width), lambda t: (chunk_of(t), 0))
    return pl.pallas_call(
        body, name=name, out_shape=[jax.ShapeDtypeStruct((rows, width), f32)] * n_out,
        grid=(rows // SCAN_CHUNK,), in_specs=[spec, spec], out_specs=[spec] * n_out,
        scratch_shapes=[pltpu.VMEM((8, width), f32)],
        compiler_params=_cparams("arbitrary"),
    )(a, v)


CONV_CHUNK = 256


def _fill_padded(pad_ref, src_ref, start, n):
    cb = pad_ref.shape[1]
    pad_ref[pl.ds(0, HALO), :] = jnp.zeros((HALO, cb), f32)
    pad_ref[pl.ds(HALO, n), :] = src_ref[pl.ds(start, n), :].astype(f32)
    pad_ref[pl.ds(HALO + n, HALO), :] = jnp.zeros((HALO, cb), f32)


def _dwconv_fwd(x, x_cb0, w, b, taps, pad_left, segments, cb, name, emit_bf16):
    rows = x.shape[0]
    width = w.shape[1]

    def body(x_ref, w_ref, b_ref, *rest):
        outs, xp = rest[:-1], rest[-1]
        for start, n in segments:
            _fill_padded(xp, x_ref, start, n)
            for c0 in range(0, n, CONV_CHUNK):
                acc = jnp.zeros((CONV_CHUNK, cb), f32) + b_ref[...]
                for k in range(taps):
                    acc = acc + w_ref[k:k + 1, :] * xp[pl.ds(HALO + c0 + k - pad_left, CONV_CHUNK), :]
                for o in outs:
                    o[pl.ds(start + c0, CONV_CHUNK), :] = acc.astype(o.dtype)

    out_dtypes = [f32, bf16] if emit_bf16 else [f32]
    return pl.pallas_call(
        body, name=name, out_shape=[jax.ShapeDtypeStruct((rows, width), dt) for dt in out_dtypes],
        grid=(width // cb,),
        in_specs=[pl.BlockSpec((rows, cb), lambda j: (0, j + x_cb0)), pl.BlockSpec((taps, cb), lambda j: (0, j)),
                  pl.BlockSpec((1, cb), lambda j: (0, j))],
        out_specs=[pl.BlockSpec((rows, cb), lambda j: (0, j))] * len(out_dtypes),
        scratch_shapes=[pltpu.VMEM((rows + 2 * HALO, cb), f32)],
        compiler_params=_cparams("parallel"),
    )(x, w, b)


def _dwconv_bwd(douts, x, x_cb0, w, taps, pad_left, segments, cb, name, dx_dtype):
    rows = x.shape[0]
    width = w.shape[1]
    nd = len(douts)

    def body(*refs):
        d_refs, x_ref, w_ref = refs[:nd], refs[nd], refs[nd + 1]
        dx_ref, dw_ref, db_ref, xp, dp, dsum = refs[nd + 2:]
        dw_ref[...] = jnp.zeros_like(dw_ref)
        db_ref[...] = jnp.zeros_like(db_ref)
        if nd > 1:
            total = d_refs[0][...]
            for r in d_refs[1:]:
                total = total + r[...]
            dsum[...] = total
            d_ref = dsum
        else:
            d_ref = d_refs[0]
        for start, n in segments:
            _fill_padded(xp, x_ref, start, n)
            _fill_padded(dp, d_ref, start, n)
            for c0 in range(0, n, CONV_CHUNK):
                dchunk = dp[pl.ds(HALO + c0, CONV_CHUNK), :]
                db_ref[...] += jnp.sum(dchunk, axis=0, keepdims=True)
                acc = jnp.zeros((CONV_CHUNK, cb), f32)
                for k in range(taps):
                    acc = acc + w_ref[k:k + 1, :] * dp[pl.ds(HALO + c0 + pad_left - k, CONV_CHUNK), :]
                    xs = xp[pl.ds(HALO + c0 + k - pad_left, CONV_CHUNK), :]
                    dw_ref[k:k + 1, :] += jnp.sum(dchunk * xs, axis=0, keepdims=True)
                dx_ref[pl.ds(start + c0, CONV_CHUNK), :] = acc.astype(dx_ref.dtype)

    dspec = pl.BlockSpec((rows, cb), lambda j: (0, j))
    return pl.pallas_call(
        body, name=name,
        out_shape=[jax.ShapeDtypeStruct((rows, width), dx_dtype), jax.ShapeDtypeStruct((taps, width), f32),
                   jax.ShapeDtypeStruct((1, width), f32)],
        grid=(width // cb,),
        in_specs=[dspec] * nd + [pl.BlockSpec((rows, cb), lambda j: (0, j + x_cb0)),
                                 pl.BlockSpec((taps, cb), lambda j: (0, j))],
        out_specs=[dspec, pl.BlockSpec((taps, cb), lambda j: (0, j)), pl.BlockSpec((1, cb), lambda j: (0, j))],
        scratch_shapes=[pltpu.VMEM((rows + 2 * HALO, cb), f32), pltpu.VMEM((rows + 2 * HALO, cb), f32),
                        pltpu.VMEM((rows, cb), f32)],
        compiler_params=_cparams("parallel"),
    )(*douts, x, w)


def _ada_forward(c16, w_ada, b_loc):
    def body(c_ref, w_ref, b_ref, o_ref):
        cv = c_ref[...]
        s = (cv * _sigmoid(cv)).astype(bf16)
        o_ref[0] = jnp.dot(s, w_ref[0].astype(bf16), preferred_element_type=f32) + b_ref[0]

    return pl.pallas_call(
        body, name="ada_forward", out_shape=jax.ShapeDtypeStruct((2, 16, ADA_SHARD), f32), grid=(2,),
        in_specs=[pl.BlockSpec((16, D), lambda l: (0, 0)), pl.BlockSpec((1, D, ADA_SHARD), lambda l: (l, 0, 0)),
                  pl.BlockSpec((1, 1, ADA_SHARD), lambda l: (l, 0, 0))],
        out_specs=pl.BlockSpec((1, 16, ADA_SHARD), lambda l: (l, 0, 0)),
        compiler_params=_cparams("parallel"),
    )(c16, w_ada, b_loc)


def _ada_backward(c16, g16, w_ada):
    def body(c_ref, g_ref, w_ref, dw_ref, ds_ref):
        cv = c_ref[...]
        s = (cv * _sigmoid(cv)).astype(bf16)
        g = g_ref[0].astype(bf16)
        dw_ref[0] = lax.dot_general(s, g, (((0,), (0,)), ((), ())), preferred_element_type=f32)
        ds = lax.dot_general(g, w_ref[0].astype(bf16), (((1,), (1,)), ((), ())), preferred_element_type=f32)
        cc = cv[8:9]
        sg = _sigmoid(cc)
        dsilu = sg * (1.0 + cc * (1.0 - sg))
        ds_ref[0] = jnp.zeros((8, D), f32) + jnp.sum(ds[8:16], axis=0, keepdims=True) * dsilu

    return pl.pallas_call(
        body, name="ada_backward",
        out_shape=[jax.ShapeDtypeStruct((2, D, ADA_SHARD), f32), jax.ShapeDtypeStruct((2, 8, D), f32)], grid=(2,),
        in_specs=[pl.BlockSpec((16, D), lambda l: (0, 0)), pl.BlockSpec((1, 16, ADA_SHARD), lambda l: (l, 0, 0)),
                  pl.BlockSpec((1, D, ADA_SHARD), lambda l: (l, 0, 0))],
        out_specs=[pl.BlockSpec((1, D, ADA_SHARD), lambda l: (l, 0, 0)), pl.BlockSpec((1, 8, D), lambda l: (l, 0, 0))],
        compiler_params=_cparams("parallel"),
    )(c16, g16, w_ada)


def _adamw(pieces, w, m, v, name):
    rows, cols = w.shape
    n_arr = len(pieces)
    counts = [cnt for _, cnt in pieces]
    pieces = [p for p, _ in pieces]
    tm = 256 if (rows % 256 == 0 and rows > 256) else rows

    def body(*refs):
        p_refs = refs[:n_arr]
        w_ref, m_ref, v_ref, g_ref, d_ref, nm_ref, nv_ref = refs[n_arr:]
        g = None
        for p_ref in p_refs:
            for j in range(p_ref.shape[0]):
                term = p_ref[j].astype(f32)
                g = term if g is None else g + term
        m2 = ADAM_B1 * m_ref[...] + (1.0 - ADAM_B1) * g
        v2 = ADAM_B2 * v_ref[...] + (1.0 - ADAM_B2) * (g * g)
        m_hat = m2 / (1.0 - ADAM_B1 ** ADAM_STEP)
        v_hat = v2 / (1.0 - ADAM_B2 ** ADAM_STEP)
        g_ref[...] = g
        d_ref[...] = -ADAM_LR * (m_hat / (jnp.sqrt(v_hat) + ADAM_EPS) + ADAM_WD * w_ref[...])
        nm_ref[...] = m2
        nv_ref[...] = v2

    spec = pl.BlockSpec((tm, cols), lambda i: (i, 0))
    return pl.pallas_call(
        body, name=name, out_shape=[jax.ShapeDtypeStruct((rows, cols), f32)] * 4, grid=(rows // tm,),
        in_specs=[pl.BlockSpec((cnt, tm, cols), lambda i: (0, i, 0)) for cnt in counts] + [spec, spec, spec],
        out_specs=[spec] * 4, compiler_params=_cparams("parallel"),
    )(*pieces, w, m, v)


MLP_TM = 256
FB = F // N_DEV


def _stack_rows(vals, n):
    cols = vals[0].shape[1]
    rid = lax.broadcasted_iota(jnp.int32, (n, cols), 0)
    out = jnp.zeros((n, cols), f32)
    for k, v in enumerate(vals):
        out = jnp.where(rid == k, v, out)
    return out


N_MLP_PARAMS = 9


class _ParamRows:
    def __init__(self, ref):
        self.ref = ref

    def __getitem__(self, sl):
        return self.ref[8 * sl.start:8 * sl.start + 1, :]


def _resident(shape, imap):
    return pl.BlockSpec(shape, imap, pipeline_mode=pl.Buffered(1))


def _mlp_forward(xa, xa_roff, out_prev, par, w_in, w_out, layer, name):
    def body(xa_ref, op_ref, par_ref, win_ref, wout_ref, x1_ref, h_ref, r_ref, mo_ref, x2_ref, hn_ref):
        p = _ParamRows(par_ref)
        x1 = xa_ref[...] + p[0:1] * (op_ref[...] + p[1:2])
        h = _normmod(x1, p[2:3], p[3:4], p[4:5]).astype(bf16)
        x1_ref[...] = x1
        h_ref[...] = h
        mo = jnp.zeros((MLP_TM, D), f32)
        for j in range(N_DEV):
            r = jnp.maximum(jnp.dot(h, win_ref[j], preferred_element_type=f32), 0.0)
            r_ref[:, j * FB:(j + 1) * FB] = r.astype(bf16)
            mo = mo + jnp.dot((r * r).astype(bf16), wout_ref[j], preferred_element_type=f32)
        mo_ref[...] = mo.astype(bf16)
        x2 = x1 + p[5:6] * mo
        x2_ref[...] = x2
        hn_ref[...] = _normmod(x2, p[6:7], p[7:8], p[8:9]).astype(bf16)

    row = lambda width: pl.BlockSpec((MLP_TM, width), lambda i: (i, 0))
    return pl.pallas_call(
        body, name=name, grid=(T_LAT // MLP_TM,),
        out_shape=[jax.ShapeDtypeStruct((T_LAT, D), f32), jax.ShapeDtypeStruct((T_LAT, D), bf16),
                   jax.ShapeDtypeStruct((T_LAT, F), bf16), jax.ShapeDtypeStruct((T_LAT, D), bf16),
                   jax.ShapeDtypeStruct((T_LAT, D), f32), jax.ShapeDtypeStruct((T_LAT, D), bf16)],
        in_specs=[pl.BlockSpec((MLP_TM, D), lambda i: (i + xa_roff, 0)), row(D), pl.BlockSpec((8 * N_MLP_PARAMS, D), lambda i: (0, 0)),
                  _resident((N_DEV, None, D, FB), lambda i: (0, layer, 0, 0)),
                  _resident((N_DEV, None, FB, D), lambda i: (0, layer, 0, 0))],
        out_specs=[row(D), row(D), row(F), row(D), row(D), row(D)],
        compiler_params=_cparams("parallel"),
    )(xa, out_prev, par, w_in, w_out)


def _mlp_backward(dx2, x1, r, mo, out_prev, par, w_in, w_out, layer, name):
    nt = (((1,), (1,)), ((), ()))

    def body(dx2_ref, x1_ref, r_ref, mo_ref, op_ref, par_ref, win_ref, wout_ref, dx1_ref, dop_ref, dmo_ref, dhid_ref,
             acc_ref):
        p = _ParamRows(par_ref)
        dx2v = dx2_ref[...]
        dmo = (p[5:6] * dx2v).astype(bf16)
        dmo_ref[...] = dmo
        dh = jnp.zeros((MLP_TM, D), f32)
        mo = mo_ref[...].astype(f32)
        for j in range(N_DEV):
            rf = r_ref[:, j * FB:(j + 1) * FB].astype(f32)
            dact = lax.dot_general(dmo, wout_ref[j], nt, preferred_element_type=f32)
            dhid = (dact * (2.0 * rf)).astype(bf16)
            dhid_ref[:, j * FB:(j + 1) * FB] = dhid
            dh = dh + lax.dot_general(dhid, win_ref[j], nt, preferred_element_type=f32)
        x1 = x1_ref[...]
        _, vjp = jax.vjp(_normmod, x1, p[2:3], p[3:4], p[4:5])
        dx, dng, dsc, dsh = vjp(dh)
        dx1 = dx2v + dx
        dx1_ref[...] = dx1
        dop_ref[...] = (p[0:1] * dx1).astype(bf16)
        sums = _stack_rows([jnp.sum(dx1 * (op_ref[...] + p[1:2]), axis=0, keepdims=True),
                            p[0:1] * jnp.sum(dx1, axis=0, keepdims=True), dng, dsc, dsh,
                            jnp.sum(dx2v * mo, axis=0, keepdims=True)], 8)

        @pl.when(pl.program_id(0) == 0)
        def _():
            acc_ref[...] = jnp.zeros_like(acc_ref)

        acc_ref[...] += sums

    row = lambda width: pl.BlockSpec((MLP_TM, width), lambda i: (i, 0))
    return pl.pallas_call(
        body, name=name, grid=(T_LAT // MLP_TM,),
        out_shape=[jax.ShapeDtypeStruct((T_LAT, D), f32), jax.ShapeDtypeStruct((T_LAT, D), bf16),
                   jax.ShapeDtypeStruct((T_LAT, D), bf16), jax.ShapeDtypeStruct((T_LAT, F), bf16),
                   jax.ShapeDtypeStruct((8, D), f32)],
        in_specs=[row(D), row(D), row(F), row(D), row(D), pl.BlockSpec((8 * N_MLP_PARAMS, D), lambda i: (0, 0)),
                  _resident((N_DEV, None, D, FB), lambda i: (0, layer, 0, 0)),
                  _resident((N_DEV, None, FB, D), lambda i: (0, layer, 0, 0))],
        out_specs=[row(D), row(D), row(D), row(F), pl.BlockSpec((8, D), lambda i: (0, 0))],
        compiler_params=_cparams("arbitrary"),
    )(dx2, x1, r, mo, out_prev, par, w_in, w_out)


def _mlp_weight_grads(h, dhid, r, dmo, layer, other, tag):
    tn = (((0,), (0,)), ((), ()))

    def body_in(h_ref, dhid_ref, *rest):
        rest[-1][...] = lax.dot_general(h_ref[...], dhid_ref[...], tn, preferred_element_type=f32).astype(bf16)

    def body_out(r_ref, dmo_ref, *rest):
        rf = r_ref[...].astype(f32)
        rest[-1][...] = lax.dot_general((rf * rf).astype(bf16), dmo_ref[...], tn,
                                        preferred_element_type=f32).astype(bf16)

    def call(body, name, operands, specs, block, prev):
        extra = [] if prev is None else [prev]
        return pl.pallas_call(
            body, name=name, grid=(N_DEV,), out_shape=jax.ShapeDtypeStruct((N_DEV, 2) + block, bf16),
            in_specs=specs + [pl.BlockSpec(memory_space=pl.ANY)] * len(extra),
            out_specs=pl.BlockSpec((None, None) + block, lambda j: (j, layer, 0, 0)),
            input_output_aliases={} if prev is None else {2: 0}, compiler_params=_cparams("parallel"),
        )(*operands, *extra)

    dw_in = call(body_in, tag + "_mlp_in_dw", [h, dhid],
                 [_resident((T_LAT, D), lambda j: (0, 0)), pl.BlockSpec((T_LAT, FB), lambda j: (0, j))], (D, FB),
                 None if other is None else other[0])
    dw_out = call(body_out, tag + "_mlp_out_dw", [r, dmo],
                  [pl.BlockSpec((T_LAT, FB), lambda j: (0, j)), _resident((T_LAT, D), lambda j: (0, 0))], (FB, D),
                  None if other is None else other[1])
    return dw_in, dw_out


def _pos_embed():
    n_rows = T_LAT // GRID_W
    q = D // 4
    omega = 1.0 / (POS_BASE ** (jnp.arange(q, dtype=f32) / q))
    er = jnp.arange(n_rows, dtype=jnp.int32).astype(f32)[:, None] * omega[None, :]
    ec = jnp.arange(GRID_W, dtype=jnp.int32).astype(f32)[:, None] * omega[None, :]
    by_row = jnp.concatenate([jnp.sin(er), jnp.cos(er)], axis=-1)[:, None, :]
    by_col = jnp.concatenate([jnp.sin(ec), jnp.cos(ec)], axis=-1)[None, :, :]
    full = jnp.concatenate([jnp.broadcast_to(by_row, (n_rows, GRID_W, D // 2)),
                            jnp.broadcast_to(by_col, (n_rows, GRID_W, D // 2))], axis=-1)
    return full.reshape(T_LAT, D)


HALF = R // 2
BLK_PER_HALF = N_BLK // 2
N_PARTS = 4


def _gate_matrix(w_a, w_x):
    eye = jnp.eye(BLK_PER_HALF, dtype=bf16)
    cols = []
    for h in range(2):
        for d in range(2):
            for w in (w_a, w_x):
                blocks = w[d, BLK_PER_HALF * h:BLK_PER_HALF * (h + 1)].astype(bf16)
                cols.append(jnp.einsum("hij,hg->higj", blocks, eye).reshape(HALF, HALF))
    return jnp.concatenate(cols, axis=1)


def _gate_blocks(dwg, part):
    out = []
    for h in range(2):
        blk = dwg[:, (N_PARTS * h + part) * HALF:(N_PARTS * h + part + 1) * HALF]
        blk = blk.reshape(BLK_PER_HALF, BLK, BLK_PER_HALF, BLK)
        out.append(jnp.moveaxis(jnp.diagonal(blk, axis1=0, axis2=2), -1, 0))
    return jnp.concatenate(out, axis=0)


def _gate_part(pre, part):
    return jnp.concatenate([pre[:, (N_PARTS * h + part) * HALF:(N_PARTS * h + part + 1) * HALF] for h in range(2)],
                           axis=1)


def _gate_unpart(parts):
    return jnp.concatenate([parts[p][:, h * HALF:(h + 1) * HALF] for h in range(2) for p in range(N_PARTS)], axis=1)


GATE_BM = 768


def _gates_fwd(u, wg):
    rows = u.shape[0]

    def body(u_ref, w_ref, o_ref):
        o_ref[...] = jnp.dot(u_ref[...], w_ref[...], preferred_element_type=f32)

    return pl.pallas_call(
        body, name="l0_gates", grid=(rows // GATE_BM, 2 * N_PARTS),
        out_shape=jax.ShapeDtypeStruct((rows, 2 * N_PARTS * HALF), f32),
        in_specs=[pl.BlockSpec((GATE_BM, HALF), lambda i, j: (i, j // N_PARTS)),
                  pl.BlockSpec((HALF, HALF), lambda i, j: (0, j))],
        out_specs=pl.BlockSpec((GATE_BM, HALF), lambda i, j: (i, j)),
        compiler_params=_cparams("parallel", "parallel"),
    )(u, wg)


def _gates_dx(dpre, wg):
    rows = dpre.shape[0]

    def body(d_ref, w_ref, o_ref, acc_ref):
        p = pl.program_id(2)

        @pl.when(p == 0)
        def _():
            acc_ref[...] = jnp.zeros_like(acc_ref)

        acc_ref[...] += lax.dot_general(d_ref[...], w_ref[...], (((1,), (1,)), ((), ())), preferred_element_type=f32)

        @pl.when(p == N_PARTS - 1)
        def _():
            o_ref[...] = acc_ref[...]

    return pl.pallas_call(
        body, name="l0_gates_dx", grid=(rows // GATE_BM, 2, N_PARTS), out_shape=jax.ShapeDtypeStruct((rows, R), f32),
        in_specs=[pl.BlockSpec((GATE_BM, HALF), lambda i, h, p: (i, N_PARTS * h + p)),
                  pl.BlockSpec((HALF, HALF), lambda i, h, p: (0, N_PARTS * h + p))],
        out_specs=pl.BlockSpec((GATE_BM, HALF), lambda i, h, p: (i, h)),
        scratch_shapes=[pltpu.VMEM((GATE_BM, HALF), f32)],
        compiler_params=_cparams("parallel", "parallel", "arbitrary"),
    )(dpre, wg)


def _gates_dw(u, dpre):
    rows = u.shape[0]

    def body(u_ref, d_ref, o_ref):
        o_ref[...] = lax.dot_general(u_ref[...], d_ref[...], (((0,), (0,)), ((), ())), preferred_element_type=f32)

    return pl.pallas_call(
        body, name="l0_gates_dw", grid=(2 * N_PARTS,), out_shape=jax.ShapeDtypeStruct((HALF, 2 * N_PARTS * HALF), f32),
        in_specs=[pl.BlockSpec((rows, HALF), lambda j: (0, j // N_PARTS)), pl.BlockSpec((rows, HALF), lambda j: (0, j))],
        out_specs=pl.BlockSpec((HALF, HALF), lambda j: (0, j)), compiler_params=_cparams("parallel"),
    )(u, dpre)


N_SCAN_CHUNKS = T_ALL // SCAN_CHUNK
SCAN_FWD = lambda t: t
SCAN_FWD_BWD = lambda t: N_SCAN_CHUNKS - 1 - t
SCAN_REV = lambda t: jnp.where(t == 0, 0, N_SCAN_CHUNKS - t)
SCAN_REV_BWD = lambda t: jnp.where(t == N_SCAN_CHUNKS - 1, 0, t + 1)
CONV_SEGMENTS = ((0, T_CTX), (T_CTX, T_LAT))
TM = 128
N_CTX_TILES = T_CTX // TM


def _local_step(x, ctx, target, mods, cmod, wts, late_weights, send_grads, start_after=()):
    sh1, sc1, g1, sh2, sc2, g2 = [[mods[l, i][None] for l in range(2)] for i in range(N_MOD)]
    ng = wts["norm_g"]
    xcat = jnp.concatenate([ctx, x], axis=0)
    poscat = jnp.concatenate([jnp.zeros((T_CTX, D), f32), _pos_embed()], axis=0)
    scp = jnp.concatenate([cmod[1][None], sc1[0]], axis=0)
    shp = jnp.concatenate([cmod[0][None], sh1[0]], axis=0)

    def blend(i, p):
        sel = jnp.where(i < N_CTX_TILES, 1.0, 0.0)
        return sel * p[0:1] + (1.0 - sel) * p[1:2]

    def f_pre0(i, xc, pos, g, scp_, shp_):
        x0 = xc + pos
        return x0, _normmod(x0, g, blend(i, scp_), blend(i, shp_))

    x0cat, h0 = _rowcall(f_pre0, "l0_prenorm", T_ALL, TM, [_rin(xcat), _rin(poscat)], [ng[0, 0][None], scp, shp],
                         [(D, f32), (D, bf16)], after=start_after)
    gr = _mm(h0, wts["rec_w_in"], "l0_in_proj")
    u, ub = _dwconv_fwd(gr, R // 256, wts["rec_conv_w"], wts["rec_conv_b"], 4, 1, CONV_SEGMENTS, 256,
                        "l0_conv", True)
    pre = _gates_fwd(ub, wts["gates"])

    def f_coeff(i, pre_, u_, ba, bx, lam):
        outs = []
        for d in range(2):
            a, b = _coeff(_gate_part(pre_, 2 * d), _gate_part(pre_, 2 * d + 1), u_,
                          ba[d:d + 1], bx[d:d + 1], lam[d:d + 1])
            outs += [a, b]
        return tuple(outs)

    a0, b0, a1, b1 = _rowcall(f_coeff, "l0_coeff", T_ALL, TM, [_rin(pre), _rin(u)],
                              [wts["rec_b_a"], wts["rec_b_x"], wts["rec_lambda"]], [(R, f32)] * 4)
    y0, yp0 = _scan_call(a0, b0, SCAN_FWD, False, "l0_scan_fwd", False)
    y1, yp1 = _scan_call(a1, b1, SCAN_REV, True, "l0_scan_rev", False)

    def f_gate(i, gp, y0_, y1_):
        return (_gelu(gp) * (y0_ + y1_),)

    (zb,) = _rowcall(f_gate, "l0_gate", T_LAT, TM,
                     [_rin(gr, R, 0, N_CTX_TILES), _rin(y0, None, 0, N_CTX_TILES), _rin(y1, None, 0, N_CTX_TILES)],
                     [], [(R, bf16)])
    out0 = _mm(zb, wts["rec_w_out"], "l0_out_proj")

    zero_d = jnp.zeros((1, D), f32)

    def mlp_params(rows):
        rows = rows + [zero_d] * (N_MLP_PARAMS - len(rows))
        return jnp.concatenate([jnp.broadcast_to(r, (8, D)) for r in rows], axis=0)

    par0 = mlp_params([g1[0], zero_d, ng[0, 1][None], sc2[0], sh2[0], g2[0], ng[1, 0][None], sc1[1], sh1[1]])
    wts = dict(wts, **late_weights("mlp", out0))
    x1, h1, r0, mo0, x2, h2 = _mlp_forward(x0cat, T_CTX // MLP_TM, out0, par0, wts["mlp_w_in"], wts["mlp_w_out"], 0,
                                           "l0_mlp")

    wts = dict(wts, **late_weights("conf", x2))
    pw = _mm(h2, wts["conf_w_pw1"], "l1_pw1")

    def f_glu(i, pa, pb, b1):
        return ((pa + b1[:, :D]) * _sigmoid(pb + b1[:, D:]),)

    (zg,) = _rowcall(f_glu, "l1_glu", T_LAT, TM, [_rin(pw, D, 0), _rin(pw, D, 1)], [wts["conf_b_pw1"]], [(D, f32)])
    (zc,) = _dwconv_fwd(zg, 0, wts["conf_conv_w"], wts["conf_conv_b"], 31, 15, ((0, T_LAT),), 128, "l1_conv", False)

    def ln_silu(z, lg, lb):
        mu = jnp.mean(z, axis=-1, keepdims=True)
        zc_ = z - mu
        var = jnp.mean(zc_ * zc_, axis=-1, keepdims=True)
        yv = zc_ * lax.rsqrt(var + EPS) * lg + lb
        return yv * _sigmoid(yv)

    def f_lnsilu(i, z, lg, lb):
        return (ln_silu(z, lg, lb),)

    (sb,) = _rowcall(f_lnsilu, "l1_ln_silu", T_LAT, TM, [_rin(zc)], [wts["conf_ln_g"], wts["conf_ln_b"]], [(D, bf16)])
    out1 = _mm(sb, wts["conf_w_pw2"], "l1_pw2")
    par1 = mlp_params([g1[1], wts["conf_b_pw2"], ng[1, 1][None], sc2[1], sh2[1], g2[1]])
    x3, h3, r1, mo1, x4, _ = _mlp_forward(x2, 0, out1, par1, wts["mlp_w_in"], wts["mlp_w_out"], 1, "l1_mlp")

    def loss_fn(x4_, fg, tgt):
        err = _rms(x4_, fg) - tgt
        per_row = jnp.mean(err * err, axis=-1, keepdims=True)
        return 0.5 * jnp.sum(per_row, axis=0, keepdims=True)

    def f_head(i, x4_, tgt, fg):
        loss, vjp = jax.vjp(lambda a, e: loss_fn(a, e, tgt), x4_, fg)
        dx, dfg = vjp(jnp.ones((1, 1), f32))
        return dx, jnp.broadcast_to(loss, (1, 128)), dfg

    dx4, loss_acc, dfinal_g = _rowcall(f_head, "head", T_LAT, TM, [_rin(x4), _rin(target)], [wts["final_g"]],
                                       [(D, f32)], [(1, 128), (1, D)])

    grads = {"final_g": dfinal_g}

    def normmod_bwd(xin, dh, dx_skip, g, sc, sh, tag, after):
        def fb(i, x_, dh_, dxs, g_, sc_, sh_):
            _, vjp = jax.vjp(_normmod, x_, g_, sc_, sh_)
            dx, dg, dsc, dsh = vjp(dh_)
            return dx + dxs, dg, dsc, dsh

        return _rowcall(fb, tag + "_normmod_bwd", T_LAT, TM, [_rin(xin), _rin(dh), _rin(dx_skip)], [g, sc, sh],
                        [(D, f32)], [(1, D)] * 3, after=after)

    dx3, dout1, dmo1, dhid1, acc1 = _mlp_backward(dx4, x3, r1, mo1, out1, par1, wts["mlp_w_in"], wts["mlp_w_out"], 1,
                                                  "l1_mlp_bwd")
    mlp_dw = _mlp_weight_grads(h3, dhid1, r1, dmo1, 1, None, "l1")
    dg1_1, db_pw2, dng11, dsc2_1, dsh2_1, dg2_1 = [acc1[k:k + 1] for k in range(6)]

    ds = _mm(dout1, wts["conf_w_pw2"], "l1_pw2_dx", tb=True)
    grads["conf_w_pw2"] = _mm(sb, dout1, "l1_pw2_dw", ta=True, out_dtype=bf16)
    grads["conf_b_pw2"] = db_pw2

    def f_lnsilu_bwd(i, z, ds_, lg, lb):
        _, vjp = jax.vjp(ln_silu, z, lg, lb)
        return vjp(ds_)

    dzc, dln_g, dln_b = _rowcall(f_lnsilu_bwd, "l1_ln_silu_bwd", T_LAT, TM, [_rin(zc), _rin(ds)],
                                 [wts["conf_ln_g"], wts["conf_ln_b"]], [(D, f32)], [(1, D)] * 2)
    grads["conf_ln_g"], grads["conf_ln_b"] = dln_g, dln_b
    dzg, dconv_w, dconv_b = _dwconv_bwd([dzc], zg, 0, wts["conf_conv_w"], 31, 15, ((0, T_LAT),), 128,
                                        "l1_conv_bwd", f32)
    grads["conf_conv_w"], grads["conf_conv_b"] = dconv_w, dconv_b

    def f_glu_bwd(i, pa, pb, dz, b1):
        _, vjp = jax.vjp(lambda a, b, c: (a + c[:, :D]) * _sigmoid(b + c[:, D:]), pa, pb, b1)
        da, db, dc = vjp(dz)
        return jnp.concatenate([da, db], axis=1), dc

    dpw, db_pw1 = _rowcall(f_glu_bwd, "l1_glu_bwd", T_LAT, TM, [_rin(pw, D, 0), _rin(pw, D, 1), _rin(dzg)],
                           [wts["conf_b_pw1"]], [(2 * D, bf16)], [(1, 2 * D)])
    grads["conf_b_pw1"] = db_pw1
    dh2 = _mm(dpw, wts["conf_w_pw1"], "l1_pw1_dx", tb=True)
    grads["conf_w_pw1"] = _mm(h2, dpw, "l1_pw1_dw", ta=True, out_dtype=bf16)
    sent = send_grads(["conf_w_pw2", "conf_w_pw1"], grads)
    dx2, dng10, dsc1_1, dsh1_1 = normmod_bwd(x2, dh2, dx3, ng[1, 0][None], sc1[1], sh1[1], "l1a", [sent])

    dx1, dout0, dmo0, dhid0, acc0 = _mlp_backward(dx2, x1, r0, mo0, out0, par0, wts["mlp_w_in"], wts["mlp_w_out"], 0,
                                                  "l0_mlp_bwd")
    grads["mlp_w_in"], grads["mlp_w_out"] = _mlp_weight_grads(h1, dhid0, r0, dmo0, 0, mlp_dw, "l0")
    sent = send_grads(["mlp_w_in", "mlp_w_out"], grads)
    dg1_0, _, dng01, dsc2_0, dsh2_0, dg2_0 = [acc0[k:k + 1] for k in range(6)]

    dz = _mm(dout0, wts["rec_w_out"], "l0_out_proj_dx", tb=True, after=[sent])
    grads["rec_w_out"] = _mm(zb, dout0, "l0_out_proj_dw", ta=True, out_dtype=bf16)
    sent = send_grads(["rec_w_out"], grads)

    def f_gate_bwd(i, gp, y0_, y1_, dz_):
        lat = jnp.where(i < N_CTX_TILES, 0.0, 1.0)
        _, vjp = jax.vjp(lambda a, b: _gelu(a) * b, gp, y0_ + y1_)
        dgp, dy = vjp(dz_)
        return dgp * lat, dy * lat

    dgp, dy = _rowcall(f_gate_bwd, "l0_gate_bwd", T_ALL, TM,
                       [_rin(gr, R, 0), _rin(y0), _rin(y1), _rin(dz, None, 0, -N_CTX_TILES)], [],
                       [(R, bf16), (R, f32)], after=[sent])
    (dh_f,) = _scan_call(a0, dy, SCAN_FWD_BWD, True, "l0_scan_fwd_bwd", True)
    (dh_r,) = _scan_call(a1, dy, SCAN_REV_BWD, False, "l0_scan_rev_bwd", True)

    def f_coeff_bwd(i, pre_, u_, dhf, dhr, ypf, ypr, ba, bx, lam):
        dpre, dba, dbx, dlam = [], [], [], []
        du = jnp.zeros_like(u_)
        for d, (dh_, yp_) in enumerate(((dhf, ypf), (dhr, ypr))):
            dpa, dpx, du_d, dba_d, dbx_d, dlam_d = _coeff_bwd(
                _gate_part(pre_, 2 * d), _gate_part(pre_, 2 * d + 1), u_, ba[d:d + 1], bx[d:d + 1], lam[d:d + 1],
                dh_ * yp_, dh_)
            dpre += [dpa, dpx]
            du = du + du_d
            dba.append(dba_d)
            dbx.append(dbx_d)
            dlam.append(dlam_d)
        return _gate_unpart(dpre), du, _rows2(*dba), _rows2(*dbx), _rows2(*dlam)

    dpre, du_direct, db_a, db_x, dlam = _rowcall(
        f_coeff_bwd, "l0_coeff_bwd", T_ALL, 64,
        [_rin(pre), _rin(u), _rin(dh_f), _rin(dh_r), _rin(yp0), _rin(yp1)],
        [wts["rec_b_a"], wts["rec_b_x"], wts["rec_lambda"]], [(4 * R, bf16), (R, f32)], [(2, R)] * 3)
    grads["rec_b_a"], grads["rec_b_x"], grads["rec_lambda"] = db_a, db_x, dlam
    du_gates = _gates_dx(dpre, wts["gates"])
    grads["gates"] = _gates_dw(ub, dpre)
    drec, dconv4_w, dconv4_b = _dwconv_bwd([du_direct, du_gates], gr, R // 256, wts["rec_conv_w"], 4, 1,
                                           CONV_SEGMENTS, 256, "l0_conv_bwd", bf16)
    grads["rec_conv_w"], grads["rec_conv_b"] = dconv4_w, dconv4_b
    dgr = jnp.concatenate([dgp, drec], axis=1)
    sent = send_grads(["replicated"], grads)
    grads["rec_w_in"] = _mm(h0, dgr, "l0_in_proj_dw", ta=True, out_dtype=bf16, after=[sent])
    dh0 = _mm(dgr, wts["rec_w_in"], "l0_in_proj_dx", tb=True, after=[send_grads(["rec_w_in"], grads)])

    def f_pre0_bwd(i, x0, dh_, dxs, g, scp_, shp_):
        lat = jnp.where(i < N_CTX_TILES, 0.0, 1.0)
        _, vjp = jax.vjp(lambda a, b, c, e: _normmod(a, b, blend(i, c), blend(i, e)), x0, g, scp_, shp_)
        dx, dg, dscp, dshp = vjp(dh_)
        return dx + lat * dxs, dg, dscp, dshp

    dx0cat, dng00, dscp, dshp = _rowcall(
        f_pre0_bwd, "l0_prenorm_bwd", T_ALL, TM, [_rin(x0cat), _rin(dh0), _rin(dx1, None, 0, -N_CTX_TILES)],
        [ng[0, 0][None], scp, shp], [(D, f32)], [(1, D), (2, D), (2, D)])

    grads["norm_g"] = jnp.stack([jnp.concatenate([dng00, dng01], 0), jnp.concatenate([dng10, dng11], 0)])
    dmods = jnp.stack([
        jnp.concatenate([dshp[1:2], dscp[1:2], dg1_0, dsh2_0, dsc2_0, dg2_0], axis=0),
        jnp.concatenate([dsh1_1, dsc1_1, dg1_1, dsh2_1, dsc2_1, dg2_1], axis=0)])
    dcmod = jnp.concatenate([dshp[0:1], dscp[0:1]], axis=0)
    return loss_acc[0, 0], dx0cat[T_CTX:], dmods, dcmod, grads


def _unshard_cols(g):
    g = jnp.moveaxis(g, 0, -2)
    return g.reshape(g.shape[:-2] + (g.shape[-2] * g.shape[-1],))


def _shard_cols(w):
    w = w.reshape(w.shape[:-1] + (N_DEV, w.shape[-1] // N_DEV))
    return jnp.moveaxis(w, -2, 0)


def _shard_rows(w):
    return w.reshape((N_DEV, w.shape[0] // N_DEV) + w.shape[1:])


SMALL_PACK_ROWS = 64


def kernel(x, c, ctx, c_ctx, w_ada, b_ada, norm_g, rec_w_in, rec_conv_w, rec_conv_b, rec_lambda, rec_w_a, rec_b_a, rec_w_x, rec_b_x, rec_w_out, conf_w_pw1, conf_b_pw1, conf_conv_w, conf_conv_b, conf_ln_g, conf_ln_b, conf_w_pw2, conf_b_pw2, mlp_w_in, mlp_w_out, final_g, loss_target, m_c_ctx, m_w_ada, m_b_ada, m_norm_g, m_rec_w_in, m_rec_conv_w, m_rec_conv_b, m_rec_lambda, m_rec_w_a, m_rec_b_a, m_rec_w_x, m_rec_b_x, m_rec_w_out, m_conf_w_pw1, m_conf_b_pw1, m_conf_conv_w, m_conf_conv_b, m_conf_ln_g, m_conf_ln_b, m_conf_w_pw2, m_conf_b_pw2, m_mlp_w_in, m_mlp_w_out, m_final_g, v_c_ctx, v_w_ada, v_b_ada, v_norm_g, v_rec_w_in, v_rec_conv_w, v_rec_conv_b, v_rec_lambda, v_rec_w_a, v_rec_b_a, v_rec_w_x, v_rec_b_x, v_rec_w_out, v_conf_w_pw1, v_conf_b_pw1, v_conf_conv_w, v_conf_conv_b, v_conf_ln_g, v_conf_ln_b, v_conf_w_pw2, v_conf_b_pw2, v_mlp_w_in, v_mlp_w_out, v_final_g):
    me = 4 * lax.axis_index("x") + 2 * lax.axis_index("y") + lax.axis_index("c")
    weights = dict(c_ctx=c_ctx, w_ada=w_ada, b_ada=b_ada, norm_g=norm_g, rec_w_in=rec_w_in, rec_conv_w=rec_conv_w,
                   rec_conv_b=rec_conv_b, rec_lambda=rec_lambda, rec_w_a=rec_w_a, rec_b_a=rec_b_a, rec_w_x=rec_w_x,
                   rec_b_x=rec_b_x, rec_w_out=rec_w_out, conf_w_pw1=conf_w_pw1, conf_b_pw1=conf_b_pw1,
                   conf_conv_w=conf_conv_w, conf_conv_b=conf_conv_b, conf_ln_g=conf_ln_g, conf_ln_b=conf_ln_b,
                   conf_w_pw2=conf_w_pw2, conf_b_pw2=conf_b_pw2, mlp_w_in=mlp_w_in, mlp_w_out=mlp_w_out, final_g=final_g)
    m_in = dict(c_ctx=m_c_ctx, w_ada=m_w_ada, b_ada=m_b_ada, norm_g=m_norm_g, rec_w_in=m_rec_w_in, rec_conv_w=m_rec_conv_w,
                rec_conv_b=m_rec_conv_b, rec_lambda=m_rec_lambda, rec_w_a=m_rec_w_a, rec_b_a=m_rec_b_a, rec_w_x=m_rec_w_x,
                rec_b_x=m_rec_b_x, rec_w_out=m_rec_w_out, conf_w_pw1=m_conf_w_pw1, conf_b_pw1=m_conf_b_pw1,
                conf_conv_w=m_conf_conv_w, conf_conv_b=m_conf_conv_b, conf_ln_g=m_conf_ln_g, conf_ln_b=m_conf_ln_b,
                conf_w_pw2=m_conf_w_pw2, conf_b_pw2=m_conf_b_pw2, mlp_w_in=m_mlp_w_in, mlp_w_out=m_mlp_w_out,
                final_g=m_final_g)
    v_in = dict(c_ctx=v_c_ctx, w_ada=v_w_ada, b_ada=v_b_ada, norm_g=v_norm_g, rec_w_in=v_rec_w_in, rec_conv_w=v_rec_conv_w,
                rec_conv_b=v_rec_conv_b, rec_lambda=v_rec_lambda, rec_w_a=v_rec_w_a, rec_b_a=v_rec_b_a, rec_w_x=v_rec_w_x,
                rec_b_x=v_rec_b_x, rec_w_out=v_rec_w_out, conf_w_pw1=v_conf_w_pw1, conf_b_pw1=v_conf_b_pw1,
                conf_conv_w=v_conf_conv_w, conf_conv_b=v_conf_conv_b, conf_ln_g=v_conf_ln_g, conf_ln_b=v_conf_ln_b,
                conf_w_pw2=v_conf_w_pw2, conf_b_pw2=v_conf_b_pw2, mlp_w_in=v_mlp_w_in, mlp_w_out=v_mlp_w_out,
                final_g=v_final_g)
    names = list(weights)

    small_items = [c, norm_g, rec_conv_w, rec_lambda, conf_b_pw1, conf_conv_w, conf_conv_b, conf_ln_g, conf_ln_b,
                   conf_b_pw2]
    flat = jnp.concatenate([a.reshape(-1) for a in small_items])
    flat = jnp.pad(flat, (0, SMALL_PACK_ROWS * 128 - flat.shape[0])).reshape(SMALL_PACK_ROWS, 128)
    (small_all,) = _all_gather([flat], "gather_small")

    small_all = small_all.reshape(N_DEV, -1)
    off = 0
    small = []
    for a in small_items:
        small.append(small_all[:, off:off + a.size].reshape((N_DEV,) + a.shape))
        off += a.size
    c_all, ng_all, rcw_all, lam_all, bpw1_all, ccw_all, ccb_all, lng_all, lnb_all, bpw2_all = small
    wts = {
        "norm_g": _unshard_cols(ng_all),
        "rec_conv_w": _unshard_cols(rcw_all)[0],
        "rec_lambda": _unshard_cols(lam_all)[0],
        "conf_b_pw1": _unshard_cols(bpw1_all),
        "conf_conv_w": _unshard_cols(ccw_all)[0],
        "conf_conv_b": _unshard_cols(ccb_all),
        "conf_ln_g": _unshard_cols(lng_all),
        "conf_ln_b": _unshard_cols(lnb_all),
        "conf_b_pw2": _unshard_cols(bpw2_all),
        "rec_conv_b": rec_conv_b,
        "rec_b_a": rec_b_a[0].reshape(2, R),
        "rec_b_x": rec_b_x[0].reshape(2, R),
        "final_g": final_g[None],
        "gates": _gate_matrix(rec_w_a[0], rec_w_x[0]),
    }

    c16 = jnp.concatenate([c_all[:, 0], jnp.broadcast_to(c_ctx[None], (8, D))], axis=0)
    b_loc = lax.dynamic_slice_in_dim(b_ada, me * ADA_SHARD, ADA_SHARD, axis=1)[:, None]
    (mods_all,) = _all_gather([_ada_forward(c16, w_ada, b_loc)], "gather_mods")
    mods_all = _unshard_cols(mods_all)
    mods = lax.dynamic_index_in_dim(mods_all, me, axis=1, keepdims=False).reshape(2, N_MOD, D)
    cmod = mods_all[0, 8, :2 * D].reshape(2, D)

    as_shard = lambda a: a.astype(bf16).reshape(-1, a.shape[-1])
    early = _all_gather_2level([as_shard(rec_w_in[0]), as_shard(rec_w_out[0])], "gather_weights_early")
    wts["rec_w_in"] = _unshard_cols(early[0])
    wts["rec_w_out"] = early[1].reshape(R, D)
    late_items = {"mlp": [mlp_w_in, mlp_w_out], "conf": [conf_w_pw1[0], conf_w_pw2[0]]}
    late_handles, order = {}, [early[0], mods]
    for group in ("mlp", "conf"):
        shards = [as_shard(a) for a in late_items[group]]
        lands = [_own_block_filled(s, me) for s in shards]
        if group == "mlp":
            late_handles[group], token = _chip_gather_start(shards, lands, "gather_mlp_start", after=order)
        else:
            late_handles[group], token = _exchange_start(shards, lands, "gather_conf_start", False, after=order)
        order = [token]

    def late_weights(group, after):
        if group == "mlp":
            forwarded = _chip_gather_forward(late_handles[group], after, "gather_mlp_forward")
            got = _chip_gather_wait(forwarded, after, "gather_mlp_wait")
        else:
            got = _exchange_wait(late_handles[group], after, "gather_conf_wait", False)
        got = [g.reshape((N_DEV,) + a.shape) for g, a in zip(got, late_items[group])]
        if group == "mlp":
            return {"mlp_w_in": got[0], "mlp_w_out": got[1]}
        return {"conf_w_pw1": _unshard_cols(got[0]), "conf_w_pw2": got[1].reshape(D, D)}

    to_blocks = {"rec_w_in": _shard_cols, "conf_w_pw1": _shard_cols, "rec_w_out": _shard_rows, "conf_w_pw2": _shard_rows,
                 "mlp_w_in": lambda g: g, "mlp_w_out": lambda g: g}
    grad_handles = []

    repl_names = ["rec_conv_b", "rec_w_a", "rec_w_x", "rec_b_a", "rec_b_x", "final_g"]

    def send_replicated(grads):
        dwg = grads["gates"]
        repl = {"rec_conv_b": grads["rec_conv_b"],
                "rec_w_a": jnp.stack([_gate_blocks(dwg, 0), _gate_blocks(dwg, 2)]),
                "rec_w_x": jnp.stack([_gate_blocks(dwg, 1), _gate_blocks(dwg, 3)]),
                "rec_b_a": grads["rec_b_a"], "rec_b_x": grads["rec_b_x"], "final_g": grads["final_g"]}
        flat = jnp.concatenate([repl[n].reshape(-1) for n in repl_names])
        rows = -(-flat.shape[0] // (16 * D)) * 16
        flat = jnp.pad(flat, (0, rows * D - flat.shape[0])).reshape(rows, D).astype(bf16)
        handle, sent = _exchange_start([flat], [_own_block_filled(flat, me)], "gather_replicated_start", False)
        grad_handles.append((["replicated"], handle))
        return sent

    def send_grads(group, grads):
        if group == ["replicated"]:
            return send_replicated(grads)
        blocks = [to_blocks[n](grads[n]) for n in group]
        blocks = [g.reshape(N_DEV, -1, g.shape[-1]) for g in blocks]
        lands = [_own_block_filled(lax.dynamic_index_in_dim(g, me, 0, keepdims=False), me) for g in blocks]
        handle, sent = _exchange_start(blocks, lands, "scatter_start_" + group[0], True)
        grad_handles.append((group, handle))
        return sent

    loss_part, grad_x, dmods, dcmod, grads = _local_step(x[0], ctx[0], loss_target[0], mods, cmod, wts, late_weights,
                                                         send_grads, start_after=order)
    loss = lax.psum(loss_part, ("x", "y", "c"))

    dm_flat = jnp.concatenate([dmods.reshape(-1), dcmod.reshape(-1)]).reshape(-1, 128)
    (dm_all,) = _all_gather([dm_flat], "gather_dmods")
    dm_all = dm_all.reshape(N_DEV, -1)
    dmods_all = dm_all[:, :2 * N_MOD * D].reshape(N_DEV, 2, N_MOD * D)
    dcmod_all = jnp.pad(dm_all[:, 2 * N_MOD * D:], ((0, 0), (0, (N_MOD - 2) * D)))
    g16_full = jnp.stack([jnp.concatenate([dmods_all[:, 0], dcmod_all], axis=0),
                          jnp.concatenate([dmods_all[:, 1], jnp.zeros_like(dcmod_all)], axis=0)])
    g16 = lax.dynamic_slice_in_dim(g16_full, me * ADA_SHARD, ADA_SHARD, axis=2)
    dw_ada, ds_part = _ada_backward(c16, g16, w_ada)
    (ds_all,) = _all_gather([ds_part[0]], "gather_dsilu")

    big_names, big_pieces, repl_all = [], [], None
    for group, handle in grad_handles:
        if group == ["replicated"]:
            repl_all = _exchange_wait(handle, grad_x, "gather_replicated_wait", False)[0].reshape(N_DEV, -1)
            continue
        for n, got in zip(group, _exchange_wait(handle, grad_x, "scatter_wait_" + group[0], True)):
            big_names.append(n)
            big_pieces.append([(got, N_DEV)])
    small_sharded = ["norm_g", "rec_conv_w", "rec_lambda", "conf_b_pw1", "conf_conv_w", "conf_conv_b", "conf_ln_g",
                     "conf_ln_b", "conf_b_pw2"]
    pack = jnp.concatenate([_shard_cols(grads[n]).reshape(N_DEV, -1) for n in small_sharded], axis=1)
    pack_len = pack.shape[1]
    pack = jnp.pad(pack, ((0, 0), (0, SMALL_PACK_ROWS * 128 - pack_len))).reshape(N_DEV, SMALL_PACK_ROWS, 128)
    (pack_recv,) = _all_to_all([pack], "scatter_small_grads")
    pack_recv = pack_recv.reshape(N_DEV, -1)


    def as2d(shape):
        rows = 1
        for s in shape[:-1]:
            rows *= s
        return (rows, shape[-1])

    def whole(arr, shape):
        arr = arr.reshape((-1,) + as2d(shape))
        return (arr, arr.shape[0])

    pieces = {}
    shard_shapes = {n: weights[n].shape for n in names}
    for n, parts in zip(big_names, big_pieces):
        pieces[n] = parts
    off = 0
    for n in small_sharded:
        size = weights[n].size
        pieces[n] = [whole(pack_recv[:, off:off + size], shard_shapes[n])]
        off += size
    off = 0
    for n in repl_names:
        size = weights[n].size
        pieces[n] = [whole(repl_all[:, off:off + size], shard_shapes[n])]
        off += size
    pieces["w_ada"] = [whole(dw_ada, shard_shapes["w_ada"])]
    db_terms = jnp.concatenate([dmods_all, jnp.stack([dcmod_all, jnp.zeros_like(dcmod_all)], axis=1)], axis=0)
    pieces["b_ada"] = [whole(db_terms, shard_shapes["b_ada"])]
    pieces["c_ctx"] = [whole(ds_all[:, 0], shard_shapes["c_ctx"])]

    g_out, d_out, m_out, v_out = {}, {}, {}, {}
    for n in names:
        shape = shard_shapes[n]
        r2, c2 = as2d(shape)
        p = pieces[n]
        g, dl, nm, nv = _adamw(p, weights[n].reshape(r2, c2), m_in[n].reshape(r2, c2), v_in[n].reshape(r2, c2),
                               "adamw_" + n)
        g_out[n], d_out[n], m_out[n], v_out[n] = (t.reshape(shape) for t in (g, dl, nm, nv))

    return (loss, grad_x[None], *[g_out[n] for n in names], *[d_out[n] for n in names],
            *[m_out[n] for n in names], *[v_out[n] for n in names])
```

```python
import functools

import jax
import jax.numpy as jnp
from jax import lax
from jax.experimental import pallas as pl
from jax.experimental.pallas import tpu as pltpu

f32 = jnp.float32
bf16 = jnp.bfloat16

N_DEV = 8
D = 1024
T_LAT = 2048
T_CTX = 256
T_ALL = T_CTX + T_LAT
R = 1280
N_BLK = 16
BLK = R // N_BLK
F = 4096
GRID_W = 64
RG_C = 8.0
EPS = 1e-6
POS_BASE = 10000.0
N_MOD = 6
ADA_SHARD = N_MOD * D // N_DEV

ADAM_LR = 0.001
ADAM_B1 = 0.9
ADAM_B2 = 0.999
ADAM_EPS = 1e-08
ADAM_WD = 0.01
ADAM_STEP = 10

VMEM_LIMIT_V7X = 56 * 1024 * 1024
HALO = 16
MESH = pl.DeviceIdType.MESH


def _cparams(*sem):
    return pltpu.CompilerParams(dimension_semantics=sem, vmem_limit_bytes=VMEM_LIMIT_V7X)


def _pick(n, cands):
    for c in cands:
        if n % c == 0:
            return c
    raise ValueError(f"no block size for {n}")


def _position():
    x, y, c = lax.axis_index("x"), lax.axis_index("y"), lax.axis_index("c")
    return x, y, c, 4 * x + 2 * y + c


def _peer(x, y, c, k):
    px = (1 - x) if (k >> 2) & 1 else x
    py = (1 - y) if (k >> 1) & 1 else y
    pc = (1 - c) if k & 1 else c
    return (px, py, pc), 4 * px + 2 * py + pc


def _exchange(arrs, name, scatter):
    n = len(arrs)

    def body(*refs):
        ins, outs = refs[:n], refs[n:2 * n]
        send_sems, recv_sems, local_sems = refs[2 * n:]
        x, y, c, me = _position()
        local = []
        for a in range(n):
            src = ins[a].at[me] if scatter else ins[a]
            cp = pltpu.make_async_copy(src, outs[a].at[me], local_sems.at[a])
            cp.start()
            local.append(cp)
        sends, recvs = [], []
        for a in range(n):
            for k in range(1, N_DEV):
                peer, peer_lin = _peer(x, y, c, k)
                src = ins[a].at[peer_lin] if scatter else ins[a]
                cp = pltpu.make_async_remote_copy(
                    src_ref=src, dst_ref=outs[a].at[me], send_sem=send_sems.at[a, k - 1],
                    recv_sem=recv_sems.at[a, k - 1], device_id=peer, device_id_type=MESH)
                cp.start()
                sends.append(cp)
                recvs.append(pltpu.make_async_remote_copy(
                    src_ref=src, dst_ref=outs[a].at[peer_lin], send_sem=send_sems.at[a, k - 1],
                    recv_sem=recv_sems.at[a, k - 1], device_id=peer, device_id_type=MESH))
        for cp in recvs:
            cp.wait_recv()
        for cp in sends:
            cp.wait_send()
        for cp in local:
            cp.wait()

    if scatter:
        out_shape = [jax.ShapeDtypeStruct(a.shape, a.dtype) for a in arrs]
    else:
        out_shape = [jax.ShapeDtypeStruct((N_DEV,) + a.shape, a.dtype) for a in arrs]
    any_spec = pl.BlockSpec(memory_space=pl.ANY)
    return pl.pallas_call(
        body, name=name, out_shape=out_shape,
        in_specs=[any_spec] * n, out_specs=[any_spec] * n,
        scratch_shapes=[pltpu.SemaphoreType.DMA((n, N_DEV - 1)), pltpu.SemaphoreType.DMA((n, N_DEV - 1)),
                        pltpu.SemaphoreType.DMA((n,))],
    )(*arrs)


def _all_gather(arrs, name):
    return _exchange(arrs, name, scatter=False)


def _all_to_all(arrs, name):
    return _exchange(arrs, name, scatter=True)


def _lin(p):
    return 4 * p[0] + 2 * p[1] + p[2]


HBM_SPEC = pl.BlockSpec(memory_space=pltpu.HBM)
SEM_SPEC = pl.BlockSpec(memory_space=pltpu.SEMAPHORE)
DATAFLOW_EFFECT = pltpu.SideEffectType.DATAFLOW_SIDE_EFFECTING


def _split_copies(srcs, lands, send_sems, recv_sems, scatter):
    x, y, c, me = _position()
    out = []
    for a in range(len(srcs)):
        for k in range(1, N_DEV):
            peer, peer_lin = _peer(x, y, c, k)
            src = srcs[a].at[peer_lin] if scatter else srcs[a]
            mk = lambda slot: pltpu.make_async_remote_copy(
                src_ref=src, dst_ref=lands[a].at[slot], send_sem=send_sems.at[a * (N_DEV - 1) + k - 1],
                recv_sem=recv_sems.at[a * (N_DEV - 1) + k - 1], device_id=peer, device_id_type=MESH)
            out.append((mk(me), mk(peer_lin)))
    return out


def _exchange_start(srcs, lands, name, scatter, after=()):
    n = len(srcs)
    n_after = len(after)

    def body(*refs):
        srcs_r, lands_r = refs[:n], refs[n:2 * n]
        send_sems, recv_sems = refs[2 * n + n_after], refs[2 * n + n_after + 1]
        token = refs[-1]
        for outgoing, _ in _split_copies(srcs_r, lands_r, send_sems, recv_sems, scatter):
            outgoing.start()
        token[...] = jnp.zeros_like(token)

    hbm = lambda a: pltpu.HBM(a.shape, a.dtype)
    res = pl.pallas_call(
        body, name=name,
        out_shape=(pltpu.SemaphoreType.DMA((n * (N_DEV - 1),)), pltpu.SemaphoreType.DMA((n * (N_DEV - 1),)),
                   *[hbm(a) for a in srcs], *[hbm(a) for a in lands], jax.ShapeDtypeStruct((8, 128), f32)),
        in_specs=[HBM_SPEC] * (2 * n) + [pl.BlockSpec(memory_space=pl.ANY)] * n_after,
        out_specs=(SEM_SPEC, SEM_SPEC, *[HBM_SPEC] * (2 * n), pl.BlockSpec(memory_space=pltpu.VMEM)),
        input_output_aliases={i: 2 + i for i in range(2 * n)},
        compiler_params=pltpu.CompilerParams(has_side_effects=DATAFLOW_EFFECT),
    )(*[pltpu.with_memory_space_constraint(a, pltpu.HBM) for a in list(srcs) + list(lands)], *after)
    return (res[0], res[1], list(res[2:2 + n]), list(res[2 + n:2 + 2 * n])), res[-1]


def _exchange_wait(handle, after, name, scatter):
    send_sems, recv_sems, srcs, lands = handle
    n = len(srcs)

    def body(*refs):
        srcs_r, lands_r = refs[:n], refs[n:2 * n]
        send_s, recv_s = refs[2 * n], refs[2 * n + 1]
        for outgoing, incoming in _split_copies(srcs_r, lands_r, send_s, recv_s, scatter):
            outgoing.wait_send()
            incoming.wait_recv()

    hbm = lambda a: pltpu.HBM(a.shape, a.dtype)
    res = pl.pallas_call(
        body, name=name, out_shape=tuple(hbm(a) for a in list(srcs) + list(lands)),
        in_specs=[HBM_SPEC] * (2 * n) + [SEM_SPEC, SEM_SPEC, pl.BlockSpec(memory_space=pl.ANY)],
        out_specs=tuple([HBM_SPEC] * (2 * n)),
        input_output_aliases={i: i for i in range(2 * n)},
        compiler_params=pltpu.CompilerParams(has_side_effects=DATAFLOW_EFFECT),
    )(*srcs, *lands, send_sems, recv_sems, after)
    return list(res[n:])


def _chip_peers(x, y, c):
    return [(x, y, 1 - c)] + [_plane_pos(x, y, q) + (c,) for q in (2, 1, 3)]


def _chip_gather_start(shards, lands, name, after=()):
    n, n_after = len(shards), len(after)

    def body(*refs):
        srcs_r, lands_r = refs[:n], refs[n:2 * n]
        send_sems, recv_sems = refs[2 * n + n_after], refs[2 * n + n_after + 1]
        x, y, c, me = _position()
        for a in range(n):
            for k, peer in enumerate(_chip_peers(x, y, c)):
                pltpu.make_async_remote_copy(
                    src_ref=srcs_r[a], dst_ref=lands_r[a].at[me], send_sem=send_sems.at[4 * a + k],
                    recv_sem=recv_sems.at[4 * a + k], device_id=peer, device_id_type=MESH).start()
        refs[-1][...] = jnp.zeros_like(refs[-1])

    hbm = lambda a: pltpu.HBM(a.shape, a.dtype)
    res = pl.pallas_call(
        body, name=name,
        out_shape=(pltpu.SemaphoreType.DMA((4 * n,)), pltpu.SemaphoreType.DMA((4 * n,)),
                   *[hbm(a) for a in shards], *[hbm(a) for a in lands], jax.ShapeDtypeStruct((8, 128), f32)),
        in_specs=[HBM_SPEC] * (2 * n) + [ANY_SPEC] * n_after,
        out_specs=(SEM_SPEC, SEM_SPEC, *[HBM_SPEC] * (2 * n), pl.BlockSpec(memory_space=pltpu.VMEM)),
        input_output_aliases={i: 2 + i for i in range(2 * n)},
        compiler_params=pltpu.CompilerParams(has_side_effects=DATAFLOW_EFFECT),
    )(*[pltpu.with_memory_space_constraint(a, pltpu.HBM) for a in list(shards) + list(lands)], *after)
    return (res[0], res[1], list(res[2:2 + n]), list(res[2 + n:2 + 2 * n])), res[-1]


def _chip_gather_forward(handle, after, name):
    send_sems, recv_sems, srcs, lands = handle
    n = len(srcs)

    def body(*refs):
        srcs_r, lands_r = refs[:n], refs[n:2 * n]
        send1, recv1 = refs[2 * n], refs[2 * n + 1]
        send2, recv2 = refs[2 * n + 3], refs[2 * n + 4]
        x, y, c, me = _position()
        peers = _chip_peers(x, y, c)
        for a in range(n):
            for k, peer in enumerate(peers):
                mk = lambda slot: pltpu.make_async_remote_copy(
                    src_ref=srcs_r[a], dst_ref=lands_r[a].at[slot], send_sem=send1.at[4 * a + k],
                    recv_sem=recv1.at[4 * a + k], device_id=peer, device_id_type=MESH)
                mk(me).wait_send()
                mk(_lin(peer)).wait_recv()
        for a in range(n):
            for k, peer in enumerate(peers[1:]):
                slot = _lin(peer)
                pltpu.make_async_remote_copy(
                    src_ref=lands_r[a].at[slot], dst_ref=lands_r[a].at[slot], send_sem=send2.at[3 * a + k],
                    recv_sem=recv2.at[3 * a + k], device_id=peers[0], device_id_type=MESH).start()

    hbm = lambda a: pltpu.HBM(a.shape, a.dtype)
    res = pl.pallas_call(
        body, name=name,
        out_shape=(pltpu.SemaphoreType.DMA((3 * n,)), pltpu.SemaphoreType.DMA((3 * n,)), *[hbm(a) for a in lands]),
        in_specs=[HBM_SPEC] * (2 * n) + [SEM_SPEC, SEM_SPEC, ANY_SPEC],
        out_specs=(SEM_SPEC, SEM_SPEC, *[HBM_SPEC] * n),
        input_output_aliases={n + i: 2 + i for i in range(n)},
        compiler_params=pltpu.CompilerParams(has_side_effects=DATAFLOW_EFFECT),
    )(*srcs, *lands, send_sems, recv_sems, after)
    return (res[0], res[1], list(res[2:]))


def _chip_gather_wait(handle, after, name):
    send_sems, recv_sems, lands = handle
    n = len(lands)

    def body(*refs):
        lands_r, send2, recv2 = refs[:n], refs[n], refs[n + 1]
        x, y, c, me = _position()
        peers = _chip_peers(x, y, c)
        for a in range(n):
            for k, (px, py, pc) in enumerate(peers[1:]):
                mk = lambda slot: pltpu.make_async_remote_copy(
                    src_ref=lands_r[a].at[slot], dst_ref=lands_r[a].at[slot], send_sem=send2.at[3 * a + k],
                    recv_sem=recv2.at[3 * a + k], device_id=peers[0], device_id_type=MESH)
                mk(_lin((px, py, pc))).wait_send()
                mk(_lin((px, py, 1 - pc))).wait_recv()

    hbm = lambda a: pltpu.HBM(a.shape, a.dtype)
    res = pl.pallas_call(
        body, name=name, out_shape=tuple(hbm(a) for a in lands),
        in_specs=[HBM_SPEC] * n + [SEM_SPEC, SEM_SPEC, ANY_SPEC], out_specs=tuple([HBM_SPEC] * n),
        input_output_aliases={i: i for i in range(n)},
        compiler_params=pltpu.CompilerParams(has_side_effects=DATAFLOW_EFFECT),
    )(*lands, send_sems, recv_sems, after)
    return list(res)


def _own_block_filled(block, me):
    land = lax.empty((N_DEV,) + block.shape, block.dtype)
    return lax.dynamic_update_index_in_dim(land, block, me, 0)


def _staged_copy(src, dst, buf, in_sems, out_sems, rows, chunk):
    n = rows // chunk

    def rd(i):
        return pltpu.make_async_copy(src.at[pl.ds(i * chunk, chunk)], buf.at[i % 2], in_sems.at[i % 2])

    def wr(i):
        return pltpu.make_async_copy(buf.at[i % 2], dst.at[pl.ds(i * chunk, chunk)], out_sems.at[i % 2])

    rd(0).start()
    for i in range(n):
        if i + 1 < n:
            if i >= 1:
                wr(i - 1).wait()
            rd(i + 1).start()
        rd(i).wait()
        wr(i).start()
    for i in range(max(n - 2, 0), n):
        wr(i).wait()


def _all_gather_2level(shards, name):
    n = len(shards)
    chunks = [_pick(s.shape[0], (416, 512, 256, 160, 128, 64, 16)) for s in shards]

    def body(*refs):
        ins, outs = refs[:n], refs[n:2 * n]
        send_sems, recv_sems, in_sems, out_sems = refs[2 * n:2 * n + 4]
        bufs = refs[2 * n + 4:]
        x, y, c, me = _position()
        sib, xn, yn, dg = (x, y, 1 - c), (1 - x, y, c), (x, 1 - y, c), (1 - x, 1 - y, c)

        def cp(a, k, src, slot, to):
            return pltpu.make_async_remote_copy(src_ref=src, dst_ref=outs[a].at[slot], send_sem=send_sems.at[a, k],
                                                recv_sem=recv_sems.at[a, k], device_id=to, device_id_type=MESH)

        for a in range(n):
            for k, to in ((0, sib), (1, xn), (2, yn)):
                cp(a, k, ins[a], me, to).start()
        for a in range(n):
            cp(a, 1, ins[a], _lin(xn), xn).wait_recv()
            cp(a, 3, outs[a].at[_lin(xn)], _lin(xn), sib).start()

            @pl.when(c == 0)
            def _():
                cp(a, 5, outs[a].at[_lin(xn)], _lin(xn), yn).start()

            cp(a, 2, ins[a], _lin(yn), yn).wait_recv()
            cp(a, 4, outs[a].at[_lin(yn)], _lin(yn), sib).start()

            @pl.when(c == 1)
            def _():
                cp(a, 5, outs[a].at[_lin(yn)], _lin(yn), xn).start()

        for a in range(n):
            cp(a, 5, ins[a], _lin(dg), xn).wait_recv()
            cp(a, 6, outs[a].at[_lin(dg)], _lin(dg), sib).start()
        for a in range(n):
            _staged_copy(ins[a], outs[a].at[me], bufs[a], in_sems.at[a], out_sems.at[a], shards[a].shape[0], chunks[a])
        for a in range(n):
            for k, origin in ((0, sib), (3, (1 - x, y, 1 - c)), (4, (x, 1 - y, 1 - c)), (6, (1 - x, 1 - y, 1 - c))):
                cp(a, k, ins[a], _lin(origin), sib).wait_recv()
            for k in range(7):
                cp(a, k, ins[a], me, sib).wait_send()

    any_spec = pl.BlockSpec(memory_space=pl.ANY)
    return pl.pallas_call(
        body, name=name, out_shape=[jax.ShapeDtypeStruct((N_DEV,) + s.shape, s.dtype) for s in shards],
        in_specs=[any_spec] * n, out_specs=[any_spec] * n,
        scratch_shapes=[pltpu.SemaphoreType.DMA((n, 7)), pltpu.SemaphoreType.DMA((n, 7)),
                        pltpu.SemaphoreType.DMA((n, 2)), pltpu.SemaphoreType.DMA((n, 2))]
        + [pltpu.VMEM((2, ch, s.shape[1]), s.dtype) for ch, s in zip(chunks, shards)],
    )(*shards)


def _plane_pos(x, y, q):
    return ((1 - x) if q & 2 else x, (1 - y) if q & 1 else y)


def _scatter_call(body, name, ins, out_shape, sems_per_array):
    n = len(ins)
    any_spec = pl.BlockSpec(memory_space=pl.ANY)
    return pl.pallas_call(
        body, name=name, out_shape=out_shape, in_specs=[any_spec] * n, out_specs=[any_spec] * n,
        scratch_shapes=[pltpu.SemaphoreType.DMA((n, sems_per_array)), pltpu.SemaphoreType.DMA((n, sems_per_array))],
    )(*ins)


def _scatter_d2d(gs, name):
    n = len(gs)

    def body(*refs):
        g_refs, recv_refs, send_sems, recv_sems = refs[:n], refs[n:2 * n], refs[2 * n], refs[2 * n + 1]
        x, y, c, me = _position()
        sib = (x, y, 1 - c)
        sends = []
        for a in range(n):
            for q in range(4):
                px, py = _plane_pos(x, y, q)
                cp = pltpu.make_async_remote_copy(
                    src_ref=g_refs[a].at[_lin((px, py, 1 - c))], dst_ref=recv_refs[a].at[q],
                    send_sem=send_sems.at[a, q], recv_sem=recv_sems.at[a, q], device_id=sib, device_id_type=MESH)
                cp.start()
                sends.append(cp)
        for cp in sends:
            cp.wait_recv()
        for cp in sends:
            cp.wait_send()

    return _scatter_call(body, name, gs, [jax.ShapeDtypeStruct((4,) + g.shape[1:], g.dtype) for g in gs], 4)


def _scatter_ici_first(hs, name):
    n = len(hs)

    def body(*refs):
        h_refs, recv_refs, send_sems, recv_sems = refs[:n], refs[n:2 * n], refs[2 * n], refs[2 * n + 1]
        x, y, c, me = _position()
        xn, yn = (1 - x, y, c), (x, 1 - y, c)

        def cp(a, k, q, to):
            return pltpu.make_async_remote_copy(
                src_ref=h_refs[a].at[q], dst_ref=recv_refs[a].at[k], send_sem=send_sems.at[a, k],
                recv_sem=recv_sems.at[a, k], device_id=to, device_id_type=MESH)

        @pl.when(c == 0)
        def _():
            for a in range(n):
                cp(a, 0, 2, xn).start()
                cp(a, 1, 3, xn).start()

        @pl.when(c == 1)
        def _():
            for a in range(n):
                cp(a, 0, 1, yn).start()
                cp(a, 1, 3, yn).start()

        for a in range(n):
            for k in range(2):
                cp(a, k, 0, xn).wait_recv()
        for a in range(n):
            for k in range(2):
                cp(a, k, 0, xn).wait_send()

    return _scatter_call(body, name, hs, [jax.ShapeDtypeStruct((2,) + h.shape[1:], h.dtype) for h in hs], 2)


def _scatter_ici_second(k1s, name):
    n = len(k1s)

    def body(*refs):
        k_refs, recv_refs, send_sems, recv_sems = refs[:n], refs[n:2 * n], refs[2 * n], refs[2 * n + 1]
        x, y, c, me = _position()
        xn, yn = (1 - x, y, c), (x, 1 - y, c)

        def cp(a, to):
            return pltpu.make_async_remote_copy(src_ref=k_refs[a], dst_ref=recv_refs[a], send_sem=send_sems.at[a, 0],
                                                recv_sem=recv_sems.at[a, 0], device_id=to, device_id_type=MESH)

        @pl.when(c == 0)
        def _():
            for a in range(n):
                cp(a, yn).start()

        @pl.when(c == 1)
        def _():
            for a in range(n):
                cp(a, xn).start()

        for a in range(n):
            cp(a, xn).wait_recv()
        for a in range(n):
            cp(a, xn).wait_send()

    return _scatter_call(body, name, k1s, [jax.ShapeDtypeStruct(k.shape, k.dtype) for k in k1s], 1)


def _add_blocks(a, a_idx, b, b_idx, out_dtype, name):
    rows, cols = a.shape[1:]
    n = a_idx.shape[0]
    tm = _pick(rows, (512, 256, 160, 128, 32, 16))

    def body(ia_ref, ib_ref, a_ref, b_ref, o_ref):
        o_ref[...] = (a_ref[...].astype(f32) + b_ref[...].astype(f32)).astype(o_ref.dtype)

    grid_spec = pltpu.PrefetchScalarGridSpec(
        num_scalar_prefetch=2, grid=(n, rows // tm),
        in_specs=[pl.BlockSpec((None, tm, cols), lambda j, i, ia, ib: (ia[j], i, 0)),
                  pl.BlockSpec((None, tm, cols), lambda j, i, ia, ib: (ib[j], i, 0))],
        out_specs=pl.BlockSpec((None, tm, cols), lambda j, i, ia, ib: (j, i, 0)))
    return pl.pallas_call(body, name=name, out_shape=jax.ShapeDtypeStruct((n, rows, cols), out_dtype),
                          grid_spec=grid_spec, compiler_params=_cparams("parallel", "parallel"))(a_idx, b_idx, a, b)


def _reduce_scatter(gs, names, tag):
    x, y, c, me = _position()
    i32 = lambda *v: jnp.stack([jnp.asarray(t, jnp.int32) for t in v])
    recvs = _scatter_d2d(gs, tag + "_d2d")
    own_idx = i32(*[_lin(_plane_pos(x, y, q) + (c,)) for q in range(4)])
    hs = [_add_blocks(g, own_idx, r, i32(0, 1, 2, 3), bf16, f"{tag}_add_chip_{nm}")
          for g, r, nm in zip(gs, recvs, names)]
    recv2s = _scatter_ici_first(hs, tag + "_ici_first")
    k1s = [_add_blocks(h, i32(1 + c), r2, i32(1), bf16, f"{tag}_add_onward_{nm}")[0]
           for h, r2, nm in zip(hs, recv2s, names)]
    lasts = _scatter_ici_second(k1s, tag + "_ici_second")
    return [[(h, 1), (r2, 1), (last[None], 1)] for h, r2, last in zip(hs, recv2s, lasts)]


ANY_SPEC = pl.BlockSpec(memory_space=pl.ANY)


def _mm(a, b, name, ta=False, tb=False, out_dtype=f32, after=()):
    if ta:
        a, ta = a.T, False
    if ta:
        k_dim, m_dim = a.shape
    else:
        m_dim, k_dim = a.shape
    if tb:
        n_dim, k2 = b.shape
    else:
        k2, n_dim = b.shape
    assert k_dim == k2, (a.shape, b.shape)
    assert a.dtype == bf16 and b.dtype == bf16
    bm = _pick(m_dim, (512, 768, 640, 256, 128))
    bn = _pick(n_dim, (512, 640, 256, 128))
    bk = _pick(k_dim, (1024, 1280, 768, 512))
    nk = k_dim // bk
    a_spec = (pl.BlockSpec((bk, bm), lambda i, j, k: (k, i)) if ta
              else pl.BlockSpec((bm, bk), lambda i, j, k: (i, k)))
    b_spec = (pl.BlockSpec((bn, bk), lambda i, j, k: (j, k)) if tb
              else pl.BlockSpec((bk, bn), lambda i, j, k: (k, j)))
    dims = (((0 if ta else 1,), (1 if tb else 0,)), ((), ()))

    n_after = len(after)

    def body_single(a_ref, b_ref, *rest):
        o_ref = rest[n_after]
        o_ref[...] = lax.dot_general(a_ref[...], b_ref[...], dims, preferred_element_type=f32).astype(o_ref.dtype)

    def body(a_ref, b_ref, *rest):
        o_ref, acc_ref = rest[n_after:]
        k = pl.program_id(2)

        @pl.when(k == 0)
        def _():
            acc_ref[...] = jnp.zeros_like(acc_ref)

        acc_ref[...] += lax.dot_general(a_ref[...], b_ref[...], dims, preferred_element_type=f32)

        @pl.when(k == nk - 1)
        def _():
            o_ref[...] = acc_ref[...].astype(o_ref.dtype)

    return pl.pallas_call(
        body_single if nk == 1 else body, name=name, out_shape=jax.ShapeDtypeStruct((m_dim, n_dim), out_dtype),
        grid=(m_dim // bm, n_dim // bn, nk), in_specs=[a_spec, b_spec] + [ANY_SPEC] * n_after,
        out_specs=pl.BlockSpec((bm, bn), lambda i, j, k: (i, j)),
        scratch_shapes=[] if nk == 1 else [pltpu.VMEM((bm, bn), f32)],
        compiler_params=_cparams("parallel", "parallel", "arbitrary"),
    )(a, b, *after)


def _rin(arr, width=None, cb=0, roff=0):
    return (arr, arr.shape[1] if width is None else width, cb, roff)


def _rowcall(fn, name, rows, tm, row_ins, par_ins, row_outs, acc_outs=(), after=()):
    nr, npar, nro, n_after = len(row_ins), len(par_ins), len(row_outs), len(after)
    in_specs, args = [], []
    for arr, width, cb, roff in row_ins:
        if roff >= 0:
            imap = lambda i, cb=cb, roff=roff: (i + roff, cb)
        else:
            imap = lambda i, cb=cb, roff=roff: (jnp.maximum(i + roff, 0), cb)
        in_specs.append(pl.BlockSpec((tm, width), imap))
        args.append(arr)
    for p in par_ins:
        in_specs.append(pl.BlockSpec(p.shape, lambda i: (0, 0)))
        args.append(p)
    out_shape, out_specs = [], []
    for width, dt in row_outs:
        out_shape.append(jax.ShapeDtypeStruct((rows, width), dt))
        out_specs.append(pl.BlockSpec((tm, width), lambda i: (i, 0)))
    for p, width in acc_outs:
        out_shape.append(jax.ShapeDtypeStruct((p, width), f32))
        out_specs.append(pl.BlockSpec((p, width), lambda i: (0, 0)))

    def body(*refs):
        i = pl.program_id(0)
        res = fn(i, *[r[...] for r in refs[:nr + npar]])
        outs = refs[nr + npar + n_after:]
        for o, v in zip(outs[:nro], res[:nro]):
            o[...] = v.astype(o.dtype)
        if acc_outs:
            @pl.when(i == 0)
            def _():
                for o in outs[nro:]:
                    o[...] = jnp.zeros_like(o)

            for o, v in zip(outs[nro:], res[nro:]):
                o[...] += v

    return pl.pallas_call(
        body, name=name, out_shape=out_shape, grid=(rows // tm,), in_specs=in_specs + [ANY_SPEC] * n_after,
        out_specs=out_specs, compiler_params=_cparams("arbitrary"),
    )(*args, *after)


def _rms(x, g):
    return x * lax.rsqrt(jnp.mean(x * x, axis=-1, keepdims=True) + EPS) * g


def _normmod(x, g, sc, sh):
    return _rms(x, g) * (1.0 + sc) + sh


def _rows2(v0, v1):
    rid = lax.broadcasted_iota(jnp.int32, (2, v0.shape[1]), 0)
    return jnp.where(rid == 0, v0, v1)


def _gelu(x):
    return 0.5 * x * (1.0 + jnp.tanh(0.7978845608028654 * (x + 0.044715 * (x * x * x))))


def _sigmoid(x):
    return 0.5 * (jnp.tanh(0.5 * x) + 1.0)


def _coeff_parts(pre_a, pre_x, ba, bx, lam):
    r = _sigmoid(pre_a + ba)
    ig = _sigmoid(pre_x + bx)
    nl = -lam
    sp = jnp.maximum(nl, 0.0) + jnp.log(1.0 + jnp.exp(-jnp.abs(nl)))
    la = -RG_C * r * sp
    a = jnp.exp(la)
    one_minus_a2 = -jnp.tanh(la) * (a * a + 1.0)
    inv_m = lax.rsqrt(one_minus_a2)
    return r, ig, sp, a, one_minus_a2 * inv_m, inv_m


def _coeff(pre_a, pre_x, u, ba, bx, lam):
    _, ig, _, a, m, _ = _coeff_parts(pre_a, pre_x, ba, bx, lam)
    return a, m * (ig * u)


def _coeff_bwd(pre_a, pre_x, u, ba, bx, lam, da, db):
    r, ig, sp, a, m, inv_m = _coeff_parts(pre_a, pre_x, ba, bx, lam)
    dbu = db * u
    dig = dbu * m
    dm = dbu * ig
    dla = a * (da - dm * a * inv_m)
    dpa = dla * (-RG_C * sp) * (r * (1.0 - r))
    dpx = dig * (ig * (1.0 - ig))
    dsp = jnp.sum(dla * (-RG_C * r), axis=0, keepdims=True)
    dlam = -dsp * _sigmoid(-lam)
    return (dpa, dpx, db * m * ig, jnp.sum(dpa, axis=0, keepdims=True), jnp.sum(dpx, axis=0, keepdims=True), dlam)


SCAN_CHUNK = 256


def _scan_call(a, v, chunk_of, reverse, name, backward):
    rows, width = a.shape
    n_out = 1 if backward else 2
    nt = SCAN_CHUNK // 8

    def body(a_ref, v_ref, *rest):
        outs, state_ref = rest[:-1], rest[-1]

        @pl.when(pl.program_id(0) == 0)
        def _():
            state_ref[...] = jnp.zeros_like(state_ref)

        rid = lax.broadcasted_iota(jnp.int32, (8, width), 0)

        def tile(j, st):
            t0 = pl.multiple_of((nt - 1 - j if reverse else j) * 8, 8)
            at = a_ref[pl.ds(t0, 8), :]
            vt = v_ref[pl.ds(t0, 8), :]
            out = jnp.zeros((8, width), f32)
            prev = jnp.zeros((8, width), f32)
            for i in (range(7, -1, -1) if reverse else range(8)):
                if backward:
                    g = vt[i:i + 1] + st
                    st = at[i:i + 1] * g
                    out = jnp.where(rid == i, g, out)
                else:
                    prev = jnp.where(rid == i, st, prev)
                    st = at[i:i + 1] * st + vt[i:i + 1]
                    out = jnp.where(rid == i, st, out)
            outs[0][pl.ds(t0, 8), :] = out
            if not backward:
                outs[1][pl.ds(t0, 8), :] = prev
            return st

        state_ref[0:1, :] = lax.fori_loop(0, nt, tile, state_ref[0:1, :])

    spec = pl.BlockSpec((SCAN_CHUNK, width), lambda t: (chunk_of(t), 0))
    return pl.pallas_call(
        body, name=name, out_shape=[jax.ShapeDtypeStruct((rows, width), f32)] * n_out,
        grid=(rows // SCAN_CHUNK,), in_specs=[spec, spec], out_specs=[spec] * n_out,
        scratch_shapes=[pltpu.VMEM((8, width), f32)],
        compiler_params=_cparams("arbitrary"),
    )(a, v)


CONV_CHUNK = 256


def _fill_padded(pad_ref, src_ref, start, n):
    cb = pad_ref.shape[1]
    pad_ref[pl.ds(0, HALO), :] = jnp.zeros((HALO, cb), f32)
    pad_ref[pl.ds(HALO, n), :] = src_ref[pl.ds(start, n), :].astype(f32)
    pad_ref[pl.ds(HALO + n, HALO), :] = jnp.zeros((HALO, cb), f32)


def _dwconv_fwd(x, x_cb0, w, b, taps, pad_left, segments, cb, name, emit_bf16):
    rows = x.shape[0]
    width = w.shape[1]

    def body(x_ref, w_ref, b_ref, *rest):
        outs, xp = rest[:-1], rest[-1]
        for start, n in segments:
            _fill_padded(xp, x_ref, start, n)
            for c0 in range(0, n, CONV_CHUNK):
                acc = jnp.zeros((CONV_CHUNK, cb), f32) + b_ref[...]
                for k in range(taps):
                    acc = acc + w_ref[k:k + 1, :] * xp[pl.ds(HALO + c0 + k - pad_left, CONV_CHUNK), :]
                for o in outs:
                    o[pl.ds(start + c0, CONV_CHUNK), :] = acc.astype(o.dtype)

    out_dtypes = [f32, bf16] if emit_bf16 else [f32]
    return pl.pallas_call(
        body, name=name, out_shape=[jax.ShapeDtypeStruct((rows, width), dt) for dt in out_dtypes],
        grid=(width // cb,),
        in_specs=[pl.BlockSpec((rows, cb), lambda j: (0, j + x_cb0)), pl.BlockSpec((taps, cb), lambda j: (0, j)),
                  pl.BlockSpec((1, cb), lambda j: (0, j))],
        out_specs=[pl.BlockSpec((rows, cb), lambda j: (0, j))] * len(out_dtypes),
        scratch_shapes=[pltpu.VMEM((rows + 2 * HALO, cb), f32)],
        compiler_params=_cparams("parallel"),
    )(x, w, b)


def _dwconv_bwd(douts, x, x_cb0, w, taps, pad_left, segments, cb, name, dx_dtype):
    rows = x.shape[0]
    width = w.shape[1]
    nd = len(douts)

    def body(*refs):
        d_refs, x_ref, w_ref = refs[:nd], refs[nd], refs[nd + 1]
        dx_ref, dw_ref, db_ref, xp, dp, dsum = refs[nd + 2:]
        dw_ref[...] = jnp.zeros_like(dw_ref)
        db_ref[...] = jnp.zeros_like(db_ref)
        if nd > 1:
            total = d_refs[0][...]
            for r in d_refs[1:]:
                total = total + r[...]
            dsum[...] = total
            d_ref = dsum
        else:
            d_ref = d_refs[0]
        for start, n in segments:
            _fill_padded(xp, x_ref, start, n)
            _fill_padded(dp, d_ref, start, n)
            for c0 in range(0, n, CONV_CHUNK):
                dchunk = dp[pl.ds(HALO + c0, CONV_CHUNK), :]
                db_ref[...] += jnp.sum(dchunk, axis=0, keepdims=True)
                acc = jnp.zeros((CONV_CHUNK, cb), f32)
                for k in range(taps):
                    acc = acc + w_ref[k:k + 1, :] * dp[pl.ds(HALO + c0 + pad_left - k, CONV_CHUNK), :]
                    xs = xp[pl.ds(HALO + c0 + k - pad_left, CONV_CHUNK), :]
                    dw_ref[k:k + 1, :] += jnp.sum(dchunk * xs, axis=0, keepdims=True)
                dx_ref[pl.ds(start + c0, CONV_CHUNK), :] = acc.astype(dx_ref.dtype)

    dspec = pl.BlockSpec((rows, cb), lambda j: (0, j))
    return pl.pallas_call(
        body, name=name,
        out_shape=[jax.ShapeDtypeStruct((rows, width), dx_dtype), jax.ShapeDtypeStruct((taps, width), f32),
                   jax.ShapeDtypeStruct((1, width), f32)],
        grid=(width // cb,),
        in_specs=[dspec] * nd + [pl.BlockSpec((rows, cb), lambda j: (0, j + x_cb0)),
                                 pl.BlockSpec((taps, cb), lambda j: (0, j))],
        out_specs=[dspec, pl.BlockSpec((taps, cb), lambda j: (0, j)), pl.BlockSpec((1, cb), lambda j: (0, j))],
        scratch_shapes=[pltpu.VMEM((rows + 2 * HALO, cb), f32), pltpu.VMEM((rows + 2 * HALO, cb), f32),
                        pltpu.VMEM((rows, cb), f32)],
        compiler_params=_cparams("parallel"),
    )(*douts, x, w)


def _ada_forward(c16, w_ada, b_loc):
    def body(c_ref, w_ref, b_ref, o_ref):
        cv = c_ref[...]
        s = (cv * _sigmoid(cv)).astype(bf16)
        o_ref[0] = jnp.dot(s, w_ref[0].astype(bf16), preferred_element_type=f32) + b_ref[0]

    return pl.pallas_call(
        body, name="ada_forward", out_shape=jax.ShapeDtypeStruct((2, 16, ADA_SHARD), f32), grid=(2,),
        in_specs=[pl.BlockSpec((16, D), lambda l: (0, 0)), pl.BlockSpec((1, D, ADA_SHARD), lambda l: (l, 0, 0)),
                  pl.BlockSpec((1, 1, ADA_SHARD), lambda l: (l, 0, 0))],
        out_specs=pl.BlockSpec((1, 16, ADA_SHARD), lambda l: (l, 0, 0)),
        compiler_params=_cparams("parallel"),
    )(c16, w_ada, b_loc)


def _ada_backward(c16, g16, w_ada):
    def body(c_ref, g_ref, w_ref, dw_ref, ds_ref):
        cv = c_ref[...]
        s = (cv * _sigmoid(cv)).astype(bf16)
        g = g_ref[0].astype(bf16)
        dw_ref[0] = lax.dot_general(s, g, (((0,), (0,)), ((), ())), preferred_element_type=f32)
        ds = lax.dot_general(g, w_ref[0].astype(bf16), (((1,), (1,)), ((), ())), preferred_element_type=f32)
        cc = cv[8:9]
        sg = _sigmoid(cc)
        dsilu = sg * (1.0 + cc * (1.0 - sg))
        ds_ref[0] = jnp.zeros((8, D), f32) + jnp.sum(ds[8:16], axis=0, keepdims=True) * dsilu

    return pl.pallas_call(
        body, name="ada_backward",
        out_shape=[jax.ShapeDtypeStruct((2, D, ADA_SHARD), f32), jax.ShapeDtypeStruct((2, 8, D), f32)], grid=(2,),
        in_specs=[pl.BlockSpec((16, D), lambda l: (0, 0)), pl.BlockSpec((1, 16, ADA_SHARD), lambda l: (l, 0, 0)),
                  pl.BlockSpec((1, D, ADA_SHARD), lambda l: (l, 0, 0))],
        out_specs=[pl.BlockSpec((1, D, ADA_SHARD), lambda l: (l, 0, 0)), pl.BlockSpec((1, 8, D), lambda l: (l, 0, 0))],
        compiler_params=_cparams("parallel"),
    )(c16, g16, w_ada)


def _adamw(pieces, w, m, v, name):
    rows, cols = w.shape
    n_arr = len(pieces)
    counts = [cnt for _, cnt in pieces]
    pieces = [p for p, _ in pieces]
    tm = 256 if (rows % 256 == 0 and rows > 256) else rows

    def body(*refs):
        p_refs = refs[:n_arr]
        w_ref, m_ref, v_ref, g_ref, d_ref, nm_ref, nv_ref = refs[n_arr:]
        g = None
        for p_ref in p_refs:
            for j in range(p_ref.shape[0]):
                term = p_ref[j].astype(f32)
                g = term if g is None else g + term
        m2 = ADAM_B1 * m_ref[...] + (1.0 - ADAM_B1) * g
        v2 = ADAM_B2 * v_ref[...] + (1.0 - ADAM_B2) * (g * g)
        m_hat = m2 / (1.0 - ADAM_B1 ** ADAM_STEP)
        v_hat = v2 / (1.0 - ADAM_B2 ** ADAM_STEP)
        g_ref[...] = g
        d_ref[...] = -ADAM_LR * (m_hat / (jnp.sqrt(v_hat) + ADAM_EPS) + ADAM_WD * w_ref[...])
        nm_ref[...] = m2
        nv_ref[...] = v2

    spec = pl.BlockSpec((tm, cols), lambda i: (i, 0))
    return pl.pallas_call(
        body, name=name, out_shape=[jax.ShapeDtypeStruct((rows, cols), f32)] * 4, grid=(rows // tm,),
        in_specs=[pl.BlockSpec((cnt, tm, cols), lambda i: (0, i, 0)) for cnt in counts] + [spec, spec, spec],
        out_specs=[spec] * 4, compiler_params=_cparams("parallel"),
    )(*pieces, w, m, v)


MLP_TM = 256
FB = F // N_DEV


def _stack_rows(vals, n):
    cols = vals[0].shape[1]
    rid = lax.broadcasted_iota(jnp.int32, (n, cols), 0)
    out = jnp.zeros((n, cols), f32)
    for k, v in enumerate(vals):
        out = jnp.where(rid == k, v, out)
    return out


N_MLP_PARAMS = 9


class _ParamRows:
    def __init__(self, ref):
        self.ref = ref

    def __getitem__(self, sl):
        return self.ref[8 * sl.start:8 * sl.start + 1, :]


def _resident(shape, imap):
    return pl.BlockSpec(shape, imap, pipeline_mode=pl.Buffered(1))


def _mlp_forward(xa, xa_roff, out_prev, par, w_in, w_out, layer, name):
    def body(xa_ref, op_ref, par_ref, win_ref, wout_ref, x1_ref, h_ref, r_ref, mo_ref, x2_ref, hn_ref):
        p = _ParamRows(par_ref)
        x1 = xa_ref[...] + p[0:1] * (op_ref[...] + p[1:2])
        h = _normmod(x1, p[2:3], p[3:4], p[4:5]).astype(bf16)
        x1_ref[...] = x1
        h_ref[...] = h
        mo = jnp.zeros((MLP_TM, D), f32)
        for j in range(N_DEV):
            r = jnp.maximum(jnp.dot(h, win_ref[j], preferred_element_type=f32), 0.0)
            r_ref[:, j * FB:(j + 1) * FB] = r.astype(bf16)
            mo = mo + jnp.dot((r * r).astype(bf16), wout_ref[j], preferred_element_type=f32)
        mo_ref[...] = mo.astype(bf16)
        x2 = x1 + p[5:6] * mo
        x2_ref[...] = x2
        hn_ref[...] = _normmod(x2, p[6:7], p[7:8], p[8:9]).astype(bf16)

    row = lambda width: pl.BlockSpec((MLP_TM, width), lambda i: (i, 0))
    return pl.pallas_call(
        body, name=name, grid=(T_LAT // MLP_TM,),
        out_shape=[jax.ShapeDtypeStruct((T_LAT, D), f32), jax.ShapeDtypeStruct((T_LAT, D), bf16),
                   jax.ShapeDtypeStruct((T_LAT, F), bf16), jax.ShapeDtypeStruct((T_LAT, D), bf16),
                   jax.ShapeDtypeStruct((T_LAT, D), f32), jax.ShapeDtypeStruct((T_LAT, D), bf16)],
        in_specs=[pl.BlockSpec((MLP_TM, D), lambda i: (i + xa_roff, 0)), row(D), pl.BlockSpec((8 * N_MLP_PARAMS, D), lambda i: (0, 0)),
                  _resident((N_DEV, None, D, FB), lambda i: (0, layer, 0, 0)),
                  _resident((N_DEV, None, FB, D), lambda i: (0, layer, 0, 0))],
        out_specs=[row(D), row(D), row(F), row(D), row(D), row(D)],
        compiler_params=_cparams("parallel"),
    )(xa, out_prev, par, w_in, w_out)


def _mlp_backward(dx2, x1, r, mo, out_prev, par, w_in, w_out, layer, name):
    nt = (((1,), (1,)), ((), ()))

    def body(dx2_ref, x1_ref, r_ref, mo_ref, op_ref, par_ref, win_ref, wout_ref, dx1_ref, dop_ref, dmo_ref, dhid_ref,
             acc_ref):
        p = _ParamRows(par_ref)
        dx2v = dx2_ref[...]
        dmo = (p[5:6] * dx2v).astype(bf16)
        dmo_ref[...] = dmo
        dh = jnp.zeros((MLP_TM, D), f32)
        mo = mo_ref[...].astype(f32)
        for j in range(N_DEV):
            rf = r_ref[:, j * FB:(j + 1) * FB].astype(f32)
            dact = lax.dot_general(dmo, wout_ref[j], nt, preferred_element_type=f32)
            dhid = (dact * (2.0 * rf)).astype(bf16)
            dhid_ref[:, j * FB:(j + 1) * FB] = dhid
            dh = dh + lax.dot_general(dhid, win_ref[j], nt, preferred_element_type=f32)
        x1 = x1_ref[...]
        _, vjp = jax.vjp(_normmod, x1, p[2:3], p[3:4], p[4:5])
        dx, dng, dsc, dsh = vjp(dh)
        dx1 = dx2v + dx
        dx1_ref[...] = dx1
        dop_ref[...] = (p[0:1] * dx1).astype(bf16)
        sums = _stack_rows([jnp.sum(dx1 * (op_ref[...] + p[1:2]), axis=0, keepdims=True),
                            p[0:1] * jnp.sum(dx1, axis=0, keepdims=True), dng, dsc, dsh,
                            jnp.sum(dx2v * mo, axis=0, keepdims=True)], 8)

        @pl.when(pl.program_id(0) == 0)
        def _():
            acc_ref[...] = jnp.zeros_like(acc_ref)

        acc_ref[...] += sums

    row = lambda width: pl.BlockSpec((MLP_TM, width), lambda i: (i, 0))
    return pl.pallas_call(
        body, name=name, grid=(T_LAT // MLP_TM,),
        out_shape=[jax.ShapeDtypeStruct((T_LAT, D), f32), jax.ShapeDtypeStruct((T_LAT, D), bf16),
                   jax.ShapeDtypeStruct((T_LAT, D), bf16), jax.ShapeDtypeStruct((T_LAT, F), bf16),
                   jax.ShapeDtypeStruct((8, D), f32)],
        in_specs=[row(D), row(D), row(F), row(D), row(D), pl.BlockSpec((8 * N_MLP_PARAMS, D), lambda i: (0, 0)),
                  _resident((N_DEV, None, D, FB), lambda i: (0, layer, 0, 0)),
                  _resident((N_DEV, None, FB, D), lambda i: (0, layer, 0, 0))],
        out_specs=[row(D), row(D), row(D), row(F), pl.BlockSpec((8, D), lambda i: (0, 0))],
        compiler_params=_cparams("arbitrary"),
    )(dx2, x1, r, mo, out_prev, par, w_in, w_out)


def _mlp_weight_grads(h, dhid, r, dmo, layer, other, tag):
    tn = (((0,), (0,)), ((), ()))

    def body_in(ht_ref, dhid_ref, *rest):
        rest[-1][...] = jnp.dot(ht_ref[...], dhid_ref[...], preferred_element_type=f32).astype(bf16)

    def body_out(r_ref, dmo_ref, *rest):
        rf = r_ref[...].astype(f32)
        rest[-1][...] = lax.dot_general((rf * rf).astype(bf16), dmo_ref[...], tn,
                                        preferred_element_type=f32).astype(bf16)

    def call(body, name, operands, specs, block, prev):
        extra = [] if prev is None else [prev]
        return pl.pallas_call(
            body, name=name, grid=(N_DEV,), out_shape=jax.ShapeDtypeStruct((N_DEV, 2) + block, bf16),
            in_specs=specs + [pl.BlockSpec(memory_space=pl.ANY)] * len(extra),
            out_specs=pl.BlockSpec((None, None) + block, lambda j: (j, layer, 0, 0)),
            input_output_aliases={} if prev is None else {2: 0},
            compiler_params=_cparams("parallel"),
        )(*operands, *extra)

    dw_in = call(body_in, tag + "_mlp_in_dw", [h.T, dhid],
                 [_resident((D, T_LAT), lambda j: (0, 0)), pl.BlockSpec((T_LAT, FB), lambda j: (0, j))], (D, FB),
                 None if other is None else other[0])
    dw_out = call(body_out, tag + "_mlp_out_dw", [r, dmo],
                  [pl.BlockSpec((T_LAT, FB), lambda j: (0, j)), _resident((T_LAT, D), lambda j: (0, 0))], (FB, D),
                  None if other is None else other[1])
    return dw_in, dw_out


def _pos_embed():
    n_rows = T_LAT // GRID_W
    q = D // 4
    omega = 1.0 / (POS_BASE ** (jnp.arange(q, dtype=f32) / q))
    er = jnp.arange(n_rows, dtype=jnp.int32).astype(f32)[:, None] * omega[None, :]
    ec = jnp.arange(GRID_W, dtype=jnp.int32).astype(f32)[:, None] * omega[None, :]
    by_row = jnp.concatenate([jnp.sin(er), jnp.cos(er)], axis=-1)[:, None, :]
    by_col = jnp.concatenate([jnp.sin(ec), jnp.cos(ec)], axis=-1)[None, :, :]
    full = jnp.concatenate([jnp.broadcast_to(by_row, (n_rows, GRID_W, D // 2)),
                            jnp.broadcast_to(by_col, (n_rows, GRID_W, D // 2))], axis=-1)
    return full.reshape(T_LAT, D)


HALF = R // 2
BLK_PER_HALF = N_BLK // 2
N_PARTS = 4


def _gate_matrix(w_a, w_x):
    eye = jnp.eye(BLK_PER_HALF, dtype=bf16)
    cols = []
    for h in range(2):
        for d in range(2):
            for w in (w_a, w_x):
                blocks = w[d, BLK_PER_HALF * h:BLK_PER_HALF * (h + 1)].astype(bf16)
                cols.append(jnp.einsum("hij,hg->higj", blocks, eye).reshape(HALF, HALF))
    return jnp.concatenate(cols, axis=1)


def _gate_blocks(dwg, part):
    out = []
    for h in range(2):
        blk = dwg[:, (N_PARTS * h + part) * HALF:(N_PARTS * h + part + 1) * HALF]
        blk = blk.reshape(BLK_PER_HALF, BLK, BLK_PER_HALF, BLK)
        out.append(jnp.moveaxis(jnp.diagonal(blk, axis1=0, axis2=2), -1, 0))
    return jnp.concatenate(out, axis=0)


def _gate_part(pre, part):
    return jnp.concatenate([pre[:, (N_PARTS * h + part) * HALF:(N_PARTS * h + part + 1) * HALF] for h in range(2)],
                           axis=1)


def _gate_unpart(parts):
    return jnp.concatenate([parts[p][:, h * HALF:(h + 1) * HALF] for h in range(2) for p in range(N_PARTS)], axis=1)


GATE_BM = 768


def _gates_fwd(u, wg):
    rows = u.shape[0]

    def body(u_ref, w_ref, o_ref):
        o_ref[...] = jnp.dot(u_ref[...], w_ref[...], preferred_element_type=f32)

    return pl.pallas_call(
        body, name="l0_gates", grid=(rows // GATE_BM, 2 * N_PARTS),
        out_shape=jax.ShapeDtypeStruct((rows, 2 * N_PARTS * HALF), f32),
        in_specs=[pl.BlockSpec((GATE_BM, HALF), lambda i, j: (i, j // N_PARTS)),
                  pl.BlockSpec((HALF, HALF), lambda i, j: (0, j))],
        out_specs=pl.BlockSpec((GATE_BM, HALF), lambda i, j: (i, j)),
        compiler_params=_cparams("parallel", "parallel"),
    )(u, wg)


def _gates_dx(dpre, wg):
    rows = dpre.shape[0]

    def body(d_ref, w_ref, o_ref, acc_ref):
        p = pl.program_id(2)

        @pl.when(p == 0)
        def _():
            acc_ref[...] = jnp.zeros_like(acc_ref)

        acc_ref[...] += lax.dot_general(d_ref[...], w_ref[...], (((1,), (1,)), ((), ())), preferred_element_type=f32)

        @pl.when(p == N_PARTS - 1)
        def _():
            o_ref[...] = acc_ref[...]

    return pl.pallas_call(
        body, name="l0_gates_dx", grid=(rows // GATE_BM, 2, N_PARTS), out_shape=jax.ShapeDtypeStruct((rows, R), f32),
        in_specs=[pl.BlockSpec((GATE_BM, HALF), lambda i, h, p: (i, N_PARTS * h + p)),
                  pl.BlockSpec((HALF, HALF), lambda i, h, p: (0, N_PARTS * h + p))],
        out_specs=pl.BlockSpec((GATE_BM, HALF), lambda i, h, p: (i, h)),
        scratch_shapes=[pltpu.VMEM((GATE_BM, HALF), f32)],
        compiler_params=_cparams("parallel", "parallel", "arbitrary"),
    )(dpre, wg)


def _gates_dw(u, dpre):
    rows = u.shape[0]

    def body(ut_ref, d_ref, o_ref):
        o_ref[...] = jnp.dot(ut_ref[...], d_ref[...], preferred_element_type=f32)

    return pl.pallas_call(
        body, name="l0_gates_dw", grid=(2 * N_PARTS,), out_shape=jax.ShapeDtypeStruct((HALF, 2 * N_PARTS * HALF), f32),
        in_specs=[pl.BlockSpec((HALF, rows), lambda j: (j // N_PARTS, 0)), pl.BlockSpec((rows, HALF), lambda j: (0, j))],
        out_specs=pl.BlockSpec((HALF, HALF), lambda j: (0, j)), compiler_params=_cparams("parallel"),
    )(u.T, dpre)


N_SCAN_CHUNKS = T_ALL // SCAN_CHUNK
SCAN_FWD = lambda t: t
SCAN_FWD_BWD = lambda t: N_SCAN_CHUNKS - 1 - t
SCAN_REV = lambda t: jnp.where(t == 0, 0, N_SCAN_CHUNKS - t)
SCAN_REV_BWD = lambda t: jnp.where(t == N_SCAN_CHUNKS - 1, 0, t + 1)
CONV_SEGMENTS = ((0, T_CTX), (T_CTX, T_LAT))
TM = 128
N_CTX_TILES = T_CTX // TM


def _local_step(x, ctx, target, mods, cmod, wts, late_weights, send_grads, start_after=()):
    sh1, sc1, g1, sh2, sc2, g2 = [[mods[l, i][None] for l in range(2)] for i in range(N_MOD)]
    ng = wts["norm_g"]
    xcat = jnp.concatenate([ctx, x], axis=0)
    poscat = jnp.concatenate([jnp.zeros((T_CTX, D), f32), _pos_embed()], axis=0)
    scp = jnp.concatenate([cmod[1][None], sc1[0]], axis=0)
    shp = jnp.concatenate([cmod[0][None], sh1[0]], axis=0)

    def blend(i, p):
        sel = jnp.where(i < N_CTX_TILES, 1.0, 0.0)
        return sel * p[0:1] + (1.0 - sel) * p[1:2]

    def f_pre0(i, xc, pos, g, scp_, shp_):
        x0 = xc + pos
        return x0, _normmod(x0, g, blend(i, scp_), blend(i, shp_))

    x0cat, h0 = _rowcall(f_pre0, "l0_prenorm", T_ALL, TM, [_rin(xcat), _rin(poscat)], [ng[0, 0][None], scp, shp],
                         [(D, f32), (D, bf16)], after=start_after)
    gr = _mm(h0, wts["rec_w_in"], "l0_in_proj")
    u, ub = _dwconv_fwd(gr, R // 256, wts["rec_conv_w"], wts["rec_conv_b"], 4, 1, CONV_SEGMENTS, 256,
                        "l0_conv", True)
    pre = _gates_fwd(ub, wts["gates"])

    def f_coeff(i, pre_, u_, ba, bx, lam):
        outs = []
        for d in range(2):
            a, b = _coeff(_gate_part(pre_, 2 * d), _gate_part(pre_, 2 * d + 1), u_,
                          ba[d:d + 1], bx[d:d + 1], lam[d:d + 1])
            outs += [a, b]
        return tuple(outs)

    a0, b0, a1, b1 = _rowcall(f_coeff, "l0_coeff", T_ALL, TM, [_rin(pre), _rin(u)],
                              [wts["rec_b_a"], wts["rec_b_x"], wts["rec_lambda"]], [(R, f32)] * 4)
    y0, yp0 = _scan_call(a0, b0, SCAN_FWD, False, "l0_scan_fwd", False)
    y1, yp1 = _scan_call(a1, b1, SCAN_REV, True, "l0_scan_rev", False)

    def f_gate(i, gp, y0_, y1_):
        return (_gelu(gp) * (y0_ + y1_),)

    (zb,) = _rowcall(f_gate, "l0_gate", T_LAT, TM,
                     [_rin(gr, R, 0, N_CTX_TILES), _rin(y0, None, 0, N_CTX_TILES), _rin(y1, None, 0, N_CTX_TILES)],
                     [], [(R, bf16)])
    wts = dict(wts, **late_weights("mlp", zb))
    out0 = _mm(zb, wts["rec_w_out"], "l0_out_proj")

    zero_d = jnp.zeros((1, D), f32)

    def mlp_params(rows):
        rows = rows + [zero_d] * (N_MLP_PARAMS - len(rows))
        return jnp.concatenate([jnp.broadcast_to(r, (8, D)) for r in rows], axis=0)

    par0 = mlp_params([g1[0], zero_d, ng[0, 1][None], sc2[0], sh2[0], g2[0], ng[1, 0][None], sc1[1], sh1[1]])
    x1, h1, r0, mo0, x2, h2 = _mlp_forward(x0cat, T_CTX // MLP_TM, out0, par0, wts["mlp_w_in"], wts["mlp_w_out"], 0,
                                           "l0_mlp")

    wts = dict(wts, **late_weights("conf", x2))
    pw = _mm(h2, wts["conf_w_pw1"], "l1_pw1")

    def f_glu(i, pa, pb, b1):
        return ((pa + b1[:, :D]) * _sigmoid(pb + b1[:, D:]),)

    (zg,) = _rowcall(f_glu, "l1_glu", T_LAT, TM, [_rin(pw, D, 0), _rin(pw, D, 1)], [wts["conf_b_pw1"]], [(D, f32)])
    (zc,) = _dwconv_fwd(zg, 0, wts["conf_conv_w"], wts["conf_conv_b"], 31, 15, ((0, T_LAT),), 128, "l1_conv", False)

    def ln_silu(z, lg, lb):
        mu = jnp.mean(z, axis=-1, keepdims=True)
        zc_ = z - mu
        var = jnp.mean(zc_ * zc_, axis=-1, keepdims=True)
        yv = zc_ * lax.rsqrt(var + EPS) * lg + lb
        return yv * _sigmoid(yv)

    def f_lnsilu(i, z, lg, lb):
        return (ln_silu(z, lg, lb),)

    (sb,) = _rowcall(f_lnsilu, "l1_ln_silu", T_LAT, TM, [_rin(zc)], [wts["conf_ln_g"], wts["conf_ln_b"]], [(D, bf16)])
    out1 = _mm(sb, wts["conf_w_pw2"], "l1_pw2")
    par1 = mlp_params([g1[1], wts["conf_b_pw2"], ng[1, 1][None], sc2[1], sh2[1], g2[1]])
    x3, h3, r1, mo1, x4, _ = _mlp_forward(x2, 0, out1, par1, wts["mlp_w_in"], wts["mlp_w_out"], 1, "l1_mlp")

    def loss_fn(x4_, fg, tgt):
        err = _rms(x4_, fg) - tgt
        per_row = jnp.mean(err * err, axis=-1, keepdims=True)
        return 0.5 * jnp.sum(per_row, axis=0, keepdims=True)

    def f_head(i, x4_, tgt, fg):
        loss, vjp = jax.vjp(lambda a, e: loss_fn(a, e, tgt), x4_, fg)
        dx, dfg = vjp(jnp.ones((1, 1), f32))
        return dx, jnp.broadcast_to(loss, (1, 128)), dfg

    dx4, loss_acc, dfinal_g = _rowcall(f_head, "head", T_LAT, TM, [_rin(x4), _rin(target)], [wts["final_g"]],
                                       [(D, f32)], [(1, 128), (1, D)])

    grads = {"final_g": dfinal_g}

    def normmod_bwd(xin, dh, dx_skip, g, sc, sh, tag, after):
        def fb(i, x_, dh_, dxs, g_, sc_, sh_):
            _, vjp = jax.vjp(_normmod, x_, g_, sc_, sh_)
            dx, dg, dsc, dsh = vjp(dh_)
            return dx + dxs, dg, dsc, dsh

        return _rowcall(fb, tag + "_normmod_bwd", T_LAT, TM, [_rin(xin), _rin(dh), _rin(dx_skip)], [g, sc, sh],
                        [(D, f32)], [(1, D)] * 3, after=after)

    dx3, dout1, dmo1, dhid1, acc1 = _mlp_backward(dx4, x3, r1, mo1, out1, par1, wts["mlp_w_in"], wts["mlp_w_out"], 1,
                                                  "l1_mlp_bwd")
    mlp_dw = _mlp_weight_grads(h3, dhid1, r1, dmo1, 1, None, "l1")
    dg1_1, db_pw2, dng11, dsc2_1, dsh2_1, dg2_1 = [acc1[k:k + 1] for k in range(6)]

    ds = _mm(dout1, wts["conf_w_pw2"], "l1_pw2_dx", tb=True)
    grads["conf_w_pw2"] = _mm(sb, dout1, "l1_pw2_dw", ta=True, out_dtype=bf16)
    grads["conf_b_pw2"] = db_pw2

    def f_lnsilu_bwd(i, z, ds_, lg, lb):
        _, vjp = jax.vjp(ln_silu, z, lg, lb)
        return vjp(ds_)

    dzc, dln_g, dln_b = _rowcall(f_lnsilu_bwd, "l1_ln_silu_bwd", T_LAT, TM, [_rin(zc), _rin(ds)],
                                 [wts["conf_ln_g"], wts["conf_ln_b"]], [(D, f32)], [(1, D)] * 2)
    grads["conf_ln_g"], grads["conf_ln_b"] = dln_g, dln_b
    dzg, dconv_w, dconv_b = _dwconv_bwd([dzc], zg, 0, wts["conf_conv_w"], 31, 15, ((0, T_LAT),), 128,
                                        "l1_conv_bwd", f32)
    grads["conf_conv_w"], grads["conf_conv_b"] = dconv_w, dconv_b

    def f_glu_bwd(i, pa, pb, dz, b1):
        _, vjp = jax.vjp(lambda a, b, c: (a + c[:, :D]) * _sigmoid(b + c[:, D:]), pa, pb, b1)
        da, db, dc = vjp(dz)
        return jnp.concatenate([da, db], axis=1), dc

    dpw, db_pw1 = _rowcall(f_glu_bwd, "l1_glu_bwd", T_LAT, TM, [_rin(pw, D, 0), _rin(pw, D, 1), _rin(dzg)],
                           [wts["conf_b_pw1"]], [(2 * D, bf16)], [(1, 2 * D)])
    grads["conf_b_pw1"] = db_pw1
    dh2 = _mm(dpw, wts["conf_w_pw1"], "l1_pw1_dx", tb=True)
    grads["conf_w_pw1"] = _mm(h2, dpw, "l1_pw1_dw", ta=True, out_dtype=bf16)
    sent = send_grads(["conf_w_pw2", "conf_w_pw1"], grads)
    dx2, dng10, dsc1_1, dsh1_1 = normmod_bwd(x2, dh2, dx3, ng[1, 0][None], sc1[1], sh1[1], "l1a", [sent])

    dx1, dout0, dmo0, dhid0, acc0 = _mlp_backward(dx2, x1, r0, mo0, out0, par0, wts["mlp_w_in"], wts["mlp_w_out"], 0,
                                                  "l0_mlp_bwd")
    grads["mlp_w_in"], grads["mlp_w_out"] = _mlp_weight_grads(h1, dhid0, r0, dmo0, 0, mlp_dw, "l0")
    sent = send_grads(["mlp_w_in", "mlp_w_out"], grads)
    dg1_0, _, dng01, dsc2_0, dsh2_0, dg2_0 = [acc0[k:k + 1] for k in range(6)]

    dz = _mm(dout0, wts["rec_w_out"], "l0_out_proj_dx", tb=True, after=[sent])
    grads["rec_w_out"] = _mm(zb, dout0, "l0_out_proj_dw", ta=True, out_dtype=bf16)
    sent = send_grads(["rec_w_out"], grads)

    def f_gate_bwd(i, gp, y0_, y1_, dz_):
        lat = jnp.where(i < N_CTX_TILES, 0.0, 1.0)
        _, vjp = jax.vjp(lambda a, b: _gelu(a) * b, gp, y0_ + y1_)
        dgp, dy = vjp(dz_)
        return dgp * lat, dy * lat

    dgp, dy = _rowcall(f_gate_bwd, "l0_gate_bwd", T_ALL, TM,
                       [_rin(gr, R, 0), _rin(y0), _rin(y1), _rin(dz, None, 0, -N_CTX_TILES)], [],
                       [(R, bf16), (R, f32)], after=[sent])
    (dh_f,) = _scan_call(a0, dy, SCAN_FWD_BWD, True, "l0_scan_fwd_bwd", True)
    (dh_r,) = _scan_call(a1, dy, SCAN_REV_BWD, False, "l0_scan_rev_bwd", True)

    def f_coeff_bwd(i, pre_, u_, dhf, dhr, ypf, ypr, ba, bx, lam):
        dpre, dba, dbx, dlam = [], [], [], []
        du = jnp.zeros_like(u_)
        for d, (dh_, yp_) in enumerate(((dhf, ypf), (dhr, ypr))):
            dpa, dpx, du_d, dba_d, dbx_d, dlam_d = _coeff_bwd(
                _gate_part(pre_, 2 * d), _gate_part(pre_, 2 * d + 1), u_, ba[d:d + 1], bx[d:d + 1], lam[d:d + 1],
                dh_ * yp_, dh_)
            dpre += [dpa, dpx]
            du = du + du_d
            dba.append(dba_d)
            dbx.append(dbx_d)
            dlam.append(dlam_d)
        return _gate_unpart(dpre), du, _rows2(*dba), _rows2(*dbx), _rows2(*dlam)

    dpre, du_direct, db_a, db_x, dlam = _rowcall(
        f_coeff_bwd, "l0_coeff_bwd", T_ALL, 64,
        [_rin(pre), _rin(u), _rin(dh_f), _rin(dh_r), _rin(yp0), _rin(yp1)],
        [wts["rec_b_a"], wts["rec_b_x"], wts["rec_lambda"]], [(4 * R, bf16), (R, f32)], [(2, R)] * 3)
    grads["rec_b_a"], grads["rec_b_x"], grads["rec_lambda"] = db_a, db_x, dlam
    du_gates = _gates_dx(dpre, wts["gates"])
    grads["gates"] = _gates_dw(ub, dpre)
    drec, dconv4_w, dconv4_b = _dwconv_bwd([du_direct, du_gates], gr, R // 256, wts["rec_conv_w"], 4, 1,
                                           CONV_SEGMENTS, 256, "l0_conv_bwd", bf16)
    grads["rec_conv_w"], grads["rec_conv_b"] = dconv4_w, dconv4_b
    dgr = jnp.concatenate([dgp, drec], axis=1)
    sent = send_grads(["replicated"], grads)
    grads["rec_w_in"] = _mm(h0, dgr, "l0_in_proj_dw", ta=True, out_dtype=bf16, after=[sent])
    dh0 = _mm(dgr, wts["rec_w_in"], "l0_in_proj_dx", tb=True, after=[send_grads(["rec_w_in"], grads)])

    def f_pre0_bwd(i, x0, dh_, dxs, g, scp_, shp_):
        lat = jnp.where(i < N_CTX_TILES, 0.0, 1.0)
        _, vjp = jax.vjp(lambda a, b, c, e: _normmod(a, b, blend(i, c), blend(i, e)), x0, g, scp_, shp_)
        dx, dg, dscp, dshp = vjp(dh_)
        return dx + lat * dxs, dg, dscp, dshp

    dx0cat, dng00, dscp, dshp = _rowcall(
        f_pre0_bwd, "l0_prenorm_bwd", T_ALL, TM, [_rin(x0cat), _rin(dh0), _rin(dx1, None, 0, -N_CTX_TILES)],
        [ng[0, 0][None], scp, shp], [(D, f32)], [(1, D), (2, D), (2, D)])

    grads["norm_g"] = jnp.stack([jnp.concatenate([dng00, dng01], 0), jnp.concatenate([dng10, dng11], 0)])
    dmods = jnp.stack([
        jnp.concatenate([dshp[1:2], dscp[1:2], dg1_0, dsh2_0, dsc2_0, dg2_0], axis=0),
        jnp.concatenate([dsh1_1, dsc1_1, dg1_1, dsh2_1, dsc2_1, dg2_1], axis=0)])
    dcmod = jnp.concatenate([dshp[0:1], dscp[0:1]], axis=0)
    return loss_acc[0, 0], dx0cat[T_CTX:], dmods, dcmod, grads


def _unshard_cols(g):
    g = jnp.moveaxis(g, 0, -2)
    return g.reshape(g.shape[:-2] + (g.shape[-2] * g.shape[-1],))


def _shard_cols(w):
    w = w.reshape(w.shape[:-1] + (N_DEV, w.shape[-1] // N_DEV))
    return jnp.moveaxis(w, -2, 0)


def _shard_rows(w):
    return w.reshape((N_DEV, w.shape[0] // N_DEV) + w.shape[1:])


SMALL_PACK_ROWS = 64


def kernel(x, c, ctx, c_ctx, w_ada, b_ada, norm_g, rec_w_in, rec_conv_w, rec_conv_b, rec_lambda, rec_w_a, rec_b_a, rec_w_x, rec_b_x, rec_w_out, conf_w_pw1, conf_b_pw1, conf_conv_w, conf_conv_b, conf_ln_g, conf_ln_b, conf_w_pw2, conf_b_pw2, mlp_w_in, mlp_w_out, final_g, loss_target, m_c_ctx, m_w_ada, m_b_ada, m_norm_g, m_rec_w_in, m_rec_conv_w, m_rec_conv_b, m_rec_lambda, m_rec_w_a, m_rec_b_a, m_rec_w_x, m_rec_b_x, m_rec_w_out, m_conf_w_pw1, m_conf_b_pw1, m_conf_conv_w, m_conf_conv_b, m_conf_ln_g, m_conf_ln_b, m_conf_w_pw2, m_conf_b_pw2, m_mlp_w_in, m_mlp_w_out, m_final_g, v_c_ctx, v_w_ada, v_b_ada, v_norm_g, v_rec_w_in, v_rec_conv_w, v_rec_conv_b, v_rec_lambda, v_rec_w_a, v_rec_b_a, v_rec_w_x, v_rec_b_x, v_rec_w_out, v_conf_w_pw1, v_conf_b_pw1, v_conf_conv_w, v_conf_conv_b, v_conf_ln_g, v_conf_ln_b, v_conf_w_pw2, v_conf_b_pw2, v_mlp_w_in, v_mlp_w_out, v_final_g):
    me = 4 * lax.axis_index("x") + 2 * lax.axis_index("y") + lax.axis_index("c")
    weights = dict(c_ctx=c_ctx, w_ada=w_ada, b_ada=b_ada, norm_g=norm_g, rec_w_in=rec_w_in, rec_conv_w=rec_conv_w,
                   rec_conv_b=rec_conv_b, rec_lambda=rec_lambda, rec_w_a=rec_w_a, rec_b_a=rec_b_a, rec_w_x=rec_w_x,
                   rec_b_x=rec_b_x, rec_w_out=rec_w_out, conf_w_pw1=conf_w_pw1, conf_b_pw1=conf_b_pw1,
                   conf_conv_w=conf_conv_w, conf_conv_b=conf_conv_b, conf_ln_g=conf_ln_g, conf_ln_b=conf_ln_b,
                   conf_w_pw2=conf_w_pw2, conf_b_pw2=conf_b_pw2, mlp_w_in=mlp_w_in, mlp_w_out=mlp_w_out, final_g=final_g)
    m_in = dict(c_ctx=m_c_ctx, w_ada=m_w_ada, b_ada=m_b_ada, norm_g=m_norm_g, rec_w_in=m_rec_w_in, rec_conv_w=m_rec_conv_w,
                rec_conv_b=m_rec_conv_b, rec_lambda=m_rec_lambda, rec_w_a=m_rec_w_a, rec_b_a=m_rec_b_a, rec_w_x=m_rec_w_x,
                rec_b_x=m_rec_b_x, rec_w_out=m_rec_w_out, conf_w_pw1=m_conf_w_pw1, conf_b_pw1=m_conf_b_pw1,
                conf_conv_w=m_conf_conv_w, conf_conv_b=m_conf_conv_b, conf_ln_g=m_conf_ln_g, conf_ln_b=m_conf_ln_b,
                conf_w_pw2=m_conf_w_pw2, conf_b_pw2=m_conf_b_pw2, mlp_w_in=m_mlp_w_in, mlp_w_out=m_mlp_w_out,
                final_g=m_final_g)
    v_in = dict(c_ctx=v_c_ctx, w_ada=v_w_ada, b_ada=v_b_ada, norm_g=v_norm_g, rec_w_in=v_rec_w_in, rec_conv_w=v_rec_conv_w,
                rec_conv_b=v_rec_conv_b, rec_lambda=v_rec_lambda, rec_w_a=v_rec_w_a, rec_b_a=v_rec_b_a, rec_w_x=v_rec_w_x,
                rec_b_x=v_rec_b_x, rec_w_out=v_rec_w_out, conf_w_pw1=v_conf_w_pw1, conf_b_pw1=v_conf_b_pw1,
                conf_conv_w=v_conf_conv_w, conf_conv_b=v_conf_conv_b, conf_ln_g=v_conf_ln_g, conf_ln_b=v_conf_ln_b,
                conf_w_pw2=v_conf_w_pw2, conf_b_pw2=v_conf_b_pw2, mlp_w_in=v_mlp_w_in, mlp_w_out=v_mlp_w_out,
                final_g=v_final_g)
    names = list(weights)

    small_items = [c, norm_g, rec_conv_w, rec_lambda, conf_b_pw1, conf_conv_w, conf_conv_b, conf_ln_g, conf_ln_b,
                   conf_b_pw2]
    flat = jnp.concatenate([a.reshape(-1) for a in small_items])
    flat = jnp.pad(flat, (0, SMALL_PACK_ROWS * 128 - flat.shape[0])).reshape(SMALL_PACK_ROWS, 128)
    (small_all,) = _all_gather([flat], "gather_small")

    small_all = small_all.reshape(N_DEV, -1)
    off = 0
    small = []
    for a in small_items:
        small.append(small_all[:, off:off + a.size].reshape((N_DEV,) + a.shape))
        off += a.size
    c_all, ng_all, rcw_all, lam_all, bpw1_all, ccw_all, ccb_all, lng_all, lnb_all, bpw2_all = small
    wts = {
        "norm_g": _unshard_cols(ng_all),
        "rec_conv_w": _unshard_cols(rcw_all)[0],
        "rec_lambda": _unshard_cols(lam_all)[0],
        "conf_b_pw1": _unshard_cols(bpw1_all),
        "conf_conv_w": _unshard_cols(ccw_all)[0],
        "conf_conv_b": _unshard_cols(ccb_all),
        "conf_ln_g": _unshard_cols(lng_all),
        "conf_ln_b": _unshard_cols(lnb_all),
        "conf_b_pw2": _unshard_cols(bpw2_all),
        "rec_conv_b": rec_conv_b,
        "rec_b_a": rec_b_a[0].reshape(2, R),
        "rec_b_x": rec_b_x[0].reshape(2, R),
        "final_g": final_g[None],
        "gates": _gate_matrix(rec_w_a[0], rec_w_x[0]),
    }

    c16 = jnp.concatenate([c_all[:, 0], jnp.broadcast_to(c_ctx[None], (8, D))], axis=0)
    b_loc = lax.dynamic_slice_in_dim(b_ada, me * ADA_SHARD, ADA_SHARD, axis=1)[:, None]
    (mods_all,) = _all_gather([_ada_forward(c16, w_ada, b_loc)], "gather_mods")
    mods_all = _unshard_cols(mods_all)
    mods = lax.dynamic_index_in_dim(mods_all, me, axis=1, keepdims=False).reshape(2, N_MOD, D)
    cmod = mods_all[0, 8, :2 * D].reshape(2, D)

    as_shard = lambda a: a.astype(bf16).reshape(-1, a.shape[-1])
    early = _all_gather_2level([as_shard(rec_w_in[0])], "gather_weights_early")
    wts["rec_w_in"] = _unshard_cols(early[0])
    late_items = {"mlp": [rec_w_out[0], mlp_w_in, mlp_w_out], "conf": [conf_w_pw1[0], conf_w_pw2[0]]}
    late_handles, order = {}, [early[0], mods]
    for group in ("mlp", "conf"):
        shards = [as_shard(a) for a in late_items[group]]
        lands = [_own_block_filled(s, me) for s in shards]
        if group == "mlp":
            late_handles[group], token = _chip_gather_start(shards, lands, "gather_mlp_start", after=order)
        else:
            late_handles[group], token = _exchange_start(shards, lands, "gather_conf_start", False, after=order)
        order = [token]

    def late_weights(group, after):
        if group == "mlp":
            forwarded = _chip_gather_forward(late_handles[group], after, "gather_mlp_forward")
            got = _chip_gather_wait(forwarded, after, "gather_mlp_wait")
        else:
            got = _exchange_wait(late_handles[group], after, "gather_conf_wait", False)
        got = [g.reshape((N_DEV,) + a.shape) for g, a in zip(got, late_items[group])]
        if group == "mlp":
            return {"rec_w_out": got[0].reshape(R, D), "mlp_w_in": got[1], "mlp_w_out": got[2]}
        return {"conf_w_pw1": _unshard_cols(got[0]), "conf_w_pw2": got[1].reshape(D, D)}

    to_blocks = {"rec_w_in": _shard_cols, "conf_w_pw1": _shard_cols, "rec_w_out": _shard_rows, "conf_w_pw2": _shard_rows,
                 "mlp_w_in": lambda g: g, "mlp_w_out": lambda g: g}
    grad_handles = []

    repl_names = ["rec_conv_b", "rec_w_a", "rec_w_x", "rec_b_a", "rec_b_x", "final_g"]

    def send_replicated(grads):
        dwg = grads["gates"]
        repl = {"rec_conv_b": grads["rec_conv_b"],
                "rec_w_a": jnp.stack([_gate_blocks(dwg, 0), _gate_blocks(dwg, 2)]),
                "rec_w_x": jnp.stack([_gate_blocks(dwg, 1), _gate_blocks(dwg, 3)]),
                "rec_b_a": grads["rec_b_a"], "rec_b_x": grads["rec_b_x"], "final_g": grads["final_g"]}
        flat = jnp.concatenate([repl[n].reshape(-1) for n in repl_names])
        rows = -(-flat.shape[0] // (16 * D)) * 16
        flat = jnp.pad(flat, (0, rows * D - flat.shape[0])).reshape(rows, D).astype(bf16)
        handle, sent = _exchange_start([flat], [_own_block_filled(flat, me)], "gather_replicated_start", False)
        grad_handles.append((["replicated"], handle))
        return sent

    def send_grads(group, grads):
        if group == ["replicated"]:
            return send_replicated(grads)
        blocks = [to_blocks[n](grads[n]) for n in group]
        blocks = [g.reshape(N_DEV, -1, g.shape[-1]) for g in blocks]
        lands = [_own_block_filled(lax.dynamic_index_in_dim(g, me, 0, keepdims=False), me) for g in blocks]
        handle, sent = _exchange_start(blocks, lands, "scatter_start_" + group[0], True)
        grad_handles.append((group, handle))
        return sent

    loss_part, grad_x, dmods, dcmod, grads = _local_step(x[0], ctx[0], loss_target[0], mods, cmod, wts, late_weights,
                                                         send_grads, start_after=order)
    loss = lax.psum(loss_part, ("x", "y", "c"))

    dm_flat = jnp.concatenate([dmods.reshape(-1), dcmod.reshape(-1)]).reshape(-1, 128)
    (dm_all,) = _all_gather([dm_flat], "gather_dmods")
    dm_all = dm_all.reshape(N_DEV, -1)
    dmods_all = dm_all[:, :2 * N_MOD * D].reshape(N_DEV, 2, N_MOD * D)
    dcmod_all = jnp.pad(dm_all[:, 2 * N_MOD * D:], ((0, 0), (0, (N_MOD - 2) * D)))
    g16_full = jnp.stack([jnp.concatenate([dmods_all[:, 0], dcmod_all], axis=0),
                          jnp.concatenate([dmods_all[:, 1], jnp.zeros_like(dcmod_all)], axis=0)])
    g16 = lax.dynamic_slice_in_dim(g16_full, me * ADA_SHARD, ADA_SHARD, axis=2)
    dw_ada, ds_part = _ada_backward(c16, g16, w_ada)
    (ds_all,) = _all_gather([ds_part[0]], "gather_dsilu")

    big_names, big_pieces, repl_all = [], [], None
    for group, handle in grad_handles:
        if group == ["replicated"]:
            repl_all = _exchange_wait(handle, grad_x, "gather_replicated_wait", False)[0].reshape(N_DEV, -1)
            continue
        for n, got in zip(group, _exchange_wait(handle, grad_x, "scatter_wait_" + group[0], True)):
            big_names.append(n)
            big_pieces.append([(got, N_DEV)])
    small_sharded = ["norm_g", "rec_conv_w", "rec_lambda", "conf_b_pw1", "conf_conv_w", "conf_conv_b", "conf_ln_g",
                     "conf_ln_b", "conf_b_pw2"]
    pack = jnp.concatenate([_shard_cols(grads[n]).reshape(N_DEV, -1) for n in small_sharded], axis=1)
    pack_len = pack.shape[1]
    pack = jnp.pad(pack, ((0, 0), (0, SMALL_PACK_ROWS * 128 - pack_len))).reshape(N_DEV, SMALL_PACK_ROWS, 128)
    (pack_recv,) = _all_to_all([pack], "scatter_small_grads")
    pack_recv = pack_recv.reshape(N_DEV, -1)


    def as2d(shape):
        rows = 1
        for s in shape[:-1]:
            rows *= s
        return (rows, shape[-1])

    def whole(arr, shape):
        arr = arr.reshape((-1,) + as2d(shape))
        return (arr, arr.shape[0])

    pieces = {}
    shard_shapes = {n: weights[n].shape for n in names}
    for n, parts in zip(big_names, big_pieces):
        pieces[n] = parts
    off = 0
    for n in small_sharded:
        size = weights[n].size
        pieces[n] = [whole(pack_recv[:, off:off + size], shard_shapes[n])]
        off += size
    off = 0
    for n in repl_names:
        size = weights[n].size
        pieces[n] = [whole(repl_all[:, off:off + size], shard_shapes[n])]
        off += size
    pieces["w_ada"] = [whole(dw_ada, shard_shapes["w_ada"])]
    db_terms = jnp.concatenate([dmods_all, jnp.stack([dcmod_all, jnp.zeros_like(dcmod_all)], axis=1)], axis=0)
    pieces["b_ada"] = [whole(db_terms, shard_shapes["b_ada"])]
    pieces["c_ctx"] = [whole(ds_all[:, 0], shard_shapes["c_ctx"])]

    g_out, d_out, m_out, v_out = {}, {}, {}, {}
    for n in names:
        shape = shard_shapes[n]
        r2, c2 = as2d(shape)
        p = pieces[n]
        g, dl, nm, nv = _adamw(p, weights[n].reshape(r2, c2), m_in[n].reshape(r2, c2), v_in[n].reshape(r2, c2),
                               "adamw_" + n)
        g_out[n], d_out[n], m_out[n], v_out[n] = (t.reshape(shape) for t in (g, dl, nm, nv))

    return (loss, grad_x[None], *[g_out[n] for n in names], *[d_out[n] for n in names],
            *[m_out[n] for n in names], *[v_out[n] for n in names])
```

```python
import functools

import jax
import jax.numpy as jnp
from jax import lax
from jax.experimental import pallas as pl
from jax.experimental.pallas import tpu as pltpu

f32 = jnp.float32
bf16 = jnp.bfloat16

N_DEV = 8
D = 1024
T_LAT = 2048
T_CTX = 256
T_ALL = T_CTX + T_LAT
R = 1280
N_BLK = 16
BLK = R // N_BLK
F = 4096
GRID_W = 64
RG_C = 8.0
EPS = 1e-6
POS_BASE = 10000.0
N_MOD = 6
ADA_SHARD = N_MOD * D // N_DEV

ADAM_LR = 0.001
ADAM_B1 = 0.9
ADAM_B2 = 0.999
ADAM_EPS = 1e-08
ADAM_WD = 0.01
ADAM_STEP = 10

VMEM_LIMIT_V7X = 56 * 1024 * 1024
HALO = 16
MESH = pl.DeviceIdType.MESH


def _cparams(*sem):
    return pltpu.CompilerParams(dimension_semantics=sem, vmem_limit_bytes=VMEM_LIMIT_V7X)


def _pick(n, cands):
    for c in cands:
        if n % c == 0:
            return c
    raise ValueError(f"no block size for {n}")


def _position():
    x, y, c = lax.axis_index("x"), lax.axis_index("y"), lax.axis_index("c")
    return x, y, c, 4 * x + 2 * y + c


def _peer(x, y, c, k):
    px = (1 - x) if (k >> 2) & 1 else x
    py = (1 - y) if (k >> 1) & 1 else y
    pc = (1 - c) if k & 1 else c
    return (px, py, pc), 4 * px + 2 * py + pc


def _exchange(arrs, name, scatter):
    n = len(arrs)

    def body(*refs):
        ins, outs = refs[:n], refs[n:2 * n]
        send_sems, recv_sems, local_sems = refs[2 * n:]
        x, y, c, me = _position()
        local = []
        for a in range(n):
            src = ins[a].at[me] if scatter else ins[a]
            cp = pltpu.make_async_copy(src, outs[a].at[me], local_sems.at[a])
            cp.start()
            local.append(cp)
        sends, recvs = [], []
        for a in range(n):
            for k in range(1, N_DEV):
                peer, peer_lin = _peer(x, y, c, k)
                src = ins[a].at[peer_lin] if scatter else ins[a]
                cp = pltpu.make_async_remote_copy(
                    src_ref=src, dst_ref=outs[a].at[me], send_sem=send_sems.at[a, k - 1],
                    recv_sem=recv_sems.at[a, k - 1], device_id=peer, device_id_type=MESH)
                cp.start()
                sends.append(cp)
                recvs.append(pltpu.make_async_remote_copy(
                    src_ref=src, dst_ref=outs[a].at[peer_lin], send_sem=send_sems.at[a, k - 1],
                    recv_sem=recv_sems.at[a, k - 1], device_id=peer, device_id_type=MESH))
        for cp in recvs:
            cp.wait_recv()
        for cp in sends:
            cp.wait_send()
        for cp in local:
            cp.wait()

    if scatter:
        out_shape = [jax.ShapeDtypeStruct(a.shape, a.dtype) for a in arrs]
    else:
        out_shape = [jax.ShapeDtypeStruct((N_DEV,) + a.shape, a.dtype) for a in arrs]
    any_spec = pl.BlockSpec(memory_space=pl.ANY)
    return pl.pallas_call(
        body, name=name, out_shape=out_shape,
        in_specs=[any_spec] * n, out_specs=[any_spec] * n,
        scratch_shapes=[pltpu.SemaphoreType.DMA((n, N_DEV - 1)), pltpu.SemaphoreType.DMA((n, N_DEV - 1)),
                        pltpu.SemaphoreType.DMA((n,))],
    )(*arrs)


def _all_gather(arrs, name):
    return _exchange(arrs, name, scatter=False)


def _all_to_all(arrs, name):
    return _exchange(arrs, name, scatter=True)


def _lin(p):
    return 4 * p[0] + 2 * p[1] + p[2]


HBM_SPEC = pl.BlockSpec(memory_space=pltpu.HBM)
SEM_SPEC = pl.BlockSpec(memory_space=pltpu.SEMAPHORE)
DATAFLOW_EFFECT = pltpu.SideEffectType.DATAFLOW_SIDE_EFFECTING


def _split_copies(srcs, lands, send_sems, recv_sems, scatter):
    x, y, c, me = _position()
    out = []
    for a in range(len(srcs)):
        for k in range(1, N_DEV):
            peer, peer_lin = _peer(x, y, c, k)
            src = srcs[a].at[peer_lin] if scatter else srcs[a]
            mk = lambda slot: pltpu.make_async_remote_copy(
                src_ref=src, dst_ref=lands[a].at[slot], send_sem=send_sems.at[a * (N_DEV - 1) + k - 1],
                recv_sem=recv_sems.at[a * (N_DEV - 1) + k - 1], device_id=peer, device_id_type=MESH)
            out.append((mk(me), mk(peer_lin)))
    return out


def _exchange_start(srcs, lands, name, scatter, after=()):
    n = len(srcs)
    n_after = len(after)

    def body(*refs):
        srcs_r, lands_r = refs[:n], refs[n:2 * n]
        send_sems, recv_sems = refs[2 * n + n_after], refs[2 * n + n_after + 1]
        token = refs[-1]
        for outgoing, _ in _split_copies(srcs_r, lands_r, send_sems, recv_sems, scatter):
            outgoing.start()
        token[...] = jnp.zeros_like(token)

    hbm = lambda a: pltpu.HBM(a.shape, a.dtype)
    res = pl.pallas_call(
        body, name=name,
        out_shape=(pltpu.SemaphoreType.DMA((n * (N_DEV - 1),)), pltpu.SemaphoreType.DMA((n * (N_DEV - 1),)),
                   *[hbm(a) for a in srcs], *[hbm(a) for a in lands], jax.ShapeDtypeStruct((8, 128), f32)),
        in_specs=[HBM_SPEC] * (2 * n) + [pl.BlockSpec(memory_space=pl.ANY)] * n_after,
        out_specs=(SEM_SPEC, SEM_SPEC, *[HBM_SPEC] * (2 * n), pl.BlockSpec(memory_space=pltpu.VMEM)),
        input_output_aliases={i: 2 + i for i in range(2 * n)},
        compiler_params=pltpu.CompilerParams(has_side_effects=DATAFLOW_EFFECT),
    )(*[pltpu.with_memory_space_constraint(a, pltpu.HBM) for a in list(srcs) + list(lands)], *after)
    return (res[0], res[1], list(res[2:2 + n]), list(res[2 + n:2 + 2 * n])), res[-1]


def _exchange_wait(handle, after, name, scatter):
    send_sems, recv_sems, srcs, lands = handle
    n = len(srcs)

    def body(*refs):
        srcs_r, lands_r = refs[:n], refs[n:2 * n]
        send_s, recv_s = refs[2 * n], refs[2 * n + 1]
        for outgoing, incoming in _split_copies(srcs_r, lands_r, send_s, recv_s, scatter):
            outgoing.wait_send()
            incoming.wait_recv()

    hbm = lambda a: pltpu.HBM(a.shape, a.dtype)
    res = pl.pallas_call(
        body, name=name, out_shape=tuple(hbm(a) for a in list(srcs) + list(lands)),
        in_specs=[HBM_SPEC] * (2 * n) + [SEM_SPEC, SEM_SPEC, pl.BlockSpec(memory_space=pl.ANY)],
        out_specs=tuple([HBM_SPEC] * (2 * n)),
        input_output_aliases={i: i for i in range(2 * n)},
        compiler_params=pltpu.CompilerParams(has_side_effects=DATAFLOW_EFFECT),
    )(*srcs, *lands, send_sems, recv_sems, after)
    return list(res[n:])


def _chip_peers(x, y, c):
    return [(x, y, 1 - c)] + [_plane_pos(x, y, q) + (c,) for q in (2, 1, 3)]


def _chip_gather_start(shards, lands, name, after=()):
    n, n_after = len(shards), len(after)

    def body(*refs):
        srcs_r, lands_r = refs[:n], refs[n:2 * n]
        send_sems, recv_sems = refs[2 * n + n_after], refs[2 * n + n_after + 1]
        x, y, c, me = _position()
        for a in range(n):
            for k, peer in enumerate(_chip_peers(x, y, c)):
                pltpu.make_async_remote_copy(
                    src_ref=srcs_r[a], dst_ref=lands_r[a].at[me], send_sem=send_sems.at[4 * a + k],
                    recv_sem=recv_sems.at[4 * a + k], device_id=peer, device_id_type=MESH).start()
        refs[-1][...] = jnp.zeros_like(refs[-1])

    hbm = lambda a: pltpu.HBM(a.shape, a.dtype)
    res = pl.pallas_call(
        body, name=name,
        out_shape=(pltpu.SemaphoreType.DMA((4 * n,)), pltpu.SemaphoreType.DMA((4 * n,)),
                   *[hbm(a) for a in shards], *[hbm(a) for a in lands], jax.ShapeDtypeStruct((8, 128), f32)),
        in_specs=[HBM_SPEC] * (2 * n) + [ANY_SPEC] * n_after,
        out_specs=(SEM_SPEC, SEM_SPEC, *[HBM_SPEC] * (2 * n), pl.BlockSpec(memory_space=pltpu.VMEM)),
        input_output_aliases={i: 2 + i for i in range(2 * n)},
        compiler_params=pltpu.CompilerParams(has_side_effects=DATAFLOW_EFFECT),
    )(*[pltpu.with_memory_space_constraint(a, pltpu.HBM) for a in list(shards) + list(lands)], *after)
    return (res[0], res[1], list(res[2:2 + n]), list(res[2 + n:2 + 2 * n])), res[-1]


def _chip_gather_forward(handle, after, name):
    send_sems, recv_sems, srcs, lands = handle
    n = len(srcs)

    def body(*refs):
        srcs_r, lands_r = refs[:n], refs[n:2 * n]
        send1, recv1 = refs[2 * n], refs[2 * n + 1]
        send2, recv2 = refs[2 * n + 3], refs[2 * n + 4]
        x, y, c, me = _position()
        peers = _chip_peers(x, y, c)
        for a in range(n):
            for k, peer in enumerate(peers):
                mk = lambda slot: pltpu.make_async_remote_copy(
                    src_ref=srcs_r[a], dst_ref=lands_r[a].at[slot], send_sem=send1.at[4 * a + k],
                    recv_sem=recv1.at[4 * a + k], device_id=peer, device_id_type=MESH)
                mk(me).wait_send()
                mk(_lin(peer)).wait_recv()
        for a in range(n):
            for k, peer in enumerate(peers[1:]):
                slot = _lin(peer)
                pltpu.make_async_remote_copy(
                    src_ref=lands_r[a].at[slot], dst_ref=lands_r[a].at[slot], send_sem=send2.at[3 * a + k],
                    recv_sem=recv2.at[3 * a + k], device_id=peers[0], device_id_type=MESH).start()

    hbm = lambda a: pltpu.HBM(a.shape, a.dtype)
    res = pl.pallas_call(
        body, name=name,
        out_shape=(pltpu.SemaphoreType.DMA((3 * n,)), pltpu.SemaphoreType.DMA((3 * n,)), *[hbm(a) for a in lands]),
        in_specs=[HBM_SPEC] * (2 * n) + [SEM_SPEC, SEM_SPEC, ANY_SPEC],
        out_specs=(SEM_SPEC, SEM_SPEC, *[HBM_SPEC] * n),
        input_output_aliases={n + i: 2 + i for i in range(n)},
        compiler_params=pltpu.CompilerParams(has_side_effects=DATAFLOW_EFFECT),
    )(*srcs, *lands, send_sems, recv_sems, after)
    return (res[0], res[1], list(res[2:]))


def _chip_gather_wait(handle, after, name):
    send_sems, recv_sems, lands = handle
    n = len(lands)

    def body(*refs):
        lands_r, send2, recv2 = refs[:n], refs[n], refs[n + 1]
        x, y, c, me = _position()
        peers = _chip_peers(x, y, c)
        for a in range(n):
            for k, (px, py, pc) in enumerate(peers[1:]):
                mk = lambda slot: pltpu.make_async_remote_copy(
                    src_ref=lands_r[a].at[slot], dst_ref=lands_r[a].at[slot], send_sem=send2.at[3 * a + k],
                    recv_sem=recv2.at[3 * a + k], device_id=peers[0], device_id_type=MESH)
                mk(_lin((px, py, pc))).wait_send()
                mk(_lin((px, py, 1 - pc))).wait_recv()

    hbm = lambda a: pltpu.HBM(a.shape, a.dtype)
    res = pl.pallas_call(
        body, name=name, out_shape=tuple(hbm(a) for a in lands),
        in_specs=[HBM_SPEC] * n + [SEM_SPEC, SEM_SPEC, ANY_SPEC], out_specs=tuple([HBM_SPEC] * n),
        input_output_aliases={i: i for i in range(n)},
        compiler_params=pltpu.CompilerParams(has_side_effects=DATAFLOW_EFFECT),
    )(*lands, send_sems, recv_sems, after)
    return list(res)


def _own_block_filled(block, me):
    land = lax.empty((N_DEV,) + block.shape, block.dtype)
    return lax.dynamic_update_index_in_dim(land, block, me, 0)


def _staged_copy(src, dst, buf, in_sems, out_sems, rows, chunk):
    n = rows // chunk

    def rd(i):
        return pltpu.make_async_copy(src.at[pl.ds(i * chunk, chunk)], buf.at[i % 2], in_sems.at[i % 2])

    def wr(i):
        return pltpu.make_async_copy(buf.at[i % 2], dst.at[pl.ds(i * chunk, chunk)], out_sems.at[i % 2])

    rd(0).start()
    for i in range(n):
        if i + 1 < n:
            if i >= 1:
                wr(i - 1).wait()
            rd(i + 1).start()
        rd(i).wait()
        wr(i).start()
    for i in range(max(n - 2, 0), n):
        wr(i).wait()


def _all_gather_2level(shards, name):
    n = len(shards)
    chunks = [_pick(s.shape[0], (416, 512, 256, 160, 128, 64, 16)) for s in shards]

    def body(*refs):
        ins, outs = refs[:n], refs[n:2 * n]
        send_sems, recv_sems, in_sems, out_sems = refs[2 * n:2 * n + 4]
        bufs = refs[2 * n + 4:]
        x, y, c, me = _position()
        sib, xn, yn, dg = (x, y, 1 - c), (1 - x, y, c), (x, 1 - y, c), (1 - x, 1 - y, c)

        def cp(a, k, src, slot, to):
            return pltpu.make_async_remote_copy(src_ref=src, dst_ref=outs[a].at[slot], send_sem=send_sems.at[a, k],
                                                recv_sem=recv_sems.at[a, k], device_id=to, device_id_type=MESH)

        for a in range(n):
            for k, to in ((0, sib), (1, xn), (2, yn)):
                cp(a, k, ins[a], me, to).start()
        for a in range(n):
            cp(a, 1, ins[a], _lin(xn), xn).wait_recv()
            cp(a, 3, outs[a].at[_lin(xn)], _lin(xn), sib).start()

            @pl.when(c == 0)
            def _():
                cp(a, 5, outs[a].at[_lin(xn)], _lin(xn), yn).start()

            cp(a, 2, ins[a], _lin(yn), yn).wait_recv()
            cp(a, 4, outs[a].at[_lin(yn)], _lin(yn), sib).start()

            @pl.when(c == 1)
            def _():
                cp(a, 5, outs[a].at[_lin(yn)], _lin(yn), xn).start()

        for a in range(n):
            cp(a, 5, ins[a], _lin(dg), xn).wait_recv()
            cp(a, 6, outs[a].at[_lin(dg)], _lin(dg), sib).start()
        for a in range(n):
            _staged_copy(ins[a], outs[a].at[me], bufs[a], in_sems.at[a], out_sems.at[a], shards[a].shape[0], chunks[a])
        for a in range(n):
            for k, origin in ((0, sib), (3, (1 - x, y, 1 - c)), (4, (x, 1 - y, 1 - c)), (6, (1 - x, 1 - y, 1 - c))):
                cp(a, k, ins[a], _lin(origin), sib).wait_recv()
            for k in range(7):
                cp(a, k, ins[a], me, sib).wait_send()

    any_spec = pl.BlockSpec(memory_space=pl.ANY)
    return pl.pallas_call(
        body, name=name, out_shape=[jax.ShapeDtypeStruct((N_DEV,) + s.shape, s.dtype) for s in shards],
        in_specs=[any_spec] * n, out_specs=[any_spec] * n,
        scratch_shapes=[pltpu.SemaphoreType.DMA((n, 7)), pltpu.SemaphoreType.DMA((n, 7)),
                        pltpu.SemaphoreType.DMA((n, 2)), pltpu.SemaphoreType.DMA((n, 2))]
        + [pltpu.VMEM((2, ch, s.shape[1]), s.dtype) for ch, s in zip(chunks, shards)],
    )(*shards)


def _plane_pos(x, y, q):
    return ((1 - x) if q & 2 else x, (1 - y) if q & 1 else y)


def _scatter_call(body, name, ins, out_shape, sems_per_array):
    n = len(ins)
    any_spec = pl.BlockSpec(memory_space=pl.ANY)
    return pl.pallas_call(
        body, name=name, out_shape=out_shape, in_specs=[any_spec] * n, out_specs=[any_spec] * n,
        scratch_shapes=[pltpu.SemaphoreType.DMA((n, sems_per_array)), pltpu.SemaphoreType.DMA((n, sems_per_array))],
    )(*ins)


def _scatter_d2d(gs, name):
    n = len(gs)

    def body(*refs):
        g_refs, recv_refs, send_sems, recv_sems = refs[:n], refs[n:2 * n], refs[2 * n], refs[2 * n + 1]
        x, y, c, me = _position()
        sib = (x, y, 1 - c)
        sends = []
        for a in range(n):
            for q in range(4):
                px, py = _plane_pos(x, y, q)
                cp = pltpu.make_async_remote_copy(
                    src_ref=g_refs[a].at[_lin((px, py, 1 - c))], dst_ref=recv_refs[a].at[q],
                    send_sem=send_sems.at[a, q], recv_sem=recv_sems.at[a, q], device_id=sib, device_id_type=MESH)
                cp.start()
                sends.append(cp)
        for cp in sends:
            cp.wait_recv()
        for cp in sends:
            cp.wait_send()

    return _scatter_call(body, name, gs, [jax.ShapeDtypeStruct((4,) + g.shape[1:], g.dtype) for g in gs], 4)


def _scatter_ici_first(hs, name):
    n = len(hs)

    def body(*refs):
        h_refs, recv_refs, send_sems, recv_sems = refs[:n], refs[n:2 * n], refs[2 * n], refs[2 * n + 1]
        x, y, c, me = _position()
        xn, yn = (1 - x, y, c), (x, 1 - y, c)

        def cp(a, k, q, to):
            return pltpu.make_async_remote_copy(
                src_ref=h_refs[a].at[q], dst_ref=recv_refs[a].at[k], send_sem=send_sems.at[a, k],
                recv_sem=recv_sems.at[a, k], device_id=to, device_id_type=MESH)

        @pl.when(c == 0)
        def _():
            for a in range(n):
                cp(a, 0, 2, xn).start()
                cp(a, 1, 3, xn).start()

        @pl.when(c == 1)
        def _():
            for a in range(n):
                cp(a, 0, 1, yn).start()
                cp(a, 1, 3, yn).start()

        for a in range(n):
            for k in range(2):
                cp(a, k, 0, xn).wait_recv()
        for a in range(n):
            for k in range(2):
                cp(a, k, 0, xn).wait_send()

    return _scatter_call(body, name, hs, [jax.ShapeDtypeStruct((2,) + h.shape[1:], h.dtype) for h in hs], 2)


def _scatter_ici_second(k1s, name):
    n = len(k1s)

    def body(*refs):
        k_refs, recv_refs, send_sems, recv_sems = refs[:n], refs[n:2 * n], refs[2 * n], refs[2 * n + 1]
        x, y, c, me = _position()
        xn, yn = (1 - x, y, c), (x, 1 - y, c)

        def cp(a, to):
            return pltpu.make_async_remote_copy(src_ref=k_refs[a], dst_ref=recv_refs[a], send_sem=send_sems.at[a, 0],
                                                recv_sem=recv_sems.at[a, 0], device_id=to, device_id_type=MESH)

        @pl.when(c == 0)
        def _():
            for a in range(n):
                cp(a, yn).start()

        @pl.when(c == 1)
        def _():
            for a in range(n):
                cp(a, xn).start()

        for a in range(n):
            cp(a, xn).wait_recv()
        for a in range(n):
            cp(a, xn).wait_send()

    return _scatter_call(body, name, k1s, [jax.ShapeDtypeStruct(k.shape, k.dtype) for k in k1s], 1)


def _add_blocks(a, a_idx, b, b_idx, out_dtype, name):
    rows, cols = a.shape[1:]
    n = a_idx.shape[0]
    tm = _pick(rows, (512, 256, 160, 128, 32, 16))

    def body(ia_ref, ib_ref, a_ref, b_ref, o_ref):
        o_ref[...] = (a_ref[...].astype(f32) + b_ref[...].astype(f32)).astype(o_ref.dtype)

    grid_spec = pltpu.PrefetchScalarGridSpec(
        num_scalar_prefetch=2, grid=(n, rows // tm),
        in_specs=[pl.BlockSpec((None, tm, cols), lambda j, i, ia, ib: (ia[j], i, 0)),
                  pl.BlockSpec((None, tm, cols), lambda j, i, ia, ib: (ib[j], i, 0))],
        out_specs=pl.BlockSpec((None, tm, cols), lambda j, i, ia, ib: (j, i, 0)))
    return pl.pallas_call(body, name=name, out_shape=jax.ShapeDtypeStruct((n, rows, cols), out_dtype),
                          grid_spec=grid_spec, compiler_params=_cparams("parallel", "parallel"))(a_idx, b_idx, a, b)


def _reduce_scatter(gs, names, tag):
    x, y, c, me = _position()
    i32 = lambda *v: jnp.stack([jnp.asarray(t, jnp.int32) for t in v])
    recvs = _scatter_d2d(gs, tag + "_d2d")
    own_idx = i32(*[_lin(_plane_pos(x, y, q) + (c,)) for q in range(4)])
    hs = [_add_blocks(g, own_idx, r, i32(0, 1, 2, 3), bf16, f"{tag}_add_chip_{nm}")
          for g, r, nm in zip(gs, recvs, names)]
    recv2s = _scatter_ici_first(hs, tag + "_ici_first")
    k1s = [_add_blocks(h, i32(1 + c), r2, i32(1), bf16, f"{tag}_add_onward_{nm}")[0]
           for h, r2, nm in zip(hs, recv2s, names)]
    lasts = _scatter_ici_second(k1s, tag + "_ici_second")
    return [[(h, 1), (r2, 1), (last[None], 1)] for h, r2, last in zip(hs, recv2s, lasts)]


ANY_SPEC = pl.BlockSpec(memory_space=pl.ANY)


def _mm(a, b, name, ta=False, tb=False, out_dtype=f32, after=()):
    if ta:
        k_dim, m_dim = a.shape
    else:
        m_dim, k_dim = a.shape
    if tb:
        n_dim, k2 = b.shape
    else:
        k2, n_dim = b.shape
    assert k_dim == k2, (a.shape, b.shape)
    assert a.dtype == bf16 and b.dtype == bf16
    bm = _pick(m_dim, (512, 768, 640, 256, 128))
    bn = _pick(n_dim, (512, 640, 256, 128))
    bk = _pick(k_dim, (1024, 1280, 768, 512))
    nk = k_dim // bk
    a_spec = (pl.BlockSpec((bk, bm), lambda i, j, k: (k, i)) if ta
              else pl.BlockSpec((bm, bk), lambda i, j, k: (i, k)))
    b_spec = (pl.BlockSpec((bn, bk), lambda i, j, k: (j, k)) if tb
              else pl.BlockSpec((bk, bn), lambda i, j, k: (k, j)))
    dims = (((0 if ta else 1,), (1 if tb else 0,)), ((), ()))

    n_after = len(after)

    def body_single(a_ref, b_ref, *rest):
        o_ref = rest[n_after]
        o_ref[...] = lax.dot_general(a_ref[...], b_ref[...], dims, preferred_element_type=f32).astype(o_ref.dtype)

    def body(a_ref, b_ref, *rest):
        o_ref, acc_ref = rest[n_after:]
        k = pl.program_id(2)

        @pl.when(k == 0)
        def _():
            acc_ref[...] = jnp.zeros_like(acc_ref)

        acc_ref[...] += lax.dot_general(a_ref[...], b_ref[...], dims, preferred_element_type=f32)

        @pl.when(k == nk - 1)
        def _():
            o_ref[...] = acc_ref[...].astype(o_ref.dtype)

    return pl.pallas_call(
        body_single if nk == 1 else body, name=name, out_shape=jax.ShapeDtypeStruct((m_dim, n_dim), out_dtype),
        grid=(m_dim // bm, n_dim // bn, nk), in_specs=[a_spec, b_spec] + [ANY_SPEC] * n_after,
        out_specs=pl.BlockSpec((bm, bn), lambda i, j, k: (i, j)),
        scratch_shapes=[] if nk == 1 else [pltpu.VMEM((bm, bn), f32)],
        compiler_params=_cparams("parallel", "parallel", "arbitrary"),
    )(a, b, *after)


def _rin(arr, width=None, cb=0, roff=0):
    return (arr, arr.shape[1] if width is None else width, cb, roff)


def _rowcall(fn, name, rows, tm, row_ins, par_ins, row_outs, acc_outs=(), after=()):
    nr, npar, nro, n_after = len(row_ins), len(par_ins), len(row_outs), len(after)
    in_specs, args = [], []
    for arr, width, cb, roff in row_ins:
        if roff >= 0:
            imap = lambda i, cb=cb, roff=roff: (i + roff, cb)
        else:
            imap = lambda i, cb=cb, roff=roff: (jnp.maximum(i + roff, 0), cb)
        in_specs.append(pl.BlockSpec((tm, width), imap))
        args.append(arr)
    for p in par_ins:
        in_specs.append(pl.BlockSpec(p.shape, lambda i: (0, 0)))
        args.append(p)
    out_shape, out_specs = [], []
    for width, dt in row_outs:
        out_shape.append(jax.ShapeDtypeStruct((rows, width), dt))
        out_specs.append(pl.BlockSpec((tm, width), lambda i: (i, 0)))
    for p, width in acc_outs:
        out_shape.append(jax.ShapeDtypeStruct((p, width), f32))
        out_specs.append(pl.BlockSpec((p, width), lambda i: (0, 0)))

    def body(*refs):
        i = pl.program_id(0)
        res = fn(i, *[r[...] for r in refs[:nr + npar]])
        outs = refs[nr + npar + n_after:]
        for o, v in zip(outs[:nro], res[:nro]):
            o[...] = v.astype(o.dtype)
        if acc_outs:
            @pl.when(i == 0)
            def _():
                for o in outs[nro:]:
                    o[...] = jnp.zeros_like(o)

            for o, v in zip(outs[nro:], res[nro:]):
                o[...] += v

    return pl.pallas_call(
        body, name=name, out_shape=out_shape, grid=(rows // tm,), in_specs=in_specs + [ANY_SPEC] * n_after,
        out_specs=out_specs, compiler_params=_cparams("arbitrary"),
    )(*args, *after)


def _rms(x, g):
    return x * lax.rsqrt(jnp.mean(x * x, axis=-1, keepdims=True) + EPS) * g


def _normmod(x, g, sc, sh):
    return _rms(x, g) * (1.0 + sc) + sh


def _gelu(x):
    return 0.5 * x * (1.0 + jnp.tanh(0.7978845608028654 * (x + 0.044715 * (x * x * x))))


def _sigmoid(x):
    return 0.5 * (jnp.tanh(0.5 * x) + 1.0)


def _coeff_parts(pre_a, pre_x, ba, bx, lam):
    r = _sigmoid(pre_a + ba)
    ig = _sigmoid(pre_x + bx)
    nl = -lam
    sp = jnp.maximum(nl, 0.0) + jnp.log(1.0 + jnp.exp(-jnp.abs(nl)))
    la = -RG_C * r * sp
    a = jnp.exp(la)
    one_minus_a2 = -jnp.tanh(la) * (a * a + 1.0)
    inv_m = lax.rsqrt(one_minus_a2)
    return r, ig, sp, a, one_minus_a2 * inv_m, inv_m


def _coeff(pre_a, pre_x, u, ba, bx, lam):
    _, ig, _, a, m, _ = _coeff_parts(pre_a, pre_x, ba, bx, lam)
    return a, m * (ig * u)


def _coeff_bwd(pre_a, pre_x, u, ba, bx, lam, da, db):
    r, ig, sp, a, m, inv_m = _coeff_parts(pre_a, pre_x, ba, bx, lam)
    dbu = db * u
    dig = dbu * m
    dm = dbu * ig
    dla = a * (da - dm * a * inv_m)
    dpa = dla * (-RG_C * sp) * (r * (1.0 - r))
    dpx = dig * (ig * (1.0 - ig))
    dsp = jnp.sum(dla * (-RG_C * r), axis=0, keepdims=True)
    dlam = -dsp * _sigmoid(-lam)
    return (dpa, dpx, db * m * ig, jnp.sum(dpa, axis=0, keepdims=True), jnp.sum(dpx, axis=0, keepdims=True), dlam)


SCAN_CHUNK = 256


def _scan_call(a, v, chunk_of, reverse, name, backward):
    rows, width = a.shape
    n_out = 1 if backward else 2
    nt = SCAN_CHUNK // 8

    def body(a_ref, v_ref, *rest):
        outs, state_ref = rest[:-1], rest[-1]

        @pl.when(pl.program_id(0) == 0)
        def _():
            state_ref[...] = jnp.zeros_like(state_ref)

        rid = lax.broadcasted_iota(jnp.int32, (8, width), 0)

        def tile(j, st):
            t0 = pl.multiple_of((nt - 1 - j if reverse else j) * 8, 8)
            at = a_ref[pl.ds(t0, 8), :]
            vt = v_ref[pl.ds(t0, 8), :]
            out = jnp.zeros((8, width), f32)
            prev = jnp.zeros((8, width), f32)
            for i in (range(7, -1, -1) if reverse else range(8)):
                if backward:
                    g = vt[i:i + 1] + st
                    st = at[i:i + 1] * g
                    out = jnp.where(rid == i, g, out)
                else:
                    prev = jnp.where(rid == i, st, prev)
                    st = at[i:i + 1] * st + vt[i:i + 1]
                    out = jnp.where(rid == i, st, out)
            outs[0][pl.ds(t0, 8), :] = out
            if not backward:
                outs[1][pl.ds(t0, 8), :] = prev
            return st

        state_ref[0:1, :] = lax.fori_loop(0, nt, tile, state_ref[0:1, :])

    spec = pl.BlockSpec((SCAN_CHUNK, width), lambda t: (chunk_of(t), 0))
    return pl.pallas_call(
        body, name=name, out_shape=[jax.ShapeDtypeStruct((rows, width), f32)] * n_out,
        grid=(rows // SCAN_CHUNK,), in_specs=[spec, spec], out_specs=[spec] * n_out,
        scratch_shapes=[pltpu.VMEM((8, width), f32)],
        compiler_params=_cparams("arbitrary"),
    )(a, v)


CONV_CHUNK = 256


def _fill_padded(pad_ref, src_ref, start, n):
    cb = pad_ref.shape[1]
    pad_ref[pl.ds(0, HALO), :] = jnp.zeros((HALO, cb), f32)
    pad_ref[pl.ds(HALO, n), :] = src_ref[pl.ds(start, n), :].astype(f32)
    pad_ref[pl.ds(HALO + n, HALO), :] = jnp.zeros((HALO, cb), f32)


def _dwconv_fwd(x, x_cb0, w, b, taps, pad_left, segments, cb, name, emit_bf16):
    rows = x.shape[0]
    width = w.shape[1]

    def body(x_ref, w_ref, b_ref, *rest):
        outs, xp = rest[:-1], rest[-1]
        for start, n in segments:
            _fill_padded(xp, x_ref, start, n)
            for c0 in range(0, n, CONV_CHUNK):
                acc = jnp.zeros((CONV_CHUNK, cb), f32) + b_ref[...]
                for k in range(taps):
                    acc = acc + w_ref[k:k + 1, :] * xp[pl.ds(HALO + c0 + k - pad_left, CONV_CHUNK), :]
                for o in outs:
                    o[pl.ds(start + c0, CONV_CHUNK), :] = acc.astype(o.dtype)

    out_dtypes = [f32, bf16] if emit_bf16 else [f32]
    return pl.pallas_call(
        body, name=name, out_shape=[jax.ShapeDtypeStruct((rows, width), dt) for dt in out_dtypes],
        grid=(width // cb,),
        in_specs=[pl.BlockSpec((rows, cb), lambda j: (0, j + x_cb0)), pl.BlockSpec((taps, cb), lambda j: (0, j)),
                  pl.BlockSpec((1, cb), lambda j: (0, j))],
        out_specs=[pl.BlockSpec((rows, cb), lambda j: (0, j))] * len(out_dtypes),
        scratch_shapes=[pltpu.VMEM((rows + 2 * HALO, cb), f32)],
        compiler_params=_cparams("parallel"),
    )(x, w, b)


def _dwconv_bwd(douts, x, x_cb0, w, taps, pad_left, segments, cb, name, dx_dtype):
    rows = x.shape[0]
    width = w.shape[1]
    nd = len(douts)

    def body(*refs):
        d_refs, x_ref, w_ref = refs[:nd], refs[nd], refs[nd + 1]
        dx_ref, dw_ref, db_ref, xp, dp, dsum = refs[nd + 2:]
        dw_ref[...] = jnp.zeros_like(dw_ref)
        db_ref[...] = jnp.zeros_like(db_ref)
        if nd > 1:
            total = d_refs[0][...]
            for r in d_refs[1:]:
                total = total + r[...]
            dsum[...] = total
            d_ref = dsum
        else:
            d_ref = d_refs[0]
        for start, n in segments:
            _fill_padded(xp, x_ref, start, n)
            _fill_padded(dp, d_ref, start, n)
            for c0 in range(0, n, CONV_CHUNK):
                dchunk = dp[pl.ds(HALO + c0, CONV_CHUNK), :]
                db_ref[...] += jnp.sum(dchunk, axis=0, keepdims=True)
                acc = jnp.zeros((CONV_CHUNK, cb), f32)
                for k in range(taps):
                    acc = acc + w_ref[k:k + 1, :] * dp[pl.ds(HALO + c0 + pad_left - k, CONV_CHUNK), :]
                    xs = xp[pl.ds(HALO + c0 + k - pad_left, CONV_CHUNK), :]
                    dw_ref[k:k + 1, :] += jnp.sum(dchunk * xs, axis=0, keepdims=True)
                dx_ref[pl.ds(start + c0, CONV_CHUNK), :] = acc.astype(dx_ref.dtype)

    dspec = pl.BlockSpec((rows, cb), lambda j: (0, j))
    return pl.pallas_call(
        body, name=name,
        out_shape=[jax.ShapeDtypeStruct((rows, width), dx_dtype), jax.ShapeDtypeStruct((taps, width), f32),
                   jax.ShapeDtypeStruct((1, width), f32)],
        grid=(width // cb,),
        in_specs=[dspec] * nd + [pl.BlockSpec((rows, cb), lambda j: (0, j + x_cb0)),
                                 pl.BlockSpec((taps, cb), lambda j: (0, j))],
        out_specs=[dspec, pl.BlockSpec((taps, cb), lambda j: (0, j)), pl.BlockSpec((1, cb), lambda j: (0, j))],
        scratch_shapes=[pltpu.VMEM((rows + 2 * HALO, cb), f32), pltpu.VMEM((rows + 2 * HALO, cb), f32),
                        pltpu.VMEM((rows, cb), f32)],
        compiler_params=_cparams("parallel"),
    )(*douts, x, w)


def _ada_forward(c16, w_ada, b_loc):
    def body(c_ref, w_ref, b_ref, o_ref):
        cv = c_ref[...]
        s = (cv * _sigmoid(cv)).astype(bf16)
        o_ref[0] = jnp.dot(s, w_ref[0].astype(bf16), preferred_element_type=f32) + b_ref[0]

    return pl.pallas_call(
        body, name="ada_forward", out_shape=jax.ShapeDtypeStruct((2, 16, ADA_SHARD), f32), grid=(2,),
        in_specs=[pl.BlockSpec((16, D), lambda l: (0, 0)), pl.BlockSpec((1, D, ADA_SHARD), lambda l: (l, 0, 0)),
                  pl.BlockSpec((1, 1, ADA_SHARD), lambda l: (l, 0, 0))],
        out_specs=pl.BlockSpec((1, 16, ADA_SHARD), lambda l: (l, 0, 0)),
        compiler_params=_cparams("parallel"),
    )(c16, w_ada, b_loc)


def _ada_backward(c16, g16, w_ada):
    def body(c_ref, g_ref, w_ref, dw_ref, ds_ref):
        cv = c_ref[...]
        s = (cv * _sigmoid(cv)).astype(bf16)
        g = g_ref[0].astype(bf16)
        dw_ref[0] = lax.dot_general(s, g, (((0,), (0,)), ((), ())), preferred_element_type=f32)
        ds = lax.dot_general(g, w_ref[0].astype(bf16), (((1,), (1,)), ((), ())), preferred_element_type=f32)
        cc = cv[8:9]
        sg = _sigmoid(cc)
        dsilu = sg * (1.0 + cc * (1.0 - sg))
        ds_ref[0] = jnp.zeros((8, D), f32) + jnp.sum(ds[8:16], axis=0, keepdims=True) * dsilu

    return pl.pallas_call(
        body, name="ada_backward",
        out_shape=[jax.ShapeDtypeStruct((2, D, ADA_SHARD), f32), jax.ShapeDtypeStruct((2, 8, D), f32)], grid=(2,),
        in_specs=[pl.BlockSpec((16, D), lambda l: (0, 0)), pl.BlockSpec((1, 16, ADA_SHARD), lambda l: (l, 0, 0)),
                  pl.BlockSpec((1, D, ADA_SHARD), lambda l: (l, 0, 0))],
        out_specs=[pl.BlockSpec((1, D, ADA_SHARD), lambda l: (l, 0, 0)), pl.BlockSpec((1, 8, D), lambda l: (l, 0, 0))],
        compiler_params=_cparams("parallel"),
    )(c16, g16, w_ada)


def _adamw(pieces, w, m, v, name):
    rows, cols = w.shape
    n_arr = len(pieces)
    counts = [cnt for _, cnt in pieces]
    pieces = [p for p, _ in pieces]
    tm = 256 if (rows % 256 == 0 and rows > 256) else rows

    def body(*refs):
        p_refs = refs[:n_arr]
        w_ref, m_ref, v_ref, g_ref, d_ref, nm_ref, nv_ref = refs[n_arr:]
        g = None
        for p_ref in p_refs:
            for j in range(p_ref.shape[0]):
                term = p_ref[j].astype(f32)
                g = term if g is None else g + term
        m2 = ADAM_B1 * m_ref[...] + (1.0 - ADAM_B1) * g
        v2 = ADAM_B2 * v_ref[...] + (1.0 - ADAM_B2) * (g * g)
        m_hat = m2 / (1.0 - ADAM_B1 ** ADAM_STEP)
        v_hat = v2 / (1.0 - ADAM_B2 ** ADAM_STEP)
        g_ref[...] = g
        d_ref[...] = -ADAM_LR * (m_hat / (jnp.sqrt(v_hat) + ADAM_EPS) + ADAM_WD * w_ref[...])
        nm_ref[...] = m2
        nv_ref[...] = v2

    spec = pl.BlockSpec((tm, cols), lambda i: (i, 0))
    return pl.pallas_call(
        body, name=name, out_shape=[jax.ShapeDtypeStruct((rows, cols), f32)] * 4, grid=(rows // tm,),
        in_specs=[pl.BlockSpec((cnt, tm, cols), lambda i: (0, i, 0)) for cnt in counts] + [spec, spec, spec],
        out_specs=[spec] * 4, compiler_params=_cparams("parallel"),
    )(*pieces, w, m, v)


MLP_TM = 256
FB = F // N_DEV


def _stack_rows(vals, n):
    cols = vals[0].shape[1]
    rid = lax.broadcasted_iota(jnp.int32, (n, cols), 0)
    out = jnp.zeros((n, cols), f32)
    for k, v in enumerate(vals):
        out = jnp.where(rid == k, v, out)
    return out


N_MLP_PARAMS = 9


class _ParamRows:
    def __init__(self, ref):
        self.ref = ref

    def __getitem__(self, sl):
        return self.ref[8 * sl.start:8 * sl.start + 1, :]


def _resident(shape, imap):
    return pl.BlockSpec(shape, imap, pipeline_mode=pl.Buffered(1))


def _mlp_forward(xa, xa_roff, out_prev, par, w_in, w_out, layer, name):
    def body(xa_ref, op_ref, par_ref, win_ref, wout_ref, x1_ref, h_ref, r_ref, mo_ref, x2_ref, hn_ref):
        p = _ParamRows(par_ref)
        x1 = xa_ref[...] + p[0:1] * (op_ref[...] + p[1:2])
        h = _normmod(x1, p[2:3], p[3:4], p[4:5]).astype(bf16)
        x1_ref[...] = x1
        h_ref[...] = h
        mo = jnp.zeros((MLP_TM, D), f32)
        for j in range(N_DEV):
            r = jnp.maximum(jnp.dot(h, win_ref[j], preferred_element_type=f32), 0.0)
            r_ref[:, j * FB:(j + 1) * FB] = r.astype(bf16)
            mo = mo + jnp.dot((r * r).astype(bf16), wout_ref[j], preferred_element_type=f32)
        mo_ref[...] = mo.astype(bf16)
        x2 = x1 + p[5:6] * mo
        x2_ref[...] = x2
        hn_ref[...] = _normmod(x2, p[6:7], p[7:8], p[8:9]).astype(bf16)

    row = lambda width: pl.BlockSpec((MLP_TM, width), lambda i: (i, 0))
    return pl.pallas_call(
        body, name=name, grid=(T_LAT // MLP_TM,),
        out_shape=[jax.ShapeDtypeStruct((T_LAT, D), f32), jax.ShapeDtypeStruct((T_LAT, D), bf16),
                   jax.ShapeDtypeStruct((T_LAT, F), bf16), jax.ShapeDtypeStruct((T_LAT, D), bf16),
                   jax.ShapeDtypeStruct((T_LAT, D), f32), jax.ShapeDtypeStruct((T_LAT, D), bf16)],
        in_specs=[pl.BlockSpec((MLP_TM, D), lambda i: (i + xa_roff, 0)), row(D), pl.BlockSpec((8 * N_MLP_PARAMS, D), lambda i: (0, 0)),
                  _resident((N_DEV, None, D, FB), lambda i: (0, layer, 0, 0)),
                  _resident((N_DEV, None, FB, D), lambda i: (0, layer, 0, 0))],
        out_specs=[row(D), row(D), row(F), row(D), row(D), row(D)],
        compiler_params=_cparams("parallel"),
    )(xa, out_prev, par, w_in, w_out)


def _mlp_backward(dx2, x1, r, mo, out_prev, par, w_in, w_out, layer, name, after=()):
    nt = (((1,), (1,)), ((), ()))

    n_after = len(after)

    def body(dx2_ref, x1_ref, r_ref, mo_ref, op_ref, par_ref, win_ref, wout_ref, *rest):
        dx1_ref, dop_ref, dmo_ref, dhid_ref, acc_ref = rest[n_after:]
        p = _ParamRows(par_ref)
        dx2v = dx2_ref[...]
        dmo = (p[5:6] * dx2v).astype(bf16)
        dmo_ref[...] = dmo
        dh = jnp.zeros((MLP_TM, D), f32)
        mo = mo_ref[...].astype(f32)
        for j in range(N_DEV):
            rf = r_ref[:, j * FB:(j + 1) * FB].astype(f32)
            dact = lax.dot_general(dmo, wout_ref[j], nt, preferred_element_type=f32)
            dhid = (dact * (2.0 * rf)).astype(bf16)
            dhid_ref[:, j * FB:(j + 1) * FB] = dhid
            dh = dh + lax.dot_general(dhid, win_ref[j], nt, preferred_element_type=f32)
        x1 = x1_ref[...]
        _, vjp = jax.vjp(_normmod, x1, p[2:3], p[3:4], p[4:5])
        dx, dng, dsc, dsh = vjp(dh)
        dx1 = dx2v + dx
        dx1_ref[...] = dx1
        dop_ref[...] = (p[0:1] * dx1).astype(bf16)
        sums = _stack_rows([jnp.sum(dx1 * (op_ref[...] + p[1:2]), axis=0, keepdims=True),
                            p[0:1] * jnp.sum(dx1, axis=0, keepdims=True), dng, dsc, dsh,
                            jnp.sum(dx2v * mo, axis=0, keepdims=True)], 8)

        @pl.when(pl.program_id(0) == 0)
        def _():
            acc_ref[...] = jnp.zeros_like(acc_ref)

        acc_ref[...] += sums

    row = lambda width: pl.BlockSpec((MLP_TM, width), lambda i: (i, 0))
    return pl.pallas_call(
        body, name=name, grid=(T_LAT // MLP_TM,),
        out_shape=[jax.ShapeDtypeStruct((T_LAT, D), f32), jax.ShapeDtypeStruct((T_LAT, D), bf16),
                   jax.ShapeDtypeStruct((T_LAT, D), bf16), jax.ShapeDtypeStruct((T_LAT, F), bf16),
                   jax.ShapeDtypeStruct((8, D), f32)],
        in_specs=[row(D), row(D), row(F), row(D), row(D), pl.BlockSpec((8 * N_MLP_PARAMS, D), lambda i: (0, 0)),
                  _resident((N_DEV, None, D, FB), lambda i: (0, layer, 0, 0)),
                  _resident((N_DEV, None, FB, D), lambda i: (0, layer, 0, 0))] + [ANY_SPEC] * n_after,
        out_specs=[row(D), row(D), row(D), row(F), pl.BlockSpec((8, D), lambda i: (0, 0))],
        compiler_params=_cparams("arbitrary"),
    )(dx2, x1, r, mo, out_prev, par, w_in, w_out, *after)


def _mlp_weight_grads(h, dhid, r, dmo, layer, other, tag):
    tn = (((0,), (0,)), ((), ()))

    def body_in(h_ref, dhid_ref, *rest):
        rest[-1][...] = lax.dot_general(h_ref[...], dhid_ref[...], tn, preferred_element_type=f32).astype(bf16)

    def body_out(r_ref, dmo_ref, *rest):
        rf = r_ref[...].astype(f32)
        rest[-1][...] = lax.dot_general((rf * rf).astype(bf16), dmo_ref[...], tn,
                                        preferred_element_type=f32).astype(bf16)

    def call(body, name, operands, specs, block, prev):
        extra = [] if prev is None else [prev]
        return pl.pallas_call(
            body, name=name, grid=(N_DEV,), out_shape=jax.ShapeDtypeStruct((N_DEV, 2) + block, bf16),
            in_specs=specs + [pl.BlockSpec(memory_space=pl.ANY)] * len(extra),
            out_specs=pl.BlockSpec((None, None) + block, lambda j: (j, layer, 0, 0)),
            input_output_aliases={} if prev is None else {2: 0},
            compiler_params=_cparams("parallel"),
        )(*operands, *extra)

    dw_in = call(body_in, tag + "_mlp_in_dw", [h, dhid],
                 [_resident((T_LAT, D), lambda j: (0, 0)), pl.BlockSpec((T_LAT, FB), lambda j: (0, j))], (D, FB),
                 None if other is None else other[0])
    dw_out = call(body_out, tag + "_mlp_out_dw", [r, dmo],
                  [pl.BlockSpec((T_LAT, FB), lambda j: (0, j)), _resident((T_LAT, D), lambda j: (0, 0))], (FB, D),
                  None if other is None else other[1])
    return dw_in, dw_out


def _pos_embed():
    n_rows = T_LAT // GRID_W
    q = D // 4
    omega = 1.0 / (POS_BASE ** (jnp.arange(q, dtype=f32) / q))
    er = jnp.arange(n_rows, dtype=jnp.int32).astype(f32)[:, None] * omega[None, :]
    ec = jnp.arange(GRID_W, dtype=jnp.int32).astype(f32)[:, None] * omega[None, :]
    by_row = jnp.concatenate([jnp.sin(er), jnp.cos(er)], axis=-1)[:, None, :]
    by_col = jnp.concatenate([jnp.sin(ec), jnp.cos(ec)], axis=-1)[None, :, :]
    full = jnp.concatenate([jnp.broadcast_to(by_row, (n_rows, GRID_W, D // 2)),
                            jnp.broadcast_to(by_col, (n_rows, GRID_W, D // 2))], axis=-1)
    return full.reshape(T_LAT, D)


HALF = R // 2
BLK_PER_HALF = N_BLK // 2
N_PARTS = 4


def _gate_matrix(w_a, w_x):
    eye = jnp.eye(BLK_PER_HALF, dtype=bf16)
    cols = []
    for h in range(2):
        for d in range(2):
            for w in (w_a, w_x):
                blocks = w[d, BLK_PER_HALF * h:BLK_PER_HALF * (h + 1)].astype(bf16)
                cols.append(jnp.einsum("hij,hg->higj", blocks, eye).reshape(HALF, HALF))
    return jnp.concatenate(cols, axis=1)


def _gate_blocks(dwg, part):
    out = []
    for h in range(2):
        blk = dwg[:, (N_PARTS * h + part) * HALF:(N_PARTS * h + part + 1) * HALF]
        blk = blk.reshape(BLK_PER_HALF, BLK, BLK_PER_HALF, BLK)
        out.append(jnp.moveaxis(jnp.diagonal(blk, axis1=0, axis2=2), -1, 0))
    return jnp.concatenate(out, axis=0)


GATE_BM = 768


def _gates_dx(dpre, wg):
    rows = dpre.shape[0]

    def body(d_ref, w_ref, o_ref, acc_ref):
        p = pl.program_id(2)

        @pl.when(p == 0)
        def _():
            acc_ref[...] = jnp.zeros_like(acc_ref)

        acc_ref[...] += lax.dot_general(d_ref[...], w_ref[...], (((1,), (1,)), ((), ())), preferred_element_type=f32)

        @pl.when(p == N_PARTS - 1)
        def _():
            o_ref[...] = acc_ref[...]

    return pl.pallas_call(
        body, name="l0_gates_dx", grid=(rows // GATE_BM, 2, N_PARTS), out_shape=jax.ShapeDtypeStruct((rows, R), f32),
        in_specs=[pl.BlockSpec((GATE_BM, HALF), lambda i, h, p: (i, N_PARTS * h + p)),
                  pl.BlockSpec((HALF, HALF), lambda i, h, p: (0, N_PARTS * h + p))],
        out_specs=pl.BlockSpec((GATE_BM, HALF), lambda i, h, p: (i, h)),
        scratch_shapes=[pltpu.VMEM((GATE_BM, HALF), f32)],
        compiler_params=_cparams("parallel", "parallel", "arbitrary"),
    )(dpre, wg)


COEFF_TM = 256


def _dir_params(d, *params):
    specs = [pl.BlockSpec((None, 1, HALF), lambda h, i: (d, 0, h))] * len(params)
    return specs, [p.reshape(2, 1, R) for p in params]


def _gates_coeff_fwd(ub, u, wg, ba, bx, lam, d):
    rows = u.shape[0]

    def body(ub_ref, u_ref, w_ref, ba_ref, bx_ref, lam_ref, a_ref, b_ref):
        pre = jnp.dot(ub_ref[...], w_ref[...], preferred_element_type=f32)
        a, b = _coeff(pre[:, :HALF], pre[:, HALF:], u_ref[...], ba_ref[...], bx_ref[...], lam_ref[...])
        a_ref[...] = a
        b_ref[...] = b

    tile = pl.BlockSpec((COEFF_TM, HALF), lambda h, i: (i, h))
    pspecs, pargs = _dir_params(d, ba, bx, lam)
    return pl.pallas_call(
        body, name=f"l0_gates_coeff_{d}", grid=(2, rows // COEFF_TM),
        out_shape=[jax.ShapeDtypeStruct((rows, R), f32)] * 2,
        in_specs=[tile, tile, pl.BlockSpec((HALF, 2 * HALF), lambda h, i: (0, 2 * h + d))] + pspecs,
        out_specs=[tile, tile], compiler_params=_cparams("parallel", "parallel"),
    )(ub, u, wg, *pargs)


def _gates_coeff_bwd(ub, u, dh, yp, wg, ba, bx, lam, d, dpre_prev):
    rows = u.shape[0]
    n_prev = 0 if dpre_prev is None else 1

    def body(ub_ref, u_ref, dh_ref, yp_ref, w_ref, ba_ref, bx_ref, lam_ref, *rest):
        dpre_ref, du_ref, dba_ref, dbx_ref, dlam_ref = rest[n_prev:]
        pre = jnp.dot(ub_ref[...], w_ref[...], preferred_element_type=f32)
        dhv = dh_ref[...]
        dpa, dpx, du, dba, dbx, dlam = _coeff_bwd(pre[:, :HALF], pre[:, HALF:], u_ref[...], ba_ref[...], bx_ref[...],
                                                  lam_ref[...], dhv * yp_ref[...], dhv)
        dpre_ref[:, :HALF] = dpa.astype(bf16)
        dpre_ref[:, HALF:] = dpx.astype(bf16)
        du_ref[...] = du

        @pl.when(pl.program_id(1) == 0)
        def _():
            dba_ref[...] = jnp.zeros_like(dba_ref)
            dbx_ref[...] = jnp.zeros_like(dbx_ref)
            dlam_ref[...] = jnp.zeros_like(dlam_ref)

        dba_ref[...] += dba
        dbx_ref[...] += dbx
        dlam_ref[...] += dlam

    tile = pl.BlockSpec((COEFF_TM, HALF), lambda h, i: (i, h))
    acc = pl.BlockSpec((1, HALF), lambda h, i: (0, h))
    pspecs, pargs = _dir_params(d, ba, bx, lam)
    extra = [] if dpre_prev is None else [dpre_prev]
    return pl.pallas_call(
        body, name=f"l0_gates_coeff_bwd_{d}", grid=(2, rows // COEFF_TM),
        out_shape=[jax.ShapeDtypeStruct((rows, 2 * N_PARTS * HALF), bf16), jax.ShapeDtypeStruct((rows, R), f32)]
        + [jax.ShapeDtypeStruct((1, R), f32)] * 3,
        in_specs=[tile] * 4 + [pl.BlockSpec((HALF, 2 * HALF), lambda h, i: (0, 2 * h + d))] + pspecs
        + [ANY_SPEC] * n_prev,
        out_specs=[pl.BlockSpec((COEFF_TM, 2 * HALF), lambda h, i: (i, 2 * h + d)), tile, acc, acc, acc],
        input_output_aliases={8: 0} if n_prev else {}, compiler_params=_cparams("parallel", "arbitrary"),
    )(ub, u, dh, yp, wg, *pargs, *extra)


def _gates_dw(u, dpre):
    rows = u.shape[0]

    def body(u_ref, d_ref, o_ref):
        o_ref[...] = lax.dot_general(u_ref[...], d_ref[...], (((0,), (0,)), ((), ())), preferred_element_type=f32)

    return pl.pallas_call(
        body, name="l0_gates_dw", grid=(2 * N_PARTS,), out_shape=jax.ShapeDtypeStruct((HALF, 2 * N_PARTS * HALF), f32),
        in_specs=[pl.BlockSpec((rows, HALF), lambda j: (0, j // N_PARTS)), pl.BlockSpec((rows, HALF), lambda j: (0, j))],
        out_specs=pl.BlockSpec((HALF, HALF), lambda j: (0, j)), compiler_params=_cparams("parallel"),
    )(u, dpre)


N_SCAN_CHUNKS = T_ALL // SCAN_CHUNK
SCAN_FWD = lambda t: t
SCAN_FWD_BWD = lambda t: N_SCAN_CHUNKS - 1 - t
SCAN_REV = lambda t: jnp.where(t == 0, 0, N_SCAN_CHUNKS - t)
SCAN_REV_BWD = lambda t: jnp.where(t == N_SCAN_CHUNKS - 1, 0, t + 1)
CONV_SEGMENTS = ((0, T_CTX), (T_CTX, T_LAT))
TM = 128
FUSED_TM = 256
N_CTX_TILES = T_CTX // TM


def _local_step(x, ctx, target, mods, cmod, wts, late_weights, send_grads, start_after=()):
    sh1, sc1, g1, sh2, sc2, g2 = [[mods[l, i][None] for l in range(2)] for i in range(N_MOD)]
    ng = wts["norm_g"]
    xcat = jnp.concatenate([ctx, x], axis=0)
    poscat = jnp.concatenate([jnp.zeros((T_CTX, D), f32), _pos_embed()], axis=0)
    scp = jnp.concatenate([cmod[1][None], sc1[0]], axis=0)
    shp = jnp.concatenate([cmod[0][None], sh1[0]], axis=0)

    def blend(i, p):
        sel = jnp.where(i < N_CTX_TILES, 1.0, 0.0)
        return sel * p[0:1] + (1.0 - sel) * p[1:2]

    def f_pre0(i, xc, pos, g, scp_, shp_):
        x0 = xc + pos
        return x0, _normmod(x0, g, blend(i, scp_), blend(i, shp_))

    x0cat, h0 = _rowcall(f_pre0, "l0_prenorm", T_ALL, TM, [_rin(xcat), _rin(poscat)], [ng[0, 0][None], scp, shp],
                         [(D, f32), (D, bf16)], after=start_after)
    gr = _mm(h0, wts["rec_w_in"], "l0_in_proj")
    u, ub = _dwconv_fwd(gr, R // 256, wts["rec_conv_w"], wts["rec_conv_b"], 4, 1, CONV_SEGMENTS, 256,
                        "l0_conv", True)
    gate_args = (wts["gates"], wts["rec_b_a"], wts["rec_b_x"], wts["rec_lambda"])
    a0, b0 = _gates_coeff_fwd(ub, u, *gate_args, 0)
    a1, b1 = _gates_coeff_fwd(ub, u, *gate_args, 1)
    y0, yp0 = _scan_call(a0, b0, SCAN_FWD, False, "l0_scan_fwd", False)
    y1, yp1 = _scan_call(a1, b1, SCAN_REV, True, "l0_scan_rev", False)

    def f_gate(i, gp, y0_, y1_):
        return (_gelu(gp) * (y0_ + y1_),)

    (zb,) = _rowcall(f_gate, "l0_gate", T_LAT, TM,
                     [_rin(gr, R, 0, N_CTX_TILES), _rin(y0, None, 0, N_CTX_TILES), _rin(y1, None, 0, N_CTX_TILES)],
                     [], [(R, bf16)])
    wts = dict(wts, **late_weights("mlp", zb))
    out0 = _mm(zb, wts["rec_w_out"], "l0_out_proj")

    zero_d = jnp.zeros((1, D), f32)

    def mlp_params(rows):
        rows = rows + [zero_d] * (N_MLP_PARAMS - len(rows))
        return jnp.concatenate([jnp.broadcast_to(r, (8, D)) for r in rows], axis=0)

    par0 = mlp_params([g1[0], zero_d, ng[0, 1][None], sc2[0], sh2[0], g2[0], ng[1, 0][None], sc1[1], sh1[1]])
    x1, h1, r0, mo0, x2, h2 = _mlp_forward(x0cat, T_CTX // MLP_TM, out0, par0, wts["mlp_w_in"], wts["mlp_w_out"], 0,
                                           "l0_mlp")

    wts = dict(wts, **late_weights("conf", x2))
    def glu(pa, pb, b1):
        return (pa + b1[:, :D]) * _sigmoid(pb + b1[:, D:])

    def f_pw1_glu(i, h_, b1, w):
        p = jnp.dot(h_, w, preferred_element_type=f32)
        return glu(p[:, :D], p[:, D:], b1), p

    zg, pw = _rowcall(f_pw1_glu, "l1_pw1_glu", T_LAT, FUSED_TM, [_rin(h2)], [wts["conf_b_pw1"], wts["conf_w_pw1"]],
                      [(D, f32), (2 * D, bf16)])
    (zc,) = _dwconv_fwd(zg, 0, wts["conf_conv_w"], wts["conf_conv_b"], 31, 15, ((0, T_LAT),), 128, "l1_conv", False)

    def ln_silu(z, lg, lb):
        mu = jnp.mean(z, axis=-1, keepdims=True)
        zc_ = z - mu
        var = jnp.mean(zc_ * zc_, axis=-1, keepdims=True)
        yv = zc_ * lax.rsqrt(var + EPS) * lg + lb
        return yv * _sigmoid(yv)

    def f_lnsilu_pw2(i, z, lg, lb, w):
        s = ln_silu(z, lg, lb).astype(bf16)
        return s, jnp.dot(s, w, preferred_element_type=f32)

    sb, out1 = _rowcall(f_lnsilu_pw2, "l1_ln_silu_pw2", T_LAT, FUSED_TM, [_rin(zc)],
                        [wts["conf_ln_g"], wts["conf_ln_b"], wts["conf_w_pw2"]], [(D, bf16), (D, f32)])
    par1 = mlp_params([g1[1], wts["conf_b_pw2"], ng[1, 1][None], sc2[1], sh2[1], g2[1]])
    x3, h3, r1, mo1, x4, _ = _mlp_forward(x2, 0, out1, par1, wts["mlp_w_in"], wts["mlp_w_out"], 1, "l1_mlp")

    def loss_fn(x4_, fg, tgt):
        err = _rms(x4_, fg) - tgt
        per_row = jnp.mean(err * err, axis=-1, keepdims=True)
        return 0.5 * jnp.sum(per_row, axis=0, keepdims=True)

    def f_head(i, x4_, tgt, fg):
        loss, vjp = jax.vjp(lambda a, e: loss_fn(a, e, tgt), x4_, fg)
        dx, dfg = vjp(jnp.ones((1, 1), f32))
        return dx, jnp.broadcast_to(loss, (1, 128)), dfg

    dx4, loss_acc, dfinal_g = _rowcall(f_head, "head", T_LAT, TM, [_rin(x4), _rin(target)], [wts["final_g"]],
                                       [(D, f32)], [(1, 128), (1, D)])

    grads = {"final_g": dfinal_g}

    dx3, dout1, dmo1, dhid1, acc1 = _mlp_backward(dx4, x3, r1, mo1, out1, par1, wts["mlp_w_in"], wts["mlp_w_out"], 1,
                                                  "l1_mlp_bwd")
    mlp_dw = _mlp_weight_grads(h3, dhid1, r1, dmo1, 1, None, "l1")
    dg1_1, db_pw2, dng11, dsc2_1, dsh2_1, dg2_1 = [acc1[k:k + 1] for k in range(6)]

    nt = (((1,), (1,)), ((), ()))
    grads["conf_w_pw2"] = _mm(sb, dout1, "l1_pw2_dw", ta=True, out_dtype=bf16)
    grads["conf_b_pw2"] = db_pw2

    def f_pw2_lnsilu_bwd(i, z, dout, lg, lb, w):
        ds = lax.dot_general(dout, w, nt, preferred_element_type=f32)
        _, vjp = jax.vjp(ln_silu, z, lg, lb)
        return vjp(ds)

    dzc, dln_g, dln_b = _rowcall(f_pw2_lnsilu_bwd, "l1_pw2_ln_silu_bwd", T_LAT, FUSED_TM, [_rin(zc), _rin(dout1)],
                                 [wts["conf_ln_g"], wts["conf_ln_b"], wts["conf_w_pw2"]], [(D, f32)], [(1, D)] * 2)
    grads["conf_ln_g"], grads["conf_ln_b"] = dln_g, dln_b
    dzg, dconv_w, dconv_b = _dwconv_bwd([dzc], zg, 0, wts["conf_conv_w"], 31, 15, ((0, T_LAT),), 128,
                                        "l1_conv_bwd", f32)
    grads["conf_conv_w"], grads["conf_conv_b"] = dconv_w, dconv_b

    def f_glu_pw1_norm_bwd(i, p_, dz, x_, dxs, b1, g_, sc_, sh_, w):
        pf = p_.astype(f32)
        _, vjp = jax.vjp(glu, pf[:, :D], pf[:, D:], b1)
        da, db, db1 = vjp(dz)
        dp = jnp.concatenate([da, db], axis=1).astype(bf16)
        dh = lax.dot_general(dp, w, nt, preferred_element_type=f32)
        _, vjp = jax.vjp(_normmod, x_, g_, sc_, sh_)
        dx, dg, dsc, dsh = vjp(dh)
        return dp, dx + dxs, db1, dg, dsc, dsh

    dpw, dx2, db_pw1, dng10, dsc1_1, dsh1_1 = _rowcall(
        f_glu_pw1_norm_bwd, "l1_glu_pw1_normmod_bwd", T_LAT, FUSED_TM, [_rin(pw), _rin(dzg), _rin(x2), _rin(dx3)],
        [wts["conf_b_pw1"], ng[1, 0][None], sc1[1], sh1[1], wts["conf_w_pw1"]], [(2 * D, bf16), (D, f32)],
        [(1, 2 * D), (1, D), (1, D), (1, D)])
    grads["conf_b_pw1"] = db_pw1
    grads["conf_w_pw1"] = _mm(h2, dpw, "l1_pw1_dw", ta=True, out_dtype=bf16)
    sent = send_grads(["conf_w_pw2", "conf_w_pw1"], grads)

    dx1, dout0, dmo0, dhid0, acc0 = _mlp_backward(dx2, x1, r0, mo0, out0, par0, wts["mlp_w_in"], wts["mlp_w_out"], 0,
                                                  "l0_mlp_bwd", after=[sent])
    grads["mlp_w_in"], grads["mlp_w_out"] = _mlp_weight_grads(h1, dhid0, r0, dmo0, 0, mlp_dw, "l0")
    sent = send_grads(["mlp_w_in", "mlp_w_out"], grads)
    dg1_0, _, dng01, dsc2_0, dsh2_0, dg2_0 = [acc0[k:k + 1] for k in range(6)]

    dz = _mm(dout0, wts["rec_w_out"], "l0_out_proj_dx", tb=True, after=[sent])
    grads["rec_w_out"] = _mm(zb, dout0, "l0_out_proj_dw", ta=True, out_dtype=bf16)
    sent = send_grads(["rec_w_out"], grads)

    def f_gate_bwd(i, gp, y0_, y1_, dz_):
        lat = jnp.where(i < N_CTX_TILES, 0.0, 1.0)
        _, vjp = jax.vjp(lambda a, b: _gelu(a) * b, gp, y0_ + y1_)
        dgp, dy = vjp(dz_)
        return dgp * lat, dy * lat

    dgp, dy = _rowcall(f_gate_bwd, "l0_gate_bwd", T_ALL, TM,
                       [_rin(gr, R, 0), _rin(y0), _rin(y1), _rin(dz, None, 0, -N_CTX_TILES)], [],
                       [(R, bf16), (R, f32)], after=[sent])
    (dh_f,) = _scan_call(a0, dy, SCAN_FWD_BWD, True, "l0_scan_fwd_bwd", True)
    (dh_r,) = _scan_call(a1, dy, SCAN_REV_BWD, False, "l0_scan_rev_bwd", True)

    dpre, du_f, *dpar_f = _gates_coeff_bwd(ub, u, dh_f, yp0, *gate_args, 0, None)
    dpre, du_r, *dpar_r = _gates_coeff_bwd(ub, u, dh_r, yp1, *gate_args, 1, dpre)
    grads["rec_b_a"], grads["rec_b_x"], grads["rec_lambda"] = [
        jnp.concatenate([f.reshape(-1), r_.reshape(-1)]).reshape(2, R) for f, r_ in zip(dpar_f, dpar_r)]
    du_gates = _gates_dx(dpre, wts["gates"])
    grads["gates"] = _gates_dw(ub, dpre)
    drec, dconv4_w, dconv4_b = _dwconv_bwd([du_f, du_r, du_gates], gr, R // 256, wts["rec_conv_w"], 4, 1,
                                           CONV_SEGMENTS, 256, "l0_conv_bwd", bf16)
    grads["rec_conv_w"], grads["rec_conv_b"] = dconv4_w, dconv4_b
    dgr = jnp.concatenate([dgp, drec], axis=1)
    sent = send_grads(["replicated"], grads)
    grads["rec_w_in"] = _mm(h0, dgr, "l0_in_proj_dw", ta=True, out_dtype=bf16, after=[sent])
    dh0 = _mm(dgr, wts["rec_w_in"], "l0_in_proj_dx", tb=True, after=[send_grads(["rec_w_in"], grads)])

    def f_pre0_bwd(i, x0, dh_, dxs, g, scp_, shp_):
        lat = jnp.where(i < N_CTX_TILES, 0.0, 1.0)
        _, vjp = jax.vjp(lambda a, b, c, e: _normmod(a, b, blend(i, c), blend(i, e)), x0, g, scp_, shp_)
        dx, dg, dscp, dshp = vjp(dh_)
        return dx + lat * dxs, dg, dscp, dshp

    dx0cat, dng00, dscp, dshp = _rowcall(
        f_pre0_bwd, "l0_prenorm_bwd", T_ALL, TM, [_rin(x0cat), _rin(dh0), _rin(dx1, None, 0, -N_CTX_TILES)],
        [ng[0, 0][None], scp, shp], [(D, f32)], [(1, D), (2, D), (2, D)])

    grads["norm_g"] = jnp.stack([jnp.concatenate([dng00, dng01], 0), jnp.concatenate([dng10, dng11], 0)])
    dmods = jnp.stack([
        jnp.concatenate([dshp[1:2], dscp[1:2], dg1_0, dsh2_0, dsc2_0, dg2_0], axis=0),
        jnp.concatenate([dsh1_1, dsc1_1, dg1_1, dsh2_1, dsc2_1, dg2_1], axis=0)])
    dcmod = jnp.concatenate([dshp[0:1], dscp[0:1]], axis=0)
    return loss_acc[0, 0], dx0cat[T_CTX:], dmods, dcmod, grads


def _unshard_cols(g):
    g = jnp.moveaxis(g, 0, -2)
    return g.reshape(g.shape[:-2] + (g.shape[-2] * g.shape[-1],))


def _shard_cols(w):
    w = w.reshape(w.shape[:-1] + (N_DEV, w.shape[-1] // N_DEV))
    return jnp.moveaxis(w, -2, 0)


def _shard_rows(w):
    return w.reshape((N_DEV, w.shape[0] // N_DEV) + w.shape[1:])


SMALL_PACK_ROWS = 64


def kernel(x, c, ctx, c_ctx, w_ada, b_ada, norm_g, rec_w_in, rec_conv_w, rec_conv_b, rec_lambda, rec_w_a, rec_b_a, rec_w_x, rec_b_x, rec_w_out, conf_w_pw1, conf_b_pw1, conf_conv_w, conf_conv_b, conf_ln_g, conf_ln_b, conf_w_pw2, conf_b_pw2, mlp_w_in, mlp_w_out, final_g, loss_target, m_c_ctx, m_w_ada, m_b_ada, m_norm_g, m_rec_w_in, m_rec_conv_w, m_rec_conv_b, m_rec_lambda, m_rec_w_a, m_rec_b_a, m_rec_w_x, m_rec_b_x, m_rec_w_out, m_conf_w_pw1, m_conf_b_pw1, m_conf_conv_w, m_conf_conv_b, m_conf_ln_g, m_conf_ln_b, m_conf_w_pw2, m_conf_b_pw2, m_mlp_w_in, m_mlp_w_out, m_final_g, v_c_ctx, v_w_ada, v_b_ada, v_norm_g, v_rec_w_in, v_rec_conv_w, v_rec_conv_b, v_rec_lambda, v_rec_w_a, v_rec_b_a, v_rec_w_x, v_rec_b_x, v_rec_w_out, v_conf_w_pw1, v_conf_b_pw1, v_conf_conv_w, v_conf_conv_b, v_conf_ln_g, v_conf_ln_b, v_conf_w_pw2, v_conf_b_pw2, v_mlp_w_in, v_mlp_w_out, v_final_g):
    me = 4 * lax.axis_index("x") + 2 * lax.axis_index("y") + lax.axis_index("c")
    weights = dict(c_ctx=c_ctx, w_ada=w_ada, b_ada=b_ada, norm_g=norm_g, rec_w_in=rec_w_in, rec_conv_w=rec_conv_w,
                   rec_conv_b=rec_conv_b, rec_lambda=rec_lambda, rec_w_a=rec_w_a, rec_b_a=rec_b_a, rec_w_x=rec_w_x,
                   rec_b_x=rec_b_x, rec_w_out=rec_w_out, conf_w_pw1=conf_w_pw1, conf_b_pw1=conf_b_pw1,
                   conf_conv_w=conf_conv_w, conf_conv_b=conf_conv_b, conf_ln_g=conf_ln_g, conf_ln_b=conf_ln_b,
                   conf_w_pw2=conf_w_pw2, conf_b_pw2=conf_b_pw2, mlp_w_in=mlp_w_in, mlp_w_out=mlp_w_out, final_g=final_g)
    m_in = dict(c_ctx=m_c_ctx, w_ada=m_w_ada, b_ada=m_b_ada, norm_g=m_norm_g, rec_w_in=m_rec_w_in, rec_conv_w=m_rec_conv_w,
                rec_conv_b=m_rec_conv_b, rec_lambda=m_rec_lambda, rec_w_a=m_rec_w_a, rec_b_a=m_rec_b_a, rec_w_x=m_rec_w_x,
                rec_b_x=m_rec_b_x, rec_w_out=m_rec_w_out, conf_w_pw1=m_conf_w_pw1, conf_b_pw1=m_conf_b_pw1,
                conf_conv_w=m_conf_conv_w, conf_conv_b=m_conf_conv_b, conf_ln_g=m_conf_ln_g, conf_ln_b=m_conf_ln_b,
                conf_w_pw2=m_conf_w_pw2, conf_b_pw2=m_conf_b_pw2, mlp_w_in=m_mlp_w_in, mlp_w_out=m_mlp_w_out,
                final_g=m_final_g)
    v_in = dict(c_ctx=v_c_ctx, w_ada=v_w_ada, b_ada=v_b_ada, norm_g=v_norm_g, rec_w_in=v_rec_w_in, rec_conv_w=v_rec_conv_w,
                rec_conv_b=v_rec_conv_b, rec_lambda=v_rec_lambda, rec_w_a=v_rec_w_a, rec_b_a=v_rec_b_a, rec_w_x=v_rec_w_x,
                rec_b_x=v_rec_b_x, rec_w_out=v_rec_w_out, conf_w_pw1=v_conf_w_pw1, conf_b_pw1=v_conf_b_pw1,
                conf_conv_w=v_conf_conv_w, conf_conv_b=v_conf_conv_b, conf_ln_g=v_conf_ln_g, conf_ln_b=v_conf_ln_b,
                conf_w_pw2=v_conf_w_pw2, conf_b_pw2=v_conf_b_pw2, mlp_w_in=v_mlp_w_in, mlp_w_out=v_mlp_w_out,
                final_g=v_final_g)
    names = list(weights)

    small_items = [c, norm_g, rec_conv_w, rec_lambda, conf_b_pw1, conf_conv_w, conf_conv_b, conf_ln_g, conf_ln_b,
                   conf_b_pw2]
    flat = jnp.concatenate([a.reshape(-1) for a in small_items])
    flat = jnp.pad(flat, (0, SMALL_PACK_ROWS * 128 - flat.shape[0])).reshape(SMALL_PACK_ROWS, 128)
    (small_all,) = _all_gather([flat], "gather_small")

    small_all = small_all.reshape(N_DEV, -1)
    off = 0
    small = []
    for a in small_items:
        small.append(small_all[:, off:off + a.size].reshape((N_DEV,) + a.shape))
        off += a.size
    c_all, ng_all, rcw_all, lam_all, bpw1_all, ccw_all, ccb_all, lng_all, lnb_all, bpw2_all = small
    wts = {
        "norm_g": _unshard_cols(ng_all),
        "rec_conv_w": _unshard_cols(rcw_all)[0],
        "rec_lambda": _unshard_cols(lam_all)[0],
        "conf_b_pw1": _unshard_cols(bpw1_all),
        "conf_conv_w": _unshard_cols(ccw_all)[0],
        "conf_conv_b": _unshard_cols(ccb_all),
        "conf_ln_g": _unshard_cols(lng_all),
        "conf_ln_b": _unshard_cols(lnb_all),
        "conf_b_pw2": _unshard_cols(bpw2_all),
        "rec_conv_b": rec_conv_b,
        "rec_b_a": rec_b_a[0].reshape(2, R),
        "rec_b_x": rec_b_x[0].reshape(2, R),
        "final_g": final_g[None],
        "gates": _gate_matrix(rec_w_a[0], rec_w_x[0]),
    }

    c16 = jnp.concatenate([c_all[:, 0], jnp.broadcast_to(c_ctx[None], (8, D))], axis=0)
    b_loc = lax.dynamic_slice_in_dim(b_ada, me * ADA_SHARD, ADA_SHARD, axis=1)[:, None]
    (mods_all,) = _all_gather([_ada_forward(c16, w_ada, b_loc)], "gather_mods")
    mods_all = _unshard_cols(mods_all)
    mods = lax.dynamic_index_in_dim(mods_all, me, axis=1, keepdims=False).reshape(2, N_MOD, D)
    cmod = mods_all[0, 8, :2 * D].reshape(2, D)

    as_shard = lambda a: a.astype(bf16).reshape(-1, a.shape[-1])
    early = _all_gather_2level([as_shard(rec_w_in[0])], "gather_weights_early")
    wts["rec_w_in"] = _unshard_cols(early[0])
    late_items = {"mlp": [rec_w_out[0], mlp_w_in, mlp_w_out], "conf": [conf_w_pw1[0], conf_w_pw2[0]]}
    late_handles, order = {}, [early[0], mods]
    for group in ("mlp", "conf"):
        shards = [as_shard(a) for a in late_items[group]]
        lands = [_own_block_filled(s, me) for s in shards]
        if group == "mlp":
            late_handles[group], token = _chip_gather_start(shards, lands, "gather_mlp_start", after=order)
        else:
            late_handles[group], token = _exchange_start(shards, lands, "gather_conf_start", False, after=order)
        order = [token]

    def late_weights(group, after):
        if group == "mlp":
            forwarded = _chip_gather_forward(late_handles[group], after, "gather_mlp_forward")
            got = _chip_gather_wait(forwarded, after, "gather_mlp_wait")
        else:
            got = _exchange_wait(late_handles[group], after, "gather_conf_wait", False)
        got = [g.reshape((N_DEV,) + a.shape) for g, a in zip(got, late_items[group])]
        if group == "mlp":
            return {"rec_w_out": got[0].reshape(R, D), "mlp_w_in": got[1], "mlp_w_out": got[2]}
        return {"conf_w_pw1": _unshard_cols(got[0]), "conf_w_pw2": got[1].reshape(D, D)}

    to_blocks = {"rec_w_in": _shard_cols, "conf_w_pw1": _shard_cols, "rec_w_out": _shard_rows, "conf_w_pw2": _shard_rows,
                 "mlp_w_in": lambda g: g, "mlp_w_out": lambda g: g}
    grad_handles = []

    repl_names = ["rec_conv_b", "rec_w_a", "rec_w_x", "rec_b_a", "rec_b_x", "final_g"]

    def send_replicated(grads):
        dwg = grads["gates"]
        repl = {"rec_conv_b": grads["rec_conv_b"],
                "rec_w_a": jnp.stack([_gate_blocks(dwg, 0), _gate_blocks(dwg, 2)]),
                "rec_w_x": jnp.stack([_gate_blocks(dwg, 1), _gate_blocks(dwg, 3)]),
                "rec_b_a": grads["rec_b_a"], "rec_b_x": grads["rec_b_x"], "final_g": grads["final_g"]}
        flat = jnp.concatenate([repl[n].reshape(-1) for n in repl_names])
        rows = -(-flat.shape[0] // (16 * D)) * 16
        flat = jnp.pad(flat, (0, rows * D - flat.shape[0])).reshape(rows, D).astype(bf16)
        handle, sent = _exchange_start([flat], [_own_block_filled(flat, me)], "gather_replicated_start", False)
        grad_handles.append((["replicated"], handle))
        return sent

    def send_grads(group, grads):
        if group == ["replicated"]:
            return send_replicated(grads)
        blocks = [to_blocks[n](grads[n]) for n in group]
        blocks = [g.reshape(N_DEV, -1, g.shape[-1]) for g in blocks]
        lands = [_own_block_filled(lax.dynamic_index_in_dim(g, me, 0, keepdims=False), me) for g in blocks]
        handle, sent = _exchange_start(blocks, lands, "scatter_start_" + group[0], True)
        grad_handles.append((group, handle))
        return sent

    loss_part, grad_x, dmods, dcmod, grads = _local_step(x[0], ctx[0], loss_target[0], mods, cmod, wts, late_weights,
                                                         send_grads, start_after=order)
    loss = lax.psum(loss_part, ("x", "y", "c"))

    dm_flat = jnp.concatenate([dmods.reshape(-1), dcmod.reshape(-1)]).reshape(-1, 128)
    (dm_all,) = _all_gather([dm_flat], "gather_dmods")
    dm_all = dm_all.reshape(N_DEV, -1)
    dmods_all = dm_all[:, :2 * N_MOD * D].reshape(N_DEV, 2, N_MOD * D)
    dcmod_all = jnp.pad(dm_all[:, 2 * N_MOD * D:], ((0, 0), (0, (N_MOD - 2) * D)))
    g16_full = jnp.stack([jnp.concatenate([dmods_all[:, 0], dcmod_all], axis=0),
                          jnp.concatenate([dmods_all[:, 1], jnp.zeros_like(dcmod_all)], axis=0)])
    g16 = lax.dynamic_slice_in_dim(g16_full, me * ADA_SHARD, ADA_SHARD, axis=2)
    dw_ada, ds_part = _ada_backward(c16, g16, w_ada)
    (ds_all,) = _all_gather([ds_part[0]], "gather_dsilu")

    big_names, big_pieces, repl_all = [], [], None
    for group, handle in grad_handles:
        if group == ["replicated"]:
            repl_all = _exchange_wait(handle, grad_x, "gather_replicated_wait", False)[0].reshape(N_DEV, -1)
            continue
        for n, got in zip(group, _exchange_wait(handle, grad_x, "scatter_wait_" + group[0], True)):
            big_names.append(n)
            big_pieces.append([(got, N_DEV)])
    small_sharded = ["norm_g", "rec_conv_w", "rec_lambda", "conf_b_pw1", "conf_conv_w", "conf_conv_b", "conf_ln_g",
                     "conf_ln_b", "conf_b_pw2"]
    pack = jnp.concatenate([_shard_cols(grads[n]).reshape(N_DEV, -1) for n in small_sharded], axis=1)
    pack_len = pack.shape[1]
    pack = jnp.pad(pack, ((0, 0), (0, SMALL_PACK_ROWS * 128 - pack_len))).reshape(N_DEV, SMALL_PACK_ROWS, 128)
    (pack_recv,) = _all_to_all([pack], "scatter_small_grads")
    pack_recv = pack_recv.reshape(N_DEV, -1)


    def as2d(shape):
        rows = 1
        for s in shape[:-1]:
            rows *= s
        return (rows, shape[-1])

    def whole(arr, shape):
        arr = arr.reshape((-1,) + as2d(shape))
        return (arr, arr.shape[0])

    pieces = {}
    shard_shapes = {n: weights[n].shape for n in names}
    for n, parts in zip(big_names, big_pieces):
        pieces[n] = parts
    off = 0
    for n in small_sharded:
        size = weights[n].size
        pieces[n] = [whole(pack_recv[:, off:off + size], shard_shapes[n])]
        off += size
    off = 0
    for n in repl_names:
        size = weights[n].size
        pieces[n] = [whole(repl_all[:, off:off + size], shard_shapes[n])]
        off += size
    pieces["w_ada"] = [whole(dw_ada, shard_shapes["w_ada"])]
    db_terms = jnp.concatenate([dmods_all, jnp.stack([dcmod_all, jnp.zeros_like(dcmod_all)], axis=1)], axis=0)
    pieces["b_ada"] = [whole(db_terms, shard_shapes["b_ada"])]
    pieces["c_ctx"] = [whole(ds_all[:, 0], shard_shapes["c_ctx"])]

    g_out, d_out, m_out, v_out = {}, {}, {}, {}
    for n in names:
        shape = shard_shapes[n]
        r2, c2 = as2d(shape)
        p = pieces[n]
        g, dl, nm, nv = _adamw(p, weights[n].reshape(r2, c2), m_in[n].reshape(r2, c2), v_in[n].reshape(r2, c2),
                               "adamw_" + n)
        g_out[n], d_out[n], m_out[n], v_out[n] = (t.reshape(shape) for t in (g, dl, nm, nv))

    return (loss, grad_x[None], *[g_out[n] for n in names], *[d_out[n] for n in names],
            *[m_out[n] for n in names], *[v_out[n] for n in names])
```

```python
import functools

import jax
import jax.numpy as jnp
from jax import lax
from jax.experimental import pallas as pl
from jax.experimental.pallas import tpu as pltpu

f32 = jnp.float32
bf16 = jnp.bfloat16

N_DEV = 8
D = 1024
T_LAT = 2048
T_CTX = 256
T_ALL = T_CTX + T_LAT
R = 1280
N_BLK = 16
BLK = R // N_BLK
F = 4096
GRID_W = 64
RG_C = 8.0
EPS = 1e-6
POS_BASE = 10000.0
N_MOD = 6
ADA_SHARD = N_MOD * D // N_DEV

ADAM_LR = 0.001
ADAM_B1 = 0.9
ADAM_B2 = 0.999
ADAM_EPS = 1e-08
ADAM_WD = 0.01
ADAM_STEP = 10

VMEM_LIMIT_V7X = 56 * 1024 * 1024
HALO = 16
MESH = pl.DeviceIdType.MESH


def _cparams(*sem):
    return pltpu.CompilerParams(dimension_semantics=sem, vmem_limit_bytes=VMEM_LIMIT_V7X)


def _pick(n, cands):
    for c in cands:
        if n % c == 0:
            return c
    raise ValueError(f"no block size for {n}")


def _position():
    x, y, c = lax.axis_index("x"), lax.axis_index("y"), lax.axis_index("c")
    return x, y, c, 4 * x + 2 * y + c


def _peer(x, y, c, k):
    px = (1 - x) if (k >> 2) & 1 else x
    py = (1 - y) if (k >> 1) & 1 else y
    pc = (1 - c) if k & 1 else c
    return (px, py, pc), 4 * px + 2 * py + pc


def _exchange(arrs, name, scatter):
    n = len(arrs)

    def body(*refs):
        ins, outs = refs[:n], refs[n:2 * n]
        send_sems, recv_sems, local_sems = refs[2 * n:]
        x, y, c, me = _position()
        local = []
        for a in range(n):
            src = ins[a].at[me] if scatter else ins[a]
            cp = pltpu.make_async_copy(src, outs[a].at[me], local_sems.at[a])
            cp.start()
            local.append(cp)
        sends, recvs = [], []
        for a in range(n):
            for k in range(1, N_DEV):
                peer, peer_lin = _peer(x, y, c, k)
                src = ins[a].at[peer_lin] if scatter else ins[a]
                cp = pltpu.make_async_remote_copy(
                    src_ref=src, dst_ref=outs[a].at[me], send_sem=send_sems.at[a, k - 1],
                    recv_sem=recv_sems.at[a, k - 1], device_id=peer, device_id_type=MESH)
                cp.start()
                sends.append(cp)
                recvs.append(pltpu.make_async_remote_copy(
                    src_ref=src, dst_ref=outs[a].at[peer_lin], send_sem=send_sems.at[a, k - 1],
                    recv_sem=recv_sems.at[a, k - 1], device_id=peer, device_id_type=MESH))
        for cp in recvs:
            cp.wait_recv()
        for cp in sends:
            cp.wait_send()
        for cp in local:
            cp.wait()

    if scatter:
        out_shape = [jax.ShapeDtypeStruct(a.shape, a.dtype) for a in arrs]
    else:
        out_shape = [jax.ShapeDtypeStruct((N_DEV,) + a.shape, a.dtype) for a in arrs]
    any_spec = pl.BlockSpec(memory_space=pl.ANY)
    return pl.pallas_call(
        body, name=name, out_shape=out_shape,
        in_specs=[any_spec] * n, out_specs=[any_spec] * n,
        scratch_shapes=[pltpu.SemaphoreType.DMA((n, N_DEV - 1)), pltpu.SemaphoreType.DMA((n, N_DEV - 1)),
                        pltpu.SemaphoreType.DMA((n,))],
    )(*arrs)


def _all_gather(arrs, name):
    return _exchange(arrs, name, scatter=False)


def _lin(p):
    return 4 * p[0] + 2 * p[1] + p[2]


HBM_SPEC = pl.BlockSpec(memory_space=pltpu.HBM)
SEM_SPEC = pl.BlockSpec(memory_space=pltpu.SEMAPHORE)
DATAFLOW_EFFECT = pltpu.SideEffectType.DATAFLOW_SIDE_EFFECTING


def _split_copies(srcs, lands, send_sems, recv_sems, scatter):
    x, y, c, me = _position()
    out = []
    for a in range(len(srcs)):
        for k in range(1, N_DEV):
            peer, peer_lin = _peer(x, y, c, k)
            src = srcs[a].at[peer_lin] if scatter else srcs[a]
            mk = lambda slot: pltpu.make_async_remote_copy(
                src_ref=src, dst_ref=lands[a].at[slot], send_sem=send_sems.at[a * (N_DEV - 1) + k - 1],
                recv_sem=recv_sems.at[a * (N_DEV - 1) + k - 1], device_id=peer, device_id_type=MESH)
            out.append((mk(me), mk(peer_lin)))
    return out


def _exchange_start(srcs, lands, name, scatter, after=()):
    n = len(srcs)
    n_after = len(after)

    def body(*refs):
        srcs_r, lands_r = refs[:n], refs[n:2 * n]
        send_sems, recv_sems = refs[2 * n + n_after], refs[2 * n + n_after + 1]
        token = refs[-1]
        for outgoing, _ in _split_copies(srcs_r, lands_r, send_sems, recv_sems, scatter):
            outgoing.start()
        token[...] = jnp.zeros_like(token)

    hbm = lambda a: pltpu.HBM(a.shape, a.dtype)
    res = pl.pallas_call(
        body, name=name,
        out_shape=(pltpu.SemaphoreType.DMA((n * (N_DEV - 1),)), pltpu.SemaphoreType.DMA((n * (N_DEV - 1),)),
                   *[hbm(a) for a in srcs], *[hbm(a) for a in lands], jax.ShapeDtypeStruct((8, 128), f32)),
        in_specs=[HBM_SPEC] * (2 * n) + [pl.BlockSpec(memory_space=pl.ANY)] * n_after,
        out_specs=(SEM_SPEC, SEM_SPEC, *[HBM_SPEC] * (2 * n), pl.BlockSpec(memory_space=pltpu.VMEM)),
        input_output_aliases={i: 2 + i for i in range(2 * n)},
        compiler_params=pltpu.CompilerParams(has_side_effects=DATAFLOW_EFFECT),
    )(*[pltpu.with_memory_space_constraint(a, pltpu.HBM) for a in list(srcs) + list(lands)], *after)
    return (res[0], res[1], list(res[2:2 + n]), list(res[2 + n:2 + 2 * n])), res[-1]


def _exchange_wait(handle, after, name, scatter):
    send_sems, recv_sems, srcs, lands = handle
    n = len(srcs)

    def body(*refs):
        srcs_r, lands_r = refs[:n], refs[n:2 * n]
        send_s, recv_s = refs[2 * n], refs[2 * n + 1]
        for outgoing, incoming in _split_copies(srcs_r, lands_r, send_s, recv_s, scatter):
            outgoing.wait_send()
            incoming.wait_recv()

    hbm = lambda a: pltpu.HBM(a.shape, a.dtype)
    res = pl.pallas_call(
        body, name=name, out_shape=tuple(hbm(a) for a in list(srcs) + list(lands)),
        in_specs=[HBM_SPEC] * (2 * n) + [SEM_SPEC, SEM_SPEC, pl.BlockSpec(memory_space=pl.ANY)],
        out_specs=tuple([HBM_SPEC] * (2 * n)),
        input_output_aliases={i: i for i in range(2 * n)},
        compiler_params=pltpu.CompilerParams(has_side_effects=DATAFLOW_EFFECT),
    )(*srcs, *lands, send_sems, recv_sems, after)
    return list(res[n:])


def _chip_peers(x, y, c):
    return [(x, y, 1 - c)] + [_plane_pos(x, y, q) + (c,) for q in (2, 1, 3)]


def _chip_gather_start(shards, lands, name, after=()):
    n, n_after = len(shards), len(after)

    def body(*refs):
        srcs_r, lands_r = refs[:n], refs[n:2 * n]
        send_sems, recv_sems = refs[2 * n + n_after], refs[2 * n + n_after + 1]
        x, y, c, me = _position()
        for a in range(n):
            for k, peer in enumerate(_chip_peers(x, y, c)):
                pltpu.make_async_remote_copy(
                    src_ref=srcs_r[a], dst_ref=lands_r[a].at[me], send_sem=send_sems.at[4 * a + k],
                    recv_sem=recv_sems.at[4 * a + k], device_id=peer, device_id_type=MESH).start()
        refs[-1][...] = jnp.zeros_like(refs[-1])

    hbm = lambda a: pltpu.HBM(a.shape, a.dtype)
    res = pl.pallas_call(
        body, name=name,
        out_shape=(pltpu.SemaphoreType.DMA((4 * n,)), pltpu.SemaphoreType.DMA((4 * n,)),
                   *[hbm(a) for a in shards], *[hbm(a) for a in lands], jax.ShapeDtypeStruct((8, 128), f32)),
        in_specs=[HBM_SPEC] * (2 * n) + [ANY_SPEC] * n_after,
        out_specs=(SEM_SPEC, SEM_SPEC, *[HBM_SPEC] * (2 * n), pl.BlockSpec(memory_space=pltpu.VMEM)),
        input_output_aliases={i: 2 + i for i in range(2 * n)},
        compiler_params=pltpu.CompilerParams(has_side_effects=DATAFLOW_EFFECT),
    )(*[pltpu.with_memory_space_constraint(a, pltpu.HBM) for a in list(shards) + list(lands)], *after)
    return (res[0], res[1], list(res[2:2 + n]), list(res[2 + n:2 + 2 * n])), res[-1]


def _chip_gather_forward(handle, after, name):
    send_sems, recv_sems, srcs, lands = handle
    n = len(srcs)

    def body(*refs):
        srcs_r, lands_r = refs[:n], refs[n:2 * n]
        send1, recv1 = refs[2 * n], refs[2 * n + 1]
        send2, recv2 = refs[2 * n + 3], refs[2 * n + 4]
        x, y, c, me = _position()
        peers = _chip_peers(x, y, c)
        for a in range(n):
            for k, peer in enumerate(peers):
                mk = lambda slot: pltpu.make_async_remote_copy(
                    src_ref=srcs_r[a], dst_ref=lands_r[a].at[slot], send_sem=send1.at[4 * a + k],
                    recv_sem=recv1.at[4 * a + k], device_id=peer, device_id_type=MESH)
                mk(me).wait_send()
                mk(_lin(peer)).wait_recv()
        for a in range(n):
            for k, peer in enumerate(peers[1:]):
                slot = _lin(peer)
                pltpu.make_async_remote_copy(
                    src_ref=lands_r[a].at[slot], dst_ref=lands_r[a].at[slot], send_sem=send2.at[3 * a + k],
                    recv_sem=recv2.at[3 * a + k], device_id=peers[0], device_id_type=MESH).start()

    hbm = lambda a: pltpu.HBM(a.shape, a.dtype)
    res = pl.pallas_call(
        body, name=name,
        out_shape=(pltpu.SemaphoreType.DMA((3 * n,)), pltpu.SemaphoreType.DMA((3 * n,)), *[hbm(a) for a in lands]),
        in_specs=[HBM_SPEC] * (2 * n) + [SEM_SPEC, SEM_SPEC, ANY_SPEC],
        out_specs=(SEM_SPEC, SEM_SPEC, *[HBM_SPEC] * n),
        input_output_aliases={n + i: 2 + i for i in range(n)},
        compiler_params=pltpu.CompilerParams(has_side_effects=DATAFLOW_EFFECT),
    )(*srcs, *lands, send_sems, recv_sems, after)
    return (res[0], res[1], list(res[2:]))


def _chip_gather_wait(handle, after, name):
    send_sems, recv_sems, lands = handle
    n = len(lands)

    def body(*refs):
        lands_r, send2, recv2 = refs[:n], refs[n], refs[n + 1]
        x, y, c, me = _position()
        peers = _chip_peers(x, y, c)
        for a in range(n):
            for k, (px, py, pc) in enumerate(peers[1:]):
                mk = lambda slot: pltpu.make_async_remote_copy(
                    src_ref=lands_r[a].at[slot], dst_ref=lands_r[a].at[slot], send_sem=send2.at[3 * a + k],
                    recv_sem=recv2.at[3 * a + k], device_id=peers[0], device_id_type=MESH)
                mk(_lin((px, py, pc))).wait_send()
                mk(_lin((px, py, 1 - pc))).wait_recv()

    hbm = lambda a: pltpu.HBM(a.shape, a.dtype)
    res = pl.pallas_call(
        body, name=name, out_shape=tuple(hbm(a) for a in lands),
        in_specs=[HBM_SPEC] * n + [SEM_SPEC, SEM_SPEC, ANY_SPEC], out_specs=tuple([HBM_SPEC] * n),
        input_output_aliases={i: i for i in range(n)},
        compiler_params=pltpu.CompilerParams(has_side_effects=DATAFLOW_EFFECT),
    )(*lands, send_sems, recv_sems, after)
    return list(res)


def _own_block_filled(block, me):
    land = lax.empty((N_DEV,) + block.shape, block.dtype)
    return lax.dynamic_update_index_in_dim(land, block, me, 0)


def _staged_copy(src, dst, buf, in_sems, out_sems, rows, chunk):
    n = rows // chunk

    def rd(i):
        return pltpu.make_async_copy(src.at[pl.ds(i * chunk, chunk)], buf.at[i % 2], in_sems.at[i % 2])

    def wr(i):
        return pltpu.make_async_copy(buf.at[i % 2], dst.at[pl.ds(i * chunk, chunk)], out_sems.at[i % 2])

    rd(0).start()
    for i in range(n):
        if i + 1 < n:
            if i >= 1:
                wr(i - 1).wait()
            rd(i + 1).start()
        rd(i).wait()
        wr(i).start()
    for i in range(max(n - 2, 0), n):
        wr(i).wait()


def _all_gather_2level(shards, name):
    n = len(shards)
    chunks = [_pick(s.shape[0], (416, 512, 256, 160, 128, 64, 16)) for s in shards]

    def body(*refs):
        ins, outs = refs[:n], refs[n:2 * n]
        send_sems, recv_sems, in_sems, out_sems = refs[2 * n:2 * n + 4]
        bufs = refs[2 * n + 4:]
        x, y, c, me = _position()
        sib, xn, yn, dg = (x, y, 1 - c), (1 - x, y, c), (x, 1 - y, c), (1 - x, 1 - y, c)

        def cp(a, k, src, slot, to):
            return pltpu.make_async_remote_copy(src_ref=src, dst_ref=outs[a].at[slot], send_sem=send_sems.at[a, k],
                                                recv_sem=recv_sems.at[a, k], device_id=to, device_id_type=MESH)

        for a in range(n):
            for k, to in ((0, sib), (1, xn), (2, yn)):
                cp(a, k, ins[a], me, to).start()
        for a in range(n):
            cp(a, 1, ins[a], _lin(xn), xn).wait_recv()
            cp(a, 3, outs[a].at[_lin(xn)], _lin(xn), sib).start()

            @pl.when(c == 0)
            def _():
                cp(a, 5, outs[a].at[_lin(xn)], _lin(xn), yn).start()

            cp(a, 2, ins[a], _lin(yn), yn).wait_recv()
            cp(a, 4, outs[a].at[_lin(yn)], _lin(yn), sib).start()

            @pl.when(c == 1)
            def _():
                cp(a, 5, outs[a].at[_lin(yn)], _lin(yn), xn).start()

        for a in range(n):
            cp(a, 5, ins[a], _lin(dg), xn).wait_recv()
            cp(a, 6, outs[a].at[_lin(dg)], _lin(dg), sib).start()
        for a in range(n):
            _staged_copy(ins[a], outs[a].at[me], bufs[a], in_sems.at[a], out_sems.at[a], shards[a].shape[0], chunks[a])
        for a in range(n):
            for k, origin in ((0, sib), (3, (1 - x, y, 1 - c)), (4, (x, 1 - y, 1 - c)), (6, (1 - x, 1 - y, 1 - c))):
                cp(a, k, ins[a], _lin(origin), sib).wait_recv()
            for k in range(7):
                cp(a, k, ins[a], me, sib).wait_send()

    any_spec = pl.BlockSpec(memory_space=pl.ANY)
    return pl.pallas_call(
        body, name=name, out_shape=[jax.ShapeDtypeStruct((N_DEV,) + s.shape, s.dtype) for s in shards],
        in_specs=[any_spec] * n, out_specs=[any_spec] * n,
        scratch_shapes=[pltpu.SemaphoreType.DMA((n, 7)), pltpu.SemaphoreType.DMA((n, 7)),
                        pltpu.SemaphoreType.DMA((n, 2)), pltpu.SemaphoreType.DMA((n, 2))]
        + [pltpu.VMEM((2, ch, s.shape[1]), s.dtype) for ch, s in zip(chunks, shards)],
    )(*shards)


def _plane_pos(x, y, q):
    return ((1 - x) if q & 2 else x, (1 - y) if q & 1 else y)


ANY_SPEC = pl.BlockSpec(memory_space=pl.ANY)


def _mm(a, b, name, ta=False, tb=False, out_dtype=f32, after=()):
    if ta:
        k_dim, m_dim = a.shape
    else:
        m_dim, k_dim = a.shape
    if tb:
        n_dim, k2 = b.shape
    else:
        k2, n_dim = b.shape
    assert k_dim == k2, (a.shape, b.shape)
    assert a.dtype == bf16 and b.dtype == bf16
    bm = _pick(m_dim, (512, 768, 640, 256, 128))
    bn = _pick(n_dim, (512, 640, 256, 128))
    bk = _pick(k_dim, (1024, 1280, 768, 512))
    nk = k_dim // bk
    a_spec = (pl.BlockSpec((bk, bm), lambda i, j, k: (k, i)) if ta
              else pl.BlockSpec((bm, bk), lambda i, j, k: (i, k)))
    b_spec = (pl.BlockSpec((bn, bk), lambda i, j, k: (j, k)) if tb
              else pl.BlockSpec((bk, bn), lambda i, j, k: (k, j)))
    dims = (((0 if ta else 1,), (1 if tb else 0,)), ((), ()))

    n_after = len(after)

    def body_single(a_ref, b_ref, *rest):
        o_ref = rest[n_after]
        o_ref[...] = lax.dot_general(a_ref[...], b_ref[...], dims, preferred_element_type=f32).astype(o_ref.dtype)

    def body(a_ref, b_ref, *rest):
        o_ref, acc_ref = rest[n_after:]
        k = pl.program_id(2)

        @pl.when(k == 0)
        def _():
            acc_ref[...] = jnp.zeros_like(acc_ref)

        acc_ref[...] += lax.dot_general(a_ref[...], b_ref[...], dims, preferred_element_type=f32)

        @pl.when(k == nk - 1)
        def _():
            o_ref[...] = acc_ref[...].astype(o_ref.dtype)

    return pl.pallas_call(
        body_single if nk == 1 else body, name=name, out_shape=jax.ShapeDtypeStruct((m_dim, n_dim), out_dtype),
        grid=(m_dim // bm, n_dim // bn, nk), in_specs=[a_spec, b_spec] + [ANY_SPEC] * n_after,
        out_specs=pl.BlockSpec((bm, bn), lambda i, j, k: (i, j)),
        scratch_shapes=[] if nk == 1 else [pltpu.VMEM((bm, bn), f32)],
        compiler_params=_cparams("parallel", "parallel", "arbitrary"),
    )(a, b, *after)


def _rin(arr, width=None, cb=0, roff=0):
    return (arr, arr.shape[1] if width is None else width, cb, roff)


def _rowcall(fn, name, rows, tm, row_ins, par_ins, row_outs, acc_outs=(), after=()):
    nr, npar, nro, n_after = len(row_ins), len(par_ins), len(row_outs), len(after)
    in_specs, args = [], []
    for arr, width, cb, roff in row_ins:
        if roff >= 0:
            imap = lambda i, cb=cb, roff=roff: (i + roff, cb)
        else:
            imap = lambda i, cb=cb, roff=roff: (jnp.maximum(i + roff, 0), cb)
        in_specs.append(pl.BlockSpec((tm, width), imap))
        args.append(arr)
    for p in par_ins:
        in_specs.append(pl.BlockSpec(p.shape, lambda i: (0, 0)))
        args.append(p)
    out_shape, out_specs = [], []
    for width, dt in row_outs:
        out_shape.append(jax.ShapeDtypeStruct((rows, width), dt))
        out_specs.append(pl.BlockSpec((tm, width), lambda i: (i, 0)))
    for p, width in acc_outs:
        out_shape.append(jax.ShapeDtypeStruct((p, width), f32))
        out_specs.append(pl.BlockSpec((p, width), lambda i: (0, 0)))

    def body(*refs):
        i = pl.program_id(0)
        res = fn(i, *[r[...] for r in refs[:nr + npar]])
        outs = refs[nr + npar + n_after:]
        for o, v in zip(outs[:nro], res[:nro]):
            o[...] = v.astype(o.dtype)
        if acc_outs:
            @pl.when(i == 0)
            def _():
                for o in outs[nro:]:
                    o[...] = jnp.zeros_like(o)

            for o, v in zip(outs[nro:], res[nro:]):
                o[...] += v

    return pl.pallas_call(
        body, name=name, out_shape=out_shape, grid=(rows // tm,), in_specs=in_specs + [ANY_SPEC] * n_after,
        out_specs=out_specs, compiler_params=_cparams("arbitrary"),
    )(*args, *after)


def _rms(x, g):
    return x * lax.rsqrt(jnp.mean(x * x, axis=-1, keepdims=True) + EPS) * g


def _normmod(x, g, sc, sh):
    return _rms(x, g) * (1.0 + sc) + sh


def _gelu(x):
    return 0.5 * x * (1.0 + jnp.tanh(0.7978845608028654 * (x + 0.044715 * (x * x * x))))


def _sigmoid(x):
    return 0.5 * (jnp.tanh(0.5 * x) + 1.0)


def _coeff_parts(pre_a, pre_x, ba, bx, lam):
    r = _sigmoid(pre_a + ba)
    ig = _sigmoid(pre_x + bx)
    nl = -lam
    sp = jnp.maximum(nl, 0.0) + jnp.log(1.0 + jnp.exp(-jnp.abs(nl)))
    la = -RG_C * r * sp
    a = jnp.exp(la)
    one_minus_a2 = -jnp.tanh(la) * (a * a + 1.0)
    inv_m = lax.rsqrt(one_minus_a2)
    return r, ig, sp, a, one_minus_a2 * inv_m, inv_m


def _coeff(pre_a, pre_x, u, ba, bx, lam):
    _, ig, _, a, m, _ = _coeff_parts(pre_a, pre_x, ba, bx, lam)
    return a, m * (ig * u)


def _coeff_bwd(pre_a, pre_x, u, ba, bx, lam, da, db):
    r, ig, sp, a, m, inv_m = _coeff_parts(pre_a, pre_x, ba, bx, lam)
    dbu = db * u
    dig = dbu * m
    dm = dbu * ig
    dla = a * (da - dm * a * inv_m)
    dpa = dla * (-RG_C * sp) * (r * (1.0 - r))
    dpx = dig * (ig * (1.0 - ig))
    dsp = jnp.sum(dla * (-RG_C * r), axis=0, keepdims=True)
    dlam = -dsp * _sigmoid(-lam)
    return (dpa, dpx, db * m * ig, jnp.sum(dpa, axis=0, keepdims=True), jnp.sum(dpx, axis=0, keepdims=True), dlam)


SCAN_CHUNK = 256


def _scan_call(a, v, chunk_of, reverse, name, backward):
    rows, width = a.shape
    n_out = 1 if backward else 2
    nt = SCAN_CHUNK // 8

    def body(a_ref, v_ref, *rest):
        outs, state_ref = rest[:-1], rest[-1]

        @pl.when(pl.program_id(0) == 0)
        def _():
            state_ref[...] = jnp.zeros_like(state_ref)

        rid = lax.broadcasted_iota(jnp.int32, (8, width), 0)

        def tile(j, st):
            t0 = pl.multiple_of((nt - 1 - j if reverse else j) * 8, 8)
            at = a_ref[pl.ds(t0, 8), :]
            vt = v_ref[pl.ds(t0, 8), :]
            out = jnp.zeros((8, width), f32)
            prev = jnp.zeros((8, width), f32)
            for i in (range(7, -1, -1) if reverse else range(8)):
                if backward:
                    g = vt[i:i + 1] + st
                    st = at[i:i + 1] * g
                    out = jnp.where(rid == i, g, out)
                else:
                    prev = jnp.where(rid == i, st, prev)
                    st = at[i:i + 1] * st + vt[i:i + 1]
                    out = jnp.where(rid == i, st, out)
            outs[0][pl.ds(t0, 8), :] = out
            if not backward:
                outs[1][pl.ds(t0, 8), :] = prev
            return st

        state_ref[0:1, :] = lax.fori_loop(0, nt, tile, state_ref[0:1, :])

    spec = pl.BlockSpec((SCAN_CHUNK, width), lambda t: (chunk_of(t), 0))
    return pl.pallas_call(
        body, name=name, out_shape=[jax.ShapeDtypeStruct((rows, width), f32)] * n_out,
        grid=(rows // SCAN_CHUNK,), in_specs=[spec, spec], out_specs=[spec] * n_out,
        scratch_shapes=[pltpu.VMEM((8, width), f32)],
        compiler_params=_cparams("arbitrary"),
    )(a, v)


CONV_CHUNK = 256


def _fill_padded(pad_ref, src_ref, start, n):
    cb = pad_ref.shape[1]
    pad_ref[pl.ds(0, HALO), :] = jnp.zeros((HALO, cb), f32)
    pad_ref[pl.ds(HALO, n), :] = src_ref[pl.ds(start, n), :].astype(f32)
    pad_ref[pl.ds(HALO + n, HALO), :] = jnp.zeros((HALO, cb), f32)


def _dwconv_fwd(x, x_cb0, w, b, taps, pad_left, segments, cb, name, emit_bf16):
    rows = x.shape[0]
    width = w.shape[1]

    def body(x_ref, w_ref, b_ref, *rest):
        outs, xp = rest[:-1], rest[-1]
        for start, n in segments:
            _fill_padded(xp, x_ref, start, n)
            for c0 in range(0, n, CONV_CHUNK):
                acc = jnp.zeros((CONV_CHUNK, cb), f32) + b_ref[...]
                for k in range(taps):
                    acc = acc + w_ref[k:k + 1, :] * xp[pl.ds(HALO + c0 + k - pad_left, CONV_CHUNK), :]
                for o in outs:
                    o[pl.ds(start + c0, CONV_CHUNK), :] = acc.astype(o.dtype)

    out_dtypes = [f32, bf16] if emit_bf16 else [f32]
    return pl.pallas_call(
        body, name=name, out_shape=[jax.ShapeDtypeStruct((rows, width), dt) for dt in out_dtypes],
        grid=(width // cb,),
        in_specs=[pl.BlockSpec((rows, cb), lambda j: (0, j + x_cb0)), pl.BlockSpec((taps, cb), lambda j: (0, j)),
                  pl.BlockSpec((1, cb), lambda j: (0, j))],
        out_specs=[pl.BlockSpec((rows, cb), lambda j: (0, j))] * len(out_dtypes),
        scratch_shapes=[pltpu.VMEM((rows + 2 * HALO, cb), f32)],
        compiler_params=_cparams("parallel"),
    )(x, w, b)


def _dwconv_bwd(douts, x, x_cb0, w, taps, pad_left, segments, cb, name, dx_dtype):
    rows = x.shape[0]
    width = w.shape[1]
    nd = len(douts)

    def body(*refs):
        d_refs, x_ref, w_ref = refs[:nd], refs[nd], refs[nd + 1]
        dx_ref, dw_ref, db_ref, xp, dp, dsum = refs[nd + 2:]
        dw_ref[...] = jnp.zeros_like(dw_ref)
        db_ref[...] = jnp.zeros_like(db_ref)
        if nd > 1:
            total = d_refs[0][...]
            for r in d_refs[1:]:
                total = total + r[...]
            dsum[...] = total
            d_ref = dsum
        else:
            d_ref = d_refs[0]
        for start, n in segments:
            _fill_padded(xp, x_ref, start, n)
            _fill_padded(dp, d_ref, start, n)
            for c0 in range(0, n, CONV_CHUNK):
                dchunk = dp[pl.ds(HALO + c0, CONV_CHUNK), :]
                db_ref[...] += jnp.sum(dchunk, axis=0, keepdims=True)
                acc = jnp.zeros((CONV_CHUNK, cb), f32)
                for k in range(taps):
                    acc = acc + w_ref[k:k + 1, :] * dp[pl.ds(HALO + c0 + pad_left - k, CONV_CHUNK), :]
                    xs = xp[pl.ds(HALO + c0 + k - pad_left, CONV_CHUNK), :]
                    dw_ref[k:k + 1, :] += jnp.sum(dchunk * xs, axis=0, keepdims=True)
                dx_ref[pl.ds(start + c0, CONV_CHUNK), :] = acc.astype(dx_ref.dtype)

    dspec = pl.BlockSpec((rows, cb), lambda j: (0, j))
    return pl.pallas_call(
        body, name=name,
        out_shape=[jax.ShapeDtypeStruct((rows, width), dx_dtype), jax.ShapeDtypeStruct((taps, width), f32),
                   jax.ShapeDtypeStruct((1, width), f32)],
        grid=(width // cb,),
        in_specs=[dspec] * nd + [pl.BlockSpec((rows, cb), lambda j: (0, j + x_cb0)),
                                 pl.BlockSpec((taps, cb), lambda j: (0, j))],
        out_specs=[dspec, pl.BlockSpec((taps, cb), lambda j: (0, j)), pl.BlockSpec((1, cb), lambda j: (0, j))],
        scratch_shapes=[pltpu.VMEM((rows + 2 * HALO, cb), f32), pltpu.VMEM((rows + 2 * HALO, cb), f32),
                        pltpu.VMEM((rows, cb), f32)],
        compiler_params=_cparams("parallel"),
    )(*douts, x, w)


def _ada_forward(c16, w_ada, b_loc):
    def body(c_ref, w_ref, b_ref, o_ref):
        cv = c_ref[...]
        s = (cv * _sigmoid(cv)).astype(bf16)
        o_ref[0] = jnp.dot(s, w_ref[0].astype(bf16), preferred_element_type=f32) + b_ref[0]

    return pl.pallas_call(
        body, name="ada_forward", out_shape=jax.ShapeDtypeStruct((2, 16, ADA_SHARD), f32), grid=(2,),
        in_specs=[pl.BlockSpec((16, D), lambda l: (0, 0)), pl.BlockSpec((1, D, ADA_SHARD), lambda l: (l, 0, 0)),
                  pl.BlockSpec((1, 1, ADA_SHARD), lambda l: (l, 0, 0))],
        out_specs=pl.BlockSpec((1, 16, ADA_SHARD), lambda l: (l, 0, 0)),
        compiler_params=_cparams("parallel"),
    )(c16, w_ada, b_loc)


def _ada_backward(c16, g16, w_ada):
    def body(c_ref, g_ref, w_ref, dw_ref, ds_ref):
        cv = c_ref[...]
        s = (cv * _sigmoid(cv)).astype(bf16)
        g = g_ref[0].astype(bf16)
        dw_ref[0] = lax.dot_general(s, g, (((0,), (0,)), ((), ())), preferred_element_type=f32)
        ds = lax.dot_general(g, w_ref[0].astype(bf16), (((1,), (1,)), ((), ())), preferred_element_type=f32)
        cc = cv[8:9]
        sg = _sigmoid(cc)
        dsilu = sg * (1.0 + cc * (1.0 - sg))
        ds_ref[0] = jnp.zeros((8, D), f32) + jnp.sum(ds[8:16], axis=0, keepdims=True) * dsilu

    return pl.pallas_call(
        body, name="ada_backward",
        out_shape=[jax.ShapeDtypeStruct((2, D, ADA_SHARD), f32), jax.ShapeDtypeStruct((2, 8, D), f32)], grid=(2,),
        in_specs=[pl.BlockSpec((16, D), lambda l: (0, 0)), pl.BlockSpec((1, 16, ADA_SHARD), lambda l: (l, 0, 0)),
                  pl.BlockSpec((1, D, ADA_SHARD), lambda l: (l, 0, 0))],
        out_specs=[pl.BlockSpec((1, D, ADA_SHARD), lambda l: (l, 0, 0)), pl.BlockSpec((1, 8, D), lambda l: (l, 0, 0))],
        compiler_params=_cparams("parallel"),
    )(c16, g16, w_ada)


def _adamw(pieces, w, m, v, name, after=()):
    rows, cols = w.shape
    n_arr, n_after = len(pieces), len(after)
    counts = [cnt for _, cnt in pieces]
    pieces = [p for p, _ in pieces]
    tm = 256 if (rows % 256 == 0 and rows > 256) else rows

    def body(*refs):
        p_refs = refs[:n_arr]
        w_ref, m_ref, v_ref = refs[n_arr:n_arr + 3]
        g_ref, d_ref, nm_ref, nv_ref = refs[n_arr + 3 + n_after:]
        g = None
        for p_ref in p_refs:
            for j in range(p_ref.shape[0]):
                term = p_ref[j].astype(f32)
                g = term if g is None else g + term
        m2 = ADAM_B1 * m_ref[...] + (1.0 - ADAM_B1) * g
        v2 = ADAM_B2 * v_ref[...] + (1.0 - ADAM_B2) * (g * g)
        m_hat = m2 / (1.0 - ADAM_B1 ** ADAM_STEP)
        v_hat = v2 / (1.0 - ADAM_B2 ** ADAM_STEP)
        g_ref[...] = g
        d_ref[...] = -ADAM_LR * (m_hat / (jnp.sqrt(v_hat) + ADAM_EPS) + ADAM_WD * w_ref[...])
        nm_ref[...] = m2
        nv_ref[...] = v2

    spec = pl.BlockSpec((tm, cols), lambda i: (i, 0))
    return pl.pallas_call(
        body, name=name, out_shape=[jax.ShapeDtypeStruct((rows, cols), f32)] * 4, grid=(rows // tm,),
        in_specs=[pl.BlockSpec((cnt, tm, cols), lambda i: (0, i, 0)) for cnt in counts] + [spec, spec, spec]
        + [ANY_SPEC] * n_after,
        out_specs=[spec] * 4, compiler_params=_cparams("parallel"),
    )(*pieces, w, m, v, *after)


MLP_TM = 256
FB = F // N_DEV


def _stack_rows(vals, n):
    cols = vals[0].shape[1]
    rid = lax.broadcasted_iota(jnp.int32, (n, cols), 0)
    out = jnp.zeros((n, cols), f32)
    for k, v in enumerate(vals):
        out = jnp.where(rid == k, v, out)
    return out


N_MLP_PARAMS = 9


class _ParamRows:
    def __init__(self, ref):
        self.ref = ref

    def __getitem__(self, sl):
        return self.ref[8 * sl.start:8 * sl.start + 1, :]


def _resident(shape, imap):
    return pl.BlockSpec(shape, imap, pipeline_mode=pl.Buffered(1))


def _mlp_forward(xa, xa_roff, out_prev, par, w_in, w_out, layer, name):
    def body(xa_ref, op_ref, par_ref, win_ref, wout_ref, x1_ref, h_ref, r_ref, mo_ref, x2_ref, hn_ref):
        p = _ParamRows(par_ref)
        x1 = xa_ref[...] + p[0:1] * (op_ref[...] + p[1:2])
        h = _normmod(x1, p[2:3], p[3:4], p[4:5]).astype(bf16)
        x1_ref[...] = x1
        h_ref[...] = h
        mo = jnp.zeros((MLP_TM, D), f32)
        for j in range(N_DEV):
            r = jnp.maximum(jnp.dot(h, win_ref[j], preferred_element_type=f32), 0.0)
            r_ref[:, j * FB:(j + 1) * FB] = r.astype(bf16)
            mo = mo + jnp.dot((r * r).astype(bf16), wout_ref[j], preferred_element_type=f32)
        mo_ref[...] = mo.astype(bf16)
        x2 = x1 + p[5:6] * mo
        x2_ref[...] = x2
        hn_ref[...] = _normmod(x2, p[6:7], p[7:8], p[8:9]).astype(bf16)

    row = lambda width: pl.BlockSpec((MLP_TM, width), lambda i: (i, 0))
    return pl.pallas_call(
        body, name=name, grid=(T_LAT // MLP_TM,),
        out_shape=[jax.ShapeDtypeStruct((T_LAT, D), f32), jax.ShapeDtypeStruct((T_LAT, D), bf16),
                   jax.ShapeDtypeStruct((T_LAT, F), bf16), jax.ShapeDtypeStruct((T_LAT, D), bf16),
                   jax.ShapeDtypeStruct((T_LAT, D), f32), jax.ShapeDtypeStruct((T_LAT, D), bf16)],
        in_specs=[pl.BlockSpec((MLP_TM, D), lambda i: (i + xa_roff, 0)), row(D), pl.BlockSpec((8 * N_MLP_PARAMS, D), lambda i: (0, 0)),
                  _resident((N_DEV, None, D, FB), lambda i: (0, layer, 0, 0)),
                  _resident((N_DEV, None, FB, D), lambda i: (0, layer, 0, 0))],
        out_specs=[row(D), row(D), row(F), row(D), row(D), row(D)],
        compiler_params=_cparams("parallel"),
    )(xa, out_prev, par, w_in, w_out)


def _mlp_backward(dx2, x1, r, mo, out_prev, par, w_in, w_out, layer, name, after=()):
    nt = (((1,), (1,)), ((), ()))

    n_after = len(after)

    def body(dx2_ref, x1_ref, r_ref, mo_ref, op_ref, par_ref, win_ref, wout_ref, *rest):
        dx1_ref, dop_ref, dmo_ref, dhid_ref, acc_ref = rest[n_after:]
        p = _ParamRows(par_ref)
        dx2v = dx2_ref[...]
        dmo = (p[5:6] * dx2v).astype(bf16)
        dmo_ref[...] = dmo
        dh = jnp.zeros((MLP_TM, D), f32)
        mo = mo_ref[...].astype(f32)
        for j in range(N_DEV):
            rf = r_ref[:, j * FB:(j + 1) * FB].astype(f32)
            dact = lax.dot_general(dmo, wout_ref[j], nt, preferred_element_type=f32)
            dhid = (dact * (2.0 * rf)).astype(bf16)
            dhid_ref[:, j * FB:(j + 1) * FB] = dhid
            dh = dh + lax.dot_general(dhid, win_ref[j], nt, preferred_element_type=f32)
        x1 = x1_ref[...]
        _, vjp = jax.vjp(_normmod, x1, p[2:3], p[3:4], p[4:5])
        dx, dng, dsc, dsh = vjp(dh)
        dx1 = dx2v + dx
        dx1_ref[...] = dx1
        dop_ref[...] = (p[0:1] * dx1).astype(bf16)
        sums = _stack_rows([jnp.sum(dx1 * (op_ref[...] + p[1:2]), axis=0, keepdims=True),
                            p[0:1] * jnp.sum(dx1, axis=0, keepdims=True), dng, dsc, dsh,
                            jnp.sum(dx2v * mo, axis=0, keepdims=True)], 8)

        @pl.when(pl.program_id(0) == 0)
        def _():
            acc_ref[...] = jnp.zeros_like(acc_ref)

        acc_ref[...] += sums

    row = lambda width: pl.BlockSpec((MLP_TM, width), lambda i: (i, 0))
    return pl.pallas_call(
        body, name=name, grid=(T_LAT // MLP_TM,),
        out_shape=[jax.ShapeDtypeStruct((T_LAT, D), f32), jax.ShapeDtypeStruct((T_LAT, D), bf16),
                   jax.ShapeDtypeStruct((T_LAT, D), bf16), jax.ShapeDtypeStruct((T_LAT, F), bf16),
                   jax.ShapeDtypeStruct((8, D), f32)],
        in_specs=[row(D), row(D), row(F), row(D), row(D), pl.BlockSpec((8 * N_MLP_PARAMS, D), lambda i: (0, 0)),
                  _resident((N_DEV, None, D, FB), lambda i: (0, layer, 0, 0)),
                  _resident((N_DEV, None, FB, D), lambda i: (0, layer, 0, 0))] + [ANY_SPEC] * n_after,
        out_specs=[row(D), row(D), row(D), row(F), pl.BlockSpec((8, D), lambda i: (0, 0))],
        compiler_params=_cparams("arbitrary"),
    )(dx2, x1, r, mo, out_prev, par, w_in, w_out, *after)


def _mlp_weight_grads(h, dhid, r, dmo, layer, other, tag):
    tn = (((0,), (0,)), ((), ()))

    def body_in(h_ref, dhid_ref, *rest):
        rest[-1][...] = lax.dot_general(h_ref[...], dhid_ref[...], tn, preferred_element_type=f32).astype(bf16)

    def body_out(r_ref, dmo_ref, *rest):
        rf = r_ref[...].astype(f32)
        rest[-1][...] = lax.dot_general((rf * rf).astype(bf16), dmo_ref[...], tn,
                                        preferred_element_type=f32).astype(bf16)

    def call(body, name, operands, specs, block, prev):
        extra = [] if prev is None else [prev]
        return pl.pallas_call(
            body, name=name, grid=(N_DEV,), out_shape=jax.ShapeDtypeStruct((N_DEV, 2) + block, bf16),
            in_specs=specs + [pl.BlockSpec(memory_space=pl.ANY)] * len(extra),
            out_specs=pl.BlockSpec((None, None) + block, lambda j: (j, layer, 0, 0)),
            input_output_aliases={} if prev is None else {2: 0},
            compiler_params=_cparams("parallel"),
        )(*operands, *extra)

    dw_in = call(body_in, tag + "_mlp_in_dw", [h, dhid],
                 [_resident((T_LAT, D), lambda j: (0, 0)), pl.BlockSpec((T_LAT, FB), lambda j: (0, j))], (D, FB),
                 None if other is None else other[0])
    dw_out = call(body_out, tag + "_mlp_out_dw", [r, dmo],
                  [pl.BlockSpec((T_LAT, FB), lambda j: (0, j)), _resident((T_LAT, D), lambda j: (0, 0))], (FB, D),
                  None if other is None else other[1])
    return dw_in, dw_out


def _pos_embed():
    n_rows = T_LAT // GRID_W
    q = D // 4
    omega = 1.0 / (POS_BASE ** (jnp.arange(q, dtype=f32) / q))
    er = jnp.arange(n_rows, dtype=jnp.int32).astype(f32)[:, None] * omega[None, :]
    ec = jnp.arange(GRID_W, dtype=jnp.int32).astype(f32)[:, None] * omega[None, :]
    by_row = jnp.concatenate([jnp.sin(er), jnp.cos(er)], axis=-1)[:, None, :]
    by_col = jnp.concatenate([jnp.sin(ec), jnp.cos(ec)], axis=-1)[None, :, :]
    full = jnp.concatenate([jnp.broadcast_to(by_row, (n_rows, GRID_W, D // 2)),
                            jnp.broadcast_to(by_col, (n_rows, GRID_W, D // 2))], axis=-1)
    return full.reshape(T_LAT, D)


HALF = R // 2
BLK_PER_HALF = N_BLK // 2
N_PARTS = 4


def _gate_matrix(w_a, w_x):
    eye = jnp.eye(BLK_PER_HALF, dtype=bf16)
    cols = []
    for h in range(2):
        for d in range(2):
            for w in (w_a, w_x):
                blocks = w[d, BLK_PER_HALF * h:BLK_PER_HALF * (h + 1)].astype(bf16)
                cols.append(jnp.einsum("hij,hg->higj", blocks, eye).reshape(HALF, HALF))
    return jnp.concatenate(cols, axis=1)


def _gate_blocks(dwg, part):
    out = []
    for h in range(2):
        blk = dwg[:, (N_PARTS * h + part) * HALF:(N_PARTS * h + part + 1) * HALF]
        blk = blk.reshape(BLK_PER_HALF, BLK, BLK_PER_HALF, BLK)
        out.append(jnp.moveaxis(jnp.diagonal(blk, axis1=0, axis2=2), -1, 0))
    return jnp.concatenate(out, axis=0)


GATE_BM = 768


def _gates_dx(dpre, wg, after=()):
    rows = dpre.shape[0]
    n_after = len(after)

    def body(d_ref, w_ref, *rest):
        o_ref, acc_ref = rest[n_after:]
        p = pl.program_id(2)

        @pl.when(p == 0)
        def _():
            acc_ref[...] = jnp.zeros_like(acc_ref)

        acc_ref[...] += lax.dot_general(d_ref[...], w_ref[...], (((1,), (1,)), ((), ())), preferred_element_type=f32)

        @pl.when(p == N_PARTS - 1)
        def _():
            o_ref[...] = acc_ref[...]

    return pl.pallas_call(
        body, name="l0_gates_dx", grid=(rows // GATE_BM, 2, N_PARTS), out_shape=jax.ShapeDtypeStruct((rows, R), f32),
        in_specs=[pl.BlockSpec((GATE_BM, HALF), lambda i, h, p: (i, N_PARTS * h + p)),
                  pl.BlockSpec((HALF, HALF), lambda i, h, p: (0, N_PARTS * h + p))] + [ANY_SPEC] * n_after,
        out_specs=pl.BlockSpec((GATE_BM, HALF), lambda i, h, p: (i, h)),
        scratch_shapes=[pltpu.VMEM((GATE_BM, HALF), f32)],
        compiler_params=_cparams("parallel", "parallel", "arbitrary"),
    )(dpre, wg, *after)


COEFF_TM = 256


def _dir_params(d, *params):
    specs = [pl.BlockSpec((None, 1, HALF), lambda h, i: (d, 0, h))] * len(params)
    return specs, [p.reshape(2, 1, R) for p in params]


def _gates_coeff_fwd(ub, u, wg, ba, bx, lam, d):
    rows = u.shape[0]

    def body(ub_ref, u_ref, w_ref, ba_ref, bx_ref, lam_ref, a_ref, b_ref):
        pre = jnp.dot(ub_ref[...], w_ref[...], preferred_element_type=f32)
        a, b = _coeff(pre[:, :HALF], pre[:, HALF:], u_ref[...], ba_ref[...], bx_ref[...], lam_ref[...])
        a_ref[...] = a
        b_ref[...] = b

    tile = pl.BlockSpec((COEFF_TM, HALF), lambda h, i: (i, h))
    pspecs, pargs = _dir_params(d, ba, bx, lam)
    return pl.pallas_call(
        body, name=f"l0_gates_coeff_{d}", grid=(2, rows // COEFF_TM),
        out_shape=[jax.ShapeDtypeStruct((rows, R), f32)] * 2,
        in_specs=[tile, tile, pl.BlockSpec((HALF, 2 * HALF), lambda h, i: (0, 2 * h + d))] + pspecs,
        out_specs=[tile, tile], compiler_params=_cparams("parallel", "parallel"),
    )(ub, u, wg, *pargs)


def _gates_coeff_bwd(ub, u, dh, yp, wg, ba, bx, lam, d, dpre_prev):
    rows = u.shape[0]
    n_prev = 0 if dpre_prev is None else 1

    def body(ub_ref, u_ref, dh_ref, yp_ref, w_ref, ba_ref, bx_ref, lam_ref, *rest):
        dpre_ref, du_ref, dba_ref, dbx_ref, dlam_ref = rest[n_prev:]
        pre = jnp.dot(ub_ref[...], w_ref[...], preferred_element_type=f32)
        dhv = dh_ref[...]
        dpa, dpx, du, dba, dbx, dlam = _coeff_bwd(pre[:, :HALF], pre[:, HALF:], u_ref[...], ba_ref[...], bx_ref[...],
                                                  lam_ref[...], dhv * yp_ref[...], dhv)
        dpre_ref[:, :HALF] = dpa.astype(bf16)
        dpre_ref[:, HALF:] = dpx.astype(bf16)
        du_ref[...] = du

        @pl.when(pl.program_id(1) == 0)
        def _():
            dba_ref[...] = jnp.zeros_like(dba_ref)
            dbx_ref[...] = jnp.zeros_like(dbx_ref)
            dlam_ref[...] = jnp.zeros_like(dlam_ref)

        dba_ref[...] += dba
        dbx_ref[...] += dbx
        dlam_ref[...] += dlam

    tile = pl.BlockSpec((COEFF_TM, HALF), lambda h, i: (i, h))
    acc = pl.BlockSpec((1, HALF), lambda h, i: (0, h))
    pspecs, pargs = _dir_params(d, ba, bx, lam)
    extra = [] if dpre_prev is None else [dpre_prev]
    return pl.pallas_call(
        body, name=f"l0_gates_coeff_bwd_{d}", grid=(2, rows // COEFF_TM),
        out_shape=[jax.ShapeDtypeStruct((rows, 2 * N_PARTS * HALF), bf16), jax.ShapeDtypeStruct((rows, R), f32)]
        + [jax.ShapeDtypeStruct((1, R), f32)] * 3,
        in_specs=[tile] * 4 + [pl.BlockSpec((HALF, 2 * HALF), lambda h, i: (0, 2 * h + d))] + pspecs
        + [ANY_SPEC] * n_prev,
        out_specs=[pl.BlockSpec((COEFF_TM, 2 * HALF), lambda h, i: (i, 2 * h + d)), tile, acc, acc, acc],
        input_output_aliases={8: 0} if n_prev else {}, compiler_params=_cparams("parallel", "arbitrary"),
    )(ub, u, dh, yp, wg, *pargs, *extra)


def _gates_dw(u, dpre):
    rows = u.shape[0]

    def body(u_ref, d_ref, o_ref):
        o_ref[...] = lax.dot_general(u_ref[...], d_ref[...], (((0,), (0,)), ((), ())), preferred_element_type=f32)

    return pl.pallas_call(
        body, name="l0_gates_dw", grid=(2 * N_PARTS,), out_shape=jax.ShapeDtypeStruct((HALF, 2 * N_PARTS * HALF), f32),
        in_specs=[pl.BlockSpec((rows, HALF), lambda j: (0, j // N_PARTS)), pl.BlockSpec((rows, HALF), lambda j: (0, j))],
        out_specs=pl.BlockSpec((HALF, HALF), lambda j: (0, j)), compiler_params=_cparams("parallel"),
    )(u, dpre)


N_SCAN_CHUNKS = T_ALL // SCAN_CHUNK
SCAN_FWD = lambda t: t
SCAN_FWD_BWD = lambda t: N_SCAN_CHUNKS - 1 - t
SCAN_REV = lambda t: jnp.where(t == 0, 0, N_SCAN_CHUNKS - t)
SCAN_REV_BWD = lambda t: jnp.where(t == N_SCAN_CHUNKS - 1, 0, t + 1)
CONV_SEGMENTS = ((0, T_CTX), (T_CTX, T_LAT))
TM = 128
FUSED_TM = 256
N_CTX_TILES = T_CTX // TM


def _local_step(x, ctx, target, mods, cmod, wts, late_weights, send_grads, start_after=()):
    sh1, sc1, g1, sh2, sc2, g2 = [[mods[l, i][None] for l in range(2)] for i in range(N_MOD)]
    ng = wts["norm_g"]
    xcat = jnp.concatenate([ctx, x], axis=0)
    poscat = jnp.concatenate([jnp.zeros((T_CTX, D), f32), _pos_embed()], axis=0)
    scp = jnp.concatenate([cmod[1][None], sc1[0]], axis=0)
    shp = jnp.concatenate([cmod[0][None], sh1[0]], axis=0)

    def blend(i, p):
        sel = jnp.where(i < N_CTX_TILES, 1.0, 0.0)
        return sel * p[0:1] + (1.0 - sel) * p[1:2]

    def f_pre0(i, xc, pos, g, scp_, shp_):
        x0 = xc + pos
        return x0, _normmod(x0, g, blend(i, scp_), blend(i, shp_))

    x0cat, h0 = _rowcall(f_pre0, "l0_prenorm", T_ALL, TM, [_rin(xcat), _rin(poscat)], [ng[0, 0][None], scp, shp],
                         [(D, f32), (D, bf16)], after=start_after)
    gr = _mm(h0, wts["rec_w_in"], "l0_in_proj")
    u, ub = _dwconv_fwd(gr, R // 256, wts["rec_conv_w"], wts["rec_conv_b"], 4, 1, CONV_SEGMENTS, 256,
                        "l0_conv", True)
    gate_args = (wts["gates"], wts["rec_b_a"], wts["rec_b_x"], wts["rec_lambda"])
    a0, b0 = _gates_coeff_fwd(ub, u, *gate_args, 0)
    a1, b1 = _gates_coeff_fwd(ub, u, *gate_args, 1)
    y0, yp0 = _scan_call(a0, b0, SCAN_FWD, False, "l0_scan_fwd", False)
    y1, yp1 = _scan_call(a1, b1, SCAN_REV, True, "l0_scan_rev", False)

    def f_gate(i, gp, y0_, y1_):
        return (_gelu(gp) * (y0_ + y1_),)

    (zb,) = _rowcall(f_gate, "l0_gate", T_LAT, TM,
                     [_rin(gr, R, 0, N_CTX_TILES), _rin(y0, None, 0, N_CTX_TILES), _rin(y1, None, 0, N_CTX_TILES)],
                     [], [(R, bf16)])
    wts = dict(wts, **late_weights("mlp", zb))
    out0 = _mm(zb, wts["rec_w_out"], "l0_out_proj")

    zero_d = jnp.zeros((1, D), f32)

    def mlp_params(rows):
        rows = rows + [zero_d] * (N_MLP_PARAMS - len(rows))
        return jnp.concatenate([jnp.broadcast_to(r, (8, D)) for r in rows], axis=0)

    par0 = mlp_params([g1[0], zero_d, ng[0, 1][None], sc2[0], sh2[0], g2[0], ng[1, 0][None], sc1[1], sh1[1]])
    x1, h1, r0, mo0, x2, h2 = _mlp_forward(x0cat, T_CTX // MLP_TM, out0, par0, wts["mlp_w_in"], wts["mlp_w_out"], 0,
                                           "l0_mlp")

    wts = dict(wts, **late_weights("conf", x2))
    def glu(pa, pb, b1):
        return (pa + b1[:, :D]) * _sigmoid(pb + b1[:, D:])

    def f_pw1_glu(i, h_, b1, w):
        p = jnp.dot(h_, w, preferred_element_type=f32)
        return glu(p[:, :D], p[:, D:], b1), p

    zg, pw = _rowcall(f_pw1_glu, "l1_pw1_glu", T_LAT, FUSED_TM, [_rin(h2)], [wts["conf_b_pw1"], wts["conf_w_pw1"]],
                      [(D, f32), (2 * D, bf16)])
    (zc,) = _dwconv_fwd(zg, 0, wts["conf_conv_w"], wts["conf_conv_b"], 31, 15, ((0, T_LAT),), 128, "l1_conv", False)

    def ln_silu(z, lg, lb):
        mu = jnp.mean(z, axis=-1, keepdims=True)
        zc_ = z - mu
        var = jnp.mean(zc_ * zc_, axis=-1, keepdims=True)
        yv = zc_ * lax.rsqrt(var + EPS) * lg + lb
        return yv * _sigmoid(yv)

    def f_lnsilu_pw2(i, z, lg, lb, w):
        s = ln_silu(z, lg, lb).astype(bf16)
        return s, jnp.dot(s, w, preferred_element_type=f32)

    sb, out1 = _rowcall(f_lnsilu_pw2, "l1_ln_silu_pw2", T_LAT, FUSED_TM, [_rin(zc)],
                        [wts["conf_ln_g"], wts["conf_ln_b"], wts["conf_w_pw2"]], [(D, bf16), (D, f32)])
    par1 = mlp_params([g1[1], wts["conf_b_pw2"], ng[1, 1][None], sc2[1], sh2[1], g2[1]])
    x3, h3, r1, mo1, x4, _ = _mlp_forward(x2, 0, out1, par1, wts["mlp_w_in"], wts["mlp_w_out"], 1, "l1_mlp")

    def loss_fn(x4_, fg, tgt):
        err = _rms(x4_, fg) - tgt
        per_row = jnp.mean(err * err, axis=-1, keepdims=True)
        return 0.5 * jnp.sum(per_row, axis=0, keepdims=True)

    def f_head(i, x4_, tgt, fg):
        loss, vjp = jax.vjp(lambda a, e: loss_fn(a, e, tgt), x4_, fg)
        dx, dfg = vjp(jnp.ones((1, 1), f32))
        return dx, jnp.broadcast_to(loss, (1, 128)), dfg

    dx4, loss_acc, dfinal_g = _rowcall(f_head, "head", T_LAT, TM, [_rin(x4), _rin(target)], [wts["final_g"]],
                                       [(D, f32)], [(1, 128), (1, D)])

    grads = {"final_g": dfinal_g}

    dx3, dout1, dmo1, dhid1, acc1 = _mlp_backward(dx4, x3, r1, mo1, out1, par1, wts["mlp_w_in"], wts["mlp_w_out"], 1,
                                                  "l1_mlp_bwd")
    mlp_dw = _mlp_weight_grads(h3, dhid1, r1, dmo1, 1, None, "l1")
    dg1_1, db_pw2, dng11, dsc2_1, dsh2_1, dg2_1 = [acc1[k:k + 1] for k in range(6)]

    nt = (((1,), (1,)), ((), ()))
    grads["conf_w_pw2"] = _mm(sb, dout1, "l1_pw2_dw", ta=True, out_dtype=bf16)
    grads["conf_b_pw2"] = db_pw2

    def f_pw2_lnsilu_bwd(i, z, dout, lg, lb, w):
        ds = lax.dot_general(dout, w, nt, preferred_element_type=f32)
        _, vjp = jax.vjp(ln_silu, z, lg, lb)
        return vjp(ds)

    dzc, dln_g, dln_b = _rowcall(f_pw2_lnsilu_bwd, "l1_pw2_ln_silu_bwd", T_LAT, FUSED_TM, [_rin(zc), _rin(dout1)],
                                 [wts["conf_ln_g"], wts["conf_ln_b"], wts["conf_w_pw2"]], [(D, f32)], [(1, D)] * 2)
    grads["conf_ln_g"], grads["conf_ln_b"] = dln_g, dln_b
    dzg, dconv_w, dconv_b = _dwconv_bwd([dzc], zg, 0, wts["conf_conv_w"], 31, 15, ((0, T_LAT),), 128,
                                        "l1_conv_bwd", f32)
    grads["conf_conv_w"], grads["conf_conv_b"] = dconv_w, dconv_b

    def f_glu_pw1_norm_bwd(i, p_, dz, x_, dxs, b1, g_, sc_, sh_, w):
        pf = p_.astype(f32)
        _, vjp = jax.vjp(glu, pf[:, :D], pf[:, D:], b1)
        da, db, db1 = vjp(dz)
        dp = jnp.concatenate([da, db], axis=1).astype(bf16)
        dh = lax.dot_general(dp, w, nt, preferred_element_type=f32)
        _, vjp = jax.vjp(_normmod, x_, g_, sc_, sh_)
        dx, dg, dsc, dsh = vjp(dh)
        return dp, dx + dxs, db1, dg, dsc, dsh

    dpw, dx2, db_pw1, dng10, dsc1_1, dsh1_1 = _rowcall(
        f_glu_pw1_norm_bwd, "l1_glu_pw1_normmod_bwd", T_LAT, FUSED_TM, [_rin(pw), _rin(dzg), _rin(x2), _rin(dx3)],
        [wts["conf_b_pw1"], ng[1, 0][None], sc1[1], sh1[1], wts["conf_w_pw1"]], [(2 * D, bf16), (D, f32)],
        [(1, 2 * D), (1, D), (1, D), (1, D)])
    grads["conf_b_pw1"] = db_pw1
    grads["conf_w_pw1"] = _mm(h2, dpw, "l1_pw1_dw", ta=True, out_dtype=bf16)
    sent = send_grads(["conf_w_pw2", "conf_w_pw1"], grads)

    dx1, dout0, dmo0, dhid0, acc0 = _mlp_backward(dx2, x1, r0, mo0, out0, par0, wts["mlp_w_in"], wts["mlp_w_out"], 0,
                                                  "l0_mlp_bwd", after=[sent])
    grads["mlp_w_in"], grads["mlp_w_out"] = _mlp_weight_grads(h1, dhid0, r0, dmo0, 0, mlp_dw, "l0")
    sent = send_grads(["mlp_w_in", "mlp_w_out"], grads)
    dg1_0, _, dng01, dsc2_0, dsh2_0, dg2_0 = [acc0[k:k + 1] for k in range(6)]

    dz = _mm(dout0, wts["rec_w_out"], "l0_out_proj_dx", tb=True, after=[sent])
    grads["rec_w_out"] = _mm(zb, dout0, "l0_out_proj_dw", ta=True, out_dtype=bf16)
    sent = send_grads(["rec_w_out"], grads)

    def f_gate_bwd(i, gp, y0_, y1_, dz_):
        lat = jnp.where(i < N_CTX_TILES, 0.0, 1.0)
        _, vjp = jax.vjp(lambda a, b: _gelu(a) * b, gp, y0_ + y1_)
        dgp, dy = vjp(dz_)
        return dgp * lat, dy * lat

    dgp, dy = _rowcall(f_gate_bwd, "l0_gate_bwd", T_ALL, TM,
                       [_rin(gr, R, 0), _rin(y0), _rin(y1), _rin(dz, None, 0, -N_CTX_TILES)], [],
                       [(R, bf16), (R, f32)], after=[sent])
    (dh_f,) = _scan_call(a0, dy, SCAN_FWD_BWD, True, "l0_scan_fwd_bwd", True)
    (dh_r,) = _scan_call(a1, dy, SCAN_REV_BWD, False, "l0_scan_rev_bwd", True)

    dpre, du_f, *dpar_f = _gates_coeff_bwd(ub, u, dh_f, yp0, *gate_args, 0, None)
    dpre, du_r, *dpar_r = _gates_coeff_bwd(ub, u, dh_r, yp1, *gate_args, 1, dpre)
    grads["rec_b_a"], grads["rec_b_x"], grads["rec_lambda"] = [
        jnp.concatenate([f.reshape(-1), r_.reshape(-1)]).reshape(2, R) for f, r_ in zip(dpar_f, dpar_r)]
    grads["gates"] = _gates_dw(ub, dpre)
    sent = send_grads(["replicated"], grads)
    du_gates = _gates_dx(dpre, wts["gates"], after=[sent])
    drec, dconv4_w, dconv4_b = _dwconv_bwd([du_f, du_r, du_gates], gr, R // 256, wts["rec_conv_w"], 4, 1,
                                           CONV_SEGMENTS, 256, "l0_conv_bwd", bf16)
    grads["rec_conv_w"], grads["rec_conv_b"] = dconv4_w, dconv4_b
    dgr = jnp.concatenate([dgp, drec], axis=1)
    grads["rec_w_in"] = _mm(h0, dgr, "l0_in_proj_dw", ta=True, out_dtype=bf16)
    dh0 = _mm(dgr, wts["rec_w_in"], "l0_in_proj_dx", tb=True, after=[send_grads(["rec_w_in"], grads)])

    def f_pre0_bwd(i, x0, dh_, dxs, g, scp_, shp_):
        lat = jnp.where(i < N_CTX_TILES, 0.0, 1.0)
        _, vjp = jax.vjp(lambda a, b, c, e: _normmod(a, b, blend(i, c), blend(i, e)), x0, g, scp_, shp_)
        dx, dg, dscp, dshp = vjp(dh_)
        return dx + lat * dxs, dg, dscp, dshp

    dx0cat, dng00, dscp, dshp = _rowcall(
        f_pre0_bwd, "l0_prenorm_bwd", T_ALL, TM, [_rin(x0cat), _rin(dh0), _rin(dx1, None, 0, -N_CTX_TILES)],
        [ng[0, 0][None], scp, shp], [(D, f32)], [(1, D), (2, D), (2, D)])

    grads["norm_g"] = jnp.stack([jnp.concatenate([dng00, dng01], 0), jnp.concatenate([dng10, dng11], 0)])
    dmods = jnp.stack([
        jnp.concatenate([dshp[1:2], dscp[1:2], dg1_0, dsh2_0, dsc2_0, dg2_0], axis=0),
        jnp.concatenate([dsh1_1, dsc1_1, dg1_1, dsh2_1, dsc2_1, dg2_1], axis=0)])
    dcmod = jnp.concatenate([dshp[0:1], dscp[0:1]], axis=0)
    return loss_acc[0, 0], dx0cat[T_CTX:], dmods, dcmod, grads


def _unshard_cols(g):
    g = jnp.moveaxis(g, 0, -2)
    return g.reshape(g.shape[:-2] + (g.shape[-2] * g.shape[-1],))


def _shard_cols(w):
    w = w.reshape(w.shape[:-1] + (N_DEV, w.shape[-1] // N_DEV))
    return jnp.moveaxis(w, -2, 0)


def _shard_rows(w):
    return w.reshape((N_DEV, w.shape[0] // N_DEV) + w.shape[1:])


SMALL_PACK_ROWS = 64


def kernel(x, c, ctx, c_ctx, w_ada, b_ada, norm_g, rec_w_in, rec_conv_w, rec_conv_b, rec_lambda, rec_w_a, rec_b_a, rec_w_x, rec_b_x, rec_w_out, conf_w_pw1, conf_b_pw1, conf_conv_w, conf_conv_b, conf_ln_g, conf_ln_b, conf_w_pw2, conf_b_pw2, mlp_w_in, mlp_w_out, final_g, loss_target, m_c_ctx, m_w_ada, m_b_ada, m_norm_g, m_rec_w_in, m_rec_conv_w, m_rec_conv_b, m_rec_lambda, m_rec_w_a, m_rec_b_a, m_rec_w_x, m_rec_b_x, m_rec_w_out, m_conf_w_pw1, m_conf_b_pw1, m_conf_conv_w, m_conf_conv_b, m_conf_ln_g, m_conf_ln_b, m_conf_w_pw2, m_conf_b_pw2, m_mlp_w_in, m_mlp_w_out, m_final_g, v_c_ctx, v_w_ada, v_b_ada, v_norm_g, v_rec_w_in, v_rec_conv_w, v_rec_conv_b, v_rec_lambda, v_rec_w_a, v_rec_b_a, v_rec_w_x, v_rec_b_x, v_rec_w_out, v_conf_w_pw1, v_conf_b_pw1, v_conf_conv_w, v_conf_conv_b, v_conf_ln_g, v_conf_ln_b, v_conf_w_pw2, v_conf_b_pw2, v_mlp_w_in, v_mlp_w_out, v_final_g):
    me = 4 * lax.axis_index("x") + 2 * lax.axis_index("y") + lax.axis_index("c")
    weights = dict(c_ctx=c_ctx, w_ada=w_ada, b_ada=b_ada, norm_g=norm_g, rec_w_in=rec_w_in, rec_conv_w=rec_conv_w,
                   rec_conv_b=rec_conv_b, rec_lambda=rec_lambda, rec_w_a=rec_w_a, rec_b_a=rec_b_a, rec_w_x=rec_w_x,
                   rec_b_x=rec_b_x, rec_w_out=rec_w_out, conf_w_pw1=conf_w_pw1, conf_b_pw1=conf_b_pw1,
                   conf_conv_w=conf_conv_w, conf_conv_b=conf_conv_b, conf_ln_g=conf_ln_g, conf_ln_b=conf_ln_b,
                   conf_w_pw2=conf_w_pw2, conf_b_pw2=conf_b_pw2, mlp_w_in=mlp_w_in, mlp_w_out=mlp_w_out, final_g=final_g)
    m_in = dict(c_ctx=m_c_ctx, w_ada=m_w_ada, b_ada=m_b_ada, norm_g=m_norm_g, rec_w_in=m_rec_w_in, rec_conv_w=m_rec_conv_w,
                rec_conv_b=m_rec_conv_b, rec_lambda=m_rec_lambda, rec_w_a=m_rec_w_a, rec_b_a=m_rec_b_a, rec_w_x=m_rec_w_x,
                rec_b_x=m_rec_b_x, rec_w_out=m_rec_w_out, conf_w_pw1=m_conf_w_pw1, conf_b_pw1=m_conf_b_pw1,
                conf_conv_w=m_conf_conv_w, conf_conv_b=m_conf_conv_b, conf_ln_g=m_conf_ln_g, conf_ln_b=m_conf_ln_b,
                conf_w_pw2=m_conf_w_pw2, conf_b_pw2=m_conf_b_pw2, mlp_w_in=m_mlp_w_in, mlp_w_out=m_mlp_w_out,
                final_g=m_final_g)
    v_in = dict(c_ctx=v_c_ctx, w_ada=v_w_ada, b_ada=v_b_ada, norm_g=v_norm_g, rec_w_in=v_rec_w_in, rec_conv_w=v_rec_conv_w,
                rec_conv_b=v_rec_conv_b, rec_lambda=v_rec_lambda, rec_w_a=v_rec_w_a, rec_b_a=v_rec_b_a, rec_w_x=v_rec_w_x,
                rec_b_x=v_rec_b_x, rec_w_out=v_rec_w_out, conf_w_pw1=v_conf_w_pw1, conf_b_pw1=v_conf_b_pw1,
                conf_conv_w=v_conf_conv_w, conf_conv_b=v_conf_conv_b, conf_ln_g=v_conf_ln_g, conf_ln_b=v_conf_ln_b,
                conf_w_pw2=v_conf_w_pw2, conf_b_pw2=v_conf_b_pw2, mlp_w_in=v_mlp_w_in, mlp_w_out=v_mlp_w_out,
                final_g=v_final_g)
    names = list(weights)

    small_items = [c, norm_g, rec_conv_w, rec_lambda, conf_b_pw1, conf_conv_w, conf_conv_b, conf_ln_g, conf_ln_b,
                   conf_b_pw2]
    flat = jnp.concatenate([a.reshape(-1) for a in small_items])
    flat = jnp.pad(flat, (0, SMALL_PACK_ROWS * 128 - flat.shape[0])).reshape(SMALL_PACK_ROWS, 128)
    (small_all,) = _all_gather([flat], "gather_small")

    small_all = small_all.reshape(N_DEV, -1)
    off = 0
    small = []
    for a in small_items:
        small.append(small_all[:, off:off + a.size].reshape((N_DEV,) + a.shape))
        off += a.size
    c_all, ng_all, rcw_all, lam_all, bpw1_all, ccw_all, ccb_all, lng_all, lnb_all, bpw2_all = small
    wts = {
        "norm_g": _unshard_cols(ng_all),
        "rec_conv_w": _unshard_cols(rcw_all)[0],
        "rec_lambda": _unshard_cols(lam_all)[0],
        "conf_b_pw1": _unshard_cols(bpw1_all),
        "conf_conv_w": _unshard_cols(ccw_all)[0],
        "conf_conv_b": _unshard_cols(ccb_all),
        "conf_ln_g": _unshard_cols(lng_all),
        "conf_ln_b": _unshard_cols(lnb_all),
        "conf_b_pw2": _unshard_cols(bpw2_all),
        "rec_conv_b": rec_conv_b,
        "rec_b_a": rec_b_a[0].reshape(2, R),
        "rec_b_x": rec_b_x[0].reshape(2, R),
        "final_g": final_g[None],
        "gates": _gate_matrix(rec_w_a[0], rec_w_x[0]),
    }

    c16 = jnp.concatenate([c_all[:, 0], jnp.broadcast_to(c_ctx[None], (8, D))], axis=0)
    b_loc = lax.dynamic_slice_in_dim(b_ada, me * ADA_SHARD, ADA_SHARD, axis=1)[:, None]
    (mods_all,) = _all_gather([_ada_forward(c16, w_ada, b_loc)], "gather_mods")
    mods_all = _unshard_cols(mods_all)
    mods = lax.dynamic_index_in_dim(mods_all, me, axis=1, keepdims=False).reshape(2, N_MOD, D)
    cmod = mods_all[0, 8, :2 * D].reshape(2, D)

    as_shard = lambda a: a.astype(bf16).reshape(-1, a.shape[-1])
    early = _all_gather_2level([as_shard(rec_w_in[0])], "gather_weights_early")
    wts["rec_w_in"] = _unshard_cols(early[0])
    late_items = {"mlp": [rec_w_out[0], mlp_w_in, mlp_w_out], "conf": [conf_w_pw1[0], conf_w_pw2[0]]}
    late_handles, order = {}, [early[0], mods]
    for group in ("mlp", "conf"):
        shards = [as_shard(a) for a in late_items[group]]
        lands = [_own_block_filled(s, me) for s in shards]
        if group == "mlp":
            late_handles[group], token = _chip_gather_start(shards, lands, "gather_mlp_start", after=order)
        else:
            late_handles[group], token = _exchange_start(shards, lands, "gather_conf_start", False, after=order)
        order = [token]

    def late_weights(group, after):
        if group == "mlp":
            forwarded = _chip_gather_forward(late_handles[group], after, "gather_mlp_forward")
            got = _chip_gather_wait(forwarded, after, "gather_mlp_wait")
        else:
            got = _exchange_wait(late_handles[group], after, "gather_conf_wait", False)
        got = [g.reshape((N_DEV,) + a.shape) for g, a in zip(got, late_items[group])]
        if group == "mlp":
            return {"rec_w_out": got[0].reshape(R, D), "mlp_w_in": got[1], "mlp_w_out": got[2]}
        return {"conf_w_pw1": _unshard_cols(got[0]), "conf_w_pw2": got[1].reshape(D, D)}

    to_blocks = {"rec_w_in": _shard_cols, "conf_w_pw1": _shard_cols, "rec_w_out": _shard_rows, "conf_w_pw2": _shard_rows,
                 "mlp_w_in": lambda g: g, "mlp_w_out": lambda g: g}
    grad_handles = []

    repl_names = ["rec_w_a", "rec_w_x", "rec_b_a", "rec_b_x", "final_g"]

    def send_replicated(grads):
        dwg = grads["gates"]
        repl = {"rec_w_a": jnp.stack([_gate_blocks(dwg, 0), _gate_blocks(dwg, 2)]),
                "rec_w_x": jnp.stack([_gate_blocks(dwg, 1), _gate_blocks(dwg, 3)]),
                "rec_b_a": grads["rec_b_a"], "rec_b_x": grads["rec_b_x"], "final_g": grads["final_g"]}
        flat = jnp.concatenate([repl[n].reshape(-1) for n in repl_names])
        rows = -(-flat.shape[0] // (16 * D)) * 16
        flat = jnp.pad(flat, (0, rows * D - flat.shape[0])).reshape(rows, D).astype(bf16)
        handle, sent = _exchange_start([flat], [_own_block_filled(flat, me)], "gather_replicated_start", False)
        grad_handles.append((["replicated"], handle))
        return sent

    def send_grads(group, grads):
        if group == ["replicated"]:
            return send_replicated(grads)
        blocks = [to_blocks[n](grads[n]) for n in group]
        blocks = [g.reshape(N_DEV, -1, g.shape[-1]) for g in blocks]
        lands = [_own_block_filled(lax.dynamic_index_in_dim(g, me, 0, keepdims=False), me) for g in blocks]
        handle, sent = _exchange_start(blocks, lands, "scatter_start_" + group[0], True)
        grad_handles.append((group, handle))
        return sent

    loss_part, grad_x, dmods, dcmod, grads = _local_step(x[0], ctx[0], loss_target[0], mods, cmod, wts, late_weights,
                                                         send_grads, start_after=order)
    loss = lax.psum(loss_part, ("x", "y", "c"))

    def as2d(shape):
        rows = 1
        for s in shape[:-1]:
            rows *= s
        return (rows, shape[-1])

    def whole(arr, shape):
        arr = arr.reshape((-1,) + as2d(shape))
        return (arr, arr.shape[0])

    shard_shapes = {n: weights[n].shape for n in names}
    g_out, d_out, m_out, v_out = {}, {}, {}, {}

    def adamw(n, pieces, after):
        shape = shard_shapes[n]
        r2, c2 = as2d(shape)
        g, dl, nm, nv = _adamw(pieces, weights[n].reshape(r2, c2), m_in[n].reshape(r2, c2), v_in[n].reshape(r2, c2),
                               "adamw_" + n, after=after)
        g_out[n], d_out[n], m_out[n], v_out[n] = (t.reshape(shape) for t in (g, dl, nm, nv))
        return g

    small_sharded = ["norm_g", "rec_conv_w", "rec_lambda", "conf_b_pw1", "conf_conv_w", "conf_conv_b", "conf_ln_g",
                     "conf_ln_b", "conf_b_pw2"]
    pack = jnp.concatenate([_shard_cols(grads[n]).reshape(N_DEV, -1) for n in small_sharded], axis=1)
    pack = jnp.pad(pack, ((0, 0), (0, SMALL_PACK_ROWS * 128 - pack.shape[1]))).reshape(N_DEV, SMALL_PACK_ROWS, 128)
    small_handle, token = _exchange_start(
        [pack], [_own_block_filled(lax.dynamic_index_in_dim(pack, me, 0, keepdims=False), me)], "scatter_small_start",
        True, after=[grad_x])
    dm_flat = jnp.concatenate([dmods.reshape(-1), dcmod.reshape(-1), grads["rec_conv_b"].reshape(-1)])
    dm_len = dm_flat.shape[0]
    dm_flat = jnp.pad(dm_flat, (0, 128 * 128 - dm_len)).reshape(128, 128)
    dm_handle, token = _exchange_start([dm_flat], [_own_block_filled(dm_flat, me)], "gather_dmods_start", False,
                                       after=[token])

    done = token
    for group, handle in grad_handles:
        if group == ["replicated"]:
            repl_all = _exchange_wait(handle, done, "gather_replicated_wait", False)[0].reshape(N_DEV, -1)
            off = 0
            for n in repl_names:
                size = weights[n].size
                done = adamw(n, [whole(repl_all[:, off:off + size], shard_shapes[n])], [done])
                off += size
            continue
        for n, got in zip(group, _exchange_wait(handle, done, "scatter_wait_" + group[0], True)):
            done = adamw(n, [(got, N_DEV)], [done])

    dm_all = _exchange_wait(dm_handle, done, "gather_dmods_wait", False)[0].reshape(N_DEV, -1)
    dmods_all = dm_all[:, :2 * N_MOD * D].reshape(N_DEV, 2, N_MOD * D)
    dcmod_all = jnp.pad(dm_all[:, 2 * N_MOD * D:2 * N_MOD * D + 2 * D], ((0, 0), (0, (N_MOD - 2) * D)))
    g16_full = jnp.stack([jnp.concatenate([dmods_all[:, 0], dcmod_all], axis=0),
                          jnp.concatenate([dmods_all[:, 1], jnp.zeros_like(dcmod_all)], axis=0)])
    g16 = lax.dynamic_slice_in_dim(g16_full, me * ADA_SHARD, ADA_SHARD, axis=2)
    dw_ada, ds_part = _ada_backward(c16, g16, w_ada)
    ds_handle, token = _exchange_start([ds_part[0]], [_own_block_filled(ds_part[0], me)], "gather_dsilu_start", False)
    done = adamw("w_ada", [whole(dw_ada, shard_shapes["w_ada"])], [token])
    done = adamw("rec_conv_b", [whole(dm_all[:, dm_len - R:dm_len], shard_shapes["rec_conv_b"])], [done])
    db_terms = jnp.concatenate([dmods_all, jnp.stack([dcmod_all, jnp.zeros_like(dcmod_all)], axis=1)], axis=0)
    done = adamw("b_ada", [whole(db_terms, shard_shapes["b_ada"])], [done])
    pack_recv = _exchange_wait(small_handle, done, "scatter_small_wait", True)[0].reshape(N_DEV, -1)
    off = 0
    for n in small_sharded:
        size = weights[n].size
        done = adamw(n, [whole(pack_recv[:, off:off + size], shard_shapes[n])], [done])
        off += size
    ds_all = _exchange_wait(ds_handle, done, "gather_dsilu_wait", False)[0]
    adamw("c_ctx", [whole(ds_all[:, 0], shard_shapes["c_ctx"])], [])

    return (loss, grad_x[None], *[g_out[n] for n in names], *[d_out[n] for n in names],
            *[m_out[n] for n in names], *[v_out[n] for n in names])
```

```python
import functools

import jax
import jax.numpy as jnp
from jax import lax
from jax.experimental import pallas as pl
from jax.experimental.pallas import tpu as pltpu

f32 = jnp.float32
bf16 = jnp.bfloat16

N_DEV = 8
D = 1024
T_LAT = 2048
T_CTX = 256
T_ALL = T_CTX + T_LAT
R = 1280
N_BLK = 16
BLK = R // N_BLK
F = 4096
GRID_W = 64
RG_C = 8.0
EPS = 1e-6
POS_BASE = 10000.0
N_MOD = 6
ADA_SHARD = N_MOD * D // N_DEV

ADAM_LR = 0.001
ADAM_B1 = 0.9
ADAM_B2 = 0.999
ADAM_EPS = 1e-08
ADAM_WD = 0.01
ADAM_STEP = 10

VMEM_LIMIT_V7X = 56 * 1024 * 1024
HALO = 16
MESH = pl.DeviceIdType.MESH


def _cparams(*sem):
    return pltpu.CompilerParams(dimension_semantics=sem, vmem_limit_bytes=VMEM_LIMIT_V7X)


def _pick(n, cands):
    for c in cands:
        if n % c == 0:
            return c
    raise ValueError(f"no block size for {n}")


def _position():
    x, y, c = lax.axis_index("x"), lax.axis_index("y"), lax.axis_index("c")
    return x, y, c, 4 * x + 2 * y + c


def _peer(x, y, c, k):
    px = (1 - x) if (k >> 2) & 1 else x
    py = (1 - y) if (k >> 1) & 1 else y
    pc = (1 - c) if k & 1 else c
    return (px, py, pc), 4 * px + 2 * py + pc


def _exchange(arrs, name, scatter):
    n = len(arrs)

    def body(*refs):
        ins, outs = refs[:n], refs[n:2 * n]
        send_sems, recv_sems, local_sems = refs[2 * n:]
        x, y, c, me = _position()
        local = []
        for a in range(n):
            src = ins[a].at[me] if scatter else ins[a]
            cp = pltpu.make_async_copy(src, outs[a].at[me], local_sems.at[a])
            cp.start()
            local.append(cp)
        sends, recvs = [], []
        for a in range(n):
            for k in range(1, N_DEV):
                peer, peer_lin = _peer(x, y, c, k)
                src = ins[a].at[peer_lin] if scatter else ins[a]
                cp = pltpu.make_async_remote_copy(
                    src_ref=src, dst_ref=outs[a].at[me], send_sem=send_sems.at[a, k - 1],
                    recv_sem=recv_sems.at[a, k - 1], device_id=peer, device_id_type=MESH)
                cp.start()
                sends.append(cp)
                recvs.append(pltpu.make_async_remote_copy(
                    src_ref=src, dst_ref=outs[a].at[peer_lin], send_sem=send_sems.at[a, k - 1],
                    recv_sem=recv_sems.at[a, k - 1], device_id=peer, device_id_type=MESH))
        for cp in recvs:
            cp.wait_recv()
        for cp in sends:
            cp.wait_send()
        for cp in local:
            cp.wait()

    if scatter:
        out_shape = [jax.ShapeDtypeStruct(a.shape, a.dtype) for a in arrs]
    else:
        out_shape = [jax.ShapeDtypeStruct((N_DEV,) + a.shape, a.dtype) for a in arrs]
    any_spec = pl.BlockSpec(memory_space=pl.ANY)
    return pl.pallas_call(
        body, name=name, out_shape=out_shape,
        in_specs=[any_spec] * n, out_specs=[any_spec] * n,
        scratch_shapes=[pltpu.SemaphoreType.DMA((n, N_DEV - 1)), pltpu.SemaphoreType.DMA((n, N_DEV - 1)),
                        pltpu.SemaphoreType.DMA((n,))],
    )(*arrs)


def _all_gather(arrs, name):
    return _exchange(arrs, name, scatter=False)


def _lin(p):
    return 4 * p[0] + 2 * p[1] + p[2]


HBM_SPEC = pl.BlockSpec(memory_space=pltpu.HBM)
SEM_SPEC = pl.BlockSpec(memory_space=pltpu.SEMAPHORE)
DATAFLOW_EFFECT = pltpu.SideEffectType.DATAFLOW_SIDE_EFFECTING


def _split_copies(srcs, lands, send_sems, recv_sems, scatter):
    x, y, c, me = _position()
    out = []
    for a in range(len(srcs)):
        for k in range(1, N_DEV):
            peer, peer_lin = _peer(x, y, c, k)
            src = srcs[a].at[peer_lin] if scatter else srcs[a]
            mk = lambda slot: pltpu.make_async_remote_copy(
                src_ref=src, dst_ref=lands[a].at[slot], send_sem=send_sems.at[a * (N_DEV - 1) + k - 1],
                recv_sem=recv_sems.at[a * (N_DEV - 1) + k - 1], device_id=peer, device_id_type=MESH)
            out.append((mk(me), mk(peer_lin)))
    return out


def _exchange_start(srcs, lands, name, scatter, after=()):
    n = len(srcs)
    n_after = len(after)

    def body(*refs):
        srcs_r, lands_r = refs[:n], refs[n:2 * n]
        send_sems, recv_sems = refs[2 * n + n_after], refs[2 * n + n_after + 1]
        token = refs[-1]
        for outgoing, _ in _split_copies(srcs_r, lands_r, send_sems, recv_sems, scatter):
            outgoing.start()
        token[...] = jnp.zeros_like(token)

    hbm = lambda a: pltpu.HBM(a.shape, a.dtype)
    res = pl.pallas_call(
        body, name=name,
        out_shape=(pltpu.SemaphoreType.DMA((n * (N_DEV - 1),)), pltpu.SemaphoreType.DMA((n * (N_DEV - 1),)),
                   *[hbm(a) for a in srcs], *[hbm(a) for a in lands], jax.ShapeDtypeStruct((8, 128), f32)),
        in_specs=[HBM_SPEC] * (2 * n) + [pl.BlockSpec(memory_space=pl.ANY)] * n_after,
        out_specs=(SEM_SPEC, SEM_SPEC, *[HBM_SPEC] * (2 * n), pl.BlockSpec(memory_space=pltpu.VMEM)),
        input_output_aliases={i: 2 + i for i in range(2 * n)},
        compiler_params=pltpu.CompilerParams(has_side_effects=DATAFLOW_EFFECT),
    )(*[pltpu.with_memory_space_constraint(a, pltpu.HBM) for a in list(srcs) + list(lands)], *after)
    return (res[0], res[1], list(res[2:2 + n]), list(res[2 + n:2 + 2 * n])), res[-1]


def _exchange_wait(handle, after, name, scatter):
    send_sems, recv_sems, srcs, lands = handle
    n = len(srcs)

    def body(*refs):
        srcs_r, lands_r = refs[:n], refs[n:2 * n]
        send_s, recv_s = refs[2 * n], refs[2 * n + 1]
        for outgoing, incoming in _split_copies(srcs_r, lands_r, send_s, recv_s, scatter):
            outgoing.wait_send()
            incoming.wait_recv()

    hbm = lambda a: pltpu.HBM(a.shape, a.dtype)
    res = pl.pallas_call(
        body, name=name, out_shape=tuple(hbm(a) for a in list(srcs) + list(lands)),
        in_specs=[HBM_SPEC] * (2 * n) + [SEM_SPEC, SEM_SPEC, pl.BlockSpec(memory_space=pl.ANY)],
        out_specs=tuple([HBM_SPEC] * (2 * n)),
        input_output_aliases={i: i for i in range(2 * n)},
        compiler_params=pltpu.CompilerParams(has_side_effects=DATAFLOW_EFFECT),
    )(*srcs, *lands, send_sems, recv_sems, after)
    return list(res[n:])


def _chip_peers(x, y, c):
    return [(x, y, 1 - c)] + [_plane_pos(x, y, q) + (c,) for q in (2, 1, 3)]


def _chip_gather_start(shards, lands, name, after=()):
    n, n_after = len(shards), len(after)

    def body(*refs):
        srcs_r, lands_r = refs[:n], refs[n:2 * n]
        send_sems, recv_sems = refs[2 * n + n_after], refs[2 * n + n_after + 1]
        x, y, c, me = _position()
        for a in range(n):
            for k, peer in enumerate(_chip_peers(x, y, c)):
                pltpu.make_async_remote_copy(
                    src_ref=srcs_r[a], dst_ref=lands_r[a].at[me], send_sem=send_sems.at[4 * a + k],
                    recv_sem=recv_sems.at[4 * a + k], device_id=peer, device_id_type=MESH).start()
        refs[-1][...] = jnp.zeros_like(refs[-1])

    hbm = lambda a: pltpu.HBM(a.shape, a.dtype)
    res = pl.pallas_call(
        body, name=name,
        out_shape=(pltpu.SemaphoreType.DMA((4 * n,)), pltpu.SemaphoreType.DMA((4 * n,)),
                   *[hbm(a) for a in shards], *[hbm(a) for a in lands], jax.ShapeDtypeStruct((8, 128), f32)),
        in_specs=[HBM_SPEC] * (2 * n) + [ANY_SPEC] * n_after,
        out_specs=(SEM_SPEC, SEM_SPEC, *[HBM_SPEC] * (2 * n), pl.BlockSpec(memory_space=pltpu.VMEM)),
        input_output_aliases={i: 2 + i for i in range(2 * n)},
        compiler_params=pltpu.CompilerParams(has_side_effects=DATAFLOW_EFFECT),
    )(*[pltpu.with_memory_space_constraint(a, pltpu.HBM) for a in list(shards) + list(lands)], *after)
    return (res[0], res[1], list(res[2:2 + n]), list(res[2 + n:2 + 2 * n])), res[-1]


def _chip_gather_forward(handle, after, name):
    send_sems, recv_sems, srcs, lands = handle
    n = len(srcs)

    def body(*refs):
        srcs_r, lands_r = refs[:n], refs[n:2 * n]
        send1, recv1 = refs[2 * n], refs[2 * n + 1]
        send2, recv2 = refs[2 * n + 3], refs[2 * n + 4]
        x, y, c, me = _position()
        peers = _chip_peers(x, y, c)
        for a in range(n):
            for k, peer in enumerate(peers):
                mk = lambda slot: pltpu.make_async_remote_copy(
                    src_ref=srcs_r[a], dst_ref=lands_r[a].at[slot], send_sem=send1.at[4 * a + k],
                    recv_sem=recv1.at[4 * a + k], device_id=peer, device_id_type=MESH)
                mk(me).wait_send()
                mk(_lin(peer)).wait_recv()
        for a in range(n):
            for k, peer in enumerate(peers[1:]):
                slot = _lin(peer)
                pltpu.make_async_remote_copy(
                    src_ref=lands_r[a].at[slot], dst_ref=lands_r[a].at[slot], send_sem=send2.at[3 * a + k],
                    recv_sem=recv2.at[3 * a + k], device_id=peers[0], device_id_type=MESH).start()

    hbm = lambda a: pltpu.HBM(a.shape, a.dtype)
    res = pl.pallas_call(
        body, name=name,
        out_shape=(pltpu.SemaphoreType.DMA((3 * n,)), pltpu.SemaphoreType.DMA((3 * n,)), *[hbm(a) for a in lands]),
        in_specs=[HBM_SPEC] * (2 * n) + [SEM_SPEC, SEM_SPEC, ANY_SPEC],
        out_specs=(SEM_SPEC, SEM_SPEC, *[HBM_SPEC] * n),
        input_output_aliases={n + i: 2 + i for i in range(n)},
        compiler_params=pltpu.CompilerParams(has_side_effects=DATAFLOW_EFFECT),
    )(*srcs, *lands, send_sems, recv_sems, after)
    return (res[0], res[1], list(res[2:]))


def _chip_gather_wait(handle, after, name):
    send_sems, recv_sems, lands = handle
    n = len(lands)

    def body(*refs):
        lands_r, send2, recv2 = refs[:n], refs[n], refs[n + 1]
        x, y, c, me = _position()
        peers = _chip_peers(x, y, c)
        for a in range(n):
            for k, (px, py, pc) in enumerate(peers[1:]):
                mk = lambda slot: pltpu.make_async_remote_copy(
                    src_ref=lands_r[a].at[slot], dst_ref=lands_r[a].at[slot], send_sem=send2.at[3 * a + k],
                    recv_sem=recv2.at[3 * a + k], device_id=peers[0], device_id_type=MESH)
                mk(_lin((px, py, pc))).wait_send()
                mk(_lin((px, py, 1 - pc))).wait_recv()

    hbm = lambda a: pltpu.HBM(a.shape, a.dtype)
    res = pl.pallas_call(
        body, name=name, out_shape=tuple(hbm(a) for a in lands),
        in_specs=[HBM_SPEC] * n + [SEM_SPEC, SEM_SPEC, ANY_SPEC], out_specs=tuple([HBM_SPEC] * n),
        input_output_aliases={i: i for i in range(n)},
        compiler_params=pltpu.CompilerParams(has_side_effects=DATAFLOW_EFFECT),
    )(*lands, send_sems, recv_sems, after)
    return list(res)


def _own_block_filled(block, me):
    land = lax.empty((N_DEV,) + block.shape, block.dtype)
    return lax.dynamic_update_index_in_dim(land, block, me, 0)


def _staged_copy(src, dst, buf, in_sems, out_sems, rows, chunk):
    n = rows // chunk

    def rd(i):
        return pltpu.make_async_copy(src.at[pl.ds(i * chunk, chunk)], buf.at[i % 2], in_sems.at[i % 2])

    def wr(i):
        return pltpu.make_async_copy(buf.at[i % 2], dst.at[pl.ds(i * chunk, chunk)], out_sems.at[i % 2])

    rd(0).start()
    for i in range(n):
        if i + 1 < n:
            if i >= 1:
                wr(i - 1).wait()
            rd(i + 1).start()
        rd(i).wait()
        wr(i).start()
    for i in range(max(n - 2, 0), n):
        wr(i).wait()


def _all_gather_2level(shards, name):
    n = len(shards)
    chunks = [_pick(s.shape[0], (416, 512, 256, 160, 128, 64, 16)) for s in shards]

    def body(*refs):
        ins, outs = refs[:n], refs[n:2 * n]
        send_sems, recv_sems, in_sems, out_sems = refs[2 * n:2 * n + 4]
        bufs = refs[2 * n + 4:]
        x, y, c, me = _position()
        sib, xn, yn, dg = (x, y, 1 - c), (1 - x, y, c), (x, 1 - y, c), (1 - x, 1 - y, c)

        def cp(a, k, src, slot, to):
            return pltpu.make_async_remote_copy(src_ref=src, dst_ref=outs[a].at[slot], send_sem=send_sems.at[a, k],
                                                recv_sem=recv_sems.at[a, k], device_id=to, device_id_type=MESH)

        for a in range(n):
            for k, to in ((0, sib), (1, xn), (2, yn)):
                cp(a, k, ins[a], me, to).start()
        for a in range(n):
            cp(a, 1, ins[a], _lin(xn), xn).wait_recv()
            cp(a, 3, outs[a].at[_lin(xn)], _lin(xn), sib).start()

            @pl.when(c == 0)
            def _():
                cp(a, 5, outs[a].at[_lin(xn)], _lin(xn), yn).start()

            cp(a, 2, ins[a], _lin(yn), yn).wait_recv()
            cp(a, 4, outs[a].at[_lin(yn)], _lin(yn), sib).start()

            @pl.when(c == 1)
            def _():
                cp(a, 5, outs[a].at[_lin(yn)], _lin(yn), xn).start()

        for a in range(n):
            cp(a, 5, ins[a], _lin(dg), xn).wait_recv()
            cp(a, 6, outs[a].at[_lin(dg)], _lin(dg), sib).start()
        for a in range(n):
            _staged_copy(ins[a], outs[a].at[me], bufs[a], in_sems.at[a], out_sems.at[a], shards[a].shape[0], chunks[a])
        for a in range(n):
            for k, origin in ((0, sib), (3, (1 - x, y, 1 - c)), (4, (x, 1 - y, 1 - c)), (6, (1 - x, 1 - y, 1 - c))):
                cp(a, k, ins[a], _lin(origin), sib).wait_recv()
            for k in range(7):
                cp(a, k, ins[a], me, sib).wait_send()

    any_spec = pl.BlockSpec(memory_space=pl.ANY)
    return pl.pallas_call(
        body, name=name, out_shape=[jax.ShapeDtypeStruct((N_DEV,) + s.shape, s.dtype) for s in shards],
        in_specs=[any_spec] * n, out_specs=[any_spec] * n,
        scratch_shapes=[pltpu.SemaphoreType.DMA((n, 7)), pltpu.SemaphoreType.DMA((n, 7)),
                        pltpu.SemaphoreType.DMA((n, 2)), pltpu.SemaphoreType.DMA((n, 2))]
        + [pltpu.VMEM((2, ch, s.shape[1]), s.dtype) for ch, s in zip(chunks, shards)],
    )(*shards)


def _plane_pos(x, y, q):
    return ((1 - x) if q & 2 else x, (1 - y) if q & 1 else y)


ANY_SPEC = pl.BlockSpec(memory_space=pl.ANY)


def _mm(a, b, name, ta=False, tb=False, out_dtype=f32, after=()):
    if ta:
        k_dim, m_dim = a.shape
    else:
        m_dim, k_dim = a.shape
    if tb:
        n_dim, k2 = b.shape
    else:
        k2, n_dim = b.shape
    assert k_dim == k2, (a.shape, b.shape)
    assert a.dtype == bf16 and b.dtype == bf16
    bm = _pick(m_dim, (512, 768, 640, 256, 128))
    bn = _pick(n_dim, (512, 640, 256, 128))
    bk = _pick(k_dim, (1024, 1280, 768, 512))
    nk = k_dim // bk
    a_spec = (pl.BlockSpec((bk, bm), lambda i, j, k: (k, i)) if ta
              else pl.BlockSpec((bm, bk), lambda i, j, k: (i, k)))
    b_spec = (pl.BlockSpec((bn, bk), lambda i, j, k: (j, k)) if tb
              else pl.BlockSpec((bk, bn), lambda i, j, k: (k, j)))
    dims = (((0 if ta else 1,), (1 if tb else 0,)), ((), ()))

    n_after = len(after)

    def body_single(a_ref, b_ref, *rest):
        o_ref = rest[n_after]
        o_ref[...] = lax.dot_general(a_ref[...], b_ref[...], dims, preferred_element_type=f32).astype(o_ref.dtype)

    def body(a_ref, b_ref, *rest):
        o_ref, acc_ref = rest[n_after:]
        k = pl.program_id(2)

        @pl.when(k == 0)
        def _():
            acc_ref[...] = jnp.zeros_like(acc_ref)

        acc_ref[...] += lax.dot_general(a_ref[...], b_ref[...], dims, preferred_element_type=f32)

        @pl.when(k == nk - 1)
        def _():
            o_ref[...] = acc_ref[...].astype(o_ref.dtype)

    return pl.pallas_call(
        body_single if nk == 1 else body, name=name, out_shape=jax.ShapeDtypeStruct((m_dim, n_dim), out_dtype),
        grid=(m_dim // bm, n_dim // bn, nk), in_specs=[a_spec, b_spec] + [ANY_SPEC] * n_after,
        out_specs=pl.BlockSpec((bm, bn), lambda i, j, k: (i, j)),
        scratch_shapes=[] if nk == 1 else [pltpu.VMEM((bm, bn), f32)],
        compiler_params=_cparams("parallel", "parallel", "arbitrary"),
    )(a, b, *after)


def _rin(arr, width=None, cb=0, roff=0):
    return (arr, arr.shape[1] if width is None else width, cb, roff)


def _rowcall(fn, name, rows, tm, row_ins, par_ins, row_outs, acc_outs=(), after=()):
    nr, npar, nro, n_after = len(row_ins), len(par_ins), len(row_outs), len(after)
    in_specs, args = [], []
    for arr, width, cb, roff in row_ins:
        if roff >= 0:
            imap = lambda i, cb=cb, roff=roff: (i + roff, cb)
        else:
            imap = lambda i, cb=cb, roff=roff: (jnp.maximum(i + roff, 0), cb)
        in_specs.append(pl.BlockSpec((tm, width), imap))
        args.append(arr)
    for p in par_ins:
        in_specs.append(pl.BlockSpec(p.shape, lambda i: (0, 0)))
        args.append(p)
    out_shape, out_specs = [], []
    for width, dt in row_outs:
        out_shape.append(jax.ShapeDtypeStruct((rows, width), dt))
        out_specs.append(pl.BlockSpec((tm, width), lambda i: (i, 0)))
    for p, width in acc_outs:
        out_shape.append(jax.ShapeDtypeStruct((p, width), f32))
        out_specs.append(pl.BlockSpec((p, width), lambda i: (0, 0)))

    def body(*refs):
        i = pl.program_id(0)
        res = fn(i, *[r[...] for r in refs[:nr + npar]])
        outs = refs[nr + npar + n_after:]
        for o, v in zip(outs[:nro], res[:nro]):
            o[...] = v.astype(o.dtype)
        if acc_outs:
            @pl.when(i == 0)
            def _():
                for o in outs[nro:]:
                    o[...] = jnp.zeros_like(o)

            for o, v in zip(outs[nro:], res[nro:]):
                o[...] += v

    return pl.pallas_call(
        body, name=name, out_shape=out_shape, grid=(rows // tm,), in_specs=in_specs + [ANY_SPEC] * n_after,
        out_specs=out_specs, compiler_params=_cparams("arbitrary"),
    )(*args, *after)


def _rms(x, g):
    return x * lax.rsqrt(jnp.mean(x * x, axis=-1, keepdims=True) + EPS) * g


def _normmod(x, g, sc, sh):
    return _rms(x, g) * (1.0 + sc) + sh


def _gelu(x):
    return 0.5 * x * (1.0 + jnp.tanh(0.7978845608028654 * (x + 0.044715 * (x * x * x))))


def _sigmoid(x):
    return 0.5 * (jnp.tanh(0.5 * x) + 1.0)


def _coeff_parts(pre_a, pre_x, ba, bx, lam):
    r = _sigmoid(pre_a + ba)
    ig = _sigmoid(pre_x + bx)
    nl = -lam
    sp = jnp.maximum(nl, 0.0) + jnp.log(1.0 + jnp.exp(-jnp.abs(nl)))
    la = -RG_C * r * sp
    a = jnp.exp(la)
    one_minus_a2 = -jnp.tanh(la) * (a * a + 1.0)
    inv_m = lax.rsqrt(one_minus_a2)
    return r, ig, sp, a, one_minus_a2 * inv_m, inv_m


def _coeff(pre_a, pre_x, u, ba, bx, lam):
    _, ig, _, a, m, _ = _coeff_parts(pre_a, pre_x, ba, bx, lam)
    return a, m * (ig * u)


def _coeff_bwd(pre_a, pre_x, u, ba, bx, lam, da, db):
    r, ig, sp, a, m, inv_m = _coeff_parts(pre_a, pre_x, ba, bx, lam)
    dbu = db * u
    dig = dbu * m
    dm = dbu * ig
    dla = a * (da - dm * a * inv_m)
    dpa = dla * (-RG_C * sp) * (r * (1.0 - r))
    dpx = dig * (ig * (1.0 - ig))
    dsp = jnp.sum(dla * (-RG_C * r), axis=0, keepdims=True)
    dlam = -dsp * _sigmoid(-lam)
    return (dpa, dpx, db * m * ig, jnp.sum(dpa, axis=0, keepdims=True), jnp.sum(dpx, axis=0, keepdims=True), dlam)


SCAN_CHUNK = 256


def _scan_call(a, v, chunk_of, reverse, name, backward):
    rows, width = a.shape
    n_out = 1 if backward else 2
    nt = SCAN_CHUNK // 8

    def body(a_ref, v_ref, *rest):
        outs, state_ref = rest[:-1], rest[-1]

        @pl.when(pl.program_id(0) == 0)
        def _():
            state_ref[...] = jnp.zeros_like(state_ref)

        rid = lax.broadcasted_iota(jnp.int32, (8, width), 0)

        def tile(j, st):
            t0 = pl.multiple_of((nt - 1 - j if reverse else j) * 8, 8)
            at = a_ref[pl.ds(t0, 8), :]
            vt = v_ref[pl.ds(t0, 8), :]
            out = jnp.zeros((8, width), f32)
            prev = jnp.zeros((8, width), f32)
            for i in (range(7, -1, -1) if reverse else range(8)):
                if backward:
                    g = vt[i:i + 1] + st
                    st = at[i:i + 1] * g
                    out = jnp.where(rid == i, g, out)
                else:
                    prev = jnp.where(rid == i, st, prev)
                    st = at[i:i + 1] * st + vt[i:i + 1]
                    out = jnp.where(rid == i, st, out)
            outs[0][pl.ds(t0, 8), :] = out
            if not backward:
                outs[1][pl.ds(t0, 8), :] = prev
            return st

        state_ref[0:1, :] = lax.fori_loop(0, nt, tile, state_ref[0:1, :])

    spec = pl.BlockSpec((SCAN_CHUNK, width), lambda t: (chunk_of(t), 0))
    return pl.pallas_call(
        body, name=name, out_shape=[jax.ShapeDtypeStruct((rows, width), f32)] * n_out,
        grid=(rows // SCAN_CHUNK,), in_specs=[spec, spec], out_specs=[spec] * n_out,
        scratch_shapes=[pltpu.VMEM((8, width), f32)],
        compiler_params=_cparams("arbitrary"),
    )(a, v)


CONV_CHUNK = 256


def _fill_padded(pad_ref, src_ref, start, n):
    cb = pad_ref.shape[1]
    pad_ref[pl.ds(0, HALO), :] = jnp.zeros((HALO, cb), f32)
    pad_ref[pl.ds(HALO, n), :] = src_ref[pl.ds(start, n), :].astype(f32)
    pad_ref[pl.ds(HALO + n, HALO), :] = jnp.zeros((HALO, cb), f32)


def _dwconv_fwd(x, x_cb0, w, b, taps, pad_left, segments, cb, name, emit_bf16):
    rows = x.shape[0]
    width = w.shape[1]

    def body(x_ref, w_ref, b_ref, *rest):
        outs, xp = rest[:-1], rest[-1]
        for start, n in segments:
            _fill_padded(xp, x_ref, start, n)
            for c0 in range(0, n, CONV_CHUNK):
                acc = jnp.zeros((CONV_CHUNK, cb), f32) + b_ref[...]
                for k in range(taps):
                    acc = acc + w_ref[k:k + 1, :] * xp[pl.ds(HALO + c0 + k - pad_left, CONV_CHUNK), :]
                for o in outs:
                    o[pl.ds(start + c0, CONV_CHUNK), :] = acc.astype(o.dtype)

    out_dtypes = [f32, bf16] if emit_bf16 else [f32]
    return pl.pallas_call(
        body, name=name, out_shape=[jax.ShapeDtypeStruct((rows, width), dt) for dt in out_dtypes],
        grid=(width // cb,),
        in_specs=[pl.BlockSpec((rows, cb), lambda j: (0, j + x_cb0)), pl.BlockSpec((taps, cb), lambda j: (0, j)),
                  pl.BlockSpec((1, cb), lambda j: (0, j))],
        out_specs=[pl.BlockSpec((rows, cb), lambda j: (0, j))] * len(out_dtypes),
        scratch_shapes=[pltpu.VMEM((rows + 2 * HALO, cb), f32)],
        compiler_params=_cparams("parallel"),
    )(x, w, b)


def _dwconv_bwd(douts, x, x_cb0, w, taps, pad_left, segments, cb, name, dx_dtype):
    rows = x.shape[0]
    width = w.shape[1]
    nd = len(douts)

    def body(*refs):
        d_refs, x_ref, w_ref = refs[:nd], refs[nd], refs[nd + 1]
        dx_ref, dw_ref, db_ref, dp, dsum = refs[nd + 2:]
        dw_ref[...] = jnp.zeros_like(dw_ref)
        db_ref[...] = jnp.zeros_like(db_ref)
        if nd > 1:
            total = d_refs[0][...]
            for r in d_refs[1:]:
                total = total + r[...]
            dsum[...] = total
            d_ref = dsum
        else:
            d_ref = d_refs[0]
        for start, n in segments:
            _fill_padded(dp, d_ref, start, n)
            for c0 in range(0, n, CONV_CHUNK):
                db_ref[...] += jnp.sum(dp[pl.ds(HALO + c0, CONV_CHUNK), :], axis=0, keepdims=True)
                xchunk = x_ref[pl.ds(start + c0, CONV_CHUNK), :].astype(f32)
                acc = jnp.zeros((CONV_CHUNK, cb), f32)
                for k in range(taps):
                    shifted = dp[pl.ds(HALO + c0 + pad_left - k, CONV_CHUNK), :]
                    acc = acc + w_ref[k:k + 1, :] * shifted
                    dw_ref[k:k + 1, :] += jnp.sum(shifted * xchunk, axis=0, keepdims=True)
                dx_ref[pl.ds(start + c0, CONV_CHUNK), :] = acc.astype(dx_ref.dtype)

    dspec = pl.BlockSpec((rows, cb), lambda j: (0, j))
    return pl.pallas_call(
        body, name=name,
        out_shape=[jax.ShapeDtypeStruct((rows, width), dx_dtype), jax.ShapeDtypeStruct((taps, width), f32),
                   jax.ShapeDtypeStruct((1, width), f32)],
        grid=(width // cb,),
        in_specs=[dspec] * nd + [pl.BlockSpec((rows, cb), lambda j: (0, j + x_cb0)),
                                 pl.BlockSpec((taps, cb), lambda j: (0, j))],
        out_specs=[dspec, pl.BlockSpec((taps, cb), lambda j: (0, j)), pl.BlockSpec((1, cb), lambda j: (0, j))],
        scratch_shapes=[pltpu.VMEM((rows + 2 * HALO, cb), f32), pltpu.VMEM((rows, cb), f32)],
        compiler_params=_cparams("parallel"),
    )(*douts, x, w)


def _ada_forward(c16, w_ada, b_loc):
    def body(c_ref, w_ref, b_ref, o_ref):
        cv = c_ref[...]
        s = (cv * _sigmoid(cv)).astype(bf16)
        o_ref[0] = jnp.dot(s, w_ref[0].astype(bf16), preferred_element_type=f32) + b_ref[0]

    return pl.pallas_call(
        body, name="ada_forward", out_shape=jax.ShapeDtypeStruct((2, 16, ADA_SHARD), f32), grid=(2,),
        in_specs=[pl.BlockSpec((16, D), lambda l: (0, 0)), pl.BlockSpec((1, D, ADA_SHARD), lambda l: (l, 0, 0)),
                  pl.BlockSpec((1, 1, ADA_SHARD), lambda l: (l, 0, 0))],
        out_specs=pl.BlockSpec((1, 16, ADA_SHARD), lambda l: (l, 0, 0)),
        compiler_params=_cparams("parallel"),
    )(c16, w_ada, b_loc)


def _ada_backward(c16, g16, w_ada):
    def body(c_ref, g_ref, w_ref, dw_ref, ds_ref):
        cv = c_ref[...]
        s = (cv * _sigmoid(cv)).astype(bf16)
        g = g_ref[0].astype(bf16)
        dw_ref[0] = lax.dot_general(s, g, (((0,), (0,)), ((), ())), preferred_element_type=f32)
        ds = lax.dot_general(g, w_ref[0].astype(bf16), (((1,), (1,)), ((), ())), preferred_element_type=f32)
        cc = cv[8:9]
        sg = _sigmoid(cc)
        dsilu = sg * (1.0 + cc * (1.0 - sg))
        ds_ref[0] = jnp.zeros((8, D), f32) + jnp.sum(ds[8:16], axis=0, keepdims=True) * dsilu

    return pl.pallas_call(
        body, name="ada_backward",
        out_shape=[jax.ShapeDtypeStruct((2, D, ADA_SHARD), f32), jax.ShapeDtypeStruct((2, 8, D), f32)], grid=(2,),
        in_specs=[pl.BlockSpec((16, D), lambda l: (0, 0)), pl.BlockSpec((1, 16, ADA_SHARD), lambda l: (l, 0, 0)),
                  pl.BlockSpec((1, D, ADA_SHARD), lambda l: (l, 0, 0))],
        out_specs=[pl.BlockSpec((1, D, ADA_SHARD), lambda l: (l, 0, 0)), pl.BlockSpec((1, 8, D), lambda l: (l, 0, 0))],
        compiler_params=_cparams("parallel"),
    )(c16, g16, w_ada)


def _adamw(pieces, w, m, v, name, after=()):
    rows, cols = w.shape
    n_arr, n_after = len(pieces), len(after)
    counts = [cnt for _, cnt in pieces]
    pieces = [p for p, _ in pieces]
    tm = 256 if (rows % 256 == 0 and rows > 256) else rows

    def body(*refs):
        p_refs = refs[:n_arr]
        w_ref, m_ref, v_ref = refs[n_arr:n_arr + 3]
        g_ref, d_ref, nm_ref, nv_ref = refs[n_arr + 3 + n_after:]
        g = None
        for p_ref in p_refs:
            for j in range(p_ref.shape[0]):
                term = p_ref[j].astype(f32)
                g = term if g is None else g + term
        m2 = ADAM_B1 * m_ref[...] + (1.0 - ADAM_B1) * g
        v2 = ADAM_B2 * v_ref[...] + (1.0 - ADAM_B2) * (g * g)
        m_hat = m2 / (1.0 - ADAM_B1 ** ADAM_STEP)
        v_hat = v2 / (1.0 - ADAM_B2 ** ADAM_STEP)
        g_ref[...] = g
        d_ref[...] = -ADAM_LR * (m_hat / (jnp.sqrt(v_hat) + ADAM_EPS) + ADAM_WD * w_ref[...])
        nm_ref[...] = m2
        nv_ref[...] = v2

    spec = pl.BlockSpec((tm, cols), lambda i: (i, 0))
    return pl.pallas_call(
        body, name=name, out_shape=[jax.ShapeDtypeStruct((rows, cols), f32)] * 4, grid=(rows // tm,),
        in_specs=[pl.BlockSpec((cnt, tm, cols), lambda i: (0, i, 0)) for cnt in counts] + [spec, spec, spec]
        + [ANY_SPEC] * n_after,
        out_specs=[spec] * 4, compiler_params=_cparams("parallel"),
    )(*pieces, w, m, v, *after)


MLP_TM = 256
FB = F // N_DEV


def _stack_rows(vals, n):
    cols = vals[0].shape[1]
    rid = lax.broadcasted_iota(jnp.int32, (n, cols), 0)
    out = jnp.zeros((n, cols), f32)
    for k, v in enumerate(vals):
        out = jnp.where(rid == k, v, out)
    return out


N_MLP_PARAMS = 9


class _ParamRows:
    def __init__(self, ref):
        self.ref = ref

    def __getitem__(self, sl):
        return self.ref[8 * sl.start:8 * sl.start + 1, :]


def _resident(shape, imap):
    return pl.BlockSpec(shape, imap, pipeline_mode=pl.Buffered(1))


def _mlp_forward(xa, xa_roff, out_prev, par, w_in, w_out, layer, name):
    def body(xa_ref, op_ref, par_ref, win_ref, wout_ref, x1_ref, h_ref, r_ref, mo_ref, x2_ref, hn_ref):
        p = _ParamRows(par_ref)
        x1 = xa_ref[...] + p[0:1] * (op_ref[...] + p[1:2])
        h = _normmod(x1, p[2:3], p[3:4], p[4:5]).astype(bf16)
        x1_ref[...] = x1
        h_ref[...] = h
        mo = jnp.zeros((MLP_TM, D), f32)
        for j in range(N_DEV):
            r = jnp.maximum(jnp.dot(h, win_ref[j], preferred_element_type=f32), 0.0)
            r_ref[:, j * FB:(j + 1) * FB] = r.astype(bf16)
            mo = mo + jnp.dot((r * r).astype(bf16), wout_ref[j], preferred_element_type=f32)
        mo_ref[...] = mo.astype(bf16)
        x2 = x1 + p[5:6] * mo
        x2_ref[...] = x2
        hn_ref[...] = _normmod(x2, p[6:7], p[7:8], p[8:9]).astype(bf16)

    row = lambda width: pl.BlockSpec((MLP_TM, width), lambda i: (i, 0))
    return pl.pallas_call(
        body, name=name, grid=(T_LAT // MLP_TM,),
        out_shape=[jax.ShapeDtypeStruct((T_LAT, D), f32), jax.ShapeDtypeStruct((T_LAT, D), bf16),
                   jax.ShapeDtypeStruct((T_LAT, F), bf16), jax.ShapeDtypeStruct((T_LAT, D), bf16),
                   jax.ShapeDtypeStruct((T_LAT, D), f32), jax.ShapeDtypeStruct((T_LAT, D), bf16)],
        in_specs=[pl.BlockSpec((MLP_TM, D), lambda i: (i + xa_roff, 0)), row(D), pl.BlockSpec((8 * N_MLP_PARAMS, D), lambda i: (0, 0)),
                  _resident((N_DEV, None, D, FB), lambda i: (0, layer, 0, 0)),
                  _resident((N_DEV, None, FB, D), lambda i: (0, layer, 0, 0))],
        out_specs=[row(D), row(D), row(F), row(D), row(D), row(D)],
        compiler_params=_cparams("parallel"),
    )(xa, out_prev, par, w_in, w_out)


def _mlp_backward(dx2, x1, r, mo, out_prev, par, w_in, w_out, layer, name, after=()):
    nt = (((1,), (1,)), ((), ()))

    n_after = len(after)

    def body(dx2_ref, x1_ref, r_ref, mo_ref, op_ref, par_ref, win_ref, wout_ref, *rest):
        dx1_ref, dop_ref, dmo_ref, dhid_ref, acc_ref = rest[n_after:]
        p = _ParamRows(par_ref)
        dx2v = dx2_ref[...]
        dmo = (p[5:6] * dx2v).astype(bf16)
        dmo_ref[...] = dmo
        dh = jnp.zeros((MLP_TM, D), f32)
        mo = mo_ref[...].astype(f32)
        for j in range(N_DEV):
            rf = r_ref[:, j * FB:(j + 1) * FB].astype(f32)
            dact = lax.dot_general(dmo, wout_ref[j], nt, preferred_element_type=f32)
            dhid = (dact * (2.0 * rf)).astype(bf16)
            dhid_ref[:, j * FB:(j + 1) * FB] = dhid
            dh = dh + lax.dot_general(dhid, win_ref[j], nt, preferred_element_type=f32)
        x1 = x1_ref[...]
        _, vjp = jax.vjp(_normmod, x1, p[2:3], p[3:4], p[4:5])
        dx, dng, dsc, dsh = vjp(dh)
        dx1 = dx2v + dx
        dx1_ref[...] = dx1
        dop_ref[...] = (p[0:1] * dx1).astype(bf16)
        sums = _stack_rows([jnp.sum(dx1 * (op_ref[...] + p[1:2]), axis=0, keepdims=True),
                            p[0:1] * jnp.sum(dx1, axis=0, keepdims=True), dng, dsc, dsh,
                            jnp.sum(dx2v * mo, axis=0, keepdims=True)], 8)

        @pl.when(pl.program_id(0) == 0)
        def _():
            acc_ref[...] = jnp.zeros_like(acc_ref)

        acc_ref[...] += sums

    row = lambda width: pl.BlockSpec((MLP_TM, width), lambda i: (i, 0))
    return pl.pallas_call(
        body, name=name, grid=(T_LAT // MLP_TM,),
        out_shape=[jax.ShapeDtypeStruct((T_LAT, D), f32), jax.ShapeDtypeStruct((T_LAT, D), bf16),
                   jax.ShapeDtypeStruct((T_LAT, D), bf16), jax.ShapeDtypeStruct((T_LAT, F), bf16),
                   jax.ShapeDtypeStruct((8, D), f32)],
        in_specs=[row(D), row(D), row(F), row(D), row(D), pl.BlockSpec((8 * N_MLP_PARAMS, D), lambda i: (0, 0)),
                  _resident((N_DEV, None, D, FB), lambda i: (0, layer, 0, 0)),
                  _resident((N_DEV, None, FB, D), lambda i: (0, layer, 0, 0))] + [ANY_SPEC] * n_after,
        out_specs=[row(D), row(D), row(D), row(F), pl.BlockSpec((8, D), lambda i: (0, 0))],
        compiler_params=_cparams("arbitrary"),
    )(dx2, x1, r, mo, out_prev, par, w_in, w_out, *after)


def _mlp_weight_grads(h, dhid, r, dmo, layer, other, tag):
    tn = (((0,), (0,)), ((), ()))

    def body_in(h_ref, dhid_ref, *rest):
        rest[-1][...] = lax.dot_general(h_ref[...], dhid_ref[...], tn, preferred_element_type=f32).astype(bf16)

    def body_out(r_ref, dmo_ref, *rest):
        rf = r_ref[...].astype(f32)
        rest[-1][...] = lax.dot_general((rf * rf).astype(bf16), dmo_ref[...], tn,
                                        preferred_element_type=f32).astype(bf16)

    def call(body, name, operands, specs, block, prev):
        extra = [] if prev is None else [prev]
        return pl.pallas_call(
            body, name=name, grid=(N_DEV,), out_shape=jax.ShapeDtypeStruct((N_DEV, 2) + block, bf16),
            in_specs=specs + [pl.BlockSpec(memory_space=pl.ANY)] * len(extra),
            out_specs=pl.BlockSpec((None, None) + block, lambda j: (j, layer, 0, 0)),
            input_output_aliases={} if prev is None else {2: 0},
            compiler_params=_cparams("parallel"),
        )(*operands, *extra)

    dw_in = call(body_in, tag + "_mlp_in_dw", [h, dhid],
                 [_resident((T_LAT, D), lambda j: (0, 0)), pl.BlockSpec((T_LAT, FB), lambda j: (0, j))], (D, FB),
                 None if other is None else other[0])
    dw_out = call(body_out, tag + "_mlp_out_dw", [r, dmo],
                  [pl.BlockSpec((T_LAT, FB), lambda j: (0, j)), _resident((T_LAT, D), lambda j: (0, 0))], (FB, D),
                  None if other is None else other[1])
    return dw_in, dw_out


def _pos_embed():
    n_rows = T_LAT // GRID_W
    q = D // 4
    omega = 1.0 / (POS_BASE ** (jnp.arange(q, dtype=f32) / q))
    er = jnp.arange(n_rows, dtype=jnp.int32).astype(f32)[:, None] * omega[None, :]
    ec = jnp.arange(GRID_W, dtype=jnp.int32).astype(f32)[:, None] * omega[None, :]
    by_row = jnp.concatenate([jnp.sin(er), jnp.cos(er)], axis=-1)[:, None, :]
    by_col = jnp.concatenate([jnp.sin(ec), jnp.cos(ec)], axis=-1)[None, :, :]
    full = jnp.concatenate([jnp.broadcast_to(by_row, (n_rows, GRID_W, D // 2)),
                            jnp.broadcast_to(by_col, (n_rows, GRID_W, D // 2))], axis=-1)
    return full.reshape(T_LAT, D)


HALF = R // 2
BLK_PER_HALF = N_BLK // 2
N_PARTS = 4


def _gate_matrix(w_a, w_x):
    eye = jnp.eye(BLK_PER_HALF, dtype=bf16)
    cols = []
    for h in range(2):
        for d in range(2):
            for w in (w_a, w_x):
                blocks = w[d, BLK_PER_HALF * h:BLK_PER_HALF * (h + 1)].astype(bf16)
                cols.append(jnp.einsum("hij,hg->higj", blocks, eye).reshape(HALF, HALF))
    return jnp.concatenate(cols, axis=1)


def _gate_blocks(dwg, part):
    out = []
    for h in range(2):
        blk = dwg[:, (N_PARTS * h + part) * HALF:(N_PARTS * h + part + 1) * HALF]
        blk = blk.reshape(BLK_PER_HALF, BLK, BLK_PER_HALF, BLK)
        out.append(jnp.moveaxis(jnp.diagonal(blk, axis1=0, axis2=2), -1, 0))
    return jnp.concatenate(out, axis=0)


GATE_BM = 768


def _gates_dx(dpre, wg, after=()):
    rows = dpre.shape[0]
    n_after = len(after)

    def body(d_ref, w_ref, *rest):
        o_ref, acc_ref = rest[n_after:]
        p = pl.program_id(2)

        @pl.when(p == 0)
        def _():
            acc_ref[...] = jnp.zeros_like(acc_ref)

        acc_ref[...] += lax.dot_general(d_ref[...], w_ref[...], (((1,), (1,)), ((), ())), preferred_element_type=f32)

        @pl.when(p == N_PARTS - 1)
        def _():
            o_ref[...] = acc_ref[...]

    return pl.pallas_call(
        body, name="l0_gates_dx", grid=(rows // GATE_BM, 2, N_PARTS), out_shape=jax.ShapeDtypeStruct((rows, R), f32),
        in_specs=[pl.BlockSpec((GATE_BM, HALF), lambda i, h, p: (i, N_PARTS * h + p)),
                  pl.BlockSpec((HALF, HALF), lambda i, h, p: (0, N_PARTS * h + p))] + [ANY_SPEC] * n_after,
        out_specs=pl.BlockSpec((GATE_BM, HALF), lambda i, h, p: (i, h)),
        scratch_shapes=[pltpu.VMEM((GATE_BM, HALF), f32)],
        compiler_params=_cparams("parallel", "parallel", "arbitrary"),
    )(dpre, wg, *after)


COEFF_TM = 256


def _dir_params(d, *params):
    specs = [pl.BlockSpec((None, 1, HALF), lambda h, i: (d, 0, h))] * len(params)
    return specs, [p.reshape(2, 1, R) for p in params]


def _gates_coeff_fwd(ub, u, wg, ba, bx, lam, d):
    rows = u.shape[0]

    def body(ub_ref, u_ref, w_ref, ba_ref, bx_ref, lam_ref, a_ref, b_ref):
        pre = jnp.dot(ub_ref[...], w_ref[...], preferred_element_type=f32)
        a, b = _coeff(pre[:, :HALF], pre[:, HALF:], u_ref[...], ba_ref[...], bx_ref[...], lam_ref[...])
        a_ref[...] = a
        b_ref[...] = b

    tile = pl.BlockSpec((COEFF_TM, HALF), lambda h, i: (i, h))
    pspecs, pargs = _dir_params(d, ba, bx, lam)
    return pl.pallas_call(
        body, name=f"l0_gates_coeff_{d}", grid=(2, rows // COEFF_TM),
        out_shape=[jax.ShapeDtypeStruct((rows, R), f32)] * 2,
        in_specs=[tile, tile, pl.BlockSpec((HALF, 2 * HALF), lambda h, i: (0, 2 * h + d))] + pspecs,
        out_specs=[tile, tile], compiler_params=_cparams("parallel", "parallel"),
    )(ub, u, wg, *pargs)


def _gates_coeff_bwd(ub, u, dh, yp, wg, ba, bx, lam, d, dpre_prev):
    rows = u.shape[0]
    n_prev = 0 if dpre_prev is None else 1

    def body(ub_ref, u_ref, dh_ref, yp_ref, w_ref, ba_ref, bx_ref, lam_ref, *rest):
        dpre_ref, du_ref, dba_ref, dbx_ref, dlam_ref = rest[n_prev:]
        pre = jnp.dot(ub_ref[...], w_ref[...], preferred_element_type=f32)
        dhv = dh_ref[...]
        dpa, dpx, du, dba, dbx, dlam = _coeff_bwd(pre[:, :HALF], pre[:, HALF:], u_ref[...], ba_ref[...], bx_ref[...],
                                                  lam_ref[...], dhv * yp_ref[...], dhv)
        dpre_ref[:, :HALF] = dpa.astype(bf16)
        dpre_ref[:, HALF:] = dpx.astype(bf16)
        du_ref[...] = du

        @pl.when(pl.program_id(1) == 0)
        def _():
            dba_ref[...] = jnp.zeros_like(dba_ref)
            dbx_ref[...] = jnp.zeros_like(dbx_ref)
            dlam_ref[...] = jnp.zeros_like(dlam_ref)

        dba_ref[...] += dba
        dbx_ref[...] += dbx
        dlam_ref[...] += dlam

    tile = pl.BlockSpec((COEFF_TM, HALF), lambda h, i: (i, h))
    acc = pl.BlockSpec((1, HALF), lambda h, i: (0, h))
    pspecs, pargs = _dir_params(d, ba, bx, lam)
    extra = [] if dpre_prev is None else [dpre_prev]
    return pl.pallas_call(
        body, name=f"l0_gates_coeff_bwd_{d}", grid=(2, rows // COEFF_TM),
        out_shape=[jax.ShapeDtypeStruct((rows, 2 * N_PARTS * HALF), bf16), jax.ShapeDtypeStruct((rows, R), f32)]
        + [jax.ShapeDtypeStruct((1, R), f32)] * 3,
        in_specs=[tile] * 4 + [pl.BlockSpec((HALF, 2 * HALF), lambda h, i: (0, 2 * h + d))] + pspecs
        + [ANY_SPEC] * n_prev,
        out_specs=[pl.BlockSpec((COEFF_TM, 2 * HALF), lambda h, i: (i, 2 * h + d)), tile, acc, acc, acc],
        input_output_aliases={8: 0} if n_prev else {}, compiler_params=_cparams("parallel", "arbitrary"),
    )(ub, u, dh, yp, wg, *pargs, *extra)


def _gates_dw(u, dpre):
    rows = u.shape[0]

    def body(u_ref, d_ref, o_ref):
        o_ref[...] = lax.dot_general(u_ref[...], d_ref[...], (((0,), (0,)), ((), ())), preferred_element_type=f32)

    return pl.pallas_call(
        body, name="l0_gates_dw", grid=(2 * N_PARTS,), out_shape=jax.ShapeDtypeStruct((HALF, 2 * N_PARTS * HALF), f32),
        in_specs=[pl.BlockSpec((rows, HALF), lambda j: (0, j // N_PARTS)), pl.BlockSpec((rows, HALF), lambda j: (0, j))],
        out_specs=pl.BlockSpec((HALF, HALF), lambda j: (0, j)), compiler_params=_cparams("parallel"),
    )(u, dpre)


N_SCAN_CHUNKS = T_ALL // SCAN_CHUNK
SCAN_FWD = lambda t: t
SCAN_FWD_BWD = lambda t: N_SCAN_CHUNKS - 1 - t
SCAN_REV = lambda t: jnp.where(t == 0, 0, N_SCAN_CHUNKS - t)
SCAN_REV_BWD = lambda t: jnp.where(t == N_SCAN_CHUNKS - 1, 0, t + 1)
CONV_SEGMENTS = ((0, T_CTX), (T_CTX, T_LAT))
TM = 128
FUSED_TM = 256


def _local_step(x, ctx, target, mods, cmod, wts, late_weights, send_grads, start_after=()):
    sh1, sc1, g1, sh2, sc2, g2 = [[mods[l, i][None] for l in range(2)] for i in range(N_MOD)]
    ng = wts["norm_g"]
    xcat = jnp.concatenate([ctx, x], axis=0)
    poscat = jnp.concatenate([jnp.zeros((T_CTX, D), f32), _pos_embed()], axis=0)
    scp = jnp.concatenate([cmod[1][None], sc1[0]], axis=0)
    shp = jnp.concatenate([cmod[0][None], sh1[0]], axis=0)

    ctx_tiles = T_CTX // FUSED_TM
    nt = (((1,), (1,)), ((), ()))

    def blend(i, p):
        sel = jnp.where(i < ctx_tiles, 1.0, 0.0)
        return sel * p[0:1] + (1.0 - sel) * p[1:2]

    def f_pre0(i, xc, pos, g, scp_, shp_, w):
        x0 = xc + pos
        h = _normmod(x0, g, blend(i, scp_), blend(i, shp_)).astype(bf16)
        return x0, h, jnp.dot(h, w, preferred_element_type=f32)

    x0cat, h0, gr = _rowcall(f_pre0, "l0_prenorm_in_proj", T_ALL, FUSED_TM, [_rin(xcat), _rin(poscat)],
                             [ng[0, 0][None], scp, shp, wts["rec_w_in"]], [(D, f32), (D, bf16), (2 * R, f32)],
                             after=start_after)
    u, ub = _dwconv_fwd(gr, R // 256, wts["rec_conv_w"], wts["rec_conv_b"], 4, 1, CONV_SEGMENTS, 256,
                        "l0_conv", True)
    gate_args = (wts["gates"], wts["rec_b_a"], wts["rec_b_x"], wts["rec_lambda"])
    a0, b0 = _gates_coeff_fwd(ub, u, *gate_args, 0)
    a1, b1 = _gates_coeff_fwd(ub, u, *gate_args, 1)
    y0, yp0 = _scan_call(a0, b0, SCAN_FWD, False, "l0_scan_fwd", False)
    y1, yp1 = _scan_call(a1, b1, SCAN_REV, True, "l0_scan_rev", False)

    wts = dict(wts, **late_weights("mlp", y1))

    def f_gate_out(i, gp, y0_, y1_, w):
        z = (_gelu(gp) * (y0_ + y1_)).astype(bf16)
        return z, jnp.dot(z, w, preferred_element_type=f32)

    zb, out0 = _rowcall(f_gate_out, "l0_gate_out_proj", T_LAT, FUSED_TM,
                        [_rin(gr, R, 0, ctx_tiles), _rin(y0, None, 0, ctx_tiles), _rin(y1, None, 0, ctx_tiles)],
                        [wts["rec_w_out"]], [(R, bf16), (D, f32)])

    zero_d = jnp.zeros((1, D), f32)

    def mlp_params(rows):
        rows = rows + [zero_d] * (N_MLP_PARAMS - len(rows))
        return jnp.concatenate([jnp.broadcast_to(r, (8, D)) for r in rows], axis=0)

    par0 = mlp_params([g1[0], zero_d, ng[0, 1][None], sc2[0], sh2[0], g2[0], ng[1, 0][None], sc1[1], sh1[1]])
    x1, h1, r0, mo0, x2, h2 = _mlp_forward(x0cat, T_CTX // MLP_TM, out0, par0, wts["mlp_w_in"], wts["mlp_w_out"], 0,
                                           "l0_mlp")

    wts = dict(wts, **late_weights("conf", x2))
    def glu(pa, pb, b1):
        return (pa + b1[:, :D]) * _sigmoid(pb + b1[:, D:])

    def f_pw1_glu(i, h_, b1, w):
        p = jnp.dot(h_, w, preferred_element_type=f32)
        return glu(p[:, :D], p[:, D:], b1), p

    zg, pw = _rowcall(f_pw1_glu, "l1_pw1_glu", T_LAT, FUSED_TM, [_rin(h2)], [wts["conf_b_pw1"], wts["conf_w_pw1"]],
                      [(D, f32), (2 * D, bf16)])
    (zc,) = _dwconv_fwd(zg, 0, wts["conf_conv_w"], wts["conf_conv_b"], 31, 15, ((0, T_LAT),), 128, "l1_conv", False)

    def ln_silu(z, lg, lb):
        mu = jnp.mean(z, axis=-1, keepdims=True)
        zc_ = z - mu
        var = jnp.mean(zc_ * zc_, axis=-1, keepdims=True)
        yv = zc_ * lax.rsqrt(var + EPS) * lg + lb
        return yv * _sigmoid(yv)

    def f_lnsilu_pw2(i, z, lg, lb, w):
        s = ln_silu(z, lg, lb).astype(bf16)
        return s, jnp.dot(s, w, preferred_element_type=f32)

    sb, out1 = _rowcall(f_lnsilu_pw2, "l1_ln_silu_pw2", T_LAT, FUSED_TM, [_rin(zc)],
                        [wts["conf_ln_g"], wts["conf_ln_b"], wts["conf_w_pw2"]], [(D, bf16), (D, f32)])
    par1 = mlp_params([g1[1], wts["conf_b_pw2"], ng[1, 1][None], sc2[1], sh2[1], g2[1]])
    x3, h3, r1, mo1, x4, _ = _mlp_forward(x2, 0, out1, par1, wts["mlp_w_in"], wts["mlp_w_out"], 1, "l1_mlp")

    def loss_fn(x4_, fg, tgt):
        err = _rms(x4_, fg) - tgt
        per_row = jnp.mean(err * err, axis=-1, keepdims=True)
        return 0.5 * jnp.sum(per_row, axis=0, keepdims=True)

    def f_head(i, x4_, tgt, fg):
        loss, vjp = jax.vjp(lambda a, e: loss_fn(a, e, tgt), x4_, fg)
        dx, dfg = vjp(jnp.ones((1, 1), f32))
        return dx, jnp.broadcast_to(loss, (1, 128)), dfg

    dx4, loss_acc, dfinal_g = _rowcall(f_head, "head", T_LAT, TM, [_rin(x4), _rin(target)], [wts["final_g"]],
                                       [(D, f32)], [(1, 128), (1, D)])

    grads = {"final_g": dfinal_g}

    dx3, dout1, dmo1, dhid1, acc1 = _mlp_backward(dx4, x3, r1, mo1, out1, par1, wts["mlp_w_in"], wts["mlp_w_out"], 1,
                                                  "l1_mlp_bwd")
    mlp_dw = _mlp_weight_grads(h3, dhid1, r1, dmo1, 1, None, "l1")
    dg1_1, db_pw2, dng11, dsc2_1, dsh2_1, dg2_1 = [acc1[k:k + 1] for k in range(6)]

    grads["conf_w_pw2"] = _mm(sb, dout1, "l1_pw2_dw", ta=True, out_dtype=bf16)
    grads["conf_b_pw2"] = db_pw2

    def f_pw2_lnsilu_bwd(i, z, dout, lg, lb, w):
        ds = lax.dot_general(dout, w, nt, preferred_element_type=f32)
        _, vjp = jax.vjp(ln_silu, z, lg, lb)
        return vjp(ds)

    dzc, dln_g, dln_b = _rowcall(f_pw2_lnsilu_bwd, "l1_pw2_ln_silu_bwd", T_LAT, FUSED_TM, [_rin(zc), _rin(dout1)],
                                 [wts["conf_ln_g"], wts["conf_ln_b"], wts["conf_w_pw2"]], [(D, f32)], [(1, D)] * 2)
    grads["conf_ln_g"], grads["conf_ln_b"] = dln_g, dln_b
    dzg, dconv_w, dconv_b = _dwconv_bwd([dzc], zg, 0, wts["conf_conv_w"], 31, 15, ((0, T_LAT),), 128,
                                        "l1_conv_bwd", f32)
    grads["conf_conv_w"], grads["conf_conv_b"] = dconv_w, dconv_b

    def f_glu_pw1_norm_bwd(i, p_, dz, x_, dxs, b1, g_, sc_, sh_, w):
        pf = p_.astype(f32)
        _, vjp = jax.vjp(glu, pf[:, :D], pf[:, D:], b1)
        da, db, db1 = vjp(dz)
        dp = jnp.concatenate([da, db], axis=1).astype(bf16)
        dh = lax.dot_general(dp, w, nt, preferred_element_type=f32)
        _, vjp = jax.vjp(_normmod, x_, g_, sc_, sh_)
        dx, dg, dsc, dsh = vjp(dh)
        return dp, dx + dxs, db1, dg, dsc, dsh

    dpw, dx2, db_pw1, dng10, dsc1_1, dsh1_1 = _rowcall(
        f_glu_pw1_norm_bwd, "l1_glu_pw1_normmod_bwd", T_LAT, FUSED_TM, [_rin(pw), _rin(dzg), _rin(x2), _rin(dx3)],
        [wts["conf_b_pw1"], ng[1, 0][None], sc1[1], sh1[1], wts["conf_w_pw1"]], [(2 * D, bf16), (D, f32)],
        [(1, 2 * D), (1, D), (1, D), (1, D)])
    grads["conf_b_pw1"] = db_pw1
    grads["conf_w_pw1"] = _mm(h2, dpw, "l1_pw1_dw", ta=True, out_dtype=bf16)
    sent = send_grads(["conf_w_pw2", "conf_w_pw1"], grads)

    dx1, dout0, dmo0, dhid0, acc0 = _mlp_backward(dx2, x1, r0, mo0, out0, par0, wts["mlp_w_in"], wts["mlp_w_out"], 0,
                                                  "l0_mlp_bwd", after=[sent])
    grads["mlp_w_in"], grads["mlp_w_out"] = _mlp_weight_grads(h1, dhid0, r0, dmo0, 0, mlp_dw, "l0")
    sent = send_grads(["mlp_w_in", "mlp_w_out"], grads)
    dg1_0, _, dng01, dsc2_0, dsh2_0, dg2_0 = [acc0[k:k + 1] for k in range(6)]

    grads["rec_w_out"] = _mm(zb, dout0, "l0_out_proj_dw", ta=True, out_dtype=bf16, after=[sent])
    sent = send_grads(["rec_w_out"], grads)

    def f_out_gate_bwd(i, gp, y0_, y1_, dout, w):
        lat = jnp.where(i < ctx_tiles, 0.0, 1.0)
        dz = lax.dot_general(dout, w, nt, preferred_element_type=f32)
        _, vjp = jax.vjp(lambda a, b: _gelu(a) * b, gp, y0_ + y1_)
        dgp, dy = vjp(dz)
        return dgp * lat, dy * lat

    dgp, dy = _rowcall(f_out_gate_bwd, "l0_out_proj_gate_bwd", T_ALL, FUSED_TM,
                       [_rin(gr, R, 0), _rin(y0), _rin(y1), _rin(dout0, None, 0, -ctx_tiles)], [wts["rec_w_out"]],
                       [(R, bf16), (R, f32)], after=[sent])
    (dh_f,) = _scan_call(a0, dy, SCAN_FWD_BWD, True, "l0_scan_fwd_bwd", True)
    (dh_r,) = _scan_call(a1, dy, SCAN_REV_BWD, False, "l0_scan_rev_bwd", True)

    dpre, du_f, *dpar_f = _gates_coeff_bwd(ub, u, dh_f, yp0, *gate_args, 0, None)
    dpre, du_r, *dpar_r = _gates_coeff_bwd(ub, u, dh_r, yp1, *gate_args, 1, dpre)
    grads["rec_b_a"], grads["rec_b_x"], grads["rec_lambda"] = [
        jnp.concatenate([f.reshape(-1), r_.reshape(-1)]).reshape(2, R) for f, r_ in zip(dpar_f, dpar_r)]
    grads["gates"] = _gates_dw(ub, dpre)
    sent = send_grads(["replicated"], grads)
    du_gates = _gates_dx(dpre, wts["gates"], after=[sent])
    drec, dconv4_w, dconv4_b = _dwconv_bwd([du_f, du_r, du_gates], gr, R // 256, wts["rec_conv_w"], 4, 1,
                                           CONV_SEGMENTS, 256, "l0_conv_bwd", bf16)
    grads["rec_conv_w"], grads["rec_conv_b"] = dconv4_w, dconv4_b
    dgr = jnp.concatenate([dgp, drec], axis=1)
    grads["rec_w_in"] = _mm(h0, dgr, "l0_in_proj_dw", ta=True, out_dtype=bf16)
    sent = send_grads(["rec_w_in"], grads)

    def f_pre0_bwd(i, x0, dgr_, dxs, g, scp_, shp_, w):
        lat = jnp.where(i < ctx_tiles, 0.0, 1.0)
        dh = lax.dot_general(dgr_, w, nt, preferred_element_type=f32)
        _, vjp = jax.vjp(lambda a, b, c, e: _normmod(a, b, blend(i, c), blend(i, e)), x0, g, scp_, shp_)
        dx, dg, dscp, dshp = vjp(dh)
        return dx + lat * dxs, dg, dscp, dshp

    dx0cat, dng00, dscp, dshp = _rowcall(
        f_pre0_bwd, "l0_in_proj_prenorm_bwd", T_ALL, FUSED_TM,
        [_rin(x0cat), _rin(dgr), _rin(dx1, None, 0, -ctx_tiles)], [ng[0, 0][None], scp, shp, wts["rec_w_in"]],
        [(D, f32)], [(1, D), (2, D), (2, D)], after=[sent])

    grads["norm_g"] = jnp.stack([jnp.concatenate([dng00, dng01], 0), jnp.concatenate([dng10, dng11], 0)])
    dmods = jnp.stack([
        jnp.concatenate([dshp[1:2], dscp[1:2], dg1_0, dsh2_0, dsc2_0, dg2_0], axis=0),
        jnp.concatenate([dsh1_1, dsc1_1, dg1_1, dsh2_1, dsc2_1, dg2_1], axis=0)])
    dcmod = jnp.concatenate([dshp[0:1], dscp[0:1]], axis=0)
    return loss_acc[0, 0], dx0cat[T_CTX:], dmods, dcmod, grads


def _unshard_cols(g):
    g = jnp.moveaxis(g, 0, -2)
    return g.reshape(g.shape[:-2] + (g.shape[-2] * g.shape[-1],))


def _shard_cols(w):
    w = w.reshape(w.shape[:-1] + (N_DEV, w.shape[-1] // N_DEV))
    return jnp.moveaxis(w, -2, 0)


def _shard_rows(w):
    return w.reshape((N_DEV, w.shape[0] // N_DEV) + w.shape[1:])


SMALL_PACK_ROWS = 64


def kernel(x, c, ctx, c_ctx, w_ada, b_ada, norm_g, rec_w_in, rec_conv_w, rec_conv_b, rec_lambda, rec_w_a, rec_b_a, rec_w_x, rec_b_x, rec_w_out, conf_w_pw1, conf_b_pw1, conf_conv_w, conf_conv_b, conf_ln_g, conf_ln_b, conf_w_pw2, conf_b_pw2, mlp_w_in, mlp_w_out, final_g, loss_target, m_c_ctx, m_w_ada, m_b_ada, m_norm_g, m_rec_w_in, m_rec_conv_w, m_rec_conv_b, m_rec_lambda, m_rec_w_a, m_rec_b_a, m_rec_w_x, m_rec_b_x, m_rec_w_out, m_conf_w_pw1, m_conf_b_pw1, m_conf_conv_w, m_conf_conv_b, m_conf_ln_g, m_conf_ln_b, m_conf_w_pw2, m_conf_b_pw2, m_mlp_w_in, m_mlp_w_out, m_final_g, v_c_ctx, v_w_ada, v_b_ada, v_norm_g, v_rec_w_in, v_rec_conv_w, v_rec_conv_b, v_rec_lambda, v_rec_w_a, v_rec_b_a, v_rec_w_x, v_rec_b_x, v_rec_w_out, v_conf_w_pw1, v_conf_b_pw1, v_conf_conv_w, v_conf_conv_b, v_conf_ln_g, v_conf_ln_b, v_conf_w_pw2, v_conf_b_pw2, v_mlp_w_in, v_mlp_w_out, v_final_g):
    me = 4 * lax.axis_index("x") + 2 * lax.axis_index("y") + lax.axis_index("c")
    weights = dict(c_ctx=c_ctx, w_ada=w_ada, b_ada=b_ada, norm_g=norm_g, rec_w_in=rec_w_in, rec_conv_w=rec_conv_w,
                   rec_conv_b=rec_conv_b, rec_lambda=rec_lambda, rec_w_a=rec_w_a, rec_b_a=rec_b_a, rec_w_x=rec_w_x,
                   rec_b_x=rec_b_x, rec_w_out=rec_w_out, conf_w_pw1=conf_w_pw1, conf_b_pw1=conf_b_pw1,
                   conf_conv_w=conf_conv_w, conf_conv_b=conf_conv_b, conf_ln_g=conf_ln_g, conf_ln_b=conf_ln_b,
                   conf_w_pw2=conf_w_pw2, conf_b_pw2=conf_b_pw2, mlp_w_in=mlp_w_in, mlp_w_out=mlp_w_out, final_g=final_g)
    m_in = dict(c_ctx=m_c_ctx, w_ada=m_w_ada, b_ada=m_b_ada, norm_g=m_norm_g, rec_w_in=m_rec_w_in, rec_conv_w=m_rec_conv_w,
                rec_conv_b=m_rec_conv_b, rec_lambda=m_rec_lambda, rec_w_a=m_rec_w_a, rec_b_a=m_rec_b_a, rec_w_x=m_rec_w_x,
                rec_b_x=m_rec_b_x, rec_w_out=m_rec_w_out, conf_w_pw1=m_conf_w_pw1, conf_b_pw1=m_conf_b_pw1,
                conf_conv_w=m_conf_conv_w, conf_conv_b=m_conf_conv_b, conf_ln_g=m_conf_ln_g, conf_ln_b=m_conf_ln_b,
                conf_w_pw2=m_conf_w_pw2, conf_b_pw2=m_conf_b_pw2, mlp_w_in=m_mlp_w_in, mlp_w_out=m_mlp_w_out,
                final_g=m_final_g)
    v_in = dict(c_ctx=v_c_ctx, w_ada=v_w_ada, b_ada=v_b_ada, norm_g=v_norm_g, rec_w_in=v_rec_w_in, rec_conv_w=v_rec_conv_w,
                rec_conv_b=v_rec_conv_b, rec_lambda=v_rec_lambda, rec_w_a=v_rec_w_a, rec_b_a=v_rec_b_a, rec_w_x=v_rec_w_x,
                rec_b_x=v_rec_b_x, rec_w_out=v_rec_w_out, conf_w_pw1=v_conf_w_pw1, conf_b_pw1=v_conf_b_pw1,
                conf_conv_w=v_conf_conv_w, conf_conv_b=v_conf_conv_b, conf_ln_g=v_conf_ln_g, conf_ln_b=v_conf_ln_b,
                conf_w_pw2=v_conf_w_pw2, conf_b_pw2=v_conf_b_pw2, mlp_w_in=v_mlp_w_in, mlp_w_out=v_mlp_w_out,
                final_g=v_final_g)
    names = list(weights)

    small_items = [c, norm_g, rec_conv_w, rec_lambda, conf_b_pw1, conf_conv_w, conf_conv_b, conf_ln_g, conf_ln_b,
                   conf_b_pw2]
    flat = jnp.concatenate([a.reshape(-1) for a in small_items])
    flat = jnp.pad(flat, (0, SMALL_PACK_ROWS * 128 - flat.shape[0])).reshape(SMALL_PACK_ROWS, 128)
    (small_all,) = _all_gather([flat], "gather_small")

    small_all = small_all.reshape(N_DEV, -1)
    off = 0
    small = []
    for a in small_items:
        small.append(small_all[:, off:off + a.size].reshape((N_DEV,) + a.shape))
        off += a.size
    c_all, ng_all, rcw_all, lam_all, bpw1_all, ccw_all, ccb_all, lng_all, lnb_all, bpw2_all = small
    wts = {
        "norm_g": _unshard_cols(ng_all),
        "rec_conv_w": _unshard_cols(rcw_all)[0],
        "rec_lambda": _unshard_cols(lam_all)[0],
        "conf_b_pw1": _unshard_cols(bpw1_all),
        "conf_conv_w": _unshard_cols(ccw_all)[0],
        "conf_conv_b": _unshard_cols(ccb_all),
        "conf_ln_g": _unshard_cols(lng_all),
        "conf_ln_b": _unshard_cols(lnb_all),
        "conf_b_pw2": _unshard_cols(bpw2_all),
        "rec_conv_b": rec_conv_b,
        "rec_b_a": rec_b_a[0].reshape(2, R),
        "rec_b_x": rec_b_x[0].reshape(2, R),
        "final_g": final_g[None],
        "gates": _gate_matrix(rec_w_a[0], rec_w_x[0]),
    }

    c16 = jnp.concatenate([c_all[:, 0], jnp.broadcast_to(c_ctx[None], (8, D))], axis=0)
    b_loc = lax.dynamic_slice_in_dim(b_ada, me * ADA_SHARD, ADA_SHARD, axis=1)[:, None]
    (mods_all,) = _all_gather([_ada_forward(c16, w_ada, b_loc)], "gather_mods")
    mods_all = _unshard_cols(mods_all)
    mods = lax.dynamic_index_in_dim(mods_all, me, axis=1, keepdims=False).reshape(2, N_MOD, D)
    cmod = mods_all[0, 8, :2 * D].reshape(2, D)

    as_shard = lambda a: a.astype(bf16).reshape(-1, a.shape[-1])
    early = _all_gather_2level([as_shard(rec_w_in[0])], "gather_weights_early")
    wts["rec_w_in"] = _unshard_cols(early[0])
    late_items = {"mlp": [rec_w_out[0], mlp_w_in, mlp_w_out], "conf": [conf_w_pw1[0], conf_w_pw2[0]]}
    late_handles, order = {}, [early[0], mods]
    for group in ("mlp", "conf"):
        shards = [as_shard(a) for a in late_items[group]]
        lands = [_own_block_filled(s, me) for s in shards]
        if group == "mlp":
            late_handles[group], token = _chip_gather_start(shards, lands, "gather_mlp_start", after=order)
        else:
            late_handles[group], token = _exchange_start(shards, lands, "gather_conf_start", False, after=order)
        order = [token]

    def late_weights(group, after):
        if group == "mlp":
            forwarded = _chip_gather_forward(late_handles[group], after, "gather_mlp_forward")
            got = _chip_gather_wait(forwarded, after, "gather_mlp_wait")
        else:
            got = _exchange_wait(late_handles[group], after, "gather_conf_wait", False)
        got = [g.reshape((N_DEV,) + a.shape) for g, a in zip(got, late_items[group])]
        if group == "mlp":
            return {"rec_w_out": got[0].reshape(R, D), "mlp_w_in": got[1], "mlp_w_out": got[2]}
        return {"conf_w_pw1": _unshard_cols(got[0]), "conf_w_pw2": got[1].reshape(D, D)}

    to_blocks = {"rec_w_in": _shard_cols, "conf_w_pw1": _shard_cols, "rec_w_out": _shard_rows, "conf_w_pw2": _shard_rows,
                 "mlp_w_in": lambda g: g, "mlp_w_out": lambda g: g}
    grad_handles = []

    repl_names = ["rec_w_a", "rec_w_x", "rec_b_a", "rec_b_x", "final_g"]

    def send_replicated(grads):
        dwg = grads["gates"]
        repl = {"rec_w_a": jnp.stack([_gate_blocks(dwg, 0), _gate_blocks(dwg, 2)]),
                "rec_w_x": jnp.stack([_gate_blocks(dwg, 1), _gate_blocks(dwg, 3)]),
                "rec_b_a": grads["rec_b_a"], "rec_b_x": grads["rec_b_x"], "final_g": grads["final_g"]}
        flat = jnp.concatenate([repl[n].reshape(-1) for n in repl_names])
        rows = -(-flat.shape[0] // (16 * D)) * 16
        flat = jnp.pad(flat, (0, rows * D - flat.shape[0])).reshape(rows, D).astype(bf16)
        handle, sent = _exchange_start([flat], [_own_block_filled(flat, me)], "gather_replicated_start", False)
        grad_handles.append((["replicated"], handle))
        return sent

    def send_grads(group, grads):
        if group == ["replicated"]:
            return send_replicated(grads)
        blocks = [to_blocks[n](grads[n]) for n in group]
        blocks = [g.reshape(N_DEV, -1, g.shape[-1]) for g in blocks]
        lands = [_own_block_filled(lax.dynamic_index_in_dim(g, me, 0, keepdims=False), me) for g in blocks]
        handle, sent = _exchange_start(blocks, lands, "scatter_start_" + group[0], True)
        grad_handles.append((group, handle))
        return sent

    loss_part, grad_x, dmods, dcmod, grads = _local_step(x[0], ctx[0], loss_target[0], mods, cmod, wts, late_weights,
                                                         send_grads, start_after=order)
    loss = lax.psum(loss_part, ("x", "y", "c"))

    def as2d(shape):
        rows = 1
        for s in shape[:-1]:
            rows *= s
        return (rows, shape[-1])

    def whole(arr, shape):
        arr = arr.reshape((-1,) + as2d(shape))
        return (arr, arr.shape[0])

    shard_shapes = {n: weights[n].shape for n in names}
    g_out, d_out, m_out, v_out = {}, {}, {}, {}

    def adamw(n, pieces, after):
        shape = shard_shapes[n]
        r2, c2 = as2d(shape)
        g, dl, nm, nv = _adamw(pieces, weights[n].reshape(r2, c2), m_in[n].reshape(r2, c2), v_in[n].reshape(r2, c2),
                               "adamw_" + n, after=after)
        g_out[n], d_out[n], m_out[n], v_out[n] = (t.reshape(shape) for t in (g, dl, nm, nv))
        return g

    small_sharded = ["norm_g", "rec_conv_w", "rec_lambda", "conf_b_pw1", "conf_conv_w", "conf_conv_b", "conf_ln_g",
                     "conf_ln_b", "conf_b_pw2"]
    pack = jnp.concatenate([_shard_cols(grads[n]).reshape(N_DEV, -1) for n in small_sharded], axis=1)
    pack = jnp.pad(pack, ((0, 0), (0, SMALL_PACK_ROWS * 128 - pack.shape[1]))).reshape(N_DEV, SMALL_PACK_ROWS, 128)
    small_handle, token = _exchange_start(
        [pack], [_own_block_filled(lax.dynamic_index_in_dim(pack, me, 0, keepdims=False), me)], "scatter_small_start",
        True, after=[grad_x])
    dm_flat = jnp.concatenate([dmods.reshape(-1), dcmod.reshape(-1), grads["rec_conv_b"].reshape(-1)])
    dm_len = dm_flat.shape[0]
    dm_flat = jnp.pad(dm_flat, (0, 128 * 128 - dm_len)).reshape(128, 128)
    dm_handle, token = _exchange_start([dm_flat], [_own_block_filled(dm_flat, me)], "gather_dmods_start", False,
                                       after=[token])

    done = token
    for group, handle in grad_handles:
        if group == ["replicated"]:
            repl_all = _exchange_wait(handle, done, "gather_replicated_wait", False)[0].reshape(N_DEV, -1)
            off = 0
            for n in repl_names:
                size = weights[n].size
                done = adamw(n, [whole(repl_all[:, off:off + size], shard_shapes[n])], [done])
                off += size
            continue
        for n, got in zip(group, _exchange_wait(handle, done, "scatter_wait_" + group[0], True)):
            done = adamw(n, [(got, N_DEV)], [done])

    dm_all = _exchange_wait(dm_handle, done, "gather_dmods_wait", False)[0].reshape(N_DEV, -1)
    dmods_all = dm_all[:, :2 * N_MOD * D].reshape(N_DEV, 2, N_MOD * D)
    dcmod_all = jnp.pad(dm_all[:, 2 * N_MOD * D:2 * N_MOD * D + 2 * D], ((0, 0), (0, (N_MOD - 2) * D)))
    g16_full = jnp.stack([jnp.concatenate([dmods_all[:, 0], dcmod_all], axis=0),
                          jnp.concatenate([dmods_all[:, 1], jnp.zeros_like(dcmod_all)], axis=0)])
    g16 = lax.dynamic_slice_in_dim(g16_full, me * ADA_SHARD, ADA_SHARD, axis=2)
    dw_ada, ds_part = _ada_backward(c16, g16, w_ada)
    ds_handle, token = _exchange_start([ds_part[0]], [_own_block_filled(ds_part[0], me)], "gather_dsilu_start", False)
    done = adamw("w_ada", [whole(dw_ada, shard_shapes["w_ada"])], [token])
    done = adamw("rec_conv_b", [whole(dm_all[:, dm_len - R:dm_len], shard_shapes["rec_conv_b"])], [done])
    db_terms = jnp.concatenate([dmods_all, jnp.stack([dcmod_all, jnp.zeros_like(dcmod_all)], axis=1)], axis=0)
    done = adamw("b_ada", [whole(db_terms, shard_shapes["b_ada"])], [done])
    pack_recv = _exchange_wait(small_handle, done, "scatter_small_wait", True)[0].reshape(N_DEV, -1)
    off = 0
    for n in small_sharded:
        size = weights[n].size
        done = adamw(n, [whole(pack_recv[:, off:off + size], shard_shapes[n])], [done])
        off += size
    ds_all = _exchange_wait(ds_handle, done, "gather_dsilu_wait", False)[0]
    adamw("c_ctx", [whole(ds_all[:, 0], shard_shapes["c_ctx"])], [])

    return (loss, grad_x[None], *[g_out[n] for n in names], *[d_out[n] for n in names],
            *[m_out[n] for n in names], *[v_out[n] for n in names])
```

```python
import functools

import jax
import jax.numpy as jnp
from jax import lax
from jax.experimental import pallas as pl
from jax.experimental.pallas import tpu as pltpu

f32 = jnp.float32
bf16 = jnp.bfloat16

N_DEV = 8
D = 1024
T_LAT = 2048
T_CTX = 256
T_ALL = T_CTX + T_LAT
R = 1280
N_BLK = 16
BLK = R // N_BLK
F = 4096
GRID_W = 64
RG_C = 8.0
EPS = 1e-6
POS_BASE = 10000.0
N_MOD = 6
ADA_SHARD = N_MOD * D // N_DEV

ADAM_LR = 0.001
ADAM_B1 = 0.9
ADAM_B2 = 0.999
ADAM_EPS = 1e-08
ADAM_WD = 0.01
ADAM_STEP = 10

VMEM_LIMIT_V7X = 56 * 1024 * 1024
HALO = 16
MESH = pl.DeviceIdType.MESH


def _cparams(*sem):
    return pltpu.CompilerParams(dimension_semantics=sem, vmem_limit_bytes=VMEM_LIMIT_V7X)


def _pick(n, cands):
    for c in cands:
        if n % c == 0:
            return c
    raise ValueError(f"no block size for {n}")


def _position():
    x, y, c = lax.axis_index("x"), lax.axis_index("y"), lax.axis_index("c")
    return x, y, c, 4 * x + 2 * y + c


def _peer(x, y, c, k):
    px = (1 - x) if (k >> 2) & 1 else x
    py = (1 - y) if (k >> 1) & 1 else y
    pc = (1 - c) if k & 1 else c
    return (px, py, pc), 4 * px + 2 * py + pc


def _exchange(arrs, name, scatter):
    n = len(arrs)

    def body(*refs):
        ins, outs = refs[:n], refs[n:2 * n]
        send_sems, recv_sems, local_sems = refs[2 * n:]
        x, y, c, me = _position()
        local = []
        for a in range(n):
            src = ins[a].at[me] if scatter else ins[a]
            cp = pltpu.make_async_copy(src, outs[a].at[me], local_sems.at[a])
            cp.start()
            local.append(cp)
        sends, recvs = [], []
        for a in range(n):
            for k in range(1, N_DEV):
                peer, peer_lin = _peer(x, y, c, k)
                src = ins[a].at[peer_lin] if scatter else ins[a]
                cp = pltpu.make_async_remote_copy(
                    src_ref=src, dst_ref=outs[a].at[me], send_sem=send_sems.at[a, k - 1],
                    recv_sem=recv_sems.at[a, k - 1], device_id=peer, device_id_type=MESH)
                cp.start()
                sends.append(cp)
                recvs.append(pltpu.make_async_remote_copy(
                    src_ref=src, dst_ref=outs[a].at[peer_lin], send_sem=send_sems.at[a, k - 1],
                    recv_sem=recv_sems.at[a, k - 1], device_id=peer, device_id_type=MESH))
        for cp in recvs:
            cp.wait_recv()
        for cp in sends:
            cp.wait_send()
        for cp in local:
            cp.wait()

    if scatter:
        out_shape = [jax.ShapeDtypeStruct(a.shape, a.dtype) for a in arrs]
    else:
        out_shape = [jax.ShapeDtypeStruct((N_DEV,) + a.shape, a.dtype) for a in arrs]
    any_spec = pl.BlockSpec(memory_space=pl.ANY)
    return pl.pallas_call(
        body, name=name, out_shape=out_shape,
        in_specs=[any_spec] * n, out_specs=[any_spec] * n,
        scratch_shapes=[pltpu.SemaphoreType.DMA((n, N_DEV - 1)), pltpu.SemaphoreType.DMA((n, N_DEV - 1)),
                        pltpu.SemaphoreType.DMA((n,))],
    )(*arrs)


def _all_gather(arrs, name):
    return _exchange(arrs, name, scatter=False)


def _lin(p):
    return 4 * p[0] + 2 * p[1] + p[2]


HBM_SPEC = pl.BlockSpec(memory_space=pltpu.HBM)
SEM_SPEC = pl.BlockSpec(memory_space=pltpu.SEMAPHORE)
DATAFLOW_EFFECT = pltpu.SideEffectType.DATAFLOW_SIDE_EFFECTING


def _split_copies(srcs, lands, send_sems, recv_sems, scatter):
    x, y, c, me = _position()
    out = []
    for a in range(len(srcs)):
        for k in range(1, N_DEV):
            peer, peer_lin = _peer(x, y, c, k)
            src = srcs[a].at[peer_lin] if scatter else srcs[a]
            mk = lambda slot: pltpu.make_async_remote_copy(
                src_ref=src, dst_ref=lands[a].at[slot], send_sem=send_sems.at[a * (N_DEV - 1) + k - 1],
                recv_sem=recv_sems.at[a * (N_DEV - 1) + k - 1], device_id=peer, device_id_type=MESH)
            out.append((mk(me), mk(peer_lin)))
    return out


def _exchange_start(srcs, lands, name, scatter, after=()):
    n = len(srcs)
    n_after = len(after)

    def body(*refs):
        srcs_r, lands_r = refs[:n], refs[n:2 * n]
        send_sems, recv_sems = refs[2 * n + n_after], refs[2 * n + n_after + 1]
        token = refs[-1]
        for outgoing, _ in _split_copies(srcs_r, lands_r, send_sems, recv_sems, scatter):
            outgoing.start()
        token[...] = jnp.zeros_like(token)

    hbm = lambda a: pltpu.HBM(a.shape, a.dtype)
    res = pl.pallas_call(
        body, name=name,
        out_shape=(pltpu.SemaphoreType.DMA((n * (N_DEV - 1),)), pltpu.SemaphoreType.DMA((n * (N_DEV - 1),)),
                   *[hbm(a) for a in srcs], *[hbm(a) for a in lands], jax.ShapeDtypeStruct((8, 128), f32)),
        in_specs=[HBM_SPEC] * (2 * n) + [pl.BlockSpec(memory_space=pl.ANY)] * n_after,
        out_specs=(SEM_SPEC, SEM_SPEC, *[HBM_SPEC] * (2 * n), pl.BlockSpec(memory_space=pltpu.VMEM)),
        input_output_aliases={i: 2 + i for i in range(2 * n)},
        compiler_params=pltpu.CompilerParams(has_side_effects=DATAFLOW_EFFECT),
    )(*[pltpu.with_memory_space_constraint(a, pltpu.HBM) for a in list(srcs) + list(lands)], *after)
    return (res[0], res[1], list(res[2:2 + n]), list(res[2 + n:2 + 2 * n])), res[-1]


def _exchange_wait(handle, after, name, scatter):
    send_sems, recv_sems, srcs, lands = handle
    n = len(srcs)

    def body(*refs):
        srcs_r, lands_r = refs[:n], refs[n:2 * n]
        send_s, recv_s = refs[2 * n], refs[2 * n + 1]
        for outgoing, incoming in _split_copies(srcs_r, lands_r, send_s, recv_s, scatter):
            outgoing.wait_send()
            incoming.wait_recv()

    hbm = lambda a: pltpu.HBM(a.shape, a.dtype)
    res = pl.pallas_call(
        body, name=name, out_shape=tuple(hbm(a) for a in list(srcs) + list(lands)),
        in_specs=[HBM_SPEC] * (2 * n) + [SEM_SPEC, SEM_SPEC, pl.BlockSpec(memory_space=pl.ANY)],
        out_specs=tuple([HBM_SPEC] * (2 * n)),
        input_output_aliases={i: i for i in range(2 * n)},
        compiler_params=pltpu.CompilerParams(has_side_effects=DATAFLOW_EFFECT),
    )(*srcs, *lands, send_sems, recv_sems, after)
    return list(res[n:])


def _chip_peers(x, y, c):
    return [(x, y, 1 - c)] + [_plane_pos(x, y, q) + (c,) for q in (2, 1, 3)]


def _chip_gather_start(shards, lands, name, after=()):
    n, n_after = len(shards), len(after)

    def body(*refs):
        srcs_r, lands_r = refs[:n], refs[n:2 * n]
        send_sems, recv_sems = refs[2 * n + n_after], refs[2 * n + n_after + 1]
        x, y, c, me = _position()
        for a in range(n):
            for k, peer in enumerate(_chip_peers(x, y, c)):
                pltpu.make_async_remote_copy(
                    src_ref=srcs_r[a], dst_ref=lands_r[a].at[me], send_sem=send_sems.at[4 * a + k],
                    recv_sem=recv_sems.at[4 * a + k], device_id=peer, device_id_type=MESH).start()
        refs[-1][...] = jnp.zeros_like(refs[-1])

    hbm = lambda a: pltpu.HBM(a.shape, a.dtype)
    res = pl.pallas_call(
        body, name=name,
        out_shape=(pltpu.SemaphoreType.DMA((4 * n,)), pltpu.SemaphoreType.DMA((4 * n,)),
                   *[hbm(a) for a in shards], *[hbm(a) for a in lands], jax.ShapeDtypeStruct((8, 128), f32)),
        in_specs=[HBM_SPEC] * (2 * n) + [ANY_SPEC] * n_after,
        out_specs=(SEM_SPEC, SEM_SPEC, *[HBM_SPEC] * (2 * n), pl.BlockSpec(memory_space=pltpu.VMEM)),
        input_output_aliases={i: 2 + i for i in range(2 * n)},
        compiler_params=pltpu.CompilerParams(has_side_effects=DATAFLOW_EFFECT),
    )(*[pltpu.with_memory_space_constraint(a, pltpu.HBM) for a in list(shards) + list(lands)], *after)
    return (res[0], res[1], list(res[2:2 + n]), list(res[2 + n:2 + 2 * n])), res[-1]


def _chip_gather_forward(handle, after, name):
    send_sems, recv_sems, srcs, lands = handle
    n = len(srcs)

    def body(*refs):
        srcs_r, lands_r = refs[:n], refs[n:2 * n]
        send1, recv1 = refs[2 * n], refs[2 * n + 1]
        send2, recv2 = refs[2 * n + 3], refs[2 * n + 4]
        x, y, c, me = _position()
        peers = _chip_peers(x, y, c)
        for a in range(n):
            for k, peer in enumerate(peers):
                mk = lambda slot: pltpu.make_async_remote_copy(
                    src_ref=srcs_r[a], dst_ref=lands_r[a].at[slot], send_sem=send1.at[4 * a + k],
                    recv_sem=recv1.at[4 * a + k], device_id=peer, device_id_type=MESH)
                mk(me).wait_send()
                mk(_lin(peer)).wait_recv()
        for a in range(n):
            for k, peer in enumerate(peers[1:]):
                slot = _lin(peer)
                pltpu.make_async_remote_copy(
                    src_ref=lands_r[a].at[slot], dst_ref=lands_r[a].at[slot], send_sem=send2.at[3 * a + k],
                    recv_sem=recv2.at[3 * a + k], device_id=peers[0], device_id_type=MESH).start()

    hbm = lambda a: pltpu.HBM(a.shape, a.dtype)
    res = pl.pallas_call(
        body, name=name,
        out_shape=(pltpu.SemaphoreType.DMA((3 * n,)), pltpu.SemaphoreType.DMA((3 * n,)), *[hbm(a) for a in lands]),
        in_specs=[HBM_SPEC] * (2 * n) + [SEM_SPEC, SEM_SPEC, ANY_SPEC],
        out_specs=(SEM_SPEC, SEM_SPEC, *[HBM_SPEC] * n),
        input_output_aliases={n + i: 2 + i for i in range(n)},
        compiler_params=pltpu.CompilerParams(has_side_effects=DATAFLOW_EFFECT),
    )(*srcs, *lands, send_sems, recv_sems, after)
    return (res[0], res[1], list(res[2:]))


def _chip_gather_wait(handle, after, name):
    send_sems, recv_sems, lands = handle
    n = len(lands)

    def body(*refs):
        lands_r, send2, recv2 = refs[:n], refs[n], refs[n + 1]
        x, y, c, me = _position()
        peers = _chip_peers(x, y, c)
        for a in range(n):
            for k, (px, py, pc) in enumerate(peers[1:]):
                mk = lambda slot: pltpu.make_async_remote_copy(
                    src_ref=lands_r[a].at[slot], dst_ref=lands_r[a].at[slot], send_sem=send2.at[3 * a + k],
                    recv_sem=recv2.at[3 * a + k], device_id=peers[0], device_id_type=MESH)
                mk(_lin((px, py, pc))).wait_send()
                mk(_lin((px, py, 1 - pc))).wait_recv()

    hbm = lambda a: pltpu.HBM(a.shape, a.dtype)
    res = pl.pallas_call(
        body, name=name, out_shape=tuple(hbm(a) for a in lands),
        in_specs=[HBM_SPEC] * n + [SEM_SPEC, SEM_SPEC, ANY_SPEC], out_specs=tuple([HBM_SPEC] * n),
        input_output_aliases={i: i for i in range(n)},
        compiler_params=pltpu.CompilerParams(has_side_effects=DATAFLOW_EFFECT),
    )(*lands, send_sems, recv_sems, after)
    return list(res)


def _own_block_filled(block, me):
    land = lax.empty((N_DEV,) + block.shape, block.dtype)
    return lax.dynamic_update_index_in_dim(land, block, me, 0)


def _staged_copy(src, dst, buf, in_sems, out_sems, rows, chunk):
    n = rows // chunk

    def rd(i):
        return pltpu.make_async_copy(src.at[pl.ds(i * chunk, chunk)], buf.at[i % 2], in_sems.at[i % 2])

    def wr(i):
        return pltpu.make_async_copy(buf.at[i % 2], dst.at[pl.ds(i * chunk, chunk)], out_sems.at[i % 2])

    rd(0).start()
    for i in range(n):
        if i + 1 < n:
            if i >= 1:
                wr(i - 1).wait()
            rd(i + 1).start()
        rd(i).wait()
        wr(i).start()
    for i in range(max(n - 2, 0), n):
        wr(i).wait()


def _all_gather_2level(shards, name):
    n = len(shards)
    chunks = [_pick(s.shape[0], (416, 512, 256, 160, 128, 64, 16)) for s in shards]

    def body(*refs):
        ins, outs = refs[:n], refs[n:2 * n]
        send_sems, recv_sems, in_sems, out_sems = refs[2 * n:2 * n + 4]
        bufs = refs[2 * n + 4:]
        x, y, c, me = _position()
        sib, xn, yn, dg = (x, y, 1 - c), (1 - x, y, c), (x, 1 - y, c), (1 - x, 1 - y, c)

        def cp(a, k, src, slot, to):
            return pltpu.make_async_remote_copy(src_ref=src, dst_ref=outs[a].at[slot], send_sem=send_sems.at[a, k],
                                                recv_sem=recv_sems.at[a, k], device_id=to, device_id_type=MESH)

        for a in range(n):
            for k, to in ((0, sib), (1, xn), (2, yn)):
                cp(a, k, ins[a], me, to).start()
        for a in range(n):
            cp(a, 1, ins[a], _lin(xn), xn).wait_recv()
            cp(a, 3, outs[a].at[_lin(xn)], _lin(xn), sib).start()

            @pl.when(c == 0)
            def _():
                cp(a, 5, outs[a].at[_lin(xn)], _lin(xn), yn).start()

            cp(a, 2, ins[a], _lin(yn), yn).wait_recv()
            cp(a, 4, outs[a].at[_lin(yn)], _lin(yn), sib).start()

            @pl.when(c == 1)
            def _():
                cp(a, 5, outs[a].at[_lin(yn)], _lin(yn), xn).start()

        for a in range(n):
            cp(a, 5, ins[a], _lin(dg), xn).wait_recv()
            cp(a, 6, outs[a].at[_lin(dg)], _lin(dg), sib).start()
        for a in range(n):
            _staged_copy(ins[a], outs[a].at[me], bufs[a], in_sems.at[a], out_sems.at[a], shards[a].shape[0], chunks[a])
        for a in range(n):
            for k, origin in ((0, sib), (3, (1 - x, y, 1 - c)), (4, (x, 1 - y, 1 - c)), (6, (1 - x, 1 - y, 1 - c))):
                cp(a, k, ins[a], _lin(origin), sib).wait_recv()
            for k in range(7):
                cp(a, k, ins[a], me, sib).wait_send()

    any_spec = pl.BlockSpec(memory_space=pl.ANY)
    return pl.pallas_call(
        body, name=name, out_shape=[jax.ShapeDtypeStruct((N_DEV,) + s.shape, s.dtype) for s in shards],
        in_specs=[any_spec] * n, out_specs=[any_spec] * n,
        scratch_shapes=[pltpu.SemaphoreType.DMA((n, 7)), pltpu.SemaphoreType.DMA((n, 7)),
                        pltpu.SemaphoreType.DMA((n, 2)), pltpu.SemaphoreType.DMA((n, 2))]
        + [pltpu.VMEM((2, ch, s.shape[1]), s.dtype) for ch, s in zip(chunks, shards)],
    )(*shards)


def _plane_pos(x, y, q):
    return ((1 - x) if q & 2 else x, (1 - y) if q & 1 else y)


ANY_SPEC = pl.BlockSpec(memory_space=pl.ANY)


def _mm(a, b, name, ta=False, tb=False, out_dtype=f32, after=()):
    if ta:
        k_dim, m_dim = a.shape
    else:
        m_dim, k_dim = a.shape
    if tb:
        n_dim, k2 = b.shape
    else:
        k2, n_dim = b.shape
    assert k_dim == k2, (a.shape, b.shape)
    assert a.dtype == bf16 and b.dtype == bf16
    bm = _pick(m_dim, (512, 768, 640, 256, 128))
    bn = _pick(n_dim, (512, 640, 256, 128))
    bk = k_dim if k_dim <= 2560 else _pick(k_dim, (1024, 1280, 768, 512))
    nk = k_dim // bk
    a_spec = (pl.BlockSpec((bk, bm), lambda i, j, k: (k, i)) if ta
              else pl.BlockSpec((bm, bk), lambda i, j, k: (i, k)))
    b_spec = (pl.BlockSpec((bn, bk), lambda i, j, k: (j, k)) if tb
              else pl.BlockSpec((bk, bn), lambda i, j, k: (k, j)))
    dims = (((0 if ta else 1,), (1 if tb else 0,)), ((), ()))

    n_after = len(after)

    def body_single(a_ref, b_ref, *rest):
        o_ref = rest[n_after]
        o_ref[...] = lax.dot_general(a_ref[...], b_ref[...], dims, preferred_element_type=f32).astype(o_ref.dtype)

    def body(a_ref, b_ref, *rest):
        o_ref, acc_ref = rest[n_after:]
        k = pl.program_id(2)

        @pl.when(k == 0)
        def _():
            acc_ref[...] = jnp.zeros_like(acc_ref)

        acc_ref[...] += lax.dot_general(a_ref[...], b_ref[...], dims, preferred_element_type=f32)

        @pl.when(k == nk - 1)
        def _():
            o_ref[...] = acc_ref[...].astype(o_ref.dtype)

    return pl.pallas_call(
        body_single if nk == 1 else body, name=name, out_shape=jax.ShapeDtypeStruct((m_dim, n_dim), out_dtype),
        grid=(m_dim // bm, n_dim // bn, nk), in_specs=[a_spec, b_spec] + [ANY_SPEC] * n_after,
        out_specs=pl.BlockSpec((bm, bn), lambda i, j, k: (i, j)),
        scratch_shapes=[] if nk == 1 else [pltpu.VMEM((bm, bn), f32)],
        compiler_params=_cparams("parallel", "parallel", "arbitrary"),
    )(a, b, *after)


def _rin(arr, width=None, cb=0, roff=0):
    return (arr, arr.shape[1] if width is None else width, cb, roff)


def _rowcall(fn, name, rows, tm, row_ins, par_ins, row_outs, acc_outs=(), after=()):
    nr, npar, nro, n_after = len(row_ins), len(par_ins), len(row_outs), len(after)
    in_specs, args = [], []
    for arr, width, cb, roff in row_ins:
        if roff >= 0:
            imap = lambda i, cb=cb, roff=roff: (i + roff, cb)
        else:
            imap = lambda i, cb=cb, roff=roff: (jnp.maximum(i + roff, 0), cb)
        in_specs.append(pl.BlockSpec((tm, width), imap))
        args.append(arr)
    for p in par_ins:
        in_specs.append(pl.BlockSpec(p.shape, lambda i: (0, 0)))
        args.append(p)
    out_shape, out_specs = [], []
    for width, dt in row_outs:
        out_shape.append(jax.ShapeDtypeStruct((rows, width), dt))
        out_specs.append(pl.BlockSpec((tm, width), lambda i: (i, 0)))
    for p, width in acc_outs:
        out_shape.append(jax.ShapeDtypeStruct((p, width), f32))
        out_specs.append(pl.BlockSpec((p, width), lambda i: (0, 0)))

    def body(*refs):
        i = pl.program_id(0)
        res = fn(i, *[r[...] for r in refs[:nr + npar]])
        outs = refs[nr + npar + n_after:]
        for o, v in zip(outs[:nro], res[:nro]):
            o[...] = v.astype(o.dtype)
        if acc_outs:
            @pl.when(i == 0)
            def _():
                for o in outs[nro:]:
                    o[...] = jnp.zeros_like(o)

            for o, v in zip(outs[nro:], res[nro:]):
                o[...] += v

    return pl.pallas_call(
        body, name=name, out_shape=out_shape, grid=(rows // tm,), in_specs=in_specs + [ANY_SPEC] * n_after,
        out_specs=out_specs, compiler_params=_cparams("arbitrary"),
    )(*args, *after)


def _rms(x, g):
    return x * lax.rsqrt(jnp.mean(x * x, axis=-1, keepdims=True) + EPS) * g


def _normmod(x, g, sc, sh):
    return _rms(x, g) * (1.0 + sc) + sh


def _gelu(x):
    return 0.5 * x * (1.0 + jnp.tanh(0.7978845608028654 * (x + 0.044715 * (x * x * x))))


def _sigmoid(x):
    return 0.5 * (jnp.tanh(0.5 * x) + 1.0)


def _coeff_parts(pre_a, pre_x, ba, bx, lam):
    r = _sigmoid(pre_a + ba)
    ig = _sigmoid(pre_x + bx)
    nl = -lam
    sp = jnp.maximum(nl, 0.0) + jnp.log(1.0 + jnp.exp(-jnp.abs(nl)))
    la = -RG_C * r * sp
    a = jnp.exp(la)
    one_minus_a2 = -jnp.tanh(la) * (a * a + 1.0)
    inv_m = lax.rsqrt(one_minus_a2)
    return r, ig, sp, a, one_minus_a2 * inv_m, inv_m


def _coeff(pre_a, pre_x, u, ba, bx, lam):
    _, ig, _, a, m, _ = _coeff_parts(pre_a, pre_x, ba, bx, lam)
    return a, m * (ig * u)


def _coeff_bwd(pre_a, pre_x, u, ba, bx, lam, da, db):
    r, ig, sp, a, m, inv_m = _coeff_parts(pre_a, pre_x, ba, bx, lam)
    dbu = db * u
    dig = dbu * m
    dm = dbu * ig
    dla = a * (da - dm * a * inv_m)
    dpa = dla * (-RG_C * sp) * (r * (1.0 - r))
    dpx = dig * (ig * (1.0 - ig))
    dsp = jnp.sum(dla * (-RG_C * r), axis=0, keepdims=True)
    dlam = -dsp * _sigmoid(-lam)
    return (dpa, dpx, db * m * ig, jnp.sum(dpa, axis=0, keepdims=True), jnp.sum(dpx, axis=0, keepdims=True), dlam)


SCAN_CHUNK = 256


def _scan_call(a, v, chunk_of, reverse, name, backward):
    rows, width = a.shape
    n_out = 1 if backward else 2
    nt = SCAN_CHUNK // 8

    def body(a_ref, v_ref, *rest):
        outs, state_ref = rest[:-1], rest[-1]

        @pl.when(pl.program_id(0) == 0)
        def _():
            state_ref[...] = jnp.zeros_like(state_ref)

        rid = lax.broadcasted_iota(jnp.int32, (8, width), 0)

        def tile(j, st):
            t0 = pl.multiple_of((nt - 1 - j if reverse else j) * 8, 8)
            at = a_ref[pl.ds(t0, 8), :]
            vt = v_ref[pl.ds(t0, 8), :]
            out = jnp.zeros((8, width), f32)
            prev = jnp.zeros((8, width), f32)
            for i in (range(7, -1, -1) if reverse else range(8)):
                if backward:
                    g = vt[i:i + 1] + st
                    st = at[i:i + 1] * g
                    out = jnp.where(rid == i, g, out)
                else:
                    prev = jnp.where(rid == i, st, prev)
                    st = at[i:i + 1] * st + vt[i:i + 1]
                    out = jnp.where(rid == i, st, out)
            outs[0][pl.ds(t0, 8), :] = out
            if not backward:
                outs[1][pl.ds(t0, 8), :] = prev
            return st

        state_ref[0:1, :] = lax.fori_loop(0, nt, tile, state_ref[0:1, :])

    spec = pl.BlockSpec((SCAN_CHUNK, width), lambda t: (chunk_of(t), 0))
    return pl.pallas_call(
        body, name=name, out_shape=[jax.ShapeDtypeStruct((rows, width), f32)] * n_out,
        grid=(rows // SCAN_CHUNK,), in_specs=[spec, spec], out_specs=[spec] * n_out,
        scratch_shapes=[pltpu.VMEM((8, width), f32)],
        compiler_params=_cparams("arbitrary"),
    )(a, v)


CONV_CHUNK = 256


def _fill_padded(pad_ref, src_ref, start, n):
    cb = pad_ref.shape[1]
    pad_ref[pl.ds(0, HALO), :] = jnp.zeros((HALO, cb), f32)
    pad_ref[pl.ds(HALO, n), :] = src_ref[pl.ds(start, n), :].astype(f32)
    pad_ref[pl.ds(HALO + n, HALO), :] = jnp.zeros((HALO, cb), f32)


def _dwconv_fwd(x, x_cb0, w, b, taps, pad_left, segments, cb, name, emit_bf16):
    rows = x.shape[0]
    width = w.shape[1]

    def body(x_ref, w_ref, b_ref, *rest):
        outs, xp = rest[:-1], rest[-1]
        for start, n in segments:
            _fill_padded(xp, x_ref, start, n)
            for c0 in range(0, n, CONV_CHUNK):
                acc = jnp.zeros((CONV_CHUNK, cb), f32) + b_ref[...]
                for k in range(taps):
                    acc = acc + w_ref[k:k + 1, :] * xp[pl.ds(HALO + c0 + k - pad_left, CONV_CHUNK), :]
                for o in outs:
                    o[pl.ds(start + c0, CONV_CHUNK), :] = acc.astype(o.dtype)

    out_dtypes = [f32, bf16] if emit_bf16 else [f32]
    return pl.pallas_call(
        body, name=name, out_shape=[jax.ShapeDtypeStruct((rows, width), dt) for dt in out_dtypes],
        grid=(width // cb,),
        in_specs=[pl.BlockSpec((rows, cb), lambda j: (0, j + x_cb0)), pl.BlockSpec((taps, cb), lambda j: (0, j)),
                  pl.BlockSpec((1, cb), lambda j: (0, j))],
        out_specs=[pl.BlockSpec((rows, cb), lambda j: (0, j))] * len(out_dtypes),
        scratch_shapes=[pltpu.VMEM((rows + 2 * HALO, cb), f32)],
        compiler_params=_cparams("parallel"),
    )(x, w, b)


def _dwconv_bwd(douts, x, x_cb0, w, taps, pad_left, segments, cb, name, dx_dtype):
    rows = x.shape[0]
    width = w.shape[1]
    nd = len(douts)

    def body(*refs):
        d_refs, x_ref, w_ref = refs[:nd], refs[nd], refs[nd + 1]
        dx_ref, dw_ref, db_ref, dp, dsum = refs[nd + 2:]
        dw_ref[...] = jnp.zeros_like(dw_ref)
        db_ref[...] = jnp.zeros_like(db_ref)
        if nd > 1:
            total = d_refs[0][...]
            for r in d_refs[1:]:
                total = total + r[...]
            dsum[...] = total
            d_ref = dsum
        else:
            d_ref = d_refs[0]
        for start, n in segments:
            _fill_padded(dp, d_ref, start, n)
            for c0 in range(0, n, CONV_CHUNK):
                db_ref[...] += jnp.sum(dp[pl.ds(HALO + c0, CONV_CHUNK), :], axis=0, keepdims=True)
                xchunk = x_ref[pl.ds(start + c0, CONV_CHUNK), :].astype(f32)
                acc = jnp.zeros((CONV_CHUNK, cb), f32)
                for k in range(taps):
                    shifted = dp[pl.ds(HALO + c0 + pad_left - k, CONV_CHUNK), :]
                    acc = acc + w_ref[k:k + 1, :] * shifted
                    dw_ref[k:k + 1, :] += jnp.sum(shifted * xchunk, axis=0, keepdims=True)
                dx_ref[pl.ds(start + c0, CONV_CHUNK), :] = acc.astype(dx_ref.dtype)

    dspec = pl.BlockSpec((rows, cb), lambda j: (0, j))
    return pl.pallas_call(
        body, name=name,
        out_shape=[jax.ShapeDtypeStruct((rows, width), dx_dtype), jax.ShapeDtypeStruct((taps, width), f32),
                   jax.ShapeDtypeStruct((1, width), f32)],
        grid=(width // cb,),
        in_specs=[dspec] * nd + [pl.BlockSpec((rows, cb), lambda j: (0, j + x_cb0)),
                                 pl.BlockSpec((taps, cb), lambda j: (0, j))],
        out_specs=[dspec, pl.BlockSpec((taps, cb), lambda j: (0, j)), pl.BlockSpec((1, cb), lambda j: (0, j))],
        scratch_shapes=[pltpu.VMEM((rows + 2 * HALO, cb), f32), pltpu.VMEM((rows, cb), f32)],
        compiler_params=_cparams("parallel"),
    )(*douts, x, w)


def _ada_forward(c16, w_ada, b_loc):
    def body(c_ref, w_ref, b_ref, o_ref):
        cv = c_ref[...]
        s = (cv * _sigmoid(cv)).astype(bf16)
        o_ref[0] = jnp.dot(s, w_ref[0].astype(bf16), preferred_element_type=f32) + b_ref[0]

    return pl.pallas_call(
        body, name="ada_forward", out_shape=jax.ShapeDtypeStruct((2, 16, ADA_SHARD), f32), grid=(2,),
        in_specs=[pl.BlockSpec((16, D), lambda l: (0, 0)), pl.BlockSpec((1, D, ADA_SHARD), lambda l: (l, 0, 0)),
                  pl.BlockSpec((1, 1, ADA_SHARD), lambda l: (l, 0, 0))],
        out_specs=pl.BlockSpec((1, 16, ADA_SHARD), lambda l: (l, 0, 0)),
        compiler_params=_cparams("parallel"),
    )(c16, w_ada, b_loc)


def _ada_backward(c16, g16, w_ada):
    def body(c_ref, g_ref, w_ref, dw_ref, ds_ref):
        cv = c_ref[...]
        s = (cv * _sigmoid(cv)).astype(bf16)
        g = g_ref[0].astype(bf16)
        dw_ref[0] = lax.dot_general(s, g, (((0,), (0,)), ((), ())), preferred_element_type=f32)
        ds = lax.dot_general(g, w_ref[0].astype(bf16), (((1,), (1,)), ((), ())), preferred_element_type=f32)
        cc = cv[8:9]
        sg = _sigmoid(cc)
        dsilu = sg * (1.0 + cc * (1.0 - sg))
        ds_ref[0] = jnp.zeros((8, D), f32) + jnp.sum(ds[8:16], axis=0, keepdims=True) * dsilu

    return pl.pallas_call(
        body, name="ada_backward",
        out_shape=[jax.ShapeDtypeStruct((2, D, ADA_SHARD), f32), jax.ShapeDtypeStruct((2, 8, D), f32)], grid=(2,),
        in_specs=[pl.BlockSpec((16, D), lambda l: (0, 0)), pl.BlockSpec((1, 16, ADA_SHARD), lambda l: (l, 0, 0)),
                  pl.BlockSpec((1, D, ADA_SHARD), lambda l: (l, 0, 0))],
        out_specs=[pl.BlockSpec((1, D, ADA_SHARD), lambda l: (l, 0, 0)), pl.BlockSpec((1, 8, D), lambda l: (l, 0, 0))],
        compiler_params=_cparams("parallel"),
    )(c16, g16, w_ada)


def _adamw(pieces, w, m, v, name, after=()):
    rows, cols = w.shape
    n_arr, n_after = len(pieces), len(after)
    counts = [cnt for _, cnt in pieces]
    pieces = [p for p, _ in pieces]
    tm = 256 if (rows % 256 == 0 and rows > 256) else rows

    def body(*refs):
        p_refs = refs[:n_arr]
        w_ref, m_ref, v_ref = refs[n_arr:n_arr + 3]
        g_ref, d_ref, nm_ref, nv_ref = refs[n_arr + 3 + n_after:]
        g = None
        for p_ref in p_refs:
            for j in range(p_ref.shape[0]):
                term = p_ref[j].astype(f32)
                g = term if g is None else g + term
        m2 = ADAM_B1 * m_ref[...] + (1.0 - ADAM_B1) * g
        v2 = ADAM_B2 * v_ref[...] + (1.0 - ADAM_B2) * (g * g)
        m_hat = m2 / (1.0 - ADAM_B1 ** ADAM_STEP)
        v_hat = v2 / (1.0 - ADAM_B2 ** ADAM_STEP)
        g_ref[...] = g
        d_ref[...] = -ADAM_LR * (m_hat / (jnp.sqrt(v_hat) + ADAM_EPS) + ADAM_WD * w_ref[...])
        nm_ref[...] = m2
        nv_ref[...] = v2

    spec = pl.BlockSpec((tm, cols), lambda i: (i, 0))
    return pl.pallas_call(
        body, name=name, out_shape=[jax.ShapeDtypeStruct((rows, cols), f32)] * 4, grid=(rows // tm,),
        in_specs=[pl.BlockSpec((cnt, tm, cols), lambda i: (0, i, 0)) for cnt in counts] + [spec, spec, spec]
        + [ANY_SPEC] * n_after,
        out_specs=[spec] * 4, compiler_params=_cparams("parallel"),
    )(*pieces, w, m, v, *after)


MLP_TM = 256
FB = F // N_DEV


def _stack_rows(vals, n):
    cols = vals[0].shape[1]
    rid = lax.broadcasted_iota(jnp.int32, (n, cols), 0)
    out = jnp.zeros((n, cols), f32)
    for k, v in enumerate(vals):
        out = jnp.where(rid == k, v, out)
    return out


N_MLP_PARAMS = 9


class _ParamRows:
    def __init__(self, ref):
        self.ref = ref

    def __getitem__(self, sl):
        return self.ref[8 * sl.start:8 * sl.start + 1, :]


def _resident(shape, imap):
    return pl.BlockSpec(shape, imap, pipeline_mode=pl.Buffered(1))


def _mlp_forward(xa, xa_roff, out_prev, par, w_in, w_out, layer, name):
    def body(xa_ref, op_ref, par_ref, win_ref, wout_ref, x1_ref, h_ref, r_ref, mo_ref, x2_ref, hn_ref):
        p = _ParamRows(par_ref)
        x1 = xa_ref[...] + p[0:1] * (op_ref[...] + p[1:2])
        h = _normmod(x1, p[2:3], p[3:4], p[4:5]).astype(bf16)
        x1_ref[...] = x1
        h_ref[...] = h
        mo = jnp.zeros((MLP_TM, D), f32)
        for j in range(N_DEV):
            r = jnp.maximum(jnp.dot(h, win_ref[j], preferred_element_type=f32), 0.0)
            r_ref[:, j * FB:(j + 1) * FB] = r.astype(bf16)
            mo = mo + jnp.dot((r * r).astype(bf16), wout_ref[j], preferred_element_type=f32)
        mo_ref[...] = mo.astype(bf16)
        x2 = x1 + p[5:6] * mo
        x2_ref[...] = x2
        hn_ref[...] = _normmod(x2, p[6:7], p[7:8], p[8:9]).astype(bf16)

    row = lambda width: pl.BlockSpec((MLP_TM, width), lambda i: (i, 0))
    return pl.pallas_call(
        body, name=name, grid=(T_LAT // MLP_TM,),
        out_shape=[jax.ShapeDtypeStruct((T_LAT, D), f32), jax.ShapeDtypeStruct((T_LAT, D), bf16),
                   jax.ShapeDtypeStruct((T_LAT, F), bf16), jax.ShapeDtypeStruct((T_LAT, D), bf16),
                   jax.ShapeDtypeStruct((T_LAT, D), f32), jax.ShapeDtypeStruct((T_LAT, D), bf16)],
        in_specs=[pl.BlockSpec((MLP_TM, D), lambda i: (i + xa_roff, 0)), row(D), pl.BlockSpec((8 * N_MLP_PARAMS, D), lambda i: (0, 0)),
                  _resident((N_DEV, None, D, FB), lambda i: (0, layer, 0, 0)),
                  _resident((N_DEV, None, FB, D), lambda i: (0, layer, 0, 0))],
        out_specs=[row(D), row(D), row(F), row(D), row(D), row(D)],
        compiler_params=_cparams("parallel"),
    )(xa, out_prev, par, w_in, w_out)


def _mlp_backward(dx2, x1, r, mo, out_prev, par, w_in, w_out, layer, name, after=()):
    nt = (((1,), (1,)), ((), ()))

    n_after = len(after)

    def body(dx2_ref, x1_ref, r_ref, mo_ref, op_ref, par_ref, win_ref, wout_ref, *rest):
        dx1_ref, dop_ref, dmo_ref, dhid_ref, acc_ref = rest[n_after:]
        p = _ParamRows(par_ref)
        dx2v = dx2_ref[...]
        dmo = (p[5:6] * dx2v).astype(bf16)
        dmo_ref[...] = dmo
        dh = jnp.zeros((MLP_TM, D), f32)
        mo = mo_ref[...].astype(f32)
        for j in range(N_DEV):
            rf = r_ref[:, j * FB:(j + 1) * FB].astype(f32)
            dact = lax.dot_general(dmo, wout_ref[j], nt, preferred_element_type=f32)
            dhid = (dact * (2.0 * rf)).astype(bf16)
            dhid_ref[:, j * FB:(j + 1) * FB] = dhid
            dh = dh + lax.dot_general(dhid, win_ref[j], nt, preferred_element_type=f32)
        x1 = x1_ref[...]
        _, vjp = jax.vjp(_normmod, x1, p[2:3], p[3:4], p[4:5])
        dx, dng, dsc, dsh = vjp(dh)
        dx1 = dx2v + dx
        dx1_ref[...] = dx1
        dop_ref[...] = (p[0:1] * dx1).astype(bf16)
        sums = _stack_rows([jnp.sum(dx1 * (op_ref[...] + p[1:2]), axis=0, keepdims=True),
                            p[0:1] * jnp.sum(dx1, axis=0, keepdims=True), dng, dsc, dsh,
                            jnp.sum(dx2v * mo, axis=0, keepdims=True)], 8)

        @pl.when(pl.program_id(0) == 0)
        def _():
            acc_ref[...] = jnp.zeros_like(acc_ref)

        acc_ref[...] += sums

    row = lambda width: pl.BlockSpec((MLP_TM, width), lambda i: (i, 0))
    return pl.pallas_call(
        body, name=name, grid=(T_LAT // MLP_TM,),
        out_shape=[jax.ShapeDtypeStruct((T_LAT, D), f32), jax.ShapeDtypeStruct((T_LAT, D), bf16),
                   jax.ShapeDtypeStruct((T_LAT, D), bf16), jax.ShapeDtypeStruct((T_LAT, F), bf16),
                   jax.ShapeDtypeStruct((8, D), f32)],
        in_specs=[row(D), row(D), row(F), row(D), row(D), pl.BlockSpec((8 * N_MLP_PARAMS, D), lambda i: (0, 0)),
                  _resident((N_DEV, None, D, FB), lambda i: (0, layer, 0, 0)),
                  _resident((N_DEV, None, FB, D), lambda i: (0, layer, 0, 0))] + [ANY_SPEC] * n_after,
        out_specs=[row(D), row(D), row(D), row(F), pl.BlockSpec((8, D), lambda i: (0, 0))],
        compiler_params=_cparams("arbitrary"),
    )(dx2, x1, r, mo, out_prev, par, w_in, w_out, *after)


def _mlp_weight_grads(h, dhid, r, dmo, layer, other, tag):
    tn = (((0,), (0,)), ((), ()))

    def body_in(h_ref, dhid_ref, *rest):
        rest[-1][...] = lax.dot_general(h_ref[...], dhid_ref[...], tn, preferred_element_type=f32).astype(bf16)

    def body_out(r_ref, dmo_ref, *rest):
        rf = r_ref[...].astype(f32)
        rest[-1][...] = lax.dot_general((rf * rf).astype(bf16), dmo_ref[...], tn,
                                        preferred_element_type=f32).astype(bf16)

    def call(body, name, operands, specs, block, prev):
        extra = [] if prev is None else [prev]
        return pl.pallas_call(
            body, name=name, grid=(N_DEV,), out_shape=jax.ShapeDtypeStruct((N_DEV, 2) + block, bf16),
            in_specs=specs + [pl.BlockSpec(memory_space=pl.ANY)] * len(extra),
            out_specs=pl.BlockSpec((None, None) + block, lambda j: (j, layer, 0, 0)),
            input_output_aliases={} if prev is None else {2: 0},
            compiler_params=_cparams("parallel"),
        )(*operands, *extra)

    dw_in = call(body_in, tag + "_mlp_in_dw", [h, dhid],
                 [_resident((T_LAT, D), lambda j: (0, 0)), pl.BlockSpec((T_LAT, FB), lambda j: (0, j))], (D, FB),
                 None if other is None else other[0])
    dw_out = call(body_out, tag + "_mlp_out_dw", [r, dmo],
                  [pl.BlockSpec((T_LAT, FB), lambda j: (0, j)), _resident((T_LAT, D), lambda j: (0, 0))], (FB, D),
                  None if other is None else other[1])
    return dw_in, dw_out


def _pos_embed():
    n_rows = T_LAT // GRID_W
    q = D // 4
    omega = 1.0 / (POS_BASE ** (jnp.arange(q, dtype=f32) / q))
    er = jnp.arange(n_rows, dtype=jnp.int32).astype(f32)[:, None] * omega[None, :]
    ec = jnp.arange(GRID_W, dtype=jnp.int32).astype(f32)[:, None] * omega[None, :]
    by_row = jnp.concatenate([jnp.sin(er), jnp.cos(er)], axis=-1)[:, None, :]
    by_col = jnp.concatenate([jnp.sin(ec), jnp.cos(ec)], axis=-1)[None, :, :]
    full = jnp.concatenate([jnp.broadcast_to(by_row, (n_rows, GRID_W, D // 2)),
                            jnp.broadcast_to(by_col, (n_rows, GRID_W, D // 2))], axis=-1)
    return full.reshape(T_LAT, D)


HALF = R // 2
BLK_PER_HALF = N_BLK // 2
N_PARTS = 4


def _gate_matrix(w_a, w_x):
    eye = jnp.eye(BLK_PER_HALF, dtype=bf16)
    cols = []
    for h in range(2):
        for d in range(2):
            for w in (w_a, w_x):
                blocks = w[d, BLK_PER_HALF * h:BLK_PER_HALF * (h + 1)].astype(bf16)
                cols.append(jnp.einsum("hij,hg->higj", blocks, eye).reshape(HALF, HALF))
    return jnp.concatenate(cols, axis=1)


def _gate_blocks(dwg, part):
    out = []
    for h in range(2):
        blk = dwg[:, (N_PARTS * h + part) * HALF:(N_PARTS * h + part + 1) * HALF]
        blk = blk.reshape(BLK_PER_HALF, BLK, BLK_PER_HALF, BLK)
        out.append(jnp.moveaxis(jnp.diagonal(blk, axis1=0, axis2=2), -1, 0))
    return jnp.concatenate(out, axis=0)


GATE_BM = 768


def _gates_dx(dpre, wg, after=()):
    rows = dpre.shape[0]
    n_after = len(after)

    def body(d_ref, w_ref, *rest):
        rest[n_after][...] = lax.dot_general(d_ref[...], w_ref[...], (((1,), (1,)), ((), ())),
                                             preferred_element_type=f32)

    return pl.pallas_call(
        body, name="l0_gates_dx", grid=(rows // GATE_BM, 2), out_shape=jax.ShapeDtypeStruct((rows, R), f32),
        in_specs=[pl.BlockSpec((GATE_BM, N_PARTS * HALF), lambda i, h: (i, h)),
                  pl.BlockSpec((HALF, N_PARTS * HALF), lambda i, h: (0, h))] + [ANY_SPEC] * n_after,
        out_specs=pl.BlockSpec((GATE_BM, HALF), lambda i, h: (i, h)),
        compiler_params=_cparams("parallel", "parallel"),
    )(dpre, wg, *after)


COEFF_TM = 256


def _dir_params(d, *params):
    specs = [pl.BlockSpec((None, 1, HALF), lambda h, i: (d, 0, h))] * len(params)
    return specs, [p.reshape(2, 1, R) for p in params]


def _gates_coeff_fwd(ub, u, wg, ba, bx, lam, d):
    rows = u.shape[0]

    def body(ub_ref, u_ref, w_ref, ba_ref, bx_ref, lam_ref, a_ref, b_ref):
        pre = jnp.dot(ub_ref[...], w_ref[...], preferred_element_type=f32)
        a, b = _coeff(pre[:, :HALF], pre[:, HALF:], u_ref[...], ba_ref[...], bx_ref[...], lam_ref[...])
        a_ref[...] = a
        b_ref[...] = b

    tile = pl.BlockSpec((COEFF_TM, HALF), lambda h, i: (i, h))
    pspecs, pargs = _dir_params(d, ba, bx, lam)
    return pl.pallas_call(
        body, name=f"l0_gates_coeff_{d}", grid=(2, rows // COEFF_TM),
        out_shape=[jax.ShapeDtypeStruct((rows, R), f32)] * 2,
        in_specs=[tile, tile, pl.BlockSpec((HALF, 2 * HALF), lambda h, i: (0, 2 * h + d))] + pspecs,
        out_specs=[tile, tile], compiler_params=_cparams("parallel", "parallel"),
    )(ub, u, wg, *pargs)


def _gates_coeff_bwd(ub, u, dh, yp, wg, ba, bx, lam, d, dpre_prev):
    rows = u.shape[0]
    n_prev = 0 if dpre_prev is None else 1

    def body(ub_ref, u_ref, dh_ref, yp_ref, w_ref, ba_ref, bx_ref, lam_ref, *rest):
        dpre_ref, du_ref, dba_ref, dbx_ref, dlam_ref = rest[n_prev:]
        pre = jnp.dot(ub_ref[...], w_ref[...], preferred_element_type=f32)
        dhv = dh_ref[...]
        dpa, dpx, du, dba, dbx, dlam = _coeff_bwd(pre[:, :HALF], pre[:, HALF:], u_ref[...], ba_ref[...], bx_ref[...],
                                                  lam_ref[...], dhv * yp_ref[...], dhv)
        dpre_ref[:, :HALF] = dpa.astype(bf16)
        dpre_ref[:, HALF:] = dpx.astype(bf16)
        du_ref[...] = du

        @pl.when(pl.program_id(1) == 0)
        def _():
            dba_ref[...] = jnp.zeros_like(dba_ref)
            dbx_ref[...] = jnp.zeros_like(dbx_ref)
            dlam_ref[...] = jnp.zeros_like(dlam_ref)

        dba_ref[...] += dba
        dbx_ref[...] += dbx
        dlam_ref[...] += dlam

    tile = pl.BlockSpec((COEFF_TM, HALF), lambda h, i: (i, h))
    acc = pl.BlockSpec((1, HALF), lambda h, i: (0, h))
    pspecs, pargs = _dir_params(d, ba, bx, lam)
    extra = [] if dpre_prev is None else [dpre_prev]
    return pl.pallas_call(
        body, name=f"l0_gates_coeff_bwd_{d}", grid=(2, rows // COEFF_TM),
        out_shape=[jax.ShapeDtypeStruct((rows, 2 * N_PARTS * HALF), bf16), jax.ShapeDtypeStruct((rows, R), f32)]
        + [jax.ShapeDtypeStruct((1, R), f32)] * 3,
        in_specs=[tile] * 4 + [pl.BlockSpec((HALF, 2 * HALF), lambda h, i: (0, 2 * h + d))] + pspecs
        + [ANY_SPEC] * n_prev,
        out_specs=[pl.BlockSpec((COEFF_TM, 2 * HALF), lambda h, i: (i, 2 * h + d)), tile, acc, acc, acc],
        input_output_aliases={8: 0} if n_prev else {}, compiler_params=_cparams("parallel", "arbitrary"),
    )(ub, u, dh, yp, wg, *pargs, *extra)


def _gates_dw(u, dpre):
    rows = u.shape[0]

    def body(u_ref, d_ref, o_ref):
        o_ref[...] = lax.dot_general(u_ref[...], d_ref[...], (((0,), (0,)), ((), ())), preferred_element_type=f32)

    return pl.pallas_call(
        body, name="l0_gates_dw", grid=(2 * N_PARTS,), out_shape=jax.ShapeDtypeStruct((HALF, 2 * N_PARTS * HALF), f32),
        in_specs=[pl.BlockSpec((rows, HALF), lambda j: (0, j // N_PARTS)), pl.BlockSpec((rows, HALF), lambda j: (0, j))],
        out_specs=pl.BlockSpec((HALF, HALF), lambda j: (0, j)), compiler_params=_cparams("parallel"),
    )(u, dpre)


N_SCAN_CHUNKS = T_ALL // SCAN_CHUNK
SCAN_FWD = lambda t: t
SCAN_FWD_BWD = lambda t: N_SCAN_CHUNKS - 1 - t
SCAN_REV = lambda t: jnp.where(t == 0, 0, N_SCAN_CHUNKS - t)
SCAN_REV_BWD = lambda t: jnp.where(t == N_SCAN_CHUNKS - 1, 0, t + 1)
CONV_SEGMENTS = ((0, T_CTX), (T_CTX, T_LAT))
TM = 128
FUSED_TM = 256


def _local_step(x, ctx, target, mods, cmod, wts, late_weights, send_grads, reduce_loss, start_after=()):
    sh1, sc1, g1, sh2, sc2, g2 = [[mods[l, i][None] for l in range(2)] for i in range(N_MOD)]
    ng = wts["norm_g"]
    xcat = jnp.concatenate([ctx, x], axis=0)
    poscat = jnp.concatenate([jnp.zeros((T_CTX, D), f32), _pos_embed()], axis=0)
    scp = jnp.concatenate([cmod[1][None], sc1[0]], axis=0)
    shp = jnp.concatenate([cmod[0][None], sh1[0]], axis=0)

    ctx_tiles = T_CTX // FUSED_TM
    nt = (((1,), (1,)), ((), ()))

    def blend(i, p):
        sel = jnp.where(i < ctx_tiles, 1.0, 0.0)
        return sel * p[0:1] + (1.0 - sel) * p[1:2]

    def f_pre0(i, xc, pos, g, scp_, shp_, w):
        x0 = xc + pos
        h = _normmod(x0, g, blend(i, scp_), blend(i, shp_)).astype(bf16)
        return x0, h, jnp.dot(h, w, preferred_element_type=f32)

    x0cat, h0, gr = _rowcall(f_pre0, "l0_prenorm_in_proj", T_ALL, FUSED_TM, [_rin(xcat), _rin(poscat)],
                             [ng[0, 0][None], scp, shp, wts["rec_w_in"]], [(D, f32), (D, bf16), (2 * R, f32)],
                             after=start_after)
    u, ub = _dwconv_fwd(gr, R // 256, wts["rec_conv_w"], wts["rec_conv_b"], 4, 1, CONV_SEGMENTS, 256,
                        "l0_conv", True)
    gate_args = (wts["gates"], wts["rec_b_a"], wts["rec_b_x"], wts["rec_lambda"])
    a0, b0 = _gates_coeff_fwd(ub, u, *gate_args, 0)
    a1, b1 = _gates_coeff_fwd(ub, u, *gate_args, 1)
    y0, yp0 = _scan_call(a0, b0, SCAN_FWD, False, "l0_scan_fwd", False)
    y1, yp1 = _scan_call(a1, b1, SCAN_REV, True, "l0_scan_rev", False)

    wts = dict(wts, **late_weights("mlp", y1))

    def f_gate_out(i, gp, y0_, y1_, w):
        z = (_gelu(gp) * (y0_ + y1_)).astype(bf16)
        return z, jnp.dot(z, w, preferred_element_type=f32)

    zb, out0 = _rowcall(f_gate_out, "l0_gate_out_proj", T_LAT, FUSED_TM,
                        [_rin(gr, R, 0, ctx_tiles), _rin(y0, None, 0, ctx_tiles), _rin(y1, None, 0, ctx_tiles)],
                        [wts["rec_w_out"]], [(R, bf16), (D, f32)])

    zero_d = jnp.zeros((1, D), f32)

    def mlp_params(rows):
        rows = rows + [zero_d] * (N_MLP_PARAMS - len(rows))
        return jnp.concatenate([jnp.broadcast_to(r, (8, D)) for r in rows], axis=0)

    par0 = mlp_params([g1[0], zero_d, ng[0, 1][None], sc2[0], sh2[0], g2[0], ng[1, 0][None], sc1[1], sh1[1]])
    x1, h1, r0, mo0, x2, h2 = _mlp_forward(x0cat, T_CTX // MLP_TM, out0, par0, wts["mlp_w_in"], wts["mlp_w_out"], 0,
                                           "l0_mlp")

    wts = dict(wts, **late_weights("conf", x2))
    def glu(pa, pb, b1):
        return (pa + b1[:, :D]) * _sigmoid(pb + b1[:, D:])

    def f_pw1_glu(i, h_, b1, w):
        p = jnp.dot(h_, w, preferred_element_type=f32)
        return glu(p[:, :D], p[:, D:], b1), p

    zg, pw = _rowcall(f_pw1_glu, "l1_pw1_glu", T_LAT, FUSED_TM, [_rin(h2)], [wts["conf_b_pw1"], wts["conf_w_pw1"]],
                      [(D, f32), (2 * D, bf16)])
    (zc,) = _dwconv_fwd(zg, 0, wts["conf_conv_w"], wts["conf_conv_b"], 31, 15, ((0, T_LAT),), 128, "l1_conv", False)

    def ln_silu(z, lg, lb):
        mu = jnp.mean(z, axis=-1, keepdims=True)
        zc_ = z - mu
        var = jnp.mean(zc_ * zc_, axis=-1, keepdims=True)
        yv = zc_ * lax.rsqrt(var + EPS) * lg + lb
        return yv * _sigmoid(yv)

    def f_lnsilu_pw2(i, z, lg, lb, w):
        s = ln_silu(z, lg, lb).astype(bf16)
        return s, jnp.dot(s, w, preferred_element_type=f32)

    sb, out1 = _rowcall(f_lnsilu_pw2, "l1_ln_silu_pw2", T_LAT, FUSED_TM, [_rin(zc)],
                        [wts["conf_ln_g"], wts["conf_ln_b"], wts["conf_w_pw2"]], [(D, bf16), (D, f32)])
    par1 = mlp_params([g1[1], wts["conf_b_pw2"], ng[1, 1][None], sc2[1], sh2[1], g2[1]])
    x3, h3, r1, mo1, x4, _ = _mlp_forward(x2, 0, out1, par1, wts["mlp_w_in"], wts["mlp_w_out"], 1, "l1_mlp")

    def loss_fn(x4_, fg, tgt):
        err = _rms(x4_, fg) - tgt
        per_row = jnp.mean(err * err, axis=-1, keepdims=True)
        return 0.5 * jnp.sum(per_row, axis=0, keepdims=True)

    def f_head(i, x4_, tgt, fg):
        loss, vjp = jax.vjp(lambda a, e: loss_fn(a, e, tgt), x4_, fg)
        dx, dfg = vjp(jnp.ones((1, 1), f32))
        return dx, jnp.broadcast_to(loss, (1, 128)), dfg

    dx4, loss_acc, dfinal_g = _rowcall(f_head, "head", T_LAT, TM, [_rin(x4), _rin(target)], [wts["final_g"]],
                                       [(D, f32)], [(1, 128), (1, D)])

    grads = {"final_g": dfinal_g}
    loss = reduce_loss(loss_acc[0, 0])

    dx3, dout1, dmo1, dhid1, acc1 = _mlp_backward(dx4, x3, r1, mo1, out1, par1, wts["mlp_w_in"], wts["mlp_w_out"], 1,
                                                  "l1_mlp_bwd", after=[loss.reshape(1, 1)])
    mlp_dw = _mlp_weight_grads(h3, dhid1, r1, dmo1, 1, None, "l1")
    dg1_1, db_pw2, dng11, dsc2_1, dsh2_1, dg2_1 = [acc1[k:k + 1] for k in range(6)]

    grads["conf_w_pw2"] = _mm(sb, dout1, "l1_pw2_dw", ta=True, out_dtype=bf16)
    grads["conf_b_pw2"] = db_pw2

    def f_pw2_lnsilu_bwd(i, z, dout, lg, lb, w):
        ds = lax.dot_general(dout, w, nt, preferred_element_type=f32)
        _, vjp = jax.vjp(ln_silu, z, lg, lb)
        return vjp(ds)

    dzc, dln_g, dln_b = _rowcall(f_pw2_lnsilu_bwd, "l1_pw2_ln_silu_bwd", T_LAT, FUSED_TM, [_rin(zc), _rin(dout1)],
                                 [wts["conf_ln_g"], wts["conf_ln_b"], wts["conf_w_pw2"]], [(D, f32)], [(1, D)] * 2)
    grads["conf_ln_g"], grads["conf_ln_b"] = dln_g, dln_b
    dzg, dconv_w, dconv_b = _dwconv_bwd([dzc], zg, 0, wts["conf_conv_w"], 31, 15, ((0, T_LAT),), 128,
                                        "l1_conv_bwd", f32)
    grads["conf_conv_w"], grads["conf_conv_b"] = dconv_w, dconv_b

    def f_glu_pw1_norm_bwd(i, p_, dz, x_, dxs, b1, g_, sc_, sh_, w):
        pf = p_.astype(f32)
        _, vjp = jax.vjp(glu, pf[:, :D], pf[:, D:], b1)
        da, db, db1 = vjp(dz)
        dp = jnp.concatenate([da, db], axis=1).astype(bf16)
        dh = lax.dot_general(dp, w, nt, preferred_element_type=f32)
        _, vjp = jax.vjp(_normmod, x_, g_, sc_, sh_)
        dx, dg, dsc, dsh = vjp(dh)
        return dp, dx + dxs, db1, dg, dsc, dsh

    dpw, dx2, db_pw1, dng10, dsc1_1, dsh1_1 = _rowcall(
        f_glu_pw1_norm_bwd, "l1_glu_pw1_normmod_bwd", T_LAT, FUSED_TM, [_rin(pw), _rin(dzg), _rin(x2), _rin(dx3)],
        [wts["conf_b_pw1"], ng[1, 0][None], sc1[1], sh1[1], wts["conf_w_pw1"]], [(2 * D, bf16), (D, f32)],
        [(1, 2 * D), (1, D), (1, D), (1, D)])
    grads["conf_b_pw1"] = db_pw1
    grads["conf_w_pw1"] = _mm(h2, dpw, "l1_pw1_dw", ta=True, out_dtype=bf16)
    sent = send_grads(["conf_w_pw2", "conf_w_pw1"], grads)

    dx1, dout0, dmo0, dhid0, acc0 = _mlp_backward(dx2, x1, r0, mo0, out0, par0, wts["mlp_w_in"], wts["mlp_w_out"], 0,
                                                  "l0_mlp_bwd", after=[sent])
    grads["mlp_w_in"], grads["mlp_w_out"] = _mlp_weight_grads(h1, dhid0, r0, dmo0, 0, mlp_dw, "l0")
    sent = send_grads(["mlp_w_in", "mlp_w_out"], grads)
    dg1_0, _, dng01, dsc2_0, dsh2_0, dg2_0 = [acc0[k:k + 1] for k in range(6)]

    grads["rec_w_out"] = _mm(zb, dout0, "l0_out_proj_dw", ta=True, out_dtype=bf16, after=[sent])
    sent = send_grads(["rec_w_out"], grads)

    def f_out_gate_bwd(i, gp, y0_, y1_, dout, w):
        lat = jnp.where(i < ctx_tiles, 0.0, 1.0)
        dz = lax.dot_general(dout, w, nt, preferred_element_type=f32)
        _, vjp = jax.vjp(lambda a, b: _gelu(a) * b, gp, y0_ + y1_)
        dgp, dy = vjp(dz)
        return dgp * lat, dy * lat

    dgp, dy = _rowcall(f_out_gate_bwd, "l0_out_proj_gate_bwd", T_ALL, FUSED_TM,
                       [_rin(gr, R, 0), _rin(y0), _rin(y1), _rin(dout0, None, 0, -ctx_tiles)], [wts["rec_w_out"]],
                       [(R, bf16), (R, f32)], after=[sent])
    (dh_f,) = _scan_call(a0, dy, SCAN_FWD_BWD, True, "l0_scan_fwd_bwd", True)
    (dh_r,) = _scan_call(a1, dy, SCAN_REV_BWD, False, "l0_scan_rev_bwd", True)

    dpre, du_f, *dpar_f = _gates_coeff_bwd(ub, u, dh_f, yp0, *gate_args, 0, None)
    dpre, du_r, *dpar_r = _gates_coeff_bwd(ub, u, dh_r, yp1, *gate_args, 1, dpre)
    grads["rec_b_a"], grads["rec_b_x"], grads["rec_lambda"] = [
        jnp.concatenate([f.reshape(-1), r_.reshape(-1)]).reshape(2, R) for f, r_ in zip(dpar_f, dpar_r)]
    grads["gates"] = _gates_dw(ub, dpre)
    sent = send_grads(["replicated"], grads)
    du_gates = _gates_dx(dpre, wts["gates"], after=[sent])
    drec, dconv4_w, dconv4_b = _dwconv_bwd([du_f, du_r, du_gates], gr, R // 256, wts["rec_conv_w"], 4, 1,
                                           CONV_SEGMENTS, 256, "l0_conv_bwd", bf16)
    grads["rec_conv_w"], grads["rec_conv_b"] = dconv4_w, dconv4_b
    dgr = jnp.concatenate([dgp, drec], axis=1)
    grads["rec_w_in"] = _mm(h0, dgr, "l0_in_proj_dw", ta=True, out_dtype=bf16)
    sent = send_grads(["rec_w_in"], grads)

    def f_pre0_bwd(i, x0, dgr_, dxs, g, scp_, shp_, w):
        lat = jnp.where(i < ctx_tiles, 0.0, 1.0)
        dh = lax.dot_general(dgr_, w, nt, preferred_element_type=f32)
        _, vjp = jax.vjp(lambda a, b, c, e: _normmod(a, b, blend(i, c), blend(i, e)), x0, g, scp_, shp_)
        dx, dg, dscp, dshp = vjp(dh)
        return dx + lat * dxs, dg, dscp, dshp

    dx0cat, dng00, dscp, dshp = _rowcall(
        f_pre0_bwd, "l0_in_proj_prenorm_bwd", T_ALL, FUSED_TM,
        [_rin(x0cat), _rin(dgr), _rin(dx1, None, 0, -ctx_tiles)], [ng[0, 0][None], scp, shp, wts["rec_w_in"]],
        [(D, f32)], [(1, D), (2, D), (2, D)], after=[sent])

    grads["norm_g"] = jnp.stack([jnp.concatenate([dng00, dng01], 0), jnp.concatenate([dng10, dng11], 0)])
    dmods = jnp.stack([
        jnp.concatenate([dshp[1:2], dscp[1:2], dg1_0, dsh2_0, dsc2_0, dg2_0], axis=0),
        jnp.concatenate([dsh1_1, dsc1_1, dg1_1, dsh2_1, dsc2_1, dg2_1], axis=0)])
    dcmod = jnp.concatenate([dshp[0:1], dscp[0:1]], axis=0)
    return loss, dx0cat[T_CTX:], dmods, dcmod, grads


def _unshard_cols(g):
    g = jnp.moveaxis(g, 0, -2)
    return g.reshape(g.shape[:-2] + (g.shape[-2] * g.shape[-1],))


def _shard_cols(w):
    w = w.reshape(w.shape[:-1] + (N_DEV, w.shape[-1] // N_DEV))
    return jnp.moveaxis(w, -2, 0)


def _shard_rows(w):
    return w.reshape((N_DEV, w.shape[0] // N_DEV) + w.shape[1:])


SMALL_PACK_ROWS = 64


def kernel(x, c, ctx, c_ctx, w_ada, b_ada, norm_g, rec_w_in, rec_conv_w, rec_conv_b, rec_lambda, rec_w_a, rec_b_a, rec_w_x, rec_b_x, rec_w_out, conf_w_pw1, conf_b_pw1, conf_conv_w, conf_conv_b, conf_ln_g, conf_ln_b, conf_w_pw2, conf_b_pw2, mlp_w_in, mlp_w_out, final_g, loss_target, m_c_ctx, m_w_ada, m_b_ada, m_norm_g, m_rec_w_in, m_rec_conv_w, m_rec_conv_b, m_rec_lambda, m_rec_w_a, m_rec_b_a, m_rec_w_x, m_rec_b_x, m_rec_w_out, m_conf_w_pw1, m_conf_b_pw1, m_conf_conv_w, m_conf_conv_b, m_conf_ln_g, m_conf_ln_b, m_conf_w_pw2, m_conf_b_pw2, m_mlp_w_in, m_mlp_w_out, m_final_g, v_c_ctx, v_w_ada, v_b_ada, v_norm_g, v_rec_w_in, v_rec_conv_w, v_rec_conv_b, v_rec_lambda, v_rec_w_a, v_rec_b_a, v_rec_w_x, v_rec_b_x, v_rec_w_out, v_conf_w_pw1, v_conf_b_pw1, v_conf_conv_w, v_conf_conv_b, v_conf_ln_g, v_conf_ln_b, v_conf_w_pw2, v_conf_b_pw2, v_mlp_w_in, v_mlp_w_out, v_final_g):
    me = 4 * lax.axis_index("x") + 2 * lax.axis_index("y") + lax.axis_index("c")
    weights = dict(c_ctx=c_ctx, w_ada=w_ada, b_ada=b_ada, norm_g=norm_g, rec_w_in=rec_w_in, rec_conv_w=rec_conv_w,
                   rec_conv_b=rec_conv_b, rec_lambda=rec_lambda, rec_w_a=rec_w_a, rec_b_a=rec_b_a, rec_w_x=rec_w_x,
                   rec_b_x=rec_b_x, rec_w_out=rec_w_out, conf_w_pw1=conf_w_pw1, conf_b_pw1=conf_b_pw1,
                   conf_conv_w=conf_conv_w, conf_conv_b=conf_conv_b, conf_ln_g=conf_ln_g, conf_ln_b=conf_ln_b,
                   conf_w_pw2=conf_w_pw2, conf_b_pw2=conf_b_pw2, mlp_w_in=mlp_w_in, mlp_w_out=mlp_w_out, final_g=final_g)
    m_in = dict(c_ctx=m_c_ctx, w_ada=m_w_ada, b_ada=m_b_ada, norm_g=m_norm_g, rec_w_in=m_rec_w_in, rec_conv_w=m_rec_conv_w,
                rec_conv_b=m_rec_conv_b, rec_lambda=m_rec_lambda, rec_w_a=m_rec_w_a, rec_b_a=m_rec_b_a, rec_w_x=m_rec_w_x,
                rec_b_x=m_rec_b_x, rec_w_out=m_rec_w_out, conf_w_pw1=m_conf_w_pw1, conf_b_pw1=m_conf_b_pw1,
                conf_conv_w=m_conf_conv_w, conf_conv_b=m_conf_conv_b, conf_ln_g=m_conf_ln_g, conf_ln_b=m_conf_ln_b,
                conf_w_pw2=m_conf_w_pw2, conf_b_pw2=m_conf_b_pw2, mlp_w_in=m_mlp_w_in, mlp_w_out=m_mlp_w_out,
                final_g=m_final_g)
    v_in = dict(c_ctx=v_c_ctx, w_ada=v_w_ada, b_ada=v_b_ada, norm_g=v_norm_g, rec_w_in=v_rec_w_in, rec_conv_w=v_rec_conv_w,
                rec_conv_b=v_rec_conv_b, rec_lambda=v_rec_lambda, rec_w_a=v_rec_w_a, rec_b_a=v_rec_b_a, rec_w_x=v_rec_w_x,
                rec_b_x=v_rec_b_x, rec_w_out=v_rec_w_out, conf_w_pw1=v_conf_w_pw1, conf_b_pw1=v_conf_b_pw1,
                conf_conv_w=v_conf_conv_w, conf_conv_b=v_conf_conv_b, conf_ln_g=v_conf_ln_g, conf_ln_b=v_conf_ln_b,
                conf_w_pw2=v_conf_w_pw2, conf_b_pw2=v_conf_b_pw2, mlp_w_in=v_mlp_w_in, mlp_w_out=v_mlp_w_out,
                final_g=v_final_g)
    names = list(weights)

    small_items = [c, norm_g, rec_conv_w, rec_lambda, conf_b_pw1, conf_conv_w, conf_conv_b, conf_ln_g, conf_ln_b,
                   conf_b_pw2]
    flat = jnp.concatenate([a.reshape(-1) for a in small_items])
    flat = jnp.pad(flat, (0, SMALL_PACK_ROWS * 128 - flat.shape[0])).reshape(SMALL_PACK_ROWS, 128)
    (small_all,) = _all_gather([flat], "gather_small")

    small_all = small_all.reshape(N_DEV, -1)
    off = 0
    small = []
    for a in small_items:
        small.append(small_all[:, off:off + a.size].reshape((N_DEV,) + a.shape))
        off += a.size
    c_all, ng_all, rcw_all, lam_all, bpw1_all, ccw_all, ccb_all, lng_all, lnb_all, bpw2_all = small
    wts = {
        "norm_g": _unshard_cols(ng_all),
        "rec_conv_w": _unshard_cols(rcw_all)[0],
        "rec_lambda": _unshard_cols(lam_all)[0],
        "conf_b_pw1": _unshard_cols(bpw1_all),
        "conf_conv_w": _unshard_cols(ccw_all)[0],
        "conf_conv_b": _unshard_cols(ccb_all),
        "conf_ln_g": _unshard_cols(lng_all),
        "conf_ln_b": _unshard_cols(lnb_all),
        "conf_b_pw2": _unshard_cols(bpw2_all),
        "rec_conv_b": rec_conv_b,
        "rec_b_a": rec_b_a[0].reshape(2, R),
        "rec_b_x": rec_b_x[0].reshape(2, R),
        "final_g": final_g[None],
        "gates": _gate_matrix(rec_w_a[0], rec_w_x[0]),
    }

    c16 = jnp.concatenate([c_all[:, 0], jnp.broadcast_to(c_ctx[None], (8, D))], axis=0)
    b_loc = lax.dynamic_slice_in_dim(b_ada, me * ADA_SHARD, ADA_SHARD, axis=1)[:, None]
    (mods_all,) = _all_gather([_ada_forward(c16, w_ada, b_loc)], "gather_mods")
    mods_all = _unshard_cols(mods_all)
    mods = lax.dynamic_index_in_dim(mods_all, me, axis=1, keepdims=False).reshape(2, N_MOD, D)
    cmod = mods_all[0, 8, :2 * D].reshape(2, D)

    as_shard = lambda a: a.astype(bf16).reshape(-1, a.shape[-1])
    early = _all_gather_2level([as_shard(rec_w_in[0])], "gather_weights_early")
    wts["rec_w_in"] = _unshard_cols(early[0])
    late_items = {"mlp": [rec_w_out[0], mlp_w_in, mlp_w_out], "conf": [conf_w_pw1[0], conf_w_pw2[0]]}
    late_handles, order = {}, [early[0], mods]
    for group in ("mlp", "conf"):
        shards = [as_shard(a) for a in late_items[group]]
        lands = [_own_block_filled(s, me) for s in shards]
        if group == "mlp":
            late_handles[group], token = _chip_gather_start(shards, lands, "gather_mlp_start", after=order)
        else:
            late_handles[group], token = _exchange_start(shards, lands, "gather_conf_start", False, after=order)
        order = [token]

    def late_weights(group, after):
        if group == "mlp":
            forwarded = _chip_gather_forward(late_handles[group], after, "gather_mlp_forward")
            got = _chip_gather_wait(forwarded, after, "gather_mlp_wait")
        else:
            got = _exchange_wait(late_handles[group], after, "gather_conf_wait", False)
        got = [g.reshape((N_DEV,) + a.shape) for g, a in zip(got, late_items[group])]
        if group == "mlp":
            return {"rec_w_out": got[0].reshape(R, D), "mlp_w_in": got[1], "mlp_w_out": got[2]}
        return {"conf_w_pw1": _unshard_cols(got[0]), "conf_w_pw2": got[1].reshape(D, D)}

    to_blocks = {"rec_w_in": _shard_cols, "conf_w_pw1": _shard_cols, "rec_w_out": _shard_rows, "conf_w_pw2": _shard_rows,
                 "mlp_w_in": lambda g: g, "mlp_w_out": lambda g: g}
    grad_handles = []

    repl_names = ["rec_w_a", "rec_w_x", "rec_b_a", "rec_b_x", "final_g"]

    def send_replicated(grads):
        dwg = grads["gates"]
        repl = {"rec_w_a": jnp.stack([_gate_blocks(dwg, 0), _gate_blocks(dwg, 2)]),
                "rec_w_x": jnp.stack([_gate_blocks(dwg, 1), _gate_blocks(dwg, 3)]),
                "rec_b_a": grads["rec_b_a"], "rec_b_x": grads["rec_b_x"], "final_g": grads["final_g"]}
        flat = jnp.concatenate([repl[n].reshape(-1) for n in repl_names])
        rows = -(-flat.shape[0] // (16 * D)) * 16
        flat = jnp.pad(flat, (0, rows * D - flat.shape[0])).reshape(rows, D).astype(bf16)
        handle, sent = _exchange_start([flat], [_own_block_filled(flat, me)], "gather_replicated_start", False)
        grad_handles.append((["replicated"], handle))
        return sent

    def send_grads(group, grads):
        if group == ["replicated"]:
            return send_replicated(grads)
        blocks = [to_blocks[n](grads[n]) for n in group]
        blocks = [g.reshape(N_DEV, -1, g.shape[-1]) for g in blocks]
        lands = [_own_block_filled(lax.dynamic_index_in_dim(g, me, 0, keepdims=False), me) for g in blocks]
        handle, sent = _exchange_start(blocks, lands, "scatter_start_" + group[0], True)
        grad_handles.append((group, handle))
        return sent

    loss, grad_x, dmods, dcmod, grads = _local_step(
        x[0], ctx[0], loss_target[0], mods, cmod, wts, late_weights, send_grads,
        lambda partial: lax.psum(partial, ("x", "y", "c")), start_after=order)

    def as2d(shape):
        rows = 1
        for s in shape[:-1]:
            rows *= s
        return (rows, shape[-1])

    def whole(arr, shape):
        arr = arr.reshape((-1,) + as2d(shape))
        return (arr, arr.shape[0])

    shard_shapes = {n: weights[n].shape for n in names}
    g_out, d_out, m_out, v_out = {}, {}, {}, {}

    def adamw(n, pieces, after):
        shape = shard_shapes[n]
        r2, c2 = as2d(shape)
        g, dl, nm, nv = _adamw(pieces, weights[n].reshape(r2, c2), m_in[n].reshape(r2, c2), v_in[n].reshape(r2, c2),
                               "adamw_" + n, after=after)
        g_out[n], d_out[n], m_out[n], v_out[n] = (t.reshape(shape) for t in (g, dl, nm, nv))
        return g

    small_sharded = ["norm_g", "rec_conv_w", "rec_lambda", "conf_b_pw1", "conf_conv_w", "conf_conv_b", "conf_ln_g",
                     "conf_ln_b", "conf_b_pw2"]
    pack = jnp.concatenate([_shard_cols(grads[n]).reshape(N_DEV, -1) for n in small_sharded], axis=1)
    pack = jnp.pad(pack, ((0, 0), (0, SMALL_PACK_ROWS * 128 - pack.shape[1]))).reshape(N_DEV, SMALL_PACK_ROWS, 128)
    small_handle, token = _exchange_start(
        [pack], [_own_block_filled(lax.dynamic_index_in_dim(pack, me, 0, keepdims=False), me)], "scatter_small_start",
        True, after=[grad_x])
    dm_flat = jnp.concatenate([dmods.reshape(-1), dcmod.reshape(-1), grads["rec_conv_b"].reshape(-1)])
    dm_len = dm_flat.shape[0]
    dm_flat = jnp.pad(dm_flat, (0, 128 * 128 - dm_len)).reshape(128, 128)
    dm_handle, token = _exchange_start([dm_flat], [_own_block_filled(dm_flat, me)], "gather_dmods_start", False,
                                       after=[token])

    done = token
    for group, handle in grad_handles:
        if group == ["replicated"]:
            repl_all = _exchange_wait(handle, done, "gather_replicated_wait", False)[0].reshape(N_DEV, -1)
            off = 0
            for n in repl_names:
                size = weights[n].size
                done = adamw(n, [whole(repl_all[:, off:off + size], shard_shapes[n])], [done])
                off += size
            continue
        for n, got in zip(group, _exchange_wait(handle, done, "scatter_wait_" + group[0], True)):
            done = adamw(n, [(got, N_DEV)], [done])

    dm_all = _exchange_wait(dm_handle, done, "gather_dmods_wait", False)[0].reshape(N_DEV, -1)
    dmods_all = dm_all[:, :2 * N_MOD * D].reshape(N_DEV, 2, N_MOD * D)
    dcmod_all = jnp.pad(dm_all[:, 2 * N_MOD * D:2 * N_MOD * D + 2 * D], ((0, 0), (0, (N_MOD - 2) * D)))
    g16_full = jnp.stack([jnp.concatenate([dmods_all[:, 0], dcmod_all], axis=0),
                          jnp.concatenate([dmods_all[:, 1], jnp.zeros_like(dcmod_all)], axis=0)])
    g16 = lax.dynamic_slice_in_dim(g16_full, me * ADA_SHARD, ADA_SHARD, axis=2)
    dw_ada, ds_part = _ada_backward(c16, g16, w_ada)
    ds_handle, token = _exchange_start([ds_part[0]], [_own_block_filled(ds_part[0], me)], "gather_dsilu_start", False)
    done = adamw("w_ada", [whole(dw_ada, shard_shapes["w_ada"])], [token])
    done = adamw("rec_conv_b", [whole(dm_all[:, dm_len - R:dm_len], shard_shapes["rec_conv_b"])], [done])
    db_terms = jnp.concatenate([dmods_all, jnp.stack([dcmod_all, jnp.zeros_like(dcmod_all)], axis=1)], axis=0)
    done = adamw("b_ada", [whole(db_terms, shard_shapes["b_ada"])], [done])
    pack_recv = _exchange_wait(small_handle, done, "scatter_small_wait", True)[0].reshape(N_DEV, -1)
    off = 0
    for n in small_sharded:
        size = weights[n].size
        done = adamw(n, [whole(pack_recv[:, off:off + size], shard_shapes[n])], [done])
        off += size
    ds_all = _exchange_wait(ds_handle, done, "gather_dsilu_wait", False)[0]
    adamw("c_ctx", [whole(ds_all[:, 0], shard_shapes["c_ctx"])], [])

    return (loss, grad_x[None], *[g_out[n] for n in names], *[d_out[n] for n in names],
            *[m_out[n] for n in names], *[v_out[n] for n in names])
```

```python
import functools

import jax
import jax.numpy as jnp
from jax import lax
from jax.experimental import pallas as pl
from jax.experimental.pallas import tpu as pltpu

f32 = jnp.float32
bf16 = jnp.bfloat16

N_DEV = 8
D = 1024
T_LAT = 2048
T_CTX = 256
T_ALL = T_CTX + T_LAT
R = 1280
N_BLK = 16
BLK = R // N_BLK
F = 4096
GRID_W = 64
RG_C = 8.0
EPS = 1e-6
POS_BASE = 10000.0
N_MOD = 6
ADA_SHARD = N_MOD * D // N_DEV

ADAM_LR = 0.001
ADAM_B1 = 0.9
ADAM_B2 = 0.999
ADAM_EPS = 1e-08
ADAM_WD = 0.01
ADAM_STEP = 10

VMEM_LIMIT_V7X = 56 * 1024 * 1024
HALO = 16
MESH = pl.DeviceIdType.MESH


def _cparams(*sem):
    return pltpu.CompilerParams(dimension_semantics=sem, vmem_limit_bytes=VMEM_LIMIT_V7X)


def _pick(n, cands):
    for c in cands:
        if n % c == 0:
            return c
    raise ValueError(f"no block size for {n}")


def _position():
    x, y, c = lax.axis_index("x"), lax.axis_index("y"), lax.axis_index("c")
    return x, y, c, 4 * x + 2 * y + c


def _peer(x, y, c, k):
    px = (1 - x) if (k >> 2) & 1 else x
    py = (1 - y) if (k >> 1) & 1 else y
    pc = (1 - c) if k & 1 else c
    return (px, py, pc), 4 * px + 2 * py + pc


def _exchange(arrs, name, scatter):
    n = len(arrs)

    def body(*refs):
        ins, outs = refs[:n], refs[n:2 * n]
        send_sems, recv_sems, local_sems = refs[2 * n:]
        x, y, c, me = _position()
        local = []
        for a in range(n):
            src = ins[a].at[me] if scatter else ins[a]
            cp = pltpu.make_async_copy(src, outs[a].at[me], local_sems.at[a])
            cp.start()
            local.append(cp)
        sends, recvs = [], []
        for a in range(n):
            for k in range(1, N_DEV):
                peer, peer_lin = _peer(x, y, c, k)
                src = ins[a].at[peer_lin] if scatter else ins[a]
                cp = pltpu.make_async_remote_copy(
                    src_ref=src, dst_ref=outs[a].at[me], send_sem=send_sems.at[a, k - 1],
                    recv_sem=recv_sems.at[a, k - 1], device_id=peer, device_id_type=MESH)
                cp.start()
                sends.append(cp)
                recvs.append(pltpu.make_async_remote_copy(
                    src_ref=src, dst_ref=outs[a].at[peer_lin], send_sem=send_sems.at[a, k - 1],
                    recv_sem=recv_sems.at[a, k - 1], device_id=peer, device_id_type=MESH))
        for cp in recvs:
            cp.wait_recv()
        for cp in sends:
            cp.wait_send()
        for cp in local:
            cp.wait()

    if scatter:
        out_shape = [jax.ShapeDtypeStruct(a.shape, a.dtype) for a in arrs]
    else:
        out_shape = [jax.ShapeDtypeStruct((N_DEV,) + a.shape, a.dtype) for a in arrs]
    any_spec = pl.BlockSpec(memory_space=pl.ANY)
    return pl.pallas_call(
        body, name=name, out_shape=out_shape,
        in_specs=[any_spec] * n, out_specs=[any_spec] * n,
        scratch_shapes=[pltpu.SemaphoreType.DMA((n, N_DEV - 1)), pltpu.SemaphoreType.DMA((n, N_DEV - 1)),
                        pltpu.SemaphoreType.DMA((n,))],
    )(*arrs)


def _all_gather(arrs, name):
    return _exchange(arrs, name, scatter=False)


def _lin(p):
    return 4 * p[0] + 2 * p[1] + p[2]


HBM_SPEC = pl.BlockSpec(memory_space=pltpu.HBM)
SEM_SPEC = pl.BlockSpec(memory_space=pltpu.SEMAPHORE)
DATAFLOW_EFFECT = pltpu.SideEffectType.DATAFLOW_SIDE_EFFECTING


def _split_copies(srcs, lands, send_sems, recv_sems, scatter):
    x, y, c, me = _position()
    out = []
    for a in range(len(srcs)):
        for k in range(1, N_DEV):
            peer, peer_lin = _peer(x, y, c, k)
            src = srcs[a].at[peer_lin] if scatter else srcs[a]
            mk = lambda slot: pltpu.make_async_remote_copy(
                src_ref=src, dst_ref=lands[a].at[slot], send_sem=send_sems.at[a * (N_DEV - 1) + k - 1],
                recv_sem=recv_sems.at[a * (N_DEV - 1) + k - 1], device_id=peer, device_id_type=MESH)
            out.append((mk(me), mk(peer_lin)))
    return out


def _exchange_start(srcs, lands, name, scatter, after=()):
    n = len(srcs)
    n_after = len(after)

    def body(*refs):
        srcs_r, lands_r = refs[:n], refs[n:2 * n]
        send_sems, recv_sems = refs[2 * n + n_after], refs[2 * n + n_after + 1]
        token = refs[-1]
        for outgoing, _ in _split_copies(srcs_r, lands_r, send_sems, recv_sems, scatter):
            outgoing.start()
        token[...] = jnp.zeros_like(token)

    hbm = lambda a: pltpu.HBM(a.shape, a.dtype)
    res = pl.pallas_call(
        body, name=name,
        out_shape=(pltpu.SemaphoreType.DMA((n * (N_DEV - 1),)), pltpu.SemaphoreType.DMA((n * (N_DEV - 1),)),
                   *[hbm(a) for a in srcs], *[hbm(a) for a in lands], jax.ShapeDtypeStruct((8, 128), f32)),
        in_specs=[HBM_SPEC] * (2 * n) + [pl.BlockSpec(memory_space=pl.ANY)] * n_after,
        out_specs=(SEM_SPEC, SEM_SPEC, *[HBM_SPEC] * (2 * n), pl.BlockSpec(memory_space=pltpu.VMEM)),
        input_output_aliases={i: 2 + i for i in range(2 * n)},
        compiler_params=pltpu.CompilerParams(has_side_effects=DATAFLOW_EFFECT),
    )(*[pltpu.with_memory_space_constraint(a, pltpu.HBM) for a in list(srcs) + list(lands)], *after)
    return (res[0], res[1], list(res[2:2 + n]), list(res[2 + n:2 + 2 * n])), res[-1]


def _exchange_wait(handle, after, name, scatter):
    send_sems, recv_sems, srcs, lands = handle
    n = len(srcs)

    def body(*refs):
        srcs_r, lands_r = refs[:n], refs[n:2 * n]
        send_s, recv_s = refs[2 * n], refs[2 * n + 1]
        for outgoing, incoming in _split_copies(srcs_r, lands_r, send_s, recv_s, scatter):
            outgoing.wait_send()
            incoming.wait_recv()

    hbm = lambda a: pltpu.HBM(a.shape, a.dtype)
    res = pl.pallas_call(
        body, name=name, out_shape=tuple(hbm(a) for a in list(srcs) + list(lands)),
        in_specs=[HBM_SPEC] * (2 * n) + [SEM_SPEC, SEM_SPEC, pl.BlockSpec(memory_space=pl.ANY)],
        out_specs=tuple([HBM_SPEC] * (2 * n)),
        input_output_aliases={i: i for i in range(2 * n)},
        compiler_params=pltpu.CompilerParams(has_side_effects=DATAFLOW_EFFECT),
    )(*srcs, *lands, send_sems, recv_sems, after)
    return list(res[n:])


def _chip_peers(x, y, c):
    return [(x, y, 1 - c)] + [_plane_pos(x, y, q) + (c,) for q in (2, 1, 3)]


def _chip_gather_start(shards, lands, name, after=()):
    n, n_after = len(shards), len(after)

    def body(*refs):
        srcs_r, lands_r = refs[:n], refs[n:2 * n]
        send_sems, recv_sems = refs[2 * n + n_after], refs[2 * n + n_after + 1]
        x, y, c, me = _position()
        for a in range(n):
            for k, peer in enumerate(_chip_peers(x, y, c)):
                pltpu.make_async_remote_copy(
                    src_ref=srcs_r[a], dst_ref=lands_r[a].at[me], send_sem=send_sems.at[4 * a + k],
                    recv_sem=recv_sems.at[4 * a + k], device_id=peer, device_id_type=MESH).start()
        refs[-1][...] = jnp.zeros_like(refs[-1])

    hbm = lambda a: pltpu.HBM(a.shape, a.dtype)
    res = pl.pallas_call(
        body, name=name,
        out_shape=(pltpu.SemaphoreType.DMA((4 * n,)), pltpu.SemaphoreType.DMA((4 * n,)),
                   *[hbm(a) for a in shards], *[hbm(a) for a in lands], jax.ShapeDtypeStruct((8, 128), f32)),
        in_specs=[HBM_SPEC] * (2 * n) + [ANY_SPEC] * n_after,
        out_specs=(SEM_SPEC, SEM_SPEC, *[HBM_SPEC] * (2 * n), pl.BlockSpec(memory_space=pltpu.VMEM)),
        input_output_aliases={i: 2 + i for i in range(2 * n)},
        compiler_params=pltpu.CompilerParams(has_side_effects=DATAFLOW_EFFECT),
    )(*[pltpu.with_memory_space_constraint(a, pltpu.HBM) for a in list(shards) + list(lands)], *after)
    return (res[0], res[1], list(res[2:2 + n]), list(res[2 + n:2 + 2 * n])), res[-1]


def _chip_gather_forward(handle, after, name):
    send_sems, recv_sems, srcs, lands = handle
    n = len(srcs)

    def body(*refs):
        srcs_r, lands_r = refs[:n], refs[n:2 * n]
        send1, recv1 = refs[2 * n], refs[2 * n + 1]
        send2, recv2 = refs[2 * n + 3], refs[2 * n + 4]
        x, y, c, me = _position()
        peers = _chip_peers(x, y, c)
        for a in range(n):
            for k, peer in enumerate(peers):
                mk = lambda slot: pltpu.make_async_remote_copy(
                    src_ref=srcs_r[a], dst_ref=lands_r[a].at[slot], send_sem=send1.at[4 * a + k],
                    recv_sem=recv1.at[4 * a + k], device_id=peer, device_id_type=MESH)
                mk(me).wait_send()
                mk(_lin(peer)).wait_recv()
        for a in range(n):
            for k, peer in enumerate(peers[1:]):
                slot = _lin(peer)
                pltpu.make_async_remote_copy(
                    src_ref=lands_r[a].at[slot], dst_ref=lands_r[a].at[slot], send_sem=send2.at[3 * a + k],
                    recv_sem=recv2.at[3 * a + k], device_id=peers[0], device_id_type=MESH).start()

    hbm = lambda a: pltpu.HBM(a.shape, a.dtype)
    res = pl.pallas_call(
        body, name=name,
        out_shape=(pltpu.SemaphoreType.DMA((3 * n,)), pltpu.SemaphoreType.DMA((3 * n,)), *[hbm(a) for a in lands]),
        in_specs=[HBM_SPEC] * (2 * n) + [SEM_SPEC, SEM_SPEC, ANY_SPEC],
        out_specs=(SEM_SPEC, SEM_SPEC, *[HBM_SPEC] * n),
        input_output_aliases={n + i: 2 + i for i in range(n)},
        compiler_params=pltpu.CompilerParams(has_side_effects=DATAFLOW_EFFECT),
    )(*srcs, *lands, send_sems, recv_sems, after)
    return (res[0], res[1], list(res[2:]))


def _chip_gather_wait(handle, after, name):
    send_sems, recv_sems, lands = handle
    n = len(lands)

    def body(*refs):
        lands_r, send2, recv2 = refs[:n], refs[n], refs[n + 1]
        x, y, c, me = _position()
        peers = _chip_peers(x, y, c)
        for a in range(n):
            for k, (px, py, pc) in enumerate(peers[1:]):
                mk = lambda slot: pltpu.make_async_remote_copy(
                    src_ref=lands_r[a].at[slot], dst_ref=lands_r[a].at[slot], send_sem=send2.at[3 * a + k],
                    recv_sem=recv2.at[3 * a + k], device_id=peers[0], device_id_type=MESH)
                mk(_lin((px, py, pc))).wait_send()
                mk(_lin((px, py, 1 - pc))).wait_recv()

    hbm = lambda a: pltpu.HBM(a.shape, a.dtype)
    res = pl.pallas_call(
        body, name=name, out_shape=tuple(hbm(a) for a in lands),
        in_specs=[HBM_SPEC] * n + [SEM_SPEC, SEM_SPEC, ANY_SPEC], out_specs=tuple([HBM_SPEC] * n),
        input_output_aliases={i: i for i in range(n)},
        compiler_params=pltpu.CompilerParams(has_side_effects=DATAFLOW_EFFECT),
    )(*lands, send_sems, recv_sems, after)
    return list(res)


def _own_block_filled(block, me):
    land = lax.empty((N_DEV,) + block.shape, block.dtype)
    return lax.dynamic_update_index_in_dim(land, block, me, 0)


def _staged_copy(src, dst, buf, in_sems, out_sems, rows, chunk):
    n = rows // chunk

    def rd(i):
        return pltpu.make_async_copy(src.at[pl.ds(i * chunk, chunk)], buf.at[i % 2], in_sems.at[i % 2])

    def wr(i):
        return pltpu.make_async_copy(buf.at[i % 2], dst.at[pl.ds(i * chunk, chunk)], out_sems.at[i % 2])

    rd(0).start()
    for i in range(n):
        if i + 1 < n:
            if i >= 1:
                wr(i - 1).wait()
            rd(i + 1).start()
        rd(i).wait()
        wr(i).start()
    for i in range(max(n - 2, 0), n):
        wr(i).wait()


def _all_gather_2level(shards, name):
    n = len(shards)
    chunks = [_pick(s.shape[0], (416, 512, 256, 160, 128, 64, 16)) for s in shards]

    def body(*refs):
        ins, outs = refs[:n], refs[n:2 * n]
        send_sems, recv_sems, in_sems, out_sems = refs[2 * n:2 * n + 4]
        bufs = refs[2 * n + 4:]
        x, y, c, me = _position()
        sib, xn, yn, dg = (x, y, 1 - c), (1 - x, y, c), (x, 1 - y, c), (1 - x, 1 - y, c)

        def cp(a, k, src, slot, to):
            return pltpu.make_async_remote_copy(src_ref=src, dst_ref=outs[a].at[slot], send_sem=send_sems.at[a, k],
                                                recv_sem=recv_sems.at[a, k], device_id=to, device_id_type=MESH)

        for a in range(n):
            for k, to in ((0, sib), (1, xn), (2, yn)):
                cp(a, k, ins[a], me, to).start()
        for a in range(n):
            cp(a, 1, ins[a], _lin(xn), xn).wait_recv()
            cp(a, 3, outs[a].at[_lin(xn)], _lin(xn), sib).start()

            @pl.when(c == 0)
            def _():
                cp(a, 5, outs[a].at[_lin(xn)], _lin(xn), yn).start()

            cp(a, 2, ins[a], _lin(yn), yn).wait_recv()
            cp(a, 4, outs[a].at[_lin(yn)], _lin(yn), sib).start()

            @pl.when(c == 1)
            def _():
                cp(a, 5, outs[a].at[_lin(yn)], _lin(yn), xn).start()

        for a in range(n):
            cp(a, 5, ins[a], _lin(dg), xn).wait_recv()
            cp(a, 6, outs[a].at[_lin(dg)], _lin(dg), sib).start()
        for a in range(n):
            _staged_copy(ins[a], outs[a].at[me], bufs[a], in_sems.at[a], out_sems.at[a], shards[a].shape[0], chunks[a])
        for a in range(n):
            for k, origin in ((0, sib), (3, (1 - x, y, 1 - c)), (4, (x, 1 - y, 1 - c)), (6, (1 - x, 1 - y, 1 - c))):
                cp(a, k, ins[a], _lin(origin), sib).wait_recv()
            for k in range(7):
                cp(a, k, ins[a], me, sib).wait_send()

    any_spec = pl.BlockSpec(memory_space=pl.ANY)
    return pl.pallas_call(
        body, name=name, out_shape=[jax.ShapeDtypeStruct((N_DEV,) + s.shape, s.dtype) for s in shards],
        in_specs=[any_spec] * n, out_specs=[any_spec] * n,
        scratch_shapes=[pltpu.SemaphoreType.DMA((n, 7)), pltpu.SemaphoreType.DMA((n, 7)),
                        pltpu.SemaphoreType.DMA((n, 2)), pltpu.SemaphoreType.DMA((n, 2))]
        + [pltpu.VMEM((2, ch, s.shape[1]), s.dtype) for ch, s in zip(chunks, shards)],
    )(*shards)


def _plane_pos(x, y, q):
    return ((1 - x) if q & 2 else x, (1 - y) if q & 1 else y)


ANY_SPEC = pl.BlockSpec(memory_space=pl.ANY)


def _mm(a, b, name, ta=False, tb=False, out_dtype=f32, after=()):
    if ta:
        k_dim, m_dim = a.shape
    else:
        m_dim, k_dim = a.shape
    if tb:
        n_dim, k2 = b.shape
    else:
        k2, n_dim = b.shape
    assert k_dim == k2, (a.shape, b.shape)
    assert a.dtype == bf16 and b.dtype == bf16
    bm = _pick(m_dim, (512, 768, 640, 256, 128))
    bn = _pick(n_dim, (512, 640, 256, 128))
    bk = k_dim if k_dim <= 2560 else _pick(k_dim, (1024, 1280, 768, 512))
    nk = k_dim // bk
    a_spec = (pl.BlockSpec((bk, bm), lambda i, j, k: (k, i)) if ta
              else pl.BlockSpec((bm, bk), lambda i, j, k: (i, k)))
    b_spec = (pl.BlockSpec((bn, bk), lambda i, j, k: (j, k)) if tb
              else pl.BlockSpec((bk, bn), lambda i, j, k: (k, j)))
    dims = (((0 if ta else 1,), (1 if tb else 0,)), ((), ()))

    n_after = len(after)

    def body_single(a_ref, b_ref, *rest):
        o_ref = rest[n_after]
        o_ref[...] = lax.dot_general(a_ref[...], b_ref[...], dims, preferred_element_type=f32).astype(o_ref.dtype)

    def body(a_ref, b_ref, *rest):
        o_ref, acc_ref = rest[n_after:]
        k = pl.program_id(2)

        @pl.when(k == 0)
        def _():
            acc_ref[...] = jnp.zeros_like(acc_ref)

        acc_ref[...] += lax.dot_general(a_ref[...], b_ref[...], dims, preferred_element_type=f32)

        @pl.when(k == nk - 1)
        def _():
            o_ref[...] = acc_ref[...].astype(o_ref.dtype)

    return pl.pallas_call(
        body_single if nk == 1 else body, name=name, out_shape=jax.ShapeDtypeStruct((m_dim, n_dim), out_dtype),
        grid=(m_dim // bm, n_dim // bn, nk), in_specs=[a_spec, b_spec] + [ANY_SPEC] * n_after,
        out_specs=pl.BlockSpec((bm, bn), lambda i, j, k: (i, j)),
        scratch_shapes=[] if nk == 1 else [pltpu.VMEM((bm, bn), f32)],
        compiler_params=_cparams("parallel", "parallel", "arbitrary"),
    )(a, b, *after)


def _rin(arr, width=None, cb=0, roff=0):
    return (arr, arr.shape[1] if width is None else width, cb, roff)


def _rowcall(fn, name, rows, tm, row_ins, par_ins, row_outs, acc_outs=(), after=()):
    nr, npar, nro, n_after = len(row_ins), len(par_ins), len(row_outs), len(after)
    in_specs, args = [], []
    for arr, width, cb, roff in row_ins:
        if roff >= 0:
            imap = lambda i, cb=cb, roff=roff: (i + roff, cb)
        else:
            imap = lambda i, cb=cb, roff=roff: (jnp.maximum(i + roff, 0), cb)
        in_specs.append(pl.BlockSpec((tm, width), imap))
        args.append(arr)
    for p in par_ins:
        in_specs.append(pl.BlockSpec(p.shape, lambda i: (0, 0)))
        args.append(p)
    out_shape, out_specs = [], []
    for width, dt in row_outs:
        out_shape.append(jax.ShapeDtypeStruct((rows, width), dt))
        out_specs.append(pl.BlockSpec((tm, width), lambda i: (i, 0)))
    for p, width in acc_outs:
        out_shape.append(jax.ShapeDtypeStruct((p, width), f32))
        out_specs.append(pl.BlockSpec((p, width), lambda i: (0, 0)))

    def body(*refs):
        i = pl.program_id(0)
        res = fn(i, *[r[...] for r in refs[:nr + npar]])
        outs = refs[nr + npar + n_after:]
        for o, v in zip(outs[:nro], res[:nro]):
            o[...] = v.astype(o.dtype)
        if acc_outs:
            @pl.when(i == 0)
            def _():
                for o in outs[nro:]:
                    o[...] = jnp.zeros_like(o)

            for o, v in zip(outs[nro:], res[nro:]):
                o[...] += v

    return pl.pallas_call(
        body, name=name, out_shape=out_shape, grid=(rows // tm,), in_specs=in_specs + [ANY_SPEC] * n_after,
        out_specs=out_specs, compiler_params=_cparams("arbitrary"),
    )(*args, *after)


def _rms(x, g):
    return x * lax.rsqrt(jnp.mean(x * x, axis=-1, keepdims=True) + EPS) * g


def _normmod(x, g, sc, sh):
    return _rms(x, g) * (1.0 + sc) + sh


def _gelu(x):
    return 0.5 * x * (1.0 + jnp.tanh(0.7978845608028654 * (x + 0.044715 * (x * x * x))))


def _sigmoid(x):
    return 0.5 * (jnp.tanh(0.5 * x) + 1.0)


def _coeff_parts(pre_a, pre_x, ba, bx, lam):
    r = _sigmoid(pre_a + ba)
    ig = _sigmoid(pre_x + bx)
    nl = -lam
    sp = jnp.maximum(nl, 0.0) + jnp.log(1.0 + jnp.exp(-jnp.abs(nl)))
    la = -RG_C * r * sp
    a = jnp.exp(la)
    one_minus_a2 = -jnp.tanh(la) * (a * a + 1.0)
    inv_m = lax.rsqrt(one_minus_a2)
    return r, ig, sp, a, one_minus_a2 * inv_m, inv_m


def _coeff(pre_a, pre_x, u, ba, bx, lam):
    _, ig, _, a, m, _ = _coeff_parts(pre_a, pre_x, ba, bx, lam)
    return a, m * (ig * u)


def _coeff_bwd(pre_a, pre_x, u, ba, bx, lam, da, db):
    r, ig, sp, a, m, inv_m = _coeff_parts(pre_a, pre_x, ba, bx, lam)
    dbu = db * u
    dig = dbu * m
    dm = dbu * ig
    dla = a * (da - dm * a * inv_m)
    dpa = dla * (-RG_C * sp) * (r * (1.0 - r))
    dpx = dig * (ig * (1.0 - ig))
    dsp = jnp.sum(dla * (-RG_C * r), axis=0, keepdims=True)
    dlam = -dsp * _sigmoid(-lam)
    return (dpa, dpx, db * m * ig, jnp.sum(dpa, axis=0, keepdims=True), jnp.sum(dpx, axis=0, keepdims=True), dlam)


SCAN_CHUNK = 256


def _scan_call(a, v, chunk_of, reverse, name, backward):
    rows, width = a.shape
    n_out = 1 if backward else 2
    nt = SCAN_CHUNK // 8

    def body(a_ref, v_ref, *rest):
        outs, state_ref = rest[:-1], rest[-1]

        @pl.when(pl.program_id(0) == 0)
        def _():
            state_ref[...] = jnp.zeros_like(state_ref)

        rid = lax.broadcasted_iota(jnp.int32, (8, width), 0)
        last_row = 0 if reverse else 7

        def shift(x, s, fill):
            rolled = pltpu.roll(x, (8 - s) if reverse else s, axis=0)
            return jnp.where((rid >= 8 - s) if reverse else (rid < s), fill, rolled)

        def tile(j, st):
            t0 = pl.multiple_of((nt - 1 - j if reverse else j) * 8, 8)
            at = a_ref[pl.ds(t0, 8), :]
            coef = shift(at, 1, 1.0) if backward else at
            acc = v_ref[pl.ds(t0, 8), :]
            for s in (1, 2, 4):
                acc = coef * shift(acc, s, 0.0) + acc
                coef = coef * shift(coef, s, 1.0)
            out = coef * st + acc
            outs[0][pl.ds(t0, 8), :] = out
            last = out[last_row:last_row + 1]
            if backward:
                return at[last_row:last_row + 1] * last
            outs[1][pl.ds(t0, 8), :] = shift(out, 1, st)
            return last

        state_ref[0:1, :] = lax.fori_loop(0, nt, tile, state_ref[0:1, :])

    spec = pl.BlockSpec((SCAN_CHUNK, width), lambda t: (chunk_of(t), 0))
    return pl.pallas_call(
        body, name=name, out_shape=[jax.ShapeDtypeStruct((rows, width), f32)] * n_out,
        grid=(rows // SCAN_CHUNK,), in_specs=[spec, spec], out_specs=[spec] * n_out,
        scratch_shapes=[pltpu.VMEM((8, width), f32)],
        compiler_params=_cparams("arbitrary"),
    )(a, v)


CONV_CHUNK = 256


def _fill_padded(pad_ref, src_ref, start, n):
    cb = pad_ref.shape[1]
    pad_ref[pl.ds(0, HALO), :] = jnp.zeros((HALO, cb), f32)
    pad_ref[pl.ds(HALO, n), :] = src_ref[pl.ds(start, n), :].astype(f32)
    pad_ref[pl.ds(HALO + n, HALO), :] = jnp.zeros((HALO, cb), f32)


def _dwconv_fwd(x, x_cb0, w, b, taps, pad_left, segments, cb, name, emit_bf16):
    rows = x.shape[0]
    width = w.shape[1]

    def body(x_ref, w_ref, b_ref, *rest):
        outs, xp = rest[:-1], rest[-1]
        for start, n in segments:
            _fill_padded(xp, x_ref, start, n)
            for c0 in range(0, n, CONV_CHUNK):
                acc = jnp.zeros((CONV_CHUNK, cb), f32) + b_ref[...]
                for k in range(taps):
                    acc = acc + w_ref[k:k + 1, :] * xp[pl.ds(HALO + c0 + k - pad_left, CONV_CHUNK), :]
                for o in outs:
                    o[pl.ds(start + c0, CONV_CHUNK), :] = acc.astype(o.dtype)

    out_dtypes = [f32, bf16] if emit_bf16 else [f32]
    return pl.pallas_call(
        body, name=name, out_shape=[jax.ShapeDtypeStruct((rows, width), dt) for dt in out_dtypes],
        grid=(width // cb,),
        in_specs=[pl.BlockSpec((rows, cb), lambda j: (0, j + x_cb0)), pl.BlockSpec((taps, cb), lambda j: (0, j)),
                  pl.BlockSpec((1, cb), lambda j: (0, j))],
        out_specs=[pl.BlockSpec((rows, cb), lambda j: (0, j))] * len(out_dtypes),
        scratch_shapes=[pltpu.VMEM((rows + 2 * HALO, cb), f32)],
        compiler_params=_cparams("parallel"),
    )(x, w, b)


def _dwconv_bwd(douts, x, x_cb0, w, taps, pad_left, segments, cb, name, dx_dtype):
    rows = x.shape[0]
    width = w.shape[1]
    nd = len(douts)

    def body(*refs):
        d_refs, x_ref, w_ref = refs[:nd], refs[nd], refs[nd + 1]
        dx_ref, dw_ref, db_ref, dp, dsum = refs[nd + 2:]
        dw_ref[...] = jnp.zeros_like(dw_ref)
        db_ref[...] = jnp.zeros_like(db_ref)
        if nd > 1:
            total = d_refs[0][...]
            for r in d_refs[1:]:
                total = total + r[...]
            dsum[...] = total
            d_ref = dsum
        else:
            d_ref = d_refs[0]
        for start, n in segments:
            _fill_padded(dp, d_ref, start, n)
            for c0 in range(0, n, CONV_CHUNK):
                db_ref[...] += jnp.sum(dp[pl.ds(HALO + c0, CONV_CHUNK), :], axis=0, keepdims=True)
                xchunk = x_ref[pl.ds(start + c0, CONV_CHUNK), :].astype(f32)
                acc = jnp.zeros((CONV_CHUNK, cb), f32)
                for k in range(taps):
                    shifted = dp[pl.ds(HALO + c0 + pad_left - k, CONV_CHUNK), :]
                    acc = acc + w_ref[k:k + 1, :] * shifted
                    dw_ref[k:k + 1, :] += jnp.sum(shifted * xchunk, axis=0, keepdims=True)
                dx_ref[pl.ds(start + c0, CONV_CHUNK), :] = acc.astype(dx_ref.dtype)

    dspec = pl.BlockSpec((rows, cb), lambda j: (0, j))
    return pl.pallas_call(
        body, name=name,
        out_shape=[jax.ShapeDtypeStruct((rows, width), dx_dtype), jax.ShapeDtypeStruct((taps, width), f32),
                   jax.ShapeDtypeStruct((1, width), f32)],
        grid=(width // cb,),
        in_specs=[dspec] * nd + [pl.BlockSpec((rows, cb), lambda j: (0, j + x_cb0)),
                                 pl.BlockSpec((taps, cb), lambda j: (0, j))],
        out_specs=[dspec, pl.BlockSpec((taps, cb), lambda j: (0, j)), pl.BlockSpec((1, cb), lambda j: (0, j))],
        scratch_shapes=[pltpu.VMEM((rows + 2 * HALO, cb), f32), pltpu.VMEM((rows, cb), f32)],
        compiler_params=_cparams("parallel"),
    )(*douts, x, w)


def _ada_forward(c16, w_ada, b_loc):
    def body(c_ref, w_ref, b_ref, o_ref):
        cv = c_ref[...]
        s = (cv * _sigmoid(cv)).astype(bf16)
        o_ref[0] = jnp.dot(s, w_ref[0].astype(bf16), preferred_element_type=f32) + b_ref[0]

    return pl.pallas_call(
        body, name="ada_forward", out_shape=jax.ShapeDtypeStruct((2, 16, ADA_SHARD), f32), grid=(2,),
        in_specs=[pl.BlockSpec((16, D), lambda l: (0, 0)), pl.BlockSpec((1, D, ADA_SHARD), lambda l: (l, 0, 0)),
                  pl.BlockSpec((1, 1, ADA_SHARD), lambda l: (l, 0, 0))],
        out_specs=pl.BlockSpec((1, 16, ADA_SHARD), lambda l: (l, 0, 0)),
        compiler_params=_cparams("parallel"),
    )(c16, w_ada, b_loc)


def _ada_backward(c16, g16, w_ada):
    def body(c_ref, g_ref, w_ref, dw_ref, ds_ref):
        cv = c_ref[...]
        s = (cv * _sigmoid(cv)).astype(bf16)
        g = g_ref[0].astype(bf16)
        dw_ref[0] = lax.dot_general(s, g, (((0,), (0,)), ((), ())), preferred_element_type=f32)
        ds = lax.dot_general(g, w_ref[0].astype(bf16), (((1,), (1,)), ((), ())), preferred_element_type=f32)
        cc = cv[8:9]
        sg = _sigmoid(cc)
        dsilu = sg * (1.0 + cc * (1.0 - sg))
        ds_ref[0] = jnp.zeros((8, D), f32) + jnp.sum(ds[8:16], axis=0, keepdims=True) * dsilu

    return pl.pallas_call(
        body, name="ada_backward",
        out_shape=[jax.ShapeDtypeStruct((2, D, ADA_SHARD), f32), jax.ShapeDtypeStruct((2, 8, D), f32)], grid=(2,),
        in_specs=[pl.BlockSpec((16, D), lambda l: (0, 0)), pl.BlockSpec((1, 16, ADA_SHARD), lambda l: (l, 0, 0)),
                  pl.BlockSpec((1, D, ADA_SHARD), lambda l: (l, 0, 0))],
        out_specs=[pl.BlockSpec((1, D, ADA_SHARD), lambda l: (l, 0, 0)), pl.BlockSpec((1, 8, D), lambda l: (l, 0, 0))],
        compiler_params=_cparams("parallel"),
    )(c16, g16, w_ada)


def _adamw(pieces, w, m, v, name, after=()):
    rows, cols = w.shape
    n_arr, n_after = len(pieces), len(after)
    counts = [cnt for _, cnt in pieces]
    pieces = [p for p, _ in pieces]
    tm = 256 if (rows % 256 == 0 and rows > 256) else rows

    def body(*refs):
        p_refs = refs[:n_arr]
        w_ref, m_ref, v_ref = refs[n_arr:n_arr + 3]
        g_ref, d_ref, nm_ref, nv_ref = refs[n_arr + 3 + n_after:]
        g = None
        for p_ref in p_refs:
            for j in range(p_ref.shape[0]):
                term = p_ref[j].astype(f32)
                g = term if g is None else g + term
        m2 = ADAM_B1 * m_ref[...] + (1.0 - ADAM_B1) * g
        v2 = ADAM_B2 * v_ref[...] + (1.0 - ADAM_B2) * (g * g)
        m_hat = m2 / (1.0 - ADAM_B1 ** ADAM_STEP)
        v_hat = v2 / (1.0 - ADAM_B2 ** ADAM_STEP)
        g_ref[...] = g
        d_ref[...] = -ADAM_LR * (m_hat / (jnp.sqrt(v_hat) + ADAM_EPS) + ADAM_WD * w_ref[...])
        nm_ref[...] = m2
        nv_ref[...] = v2

    spec = pl.BlockSpec((tm, cols), lambda i: (i, 0))
    return pl.pallas_call(
        body, name=name, out_shape=[jax.ShapeDtypeStruct((rows, cols), f32)] * 4, grid=(rows // tm,),
        in_specs=[pl.BlockSpec((cnt, tm, cols), lambda i: (0, i, 0)) for cnt in counts] + [spec, spec, spec]
        + [ANY_SPEC] * n_after,
        out_specs=[spec] * 4, compiler_params=_cparams("parallel"),
    )(*pieces, w, m, v, *after)


MLP_TM = 256
FB = F // N_DEV


def _stack_rows(vals, n):
    cols = vals[0].shape[1]
    rid = lax.broadcasted_iota(jnp.int32, (n, cols), 0)
    out = jnp.zeros((n, cols), f32)
    for k, v in enumerate(vals):
        out = jnp.where(rid == k, v, out)
    return out


N_MLP_PARAMS = 9


class _ParamRows:
    def __init__(self, ref):
        self.ref = ref

    def __getitem__(self, sl):
        return self.ref[8 * sl.start:8 * sl.start + 1, :]


def _resident(shape, imap):
    return pl.BlockSpec(shape, imap, pipeline_mode=pl.Buffered(1))


def _mlp_forward(xa, xa_roff, out_prev, par, w_in, w_out, layer, name):
    def body(xa_ref, op_ref, par_ref, win_ref, wout_ref, x1_ref, h_ref, r_ref, mo_ref, x2_ref, hn_ref):
        p = _ParamRows(par_ref)
        x1 = xa_ref[...] + p[0:1] * (op_ref[...] + p[1:2])
        h = _normmod(x1, p[2:3], p[3:4], p[4:5]).astype(bf16)
        x1_ref[...] = x1
        h_ref[...] = h
        mo = jnp.zeros((MLP_TM, D), f32)
        for j in range(N_DEV):
            r = jnp.maximum(jnp.dot(h, win_ref[j], preferred_element_type=f32), 0.0)
            r_ref[:, j * FB:(j + 1) * FB] = r.astype(bf16)
            mo = mo + jnp.dot((r * r).astype(bf16), wout_ref[j], preferred_element_type=f32)
        mo_ref[...] = mo.astype(bf16)
        x2 = x1 + p[5:6] * mo
        x2_ref[...] = x2
        hn_ref[...] = _normmod(x2, p[6:7], p[7:8], p[8:9]).astype(bf16)

    row = lambda width: pl.BlockSpec((MLP_TM, width), lambda i: (i, 0))
    return pl.pallas_call(
        body, name=name, grid=(T_LAT // MLP_TM,),
        out_shape=[jax.ShapeDtypeStruct((T_LAT, D), f32), jax.ShapeDtypeStruct((T_LAT, D), bf16),
                   jax.ShapeDtypeStruct((T_LAT, F), bf16), jax.ShapeDtypeStruct((T_LAT, D), bf16),
                   jax.ShapeDtypeStruct((T_LAT, D), f32), jax.ShapeDtypeStruct((T_LAT, D), bf16)],
        in_specs=[pl.BlockSpec((MLP_TM, D), lambda i: (i + xa_roff, 0)), row(D), pl.BlockSpec((8 * N_MLP_PARAMS, D), lambda i: (0, 0)),
                  _resident((N_DEV, None, D, FB), lambda i: (0, layer, 0, 0)),
                  _resident((N_DEV, None, FB, D), lambda i: (0, layer, 0, 0))],
        out_specs=[row(D), row(D), row(F), row(D), row(D), row(D)],
        compiler_params=_cparams("parallel"),
    )(xa, out_prev, par, w_in, w_out)


def _mlp_backward(dx2, x1, r, mo, out_prev, par, w_in, w_out, layer, name, after=()):
    nt = (((1,), (1,)), ((), ()))

    n_after = len(after)

    def body(dx2_ref, x1_ref, r_ref, mo_ref, op_ref, par_ref, win_ref, wout_ref, *rest):
        dx1_ref, dop_ref, dmo_ref, dhid_ref, acc_ref = rest[n_after:]
        p = _ParamRows(par_ref)
        dx2v = dx2_ref[...]
        dmo = (p[5:6] * dx2v).astype(bf16)
        dmo_ref[...] = dmo
        dh = jnp.zeros((MLP_TM, D), f32)
        mo = mo_ref[...].astype(f32)
        for j in range(N_DEV):
            rf = r_ref[:, j * FB:(j + 1) * FB].astype(f32)
            dact = lax.dot_general(dmo, wout_ref[j], nt, preferred_element_type=f32)
            dhid = (dact * (2.0 * rf)).astype(bf16)
            dhid_ref[:, j * FB:(j + 1) * FB] = dhid
            dh = dh + lax.dot_general(dhid, win_ref[j], nt, preferred_element_type=f32)
        x1 = x1_ref[...]
        _, vjp = jax.vjp(_normmod, x1, p[2:3], p[3:4], p[4:5])
        dx, dng, dsc, dsh = vjp(dh)
        dx1 = dx2v + dx
        dx1_ref[...] = dx1
        dop_ref[...] = (p[0:1] * dx1).astype(bf16)
        sums = _stack_rows([jnp.sum(dx1 * (op_ref[...] + p[1:2]), axis=0, keepdims=True),
                            p[0:1] * jnp.sum(dx1, axis=0, keepdims=True), dng, dsc, dsh,
                            jnp.sum(dx2v * mo, axis=0, keepdims=True)], 8)

        @pl.when(pl.program_id(0) == 0)
        def _():
            acc_ref[...] = jnp.zeros_like(acc_ref)

        acc_ref[...] += sums

    row = lambda width: pl.BlockSpec((MLP_TM, width), lambda i: (i, 0))
    return pl.pallas_call(
        body, name=name, grid=(T_LAT // MLP_TM,),
        out_shape=[jax.ShapeDtypeStruct((T_LAT, D), f32), jax.ShapeDtypeStruct((T_LAT, D), bf16),
                   jax.ShapeDtypeStruct((T_LAT, D), bf16), jax.ShapeDtypeStruct((T_LAT, F), bf16),
                   jax.ShapeDtypeStruct((8, D), f32)],
        in_specs=[row(D), row(D), row(F), row(D), row(D), pl.BlockSpec((8 * N_MLP_PARAMS, D), lambda i: (0, 0)),
                  _resident((N_DEV, None, D, FB), lambda i: (0, layer, 0, 0)),
                  _resident((N_DEV, None, FB, D), lambda i: (0, layer, 0, 0))] + [ANY_SPEC] * n_after,
        out_specs=[row(D), row(D), row(D), row(F), pl.BlockSpec((8, D), lambda i: (0, 0))],
        compiler_params=_cparams("arbitrary"),
    )(dx2, x1, r, mo, out_prev, par, w_in, w_out, *after)


def _mlp_weight_grads(h, dhid, r, dmo, layer, other, tag):
    tn = (((0,), (0,)), ((), ()))

    def body_in(h_ref, dhid_ref, *rest):
        rest[-1][...] = lax.dot_general(h_ref[...], dhid_ref[...], tn, preferred_element_type=f32).astype(bf16)

    def body_out(r_ref, dmo_ref, *rest):
        rf = r_ref[...].astype(f32)
        rest[-1][...] = lax.dot_general((rf * rf).astype(bf16), dmo_ref[...], tn,
                                        preferred_element_type=f32).astype(bf16)

    def call(body, name, operands, specs, block, prev):
        extra = [] if prev is None else [prev]
        return pl.pallas_call(
            body, name=name, grid=(N_DEV,), out_shape=jax.ShapeDtypeStruct((N_DEV, 2) + block, bf16),
            in_specs=specs + [pl.BlockSpec(memory_space=pl.ANY)] * len(extra),
            out_specs=pl.BlockSpec((None, None) + block, lambda j: (j, layer, 0, 0)),
            input_output_aliases={} if prev is None else {2: 0},
            compiler_params=_cparams("parallel"),
        )(*operands, *extra)

    dw_in = call(body_in, tag + "_mlp_in_dw", [h, dhid],
                 [_resident((T_LAT, D), lambda j: (0, 0)), pl.BlockSpec((T_LAT, FB), lambda j: (0, j))], (D, FB),
                 None if other is None else other[0])
    dw_out = call(body_out, tag + "_mlp_out_dw", [r, dmo],
                  [pl.BlockSpec((T_LAT, FB), lambda j: (0, j)), _resident((T_LAT, D), lambda j: (0, 0))], (FB, D),
                  None if other is None else other[1])
    return dw_in, dw_out


def _pos_embed():
    n_rows = T_LAT // GRID_W
    q = D // 4
    omega = 1.0 / (POS_BASE ** (jnp.arange(q, dtype=f32) / q))
    er = jnp.arange(n_rows, dtype=jnp.int32).astype(f32)[:, None] * omega[None, :]
    ec = jnp.arange(GRID_W, dtype=jnp.int32).astype(f32)[:, None] * omega[None, :]
    by_row = jnp.concatenate([jnp.sin(er), jnp.cos(er)], axis=-1)[:, None, :]
    by_col = jnp.concatenate([jnp.sin(ec), jnp.cos(ec)], axis=-1)[None, :, :]
    full = jnp.concatenate([jnp.broadcast_to(by_row, (n_rows, GRID_W, D // 2)),
                            jnp.broadcast_to(by_col, (n_rows, GRID_W, D // 2))], axis=-1)
    return full.reshape(T_LAT, D)


HALF = R // 2
BLK_PER_HALF = N_BLK // 2
N_PARTS = 4


def _gate_matrix(w_a, w_x):
    eye = jnp.eye(BLK_PER_HALF, dtype=bf16)
    cols = []
    for h in range(2):
        for d in range(2):
            for w in (w_a, w_x):
                blocks = w[d, BLK_PER_HALF * h:BLK_PER_HALF * (h + 1)].astype(bf16)
                cols.append(jnp.einsum("hij,hg->higj", blocks, eye).reshape(HALF, HALF))
    return jnp.concatenate(cols, axis=1)


def _gate_blocks(dwg, part):
    out = []
    for h in range(2):
        blk = dwg[:, (N_PARTS * h + part) * HALF:(N_PARTS * h + part + 1) * HALF]
        blk = blk.reshape(BLK_PER_HALF, BLK, BLK_PER_HALF, BLK)
        out.append(jnp.moveaxis(jnp.diagonal(blk, axis1=0, axis2=2), -1, 0))
    return jnp.concatenate(out, axis=0)


GATE_BM = 768


def _gates_dx(dpre, wg, after=()):
    rows = dpre.shape[0]
    n_after = len(after)

    def body(d_ref, w_ref, *rest):
        rest[n_after][...] = lax.dot_general(d_ref[...], w_ref[...], (((1,), (1,)), ((), ())),
                                             preferred_element_type=f32)

    return pl.pallas_call(
        body, name="l0_gates_dx", grid=(rows // GATE_BM, 2), out_shape=jax.ShapeDtypeStruct((rows, R), f32),
        in_specs=[pl.BlockSpec((GATE_BM, N_PARTS * HALF), lambda i, h: (i, h)),
                  pl.BlockSpec((HALF, N_PARTS * HALF), lambda i, h: (0, h))] + [ANY_SPEC] * n_after,
        out_specs=pl.BlockSpec((GATE_BM, HALF), lambda i, h: (i, h)),
        compiler_params=_cparams("parallel", "parallel"),
    )(dpre, wg, *after)


COEFF_TM = 256


def _dir_params(d, *params):
    specs = [pl.BlockSpec((None, 1, HALF), lambda h, i: (d, 0, h))] * len(params)
    return specs, [p.reshape(2, 1, R) for p in params]


def _gates_coeff_fwd(ub, u, wg, ba, bx, lam, d):
    rows = u.shape[0]

    def body(ub_ref, u_ref, w_ref, ba_ref, bx_ref, lam_ref, a_ref, b_ref):
        pre = jnp.dot(ub_ref[...], w_ref[...], preferred_element_type=f32)
        a, b = _coeff(pre[:, :HALF], pre[:, HALF:], u_ref[...], ba_ref[...], bx_ref[...], lam_ref[...])
        a_ref[...] = a
        b_ref[...] = b

    tile = pl.BlockSpec((COEFF_TM, HALF), lambda h, i: (i, h))
    pspecs, pargs = _dir_params(d, ba, bx, lam)
    return pl.pallas_call(
        body, name=f"l0_gates_coeff_{d}", grid=(2, rows // COEFF_TM),
        out_shape=[jax.ShapeDtypeStruct((rows, R), f32)] * 2,
        in_specs=[tile, tile, pl.BlockSpec((HALF, 2 * HALF), lambda h, i: (0, 2 * h + d))] + pspecs,
        out_specs=[tile, tile], compiler_params=_cparams("parallel", "parallel"),
    )(ub, u, wg, *pargs)


def _gates_coeff_bwd(ub, u, dh, yp, wg, ba, bx, lam, d, dpre_prev):
    rows = u.shape[0]
    n_prev = 0 if dpre_prev is None else 1

    def body(ub_ref, u_ref, dh_ref, yp_ref, w_ref, ba_ref, bx_ref, lam_ref, *rest):
        dpre_ref, du_ref, dba_ref, dbx_ref, dlam_ref = rest[n_prev:]
        pre = jnp.dot(ub_ref[...], w_ref[...], preferred_element_type=f32)
        dhv = dh_ref[...]
        dpa, dpx, du, dba, dbx, dlam = _coeff_bwd(pre[:, :HALF], pre[:, HALF:], u_ref[...], ba_ref[...], bx_ref[...],
                                                  lam_ref[...], dhv * yp_ref[...], dhv)
        dpre_ref[:, :HALF] = dpa.astype(bf16)
        dpre_ref[:, HALF:] = dpx.astype(bf16)
        du_ref[...] = du

        @pl.when(pl.program_id(1) == 0)
        def _():
            dba_ref[...] = jnp.zeros_like(dba_ref)
            dbx_ref[...] = jnp.zeros_like(dbx_ref)
            dlam_ref[...] = jnp.zeros_like(dlam_ref)

        dba_ref[...] += dba
        dbx_ref[...] += dbx
        dlam_ref[...] += dlam

    tile = pl.BlockSpec((COEFF_TM, HALF), lambda h, i: (i, h))
    acc = pl.BlockSpec((1, HALF), lambda h, i: (0, h))
    pspecs, pargs = _dir_params(d, ba, bx, lam)
    extra = [] if dpre_prev is None else [dpre_prev]
    return pl.pallas_call(
        body, name=f"l0_gates_coeff_bwd_{d}", grid=(2, rows // COEFF_TM),
        out_shape=[jax.ShapeDtypeStruct((rows, 2 * N_PARTS * HALF), bf16), jax.ShapeDtypeStruct((rows, R), f32)]
        + [jax.ShapeDtypeStruct((1, R), f32)] * 3,
        in_specs=[tile] * 4 + [pl.BlockSpec((HALF, 2 * HALF), lambda h, i: (0, 2 * h + d))] + pspecs
        + [ANY_SPEC] * n_prev,
        out_specs=[pl.BlockSpec((COEFF_TM, 2 * HALF), lambda h, i: (i, 2 * h + d)), tile, acc, acc, acc],
        input_output_aliases={8: 0} if n_prev else {}, compiler_params=_cparams("parallel", "arbitrary"),
    )(ub, u, dh, yp, wg, *pargs, *extra)


def _gates_dw(u, dpre):
    rows = u.shape[0]

    def body(u_ref, d_ref, o_ref):
        o_ref[...] = lax.dot_general(u_ref[...], d_ref[...], (((0,), (0,)), ((), ())), preferred_element_type=f32)

    return pl.pallas_call(
        body, name="l0_gates_dw", grid=(2 * N_PARTS,), out_shape=jax.ShapeDtypeStruct((HALF, 2 * N_PARTS * HALF), f32),
        in_specs=[pl.BlockSpec((rows, HALF), lambda j: (0, j // N_PARTS)), pl.BlockSpec((rows, HALF), lambda j: (0, j))],
        out_specs=pl.BlockSpec((HALF, HALF), lambda j: (0, j)), compiler_params=_cparams("parallel"),
    )(u, dpre)


N_SCAN_CHUNKS = T_ALL // SCAN_CHUNK
SCAN_FWD = lambda t: t
SCAN_FWD_BWD = lambda t: N_SCAN_CHUNKS - 1 - t
SCAN_REV = lambda t: jnp.where(t == 0, 0, N_SCAN_CHUNKS - t)
SCAN_REV_BWD = lambda t: jnp.where(t == N_SCAN_CHUNKS - 1, 0, t + 1)
CONV_SEGMENTS = ((0, T_CTX), (T_CTX, T_LAT))
TM = 128
FUSED_TM = 256


def _local_step(x, ctx, target, mods, cmod, wts, late_weights, send_grads, reduce_loss, start_after=()):
    sh1, sc1, g1, sh2, sc2, g2 = [[mods[l, i][None] for l in range(2)] for i in range(N_MOD)]
    ng = wts["norm_g"]
    xcat = jnp.concatenate([ctx, x], axis=0)
    poscat = jnp.concatenate([jnp.zeros((T_CTX, D), f32), _pos_embed()], axis=0)
    scp = jnp.concatenate([cmod[1][None], sc1[0]], axis=0)
    shp = jnp.concatenate([cmod[0][None], sh1[0]], axis=0)

    ctx_tiles = T_CTX // FUSED_TM
    nt = (((1,), (1,)), ((), ()))

    def blend(i, p):
        sel = jnp.where(i < ctx_tiles, 1.0, 0.0)
        return sel * p[0:1] + (1.0 - sel) * p[1:2]

    def f_pre0(i, xc, pos, g, scp_, shp_, w):
        x0 = xc + pos
        h = _normmod(x0, g, blend(i, scp_), blend(i, shp_)).astype(bf16)
        return x0, h, jnp.dot(h, w, preferred_element_type=f32)

    x0cat, h0, gr = _rowcall(f_pre0, "l0_prenorm_in_proj", T_ALL, FUSED_TM, [_rin(xcat), _rin(poscat)],
                             [ng[0, 0][None], scp, shp, wts["rec_w_in"]], [(D, f32), (D, bf16), (2 * R, f32)],
                             after=start_after)
    u, ub = _dwconv_fwd(gr, R // 256, wts["rec_conv_w"], wts["rec_conv_b"], 4, 1, CONV_SEGMENTS, 256,
                        "l0_conv", True)
    gate_args = (wts["gates"], wts["rec_b_a"], wts["rec_b_x"], wts["rec_lambda"])
    a0, b0 = _gates_coeff_fwd(ub, u, *gate_args, 0)
    a1, b1 = _gates_coeff_fwd(ub, u, *gate_args, 1)
    y0, yp0 = _scan_call(a0, b0, SCAN_FWD, False, "l0_scan_fwd", False)
    y1, yp1 = _scan_call(a1, b1, SCAN_REV, True, "l0_scan_rev", False)

    wts = dict(wts, **late_weights("mlp", y1))

    def f_gate_out(i, gp, y0_, y1_, w):
        z = (_gelu(gp) * (y0_ + y1_)).astype(bf16)
        return z, jnp.dot(z, w, preferred_element_type=f32)

    zb, out0 = _rowcall(f_gate_out, "l0_gate_out_proj", T_LAT, FUSED_TM,
                        [_rin(gr, R, 0, ctx_tiles), _rin(y0, None, 0, ctx_tiles), _rin(y1, None, 0, ctx_tiles)],
                        [wts["rec_w_out"]], [(R, bf16), (D, f32)])

    zero_d = jnp.zeros((1, D), f32)

    def mlp_params(rows):
        rows = rows + [zero_d] * (N_MLP_PARAMS - len(rows))
        return jnp.concatenate([jnp.broadcast_to(r, (8, D)) for r in rows], axis=0)

    par0 = mlp_params([g1[0], zero_d, ng[0, 1][None], sc2[0], sh2[0], g2[0], ng[1, 0][None], sc1[1], sh1[1]])
    x1, h1, r0, mo0, x2, h2 = _mlp_forward(x0cat, T_CTX // MLP_TM, out0, par0, wts["mlp_w_in"], wts["mlp_w_out"], 0,
                                           "l0_mlp")

    wts = dict(wts, **late_weights("conf", x2))
    def glu(pa, pb, b1):
        return (pa + b1[:, :D]) * _sigmoid(pb + b1[:, D:])

    def f_pw1_glu(i, h_, b1, w):
        p = jnp.dot(h_, w, preferred_element_type=f32)
        return glu(p[:, :D], p[:, D:], b1), p

    zg, pw = _rowcall(f_pw1_glu, "l1_pw1_glu", T_LAT, FUSED_TM, [_rin(h2)], [wts["conf_b_pw1"], wts["conf_w_pw1"]],
                      [(D, f32), (2 * D, bf16)])
    (zc,) = _dwconv_fwd(zg, 0, wts["conf_conv_w"], wts["conf_conv_b"], 31, 15, ((0, T_LAT),), 128, "l1_conv", False)

    def ln_silu(z, lg, lb):
        mu = jnp.mean(z, axis=-1, keepdims=True)
        zc_ = z - mu
        var = jnp.mean(zc_ * zc_, axis=-1, keepdims=True)
        yv = zc_ * lax.rsqrt(var + EPS) * lg + lb
        return yv * _sigmoid(yv)

    def f_lnsilu_pw2(i, z, lg, lb, w):
        s = ln_silu(z, lg, lb).astype(bf16)
        return s, jnp.dot(s, w, preferred_element_type=f32)

    sb, out1 = _rowcall(f_lnsilu_pw2, "l1_ln_silu_pw2", T_LAT, FUSED_TM, [_rin(zc)],
                        [wts["conf_ln_g"], wts["conf_ln_b"], wts["conf_w_pw2"]], [(D, bf16), (D, f32)])
    par1 = mlp_params([g1[1], wts["conf_b_pw2"], ng[1, 1][None], sc2[1], sh2[1], g2[1]])
    x3, h3, r1, mo1, x4, _ = _mlp_forward(x2, 0, out1, par1, wts["mlp_w_in"], wts["mlp_w_out"], 1, "l1_mlp")

    def loss_fn(x4_, fg, tgt):
        err = _rms(x4_, fg) - tgt
        per_row = jnp.mean(err * err, axis=-1, keepdims=True)
        return 0.5 * jnp.sum(per_row, axis=0, keepdims=True)

    def f_head(i, x4_, tgt, fg):
        loss, vjp = jax.vjp(lambda a, e: loss_fn(a, e, tgt), x4_, fg)
        dx, dfg = vjp(jnp.ones((1, 1), f32))
        return dx, jnp.broadcast_to(loss, (1, 128)), dfg

    dx4, loss_acc, dfinal_g = _rowcall(f_head, "head", T_LAT, TM, [_rin(x4), _rin(target)], [wts["final_g"]],
                                       [(D, f32)], [(1, 128), (1, D)])

    grads = {"final_g": dfinal_g}
    loss = reduce_loss(loss_acc[0, 0])

    dx3, dout1, dmo1, dhid1, acc1 = _mlp_backward(dx4, x3, r1, mo1, out1, par1, wts["mlp_w_in"], wts["mlp_w_out"], 1,
                                                  "l1_mlp_bwd", after=[loss.reshape(1, 1)])
    mlp_dw = _mlp_weight_grads(h3, dhid1, r1, dmo1, 1, None, "l1")
    dg1_1, db_pw2, dng11, dsc2_1, dsh2_1, dg2_1 = [acc1[k:k + 1] for k in range(6)]

    grads["conf_w_pw2"] = _mm(sb, dout1, "l1_pw2_dw", ta=True, out_dtype=bf16)
    grads["conf_b_pw2"] = db_pw2

    def f_pw2_lnsilu_bwd(i, z, dout, lg, lb, w):
        ds = lax.dot_general(dout, w, nt, preferred_element_type=f32)
        _, vjp = jax.vjp(ln_silu, z, lg, lb)
        return vjp(ds)

    dzc, dln_g, dln_b = _rowcall(f_pw2_lnsilu_bwd, "l1_pw2_ln_silu_bwd", T_LAT, FUSED_TM, [_rin(zc), _rin(dout1)],
                                 [wts["conf_ln_g"], wts["conf_ln_b"], wts["conf_w_pw2"]], [(D, f32)], [(1, D)] * 2)
    grads["conf_ln_g"], grads["conf_ln_b"] = dln_g, dln_b
    dzg, dconv_w, dconv_b = _dwconv_bwd([dzc], zg, 0, wts["conf_conv_w"], 31, 15, ((0, T_LAT),), 128,
                                        "l1_conv_bwd", f32)
    grads["conf_conv_w"], grads["conf_conv_b"] = dconv_w, dconv_b

    def f_glu_pw1_norm_bwd(i, p_, dz, x_, dxs, b1, g_, sc_, sh_, w):
        pf = p_.astype(f32)
        _, vjp = jax.vjp(glu, pf[:, :D], pf[:, D:], b1)
        da, db, db1 = vjp(dz)
        dp = jnp.concatenate([da, db], axis=1).astype(bf16)
        dh = lax.dot_general(dp, w, nt, preferred_element_type=f32)
        _, vjp = jax.vjp(_normmod, x_, g_, sc_, sh_)
        dx, dg, dsc, dsh = vjp(dh)
        return dp, dx + dxs, db1, dg, dsc, dsh

    dpw, dx2, db_pw1, dng10, dsc1_1, dsh1_1 = _rowcall(
        f_glu_pw1_norm_bwd, "l1_glu_pw1_normmod_bwd", T_LAT, FUSED_TM, [_rin(pw), _rin(dzg), _rin(x2), _rin(dx3)],
        [wts["conf_b_pw1"], ng[1, 0][None], sc1[1], sh1[1], wts["conf_w_pw1"]], [(2 * D, bf16), (D, f32)],
        [(1, 2 * D), (1, D), (1, D), (1, D)])
    grads["conf_b_pw1"] = db_pw1
    grads["conf_w_pw1"] = _mm(h2, dpw, "l1_pw1_dw", ta=True, out_dtype=bf16)
    sent = send_grads(["conf_w_pw2", "conf_w_pw1"], grads)

    dx1, dout0, dmo0, dhid0, acc0 = _mlp_backward(dx2, x1, r0, mo0, out0, par0, wts["mlp_w_in"], wts["mlp_w_out"], 0,
                                                  "l0_mlp_bwd", after=[sent])
    grads["mlp_w_in"], grads["mlp_w_out"] = _mlp_weight_grads(h1, dhid0, r0, dmo0, 0, mlp_dw, "l0")
    sent = send_grads(["mlp_w_in", "mlp_w_out"], grads)
    dg1_0, _, dng01, dsc2_0, dsh2_0, dg2_0 = [acc0[k:k + 1] for k in range(6)]

    grads["rec_w_out"] = _mm(zb, dout0, "l0_out_proj_dw", ta=True, out_dtype=bf16, after=[sent])
    sent = send_grads(["rec_w_out"], grads)

    def f_out_gate_bwd(i, gp, y0_, y1_, dout, w):
        lat = jnp.where(i < ctx_tiles, 0.0, 1.0)
        dz = lax.dot_general(dout, w, nt, preferred_element_type=f32)
        _, vjp = jax.vjp(lambda a, b: _gelu(a) * b, gp, y0_ + y1_)
        dgp, dy = vjp(dz)
        return dgp * lat, dy * lat

    dgp, dy = _rowcall(f_out_gate_bwd, "l0_out_proj_gate_bwd", T_ALL, FUSED_TM,
                       [_rin(gr, R, 0), _rin(y0), _rin(y1), _rin(dout0, None, 0, -ctx_tiles)], [wts["rec_w_out"]],
                       [(R, bf16), (R, f32)], after=[sent])
    (dh_f,) = _scan_call(a0, dy, SCAN_FWD_BWD, True, "l0_scan_fwd_bwd", True)
    (dh_r,) = _scan_call(a1, dy, SCAN_REV_BWD, False, "l0_scan_rev_bwd", True)

    dpre, du_f, *dpar_f = _gates_coeff_bwd(ub, u, dh_f, yp0, *gate_args, 0, None)
    dpre, du_r, *dpar_r = _gates_coeff_bwd(ub, u, dh_r, yp1, *gate_args, 1, dpre)
    grads["rec_b_a"], grads["rec_b_x"], grads["rec_lambda"] = [
        jnp.concatenate([f.reshape(-1), r_.reshape(-1)]).reshape(2, R) for f, r_ in zip(dpar_f, dpar_r)]
    grads["gates"] = _gates_dw(ub, dpre)
    sent = send_grads(["replicated"], grads)
    du_gates = _gates_dx(dpre, wts["gates"], after=[sent])
    drec, dconv4_w, dconv4_b = _dwconv_bwd([du_f, du_r, du_gates], gr, R // 256, wts["rec_conv_w"], 4, 1,
                                           CONV_SEGMENTS, 256, "l0_conv_bwd", bf16)
    grads["rec_conv_w"], grads["rec_conv_b"] = dconv4_w, dconv4_b
    dgr = jnp.concatenate([dgp, drec], axis=1)
    grads["rec_w_in"] = _mm(h0, dgr, "l0_in_proj_dw", ta=True, out_dtype=bf16)
    sent = send_grads(["rec_w_in"], grads)

    def f_pre0_bwd(i, x0, dgr_, dxs, g, scp_, shp_, w):
        lat = jnp.where(i < ctx_tiles, 0.0, 1.0)
        dh = lax.dot_general(dgr_, w, nt, preferred_element_type=f32)
        _, vjp = jax.vjp(lambda a, b, c, e: _normmod(a, b, blend(i, c), blend(i, e)), x0, g, scp_, shp_)
        dx, dg, dscp, dshp = vjp(dh)
        return dx + lat * dxs, dg, dscp, dshp

    dx0cat, dng00, dscp, dshp = _rowcall(
        f_pre0_bwd, "l0_in_proj_prenorm_bwd", T_ALL, FUSED_TM,
        [_rin(x0cat), _rin(dgr), _rin(dx1, None, 0, -ctx_tiles)], [ng[0, 0][None], scp, shp, wts["rec_w_in"]],
        [(D, f32)], [(1, D), (2, D), (2, D)], after=[sent])

    grads["norm_g"] = jnp.stack([jnp.concatenate([dng00, dng01], 0), jnp.concatenate([dng10, dng11], 0)])
    dmods = jnp.stack([
        jnp.concatenate([dshp[1:2], dscp[1:2], dg1_0, dsh2_0, dsc2_0, dg2_0], axis=0),
        jnp.concatenate([dsh1_1, dsc1_1, dg1_1, dsh2_1, dsc2_1, dg2_1], axis=0)])
    dcmod = jnp.concatenate([dshp[0:1], dscp[0:1]], axis=0)
    return loss, dx0cat[T_CTX:], dmods, dcmod, grads


def _unshard_cols(g):
    g = jnp.moveaxis(g, 0, -2)
    return g.reshape(g.shape[:-2] + (g.shape[-2] * g.shape[-1],))


def _shard_cols(w):
    w = w.reshape(w.shape[:-1] + (N_DEV, w.shape[-1] // N_DEV))
    return jnp.moveaxis(w, -2, 0)


def _shard_rows(w):
    return w.reshape((N_DEV, w.shape[0] // N_DEV) + w.shape[1:])


SMALL_PACK_ROWS = 64


def kernel(x, c, ctx, c_ctx, w_ada, b_ada, norm_g, rec_w_in, rec_conv_w, rec_conv_b, rec_lambda, rec_w_a, rec_b_a, rec_w_x, rec_b_x, rec_w_out, conf_w_pw1, conf_b_pw1, conf_conv_w, conf_conv_b, conf_ln_g, conf_ln_b, conf_w_pw2, conf_b_pw2, mlp_w_in, mlp_w_out, final_g, loss_target, m_c_ctx, m_w_ada, m_b_ada, m_norm_g, m_rec_w_in, m_rec_conv_w, m_rec_conv_b, m_rec_lambda, m_rec_w_a, m_rec_b_a, m_rec_w_x, m_rec_b_x, m_rec_w_out, m_conf_w_pw1, m_conf_b_pw1, m_conf_conv_w, m_conf_conv_b, m_conf_ln_g, m_conf_ln_b, m_conf_w_pw2, m_conf_b_pw2, m_mlp_w_in, m_mlp_w_out, m_final_g, v_c_ctx, v_w_ada, v_b_ada, v_norm_g, v_rec_w_in, v_rec_conv_w, v_rec_conv_b, v_rec_lambda, v_rec_w_a, v_rec_b_a, v_rec_w_x, v_rec_b_x, v_rec_w_out, v_conf_w_pw1, v_conf_b_pw1, v_conf_conv_w, v_conf_conv_b, v_conf_ln_g, v_conf_ln_b, v_conf_w_pw2, v_conf_b_pw2, v_mlp_w_in, v_mlp_w_out, v_final_g):
    me = 4 * lax.axis_index("x") + 2 * lax.axis_index("y") + lax.axis_index("c")
    weights = dict(c_ctx=c_ctx, w_ada=w_ada, b_ada=b_ada, norm_g=norm_g, rec_w_in=rec_w_in, rec_conv_w=rec_conv_w,
                   rec_conv_b=rec_conv_b, rec_lambda=rec_lambda, rec_w_a=rec_w_a, rec_b_a=rec_b_a, rec_w_x=rec_w_x,
                   rec_b_x=rec_b_x, rec_w_out=rec_w_out, conf_w_pw1=conf_w_pw1, conf_b_pw1=conf_b_pw1,
                   conf_conv_w=conf_conv_w, conf_conv_b=conf_conv_b, conf_ln_g=conf_ln_g, conf_ln_b=conf_ln_b,
                   conf_w_pw2=conf_w_pw2, conf_b_pw2=conf_b_pw2, mlp_w_in=mlp_w_in, mlp_w_out=mlp_w_out, final_g=final_g)
    m_in = dict(c_ctx=m_c_ctx, w_ada=m_w_ada, b_ada=m_b_ada, norm_g=m_norm_g, rec_w_in=m_rec_w_in, rec_conv_w=m_rec_conv_w,
                rec_conv_b=m_rec_conv_b, rec_lambda=m_rec_lambda, rec_w_a=m_rec_w_a, rec_b_a=m_rec_b_a, rec_w_x=m_rec_w_x,
                rec_b_x=m_rec_b_x, rec_w_out=m_rec_w_out, conf_w_pw1=m_conf_w_pw1, conf_b_pw1=m_conf_b_pw1,
                conf_conv_w=m_conf_conv_w, conf_conv_b=m_conf_conv_b, conf_ln_g=m_conf_ln_g, conf_ln_b=m_conf_ln_b,
                conf_w_pw2=m_conf_w_pw2, conf_b_pw2=m_conf_b_pw2, mlp_w_in=m_mlp_w_in, mlp_w_out=m_mlp_w_out,
                final_g=m_final_g)
    v_in = dict(c_ctx=v_c_ctx, w_ada=v_w_ada, b_ada=v_b_ada, norm_g=v_norm_g, rec_w_in=v_rec_w_in, rec_conv_w=v_rec_conv_w,
                rec_conv_b=v_rec_conv_b, rec_lambda=v_rec_lambda, rec_w_a=v_rec_w_a, rec_b_a=v_rec_b_a, rec_w_x=v_rec_w_x,
                rec_b_x=v_rec_b_x, rec_w_out=v_rec_w_out, conf_w_pw1=v_conf_w_pw1, conf_b_pw1=v_conf_b_pw1,
                conf_conv_w=v_conf_conv_w, conf_conv_b=v_conf_conv_b, conf_ln_g=v_conf_ln_g, conf_ln_b=v_conf_ln_b,
                conf_w_pw2=v_conf_w_pw2, conf_b_pw2=v_conf_b_pw2, mlp_w_in=v_mlp_w_in, mlp_w_out=v_mlp_w_out,
                final_g=v_final_g)
    names = list(weights)

    small_items = [c, norm_g, rec_conv_w, rec_lambda, conf_b_pw1, conf_conv_w, conf_conv_b, conf_ln_g, conf_ln_b,
                   conf_b_pw2]
    flat = jnp.concatenate([a.reshape(-1) for a in small_items])
    flat = jnp.pad(flat, (0, SMALL_PACK_ROWS * 128 - flat.shape[0])).reshape(SMALL_PACK_ROWS, 128)
    as_shard = lambda a: a.astype(bf16).reshape(-1, a.shape[-1])
    small_all, early = _all_gather_2level([flat, as_shard(rec_w_in[0])], "gather_small_and_early")

    small_all = small_all.reshape(N_DEV, -1)
    off = 0
    small = []
    for a in small_items:
        small.append(small_all[:, off:off + a.size].reshape((N_DEV,) + a.shape))
        off += a.size
    c_all, ng_all, rcw_all, lam_all, bpw1_all, ccw_all, ccb_all, lng_all, lnb_all, bpw2_all = small
    wts = {
        "norm_g": _unshard_cols(ng_all),
        "rec_conv_w": _unshard_cols(rcw_all)[0],
        "rec_lambda": _unshard_cols(lam_all)[0],
        "conf_b_pw1": _unshard_cols(bpw1_all),
        "conf_conv_w": _unshard_cols(ccw_all)[0],
        "conf_conv_b": _unshard_cols(ccb_all),
        "conf_ln_g": _unshard_cols(lng_all),
        "conf_ln_b": _unshard_cols(lnb_all),
        "conf_b_pw2": _unshard_cols(bpw2_all),
        "rec_conv_b": rec_conv_b,
        "rec_b_a": rec_b_a[0].reshape(2, R),
        "rec_b_x": rec_b_x[0].reshape(2, R),
        "final_g": final_g[None],
        "gates": _gate_matrix(rec_w_a[0], rec_w_x[0]),
    }

    c16 = jnp.concatenate([c_all[:, 0], jnp.broadcast_to(c_ctx[None], (8, D))], axis=0)
    b_loc = lax.dynamic_slice_in_dim(b_ada, me * ADA_SHARD, ADA_SHARD, axis=1)[:, None]
    (mods_all,) = _all_gather([_ada_forward(c16, w_ada, b_loc)], "gather_mods")
    mods_all = _unshard_cols(mods_all)
    mods = lax.dynamic_index_in_dim(mods_all, me, axis=1, keepdims=False).reshape(2, N_MOD, D)
    cmod = mods_all[0, 8, :2 * D].reshape(2, D)

    wts["rec_w_in"] = _unshard_cols(early)
    late_items = {"mlp": [rec_w_out[0], mlp_w_in, mlp_w_out], "conf": [conf_w_pw1[0], conf_w_pw2[0]]}
    late_handles, order = {}, [early, mods]
    for group in ("mlp", "conf"):
        shards = [as_shard(a) for a in late_items[group]]
        lands = [_own_block_filled(s, me) for s in shards]
        if group == "mlp":
            late_handles[group], token = _chip_gather_start(shards, lands, "gather_mlp_start", after=order)
        else:
            late_handles[group], token = _exchange_start(shards, lands, "gather_conf_start", False, after=order)
        order = [token]

    def late_weights(group, after):
        if group == "mlp":
            forwarded = _chip_gather_forward(late_handles[group], after, "gather_mlp_forward")
            got = _chip_gather_wait(forwarded, after, "gather_mlp_wait")
        else:
            got = _exchange_wait(late_handles[group], after, "gather_conf_wait", False)
        got = [g.reshape((N_DEV,) + a.shape) for g, a in zip(got, late_items[group])]
        if group == "mlp":
            return {"rec_w_out": got[0].reshape(R, D), "mlp_w_in": got[1], "mlp_w_out": got[2]}
        return {"conf_w_pw1": _unshard_cols(got[0]), "conf_w_pw2": got[1].reshape(D, D)}

    to_blocks = {"rec_w_in": _shard_cols, "conf_w_pw1": _shard_cols, "rec_w_out": _shard_rows, "conf_w_pw2": _shard_rows,
                 "mlp_w_in": lambda g: g, "mlp_w_out": lambda g: g}
    grad_handles = []

    repl_names = ["rec_w_a", "rec_w_x", "rec_b_a", "rec_b_x", "final_g"]

    def send_replicated(grads):
        dwg = grads["gates"]
        repl = {"rec_w_a": jnp.stack([_gate_blocks(dwg, 0), _gate_blocks(dwg, 2)]),
                "rec_w_x": jnp.stack([_gate_blocks(dwg, 1), _gate_blocks(dwg, 3)]),
                "rec_b_a": grads["rec_b_a"], "rec_b_x": grads["rec_b_x"], "final_g": grads["final_g"]}
        flat = jnp.concatenate([repl[n].reshape(-1) for n in repl_names])
        rows = -(-flat.shape[0] // (16 * D)) * 16
        flat = jnp.pad(flat, (0, rows * D - flat.shape[0])).reshape(rows, D).astype(bf16)
        handle, sent = _exchange_start([flat], [_own_block_filled(flat, me)], "gather_replicated_start", False)
        grad_handles.append((["replicated"], handle))
        return sent

    def send_grads(group, grads):
        if group == ["replicated"]:
            return send_replicated(grads)
        blocks = [to_blocks[n](grads[n]) for n in group]
        blocks = [g.reshape(N_DEV, -1, g.shape[-1]) for g in blocks]
        lands = [_own_block_filled(lax.dynamic_index_in_dim(g, me, 0, keepdims=False), me) for g in blocks]
        handle, sent = _exchange_start(blocks, lands, "scatter_start_" + group[0], True)
        grad_handles.append((group, handle))
        return sent

    loss, grad_x, dmods, dcmod, grads = _local_step(
        x[0], ctx[0], loss_target[0], mods, cmod, wts, late_weights, send_grads,
        lambda partial: lax.psum(partial, ("x", "y", "c")), start_after=order)

    def as2d(shape):
        rows = 1
        for s in shape[:-1]:
            rows *= s
        return (rows, shape[-1])

    def whole(arr, shape):
        arr = arr.reshape((-1,) + as2d(shape))
        return (arr, arr.shape[0])

    shard_shapes = {n: weights[n].shape for n in names}
    g_out, d_out, m_out, v_out = {}, {}, {}, {}

    def adamw(n, pieces, after):
        shape = shard_shapes[n]
        r2, c2 = as2d(shape)
        g, dl, nm, nv = _adamw(pieces, weights[n].reshape(r2, c2), m_in[n].reshape(r2, c2), v_in[n].reshape(r2, c2),
                               "adamw_" + n, after=after)
        g_out[n], d_out[n], m_out[n], v_out[n] = (t.reshape(shape) for t in (g, dl, nm, nv))
        return g

    small_sharded = ["norm_g", "rec_conv_w", "rec_lambda", "conf_b_pw1", "conf_conv_w", "conf_conv_b", "conf_ln_g",
                     "conf_ln_b", "conf_b_pw2"]
    pack = jnp.concatenate([_shard_cols(grads[n]).reshape(N_DEV, -1) for n in small_sharded], axis=1)
    pack = jnp.pad(pack, ((0, 0), (0, SMALL_PACK_ROWS * 128 - pack.shape[1]))).reshape(N_DEV, SMALL_PACK_ROWS, 128)
    small_handle, token = _exchange_start(
        [pack], [_own_block_filled(lax.dynamic_index_in_dim(pack, me, 0, keepdims=False), me)], "scatter_small_start",
        True, after=[grad_x])
    dm_flat = jnp.concatenate([dmods.reshape(-1), dcmod.reshape(-1), grads["rec_conv_b"].reshape(-1)])
    dm_len = dm_flat.shape[0]
    dm_flat = jnp.pad(dm_flat, (0, 128 * 128 - dm_len)).reshape(128, 128)
    dm_handle, token = _exchange_start([dm_flat], [_own_block_filled(dm_flat, me)], "gather_dmods_start", False,
                                       after=[token])

    done = token
    for group, handle in grad_handles:
        if group == ["replicated"]:
            repl_all = _exchange_wait(handle, done, "gather_replicated_wait", False)[0].reshape(N_DEV, -1)
            off = 0
            for n in repl_names:
                size = weights[n].size
                done = adamw(n, [whole(repl_all[:, off:off + size], shard_shapes[n])], [done])
                off += size
            continue
        for n, got in zip(group, _exchange_wait(handle, done, "scatter_wait_" + group[0], True)):
            done = adamw(n, [(got, N_DEV)], [done])

    dm_all = _exchange_wait(dm_handle, done, "gather_dmods_wait", False)[0].reshape(N_DEV, -1)
    dmods_all = dm_all[:, :2 * N_MOD * D].reshape(N_DEV, 2, N_MOD * D)
    dcmod_all = jnp.pad(dm_all[:, 2 * N_MOD * D:2 * N_MOD * D + 2 * D], ((0, 0), (0, (N_MOD - 2) * D)))
    g16_full = jnp.stack([jnp.concatenate([dmods_all[:, 0], dcmod_all], axis=0),
                          jnp.concatenate([dmods_all[:, 1], jnp.zeros_like(dcmod_all)], axis=0)])
    g16 = lax.dynamic_slice_in_dim(g16_full, me * ADA_SHARD, ADA_SHARD, axis=2)
    dw_ada, ds_part = _ada_backward(c16, g16, w_ada)
    ds_handle, token = _exchange_start([ds_part[0]], [_own_block_filled(ds_part[0], me)], "gather_dsilu_start", False)
    done = adamw("w_ada", [whole(dw_ada, shard_shapes["w_ada"])], [token])
    done = adamw("rec_conv_b", [whole(dm_all[:, dm_len - R:dm_len], shard_shapes["rec_conv_b"])], [done])
    db_terms = jnp.concatenate([dmods_all, jnp.stack([dcmod_all, jnp.zeros_like(dcmod_all)], axis=1)], axis=0)
    done = adamw("b_ada", [whole(db_terms, shard_shapes["b_ada"])], [done])
    pack_recv = _exchange_wait(small_handle, done, "scatter_small_wait", True)[0].reshape(N_DEV, -1)
    off = 0
    for n in small_sharded:
        size = weights[n].size
        done = adamw(n, [whole(pack_recv[:, off:off + size], shard_shapes[n])], [done])
        off += size
    ds_all = _exchange_wait(ds_handle, done, "gather_dsilu_wait", False)[0]
    adamw("c_ctx", [whole(ds_all[:, 0], shard_shapes["c_ctx"])], [])

    return (loss, grad_x[None], *[g_out[n] for n in names], *[d_out[n] for n in names],
            *[m_out[n] for n in names], *[v_out[n] for n in names])
```

```python
import functools

import jax
import jax.numpy as jnp
from jax import lax
from jax.experimental import pallas as pl
from jax.experimental.pallas import tpu as pltpu

f32 = jnp.float32
bf16 = jnp.bfloat16

N_DEV = 8
D = 1024
T_LAT = 2048
T_CTX = 256
T_ALL = T_CTX + T_LAT
R = 1280
N_BLK = 16
BLK = R // N_BLK
F = 4096
GRID_W = 64
RG_C = 8.0
EPS = 1e-6
POS_BASE = 10000.0
N_MOD = 6
ADA_SHARD = N_MOD * D // N_DEV

ADAM_LR = 0.001
ADAM_B1 = 0.9
ADAM_B2 = 0.999
ADAM_EPS = 1e-08
ADAM_WD = 0.01
ADAM_STEP = 10

VMEM_LIMIT_V7X = 56 * 1024 * 1024
HALO = 16
MESH = pl.DeviceIdType.MESH


def _cparams(*sem):
    return pltpu.CompilerParams(dimension_semantics=sem, vmem_limit_bytes=VMEM_LIMIT_V7X)


def _pick(n, cands):
    for c in cands:
        if n % c == 0:
            return c
    raise ValueError(f"no block size for {n}")


def _position():
    x, y, c = lax.axis_index("x"), lax.axis_index("y"), lax.axis_index("c")
    return x, y, c, 4 * x + 2 * y + c


def _peer(x, y, c, k):
    px = (1 - x) if (k >> 2) & 1 else x
    py = (1 - y) if (k >> 1) & 1 else y
    pc = (1 - c) if k & 1 else c
    return (px, py, pc), 4 * px + 2 * py + pc


def _exchange(arrs, name, scatter):
    n = len(arrs)

    def body(*refs):
        ins, outs = refs[:n], refs[n:2 * n]
        send_sems, recv_sems, local_sems = refs[2 * n:]
        x, y, c, me = _position()
        local = []
        for a in range(n):
            src = ins[a].at[me] if scatter else ins[a]
            cp = pltpu.make_async_copy(src, outs[a].at[me], local_sems.at[a])
            cp.start()
            local.append(cp)
        sends, recvs = [], []
        for a in range(n):
            for k in range(1, N_DEV):
                peer, peer_lin = _peer(x, y, c, k)
                src = ins[a].at[peer_lin] if scatter else ins[a]
                cp = pltpu.make_async_remote_copy(
                    src_ref=src, dst_ref=outs[a].at[me], send_sem=send_sems.at[a, k - 1],
                    recv_sem=recv_sems.at[a, k - 1], device_id=peer, device_id_type=MESH)
                cp.start()
                sends.append(cp)
                recvs.append(pltpu.make_async_remote_copy(
                    src_ref=src, dst_ref=outs[a].at[peer_lin], send_sem=send_sems.at[a, k - 1],
                    recv_sem=recv_sems.at[a, k - 1], device_id=peer, device_id_type=MESH))
        for cp in recvs:
            cp.wait_recv()
        for cp in sends:
            cp.wait_send()
        for cp in local:
            cp.wait()

    if scatter:
        out_shape = [jax.ShapeDtypeStruct(a.shape, a.dtype) for a in arrs]
    else:
        out_shape = [jax.ShapeDtypeStruct((N_DEV,) + a.shape, a.dtype) for a in arrs]
    any_spec = pl.BlockSpec(memory_space=pl.ANY)
    return pl.pallas_call(
        body, name=name, out_shape=out_shape,
        in_specs=[any_spec] * n, out_specs=[any_spec] * n,
        scratch_shapes=[pltpu.SemaphoreType.DMA((n, N_DEV - 1)), pltpu.SemaphoreType.DMA((n, N_DEV - 1)),
                        pltpu.SemaphoreType.DMA((n,))],
    )(*arrs)


def _all_gather(arrs, name):
    return _exchange(arrs, name, scatter=False)


def _lin(p):
    return 4 * p[0] + 2 * p[1] + p[2]


HBM_SPEC = pl.BlockSpec(memory_space=pltpu.HBM)
SEM_SPEC = pl.BlockSpec(memory_space=pltpu.SEMAPHORE)
DATAFLOW_EFFECT = pltpu.SideEffectType.DATAFLOW_SIDE_EFFECTING


def _split_copies(srcs, lands, send_sems, recv_sems, scatter):
    x, y, c, me = _position()
    out = []
    for a in range(len(srcs)):
        for k in range(1, N_DEV):
            peer, peer_lin = _peer(x, y, c, k)
            src = srcs[a].at[peer_lin] if scatter else srcs[a]
            mk = lambda slot: pltpu.make_async_remote_copy(
                src_ref=src, dst_ref=lands[a].at[slot], send_sem=send_sems.at[a * (N_DEV - 1) + k - 1],
                recv_sem=recv_sems.at[a * (N_DEV - 1) + k - 1], device_id=peer, device_id_type=MESH)
            out.append((mk(me), mk(peer_lin)))
    return out


def _exchange_start(srcs, lands, name, scatter, after=()):
    n = len(srcs)
    n_after = len(after)

    def body(*refs):
        srcs_r, lands_r = refs[:n], refs[n:2 * n]
        send_sems, recv_sems = refs[2 * n + n_after], refs[2 * n + n_after + 1]
        token = refs[-1]
        for outgoing, _ in _split_copies(srcs_r, lands_r, send_sems, recv_sems, scatter):
            outgoing.start()
        token[...] = jnp.zeros_like(token)

    hbm = lambda a: pltpu.HBM(a.shape, a.dtype)
    res = pl.pallas_call(
        body, name=name,
        out_shape=(pltpu.SemaphoreType.DMA((n * (N_DEV - 1),)), pltpu.SemaphoreType.DMA((n * (N_DEV - 1),)),
                   *[hbm(a) for a in srcs], *[hbm(a) for a in lands], jax.ShapeDtypeStruct((8, 128), f32)),
        in_specs=[HBM_SPEC] * (2 * n) + [pl.BlockSpec(memory_space=pl.ANY)] * n_after,
        out_specs=(SEM_SPEC, SEM_SPEC, *[HBM_SPEC] * (2 * n), pl.BlockSpec(memory_space=pltpu.VMEM)),
        input_output_aliases={i: 2 + i for i in range(2 * n)},
        compiler_params=pltpu.CompilerParams(has_side_effects=DATAFLOW_EFFECT),
    )(*[pltpu.with_memory_space_constraint(a, pltpu.HBM) for a in list(srcs) + list(lands)], *after)
    return (res[0], res[1], list(res[2:2 + n]), list(res[2 + n:2 + 2 * n])), res[-1]


def _exchange_wait(handle, after, name, scatter):
    send_sems, recv_sems, srcs, lands = handle
    n = len(srcs)

    def body(*refs):
        srcs_r, lands_r = refs[:n], refs[n:2 * n]
        send_s, recv_s = refs[2 * n], refs[2 * n + 1]
        for outgoing, incoming in _split_copies(srcs_r, lands_r, send_s, recv_s, scatter):
            outgoing.wait_send()
            incoming.wait_recv()

    hbm = lambda a: pltpu.HBM(a.shape, a.dtype)
    res = pl.pallas_call(
        body, name=name, out_shape=tuple(hbm(a) for a in list(srcs) + list(lands)),
        in_specs=[HBM_SPEC] * (2 * n) + [SEM_SPEC, SEM_SPEC, pl.BlockSpec(memory_space=pl.ANY)],
        out_specs=tuple([HBM_SPEC] * (2 * n)),
        input_output_aliases={i: i for i in range(2 * n)},
        compiler_params=pltpu.CompilerParams(has_side_effects=DATAFLOW_EFFECT),
    )(*srcs, *lands, send_sems, recv_sems, after)
    return list(res[n:])


def _split_call(body, name, hbm_ins, kept, in_sems, n_new_sems, after, with_token):
    n_in, n_sem = len(hbm_ins), len(in_sems)
    out_shape, out_specs = [], []
    if n_new_sems:
        out_shape += [pltpu.SemaphoreType.DMA((n_new_sems,))] * 2
        out_specs += [SEM_SPEC] * 2
    first_kept = len(out_shape)
    out_shape += [pltpu.HBM(hbm_ins[i].shape, hbm_ins[i].dtype) for i in kept]
    out_specs += [HBM_SPEC] * len(kept)
    if with_token:
        out_shape.append(jax.ShapeDtypeStruct((8, 128), f32))
        out_specs.append(pl.BlockSpec(memory_space=pltpu.VMEM))

    def wrapped(*refs):
        outs = refs[n_in + n_sem + len(after):]
        body(refs[:n_in], refs[n_in:n_in + n_sem], outs[:2] if n_new_sems else ())
        if with_token:
            outs[-1][...] = jnp.zeros_like(outs[-1])

    return pl.pallas_call(
        wrapped, name=name, out_shape=tuple(out_shape),
        in_specs=[HBM_SPEC] * n_in + [SEM_SPEC] * n_sem + [pl.BlockSpec(memory_space=pl.ANY)] * len(after),
        out_specs=tuple(out_specs), input_output_aliases={i: first_kept + j for j, i in enumerate(kept)},
        compiler_params=pltpu.CompilerParams(has_side_effects=DATAFLOW_EFFECT),
    )(*[pltpu.with_memory_space_constraint(a, pltpu.HBM) for a in hbm_ins], *in_sems, *after)


def _rcopy(src, dst, sems, k, to):
    return pltpu.make_async_remote_copy(src_ref=src, dst_ref=dst, send_sem=sems[0].at[k], recv_sem=sems[1].at[k],
                                        device_id=to, device_id_type=MESH)


def _gather2_start(shards, lands, name, after):
    n = len(shards)

    def body(ins, sems_in, sems_out):
        x, y, c, me = _position()
        for a in range(n):
            for k, to in enumerate(((x, y, 1 - c), (1 - x, y, c), (x, 1 - y, c))):
                _rcopy(ins[a], ins[n + a].at[me], sems_out, 3 * a + k, to).start()

    res = _split_call(body, name, list(shards) + list(lands), range(2 * n), (), 3 * n, after, True)
    return (res[0], res[1], list(res[2:2 + n]), list(res[2 + n:2 + 2 * n])), res[-1]


def _gather2_forward1(handle, after, name):
    send_sems, recv_sems, srcs, lands = handle
    n = len(srcs)

    def body(ins, sems_in, sems_out):
        x, y, c, me = _position()
        sib, xn, yn = (x, y, 1 - c), (1 - x, y, c), (x, 1 - y, c)
        for a in range(n):
            for k, peer in enumerate((sib, xn, yn)):
                _rcopy(ins[a], ins[n + a].at[me], sems_in, 3 * a + k, peer).wait_send()
                _rcopy(ins[a], ins[n + a].at[_lin(peer)], sems_in, 3 * a + k, peer).wait_recv()
        for a in range(n):
            land = ins[n + a]
            _rcopy(land.at[_lin(xn)], land.at[_lin(xn)], sems_out, 3 * a, sib).start()
            _rcopy(land.at[_lin(yn)], land.at[_lin(yn)], sems_out, 3 * a + 1, sib).start()

            @pl.when(c == 0)
            def _():
                _rcopy(land.at[_lin(xn)], land.at[_lin(xn)], sems_out, 3 * a + 2, yn).start()

            @pl.when(c == 1)
            def _():
                _rcopy(land.at[_lin(yn)], land.at[_lin(yn)], sems_out, 3 * a + 2, xn).start()

    res = _split_call(body, name, list(srcs) + list(lands), range(n, 2 * n), (send_sems, recv_sems), 3 * n, [after], False)
    return (res[0], res[1], list(res[2:]))


def _gather2_forward2(handle, after, name):
    send_sems, recv_sems, lands = handle
    n = len(lands)

    def body(ins, sems_in, sems_out):
        x, y, c, me = _position()
        sib, dg = (x, y, 1 - c), _lin((1 - x, 1 - y, c))
        for a in range(n):
            for k, slot in enumerate((_lin((1 - x, y, 1 - c)), _lin((x, 1 - y, 1 - c)), dg)):
                done = _rcopy(ins[a].at[slot], ins[a].at[slot], sems_in, 3 * a + k, sib)
                done.wait_send()
                done.wait_recv()
        for a in range(n):
            _rcopy(ins[a].at[dg], ins[a].at[dg], sems_out, a, sib).start()

    res = _split_call(body, name, list(lands), range(n), (send_sems, recv_sems), n, [after], False)
    return (res[0], res[1], list(res[2:]))


def _gather2_wait(handle, after, name):
    send_sems, recv_sems, lands = handle
    n = len(lands)

    def body(ins, sems_in, sems_out):
        x, y, c, me = _position()
        slot = _lin((1 - x, 1 - y, 1 - c))
        for a in range(n):
            done = _rcopy(ins[a].at[slot], ins[a].at[slot], sems_in, a, (x, y, 1 - c))
            done.wait_send()
            done.wait_recv()

    return list(_split_call(body, name, list(lands), range(n), (send_sems, recv_sems), 0, [after], False))


def _own_block_filled(block, me):
    land = lax.empty((N_DEV,) + block.shape, block.dtype)
    return lax.dynamic_update_index_in_dim(land, block, me, 0)


def _staged_copy(src, dst, buf, in_sems, out_sems, rows, chunk):
    n = rows // chunk

    def rd(i):
        return pltpu.make_async_copy(src.at[pl.ds(i * chunk, chunk)], buf.at[i % 2], in_sems.at[i % 2])

    def wr(i):
        return pltpu.make_async_copy(buf.at[i % 2], dst.at[pl.ds(i * chunk, chunk)], out_sems.at[i % 2])

    rd(0).start()
    for i in range(n):
        if i + 1 < n:
            if i >= 1:
                wr(i - 1).wait()
            rd(i + 1).start()
        rd(i).wait()
        wr(i).start()
    for i in range(max(n - 2, 0), n):
        wr(i).wait()


def _all_gather_2level(shards, name):
    n = len(shards)
    chunks = [_pick(s.shape[0], (416, 512, 256, 160, 128, 64, 16)) for s in shards]

    def body(*refs):
        ins, outs = refs[:n], refs[n:2 * n]
        send_sems, recv_sems, in_sems, out_sems = refs[2 * n:2 * n + 4]
        bufs = refs[2 * n + 4:]
        x, y, c, me = _position()
        sib, xn, yn, dg = (x, y, 1 - c), (1 - x, y, c), (x, 1 - y, c), (1 - x, 1 - y, c)

        def cp(a, k, src, slot, to):
            return pltpu.make_async_remote_copy(src_ref=src, dst_ref=outs[a].at[slot], send_sem=send_sems.at[a, k],
                                                recv_sem=recv_sems.at[a, k], device_id=to, device_id_type=MESH)

        for a in range(n):
            for k, to in ((0, sib), (1, xn), (2, yn)):
                cp(a, k, ins[a], me, to).start()
        for a in range(n):
            cp(a, 1, ins[a], _lin(xn), xn).wait_recv()
            cp(a, 3, outs[a].at[_lin(xn)], _lin(xn), sib).start()

            @pl.when(c == 0)
            def _():
                cp(a, 5, outs[a].at[_lin(xn)], _lin(xn), yn).start()

            cp(a, 2, ins[a], _lin(yn), yn).wait_recv()
            cp(a, 4, outs[a].at[_lin(yn)], _lin(yn), sib).start()

            @pl.when(c == 1)
            def _():
                cp(a, 5, outs[a].at[_lin(yn)], _lin(yn), xn).start()

        for a in range(n):
            cp(a, 5, ins[a], _lin(dg), xn).wait_recv()
            cp(a, 6, outs[a].at[_lin(dg)], _lin(dg), sib).start()
        for a in range(n):
            _staged_copy(ins[a], outs[a].at[me], bufs[a], in_sems.at[a], out_sems.at[a], shards[a].shape[0], chunks[a])
        for a in range(n):
            for k, origin in ((0, sib), (3, (1 - x, y, 1 - c)), (4, (x, 1 - y, 1 - c)), (6, (1 - x, 1 - y, 1 - c))):
                cp(a, k, ins[a], _lin(origin), sib).wait_recv()
            for k in range(7):
                cp(a, k, ins[a], me, sib).wait_send()

    any_spec = pl.BlockSpec(memory_space=pl.ANY)
    return pl.pallas_call(
        body, name=name, out_shape=[jax.ShapeDtypeStruct((N_DEV,) + s.shape, s.dtype) for s in shards],
        in_specs=[any_spec] * n, out_specs=[any_spec] * n,
        scratch_shapes=[pltpu.SemaphoreType.DMA((n, 7)), pltpu.SemaphoreType.DMA((n, 7)),
                        pltpu.SemaphoreType.DMA((n, 2)), pltpu.SemaphoreType.DMA((n, 2))]
        + [pltpu.VMEM((2, ch, s.shape[1]), s.dtype) for ch, s in zip(chunks, shards)],
    )(*shards)


def _plane_pos(x, y, q):
    return ((1 - x) if q & 2 else x, (1 - y) if q & 1 else y)


ANY_SPEC = pl.BlockSpec(memory_space=pl.ANY)


def _mm(a, b, name, ta=False, tb=False, out_dtype=f32, after=()):
    if ta:
        k_dim, m_dim = a.shape
    else:
        m_dim, k_dim = a.shape
    if tb:
        n_dim, k2 = b.shape
    else:
        k2, n_dim = b.shape
    assert k_dim == k2, (a.shape, b.shape)
    assert a.dtype == bf16 and b.dtype == bf16
    bm = _pick(m_dim, (512, 768, 640, 256, 128))
    bn = _pick(n_dim, (512, 640, 256, 128))
    bk = k_dim if k_dim <= 2560 else _pick(k_dim, (1024, 1280, 768, 512))
    nk = k_dim // bk
    a_spec = (pl.BlockSpec((bk, bm), lambda i, j, k: (k, i)) if ta
              else pl.BlockSpec((bm, bk), lambda i, j, k: (i, k)))
    b_spec = (pl.BlockSpec((bn, bk), lambda i, j, k: (j, k)) if tb
              else pl.BlockSpec((bk, bn), lambda i, j, k: (k, j)))
    dims = (((0 if ta else 1,), (1 if tb else 0,)), ((), ()))

    n_after = len(after)

    def body_single(a_ref, b_ref, *rest):
        o_ref = rest[n_after]
        o_ref[...] = lax.dot_general(a_ref[...], b_ref[...], dims, preferred_element_type=f32).astype(o_ref.dtype)

    def body(a_ref, b_ref, *rest):
        o_ref, acc_ref = rest[n_after:]
        k = pl.program_id(2)

        @pl.when(k == 0)
        def _():
            acc_ref[...] = jnp.zeros_like(acc_ref)

        acc_ref[...] += lax.dot_general(a_ref[...], b_ref[...], dims, preferred_element_type=f32)

        @pl.when(k == nk - 1)
        def _():
            o_ref[...] = acc_ref[...].astype(o_ref.dtype)

    return pl.pallas_call(
        body_single if nk == 1 else body, name=name, out_shape=jax.ShapeDtypeStruct((m_dim, n_dim), out_dtype),
        grid=(m_dim // bm, n_dim // bn, nk), in_specs=[a_spec, b_spec] + [ANY_SPEC] * n_after,
        out_specs=pl.BlockSpec((bm, bn), lambda i, j, k: (i, j)),
        scratch_shapes=[] if nk == 1 else [pltpu.VMEM((bm, bn), f32)],
        compiler_params=_cparams("parallel", "parallel", "arbitrary"),
    )(a, b, *after)


def _rin(arr, width=None, cb=0, roff=0):
    return (arr, arr.shape[1] if width is None else width, cb, roff)


def _rowcall(fn, name, rows, tm, row_ins, par_ins, row_outs, acc_outs=(), after=()):
    nr, npar, nro, n_after = len(row_ins), len(par_ins), len(row_outs), len(after)
    in_specs, args = [], []
    for arr, width, cb, roff in row_ins:
        if roff >= 0:
            imap = lambda i, cb=cb, roff=roff: (i + roff, cb)
        else:
            imap = lambda i, cb=cb, roff=roff: (jnp.maximum(i + roff, 0), cb)
        in_specs.append(pl.BlockSpec((tm, width), imap))
        args.append(arr)
    for p in par_ins:
        in_specs.append(pl.BlockSpec(p.shape, lambda i: (0, 0)))
        args.append(p)
    out_shape, out_specs = [], []
    for width, dt in row_outs:
        out_shape.append(jax.ShapeDtypeStruct((rows, width), dt))
        out_specs.append(pl.BlockSpec((tm, width), lambda i: (i, 0)))
    for p, width in acc_outs:
        out_shape.append(jax.ShapeDtypeStruct((p, width), f32))
        out_specs.append(pl.BlockSpec((p, width), lambda i: (0, 0)))

    def body(*refs):
        i = pl.program_id(0)
        res = fn(i, *[r[...] for r in refs[:nr + npar]])
        outs = refs[nr + npar + n_after:]
        for o, v in zip(outs[:nro], res[:nro]):
            o[...] = v.astype(o.dtype)
        if acc_outs:
            @pl.when(i == 0)
            def _():
                for o in outs[nro:]:
                    o[...] = jnp.zeros_like(o)

            for o, v in zip(outs[nro:], res[nro:]):
                o[...] += v

    return pl.pallas_call(
        body, name=name, out_shape=out_shape, grid=(rows // tm,), in_specs=in_specs + [ANY_SPEC] * n_after,
        out_specs=out_specs, compiler_params=_cparams("arbitrary"),
    )(*args, *after)


def _rms(x, g):
    return x * lax.rsqrt(jnp.mean(x * x, axis=-1, keepdims=True) + EPS) * g


def _normmod(x, g, sc, sh):
    return _rms(x, g) * (1.0 + sc) + sh


def _gelu(x):
    return 0.5 * x * (1.0 + jnp.tanh(0.7978845608028654 * (x + 0.044715 * (x * x * x))))


def _sigmoid(x):
    return 0.5 * (jnp.tanh(0.5 * x) + 1.0)


def _coeff_parts(pre_a, pre_x, ba, bx, lam):
    r = _sigmoid(pre_a + ba)
    ig = _sigmoid(pre_x + bx)
    nl = -lam
    sp = jnp.maximum(nl, 0.0) + jnp.log(1.0 + jnp.exp(-jnp.abs(nl)))
    la = -RG_C * r * sp
    a = jnp.exp(la)
    one_minus_a2 = -jnp.tanh(la) * (a * a + 1.0)
    inv_m = lax.rsqrt(one_minus_a2)
    return r, ig, sp, a, one_minus_a2 * inv_m, inv_m


def _coeff(pre_a, pre_x, u, ba, bx, lam):
    _, ig, _, a, m, _ = _coeff_parts(pre_a, pre_x, ba, bx, lam)
    return a, m * (ig * u)


def _coeff_bwd(pre_a, pre_x, u, ba, bx, lam, da, db):
    r, ig, sp, a, m, inv_m = _coeff_parts(pre_a, pre_x, ba, bx, lam)
    dbu = db * u
    dig = dbu * m
    dm = dbu * ig
    dla = a * (da - dm * a * inv_m)
    dpa = dla * (-RG_C * sp) * (r * (1.0 - r))
    dpx = dig * (ig * (1.0 - ig))
    dsp = jnp.sum(dla * (-RG_C * r), axis=0, keepdims=True)
    dlam = -dsp * _sigmoid(-lam)
    return (dpa, dpx, db * m * ig, jnp.sum(dpa, axis=0, keepdims=True), jnp.sum(dpx, axis=0, keepdims=True), dlam)


SCAN_CHUNK = 256


def _scan_call(a, v, chunk_of, reverse, name, backward, after=()):
    rows, width = a.shape
    n_out = 1 if backward else 2
    nt = SCAN_CHUNK // 8

    def body(a_ref, v_ref, *rest):
        outs, state_ref = rest[len(after):-1], rest[-1]

        @pl.when(pl.program_id(0) == 0)
        def _():
            state_ref[...] = jnp.zeros_like(state_ref)

        rid = lax.broadcasted_iota(jnp.int32, (8, width), 0)
        last_row = 0 if reverse else 7

        def shift(x, s, fill):
            rolled = pltpu.roll(x, (8 - s) if reverse else s, axis=0)
            return jnp.where((rid >= 8 - s) if reverse else (rid < s), fill, rolled)

        def tile(j, st):
            t0 = pl.multiple_of((nt - 1 - j if reverse else j) * 8, 8)
            at = a_ref[pl.ds(t0, 8), :]
            coef = shift(at, 1, 1.0) if backward else at
            acc = v_ref[pl.ds(t0, 8), :]
            for s in (1, 2, 4):
                acc = coef * shift(acc, s, 0.0) + acc
                coef = coef * shift(coef, s, 1.0)
            out = coef * st + acc
            outs[0][pl.ds(t0, 8), :] = out
            last = out[last_row:last_row + 1]
            if backward:
                return at[last_row:last_row + 1] * last
            outs[1][pl.ds(t0, 8), :] = shift(out, 1, st)
            return last

        state_ref[0:1, :] = lax.fori_loop(0, nt, tile, state_ref[0:1, :])

    spec = pl.BlockSpec((SCAN_CHUNK, width), lambda t: (chunk_of(t), 0))
    return pl.pallas_call(
        body, name=name, out_shape=[jax.ShapeDtypeStruct((rows, width), f32)] * n_out,
        grid=(rows // SCAN_CHUNK,), in_specs=[spec, spec] + [ANY_SPEC] * len(after), out_specs=[spec] * n_out,
        scratch_shapes=[pltpu.VMEM((8, width), f32)],
        compiler_params=_cparams("arbitrary"),
    )(a, v, *after)


CONV_CHUNK = 256


def _fill_padded(pad_ref, src_ref, start, n):
    cb = pad_ref.shape[1]
    pad_ref[pl.ds(0, HALO), :] = jnp.zeros((HALO, cb), f32)
    pad_ref[pl.ds(HALO, n), :] = src_ref[pl.ds(start, n), :].astype(f32)
    pad_ref[pl.ds(HALO + n, HALO), :] = jnp.zeros((HALO, cb), f32)


def _dwconv_fwd(x, x_cb0, w, b, taps, pad_left, segments, cb, name, emit_bf16):
    rows = x.shape[0]
    width = w.shape[1]

    def body(x_ref, w_ref, b_ref, *rest):
        outs, xp = rest[:-1], rest[-1]
        for start, n in segments:
            _fill_padded(xp, x_ref, start, n)
            for c0 in range(0, n, CONV_CHUNK):
                acc = jnp.zeros((CONV_CHUNK, cb), f32) + b_ref[...]
                for k in range(taps):
                    acc = acc + w_ref[k:k + 1, :] * xp[pl.ds(HALO + c0 + k - pad_left, CONV_CHUNK), :]
                for o in outs:
                    o[pl.ds(start + c0, CONV_CHUNK), :] = acc.astype(o.dtype)

    out_dtypes = [f32, bf16] if emit_bf16 else [f32]
    return pl.pallas_call(
        body, name=name, out_shape=[jax.ShapeDtypeStruct((rows, width), dt) for dt in out_dtypes],
        grid=(width // cb,),
        in_specs=[pl.BlockSpec((rows, cb), lambda j: (0, j + x_cb0)), pl.BlockSpec((taps, cb), lambda j: (0, j)),
                  pl.BlockSpec((1, cb), lambda j: (0, j))],
        out_specs=[pl.BlockSpec((rows, cb), lambda j: (0, j))] * len(out_dtypes),
        scratch_shapes=[pltpu.VMEM((rows + 2 * HALO, cb), f32)],
        compiler_params=_cparams("parallel"),
    )(x, w, b)


def _dwconv_bwd(douts, x, x_cb0, w, taps, pad_left, segments, cb, name, dx_dtype):
    rows = x.shape[0]
    width = w.shape[1]
    nd = len(douts)

    def body(*refs):
        d_refs, x_ref, w_ref = refs[:nd], refs[nd], refs[nd + 1]
        dx_ref, dw_ref, db_ref, dp, dsum = refs[nd + 2:]
        dw_ref[...] = jnp.zeros_like(dw_ref)
        db_ref[...] = jnp.zeros_like(db_ref)
        if nd > 1:
            total = d_refs[0][...]
            for r in d_refs[1:]:
                total = total + r[...]
            dsum[...] = total
            d_ref = dsum
        else:
            d_ref = d_refs[0]
        for start, n in segments:
            _fill_padded(dp, d_ref, start, n)
            for c0 in range(0, n, CONV_CHUNK):
                db_ref[...] += jnp.sum(dp[pl.ds(HALO + c0, CONV_CHUNK), :], axis=0, keepdims=True)
                xchunk = x_ref[pl.ds(start + c0, CONV_CHUNK), :].astype(f32)
                acc = jnp.zeros((CONV_CHUNK, cb), f32)
                for k in range(taps):
                    shifted = dp[pl.ds(HALO + c0 + pad_left - k, CONV_CHUNK), :]
                    acc = acc + w_ref[k:k + 1, :] * shifted
                    dw_ref[k:k + 1, :] += jnp.sum(shifted * xchunk, axis=0, keepdims=True)
                dx_ref[pl.ds(start + c0, CONV_CHUNK), :] = acc.astype(dx_ref.dtype)

    dspec = pl.BlockSpec((rows, cb), lambda j: (0, j))
    return pl.pallas_call(
        body, name=name,
        out_shape=[jax.ShapeDtypeStruct((rows, width), dx_dtype), jax.ShapeDtypeStruct((taps, width), f32),
                   jax.ShapeDtypeStruct((1, width), f32)],
        grid=(width // cb,),
        in_specs=[dspec] * nd + [pl.BlockSpec((rows, cb), lambda j: (0, j + x_cb0)),
                                 pl.BlockSpec((taps, cb), lambda j: (0, j))],
        out_specs=[dspec, pl.BlockSpec((taps, cb), lambda j: (0, j)), pl.BlockSpec((1, cb), lambda j: (0, j))],
        scratch_shapes=[pltpu.VMEM((rows + 2 * HALO, cb), f32), pltpu.VMEM((rows, cb), f32)],
        compiler_params=_cparams("parallel"),
    )(*douts, x, w)


def _ada_forward(c16, w_ada, b_loc):
    def body(c_ref, w_ref, b_ref, o_ref):
        cv = c_ref[...]
        s = (cv * _sigmoid(cv)).astype(bf16)
        o_ref[0] = jnp.dot(s, w_ref[0].astype(bf16), preferred_element_type=f32) + b_ref[0]

    return pl.pallas_call(
        body, name="ada_forward", out_shape=jax.ShapeDtypeStruct((2, 16, ADA_SHARD), f32), grid=(2,),
        in_specs=[pl.BlockSpec((16, D), lambda l: (0, 0)), pl.BlockSpec((1, D, ADA_SHARD), lambda l: (l, 0, 0)),
                  pl.BlockSpec((1, 1, ADA_SHARD), lambda l: (l, 0, 0))],
        out_specs=pl.BlockSpec((1, 16, ADA_SHARD), lambda l: (l, 0, 0)),
        compiler_params=_cparams("parallel"),
    )(c16, w_ada, b_loc)


def _ada_backward(c16, g16, w_ada):
    def body(c_ref, g_ref, w_ref, dw_ref, ds_ref):
        cv = c_ref[...]
        s = (cv * _sigmoid(cv)).astype(bf16)
        g = g_ref[0].astype(bf16)
        dw_ref[0] = lax.dot_general(s, g, (((0,), (0,)), ((), ())), preferred_element_type=f32)
        ds = lax.dot_general(g, w_ref[0].astype(bf16), (((1,), (1,)), ((), ())), preferred_element_type=f32)
        cc = cv[8:9]
        sg = _sigmoid(cc)
        dsilu = sg * (1.0 + cc * (1.0 - sg))
        ds_ref[0] = jnp.zeros((8, D), f32) + jnp.sum(ds[8:16], axis=0, keepdims=True) * dsilu

    return pl.pallas_call(
        body, name="ada_backward",
        out_shape=[jax.ShapeDtypeStruct((2, D, ADA_SHARD), f32), jax.ShapeDtypeStruct((2, 8, D), f32)], grid=(2,),
        in_specs=[pl.BlockSpec((16, D), lambda l: (0, 0)), pl.BlockSpec((1, 16, ADA_SHARD), lambda l: (l, 0, 0)),
                  pl.BlockSpec((1, D, ADA_SHARD), lambda l: (l, 0, 0))],
        out_specs=[pl.BlockSpec((1, D, ADA_SHARD), lambda l: (l, 0, 0)), pl.BlockSpec((1, 8, D), lambda l: (l, 0, 0))],
        compiler_params=_cparams("parallel"),
    )(c16, g16, w_ada)


def _adamw(pieces, w, m, v, name, after=()):
    rows, cols = w.shape
    n_arr, n_after = len(pieces), len(after)
    counts = [cnt for _, cnt in pieces]
    pieces = [p for p, _ in pieces]
    tm = 256 if (rows % 256 == 0 and rows > 256) else rows

    def body(*refs):
        p_refs = refs[:n_arr]
        w_ref, m_ref, v_ref = refs[n_arr:n_arr + 3]
        g_ref, d_ref, nm_ref, nv_ref = refs[n_arr + 3 + n_after:]
        g = None
        for p_ref in p_refs:
            for j in range(p_ref.shape[0]):
                term = p_ref[j].astype(f32)
                g = term if g is None else g + term
        m2 = ADAM_B1 * m_ref[...] + (1.0 - ADAM_B1) * g
        v2 = ADAM_B2 * v_ref[...] + (1.0 - ADAM_B2) * (g * g)
        m_hat = m2 / (1.0 - ADAM_B1 ** ADAM_STEP)
        v_hat = v2 / (1.0 - ADAM_B2 ** ADAM_STEP)
        g_ref[...] = g
        d_ref[...] = -ADAM_LR * (m_hat / (jnp.sqrt(v_hat) + ADAM_EPS) + ADAM_WD * w_ref[...])
        nm_ref[...] = m2
        nv_ref[...] = v2

    spec = pl.BlockSpec((tm, cols), lambda i: (i, 0))
    return pl.pallas_call(
        body, name=name, out_shape=[jax.ShapeDtypeStruct((rows, cols), f32)] * 4, grid=(rows // tm,),
        in_specs=[pl.BlockSpec((cnt, tm, cols), lambda i: (0, i, 0)) for cnt in counts] + [spec, spec, spec]
        + [ANY_SPEC] * n_after,
        out_specs=[spec] * 4, compiler_params=_cparams("parallel"),
    )(*pieces, w, m, v, *after)


MLP_TM = 256
FB = F // N_DEV


def _stack_rows(vals, n):
    cols = vals[0].shape[1]
    rid = lax.broadcasted_iota(jnp.int32, (n, cols), 0)
    out = jnp.zeros((n, cols), f32)
    for k, v in enumerate(vals):
        out = jnp.where(rid == k, v, out)
    return out


N_MLP_PARAMS = 9


class _ParamRows:
    def __init__(self, ref):
        self.ref = ref

    def __getitem__(self, sl):
        return self.ref[8 * sl.start:8 * sl.start + 1, :]


def _resident(shape, imap):
    return pl.BlockSpec(shape, imap, pipeline_mode=pl.Buffered(1))


def _mlp_forward(xa, xa_roff, out_prev, par, w_in, w_out, layer, name):
    def body(xa_ref, op_ref, par_ref, win_ref, wout_ref, x1_ref, h_ref, r_ref, mo_ref, x2_ref, hn_ref):
        p = _ParamRows(par_ref)
        x1 = xa_ref[...] + p[0:1] * (op_ref[...] + p[1:2])
        h = _normmod(x1, p[2:3], p[3:4], p[4:5]).astype(bf16)
        x1_ref[...] = x1
        h_ref[...] = h
        mo = jnp.zeros((MLP_TM, D), f32)
        for j in range(N_DEV):
            r = jnp.maximum(jnp.dot(h, win_ref[j], preferred_element_type=f32), 0.0)
            r_ref[:, j * FB:(j + 1) * FB] = r.astype(bf16)
            mo = mo + jnp.dot((r * r).astype(bf16), wout_ref[j], preferred_element_type=f32)
        mo_ref[...] = mo.astype(bf16)
        x2 = x1 + p[5:6] * mo
        x2_ref[...] = x2
        hn_ref[...] = _normmod(x2, p[6:7], p[7:8], p[8:9]).astype(bf16)

    row = lambda width: pl.BlockSpec((MLP_TM, width), lambda i: (i, 0))
    return pl.pallas_call(
        body, name=name, grid=(T_LAT // MLP_TM,),
        out_shape=[jax.ShapeDtypeStruct((T_LAT, D), f32), jax.ShapeDtypeStruct((T_LAT, D), bf16),
                   jax.ShapeDtypeStruct((T_LAT, F), bf16), jax.ShapeDtypeStruct((T_LAT, D), bf16),
                   jax.ShapeDtypeStruct((T_LAT, D), f32), jax.ShapeDtypeStruct((T_LAT, D), bf16)],
        in_specs=[pl.BlockSpec((MLP_TM, D), lambda i: (i + xa_roff, 0)), row(D), pl.BlockSpec((8 * N_MLP_PARAMS, D), lambda i: (0, 0)),
                  _resident((N_DEV, None, D, FB), lambda i: (0, layer, 0, 0)),
                  _resident((N_DEV, None, FB, D), lambda i: (0, layer, 0, 0))],
        out_specs=[row(D), row(D), row(F), row(D), row(D), row(D)],
        compiler_params=_cparams("parallel"),
    )(xa, out_prev, par, w_in, w_out)


def _mlp_backward(dx2, x1, r, mo, out_prev, par, w_in, w_out, layer, name, after=()):
    nt = (((1,), (1,)), ((), ()))

    n_after = len(after)

    def body(dx2_ref, x1_ref, r_ref, mo_ref, op_ref, par_ref, win_ref, wout_ref, *rest):
        dx1_ref, dop_ref, dmo_ref, dhid_ref, acc_ref = rest[n_after:]
        p = _ParamRows(par_ref)
        dx2v = dx2_ref[...]
        dmo = (p[5:6] * dx2v).astype(bf16)
        dmo_ref[...] = dmo
        dh = jnp.zeros((MLP_TM, D), f32)
        mo = mo_ref[...].astype(f32)
        for j in range(N_DEV):
            rf = r_ref[:, j * FB:(j + 1) * FB].astype(f32)
            dact = lax.dot_general(dmo, wout_ref[j], nt, preferred_element_type=f32)
            dhid = (dact * (2.0 * rf)).astype(bf16)
            dhid_ref[:, j * FB:(j + 1) * FB] = dhid
            dh = dh + lax.dot_general(dhid, win_ref[j], nt, preferred_element_type=f32)
        x1 = x1_ref[...]
        _, vjp = jax.vjp(_normmod, x1, p[2:3], p[3:4], p[4:5])
        dx, dng, dsc, dsh = vjp(dh)
        dx1 = dx2v + dx
        dx1_ref[...] = dx1
        dop_ref[...] = (p[0:1] * dx1).astype(bf16)
        sums = _stack_rows([jnp.sum(dx1 * (op_ref[...] + p[1:2]), axis=0, keepdims=True),
                            p[0:1] * jnp.sum(dx1, axis=0, keepdims=True), dng, dsc, dsh,
                            jnp.sum(dx2v * mo, axis=0, keepdims=True)], 8)

        @pl.when(pl.program_id(0) == 0)
        def _():
            acc_ref[...] = jnp.zeros_like(acc_ref)

        acc_ref[...] += sums

    row = lambda width: pl.BlockSpec((MLP_TM, width), lambda i: (i, 0))
    return pl.pallas_call(
        body, name=name, grid=(T_LAT // MLP_TM,),
        out_shape=[jax.ShapeDtypeStruct((T_LAT, D), f32), jax.ShapeDtypeStruct((T_LAT, D), bf16),
                   jax.ShapeDtypeStruct((T_LAT, D), bf16), jax.ShapeDtypeStruct((T_LAT, F), bf16),
                   jax.ShapeDtypeStruct((8, D), f32)],
        in_specs=[row(D), row(D), row(F), row(D), row(D), pl.BlockSpec((8 * N_MLP_PARAMS, D), lambda i: (0, 0)),
                  _resident((N_DEV, None, D, FB), lambda i: (0, layer, 0, 0)),
                  _resident((N_DEV, None, FB, D), lambda i: (0, layer, 0, 0))] + [ANY_SPEC] * n_after,
        out_specs=[row(D), row(D), row(D), row(F), pl.BlockSpec((8, D), lambda i: (0, 0))],
        compiler_params=_cparams("arbitrary"),
    )(dx2, x1, r, mo, out_prev, par, w_in, w_out, *after)


def _mlp_weight_grads(h, dhid, r, dmo, layer, other, tag):
    tn = (((0,), (0,)), ((), ()))

    def body_in(h_ref, dhid_ref, *rest):
        rest[-1][...] = lax.dot_general(h_ref[...], dhid_ref[...], tn, preferred_element_type=f32).astype(bf16)

    def body_out(r_ref, dmo_ref, *rest):
        rf = r_ref[...].astype(f32)
        rest[-1][...] = lax.dot_general((rf * rf).astype(bf16), dmo_ref[...], tn,
                                        preferred_element_type=f32).astype(bf16)

    def call(body, name, operands, specs, block, prev):
        extra = [] if prev is None else [prev]
        return pl.pallas_call(
            body, name=name, grid=(N_DEV,), out_shape=jax.ShapeDtypeStruct((N_DEV, 2) + block, bf16),
            in_specs=specs + [pl.BlockSpec(memory_space=pl.ANY)] * len(extra),
            out_specs=pl.BlockSpec((None, None) + block, lambda j: (j, layer, 0, 0)),
            input_output_aliases={} if prev is None else {2: 0},
            compiler_params=_cparams("parallel"),
        )(*operands, *extra)

    dw_in = call(body_in, tag + "_mlp_in_dw", [h, dhid],
                 [_resident((T_LAT, D), lambda j: (0, 0)), pl.BlockSpec((T_LAT, FB), lambda j: (0, j))], (D, FB),
                 None if other is None else other[0])
    dw_out = call(body_out, tag + "_mlp_out_dw", [r, dmo],
                  [pl.BlockSpec((T_LAT, FB), lambda j: (0, j)), _resident((T_LAT, D), lambda j: (0, 0))], (FB, D),
                  None if other is None else other[1])
    return dw_in, dw_out


def _pos_embed():
    n_rows = T_LAT // GRID_W
    q = D // 4
    omega = 1.0 / (POS_BASE ** (jnp.arange(q, dtype=f32) / q))
    er = jnp.arange(n_rows, dtype=jnp.int32).astype(f32)[:, None] * omega[None, :]
    ec = jnp.arange(GRID_W, dtype=jnp.int32).astype(f32)[:, None] * omega[None, :]
    by_row = jnp.concatenate([jnp.sin(er), jnp.cos(er)], axis=-1)[:, None, :]
    by_col = jnp.concatenate([jnp.sin(ec), jnp.cos(ec)], axis=-1)[None, :, :]
    full = jnp.concatenate([jnp.broadcast_to(by_row, (n_rows, GRID_W, D // 2)),
                            jnp.broadcast_to(by_col, (n_rows, GRID_W, D // 2))], axis=-1)
    return full.reshape(T_LAT, D)


HALF = R // 2
BLK_PER_HALF = N_BLK // 2
N_PARTS = 4


def _gate_matrix(w_a, w_x):
    eye = jnp.eye(BLK_PER_HALF, dtype=bf16)
    cols = []
    for h in range(2):
        for d in range(2):
            for w in (w_a, w_x):
                blocks = w[d, BLK_PER_HALF * h:BLK_PER_HALF * (h + 1)].astype(bf16)
                cols.append(jnp.einsum("hij,hg->higj", blocks, eye).reshape(HALF, HALF))
    return jnp.concatenate(cols, axis=1)


def _gate_blocks(dwg, part):
    out = []
    for h in range(2):
        blk = dwg[:, (N_PARTS * h + part) * HALF:(N_PARTS * h + part + 1) * HALF]
        blk = blk.reshape(BLK_PER_HALF, BLK, BLK_PER_HALF, BLK)
        out.append(jnp.moveaxis(jnp.diagonal(blk, axis1=0, axis2=2), -1, 0))
    return jnp.concatenate(out, axis=0)


GATE_BM = 768


def _gates_dx(dpre, wg, after=()):
    rows = dpre.shape[0]
    n_after = len(after)

    def body(d_ref, w_ref, *rest):
        rest[n_after][...] = lax.dot_general(d_ref[...], w_ref[...], (((1,), (1,)), ((), ())),
                                             preferred_element_type=f32)

    return pl.pallas_call(
        body, name="l0_gates_dx", grid=(rows // GATE_BM, 2), out_shape=jax.ShapeDtypeStruct((rows, R), f32),
        in_specs=[pl.BlockSpec((GATE_BM, N_PARTS * HALF), lambda i, h: (i, h)),
                  pl.BlockSpec((HALF, N_PARTS * HALF), lambda i, h: (0, h))] + [ANY_SPEC] * n_after,
        out_specs=pl.BlockSpec((GATE_BM, HALF), lambda i, h: (i, h)),
        compiler_params=_cparams("parallel", "parallel"),
    )(dpre, wg, *after)


COEFF_TM = 256


def _dir_params(d, *params):
    specs = [pl.BlockSpec((None, 1, HALF), lambda h, i: (d, 0, h))] * len(params)
    return specs, [p.reshape(2, 1, R) for p in params]


def _gates_coeff_fwd(ub, u, wg, ba, bx, lam, d):
    rows = u.shape[0]

    def body(ub_ref, u_ref, w_ref, ba_ref, bx_ref, lam_ref, a_ref, b_ref):
        pre = jnp.dot(ub_ref[...], w_ref[...], preferred_element_type=f32)
        a, b = _coeff(pre[:, :HALF], pre[:, HALF:], u_ref[...], ba_ref[...], bx_ref[...], lam_ref[...])
        a_ref[...] = a
        b_ref[...] = b

    tile = pl.BlockSpec((COEFF_TM, HALF), lambda h, i: (i, h))
    pspecs, pargs = _dir_params(d, ba, bx, lam)
    return pl.pallas_call(
        body, name=f"l0_gates_coeff_{d}", grid=(2, rows // COEFF_TM),
        out_shape=[jax.ShapeDtypeStruct((rows, R), f32)] * 2,
        in_specs=[tile, tile, pl.BlockSpec((HALF, 2 * HALF), lambda h, i: (0, 2 * h + d))] + pspecs,
        out_specs=[tile, tile], compiler_params=_cparams("parallel", "parallel"),
    )(ub, u, wg, *pargs)


def _gates_coeff_bwd(ub, u, dh, yp, wg, ba, bx, lam, d, dpre_prev):
    rows = u.shape[0]
    n_prev = 0 if dpre_prev is None else 1

    def body(ub_ref, u_ref, dh_ref, yp_ref, w_ref, ba_ref, bx_ref, lam_ref, *rest):
        dpre_ref, du_ref, dba_ref, dbx_ref, dlam_ref = rest[n_prev:]
        pre = jnp.dot(ub_ref[...], w_ref[...], preferred_element_type=f32)
        dhv = dh_ref[...]
        dpa, dpx, du, dba, dbx, dlam = _coeff_bwd(pre[:, :HALF], pre[:, HALF:], u_ref[...], ba_ref[...], bx_ref[...],
                                                  lam_ref[...], dhv * yp_ref[...], dhv)
        dpre_ref[:, :HALF] = dpa.astype(bf16)
        dpre_ref[:, HALF:] = dpx.astype(bf16)
        du_ref[...] = du

        @pl.when(pl.program_id(1) == 0)
        def _():
            dba_ref[...] = jnp.zeros_like(dba_ref)
            dbx_ref[...] = jnp.zeros_like(dbx_ref)
            dlam_ref[...] = jnp.zeros_like(dlam_ref)

        dba_ref[...] += dba
        dbx_ref[...] += dbx
        dlam_ref[...] += dlam

    tile = pl.BlockSpec((COEFF_TM, HALF), lambda h, i: (i, h))
    acc = pl.BlockSpec((1, HALF), lambda h, i: (0, h))
    pspecs, pargs = _dir_params(d, ba, bx, lam)
    extra = [] if dpre_prev is None else [dpre_prev]
    return pl.pallas_call(
        body, name=f"l0_gates_coeff_bwd_{d}", grid=(2, rows // COEFF_TM),
        out_shape=[jax.ShapeDtypeStruct((rows, 2 * N_PARTS * HALF), bf16), jax.ShapeDtypeStruct((rows, R), f32)]
        + [jax.ShapeDtypeStruct((1, R), f32)] * 3,
        in_specs=[tile] * 4 + [pl.BlockSpec((HALF, 2 * HALF), lambda h, i: (0, 2 * h + d))] + pspecs
        + [ANY_SPEC] * n_prev,
        out_specs=[pl.BlockSpec((COEFF_TM, 2 * HALF), lambda h, i: (i, 2 * h + d)), tile, acc, acc, acc],
        input_output_aliases={8: 0} if n_prev else {}, compiler_params=_cparams("parallel", "arbitrary"),
    )(ub, u, dh, yp, wg, *pargs, *extra)


def _gates_dw(u, dpre):
    rows = u.shape[0]

    def body(u_ref, d_ref, o_ref):
        o_ref[...] = lax.dot_general(u_ref[...], d_ref[...], (((0,), (0,)), ((), ())), preferred_element_type=f32)

    return pl.pallas_call(
        body, name="l0_gates_dw", grid=(2 * N_PARTS,), out_shape=jax.ShapeDtypeStruct((HALF, 2 * N_PARTS * HALF), f32),
        in_specs=[pl.BlockSpec((rows, HALF), lambda j: (0, j // N_PARTS)), pl.BlockSpec((rows, HALF), lambda j: (0, j))],
        out_specs=pl.BlockSpec((HALF, HALF), lambda j: (0, j)), compiler_params=_cparams("parallel"),
    )(u, dpre)


N_SCAN_CHUNKS = T_ALL // SCAN_CHUNK
SCAN_FWD = lambda t: t
SCAN_FWD_BWD = lambda t: N_SCAN_CHUNKS - 1 - t
SCAN_REV = lambda t: jnp.where(t == 0, 0, N_SCAN_CHUNKS - t)
SCAN_REV_BWD = lambda t: jnp.where(t == N_SCAN_CHUNKS - 1, 0, t + 1)
CONV_SEGMENTS = ((0, T_CTX), (T_CTX, T_LAT))
TM = 128
FUSED_TM = 256


def _local_step(x, ctx, target, mods, cmod, wts, late_weights, send_grads, reduce_loss, start_after=()):
    sh1, sc1, g1, sh2, sc2, g2 = [[mods[l, i][None] for l in range(2)] for i in range(N_MOD)]
    ng = wts["norm_g"]
    xcat = jnp.concatenate([ctx, x], axis=0)
    poscat = jnp.concatenate([jnp.zeros((T_CTX, D), f32), _pos_embed()], axis=0)
    scp = jnp.concatenate([cmod[1][None], sc1[0]], axis=0)
    shp = jnp.concatenate([cmod[0][None], sh1[0]], axis=0)

    ctx_tiles = T_CTX // FUSED_TM
    nt = (((1,), (1,)), ((), ()))

    def blend(i, p):
        sel = jnp.where(i < ctx_tiles, 1.0, 0.0)
        return sel * p[0:1] + (1.0 - sel) * p[1:2]

    def f_pre0(i, xc, pos, g, scp_, shp_, w):
        x0 = xc + pos
        h = _normmod(x0, g, blend(i, scp_), blend(i, shp_)).astype(bf16)
        return x0, h, jnp.dot(h, w, preferred_element_type=f32)

    x0cat, h0, gr = _rowcall(f_pre0, "l0_prenorm_in_proj", T_ALL, FUSED_TM, [_rin(xcat), _rin(poscat)],
                             [ng[0, 0][None], scp, shp, wts["rec_w_in"]], [(D, f32), (D, bf16), (2 * R, f32)],
                             after=start_after)
    u, ub = _dwconv_fwd(gr, R // 256, wts["rec_conv_w"], wts["rec_conv_b"], 4, 1, CONV_SEGMENTS, 256,
                        "l0_conv", True)
    gate_args = (wts["gates"], wts["rec_b_a"], wts["rec_b_x"], wts["rec_lambda"])
    a0, b0 = _gates_coeff_fwd(ub, u, *gate_args, 0)
    a1, b1 = _gates_coeff_fwd(ub, u, *gate_args, 1)
    halfway = late_weights("mlp_halfway", a1)
    y0, yp0 = _scan_call(a0, b0, SCAN_FWD, False, "l0_scan_fwd", False, after=[halfway])
    y1, yp1 = _scan_call(a1, b1, SCAN_REV, True, "l0_scan_rev", False)

    wts = dict(wts, **late_weights("mlp", y1))

    def f_gate_out(i, gp, y0_, y1_, w):
        z = (_gelu(gp) * (y0_ + y1_)).astype(bf16)
        return z, jnp.dot(z, w, preferred_element_type=f32)

    zb, out0 = _rowcall(f_gate_out, "l0_gate_out_proj", T_LAT, FUSED_TM,
                        [_rin(gr, R, 0, ctx_tiles), _rin(y0, None, 0, ctx_tiles), _rin(y1, None, 0, ctx_tiles)],
                        [wts["rec_w_out"]], [(R, bf16), (D, f32)])

    zero_d = jnp.zeros((1, D), f32)

    def mlp_params(rows):
        rows = rows + [zero_d] * (N_MLP_PARAMS - len(rows))
        return jnp.concatenate([jnp.broadcast_to(r, (8, D)) for r in rows], axis=0)

    par0 = mlp_params([g1[0], zero_d, ng[0, 1][None], sc2[0], sh2[0], g2[0], ng[1, 0][None], sc1[1], sh1[1]])
    x1, h1, r0, mo0, x2, h2 = _mlp_forward(x0cat, T_CTX // MLP_TM, out0, par0, wts["mlp_w_in"], wts["mlp_w_out"], 0,
                                           "l0_mlp")

    wts = dict(wts, **late_weights("conf", x2))
    def glu(pa, pb, b1):
        return (pa + b1[:, :D]) * _sigmoid(pb + b1[:, D:])

    def f_pw1_glu(i, h_, b1, w):
        p = jnp.dot(h_, w, preferred_element_type=f32)
        return glu(p[:, :D], p[:, D:], b1), p

    zg, pw = _rowcall(f_pw1_glu, "l1_pw1_glu", T_LAT, FUSED_TM, [_rin(h2)], [wts["conf_b_pw1"], wts["conf_w_pw1"]],
                      [(D, f32), (2 * D, bf16)])
    (zc,) = _dwconv_fwd(zg, 0, wts["conf_conv_w"], wts["conf_conv_b"], 31, 15, ((0, T_LAT),), 128, "l1_conv", False)

    def ln_silu(z, lg, lb):
        mu = jnp.mean(z, axis=-1, keepdims=True)
        zc_ = z - mu
        var = jnp.mean(zc_ * zc_, axis=-1, keepdims=True)
        yv = zc_ * lax.rsqrt(var + EPS) * lg + lb
        return yv * _sigmoid(yv)

    def f_lnsilu_pw2(i, z, lg, lb, w):
        s = ln_silu(z, lg, lb).astype(bf16)
        return s, jnp.dot(s, w, preferred_element_type=f32)

    sb, out1 = _rowcall(f_lnsilu_pw2, "l1_ln_silu_pw2", T_LAT, FUSED_TM, [_rin(zc)],
                        [wts["conf_ln_g"], wts["conf_ln_b"], wts["conf_w_pw2"]], [(D, bf16), (D, f32)])
    par1 = mlp_params([g1[1], wts["conf_b_pw2"], ng[1, 1][None], sc2[1], sh2[1], g2[1]])
    x3, h3, r1, mo1, x4, _ = _mlp_forward(x2, 0, out1, par1, wts["mlp_w_in"], wts["mlp_w_out"], 1, "l1_mlp")

    def loss_fn(x4_, fg, tgt):
        err = _rms(x4_, fg) - tgt
        per_row = jnp.mean(err * err, axis=-1, keepdims=True)
        return 0.5 * jnp.sum(per_row, axis=0, keepdims=True)

    def f_head(i, x4_, tgt, fg):
        loss, vjp = jax.vjp(lambda a, e: loss_fn(a, e, tgt), x4_, fg)
        dx, dfg = vjp(jnp.ones((1, 1), f32))
        return dx, jnp.broadcast_to(loss, (1, 128)), dfg

    dx4, loss_acc, dfinal_g = _rowcall(f_head, "head", T_LAT, TM, [_rin(x4), _rin(target)], [wts["final_g"]],
                                       [(D, f32)], [(1, 128), (1, D)])

    grads = {"final_g": dfinal_g}
    loss = reduce_loss(loss_acc[0, 0])

    dx3, dout1, dmo1, dhid1, acc1 = _mlp_backward(dx4, x3, r1, mo1, out1, par1, wts["mlp_w_in"], wts["mlp_w_out"], 1,
                                                  "l1_mlp_bwd", after=[loss.reshape(1, 1)])
    mlp_dw = _mlp_weight_grads(h3, dhid1, r1, dmo1, 1, None, "l1")
    dg1_1, db_pw2, dng11, dsc2_1, dsh2_1, dg2_1 = [acc1[k:k + 1] for k in range(6)]

    grads["conf_w_pw2"] = _mm(sb, dout1, "l1_pw2_dw", ta=True, out_dtype=bf16)
    grads["conf_b_pw2"] = db_pw2

    def f_pw2_lnsilu_bwd(i, z, dout, lg, lb, w):
        ds = lax.dot_general(dout, w, nt, preferred_element_type=f32)
        _, vjp = jax.vjp(ln_silu, z, lg, lb)
        return vjp(ds)

    dzc, dln_g, dln_b = _rowcall(f_pw2_lnsilu_bwd, "l1_pw2_ln_silu_bwd", T_LAT, FUSED_TM, [_rin(zc), _rin(dout1)],
                                 [wts["conf_ln_g"], wts["conf_ln_b"], wts["conf_w_pw2"]], [(D, f32)], [(1, D)] * 2)
    grads["conf_ln_g"], grads["conf_ln_b"] = dln_g, dln_b
    dzg, dconv_w, dconv_b = _dwconv_bwd([dzc], zg, 0, wts["conf_conv_w"], 31, 15, ((0, T_LAT),), 128,
                                        "l1_conv_bwd", f32)
    grads["conf_conv_w"], grads["conf_conv_b"] = dconv_w, dconv_b

    def f_glu_pw1_norm_bwd(i, p_, dz, x_, dxs, b1, g_, sc_, sh_, w):
        pf = p_.astype(f32)
        _, vjp = jax.vjp(glu, pf[:, :D], pf[:, D:], b1)
        da, db, db1 = vjp(dz)
        dp = jnp.concatenate([da, db], axis=1).astype(bf16)
        dh = lax.dot_general(dp, w, nt, preferred_element_type=f32)
        _, vjp = jax.vjp(_normmod, x_, g_, sc_, sh_)
        dx, dg, dsc, dsh = vjp(dh)
        return dp, dx + dxs, db1, dg, dsc, dsh

    dpw, dx2, db_pw1, dng10, dsc1_1, dsh1_1 = _rowcall(
        f_glu_pw1_norm_bwd, "l1_glu_pw1_normmod_bwd", T_LAT, FUSED_TM, [_rin(pw), _rin(dzg), _rin(x2), _rin(dx3)],
        [wts["conf_b_pw1"], ng[1, 0][None], sc1[1], sh1[1], wts["conf_w_pw1"]], [(2 * D, bf16), (D, f32)],
        [(1, 2 * D), (1, D), (1, D), (1, D)])
    grads["conf_b_pw1"] = db_pw1
    grads["conf_w_pw1"] = _mm(h2, dpw, "l1_pw1_dw", ta=True, out_dtype=bf16)
    sent = send_grads(["conf_w_pw2", "conf_w_pw1"], grads)

    dx1, dout0, dmo0, dhid0, acc0 = _mlp_backward(dx2, x1, r0, mo0, out0, par0, wts["mlp_w_in"], wts["mlp_w_out"], 0,
                                                  "l0_mlp_bwd", after=[sent])
    grads["mlp_w_in"], grads["mlp_w_out"] = _mlp_weight_grads(h1, dhid0, r0, dmo0, 0, mlp_dw, "l0")
    sent = send_grads(["mlp_w_in", "mlp_w_out"], grads)
    dg1_0, _, dng01, dsc2_0, dsh2_0, dg2_0 = [acc0[k:k + 1] for k in range(6)]

    grads["rec_w_out"] = _mm(zb, dout0, "l0_out_proj_dw", ta=True, out_dtype=bf16, after=[sent])
    sent = send_grads(["rec_w_out"], grads)

    def f_out_gate_bwd(i, gp, y0_, y1_, dout, w):
        lat = jnp.where(i < ctx_tiles, 0.0, 1.0)
        dz = lax.dot_general(dout, w, nt, preferred_element_type=f32)
        _, vjp = jax.vjp(lambda a, b: _gelu(a) * b, gp, y0_ + y1_)
        dgp, dy = vjp(dz)
        return dgp * lat, dy * lat

    dgp, dy = _rowcall(f_out_gate_bwd, "l0_out_proj_gate_bwd", T_ALL, FUSED_TM,
                       [_rin(gr, R, 0), _rin(y0), _rin(y1), _rin(dout0, None, 0, -ctx_tiles)], [wts["rec_w_out"]],
                       [(R, bf16), (R, f32)], after=[sent])
    (dh_f,) = _scan_call(a0, dy, SCAN_FWD_BWD, True, "l0_scan_fwd_bwd", True)
    (dh_r,) = _scan_call(a1, dy, SCAN_REV_BWD, False, "l0_scan_rev_bwd", True)

    dpre, du_f, *dpar_f = _gates_coeff_bwd(ub, u, dh_f, yp0, *gate_args, 0, None)
    dpre, du_r, *dpar_r = _gates_coeff_bwd(ub, u, dh_r, yp1, *gate_args, 1, dpre)
    grads["rec_b_a"], grads["rec_b_x"], grads["rec_lambda"] = [
        jnp.concatenate([f.reshape(-1), r_.reshape(-1)]).reshape(2, R) for f, r_ in zip(dpar_f, dpar_r)]
    grads["gates"] = _gates_dw(ub, dpre)
    sent = send_grads(["replicated"], grads)
    du_gates = _gates_dx(dpre, wts["gates"], after=[sent])
    drec, dconv4_w, dconv4_b = _dwconv_bwd([du_f, du_r, du_gates], gr, R // 256, wts["rec_conv_w"], 4, 1,
                                           CONV_SEGMENTS, 256, "l0_conv_bwd", bf16)
    grads["rec_conv_w"], grads["rec_conv_b"] = dconv4_w, dconv4_b
    dgr = jnp.concatenate([dgp, drec], axis=1)
    grads["rec_w_in"] = _mm(h0, dgr, "l0_in_proj_dw", ta=True, out_dtype=bf16)
    sent = send_grads(["rec_w_in"], grads)

    def f_pre0_bwd(i, x0, dgr_, dxs, g, scp_, shp_, w):
        lat = jnp.where(i < ctx_tiles, 0.0, 1.0)
        dh = lax.dot_general(dgr_, w, nt, preferred_element_type=f32)
        _, vjp = jax.vjp(lambda a, b, c, e: _normmod(a, b, blend(i, c), blend(i, e)), x0, g, scp_, shp_)
        dx, dg, dscp, dshp = vjp(dh)
        return dx + lat * dxs, dg, dscp, dshp

    dx0cat, dng00, dscp, dshp = _rowcall(
        f_pre0_bwd, "l0_in_proj_prenorm_bwd", T_ALL, FUSED_TM,
        [_rin(x0cat), _rin(dgr), _rin(dx1, None, 0, -ctx_tiles)], [ng[0, 0][None], scp, shp, wts["rec_w_in"]],
        [(D, f32)], [(1, D), (2, D), (2, D)], after=[sent])

    grads["norm_g"] = jnp.stack([jnp.concatenate([dng00, dng01], 0), jnp.concatenate([dng10, dng11], 0)])
    dmods = jnp.stack([
        jnp.concatenate([dshp[1:2], dscp[1:2], dg1_0, dsh2_0, dsc2_0, dg2_0], axis=0),
        jnp.concatenate([dsh1_1, dsc1_1, dg1_1, dsh2_1, dsc2_1, dg2_1], axis=0)])
    dcmod = jnp.concatenate([dshp[0:1], dscp[0:1]], axis=0)
    return loss, dx0cat[T_CTX:], dmods, dcmod, grads


def _unshard_cols(g):
    g = jnp.moveaxis(g, 0, -2)
    return g.reshape(g.shape[:-2] + (g.shape[-2] * g.shape[-1],))


def _shard_cols(w):
    w = w.reshape(w.shape[:-1] + (N_DEV, w.shape[-1] // N_DEV))
    return jnp.moveaxis(w, -2, 0)


def _shard_rows(w):
    return w.reshape((N_DEV, w.shape[0] // N_DEV) + w.shape[1:])


SMALL_PACK_ROWS = 64


def kernel(x, c, ctx, c_ctx, w_ada, b_ada, norm_g, rec_w_in, rec_conv_w, rec_conv_b, rec_lambda, rec_w_a, rec_b_a, rec_w_x, rec_b_x, rec_w_out, conf_w_pw1, conf_b_pw1, conf_conv_w, conf_conv_b, conf_ln_g, conf_ln_b, conf_w_pw2, conf_b_pw2, mlp_w_in, mlp_w_out, final_g, loss_target, m_c_ctx, m_w_ada, m_b_ada, m_norm_g, m_rec_w_in, m_rec_conv_w, m_rec_conv_b, m_rec_lambda, m_rec_w_a, m_rec_b_a, m_rec_w_x, m_rec_b_x, m_rec_w_out, m_conf_w_pw1, m_conf_b_pw1, m_conf_conv_w, m_conf_conv_b, m_conf_ln_g, m_conf_ln_b, m_conf_w_pw2, m_conf_b_pw2, m_mlp_w_in, m_mlp_w_out, m_final_g, v_c_ctx, v_w_ada, v_b_ada, v_norm_g, v_rec_w_in, v_rec_conv_w, v_rec_conv_b, v_rec_lambda, v_rec_w_a, v_rec_b_a, v_rec_w_x, v_rec_b_x, v_rec_w_out, v_conf_w_pw1, v_conf_b_pw1, v_conf_conv_w, v_conf_conv_b, v_conf_ln_g, v_conf_ln_b, v_conf_w_pw2, v_conf_b_pw2, v_mlp_w_in, v_mlp_w_out, v_final_g):
    me = 4 * lax.axis_index("x") + 2 * lax.axis_index("y") + lax.axis_index("c")
    weights = dict(c_ctx=c_ctx, w_ada=w_ada, b_ada=b_ada, norm_g=norm_g, rec_w_in=rec_w_in, rec_conv_w=rec_conv_w,
                   rec_conv_b=rec_conv_b, rec_lambda=rec_lambda, rec_w_a=rec_w_a, rec_b_a=rec_b_a, rec_w_x=rec_w_x,
                   rec_b_x=rec_b_x, rec_w_out=rec_w_out, conf_w_pw1=conf_w_pw1, conf_b_pw1=conf_b_pw1,
                   conf_conv_w=conf_conv_w, conf_conv_b=conf_conv_b, conf_ln_g=conf_ln_g, conf_ln_b=conf_ln_b,
                   conf_w_pw2=conf_w_pw2, conf_b_pw2=conf_b_pw2, mlp_w_in=mlp_w_in, mlp_w_out=mlp_w_out, final_g=final_g)
    m_in = dict(c_ctx=m_c_ctx, w_ada=m_w_ada, b_ada=m_b_ada, norm_g=m_norm_g, rec_w_in=m_rec_w_in, rec_conv_w=m_rec_conv_w,
                rec_conv_b=m_rec_conv_b, rec_lambda=m_rec_lambda, rec_w_a=m_rec_w_a, rec_b_a=m_rec_b_a, rec_w_x=m_rec_w_x,
                rec_b_x=m_rec_b_x, rec_w_out=m_rec_w_out, conf_w_pw1=m_conf_w_pw1, conf_b_pw1=m_conf_b_pw1,
                conf_conv_w=m_conf_conv_w, conf_conv_b=m_conf_conv_b, conf_ln_g=m_conf_ln_g, conf_ln_b=m_conf_ln_b,
                conf_w_pw2=m_conf_w_pw2, conf_b_pw2=m_conf_b_pw2, mlp_w_in=m_mlp_w_in, mlp_w_out=m_mlp_w_out,
                final_g=m_final_g)
    v_in = dict(c_ctx=v_c_ctx, w_ada=v_w_ada, b_ada=v_b_ada, norm_g=v_norm_g, rec_w_in=v_rec_w_in, rec_conv_w=v_rec_conv_w,
                rec_conv_b=v_rec_conv_b, rec_lambda=v_rec_lambda, rec_w_a=v_rec_w_a, rec_b_a=v_rec_b_a, rec_w_x=v_rec_w_x,
                rec_b_x=v_rec_b_x, rec_w_out=v_rec_w_out, conf_w_pw1=v_conf_w_pw1, conf_b_pw1=v_conf_b_pw1,
                conf_conv_w=v_conf_conv_w, conf_conv_b=v_conf_conv_b, conf_ln_g=v_conf_ln_g, conf_ln_b=v_conf_ln_b,
                conf_w_pw2=v_conf_w_pw2, conf_b_pw2=v_conf_b_pw2, mlp_w_in=v_mlp_w_in, mlp_w_out=v_mlp_w_out,
                final_g=v_final_g)
    names = list(weights)

    small_items = [c, norm_g, rec_conv_w, rec_lambda, conf_b_pw1, conf_conv_w, conf_conv_b, conf_ln_g, conf_ln_b,
                   conf_b_pw2]
    flat = jnp.concatenate([a.reshape(-1) for a in small_items])
    flat = jnp.pad(flat, (0, SMALL_PACK_ROWS * 128 - flat.shape[0])).reshape(SMALL_PACK_ROWS, 128)
    as_shard = lambda a: a.astype(bf16).reshape(-1, a.shape[-1])
    small_all, early = _all_gather_2level([flat, as_shard(rec_w_in[0])], "gather_small_and_early")

    small_all = small_all.reshape(N_DEV, -1)
    off = 0
    small = []
    for a in small_items:
        small.append(small_all[:, off:off + a.size].reshape((N_DEV,) + a.shape))
        off += a.size
    c_all, ng_all, rcw_all, lam_all, bpw1_all, ccw_all, ccb_all, lng_all, lnb_all, bpw2_all = small
    wts = {
        "norm_g": _unshard_cols(ng_all),
        "rec_conv_w": _unshard_cols(rcw_all)[0],
        "rec_lambda": _unshard_cols(lam_all)[0],
        "conf_b_pw1": _unshard_cols(bpw1_all),
        "conf_conv_w": _unshard_cols(ccw_all)[0],
        "conf_conv_b": _unshard_cols(ccb_all),
        "conf_ln_g": _unshard_cols(lng_all),
        "conf_ln_b": _unshard_cols(lnb_all),
        "conf_b_pw2": _unshard_cols(bpw2_all),
        "rec_conv_b": rec_conv_b,
        "rec_b_a": rec_b_a[0].reshape(2, R),
        "rec_b_x": rec_b_x[0].reshape(2, R),
        "final_g": final_g[None],
        "gates": _gate_matrix(rec_w_a[0], rec_w_x[0]),
    }

    c16 = jnp.concatenate([c_all[:, 0], jnp.broadcast_to(c_ctx[None], (8, D))], axis=0)
    b_loc = lax.dynamic_slice_in_dim(b_ada, me * ADA_SHARD, ADA_SHARD, axis=1)[:, None]
    (mods_all,) = _all_gather([_ada_forward(c16, w_ada, b_loc)], "gather_mods")
    mods_all = _unshard_cols(mods_all)
    mods = lax.dynamic_index_in_dim(mods_all, me, axis=1, keepdims=False).reshape(2, N_MOD, D)
    cmod = mods_all[0, 8, :2 * D].reshape(2, D)

    wts["rec_w_in"] = _unshard_cols(early)
    late_items = {"mlp": [rec_w_out[0], mlp_w_in, mlp_w_out], "conf": [conf_w_pw1[0], conf_w_pw2[0]]}
    late_shards = {g: [as_shard(a) for a in items] for g, items in late_items.items()}
    late_lands = {g: [_own_block_filled(s, me) for s in shards] for g, shards in late_shards.items()}
    late_handles = {}
    late_handles["mlp"], token = _gather2_start(late_shards["mlp"], late_lands["mlp"], "gather_mlp_start", [early, mods])
    order = [token]

    def late_weights(group, after):
        if group == "mlp_halfway":
            late_handles["mlp"] = _gather2_forward1(late_handles["mlp"], after, "gather_mlp_forward1")
            late_handles["conf"], started = _exchange_start(late_shards["conf"], late_lands["conf"], "gather_conf_start",
                                                            False, after=[late_handles["mlp"][2][0]])
            return started
        if group == "mlp":
            passed = _gather2_forward2(late_handles["mlp"], after, "gather_mlp_forward2")
            got = _gather2_wait(passed, after, "gather_mlp_wait")
        else:
            got = _exchange_wait(late_handles[group], after, "gather_conf_wait", False)
        got = [g.reshape((N_DEV,) + a.shape) for g, a in zip(got, late_items[group])]
        if group == "mlp":
            return {"rec_w_out": got[0].reshape(R, D), "mlp_w_in": got[1], "mlp_w_out": got[2]}
        return {"conf_w_pw1": _unshard_cols(got[0]), "conf_w_pw2": got[1].reshape(D, D)}

    to_blocks = {"rec_w_in": _shard_cols, "conf_w_pw1": _shard_cols, "rec_w_out": _shard_rows, "conf_w_pw2": _shard_rows,
                 "mlp_w_in": lambda g: g, "mlp_w_out": lambda g: g}
    grad_handles = []

    repl_names = ["rec_w_a", "rec_w_x", "rec_b_a", "rec_b_x", "final_g"]

    def send_replicated(grads):
        dwg = grads["gates"]
        repl = {"rec_w_a": jnp.stack([_gate_blocks(dwg, 0), _gate_blocks(dwg, 2)]),
                "rec_w_x": jnp.stack([_gate_blocks(dwg, 1), _gate_blocks(dwg, 3)]),
                "rec_b_a": grads["rec_b_a"], "rec_b_x": grads["rec_b_x"], "final_g": grads["final_g"]}
        flat = jnp.concatenate([repl[n].reshape(-1) for n in repl_names])
        rows = -(-flat.shape[0] // (16 * D)) * 16
        flat = jnp.pad(flat, (0, rows * D - flat.shape[0])).reshape(rows, D).astype(bf16)
        handle, sent = _exchange_start([flat], [_own_block_filled(flat, me)], "gather_replicated_start", False)
        grad_handles.append((["replicated"], handle))
        return sent

    def send_grads(group, grads):
        if group == ["replicated"]:
            return send_replicated(grads)
        blocks = [to_blocks[n](grads[n]) for n in group]
        blocks = [g.reshape(N_DEV, -1, g.shape[-1]) for g in blocks]
        lands = [_own_block_filled(lax.dynamic_index_in_dim(g, me, 0, keepdims=False), me) for g in blocks]
        handle, sent = _exchange_start(blocks, lands, "scatter_start_" + group[0], True)
        grad_handles.append((group, handle))
        return sent

    loss, grad_x, dmods, dcmod, grads = _local_step(
        x[0], ctx[0], loss_target[0], mods, cmod, wts, late_weights, send_grads,
        lambda partial: lax.psum(partial, ("x", "y", "c")), start_after=order)

    def as2d(shape):
        rows = 1
        for s in shape[:-1]:
            rows *= s
        return (rows, shape[-1])

    def whole(arr, shape):
        arr = arr.reshape((-1,) + as2d(shape))
        return (arr, arr.shape[0])

    shard_shapes = {n: weights[n].shape for n in names}
    g_out, d_out, m_out, v_out = {}, {}, {}, {}

    def adamw(n, pieces, after):
        shape = shard_shapes[n]
        r2, c2 = as2d(shape)
        g, dl, nm, nv = _adamw(pieces, weights[n].reshape(r2, c2), m_in[n].reshape(r2, c2), v_in[n].reshape(r2, c2),
                               "adamw_" + n, after=after)
        g_out[n], d_out[n], m_out[n], v_out[n] = (t.reshape(shape) for t in (g, dl, nm, nv))
        return g

    small_sharded = ["norm_g", "rec_conv_w", "rec_lambda", "conf_b_pw1", "conf_conv_w", "conf_conv_b", "conf_ln_g",
                     "conf_ln_b", "conf_b_pw2"]
    pack = jnp.concatenate([_shard_cols(grads[n]).reshape(N_DEV, -1) for n in small_sharded], axis=1)
    pack = jnp.pad(pack, ((0, 0), (0, SMALL_PACK_ROWS * 128 - pack.shape[1]))).reshape(N_DEV, SMALL_PACK_ROWS, 128)
    small_handle, token = _exchange_start(
        [pack], [_own_block_filled(lax.dynamic_index_in_dim(pack, me, 0, keepdims=False), me)], "scatter_small_start",
        True, after=[grad_x])
    dm_flat = jnp.concatenate([dmods.reshape(-1), dcmod.reshape(-1), grads["rec_conv_b"].reshape(-1)])
    dm_len = dm_flat.shape[0]
    dm_flat = jnp.pad(dm_flat, (0, 128 * 128 - dm_len)).reshape(128, 128)
    dm_handle, token = _exchange_start([dm_flat], [_own_block_filled(dm_flat, me)], "gather_dmods_start", False,
                                       after=[token])

    done = token
    for group, handle in grad_handles:
        if group == ["replicated"]:
            repl_all = _exchange_wait(handle, done, "gather_replicated_wait", False)[0].reshape(N_DEV, -1)
            off = 0
            for n in repl_names:
                size = weights[n].size
                done = adamw(n, [whole(repl_all[:, off:off + size], shard_shapes[n])], [done])
                off += size
            continue
        for n, got in zip(group, _exchange_wait(handle, done, "scatter_wait_" + group[0], True)):
            done = adamw(n, [(got, N_DEV)], [done])

    dm_all = _exchange_wait(dm_handle, done, "gather_dmods_wait", False)[0].reshape(N_DEV, -1)
    dmods_all = dm_all[:, :2 * N_MOD * D].reshape(N_DEV, 2, N_MOD * D)
    dcmod_all = jnp.pad(dm_all[:, 2 * N_MOD * D:2 * N_MOD * D + 2 * D], ((0, 0), (0, (N_MOD - 2) * D)))
    g16_full = jnp.stack([jnp.concatenate([dmods_all[:, 0], dcmod_all], axis=0),
                          jnp.concatenate([dmods_all[:, 1], jnp.zeros_like(dcmod_all)], axis=0)])
    g16 = lax.dynamic_slice_in_dim(g16_full, me * ADA_SHARD, ADA_SHARD, axis=2)
    dw_ada, ds_part = _ada_backward(c16, g16, w_ada)
    ds_handle, token = _exchange_start([ds_part[0]], [_own_block_filled(ds_part[0], me)], "gather_dsilu_start", False)
    done = adamw("w_ada", [whole(dw_ada, shard_shapes["w_ada"])], [token])
    done = adamw("rec_conv_b", [whole(dm_all[:, dm_len - R:dm_len], shard_shapes["rec_conv_b"])], [done])
    db_terms = jnp.concatenate([dmods_all, jnp.stack([dcmod_all, jnp.zeros_like(dcmod_all)], axis=1)], axis=0)
    done = adamw("b_ada", [whole(db_terms, shard_shapes["b_ada"])], [done])
    pack_recv = _exchange_wait(small_handle, done, "scatter_small_wait", True)[0].reshape(N_DEV, -1)
    off = 0
    for n in small_sharded:
        size = weights[n].size
        done = adamw(n, [whole(pack_recv[:, off:off + size], shard_shapes[n])], [done])
        off += size
    ds_all = _exchange_wait(ds_handle, done, "gather_dsilu_wait", False)[0]
    adamw("c_ctx", [whole(ds_all[:, 0], shard_shapes["c_ctx"])], [])

    return (loss, grad_x[None], *[g_out[n] for n in names], *[d_out[n] for n in names],
            *[m_out[n] for n in names], *[v_out[n] for n in names])
```

```python
import functools

import jax
import jax.numpy as jnp
from jax import lax
from jax.experimental import pallas as pl
from jax.experimental.pallas import tpu as pltpu

f32 = jnp.float32
bf16 = jnp.bfloat16

N_DEV = 8
D = 1024
T_LAT = 2048
T_CTX = 256
T_ALL = T_CTX + T_LAT
R = 1280
N_BLK = 16
BLK = R // N_BLK
F = 4096
GRID_W = 64
RG_C = 8.0
EPS = 1e-6
POS_BASE = 10000.0
N_MOD = 6
ADA_SHARD = N_MOD * D // N_DEV

ADAM_LR = 0.001
ADAM_B1 = 0.9
ADAM_B2 = 0.999
ADAM_EPS = 1e-08
ADAM_WD = 0.01
ADAM_STEP = 10

VMEM_LIMIT_V7X = 56 * 1024 * 1024
HALO = 16
MESH = pl.DeviceIdType.MESH


def _cparams(*sem):
    return pltpu.CompilerParams(dimension_semantics=sem, vmem_limit_bytes=VMEM_LIMIT_V7X)


def _pick(n, cands):
    for c in cands:
        if n % c == 0:
            return c
    raise ValueError(f"no block size for {n}")


def _position():
    x, y, c = lax.axis_index("x"), lax.axis_index("y"), lax.axis_index("c")
    return x, y, c, 4 * x + 2 * y + c


def _peer(x, y, c, k):
    px = (1 - x) if (k >> 2) & 1 else x
    py = (1 - y) if (k >> 1) & 1 else y
    pc = (1 - c) if k & 1 else c
    return (px, py, pc), 4 * px + 2 * py + pc


def _exchange(arrs, name, scatter):
    n = len(arrs)

    def body(*refs):
        ins, outs = refs[:n], refs[n:2 * n]
        send_sems, recv_sems, local_sems = refs[2 * n:]
        x, y, c, me = _position()
        local = []
        for a in range(n):
            src = ins[a].at[me] if scatter else ins[a]
            cp = pltpu.make_async_copy(src, outs[a].at[me], local_sems.at[a])
            cp.start()
            local.append(cp)
        sends, recvs = [], []
        for a in range(n):
            for k in range(1, N_DEV):
                peer, peer_lin = _peer(x, y, c, k)
                src = ins[a].at[peer_lin] if scatter else ins[a]
                cp = pltpu.make_async_remote_copy(
                    src_ref=src, dst_ref=outs[a].at[me], send_sem=send_sems.at[a, k - 1],
                    recv_sem=recv_sems.at[a, k - 1], device_id=peer, device_id_type=MESH)
                cp.start()
                sends.append(cp)
                recvs.append(pltpu.make_async_remote_copy(
                    src_ref=src, dst_ref=outs[a].at[peer_lin], send_sem=send_sems.at[a, k - 1],
                    recv_sem=recv_sems.at[a, k - 1], device_id=peer, device_id_type=MESH))
        for cp in recvs:
            cp.wait_recv()
        for cp in sends:
            cp.wait_send()
        for cp in local:
            cp.wait()

    if scatter:
        out_shape = [jax.ShapeDtypeStruct(a.shape, a.dtype) for a in arrs]
    else:
        out_shape = [jax.ShapeDtypeStruct((N_DEV,) + a.shape, a.dtype) for a in arrs]
    any_spec = pl.BlockSpec(memory_space=pl.ANY)
    return pl.pallas_call(
        body, name=name, out_shape=out_shape,
        in_specs=[any_spec] * n, out_specs=[any_spec] * n,
        scratch_shapes=[pltpu.SemaphoreType.DMA((n, N_DEV - 1)), pltpu.SemaphoreType.DMA((n, N_DEV - 1)),
                        pltpu.SemaphoreType.DMA((n,))],
    )(*arrs)


def _all_gather(arrs, name):
    return _exchange(arrs, name, scatter=False)


def _lin(p):
    return 4 * p[0] + 2 * p[1] + p[2]


HBM_SPEC = pl.BlockSpec(memory_space=pltpu.HBM)
SEM_SPEC = pl.BlockSpec(memory_space=pltpu.SEMAPHORE)
DATAFLOW_EFFECT = pltpu.SideEffectType.DATAFLOW_SIDE_EFFECTING


def _split_copies(srcs, lands, send_sems, recv_sems, scatter):
    x, y, c, me = _position()
    out = []
    for a in range(len(srcs)):
        for k in range(1, N_DEV):
            peer, peer_lin = _peer(x, y, c, k)
            src = srcs[a].at[peer_lin] if scatter else srcs[a]
            mk = lambda slot: pltpu.make_async_remote_copy(
                src_ref=src, dst_ref=lands[a].at[slot], send_sem=send_sems.at[a * (N_DEV - 1) + k - 1],
                recv_sem=recv_sems.at[a * (N_DEV - 1) + k - 1], device_id=peer, device_id_type=MESH)
            out.append((mk(me), mk(peer_lin)))
    return out


def _exchange_start(srcs, lands, name, scatter, after=()):
    n = len(srcs)
    n_after = len(after)

    def body(*refs):
        srcs_r, lands_r = refs[:n], refs[n:2 * n]
        send_sems, recv_sems = refs[2 * n + n_after], refs[2 * n + n_after + 1]
        token = refs[-1]
        for outgoing, _ in _split_copies(srcs_r, lands_r, send_sems, recv_sems, scatter):
            outgoing.start()
        token[...] = jnp.zeros_like(token)

    hbm = lambda a: pltpu.HBM(a.shape, a.dtype)
    res = pl.pallas_call(
        body, name=name,
        out_shape=(pltpu.SemaphoreType.DMA((n * (N_DEV - 1),)), pltpu.SemaphoreType.DMA((n * (N_DEV - 1),)),
                   *[hbm(a) for a in srcs], *[hbm(a) for a in lands], jax.ShapeDtypeStruct((8, 128), f32)),
        in_specs=[HBM_SPEC] * (2 * n) + [pl.BlockSpec(memory_space=pl.ANY)] * n_after,
        out_specs=(SEM_SPEC, SEM_SPEC, *[HBM_SPEC] * (2 * n), pl.BlockSpec(memory_space=pltpu.VMEM)),
        input_output_aliases={i: 2 + i for i in range(2 * n)},
        compiler_params=pltpu.CompilerParams(has_side_effects=DATAFLOW_EFFECT),
    )(*[pltpu.with_memory_space_constraint(a, pltpu.HBM) for a in list(srcs) + list(lands)], *after)
    return (res[0], res[1], list(res[2:2 + n]), list(res[2 + n:2 + 2 * n])), res[-1]


def _exchange_wait(handle, after, name, scatter):
    send_sems, recv_sems, srcs, lands = handle
    n = len(srcs)

    def body(*refs):
        srcs_r, lands_r = refs[:n], refs[n:2 * n]
        send_s, recv_s = refs[2 * n], refs[2 * n + 1]
        for outgoing, incoming in _split_copies(srcs_r, lands_r, send_s, recv_s, scatter):
            outgoing.wait_send()
            incoming.wait_recv()

    hbm = lambda a: pltpu.HBM(a.shape, a.dtype)
    res = pl.pallas_call(
        body, name=name, out_shape=tuple(hbm(a) for a in list(srcs) + list(lands)),
        in_specs=[HBM_SPEC] * (2 * n) + [SEM_SPEC, SEM_SPEC, pl.BlockSpec(memory_space=pl.ANY)],
        out_specs=tuple([HBM_SPEC] * (2 * n)),
        input_output_aliases={i: i for i in range(2 * n)},
        compiler_params=pltpu.CompilerParams(has_side_effects=DATAFLOW_EFFECT),
    )(*srcs, *lands, send_sems, recv_sems, after)
    return list(res[n:])


def _split_call(body, name, hbm_ins, kept, in_sems, n_new_sems, after, with_token):
    n_in, n_sem = len(hbm_ins), len(in_sems)
    out_shape, out_specs = [], []
    if n_new_sems:
        out_shape += [pltpu.SemaphoreType.DMA((n_new_sems,))] * 2
        out_specs += [SEM_SPEC] * 2
    first_kept = len(out_shape)
    out_shape += [pltpu.HBM(hbm_ins[i].shape, hbm_ins[i].dtype) for i in kept]
    out_specs += [HBM_SPEC] * len(kept)
    if with_token:
        out_shape.append(jax.ShapeDtypeStruct((8, 128), f32))
        out_specs.append(pl.BlockSpec(memory_space=pltpu.VMEM))

    def wrapped(*refs):
        outs = refs[n_in + n_sem + len(after):]
        body(refs[:n_in], refs[n_in:n_in + n_sem], outs[:2] if n_new_sems else ())
        if with_token:
            outs[-1][...] = jnp.zeros_like(outs[-1])

    return pl.pallas_call(
        wrapped, name=name, out_shape=tuple(out_shape),
        in_specs=[HBM_SPEC] * n_in + [SEM_SPEC] * n_sem + [pl.BlockSpec(memory_space=pl.ANY)] * len(after),
        out_specs=tuple(out_specs), input_output_aliases={i: first_kept + j for j, i in enumerate(kept)},
        compiler_params=pltpu.CompilerParams(has_side_effects=DATAFLOW_EFFECT),
    )(*[pltpu.with_memory_space_constraint(a, pltpu.HBM) for a in hbm_ins], *in_sems, *after)


def _rcopy(src, dst, sems, k, to):
    return pltpu.make_async_remote_copy(src_ref=src, dst_ref=dst, send_sem=sems[0].at[k], recv_sem=sems[1].at[k],
                                        device_id=to, device_id_type=MESH)


def _gather2_start(shards, lands, name, after):
    n = len(shards)

    def body(ins, sems_in, sems_out):
        x, y, c, me = _position()
        for a in range(n):
            for k, to in enumerate(((x, y, 1 - c), (1 - x, y, c), (x, 1 - y, c))):
                _rcopy(ins[a], ins[n + a].at[me], sems_out, 3 * a + k, to).start()

    res = _split_call(body, name, list(shards) + list(lands), range(2 * n), (), 3 * n, after, True)
    return (res[0], res[1], list(res[2:2 + n]), list(res[2 + n:2 + 2 * n])), res[-1]


def _gather2_forward1(handle, after, name):
    send_sems, recv_sems, srcs, lands = handle
    n = len(srcs)

    def body(ins, sems_in, sems_out):
        x, y, c, me = _position()
        sib, xn, yn = (x, y, 1 - c), (1 - x, y, c), (x, 1 - y, c)
        for a in range(n):
            for k, peer in enumerate((sib, xn, yn)):
                _rcopy(ins[a], ins[n + a].at[me], sems_in, 3 * a + k, peer).wait_send()
                _rcopy(ins[a], ins[n + a].at[_lin(peer)], sems_in, 3 * a + k, peer).wait_recv()
        for a in range(n):
            land = ins[n + a]
            _rcopy(land.at[_lin(xn)], land.at[_lin(xn)], sems_out, 3 * a, sib).start()
            _rcopy(land.at[_lin(yn)], land.at[_lin(yn)], sems_out, 3 * a + 1, sib).start()

            @pl.when(c == 0)
            def _():
                _rcopy(land.at[_lin(xn)], land.at[_lin(xn)], sems_out, 3 * a + 2, yn).start()

            @pl.when(c == 1)
            def _():
                _rcopy(land.at[_lin(yn)], land.at[_lin(yn)], sems_out, 3 * a + 2, xn).start()

    res = _split_call(body, name, list(srcs) + list(lands), range(n, 2 * n), (send_sems, recv_sems), 3 * n, [after], False)
    return (res[0], res[1], list(res[2:]))


def _gather2_forward2(handle, after, name):
    send_sems, recv_sems, lands = handle
    n = len(lands)

    def body(ins, sems_in, sems_out):
        x, y, c, me = _position()
        sib, dg = (x, y, 1 - c), _lin((1 - x, 1 - y, c))
        for a in range(n):
            for k, slot in enumerate((_lin((1 - x, y, 1 - c)), _lin((x, 1 - y, 1 - c)), dg)):
                done = _rcopy(ins[a].at[slot], ins[a].at[slot], sems_in, 3 * a + k, sib)
                done.wait_send()
                done.wait_recv()
        for a in range(n):
            _rcopy(ins[a].at[dg], ins[a].at[dg], sems_out, a, sib).start()

    res = _split_call(body, name, list(lands), range(n), (send_sems, recv_sems), n, [after], False)
    return (res[0], res[1], list(res[2:]))


def _gather2_wait(handle, after, name):
    send_sems, recv_sems, lands = handle
    n = len(lands)

    def body(ins, sems_in, sems_out):
        x, y, c, me = _position()
        slot = _lin((1 - x, 1 - y, 1 - c))
        for a in range(n):
            done = _rcopy(ins[a].at[slot], ins[a].at[slot], sems_in, a, (x, y, 1 - c))
            done.wait_send()
            done.wait_recv()

    return list(_split_call(body, name, list(lands), range(n), (send_sems, recv_sems), 0, [after], False))


def _own_block_filled(block, me):
    land = lax.empty((N_DEV,) + block.shape, block.dtype)
    return lax.dynamic_update_index_in_dim(land, block, me, 0)


def _staged_copy(src, dst, buf, in_sems, out_sems, rows, chunk):
    n = rows // chunk

    def rd(i):
        return pltpu.make_async_copy(src.at[pl.ds(i * chunk, chunk)], buf.at[i % 2], in_sems.at[i % 2])

    def wr(i):
        return pltpu.make_async_copy(buf.at[i % 2], dst.at[pl.ds(i * chunk, chunk)], out_sems.at[i % 2])

    rd(0).start()
    for i in range(n):
        if i + 1 < n:
            if i >= 1:
                wr(i - 1).wait()
            rd(i + 1).start()
        rd(i).wait()
        wr(i).start()
    for i in range(max(n - 2, 0), n):
        wr(i).wait()


def _all_gather_2level(shards, name):
    n = len(shards)
    chunks = [_pick(s.shape[0], (416, 512, 256, 160, 128, 64, 16)) for s in shards]

    def body(*refs):
        ins, outs = refs[:n], refs[n:2 * n]
        send_sems, recv_sems, in_sems, out_sems = refs[2 * n:2 * n + 4]
        bufs = refs[2 * n + 4:]
        x, y, c, me = _position()
        sib, xn, yn, dg = (x, y, 1 - c), (1 - x, y, c), (x, 1 - y, c), (1 - x, 1 - y, c)

        def cp(a, k, src, slot, to):
            return pltpu.make_async_remote_copy(src_ref=src, dst_ref=outs[a].at[slot], send_sem=send_sems.at[a, k],
                                                recv_sem=recv_sems.at[a, k], device_id=to, device_id_type=MESH)

        for a in range(n):
            for k, to in ((0, sib), (1, xn), (2, yn)):
                cp(a, k, ins[a], me, to).start()
        for a in range(n):
            cp(a, 1, ins[a], _lin(xn), xn).wait_recv()
            cp(a, 3, outs[a].at[_lin(xn)], _lin(xn), sib).start()

            @pl.when(c == 0)
            def _():
                cp(a, 5, outs[a].at[_lin(xn)], _lin(xn), yn).start()

            cp(a, 2, ins[a], _lin(yn), yn).wait_recv()
            cp(a, 4, outs[a].at[_lin(yn)], _lin(yn), sib).start()

            @pl.when(c == 1)
            def _():
                cp(a, 5, outs[a].at[_lin(yn)], _lin(yn), xn).start()

        for a in range(n):
            cp(a, 5, ins[a], _lin(dg), xn).wait_recv()
            cp(a, 6, outs[a].at[_lin(dg)], _lin(dg), sib).start()
        for a in range(n):
            _staged_copy(ins[a], outs[a].at[me], bufs[a], in_sems.at[a], out_sems.at[a], shards[a].shape[0], chunks[a])
        for a in range(n):
            for k, origin in ((0, sib), (3, (1 - x, y, 1 - c)), (4, (x, 1 - y, 1 - c)), (6, (1 - x, 1 - y, 1 - c))):
                cp(a, k, ins[a], _lin(origin), sib).wait_recv()
            for k in range(7):
                cp(a, k, ins[a], me, sib).wait_send()

    any_spec = pl.BlockSpec(memory_space=pl.ANY)
    return pl.pallas_call(
        body, name=name, out_shape=[jax.ShapeDtypeStruct((N_DEV,) + s.shape, s.dtype) for s in shards],
        in_specs=[any_spec] * n, out_specs=[any_spec] * n,
        scratch_shapes=[pltpu.SemaphoreType.DMA((n, 7)), pltpu.SemaphoreType.DMA((n, 7)),
                        pltpu.SemaphoreType.DMA((n, 2)), pltpu.SemaphoreType.DMA((n, 2))]
        + [pltpu.VMEM((2, ch, s.shape[1]), s.dtype) for ch, s in zip(chunks, shards)],
    )(*shards)


def _plane_pos(x, y, q):
    return ((1 - x) if q & 2 else x, (1 - y) if q & 1 else y)


ANY_SPEC = pl.BlockSpec(memory_space=pl.ANY)


def _mm(a, b, name, ta=False, tb=False, out_dtype=f32, after=()):
    if ta:
        k_dim, m_dim = a.shape
    else:
        m_dim, k_dim = a.shape
    if tb:
        n_dim, k2 = b.shape
    else:
        k2, n_dim = b.shape
    assert k_dim == k2, (a.shape, b.shape)
    assert a.dtype == bf16 and b.dtype == bf16
    bm = _pick(m_dim, (512, 768, 640, 256, 128))
    bn = _pick(n_dim, (512, 640, 256, 128))
    bk = k_dim if k_dim <= 2560 else _pick(k_dim, (1024, 1280, 768, 512))
    nk = k_dim // bk
    a_spec = (pl.BlockSpec((bk, bm), lambda i, j, k: (k, i)) if ta
              else pl.BlockSpec((bm, bk), lambda i, j, k: (i, k)))
    b_spec = (pl.BlockSpec((bn, bk), lambda i, j, k: (j, k)) if tb
              else pl.BlockSpec((bk, bn), lambda i, j, k: (k, j)))
    dims = (((0 if ta else 1,), (1 if tb else 0,)), ((), ()))

    n_after = len(after)

    def body_single(a_ref, b_ref, *rest):
        o_ref = rest[n_after]
        o_ref[...] = lax.dot_general(a_ref[...], b_ref[...], dims, preferred_element_type=f32).astype(o_ref.dtype)

    def body(a_ref, b_ref, *rest):
        o_ref, acc_ref = rest[n_after:]
        k = pl.program_id(2)

        @pl.when(k == 0)
        def _():
            acc_ref[...] = jnp.zeros_like(acc_ref)

        acc_ref[...] += lax.dot_general(a_ref[...], b_ref[...], dims, preferred_element_type=f32)

        @pl.when(k == nk - 1)
        def _():
            o_ref[...] = acc_ref[...].astype(o_ref.dtype)

    return pl.pallas_call(
        body_single if nk == 1 else body, name=name, out_shape=jax.ShapeDtypeStruct((m_dim, n_dim), out_dtype),
        grid=(m_dim // bm, n_dim // bn, nk), in_specs=[a_spec, b_spec] + [ANY_SPEC] * n_after,
        out_specs=pl.BlockSpec((bm, bn), lambda i, j, k: (i, j)),
        scratch_shapes=[] if nk == 1 else [pltpu.VMEM((bm, bn), f32)],
        compiler_params=_cparams("parallel", "parallel", "arbitrary"),
    )(a, b, *after)


def _rin(arr, width=None, cb=0, roff=0):
    return (arr, arr.shape[1] if width is None else width, cb, roff)


def _rowcall(fn, name, rows, tm, row_ins, par_ins, row_outs, acc_outs=(), after=()):
    nr, npar, nro, n_after = len(row_ins), len(par_ins), len(row_outs), len(after)
    in_specs, args = [], []
    for arr, width, cb, roff in row_ins:
        if roff >= 0:
            imap = lambda i, cb=cb, roff=roff: (i + roff, cb)
        else:
            imap = lambda i, cb=cb, roff=roff: (jnp.maximum(i + roff, 0), cb)
        in_specs.append(pl.BlockSpec((tm, width), imap))
        args.append(arr)
    for p in par_ins:
        in_specs.append(pl.BlockSpec(p.shape, lambda i: (0, 0)))
        args.append(p)
    out_shape, out_specs = [], []
    for width, dt in row_outs:
        out_shape.append(jax.ShapeDtypeStruct((rows, width), dt))
        out_specs.append(pl.BlockSpec((tm, width), lambda i: (i, 0)))
    for p, width in acc_outs:
        out_shape.append(jax.ShapeDtypeStruct((p, width), f32))
        out_specs.append(pl.BlockSpec((p, width), lambda i: (0, 0)))

    def body(*refs):
        i = pl.program_id(0)
        res = fn(i, *[r[...] for r in refs[:nr + npar]])
        outs = refs[nr + npar + n_after:]
        for o, v in zip(outs[:nro], res[:nro]):
            o[...] = v.astype(o.dtype)
        if acc_outs:
            @pl.when(i == 0)
            def _():
                for o in outs[nro:]:
                    o[...] = jnp.zeros_like(o)

            for o, v in zip(outs[nro:], res[nro:]):
                o[...] += v

    return pl.pallas_call(
        body, name=name, out_shape=out_shape, grid=(rows // tm,), in_specs=in_specs + [ANY_SPEC] * n_after,
        out_specs=out_specs, compiler_params=_cparams("arbitrary"),
    )(*args, *after)


def _rms(x, g):
    return x * lax.rsqrt(jnp.mean(x * x, axis=-1, keepdims=True) + EPS) * g


def _normmod(x, g, sc, sh):
    return _rms(x, g) * (1.0 + sc) + sh


def _gelu(x):
    return 0.5 * x * (1.0 + jnp.tanh(0.7978845608028654 * (x + 0.044715 * (x * x * x))))


def _sigmoid(x):
    return 0.5 * (jnp.tanh(0.5 * x) + 1.0)


def _coeff_parts(pre_a, pre_x, ba, bx, lam):
    r = _sigmoid(pre_a + ba)
    ig = _sigmoid(pre_x + bx)
    nl = -lam
    sp = jnp.maximum(nl, 0.0) + jnp.log(1.0 + jnp.exp(-jnp.abs(nl)))
    la = -RG_C * r * sp
    a = jnp.exp(la)
    one_minus_a2 = -jnp.tanh(la) * (a * a + 1.0)
    inv_m = lax.rsqrt(one_minus_a2)
    return r, ig, sp, a, one_minus_a2 * inv_m, inv_m


def _coeff(pre_a, pre_x, u, ba, bx, lam):
    _, ig, _, a, m, _ = _coeff_parts(pre_a, pre_x, ba, bx, lam)
    return a, m * (ig * u)


def _coeff_bwd(pre_a, pre_x, u, ba, bx, lam, da, db):
    r, ig, sp, a, m, inv_m = _coeff_parts(pre_a, pre_x, ba, bx, lam)
    dbu = db * u
    dig = dbu * m
    dm = dbu * ig
    dla = a * (da - dm * a * inv_m)
    dpa = dla * (-RG_C * sp) * (r * (1.0 - r))
    dpx = dig * (ig * (1.0 - ig))
    dsp = jnp.sum(dla * (-RG_C * r), axis=0, keepdims=True)
    dlam = -dsp * _sigmoid(-lam)
    return (dpa, dpx, db * m * ig, jnp.sum(dpa, axis=0, keepdims=True), jnp.sum(dpx, axis=0, keepdims=True), dlam)


SCAN_CHUNK = 256


def _scan_call(a, v, chunk_of, reverse, name, backward, after=()):
    rows, width = a.shape
    n_out = 1 if backward else 2
    nt = SCAN_CHUNK // 8

    def body(a_ref, v_ref, *rest):
        outs, state_ref = rest[len(after):-1], rest[-1]

        @pl.when(pl.program_id(0) == 0)
        def _():
            state_ref[...] = jnp.zeros_like(state_ref)

        rid = lax.broadcasted_iota(jnp.int32, (8, width), 0)
        last_row = 0 if reverse else 7

        def shift(x, s, fill):
            rolled = pltpu.roll(x, (8 - s) if reverse else s, axis=0)
            return jnp.where((rid >= 8 - s) if reverse else (rid < s), fill, rolled)

        def tile(j, st):
            t0 = pl.multiple_of((nt - 1 - j if reverse else j) * 8, 8)
            at = a_ref[pl.ds(t0, 8), :]
            coef = shift(at, 1, 1.0) if backward else at
            acc = v_ref[pl.ds(t0, 8), :]
            for s in (1, 2, 4):
                acc = coef * shift(acc, s, 0.0) + acc
                coef = coef * shift(coef, s, 1.0)
            out = coef * st + acc
            outs[0][pl.ds(t0, 8), :] = out
            last = out[last_row:last_row + 1]
            if backward:
                return at[last_row:last_row + 1] * last
            outs[1][pl.ds(t0, 8), :] = shift(out, 1, st)
            return last

        state_ref[0:1, :] = lax.fori_loop(0, nt, tile, state_ref[0:1, :])

    spec = pl.BlockSpec((SCAN_CHUNK, width), lambda t: (chunk_of(t), 0))
    return pl.pallas_call(
        body, name=name, out_shape=[jax.ShapeDtypeStruct((rows, width), f32)] * n_out,
        grid=(rows // SCAN_CHUNK,), in_specs=[spec, spec] + [ANY_SPEC] * len(after), out_specs=[spec] * n_out,
        scratch_shapes=[pltpu.VMEM((8, width), f32)],
        compiler_params=_cparams("arbitrary"),
    )(a, v, *after)


CONV_CHUNK = 256


def _fill_padded(pad_ref, src_ref, start, n):
    cb = pad_ref.shape[1]
    pad_ref[pl.ds(0, HALO), :] = jnp.zeros((HALO, cb), f32)
    pad_ref[pl.ds(HALO, n), :] = src_ref[pl.ds(start, n), :].astype(f32)
    pad_ref[pl.ds(HALO + n, HALO), :] = jnp.zeros((HALO, cb), f32)


def _dwconv_fwd(x, x_cb0, w, b, taps, pad_left, segments, cb, name, emit_bf16):
    rows = x.shape[0]
    width = w.shape[1]

    def body(x_ref, w_ref, b_ref, *rest):
        outs, xp = rest[:-1], rest[-1]
        for start, n in segments:
            _fill_padded(xp, x_ref, start, n)
            for c0 in range(0, n, CONV_CHUNK):
                acc = jnp.zeros((CONV_CHUNK, cb), f32) + b_ref[...]
                for k in range(taps):
                    acc = acc + w_ref[k:k + 1, :] * xp[pl.ds(HALO + c0 + k - pad_left, CONV_CHUNK), :]
                for o in outs:
                    o[pl.ds(start + c0, CONV_CHUNK), :] = acc.astype(o.dtype)

    out_dtypes = [f32, bf16] if emit_bf16 else [f32]
    return pl.pallas_call(
        body, name=name, out_shape=[jax.ShapeDtypeStruct((rows, width), dt) for dt in out_dtypes],
        grid=(width // cb,),
        in_specs=[pl.BlockSpec((rows, cb), lambda j: (0, j + x_cb0)), pl.BlockSpec((taps, cb), lambda j: (0, j)),
                  pl.BlockSpec((1, cb), lambda j: (0, j))],
        out_specs=[pl.BlockSpec((rows, cb), lambda j: (0, j))] * len(out_dtypes),
        scratch_shapes=[pltpu.VMEM((rows + 2 * HALO, cb), f32)],
        compiler_params=_cparams("parallel"),
    )(x, w, b)


def _dwconv_bwd(douts, x, x_cb0, w, taps, pad_left, segments, cb, name, dx_dtype):
    rows = x.shape[0]
    width = w.shape[1]
    nd = len(douts)

    def body(*refs):
        d_refs, x_ref, w_ref = refs[:nd], refs[nd], refs[nd + 1]
        dx_ref, dw_ref, db_ref, dp, dsum = refs[nd + 2:]
        dw_ref[...] = jnp.zeros_like(dw_ref)
        db_ref[...] = jnp.zeros_like(db_ref)
        if nd > 1:
            total = d_refs[0][...]
            for r in d_refs[1:]:
                total = total + r[...]
            dsum[...] = total
            d_ref = dsum
        else:
            d_ref = d_refs[0]
        for start, n in segments:
            _fill_padded(dp, d_ref, start, n)
            for c0 in range(0, n, CONV_CHUNK):
                db_ref[...] += jnp.sum(dp[pl.ds(HALO + c0, CONV_CHUNK), :], axis=0, keepdims=True)
                xchunk = x_ref[pl.ds(start + c0, CONV_CHUNK), :].astype(f32)
                acc = jnp.zeros((CONV_CHUNK, cb), f32)
                for k in range(taps):
                    shifted = dp[pl.ds(HALO + c0 + pad_left - k, CONV_CHUNK), :]
                    acc = acc + w_ref[k:k + 1, :] * shifted
                    dw_ref[k:k + 1, :] += jnp.sum(shifted * xchunk, axis=0, keepdims=True)
                dx_ref[pl.ds(start + c0, CONV_CHUNK), :] = acc.astype(dx_ref.dtype)

    dspec = pl.BlockSpec((rows, cb), lambda j: (0, j))
    return pl.pallas_call(
        body, name=name,
        out_shape=[jax.ShapeDtypeStruct((rows, width), dx_dtype), jax.ShapeDtypeStruct((taps, width), f32),
                   jax.ShapeDtypeStruct((1, width), f32)],
        grid=(width // cb,),
        in_specs=[dspec] * nd + [pl.BlockSpec((rows, cb), lambda j: (0, j + x_cb0)),
                                 pl.BlockSpec((taps, cb), lambda j: (0, j))],
        out_specs=[dspec, pl.BlockSpec((taps, cb), lambda j: (0, j)), pl.BlockSpec((1, cb), lambda j: (0, j))],
        scratch_shapes=[pltpu.VMEM((rows + 2 * HALO, cb), f32), pltpu.VMEM((rows, cb), f32)],
        compiler_params=_cparams("parallel"),
    )(*douts, x, w)


def _ada_forward(c16, w_ada, b_loc):
    def body(c_ref, w_ref, b_ref, o_ref):
        cv = c_ref[...]
        s = (cv * _sigmoid(cv)).astype(bf16)
        o_ref[0] = jnp.dot(s, w_ref[0].astype(bf16), preferred_element_type=f32) + b_ref[0]

    return pl.pallas_call(
        body, name="ada_forward", out_shape=jax.ShapeDtypeStruct((2, 16, ADA_SHARD), f32), grid=(2,),
        in_specs=[pl.BlockSpec((16, D), lambda l: (0, 0)), pl.BlockSpec((1, D, ADA_SHARD), lambda l: (l, 0, 0)),
                  pl.BlockSpec((1, 1, ADA_SHARD), lambda l: (l, 0, 0))],
        out_specs=pl.BlockSpec((1, 16, ADA_SHARD), lambda l: (l, 0, 0)),
        compiler_params=_cparams("parallel"),
    )(c16, w_ada, b_loc)


def _ada_backward(c16, g16, w_ada):
    def body(c_ref, g_ref, w_ref, dw_ref, ds_ref):
        cv = c_ref[...]
        s = (cv * _sigmoid(cv)).astype(bf16)
        g = g_ref[0].astype(bf16)
        dw_ref[0] = lax.dot_general(s, g, (((0,), (0,)), ((), ())), preferred_element_type=f32)
        ds = lax.dot_general(g, w_ref[0].astype(bf16), (((1,), (1,)), ((), ())), preferred_element_type=f32)
        cc = cv[8:9]
        sg = _sigmoid(cc)
        dsilu = sg * (1.0 + cc * (1.0 - sg))
        ds_ref[0] = jnp.zeros((8, D), f32) + jnp.sum(ds[8:16], axis=0, keepdims=True) * dsilu

    return pl.pallas_call(
        body, name="ada_backward",
        out_shape=[jax.ShapeDtypeStruct((2, D, ADA_SHARD), f32), jax.ShapeDtypeStruct((2, 8, D), f32)], grid=(2,),
        in_specs=[pl.BlockSpec((16, D), lambda l: (0, 0)), pl.BlockSpec((1, 16, ADA_SHARD), lambda l: (l, 0, 0)),
                  pl.BlockSpec((1, D, ADA_SHARD), lambda l: (l, 0, 0))],
        out_specs=[pl.BlockSpec((1, D, ADA_SHARD), lambda l: (l, 0, 0)), pl.BlockSpec((1, 8, D), lambda l: (l, 0, 0))],
        compiler_params=_cparams("parallel"),
    )(c16, g16, w_ada)


def _adamw(pieces, w, m, v, name, after=()):
    rows, cols = w.shape
    n_arr, n_after = len(pieces), len(after)
    counts = [cnt for _, cnt in pieces]
    pieces = [p for p, _ in pieces]
    tm = 256 if (rows % 256 == 0 and rows > 256) else rows

    def body(*refs):
        p_refs = refs[:n_arr]
        w_ref, m_ref, v_ref = refs[n_arr:n_arr + 3]
        g_ref, d_ref, nm_ref, nv_ref = refs[n_arr + 3 + n_after:]
        g = None
        for p_ref in p_refs:
            for j in range(p_ref.shape[0]):
                term = p_ref[j].astype(f32)
                g = term if g is None else g + term
        m2 = ADAM_B1 * m_ref[...] + (1.0 - ADAM_B1) * g
        v2 = ADAM_B2 * v_ref[...] + (1.0 - ADAM_B2) * (g * g)
        m_hat = m2 / (1.0 - ADAM_B1 ** ADAM_STEP)
        v_hat = v2 / (1.0 - ADAM_B2 ** ADAM_STEP)
        g_ref[...] = g
        d_ref[...] = -ADAM_LR * (m_hat / (jnp.sqrt(v_hat) + ADAM_EPS) + ADAM_WD * w_ref[...])
        nm_ref[...] = m2
        nv_ref[...] = v2

    spec = pl.BlockSpec((tm, cols), lambda i: (i, 0))
    return pl.pallas_call(
        body, name=name, out_shape=[jax.ShapeDtypeStruct((rows, cols), f32)] * 4, grid=(rows // tm,),
        in_specs=[pl.BlockSpec((cnt, tm, cols), lambda i: (0, i, 0)) for cnt in counts] + [spec, spec, spec]
        + [ANY_SPEC] * n_after,
        out_specs=[spec] * 4, compiler_params=_cparams("parallel"),
    )(*pieces, w, m, v, *after)


MLP_TM = 256
FB = F // N_DEV


def _stack_rows(vals, n):
    cols = vals[0].shape[1]
    rid = lax.broadcasted_iota(jnp.int32, (n, cols), 0)
    out = jnp.zeros((n, cols), f32)
    for k, v in enumerate(vals):
        out = jnp.where(rid == k, v, out)
    return out


N_MLP_PARAMS = 9


class _ParamRows:
    def __init__(self, ref):
        self.ref = ref

    def __getitem__(self, sl):
        return self.ref[8 * sl.start:8 * sl.start + 1, :]


def _resident(shape, imap):
    return pl.BlockSpec(shape, imap, pipeline_mode=pl.Buffered(1))


def _mlp_forward(xa, xa_roff, out_prev, par, w_in, w_out, layer, name):
    def body(xa_ref, op_ref, par_ref, win_ref, wout_ref, x1_ref, h_ref, r_ref, mo_ref, x2_ref, hn_ref):
        p = _ParamRows(par_ref)
        x1 = xa_ref[...] + p[0:1] * (op_ref[...] + p[1:2])
        h = _normmod(x1, p[2:3], p[3:4], p[4:5]).astype(bf16)
        x1_ref[...] = x1
        h_ref[...] = h
        mo = jnp.zeros((MLP_TM, D), f32)
        for j in range(N_DEV):
            r = jnp.maximum(jnp.dot(h, win_ref[j], preferred_element_type=f32), 0.0)
            r_ref[:, j * FB:(j + 1) * FB] = r.astype(bf16)
            mo = mo + jnp.dot((r * r).astype(bf16), wout_ref[j], preferred_element_type=f32)
        mo_ref[...] = mo.astype(bf16)
        x2 = x1 + p[5:6] * mo
        x2_ref[...] = x2
        hn_ref[...] = _normmod(x2, p[6:7], p[7:8], p[8:9]).astype(bf16)

    row = lambda width: pl.BlockSpec((MLP_TM, width), lambda i: (i, 0))
    return pl.pallas_call(
        body, name=name, grid=(T_LAT // MLP_TM,),
        out_shape=[jax.ShapeDtypeStruct((T_LAT, D), f32), jax.ShapeDtypeStruct((T_LAT, D), bf16),
                   jax.ShapeDtypeStruct((T_LAT, F), bf16), jax.ShapeDtypeStruct((T_LAT, D), bf16),
                   jax.ShapeDtypeStruct((T_LAT, D), f32), jax.ShapeDtypeStruct((T_LAT, D), bf16)],
        in_specs=[pl.BlockSpec((MLP_TM, D), lambda i: (i + xa_roff, 0)), row(D), pl.BlockSpec((8 * N_MLP_PARAMS, D), lambda i: (0, 0)),
                  _resident((N_DEV, None, D, FB), lambda i: (0, layer, 0, 0)),
                  _resident((N_DEV, None, FB, D), lambda i: (0, layer, 0, 0))],
        out_specs=[row(D), row(D), row(F), row(D), row(D), row(D)],
        compiler_params=_cparams("parallel"),
    )(xa, out_prev, par, w_in, w_out)


def _mlp_backward(dx2, x1, r, mo, out_prev, par, w_in, w_out, layer, name, after=()):
    nt = (((1,), (1,)), ((), ()))

    n_after = len(after)

    def body(dx2_ref, x1_ref, r_ref, mo_ref, op_ref, par_ref, win_ref, wout_ref, *rest):
        dx1_ref, dop_ref, dmo_ref, dhid_ref, acc_ref = rest[n_after:]
        p = _ParamRows(par_ref)
        dx2v = dx2_ref[...]
        dmo = (p[5:6] * dx2v).astype(bf16)
        dmo_ref[...] = dmo
        dh = jnp.zeros((MLP_TM, D), f32)
        mo = mo_ref[...].astype(f32)
        for j in range(N_DEV):
            rf = r_ref[:, j * FB:(j + 1) * FB].astype(f32)
            dact = lax.dot_general(dmo, wout_ref[j], nt, preferred_element_type=f32)
            dhid = (dact * (2.0 * rf)).astype(bf16)
            dhid_ref[:, j * FB:(j + 1) * FB] = dhid
            dh = dh + lax.dot_general(dhid, win_ref[j], nt, preferred_element_type=f32)
        x1 = x1_ref[...]
        _, vjp = jax.vjp(_normmod, x1, p[2:3], p[3:4], p[4:5])
        dx, dng, dsc, dsh = vjp(dh)
        dx1 = dx2v + dx
        dx1_ref[...] = dx1
        dop_ref[...] = (p[0:1] * dx1).astype(bf16)
        sums = _stack_rows([jnp.sum(dx1 * (op_ref[...] + p[1:2]), axis=0, keepdims=True),
                            p[0:1] * jnp.sum(dx1, axis=0, keepdims=True), dng, dsc, dsh,
                            jnp.sum(dx2v * mo, axis=0, keepdims=True)], 8)

        @pl.when(pl.program_id(0) == 0)
        def _():
            acc_ref[...] = jnp.zeros_like(acc_ref)

        acc_ref[...] += sums

    row = lambda width: pl.BlockSpec((MLP_TM, width), lambda i: (i, 0))
    return pl.pallas_call(
        body, name=name, grid=(T_LAT // MLP_TM,),
        out_shape=[jax.ShapeDtypeStruct((T_LAT, D), f32), jax.ShapeDtypeStruct((T_LAT, D), bf16),
                   jax.ShapeDtypeStruct((T_LAT, D), bf16), jax.ShapeDtypeStruct((T_LAT, F), bf16),
                   jax.ShapeDtypeStruct((8, D), f32)],
        in_specs=[row(D), row(D), row(F), row(D), row(D), pl.BlockSpec((8 * N_MLP_PARAMS, D), lambda i: (0, 0)),
                  _resident((N_DEV, None, D, FB), lambda i: (0, layer, 0, 0)),
                  _resident((N_DEV, None, FB, D), lambda i: (0, layer, 0, 0))] + [ANY_SPEC] * n_after,
        out_specs=[row(D), row(D), row(D), row(F), pl.BlockSpec((8, D), lambda i: (0, 0))],
        compiler_params=_cparams("arbitrary"),
    )(dx2, x1, r, mo, out_prev, par, w_in, w_out, *after)


def _mlp_weight_grads(h, dhid, r, dmo, layer, other, tag):
    tn = (((0,), (0,)), ((), ()))

    def body_in(h_ref, dhid_ref, *rest):
        rest[-1][...] = lax.dot_general(h_ref[...], dhid_ref[...], tn, preferred_element_type=f32).astype(bf16)

    def body_out(r_ref, dmo_ref, *rest):
        rf = r_ref[...].astype(f32)
        rest[-1][...] = lax.dot_general((rf * rf).astype(bf16), dmo_ref[...], tn,
                                        preferred_element_type=f32).astype(bf16)

    def call(body, name, operands, specs, block, prev):
        extra = [] if prev is None else [prev]
        return pl.pallas_call(
            body, name=name, grid=(N_DEV,), out_shape=jax.ShapeDtypeStruct((N_DEV, 2) + block, bf16),
            in_specs=specs + [pl.BlockSpec(memory_space=pl.ANY)] * len(extra),
            out_specs=pl.BlockSpec((None, None) + block, lambda j: (j, layer, 0, 0)),
            input_output_aliases={} if prev is None else {2: 0},
            compiler_params=_cparams("parallel"),
        )(*operands, *extra)

    dw_in = call(body_in, tag + "_mlp_in_dw", [h, dhid],
                 [_resident((T_LAT, D), lambda j: (0, 0)), pl.BlockSpec((T_LAT, FB), lambda j: (0, j))], (D, FB),
                 None if other is None else other[0])
    dw_out = call(body_out, tag + "_mlp_out_dw", [r, dmo],
                  [pl.BlockSpec((T_LAT, FB), lambda j: (0, j)), _resident((T_LAT, D), lambda j: (0, 0))], (FB, D),
                  None if other is None else other[1])
    return dw_in, dw_out


def _pos_embed():
    n_rows = T_LAT // GRID_W
    q = D // 4
    omega = 1.0 / (POS_BASE ** (jnp.arange(q, dtype=f32) / q))
    er = jnp.arange(n_rows, dtype=jnp.int32).astype(f32)[:, None] * omega[None, :]
    ec = jnp.arange(GRID_W, dtype=jnp.int32).astype(f32)[:, None] * omega[None, :]
    by_row = jnp.concatenate([jnp.sin(er), jnp.cos(er)], axis=-1)[:, None, :]
    by_col = jnp.concatenate([jnp.sin(ec), jnp.cos(ec)], axis=-1)[None, :, :]
    full = jnp.concatenate([jnp.broadcast_to(by_row, (n_rows, GRID_W, D // 2)),
                            jnp.broadcast_to(by_col, (n_rows, GRID_W, D // 2))], axis=-1)
    return full.reshape(T_LAT, D)


HALF = R // 2
BLK_PER_HALF = N_BLK // 2
N_PARTS = 4


def _gate_matrix(w_a, w_x):
    eye = jnp.eye(BLK_PER_HALF, dtype=bf16)
    cols = []
    for h in range(2):
        for d in range(2):
            for w in (w_a, w_x):
                blocks = w[d, BLK_PER_HALF * h:BLK_PER_HALF * (h + 1)].astype(bf16)
                cols.append(jnp.einsum("hij,hg->higj", blocks, eye).reshape(HALF, HALF))
    return jnp.concatenate(cols, axis=1)


def _gate_blocks(dwg, part):
    out = []
    for h in range(2):
        blk = dwg[:, (N_PARTS * h + part) * HALF:(N_PARTS * h + part + 1) * HALF]
        blk = blk.reshape(BLK_PER_HALF, BLK, BLK_PER_HALF, BLK)
        out.append(jnp.moveaxis(jnp.diagonal(blk, axis1=0, axis2=2), -1, 0))
    return jnp.concatenate(out, axis=0)


GATE_BM = 768


def _gates_dx(dpre, wg, after=()):
    rows = dpre.shape[0]
    n_after = len(after)

    def body(d_ref, w_ref, *rest):
        rest[n_after][...] = lax.dot_general(d_ref[...], w_ref[...], (((1,), (1,)), ((), ())),
                                             preferred_element_type=f32)

    return pl.pallas_call(
        body, name="l0_gates_dx", grid=(rows // GATE_BM, 2), out_shape=jax.ShapeDtypeStruct((rows, R), f32),
        in_specs=[pl.BlockSpec((GATE_BM, N_PARTS * HALF), lambda i, h: (i, h)),
                  pl.BlockSpec((HALF, N_PARTS * HALF), lambda i, h: (0, h))] + [ANY_SPEC] * n_after,
        out_specs=pl.BlockSpec((GATE_BM, HALF), lambda i, h: (i, h)),
        compiler_params=_cparams("parallel", "parallel"),
    )(dpre, wg, *after)


COEFF_TM = 256


def _dir_params(d, *params):
    specs = [pl.BlockSpec((None, 1, HALF), lambda h, i: (d, 0, h))] * len(params)
    return specs, [p.reshape(2, 1, R) for p in params]


def _gates_coeff_fwd(ub, u, wg, ba, bx, lam, d):
    rows = u.shape[0]

    def body(ub_ref, u_ref, w_ref, ba_ref, bx_ref, lam_ref, a_ref, b_ref):
        pre = jnp.dot(ub_ref[...], w_ref[...], preferred_element_type=f32)
        a, b = _coeff(pre[:, :HALF], pre[:, HALF:], u_ref[...], ba_ref[...], bx_ref[...], lam_ref[...])
        a_ref[...] = a
        b_ref[...] = b

    tile = pl.BlockSpec((COEFF_TM, HALF), lambda h, i: (i, h))
    pspecs, pargs = _dir_params(d, ba, bx, lam)
    return pl.pallas_call(
        body, name=f"l0_gates_coeff_{d}", grid=(2, rows // COEFF_TM),
        out_shape=[jax.ShapeDtypeStruct((rows, R), f32)] * 2,
        in_specs=[tile, tile, pl.BlockSpec((HALF, 2 * HALF), lambda h, i: (0, 2 * h + d))] + pspecs,
        out_specs=[tile, tile], compiler_params=_cparams("parallel", "parallel"),
    )(ub, u, wg, *pargs)


def _gates_coeff_bwd(ub, u, dh, yp, wg, ba, bx, lam, d, dpre_prev):
    rows = u.shape[0]
    n_prev = 0 if dpre_prev is None else 1

    def body(ub_ref, u_ref, dh_ref, yp_ref, w_ref, ba_ref, bx_ref, lam_ref, *rest):
        dpre_ref, du_ref, dba_ref, dbx_ref, dlam_ref = rest[n_prev:]
        pre = jnp.dot(ub_ref[...], w_ref[...], preferred_element_type=f32)
        dhv = dh_ref[...]
        dpa, dpx, du, dba, dbx, dlam = _coeff_bwd(pre[:, :HALF], pre[:, HALF:], u_ref[...], ba_ref[...], bx_ref[...],
                                                  lam_ref[...], dhv * yp_ref[...], dhv)
        dpre_ref[:, :HALF] = dpa.astype(bf16)
        dpre_ref[:, HALF:] = dpx.astype(bf16)
        du_ref[...] = du

        @pl.when(pl.program_id(1) == 0)
        def _():
            dba_ref[...] = jnp.zeros_like(dba_ref)
            dbx_ref[...] = jnp.zeros_like(dbx_ref)
            dlam_ref[...] = jnp.zeros_like(dlam_ref)

        dba_ref[...] += dba
        dbx_ref[...] += dbx
        dlam_ref[...] += dlam

    tile = pl.BlockSpec((COEFF_TM, HALF), lambda h, i: (i, h))
    acc = pl.BlockSpec((1, HALF), lambda h, i: (0, h))
    pspecs, pargs = _dir_params(d, ba, bx, lam)
    extra = [] if dpre_prev is None else [dpre_prev]
    return pl.pallas_call(
        body, name=f"l0_gates_coeff_bwd_{d}", grid=(2, rows // COEFF_TM),
        out_shape=[jax.ShapeDtypeStruct((rows, 2 * N_PARTS * HALF), bf16), jax.ShapeDtypeStruct((rows, R), f32)]
        + [jax.ShapeDtypeStruct((1, R), f32)] * 3,
        in_specs=[tile] * 4 + [pl.BlockSpec((HALF, 2 * HALF), lambda h, i: (0, 2 * h + d))] + pspecs
        + [ANY_SPEC] * n_prev,
        out_specs=[pl.BlockSpec((COEFF_TM, 2 * HALF), lambda h, i: (i, 2 * h + d)), tile, acc, acc, acc],
        input_output_aliases={8: 0} if n_prev else {}, compiler_params=_cparams("parallel", "arbitrary"),
    )(ub, u, dh, yp, wg, *pargs, *extra)


def _gates_dw(u, dpre):
    rows = u.shape[0]

    def body(u_ref, d_ref, o_ref):
        o_ref[...] = lax.dot_general(u_ref[...], d_ref[...], (((0,), (0,)), ((), ())), preferred_element_type=f32)

    return pl.pallas_call(
        body, name="l0_gates_dw", grid=(2 * N_PARTS,), out_shape=jax.ShapeDtypeStruct((HALF, 2 * N_PARTS * HALF), f32),
        in_specs=[pl.BlockSpec((rows, HALF), lambda j: (0, j // N_PARTS)), pl.BlockSpec((rows, HALF), lambda j: (0, j))],
        out_specs=pl.BlockSpec((HALF, HALF), lambda j: (0, j)), compiler_params=_cparams("parallel"),
    )(u, dpre)


N_SCAN_CHUNKS = T_ALL // SCAN_CHUNK
SCAN_FWD = lambda t: t
SCAN_FWD_BWD = lambda t: N_SCAN_CHUNKS - 1 - t
SCAN_REV = lambda t: jnp.where(t == 0, 0, N_SCAN_CHUNKS - t)
SCAN_REV_BWD = lambda t: jnp.where(t == N_SCAN_CHUNKS - 1, 0, t + 1)
CONV_SEGMENTS = ((0, T_CTX), (T_CTX, T_LAT))
TM = 128
FUSED_TM = 256


def _local_step(x, ctx, target, mods, cmod, wts, late_weights, send_grads, reduce_loss, start_after=()):
    sh1, sc1, g1, sh2, sc2, g2 = [[mods[l, i][None] for l in range(2)] for i in range(N_MOD)]
    ng = wts["norm_g"]
    xcat = jnp.concatenate([ctx, x], axis=0)
    poscat = jnp.concatenate([jnp.zeros((T_CTX, D), f32), _pos_embed()], axis=0)
    scp = jnp.concatenate([cmod[1][None], sc1[0]], axis=0)
    shp = jnp.concatenate([cmod[0][None], sh1[0]], axis=0)

    ctx_tiles = T_CTX // FUSED_TM
    nt = (((1,), (1,)), ((), ()))

    def blend(i, p):
        sel = jnp.where(i < ctx_tiles, 1.0, 0.0)
        return sel * p[0:1] + (1.0 - sel) * p[1:2]

    def f_pre0(i, xc, pos, g, scp_, shp_, w):
        x0 = xc + pos
        h = _normmod(x0, g, blend(i, scp_), blend(i, shp_)).astype(bf16)
        return x0, h, jnp.dot(h, w, preferred_element_type=f32)

    x0cat, h0, gr = _rowcall(f_pre0, "l0_prenorm_in_proj", T_ALL, FUSED_TM, [_rin(xcat), _rin(poscat)],
                             [ng[0, 0][None], scp, shp, wts["rec_w_in"]], [(D, f32), (D, bf16), (2 * R, f32)],
                             after=start_after)
    u, ub = _dwconv_fwd(gr, R // 256, wts["rec_conv_w"], wts["rec_conv_b"], 4, 1, CONV_SEGMENTS, 256,
                        "l0_conv", True)
    gate_args = (wts["gates"], wts["rec_b_a"], wts["rec_b_x"], wts["rec_lambda"])
    a0, b0 = _gates_coeff_fwd(ub, u, *gate_args, 0)
    a1, b1 = _gates_coeff_fwd(ub, u, *gate_args, 1)
    halfway = late_weights("mlp_halfway", a1)
    y0, yp0 = _scan_call(a0, b0, SCAN_FWD, False, "l0_scan_fwd", False, after=[halfway])
    y1, yp1 = _scan_call(a1, b1, SCAN_REV, True, "l0_scan_rev", False)

    wts = dict(wts, **late_weights("mlp", y1))

    def f_gate_out(i, gp, y0_, y1_, w):
        z = (_gelu(gp) * (y0_ + y1_)).astype(bf16)
        return z, jnp.dot(z, w, preferred_element_type=f32)

    zb, out0 = _rowcall(f_gate_out, "l0_gate_out_proj", T_LAT, FUSED_TM,
                        [_rin(gr, R, 0, ctx_tiles), _rin(y0, None, 0, ctx_tiles), _rin(y1, None, 0, ctx_tiles)],
                        [wts["rec_w_out"]], [(R, bf16), (D, f32)])

    zero_d = jnp.zeros((1, D), f32)

    def mlp_params(rows):
        rows = rows + [zero_d] * (N_MLP_PARAMS - len(rows))
        return jnp.concatenate([jnp.broadcast_to(r, (8, D)) for r in rows], axis=0)

    par0 = mlp_params([g1[0], zero_d, ng[0, 1][None], sc2[0], sh2[0], g2[0], ng[1, 0][None], sc1[1], sh1[1]])
    x1, h1, r0, mo0, x2, h2 = _mlp_forward(x0cat, T_CTX // MLP_TM, out0, par0, wts["mlp_w_in"], wts["mlp_w_out"], 0,
                                           "l0_mlp")

    wts = dict(wts, **late_weights("conf", x2))
    def glu(pa, pb, b1):
        return (pa + b1[:, :D]) * _sigmoid(pb + b1[:, D:])

    def f_pw1_glu(i, h_, b1, w):
        p = jnp.dot(h_, w, preferred_element_type=f32)
        return glu(p[:, :D], p[:, D:], b1), p

    zg, pw = _rowcall(f_pw1_glu, "l1_pw1_glu", T_LAT, FUSED_TM, [_rin(h2)], [wts["conf_b_pw1"], wts["conf_w_pw1"]],
                      [(D, f32), (2 * D, bf16)])
    (zc,) = _dwconv_fwd(zg, 0, wts["conf_conv_w"], wts["conf_conv_b"], 31, 15, ((0, T_LAT),), 128, "l1_conv", False)

    def ln_silu(z, lg, lb):
        mu = jnp.mean(z, axis=-1, keepdims=True)
        zc_ = z - mu
        var = jnp.mean(zc_ * zc_, axis=-1, keepdims=True)
        yv = zc_ * lax.rsqrt(var + EPS) * lg + lb
        return yv * _sigmoid(yv)

    def f_lnsilu_pw2(i, z, lg, lb, w):
        s = ln_silu(z, lg, lb).astype(bf16)
        return s, jnp.dot(s, w, preferred_element_type=f32)

    sb, out1 = _rowcall(f_lnsilu_pw2, "l1_ln_silu_pw2", T_LAT, FUSED_TM, [_rin(zc)],
                        [wts["conf_ln_g"], wts["conf_ln_b"], wts["conf_w_pw2"]], [(D, bf16), (D, f32)])
    par1 = mlp_params([g1[1], wts["conf_b_pw2"], ng[1, 1][None], sc2[1], sh2[1], g2[1]])
    x3, h3, r1, mo1, x4, _ = _mlp_forward(x2, 0, out1, par1, wts["mlp_w_in"], wts["mlp_w_out"], 1, "l1_mlp")

    def loss_fn(x4_, fg, tgt):
        err = _rms(x4_, fg) - tgt
        per_row = jnp.mean(err * err, axis=-1, keepdims=True)
        return 0.5 * jnp.sum(per_row, axis=0, keepdims=True)

    def f_head(i, x4_, tgt, fg):
        loss, vjp = jax.vjp(lambda a, e: loss_fn(a, e, tgt), x4_, fg)
        dx, dfg = vjp(jnp.ones((1, 1), f32))
        return dx, jnp.broadcast_to(loss, (1, 128)), dfg

    dx4, loss_acc, dfinal_g = _rowcall(f_head, "head", T_LAT, TM, [_rin(x4), _rin(target)], [wts["final_g"]],
                                       [(D, f32)], [(1, 128), (1, D)])

    grads = {"final_g": dfinal_g}
    loss = reduce_loss(loss_acc[0, 0])

    dx3, dout1, dmo1, dhid1, acc1 = _mlp_backward(dx4, x3, r1, mo1, out1, par1, wts["mlp_w_in"], wts["mlp_w_out"], 1,
                                                  "l1_mlp_bwd", after=[loss.reshape(1, 1)])
    mlp_dw = _mlp_weight_grads(h3, dhid1, r1, dmo1, 1, None, "l1")
    dg1_1, db_pw2, dng11, dsc2_1, dsh2_1, dg2_1 = [acc1[k:k + 1] for k in range(6)]

    grads["conf_w_pw2"] = _mm(sb, dout1, "l1_pw2_dw", ta=True, out_dtype=bf16)
    grads["conf_b_pw2"] = db_pw2

    def f_pw2_lnsilu_bwd(i, z, dout, lg, lb, w):
        ds = lax.dot_general(dout, w, nt, preferred_element_type=f32)
        _, vjp = jax.vjp(ln_silu, z, lg, lb)
        return vjp(ds)

    dzc, dln_g, dln_b = _rowcall(f_pw2_lnsilu_bwd, "l1_pw2_ln_silu_bwd", T_LAT, FUSED_TM, [_rin(zc), _rin(dout1)],
                                 [wts["conf_ln_g"], wts["conf_ln_b"], wts["conf_w_pw2"]], [(D, f32)], [(1, D)] * 2)
    grads["conf_ln_g"], grads["conf_ln_b"] = dln_g, dln_b
    dzg, dconv_w, dconv_b = _dwconv_bwd([dzc], zg, 0, wts["conf_conv_w"], 31, 15, ((0, T_LAT),), 128,
                                        "l1_conv_bwd", f32)
    grads["conf_conv_w"], grads["conf_conv_b"] = dconv_w, dconv_b

    def f_glu_pw1_norm_bwd(i, p_, dz, x_, dxs, b1, g_, sc_, sh_, w):
        pf = p_.astype(f32)
        _, vjp = jax.vjp(glu, pf[:, :D], pf[:, D:], b1)
        da, db, db1 = vjp(dz)
        dp = jnp.concatenate([da, db], axis=1).astype(bf16)
        dh = lax.dot_general(dp, w, nt, preferred_element_type=f32)
        _, vjp = jax.vjp(_normmod, x_, g_, sc_, sh_)
        dx, dg, dsc, dsh = vjp(dh)
        return dp, dx + dxs, db1, dg, dsc, dsh

    dpw, dx2, db_pw1, dng10, dsc1_1, dsh1_1 = _rowcall(
        f_glu_pw1_norm_bwd, "l1_glu_pw1_normmod_bwd", T_LAT, FUSED_TM, [_rin(pw), _rin(dzg), _rin(x2), _rin(dx3)],
        [wts["conf_b_pw1"], ng[1, 0][None], sc1[1], sh1[1], wts["conf_w_pw1"]], [(2 * D, bf16), (D, f32)],
        [(1, 2 * D), (1, D), (1, D), (1, D)])
    grads["conf_b_pw1"] = db_pw1
    grads["conf_w_pw1"] = _mm(h2, dpw, "l1_pw1_dw", ta=True, out_dtype=bf16)
    sent = send_grads(["conf_w_pw2", "conf_w_pw1"], grads)

    dx1, dout0, dmo0, dhid0, acc0 = _mlp_backward(dx2, x1, r0, mo0, out0, par0, wts["mlp_w_in"], wts["mlp_w_out"], 0,
                                                  "l0_mlp_bwd", after=[sent])
    grads["mlp_w_in"], grads["mlp_w_out"] = _mlp_weight_grads(h1, dhid0, r0, dmo0, 0, mlp_dw, "l0")
    sent = send_grads(["mlp_w_in", "mlp_w_out"], grads)
    dg1_0, _, dng01, dsc2_0, dsh2_0, dg2_0 = [acc0[k:k + 1] for k in range(6)]

    grads["rec_w_out"] = _mm(zb, dout0, "l0_out_proj_dw", ta=True, out_dtype=bf16, after=[sent])
    sent = send_grads(["rec_w_out"], grads)

    def f_out_gate_bwd(i, gp, y0_, y1_, dout, w):
        lat = jnp.where(i < ctx_tiles, 0.0, 1.0)
        dz = lax.dot_general(dout, w, nt, preferred_element_type=f32)
        _, vjp = jax.vjp(lambda a, b: _gelu(a) * b, gp, y0_ + y1_)
        dgp, dy = vjp(dz)
        return dgp * lat, dy * lat

    dgp, dy = _rowcall(f_out_gate_bwd, "l0_out_proj_gate_bwd", T_ALL, FUSED_TM,
                       [_rin(gr, R, 0), _rin(y0), _rin(y1), _rin(dout0, None, 0, -ctx_tiles)], [wts["rec_w_out"]],
                       [(R, bf16), (R, f32)], after=[sent])
    (dh_f,) = _scan_call(a0, dy, SCAN_FWD_BWD, True, "l0_scan_fwd_bwd", True)
    (dh_r,) = _scan_call(a1, dy, SCAN_REV_BWD, False, "l0_scan_rev_bwd", True)

    dpre, du_f, *dpar_f = _gates_coeff_bwd(ub, u, dh_f, yp0, *gate_args, 0, None)
    dpre, du_r, *dpar_r = _gates_coeff_bwd(ub, u, dh_r, yp1, *gate_args, 1, dpre)
    grads["rec_b_a"], grads["rec_b_x"], grads["rec_lambda"] = [
        jnp.concatenate([f.reshape(-1), r_.reshape(-1)]).reshape(2, R) for f, r_ in zip(dpar_f, dpar_r)]
    grads["gates"] = _gates_dw(ub, dpre)
    sent = send_grads(["replicated"], grads)
    du_gates = _gates_dx(dpre, wts["gates"], after=[sent])
    drec, dconv4_w, dconv4_b = _dwconv_bwd([du_f, du_r, du_gates], gr, R // 256, wts["rec_conv_w"], 4, 1,
                                           CONV_SEGMENTS, 256, "l0_conv_bwd", bf16)
    grads["rec_conv_w"], grads["rec_conv_b"] = dconv4_w, dconv4_b
    dgr = jnp.concatenate([dgp, drec], axis=1)
    grads["rec_w_in"] = _mm(h0, dgr, "l0_in_proj_dw", ta=True, out_dtype=bf16)
    sent = send_grads(["rec_w_in"], grads)

    def f_pre0_bwd(i, x0, dgr_, dxs, g, scp_, shp_, w):
        lat = jnp.where(i < ctx_tiles, 0.0, 1.0)
        dh = lax.dot_general(dgr_, w, nt, preferred_element_type=f32)
        _, vjp = jax.vjp(lambda a, b, c, e: _normmod(a, b, blend(i, c), blend(i, e)), x0, g, scp_, shp_)
        dx, dg, dscp, dshp = vjp(dh)
        return dx + lat * dxs, dg, dscp, dshp

    dx0cat, dng00, dscp, dshp = _rowcall(
        f_pre0_bwd, "l0_in_proj_prenorm_bwd", T_ALL, FUSED_TM,
        [_rin(x0cat), _rin(dgr), _rin(dx1, None, 0, -ctx_tiles)], [ng[0, 0][None], scp, shp, wts["rec_w_in"]],
        [(D, f32)], [(1, D), (2, D), (2, D)], after=[sent])

    grads["norm_g"] = jnp.stack([jnp.concatenate([dng00, dng01], 0), jnp.concatenate([dng10, dng11], 0)])
    dmods = jnp.stack([
        jnp.concatenate([dshp[1:2], dscp[1:2], dg1_0, dsh2_0, dsc2_0, dg2_0], axis=0),
        jnp.concatenate([dsh1_1, dsc1_1, dg1_1, dsh2_1, dsc2_1, dg2_1], axis=0)])
    dcmod = jnp.concatenate([dshp[0:1], dscp[0:1]], axis=0)
    return loss, dx0cat[T_CTX:], dmods, dcmod, grads


def _unshard_cols(g):
    g = jnp.moveaxis(g, 0, -2)
    return g.reshape(g.shape[:-2] + (g.shape[-2] * g.shape[-1],))


def _shard_cols(w):
    w = w.reshape(w.shape[:-1] + (N_DEV, w.shape[-1] // N_DEV))
    return jnp.moveaxis(w, -2, 0)


def _shard_rows(w):
    return w.reshape((N_DEV, w.shape[0] // N_DEV) + w.shape[1:])


SMALL_PACK_ROWS = 64


def kernel(x, c, ctx, c_ctx, w_ada, b_ada, norm_g, rec_w_in, rec_conv_w, rec_conv_b, rec_lambda, rec_w_a, rec_b_a, rec_w_x, rec_b_x, rec_w_out, conf_w_pw1, conf_b_pw1, conf_conv_w, conf_conv_b, conf_ln_g, conf_ln_b, conf_w_pw2, conf_b_pw2, mlp_w_in, mlp_w_out, final_g, loss_target, m_c_ctx, m_w_ada, m_b_ada, m_norm_g, m_rec_w_in, m_rec_conv_w, m_rec_conv_b, m_rec_lambda, m_rec_w_a, m_rec_b_a, m_rec_w_x, m_rec_b_x, m_rec_w_out, m_conf_w_pw1, m_conf_b_pw1, m_conf_conv_w, m_conf_conv_b, m_conf_ln_g, m_conf_ln_b, m_conf_w_pw2, m_conf_b_pw2, m_mlp_w_in, m_mlp_w_out, m_final_g, v_c_ctx, v_w_ada, v_b_ada, v_norm_g, v_rec_w_in, v_rec_conv_w, v_rec_conv_b, v_rec_lambda, v_rec_w_a, v_rec_b_a, v_rec_w_x, v_rec_b_x, v_rec_w_out, v_conf_w_pw1, v_conf_b_pw1, v_conf_conv_w, v_conf_conv_b, v_conf_ln_g, v_conf_ln_b, v_conf_w_pw2, v_conf_b_pw2, v_mlp_w_in, v_mlp_w_out, v_final_g):
    me = 4 * lax.axis_index("x") + 2 * lax.axis_index("y") + lax.axis_index("c")
    weights = dict(c_ctx=c_ctx, w_ada=w_ada, b_ada=b_ada, norm_g=norm_g, rec_w_in=rec_w_in, rec_conv_w=rec_conv_w,
                   rec_conv_b=rec_conv_b, rec_lambda=rec_lambda, rec_w_a=rec_w_a, rec_b_a=rec_b_a, rec_w_x=rec_w_x,
                   rec_b_x=rec_b_x, rec_w_out=rec_w_out, conf_w_pw1=conf_w_pw1, conf_b_pw1=conf_b_pw1,
                   conf_conv_w=conf_conv_w, conf_conv_b=conf_conv_b, conf_ln_g=conf_ln_g, conf_ln_b=conf_ln_b,
                   conf_w_pw2=conf_w_pw2, conf_b_pw2=conf_b_pw2, mlp_w_in=mlp_w_in, mlp_w_out=mlp_w_out, final_g=final_g)
    m_in = dict(c_ctx=m_c_ctx, w_ada=m_w_ada, b_ada=m_b_ada, norm_g=m_norm_g, rec_w_in=m_rec_w_in, rec_conv_w=m_rec_conv_w,
                rec_conv_b=m_rec_conv_b, rec_lambda=m_rec_lambda, rec_w_a=m_rec_w_a, rec_b_a=m_rec_b_a, rec_w_x=m_rec_w_x,
                rec_b_x=m_rec_b_x, rec_w_out=m_rec_w_out, conf_w_pw1=m_conf_w_pw1, conf_b_pw1=m_conf_b_pw1,
                conf_conv_w=m_conf_conv_w, conf_conv_b=m_conf_conv_b, conf_ln_g=m_conf_ln_g, conf_ln_b=m_conf_ln_b,
                conf_w_pw2=m_conf_w_pw2, conf_b_pw2=m_conf_b_pw2, mlp_w_in=m_mlp_w_in, mlp_w_out=m_mlp_w_out,
                final_g=m_final_g)
    v_in = dict(c_ctx=v_c_ctx, w_ada=v_w_ada, b_ada=v_b_ada, norm_g=v_norm_g, rec_w_in=v_rec_w_in, rec_conv_w=v_rec_conv_w,
                rec_conv_b=v_rec_conv_b, rec_lambda=v_rec_lambda, rec_w_a=v_rec_w_a, rec_b_a=v_rec_b_a, rec_w_x=v_rec_w_x,
                rec_b_x=v_rec_b_x, rec_w_out=v_rec_w_out, conf_w_pw1=v_conf_w_pw1, conf_b_pw1=v_conf_b_pw1,
                conf_conv_w=v_conf_conv_w, conf_conv_b=v_conf_conv_b, conf_ln_g=v_conf_ln_g, conf_ln_b=v_conf_ln_b,
                conf_w_pw2=v_conf_w_pw2, conf_b_pw2=v_conf_b_pw2, mlp_w_in=v_mlp_w_in, mlp_w_out=v_mlp_w_out,
                final_g=v_final_g)
    names = list(weights)

    small_items = [c, norm_g, rec_conv_w, rec_lambda, conf_b_pw1, conf_conv_w, conf_conv_b, conf_ln_g, conf_ln_b,
                   conf_b_pw2]
    flat = jnp.concatenate([a.reshape(-1) for a in small_items])
    flat = jnp.pad(flat, (0, SMALL_PACK_ROWS * 128 - flat.shape[0])).reshape(SMALL_PACK_ROWS, 128)
    as_shard = lambda a: a.astype(bf16).reshape(-1, a.shape[-1])
    small_all, early = _all_gather_2level([flat, as_shard(rec_w_in[0])], "gather_small_and_early")

    small_all = small_all.reshape(N_DEV, -1)
    off = 0
    small = []
    for a in small_items:
        small.append(small_all[:, off:off + a.size].reshape((N_DEV,) + a.shape))
        off += a.size
    c_all, ng_all, rcw_all, lam_all, bpw1_all, ccw_all, ccb_all, lng_all, lnb_all, bpw2_all = small
    wts = {
        "norm_g": _unshard_cols(ng_all),
        "rec_conv_w": _unshard_cols(rcw_all)[0],
        "rec_lambda": _unshard_cols(lam_all)[0],
        "conf_b_pw1": _unshard_cols(bpw1_all),
        "conf_conv_w": _unshard_cols(ccw_all)[0],
        "conf_conv_b": _unshard_cols(ccb_all),
        "conf_ln_g": _unshard_cols(lng_all),
        "conf_ln_b": _unshard_cols(lnb_all),
        "conf_b_pw2": _unshard_cols(bpw2_all),
        "rec_conv_b": rec_conv_b,
        "rec_b_a": rec_b_a[0].reshape(2, R),
        "rec_b_x": rec_b_x[0].reshape(2, R),
        "final_g": final_g[None],
        "gates": _gate_matrix(rec_w_a[0], rec_w_x[0]),
    }

    c16 = jnp.concatenate([c_all[:, 0], jnp.broadcast_to(c_ctx[None], (8, D))], axis=0)
    b_loc = lax.dynamic_slice_in_dim(b_ada, me * ADA_SHARD, ADA_SHARD, axis=1)[:, None]
    (mods_all,) = _all_gather([_ada_forward(c16, w_ada, b_loc)], "gather_mods")
    mods_all = _unshard_cols(mods_all)
    mods = lax.dynamic_index_in_dim(mods_all, me, axis=1, keepdims=False).reshape(2, N_MOD, D)
    cmod = mods_all[0, 8, :2 * D].reshape(2, D)

    wts["rec_w_in"] = _unshard_cols(early)
    late_items = {"mlp": [rec_w_out[0], mlp_w_in, mlp_w_out], "conf": [conf_w_pw1[0], conf_w_pw2[0]]}
    late_shards = {g: [as_shard(a) for a in items] for g, items in late_items.items()}
    late_lands = {g: [_own_block_filled(s, me) for s in shards] for g, shards in late_shards.items()}
    late_handles = {}
    late_handles["mlp"], token = _gather2_start(late_shards["mlp"], late_lands["mlp"], "gather_mlp_start", [early, mods])
    order = [token]

    def late_weights(group, after):
        if group == "mlp_halfway":
            late_handles["mlp"] = _gather2_forward1(late_handles["mlp"], after, "gather_mlp_forward1")
            return late_handles["mlp"][2][0]
        if group == "mlp":
            passed = _gather2_forward2(late_handles["mlp"], after, "gather_mlp_forward2")
            late_handles["conf"], started = _exchange_start(late_shards["conf"], late_lands["conf"], "gather_conf_start",
                                                            False, after=[passed[2][0]])
            got = _gather2_wait(passed, started, "gather_mlp_wait")
        else:
            got = _exchange_wait(late_handles[group], after, "gather_conf_wait", False)
        got = [g.reshape((N_DEV,) + a.shape) for g, a in zip(got, late_items[group])]
        if group == "mlp":
            return {"rec_w_out": got[0].reshape(R, D), "mlp_w_in": got[1], "mlp_w_out": got[2]}
        return {"conf_w_pw1": _unshard_cols(got[0]), "conf_w_pw2": got[1].reshape(D, D)}

    to_blocks = {"rec_w_in": _shard_cols, "conf_w_pw1": _shard_cols, "rec_w_out": _shard_rows, "conf_w_pw2": _shard_rows,
                 "mlp_w_in": lambda g: g, "mlp_w_out": lambda g: g}
    grad_handles = []

    repl_names = ["rec_w_a", "rec_w_x", "rec_b_a", "rec_b_x", "final_g"]

    def send_replicated(grads):
        dwg = grads["gates"]
        repl = {"rec_w_a": jnp.stack([_gate_blocks(dwg, 0), _gate_blocks(dwg, 2)]),
                "rec_w_x": jnp.stack([_gate_blocks(dwg, 1), _gate_blocks(dwg, 3)]),
                "rec_b_a": grads["rec_b_a"], "rec_b_x": grads["rec_b_x"], "final_g": grads["final_g"]}
        flat = jnp.concatenate([repl[n].reshape(-1) for n in repl_names])
        rows = -(-flat.shape[0] // (16 * D)) * 16
        flat = jnp.pad(flat, (0, rows * D - flat.shape[0])).reshape(rows, D).astype(bf16)
        handle, sent = _exchange_start([flat], [_own_block_filled(flat, me)], "gather_replicated_start", False)
        grad_handles.append((["replicated"], handle))
        return sent

    def send_grads(group, grads):
        if group == ["replicated"]:
            return send_replicated(grads)
        blocks = [to_blocks[n](grads[n]) for n in group]
        blocks = [g.reshape(N_DEV, -1, g.shape[-1]) for g in blocks]
        lands = [_own_block_filled(lax.dynamic_index_in_dim(g, me, 0, keepdims=False), me) for g in blocks]
        handle, sent = _exchange_start(blocks, lands, "scatter_start_" + group[0], True)
        grad_handles.append((group, handle))
        return sent

    loss, grad_x, dmods, dcmod, grads = _local_step(
        x[0], ctx[0], loss_target[0], mods, cmod, wts, late_weights, send_grads,
        lambda partial: lax.psum(partial, ("x", "y", "c")), start_after=order)

    def as2d(shape):
        rows = 1
        for s in shape[:-1]:
            rows *= s
        return (rows, shape[-1])

    def whole(arr, shape):
        arr = arr.reshape((-1,) + as2d(shape))
        return (arr, arr.shape[0])

    shard_shapes = {n: weights[n].shape for n in names}
    g_out, d_out, m_out, v_out = {}, {}, {}, {}

    def adamw(n, pieces, after):
        shape = shard_shapes[n]
        r2, c2 = as2d(shape)
        g, dl, nm, nv = _adamw(pieces, weights[n].reshape(r2, c2), m_in[n].reshape(r2, c2), v_in[n].reshape(r2, c2),
                               "adamw_" + n, after=after)
        g_out[n], d_out[n], m_out[n], v_out[n] = (t.reshape(shape) for t in (g, dl, nm, nv))
        return g

    small_sharded = ["norm_g", "rec_conv_w", "rec_lambda", "conf_b_pw1", "conf_conv_w", "conf_conv_b", "conf_ln_g",
                     "conf_ln_b", "conf_b_pw2"]
    pack = jnp.concatenate([_shard_cols(grads[n]).reshape(N_DEV, -1) for n in small_sharded], axis=1)
    pack = jnp.pad(pack, ((0, 0), (0, SMALL_PACK_ROWS * 128 - pack.shape[1]))).reshape(N_DEV, SMALL_PACK_ROWS, 128)
    small_handle, token = _exchange_start(
        [pack], [_own_block_filled(lax.dynamic_index_in_dim(pack, me, 0, keepdims=False), me)], "scatter_small_start",
        True, after=[grad_x])
    dm_flat = jnp.concatenate([dmods.reshape(-1), dcmod.reshape(-1), grads["rec_conv_b"].reshape(-1)])
    dm_len = dm_flat.shape[0]
    dm_flat = jnp.pad(dm_flat, (0, 128 * 128 - dm_len)).reshape(128, 128)
    dm_handle, token = _exchange_start([dm_flat], [_own_block_filled(dm_flat, me)], "gather_dmods_start", False,
                                       after=[token])

    done = token
    for group, handle in grad_handles:
        if group == ["replicated"]:
            repl_all = _exchange_wait(handle, done, "gather_replicated_wait", False)[0].reshape(N_DEV, -1)
            off = 0
            for n in repl_names:
                size = weights[n].size
                done = adamw(n, [whole(repl_all[:, off:off + size], shard_shapes[n])], [done])
                off += size
            continue
        for n, got in zip(group, _exchange_wait(handle, done, "scatter_wait_" + group[0], True)):
            done = adamw(n, [(got, N_DEV)], [done])

    dm_all = _exchange_wait(dm_handle, done, "gather_dmods_wait", False)[0].reshape(N_DEV, -1)
    dmods_all = dm_all[:, :2 * N_MOD * D].reshape(N_DEV, 2, N_MOD * D)
    dcmod_all = jnp.pad(dm_all[:, 2 * N_MOD * D:2 * N_MOD * D + 2 * D], ((0, 0), (0, (N_MOD - 2) * D)))
    g16_full = jnp.stack([jnp.concatenate([dmods_all[:, 0], dcmod_all], axis=0),
                          jnp.concatenate([dmods_all[:, 1], jnp.zeros_like(dcmod_all)], axis=0)])
    g16 = lax.dynamic_slice_in_dim(g16_full, me * ADA_SHARD, ADA_SHARD, axis=2)
    dw_ada, ds_part = _ada_backward(c16, g16, w_ada)
    ds_handle, token = _exchange_start([ds_part[0]], [_own_block_filled(ds_part[0], me)], "gather_dsilu_start", False)
    done = adamw("w_ada", [whole(dw_ada, shard_shapes["w_ada"])], [token])
    done = adamw("rec_conv_b", [whole(dm_all[:, dm_len - R:dm_len], shard_shapes["rec_conv_b"])], [done])
    db_terms = jnp.concatenate([dmods_all, jnp.stack([dcmod_all, jnp.zeros_like(dcmod_all)], axis=1)], axis=0)
    done = adamw("b_ada", [whole(db_terms, shard_shapes["b_ada"])], [done])
    pack_recv = _exchange_wait(small_handle, done, "scatter_small_wait", True)[0].reshape(N_DEV, -1)
    off = 0
    for n in small_sharded:
        size = weights[n].size
        done = adamw(n, [whole(pack_recv[:, off:off + size], shard_shapes[n])], [done])
        off += size
    ds_all = _exchange_wait(ds_handle, done, "gather_dsilu_wait", False)[0]
    adamw("c_ctx", [whole(ds_all[:, 0], shard_shapes["c_ctx"])], [])

    return (loss, grad_x[None], *[g_out[n] for n in names], *[d_out[n] for n in names],
            *[m_out[n] for n in names], *[v_out[n] for n in names])
```

```python
import functools

import jax
import jax.numpy as jnp
from jax import lax
from jax.experimental import pallas as pl
from jax.experimental.pallas import tpu as pltpu

f32 = jnp.float32
bf16 = jnp.bfloat16

N_DEV = 8
D = 1024
T_LAT = 2048
T_CTX = 256
T_ALL = T_CTX + T_LAT
R = 1280
N_BLK = 16
BLK = R // N_BLK
F = 4096
GRID_W = 64
RG_C = 8.0
EPS = 1e-6
POS_BASE = 10000.0
N_MOD = 6
ADA_SHARD = N_MOD * D // N_DEV

ADAM_LR = 0.001
ADAM_B1 = 0.9
ADAM_B2 = 0.999
ADAM_EPS = 1e-08
ADAM_WD = 0.01
ADAM_STEP = 10

VMEM_LIMIT_V7X = 56 * 1024 * 1024
HALO = 16
MESH = pl.DeviceIdType.MESH


def _cparams(*sem):
    return pltpu.CompilerParams(dimension_semantics=sem, vmem_limit_bytes=VMEM_LIMIT_V7X)


def _pick(n, cands):
    for c in cands:
        if n % c == 0:
            return c
    raise ValueError(f"no block size for {n}")


def _position():
    x, y, c = lax.axis_index("x"), lax.axis_index("y"), lax.axis_index("c")
    return x, y, c, 4 * x + 2 * y + c


def _peer(x, y, c, k):
    px = (1 - x) if (k >> 2) & 1 else x
    py = (1 - y) if (k >> 1) & 1 else y
    pc = (1 - c) if k & 1 else c
    return (px, py, pc), 4 * px + 2 * py + pc


def _exchange(arrs, name, scatter):
    n = len(arrs)

    def body(*refs):
        ins, outs = refs[:n], refs[n:2 * n]
        send_sems, recv_sems, local_sems = refs[2 * n:]
        x, y, c, me = _position()
        local = []
        for a in range(n):
            src = ins[a].at[me] if scatter else ins[a]
            cp = pltpu.make_async_copy(src, outs[a].at[me], local_sems.at[a])
            cp.start()
            local.append(cp)
        sends, recvs = [], []
        for a in range(n):
            for k in range(1, N_DEV):
                peer, peer_lin = _peer(x, y, c, k)
                src = ins[a].at[peer_lin] if scatter else ins[a]
                cp = pltpu.make_async_remote_copy(
                    src_ref=src, dst_ref=outs[a].at[me], send_sem=send_sems.at[a, k - 1],
                    recv_sem=recv_sems.at[a, k - 1], device_id=peer, device_id_type=MESH)
                cp.start()
                sends.append(cp)
                recvs.append(pltpu.make_async_remote_copy(
                    src_ref=src, dst_ref=outs[a].at[peer_lin], send_sem=send_sems.at[a, k - 1],
                    recv_sem=recv_sems.at[a, k - 1], device_id=peer, device_id_type=MESH))
        for cp in recvs:
            cp.wait_recv()
        for cp in sends:
            cp.wait_send()
        for cp in local:
            cp.wait()

    if scatter:
        out_shape = [jax.ShapeDtypeStruct(a.shape, a.dtype) for a in arrs]
    else:
        out_shape = [jax.ShapeDtypeStruct((N_DEV,) + a.shape, a.dtype) for a in arrs]
    any_spec = pl.BlockSpec(memory_space=pl.ANY)
    return pl.pallas_call(
        body, name=name, out_shape=out_shape,
        in_specs=[any_spec] * n, out_specs=[any_spec] * n,
        scratch_shapes=[pltpu.SemaphoreType.DMA((n, N_DEV - 1)), pltpu.SemaphoreType.DMA((n, N_DEV - 1)),
                        pltpu.SemaphoreType.DMA((n,))],
    )(*arrs)


def _all_gather(arrs, name):
    return _exchange(arrs, name, scatter=False)


def _lin(p):
    return 4 * p[0] + 2 * p[1] + p[2]


HBM_SPEC = pl.BlockSpec(memory_space=pltpu.HBM)
SEM_SPEC = pl.BlockSpec(memory_space=pltpu.SEMAPHORE)
DATAFLOW_EFFECT = pltpu.SideEffectType.DATAFLOW_SIDE_EFFECTING


def _split_copies(srcs, lands, send_sems, recv_sems, scatter):
    x, y, c, me = _position()
    out = []
    for a in range(len(srcs)):
        for k in range(1, N_DEV):
            peer, peer_lin = _peer(x, y, c, k)
            src = srcs[a].at[peer_lin] if scatter else srcs[a]
            mk = lambda slot: pltpu.make_async_remote_copy(
                src_ref=src, dst_ref=lands[a].at[slot], send_sem=send_sems.at[a * (N_DEV - 1) + k - 1],
                recv_sem=recv_sems.at[a * (N_DEV - 1) + k - 1], device_id=peer, device_id_type=MESH)
            out.append((mk(me), mk(peer_lin)))
    return out


def _exchange_start(srcs, lands, name, scatter, after=()):
    n = len(srcs)
    n_after = len(after)

    def body(*refs):
        srcs_r, lands_r = refs[:n], refs[n:2 * n]
        send_sems, recv_sems = refs[2 * n + n_after], refs[2 * n + n_after + 1]
        token = refs[-1]
        for outgoing, _ in _split_copies(srcs_r, lands_r, send_sems, recv_sems, scatter):
            outgoing.start()
        token[...] = jnp.zeros_like(token)

    hbm = lambda a: pltpu.HBM(a.shape, a.dtype)
    res = pl.pallas_call(
        body, name=name,
        out_shape=(pltpu.SemaphoreType.DMA((n * (N_DEV - 1),)), pltpu.SemaphoreType.DMA((n * (N_DEV - 1),)),
                   *[hbm(a) for a in srcs], *[hbm(a) for a in lands], jax.ShapeDtypeStruct((8, 128), f32)),
        in_specs=[HBM_SPEC] * (2 * n) + [pl.BlockSpec(memory_space=pl.ANY)] * n_after,
        out_specs=(SEM_SPEC, SEM_SPEC, *[HBM_SPEC] * (2 * n), pl.BlockSpec(memory_space=pltpu.VMEM)),
        input_output_aliases={i: 2 + i for i in range(2 * n)},
        compiler_params=pltpu.CompilerParams(has_side_effects=DATAFLOW_EFFECT),
    )(*[pltpu.with_memory_space_constraint(a, pltpu.HBM) for a in list(srcs) + list(lands)], *after)
    return (res[0], res[1], list(res[2:2 + n]), list(res[2 + n:2 + 2 * n])), res[-1]


def _exchange_wait(handle, after, name, scatter):
    send_sems, recv_sems, srcs, lands = handle
    n = len(srcs)
    after = list(after) if isinstance(after, (list, tuple)) else [after]

    def body(*refs):
        srcs_r, lands_r = refs[:n], refs[n:2 * n]
        send_s, recv_s = refs[2 * n], refs[2 * n + 1]
        for outgoing, incoming in _split_copies(srcs_r, lands_r, send_s, recv_s, scatter):
            outgoing.wait_send()
            incoming.wait_recv()

    hbm = lambda a: pltpu.HBM(a.shape, a.dtype)
    res = pl.pallas_call(
        body, name=name, out_shape=tuple(hbm(a) for a in list(srcs) + list(lands)),
        in_specs=[HBM_SPEC] * (2 * n) + [SEM_SPEC, SEM_SPEC] + [pl.BlockSpec(memory_space=pl.ANY)] * len(after),
        out_specs=tuple([HBM_SPEC] * (2 * n)),
        input_output_aliases={i: i for i in range(2 * n)},
        compiler_params=pltpu.CompilerParams(has_side_effects=DATAFLOW_EFFECT),
    )(*srcs, *lands, send_sems, recv_sems, *after)
    return list(res[n:])


def _split_call(body, name, hbm_ins, kept, in_sems, n_new_sems, after, with_token):
    n_in, n_sem = len(hbm_ins), len(in_sems)
    out_shape, out_specs = [], []
    if n_new_sems:
        out_shape += [pltpu.SemaphoreType.DMA((n_new_sems,))] * 2
        out_specs += [SEM_SPEC] * 2
    first_kept = len(out_shape)
    out_shape += [pltpu.HBM(hbm_ins[i].shape, hbm_ins[i].dtype) for i in kept]
    out_specs += [HBM_SPEC] * len(kept)
    if with_token:
        out_shape.append(jax.ShapeDtypeStruct((8, 128), f32))
        out_specs.append(pl.BlockSpec(memory_space=pltpu.VMEM))

    def wrapped(*refs):
        outs = refs[n_in + n_sem + len(after):]
        body(refs[:n_in], refs[n_in:n_in + n_sem], outs[:2] if n_new_sems else ())
        if with_token:
            outs[-1][...] = jnp.zeros_like(outs[-1])

    return pl.pallas_call(
        wrapped, name=name, out_shape=tuple(out_shape),
        in_specs=[HBM_SPEC] * n_in + [SEM_SPEC] * n_sem + [pl.BlockSpec(memory_space=pl.ANY)] * len(after),
        out_specs=tuple(out_specs), input_output_aliases={i: first_kept + j for j, i in enumerate(kept)},
        compiler_params=pltpu.CompilerParams(has_side_effects=DATAFLOW_EFFECT),
    )(*[pltpu.with_memory_space_constraint(a, pltpu.HBM) for a in hbm_ins], *in_sems, *after)


def _rcopy(src, dst, sems, k, to):
    return pltpu.make_async_remote_copy(src_ref=src, dst_ref=dst, send_sem=sems[0].at[k], recv_sem=sems[1].at[k],
                                        device_id=to, device_id_type=MESH)


def _gather2_start(shards, lands, name, after):
    n = len(shards)

    def body(ins, sems_in, sems_out):
        x, y, c, me = _position()
        for a in range(n):
            for k, to in enumerate(((x, y, 1 - c), (1 - x, y, c), (x, 1 - y, c))):
                _rcopy(ins[a], ins[n + a].at[me], sems_out, 3 * a + k, to).start()

    res = _split_call(body, name, list(shards) + list(lands), range(2 * n), (), 3 * n, after, True)
    return (res[0], res[1], list(res[2:2 + n]), list(res[2 + n:2 + 2 * n])), res[-1]


def _gather2_forward1(handle, after, name):
    send_sems, recv_sems, srcs, lands = handle
    n = len(srcs)

    def body(ins, sems_in, sems_out):
        x, y, c, me = _position()
        sib, xn, yn = (x, y, 1 - c), (1 - x, y, c), (x, 1 - y, c)
        for a in range(n):
            for k, peer in enumerate((sib, xn, yn)):
                _rcopy(ins[a], ins[n + a].at[me], sems_in, 3 * a + k, peer).wait_send()
                _rcopy(ins[a], ins[n + a].at[_lin(peer)], sems_in, 3 * a + k, peer).wait_recv()
        for a in range(n):
            land = ins[n + a]
            _rcopy(land.at[_lin(xn)], land.at[_lin(xn)], sems_out, 3 * a, sib).start()
            _rcopy(land.at[_lin(yn)], land.at[_lin(yn)], sems_out, 3 * a + 1, sib).start()

            @pl.when(c == 0)
            def _():
                _rcopy(land.at[_lin(xn)], land.at[_lin(xn)], sems_out, 3 * a + 2, yn).start()

            @pl.when(c == 1)
            def _():
                _rcopy(land.at[_lin(yn)], land.at[_lin(yn)], sems_out, 3 * a + 2, xn).start()

    res = _split_call(body, name, list(srcs) + list(lands), range(n, 2 * n), (send_sems, recv_sems), 3 * n, [after], False)
    return (res[0], res[1], list(res[2:]))


def _gather2_forward2(handle, after, name):
    send_sems, recv_sems, lands = handle
    n = len(lands)

    def body(ins, sems_in, sems_out):
        x, y, c, me = _position()
        sib, dg = (x, y, 1 - c), _lin((1 - x, 1 - y, c))
        for a in range(n):
            for k, slot in enumerate((_lin((1 - x, y, 1 - c)), _lin((x, 1 - y, 1 - c)), dg)):
                done = _rcopy(ins[a].at[slot], ins[a].at[slot], sems_in, 3 * a + k, sib)
                done.wait_send()
                done.wait_recv()
        for a in range(n):
            _rcopy(ins[a].at[dg], ins[a].at[dg], sems_out, a, sib).start()

    res = _split_call(body, name, list(lands), range(n), (send_sems, recv_sems), n, [after], False)
    return (res[0], res[1], list(res[2:]))


def _gather2_wait(handle, after, name):
    send_sems, recv_sems, lands = handle
    n = len(lands)

    def body(ins, sems_in, sems_out):
        x, y, c, me = _position()
        slot = _lin((1 - x, 1 - y, 1 - c))
        for a in range(n):
            done = _rcopy(ins[a].at[slot], ins[a].at[slot], sems_in, a, (x, y, 1 - c))
            done.wait_send()
            done.wait_recv()

    return list(_split_call(body, name, list(lands), range(n), (send_sems, recv_sems), 0, [after], False))


def _own_block_filled(block, me):
    land = lax.empty((N_DEV,) + block.shape, block.dtype)
    return lax.dynamic_update_index_in_dim(land, block, me, 0)


def _staged_copy(src, dst, buf, in_sems, out_sems, rows, chunk):
    n = rows // chunk

    def rd(i):
        return pltpu.make_async_copy(src.at[pl.ds(i * chunk, chunk)], buf.at[i % 2], in_sems.at[i % 2])

    def wr(i):
        return pltpu.make_async_copy(buf.at[i % 2], dst.at[pl.ds(i * chunk, chunk)], out_sems.at[i % 2])

    rd(0).start()
    for i in range(n):
        if i + 1 < n:
            if i >= 1:
                wr(i - 1).wait()
            rd(i + 1).start()
        rd(i).wait()
        wr(i).start()
    for i in range(max(n - 2, 0), n):
        wr(i).wait()


def _all_gather_2level(shards, name):
    n = len(shards)
    chunks = [_pick(s.shape[0], (416, 512, 256, 160, 128, 64, 16)) for s in shards]

    def body(*refs):
        ins, outs = refs[:n], refs[n:2 * n]
        send_sems, recv_sems, in_sems, out_sems = refs[2 * n:2 * n + 4]
        bufs = refs[2 * n + 4:]
        x, y, c, me = _position()
        sib, xn, yn, dg = (x, y, 1 - c), (1 - x, y, c), (x, 1 - y, c), (1 - x, 1 - y, c)

        def cp(a, k, src, slot, to):
            return pltpu.make_async_remote_copy(src_ref=src, dst_ref=outs[a].at[slot], send_sem=send_sems.at[a, k],
                                                recv_sem=recv_sems.at[a, k], device_id=to, device_id_type=MESH)

        for a in range(n):
            for k, to in ((0, sib), (1, xn), (2, yn)):
                cp(a, k, ins[a], me, to).start()
        for a in range(n):
            cp(a, 1, ins[a], _lin(xn), xn).wait_recv()
            cp(a, 3, outs[a].at[_lin(xn)], _lin(xn), sib).start()

            @pl.when(c == 0)
            def _():
                cp(a, 5, outs[a].at[_lin(xn)], _lin(xn), yn).start()

            cp(a, 2, ins[a], _lin(yn), yn).wait_recv()
            cp(a, 4, outs[a].at[_lin(yn)], _lin(yn), sib).start()

            @pl.when(c == 1)
            def _():
                cp(a, 5, outs[a].at[_lin(yn)], _lin(yn), xn).start()

        for a in range(n):
            cp(a, 5, ins[a], _lin(dg), xn).wait_recv()
            cp(a, 6, outs[a].at[_lin(dg)], _lin(dg), sib).start()
        for a in range(n):
            _staged_copy(ins[a], outs[a].at[me], bufs[a], in_sems.at[a], out_sems.at[a], shards[a].shape[0], chunks[a])
        for a in range(n):
            for k, origin in ((0, sib), (3, (1 - x, y, 1 - c)), (4, (x, 1 - y, 1 - c)), (6, (1 - x, 1 - y, 1 - c))):
                cp(a, k, ins[a], _lin(origin), sib).wait_recv()
            for k in range(7):
                cp(a, k, ins[a], me, sib).wait_send()

    any_spec = pl.BlockSpec(memory_space=pl.ANY)
    return pl.pallas_call(
        body, name=name, out_shape=[jax.ShapeDtypeStruct((N_DEV,) + s.shape, s.dtype) for s in shards],
        in_specs=[any_spec] * n, out_specs=[any_spec] * n,
        scratch_shapes=[pltpu.SemaphoreType.DMA((n, 7)), pltpu.SemaphoreType.DMA((n, 7)),
                        pltpu.SemaphoreType.DMA((n, 2)), pltpu.SemaphoreType.DMA((n, 2))]
        + [pltpu.VMEM((2, ch, s.shape[1]), s.dtype) for ch, s in zip(chunks, shards)],
    )(*shards)


def _plane_pos(x, y, q):
    return ((1 - x) if q & 2 else x, (1 - y) if q & 1 else y)


ANY_SPEC = pl.BlockSpec(memory_space=pl.ANY)


def _mm(a, b, name, ta=False, tb=False, out_dtype=f32, after=()):
    if ta:
        k_dim, m_dim = a.shape
    else:
        m_dim, k_dim = a.shape
    if tb:
        n_dim, k2 = b.shape
    else:
        k2, n_dim = b.shape
    assert k_dim == k2, (a.shape, b.shape)
    assert a.dtype == bf16 and b.dtype == bf16
    bm = _pick(m_dim, (512, 768, 640, 256, 128))
    bn = _pick(n_dim, (512, 640, 256, 128))
    bk = k_dim if k_dim <= 2560 else _pick(k_dim, (1024, 1280, 768, 512))
    nk = k_dim // bk
    a_spec = (pl.BlockSpec((bk, bm), lambda i, j, k: (k, i)) if ta
              else pl.BlockSpec((bm, bk), lambda i, j, k: (i, k)))
    b_spec = (pl.BlockSpec((bn, bk), lambda i, j, k: (j, k)) if tb
              else pl.BlockSpec((bk, bn), lambda i, j, k: (k, j)))
    dims = (((0 if ta else 1,), (1 if tb else 0,)), ((), ()))

    n_after = len(after)

    def body_single(a_ref, b_ref, *rest):
        o_ref = rest[n_after]
        o_ref[...] = lax.dot_general(a_ref[...], b_ref[...], dims, preferred_element_type=f32).astype(o_ref.dtype)

    def body(a_ref, b_ref, *rest):
        o_ref, acc_ref = rest[n_after:]
        k = pl.program_id(2)

        @pl.when(k == 0)
        def _():
            acc_ref[...] = jnp.zeros_like(acc_ref)

        acc_ref[...] += lax.dot_general(a_ref[...], b_ref[...], dims, preferred_element_type=f32)

        @pl.when(k == nk - 1)
        def _():
            o_ref[...] = acc_ref[...].astype(o_ref.dtype)

    return pl.pallas_call(
        body_single if nk == 1 else body, name=name, out_shape=jax.ShapeDtypeStruct((m_dim, n_dim), out_dtype),
        grid=(m_dim // bm, n_dim // bn, nk), in_specs=[a_spec, b_spec] + [ANY_SPEC] * n_after,
        out_specs=pl.BlockSpec((bm, bn), lambda i, j, k: (i, j)),
        scratch_shapes=[] if nk == 1 else [pltpu.VMEM((bm, bn), f32)],
        compiler_params=_cparams("parallel", "parallel", "arbitrary"),
    )(a, b, *after)


def _rin(arr, width=None, cb=0, roff=0):
    return (arr, arr.shape[1] if width is None else width, cb, roff)


def _rowcall(fn, name, rows, tm, row_ins, par_ins, row_outs, acc_outs=(), after=()):
    nr, npar, nro, n_after = len(row_ins), len(par_ins), len(row_outs), len(after)
    in_specs, args = [], []
    for arr, width, cb, roff in row_ins:
        if roff >= 0:
            imap = lambda i, cb=cb, roff=roff: (i + roff, cb)
        else:
            imap = lambda i, cb=cb, roff=roff: (jnp.maximum(i + roff, 0), cb)
        in_specs.append(pl.BlockSpec((tm, width), imap))
        args.append(arr)
    for p in par_ins:
        in_specs.append(pl.BlockSpec(p.shape, lambda i: (0, 0)))
        args.append(p)
    out_shape, out_specs = [], []
    for width, dt in row_outs:
        out_shape.append(jax.ShapeDtypeStruct((rows, width), dt))
        out_specs.append(pl.BlockSpec((tm, width), lambda i: (i, 0)))
    for p, width in acc_outs:
        out_shape.append(jax.ShapeDtypeStruct((p, width), f32))
        out_specs.append(pl.BlockSpec((p, width), lambda i: (0, 0)))

    def body(*refs):
        i = pl.program_id(0)
        res = fn(i, *[r[...] for r in refs[:nr + npar]])
        outs = refs[nr + npar + n_after:]
        for o, v in zip(outs[:nro], res[:nro]):
            o[...] = v.astype(o.dtype)
        if acc_outs:
            @pl.when(i == 0)
            def _():
                for o in outs[nro:]:
                    o[...] = jnp.zeros_like(o)

            for o, v in zip(outs[nro:], res[nro:]):
                o[...] += v

    return pl.pallas_call(
        body, name=name, out_shape=out_shape, grid=(rows // tm,), in_specs=in_specs + [ANY_SPEC] * n_after,
        out_specs=out_specs, compiler_params=_cparams("arbitrary"),
    )(*args, *after)


def _rms(x, g):
    return x * lax.rsqrt(jnp.mean(x * x, axis=-1, keepdims=True) + EPS) * g


def _normmod(x, g, sc, sh):
    return _rms(x, g) * (1.0 + sc) + sh


def _gelu(x):
    return 0.5 * x * (1.0 + jnp.tanh(0.7978845608028654 * (x + 0.044715 * (x * x * x))))


def _sigmoid(x):
    return 0.5 * (jnp.tanh(0.5 * x) + 1.0)


def _coeff_parts(pre_a, pre_x, ba, bx, lam):
    r = _sigmoid(pre_a + ba)
    ig = _sigmoid(pre_x + bx)
    nl = -lam
    sp = jnp.maximum(nl, 0.0) + jnp.log(1.0 + jnp.exp(-jnp.abs(nl)))
    la = -RG_C * r * sp
    a = jnp.exp(la)
    one_minus_a2 = -jnp.tanh(la) * (a * a + 1.0)
    inv_m = lax.rsqrt(one_minus_a2)
    return r, ig, sp, a, one_minus_a2 * inv_m, inv_m


def _coeff(pre_a, pre_x, u, ba, bx, lam):
    _, ig, _, a, m, _ = _coeff_parts(pre_a, pre_x, ba, bx, lam)
    return a, m * (ig * u)


def _coeff_bwd(pre_a, pre_x, u, ba, bx, lam, da, db):
    r, ig, sp, a, m, inv_m = _coeff_parts(pre_a, pre_x, ba, bx, lam)
    dbu = db * u
    dig = dbu * m
    dm = dbu * ig
    dla = a * (da - dm * a * inv_m)
    dpa = dla * (-RG_C * sp) * (r * (1.0 - r))
    dpx = dig * (ig * (1.0 - ig))
    dsp = jnp.sum(dla * (-RG_C * r), axis=0, keepdims=True)
    dlam = -dsp * _sigmoid(-lam)
    return (dpa, dpx, db * m * ig, jnp.sum(dpa, axis=0, keepdims=True), jnp.sum(dpx, axis=0, keepdims=True), dlam)


SCAN_CHUNK = 256


def _scan_call(a, v, chunk_of, reverse, name, backward, after=()):
    rows, width = a.shape
    n_out = 1 if backward else 2
    nt = SCAN_CHUNK // 8

    def body(a_ref, v_ref, *rest):
        outs, state_ref = rest[len(after):-1], rest[-1]

        @pl.when(pl.program_id(0) == 0)
        def _():
            state_ref[...] = jnp.zeros_like(state_ref)

        rid = lax.broadcasted_iota(jnp.int32, (8, width), 0)
        last_row = 0 if reverse else 7

        def shift(x, s, fill):
            rolled = pltpu.roll(x, (8 - s) if reverse else s, axis=0)
            return jnp.where((rid >= 8 - s) if reverse else (rid < s), fill, rolled)

        def tile(j, st):
            t0 = pl.multiple_of((nt - 1 - j if reverse else j) * 8, 8)
            at = a_ref[pl.ds(t0, 8), :]
            coef = shift(at, 1, 1.0) if backward else at
            acc = v_ref[pl.ds(t0, 8), :]
            for s in (1, 2, 4):
                acc = coef * shift(acc, s, 0.0) + acc
                coef = coef * shift(coef, s, 1.0)
            out = coef * st + acc
            outs[0][pl.ds(t0, 8), :] = out
            last = out[last_row:last_row + 1]
            if backward:
                return at[last_row:last_row + 1] * last
            outs[1][pl.ds(t0, 8), :] = shift(out, 1, st)
            return last

        state_ref[0:1, :] = lax.fori_loop(0, nt, tile, state_ref[0:1, :])

    spec = pl.BlockSpec((SCAN_CHUNK, width), lambda t: (chunk_of(t), 0))
    return pl.pallas_call(
        body, name=name, out_shape=[jax.ShapeDtypeStruct((rows, width), f32)] * n_out,
        grid=(rows // SCAN_CHUNK,), in_specs=[spec, spec] + [ANY_SPEC] * len(after), out_specs=[spec] * n_out,
        scratch_shapes=[pltpu.VMEM((8, width), f32)],
        compiler_params=_cparams("arbitrary"),
    )(a, v, *after)


CONV_CHUNK = 256


def _fill_padded(pad_ref, src_ref, start, n):
    cb = pad_ref.shape[1]
    pad_ref[pl.ds(0, HALO), :] = jnp.zeros((HALO, cb), f32)
    pad_ref[pl.ds(HALO, n), :] = src_ref[pl.ds(start, n), :].astype(f32)
    pad_ref[pl.ds(HALO + n, HALO), :] = jnp.zeros((HALO, cb), f32)


def _dwconv_fwd(x, x_cb0, w, b, taps, pad_left, segments, cb, name, emit_bf16):
    rows = x.shape[0]
    width = w.shape[1]

    def body(x_ref, w_ref, b_ref, *rest):
        outs, xp = rest[:-1], rest[-1]
        for start, n in segments:
            _fill_padded(xp, x_ref, start, n)
            for c0 in range(0, n, CONV_CHUNK):
                acc = jnp.zeros((CONV_CHUNK, cb), f32) + b_ref[...]
                for k in range(taps):
                    acc = acc + w_ref[k:k + 1, :] * xp[pl.ds(HALO + c0 + k - pad_left, CONV_CHUNK), :]
                for o in outs:
                    o[pl.ds(start + c0, CONV_CHUNK), :] = acc.astype(o.dtype)

    out_dtypes = [f32, bf16] if emit_bf16 else [f32]
    return pl.pallas_call(
        body, name=name, out_shape=[jax.ShapeDtypeStruct((rows, width), dt) for dt in out_dtypes],
        grid=(width // cb,),
        in_specs=[pl.BlockSpec((rows, cb), lambda j: (0, j + x_cb0)), pl.BlockSpec((taps, cb), lambda j: (0, j)),
                  pl.BlockSpec((1, cb), lambda j: (0, j))],
        out_specs=[pl.BlockSpec((rows, cb), lambda j: (0, j))] * len(out_dtypes),
        scratch_shapes=[pltpu.VMEM((rows + 2 * HALO, cb), f32)],
        compiler_params=_cparams("parallel"),
    )(x, w, b)


def _dwconv_bwd(douts, x, x_cb0, w, taps, pad_left, segments, cb, name, dx_dtype):
    rows = x.shape[0]
    width = w.shape[1]
    nd = len(douts)

    def body(*refs):
        d_refs, x_ref, w_ref = refs[:nd], refs[nd], refs[nd + 1]
        dx_ref, dw_ref, db_ref, dp, dsum = refs[nd + 2:]
        dw_ref[...] = jnp.zeros_like(dw_ref)
        db_ref[...] = jnp.zeros_like(db_ref)
        if nd > 1:
            total = d_refs[0][...]
            for r in d_refs[1:]:
                total = total + r[...]
            dsum[...] = total
            d_ref = dsum
        else:
            d_ref = d_refs[0]
        for start, n in segments:
            _fill_padded(dp, d_ref, start, n)
            for c0 in range(0, n, CONV_CHUNK):
                db_ref[...] += jnp.sum(dp[pl.ds(HALO + c0, CONV_CHUNK), :], axis=0, keepdims=True)
                xchunk = x_ref[pl.ds(start + c0, CONV_CHUNK), :].astype(f32)
                acc = jnp.zeros((CONV_CHUNK, cb), f32)
                for k in range(taps):
                    shifted = dp[pl.ds(HALO + c0 + pad_left - k, CONV_CHUNK), :]
                    acc = acc + w_ref[k:k + 1, :] * shifted
                    dw_ref[k:k + 1, :] += jnp.sum(shifted * xchunk, axis=0, keepdims=True)
                dx_ref[pl.ds(start + c0, CONV_CHUNK), :] = acc.astype(dx_ref.dtype)

    dspec = pl.BlockSpec((rows, cb), lambda j: (0, j))
    return pl.pallas_call(
        body, name=name,
        out_shape=[jax.ShapeDtypeStruct((rows, width), dx_dtype), jax.ShapeDtypeStruct((taps, width), f32),
                   jax.ShapeDtypeStruct((1, width), f32)],
        grid=(width // cb,),
        in_specs=[dspec] * nd + [pl.BlockSpec((rows, cb), lambda j: (0, j + x_cb0)),
                                 pl.BlockSpec((taps, cb), lambda j: (0, j))],
        out_specs=[dspec, pl.BlockSpec((taps, cb), lambda j: (0, j)), pl.BlockSpec((1, cb), lambda j: (0, j))],
        scratch_shapes=[pltpu.VMEM((rows + 2 * HALO, cb), f32), pltpu.VMEM((rows, cb), f32)],
        compiler_params=_cparams("parallel"),
    )(*douts, x, w)


def _ada_forward(c16, w_ada, b_loc):
    def body(c_ref, w_ref, b_ref, o_ref):
        cv = c_ref[...]
        s = (cv * _sigmoid(cv)).astype(bf16)
        o_ref[0] = jnp.dot(s, w_ref[0].astype(bf16), preferred_element_type=f32) + b_ref[0]

    return pl.pallas_call(
        body, name="ada_forward", out_shape=jax.ShapeDtypeStruct((2, 16, ADA_SHARD), f32), grid=(2,),
        in_specs=[pl.BlockSpec((16, D), lambda l: (0, 0)), pl.BlockSpec((1, D, ADA_SHARD), lambda l: (l, 0, 0)),
                  pl.BlockSpec((1, 1, ADA_SHARD), lambda l: (l, 0, 0))],
        out_specs=pl.BlockSpec((1, 16, ADA_SHARD), lambda l: (l, 0, 0)),
        compiler_params=_cparams("parallel"),
    )(c16, w_ada, b_loc)


def _ada_backward(c16, g16, w_ada):
    def body(c_ref, g_ref, w_ref, dw_ref, ds_ref):
        cv = c_ref[...]
        s = (cv * _sigmoid(cv)).astype(bf16)
        g = g_ref[0].astype(bf16)
        dw_ref[0] = lax.dot_general(s, g, (((0,), (0,)), ((), ())), preferred_element_type=f32)
        ds = lax.dot_general(g, w_ref[0].astype(bf16), (((1,), (1,)), ((), ())), preferred_element_type=f32)
        cc = cv[8:9]
        sg = _sigmoid(cc)
        dsilu = sg * (1.0 + cc * (1.0 - sg))
        ds_ref[0] = jnp.zeros((8, D), f32) + jnp.sum(ds[8:16], axis=0, keepdims=True) * dsilu

    return pl.pallas_call(
        body, name="ada_backward",
        out_shape=[jax.ShapeDtypeStruct((2, D, ADA_SHARD), f32), jax.ShapeDtypeStruct((2, 8, D), f32)], grid=(2,),
        in_specs=[pl.BlockSpec((16, D), lambda l: (0, 0)), pl.BlockSpec((1, 16, ADA_SHARD), lambda l: (l, 0, 0)),
                  pl.BlockSpec((1, D, ADA_SHARD), lambda l: (l, 0, 0))],
        out_specs=[pl.BlockSpec((1, D, ADA_SHARD), lambda l: (l, 0, 0)), pl.BlockSpec((1, 8, D), lambda l: (l, 0, 0))],
        compiler_params=_cparams("parallel"),
    )(c16, g16, w_ada)


def _adamw(pieces, w, m, v, name, after=()):
    rows, cols = w.shape
    n_arr, n_after = len(pieces), len(after)
    counts = [cnt for _, cnt in pieces]
    pieces = [p for p, _ in pieces]
    tm = 256 if (rows % 256 == 0 and rows > 256) else rows

    def body(*refs):
        p_refs = refs[:n_arr]
        w_ref, m_ref, v_ref = refs[n_arr:n_arr + 3]
        g_ref, d_ref, nm_ref, nv_ref = refs[n_arr + 3 + n_after:]
        g = None
        for p_ref in p_refs:
            for j in range(p_ref.shape[0]):
                term = p_ref[j].astype(f32)
                g = term if g is None else g + term
        m2 = ADAM_B1 * m_ref[...] + (1.0 - ADAM_B1) * g
        v2 = ADAM_B2 * v_ref[...] + (1.0 - ADAM_B2) * (g * g)
        m_hat = m2 / (1.0 - ADAM_B1 ** ADAM_STEP)
        v_hat = v2 / (1.0 - ADAM_B2 ** ADAM_STEP)
        g_ref[...] = g
        d_ref[...] = -ADAM_LR * (m_hat / (jnp.sqrt(v_hat) + ADAM_EPS) + ADAM_WD * w_ref[...])
        nm_ref[...] = m2
        nv_ref[...] = v2

    spec = pl.BlockSpec((tm, cols), lambda i: (i, 0))
    return pl.pallas_call(
        body, name=name, out_shape=[jax.ShapeDtypeStruct((rows, cols), f32)] * 4, grid=(rows // tm,),
        in_specs=[pl.BlockSpec((cnt, tm, cols), lambda i: (0, i, 0)) for cnt in counts] + [spec, spec, spec]
        + [ANY_SPEC] * n_after,
        out_specs=[spec] * 4, compiler_params=_cparams("parallel"),
    )(*pieces, w, m, v, *after)


MLP_TM = 256
FB = F // N_DEV


def _stack_rows(vals, n):
    cols = vals[0].shape[1]
    rid = lax.broadcasted_iota(jnp.int32, (n, cols), 0)
    out = jnp.zeros((n, cols), f32)
    for k, v in enumerate(vals):
        out = jnp.where(rid == k, v, out)
    return out


N_MLP_PARAMS = 9


class _ParamRows:
    def __init__(self, ref):
        self.ref = ref

    def __getitem__(self, sl):
        return self.ref[8 * sl.start:8 * sl.start + 1, :]


def _resident(shape, imap):
    return pl.BlockSpec(shape, imap, pipeline_mode=pl.Buffered(1))


def _mlp_forward(xa, xa_roff, out_prev, par, w_in, w_out, layer, name):
    def body(xa_ref, op_ref, par_ref, win_ref, wout_ref, x1_ref, h_ref, r_ref, mo_ref, x2_ref, hn_ref):
        p = _ParamRows(par_ref)
        x1 = xa_ref[...] + p[0:1] * (op_ref[...] + p[1:2])
        h = _normmod(x1, p[2:3], p[3:4], p[4:5]).astype(bf16)
        x1_ref[...] = x1
        h_ref[...] = h
        mo = jnp.zeros((MLP_TM, D), f32)
        for j in range(N_DEV):
            r = jnp.maximum(jnp.dot(h, win_ref[j], preferred_element_type=f32), 0.0)
            r_ref[:, j * FB:(j + 1) * FB] = r.astype(bf16)
            mo = mo + jnp.dot((r * r).astype(bf16), wout_ref[j], preferred_element_type=f32)
        mo_ref[...] = mo.astype(bf16)
        x2 = x1 + p[5:6] * mo
        x2_ref[...] = x2
        hn_ref[...] = _normmod(x2, p[6:7], p[7:8], p[8:9]).astype(bf16)

    row = lambda width: pl.BlockSpec((MLP_TM, width), lambda i: (i, 0))
    return pl.pallas_call(
        body, name=name, grid=(T_LAT // MLP_TM,),
        out_shape=[jax.ShapeDtypeStruct((T_LAT, D), f32), jax.ShapeDtypeStruct((T_LAT, D), bf16),
                   jax.ShapeDtypeStruct((T_LAT, F), bf16), jax.ShapeDtypeStruct((T_LAT, D), bf16),
                   jax.ShapeDtypeStruct((T_LAT, D), f32), jax.ShapeDtypeStruct((T_LAT, D), bf16)],
        in_specs=[pl.BlockSpec((MLP_TM, D), lambda i: (i + xa_roff, 0)), row(D), pl.BlockSpec((8 * N_MLP_PARAMS, D), lambda i: (0, 0)),
                  _resident((N_DEV, None, D, FB), lambda i: (0, layer, 0, 0)),
                  _resident((N_DEV, None, FB, D), lambda i: (0, layer, 0, 0))],
        out_specs=[row(D), row(D), row(F), row(D), row(D), row(D)],
        compiler_params=_cparams("parallel"),
    )(xa, out_prev, par, w_in, w_out)


def _mlp_backward(dx2, x1, r, mo, out_prev, par, w_in, w_out, layer, name, after=()):
    nt = (((1,), (1,)), ((), ()))

    n_after = len(after)

    def body(dx2_ref, x1_ref, r_ref, mo_ref, op_ref, par_ref, win_ref, wout_ref, *rest):
        dx1_ref, dop_ref, dmo_ref, dhid_ref, acc_ref = rest[n_after:]
        p = _ParamRows(par_ref)
        dx2v = dx2_ref[...]
        dmo = (p[5:6] * dx2v).astype(bf16)
        dmo_ref[...] = dmo
        dh = jnp.zeros((MLP_TM, D), f32)
        mo = mo_ref[...].astype(f32)
        for j in range(N_DEV):
            rf = r_ref[:, j * FB:(j + 1) * FB].astype(f32)
            dact = lax.dot_general(dmo, wout_ref[j], nt, preferred_element_type=f32)
            dhid = (dact * (2.0 * rf)).astype(bf16)
            dhid_ref[:, j * FB:(j + 1) * FB] = dhid
            dh = dh + lax.dot_general(dhid, win_ref[j], nt, preferred_element_type=f32)
        x1 = x1_ref[...]
        _, vjp = jax.vjp(_normmod, x1, p[2:3], p[3:4], p[4:5])
        dx, dng, dsc, dsh = vjp(dh)
        dx1 = dx2v + dx
        dx1_ref[...] = dx1
        dop_ref[...] = (p[0:1] * dx1).astype(bf16)
        sums = _stack_rows([jnp.sum(dx1 * (op_ref[...] + p[1:2]), axis=0, keepdims=True),
                            p[0:1] * jnp.sum(dx1, axis=0, keepdims=True), dng, dsc, dsh,
                            jnp.sum(dx2v * mo, axis=0, keepdims=True)], 8)

        @pl.when(pl.program_id(0) == 0)
        def _():
            acc_ref[...] = jnp.zeros_like(acc_ref)

        acc_ref[...] += sums

    row = lambda width: pl.BlockSpec((MLP_TM, width), lambda i: (i, 0))
    return pl.pallas_call(
        body, name=name, grid=(T_LAT // MLP_TM,),
        out_shape=[jax.ShapeDtypeStruct((T_LAT, D), f32), jax.ShapeDtypeStruct((T_LAT, D), bf16),
                   jax.ShapeDtypeStruct((T_LAT, D), bf16), jax.ShapeDtypeStruct((T_LAT, F), bf16),
                   jax.ShapeDtypeStruct((8, D), f32)],
        in_specs=[row(D), row(D), row(F), row(D), row(D), pl.BlockSpec((8 * N_MLP_PARAMS, D), lambda i: (0, 0)),
                  _resident((N_DEV, None, D, FB), lambda i: (0, layer, 0, 0)),
                  _resident((N_DEV, None, FB, D), lambda i: (0, layer, 0, 0))] + [ANY_SPEC] * n_after,
        out_specs=[row(D), row(D), row(D), row(F), pl.BlockSpec((8, D), lambda i: (0, 0))],
        compiler_params=_cparams("arbitrary"),
    )(dx2, x1, r, mo, out_prev, par, w_in, w_out, *after)


def _mlp_weight_grads(h, dhid, r, dmo, layer, other, tag):
    tn = (((0,), (0,)), ((), ()))

    def body_in(h_ref, dhid_ref, *rest):
        rest[-1][...] = lax.dot_general(h_ref[...], dhid_ref[...], tn, preferred_element_type=f32).astype(bf16)

    def body_out(r_ref, dmo_ref, *rest):
        rf = r_ref[...].astype(f32)
        rest[-1][...] = lax.dot_general((rf * rf).astype(bf16), dmo_ref[...], tn,
                                        preferred_element_type=f32).astype(bf16)

    def call(body, name, operands, specs, block, prev):
        extra = [] if prev is None else [prev]
        return pl.pallas_call(
            body, name=name, grid=(N_DEV,), out_shape=jax.ShapeDtypeStruct((N_DEV, 2) + block, bf16),
            in_specs=specs + [pl.BlockSpec(memory_space=pl.ANY)] * len(extra),
            out_specs=pl.BlockSpec((None, None) + block, lambda j: (j, layer, 0, 0)),
            input_output_aliases={} if prev is None else {2: 0},
            compiler_params=_cparams("parallel"),
        )(*operands, *extra)

    dw_in = call(body_in, tag + "_mlp_in_dw", [h, dhid],
                 [_resident((T_LAT, D), lambda j: (0, 0)), pl.BlockSpec((T_LAT, FB), lambda j: (0, j))], (D, FB),
                 None if other is None else other[0])
    dw_out = call(body_out, tag + "_mlp_out_dw", [r, dmo],
                  [pl.BlockSpec((T_LAT, FB), lambda j: (0, j)), _resident((T_LAT, D), lambda j: (0, 0))], (FB, D),
                  None if other is None else other[1])
    return dw_in, dw_out


def _pos_embed():
    n_rows = T_LAT // GRID_W
    q = D // 4
    omega = 1.0 / (POS_BASE ** (jnp.arange(q, dtype=f32) / q))
    er = jnp.arange(n_rows, dtype=jnp.int32).astype(f32)[:, None] * omega[None, :]
    ec = jnp.arange(GRID_W, dtype=jnp.int32).astype(f32)[:, None] * omega[None, :]
    by_row = jnp.concatenate([jnp.sin(er), jnp.cos(er)], axis=-1)[:, None, :]
    by_col = jnp.concatenate([jnp.sin(ec), jnp.cos(ec)], axis=-1)[None, :, :]
    full = jnp.concatenate([jnp.broadcast_to(by_row, (n_rows, GRID_W, D // 2)),
                            jnp.broadcast_to(by_col, (n_rows, GRID_W, D // 2))], axis=-1)
    return full.reshape(T_LAT, D)


HALF = R // 2
BLK_PER_HALF = N_BLK // 2
N_PARTS = 4


def _gate_matrix(w_a, w_x):
    eye = jnp.eye(BLK_PER_HALF, dtype=bf16)
    cols = []
    for h in range(2):
        for d in range(2):
            for w in (w_a, w_x):
                blocks = w[d, BLK_PER_HALF * h:BLK_PER_HALF * (h + 1)].astype(bf16)
                cols.append(jnp.einsum("hij,hg->higj", blocks, eye).reshape(HALF, HALF))
    return jnp.concatenate(cols, axis=1)


def _gate_blocks(dwg, part):
    out = []
    for h in range(2):
        blk = dwg[:, (N_PARTS * h + part) * HALF:(N_PARTS * h + part + 1) * HALF]
        blk = blk.reshape(BLK_PER_HALF, BLK, BLK_PER_HALF, BLK)
        out.append(jnp.moveaxis(jnp.diagonal(blk, axis1=0, axis2=2), -1, 0))
    return jnp.concatenate(out, axis=0)


GATE_BM = 768


def _gates_dx(dpre, wg, after=()):
    rows = dpre.shape[0]
    n_after = len(after)

    def body(d_ref, w_ref, *rest):
        rest[n_after][...] = lax.dot_general(d_ref[...], w_ref[...], (((1,), (1,)), ((), ())),
                                             preferred_element_type=f32)

    return pl.pallas_call(
        body, name="l0_gates_dx", grid=(rows // GATE_BM, 2), out_shape=jax.ShapeDtypeStruct((rows, R), f32),
        in_specs=[pl.BlockSpec((GATE_BM, N_PARTS * HALF), lambda i, h: (i, h)),
                  pl.BlockSpec((HALF, N_PARTS * HALF), lambda i, h: (0, h))] + [ANY_SPEC] * n_after,
        out_specs=pl.BlockSpec((GATE_BM, HALF), lambda i, h: (i, h)),
        compiler_params=_cparams("parallel", "parallel"),
    )(dpre, wg, *after)


COEFF_TM = 256


def _dir_params(d, *params):
    specs = [pl.BlockSpec((None, 1, HALF), lambda h, i: (d, 0, h))] * len(params)
    return specs, [p.reshape(2, 1, R) for p in params]


def _gates_coeff_fwd(ub, u, wg, ba, bx, lam, d):
    rows = u.shape[0]

    def body(ub_ref, u_ref, w_ref, ba_ref, bx_ref, lam_ref, a_ref, b_ref):
        pre = jnp.dot(ub_ref[...], w_ref[...], preferred_element_type=f32)
        a, b = _coeff(pre[:, :HALF], pre[:, HALF:], u_ref[...], ba_ref[...], bx_ref[...], lam_ref[...])
        a_ref[...] = a
        b_ref[...] = b

    tile = pl.BlockSpec((COEFF_TM, HALF), lambda h, i: (i, h))
    pspecs, pargs = _dir_params(d, ba, bx, lam)
    return pl.pallas_call(
        body, name=f"l0_gates_coeff_{d}", grid=(2, rows // COEFF_TM),
        out_shape=[jax.ShapeDtypeStruct((rows, R), f32)] * 2,
        in_specs=[tile, tile, pl.BlockSpec((HALF, 2 * HALF), lambda h, i: (0, 2 * h + d))] + pspecs,
        out_specs=[tile, tile], compiler_params=_cparams("parallel", "parallel"),
    )(ub, u, wg, *pargs)


def _gates_coeff_bwd(ub, u, dh, yp, wg, ba, bx, lam, d, dpre_prev):
    rows = u.shape[0]
    n_prev = 0 if dpre_prev is None else 1

    def body(ub_ref, u_ref, dh_ref, yp_ref, w_ref, ba_ref, bx_ref, lam_ref, *rest):
        dpre_ref, du_ref, dba_ref, dbx_ref, dlam_ref = rest[n_prev:]
        pre = jnp.dot(ub_ref[...], w_ref[...], preferred_element_type=f32)
        dhv = dh_ref[...]
        dpa, dpx, du, dba, dbx, dlam = _coeff_bwd(pre[:, :HALF], pre[:, HALF:], u_ref[...], ba_ref[...], bx_ref[...],
                                                  lam_ref[...], dhv * yp_ref[...], dhv)
        dpre_ref[:, :HALF] = dpa.astype(bf16)
        dpre_ref[:, HALF:] = dpx.astype(bf16)
        du_ref[...] = du

        @pl.when(pl.program_id(1) == 0)
        def _():
            dba_ref[...] = jnp.zeros_like(dba_ref)
            dbx_ref[...] = jnp.zeros_like(dbx_ref)
            dlam_ref[...] = jnp.zeros_like(dlam_ref)

        dba_ref[...] += dba
        dbx_ref[...] += dbx
        dlam_ref[...] += dlam

    tile = pl.BlockSpec((COEFF_TM, HALF), lambda h, i: (i, h))
    acc = pl.BlockSpec((1, HALF), lambda h, i: (0, h))
    pspecs, pargs = _dir_params(d, ba, bx, lam)
    extra = [] if dpre_prev is None else [dpre_prev]
    return pl.pallas_call(
        body, name=f"l0_gates_coeff_bwd_{d}", grid=(2, rows // COEFF_TM),
        out_shape=[jax.ShapeDtypeStruct((rows, 2 * N_PARTS * HALF), bf16), jax.ShapeDtypeStruct((rows, R), f32)]
        + [jax.ShapeDtypeStruct((1, R), f32)] * 3,
        in_specs=[tile] * 4 + [pl.BlockSpec((HALF, 2 * HALF), lambda h, i: (0, 2 * h + d))] + pspecs
        + [ANY_SPEC] * n_prev,
        out_specs=[pl.BlockSpec((COEFF_TM, 2 * HALF), lambda h, i: (i, 2 * h + d)), tile, acc, acc, acc],
        input_output_aliases={8: 0} if n_prev else {}, compiler_params=_cparams("parallel", "arbitrary"),
    )(ub, u, dh, yp, wg, *pargs, *extra)


def _gates_dw(u, dpre):
    rows = u.shape[0]

    def body(u_ref, d_ref, o_ref):
        o_ref[...] = lax.dot_general(u_ref[...], d_ref[...], (((0,), (0,)), ((), ())), preferred_element_type=f32)

    return pl.pallas_call(
        body, name="l0_gates_dw", grid=(2 * N_PARTS,), out_shape=jax.ShapeDtypeStruct((HALF, 2 * N_PARTS * HALF), f32),
        in_specs=[pl.BlockSpec((rows, HALF), lambda j: (0, j // N_PARTS)), pl.BlockSpec((rows, HALF), lambda j: (0, j))],
        out_specs=pl.BlockSpec((HALF, HALF), lambda j: (0, j)), compiler_params=_cparams("parallel"),
    )(u, dpre)


N_SCAN_CHUNKS = T_ALL // SCAN_CHUNK
SCAN_FWD = lambda t: t
SCAN_FWD_BWD = lambda t: N_SCAN_CHUNKS - 1 - t
SCAN_REV = lambda t: jnp.where(t == 0, 0, N_SCAN_CHUNKS - t)
SCAN_REV_BWD = lambda t: jnp.where(t == N_SCAN_CHUNKS - 1, 0, t + 1)
CONV_SEGMENTS = ((0, T_CTX), (T_CTX, T_LAT))
TM = 128
FUSED_TM = 256


def _local_step(x, ctx, target, mods, cmod, wts, late_weights, send_grads, reduce_loss, start_after=()):
    sh1, sc1, g1, sh2, sc2, g2 = [[mods[l, i][None] for l in range(2)] for i in range(N_MOD)]
    ng = wts["norm_g"]
    xcat = jnp.concatenate([ctx, x], axis=0)
    poscat = jnp.concatenate([jnp.zeros((T_CTX, D), f32), _pos_embed()], axis=0)
    scp = jnp.concatenate([cmod[1][None], sc1[0]], axis=0)
    shp = jnp.concatenate([cmod[0][None], sh1[0]], axis=0)

    ctx_tiles = T_CTX // FUSED_TM
    nt = (((1,), (1,)), ((), ()))

    def blend(i, p):
        sel = jnp.where(i < ctx_tiles, 1.0, 0.0)
        return sel * p[0:1] + (1.0 - sel) * p[1:2]

    def f_pre0(i, xc, pos, g, scp_, shp_, w):
        x0 = xc + pos
        h = _normmod(x0, g, blend(i, scp_), blend(i, shp_)).astype(bf16)
        return x0, h, jnp.dot(h, w, preferred_element_type=f32)

    x0cat, h0, gr = _rowcall(f_pre0, "l0_prenorm_in_proj", T_ALL, FUSED_TM, [_rin(xcat), _rin(poscat)],
                             [ng[0, 0][None], scp, shp, wts["rec_w_in"]], [(D, f32), (D, bf16), (2 * R, f32)],
                             after=start_after)
    u, ub = _dwconv_fwd(gr, R // 256, wts["rec_conv_w"], wts["rec_conv_b"], 4, 1, CONV_SEGMENTS, 256,
                        "l0_conv", True)
    gate_args = (wts["gates"], wts["rec_b_a"], wts["rec_b_x"], wts["rec_lambda"])
    a0, b0 = _gates_coeff_fwd(ub, u, *gate_args, 0)
    a1, b1 = _gates_coeff_fwd(ub, u, *gate_args, 1)
    halfway = late_weights("mlp_halfway", a1)
    y0, yp0 = _scan_call(a0, b0, SCAN_FWD, False, "l0_scan_fwd", False, after=[halfway])
    y1, yp1 = _scan_call(a1, b1, SCAN_REV, True, "l0_scan_rev", False)

    wts = dict(wts, **late_weights("mlp", y1))

    def f_gate_out(i, gp, y0_, y1_, w):
        z = (_gelu(gp) * (y0_ + y1_)).astype(bf16)
        return z, jnp.dot(z, w, preferred_element_type=f32)

    zb, out0 = _rowcall(f_gate_out, "l0_gate_out_proj", T_LAT, FUSED_TM,
                        [_rin(gr, R, 0, ctx_tiles), _rin(y0, None, 0, ctx_tiles), _rin(y1, None, 0, ctx_tiles)],
                        [wts["rec_w_out"]], [(R, bf16), (D, f32)])

    zero_d = jnp.zeros((1, D), f32)

    def mlp_params(rows):
        rows = rows + [zero_d] * (N_MLP_PARAMS - len(rows))
        return jnp.concatenate([jnp.broadcast_to(r, (8, D)) for r in rows], axis=0)

    par0 = mlp_params([g1[0], zero_d, ng[0, 1][None], sc2[0], sh2[0], g2[0], ng[1, 0][None], sc1[1], sh1[1]])
    x1, h1, r0, mo0, x2, h2 = _mlp_forward(x0cat, T_CTX // MLP_TM, out0, par0, wts["mlp_w_in"], wts["mlp_w_out"], 0,
                                           "l0_mlp")

    wts = dict(wts, **late_weights("conf", x2))
    def glu(pa, pb, b1):
        return (pa + b1[:, :D]) * _sigmoid(pb + b1[:, D:])

    def f_pw1_glu(i, h_, b1, w):
        p = jnp.dot(h_, w, preferred_element_type=f32)
        return glu(p[:, :D], p[:, D:], b1), p

    zg, pw = _rowcall(f_pw1_glu, "l1_pw1_glu", T_LAT, FUSED_TM, [_rin(h2)], [wts["conf_b_pw1"], wts["conf_w_pw1"]],
                      [(D, f32), (2 * D, bf16)])
    (zc,) = _dwconv_fwd(zg, 0, wts["conf_conv_w"], wts["conf_conv_b"], 31, 15, ((0, T_LAT),), 128, "l1_conv", False)

    def ln_silu(z, lg, lb):
        mu = jnp.mean(z, axis=-1, keepdims=True)
        zc_ = z - mu
        var = jnp.mean(zc_ * zc_, axis=-1, keepdims=True)
        yv = zc_ * lax.rsqrt(var + EPS) * lg + lb
        return yv * _sigmoid(yv)

    def f_lnsilu_pw2(i, z, lg, lb, w):
        s = ln_silu(z, lg, lb).astype(bf16)
        return s, jnp.dot(s, w, preferred_element_type=f32)

    sb, out1 = _rowcall(f_lnsilu_pw2, "l1_ln_silu_pw2", T_LAT, FUSED_TM, [_rin(zc)],
                        [wts["conf_ln_g"], wts["conf_ln_b"], wts["conf_w_pw2"]], [(D, bf16), (D, f32)])
    par1 = mlp_params([g1[1], wts["conf_b_pw2"], ng[1, 1][None], sc2[1], sh2[1], g2[1]])
    x3, h3, r1, mo1, x4, _ = _mlp_forward(x2, 0, out1, par1, wts["mlp_w_in"], wts["mlp_w_out"], 1, "l1_mlp")

    def loss_fn(x4_, fg, tgt):
        err = _rms(x4_, fg) - tgt
        per_row = jnp.mean(err * err, axis=-1, keepdims=True)
        return 0.5 * jnp.sum(per_row, axis=0, keepdims=True)

    def f_head(i, x4_, tgt, fg):
        loss, vjp = jax.vjp(lambda a, e: loss_fn(a, e, tgt), x4_, fg)
        dx, dfg = vjp(jnp.ones((1, 1), f32))
        return dx, jnp.broadcast_to(loss, (1, 128)), dfg

    dx4, loss_acc, dfinal_g = _rowcall(f_head, "head", T_LAT, TM, [_rin(x4), _rin(target)], [wts["final_g"]],
                                       [(D, f32)], [(1, 128), (1, D)])

    grads = {"final_g": dfinal_g}
    loss = reduce_loss(loss_acc[0, 0])

    dx3, dout1, dmo1, dhid1, acc1 = _mlp_backward(dx4, x3, r1, mo1, out1, par1, wts["mlp_w_in"], wts["mlp_w_out"], 1,
                                                  "l1_mlp_bwd", after=[loss.reshape(1, 1)])
    mlp_dw = _mlp_weight_grads(h3, dhid1, r1, dmo1, 1, None, "l1")
    dg1_1, db_pw2, dng11, dsc2_1, dsh2_1, dg2_1 = [acc1[k:k + 1] for k in range(6)]

    grads["conf_w_pw2"] = _mm(sb, dout1, "l1_pw2_dw", ta=True, out_dtype=bf16)
    grads["conf_b_pw2"] = db_pw2

    def f_pw2_lnsilu_bwd(i, z, dout, lg, lb, w):
        ds = lax.dot_general(dout, w, nt, preferred_element_type=f32)
        _, vjp = jax.vjp(ln_silu, z, lg, lb)
        return vjp(ds)

    dzc, dln_g, dln_b = _rowcall(f_pw2_lnsilu_bwd, "l1_pw2_ln_silu_bwd", T_LAT, FUSED_TM, [_rin(zc), _rin(dout1)],
                                 [wts["conf_ln_g"], wts["conf_ln_b"], wts["conf_w_pw2"]], [(D, f32)], [(1, D)] * 2)
    grads["conf_ln_g"], grads["conf_ln_b"] = dln_g, dln_b
    dzg, dconv_w, dconv_b = _dwconv_bwd([dzc], zg, 0, wts["conf_conv_w"], 31, 15, ((0, T_LAT),), 128,
                                        "l1_conv_bwd", f32)
    grads["conf_conv_w"], grads["conf_conv_b"] = dconv_w, dconv_b

    def f_glu_pw1_norm_bwd(i, p_, dz, x_, dxs, b1, g_, sc_, sh_, w):
        pf = p_.astype(f32)
        _, vjp = jax.vjp(glu, pf[:, :D], pf[:, D:], b1)
        da, db, db1 = vjp(dz)
        dp = jnp.concatenate([da, db], axis=1).astype(bf16)
        dh = lax.dot_general(dp, w, nt, preferred_element_type=f32)
        _, vjp = jax.vjp(_normmod, x_, g_, sc_, sh_)
        dx, dg, dsc, dsh = vjp(dh)
        return dp, dx + dxs, db1, dg, dsc, dsh

    dpw, dx2, db_pw1, dng10, dsc1_1, dsh1_1 = _rowcall(
        f_glu_pw1_norm_bwd, "l1_glu_pw1_normmod_bwd", T_LAT, FUSED_TM, [_rin(pw), _rin(dzg), _rin(x2), _rin(dx3)],
        [wts["conf_b_pw1"], ng[1, 0][None], sc1[1], sh1[1], wts["conf_w_pw1"]], [(2 * D, bf16), (D, f32)],
        [(1, 2 * D), (1, D), (1, D), (1, D)])
    grads["conf_b_pw1"] = db_pw1
    grads["conf_w_pw1"] = _mm(h2, dpw, "l1_pw1_dw", ta=True, out_dtype=bf16)
    sent = send_grads(["conf_w_pw2", "conf_w_pw1"], grads)

    dx1, dout0, dmo0, dhid0, acc0 = _mlp_backward(dx2, x1, r0, mo0, out0, par0, wts["mlp_w_in"], wts["mlp_w_out"], 0,
                                                  "l0_mlp_bwd", after=[sent])
    grads["mlp_w_in"], grads["mlp_w_out"] = _mlp_weight_grads(h1, dhid0, r0, dmo0, 0, mlp_dw, "l0")
    sent = send_grads(["mlp_w_in", "mlp_w_out"], grads)
    dg1_0, _, dng01, dsc2_0, dsh2_0, dg2_0 = [acc0[k:k + 1] for k in range(6)]

    grads["rec_w_out"] = _mm(zb, dout0, "l0_out_proj_dw", ta=True, out_dtype=bf16, after=[sent])
    sent = send_grads(["rec_w_out"], grads)

    def f_out_gate_bwd(i, gp, y0_, y1_, dout, w):
        lat = jnp.where(i < ctx_tiles, 0.0, 1.0)
        dz = lax.dot_general(dout, w, nt, preferred_element_type=f32)
        _, vjp = jax.vjp(lambda a, b: _gelu(a) * b, gp, y0_ + y1_)
        dgp, dy = vjp(dz)
        return dgp * lat, dy * lat

    dgp, dy = _rowcall(f_out_gate_bwd, "l0_out_proj_gate_bwd", T_ALL, FUSED_TM,
                       [_rin(gr, R, 0), _rin(y0), _rin(y1), _rin(dout0, None, 0, -ctx_tiles)], [wts["rec_w_out"]],
                       [(R, bf16), (R, f32)], after=[sent])
    (dh_f,) = _scan_call(a0, dy, SCAN_FWD_BWD, True, "l0_scan_fwd_bwd", True)
    (dh_r,) = _scan_call(a1, dy, SCAN_REV_BWD, False, "l0_scan_rev_bwd", True)

    dpre, du_f, *dpar_f = _gates_coeff_bwd(ub, u, dh_f, yp0, *gate_args, 0, None)
    dpre, du_r, *dpar_r = _gates_coeff_bwd(ub, u, dh_r, yp1, *gate_args, 1, dpre)
    grads["rec_b_a"], grads["rec_b_x"], grads["rec_lambda"] = [
        jnp.concatenate([f.reshape(-1), r_.reshape(-1)]).reshape(2, R) for f, r_ in zip(dpar_f, dpar_r)]
    grads["gates"] = _gates_dw(ub, dpre)
    sent = send_grads(["replicated"], grads)
    du_gates = _gates_dx(dpre, wts["gates"], after=[sent])
    drec, dconv4_w, dconv4_b = _dwconv_bwd([du_f, du_r, du_gates], gr, R // 256, wts["rec_conv_w"], 4, 1,
                                           CONV_SEGMENTS, 256, "l0_conv_bwd", bf16)
    grads["rec_conv_w"], grads["rec_conv_b"] = dconv4_w, dconv4_b
    dgr = jnp.concatenate([dgp, drec], axis=1)
    grads["rec_w_in"] = _mm(h0, dgr, "l0_in_proj_dw", ta=True, out_dtype=bf16)
    sent = send_grads(["rec_w_in"], grads)

    def f_pre0_bwd(i, x0, dgr_, dxs, g, scp_, shp_, w):
        lat = jnp.where(i < ctx_tiles, 0.0, 1.0)
        dh = lax.dot_general(dgr_, w, nt, preferred_element_type=f32)
        _, vjp = jax.vjp(lambda a, b, c, e: _normmod(a, b, blend(i, c), blend(i, e)), x0, g, scp_, shp_)
        dx, dg, dscp, dshp = vjp(dh)
        return dx + lat * dxs, dg, dscp, dshp

    dx0cat, dng00, dscp, dshp = _rowcall(
        f_pre0_bwd, "l0_in_proj_prenorm_bwd", T_ALL, FUSED_TM,
        [_rin(x0cat), _rin(dgr), _rin(dx1, None, 0, -ctx_tiles)], [ng[0, 0][None], scp, shp, wts["rec_w_in"]],
        [(D, f32)], [(1, D), (2, D), (2, D)], after=[sent])

    grads["norm_g"] = jnp.stack([jnp.concatenate([dng00, dng01], 0), jnp.concatenate([dng10, dng11], 0)])
    dmods = jnp.stack([
        jnp.concatenate([dshp[1:2], dscp[1:2], dg1_0, dsh2_0, dsc2_0, dg2_0], axis=0),
        jnp.concatenate([dsh1_1, dsc1_1, dg1_1, dsh2_1, dsc2_1, dg2_1], axis=0)])
    dcmod = jnp.concatenate([dshp[0:1], dscp[0:1]], axis=0)
    return loss, dx0cat[T_CTX:], dmods, dcmod, grads


def _unshard_cols(g):
    g = jnp.moveaxis(g, 0, -2)
    return g.reshape(g.shape[:-2] + (g.shape[-2] * g.shape[-1],))


def _shard_cols(w):
    w = w.reshape(w.shape[:-1] + (N_DEV, w.shape[-1] // N_DEV))
    return jnp.moveaxis(w, -2, 0)


def _shard_rows(w):
    return w.reshape((N_DEV, w.shape[0] // N_DEV) + w.shape[1:])


SMALL_PACK_ROWS = 64


def kernel(x, c, ctx, c_ctx, w_ada, b_ada, norm_g, rec_w_in, rec_conv_w, rec_conv_b, rec_lambda, rec_w_a, rec_b_a, rec_w_x, rec_b_x, rec_w_out, conf_w_pw1, conf_b_pw1, conf_conv_w, conf_conv_b, conf_ln_g, conf_ln_b, conf_w_pw2, conf_b_pw2, mlp_w_in, mlp_w_out, final_g, loss_target, m_c_ctx, m_w_ada, m_b_ada, m_norm_g, m_rec_w_in, m_rec_conv_w, m_rec_conv_b, m_rec_lambda, m_rec_w_a, m_rec_b_a, m_rec_w_x, m_rec_b_x, m_rec_w_out, m_conf_w_pw1, m_conf_b_pw1, m_conf_conv_w, m_conf_conv_b, m_conf_ln_g, m_conf_ln_b, m_conf_w_pw2, m_conf_b_pw2, m_mlp_w_in, m_mlp_w_out, m_final_g, v_c_ctx, v_w_ada, v_b_ada, v_norm_g, v_rec_w_in, v_rec_conv_w, v_rec_conv_b, v_rec_lambda, v_rec_w_a, v_rec_b_a, v_rec_w_x, v_rec_b_x, v_rec_w_out, v_conf_w_pw1, v_conf_b_pw1, v_conf_conv_w, v_conf_conv_b, v_conf_ln_g, v_conf_ln_b, v_conf_w_pw2, v_conf_b_pw2, v_mlp_w_in, v_mlp_w_out, v_final_g):
    me = 4 * lax.axis_index("x") + 2 * lax.axis_index("y") + lax.axis_index("c")
    weights = dict(c_ctx=c_ctx, w_ada=w_ada, b_ada=b_ada, norm_g=norm_g, rec_w_in=rec_w_in, rec_conv_w=rec_conv_w,
                   rec_conv_b=rec_conv_b, rec_lambda=rec_lambda, rec_w_a=rec_w_a, rec_b_a=rec_b_a, rec_w_x=rec_w_x,
                   rec_b_x=rec_b_x, rec_w_out=rec_w_out, conf_w_pw1=conf_w_pw1, conf_b_pw1=conf_b_pw1,
                   conf_conv_w=conf_conv_w, conf_conv_b=conf_conv_b, conf_ln_g=conf_ln_g, conf_ln_b=conf_ln_b,
                   conf_w_pw2=conf_w_pw2, conf_b_pw2=conf_b_pw2, mlp_w_in=mlp_w_in, mlp_w_out=mlp_w_out, final_g=final_g)
    m_in = dict(c_ctx=m_c_ctx, w_ada=m_w_ada, b_ada=m_b_ada, norm_g=m_norm_g, rec_w_in=m_rec_w_in, rec_conv_w=m_rec_conv_w,
                rec_conv_b=m_rec_conv_b, rec_lambda=m_rec_lambda, rec_w_a=m_rec_w_a, rec_b_a=m_rec_b_a, rec_w_x=m_rec_w_x,
                rec_b_x=m_rec_b_x, rec_w_out=m_rec_w_out, conf_w_pw1=m_conf_w_pw1, conf_b_pw1=m_conf_b_pw1,
                conf_conv_w=m_conf_conv_w, conf_conv_b=m_conf_conv_b, conf_ln_g=m_conf_ln_g, conf_ln_b=m_conf_ln_b,
                conf_w_pw2=m_conf_w_pw2, conf_b_pw2=m_conf_b_pw2, mlp_w_in=m_mlp_w_in, mlp_w_out=m_mlp_w_out,
                final_g=m_final_g)
    v_in = dict(c_ctx=v_c_ctx, w_ada=v_w_ada, b_ada=v_b_ada, norm_g=v_norm_g, rec_w_in=v_rec_w_in, rec_conv_w=v_rec_conv_w,
                rec_conv_b=v_rec_conv_b, rec_lambda=v_rec_lambda, rec_w_a=v_rec_w_a, rec_b_a=v_rec_b_a, rec_w_x=v_rec_w_x,
                rec_b_x=v_rec_b_x, rec_w_out=v_rec_w_out, conf_w_pw1=v_conf_w_pw1, conf_b_pw1=v_conf_b_pw1,
                conf_conv_w=v_conf_conv_w, conf_conv_b=v_conf_conv_b, conf_ln_g=v_conf_ln_g, conf_ln_b=v_conf_ln_b,
                conf_w_pw2=v_conf_w_pw2, conf_b_pw2=v_conf_b_pw2, mlp_w_in=v_mlp_w_in, mlp_w_out=v_mlp_w_out,
                final_g=v_final_g)
    names = list(weights)

    small_items = [c, norm_g, rec_conv_w, rec_lambda, conf_b_pw1, conf_conv_w, conf_conv_b, conf_ln_g, conf_ln_b,
                   conf_b_pw2]
    flat = jnp.concatenate([a.reshape(-1) for a in small_items])
    flat = jnp.pad(flat, (0, SMALL_PACK_ROWS * 128 - flat.shape[0])).reshape(SMALL_PACK_ROWS, 128)
    as_shard = lambda a: a.astype(bf16).reshape(-1, a.shape[-1])
    early_srcs = [flat, as_shard(rec_w_in[0])]
    early_handle, started = _exchange_start(early_srcs, [_own_block_filled(s, me) for s in early_srcs],
                                            "gather_early_start", False)
    zero = started[0, 0]
    gates = _gate_matrix(rec_w_a[0] + zero, rec_w_x[0] + zero)
    late_items = {"mlp": [rec_w_out[0], mlp_w_in, mlp_w_out], "conf": [conf_w_pw1[0], conf_w_pw2[0]]}
    late_shards = {g: [as_shard(a + zero) for a in items] for g, items in late_items.items()}
    late_lands = {g: [_own_block_filled(s, me) for s in shards] for g, shards in late_shards.items()}
    small_all, early = _exchange_wait(early_handle, [gates] + late_lands["mlp"] + late_lands["conf"],
                                      "gather_early_wait", False)

    small_all = small_all.reshape(N_DEV, -1)
    off = 0
    small = []
    for a in small_items:
        small.append(small_all[:, off:off + a.size].reshape((N_DEV,) + a.shape))
        off += a.size
    c_all, ng_all, rcw_all, lam_all, bpw1_all, ccw_all, ccb_all, lng_all, lnb_all, bpw2_all = small
    wts = {
        "norm_g": _unshard_cols(ng_all),
        "rec_conv_w": _unshard_cols(rcw_all)[0],
        "rec_lambda": _unshard_cols(lam_all)[0],
        "conf_b_pw1": _unshard_cols(bpw1_all),
        "conf_conv_w": _unshard_cols(ccw_all)[0],
        "conf_conv_b": _unshard_cols(ccb_all),
        "conf_ln_g": _unshard_cols(lng_all),
        "conf_ln_b": _unshard_cols(lnb_all),
        "conf_b_pw2": _unshard_cols(bpw2_all),
        "rec_conv_b": rec_conv_b,
        "rec_b_a": rec_b_a[0].reshape(2, R),
        "rec_b_x": rec_b_x[0].reshape(2, R),
        "final_g": final_g[None],
        "gates": gates,
    }

    c16 = jnp.concatenate([c_all[:, 0], jnp.broadcast_to(c_ctx[None], (8, D))], axis=0)
    b_loc = lax.dynamic_slice_in_dim(b_ada, me * ADA_SHARD, ADA_SHARD, axis=1)[:, None]
    (mods_all,) = _all_gather([_ada_forward(c16, w_ada, b_loc)], "gather_mods")
    mods_all = _unshard_cols(mods_all)
    mods = lax.dynamic_index_in_dim(mods_all, me, axis=1, keepdims=False).reshape(2, N_MOD, D)
    cmod = mods_all[0, 8, :2 * D].reshape(2, D)

    wts["rec_w_in"] = _unshard_cols(early)
    late_handles = {}
    late_handles["mlp"], token = _gather2_start(late_shards["mlp"], late_lands["mlp"], "gather_mlp_start", [early, mods])
    order = [token]

    def late_weights(group, after):
        if group == "mlp_halfway":
            late_handles["mlp"] = _gather2_forward1(late_handles["mlp"], after, "gather_mlp_forward1")
            return late_handles["mlp"][2][0]
        if group == "mlp":
            passed = _gather2_forward2(late_handles["mlp"], after, "gather_mlp_forward2")
            late_handles["conf"], started = _exchange_start(late_shards["conf"], late_lands["conf"], "gather_conf_start",
                                                            False, after=[passed[2][0]])
            got = _gather2_wait(passed, started, "gather_mlp_wait")
        else:
            got = _exchange_wait(late_handles[group], after, "gather_conf_wait", False)
        got = [g.reshape((N_DEV,) + a.shape) for g, a in zip(got, late_items[group])]
        if group == "mlp":
            return {"rec_w_out": got[0].reshape(R, D), "mlp_w_in": got[1], "mlp_w_out": got[2]}
        return {"conf_w_pw1": _unshard_cols(got[0]), "conf_w_pw2": got[1].reshape(D, D)}

    to_blocks = {"rec_w_in": _shard_cols, "conf_w_pw1": _shard_cols, "rec_w_out": _shard_rows, "conf_w_pw2": _shard_rows,
                 "mlp_w_in": lambda g: g, "mlp_w_out": lambda g: g}
    grad_handles = []

    repl_names = ["rec_w_a", "rec_w_x", "rec_b_a", "rec_b_x", "final_g"]

    def send_replicated(grads):
        dwg = grads["gates"]
        repl = {"rec_w_a": jnp.stack([_gate_blocks(dwg, 0), _gate_blocks(dwg, 2)]),
                "rec_w_x": jnp.stack([_gate_blocks(dwg, 1), _gate_blocks(dwg, 3)]),
                "rec_b_a": grads["rec_b_a"], "rec_b_x": grads["rec_b_x"], "final_g": grads["final_g"]}
        flat = jnp.concatenate([repl[n].reshape(-1) for n in repl_names])
        rows = -(-flat.shape[0] // (16 * D)) * 16
        flat = jnp.pad(flat, (0, rows * D - flat.shape[0])).reshape(rows, D).astype(bf16)
        handle, sent = _exchange_start([flat], [_own_block_filled(flat, me)], "gather_replicated_start", False)
        grad_handles.append((["replicated"], handle))
        return sent

    def send_grads(group, grads):
        if group == ["replicated"]:
            return send_replicated(grads)
        blocks = [to_blocks[n](grads[n]) for n in group]
        blocks = [g.reshape(N_DEV, -1, g.shape[-1]) for g in blocks]
        lands = [_own_block_filled(lax.dynamic_index_in_dim(g, me, 0, keepdims=False), me) for g in blocks]
        handle, sent = _exchange_start(blocks, lands, "scatter_start_" + group[0], True)
        grad_handles.append((group, handle))
        return sent

    loss, grad_x, dmods, dcmod, grads = _local_step(
        x[0], ctx[0], loss_target[0], mods, cmod, wts, late_weights, send_grads,
        lambda partial: lax.psum(partial, ("x", "y", "c")), start_after=order)

    def as2d(shape):
        rows = 1
        for s in shape[:-1]:
            rows *= s
        return (rows, shape[-1])

    def whole(arr, shape):
        arr = arr.reshape((-1,) + as2d(shape))
        return (arr, arr.shape[0])

    shard_shapes = {n: weights[n].shape for n in names}
    g_out, d_out, m_out, v_out = {}, {}, {}, {}

    def adamw(n, pieces, after):
        shape = shard_shapes[n]
        r2, c2 = as2d(shape)
        g, dl, nm, nv = _adamw(pieces, weights[n].reshape(r2, c2), m_in[n].reshape(r2, c2), v_in[n].reshape(r2, c2),
                               "adamw_" + n, after=after)
        g_out[n], d_out[n], m_out[n], v_out[n] = (t.reshape(shape) for t in (g, dl, nm, nv))
        return g

    small_sharded = ["norm_g", "rec_conv_w", "rec_lambda", "conf_b_pw1", "conf_conv_w", "conf_conv_b", "conf_ln_g",
                     "conf_ln_b", "conf_b_pw2"]
    pack = jnp.concatenate([_shard_cols(grads[n]).reshape(N_DEV, -1) for n in small_sharded], axis=1)
    pack = jnp.pad(pack, ((0, 0), (0, SMALL_PACK_ROWS * 128 - pack.shape[1]))).reshape(N_DEV, SMALL_PACK_ROWS, 128)
    small_handle, token = _exchange_start(
        [pack], [_own_block_filled(lax.dynamic_index_in_dim(pack, me, 0, keepdims=False), me)], "scatter_small_start",
        True, after=[grad_x])
    dm_flat = jnp.concatenate([dmods.reshape(-1), dcmod.reshape(-1), grads["rec_conv_b"].reshape(-1)])
    dm_len = dm_flat.shape[0]
    dm_flat = jnp.pad(dm_flat, (0, 128 * 128 - dm_len)).reshape(128, 128)
    dm_handle, token = _exchange_start([dm_flat], [_own_block_filled(dm_flat, me)], "gather_dmods_start", False,
                                       after=[token])

    done = token
    for group, handle in grad_handles:
        if group == ["replicated"]:
            repl_all = _exchange_wait(handle, done, "gather_replicated_wait", False)[0].reshape(N_DEV, -1)
            off = 0
            for n in repl_names:
                size = weights[n].size
                done = adamw(n, [whole(repl_all[:, off:off + size], shard_shapes[n])], [done])
                off += size
            continue
        for n, got in zip(group, _exchange_wait(handle, done, "scatter_wait_" + group[0], True)):
            done = adamw(n, [(got, N_DEV)], [done])

    dm_all = _exchange_wait(dm_handle, done, "gather_dmods_wait", False)[0].reshape(N_DEV, -1)
    dmods_all = dm_all[:, :2 * N_MOD * D].reshape(N_DEV, 2, N_MOD * D)
    dcmod_all = jnp.pad(dm_all[:, 2 * N_MOD * D:2 * N_MOD * D + 2 * D], ((0, 0), (0, (N_MOD - 2) * D)))
    g16_full = jnp.stack([jnp.concatenate([dmods_all[:, 0], dcmod_all], axis=0),
                          jnp.concatenate([dmods_all[:, 1], jnp.zeros_like(dcmod_all)], axis=0)])
    g16 = lax.dynamic_slice_in_dim(g16_full, me * ADA_SHARD, ADA_SHARD, axis=2)
    dw_ada, ds_part = _ada_backward(c16, g16, w_ada)
    ds_handle, token = _exchange_start([ds_part[0]], [_own_block_filled(ds_part[0], me)], "gather_dsilu_start", False)
    done = adamw("w_ada", [whole(dw_ada, shard_shapes["w_ada"])], [token])
    done = adamw("rec_conv_b", [whole(dm_all[:, dm_len - R:dm_len], shard_shapes["rec_conv_b"])], [done])
    db_terms = jnp.concatenate([dmods_all, jnp.stack([dcmod_all, jnp.zeros_like(dcmod_all)], axis=1)], axis=0)
    done = adamw("b_ada", [whole(db_terms, shard_shapes["b_ada"])], [done])
    pack_recv = _exchange_wait(small_handle, done, "scatter_small_wait", True)[0].reshape(N_DEV, -1)
    off = 0
    for n in small_sharded:
        size = weights[n].size
        done = adamw(n, [whole(pack_recv[:, off:off + size], shard_shapes[n])], [done])
        off += size
    ds_all = _exchange_wait(ds_handle, done, "gather_dsilu_wait", False)[0]
    adamw("c_ctx", [whole(ds_all[:, 0], shard_shapes["c_ctx"])], [])

    return (loss, grad_x[None], *[g_out[n] for n in names], *[d_out[n] for n in names],
            *[m_out[n] for n in names], *[v_out[n] for n in names])
```

```python
import functools

import jax
import jax.numpy as jnp
from jax import lax
from jax.experimental import pallas as pl
from jax.experimental.pallas import tpu as pltpu

f32 = jnp.float32
bf16 = jnp.bfloat16

N_DEV = 8
D = 1024
T_LAT = 2048
T_CTX = 256
T_ALL = T_CTX + T_LAT
R = 1280
N_BLK = 16
BLK = R // N_BLK
F = 4096
GRID_W = 64
RG_C = 8.0
EPS = 1e-6
POS_BASE = 10000.0
N_MOD = 6
ADA_SHARD = N_MOD * D // N_DEV

ADAM_LR = 0.001
ADAM_B1 = 0.9
ADAM_B2 = 0.999
ADAM_EPS = 1e-08
ADAM_WD = 0.01
ADAM_STEP = 10

VMEM_LIMIT_V7X = 56 * 1024 * 1024
HALO = 16
MESH = pl.DeviceIdType.MESH


def _cparams(*sem):
    return pltpu.CompilerParams(dimension_semantics=sem, vmem_limit_bytes=VMEM_LIMIT_V7X)


def _pick(n, cands):
    for c in cands:
        if n % c == 0:
            return c
    raise ValueError(f"no block size for {n}")


def _position():
    x, y, c = lax.axis_index("x"), lax.axis_index("y"), lax.axis_index("c")
    return x, y, c, 4 * x + 2 * y + c


def _peer(x, y, c, k):
    px = (1 - x) if (k >> 2) & 1 else x
    py = (1 - y) if (k >> 1) & 1 else y
    pc = (1 - c) if k & 1 else c
    return (px, py, pc), 4 * px + 2 * py + pc


def _exchange(arrs, name, scatter):
    n = len(arrs)

    def body(*refs):
        ins, outs = refs[:n], refs[n:2 * n]
        send_sems, recv_sems, local_sems = refs[2 * n:]
        x, y, c, me = _position()
        local = []
        for a in range(n):
            src = ins[a].at[me] if scatter else ins[a]
            cp = pltpu.make_async_copy(src, outs[a].at[me], local_sems.at[a])
            cp.start()
            local.append(cp)
        sends, recvs = [], []
        for a in range(n):
            for k in range(1, N_DEV):
                peer, peer_lin = _peer(x, y, c, k)
                src = ins[a].at[peer_lin] if scatter else ins[a]
                cp = pltpu.make_async_remote_copy(
                    src_ref=src, dst_ref=outs[a].at[me], send_sem=send_sems.at[a, k - 1],
                    recv_sem=recv_sems.at[a, k - 1], device_id=peer, device_id_type=MESH)
                cp.start()
                sends.append(cp)
                recvs.append(pltpu.make_async_remote_copy(
                    src_ref=src, dst_ref=outs[a].at[peer_lin], send_sem=send_sems.at[a, k - 1],
                    recv_sem=recv_sems.at[a, k - 1], device_id=peer, device_id_type=MESH))
        for cp in recvs:
            cp.wait_recv()
        for cp in sends:
            cp.wait_send()
        for cp in local:
            cp.wait()

    if scatter:
        out_shape = [jax.ShapeDtypeStruct(a.shape, a.dtype) for a in arrs]
    else:
        out_shape = [jax.ShapeDtypeStruct((N_DEV,) + a.shape, a.dtype) for a in arrs]
    any_spec = pl.BlockSpec(memory_space=pl.ANY)
    return pl.pallas_call(
        body, name=name, out_shape=out_shape,
        in_specs=[any_spec] * n, out_specs=[any_spec] * n,
        scratch_shapes=[pltpu.SemaphoreType.DMA((n, N_DEV - 1)), pltpu.SemaphoreType.DMA((n, N_DEV - 1)),
                        pltpu.SemaphoreType.DMA((n,))],
    )(*arrs)


def _all_gather(arrs, name):
    return _exchange(arrs, name, scatter=False)


def _lin(p):
    return 4 * p[0] + 2 * p[1] + p[2]


HBM_SPEC = pl.BlockSpec(memory_space=pltpu.HBM)
SEM_SPEC = pl.BlockSpec(memory_space=pltpu.SEMAPHORE)
DATAFLOW_EFFECT = pltpu.SideEffectType.DATAFLOW_SIDE_EFFECTING


def _split_copies(srcs, lands, send_sems, recv_sems, scatter):
    x, y, c, me = _position()
    out = []
    for a in range(len(srcs)):
        for k in range(1, N_DEV):
            peer, peer_lin = _peer(x, y, c, k)
            src = srcs[a].at[peer_lin] if scatter else srcs[a]
            mk = lambda slot: pltpu.make_async_remote_copy(
                src_ref=src, dst_ref=lands[a].at[slot], send_sem=send_sems.at[a * (N_DEV - 1) + k - 1],
                recv_sem=recv_sems.at[a * (N_DEV - 1) + k - 1], device_id=peer, device_id_type=MESH)
            out.append((mk(me), mk(peer_lin)))
    return out


def _exchange_start(srcs, lands, name, scatter, after=()):
    n = len(srcs)
    n_after = len(after)

    def body(*refs):
        srcs_r, lands_r = refs[:n], refs[n:2 * n]
        send_sems, recv_sems = refs[2 * n + n_after], refs[2 * n + n_after + 1]
        token = refs[-1]
        for outgoing, _ in _split_copies(srcs_r, lands_r, send_sems, recv_sems, scatter):
            outgoing.start()
        token[...] = jnp.zeros_like(token)

    hbm = lambda a: pltpu.HBM(a.shape, a.dtype)
    res = pl.pallas_call(
        body, name=name,
        out_shape=(pltpu.SemaphoreType.DMA((n * (N_DEV - 1),)), pltpu.SemaphoreType.DMA((n * (N_DEV - 1),)),
                   *[hbm(a) for a in srcs], *[hbm(a) for a in lands], jax.ShapeDtypeStruct((8, 128), f32)),
        in_specs=[HBM_SPEC] * (2 * n) + [pl.BlockSpec(memory_space=pl.ANY)] * n_after,
        out_specs=(SEM_SPEC, SEM_SPEC, *[HBM_SPEC] * (2 * n), pl.BlockSpec(memory_space=pltpu.VMEM)),
        input_output_aliases={i: 2 + i for i in range(2 * n)},
        compiler_params=pltpu.CompilerParams(has_side_effects=DATAFLOW_EFFECT),
    )(*[pltpu.with_memory_space_constraint(a, pltpu.HBM) for a in list(srcs) + list(lands)], *after)
    return (res[0], res[1], list(res[2:2 + n]), list(res[2 + n:2 + 2 * n])), res[-1]


def _exchange_wait(handle, after, name, scatter):
    send_sems, recv_sems, srcs, lands = handle
    n = len(srcs)
    after = list(after) if isinstance(after, (list, tuple)) else [after]

    def body(*refs):
        srcs_r, lands_r = refs[:n], refs[n:2 * n]
        send_s, recv_s = refs[2 * n], refs[2 * n + 1]
        for outgoing, incoming in _split_copies(srcs_r, lands_r, send_s, recv_s, scatter):
            outgoing.wait_send()
            incoming.wait_recv()

    hbm = lambda a: pltpu.HBM(a.shape, a.dtype)
    res = pl.pallas_call(
        body, name=name, out_shape=tuple(hbm(a) for a in list(srcs) + list(lands)),
        in_specs=[HBM_SPEC] * (2 * n) + [SEM_SPEC, SEM_SPEC] + [pl.BlockSpec(memory_space=pl.ANY)] * len(after),
        out_specs=tuple([HBM_SPEC] * (2 * n)),
        input_output_aliases={i: i for i in range(2 * n)},
        compiler_params=pltpu.CompilerParams(has_side_effects=DATAFLOW_EFFECT),
    )(*srcs, *lands, send_sems, recv_sems, *after)
    return list(res[n:])


def _split_call(body, name, hbm_ins, kept, in_sems, n_new_sems, after, with_token):
    n_in, n_sem = len(hbm_ins), len(in_sems)
    out_shape, out_specs = [], []
    if n_new_sems:
        out_shape += [pltpu.SemaphoreType.DMA((n_new_sems,))] * 2
        out_specs += [SEM_SPEC] * 2
    first_kept = len(out_shape)
    out_shape += [pltpu.HBM(hbm_ins[i].shape, hbm_ins[i].dtype) for i in kept]
    out_specs += [HBM_SPEC] * len(kept)
    if with_token:
        out_shape.append(jax.ShapeDtypeStruct((8, 128), f32))
        out_specs.append(pl.BlockSpec(memory_space=pltpu.VMEM))

    def wrapped(*refs):
        outs = refs[n_in + n_sem + len(after):]
        body(refs[:n_in], refs[n_in:n_in + n_sem], outs[:2] if n_new_sems else ())
        if with_token:
            outs[-1][...] = jnp.zeros_like(outs[-1])

    return pl.pallas_call(
        wrapped, name=name, out_shape=tuple(out_shape),
        in_specs=[HBM_SPEC] * n_in + [SEM_SPEC] * n_sem + [pl.BlockSpec(memory_space=pl.ANY)] * len(after),
        out_specs=tuple(out_specs), input_output_aliases={i: first_kept + j for j, i in enumerate(kept)},
        compiler_params=pltpu.CompilerParams(has_side_effects=DATAFLOW_EFFECT),
    )(*[pltpu.with_memory_space_constraint(a, pltpu.HBM) for a in hbm_ins], *in_sems, *after)


def _rcopy(src, dst, sems, k, to):
    return pltpu.make_async_remote_copy(src_ref=src, dst_ref=dst, send_sem=sems[0].at[k], recv_sem=sems[1].at[k],
                                        device_id=to, device_id_type=MESH)


def _gather2_start(shards, lands, name, after):
    n = len(shards)

    def body(ins, sems_in, sems_out):
        x, y, c, me = _position()
        for a in range(n):
            for k, to in enumerate(((x, y, 1 - c), (1 - x, y, c), (x, 1 - y, c))):
                _rcopy(ins[a], ins[n + a].at[me], sems_out, 3 * a + k, to).start()

    res = _split_call(body, name, list(shards) + list(lands), range(2 * n), (), 3 * n, after, True)
    return (res[0], res[1], list(res[2:2 + n]), list(res[2 + n:2 + 2 * n])), res[-1]


def _gather2_forward1(handle, after, name):
    send_sems, recv_sems, srcs, lands = handle
    n = len(srcs)

    def body(ins, sems_in, sems_out):
        x, y, c, me = _position()
        sib, xn, yn = (x, y, 1 - c), (1 - x, y, c), (x, 1 - y, c)
        for a in range(n):
            for k, peer in enumerate((sib, xn, yn)):
                _rcopy(ins[a], ins[n + a].at[me], sems_in, 3 * a + k, peer).wait_send()
                _rcopy(ins[a], ins[n + a].at[_lin(peer)], sems_in, 3 * a + k, peer).wait_recv()
        for a in range(n):
            land = ins[n + a]
            _rcopy(land.at[_lin(xn)], land.at[_lin(xn)], sems_out, 3 * a, sib).start()
            _rcopy(land.at[_lin(yn)], land.at[_lin(yn)], sems_out, 3 * a + 1, sib).start()

            @pl.when(c == 0)
            def _():
                _rcopy(land.at[_lin(xn)], land.at[_lin(xn)], sems_out, 3 * a + 2, yn).start()

            @pl.when(c == 1)
            def _():
                _rcopy(land.at[_lin(yn)], land.at[_lin(yn)], sems_out, 3 * a + 2, xn).start()

    res = _split_call(body, name, list(srcs) + list(lands), range(n, 2 * n), (send_sems, recv_sems), 3 * n, [after], False)
    return (res[0], res[1], list(res[2:]))


def _gather2_forward2(handle, after, name):
    send_sems, recv_sems, lands = handle
    n = len(lands)

    def body(ins, sems_in, sems_out):
        x, y, c, me = _position()
        sib, dg = (x, y, 1 - c), _lin((1 - x, 1 - y, c))
        for a in range(n):
            for k, slot in enumerate((_lin((1 - x, y, 1 - c)), _lin((x, 1 - y, 1 - c)), dg)):
                done = _rcopy(ins[a].at[slot], ins[a].at[slot], sems_in, 3 * a + k, sib)
                done.wait_send()
                done.wait_recv()
        for a in range(n):
            _rcopy(ins[a].at[dg], ins[a].at[dg], sems_out, a, sib).start()

    res = _split_call(body, name, list(lands), range(n), (send_sems, recv_sems), n, [after], False)
    return (res[0], res[1], list(res[2:]))


def _gather2_wait(handle, after, name):
    send_sems, recv_sems, lands = handle
    n = len(lands)

    def body(ins, sems_in, sems_out):
        x, y, c, me = _position()
        slot = _lin((1 - x, 1 - y, 1 - c))
        for a in range(n):
            done = _rcopy(ins[a].at[slot], ins[a].at[slot], sems_in, a, (x, y, 1 - c))
            done.wait_send()
            done.wait_recv()

    return list(_split_call(body, name, list(lands), range(n), (send_sems, recv_sems), 0, [after], False))


def _own_block_filled(block, me):
    land = lax.empty((N_DEV,) + block.shape, block.dtype)
    return lax.dynamic_update_index_in_dim(land, block, me, 0)


ANY_SPEC = pl.BlockSpec(memory_space=pl.ANY)


def _mm(a, b, name, ta=False, tb=False, out_dtype=f32, after=()):
    if ta:
        k_dim, m_dim = a.shape
    else:
        m_dim, k_dim = a.shape
    if tb:
        n_dim, k2 = b.shape
    else:
        k2, n_dim = b.shape
    assert k_dim == k2, (a.shape, b.shape)
    assert a.dtype == bf16 and b.dtype == bf16
    bm = _pick(m_dim, (512, 768, 640, 256, 128))
    bn = _pick(n_dim, (512, 640, 256, 128))
    bk = k_dim if k_dim <= 2560 else _pick(k_dim, (1024, 1280, 768, 512))
    nk = k_dim // bk
    a_spec = (pl.BlockSpec((bk, bm), lambda i, j, k: (k, i)) if ta
              else pl.BlockSpec((bm, bk), lambda i, j, k: (i, k)))
    b_spec = (pl.BlockSpec((bn, bk), lambda i, j, k: (j, k)) if tb
              else pl.BlockSpec((bk, bn), lambda i, j, k: (k, j)))
    dims = (((0 if ta else 1,), (1 if tb else 0,)), ((), ()))

    n_after = len(after)

    def body_single(a_ref, b_ref, *rest):
        o_ref = rest[n_after]
        o_ref[...] = lax.dot_general(a_ref[...], b_ref[...], dims, preferred_element_type=f32).astype(o_ref.dtype)

    def body(a_ref, b_ref, *rest):
        o_ref, acc_ref = rest[n_after:]
        k = pl.program_id(2)

        @pl.when(k == 0)
        def _():
            acc_ref[...] = jnp.zeros_like(acc_ref)

        acc_ref[...] += lax.dot_general(a_ref[...], b_ref[...], dims, preferred_element_type=f32)

        @pl.when(k == nk - 1)
        def _():
            o_ref[...] = acc_ref[...].astype(o_ref.dtype)

    return pl.pallas_call(
        body_single if nk == 1 else body, name=name, out_shape=jax.ShapeDtypeStruct((m_dim, n_dim), out_dtype),
        grid=(m_dim // bm, n_dim // bn, nk), in_specs=[a_spec, b_spec] + [ANY_SPEC] * n_after,
        out_specs=pl.BlockSpec((bm, bn), lambda i, j, k: (i, j)),
        scratch_shapes=[] if nk == 1 else [pltpu.VMEM((bm, bn), f32)],
        compiler_params=_cparams("parallel", "parallel", "arbitrary"),
    )(a, b, *after)


def _rin(arr, width=None, cb=0, roff=0):
    return (arr, arr.shape[1] if width is None else width, cb, roff)


def _rowcall(fn, name, rows, tm, row_ins, par_ins, row_outs, acc_outs=(), after=()):
    nr, npar, nro, n_after = len(row_ins), len(par_ins), len(row_outs), len(after)
    in_specs, args = [], []
    for arr, width, cb, roff in row_ins:
        if roff >= 0:
            imap = lambda i, cb=cb, roff=roff: (i + roff, cb)
        else:
            imap = lambda i, cb=cb, roff=roff: (jnp.maximum(i + roff, 0), cb)
        in_specs.append(pl.BlockSpec((tm, width), imap))
        args.append(arr)
    for p in par_ins:
        in_specs.append(pl.BlockSpec(p.shape, lambda i: (0, 0)))
        args.append(p)
    out_shape, out_specs = [], []
    for width, dt in row_outs:
        out_shape.append(jax.ShapeDtypeStruct((rows, width), dt))
        out_specs.append(pl.BlockSpec((tm, width), lambda i: (i, 0)))
    for p, width in acc_outs:
        out_shape.append(jax.ShapeDtypeStruct((p, width), f32))
        out_specs.append(pl.BlockSpec((p, width), lambda i: (0, 0)))

    def body(*refs):
        i = pl.program_id(0)
        res = fn(i, *[r[...] for r in refs[:nr + npar]])
        outs = refs[nr + npar + n_after:]
        for o, v in zip(outs[:nro], res[:nro]):
            o[...] = v.astype(o.dtype)
        if acc_outs:
            @pl.when(i == 0)
            def _():
                for o in outs[nro:]:
                    o[...] = jnp.zeros_like(o)

            for o, v in zip(outs[nro:], res[nro:]):
                o[...] += v

    return pl.pallas_call(
        body, name=name, out_shape=out_shape, grid=(rows // tm,), in_specs=in_specs + [ANY_SPEC] * n_after,
        out_specs=out_specs, compiler_params=_cparams("arbitrary"),
    )(*args, *after)


def _rms(x, g):
    return x * lax.rsqrt(jnp.mean(x * x, axis=-1, keepdims=True) + EPS) * g


def _normmod(x, g, sc, sh):
    return _rms(x, g) * (1.0 + sc) + sh


def _gelu(x):
    return 0.5 * x * (1.0 + jnp.tanh(0.7978845608028654 * (x + 0.044715 * (x * x * x))))


def _sigmoid(x):
    return 0.5 * (jnp.tanh(0.5 * x) + 1.0)


def _coeff_parts(pre_a, pre_x, ba, bx, lam):
    r = _sigmoid(pre_a + ba)
    ig = _sigmoid(pre_x + bx)
    nl = -lam
    sp = jnp.maximum(nl, 0.0) + jnp.log(1.0 + jnp.exp(-jnp.abs(nl)))
    la = -RG_C * r * sp
    a = jnp.exp(la)
    one_minus_a2 = -jnp.tanh(la) * (a * a + 1.0)
    inv_m = lax.rsqrt(one_minus_a2)
    return r, ig, sp, a, one_minus_a2 * inv_m, inv_m


def _coeff(pre_a, pre_x, u, ba, bx, lam):
    _, ig, _, a, m, _ = _coeff_parts(pre_a, pre_x, ba, bx, lam)
    return a, m * (ig * u)


def _coeff_bwd(pre_a, pre_x, u, ba, bx, lam, da, db):
    r, ig, sp, a, m, inv_m = _coeff_parts(pre_a, pre_x, ba, bx, lam)
    dbu = db * u
    dig = dbu * m
    dm = dbu * ig
    dla = a * (da - dm * a * inv_m)
    dpa = dla * (-RG_C * sp) * (r * (1.0 - r))
    dpx = dig * (ig * (1.0 - ig))
    dsp = jnp.sum(dla * (-RG_C * r), axis=0, keepdims=True)
    dlam = -dsp * _sigmoid(-lam)
    return (dpa, dpx, db * m * ig, jnp.sum(dpa, axis=0, keepdims=True), jnp.sum(dpx, axis=0, keepdims=True), dlam)


SCAN_CHUNK = 256


def _scan_call(a, v, chunk_of, reverse, name, backward, after=()):
    rows, width = a.shape
    n_out = 1 if backward else 2
    nt = SCAN_CHUNK // 8

    def body(a_ref, v_ref, *rest):
        outs, state_ref = rest[len(after):-1], rest[-1]

        @pl.when(pl.program_id(0) == 0)
        def _():
            state_ref[...] = jnp.zeros_like(state_ref)

        rid = lax.broadcasted_iota(jnp.int32, (8, width), 0)
        last_row = 0 if reverse else 7

        def shift(x, s, fill):
            rolled = pltpu.roll(x, (8 - s) if reverse else s, axis=0)
            return jnp.where((rid >= 8 - s) if reverse else (rid < s), fill, rolled)

        def tile(j, st):
            t0 = pl.multiple_of((nt - 1 - j if reverse else j) * 8, 8)
            at = a_ref[pl.ds(t0, 8), :]
            coef = shift(at, 1, 1.0) if backward else at
            acc = v_ref[pl.ds(t0, 8), :]
            for s in (1, 2, 4):
                acc = coef * shift(acc, s, 0.0) + acc
                coef = coef * shift(coef, s, 1.0)
            out = coef * st + acc
            outs[0][pl.ds(t0, 8), :] = out
            last = out[last_row:last_row + 1]
            if backward:
                return at[last_row:last_row + 1] * last
            outs[1][pl.ds(t0, 8), :] = shift(out, 1, st)
            return last

        state_ref[0:1, :] = lax.fori_loop(0, nt, tile, state_ref[0:1, :])

    spec = pl.BlockSpec((SCAN_CHUNK, width), lambda t: (chunk_of(t), 0))
    return pl.pallas_call(
        body, name=name, out_shape=[jax.ShapeDtypeStruct((rows, width), f32)] * n_out,
        grid=(rows // SCAN_CHUNK,), in_specs=[spec, spec] + [ANY_SPEC] * len(after), out_specs=[spec] * n_out,
        scratch_shapes=[pltpu.VMEM((8, width), f32)],
        compiler_params=_cparams("arbitrary"),
    )(a, v, *after)


CONV_CHUNK = 256


def _fill_padded(pad_ref, src_ref, start, n):
    cb = pad_ref.shape[1]
    pad_ref[pl.ds(0, HALO), :] = jnp.zeros((HALO, cb), f32)
    pad_ref[pl.ds(HALO, n), :] = src_ref[pl.ds(start, n), :].astype(f32)
    pad_ref[pl.ds(HALO + n, HALO), :] = jnp.zeros((HALO, cb), f32)


def _dwconv_fwd(x, x_cb0, w, b, taps, pad_left, segments, cb, name, emit_bf16):
    rows = x.shape[0]
    width = w.shape[1]

    def body(x_ref, w_ref, b_ref, *rest):
        outs, xp = rest[:-1], rest[-1]
        for start, n in segments:
            _fill_padded(xp, x_ref, start, n)
            for c0 in range(0, n, CONV_CHUNK):
                acc = jnp.zeros((CONV_CHUNK, cb), f32) + b_ref[...]
                for k in range(taps):
                    acc = acc + w_ref[k:k + 1, :] * xp[pl.ds(HALO + c0 + k - pad_left, CONV_CHUNK), :]
                for o in outs:
                    o[pl.ds(start + c0, CONV_CHUNK), :] = acc.astype(o.dtype)

    out_dtypes = [f32, bf16] if emit_bf16 else [f32]
    return pl.pallas_call(
        body, name=name, out_shape=[jax.ShapeDtypeStruct((rows, width), dt) for dt in out_dtypes],
        grid=(width // cb,),
        in_specs=[pl.BlockSpec((rows, cb), lambda j: (0, j + x_cb0)), pl.BlockSpec((taps, cb), lambda j: (0, j)),
                  pl.BlockSpec((1, cb), lambda j: (0, j))],
        out_specs=[pl.BlockSpec((rows, cb), lambda j: (0, j))] * len(out_dtypes),
        scratch_shapes=[pltpu.VMEM((rows + 2 * HALO, cb), f32)],
        compiler_params=_cparams("parallel"),
    )(x, w, b)


def _dwconv_bwd(douts, x, x_cb0, w, taps, pad_left, segments, cb, name, dx_dtype):
    rows = x.shape[0]
    width = w.shape[1]
    nd = len(douts)

    def body(*refs):
        d_refs, x_ref, w_ref = refs[:nd], refs[nd], refs[nd + 1]
        dx_ref, dw_ref, db_ref, dp, dsum = refs[nd + 2:]
        dw_ref[...] = jnp.zeros_like(dw_ref)
        db_ref[...] = jnp.zeros_like(db_ref)
        if nd > 1:
            total = d_refs[0][...]
            for r in d_refs[1:]:
                total = total + r[...]
            dsum[...] = total
            d_ref = dsum
        else:
            d_ref = d_refs[0]
        for start, n in segments:
            _fill_padded(dp, d_ref, start, n)
            for c0 in range(0, n, CONV_CHUNK):
                db_ref[...] += jnp.sum(dp[pl.ds(HALO + c0, CONV_CHUNK), :], axis=0, keepdims=True)
                xchunk = x_ref[pl.ds(start + c0, CONV_CHUNK), :].astype(f32)
                acc = jnp.zeros((CONV_CHUNK, cb), f32)
                for k in range(taps):
                    shifted = dp[pl.ds(HALO + c0 + pad_left - k, CONV_CHUNK), :]
                    acc = acc + w_ref[k:k + 1, :] * shifted
                    dw_ref[k:k + 1, :] += jnp.sum(shifted * xchunk, axis=0, keepdims=True)
                dx_ref[pl.ds(start + c0, CONV_CHUNK), :] = acc.astype(dx_ref.dtype)

    dspec = pl.BlockSpec((rows, cb), lambda j: (0, j))
    return pl.pallas_call(
        body, name=name,
        out_shape=[jax.ShapeDtypeStruct((rows, width), dx_dtype), jax.ShapeDtypeStruct((taps, width), f32),
                   jax.ShapeDtypeStruct((1, width), f32)],
        grid=(width // cb,),
        in_specs=[dspec] * nd + [pl.BlockSpec((rows, cb), lambda j: (0, j + x_cb0)),
                                 pl.BlockSpec((taps, cb), lambda j: (0, j))],
        out_specs=[dspec, pl.BlockSpec((taps, cb), lambda j: (0, j)), pl.BlockSpec((1, cb), lambda j: (0, j))],
        scratch_shapes=[pltpu.VMEM((rows + 2 * HALO, cb), f32), pltpu.VMEM((rows, cb), f32)],
        compiler_params=_cparams("parallel"),
    )(*douts, x, w)


def _ada_forward(c16, w_ada, b_loc):
    def body(c_ref, w_ref, b_ref, o_ref):
        cv = c_ref[...]
        s = (cv * _sigmoid(cv)).astype(bf16)
        o_ref[0] = jnp.dot(s, w_ref[0].astype(bf16), preferred_element_type=f32) + b_ref[0]

    return pl.pallas_call(
        body, name="ada_forward", out_shape=jax.ShapeDtypeStruct((2, 16, ADA_SHARD), f32), grid=(2,),
        in_specs=[pl.BlockSpec((16, D), lambda l: (0, 0)), pl.BlockSpec((1, D, ADA_SHARD), lambda l: (l, 0, 0)),
                  pl.BlockSpec((1, 1, ADA_SHARD), lambda l: (l, 0, 0))],
        out_specs=pl.BlockSpec((1, 16, ADA_SHARD), lambda l: (l, 0, 0)),
        compiler_params=_cparams("parallel"),
    )(c16, w_ada, b_loc)


def _ada_backward(c16, g16, w_ada):
    def body(c_ref, g_ref, w_ref, dw_ref, ds_ref):
        cv = c_ref[...]
        s = (cv * _sigmoid(cv)).astype(bf16)
        g = g_ref[0].astype(bf16)
        dw_ref[0] = lax.dot_general(s, g, (((0,), (0,)), ((), ())), preferred_element_type=f32)
        ds = lax.dot_general(g, w_ref[0].astype(bf16), (((1,), (1,)), ((), ())), preferred_element_type=f32)
        cc = cv[8:9]
        sg = _sigmoid(cc)
        dsilu = sg * (1.0 + cc * (1.0 - sg))
        ds_ref[0] = jnp.zeros((8, D), f32) + jnp.sum(ds[8:16], axis=0, keepdims=True) * dsilu

    return pl.pallas_call(
        body, name="ada_backward",
        out_shape=[jax.ShapeDtypeStruct((2, D, ADA_SHARD), f32), jax.ShapeDtypeStruct((2, 8, D), f32)], grid=(2,),
        in_specs=[pl.BlockSpec((16, D), lambda l: (0, 0)), pl.BlockSpec((1, 16, ADA_SHARD), lambda l: (l, 0, 0)),
                  pl.BlockSpec((1, D, ADA_SHARD), lambda l: (l, 0, 0))],
        out_specs=[pl.BlockSpec((1, D, ADA_SHARD), lambda l: (l, 0, 0)), pl.BlockSpec((1, 8, D), lambda l: (l, 0, 0))],
        compiler_params=_cparams("parallel"),
    )(c16, g16, w_ada)


def _adamw(pieces, w, m, v, name, after=()):
    rows, cols = w.shape
    n_arr, n_after = len(pieces), len(after)
    counts = [cnt for _, cnt in pieces]
    pieces = [p for p, _ in pieces]
    tm = 256 if (rows % 256 == 0 and rows > 256) else rows

    def body(*refs):
        p_refs = refs[:n_arr]
        w_ref, m_ref, v_ref = refs[n_arr:n_arr + 3]
        g_ref, d_ref, nm_ref, nv_ref = refs[n_arr + 3 + n_after:]
        g = None
        for p_ref in p_refs:
            for j in range(p_ref.shape[0]):
                term = p_ref[j].astype(f32)
                g = term if g is None else g + term
        m2 = ADAM_B1 * m_ref[...] + (1.0 - ADAM_B1) * g
        v2 = ADAM_B2 * v_ref[...] + (1.0 - ADAM_B2) * (g * g)
        m_hat = m2 / (1.0 - ADAM_B1 ** ADAM_STEP)
        v_hat = v2 / (1.0 - ADAM_B2 ** ADAM_STEP)
        g_ref[...] = g
        d_ref[...] = -ADAM_LR * (m_hat / (jnp.sqrt(v_hat) + ADAM_EPS) + ADAM_WD * w_ref[...])
        nm_ref[...] = m2
        nv_ref[...] = v2

    spec = pl.BlockSpec((tm, cols), lambda i: (i, 0))
    return pl.pallas_call(
        body, name=name, out_shape=[jax.ShapeDtypeStruct((rows, cols), f32)] * 4, grid=(rows // tm,),
        in_specs=[pl.BlockSpec((cnt, tm, cols), lambda i: (0, i, 0)) for cnt in counts] + [spec, spec, spec]
        + [ANY_SPEC] * n_after,
        out_specs=[spec] * 4, compiler_params=_cparams("parallel"),
    )(*pieces, w, m, v, *after)


MLP_TM = 256
FB = F // N_DEV


def _stack_rows(vals, n):
    cols = vals[0].shape[1]
    rid = lax.broadcasted_iota(jnp.int32, (n, cols), 0)
    out = jnp.zeros((n, cols), f32)
    for k, v in enumerate(vals):
        out = jnp.where(rid == k, v, out)
    return out


N_MLP_PARAMS = 9


class _ParamRows:
    def __init__(self, ref):
        self.ref = ref

    def __getitem__(self, sl):
        return self.ref[8 * sl.start:8 * sl.start + 1, :]


def _resident(shape, imap):
    return pl.BlockSpec(shape, imap, pipeline_mode=pl.Buffered(1))


def _mlp_forward(xa, xa_roff, out_prev, par, w_in, w_out, layer, name):
    def body(xa_ref, op_ref, par_ref, win_ref, wout_ref, x1_ref, h_ref, r_ref, mo_ref, x2_ref, hn_ref):
        p = _ParamRows(par_ref)
        x1 = xa_ref[...] + p[0:1] * (op_ref[...] + p[1:2])
        h = _normmod(x1, p[2:3], p[3:4], p[4:5]).astype(bf16)
        x1_ref[...] = x1
        h_ref[...] = h
        mo = jnp.zeros((MLP_TM, D), f32)
        for j in range(N_DEV):
            r = jnp.maximum(jnp.dot(h, win_ref[j], preferred_element_type=f32), 0.0)
            r_ref[:, j * FB:(j + 1) * FB] = r.astype(bf16)
            mo = mo + jnp.dot((r * r).astype(bf16), wout_ref[j], preferred_element_type=f32)
        mo_ref[...] = mo.astype(bf16)
        x2 = x1 + p[5:6] * mo
        x2_ref[...] = x2
        hn_ref[...] = _normmod(x2, p[6:7], p[7:8], p[8:9]).astype(bf16)

    row = lambda width: pl.BlockSpec((MLP_TM, width), lambda i: (i, 0))
    return pl.pallas_call(
        body, name=name, grid=(T_LAT // MLP_TM,),
        out_shape=[jax.ShapeDtypeStruct((T_LAT, D), f32), jax.ShapeDtypeStruct((T_LAT, D), bf16),
                   jax.ShapeDtypeStruct((T_LAT, F), bf16), jax.ShapeDtypeStruct((T_LAT, D), bf16),
                   jax.ShapeDtypeStruct((T_LAT, D), f32), jax.ShapeDtypeStruct((T_LAT, D), bf16)],
        in_specs=[pl.BlockSpec((MLP_TM, D), lambda i: (i + xa_roff, 0)), row(D), pl.BlockSpec((8 * N_MLP_PARAMS, D), lambda i: (0, 0)),
                  _resident((N_DEV, None, D, FB), lambda i: (0, layer, 0, 0)),
                  _resident((N_DEV, None, FB, D), lambda i: (0, layer, 0, 0))],
        out_specs=[row(D), row(D), row(F), row(D), row(D), row(D)],
        compiler_params=_cparams("parallel"),
    )(xa, out_prev, par, w_in, w_out)


def _mlp_backward(dx2, x1, r, mo, out_prev, par, w_in, w_out, layer, name, after=()):
    nt = (((1,), (1,)), ((), ()))

    n_after = len(after)

    def body(dx2_ref, x1_ref, r_ref, mo_ref, op_ref, par_ref, win_ref, wout_ref, *rest):
        dx1_ref, dop_ref, dmo_ref, dhid_ref, acc_ref = rest[n_after:]
        p = _ParamRows(par_ref)
        dx2v = dx2_ref[...]
        dmo = (p[5:6] * dx2v).astype(bf16)
        dmo_ref[...] = dmo
        dh = jnp.zeros((MLP_TM, D), f32)
        mo = mo_ref[...].astype(f32)
        for j in range(N_DEV):
            rf = r_ref[:, j * FB:(j + 1) * FB].astype(f32)
            dact = lax.dot_general(dmo, wout_ref[j], nt, preferred_element_type=f32)
            dhid = (dact * (2.0 * rf)).astype(bf16)
            dhid_ref[:, j * FB:(j + 1) * FB] = dhid
            dh = dh + lax.dot_general(dhid, win_ref[j], nt, preferred_element_type=f32)
        x1 = x1_ref[...]
        _, vjp = jax.vjp(_normmod, x1, p[2:3], p[3:4], p[4:5])
        dx, dng, dsc, dsh = vjp(dh)
        dx1 = dx2v + dx
        dx1_ref[...] = dx1
        dop_ref[...] = (p[0:1] * dx1).astype(bf16)
        sums = _stack_rows([jnp.sum(dx1 * (op_ref[...] + p[1:2]), axis=0, keepdims=True),
                            p[0:1] * jnp.sum(dx1, axis=0, keepdims=True), dng, dsc, dsh,
                            jnp.sum(dx2v * mo, axis=0, keepdims=True)], 8)

        @pl.when(pl.program_id(0) == 0)
        def _():
            acc_ref[...] = jnp.zeros_like(acc_ref)

        acc_ref[...] += sums

    row = lambda width: pl.BlockSpec((MLP_TM, width), lambda i: (i, 0))
    return pl.pallas_call(
        body, name=name, grid=(T_LAT // MLP_TM,),
        out_shape=[jax.ShapeDtypeStruct((T_LAT, D), f32), jax.ShapeDtypeStruct((T_LAT, D), bf16),
                   jax.ShapeDtypeStruct((T_LAT, D), bf16), jax.ShapeDtypeStruct((T_LAT, F), bf16),
                   jax.ShapeDtypeStruct((8, D), f32)],
        in_specs=[row(D), row(D), row(F), row(D), row(D), pl.BlockSpec((8 * N_MLP_PARAMS, D), lambda i: (0, 0)),
                  _resident((N_DEV, None, D, FB), lambda i: (0, layer, 0, 0)),
                  _resident((N_DEV, None, FB, D), lambda i: (0, layer, 0, 0))] + [ANY_SPEC] * n_after,
        out_specs=[row(D), row(D), row(D), row(F), pl.BlockSpec((8, D), lambda i: (0, 0))],
        compiler_params=_cparams("arbitrary"),
    )(dx2, x1, r, mo, out_prev, par, w_in, w_out, *after)


def _mlp_weight_grads(h, dhid, r, dmo, layer, other, tag):
    tn = (((0,), (0,)), ((), ()))

    def body_in(h_ref, dhid_ref, *rest):
        rest[-1][...] = lax.dot_general(h_ref[...], dhid_ref[...], tn, preferred_element_type=f32).astype(bf16)

    def body_out(r_ref, dmo_ref, *rest):
        rf = r_ref[...].astype(f32)
        rest[-1][...] = lax.dot_general((rf * rf).astype(bf16), dmo_ref[...], tn,
                                        preferred_element_type=f32).astype(bf16)

    def call(body, name, operands, specs, block, prev):
        extra = [] if prev is None else [prev]
        return pl.pallas_call(
            body, name=name, grid=(N_DEV,), out_shape=jax.ShapeDtypeStruct((N_DEV, 2) + block, bf16),
            in_specs=specs + [pl.BlockSpec(memory_space=pl.ANY)] * len(extra),
            out_specs=pl.BlockSpec((None, None) + block, lambda j: (j, layer, 0, 0)),
            input_output_aliases={} if prev is None else {2: 0},
            compiler_params=_cparams("parallel"),
        )(*operands, *extra)

    dw_in = call(body_in, tag + "_mlp_in_dw", [h, dhid],
                 [_resident((T_LAT, D), lambda j: (0, 0)), pl.BlockSpec((T_LAT, FB), lambda j: (0, j))], (D, FB),
                 None if other is None else other[0])
    dw_out = call(body_out, tag + "_mlp_out_dw", [r, dmo],
                  [pl.BlockSpec((T_LAT, FB), lambda j: (0, j)), _resident((T_LAT, D), lambda j: (0, 0))], (FB, D),
                  None if other is None else other[1])
    return dw_in, dw_out


def _pos_embed():
    n_rows = T_LAT // GRID_W
    q = D // 4
    omega = 1.0 / (POS_BASE ** (jnp.arange(q, dtype=f32) / q))
    er = jnp.arange(n_rows, dtype=jnp.int32).astype(f32)[:, None] * omega[None, :]
    ec = jnp.arange(GRID_W, dtype=jnp.int32).astype(f32)[:, None] * omega[None, :]
    by_row = jnp.concatenate([jnp.sin(er), jnp.cos(er)], axis=-1)[:, None, :]
    by_col = jnp.concatenate([jnp.sin(ec), jnp.cos(ec)], axis=-1)[None, :, :]
    full = jnp.concatenate([jnp.broadcast_to(by_row, (n_rows, GRID_W, D // 2)),
                            jnp.broadcast_to(by_col, (n_rows, GRID_W, D // 2))], axis=-1)
    return full.reshape(T_LAT, D)


HALF = R // 2
BLK_PER_HALF = N_BLK // 2
N_PARTS = 4


def _gate_matrix(w_a, w_x):
    eye = jnp.eye(BLK_PER_HALF, dtype=bf16)
    cols = []
    for h in range(2):
        for d in range(2):
            for w in (w_a, w_x):
                blocks = w[d, BLK_PER_HALF * h:BLK_PER_HALF * (h + 1)].astype(bf16)
                cols.append(jnp.einsum("hij,hg->higj", blocks, eye).reshape(HALF, HALF))
    return jnp.concatenate(cols, axis=1)


def _gate_blocks(dwg, part):
    out = []
    for h in range(2):
        blk = dwg[:, (N_PARTS * h + part) * HALF:(N_PARTS * h + part + 1) * HALF]
        blk = blk.reshape(BLK_PER_HALF, BLK, BLK_PER_HALF, BLK)
        out.append(jnp.moveaxis(jnp.diagonal(blk, axis1=0, axis2=2), -1, 0))
    return jnp.concatenate(out, axis=0)


GATE_BM = 768


def _gates_dx(dpre, wg, after=()):
    rows = dpre.shape[0]
    n_after = len(after)

    def body(d_ref, w_ref, *rest):
        rest[n_after][...] = lax.dot_general(d_ref[...], w_ref[...], (((1,), (1,)), ((), ())),
                                             preferred_element_type=f32)

    return pl.pallas_call(
        body, name="l0_gates_dx", grid=(rows // GATE_BM, 2), out_shape=jax.ShapeDtypeStruct((rows, R), f32),
        in_specs=[pl.BlockSpec((GATE_BM, N_PARTS * HALF), lambda i, h: (i, h)),
                  pl.BlockSpec((HALF, N_PARTS * HALF), lambda i, h: (0, h))] + [ANY_SPEC] * n_after,
        out_specs=pl.BlockSpec((GATE_BM, HALF), lambda i, h: (i, h)),
        compiler_params=_cparams("parallel", "parallel"),
    )(dpre, wg, *after)


COEFF_TM = 256


def _dir_params(d, *params):
    specs = [pl.BlockSpec((None, 1, HALF), lambda h, i: (d, 0, h))] * len(params)
    return specs, [p.reshape(2, 1, R) for p in params]


def _gates_coeff_fwd(ub, u, wg, ba, bx, lam, d):
    rows = u.shape[0]

    def body(ub_ref, u_ref, w_ref, ba_ref, bx_ref, lam_ref, a_ref, b_ref):
        pre = jnp.dot(ub_ref[...], w_ref[...], preferred_element_type=f32)
        a, b = _coeff(pre[:, :HALF], pre[:, HALF:], u_ref[...], ba_ref[...], bx_ref[...], lam_ref[...])
        a_ref[...] = a
        b_ref[...] = b

    tile = pl.BlockSpec((COEFF_TM, HALF), lambda h, i: (i, h))
    pspecs, pargs = _dir_params(d, ba, bx, lam)
    return pl.pallas_call(
        body, name=f"l0_gates_coeff_{d}", grid=(2, rows // COEFF_TM),
        out_shape=[jax.ShapeDtypeStruct((rows, R), f32)] * 2,
        in_specs=[tile, tile, pl.BlockSpec((HALF, 2 * HALF), lambda h, i: (0, 2 * h + d))] + pspecs,
        out_specs=[tile, tile], compiler_params=_cparams("parallel", "parallel"),
    )(ub, u, wg, *pargs)


def _gates_coeff_bwd(ub, u, dh, yp, wg, ba, bx, lam, d, dpre_prev):
    rows = u.shape[0]
    n_prev = 0 if dpre_prev is None else 1

    def body(ub_ref, u_ref, dh_ref, yp_ref, w_ref, ba_ref, bx_ref, lam_ref, *rest):
        dpre_ref, du_ref, dba_ref, dbx_ref, dlam_ref = rest[n_prev:]
        pre = jnp.dot(ub_ref[...], w_ref[...], preferred_element_type=f32)
        dhv = dh_ref[...]
        dpa, dpx, du, dba, dbx, dlam = _coeff_bwd(pre[:, :HALF], pre[:, HALF:], u_ref[...], ba_ref[...], bx_ref[...],
                                                  lam_ref[...], dhv * yp_ref[...], dhv)
        dpre_ref[:, :HALF] = dpa.astype(bf16)
        dpre_ref[:, HALF:] = dpx.astype(bf16)
        du_ref[...] = du

        @pl.when(pl.program_id(1) == 0)
        def _():
            dba_ref[...] = jnp.zeros_like(dba_ref)
            dbx_ref[...] = jnp.zeros_like(dbx_ref)
            dlam_ref[...] = jnp.zeros_like(dlam_ref)

        dba_ref[...] += dba
        dbx_ref[...] += dbx
        dlam_ref[...] += dlam

    tile = pl.BlockSpec((COEFF_TM, HALF), lambda h, i: (i, h))
    acc = pl.BlockSpec((1, HALF), lambda h, i: (0, h))
    pspecs, pargs = _dir_params(d, ba, bx, lam)
    extra = [] if dpre_prev is None else [dpre_prev]
    return pl.pallas_call(
        body, name=f"l0_gates_coeff_bwd_{d}", grid=(2, rows // COEFF_TM),
        out_shape=[jax.ShapeDtypeStruct((rows, 2 * N_PARTS * HALF), bf16), jax.ShapeDtypeStruct((rows, R), f32)]
        + [jax.ShapeDtypeStruct((1, R), f32)] * 3,
        in_specs=[tile] * 4 + [pl.BlockSpec((HALF, 2 * HALF), lambda h, i: (0, 2 * h + d))] + pspecs
        + [ANY_SPEC] * n_prev,
        out_specs=[pl.BlockSpec((COEFF_TM, 2 * HALF), lambda h, i: (i, 2 * h + d)), tile, acc, acc, acc],
        input_output_aliases={8: 0} if n_prev else {}, compiler_params=_cparams("parallel", "arbitrary"),
    )(ub, u, dh, yp, wg, *pargs, *extra)


def _gates_dw(u, dpre):
    rows = u.shape[0]

    def body(u_ref, d_ref, o_ref):
        o_ref[...] = lax.dot_general(u_ref[...], d_ref[...], (((0,), (0,)), ((), ())), preferred_element_type=f32)

    return pl.pallas_call(
        body, name="l0_gates_dw", grid=(2 * N_PARTS,), out_shape=jax.ShapeDtypeStruct((HALF, 2 * N_PARTS * HALF), f32),
        in_specs=[pl.BlockSpec((rows, HALF), lambda j: (0, j // N_PARTS)), pl.BlockSpec((rows, HALF), lambda j: (0, j))],
        out_specs=pl.BlockSpec((HALF, HALF), lambda j: (0, j)), compiler_params=_cparams("parallel"),
    )(u, dpre)


N_SCAN_CHUNKS = T_ALL // SCAN_CHUNK
SCAN_FWD = lambda t: t
SCAN_FWD_BWD = lambda t: N_SCAN_CHUNKS - 1 - t
SCAN_REV = lambda t: jnp.where(t == 0, 0, N_SCAN_CHUNKS - t)
SCAN_REV_BWD = lambda t: jnp.where(t == N_SCAN_CHUNKS - 1, 0, t + 1)
CONV_SEGMENTS = ((0, T_CTX), (T_CTX, T_LAT))
TM = 128
FUSED_TM = 256


def _token_rows(x, ctx):
    return (jnp.concatenate([ctx, x], axis=0),
            jnp.concatenate([jnp.zeros((T_CTX, D), f32), _pos_embed()], axis=0))


def _local_step(xcat, poscat, target, mods, cmod, wts, late_weights, send_grads, reduce_loss, start_after=()):
    sh1, sc1, g1, sh2, sc2, g2 = [[mods[l, i][None] for l in range(2)] for i in range(N_MOD)]
    ng = wts["norm_g"]
    scp =jnp.concatenate([cmod[1][None], sc1[0]], axis=0)
    shp = jnp.concatenate([cmod[0][None], sh1[0]], axis=0)

    ctx_tiles = T_CTX // FUSED_TM
    nt = (((1,), (1,)), ((), ()))

    def blend(i, p):
        sel = jnp.where(i < ctx_tiles, 1.0, 0.0)
        return sel * p[0:1] + (1.0 - sel) * p[1:2]

    def f_pre0(i, xc, pos, g, scp_, shp_, w):
        x0 = xc + pos
        h = _normmod(x0, g, blend(i, scp_), blend(i, shp_)).astype(bf16)
        return x0, h, jnp.dot(h, w, preferred_element_type=f32)

    x0cat, h0, gr = _rowcall(f_pre0, "l0_prenorm_in_proj", T_ALL, FUSED_TM, [_rin(xcat), _rin(poscat)],
                             [ng[0, 0][None], scp, shp, wts["rec_w_in"]], [(D, f32), (D, bf16), (2 * R, f32)],
                             after=start_after)
    u, ub = _dwconv_fwd(gr, R // 256, wts["rec_conv_w"], wts["rec_conv_b"], 4, 1, CONV_SEGMENTS, 256,
                        "l0_conv", True)
    gate_args = (wts["gates"], wts["rec_b_a"], wts["rec_b_x"], wts["rec_lambda"])
    a0, b0 = _gates_coeff_fwd(ub, u, *gate_args, 0)
    a1, b1 = _gates_coeff_fwd(ub, u, *gate_args, 1)
    halfway = late_weights("mlp_halfway", a1)
    y0, yp0 = _scan_call(a0, b0, SCAN_FWD, False, "l0_scan_fwd", False, after=[halfway])
    y1, yp1 = _scan_call(a1, b1, SCAN_REV, True, "l0_scan_rev", False)

    wts = dict(wts, **late_weights("mlp", y1))

    def f_gate_out(i, gp, y0_, y1_, w):
        z = (_gelu(gp) * (y0_ + y1_)).astype(bf16)
        return z, jnp.dot(z, w, preferred_element_type=f32)

    zb, out0 = _rowcall(f_gate_out, "l0_gate_out_proj", T_LAT, FUSED_TM,
                        [_rin(gr, R, 0, ctx_tiles), _rin(y0, None, 0, ctx_tiles), _rin(y1, None, 0, ctx_tiles)],
                        [wts["rec_w_out"]], [(R, bf16), (D, f32)])

    zero_d = jnp.zeros((1, D), f32)

    def mlp_params(rows):
        rows = rows + [zero_d] * (N_MLP_PARAMS - len(rows))
        return jnp.concatenate([jnp.broadcast_to(r, (8, D)) for r in rows], axis=0)

    par0 = mlp_params([g1[0], zero_d, ng[0, 1][None], sc2[0], sh2[0], g2[0], ng[1, 0][None], sc1[1], sh1[1]])
    x1, h1, r0, mo0, x2, h2 = _mlp_forward(x0cat, T_CTX // MLP_TM, out0, par0, wts["mlp_w_in"], wts["mlp_w_out"], 0,
                                           "l0_mlp")

    wts = dict(wts, **late_weights("conf", x2))
    def glu(pa, pb, b1):
        return (pa + b1[:, :D]) * _sigmoid(pb + b1[:, D:])

    def f_pw1_glu(i, h_, b1, w):
        p = jnp.dot(h_, w, preferred_element_type=f32)
        return glu(p[:, :D], p[:, D:], b1), p

    zg, pw = _rowcall(f_pw1_glu, "l1_pw1_glu", T_LAT, FUSED_TM, [_rin(h2)], [wts["conf_b_pw1"], wts["conf_w_pw1"]],
                      [(D, f32), (2 * D, bf16)])
    (zc,) = _dwconv_fwd(zg, 0, wts["conf_conv_w"], wts["conf_conv_b"], 31, 15, ((0, T_LAT),), 128, "l1_conv", False)

    def ln_silu(z, lg, lb):
        mu = jnp.mean(z, axis=-1, keepdims=True)
        zc_ = z - mu
        var = jnp.mean(zc_ * zc_, axis=-1, keepdims=True)
        yv = zc_ * lax.rsqrt(var + EPS) * lg + lb
        return yv * _sigmoid(yv)

    def f_lnsilu_pw2(i, z, lg, lb, w):
        s = ln_silu(z, lg, lb).astype(bf16)
        return s, jnp.dot(s, w, preferred_element_type=f32)

    sb, out1 = _rowcall(f_lnsilu_pw2, "l1_ln_silu_pw2", T_LAT, FUSED_TM, [_rin(zc)],
                        [wts["conf_ln_g"], wts["conf_ln_b"], wts["conf_w_pw2"]], [(D, bf16), (D, f32)])
    par1 = mlp_params([g1[1], wts["conf_b_pw2"], ng[1, 1][None], sc2[1], sh2[1], g2[1]])
    x3, h3, r1, mo1, x4, _ = _mlp_forward(x2, 0, out1, par1, wts["mlp_w_in"], wts["mlp_w_out"], 1, "l1_mlp")

    def loss_fn(x4_, fg, tgt):
        err = _rms(x4_, fg) - tgt
        per_row = jnp.mean(err * err, axis=-1, keepdims=True)
        return 0.5 * jnp.sum(per_row, axis=0, keepdims=True)

    def f_head(i, x4_, tgt, fg):
        loss, vjp = jax.vjp(lambda a, e: loss_fn(a, e, tgt), x4_, fg)
        dx, dfg = vjp(jnp.ones((1, 1), f32))
        return dx, jnp.broadcast_to(loss, (1, 128)), dfg

    dx4, loss_acc, dfinal_g = _rowcall(f_head, "head", T_LAT, TM, [_rin(x4), _rin(target)], [wts["final_g"]],
                                       [(D, f32)], [(1, 128), (1, D)])

    grads = {"final_g": dfinal_g}
    loss = reduce_loss(loss_acc[0, 0])

    dx3, dout1, dmo1, dhid1, acc1 = _mlp_backward(dx4, x3, r1, mo1, out1, par1, wts["mlp_w_in"], wts["mlp_w_out"], 1,
                                                  "l1_mlp_bwd", after=[loss.reshape(1, 1)])
    mlp_dw = _mlp_weight_grads(h3, dhid1, r1, dmo1, 1, None, "l1")
    dg1_1, db_pw2, dng11, dsc2_1, dsh2_1, dg2_1 = [acc1[k:k + 1] for k in range(6)]

    grads["conf_w_pw2"] = _mm(sb, dout1, "l1_pw2_dw", ta=True, out_dtype=bf16)
    grads["conf_b_pw2"] = db_pw2

    def f_pw2_lnsilu_bwd(i, z, dout, lg, lb, w):
        ds = lax.dot_general(dout, w, nt, preferred_element_type=f32)
        _, vjp = jax.vjp(ln_silu, z, lg, lb)
        return vjp(ds)

    dzc, dln_g, dln_b = _rowcall(f_pw2_lnsilu_bwd, "l1_pw2_ln_silu_bwd", T_LAT, FUSED_TM, [_rin(zc), _rin(dout1)],
                                 [wts["conf_ln_g"], wts["conf_ln_b"], wts["conf_w_pw2"]], [(D, f32)], [(1, D)] * 2)
    grads["conf_ln_g"], grads["conf_ln_b"] = dln_g, dln_b
    dzg, dconv_w, dconv_b = _dwconv_bwd([dzc], zg, 0, wts["conf_conv_w"], 31, 15, ((0, T_LAT),), 128,
                                        "l1_conv_bwd", f32)
    grads["conf_conv_w"], grads["conf_conv_b"] = dconv_w, dconv_b

    def f_glu_pw1_norm_bwd(i, p_, dz, x_, dxs, b1, g_, sc_, sh_, w):
        pf = p_.astype(f32)
        _, vjp = jax.vjp(glu, pf[:, :D], pf[:, D:], b1)
        da, db, db1 = vjp(dz)
        dp = jnp.concatenate([da, db], axis=1).astype(bf16)
        dh = lax.dot_general(dp, w, nt, preferred_element_type=f32)
        _, vjp = jax.vjp(_normmod, x_, g_, sc_, sh_)
        dx, dg, dsc, dsh = vjp(dh)
        return dp, dx + dxs, db1, dg, dsc, dsh

    dpw, dx2, db_pw1, dng10, dsc1_1, dsh1_1 = _rowcall(
        f_glu_pw1_norm_bwd, "l1_glu_pw1_normmod_bwd", T_LAT, FUSED_TM, [_rin(pw), _rin(dzg), _rin(x2), _rin(dx3)],
        [wts["conf_b_pw1"], ng[1, 0][None], sc1[1], sh1[1], wts["conf_w_pw1"]], [(2 * D, bf16), (D, f32)],
        [(1, 2 * D), (1, D), (1, D), (1, D)])
    grads["conf_b_pw1"] = db_pw1
    grads["conf_w_pw1"] = _mm(h2, dpw, "l1_pw1_dw", ta=True, out_dtype=bf16)
    sent = send_grads(["conf_w_pw2", "conf_w_pw1"], grads)

    dx1, dout0, dmo0, dhid0, acc0 = _mlp_backward(dx2, x1, r0, mo0, out0, par0, wts["mlp_w_in"], wts["mlp_w_out"], 0,
                                                  "l0_mlp_bwd", after=[sent])
    grads["mlp_w_in"], grads["mlp_w_out"] = _mlp_weight_grads(h1, dhid0, r0, dmo0, 0, mlp_dw, "l0")
    sent = send_grads(["mlp_w_in", "mlp_w_out"], grads)
    dg1_0, _, dng01, dsc2_0, dsh2_0, dg2_0 = [acc0[k:k + 1] for k in range(6)]

    grads["rec_w_out"] = _mm(zb, dout0, "l0_out_proj_dw", ta=True, out_dtype=bf16, after=[sent])
    sent = send_grads(["rec_w_out"], grads)

    def f_out_gate_bwd(i, gp, y0_, y1_, dout, w):
        lat = jnp.where(i < ctx_tiles, 0.0, 1.0)
        dz = lax.dot_general(dout, w, nt, preferred_element_type=f32)
        _, vjp = jax.vjp(lambda a, b: _gelu(a) * b, gp, y0_ + y1_)
        dgp, dy = vjp(dz)
        return dgp * lat, dy * lat

    dgp, dy = _rowcall(f_out_gate_bwd, "l0_out_proj_gate_bwd", T_ALL, FUSED_TM,
                       [_rin(gr, R, 0), _rin(y0), _rin(y1), _rin(dout0, None, 0, -ctx_tiles)], [wts["rec_w_out"]],
                       [(R, bf16), (R, f32)], after=[sent])
    (dh_f,) = _scan_call(a0, dy, SCAN_FWD_BWD, True, "l0_scan_fwd_bwd", True)
    (dh_r,) = _scan_call(a1, dy, SCAN_REV_BWD, False, "l0_scan_rev_bwd", True)

    dpre, du_f, *dpar_f = _gates_coeff_bwd(ub, u, dh_f, yp0, *gate_args, 0, None)
    dpre, du_r, *dpar_r = _gates_coeff_bwd(ub, u, dh_r, yp1, *gate_args, 1, dpre)
    grads["rec_b_a"], grads["rec_b_x"], grads["rec_lambda"] = [
        jnp.concatenate([f.reshape(-1), r_.reshape(-1)]).reshape(2, R) for f, r_ in zip(dpar_f, dpar_r)]
    grads["gates"] = _gates_dw(ub, dpre)
    sent = send_grads(["replicated"], grads)
    du_gates = _gates_dx(dpre, wts["gates"], after=[sent])
    drec, dconv4_w, dconv4_b = _dwconv_bwd([du_f, du_r, du_gates], gr, R // 256, wts["rec_conv_w"], 4, 1,
                                           CONV_SEGMENTS, 256, "l0_conv_bwd", bf16)
    grads["rec_conv_w"], grads["rec_conv_b"] = dconv4_w, dconv4_b
    dgr = jnp.concatenate([dgp, drec], axis=1)
    grads["rec_w_in"] = _mm(h0, dgr, "l0_in_proj_dw", ta=True, out_dtype=bf16)
    sent = send_grads(["rec_w_in"], grads)

    def f_pre0_bwd(i, x0, dgr_, dxs, g, scp_, shp_, w):
        lat = jnp.where(i < ctx_tiles, 0.0, 1.0)
        dh = lax.dot_general(dgr_, w, nt, preferred_element_type=f32)
        _, vjp = jax.vjp(lambda a, b, c, e: _normmod(a, b, blend(i, c), blend(i, e)), x0, g, scp_, shp_)
        dx, dg, dscp, dshp = vjp(dh)
        return dx + lat * dxs, dg, dscp, dshp

    dx0cat, dng00, dscp, dshp = _rowcall(
        f_pre0_bwd, "l0_in_proj_prenorm_bwd", T_ALL, FUSED_TM,
        [_rin(x0cat), _rin(dgr), _rin(dx1, None, 0, -ctx_tiles)], [ng[0, 0][None], scp, shp, wts["rec_w_in"]],
        [(D, f32)], [(1, D), (2, D), (2, D)], after=[sent])

    grads["norm_g"] = jnp.stack([jnp.concatenate([dng00, dng01], 0), jnp.concatenate([dng10, dng11], 0)])
    dmods = jnp.stack([
        jnp.concatenate([dshp[1:2], dscp[1:2], dg1_0, dsh2_0, dsc2_0, dg2_0], axis=0),
        jnp.concatenate([dsh1_1, dsc1_1, dg1_1, dsh2_1, dsc2_1, dg2_1], axis=0)])
    dcmod = jnp.concatenate([dshp[0:1], dscp[0:1]], axis=0)
    return loss, dx0cat[T_CTX:], dmods, dcmod, grads


def _unshard_cols(g):
    g = jnp.moveaxis(g, 0, -2)
    return g.reshape(g.shape[:-2] + (g.shape[-2] * g.shape[-1],))


def _shard_cols(w):
    w = w.reshape(w.shape[:-1] + (N_DEV, w.shape[-1] // N_DEV))
    return jnp.moveaxis(w, -2, 0)


def _shard_rows(w):
    return w.reshape((N_DEV, w.shape[0] // N_DEV) + w.shape[1:])


SMALL_PACK_ROWS = 64


def kernel(x, c, ctx, c_ctx, w_ada, b_ada, norm_g, rec_w_in, rec_conv_w, rec_conv_b, rec_lambda, rec_w_a, rec_b_a, rec_w_x, rec_b_x, rec_w_out, conf_w_pw1, conf_b_pw1, conf_conv_w, conf_conv_b, conf_ln_g, conf_ln_b, conf_w_pw2, conf_b_pw2, mlp_w_in, mlp_w_out, final_g, loss_target, m_c_ctx, m_w_ada, m_b_ada, m_norm_g, m_rec_w_in, m_rec_conv_w, m_rec_conv_b, m_rec_lambda, m_rec_w_a, m_rec_b_a, m_rec_w_x, m_rec_b_x, m_rec_w_out, m_conf_w_pw1, m_conf_b_pw1, m_conf_conv_w, m_conf_conv_b, m_conf_ln_g, m_conf_ln_b, m_conf_w_pw2, m_conf_b_pw2, m_mlp_w_in, m_mlp_w_out, m_final_g, v_c_ctx, v_w_ada, v_b_ada, v_norm_g, v_rec_w_in, v_rec_conv_w, v_rec_conv_b, v_rec_lambda, v_rec_w_a, v_rec_b_a, v_rec_w_x, v_rec_b_x, v_rec_w_out, v_conf_w_pw1, v_conf_b_pw1, v_conf_conv_w, v_conf_conv_b, v_conf_ln_g, v_conf_ln_b, v_conf_w_pw2, v_conf_b_pw2, v_mlp_w_in, v_mlp_w_out, v_final_g):
    me = 4 * lax.axis_index("x") + 2 * lax.axis_index("y") + lax.axis_index("c")
    weights = dict(c_ctx=c_ctx, w_ada=w_ada, b_ada=b_ada, norm_g=norm_g, rec_w_in=rec_w_in, rec_conv_w=rec_conv_w,
                   rec_conv_b=rec_conv_b, rec_lambda=rec_lambda, rec_w_a=rec_w_a, rec_b_a=rec_b_a, rec_w_x=rec_w_x,
                   rec_b_x=rec_b_x, rec_w_out=rec_w_out, conf_w_pw1=conf_w_pw1, conf_b_pw1=conf_b_pw1,
                   conf_conv_w=conf_conv_w, conf_conv_b=conf_conv_b, conf_ln_g=conf_ln_g, conf_ln_b=conf_ln_b,
                   conf_w_pw2=conf_w_pw2, conf_b_pw2=conf_b_pw2, mlp_w_in=mlp_w_in, mlp_w_out=mlp_w_out, final_g=final_g)
    m_in = dict(c_ctx=m_c_ctx, w_ada=m_w_ada, b_ada=m_b_ada, norm_g=m_norm_g, rec_w_in=m_rec_w_in, rec_conv_w=m_rec_conv_w,
                rec_conv_b=m_rec_conv_b, rec_lambda=m_rec_lambda, rec_w_a=m_rec_w_a, rec_b_a=m_rec_b_a, rec_w_x=m_rec_w_x,
                rec_b_x=m_rec_b_x, rec_w_out=m_rec_w_out, conf_w_pw1=m_conf_w_pw1, conf_b_pw1=m_conf_b_pw1,
                conf_conv_w=m_conf_conv_w, conf_conv_b=m_conf_conv_b, conf_ln_g=m_conf_ln_g, conf_ln_b=m_conf_ln_b,
                conf_w_pw2=m_conf_w_pw2, conf_b_pw2=m_conf_b_pw2, mlp_w_in=m_mlp_w_in, mlp_w_out=m_mlp_w_out,
                final_g=m_final_g)
    v_in = dict(c_ctx=v_c_ctx, w_ada=v_w_ada, b_ada=v_b_ada, norm_g=v_norm_g, rec_w_in=v_rec_w_in, rec_conv_w=v_rec_conv_w,
                rec_conv_b=v_rec_conv_b, rec_lambda=v_rec_lambda, rec_w_a=v_rec_w_a, rec_b_a=v_rec_b_a, rec_w_x=v_rec_w_x,
                rec_b_x=v_rec_b_x, rec_w_out=v_rec_w_out, conf_w_pw1=v_conf_w_pw1, conf_b_pw1=v_conf_b_pw1,
                conf_conv_w=v_conf_conv_w, conf_conv_b=v_conf_conv_b, conf_ln_g=v_conf_ln_g, conf_ln_b=v_conf_ln_b,
                conf_w_pw2=v_conf_w_pw2, conf_b_pw2=v_conf_b_pw2, mlp_w_in=v_mlp_w_in, mlp_w_out=v_mlp_w_out,
                final_g=v_final_g)
    names = list(weights)

    small_items = [c, norm_g, rec_conv_w, rec_lambda, conf_b_pw1, conf_conv_w, conf_conv_b, conf_ln_g, conf_ln_b,
                   conf_b_pw2]
    flat = jnp.concatenate([a.reshape(-1) for a in small_items])
    flat = jnp.pad(flat, (0, SMALL_PACK_ROWS * 128 - flat.shape[0])).reshape(SMALL_PACK_ROWS, 128)
    as_shard = lambda a: a.astype(bf16).reshape(-1, a.shape[-1])
    early_srcs = [flat, as_shard(rec_w_in[0])]
    early_handle, started = _exchange_start(early_srcs, [_own_block_filled(s, me) for s in early_srcs],
                                            "gather_early_start", False)
    zero = started[0, 0]
    gates = _gate_matrix(rec_w_a[0] + zero, rec_w_x[0] + zero)
    late_items = {"mlp": [rec_w_out[0], mlp_w_in, mlp_w_out], "conf": [conf_w_pw1[0], conf_w_pw2[0]]}
    late_shards = {g: [as_shard(a + zero) for a in items] for g, items in late_items.items()}
    late_lands = {g: [_own_block_filled(s, me) for s in shards] for g, shards in late_shards.items()}
    xcat, poscat = _token_rows(x[0] + zero, ctx[0])
    small_all, early = _exchange_wait(early_handle, [gates, xcat, poscat] + late_lands["mlp"] + late_lands["conf"],
                                      "gather_early_wait", False)

    small_all = small_all.reshape(N_DEV, -1)
    off = 0
    small = []
    for a in small_items:
        small.append(small_all[:, off:off + a.size].reshape((N_DEV,) + a.shape))
        off += a.size
    c_all, ng_all, rcw_all, lam_all, bpw1_all, ccw_all, ccb_all, lng_all, lnb_all, bpw2_all = small
    wts = {
        "norm_g": _unshard_cols(ng_all),
        "rec_conv_w": _unshard_cols(rcw_all)[0],
        "rec_lambda": _unshard_cols(lam_all)[0],
        "conf_b_pw1": _unshard_cols(bpw1_all),
        "conf_conv_w": _unshard_cols(ccw_all)[0],
        "conf_conv_b": _unshard_cols(ccb_all),
        "conf_ln_g": _unshard_cols(lng_all),
        "conf_ln_b": _unshard_cols(lnb_all),
        "conf_b_pw2": _unshard_cols(bpw2_all),
        "rec_conv_b": rec_conv_b,
        "rec_b_a": rec_b_a[0].reshape(2, R),
        "rec_b_x": rec_b_x[0].reshape(2, R),
        "final_g": final_g[None],
        "gates": gates,
    }

    c16 = jnp.concatenate([c_all[:, 0], jnp.broadcast_to(c_ctx[None], (8, D))], axis=0)
    b_loc = lax.dynamic_slice_in_dim(b_ada, me * ADA_SHARD, ADA_SHARD, axis=1)[:, None]
    (mods_all,) = _all_gather([_ada_forward(c16, w_ada, b_loc)], "gather_mods")
    mods_all = _unshard_cols(mods_all)
    mods = lax.dynamic_index_in_dim(mods_all, me, axis=1, keepdims=False).reshape(2, N_MOD, D)
    cmod = mods_all[0, 8, :2 * D].reshape(2, D)

    wts["rec_w_in"] = _unshard_cols(early)
    late_handles = {}
    late_handles["mlp"], token = _gather2_start(late_shards["mlp"], late_lands["mlp"], "gather_mlp_start", [early, mods])
    order = [token]

    def late_weights(group, after):
        if group == "mlp_halfway":
            late_handles["mlp"] = _gather2_forward1(late_handles["mlp"], after, "gather_mlp_forward1")
            return late_handles["mlp"][2][0]
        if group == "mlp":
            passed = _gather2_forward2(late_handles["mlp"], after, "gather_mlp_forward2")
            late_handles["conf"], started = _exchange_start(late_shards["conf"], late_lands["conf"], "gather_conf_start",
                                                            False, after=[passed[2][0]])
            got = _gather2_wait(passed, started, "gather_mlp_wait")
        else:
            got = _exchange_wait(late_handles[group], after, "gather_conf_wait", False)
        got = [g.reshape((N_DEV,) + a.shape) for g, a in zip(got, late_items[group])]
        if group == "mlp":
            return {"rec_w_out": got[0].reshape(R, D), "mlp_w_in": got[1], "mlp_w_out": got[2]}
        return {"conf_w_pw1": _unshard_cols(got[0]), "conf_w_pw2": got[1].reshape(D, D)}

    to_blocks = {"rec_w_in": _shard_cols, "conf_w_pw1": _shard_cols, "rec_w_out": _shard_rows, "conf_w_pw2": _shard_rows,
                 "mlp_w_in": lambda g: g, "mlp_w_out": lambda g: g}
    grad_handles = []

    repl_names = ["rec_w_a", "rec_w_x", "rec_b_a", "rec_b_x", "final_g"]

    def send_replicated(grads):
        dwg = grads["gates"]
        repl = {"rec_w_a": jnp.stack([_gate_blocks(dwg, 0), _gate_blocks(dwg, 2)]),
                "rec_w_x": jnp.stack([_gate_blocks(dwg, 1), _gate_blocks(dwg, 3)]),
                "rec_b_a": grads["rec_b_a"], "rec_b_x": grads["rec_b_x"], "final_g": grads["final_g"]}
        flat = jnp.concatenate([repl[n].reshape(-1) for n in repl_names])
        rows = -(-flat.shape[0] // (16 * D)) * 16
        flat = jnp.pad(flat, (0, rows * D - flat.shape[0])).reshape(rows, D).astype(bf16)
        handle, sent = _exchange_start([flat], [_own_block_filled(flat, me)], "gather_replicated_start", False)
        grad_handles.append((["replicated"], handle))
        return sent

    def send_grads(group, grads):
        if group == ["replicated"]:
            return send_replicated(grads)
        blocks = [to_blocks[n](grads[n]) for n in group]
        blocks = [g.reshape(N_DEV, -1, g.shape[-1]) for g in blocks]
        lands = [_own_block_filled(lax.dynamic_index_in_dim(g, me, 0, keepdims=False), me) for g in blocks]
        handle, sent = _exchange_start(blocks, lands, "scatter_start_" + group[0], True)
        grad_handles.append((group, handle))
        return sent

    loss, grad_x, dmods, dcmod, grads = _local_step(
        xcat, poscat, loss_target[0], mods, cmod, wts, late_weights, send_grads,
        lambda partial: lax.psum(partial, ("x", "y", "c")), start_after=order)

    def as2d(shape):
        rows = 1
        for s in shape[:-1]:
            rows *= s
        return (rows, shape[-1])

    def whole(arr, shape):
        arr = arr.reshape((-1,) + as2d(shape))
        return (arr, arr.shape[0])

    shard_shapes = {n: weights[n].shape for n in names}
    g_out, d_out, m_out, v_out = {}, {}, {}, {}

    def adamw(n, pieces, after):
        shape = shard_shapes[n]
        r2, c2 = as2d(shape)
        g, dl, nm, nv = _adamw(pieces, weights[n].reshape(r2, c2), m_in[n].reshape(r2, c2), v_in[n].reshape(r2, c2),
                               "adamw_" + n, after=after)
        g_out[n], d_out[n], m_out[n], v_out[n] = (t.reshape(shape) for t in (g, dl, nm, nv))
        return g

    small_sharded = ["norm_g", "rec_conv_w", "rec_lambda", "conf_b_pw1", "conf_conv_w", "conf_conv_b", "conf_ln_g",
                     "conf_ln_b", "conf_b_pw2"]
    pack = jnp.concatenate([_shard_cols(grads[n]).reshape(N_DEV, -1) for n in small_sharded], axis=1)
    pack = jnp.pad(pack, ((0, 0), (0, SMALL_PACK_ROWS * 128 - pack.shape[1]))).reshape(N_DEV, SMALL_PACK_ROWS, 128)
    small_handle, token = _exchange_start(
        [pack], [_own_block_filled(lax.dynamic_index_in_dim(pack, me, 0, keepdims=False), me)], "scatter_small_start",
        True, after=[grad_x])
    dm_flat = jnp.concatenate([dmods.reshape(-1), dcmod.reshape(-1), grads["rec_conv_b"].reshape(-1)])
    dm_len = dm_flat.shape[0]
    dm_flat = jnp.pad(dm_flat, (0, 128 * 128 - dm_len)).reshape(128, 128)
    dm_handle, token = _exchange_start([dm_flat], [_own_block_filled(dm_flat, me)], "gather_dmods_start", False,
                                       after=[token])

    done = token
    for group, handle in grad_handles:
        if group == ["replicated"]:
            repl_all = _exchange_wait(handle, done, "gather_replicated_wait", False)[0].reshape(N_DEV, -1)
            off = 0
            for n in repl_names:
                size = weights[n].size
                done = adamw(n, [whole(repl_all[:, off:off + size], shard_shapes[n])], [done])
                off += size
            continue
        for n, got in zip(group, _exchange_wait(handle, done, "scatter_wait_" + group[0], True)):
            done = adamw(n, [(got, N_DEV)], [done])

    dm_all = _exchange_wait(dm_handle, done, "gather_dmods_wait", False)[0].reshape(N_DEV, -1)
    dmods_all = dm_all[:, :2 * N_MOD * D].reshape(N_DEV, 2, N_MOD * D)
    dcmod_all = jnp.pad(dm_all[:, 2 * N_MOD * D:2 * N_MOD * D + 2 * D], ((0, 0), (0, (N_MOD - 2) * D)))
    g16_full = jnp.stack([jnp.concatenate([dmods_all[:, 0], dcmod_all], axis=0),
                          jnp.concatenate([dmods_all[:, 1], jnp.zeros_like(dcmod_all)], axis=0)])
    g16 = lax.dynamic_slice_in_dim(g16_full, me * ADA_SHARD, ADA_SHARD, axis=2)
    dw_ada, ds_part = _ada_backward(c16, g16, w_ada)
    ds_handle, token = _exchange_start([ds_part[0]], [_own_block_filled(ds_part[0], me)], "gather_dsilu_start", False)
    done = adamw("w_ada", [whole(dw_ada, shard_shapes["w_ada"])], [token])
    done = adamw("rec_conv_b", [whole(dm_all[:, dm_len - R:dm_len], shard_shapes["rec_conv_b"])], [done])
    db_terms = jnp.concatenate([dmods_all, jnp.stack([dcmod_all, jnp.zeros_like(dcmod_all)], axis=1)], axis=0)
    done = adamw("b_ada", [whole(db_terms, shard_shapes["b_ada"])], [done])
    pack_recv = _exchange_wait(small_handle, done, "scatter_small_wait", True)[0].reshape(N_DEV, -1)
    off = 0
    for n in small_sharded:
        size = weights[n].size
        done = adamw(n, [whole(pack_recv[:, off:off + size], shard_shapes[n])], [done])
        off += size
    ds_all = _exchange_wait(ds_handle, done, "gather_dsilu_wait", False)[0]
    adamw("c_ctx", [whole(ds_all[:, 0], shard_shapes["c_ctx"])], [])

    return (loss, grad_x[None], *[g_out[n] for n in names], *[d_out[n] for n in names],
            *[m_out[n] for n in names], *[v_out[n] for n in names])
```

```python
import functools

import jax
import jax.numpy as jnp
from jax import lax
from jax.experimental import pallas as pl
from jax.experimental.pallas import tpu as pltpu

f32 = jnp.float32
bf16 = jnp.bfloat16

N_DEV = 8
D = 1024
T_LAT = 2048
T_CTX = 256
T_ALL = T_CTX + T_LAT
R = 1280
N_BLK = 16
BLK = R // N_BLK
F = 4096
GRID_W = 64
RG_C = 8.0
EPS = 1e-6
POS_BASE = 10000.0
N_MOD = 6
ADA_SHARD = N_MOD * D // N_DEV

ADAM_LR = 0.001
ADAM_B1 = 0.9
ADAM_B2 = 0.999
ADAM_EPS = 1e-08
ADAM_WD = 0.01
ADAM_STEP = 10

VMEM_LIMIT_V7X = 56 * 1024 * 1024
HALO = 16
MESH = pl.DeviceIdType.MESH


def _cparams(*sem):
    return pltpu.CompilerParams(dimension_semantics=sem, vmem_limit_bytes=VMEM_LIMIT_V7X)


def _pick(n, cands):
    for c in cands:
        if n % c == 0:
            return c
    raise ValueError(f"no block size for {n}")


def _position():
    x, y, c = lax.axis_index("x"), lax.axis_index("y"), lax.axis_index("c")
    return x, y, c, 4 * x + 2 * y + c


def _peer(x, y, c, k):
    px = (1 - x) if (k >> 2) & 1 else x
    py = (1 - y) if (k >> 1) & 1 else y
    pc = (1 - c) if k & 1 else c
    return (px, py, pc), 4 * px + 2 * py + pc


def _exchange(arrs, name, scatter):
    n = len(arrs)

    def body(*refs):
        ins, outs = refs[:n], refs[n:2 * n]
        send_sems, recv_sems, local_sems = refs[2 * n:]
        x, y, c, me = _position()
        local = []
        for a in range(n):
            src = ins[a].at[me] if scatter else ins[a]
            cp = pltpu.make_async_copy(src, outs[a].at[me], local_sems.at[a])
            cp.start()
            local.append(cp)
        sends, recvs = [], []
        for a in range(n):
            for k in range(1, N_DEV):
                peer, peer_lin = _peer(x, y, c, k)
                src = ins[a].at[peer_lin] if scatter else ins[a]
                cp = pltpu.make_async_remote_copy(
                    src_ref=src, dst_ref=outs[a].at[me], send_sem=send_sems.at[a, k - 1],
                    recv_sem=recv_sems.at[a, k - 1], device_id=peer, device_id_type=MESH)
                cp.start()
                sends.append(cp)
                recvs.append(pltpu.make_async_remote_copy(
                    src_ref=src, dst_ref=outs[a].at[peer_lin], send_sem=send_sems.at[a, k - 1],
                    recv_sem=recv_sems.at[a, k - 1], device_id=peer, device_id_type=MESH))
        for cp in recvs:
            cp.wait_recv()
        for cp in sends:
            cp.wait_send()
        for cp in local:
            cp.wait()

    if scatter:
        out_shape = [jax.ShapeDtypeStruct(a.shape, a.dtype) for a in arrs]
    else:
        out_shape = [jax.ShapeDtypeStruct((N_DEV,) + a.shape, a.dtype) for a in arrs]
    any_spec = pl.BlockSpec(memory_space=pl.ANY)
    return pl.pallas_call(
        body, name=name, out_shape=out_shape,
        in_specs=[any_spec] * n, out_specs=[any_spec] * n,
        scratch_shapes=[pltpu.SemaphoreType.DMA((n, N_DEV - 1)), pltpu.SemaphoreType.DMA((n, N_DEV - 1)),
                        pltpu.SemaphoreType.DMA((n,))],
    )(*arrs)


def _all_gather(arrs, name):
    return _exchange(arrs, name, scatter=False)


def _lin(p):
    return 4 * p[0] + 2 * p[1] + p[2]


HBM_SPEC = pl.BlockSpec(memory_space=pltpu.HBM)
SEM_SPEC = pl.BlockSpec(memory_space=pltpu.SEMAPHORE)
DATAFLOW_EFFECT = pltpu.SideEffectType.DATAFLOW_SIDE_EFFECTING


def _split_copies(srcs, lands, send_sems, recv_sems, scatter):
    x, y, c, me = _position()
    out = []
    for a in range(len(srcs)):
        for k in range(1, N_DEV):
            peer, peer_lin = _peer(x, y, c, k)
            src = srcs[a].at[peer_lin] if scatter else srcs[a]
            mk = lambda slot: pltpu.make_async_remote_copy(
                src_ref=src, dst_ref=lands[a].at[slot], send_sem=send_sems.at[a * (N_DEV - 1) + k - 1],
                recv_sem=recv_sems.at[a * (N_DEV - 1) + k - 1], device_id=peer, device_id_type=MESH)
            out.append((mk(me), mk(peer_lin)))
    return out


def _exchange_start(srcs, lands, name, scatter, after=()):
    n = len(srcs)
    n_after = len(after)

    def body(*refs):
        srcs_r, lands_r = refs[:n], refs[n:2 * n]
        send_sems, recv_sems = refs[2 * n + n_after], refs[2 * n + n_after + 1]
        token = refs[-1]
        for outgoing, _ in _split_copies(srcs_r, lands_r, send_sems, recv_sems, scatter):
            outgoing.start()
        token[...] = jnp.zeros_like(token)

    hbm = lambda a: pltpu.HBM(a.shape, a.dtype)
    res = pl.pallas_call(
        body, name=name,
        out_shape=(pltpu.SemaphoreType.DMA((n * (N_DEV - 1),)), pltpu.SemaphoreType.DMA((n * (N_DEV - 1),)),
                   *[hbm(a) for a in srcs], *[hbm(a) for a in lands], jax.ShapeDtypeStruct((8, 128), f32)),
        in_specs=[HBM_SPEC] * (2 * n) + [pl.BlockSpec(memory_space=pl.ANY)] * n_after,
        out_specs=(SEM_SPEC, SEM_SPEC, *[HBM_SPEC] * (2 * n), pl.BlockSpec(memory_space=pltpu.VMEM)),
        input_output_aliases={i: 2 + i for i in range(2 * n)},
        compiler_params=pltpu.CompilerParams(has_side_effects=DATAFLOW_EFFECT),
    )(*[pltpu.with_memory_space_constraint(a, pltpu.HBM) for a in list(srcs) + list(lands)], *after)
    return (res[0], res[1], list(res[2:2 + n]), list(res[2 + n:2 + 2 * n])), res[-1]


def _exchange_wait(handle, after, name, scatter):
    send_sems, recv_sems, srcs, lands = handle
    n = len(srcs)
    after = list(after) if isinstance(after, (list, tuple)) else [after]

    def body(*refs):
        srcs_r, lands_r = refs[:n], refs[n:2 * n]
        send_s, recv_s = refs[2 * n], refs[2 * n + 1]
        for outgoing, incoming in _split_copies(srcs_r, lands_r, send_s, recv_s, scatter):
            outgoing.wait_send()
            incoming.wait_recv()

    hbm = lambda a: pltpu.HBM(a.shape, a.dtype)
    res = pl.pallas_call(
        body, name=name, out_shape=tuple(hbm(a) for a in list(srcs) + list(lands)),
        in_specs=[HBM_SPEC] * (2 * n) + [SEM_SPEC, SEM_SPEC] + [pl.BlockSpec(memory_space=pl.ANY)] * len(after),
        out_specs=tuple([HBM_SPEC] * (2 * n)),
        input_output_aliases={i: i for i in range(2 * n)},
        compiler_params=pltpu.CompilerParams(has_side_effects=DATAFLOW_EFFECT),
    )(*srcs, *lands, send_sems, recv_sems, *after)
    return list(res[n:])


def _split_call(body, name, hbm_ins, kept, in_sems, n_new_sems, after, with_token):
    n_in, n_sem = len(hbm_ins), len(in_sems)
    out_shape, out_specs = [], []
    if n_new_sems:
        out_shape += [pltpu.SemaphoreType.DMA((n_new_sems,))] * 2
        out_specs += [SEM_SPEC] * 2
    first_kept = len(out_shape)
    out_shape += [pltpu.HBM(hbm_ins[i].shape, hbm_ins[i].dtype) for i in kept]
    out_specs += [HBM_SPEC] * len(kept)
    if with_token:
        out_shape.append(jax.ShapeDtypeStruct((8, 128), f32))
        out_specs.append(pl.BlockSpec(memory_space=pltpu.VMEM))

    def wrapped(*refs):
        outs = refs[n_in + n_sem + len(after):]
        body(refs[:n_in], refs[n_in:n_in + n_sem], outs[:2] if n_new_sems else ())
        if with_token:
            outs[-1][...] = jnp.zeros_like(outs[-1])

    return pl.pallas_call(
        wrapped, name=name, out_shape=tuple(out_shape),
        in_specs=[HBM_SPEC] * n_in + [SEM_SPEC] * n_sem + [pl.BlockSpec(memory_space=pl.ANY)] * len(after),
        out_specs=tuple(out_specs), input_output_aliases={i: first_kept + j for j, i in enumerate(kept)},
        compiler_params=pltpu.CompilerParams(has_side_effects=DATAFLOW_EFFECT),
    )(*[pltpu.with_memory_space_constraint(a, pltpu.HBM) for a in hbm_ins], *in_sems, *after)


def _rcopy(src, dst, sems, k, to):
    return pltpu.make_async_remote_copy(src_ref=src, dst_ref=dst, send_sem=sems[0].at[k], recv_sem=sems[1].at[k],
                                        device_id=to, device_id_type=MESH)


def _gather2_start(shards, lands, name, after):
    n = len(shards)

    def body(ins, sems_in, sems_out):
        x, y, c, me = _position()
        for a in range(n):
            for k, to in enumerate(((x, y, 1 - c), (1 - x, y, c), (x, 1 - y, c))):
                _rcopy(ins[a], ins[n + a].at[me], sems_out, 3 * a + k, to).start()

    res = _split_call(body, name, list(shards) + list(lands), range(2 * n), (), 3 * n, after, True)
    return (res[0], res[1], list(res[2:2 + n]), list(res[2 + n:2 + 2 * n])), res[-1]


def _gather2_forward1(handle, after, name):
    send_sems, recv_sems, srcs, lands = handle
    n = len(srcs)

    def body(ins, sems_in, sems_out):
        x, y, c, me = _position()
        sib, xn, yn = (x, y, 1 - c), (1 - x, y, c), (x, 1 - y, c)
        for a in range(n):
            for k, peer in enumerate((sib, xn, yn)):
                _rcopy(ins[a], ins[n + a].at[me], sems_in, 3 * a + k, peer).wait_send()
                _rcopy(ins[a], ins[n + a].at[_lin(peer)], sems_in, 3 * a + k, peer).wait_recv()
        for a in range(n):
            land = ins[n + a]
            _rcopy(land.at[_lin(xn)], land.at[_lin(xn)], sems_out, 3 * a, sib).start()
            _rcopy(land.at[_lin(yn)], land.at[_lin(yn)], sems_out, 3 * a + 1, sib).start()

            @pl.when(c == 0)
            def _():
                _rcopy(land.at[_lin(xn)], land.at[_lin(xn)], sems_out, 3 * a + 2, yn).start()

            @pl.when(c == 1)
            def _():
                _rcopy(land.at[_lin(yn)], land.at[_lin(yn)], sems_out, 3 * a + 2, xn).start()

    res = _split_call(body, name, list(srcs) + list(lands), range(n, 2 * n), (send_sems, recv_sems), 3 * n, [after], False)
    return (res[0], res[1], list(res[2:]))


def _gather2_forward2(handle, after, name):
    send_sems, recv_sems, lands = handle
    n = len(lands)

    def body(ins, sems_in, sems_out):
        x, y, c, me = _position()
        sib, dg = (x, y, 1 - c), _lin((1 - x, 1 - y, c))
        for a in range(n):
            for k, slot in enumerate((_lin((1 - x, y, 1 - c)), _lin((x, 1 - y, 1 - c)), dg)):
                done = _rcopy(ins[a].at[slot], ins[a].at[slot], sems_in, 3 * a + k, sib)
                done.wait_send()
                done.wait_recv()
        for a in range(n):
            _rcopy(ins[a].at[dg], ins[a].at[dg], sems_out, a, sib).start()

    res = _split_call(body, name, list(lands), range(n), (send_sems, recv_sems), n, [after], False)
    return (res[0], res[1], list(res[2:]))


def _gather2_wait(handle, after, name):
    send_sems, recv_sems, lands = handle
    n = len(lands)

    def body(ins, sems_in, sems_out):
        x, y, c, me = _position()
        slot = _lin((1 - x, 1 - y, 1 - c))
        for a in range(n):
            done = _rcopy(ins[a].at[slot], ins[a].at[slot], sems_in, a, (x, y, 1 - c))
            done.wait_send()
            done.wait_recv()

    return list(_split_call(body, name, list(lands), range(n), (send_sems, recv_sems), 0, [after], False))


def _own_block_filled(block, me):
    land = lax.empty((N_DEV,) + block.shape, block.dtype)
    return lax.dynamic_update_index_in_dim(land, block, me, 0)


ANY_SPEC = pl.BlockSpec(memory_space=pl.ANY)


def _mm(a, b, name, ta=False, tb=False, out_dtype=f32, after=()):
    if ta:
        k_dim, m_dim = a.shape
    else:
        m_dim, k_dim = a.shape
    if tb:
        n_dim, k2 = b.shape
    else:
        k2, n_dim = b.shape
    assert k_dim == k2, (a.shape, b.shape)
    assert a.dtype == bf16 and b.dtype == bf16
    bm = _pick(m_dim, (512, 768, 640, 256, 128))
    bn = _pick(n_dim, (512, 640, 256, 128))
    bk = k_dim if k_dim <= 2560 else _pick(k_dim, (1024, 1280, 768, 512))
    nk = k_dim // bk
    a_spec = (pl.BlockSpec((bk, bm), lambda i, j, k: (k, i)) if ta
              else pl.BlockSpec((bm, bk), lambda i, j, k: (i, k)))
    b_spec = (pl.BlockSpec((bn, bk), lambda i, j, k: (j, k)) if tb
              else pl.BlockSpec((bk, bn), lambda i, j, k: (k, j)))
    dims = (((0 if ta else 1,), (1 if tb else 0,)), ((), ()))

    n_after = len(after)

    def body_single(a_ref, b_ref, *rest):
        o_ref = rest[n_after]
        o_ref[...] = lax.dot_general(a_ref[...], b_ref[...], dims, preferred_element_type=f32).astype(o_ref.dtype)

    def body(a_ref, b_ref, *rest):
        o_ref, acc_ref = rest[n_after:]
        k = pl.program_id(2)

        @pl.when(k == 0)
        def _():
            acc_ref[...] = jnp.zeros_like(acc_ref)

        acc_ref[...] += lax.dot_general(a_ref[...], b_ref[...], dims, preferred_element_type=f32)

        @pl.when(k == nk - 1)
        def _():
            o_ref[...] = acc_ref[...].astype(o_ref.dtype)

    return pl.pallas_call(
        body_single if nk == 1 else body, name=name, out_shape=jax.ShapeDtypeStruct((m_dim, n_dim), out_dtype),
        grid=(m_dim // bm, n_dim // bn, nk), in_specs=[a_spec, b_spec] + [ANY_SPEC] * n_after,
        out_specs=pl.BlockSpec((bm, bn), lambda i, j, k: (i, j)),
        scratch_shapes=[] if nk == 1 else [pltpu.VMEM((bm, bn), f32)],
        compiler_params=_cparams("parallel", "parallel", "arbitrary"),
    )(a, b, *after)


def _rin(arr, width=None, cb=0, roff=0):
    return (arr, arr.shape[1] if width is None else width, cb, roff)


def _rowcall(fn, name, rows, tm, row_ins, par_ins, row_outs, acc_outs=(), after=()):
    nr, npar, nro, n_after = len(row_ins), len(par_ins), len(row_outs), len(after)
    in_specs, args = [], []
    for arr, width, cb, roff in row_ins:
        if roff >= 0:
            imap = lambda i, cb=cb, roff=roff: (i + roff, cb)
        else:
            imap = lambda i, cb=cb, roff=roff: (jnp.maximum(i + roff, 0), cb)
        in_specs.append(pl.BlockSpec((tm, width), imap))
        args.append(arr)
    for p in par_ins:
        in_specs.append(pl.BlockSpec(p.shape, lambda i: (0, 0)))
        args.append(p)
    out_shape, out_specs = [], []
    for width, dt in row_outs:
        out_shape.append(jax.ShapeDtypeStruct((rows, width), dt))
        out_specs.append(pl.BlockSpec((tm, width), lambda i: (i, 0)))
    for p, width in acc_outs:
        out_shape.append(jax.ShapeDtypeStruct((p, width), f32))
        out_specs.append(pl.BlockSpec((p, width), lambda i: (0, 0)))

    def body(*refs):
        i = pl.program_id(0)
        res = fn(i, *[r[...] for r in refs[:nr + npar]])
        outs = refs[nr + npar + n_after:]
        for o, v in zip(outs[:nro], res[:nro]):
            o[...] = v.astype(o.dtype)
        if acc_outs:
            @pl.when(i == 0)
            def _():
                for o in outs[nro:]:
                    o[...] = jnp.zeros_like(o)

            for o, v in zip(outs[nro:], res[nro:]):
                o[...] += v

    return pl.pallas_call(
        body, name=name, out_shape=out_shape, grid=(rows // tm,), in_specs=in_specs + [ANY_SPEC] * n_after,
        out_specs=out_specs, compiler_params=_cparams("arbitrary"),
    )(*args, *after)


def _rms(x, g):
    return x * lax.rsqrt(jnp.mean(x * x, axis=-1, keepdims=True) + EPS) * g


def _normmod(x, g, sc, sh):
    return _rms(x, g) * (1.0 + sc) + sh


def _gelu(x):
    return 0.5 * x * (1.0 + jnp.tanh(0.7978845608028654 * (x + 0.044715 * (x * x * x))))


def _sigmoid(x):
    return 0.5 * (jnp.tanh(0.5 * x) + 1.0)


def _coeff_parts(pre_a, pre_x, ba, bx, lam):
    r = _sigmoid(pre_a + ba)
    ig = _sigmoid(pre_x + bx)
    nl = -lam
    sp = jnp.maximum(nl, 0.0) + jnp.log(1.0 + jnp.exp(-jnp.abs(nl)))
    la = -RG_C * r * sp
    a = jnp.exp(la)
    one_minus_a2 = -jnp.tanh(la) * (a * a + 1.0)
    inv_m = lax.rsqrt(one_minus_a2)
    return r, ig, sp, a, one_minus_a2 * inv_m, inv_m


def _coeff(pre_a, pre_x, u, ba, bx, lam):
    _, ig, _, a, m, _ = _coeff_parts(pre_a, pre_x, ba, bx, lam)
    return a, m * (ig * u)


def _coeff_bwd(pre_a, pre_x, u, ba, bx, lam, da, db):
    r, ig, sp, a, m, inv_m = _coeff_parts(pre_a, pre_x, ba, bx, lam)
    dbu = db * u
    dig = dbu * m
    dm = dbu * ig
    dla = a * (da - dm * a * inv_m)
    dpa = dla * (-RG_C * sp) * (r * (1.0 - r))
    dpx = dig * (ig * (1.0 - ig))
    dsp = jnp.sum(dla * (-RG_C * r), axis=0, keepdims=True)
    dlam = -dsp * _sigmoid(-lam)
    return (dpa, dpx, db * m * ig, jnp.sum(dpa, axis=0, keepdims=True), jnp.sum(dpx, axis=0, keepdims=True), dlam)


SCAN_CHUNK = 256


def _scan_call(a, v, chunk_of, reverse, name, backward, after=()):
    rows, width = a.shape
    n_out = 1 if backward else 2
    nt = SCAN_CHUNK // 8

    def body(a_ref, v_ref, *rest):
        outs, state_ref = rest[len(after):-1], rest[-1]

        @pl.when(pl.program_id(0) == 0)
        def _():
            state_ref[...] = jnp.zeros_like(state_ref)

        rid = lax.broadcasted_iota(jnp.int32, (8, width), 0)
        last_row = 0 if reverse else 7

        def shift(x, s, fill):
            rolled = pltpu.roll(x, (8 - s) if reverse else s, axis=0)
            return jnp.where((rid >= 8 - s) if reverse else (rid < s), fill, rolled)

        def tile(j, st):
            t0 = pl.multiple_of((nt - 1 - j if reverse else j) * 8, 8)
            at = a_ref[pl.ds(t0, 8), :]
            coef = shift(at, 1, 1.0) if backward else at
            acc = v_ref[pl.ds(t0, 8), :]
            for s in (1, 2, 4):
                acc = coef * shift(acc, s, 0.0) + acc
                coef = coef * shift(coef, s, 1.0)
            out = coef * st + acc
            outs[0][pl.ds(t0, 8), :] = out
            last = out[last_row:last_row + 1]
            if backward:
                return at[last_row:last_row + 1] * last
            outs[1][pl.ds(t0, 8), :] = shift(out, 1, st)
            return last

        state_ref[0:1, :] = lax.fori_loop(0, nt, tile, state_ref[0:1, :])

    spec = pl.BlockSpec((SCAN_CHUNK, width), lambda t: (chunk_of(t), 0))
    return pl.pallas_call(
        body, name=name, out_shape=[jax.ShapeDtypeStruct((rows, width), f32)] * n_out,
        grid=(rows // SCAN_CHUNK,), in_specs=[spec, spec] + [ANY_SPEC] * len(after), out_specs=[spec] * n_out,
        scratch_shapes=[pltpu.VMEM((8, width), f32)],
        compiler_params=_cparams("arbitrary"),
    )(a, v, *after)


CONV_CHUNK = 256


def _fill_padded(pad_ref, src_ref, start, n):
    cb = pad_ref.shape[1]
    pad_ref[pl.ds(0, HALO), :] = jnp.zeros((HALO, cb), f32)
    pad_ref[pl.ds(HALO, n), :] = src_ref[pl.ds(start, n), :].astype(f32)
    pad_ref[pl.ds(HALO + n, HALO), :] = jnp.zeros((HALO, cb), f32)


def _dwconv_fwd(x, x_cb0, w, b, taps, pad_left, segments, cb, name, emit_bf16):
    rows = x.shape[0]
    width = w.shape[1]

    def body(x_ref, w_ref, b_ref, *rest):
        outs, xp = rest[:-1], rest[-1]
        for start, n in segments:
            _fill_padded(xp, x_ref, start, n)
            for c0 in range(0, n, CONV_CHUNK):
                acc = jnp.zeros((CONV_CHUNK, cb), f32) + b_ref[...]
                for k in range(taps):
                    acc = acc + w_ref[k:k + 1, :] * xp[pl.ds(HALO + c0 + k - pad_left, CONV_CHUNK), :]
                for o in outs:
                    o[pl.ds(start + c0, CONV_CHUNK), :] = acc.astype(o.dtype)

    out_dtypes = [f32, bf16] if emit_bf16 else [f32]
    return pl.pallas_call(
        body, name=name, out_shape=[jax.ShapeDtypeStruct((rows, width), dt) for dt in out_dtypes],
        grid=(width // cb,),
        in_specs=[pl.BlockSpec((rows, cb), lambda j: (0, j + x_cb0)), pl.BlockSpec((taps, cb), lambda j: (0, j)),
                  pl.BlockSpec((1, cb), lambda j: (0, j))],
        out_specs=[pl.BlockSpec((rows, cb), lambda j: (0, j))] * len(out_dtypes),
        scratch_shapes=[pltpu.VMEM((rows + 2 * HALO, cb), f32)],
        compiler_params=_cparams("parallel"),
    )(x, w, b)


def _dwconv_bwd(douts, x, x_cb0, w, taps, pad_left, segments, cb, name, dx_dtype):
    rows = x.shape[0]
    width = w.shape[1]
    nd = len(douts)

    def body(*refs):
        d_refs, x_ref, w_ref = refs[:nd], refs[nd], refs[nd + 1]
        dx_ref, dw_ref, db_ref, dp, dsum = refs[nd + 2:]
        dw_ref[...] = jnp.zeros_like(dw_ref)
        db_ref[...] = jnp.zeros_like(db_ref)
        if nd > 1:
            total = d_refs[0][...]
            for r in d_refs[1:]:
                total = total + r[...]
            dsum[...] = total
            d_ref = dsum
        else:
            d_ref = d_refs[0]
        for start, n in segments:
            _fill_padded(dp, d_ref, start, n)
            for c0 in range(0, n, CONV_CHUNK):
                db_ref[...] += jnp.sum(dp[pl.ds(HALO + c0, CONV_CHUNK), :], axis=0, keepdims=True)
                xchunk = x_ref[pl.ds(start + c0, CONV_CHUNK), :].astype(f32)
                acc = jnp.zeros((CONV_CHUNK, cb), f32)
                for k in range(taps):
                    shifted = dp[pl.ds(HALO + c0 + pad_left - k, CONV_CHUNK), :]
                    acc = acc + w_ref[k:k + 1, :] * shifted
                    dw_ref[k:k + 1, :] += jnp.sum(shifted * xchunk, axis=0, keepdims=True)
                dx_ref[pl.ds(start + c0, CONV_CHUNK), :] = acc.astype(dx_ref.dtype)

    dspec = pl.BlockSpec((rows, cb), lambda j: (0, j))
    return pl.pallas_call(
        body, name=name,
        out_shape=[jax.ShapeDtypeStruct((rows, width), dx_dtype), jax.ShapeDtypeStruct((taps, width), f32),
                   jax.ShapeDtypeStruct((1, width), f32)],
        grid=(width // cb,),
        in_specs=[dspec] * nd + [pl.BlockSpec((rows, cb), lambda j: (0, j + x_cb0)),
                                 pl.BlockSpec((taps, cb), lambda j: (0, j))],
        out_specs=[dspec, pl.BlockSpec((taps, cb), lambda j: (0, j)), pl.BlockSpec((1, cb), lambda j: (0, j))],
        scratch_shapes=[pltpu.VMEM((rows + 2 * HALO, cb), f32), pltpu.VMEM((rows, cb), f32)],
        compiler_params=_cparams("parallel"),
    )(*douts, x, w)


def _ada_forward(c16, w_ada, b_loc):
    def body(c_ref, w_ref, b_ref, o_ref):
        cv = c_ref[...]
        s = (cv * _sigmoid(cv)).astype(bf16)
        o_ref[0] = jnp.dot(s, w_ref[0].astype(bf16), preferred_element_type=f32) + b_ref[0]

    return pl.pallas_call(
        body, name="ada_forward", out_shape=jax.ShapeDtypeStruct((2, 16, ADA_SHARD), f32), grid=(2,),
        in_specs=[pl.BlockSpec((16, D), lambda l: (0, 0)), pl.BlockSpec((1, D, ADA_SHARD), lambda l: (l, 0, 0)),
                  pl.BlockSpec((1, 1, ADA_SHARD), lambda l: (l, 0, 0))],
        out_specs=pl.BlockSpec((1, 16, ADA_SHARD), lambda l: (l, 0, 0)),
        compiler_params=_cparams("parallel"),
    )(c16, w_ada, b_loc)


def _ada_backward(c16, g16, w_ada):
    def body(c_ref, g_ref, w_ref, dw_ref, ds_ref):
        cv = c_ref[...]
        s = (cv * _sigmoid(cv)).astype(bf16)
        g = g_ref[0].astype(bf16)
        dw_ref[0] = lax.dot_general(s, g, (((0,), (0,)), ((), ())), preferred_element_type=f32)
        ds = lax.dot_general(g, w_ref[0].astype(bf16), (((1,), (1,)), ((), ())), preferred_element_type=f32)
        cc = cv[8:9]
        sg = _sigmoid(cc)
        dsilu = sg * (1.0 + cc * (1.0 - sg))
        ds_ref[0] = jnp.zeros((8, D), f32) + jnp.sum(ds[8:16], axis=0, keepdims=True) * dsilu

    return pl.pallas_call(
        body, name="ada_backward",
        out_shape=[jax.ShapeDtypeStruct((2, D, ADA_SHARD), f32), jax.ShapeDtypeStruct((2, 8, D), f32)], grid=(2,),
        in_specs=[pl.BlockSpec((16, D), lambda l: (0, 0)), pl.BlockSpec((1, 16, ADA_SHARD), lambda l: (l, 0, 0)),
                  pl.BlockSpec((1, D, ADA_SHARD), lambda l: (l, 0, 0))],
        out_specs=[pl.BlockSpec((1, D, ADA_SHARD), lambda l: (l, 0, 0)), pl.BlockSpec((1, 8, D), lambda l: (l, 0, 0))],
        compiler_params=_cparams("parallel"),
    )(c16, g16, w_ada)


def _adamw(pieces, w, m, v, name, after=()):
    rows, cols = w.shape
    n_arr, n_after = len(pieces), len(after)
    tm = 256 if (rows % 256 == 0 and rows > 256) else rows
    counts = [p[1] for p in pieces]
    first_tiles = [(p[2] if len(p) > 2 else 0) // tm for p in pieces]
    pieces = [p[0] for p in pieces]

    def body(*refs):
        p_refs = refs[:n_arr]
        w_ref, m_ref, v_ref = refs[n_arr:n_arr + 3]
        g_ref, d_ref, nm_ref, nv_ref = refs[n_arr + 3 + n_after:]
        g = None
        for p_ref in p_refs:
            for j in range(p_ref.shape[0]):
                term = p_ref[j].astype(f32)
                g = term if g is None else g + term
        m2 = ADAM_B1 * m_ref[...] + (1.0 - ADAM_B1) * g
        v2 = ADAM_B2 * v_ref[...] + (1.0 - ADAM_B2) * (g * g)
        m_hat = m2 / (1.0 - ADAM_B1 ** ADAM_STEP)
        v_hat = v2 / (1.0 - ADAM_B2 ** ADAM_STEP)
        g_ref[...] = g
        d_ref[...] = -ADAM_LR * (m_hat / (jnp.sqrt(v_hat) + ADAM_EPS) + ADAM_WD * w_ref[...])
        nm_ref[...] = m2
        nv_ref[...] = v2

    spec = pl.BlockSpec((tm, cols), lambda i: (i, 0))
    return pl.pallas_call(
        body, name=name, out_shape=[jax.ShapeDtypeStruct((rows, cols), f32)] * 4, grid=(rows // tm,),
        in_specs=[pl.BlockSpec((cnt, tm, cols), lambda i, t=t: (0, i + t, 0)) for cnt, t in zip(counts, first_tiles)]
        + [spec, spec, spec]
        + [ANY_SPEC] * n_after,
        out_specs=[spec] * 4, compiler_params=_cparams("parallel"),
    )(*pieces, w, m, v, *after)


MLP_TM = 256
FB = F // N_DEV


def _stack_rows(vals, n):
    cols = vals[0].shape[1]
    rid = lax.broadcasted_iota(jnp.int32, (n, cols), 0)
    out = jnp.zeros((n, cols), f32)
    for k, v in enumerate(vals):
        out = jnp.where(rid == k, v, out)
    return out


N_MLP_PARAMS = 9


class _ParamRows:
    def __init__(self, ref):
        self.ref = ref

    def __getitem__(self, sl):
        return self.ref[8 * sl.start:8 * sl.start + 1, :]


def _resident(shape, imap):
    return pl.BlockSpec(shape, imap, pipeline_mode=pl.Buffered(1))


def _mlp_forward(xa, xa_roff, out_prev, par, w_in, w_out, layer, name):
    def body(xa_ref, op_ref, par_ref, win_ref, wout_ref, x1_ref, h_ref, r_ref, mo_ref, x2_ref, hn_ref):
        p = _ParamRows(par_ref)
        x1 = xa_ref[...] + p[0:1] * (op_ref[...] + p[1:2])
        h = _normmod(x1, p[2:3], p[3:4], p[4:5]).astype(bf16)
        x1_ref[...] = x1
        h_ref[...] = h
        mo = jnp.zeros((MLP_TM, D), f32)
        for j in range(N_DEV):
            r = jnp.maximum(jnp.dot(h, win_ref[j], preferred_element_type=f32), 0.0)
            r_ref[:, j * FB:(j + 1) * FB] = r.astype(bf16)
            mo = mo + jnp.dot((r * r).astype(bf16), wout_ref[j], preferred_element_type=f32)
        mo_ref[...] = mo.astype(bf16)
        x2 = x1 + p[5:6] * mo
        x2_ref[...] = x2
        hn_ref[...] = _normmod(x2, p[6:7], p[7:8], p[8:9]).astype(bf16)

    row = lambda width: pl.BlockSpec((MLP_TM, width), lambda i: (i, 0))
    return pl.pallas_call(
        body, name=name, grid=(T_LAT // MLP_TM,),
        out_shape=[jax.ShapeDtypeStruct((T_LAT, D), f32), jax.ShapeDtypeStruct((T_LAT, D), bf16),
                   jax.ShapeDtypeStruct((T_LAT, F), bf16), jax.ShapeDtypeStruct((T_LAT, D), bf16),
                   jax.ShapeDtypeStruct((T_LAT, D), f32), jax.ShapeDtypeStruct((T_LAT, D), bf16)],
        in_specs=[pl.BlockSpec((MLP_TM, D), lambda i: (i + xa_roff, 0)), row(D), pl.BlockSpec((8 * N_MLP_PARAMS, D), lambda i: (0, 0)),
                  _resident((N_DEV, None, D, FB), lambda i: (0, layer, 0, 0)),
                  _resident((N_DEV, None, FB, D), lambda i: (0, layer, 0, 0))],
        out_specs=[row(D), row(D), row(F), row(D), row(D), row(D)],
        compiler_params=_cparams("parallel"),
    )(xa, out_prev, par, w_in, w_out)


def _mlp_backward(dx2, x1, r, mo, out_prev, par, w_in, w_out, layer, name, after=()):
    nt = (((1,), (1,)), ((), ()))

    n_after = len(after)

    def body(dx2_ref, x1_ref, r_ref, mo_ref, op_ref, par_ref, win_ref, wout_ref, *rest):
        dx1_ref, dop_ref, dmo_ref, dhid_ref, acc_ref = rest[n_after:]
        p = _ParamRows(par_ref)
        dx2v = dx2_ref[...]
        dmo = (p[5:6] * dx2v).astype(bf16)
        dmo_ref[...] = dmo
        dh = jnp.zeros((MLP_TM, D), f32)
        mo = mo_ref[...].astype(f32)
        for j in range(N_DEV):
            rf = r_ref[:, j * FB:(j + 1) * FB].astype(f32)
            dact = lax.dot_general(dmo, wout_ref[j], nt, preferred_element_type=f32)
            dhid = (dact * (2.0 * rf)).astype(bf16)
            dhid_ref[:, j * FB:(j + 1) * FB] = dhid
            dh = dh + lax.dot_general(dhid, win_ref[j], nt, preferred_element_type=f32)
        x1 = x1_ref[...]
        _, vjp = jax.vjp(_normmod, x1, p[2:3], p[3:4], p[4:5])
        dx, dng, dsc, dsh = vjp(dh)
        dx1 = dx2v + dx
        dx1_ref[...] = dx1
        dop_ref[...] = (p[0:1] * dx1).astype(bf16)
        sums = _stack_rows([jnp.sum(dx1 * (op_ref[...] + p[1:2]), axis=0, keepdims=True),
                            p[0:1] * jnp.sum(dx1, axis=0, keepdims=True), dng, dsc, dsh,
                            jnp.sum(dx2v * mo, axis=0, keepdims=True)], 8)

        @pl.when(pl.program_id(0) == 0)
        def _():
            acc_ref[...] = jnp.zeros_like(acc_ref)

        acc_ref[...] += sums

    row = lambda width: pl.BlockSpec((MLP_TM, width), lambda i: (i, 0))
    return pl.pallas_call(
        body, name=name, grid=(T_LAT // MLP_TM,),
        out_shape=[jax.ShapeDtypeStruct((T_LAT, D), f32), jax.ShapeDtypeStruct((T_LAT, D), bf16),
                   jax.ShapeDtypeStruct((T_LAT, D), bf16), jax.ShapeDtypeStruct((T_LAT, F), bf16),
                   jax.ShapeDtypeStruct((8, D), f32)],
        in_specs=[row(D), row(D), row(F), row(D), row(D), pl.BlockSpec((8 * N_MLP_PARAMS, D), lambda i: (0, 0)),
                  _resident((N_DEV, None, D, FB), lambda i: (0, layer, 0, 0)),
                  _resident((N_DEV, None, FB, D), lambda i: (0, layer, 0, 0))] + [ANY_SPEC] * n_after,
        out_specs=[row(D), row(D), row(D), row(F), pl.BlockSpec((8, D), lambda i: (0, 0))],
        compiler_params=_cparams("arbitrary"),
    )(dx2, x1, r, mo, out_prev, par, w_in, w_out, *after)


def _mlp_weight_grads(h, dhid, r, dmo, layer, other, tag):
    tn = (((0,), (0,)), ((), ()))

    def body_in(h_ref, dhid_ref, *rest):
        rest[-1][...] = lax.dot_general(h_ref[...], dhid_ref[...], tn, preferred_element_type=f32).astype(bf16)

    def body_out(r_ref, dmo_ref, *rest):
        rf = r_ref[...].astype(f32)
        rest[-1][...] = lax.dot_general((rf * rf).astype(bf16), dmo_ref[...], tn,
                                        preferred_element_type=f32).astype(bf16)

    def call(body, name, operands, specs, block, prev):
        extra = [] if prev is None else [prev]
        return pl.pallas_call(
            body, name=name, grid=(N_DEV,), out_shape=jax.ShapeDtypeStruct((N_DEV, 2) + block, bf16),
            in_specs=specs + [pl.BlockSpec(memory_space=pl.ANY)] * len(extra),
            out_specs=pl.BlockSpec((None, None) + block, lambda j: (j, layer, 0, 0)),
            input_output_aliases={} if prev is None else {2: 0},
            compiler_params=_cparams("parallel"),
        )(*operands, *extra)

    dw_in = call(body_in, tag + "_mlp_in_dw", [h, dhid],
                 [_resident((T_LAT, D), lambda j: (0, 0)), pl.BlockSpec((T_LAT, FB), lambda j: (0, j))], (D, FB),
                 None if other is None else other[0])
    dw_out = call(body_out, tag + "_mlp_out_dw", [r, dmo],
                  [pl.BlockSpec((T_LAT, FB), lambda j: (0, j)), _resident((T_LAT, D), lambda j: (0, 0))], (FB, D),
                  None if other is None else other[1])
    return dw_in, dw_out


def _pos_embed():
    n_rows = T_LAT // GRID_W
    q = D // 4
    omega = 1.0 / (POS_BASE ** (jnp.arange(q, dtype=f32) / q))
    er = jnp.arange(n_rows, dtype=jnp.int32).astype(f32)[:, None] * omega[None, :]
    ec = jnp.arange(GRID_W, dtype=jnp.int32).astype(f32)[:, None] * omega[None, :]
    by_row = jnp.concatenate([jnp.sin(er), jnp.cos(er)], axis=-1)[:, None, :]
    by_col = jnp.concatenate([jnp.sin(ec), jnp.cos(ec)], axis=-1)[None, :, :]
    full = jnp.concatenate([jnp.broadcast_to(by_row, (n_rows, GRID_W, D // 2)),
                            jnp.broadcast_to(by_col, (n_rows, GRID_W, D // 2))], axis=-1)
    return full.reshape(T_LAT, D)


HALF = R // 2
BLK_PER_HALF = N_BLK // 2
N_PARTS = 4


def _gate_matrix(w_a, w_x):
    eye = jnp.eye(BLK_PER_HALF, dtype=bf16)
    cols = []
    for h in range(2):
        for d in range(2):
            for w in (w_a, w_x):
                blocks = w[d, BLK_PER_HALF * h:BLK_PER_HALF * (h + 1)].astype(bf16)
                cols.append(jnp.einsum("hij,hg->higj", blocks, eye).reshape(HALF, HALF))
    return jnp.concatenate(cols, axis=1)


def _gate_blocks(dwg, part):
    out = []
    for h in range(2):
        blk = dwg[:, (N_PARTS * h + part) * HALF:(N_PARTS * h + part + 1) * HALF]
        blk = blk.reshape(BLK_PER_HALF, BLK, BLK_PER_HALF, BLK)
        out.append(jnp.moveaxis(jnp.diagonal(blk, axis1=0, axis2=2), -1, 0))
    return jnp.concatenate(out, axis=0)


GATE_BM = 768


def _gates_dx(dpre, wg, after=()):
    rows = dpre.shape[0]
    n_after = len(after)

    def body(d_ref, w_ref, *rest):
        rest[n_after][...] = lax.dot_general(d_ref[...], w_ref[...], (((1,), (1,)), ((), ())),
                                             preferred_element_type=f32)

    return pl.pallas_call(
        body, name="l0_gates_dx", grid=(rows // GATE_BM, 2), out_shape=jax.ShapeDtypeStruct((rows, R), f32),
        in_specs=[pl.BlockSpec((GATE_BM, N_PARTS * HALF), lambda i, h: (i, h)),
                  pl.BlockSpec((HALF, N_PARTS * HALF), lambda i, h: (0, h))] + [ANY_SPEC] * n_after,
        out_specs=pl.BlockSpec((GATE_BM, HALF), lambda i, h: (i, h)),
        compiler_params=_cparams("parallel", "parallel"),
    )(dpre, wg, *after)


COEFF_TM = 256


def _dir_params(d, *params):
    specs = [pl.BlockSpec((None, 1, HALF), lambda h, i: (d, 0, h))] * len(params)
    return specs, [p.reshape(2, 1, R) for p in params]


def _gates_coeff_fwd(ub, u, wg, ba, bx, lam, d):
    rows = u.shape[0]

    def body(ub_ref, u_ref, w_ref, ba_ref, bx_ref, lam_ref, a_ref, b_ref):
        pre = jnp.dot(ub_ref[...], w_ref[...], preferred_element_type=f32)
        a, b = _coeff(pre[:, :HALF], pre[:, HALF:], u_ref[...], ba_ref[...], bx_ref[...], lam_ref[...])
        a_ref[...] = a
        b_ref[...] = b

    tile = pl.BlockSpec((COEFF_TM, HALF), lambda h, i: (i, h))
    pspecs, pargs = _dir_params(d, ba, bx, lam)
    return pl.pallas_call(
        body, name=f"l0_gates_coeff_{d}", grid=(2, rows // COEFF_TM),
        out_shape=[jax.ShapeDtypeStruct((rows, R), f32)] * 2,
        in_specs=[tile, tile, pl.BlockSpec((HALF, 2 * HALF), lambda h, i: (0, 2 * h + d))] + pspecs,
        out_specs=[tile, tile], compiler_params=_cparams("parallel", "parallel"),
    )(ub, u, wg, *pargs)


def _gates_coeff_bwd(ub, u, dh, yp, wg, ba, bx, lam, d, dpre_prev):
    rows = u.shape[0]
    n_prev = 0 if dpre_prev is None else 1

    def body(ub_ref, u_ref, dh_ref, yp_ref, w_ref, ba_ref, bx_ref, lam_ref, *rest):
        dpre_ref, du_ref, dba_ref, dbx_ref, dlam_ref = rest[n_prev:]
        pre = jnp.dot(ub_ref[...], w_ref[...], preferred_element_type=f32)
        dhv = dh_ref[...]
        dpa, dpx, du, dba, dbx, dlam = _coeff_bwd(pre[:, :HALF], pre[:, HALF:], u_ref[...], ba_ref[...], bx_ref[...],
                                                  lam_ref[...], dhv * yp_ref[...], dhv)
        dpre_ref[:, :HALF] = dpa.astype(bf16)
        dpre_ref[:, HALF:] = dpx.astype(bf16)
        du_ref[...] = du

        @pl.when(pl.program_id(1) == 0)
        def _():
            dba_ref[...] = jnp.zeros_like(dba_ref)
            dbx_ref[...] = jnp.zeros_like(dbx_ref)
            dlam_ref[...] = jnp.zeros_like(dlam_ref)

        dba_ref[...] += dba
        dbx_ref[...] += dbx
        dlam_ref[...] += dlam

    tile = pl.BlockSpec((COEFF_TM, HALF), lambda h, i: (i, h))
    acc = pl.BlockSpec((1, HALF), lambda h, i: (0, h))
    pspecs, pargs = _dir_params(d, ba, bx, lam)
    extra = [] if dpre_prev is None else [dpre_prev]
    return pl.pallas_call(
        body, name=f"l0_gates_coeff_bwd_{d}", grid=(2, rows // COEFF_TM),
        out_shape=[jax.ShapeDtypeStruct((rows, 2 * N_PARTS * HALF), bf16), jax.ShapeDtypeStruct((rows, R), f32)]
        + [jax.ShapeDtypeStruct((1, R), f32)] * 3,
        in_specs=[tile] * 4 + [pl.BlockSpec((HALF, 2 * HALF), lambda h, i: (0, 2 * h + d))] + pspecs
        + [ANY_SPEC] * n_prev,
        out_specs=[pl.BlockSpec((COEFF_TM, 2 * HALF), lambda h, i: (i, 2 * h + d)), tile, acc, acc, acc],
        input_output_aliases={8: 0} if n_prev else {}, compiler_params=_cparams("parallel", "arbitrary"),
    )(ub, u, dh, yp, wg, *pargs, *extra)


def _gates_dw(u, dpre):
    rows = u.shape[0]

    def body(u_ref, d_ref, o_ref):
        o_ref[...] = lax.dot_general(u_ref[...], d_ref[...], (((0,), (0,)), ((), ())), preferred_element_type=f32)

    return pl.pallas_call(
        body, name="l0_gates_dw", grid=(2 * N_PARTS,), out_shape=jax.ShapeDtypeStruct((HALF, 2 * N_PARTS * HALF), f32),
        in_specs=[pl.BlockSpec((rows, HALF), lambda j: (0, j // N_PARTS)), pl.BlockSpec((rows, HALF), lambda j: (0, j))],
        out_specs=pl.BlockSpec((HALF, HALF), lambda j: (0, j)), compiler_params=_cparams("parallel"),
    )(u, dpre)


N_SCAN_CHUNKS = T_ALL // SCAN_CHUNK
SCAN_FWD = lambda t: t
SCAN_FWD_BWD = lambda t: N_SCAN_CHUNKS - 1 - t
SCAN_REV = lambda t: jnp.where(t == 0, 0, N_SCAN_CHUNKS - t)
SCAN_REV_BWD = lambda t: jnp.where(t == N_SCAN_CHUNKS - 1, 0, t + 1)
CONV_SEGMENTS = ((0, T_CTX), (T_CTX, T_LAT))
TM = 128
FUSED_TM = 256


def _token_rows(x, ctx):
    return (jnp.concatenate([ctx, x], axis=0),
            jnp.concatenate([jnp.zeros((T_CTX, D), f32), _pos_embed()], axis=0))


def _local_step(xcat, poscat, target, mods, cmod, wts, late_weights, send_grads, reduce_loss, start_after=()):
    sh1, sc1, g1, sh2, sc2, g2 = [[mods[l, i][None] for l in range(2)] for i in range(N_MOD)]
    ng = wts["norm_g"]
    scp = jnp.concatenate([cmod[1][None], sc1[0]], axis=0)
    shp = jnp.concatenate([cmod[0][None], sh1[0]], axis=0)

    ctx_tiles = T_CTX // FUSED_TM
    nt = (((1,), (1,)), ((), ()))

    def blend(i, p):
        sel = jnp.where(i < ctx_tiles, 1.0, 0.0)
        return sel * p[0:1] + (1.0 - sel) * p[1:2]

    def f_pre0(i, xc, pos, g, scp_, shp_, w):
        x0 = xc + pos
        h = _normmod(x0, g, blend(i, scp_), blend(i, shp_)).astype(bf16)
        return x0, h, jnp.dot(h, w, preferred_element_type=f32)

    x0cat, h0, gr = _rowcall(f_pre0, "l0_prenorm_in_proj", T_ALL, FUSED_TM, [_rin(xcat), _rin(poscat)],
                             [ng[0, 0][None], scp, shp, wts["rec_w_in"]], [(D, f32), (D, bf16), (2 * R, f32)],
                             after=start_after)
    u, ub = _dwconv_fwd(gr, R // 256, wts["rec_conv_w"], wts["rec_conv_b"], 4, 1, CONV_SEGMENTS, 256,
                        "l0_conv", True)
    gate_args = (wts["gates"], wts["rec_b_a"], wts["rec_b_x"], wts["rec_lambda"])
    a0, b0 = _gates_coeff_fwd(ub, u, *gate_args, 0)
    a1, b1 = _gates_coeff_fwd(ub, u, *gate_args, 1)
    halfway = late_weights("mlp_halfway", a1)
    y0, yp0 = _scan_call(a0, b0, SCAN_FWD, False, "l0_scan_fwd", False, after=[halfway])
    y1, yp1 = _scan_call(a1, b1, SCAN_REV, True, "l0_scan_rev", False)

    wts = dict(wts, **late_weights("mlp", y1))

    def f_gate_out(i, gp, y0_, y1_, w):
        z = (_gelu(gp) * (y0_ + y1_)).astype(bf16)
        return z, jnp.dot(z, w, preferred_element_type=f32)

    zb, out0 = _rowcall(f_gate_out, "l0_gate_out_proj", T_LAT, FUSED_TM,
                        [_rin(gr, R, 0, ctx_tiles), _rin(y0, None, 0, ctx_tiles), _rin(y1, None, 0, ctx_tiles)],
                        [wts["rec_w_out"]], [(R, bf16), (D, f32)])

    zero_d = jnp.zeros((1, D), f32)

    def mlp_params(rows):
        rows = rows + [zero_d] * (N_MLP_PARAMS - len(rows))
        return jnp.concatenate([jnp.broadcast_to(r, (8, D)) for r in rows], axis=0)

    par0 = mlp_params([g1[0], zero_d, ng[0, 1][None], sc2[0], sh2[0], g2[0], ng[1, 0][None], sc1[1], sh1[1]])
    x1, h1, r0, mo0, x2, h2 = _mlp_forward(x0cat, T_CTX // MLP_TM, out0, par0, wts["mlp_w_in"], wts["mlp_w_out"], 0,
                                           "l0_mlp")

    wts = dict(wts, **late_weights("conf", x2))
    def glu(pa, pb, b1):
        return (pa + b1[:, :D]) * _sigmoid(pb + b1[:, D:])

    def f_pw1_glu(i, h_, b1, w):
        p = jnp.dot(h_, w, preferred_element_type=f32)
        return glu(p[:, :D], p[:, D:], b1), p

    zg, pw = _rowcall(f_pw1_glu, "l1_pw1_glu", T_LAT, FUSED_TM, [_rin(h2)], [wts["conf_b_pw1"], wts["conf_w_pw1"]],
                      [(D, f32), (2 * D, bf16)])
    (zc,) = _dwconv_fwd(zg, 0, wts["conf_conv_w"], wts["conf_conv_b"], 31, 15, ((0, T_LAT),), 128, "l1_conv", False)

    def ln_silu(z, lg, lb):
        mu = jnp.mean(z, axis=-1, keepdims=True)
        zc_ = z - mu
        var = jnp.mean(zc_ * zc_, axis=-1, keepdims=True)
        yv = zc_ * lax.rsqrt(var + EPS) * lg + lb
        return yv * _sigmoid(yv)

    def f_lnsilu_pw2(i, z, lg, lb, w):
        s = ln_silu(z, lg, lb).astype(bf16)
        return s, jnp.dot(s, w, preferred_element_type=f32)

    sb, out1 = _rowcall(f_lnsilu_pw2, "l1_ln_silu_pw2", T_LAT, FUSED_TM, [_rin(zc)],
                        [wts["conf_ln_g"], wts["conf_ln_b"], wts["conf_w_pw2"]], [(D, bf16), (D, f32)])
    par1 = mlp_params([g1[1], wts["conf_b_pw2"], ng[1, 1][None], sc2[1], sh2[1], g2[1]])
    x3, h3, r1, mo1, x4, _ = _mlp_forward(x2, 0, out1, par1, wts["mlp_w_in"], wts["mlp_w_out"], 1, "l1_mlp")

    def loss_fn(x4_, fg, tgt):
        err = _rms(x4_, fg) - tgt
        per_row = jnp.mean(err * err, axis=-1, keepdims=True)
        return 0.5 * jnp.sum(per_row, axis=0, keepdims=True)

    def f_head(i, x4_, tgt, fg):
        loss, vjp = jax.vjp(lambda a, e: loss_fn(a, e, tgt), x4_, fg)
        dx, dfg = vjp(jnp.ones((1, 1), f32))
        return dx, jnp.broadcast_to(loss, (1, 128)), dfg

    dx4, loss_acc, dfinal_g = _rowcall(f_head, "head", T_LAT, TM, [_rin(x4), _rin(target)], [wts["final_g"]],
                                       [(D, f32)], [(1, 128), (1, D)])

    grads = {"final_g": dfinal_g}
    loss = reduce_loss(loss_acc[0, 0])

    dx3, dout1, dmo1, dhid1, acc1 = _mlp_backward(dx4, x3, r1, mo1, out1, par1, wts["mlp_w_in"], wts["mlp_w_out"], 1,
                                                  "l1_mlp_bwd", after=[loss.reshape(1, 1)])
    mlp_dw = _mlp_weight_grads(h3, dhid1, r1, dmo1, 1, None, "l1")
    dg1_1, db_pw2, dng11, dsc2_1, dsh2_1, dg2_1 = [acc1[k:k + 1] for k in range(6)]

    grads["conf_w_pw2"] = _mm(sb, dout1, "l1_pw2_dw", ta=True, out_dtype=bf16)
    grads["conf_b_pw2"] = db_pw2

    def f_pw2_lnsilu_bwd(i, z, dout, lg, lb, w):
        ds = lax.dot_general(dout, w, nt, preferred_element_type=f32)
        _, vjp = jax.vjp(ln_silu, z, lg, lb)
        return vjp(ds)

    dzc, dln_g, dln_b = _rowcall(f_pw2_lnsilu_bwd, "l1_pw2_ln_silu_bwd", T_LAT, FUSED_TM, [_rin(zc), _rin(dout1)],
                                 [wts["conf_ln_g"], wts["conf_ln_b"], wts["conf_w_pw2"]], [(D, f32)], [(1, D)] * 2)
    grads["conf_ln_g"], grads["conf_ln_b"] = dln_g, dln_b
    dzg, dconv_w, dconv_b = _dwconv_bwd([dzc], zg, 0, wts["conf_conv_w"], 31, 15, ((0, T_LAT),), 128,
                                        "l1_conv_bwd", f32)
    grads["conf_conv_w"], grads["conf_conv_b"] = dconv_w, dconv_b

    def f_glu_pw1_norm_bwd(i, p_, dz, x_, dxs, b1, g_, sc_, sh_, w):
        pf = p_.astype(f32)
        _, vjp = jax.vjp(glu, pf[:, :D], pf[:, D:], b1)
        da, db, db1 = vjp(dz)
        dp = jnp.concatenate([da, db], axis=1).astype(bf16)
        dh = lax.dot_general(dp, w, nt, preferred_element_type=f32)
        _, vjp = jax.vjp(_normmod, x_, g_, sc_, sh_)
        dx, dg, dsc, dsh = vjp(dh)
        return dp, dx + dxs, db1, dg, dsc, dsh

    dpw, dx2, db_pw1, dng10, dsc1_1, dsh1_1 = _rowcall(
        f_glu_pw1_norm_bwd, "l1_glu_pw1_normmod_bwd", T_LAT, FUSED_TM, [_rin(pw), _rin(dzg), _rin(x2), _rin(dx3)],
        [wts["conf_b_pw1"], ng[1, 0][None], sc1[1], sh1[1], wts["conf_w_pw1"]], [(2 * D, bf16), (D, f32)],
        [(1, 2 * D), (1, D), (1, D), (1, D)])
    grads["conf_b_pw1"] = db_pw1
    grads["conf_w_pw1"] = _mm(h2, dpw, "l1_pw1_dw", ta=True, out_dtype=bf16)
    sent = send_grads(["conf_w_pw2", "conf_w_pw1"], grads)

    dx1, dout0, dmo0, dhid0, acc0 = _mlp_backward(dx2, x1, r0, mo0, out0, par0, wts["mlp_w_in"], wts["mlp_w_out"], 0,
                                                  "l0_mlp_bwd", after=[sent])
    grads["mlp_w_in"], grads["mlp_w_out"] = _mlp_weight_grads(h1, dhid0, r0, dmo0, 0, mlp_dw, "l0")
    sent = send_grads(["mlp_w_in", "mlp_w_out"], grads)
    dg1_0, _, dng01, dsc2_0, dsh2_0, dg2_0 = [acc0[k:k + 1] for k in range(6)]

    grads["rec_w_out"] = _mm(zb, dout0, "l0_out_proj_dw", ta=True, out_dtype=bf16, after=[sent])
    sent = send_grads(["rec_w_out"], grads)

    def f_out_gate_bwd(i, gp, y0_, y1_, dout, w):
        lat = jnp.where(i < ctx_tiles, 0.0, 1.0)
        dz = lax.dot_general(dout, w, nt, preferred_element_type=f32)
        _, vjp = jax.vjp(lambda a, b: _gelu(a) * b, gp, y0_ + y1_)
        dgp, dy = vjp(dz)
        return dgp * lat, dy * lat

    dgp, dy = _rowcall(f_out_gate_bwd, "l0_out_proj_gate_bwd", T_ALL, FUSED_TM,
                       [_rin(gr, R, 0), _rin(y0), _rin(y1), _rin(dout0, None, 0, -ctx_tiles)], [wts["rec_w_out"]],
                       [(R, bf16), (R, f32)], after=[sent])
    (dh_f,) = _scan_call(a0, dy, SCAN_FWD_BWD, True, "l0_scan_fwd_bwd", True)
    (dh_r,) = _scan_call(a1, dy, SCAN_REV_BWD, False, "l0_scan_rev_bwd", True)

    dpre, du_f, *dpar_f = _gates_coeff_bwd(ub, u, dh_f, yp0, *gate_args, 0, None)
    dpre, du_r, *dpar_r = _gates_coeff_bwd(ub, u, dh_r, yp1, *gate_args, 1, dpre)
    grads["rec_b_a"], grads["rec_b_x"], grads["rec_lambda"] = [
        jnp.concatenate([f.reshape(-1), r_.reshape(-1)]).reshape(2, R) for f, r_ in zip(dpar_f, dpar_r)]
    grads["gates"] = _gates_dw(ub, dpre)
    sent = send_grads(["replicated"], grads)
    du_gates = _gates_dx(dpre, wts["gates"], after=[sent])
    drec, dconv4_w, dconv4_b = _dwconv_bwd([du_f, du_r, du_gates], gr, R // 256, wts["rec_conv_w"], 4, 1,
                                           CONV_SEGMENTS, 256, "l0_conv_bwd", bf16)
    grads["rec_conv_w"], grads["rec_conv_b"] = dconv4_w, dconv4_b
    dgr = jnp.concatenate([dgp, drec], axis=1)
    grads["rec_w_in"] = _mm(h0, dgr, "l0_in_proj_dw", ta=True, out_dtype=bf16)
    sent = send_grads(["rec_w_in"], grads)

    def f_pre0_bwd(i, x0, dgr_, dxs, g, scp_, shp_, w):
        lat = jnp.where(i < ctx_tiles, 0.0, 1.0)
        dh = lax.dot_general(dgr_, w, nt, preferred_element_type=f32)
        _, vjp = jax.vjp(lambda a, b, c, e: _normmod(a, b, blend(i, c), blend(i, e)), x0, g, scp_, shp_)
        dx, dg, dscp, dshp = vjp(dh)
        return dx + lat * dxs, dg, dscp, dshp

    dx0cat, dng00, dscp, dshp = _rowcall(
        f_pre0_bwd, "l0_in_proj_prenorm_bwd", T_ALL, FUSED_TM,
        [_rin(x0cat), _rin(dgr), _rin(dx1, None, 0, -ctx_tiles)], [ng[0, 0][None], scp, shp, wts["rec_w_in"]],
        [(D, f32)], [(1, D), (2, D), (2, D)], after=[sent])

    grads["norm_g"] = jnp.stack([jnp.concatenate([dng00, dng01], 0), jnp.concatenate([dng10, dng11], 0)])
    dmods = jnp.stack([
        jnp.concatenate([dshp[1:2], dscp[1:2], dg1_0, dsh2_0, dsc2_0, dg2_0], axis=0),
        jnp.concatenate([dsh1_1, dsc1_1, dg1_1, dsh2_1, dsc2_1, dg2_1], axis=0)])
    dcmod = jnp.concatenate([dshp[0:1], dscp[0:1]], axis=0)
    return loss, dx0cat[T_CTX:], dmods, dcmod, grads


def _unshard_cols(g):
    g = jnp.moveaxis(g, 0, -2)
    return g.reshape(g.shape[:-2] + (g.shape[-2] * g.shape[-1],))


def _shard_cols(w):
    w = w.reshape(w.shape[:-1] + (N_DEV, w.shape[-1] // N_DEV))
    return jnp.moveaxis(w, -2, 0)


def _shard_rows(w):
    return w.reshape((N_DEV, w.shape[0] // N_DEV) + w.shape[1:])


SMALL_PACK_ROWS = 64
REPL_FINAL_G_ROWS = -(-D // BLK)
REPL_ROWS = -(-(2 * 2 * N_BLK * BLK + 2 * 2 * N_BLK + REPL_FINAL_G_ROWS) // 16) * 16


def kernel(x, c, ctx, c_ctx, w_ada, b_ada, norm_g, rec_w_in, rec_conv_w, rec_conv_b, rec_lambda, rec_w_a, rec_b_a, rec_w_x, rec_b_x, rec_w_out, conf_w_pw1, conf_b_pw1, conf_conv_w, conf_conv_b, conf_ln_g, conf_ln_b, conf_w_pw2, conf_b_pw2, mlp_w_in, mlp_w_out, final_g, loss_target, m_c_ctx, m_w_ada, m_b_ada, m_norm_g, m_rec_w_in, m_rec_conv_w, m_rec_conv_b, m_rec_lambda, m_rec_w_a, m_rec_b_a, m_rec_w_x, m_rec_b_x, m_rec_w_out, m_conf_w_pw1, m_conf_b_pw1, m_conf_conv_w, m_conf_conv_b, m_conf_ln_g, m_conf_ln_b, m_conf_w_pw2, m_conf_b_pw2, m_mlp_w_in, m_mlp_w_out, m_final_g, v_c_ctx, v_w_ada, v_b_ada, v_norm_g, v_rec_w_in, v_rec_conv_w, v_rec_conv_b, v_rec_lambda, v_rec_w_a, v_rec_b_a, v_rec_w_x, v_rec_b_x, v_rec_w_out, v_conf_w_pw1, v_conf_b_pw1, v_conf_conv_w, v_conf_conv_b, v_conf_ln_g, v_conf_ln_b, v_conf_w_pw2, v_conf_b_pw2, v_mlp_w_in, v_mlp_w_out, v_final_g):
    me = 4 * lax.axis_index("x") + 2 * lax.axis_index("y") + lax.axis_index("c")
    weights = dict(c_ctx=c_ctx, w_ada=w_ada, b_ada=b_ada, norm_g=norm_g, rec_w_in=rec_w_in, rec_conv_w=rec_conv_w,
                   rec_conv_b=rec_conv_b, rec_lambda=rec_lambda, rec_w_a=rec_w_a, rec_b_a=rec_b_a, rec_w_x=rec_w_x,
                   rec_b_x=rec_b_x, rec_w_out=rec_w_out, conf_w_pw1=conf_w_pw1, conf_b_pw1=conf_b_pw1,
                   conf_conv_w=conf_conv_w, conf_conv_b=conf_conv_b, conf_ln_g=conf_ln_g, conf_ln_b=conf_ln_b,
                   conf_w_pw2=conf_w_pw2, conf_b_pw2=conf_b_pw2, mlp_w_in=mlp_w_in, mlp_w_out=mlp_w_out, final_g=final_g)
    m_in = dict(c_ctx=m_c_ctx, w_ada=m_w_ada, b_ada=m_b_ada, norm_g=m_norm_g, rec_w_in=m_rec_w_in, rec_conv_w=m_rec_conv_w,
                rec_conv_b=m_rec_conv_b, rec_lambda=m_rec_lambda, rec_w_a=m_rec_w_a, rec_b_a=m_rec_b_a, rec_w_x=m_rec_w_x,
                rec_b_x=m_rec_b_x, rec_w_out=m_rec_w_out, conf_w_pw1=m_conf_w_pw1, conf_b_pw1=m_conf_b_pw1,
                conf_conv_w=m_conf_conv_w, conf_conv_b=m_conf_conv_b, conf_ln_g=m_conf_ln_g, conf_ln_b=m_conf_ln_b,
                conf_w_pw2=m_conf_w_pw2, conf_b_pw2=m_conf_b_pw2, mlp_w_in=m_mlp_w_in, mlp_w_out=m_mlp_w_out,
                final_g=m_final_g)
    v_in = dict(c_ctx=v_c_ctx, w_ada=v_w_ada, b_ada=v_b_ada, norm_g=v_norm_g, rec_w_in=v_rec_w_in, rec_conv_w=v_rec_conv_w,
                rec_conv_b=v_rec_conv_b, rec_lambda=v_rec_lambda, rec_w_a=v_rec_w_a, rec_b_a=v_rec_b_a, rec_w_x=v_rec_w_x,
                rec_b_x=v_rec_b_x, rec_w_out=v_rec_w_out, conf_w_pw1=v_conf_w_pw1, conf_b_pw1=v_conf_b_pw1,
                conf_conv_w=v_conf_conv_w, conf_conv_b=v_conf_conv_b, conf_ln_g=v_conf_ln_g, conf_ln_b=v_conf_ln_b,
                conf_w_pw2=v_conf_w_pw2, conf_b_pw2=v_conf_b_pw2, mlp_w_in=v_mlp_w_in, mlp_w_out=v_mlp_w_out,
                final_g=v_final_g)
    names = list(weights)

    small_items = [c, norm_g, rec_conv_w, rec_lambda, conf_b_pw1, conf_conv_w, conf_conv_b, conf_ln_g, conf_ln_b,
                   conf_b_pw2]
    flat = jnp.concatenate([a.reshape(-1) for a in small_items])
    flat = jnp.pad(flat, (0, SMALL_PACK_ROWS * 128 - flat.shape[0])).reshape(SMALL_PACK_ROWS, 128)
    as_shard = lambda a: a.astype(bf16).reshape(-1, a.shape[-1])
    early_srcs = [flat, as_shard(rec_w_in[0])]
    early_handle, started = _exchange_start(early_srcs, [_own_block_filled(s, me) for s in early_srcs],
                                            "gather_early_start", False)
    zero = started[0, 0]
    gates = _gate_matrix(rec_w_a[0] + zero, rec_w_x[0] + zero)
    late_items = {"mlp": [rec_w_out[0], mlp_w_in, mlp_w_out], "conf": [conf_w_pw1[0], conf_w_pw2[0]]}
    late_shards = {g: [as_shard(a + zero) for a in items] for g, items in late_items.items()}
    late_lands = {g: [_own_block_filled(s, me) for s in shards] for g, shards in late_shards.items()}
    xcat, poscat = _token_rows(x[0] + zero, ctx[0])
    small_all, early = _exchange_wait(early_handle, [gates, xcat, poscat] + late_lands["mlp"] + late_lands["conf"],
                                      "gather_early_wait", False)

    small_all = small_all.reshape(N_DEV, -1)
    off = 0
    small = []
    for a in small_items:
        small.append(small_all[:, off:off + a.size].reshape((N_DEV,) + a.shape))
        off += a.size
    c_all, ng_all, rcw_all, lam_all, bpw1_all, ccw_all, ccb_all, lng_all, lnb_all, bpw2_all = small
    wts = {
        "norm_g": _unshard_cols(ng_all),
        "rec_conv_w": _unshard_cols(rcw_all)[0],
        "rec_lambda": _unshard_cols(lam_all)[0],
        "conf_b_pw1": _unshard_cols(bpw1_all),
        "conf_conv_w": _unshard_cols(ccw_all)[0],
        "conf_conv_b": _unshard_cols(ccb_all),
        "conf_ln_g": _unshard_cols(lng_all),
        "conf_ln_b": _unshard_cols(lnb_all),
        "conf_b_pw2": _unshard_cols(bpw2_all),
        "rec_conv_b": rec_conv_b,
        "rec_b_a": rec_b_a[0].reshape(2, R),
        "rec_b_x": rec_b_x[0].reshape(2, R),
        "final_g": final_g[None],
        "gates": gates,
    }

    c16 = jnp.concatenate([c_all[:, 0], jnp.broadcast_to(c_ctx[None], (8, D))], axis=0)
    b_loc = lax.dynamic_slice_in_dim(b_ada, me * ADA_SHARD, ADA_SHARD, axis=1)[:, None]
    (mods_all,) = _all_gather([_ada_forward(c16, w_ada, b_loc)], "gather_mods")
    mods_all = _unshard_cols(mods_all)
    mods = lax.dynamic_index_in_dim(mods_all, me, axis=1, keepdims=False).reshape(2, N_MOD, D)
    cmod = mods_all[0, 8, :2 * D].reshape(2, D)

    wts["rec_w_in"] = _unshard_cols(early)
    late_handles = {}
    late_handles["mlp"], token = _gather2_start(late_shards["mlp"], late_lands["mlp"], "gather_mlp_start", [early, mods])
    order = [token]

    def late_weights(group, after):
        if group == "mlp_halfway":
            late_handles["mlp"] = _gather2_forward1(late_handles["mlp"], after, "gather_mlp_forward1")
            return late_handles["mlp"][2][0]
        if group == "mlp":
            passed = _gather2_forward2(late_handles["mlp"], after, "gather_mlp_forward2")
            late_handles["conf"], started = _exchange_start(late_shards["conf"], late_lands["conf"], "gather_conf_start",
                                                            False, after=[passed[2][0]])
            got = _gather2_wait(passed, started, "gather_mlp_wait")
        else:
            got = _exchange_wait(late_handles[group], after, "gather_conf_wait", False)
        got = [g.reshape((N_DEV,) + a.shape) for g, a in zip(got, late_items[group])]
        if group == "mlp":
            return {"rec_w_out": got[0].reshape(R, D), "mlp_w_in": got[1], "mlp_w_out": got[2]}
        return {"conf_w_pw1": _unshard_cols(got[0]), "conf_w_pw2": got[1].reshape(D, D)}

    to_blocks = {"rec_w_in": _shard_cols, "conf_w_pw1": _shard_cols, "rec_w_out": _shard_rows, "conf_w_pw2": _shard_rows,
                 "mlp_w_in": lambda g: g, "mlp_w_out": lambda g: g}
    grad_handles = []

    repl_names = ["rec_w_a", "rec_w_x", "rec_b_a", "rec_b_x", "final_g"]

    def send_replicated(grads):
        dwg = grads["gates"]
        repl = {"rec_w_a": jnp.stack([_gate_blocks(dwg, 0), _gate_blocks(dwg, 2)]),
                "rec_w_x": jnp.stack([_gate_blocks(dwg, 1), _gate_blocks(dwg, 3)]),
                "rec_b_a": grads["rec_b_a"], "rec_b_x": grads["rec_b_x"],
                "final_g": jnp.pad(grads["final_g"], ((0, 0), (0, REPL_FINAL_G_ROWS * BLK - D)))}
        flat = jnp.concatenate([repl[n].reshape(-1, BLK) for n in repl_names], axis=0)
        flat = jnp.pad(flat, ((0, REPL_ROWS - flat.shape[0]), (0, 0))).astype(bf16)
        handle, sent = _exchange_start([flat], [_own_block_filled(flat, me)], "gather_replicated_start", False)
        grad_handles.append((["replicated"], handle))
        return sent

    def send_grads(group, grads):
        if group == ["replicated"]:
            return send_replicated(grads)
        blocks = [to_blocks[n](grads[n]) for n in group]
        blocks = [g.reshape(N_DEV, -1, g.shape[-1]) for g in blocks]
        lands = [_own_block_filled(lax.dynamic_index_in_dim(g, me, 0, keepdims=False), me) for g in blocks]
        handle, sent = _exchange_start(blocks, lands, "scatter_start_" + group[0], True)
        grad_handles.append((group, handle))
        return sent

    loss, grad_x, dmods, dcmod, grads = _local_step(
        xcat, poscat, loss_target[0], mods, cmod, wts, late_weights, send_grads,
        lambda partial: lax.psum(partial, ("x", "y", "c")), start_after=order)

    def as2d(shape):
        rows = 1
        for s in shape[:-1]:
            rows *= s
        return (rows, shape[-1])

    def whole(arr, shape):
        arr = arr.reshape((-1,) + as2d(shape))
        return (arr, arr.shape[0])

    shard_shapes = {n: weights[n].shape for n in names}
    g_out, d_out, m_out, v_out = {}, {}, {}, {}

    def adamw(n, pieces, after):
        shape = shard_shapes[n]
        r2, c2 = as2d(shape)
        g, dl, nm, nv = _adamw(pieces, weights[n].reshape(r2, c2), m_in[n].reshape(r2, c2), v_in[n].reshape(r2, c2),
                               "adamw_" + n, after=after)
        g_out[n], d_out[n], m_out[n], v_out[n] = (t.reshape(shape) for t in (g, dl, nm, nv))
        return g

    small_sharded = ["norm_g", "rec_conv_w", "rec_lambda", "conf_b_pw1", "conf_conv_w", "conf_conv_b", "conf_ln_g",
                     "conf_ln_b", "conf_b_pw2"]
    pack = jnp.concatenate([_shard_cols(grads[n]).reshape(N_DEV, -1) for n in small_sharded], axis=1)
    pack = jnp.pad(pack, ((0, 0), (0, SMALL_PACK_ROWS * 128 - pack.shape[1]))).reshape(N_DEV, SMALL_PACK_ROWS, 128)
    small_handle, token = _exchange_start(
        [pack], [_own_block_filled(lax.dynamic_index_in_dim(pack, me, 0, keepdims=False), me)], "scatter_small_start",
        True, after=[grad_x])
    dm_flat = jnp.concatenate([dmods.reshape(-1), dcmod.reshape(-1), grads["rec_conv_b"].reshape(-1)])
    dm_len = dm_flat.shape[0]
    dm_flat = jnp.pad(dm_flat, (0, 128 * 128 - dm_len)).reshape(128, 128)
    dm_handle, token = _exchange_start([dm_flat], [_own_block_filled(dm_flat, me)], "gather_dmods_start", False,
                                       after=[token])

    done = token
    for group, handle in grad_handles:
        if group == ["replicated"]:
            repl_all = _exchange_wait(handle, done, "gather_replicated_wait", False)[0]
            row = 0
            for n in repl_names:
                n_rows = -(-weights[n].size // BLK)
                if as2d(shard_shapes[n]) == (n_rows, BLK) and row % 256 == 0:
                    done = adamw(n, [(repl_all, N_DEV, row)], [done])
                else:
                    got = repl_all[:, row:row + n_rows].reshape(N_DEV, -1)[:, :weights[n].size]
                    done = adamw(n, [whole(got, shard_shapes[n])], [done])
                row += n_rows
            continue
        for n, got in zip(group, _exchange_wait(handle, done, "scatter_wait_" + group[0], True)):
            done = adamw(n, [(got, N_DEV)], [done])

    dm_all = _exchange_wait(dm_handle, done, "gather_dmods_wait", False)[0].reshape(N_DEV, -1)
    dmods_all = dm_all[:, :2 * N_MOD * D].reshape(N_DEV, 2, N_MOD * D)
    dcmod_all = jnp.pad(dm_all[:, 2 * N_MOD * D:2 * N_MOD * D + 2 * D], ((0, 0), (0, (N_MOD - 2) * D)))
    g16_full = jnp.stack([jnp.concatenate([dmods_all[:, 0], dcmod_all], axis=0),
                          jnp.concatenate([dmods_all[:, 1], jnp.zeros_like(dcmod_all)], axis=0)])
    g16 = lax.dynamic_slice_in_dim(g16_full, me * ADA_SHARD, ADA_SHARD, axis=2)
    dw_ada, ds_part = _ada_backward(c16, g16, w_ada)
    ds_handle, token = _exchange_start([ds_part[0]], [_own_block_filled(ds_part[0], me)], "gather_dsilu_start", False)
    done = adamw("w_ada", [whole(dw_ada, shard_shapes["w_ada"])], [token])
    done = adamw("rec_conv_b", [whole(dm_all[:, dm_len - R:dm_len], shard_shapes["rec_conv_b"])], [done])
    db_terms = jnp.concatenate([dmods_all, jnp.stack([dcmod_all, jnp.zeros_like(dcmod_all)], axis=1)], axis=0)
    done = adamw("b_ada", [whole(db_terms, shard_shapes["b_ada"])], [done])
    pack_recv = _exchange_wait(small_handle, done, "scatter_small_wait", True)[0].reshape(N_DEV, -1)
    off = 0
    for n in small_sharded:
        size = weights[n].size
        done = adamw(n, [whole(pack_recv[:, off:off + size], shard_shapes[n])], [done])
        off += size
    ds_all = _exchange_wait(ds_handle, done, "gather_dsilu_wait", False)[0]
    adamw("c_ctx", [whole(ds_all[:, 0], shard_shapes["c_ctx"])], [])

    return (loss, grad_x[None], *[g_out[n] for n in names], *[d_out[n] for n in names],
            *[m_out[n] for n in names], *[v_out[n] for n in names])
```

```python
import functools

import jax
import jax.numpy as jnp
from jax import lax
from jax.experimental import pallas as pl
from jax.experimental.pallas import tpu as pltpu

f32 = jnp.float32
bf16 = jnp.bfloat16

N_DEV = 8
D = 1024
T_LAT = 2048
T_CTX = 256
T_ALL = T_CTX + T_LAT
R = 1280
N_BLK = 16
BLK = R // N_BLK
F = 4096
GRID_W = 64
RG_C = 8.0
EPS = 1e-6
POS_BASE = 10000.0
N_MOD = 6
ADA_SHARD = N_MOD * D // N_DEV

ADAM_LR = 0.001
ADAM_B1 = 0.9
ADAM_B2 = 0.999
ADAM_EPS = 1e-08
ADAM_WD = 0.01
ADAM_STEP = 10

VMEM_LIMIT_V7X = 56 * 1024 * 1024
HALO = 16
MESH = pl.DeviceIdType.MESH


def _cparams(*sem):
    return pltpu.CompilerParams(dimension_semantics=sem, vmem_limit_bytes=VMEM_LIMIT_V7X)


def _pick(n, cands):
    for c in cands:
        if n % c == 0:
            return c
    raise ValueError(f"no block size for {n}")


def _position():
    x, y, c = lax.axis_index("x"), lax.axis_index("y"), lax.axis_index("c")
    return x, y, c, 4 * x + 2 * y + c


def _peer(x, y, c, k):
    px = (1 - x) if (k >> 2) & 1 else x
    py = (1 - y) if (k >> 1) & 1 else y
    pc = (1 - c) if k & 1 else c
    return (px, py, pc), 4 * px + 2 * py + pc


def _exchange(arrs, name, scatter):
    n = len(arrs)

    def body(*refs):
        ins, outs = refs[:n], refs[n:2 * n]
        send_sems, recv_sems, local_sems = refs[2 * n:]
        x, y, c, me = _position()
        local = []
        for a in range(n):
            src = ins[a].at[me] if scatter else ins[a]
            cp = pltpu.make_async_copy(src, outs[a].at[me], local_sems.at[a])
            cp.start()
            local.append(cp)
        sends, recvs = [], []
        for a in range(n):
            for k in range(1, N_DEV):
                peer, peer_lin = _peer(x, y, c, k)
                src = ins[a].at[peer_lin] if scatter else ins[a]
                cp = pltpu.make_async_remote_copy(
                    src_ref=src, dst_ref=outs[a].at[me], send_sem=send_sems.at[a, k - 1],
                    recv_sem=recv_sems.at[a, k - 1], device_id=peer, device_id_type=MESH)
                cp.start()
                sends.append(cp)
                recvs.append(pltpu.make_async_remote_copy(
                    src_ref=src, dst_ref=outs[a].at[peer_lin], send_sem=send_sems.at[a, k - 1],
                    recv_sem=recv_sems.at[a, k - 1], device_id=peer, device_id_type=MESH))
        for cp in recvs:
            cp.wait_recv()
        for cp in sends:
            cp.wait_send()
        for cp in local:
            cp.wait()

    if scatter:
        out_shape = [jax.ShapeDtypeStruct(a.shape, a.dtype) for a in arrs]
    else:
        out_shape = [jax.ShapeDtypeStruct((N_DEV,) + a.shape, a.dtype) for a in arrs]
    any_spec = pl.BlockSpec(memory_space=pl.ANY)
    return pl.pallas_call(
        body, name=name, out_shape=out_shape,
        in_specs=[any_spec] * n, out_specs=[any_spec] * n,
        scratch_shapes=[pltpu.SemaphoreType.DMA((n, N_DEV - 1)), pltpu.SemaphoreType.DMA((n, N_DEV - 1)),
                        pltpu.SemaphoreType.DMA((n,))],
    )(*arrs)


def _all_gather(arrs, name):
    return _exchange(arrs, name, scatter=False)


def _lin(p):
    return 4 * p[0] + 2 * p[1] + p[2]


HBM_SPEC = pl.BlockSpec(memory_space=pltpu.HBM)
SEM_SPEC = pl.BlockSpec(memory_space=pltpu.SEMAPHORE)
DATAFLOW_EFFECT = pltpu.SideEffectType.DATAFLOW_SIDE_EFFECTING


def _split_copies(srcs, lands, send_sems, recv_sems, scatter):
    x, y, c, me = _position()
    out = []
    for a in range(len(srcs)):
        for k in range(1, N_DEV):
            peer, peer_lin = _peer(x, y, c, k)
            src = srcs[a].at[peer_lin] if scatter else srcs[a]
            mk = lambda slot: pltpu.make_async_remote_copy(
                src_ref=src, dst_ref=lands[a].at[slot], send_sem=send_sems.at[a * (N_DEV - 1) + k - 1],
                recv_sem=recv_sems.at[a * (N_DEV - 1) + k - 1], device_id=peer, device_id_type=MESH)
            out.append((mk(me), mk(peer_lin)))
    return out


def _exchange_start(srcs, lands, name, scatter, after=()):
    n = len(srcs)
    n_after = len(after)

    def body(*refs):
        srcs_r, lands_r = refs[:n], refs[n:2 * n]
        send_sems, recv_sems = refs[2 * n + n_after], refs[2 * n + n_after + 1]
        token = refs[-1]
        for outgoing, _ in _split_copies(srcs_r, lands_r, send_sems, recv_sems, scatter):
            outgoing.start()
        token[...] = jnp.zeros_like(token)

    hbm = lambda a: pltpu.HBM(a.shape, a.dtype)
    res = pl.pallas_call(
        body, name=name,
        out_shape=(pltpu.SemaphoreType.DMA((n * (N_DEV - 1),)), pltpu.SemaphoreType.DMA((n * (N_DEV - 1),)),
                   *[hbm(a) for a in srcs], *[hbm(a) for a in lands], jax.ShapeDtypeStruct((8, 128), f32)),
        in_specs=[HBM_SPEC] * (2 * n) + [pl.BlockSpec(memory_space=pl.ANY)] * n_after,
        out_specs=(SEM_SPEC, SEM_SPEC, *[HBM_SPEC] * (2 * n), pl.BlockSpec(memory_space=pltpu.VMEM)),
        input_output_aliases={i: 2 + i for i in range(2 * n)},
        compiler_params=pltpu.CompilerParams(has_side_effects=DATAFLOW_EFFECT),
    )(*[pltpu.with_memory_space_constraint(a, pltpu.HBM) for a in list(srcs) + list(lands)], *after)
    return (res[0], res[1], list(res[2:2 + n]), list(res[2 + n:2 + 2 * n])), res[-1]


def _exchange_wait(handle, after, name, scatter):
    send_sems, recv_sems, srcs, lands = handle
    n = len(srcs)
    after = list(after) if isinstance(after, (list, tuple)) else [after]

    def body(*refs):
        srcs_r, lands_r = refs[:n], refs[n:2 * n]
        send_s, recv_s = refs[2 * n], refs[2 * n + 1]
        for outgoing, incoming in _split_copies(srcs_r, lands_r, send_s, recv_s, scatter):
            outgoing.wait_send()
            incoming.wait_recv()

    hbm = lambda a: pltpu.HBM(a.shape, a.dtype)
    res = pl.pallas_call(
        body, name=name, out_shape=tuple(hbm(a) for a in list(srcs) + list(lands)),
        in_specs=[HBM_SPEC] * (2 * n) + [SEM_SPEC, SEM_SPEC] + [pl.BlockSpec(memory_space=pl.ANY)] * len(after),
        out_specs=tuple([HBM_SPEC] * (2 * n)),
        input_output_aliases={i: i for i in range(2 * n)},
        compiler_params=pltpu.CompilerParams(has_side_effects=DATAFLOW_EFFECT),
    )(*srcs, *lands, send_sems, recv_sems, *after)
    return list(res[n:])


def _split_call(body, name, hbm_ins, kept, in_sems, n_new_sems, after, with_token):
    n_in, n_sem = len(hbm_ins), len(in_sems)
    out_shape, out_specs = [], []
    if n_new_sems:
        out_shape += [pltpu.SemaphoreType.DMA((n_new_sems,))] * 2
        out_specs += [SEM_SPEC] * 2
    first_kept = len(out_shape)
    out_shape += [pltpu.HBM(hbm_ins[i].shape, hbm_ins[i].dtype) for i in kept]
    out_specs += [HBM_SPEC] * len(kept)
    if with_token:
        out_shape.append(jax.ShapeDtypeStruct((8, 128), f32))
        out_specs.append(pl.BlockSpec(memory_space=pltpu.VMEM))

    def wrapped(*refs):
        outs = refs[n_in + n_sem + len(after):]
        body(refs[:n_in], refs[n_in:n_in + n_sem], outs[:2] if n_new_sems else ())
        if with_token:
            outs[-1][...] = jnp.zeros_like(outs[-1])

    return pl.pallas_call(
        wrapped, name=name, out_shape=tuple(out_shape),
        in_specs=[HBM_SPEC] * n_in + [SEM_SPEC] * n_sem + [pl.BlockSpec(memory_space=pl.ANY)] * len(after),
        out_specs=tuple(out_specs), input_output_aliases={i: first_kept + j for j, i in enumerate(kept)},
        compiler_params=pltpu.CompilerParams(has_side_effects=DATAFLOW_EFFECT),
    )(*[pltpu.with_memory_space_constraint(a, pltpu.HBM) for a in hbm_ins], *in_sems, *after)


def _rcopy(src, dst, sems, k, to):
    return pltpu.make_async_remote_copy(src_ref=src, dst_ref=dst, send_sem=sems[0].at[k], recv_sem=sems[1].at[k],
                                        device_id=to, device_id_type=MESH)


def _gather2_start(shards, lands, name, after):
    n = len(shards)

    def body(ins, sems_in, sems_out):
        x, y, c, me = _position()
        for a in range(n):
            for k, to in enumerate(((x, y, 1 - c), (1 - x, y, c), (x, 1 - y, c))):
                _rcopy(ins[a], ins[n + a].at[me], sems_out, 3 * a + k, to).start()

    res = _split_call(body, name, list(shards) + list(lands), range(2 * n), (), 3 * n, after, True)
    return (res[0], res[1], list(res[2:2 + n]), list(res[2 + n:2 + 2 * n])), res[-1]


def _gather2_forward1(handle, after, name):
    send_sems, recv_sems, srcs, lands = handle
    n = len(srcs)

    def body(ins, sems_in, sems_out):
        x, y, c, me = _position()
        sib, xn, yn = (x, y, 1 - c), (1 - x, y, c), (x, 1 - y, c)
        for a in range(n):
            for k, peer in enumerate((sib, xn, yn)):
                _rcopy(ins[a], ins[n + a].at[me], sems_in, 3 * a + k, peer).wait_send()
                _rcopy(ins[a], ins[n + a].at[_lin(peer)], sems_in, 3 * a + k, peer).wait_recv()
        for a in range(n):
            land = ins[n + a]
            _rcopy(land.at[_lin(xn)], land.at[_lin(xn)], sems_out, 3 * a, sib).start()
            _rcopy(land.at[_lin(yn)], land.at[_lin(yn)], sems_out, 3 * a + 1, sib).start()

            @pl.when(c == 0)
            def _():
                _rcopy(land.at[_lin(xn)], land.at[_lin(xn)], sems_out, 3 * a + 2, yn).start()

            @pl.when(c == 1)
            def _():
                _rcopy(land.at[_lin(yn)], land.at[_lin(yn)], sems_out, 3 * a + 2, xn).start()

    res = _split_call(body, name, list(srcs) + list(lands), range(n, 2 * n), (send_sems, recv_sems), 3 * n, [after], False)
    return (res[0], res[1], list(res[2:]))


def _gather2_forward2(handle, after, name):
    send_sems, recv_sems, lands = handle
    n = len(lands)

    def body(ins, sems_in, sems_out):
        x, y, c, me = _position()
        sib, dg = (x, y, 1 - c), _lin((1 - x, 1 - y, c))
        for a in range(n):
            for k, slot in enumerate((_lin((1 - x, y, 1 - c)), _lin((x, 1 - y, 1 - c)), dg)):
                done = _rcopy(ins[a].at[slot], ins[a].at[slot], sems_in, 3 * a + k, sib)
                done.wait_send()
                done.wait_recv()
        for a in range(n):
            _rcopy(ins[a].at[dg], ins[a].at[dg], sems_out, a, sib).start()

    res = _split_call(body, name, list(lands), range(n), (send_sems, recv_sems), n, [after], False)
    return (res[0], res[1], list(res[2:]))


def _gather2_wait(handle, after, name):
    send_sems, recv_sems, lands = handle
    n = len(lands)

    def body(ins, sems_in, sems_out):
        x, y, c, me = _position()
        slot = _lin((1 - x, 1 - y, 1 - c))
        for a in range(n):
            done = _rcopy(ins[a].at[slot], ins[a].at[slot], sems_in, a, (x, y, 1 - c))
            done.wait_send()
            done.wait_recv()

    return list(_split_call(body, name, list(lands), range(n), (send_sems, recv_sems), 0, [after], False))


def _own_block_filled(block, me):
    land = lax.empty((N_DEV,) + block.shape, block.dtype)
    return lax.dynamic_update_index_in_dim(land, block, me, 0)


ANY_SPEC = pl.BlockSpec(memory_space=pl.ANY)


def _mm(a, b, name, ta=False, tb=False, out_dtype=f32, after=()):
    if ta:
        k_dim, m_dim = a.shape
    else:
        m_dim, k_dim = a.shape
    if tb:
        n_dim, k2 = b.shape
    else:
        k2, n_dim = b.shape
    assert k_dim == k2, (a.shape, b.shape)
    assert a.dtype == bf16 and b.dtype == bf16
    bm = _pick(m_dim, (512, 768, 640, 256, 128))
    bn = _pick(n_dim, (512, 640, 256, 128))
    bk = k_dim if k_dim <= 2560 else _pick(k_dim, (1024, 1280, 768, 512))
    nk = k_dim // bk
    a_spec = (pl.BlockSpec((bk, bm), lambda i, j, k: (k, i)) if ta
              else pl.BlockSpec((bm, bk), lambda i, j, k: (i, k)))
    b_spec = (pl.BlockSpec((bn, bk), lambda i, j, k: (j, k)) if tb
              else pl.BlockSpec((bk, bn), lambda i, j, k: (k, j)))
    dims = (((0 if ta else 1,), (1 if tb else 0,)), ((), ()))

    n_after = len(after)

    def body_single(a_ref, b_ref, *rest):
        o_ref = rest[n_after]
        o_ref[...] = lax.dot_general(a_ref[...], b_ref[...], dims, preferred_element_type=f32).astype(o_ref.dtype)

    def body(a_ref, b_ref, *rest):
        o_ref, acc_ref = rest[n_after:]
        k = pl.program_id(2)

        @pl.when(k == 0)
        def _():
            acc_ref[...] = jnp.zeros_like(acc_ref)

        acc_ref[...] += lax.dot_general(a_ref[...], b_ref[...], dims, preferred_element_type=f32)

        @pl.when(k == nk - 1)
        def _():
            o_ref[...] = acc_ref[...].astype(o_ref.dtype)

    return pl.pallas_call(
        body_single if nk == 1 else body, name=name, out_shape=jax.ShapeDtypeStruct((m_dim, n_dim), out_dtype),
        grid=(m_dim // bm, n_dim // bn, nk), in_specs=[a_spec, b_spec] + [ANY_SPEC] * n_after,
        out_specs=pl.BlockSpec((bm, bn), lambda i, j, k: (i, j)),
        scratch_shapes=[] if nk == 1 else [pltpu.VMEM((bm, bn), f32)],
        compiler_params=_cparams("parallel", "parallel", "arbitrary"),
    )(a, b, *after)


def _rin(arr, width=None, cb=0, roff=0):
    return (arr, arr.shape[1] if width is None else width, cb, roff)


def _rowcall(fn, name, rows, tm, row_ins, par_ins, row_outs, acc_outs=(), after=()):
    nr, npar, nro, n_after = len(row_ins), len(par_ins), len(row_outs), len(after)
    in_specs, args = [], []
    for arr, width, cb, roff in row_ins:
        if roff >= 0:
            imap = lambda i, cb=cb, roff=roff: (i + roff, cb)
        else:
            imap = lambda i, cb=cb, roff=roff: (jnp.maximum(i + roff, 0), cb)
        in_specs.append(pl.BlockSpec((tm, width), imap))
        args.append(arr)
    for p in par_ins:
        in_specs.append(pl.BlockSpec(p.shape, lambda i: (0, 0)))
        args.append(p)
    out_shape, out_specs = [], []
    for width, dt in row_outs:
        out_shape.append(jax.ShapeDtypeStruct((rows, width), dt))
        out_specs.append(pl.BlockSpec((tm, width), lambda i: (i, 0)))
    for p, width in acc_outs:
        out_shape.append(jax.ShapeDtypeStruct((p, width), f32))
        out_specs.append(pl.BlockSpec((p, width), lambda i: (0, 0)))

    def body(*refs):
        i = pl.program_id(0)
        res = fn(i, *[r[...] for r in refs[:nr + npar]])
        outs = refs[nr + npar + n_after:]
        for o, v in zip(outs[:nro], res[:nro]):
            o[...] = v.astype(o.dtype)
        if acc_outs:
            @pl.when(i == 0)
            def _():
                for o in outs[nro:]:
                    o[...] = jnp.zeros_like(o)

            for o, v in zip(outs[nro:], res[nro:]):
                o[...] += v

    return pl.pallas_call(
        body, name=name, out_shape=out_shape, grid=(rows // tm,), in_specs=in_specs + [ANY_SPEC] * n_after,
        out_specs=out_specs, compiler_params=_cparams("arbitrary"),
    )(*args, *after)


def _rms(x, g):
    return x * lax.rsqrt(jnp.mean(x * x, axis=-1, keepdims=True) + EPS) * g


def _normmod(x, g, sc, sh):
    return _rms(x, g) * (1.0 + sc) + sh


def _gelu(x):
    return 0.5 * x * (1.0 + jnp.tanh(0.7978845608028654 * (x + 0.044715 * (x * x * x))))


def _sigmoid(x):
    return 0.5 * (jnp.tanh(0.5 * x) + 1.0)


def _coeff_parts(pre_a, pre_x, ba, bx, lam):
    r = _sigmoid(pre_a + ba)
    ig = _sigmoid(pre_x + bx)
    nl = -lam
    sp = jnp.maximum(nl, 0.0) + jnp.log(1.0 + jnp.exp(-jnp.abs(nl)))
    la = -RG_C * r * sp
    a = jnp.exp(la)
    one_minus_a2 = -jnp.tanh(la) * (a * a + 1.0)
    inv_m = lax.rsqrt(one_minus_a2)
    return r, ig, sp, a, one_minus_a2 * inv_m, inv_m


def _coeff(pre_a, pre_x, u, ba, bx, lam):
    _, ig, _, a, m, _ = _coeff_parts(pre_a, pre_x, ba, bx, lam)
    return a, m * (ig * u)


def _coeff_bwd(pre_a, pre_x, u, ba, bx, lam, da, db):
    r, ig, sp, a, m, inv_m = _coeff_parts(pre_a, pre_x, ba, bx, lam)
    dbu = db * u
    dig = dbu * m
    dm = dbu * ig
    dla = a * (da - dm * a * inv_m)
    dpa = dla * (-RG_C * sp) * (r * (1.0 - r))
    dpx = dig * (ig * (1.0 - ig))
    dsp = jnp.sum(dla * (-RG_C * r), axis=0, keepdims=True)
    dlam = -dsp * _sigmoid(-lam)
    return (dpa, dpx, db * m * ig, jnp.sum(dpa, axis=0, keepdims=True), jnp.sum(dpx, axis=0, keepdims=True), dlam)


SCAN_CHUNK = 256


def _scan_call(a, v, chunk_of, reverse, name, backward, after=()):
    rows, width = a.shape
    n_out = 1 if backward else 2
    nt = SCAN_CHUNK // 8

    def body(a_ref, v_ref, *rest):
        outs, state_ref = rest[len(after):-1], rest[-1]

        @pl.when(pl.program_id(0) == 0)
        def _():
            state_ref[...] = jnp.zeros_like(state_ref)

        rid = lax.broadcasted_iota(jnp.int32, (8, width), 0)
        last_row = 0 if reverse else 7

        def shift(x, s, fill):
            rolled = pltpu.roll(x, (8 - s) if reverse else s, axis=0)
            return jnp.where((rid >= 8 - s) if reverse else (rid < s), fill, rolled)

        def tile(j, st):
            t0 = pl.multiple_of((nt - 1 - j if reverse else j) * 8, 8)
            at = a_ref[pl.ds(t0, 8), :]
            coef = shift(at, 1, 1.0) if backward else at
            acc = v_ref[pl.ds(t0, 8), :]
            for s in (1, 2, 4):
                acc = coef * shift(acc, s, 0.0) + acc
                coef = coef * shift(coef, s, 1.0)
            out = coef * st + acc
            outs[0][pl.ds(t0, 8), :] = out
            last = out[last_row:last_row + 1]
            if backward:
                return at[last_row:last_row + 1] * last
            outs[1][pl.ds(t0, 8), :] = shift(out, 1, st)
            return last

        state_ref[0:1, :] = lax.fori_loop(0, nt, tile, state_ref[0:1, :])

    spec = pl.BlockSpec((SCAN_CHUNK, width), lambda t: (chunk_of(t), 0))
    return pl.pallas_call(
        body, name=name, out_shape=[jax.ShapeDtypeStruct((rows, width), f32)] * n_out,
        grid=(rows // SCAN_CHUNK,), in_specs=[spec, spec] + [ANY_SPEC] * len(after), out_specs=[spec] * n_out,
        scratch_shapes=[pltpu.VMEM((8, width), f32)],
        compiler_params=_cparams("arbitrary"),
    )(a, v, *after)


CONV_CHUNK = 256


def _fill_padded(pad_ref, src_ref, start, n):
    cb = pad_ref.shape[1]
    pad_ref[pl.ds(0, HALO), :] = jnp.zeros((HALO, cb), f32)
    pad_ref[pl.ds(HALO, n), :] = src_ref[pl.ds(start, n), :].astype(f32)
    pad_ref[pl.ds(HALO + n, HALO), :] = jnp.zeros((HALO, cb), f32)


def _dwconv_fwd(x, x_cb0, w, b, taps, pad_left, segments, cb, name, emit_bf16):
    rows = x.shape[0]
    width = w.shape[1]

    def body(x_ref, w_ref, b_ref, *rest):
        outs, xp = rest[:-1], rest[-1]
        for start, n in segments:
            _fill_padded(xp, x_ref, start, n)
            for c0 in range(0, n, CONV_CHUNK):
                acc = jnp.zeros((CONV_CHUNK, cb), f32) + b_ref[...]
                for k in range(taps):
                    acc = acc + w_ref[k:k + 1, :] * xp[pl.ds(HALO + c0 + k - pad_left, CONV_CHUNK), :]
                for o in outs:
                    o[pl.ds(start + c0, CONV_CHUNK), :] = acc.astype(o.dtype)

    out_dtypes = [f32, bf16] if emit_bf16 else [f32]
    return pl.pallas_call(
        body, name=name, out_shape=[jax.ShapeDtypeStruct((rows, width), dt) for dt in out_dtypes],
        grid=(width // cb,),
        in_specs=[pl.BlockSpec((rows, cb), lambda j: (0, j + x_cb0)), pl.BlockSpec((taps, cb), lambda j: (0, j)),
                  pl.BlockSpec((1, cb), lambda j: (0, j))],
        out_specs=[pl.BlockSpec((rows, cb), lambda j: (0, j))] * len(out_dtypes),
        scratch_shapes=[pltpu.VMEM((rows + 2 * HALO, cb), f32)],
        compiler_params=_cparams("parallel"),
    )(x, w, b)


def _dwconv_bwd(douts, x, x_cb0, w, taps, pad_left, segments, cb, name, dx_dtype):
    rows = x.shape[0]
    width = w.shape[1]
    nd = len(douts)

    def body(*refs):
        d_refs, x_ref, w_ref = refs[:nd], refs[nd], refs[nd + 1]
        dx_ref, dw_ref, db_ref, dp, dsum = refs[nd + 2:]
        dw_ref[...] = jnp.zeros_like(dw_ref)
        db_ref[...] = jnp.zeros_like(db_ref)
        if nd > 1:
            total = d_refs[0][...]
            for r in d_refs[1:]:
                total = total + r[...]
            dsum[...] = total
            d_ref = dsum
        else:
            d_ref = d_refs[0]
        for start, n in segments:
            _fill_padded(dp, d_ref, start, n)
            for c0 in range(0, n, CONV_CHUNK):
                db_ref[...] += jnp.sum(dp[pl.ds(HALO + c0, CONV_CHUNK), :], axis=0, keepdims=True)
                xchunk = x_ref[pl.ds(start + c0, CONV_CHUNK), :].astype(f32)
                acc = jnp.zeros((CONV_CHUNK, cb), f32)
                for k in range(taps):
                    shifted = dp[pl.ds(HALO + c0 + pad_left - k, CONV_CHUNK), :]
                    acc = acc + w_ref[k:k + 1, :] * shifted
                    dw_ref[k:k + 1, :] += jnp.sum(shifted * xchunk, axis=0, keepdims=True)
                dx_ref[pl.ds(start + c0, CONV_CHUNK), :] = acc.astype(dx_ref.dtype)

    dspec = pl.BlockSpec((rows, cb), lambda j: (0, j))
    return pl.pallas_call(
        body, name=name,
        out_shape=[jax.ShapeDtypeStruct((rows, width), dx_dtype), jax.ShapeDtypeStruct((taps, width), f32),
                   jax.ShapeDtypeStruct((1, width), f32)],
        grid=(width // cb,),
        in_specs=[dspec] * nd + [pl.BlockSpec((rows, cb), lambda j: (0, j + x_cb0)),
                                 pl.BlockSpec((taps, cb), lambda j: (0, j))],
        out_specs=[dspec, pl.BlockSpec((taps, cb), lambda j: (0, j)), pl.BlockSpec((1, cb), lambda j: (0, j))],
        scratch_shapes=[pltpu.VMEM((rows + 2 * HALO, cb), f32), pltpu.VMEM((rows, cb), f32)],
        compiler_params=_cparams("parallel"),
    )(*douts, x, w)


def _ada_forward(c16, w_ada, b_loc):
    def body(c_ref, w_ref, b_ref, o_ref):
        cv = c_ref[...]
        s = (cv * _sigmoid(cv)).astype(bf16)
        o_ref[0] = jnp.dot(s, w_ref[0].astype(bf16), preferred_element_type=f32) + b_ref[0]

    return pl.pallas_call(
        body, name="ada_forward", out_shape=jax.ShapeDtypeStruct((2, 16, ADA_SHARD), f32), grid=(2,),
        in_specs=[pl.BlockSpec((16, D), lambda l: (0, 0)), pl.BlockSpec((1, D, ADA_SHARD), lambda l: (l, 0, 0)),
                  pl.BlockSpec((1, 1, ADA_SHARD), lambda l: (l, 0, 0))],
        out_specs=pl.BlockSpec((1, 16, ADA_SHARD), lambda l: (l, 0, 0)),
        compiler_params=_cparams("parallel"),
    )(c16, w_ada, b_loc)


def _ada_backward(c16, g16, w_ada):
    def body(c_ref, g_ref, w_ref, dw_ref, ds_ref):
        cv = c_ref[...]
        s = (cv * _sigmoid(cv)).astype(bf16)
        g = g_ref[0].astype(bf16)
        dw_ref[0] = lax.dot_general(s, g, (((0,), (0,)), ((), ())), preferred_element_type=f32)
        ds = lax.dot_general(g, w_ref[0].astype(bf16), (((1,), (1,)), ((), ())), preferred_element_type=f32)
        cc = cv[8:9]
        sg = _sigmoid(cc)
        dsilu = sg * (1.0 + cc * (1.0 - sg))
        ds_ref[0] = jnp.zeros((8, D), f32) + jnp.sum(ds[8:16], axis=0, keepdims=True) * dsilu

    return pl.pallas_call(
        body, name="ada_backward",
        out_shape=[jax.ShapeDtypeStruct((2, D, ADA_SHARD), f32), jax.ShapeDtypeStruct((2, 8, D), f32)], grid=(2,),
        in_specs=[pl.BlockSpec((16, D), lambda l: (0, 0)), pl.BlockSpec((1, 16, ADA_SHARD), lambda l: (l, 0, 0)),
                  pl.BlockSpec((1, D, ADA_SHARD), lambda l: (l, 0, 0))],
        out_specs=[pl.BlockSpec((1, D, ADA_SHARD), lambda l: (l, 0, 0)), pl.BlockSpec((1, 8, D), lambda l: (l, 0, 0))],
        compiler_params=_cparams("parallel"),
    )(c16, g16, w_ada)


def _adamw(pieces, w, m, v, name, after=()):
    rows, cols = w.shape
    n_arr, n_after = len(pieces), len(after)
    tm = 256 if (rows % 256 == 0 and rows > 256) else rows
    counts = [p[1] for p in pieces]
    first_tiles = [(p[2] if len(p) > 2 else 0) // tm for p in pieces]
    pieces = [p[0] for p in pieces]

    def body(*refs):
        p_refs = refs[:n_arr]
        w_ref, m_ref, v_ref = refs[n_arr:n_arr + 3]
        g_ref, d_ref, nm_ref, nv_ref = refs[n_arr + 3 + n_after:]
        g = None
        for p_ref in p_refs:
            for j in range(p_ref.shape[0]):
                term = p_ref[j].astype(f32)
                g = term if g is None else g + term
        m2 = ADAM_B1 * m_ref[...] + (1.0 - ADAM_B1) * g
        v2 = ADAM_B2 * v_ref[...] + (1.0 - ADAM_B2) * (g * g)
        m_hat = m2 / (1.0 - ADAM_B1 ** ADAM_STEP)
        v_hat = v2 / (1.0 - ADAM_B2 ** ADAM_STEP)
        g_ref[...] = g
        d_ref[...] = -ADAM_LR * (m_hat / (jnp.sqrt(v_hat) + ADAM_EPS) + ADAM_WD * w_ref[...])
        nm_ref[...] = m2
        nv_ref[...] = v2

    spec = pl.BlockSpec((tm, cols), lambda i: (i, 0))
    return pl.pallas_call(
        body, name=name, out_shape=[jax.ShapeDtypeStruct((rows, cols), f32)] * 4, grid=(rows // tm,),
        in_specs=[pl.BlockSpec((cnt, tm, cols), lambda i, t=t: (0, i + t, 0)) for cnt, t in zip(counts, first_tiles)]
        + [spec, spec, spec]
        + [ANY_SPEC] * n_after,
        out_specs=[spec] * 4, compiler_params=_cparams("parallel"),
    )(*pieces, w, m, v, *after)


MLP_TM = 256
FB = F // N_DEV


def _stack_rows(vals, n):
    cols = vals[0].shape[1]
    rid = lax.broadcasted_iota(jnp.int32, (n, cols), 0)
    out = jnp.zeros((n, cols), f32)
    for k, v in enumerate(vals):
        out = jnp.where(rid == k, v, out)
    return out


N_MLP_PARAMS = 9


class _ParamRows:
    def __init__(self, ref):
        self.ref = ref

    def __getitem__(self, sl):
        return self.ref[8 * sl.start:8 * sl.start + 1, :]


def _resident(shape, imap):
    return pl.BlockSpec(shape, imap, pipeline_mode=pl.Buffered(1))


def _mlp_forward(xa, xa_roff, out_prev, par, w_in, w_out, layer, name):
    def body(xa_ref, op_ref, par_ref, win_ref, wout_ref, x1_ref, h_ref, r_ref, mo_ref, x2_ref, hn_ref):
        p = _ParamRows(par_ref)
        x1 = xa_ref[...] + p[0:1] * (op_ref[...] + p[1:2])
        h = _normmod(x1, p[2:3], p[3:4], p[4:5]).astype(bf16)
        x1_ref[...] = x1
        h_ref[...] = h
        mo = jnp.zeros((MLP_TM, D), f32)
        for j in range(N_DEV):
            r = jnp.maximum(jnp.dot(h, win_ref[j], preferred_element_type=f32), 0.0)
            r_ref[:, j * FB:(j + 1) * FB] = r.astype(bf16)
            mo = mo + jnp.dot((r * r).astype(bf16), wout_ref[j], preferred_element_type=f32)
        mo_ref[...] = mo.astype(bf16)
        x2 = x1 + p[5:6] * mo
        x2_ref[...] = x2
        hn_ref[...] = _normmod(x2, p[6:7], p[7:8], p[8:9]).astype(bf16)

    row = lambda width: pl.BlockSpec((MLP_TM, width), lambda i: (i, 0))
    return pl.pallas_call(
        body, name=name, grid=(T_LAT // MLP_TM,),
        out_shape=[jax.ShapeDtypeStruct((T_LAT, D), f32), jax.ShapeDtypeStruct((T_LAT, D), bf16),
                   jax.ShapeDtypeStruct((T_LAT, F), bf16), jax.ShapeDtypeStruct((T_LAT, D), bf16),
                   jax.ShapeDtypeStruct((T_LAT, D), f32), jax.ShapeDtypeStruct((T_LAT, D), bf16)],
        in_specs=[pl.BlockSpec((MLP_TM, D), lambda i: (i + xa_roff, 0)), row(D), pl.BlockSpec((8 * N_MLP_PARAMS, D), lambda i: (0, 0)),
                  _resident((N_DEV, None, D, FB), lambda i: (0, layer, 0, 0)),
                  _resident((N_DEV, None, FB, D), lambda i: (0, layer, 0, 0))],
        out_specs=[row(D), row(D), row(F), row(D), row(D), row(D)],
        compiler_params=_cparams("parallel"),
    )(xa, out_prev, par, w_in, w_out)


def _mlp_backward(dx2, x1, r, mo, out_prev, par, w_in, w_out, layer, name, after=()):
    nt = (((1,), (1,)), ((), ()))

    n_after = len(after)

    def body(dx2_ref, x1_ref, r_ref, mo_ref, op_ref, par_ref, win_ref, wout_ref, *rest):
        dx1_ref, dop_ref, dmo_ref, dhid_ref, acc_ref = rest[n_after:]
        p = _ParamRows(par_ref)
        dx2v = dx2_ref[...]
        dmo = (p[5:6] * dx2v).astype(bf16)
        dmo_ref[...] = dmo
        dh = jnp.zeros((MLP_TM, D), f32)
        mo = mo_ref[...].astype(f32)
        for j in range(N_DEV):
            rf = r_ref[:, j * FB:(j + 1) * FB].astype(f32)
            dact = lax.dot_general(dmo, wout_ref[j], nt, preferred_element_type=f32)
            dhid = (dact * (2.0 * rf)).astype(bf16)
            dhid_ref[:, j * FB:(j + 1) * FB] = dhid
            dh = dh + lax.dot_general(dhid, win_ref[j], nt, preferred_element_type=f32)
        x1 = x1_ref[...]
        _, vjp = jax.vjp(_normmod, x1, p[2:3], p[3:4], p[4:5])
        dx, dng, dsc, dsh = vjp(dh)
        dx1 = dx2v + dx
        dx1_ref[...] = dx1
        dop_ref[...] = (p[0:1] * dx1).astype(bf16)
        sums = _stack_rows([jnp.sum(dx1 * (op_ref[...] + p[1:2]), axis=0, keepdims=True),
                            p[0:1] * jnp.sum(dx1, axis=0, keepdims=True), dng, dsc, dsh,
                            jnp.sum(dx2v * mo, axis=0, keepdims=True)], 8)

        @pl.when(pl.program_id(0) == 0)
        def _():
            acc_ref[...] = jnp.zeros_like(acc_ref)

        acc_ref[...] += sums

    row = lambda width: pl.BlockSpec((MLP_TM, width), lambda i: (i, 0))
    return pl.pallas_call(
        body, name=name, grid=(T_LAT // MLP_TM,),
        out_shape=[jax.ShapeDtypeStruct((T_LAT, D), f32), jax.ShapeDtypeStruct((T_LAT, D), bf16),
                   jax.ShapeDtypeStruct((T_LAT, D), bf16), jax.ShapeDtypeStruct((T_LAT, F), bf16),
                   jax.ShapeDtypeStruct((8, D), f32)],
        in_specs=[row(D), row(D), row(F), row(D), row(D), pl.BlockSpec((8 * N_MLP_PARAMS, D), lambda i: (0, 0)),
                  _resident((N_DEV, None, D, FB), lambda i: (0, layer, 0, 0)),
                  _resident((N_DEV, None, FB, D), lambda i: (0, layer, 0, 0))] + [ANY_SPEC] * n_after,
        out_specs=[row(D), row(D), row(D), row(F), pl.BlockSpec((8, D), lambda i: (0, 0))],
        compiler_params=_cparams("arbitrary"),
    )(dx2, x1, r, mo, out_prev, par, w_in, w_out, *after)


def _mlp_weight_grads(h, dhid, r, dmo, layer, other, tag):
    tn = (((0,), (0,)), ((), ()))

    def body_in(h_ref, dhid_ref, *rest):
        rest[-1][...] = lax.dot_general(h_ref[...], dhid_ref[...], tn, preferred_element_type=f32).astype(bf16)

    def body_out(r_ref, dmo_ref, *rest):
        rf = r_ref[...].astype(f32)
        rest[-1][...] = lax.dot_general((rf * rf).astype(bf16), dmo_ref[...], tn,
                                        preferred_element_type=f32).astype(bf16)

    def call(body, name, operands, specs, block, prev):
        extra = [] if prev is None else [prev]
        return pl.pallas_call(
            body, name=name, grid=(N_DEV,), out_shape=jax.ShapeDtypeStruct((N_DEV, 2) + block, bf16),
            in_specs=specs + [pl.BlockSpec(memory_space=pl.ANY)] * len(extra),
            out_specs=pl.BlockSpec((None, None) + block, lambda j: (j, layer, 0, 0)),
            input_output_aliases={} if prev is None else {2: 0},
            compiler_params=_cparams("parallel"),
        )(*operands, *extra)

    dw_in = call(body_in, tag + "_mlp_in_dw", [h, dhid],
                 [_resident((T_LAT, D), lambda j: (0, 0)), pl.BlockSpec((T_LAT, FB), lambda j: (0, j))], (D, FB),
                 None if other is None else other[0])
    dw_out = call(body_out, tag + "_mlp_out_dw", [r, dmo],
                  [pl.BlockSpec((T_LAT, FB), lambda j: (0, j)), _resident((T_LAT, D), lambda j: (0, 0))], (FB, D),
                  None if other is None else other[1])
    return dw_in, dw_out


def _pos_embed():
    n_rows = T_LAT // GRID_W
    q = D // 4
    omega = 1.0 / (POS_BASE ** (jnp.arange(q, dtype=f32) / q))
    er = jnp.arange(n_rows, dtype=jnp.int32).astype(f32)[:, None] * omega[None, :]
    ec = jnp.arange(GRID_W, dtype=jnp.int32).astype(f32)[:, None] * omega[None, :]
    by_row = jnp.concatenate([jnp.sin(er), jnp.cos(er)], axis=-1)[:, None, :]
    by_col = jnp.concatenate([jnp.sin(ec), jnp.cos(ec)], axis=-1)[None, :, :]
    full = jnp.concatenate([jnp.broadcast_to(by_row, (n_rows, GRID_W, D // 2)),
                            jnp.broadcast_to(by_col, (n_rows, GRID_W, D // 2))], axis=-1)
    return full.reshape(T_LAT, D)


HALF = R // 2
BLK_PER_HALF = N_BLK // 2
N_PARTS = 4


def _gate_matrix(w_a, w_x):
    eye = jnp.eye(BLK_PER_HALF, dtype=bf16)
    cols = []
    for h in range(2):
        for d in range(2):
            for w in (w_a, w_x):
                blocks = w[d, BLK_PER_HALF * h:BLK_PER_HALF * (h + 1)].astype(bf16)
                cols.append(jnp.einsum("hij,hg->higj", blocks, eye).reshape(HALF, HALF))
    return jnp.concatenate(cols, axis=1)


def _gate_blocks(dwg, part):
    out = []
    for h in range(2):
        blk = dwg[:, (N_PARTS * h + part) * HALF:(N_PARTS * h + part + 1) * HALF]
        blk = blk.reshape(BLK_PER_HALF, BLK, BLK_PER_HALF, BLK)
        out.append(jnp.moveaxis(jnp.diagonal(blk, axis1=0, axis2=2), -1, 0))
    return jnp.concatenate(out, axis=0)


GATE_BM = 768


def _gates_dx(dpre, wg, after=()):
    rows = dpre.shape[0]
    n_after = len(after)

    def body(d_ref, w_ref, *rest):
        rest[n_after][...] = lax.dot_general(d_ref[...], w_ref[...], (((1,), (1,)), ((), ())),
                                             preferred_element_type=f32)

    return pl.pallas_call(
        body, name="l0_gates_dx", grid=(rows // GATE_BM, 2), out_shape=jax.ShapeDtypeStruct((rows, R), f32),
        in_specs=[pl.BlockSpec((GATE_BM, N_PARTS * HALF), lambda i, h: (i, h)),
                  pl.BlockSpec((HALF, N_PARTS * HALF), lambda i, h: (0, h))] + [ANY_SPEC] * n_after,
        out_specs=pl.BlockSpec((GATE_BM, HALF), lambda i, h: (i, h)),
        compiler_params=_cparams("parallel", "parallel"),
    )(dpre, wg, *after)


COEFF_TM = 256


def _dir_params(d, *params):
    specs = [pl.BlockSpec((None, 1, HALF), lambda h, i: (d, 0, h))] * len(params)
    return specs, [p.reshape(2, 1, R) for p in params]


def _gates_coeff_fwd(ub, u, wg, ba, bx, lam, d):
    rows = u.shape[0]

    def body(ub_ref, u_ref, w_ref, ba_ref, bx_ref, lam_ref, a_ref, b_ref):
        pre = jnp.dot(ub_ref[...], w_ref[...], preferred_element_type=f32)
        a, b = _coeff(pre[:, :HALF], pre[:, HALF:], u_ref[...], ba_ref[...], bx_ref[...], lam_ref[...])
        a_ref[...] = a
        b_ref[...] = b

    tile = pl.BlockSpec((COEFF_TM, HALF), lambda h, i: (i, h))
    pspecs, pargs = _dir_params(d, ba, bx, lam)
    return pl.pallas_call(
        body, name=f"l0_gates_coeff_{d}", grid=(2, rows // COEFF_TM),
        out_shape=[jax.ShapeDtypeStruct((rows, R), f32)] * 2,
        in_specs=[tile, tile, pl.BlockSpec((HALF, 2 * HALF), lambda h, i: (0, 2 * h + d))] + pspecs,
        out_specs=[tile, tile], compiler_params=_cparams("parallel", "parallel"),
    )(ub, u, wg, *pargs)


def _gates_coeff_bwd(ub, u, dh, yp, wg, ba, bx, lam, d, dpre_prev):
    rows = u.shape[0]
    n_prev = 0 if dpre_prev is None else 1

    def body(ub_ref, u_ref, dh_ref, yp_ref, w_ref, ba_ref, bx_ref, lam_ref, *rest):
        dpre_ref, du_ref, dba_ref, dbx_ref, dlam_ref = rest[n_prev:]
        pre = jnp.dot(ub_ref[...], w_ref[...], preferred_element_type=f32)
        dhv = dh_ref[...]
        dpa, dpx, du, dba, dbx, dlam = _coeff_bwd(pre[:, :HALF], pre[:, HALF:], u_ref[...], ba_ref[...], bx_ref[...],
                                                  lam_ref[...], dhv * yp_ref[...], dhv)
        dpre_ref[:, :HALF] = dpa.astype(bf16)
        dpre_ref[:, HALF:] = dpx.astype(bf16)
        du_ref[...] = du

        @pl.when(pl.program_id(1) == 0)
        def _():
            dba_ref[...] = jnp.zeros_like(dba_ref)
            dbx_ref[...] = jnp.zeros_like(dbx_ref)
            dlam_ref[...] = jnp.zeros_like(dlam_ref)

        dba_ref[...] += dba
        dbx_ref[...] += dbx
        dlam_ref[...] += dlam

    tile = pl.BlockSpec((COEFF_TM, HALF), lambda h, i: (i, h))
    acc = pl.BlockSpec((1, HALF), lambda h, i: (0, h))
    pspecs, pargs = _dir_params(d, ba, bx, lam)
    extra = [] if dpre_prev is None else [dpre_prev]
    return pl.pallas_call(
        body, name=f"l0_gates_coeff_bwd_{d}", grid=(2, rows // COEFF_TM),
        out_shape=[jax.ShapeDtypeStruct((rows, 2 * N_PARTS * HALF), bf16), jax.ShapeDtypeStruct((rows, R), f32)]
        + [jax.ShapeDtypeStruct((1, R), f32)] * 3,
        in_specs=[tile] * 4 + [pl.BlockSpec((HALF, 2 * HALF), lambda h, i: (0, 2 * h + d))] + pspecs
        + [ANY_SPEC] * n_prev,
        out_specs=[pl.BlockSpec((COEFF_TM, 2 * HALF), lambda h, i: (i, 2 * h + d)), tile, acc, acc, acc],
        input_output_aliases={8: 0} if n_prev else {}, compiler_params=_cparams("parallel", "arbitrary"),
    )(ub, u, dh, yp, wg, *pargs, *extra)


def _gates_dw(u, dpre):
    rows = u.shape[0]

    def body(u_ref, d_ref, o_ref):
        o_ref[...] = lax.dot_general(u_ref[...], d_ref[...], (((0,), (0,)), ((), ())), preferred_element_type=f32)

    return pl.pallas_call(
        body, name="l0_gates_dw", grid=(2 * N_PARTS,), out_shape=jax.ShapeDtypeStruct((HALF, 2 * N_PARTS * HALF), f32),
        in_specs=[pl.BlockSpec((rows, HALF), lambda j: (0, j // N_PARTS)), pl.BlockSpec((rows, HALF), lambda j: (0, j))],
        out_specs=pl.BlockSpec((HALF, HALF), lambda j: (0, j)), compiler_params=_cparams("parallel"),
    )(u, dpre)


N_SCAN_CHUNKS = T_ALL // SCAN_CHUNK
SCAN_FWD = lambda t: t
SCAN_FWD_BWD = lambda t: N_SCAN_CHUNKS - 1 - t
SCAN_REV = lambda t: jnp.where(t == 0, 0, N_SCAN_CHUNKS - t)
SCAN_REV_BWD = lambda t: jnp.where(t == N_SCAN_CHUNKS - 1, 0, t + 1)
CONV_SEGMENTS = ((0, T_CTX), (T_CTX, T_LAT))
FUSED_TM = 256


def _token_rows(x, ctx):
    return (jnp.concatenate([ctx, x], axis=0),
            jnp.concatenate([jnp.zeros((T_CTX, D), f32), _pos_embed()], axis=0))


def _local_step(xcat, poscat, target, mods, cmod, wts, late_weights, send_grads, reduce_loss, start_after=()):
    sh1, sc1, g1, sh2, sc2, g2 = [[mods[l, i][None] for l in range(2)] for i in range(N_MOD)]
    ng = wts["norm_g"]
    scp = jnp.concatenate([cmod[1][None], sc1[0]], axis=0)
    shp = jnp.concatenate([cmod[0][None], sh1[0]], axis=0)

    ctx_tiles = T_CTX // FUSED_TM
    nt = (((1,), (1,)), ((), ()))

    def blend(i, p):
        sel = jnp.where(i < ctx_tiles, 1.0, 0.0)
        return sel * p[0:1] + (1.0 - sel) * p[1:2]

    def f_pre0(i, xc, pos, g, scp_, shp_, w):
        x0 = xc + pos
        h = _normmod(x0, g, blend(i, scp_), blend(i, shp_)).astype(bf16)
        return x0, h, jnp.dot(h, w, preferred_element_type=f32)

    x0cat, h0, gr = _rowcall(f_pre0, "l0_prenorm_in_proj", T_ALL, FUSED_TM, [_rin(xcat), _rin(poscat)],
                             [ng[0, 0][None], scp, shp, wts["rec_w_in"]], [(D, f32), (D, bf16), (2 * R, f32)],
                             after=start_after)
    u, ub = _dwconv_fwd(gr, R // 256, wts["rec_conv_w"], wts["rec_conv_b"], 4, 1, CONV_SEGMENTS, 256,
                        "l0_conv", True)
    gate_args = (wts["gates"], wts["rec_b_a"], wts["rec_b_x"], wts["rec_lambda"])
    a0, b0 = _gates_coeff_fwd(ub, u, *gate_args, 0)
    a1, b1 = _gates_coeff_fwd(ub, u, *gate_args, 1)
    halfway = late_weights("mlp_halfway", a1)
    y0, yp0 = _scan_call(a0, b0, SCAN_FWD, False, "l0_scan_fwd", False, after=[halfway])
    y1, yp1 = _scan_call(a1, b1, SCAN_REV, True, "l0_scan_rev", False)

    wts = dict(wts, **late_weights("mlp", y1))

    def f_gate_out(i, gp, y0_, y1_, w):
        z = (_gelu(gp) * (y0_ + y1_)).astype(bf16)
        return z, jnp.dot(z, w, preferred_element_type=f32)

    zb, out0 = _rowcall(f_gate_out, "l0_gate_out_proj", T_LAT, FUSED_TM,
                        [_rin(gr, R, 0, ctx_tiles), _rin(y0, None, 0, ctx_tiles), _rin(y1, None, 0, ctx_tiles)],
                        [wts["rec_w_out"]], [(R, bf16), (D, f32)])

    zero_d = jnp.zeros((1, D), f32)

    def mlp_params(rows):
        rows = rows + [zero_d] * (N_MLP_PARAMS - len(rows))
        return jnp.concatenate([jnp.broadcast_to(r, (8, D)) for r in rows], axis=0)

    par0 = mlp_params([g1[0], zero_d, ng[0, 1][None], sc2[0], sh2[0], g2[0], ng[1, 0][None], sc1[1], sh1[1]])
    x1, h1, r0, mo0, x2, h2 = _mlp_forward(x0cat, T_CTX // MLP_TM, out0, par0, wts["mlp_w_in"], wts["mlp_w_out"], 0,
                                           "l0_mlp")

    wts = dict(wts, **late_weights("conf", x2))
    def glu(pa, pb, b1):
        return (pa + b1[:, :D]) * _sigmoid(pb + b1[:, D:])

    def f_pw1_glu(i, h_, b1, w):
        p = jnp.dot(h_, w, preferred_element_type=f32)
        return glu(p[:, :D], p[:, D:], b1), p

    zg, pw = _rowcall(f_pw1_glu, "l1_pw1_glu", T_LAT, FUSED_TM, [_rin(h2)], [wts["conf_b_pw1"], wts["conf_w_pw1"]],
                      [(D, f32), (2 * D, bf16)])
    (zc,) = _dwconv_fwd(zg, 0, wts["conf_conv_w"], wts["conf_conv_b"], 31, 15, ((0, T_LAT),), 128, "l1_conv", False)

    def ln_silu(z, lg, lb):
        mu = jnp.mean(z, axis=-1, keepdims=True)
        zc_ = z - mu
        var = jnp.mean(zc_ * zc_, axis=-1, keepdims=True)
        yv = zc_ * lax.rsqrt(var + EPS) * lg + lb
        return yv * _sigmoid(yv)

    def f_lnsilu_pw2(i, z, lg, lb, w):
        s = ln_silu(z, lg, lb).astype(bf16)
        return s, jnp.dot(s, w, preferred_element_type=f32)

    sb, out1 = _rowcall(f_lnsilu_pw2, "l1_ln_silu_pw2", T_LAT, FUSED_TM, [_rin(zc)],
                        [wts["conf_ln_g"], wts["conf_ln_b"], wts["conf_w_pw2"]], [(D, bf16), (D, f32)])
    par1 = mlp_params([g1[1], wts["conf_b_pw2"], ng[1, 1][None], sc2[1], sh2[1], g2[1]])
    x3, h3, r1, mo1, x4, _ = _mlp_forward(x2, 0, out1, par1, wts["mlp_w_in"], wts["mlp_w_out"], 1, "l1_mlp")

    def loss_fn(x4_, fg, tgt):
        err = _rms(x4_, fg) - tgt
        per_row = jnp.mean(err * err, axis=-1, keepdims=True)
        return 0.5 * jnp.sum(per_row, axis=0, keepdims=True)

    def f_head(i, x4_, tgt, fg):
        loss, vjp = jax.vjp(lambda a, e: loss_fn(a, e, tgt), x4_, fg)
        dx, dfg = vjp(jnp.ones((1, 1), f32))
        return dx, jnp.broadcast_to(loss, (1, 128)), dfg

    dx4, loss_acc, dfinal_g = _rowcall(f_head, "head", T_LAT, FUSED_TM, [_rin(x4), _rin(target)], [wts["final_g"]],
                                       [(D, f32)], [(1, 128), (1, D)])

    grads = {"final_g": dfinal_g}
    loss = reduce_loss(loss_acc[0, 0])

    dx3, dout1, dmo1, dhid1, acc1 = _mlp_backward(dx4, x3, r1, mo1, out1, par1, wts["mlp_w_in"], wts["mlp_w_out"], 1,
                                                  "l1_mlp_bwd", after=[loss.reshape(1, 1)])
    mlp_dw = _mlp_weight_grads(h3, dhid1, r1, dmo1, 1, None, "l1")
    dg1_1, db_pw2, dng11, dsc2_1, dsh2_1, dg2_1 = [acc1[k:k + 1] for k in range(6)]

    grads["conf_w_pw2"] = _mm(sb, dout1, "l1_pw2_dw", ta=True, out_dtype=bf16)
    grads["conf_b_pw2"] = db_pw2

    def f_pw2_lnsilu_bwd(i, z, dout, lg, lb, w):
        ds = lax.dot_general(dout, w, nt, preferred_element_type=f32)
        _, vjp = jax.vjp(ln_silu, z, lg, lb)
        return vjp(ds)

    dzc, dln_g, dln_b = _rowcall(f_pw2_lnsilu_bwd, "l1_pw2_ln_silu_bwd", T_LAT, FUSED_TM, [_rin(zc), _rin(dout1)],
                                 [wts["conf_ln_g"], wts["conf_ln_b"], wts["conf_w_pw2"]], [(D, f32)], [(1, D)] * 2)
    grads["conf_ln_g"], grads["conf_ln_b"] = dln_g, dln_b
    dzg, dconv_w, dconv_b = _dwconv_bwd([dzc], zg, 0, wts["conf_conv_w"], 31, 15, ((0, T_LAT),), 128,
                                        "l1_conv_bwd", f32)
    grads["conf_conv_w"], grads["conf_conv_b"] = dconv_w, dconv_b

    def f_glu_pw1_norm_bwd(i, p_, dz, x_, dxs, b1, g_, sc_, sh_, w):
        pf = p_.astype(f32)
        _, vjp = jax.vjp(glu, pf[:, :D], pf[:, D:], b1)
        da, db, db1 = vjp(dz)
        dp = jnp.concatenate([da, db], axis=1).astype(bf16)
        dh = lax.dot_general(dp, w, nt, preferred_element_type=f32)
        _, vjp = jax.vjp(_normmod, x_, g_, sc_, sh_)
        dx, dg, dsc, dsh = vjp(dh)
        return dp, dx + dxs, db1, dg, dsc, dsh

    dpw, dx2, db_pw1, dng10, dsc1_1, dsh1_1 = _rowcall(
        f_glu_pw1_norm_bwd, "l1_glu_pw1_normmod_bwd", T_LAT, FUSED_TM, [_rin(pw), _rin(dzg), _rin(x2), _rin(dx3)],
        [wts["conf_b_pw1"], ng[1, 0][None], sc1[1], sh1[1], wts["conf_w_pw1"]], [(2 * D, bf16), (D, f32)],
        [(1, 2 * D), (1, D), (1, D), (1, D)])
    grads["conf_b_pw1"] = db_pw1
    grads["conf_w_pw1"] = _mm(h2, dpw, "l1_pw1_dw", ta=True, out_dtype=bf16)
    sent = send_grads(["conf_w_pw2", "conf_w_pw1"], grads)

    dx1, dout0, dmo0, dhid0, acc0 = _mlp_backward(dx2, x1, r0, mo0, out0, par0, wts["mlp_w_in"], wts["mlp_w_out"], 0,
                                                  "l0_mlp_bwd", after=[sent])
    grads["mlp_w_in"], grads["mlp_w_out"] = _mlp_weight_grads(h1, dhid0, r0, dmo0, 0, mlp_dw, "l0")
    sent = send_grads(["mlp_w_in", "mlp_w_out"], grads)
    dg1_0, _, dng01, dsc2_0, dsh2_0, dg2_0 = [acc0[k:k + 1] for k in range(6)]

    grads["rec_w_out"] = _mm(zb, dout0, "l0_out_proj_dw", ta=True, out_dtype=bf16, after=[sent])
    sent = send_grads(["rec_w_out"], grads)

    def f_out_gate_bwd(i, gp, y0_, y1_, dout, w):
        lat = jnp.where(i < ctx_tiles, 0.0, 1.0)
        dz = lax.dot_general(dout, w, nt, preferred_element_type=f32)
        _, vjp = jax.vjp(lambda a, b: _gelu(a) * b, gp, y0_ + y1_)
        dgp, dy = vjp(dz)
        return dgp * lat, dy * lat

    dgp, dy = _rowcall(f_out_gate_bwd, "l0_out_proj_gate_bwd", T_ALL, FUSED_TM,
                       [_rin(gr, R, 0), _rin(y0), _rin(y1), _rin(dout0, None, 0, -ctx_tiles)], [wts["rec_w_out"]],
                       [(R, bf16), (R, f32)], after=[sent])
    (dh_f,) = _scan_call(a0, dy, SCAN_FWD_BWD, True, "l0_scan_fwd_bwd", True)
    (dh_r,) = _scan_call(a1, dy, SCAN_REV_BWD, False, "l0_scan_rev_bwd", True)

    dpre, du_f, *dpar_f = _gates_coeff_bwd(ub, u, dh_f, yp0, *gate_args, 0, None)
    dpre, du_r, *dpar_r = _gates_coeff_bwd(ub, u, dh_r, yp1, *gate_args, 1, dpre)
    grads["rec_b_a"], grads["rec_b_x"], grads["rec_lambda"] = [
        jnp.concatenate([f.reshape(-1), r_.reshape(-1)]).reshape(2, R) for f, r_ in zip(dpar_f, dpar_r)]
    grads["gates"] = _gates_dw(ub, dpre)
    sent = send_grads(["replicated"], grads)
    du_gates = _gates_dx(dpre, wts["gates"], after=[sent])
    drec, dconv4_w, dconv4_b = _dwconv_bwd([du_f, du_r, du_gates], gr, R // 256, wts["rec_conv_w"], 4, 1,
                                           CONV_SEGMENTS, 256, "l0_conv_bwd", bf16)
    grads["rec_conv_w"], grads["rec_conv_b"] = dconv4_w, dconv4_b
    dgr = jnp.concatenate([dgp, drec], axis=1)
    grads["rec_w_in"] = _mm(h0, dgr, "l0_in_proj_dw", ta=True, out_dtype=bf16)
    sent = send_grads(["rec_w_in"], grads)

    def f_pre0_bwd(i, x0, dgr_, dxs, g, scp_, shp_, w):
        lat = jnp.where(i < ctx_tiles, 0.0, 1.0)
        dh = lax.dot_general(dgr_, w, nt, preferred_element_type=f32)
        _, vjp = jax.vjp(lambda a, b, c, e: _normmod(a, b, blend(i, c), blend(i, e)), x0, g, scp_, shp_)
        dx, dg, dscp, dshp = vjp(dh)
        return dx + lat * dxs, dg, dscp, dshp

    dx0cat, dng00, dscp, dshp = _rowcall(
        f_pre0_bwd, "l0_in_proj_prenorm_bwd", T_ALL, FUSED_TM,
        [_rin(x0cat), _rin(dgr), _rin(dx1, None, 0, -ctx_tiles)], [ng[0, 0][None], scp, shp, wts["rec_w_in"]],
        [(D, f32)], [(1, D), (2, D), (2, D)], after=[sent])

    grads["norm_g"] = jnp.stack([jnp.concatenate([dng00, dng01], 0), jnp.concatenate([dng10, dng11], 0)])
    dmods = jnp.stack([
        jnp.concatenate([dshp[1:2], dscp[1:2], dg1_0, dsh2_0, dsc2_0, dg2_0], axis=0),
        jnp.concatenate([dsh1_1, dsc1_1, dg1_1, dsh2_1, dsc2_1, dg2_1], axis=0)])
    dcmod = jnp.concatenate([dshp[0:1], dscp[0:1]], axis=0)
    return loss, dx0cat[T_CTX:], dmods, dcmod, grads


def _unshard_cols(g):
    g = jnp.moveaxis(g, 0, -2)
    return g.reshape(g.shape[:-2] + (g.shape[-2] * g.shape[-1],))


def _shard_cols(w):
    w = w.reshape(w.shape[:-1] + (N_DEV, w.shape[-1] // N_DEV))
    return jnp.moveaxis(w, -2, 0)


def _shard_rows(w):
    return w.reshape((N_DEV, w.shape[0] // N_DEV) + w.shape[1:])


SMALL_PACK_ROWS = 64
REPL_FINAL_G_ROWS = -(-D // BLK)
REPL_ROWS = -(-(2 * 2 * N_BLK * BLK + 2 * 2 * N_BLK + REPL_FINAL_G_ROWS) // 16) * 16


def kernel(x, c, ctx, c_ctx, w_ada, b_ada, norm_g, rec_w_in, rec_conv_w, rec_conv_b, rec_lambda, rec_w_a, rec_b_a, rec_w_x, rec_b_x, rec_w_out, conf_w_pw1, conf_b_pw1, conf_conv_w, conf_conv_b, conf_ln_g, conf_ln_b, conf_w_pw2, conf_b_pw2, mlp_w_in, mlp_w_out, final_g, loss_target, m_c_ctx, m_w_ada, m_b_ada, m_norm_g, m_rec_w_in, m_rec_conv_w, m_rec_conv_b, m_rec_lambda, m_rec_w_a, m_rec_b_a, m_rec_w_x, m_rec_b_x, m_rec_w_out, m_conf_w_pw1, m_conf_b_pw1, m_conf_conv_w, m_conf_conv_b, m_conf_ln_g, m_conf_ln_b, m_conf_w_pw2, m_conf_b_pw2, m_mlp_w_in, m_mlp_w_out, m_final_g, v_c_ctx, v_w_ada, v_b_ada, v_norm_g, v_rec_w_in, v_rec_conv_w, v_rec_conv_b, v_rec_lambda, v_rec_w_a, v_rec_b_a, v_rec_w_x, v_rec_b_x, v_rec_w_out, v_conf_w_pw1, v_conf_b_pw1, v_conf_conv_w, v_conf_conv_b, v_conf_ln_g, v_conf_ln_b, v_conf_w_pw2, v_conf_b_pw2, v_mlp_w_in, v_mlp_w_out, v_final_g):
    me = 4 * lax.axis_index("x") + 2 * lax.axis_index("y") + lax.axis_index("c")
    weights = dict(c_ctx=c_ctx, w_ada=w_ada, b_ada=b_ada, norm_g=norm_g, rec_w_in=rec_w_in, rec_conv_w=rec_conv_w,
                   rec_conv_b=rec_conv_b, rec_lambda=rec_lambda, rec_w_a=rec_w_a, rec_b_a=rec_b_a, rec_w_x=rec_w_x,
                   rec_b_x=rec_b_x, rec_w_out=rec_w_out, conf_w_pw1=conf_w_pw1, conf_b_pw1=conf_b_pw1,
                   conf_conv_w=conf_conv_w, conf_conv_b=conf_conv_b, conf_ln_g=conf_ln_g, conf_ln_b=conf_ln_b,
                   conf_w_pw2=conf_w_pw2, conf_b_pw2=conf_b_pw2, mlp_w_in=mlp_w_in, mlp_w_out=mlp_w_out, final_g=final_g)
    m_in = dict(c_ctx=m_c_ctx, w_ada=m_w_ada, b_ada=m_b_ada, norm_g=m_norm_g, rec_w_in=m_rec_w_in, rec_conv_w=m_rec_conv_w,
                rec_conv_b=m_rec_conv_b, rec_lambda=m_rec_lambda, rec_w_a=m_rec_w_a, rec_b_a=m_rec_b_a, rec_w_x=m_rec_w_x,
                rec_b_x=m_rec_b_x, rec_w_out=m_rec_w_out, conf_w_pw1=m_conf_w_pw1, conf_b_pw1=m_conf_b_pw1,
                conf_conv_w=m_conf_conv_w, conf_conv_b=m_conf_conv_b, conf_ln_g=m_conf_ln_g, conf_ln_b=m_conf_ln_b,
                conf_w_pw2=m_conf_w_pw2, conf_b_pw2=m_conf_b_pw2, mlp_w_in=m_mlp_w_in, mlp_w_out=m_mlp_w_out,
                final_g=m_final_g)
    v_in = dict(c_ctx=v_c_ctx, w_ada=v_w_ada, b_ada=v_b_ada, norm_g=v_norm_g, rec_w_in=v_rec_w_in, rec_conv_w=v_rec_conv_w,
                rec_conv_b=v_rec_conv_b, rec_lambda=v_rec_lambda, rec_w_a=v_rec_w_a, rec_b_a=v_rec_b_a, rec_w_x=v_rec_w_x,
                rec_b_x=v_rec_b_x, rec_w_out=v_rec_w_out, conf_w_pw1=v_conf_w_pw1, conf_b_pw1=v_conf_b_pw1,
                conf_conv_w=v_conf_conv_w, conf_conv_b=v_conf_conv_b, conf_ln_g=v_conf_ln_g, conf_ln_b=v_conf_ln_b,
                conf_w_pw2=v_conf_w_pw2, conf_b_pw2=v_conf_b_pw2, mlp_w_in=v_mlp_w_in, mlp_w_out=v_mlp_w_out,
                final_g=v_final_g)
    names = list(weights)

    small_items = [c, norm_g, rec_conv_w, rec_lambda, conf_b_pw1, conf_conv_w, conf_conv_b, conf_ln_g, conf_ln_b,
                   conf_b_pw2]
    flat = jnp.concatenate([a.reshape(-1) for a in small_items])
    flat = jnp.pad(flat, (0, SMALL_PACK_ROWS * 128 - flat.shape[0])).reshape(SMALL_PACK_ROWS, 128)
    as_shard = lambda a: a.astype(bf16).reshape(-1, a.shape[-1])
    early_srcs = [flat, as_shard(rec_w_in[0])]
    early_handle, started = _exchange_start(early_srcs, [_own_block_filled(s, me) for s in early_srcs],
                                            "gather_early_start", False)
    zero = started[0, 0]
    gates = _gate_matrix(rec_w_a[0] + zero, rec_w_x[0] + zero)
    late_items = {"mlp": [rec_w_out[0], mlp_w_in, mlp_w_out], "conf": [conf_w_pw1[0], conf_w_pw2[0]]}
    late_shards = {g: [as_shard(a + zero) for a in items] for g, items in late_items.items()}
    late_lands = {g: [_own_block_filled(s, me) for s in shards] for g, shards in late_shards.items()}
    xcat, poscat = _token_rows(x[0] + zero, ctx[0])
    small_all, early = _exchange_wait(early_handle, [gates, xcat, poscat] + late_lands["mlp"] + late_lands["conf"],
                                      "gather_early_wait", False)

    small_all = small_all.reshape(N_DEV, -1)
    off = 0
    small = []
    for a in small_items:
        small.append(small_all[:, off:off + a.size].reshape((N_DEV,) + a.shape))
        off += a.size
    c_all, ng_all, rcw_all, lam_all, bpw1_all, ccw_all, ccb_all, lng_all, lnb_all, bpw2_all = small
    wts = {
        "norm_g": _unshard_cols(ng_all),
        "rec_conv_w": _unshard_cols(rcw_all)[0],
        "rec_lambda": _unshard_cols(lam_all)[0],
        "conf_b_pw1": _unshard_cols(bpw1_all),
        "conf_conv_w": _unshard_cols(ccw_all)[0],
        "conf_conv_b": _unshard_cols(ccb_all),
        "conf_ln_g": _unshard_cols(lng_all),
        "conf_ln_b": _unshard_cols(lnb_all),
        "conf_b_pw2": _unshard_cols(bpw2_all),
        "rec_conv_b": rec_conv_b,
        "rec_b_a": rec_b_a[0].reshape(2, R),
        "rec_b_x": rec_b_x[0].reshape(2, R),
        "final_g": final_g[None],
        "gates": gates,
    }

    c16 = jnp.concatenate([c_all[:, 0], jnp.broadcast_to(c_ctx[None], (8, D))], axis=0)
    b_loc = lax.dynamic_slice_in_dim(b_ada, me * ADA_SHARD, ADA_SHARD, axis=1)[:, None]
    (mods_gathered,) = _all_gather([_ada_forward(c16, w_ada, b_loc)], "gather_mods")
    mods_all = _unshard_cols(mods_gathered)
    mods = lax.dynamic_index_in_dim(mods_all, me, axis=1, keepdims=False).reshape(2, N_MOD, D)
    cmod = mods_all[0, 8, :2 * D].reshape(2, D)

    late_handles = {}
    late_handles["mlp"], token = _gather2_start(late_shards["mlp"], late_lands["mlp"], "gather_mlp_start",
                                                [early, mods_gathered])
    order = [token]
    wts["rec_w_in"] = _unshard_cols(early + token[0, 0].astype(bf16))

    def late_weights(group, after):
        if group == "mlp_halfway":
            late_handles["mlp"] = _gather2_forward1(late_handles["mlp"], after, "gather_mlp_forward1")
            return late_handles["mlp"][2][0]
        if group == "mlp":
            passed = _gather2_forward2(late_handles["mlp"], after, "gather_mlp_forward2")
            late_handles["conf"], started = _exchange_start(late_shards["conf"], late_lands["conf"], "gather_conf_start",
                                                            False, after=[passed[2][0]])
            got = _gather2_wait(passed, started, "gather_mlp_wait")
        else:
            got = _exchange_wait(late_handles[group], after, "gather_conf_wait", False)
        got = [g.reshape((N_DEV,) + a.shape) for g, a in zip(got, late_items[group])]
        if group == "mlp":
            return {"rec_w_out": got[0].reshape(R, D), "mlp_w_in": got[1], "mlp_w_out": got[2]}
        return {"conf_w_pw1": _unshard_cols(got[0]), "conf_w_pw2": got[1].reshape(D, D)}

    to_blocks = {"rec_w_in": _shard_cols, "conf_w_pw1": _shard_cols, "rec_w_out": _shard_rows, "conf_w_pw2": _shard_rows,
                 "mlp_w_in": lambda g: g, "mlp_w_out": lambda g: g}
    grad_handles = []

    repl_names = ["rec_w_a", "rec_w_x", "rec_b_a", "rec_b_x", "final_g"]

    def send_replicated(grads):
        dwg = grads["gates"]
        repl = {"rec_w_a": jnp.stack([_gate_blocks(dwg, 0), _gate_blocks(dwg, 2)]),
                "rec_w_x": jnp.stack([_gate_blocks(dwg, 1), _gate_blocks(dwg, 3)]),
                "rec_b_a": grads["rec_b_a"], "rec_b_x": grads["rec_b_x"],
                "final_g": jnp.pad(grads["final_g"], ((0, 0), (0, REPL_FINAL_G_ROWS * BLK - D)))}
        flat = jnp.concatenate([repl[n].reshape(-1, BLK) for n in repl_names], axis=0)
        flat = jnp.pad(flat, ((0, REPL_ROWS - flat.shape[0]), (0, 0))).astype(bf16)
        handle, sent = _exchange_start([flat], [_own_block_filled(flat, me)], "gather_replicated_start", False)
        grad_handles.append((["replicated"], handle))
        return sent

    def send_grads(group, grads):
        if group == ["replicated"]:
            return send_replicated(grads)
        blocks = [to_blocks[n](grads[n]) for n in group]
        blocks = [g.reshape(N_DEV, -1, g.shape[-1]) for g in blocks]
        lands = [_own_block_filled(lax.dynamic_index_in_dim(g, me, 0, keepdims=False), me) for g in blocks]
        handle, sent = _exchange_start(blocks, lands, "scatter_start_" + group[0], True)
        grad_handles.append((group, handle))
        return sent

    loss, grad_x, dmods, dcmod, grads = _local_step(
        xcat, poscat, loss_target[0], mods, cmod, wts, late_weights, send_grads,
        lambda partial: lax.psum(partial, ("x", "y", "c")), start_after=order)

    def as2d(shape):
        rows = 1
        for s in shape[:-1]:
            rows *= s
        return (rows, shape[-1])

    def whole(arr, shape):
        arr = arr.reshape((-1,) + as2d(shape))
        return (arr, arr.shape[0])

    shard_shapes = {n: weights[n].shape for n in names}
    g_out, d_out, m_out, v_out = {}, {}, {}, {}

    def adamw(n, pieces, after):
        shape = shard_shapes[n]
        r2, c2 = as2d(shape)
        g, dl, nm, nv = _adamw(pieces, weights[n].reshape(r2, c2), m_in[n].reshape(r2, c2), v_in[n].reshape(r2, c2),
                               "adamw_" + n, after=after)
        g_out[n], d_out[n], m_out[n], v_out[n] = (t.reshape(shape) for t in (g, dl, nm, nv))
        return g

    small_sharded = ["norm_g", "rec_conv_w", "rec_lambda", "conf_b_pw1", "conf_conv_w", "conf_conv_b", "conf_ln_g",
                     "conf_ln_b", "conf_b_pw2"]
    pack = jnp.concatenate([_shard_cols(grads[n]).reshape(N_DEV, -1) for n in small_sharded], axis=1)
    pack = jnp.pad(pack, ((0, 0), (0, SMALL_PACK_ROWS * 128 - pack.shape[1]))).reshape(N_DEV, SMALL_PACK_ROWS, 128)
    small_handle, token = _exchange_start(
        [pack], [_own_block_filled(lax.dynamic_index_in_dim(pack, me, 0, keepdims=False), me)], "scatter_small_start",
        True, after=[grad_x])
    dm_flat = jnp.concatenate([dmods.reshape(-1), dcmod.reshape(-1), grads["rec_conv_b"].reshape(-1)])
    dm_len = dm_flat.shape[0]
    dm_flat = jnp.pad(dm_flat, (0, 128 * 128 - dm_len)).reshape(128, 128)
    dm_handle, token = _exchange_start([dm_flat], [_own_block_filled(dm_flat, me)], "gather_dmods_start", False,
                                       after=[token])

    done = token
    for group, handle in grad_handles:
        if group == ["replicated"]:
            repl_all = _exchange_wait(handle, done, "gather_replicated_wait", False)[0]
            row = 0
            for n in repl_names:
                n_rows = -(-weights[n].size // BLK)
                if as2d(shard_shapes[n]) == (n_rows, BLK) and row % 256 == 0:
                    done = adamw(n, [(repl_all, N_DEV, row)], [done])
                else:
                    got = repl_all[:, row:row + n_rows].reshape(N_DEV, -1)[:, :weights[n].size]
                    done = adamw(n, [whole(got, shard_shapes[n])], [done])
                row += n_rows
            continue
        for n, got in zip(group, _exchange_wait(handle, done, "scatter_wait_" + group[0], True)):
            done = adamw(n, [(got, N_DEV)], [done])

    dm_all = _exchange_wait(dm_handle, done, "gather_dmods_wait", False)[0].reshape(N_DEV, -1)
    dmods_all = dm_all[:, :2 * N_MOD * D].reshape(N_DEV, 2, N_MOD * D)
    dcmod_all = jnp.pad(dm_all[:, 2 * N_MOD * D:2 * N_MOD * D + 2 * D], ((0, 0), (0, (N_MOD - 2) * D)))
    g16_full = jnp.stack([jnp.concatenate([dmods_all[:, 0], dcmod_all], axis=0),
                          jnp.concatenate([dmods_all[:, 1], jnp.zeros_like(dcmod_all)], axis=0)])
    g16 = lax.dynamic_slice_in_dim(g16_full, me * ADA_SHARD, ADA_SHARD, axis=2)
    dw_ada, ds_part = _ada_backward(c16, g16, w_ada)
    ds_handle, token = _exchange_start([ds_part[0]], [_own_block_filled(ds_part[0], me)], "gather_dsilu_start", False)
    done = adamw("w_ada", [whole(dw_ada, shard_shapes["w_ada"])], [token])
    done = adamw("rec_conv_b", [whole(dm_all[:, dm_len - R:dm_len], shard_shapes["rec_conv_b"])], [done])
    db_terms = jnp.concatenate([dmods_all, jnp.stack([dcmod_all, jnp.zeros_like(dcmod_all)], axis=1)], axis=0)
    done = adamw("b_ada", [whole(db_terms, shard_shapes["b_ada"])], [done])
    pack_recv = _exchange_wait(small_handle, done, "scatter_small_wait", True)[0].reshape(N_DEV, -1)
    off = 0
    for n in small_sharded:
        size = weights[n].size
        done = adamw(n, [whole(pack_recv[:, off:off + size], shard_shapes[n])], [done])
        off += size
    ds_all = _exchange_wait(ds_handle, done, "gather_dsilu_wait", False)[0]
    adamw("c_ctx", [whole(ds_all[:, 0], shard_shapes["c_ctx"])], [])

    return (loss, grad_x[None], *[g_out[n] for n in names], *[d_out[n] for n in names],
            *[m_out[n] for n in names], *[v_out[n] for n in names])
```

```python
import functools

import jax
import jax.numpy as jnp
from jax import lax
from jax.experimental import pallas as pl
from jax.experimental.pallas import tpu as pltpu

f32 = jnp.float32
bf16 = jnp.bfloat16

N_DEV = 8
D = 1024
T_LAT = 2048
T_CTX = 256
T_ALL = T_CTX + T_LAT
R = 1280
N_BLK = 16
BLK = R // N_BLK
F = 4096
GRID_W = 64
RG_C = 8.0
EPS = 1e-6
POS_BASE = 10000.0
N_MOD = 6
ADA_SHARD = N_MOD * D // N_DEV

ADAM_LR = 0.001
ADAM_B1 = 0.9
ADAM_B2 = 0.999
ADAM_EPS = 1e-08
ADAM_WD = 0.01
ADAM_STEP = 10

VMEM_LIMIT_V7X = 56 * 1024 * 1024
HALO = 16
MESH = pl.DeviceIdType.MESH


def _cparams(*sem):
    return pltpu.CompilerParams(dimension_semantics=sem, vmem_limit_bytes=VMEM_LIMIT_V7X)


def _pick(n, cands):
    for c in cands:
        if n % c == 0:
            return c
    raise ValueError(f"no block size for {n}")


def _position():
    x, y, c = lax.axis_index("x"), lax.axis_index("y"), lax.axis_index("c")
    return x, y, c, 4 * x + 2 * y + c


def _peer(x, y, c, k):
    px = (1 - x) if (k >> 2) & 1 else x
    py = (1 - y) if (k >> 1) & 1 else y
    pc = (1 - c) if k & 1 else c
    return (px, py, pc), 4 * px + 2 * py + pc


def _exchange(arrs, name, scatter):
    n = len(arrs)

    def body(*refs):
        ins, outs = refs[:n], refs[n:2 * n]
        send_sems, recv_sems, local_sems = refs[2 * n:]
        x, y, c, me = _position()
        local = []
        for a in range(n):
            src = ins[a].at[me] if scatter else ins[a]
            cp = pltpu.make_async_copy(src, outs[a].at[me], local_sems.at[a])
            cp.start()
            local.append(cp)
        sends, recvs = [], []
        for a in range(n):
            for k in range(1, N_DEV):
                peer, peer_lin = _peer(x, y, c, k)
                src = ins[a].at[peer_lin] if scatter else ins[a]
                cp = pltpu.make_async_remote_copy(
                    src_ref=src, dst_ref=outs[a].at[me], send_sem=send_sems.at[a, k - 1],
                    recv_sem=recv_sems.at[a, k - 1], device_id=peer, device_id_type=MESH)
                cp.start()
                sends.append(cp)
                recvs.append(pltpu.make_async_remote_copy(
                    src_ref=src, dst_ref=outs[a].at[peer_lin], send_sem=send_sems.at[a, k - 1],
                    recv_sem=recv_sems.at[a, k - 1], device_id=peer, device_id_type=MESH))
        for cp in recvs:
            cp.wait_recv()
        for cp in sends:
            cp.wait_send()
        for cp in local:
            cp.wait()

    if scatter:
        out_shape = [jax.ShapeDtypeStruct(a.shape, a.dtype) for a in arrs]
    else:
        out_shape = [jax.ShapeDtypeStruct((N_DEV,) + a.shape, a.dtype) for a in arrs]
    any_spec = pl.BlockSpec(memory_space=pl.ANY)
    return pl.pallas_call(
        body, name=name, out_shape=out_shape,
        in_specs=[any_spec] * n, out_specs=[any_spec] * n,
        scratch_shapes=[pltpu.SemaphoreType.DMA((n, N_DEV - 1)), pltpu.SemaphoreType.DMA((n, N_DEV - 1)),
                        pltpu.SemaphoreType.DMA((n,))],
    )(*arrs)


def _all_gather(arrs, name):
    return _exchange(arrs, name, scatter=False)


def _lin(p):
    return 4 * p[0] + 2 * p[1] + p[2]


HBM_SPEC = pl.BlockSpec(memory_space=pltpu.HBM)
SEM_SPEC = pl.BlockSpec(memory_space=pltpu.SEMAPHORE)
DATAFLOW_EFFECT = pltpu.SideEffectType.DATAFLOW_SIDE_EFFECTING


def _split_copies(srcs, lands, send_sems, recv_sems, scatter):
    x, y, c, me = _position()
    out = []
    for a in range(len(srcs)):
        for k in range(1, N_DEV):
            peer, peer_lin = _peer(x, y, c, k)
            src = srcs[a].at[peer_lin] if scatter else srcs[a]
            mk = lambda slot: pltpu.make_async_remote_copy(
                src_ref=src, dst_ref=lands[a].at[slot], send_sem=send_sems.at[a * (N_DEV - 1) + k - 1],
                recv_sem=recv_sems.at[a * (N_DEV - 1) + k - 1], device_id=peer, device_id_type=MESH)
            out.append((mk(me), mk(peer_lin)))
    return out


def _exchange_start(srcs, lands, name, scatter, after=()):
    n = len(srcs)
    n_after = len(after)

    def body(*refs):
        srcs_r, lands_r = refs[:n], refs[n:2 * n]
        send_sems, recv_sems = refs[2 * n + n_after], refs[2 * n + n_after + 1]
        token = refs[-1]
        for outgoing, _ in _split_copies(srcs_r, lands_r, send_sems, recv_sems, scatter):
            outgoing.start()
        token[...] = jnp.zeros_like(token)

    hbm = lambda a: pltpu.HBM(a.shape, a.dtype)
    res = pl.pallas_call(
        body, name=name,
        out_shape=(pltpu.SemaphoreType.DMA((n * (N_DEV - 1),)), pltpu.SemaphoreType.DMA((n * (N_DEV - 1),)),
                   *[hbm(a) for a in srcs], *[hbm(a) for a in lands], jax.ShapeDtypeStruct((8, 128), f32)),
        in_specs=[HBM_SPEC] * (2 * n) + [pl.BlockSpec(memory_space=pl.ANY)] * n_after,
        out_specs=(SEM_SPEC, SEM_SPEC, *[HBM_SPEC] * (2 * n), pl.BlockSpec(memory_space=pltpu.VMEM)),
        input_output_aliases={i: 2 + i for i in range(2 * n)},
        compiler_params=pltpu.CompilerParams(has_side_effects=DATAFLOW_EFFECT),
    )(*[pltpu.with_memory_space_constraint(a, pltpu.HBM) for a in list(srcs) + list(lands)], *after)
    return (res[0], res[1], list(res[2:2 + n]), list(res[2 + n:2 + 2 * n])), res[-1]


def _exchange_wait(handle, after, name, scatter):
    send_sems, recv_sems, srcs, lands = handle
    n = len(srcs)
    after = list(after) if isinstance(after, (list, tuple)) else [after]

    def body(*refs):
        srcs_r, lands_r = refs[:n], refs[n:2 * n]
        send_s, recv_s = refs[2 * n], refs[2 * n + 1]
        for outgoing, incoming in _split_copies(srcs_r, lands_r, send_s, recv_s, scatter):
            outgoing.wait_send()
            incoming.wait_recv()

    hbm = lambda a: pltpu.HBM(a.shape, a.dtype)
    res = pl.pallas_call(
        body, name=name, out_shape=tuple(hbm(a) for a in list(srcs) + list(lands)),
        in_specs=[HBM_SPEC] * (2 * n) + [SEM_SPEC, SEM_SPEC] + [pl.BlockSpec(memory_space=pl.ANY)] * len(after),
        out_specs=tuple([HBM_SPEC] * (2 * n)),
        input_output_aliases={i: i for i in range(2 * n)},
        compiler_params=pltpu.CompilerParams(has_side_effects=DATAFLOW_EFFECT),
    )(*srcs, *lands, send_sems, recv_sems, *after)
    return list(res[n:])


def _split_call(body, name, hbm_ins, kept, in_sems, n_new_sems, after, with_token):
    n_in, n_sem = len(hbm_ins), len(in_sems)
    out_shape, out_specs = [], []
    if n_new_sems:
        out_shape += [pltpu.SemaphoreType.DMA((n_new_sems,))] * 2
        out_specs += [SEM_SPEC] * 2
    first_kept = len(out_shape)
    out_shape += [pltpu.HBM(hbm_ins[i].shape, hbm_ins[i].dtype) for i in kept]
    out_specs += [HBM_SPEC] * len(kept)
    if with_token:
        out_shape.append(jax.ShapeDtypeStruct((8, 128), f32))
        out_specs.append(pl.BlockSpec(memory_space=pltpu.VMEM))

    def wrapped(*refs):
        outs = refs[n_in + n_sem + len(after):]
        body(refs[:n_in], refs[n_in:n_in + n_sem], outs[:2] if n_new_sems else ())
        if with_token:
            outs[-1][...] = jnp.zeros_like(outs[-1])

    return pl.pallas_call(
        wrapped, name=name, out_shape=tuple(out_shape),
        in_specs=[HBM_SPEC] * n_in + [SEM_SPEC] * n_sem + [pl.BlockSpec(memory_space=pl.ANY)] * len(after),
        out_specs=tuple(out_specs), input_output_aliases={i: first_kept + j for j, i in enumerate(kept)},
        compiler_params=pltpu.CompilerParams(has_side_effects=DATAFLOW_EFFECT),
    )(*[pltpu.with_memory_space_constraint(a, pltpu.HBM) for a in hbm_ins], *in_sems, *after)


def _rcopy(src, dst, sems, k, to):
    return pltpu.make_async_remote_copy(src_ref=src, dst_ref=dst, send_sem=sems[0].at[k], recv_sem=sems[1].at[k],
                                        device_id=to, device_id_type=MESH)


def _gather2_start(shards, lands, name, after):
    n = len(shards)

    def body(ins, sems_in, sems_out):
        x, y, c, me = _position()
        for a in range(n):
            for k, to in enumerate(((x, y, 1 - c), (1 - x, y, c), (x, 1 - y, c))):
                _rcopy(ins[a], ins[n + a].at[me], sems_out, 3 * a + k, to).start()

    res = _split_call(body, name, list(shards) + list(lands), range(2 * n), (), 3 * n, after, True)
    return (res[0], res[1], list(res[2:2 + n]), list(res[2 + n:2 + 2 * n])), res[-1]


def _gather2_forward1(handle, after, name):
    send_sems, recv_sems, srcs, lands = handle
    n = len(srcs)

    def body(ins, sems_in, sems_out):
        x, y, c, me = _position()
        sib, xn, yn = (x, y, 1 - c), (1 - x, y, c), (x, 1 - y, c)
        for a in range(n):
            for k, peer in enumerate((sib, xn, yn)):
                _rcopy(ins[a], ins[n + a].at[me], sems_in, 3 * a + k, peer).wait_send()
                _rcopy(ins[a], ins[n + a].at[_lin(peer)], sems_in, 3 * a + k, peer).wait_recv()
        for a in range(n):
            land = ins[n + a]
            _rcopy(land.at[_lin(xn)], land.at[_lin(xn)], sems_out, 3 * a, sib).start()
            _rcopy(land.at[_lin(yn)], land.at[_lin(yn)], sems_out, 3 * a + 1, sib).start()

            @pl.when(c == 0)
            def _():
                _rcopy(land.at[_lin(xn)], land.at[_lin(xn)], sems_out, 3 * a + 2, yn).start()

            @pl.when(c == 1)
            def _():
                _rcopy(land.at[_lin(yn)], land.at[_lin(yn)], sems_out, 3 * a + 2, xn).start()

    res = _split_call(body, name, list(srcs) + list(lands), range(n, 2 * n), (send_sems, recv_sems), 3 * n, [after], False)
    return (res[0], res[1], list(res[2:]))


def _gather2_forward2(handle, after, name):
    send_sems, recv_sems, lands = handle
    n = len(lands)

    def body(ins, sems_in, sems_out):
        x, y, c, me = _position()
        sib, dg = (x, y, 1 - c), _lin((1 - x, 1 - y, c))
        for a in range(n):
            for k, slot in enumerate((_lin((1 - x, y, 1 - c)), _lin((x, 1 - y, 1 - c)), dg)):
                done = _rcopy(ins[a].at[slot], ins[a].at[slot], sems_in, 3 * a + k, sib)
                done.wait_send()
                done.wait_recv()
        for a in range(n):
            _rcopy(ins[a].at[dg], ins[a].at[dg], sems_out, a, sib).start()

    res = _split_call(body, name, list(lands), range(n), (send_sems, recv_sems), n, [after], False)
    return (res[0], res[1], list(res[2:]))


def _gather2_wait(handle, after, name):
    send_sems, recv_sems, lands = handle
    n = len(lands)

    def body(ins, sems_in, sems_out):
        x, y, c, me = _position()
        slot = _lin((1 - x, 1 - y, 1 - c))
        for a in range(n):
            done = _rcopy(ins[a].at[slot], ins[a].at[slot], sems_in, a, (x, y, 1 - c))
            done.wait_send()
            done.wait_recv()

    return list(_split_call(body, name, list(lands), range(n), (send_sems, recv_sems), 0, [after], False))


def _own_block_filled(block, me):
    land = lax.empty((N_DEV,) + block.shape, block.dtype)
    return lax.dynamic_update_index_in_dim(land, block, me, 0)


ANY_SPEC = pl.BlockSpec(memory_space=pl.ANY)


def _mm(a, b, name, ta=False, tb=False, out_dtype=f32, after=()):
    if ta:
        k_dim, m_dim = a.shape
    else:
        m_dim, k_dim = a.shape
    if tb:
        n_dim, k2 = b.shape
    else:
        k2, n_dim = b.shape
    assert k_dim == k2, (a.shape, b.shape)
    assert a.dtype == bf16 and b.dtype == bf16
    bm = _pick(m_dim, (512, 768, 640, 256, 128))
    bn = _pick(n_dim, (512, 640, 256, 128))
    bk = k_dim if k_dim <= 2560 else _pick(k_dim, (1024, 1280, 768, 512))
    nk = k_dim // bk
    a_spec = (pl.BlockSpec((bk, bm), lambda i, j, k: (k, i)) if ta
              else pl.BlockSpec((bm, bk), lambda i, j, k: (i, k)))
    b_spec = (pl.BlockSpec((bn, bk), lambda i, j, k: (j, k)) if tb
              else pl.BlockSpec((bk, bn), lambda i, j, k: (k, j)))
    dims = (((0 if ta else 1,), (1 if tb else 0,)), ((), ()))

    n_after = len(after)

    def body_single(a_ref, b_ref, *rest):
        o_ref = rest[n_after]
        o_ref[...] = lax.dot_general(a_ref[...], b_ref[...], dims, preferred_element_type=f32).astype(o_ref.dtype)

    def body(a_ref, b_ref, *rest):
        o_ref, acc_ref = rest[n_after:]
        k = pl.program_id(2)

        @pl.when(k == 0)
        def _():
            acc_ref[...] = jnp.zeros_like(acc_ref)

        acc_ref[...] += lax.dot_general(a_ref[...], b_ref[...], dims, preferred_element_type=f32)

        @pl.when(k == nk - 1)
        def _():
            o_ref[...] = acc_ref[...].astype(o_ref.dtype)

    return pl.pallas_call(
        body_single if nk == 1 else body, name=name, out_shape=jax.ShapeDtypeStruct((m_dim, n_dim), out_dtype),
        grid=(m_dim // bm, n_dim // bn, nk), in_specs=[a_spec, b_spec] + [ANY_SPEC] * n_after,
        out_specs=pl.BlockSpec((bm, bn), lambda i, j, k: (i, j)),
        scratch_shapes=[] if nk == 1 else [pltpu.VMEM((bm, bn), f32)],
        compiler_params=_cparams("parallel", "parallel", "arbitrary"),
    )(a, b, *after)


def _rin(arr, width=None, cb=0, roff=0):
    return (arr, arr.shape[1] if width is None else width, cb, roff)


def _rowcall(fn, name, rows, tm, row_ins, par_ins, row_outs, acc_outs=(), after=()):
    nr, npar, nro, n_after = len(row_ins), len(par_ins), len(row_outs), len(after)
    in_specs, args = [], []
    for arr, width, cb, roff in row_ins:
        if roff >= 0:
            imap = lambda i, cb=cb, roff=roff: (i + roff, cb)
        else:
            imap = lambda i, cb=cb, roff=roff: (jnp.maximum(i + roff, 0), cb)
        in_specs.append(pl.BlockSpec((tm, width), imap))
        args.append(arr)
    for p in par_ins:
        in_specs.append(pl.BlockSpec(p.shape, lambda i: (0, 0)))
        args.append(p)
    out_shape, out_specs = [], []
    for width, dt in row_outs:
        out_shape.append(jax.ShapeDtypeStruct((rows, width), dt))
        out_specs.append(pl.BlockSpec((tm, width), lambda i: (i, 0)))
    for p, width in acc_outs:
        out_shape.append(jax.ShapeDtypeStruct((p, width), f32))
        out_specs.append(pl.BlockSpec((p, width), lambda i: (0, 0)))

    def body(*refs):
        i = pl.program_id(0)
        res = fn(i, *[r[...] for r in refs[:nr + npar]])
        outs = refs[nr + npar + n_after:]
        for o, v in zip(outs[:nro], res[:nro]):
            o[...] = v.astype(o.dtype)
        if acc_outs:
            @pl.when(i == 0)
            def _():
                for o in outs[nro:]:
                    o[...] = jnp.zeros_like(o)

            for o, v in zip(outs[nro:], res[nro:]):
                o[...] += v

    return pl.pallas_call(
        body, name=name, out_shape=out_shape, grid=(rows // tm,), in_specs=in_specs + [ANY_SPEC] * n_after,
        out_specs=out_specs, compiler_params=_cparams("arbitrary"),
    )(*args, *after)


def _rms(x, g):
    return x * lax.rsqrt(jnp.mean(x * x, axis=-1, keepdims=True) + EPS) * g


def _normmod(x, g, sc, sh):
    return _rms(x, g) * (1.0 + sc) + sh


def _gelu(x):
    return 0.5 * x * (1.0 + jnp.tanh(0.7978845608028654 * (x + 0.044715 * (x * x * x))))


def _sigmoid(x):
    return 0.5 * (jnp.tanh(0.5 * x) + 1.0)


def _coeff_parts(pre_a, pre_x, ba, bx, lam):
    r = _sigmoid(pre_a + ba)
    ig = _sigmoid(pre_x + bx)
    nl = -lam
    sp = jnp.maximum(nl, 0.0) + jnp.log(1.0 + jnp.exp(-jnp.abs(nl)))
    la = -RG_C * r * sp
    a = jnp.exp(la)
    one_minus_a2 = -jnp.tanh(la) * (a * a + 1.0)
    inv_m = lax.rsqrt(one_minus_a2)
    return r, ig, sp, a, one_minus_a2 * inv_m, inv_m


def _coeff(pre_a, pre_x, u, ba, bx, lam):
    _, ig, _, a, m, _ = _coeff_parts(pre_a, pre_x, ba, bx, lam)
    return a, m * (ig * u)


def _coeff_bwd(pre_a, pre_x, u, ba, bx, lam, da, db):
    r, ig, sp, a, m, inv_m = _coeff_parts(pre_a, pre_x, ba, bx, lam)
    dbu = db * u
    dig = dbu * m
    dm = dbu * ig
    dla = a * (da - dm * a * inv_m)
    dpa = dla * (-RG_C * sp) * (r * (1.0 - r))
    dpx = dig * (ig * (1.0 - ig))
    dsp = jnp.sum(dla * (-RG_C * r), axis=0, keepdims=True)
    dlam = -dsp * _sigmoid(-lam)
    return (dpa, dpx, db * m * ig, jnp.sum(dpa, axis=0, keepdims=True), jnp.sum(dpx, axis=0, keepdims=True), dlam)


SCAN_CHUNK = 256


def _scan_call(a, v, chunk_of, reverse, name, backward, after=()):
    rows, width = a.shape
    n_out = 1 if backward else 2
    nt = SCAN_CHUNK // 8

    def body(a_ref, v_ref, *rest):
        outs, state_ref = rest[len(after):-1], rest[-1]

        @pl.when(pl.program_id(0) == 0)
        def _():
            state_ref[...] = jnp.zeros_like(state_ref)

        rid = lax.broadcasted_iota(jnp.int32, (8, width), 0)
        last_row = 0 if reverse else 7

        def shift(x, s, fill):
            rolled = pltpu.roll(x, (8 - s) if reverse else s, axis=0)
            return jnp.where((rid >= 8 - s) if reverse else (rid < s), fill, rolled)

        def tile(j, st):
            t0 = pl.multiple_of((nt - 1 - j if reverse else j) * 8, 8)
            at = a_ref[pl.ds(t0, 8), :]
            coef = shift(at, 1, 1.0) if backward else at
            acc = v_ref[pl.ds(t0, 8), :]
            for s in (1, 2, 4):
                acc = coef * shift(acc, s, 0.0) + acc
                coef = coef * shift(coef, s, 1.0)
            out = coef * st + acc
            outs[0][pl.ds(t0, 8), :] = out
            last = out[last_row:last_row + 1]
            if backward:
                return at[last_row:last_row + 1] * last
            outs[1][pl.ds(t0, 8), :] = shift(out, 1, st)
            return last

        state_ref[0:1, :] = lax.fori_loop(0, nt, tile, state_ref[0:1, :])

    spec = pl.BlockSpec((SCAN_CHUNK, width), lambda t: (chunk_of(t), 0))
    return pl.pallas_call(
        body, name=name, out_shape=[jax.ShapeDtypeStruct((rows, width), f32)] * n_out,
        grid=(rows // SCAN_CHUNK,), in_specs=[spec, spec] + [ANY_SPEC] * len(after), out_specs=[spec] * n_out,
        scratch_shapes=[pltpu.VMEM((8, width), f32)],
        compiler_params=_cparams("arbitrary"),
    )(a, v, *after)


CONV_CHUNK = 256


def _fill_padded(pad_ref, src_ref, start, n):
    cb = pad_ref.shape[1]
    pad_ref[pl.ds(0, HALO), :] = jnp.zeros((HALO, cb), f32)
    pad_ref[pl.ds(HALO, n), :] = src_ref[pl.ds(start, n), :].astype(f32)
    pad_ref[pl.ds(HALO + n, HALO), :] = jnp.zeros((HALO, cb), f32)


def _dwconv_fwd(x, x_cb0, w, b, taps, pad_left, segments, cb, name, emit_bf16):
    rows = x.shape[0]
    width = w.shape[1]

    def body(x_ref, w_ref, b_ref, *rest):
        outs, xp = rest[:-1], rest[-1]
        for start, n in segments:
            _fill_padded(xp, x_ref, start, n)
            for c0 in range(0, n, CONV_CHUNK):
                acc = jnp.zeros((CONV_CHUNK, cb), f32) + b_ref[...]
                for k in range(taps):
                    acc = acc + w_ref[k:k + 1, :] * xp[pl.ds(HALO + c0 + k - pad_left, CONV_CHUNK), :]
                for o in outs:
                    o[pl.ds(start + c0, CONV_CHUNK), :] = acc.astype(o.dtype)

    out_dtypes = [f32, bf16] if emit_bf16 else [f32]
    return pl.pallas_call(
        body, name=name, out_shape=[jax.ShapeDtypeStruct((rows, width), dt) for dt in out_dtypes],
        grid=(width // cb,),
        in_specs=[pl.BlockSpec((rows, cb), lambda j: (0, j + x_cb0)), pl.BlockSpec((taps, cb), lambda j: (0, j)),
                  pl.BlockSpec((1, cb), lambda j: (0, j))],
        out_specs=[pl.BlockSpec((rows, cb), lambda j: (0, j))] * len(out_dtypes),
        scratch_shapes=[pltpu.VMEM((rows + 2 * HALO, cb), f32)],
        compiler_params=_cparams("parallel"),
    )(x, w, b)


def _dwconv_bwd(douts, x, x_cb0, w, taps, pad_left, segments, cb, name, dx_dtype):
    rows = x.shape[0]
    width = w.shape[1]
    nd = len(douts)

    def body(*refs):
        d_refs, x_ref, w_ref = refs[:nd], refs[nd], refs[nd + 1]
        dx_ref, dw_ref, db_ref, dp, dsum = refs[nd + 2:]
        dw_ref[...] = jnp.zeros_like(dw_ref)
        db_ref[...] = jnp.zeros_like(db_ref)
        if nd > 1:
            total = d_refs[0][...]
            for r in d_refs[1:]:
                total = total + r[...]
            dsum[...] = total
            d_ref = dsum
        else:
            d_ref = d_refs[0]
        for start, n in segments:
            _fill_padded(dp, d_ref, start, n)
            for c0 in range(0, n, CONV_CHUNK):
                db_ref[...] += jnp.sum(dp[pl.ds(HALO + c0, CONV_CHUNK), :], axis=0, keepdims=True)
                xchunk = x_ref[pl.ds(start + c0, CONV_CHUNK), :].astype(f32)
                acc = jnp.zeros((CONV_CHUNK, cb), f32)
                for k in range(taps):
                    shifted = dp[pl.ds(HALO + c0 + pad_left - k, CONV_CHUNK), :]
                    acc = acc + w_ref[k:k + 1, :] * shifted
                    dw_ref[k:k + 1, :] += jnp.sum(shifted * xchunk, axis=0, keepdims=True)
                dx_ref[pl.ds(start + c0, CONV_CHUNK), :] = acc.astype(dx_ref.dtype)

    dspec = pl.BlockSpec((rows, cb), lambda j: (0, j))
    return pl.pallas_call(
        body, name=name,
        out_shape=[jax.ShapeDtypeStruct((rows, width), dx_dtype), jax.ShapeDtypeStruct((taps, width), f32),
                   jax.ShapeDtypeStruct((1, width), f32)],
        grid=(width // cb,),
        in_specs=[dspec] * nd + [pl.BlockSpec((rows, cb), lambda j: (0, j + x_cb0)),
                                 pl.BlockSpec((taps, cb), lambda j: (0, j))],
        out_specs=[dspec, pl.BlockSpec((taps, cb), lambda j: (0, j)), pl.BlockSpec((1, cb), lambda j: (0, j))],
        scratch_shapes=[pltpu.VMEM((rows + 2 * HALO, cb), f32), pltpu.VMEM((rows, cb), f32)],
        compiler_params=_cparams("parallel"),
    )(*douts, x, w)


def _ada_forward(c16, w_ada, b_loc):
    def body(c_ref, w_ref, b_ref, o_ref):
        cv = c_ref[...]
        s = (cv * _sigmoid(cv)).astype(bf16)
        o_ref[0] = jnp.dot(s, w_ref[0].astype(bf16), preferred_element_type=f32) + b_ref[0]

    return pl.pallas_call(
        body, name="ada_forward", out_shape=jax.ShapeDtypeStruct((2, 16, ADA_SHARD), f32), grid=(2,),
        in_specs=[pl.BlockSpec((16, D), lambda l: (0, 0)), pl.BlockSpec((1, D, ADA_SHARD), lambda l: (l, 0, 0)),
                  pl.BlockSpec((1, 1, ADA_SHARD), lambda l: (l, 0, 0))],
        out_specs=pl.BlockSpec((1, 16, ADA_SHARD), lambda l: (l, 0, 0)),
        compiler_params=_cparams("parallel"),
    )(c16, w_ada, b_loc)


def _ada_backward(c16, g16, w_ada):
    def body(c_ref, g_ref, w_ref, dw_ref, ds_ref):
        cv = c_ref[...]
        s = (cv * _sigmoid(cv)).astype(bf16)
        g = g_ref[0].astype(bf16)
        dw_ref[0] = lax.dot_general(s, g, (((0,), (0,)), ((), ())), preferred_element_type=f32)
        ds = lax.dot_general(g, w_ref[0].astype(bf16), (((1,), (1,)), ((), ())), preferred_element_type=f32)
        cc = cv[8:9]
        sg = _sigmoid(cc)
        dsilu = sg * (1.0 + cc * (1.0 - sg))
        ds_ref[0] = jnp.zeros((8, D), f32) + jnp.sum(ds[8:16], axis=0, keepdims=True) * dsilu

    return pl.pallas_call(
        body, name="ada_backward",
        out_shape=[jax.ShapeDtypeStruct((2, D, ADA_SHARD), f32), jax.ShapeDtypeStruct((2, 8, D), f32)], grid=(2,),
        in_specs=[pl.BlockSpec((16, D), lambda l: (0, 0)), pl.BlockSpec((1, 16, ADA_SHARD), lambda l: (l, 0, 0)),
                  pl.BlockSpec((1, D, ADA_SHARD), lambda l: (l, 0, 0))],
        out_specs=[pl.BlockSpec((1, D, ADA_SHARD), lambda l: (l, 0, 0)), pl.BlockSpec((1, 8, D), lambda l: (l, 0, 0))],
        compiler_params=_cparams("parallel"),
    )(c16, g16, w_ada)


def _adamw(pieces, w, m, v, name, after=()):
    rows, cols = w.shape
    n_arr, n_after = len(pieces), len(after)
    tm = 256 if (rows % 256 == 0 and rows > 256) else rows
    counts = [p[1] for p in pieces]
    first_tiles = [(p[2] if len(p) > 2 else 0) // tm for p in pieces]
    pieces = [p[0] for p in pieces]

    def body(*refs):
        p_refs = refs[:n_arr]
        w_ref, m_ref, v_ref = refs[n_arr:n_arr + 3]
        g_ref, d_ref, nm_ref, nv_ref = refs[n_arr + 3 + n_after:]
        g = None
        for p_ref in p_refs:
            for j in range(p_ref.shape[0]):
                term = p_ref[j].astype(f32)
                g = term if g is None else g + term
        m2 = ADAM_B1 * m_ref[...] + (1.0 - ADAM_B1) * g
        v2 = ADAM_B2 * v_ref[...] + (1.0 - ADAM_B2) * (g * g)
        m_hat = m2 / (1.0 - ADAM_B1 ** ADAM_STEP)
        v_hat = v2 / (1.0 - ADAM_B2 ** ADAM_STEP)
        g_ref[...] = g
        d_ref[...] = -ADAM_LR * (m_hat / (jnp.sqrt(v_hat) + ADAM_EPS) + ADAM_WD * w_ref[...])
        nm_ref[...] = m2
        nv_ref[...] = v2

    spec = pl.BlockSpec((tm, cols), lambda i: (i, 0))
    return pl.pallas_call(
        body, name=name, out_shape=[jax.ShapeDtypeStruct((rows, cols), f32)] * 4, grid=(rows // tm,),
        in_specs=[pl.BlockSpec((cnt, tm, cols), lambda i, t=t: (0, i + t, 0)) for cnt, t in zip(counts, first_tiles)]
        + [spec, spec, spec]
        + [ANY_SPEC] * n_after,
        out_specs=[spec] * 4, compiler_params=_cparams("parallel"),
    )(*pieces, w, m, v, *after)


MLP_TM = 256
FB = F // N_DEV


def _stack_rows(vals, n):
    cols = vals[0].shape[1]
    rid = lax.broadcasted_iota(jnp.int32, (n, cols), 0)
    out = jnp.zeros((n, cols), f32)
    for k, v in enumerate(vals):
        out = jnp.where(rid == k, v, out)
    return out


N_MLP_PARAMS = 9


class _ParamRows:
    def __init__(self, ref):
        self.ref = ref

    def __getitem__(self, sl):
        return self.ref[8 * sl.start:8 * sl.start + 1, :]


def _resident(shape, imap):
    return pl.BlockSpec(shape, imap, pipeline_mode=pl.Buffered(1))


def _mlp_forward(xa, xa_roff, out_prev, par, w_in, w_out, layer, name):
    def body(xa_ref, op_ref, par_ref, win_ref, wout_ref, x1_ref, h_ref, r_ref, mo_ref, x2_ref, hn_ref):
        p = _ParamRows(par_ref)
        x1 = xa_ref[...] + p[0:1] * (op_ref[...] + p[1:2])
        h = _normmod(x1, p[2:3], p[3:4], p[4:5]).astype(bf16)
        x1_ref[...] = x1
        h_ref[...] = h
        mo = jnp.zeros((MLP_TM, D), f32)
        for j in range(N_DEV):
            r = jnp.maximum(jnp.dot(h, win_ref[j], preferred_element_type=f32), 0.0)
            r_ref[:, j * FB:(j + 1) * FB] = r.astype(bf16)
            mo = mo + jnp.dot((r * r).astype(bf16), wout_ref[j], preferred_element_type=f32)
        mo_ref[...] = mo.astype(bf16)
        x2 = x1 + p[5:6] * mo
        x2_ref[...] = x2
        hn_ref[...] = _normmod(x2, p[6:7], p[7:8], p[8:9]).astype(bf16)

    row = lambda width: pl.BlockSpec((MLP_TM, width), lambda i: (i, 0))
    return pl.pallas_call(
        body, name=name, grid=(T_LAT // MLP_TM,),
        out_shape=[jax.ShapeDtypeStruct((T_LAT, D), f32), jax.ShapeDtypeStruct((T_LAT, D), bf16),
                   jax.ShapeDtypeStruct((T_LAT, F), bf16), jax.ShapeDtypeStruct((T_LAT, D), bf16),
                   jax.ShapeDtypeStruct((T_LAT, D), f32), jax.ShapeDtypeStruct((T_LAT, D), bf16)],
        in_specs=[pl.BlockSpec((MLP_TM, D), lambda i: (i + xa_roff, 0)), row(D), pl.BlockSpec((8 * N_MLP_PARAMS, D), lambda i: (0, 0)),
                  _resident((N_DEV, None, D, FB), lambda i: (0, layer, 0, 0)),
                  _resident((N_DEV, None, FB, D), lambda i: (0, layer, 0, 0))],
        out_specs=[row(D), row(D), row(F), row(D), row(D), row(D)],
        compiler_params=_cparams("parallel"),
    )(xa, out_prev, par, w_in, w_out)


def _mlp_backward(dx2, x1, r, mo, out_prev, par, w_in, w_out, layer, name, after=()):
    nt = (((1,), (1,)), ((), ()))

    n_after = len(after)

    def body(dx2_ref, x1_ref, r_ref, mo_ref, op_ref, par_ref, win_ref, wout_ref, *rest):
        dx1_ref, dop_ref, dmo_ref, dhid_ref, acc_ref = rest[n_after:]
        p = _ParamRows(par_ref)
        dx2v = dx2_ref[...]
        dmo = (p[5:6] * dx2v).astype(bf16)
        dmo_ref[...] = dmo
        dh = jnp.zeros((MLP_TM, D), f32)
        mo = mo_ref[...].astype(f32)
        for j in range(N_DEV):
            rf = r_ref[:, j * FB:(j + 1) * FB].astype(f32)
            dact = lax.dot_general(dmo, wout_ref[j], nt, preferred_element_type=f32)
            dhid = (dact * (2.0 * rf)).astype(bf16)
            dhid_ref[:, j * FB:(j + 1) * FB] = dhid
            dh = dh + lax.dot_general(dhid, win_ref[j], nt, preferred_element_type=f32)
        x1 = x1_ref[...]
        _, vjp = jax.vjp(_normmod, x1, p[2:3], p[3:4], p[4:5])
        dx, dng, dsc, dsh = vjp(dh)
        dx1 = dx2v + dx
        dx1_ref[...] = dx1
        dop_ref[...] = (p[0:1] * dx1).astype(bf16)
        sums = _stack_rows([jnp.sum(dx1 * (op_ref[...] + p[1:2]), axis=0, keepdims=True),
                            p[0:1] * jnp.sum(dx1, axis=0, keepdims=True), dng, dsc, dsh,
                            jnp.sum(dx2v * mo, axis=0, keepdims=True)], 8)

        @pl.when(pl.program_id(0) == 0)
        def _():
            acc_ref[...] = jnp.zeros_like(acc_ref)

        acc_ref[...] += sums

    row = lambda width: pl.BlockSpec((MLP_TM, width), lambda i: (i, 0))
    return pl.pallas_call(
        body, name=name, grid=(T_LAT // MLP_TM,),
        out_shape=[jax.ShapeDtypeStruct((T_LAT, D), f32), jax.ShapeDtypeStruct((T_LAT, D), bf16),
                   jax.ShapeDtypeStruct((T_LAT, D), bf16), jax.ShapeDtypeStruct((T_LAT, F), bf16),
                   jax.ShapeDtypeStruct((8, D), f32)],
        in_specs=[row(D), row(D), row(F), row(D), row(D), pl.BlockSpec((8 * N_MLP_PARAMS, D), lambda i: (0, 0)),
                  _resident((N_DEV, None, D, FB), lambda i: (0, layer, 0, 0)),
                  _resident((N_DEV, None, FB, D), lambda i: (0, layer, 0, 0))] + [ANY_SPEC] * n_after,
        out_specs=[row(D), row(D), row(D), row(F), pl.BlockSpec((8, D), lambda i: (0, 0))],
        compiler_params=_cparams("arbitrary"),
    )(dx2, x1, r, mo, out_prev, par, w_in, w_out, *after)


def _mlp_weight_grads(h, dhid, r, dmo, layer, other, tag):
    tn = (((0,), (0,)), ((), ()))

    def body_in(h_ref, dhid_ref, *rest):
        rest[-1][...] = lax.dot_general(h_ref[...], dhid_ref[...], tn, preferred_element_type=f32).astype(bf16)

    def body_out(r_ref, dmo_ref, *rest):
        rf = r_ref[...].astype(f32)
        rest[-1][...] = lax.dot_general((rf * rf).astype(bf16), dmo_ref[...], tn,
                                        preferred_element_type=f32).astype(bf16)

    def call(body, name, operands, specs, block, prev):
        extra = [] if prev is None else [prev]
        return pl.pallas_call(
            body, name=name, grid=(N_DEV,), out_shape=jax.ShapeDtypeStruct((N_DEV, 2) + block, bf16),
            in_specs=specs + [pl.BlockSpec(memory_space=pl.ANY)] * len(extra),
            out_specs=pl.BlockSpec((None, None) + block, lambda j: (j, layer, 0, 0)),
            input_output_aliases={} if prev is None else {2: 0},
            compiler_params=_cparams("parallel"),
        )(*operands, *extra)

    dw_in = call(body_in, tag + "_mlp_in_dw", [h, dhid],
                 [_resident((T_LAT, D), lambda j: (0, 0)), pl.BlockSpec((T_LAT, FB), lambda j: (0, j))], (D, FB),
                 None if other is None else other[0])
    dw_out = call(body_out, tag + "_mlp_out_dw", [r, dmo],
                  [pl.BlockSpec((T_LAT, FB), lambda j: (0, j)), _resident((T_LAT, D), lambda j: (0, 0))], (FB, D),
                  None if other is None else other[1])
    return dw_in, dw_out


def _pos_embed():
    n_rows = T_LAT // GRID_W
    q = D // 4
    omega = 1.0 / (POS_BASE ** (jnp.arange(q, dtype=f32) / q))
    er = jnp.arange(n_rows, dtype=jnp.int32).astype(f32)[:, None] * omega[None, :]
    ec = jnp.arange(GRID_W, dtype=jnp.int32).astype(f32)[:, None] * omega[None, :]
    by_row = jnp.concatenate([jnp.sin(er), jnp.cos(er)], axis=-1)[:, None, :]
    by_col = jnp.concatenate([jnp.sin(ec), jnp.cos(ec)], axis=-1)[None, :, :]
    full = jnp.concatenate([jnp.broadcast_to(by_row, (n_rows, GRID_W, D // 2)),
                            jnp.broadcast_to(by_col, (n_rows, GRID_W, D // 2))], axis=-1)
    return full.reshape(T_LAT, D)


HALF = R // 2
BLK_PER_HALF = N_BLK // 2
N_PARTS = 4


def _gate_matrix(w_a, w_x):
    eye = jnp.eye(BLK_PER_HALF, dtype=bf16)
    cols = []
    for h in range(2):
        for d in range(2):
            for w in (w_a, w_x):
                blocks = w[d, BLK_PER_HALF * h:BLK_PER_HALF * (h + 1)].astype(bf16)
                cols.append(jnp.einsum("hij,hg->higj", blocks, eye).reshape(HALF, HALF))
    return jnp.concatenate(cols, axis=1)


def _gate_blocks(dwg, part):
    out = []
    for h in range(2):
        blk = dwg[:, (N_PARTS * h + part) * HALF:(N_PARTS * h + part + 1) * HALF]
        blk = blk.reshape(BLK_PER_HALF, BLK, BLK_PER_HALF, BLK)
        out.append(jnp.moveaxis(jnp.diagonal(blk, axis1=0, axis2=2), -1, 0))
    return jnp.concatenate(out, axis=0)


GATE_BM = 768


def _gates_dx(dpre, wg, after=()):
    rows = dpre.shape[0]
    n_after = len(after)

    def body(d_ref, w_ref, *rest):
        rest[n_after][...] = lax.dot_general(d_ref[...], w_ref[...], (((1,), (1,)), ((), ())),
                                             preferred_element_type=f32)

    return pl.pallas_call(
        body, name="l0_gates_dx", grid=(rows // GATE_BM, 2), out_shape=jax.ShapeDtypeStruct((rows, R), f32),
        in_specs=[pl.BlockSpec((GATE_BM, N_PARTS * HALF), lambda i, h: (i, h)),
                  pl.BlockSpec((HALF, N_PARTS * HALF), lambda i, h: (0, h))] + [ANY_SPEC] * n_after,
        out_specs=pl.BlockSpec((GATE_BM, HALF), lambda i, h: (i, h)),
        compiler_params=_cparams("parallel", "parallel"),
    )(dpre, wg, *after)


COEFF_TM = 256


def _dir_params(d, *params):
    specs = [pl.BlockSpec((None, 1, HALF), lambda h, i: (d, 0, h))] * len(params)
    return specs, [p.reshape(2, 1, R) for p in params]


def _gates_coeff_fwd(ub, u, wg, ba, bx, lam, d):
    rows = u.shape[0]

    def body(ub_ref, u_ref, w_ref, ba_ref, bx_ref, lam_ref, a_ref, b_ref):
        pre = jnp.dot(ub_ref[...], w_ref[...], preferred_element_type=f32)
        a, b = _coeff(pre[:, :HALF], pre[:, HALF:], u_ref[...], ba_ref[...], bx_ref[...], lam_ref[...])
        a_ref[...] = a
        b_ref[...] = b

    tile = pl.BlockSpec((COEFF_TM, HALF), lambda h, i: (i, h))
    pspecs, pargs = _dir_params(d, ba, bx, lam)
    return pl.pallas_call(
        body, name=f"l0_gates_coeff_{d}", grid=(2, rows // COEFF_TM),
        out_shape=[jax.ShapeDtypeStruct((rows, R), f32)] * 2,
        in_specs=[tile, tile, pl.BlockSpec((HALF, 2 * HALF), lambda h, i: (0, 2 * h + d))] + pspecs,
        out_specs=[tile, tile], compiler_params=_cparams("parallel", "parallel"),
    )(ub, u, wg, *pargs)


def _gates_coeff_bwd(ub, u, dh, yp, wg, ba, bx, lam, d, dpre_prev):
    rows = u.shape[0]
    n_prev = 0 if dpre_prev is None else 1

    def body(ub_ref, u_ref, dh_ref, yp_ref, w_ref, ba_ref, bx_ref, lam_ref, *rest):
        dpre_ref, du_ref, dba_ref, dbx_ref, dlam_ref = rest[n_prev:]
        pre = jnp.dot(ub_ref[...], w_ref[...], preferred_element_type=f32)
        dhv = dh_ref[...]
        dpa, dpx, du, dba, dbx, dlam = _coeff_bwd(pre[:, :HALF], pre[:, HALF:], u_ref[...], ba_ref[...], bx_ref[...],
                                                  lam_ref[...], dhv * yp_ref[...], dhv)
        dpre_ref[:, :HALF] = dpa.astype(bf16)
        dpre_ref[:, HALF:] = dpx.astype(bf16)
        du_ref[...] = du

        @pl.when(pl.program_id(1) == 0)
        def _():
            dba_ref[...] = jnp.zeros_like(dba_ref)
            dbx_ref[...] = jnp.zeros_like(dbx_ref)
            dlam_ref[...] = jnp.zeros_like(dlam_ref)

        dba_ref[...] += dba
        dbx_ref[...] += dbx
        dlam_ref[...] += dlam

    tile = pl.BlockSpec((COEFF_TM, HALF), lambda h, i: (i, h))
    acc = pl.BlockSpec((1, HALF), lambda h, i: (0, h))
    pspecs, pargs = _dir_params(d, ba, bx, lam)
    extra = [] if dpre_prev is None else [dpre_prev]
    return pl.pallas_call(
        body, name=f"l0_gates_coeff_bwd_{d}", grid=(2, rows // COEFF_TM),
        out_shape=[jax.ShapeDtypeStruct((rows, 2 * N_PARTS * HALF), bf16), jax.ShapeDtypeStruct((rows, R), f32)]
        + [jax.ShapeDtypeStruct((1, R), f32)] * 3,
        in_specs=[tile] * 4 + [pl.BlockSpec((HALF, 2 * HALF), lambda h, i: (0, 2 * h + d))] + pspecs
        + [ANY_SPEC] * n_prev,
        out_specs=[pl.BlockSpec((COEFF_TM, 2 * HALF), lambda h, i: (i, 2 * h + d)), tile, acc, acc, acc],
        input_output_aliases={8: 0} if n_prev else {}, compiler_params=_cparams("parallel", "arbitrary"),
    )(ub, u, dh, yp, wg, *pargs, *extra)


def _gates_dw(u, dpre):
    rows = u.shape[0]

    def body(u_ref, d_ref, o_ref):
        o_ref[...] = lax.dot_general(u_ref[...], d_ref[...], (((0,), (0,)), ((), ())), preferred_element_type=f32)

    return pl.pallas_call(
        body, name="l0_gates_dw", grid=(2 * N_PARTS,), out_shape=jax.ShapeDtypeStruct((HALF, 2 * N_PARTS * HALF), f32),
        in_specs=[pl.BlockSpec((rows, HALF), lambda j: (0, j // N_PARTS)), pl.BlockSpec((rows, HALF), lambda j: (0, j))],
        out_specs=pl.BlockSpec((HALF, HALF), lambda j: (0, j)), compiler_params=_cparams("parallel"),
    )(u, dpre)


N_SCAN_CHUNKS = T_ALL // SCAN_CHUNK
SCAN_FWD = lambda t: t
SCAN_FWD_BWD = lambda t: N_SCAN_CHUNKS - 1 - t
SCAN_REV = lambda t: jnp.where(t == 0, 0, N_SCAN_CHUNKS - t)
SCAN_REV_BWD = lambda t: jnp.where(t == N_SCAN_CHUNKS - 1, 0, t + 1)
CONV_SEGMENTS = ((0, T_CTX), (T_CTX, T_LAT))
FUSED_TM = 256


def _token_rows(x, ctx):
    return (jnp.concatenate([ctx, x], axis=0),
            jnp.concatenate([jnp.zeros((T_CTX, D), f32), _pos_embed()], axis=0))


def _local_step(xcat, poscat, target, mods, cmod, wts, late_weights, send_grads, reduce_loss, start_after=()):
    sh1, sc1, g1, sh2, sc2, g2 = [[mods[l, i][None] for l in range(2)] for i in range(N_MOD)]
    ng = wts["norm_g"]
    scp = jnp.concatenate([cmod[1][None], sc1[0]], axis=0)
    shp = jnp.concatenate([cmod[0][None], sh1[0]], axis=0)

    ctx_tiles = T_CTX // FUSED_TM
    nt = (((1,), (1,)), ((), ()))

    def blend(i, p):
        sel = jnp.where(i < ctx_tiles, 1.0, 0.0)
        return sel * p[0:1] + (1.0 - sel) * p[1:2]

    def f_pre0(i, xc, pos, g, scp_, shp_, w):
        x0 = xc + pos
        h = _normmod(x0, g, blend(i, scp_), blend(i, shp_)).astype(bf16)
        return x0, h, jnp.dot(h, w, preferred_element_type=f32)

    x0cat, h0, gr = _rowcall(f_pre0, "l0_prenorm_in_proj", T_ALL, FUSED_TM, [_rin(xcat), _rin(poscat)],
                             [ng[0, 0][None], scp, shp, wts["rec_w_in"]], [(D, f32), (D, bf16), (2 * R, f32)],
                             after=start_after)
    u, ub = _dwconv_fwd(gr, R // 256, wts["rec_conv_w"], wts["rec_conv_b"], 4, 1, CONV_SEGMENTS, 256,
                        "l0_conv", True)
    gate_args = (wts["gates"], wts["rec_b_a"], wts["rec_b_x"], wts["rec_lambda"])
    a0, b0 = _gates_coeff_fwd(ub, u, *gate_args, 0)
    a1, b1 = _gates_coeff_fwd(ub, u, *gate_args, 1)
    halfway = late_weights("mlp_halfway", a1)
    y0, yp0 = _scan_call(a0, b0, SCAN_FWD, False, "l0_scan_fwd", False, after=[halfway])
    y1, yp1 = _scan_call(a1, b1, SCAN_REV, True, "l0_scan_rev", False)

    wts = dict(wts, **late_weights("mlp", y1))

    def f_gate_out(i, gp, y0_, y1_, w):
        z = (_gelu(gp) * (y0_ + y1_)).astype(bf16)
        return z, jnp.dot(z, w, preferred_element_type=f32)

    zb, out0 = _rowcall(f_gate_out, "l0_gate_out_proj", T_LAT, FUSED_TM,
                        [_rin(gr, R, 0, ctx_tiles), _rin(y0, None, 0, ctx_tiles), _rin(y1, None, 0, ctx_tiles)],
                        [wts["rec_w_out"]], [(R, bf16), (D, f32)])

    zero_d = jnp.zeros((1, D), f32)

    def mlp_params(rows):
        rows = rows + [zero_d] * (N_MLP_PARAMS - len(rows))
        return jnp.concatenate([jnp.broadcast_to(r, (8, D)) for r in rows], axis=0)

    par0 = mlp_params([g1[0], zero_d, ng[0, 1][None], sc2[0], sh2[0], g2[0], ng[1, 0][None], sc1[1], sh1[1]])
    x1, h1, r0, mo0, x2, h2 = _mlp_forward(x0cat, T_CTX // MLP_TM, out0, par0, wts["mlp_w_in"], wts["mlp_w_out"], 0,
                                           "l0_mlp")

    wts = dict(wts, **late_weights("conf", x2))
    def glu(pa, pb, b1):
        return (pa + b1[:, :D]) * _sigmoid(pb + b1[:, D:])

    def f_pw1_glu(i, h_, b1, w):
        p = jnp.dot(h_, w, preferred_element_type=f32)
        return glu(p[:, :D], p[:, D:], b1), p

    zg, pw = _rowcall(f_pw1_glu, "l1_pw1_glu", T_LAT, FUSED_TM, [_rin(h2)], [wts["conf_b_pw1"], wts["conf_w_pw1"]],
                      [(D, f32), (2 * D, bf16)])
    (zc,) = _dwconv_fwd(zg, 0, wts["conf_conv_w"], wts["conf_conv_b"], 31, 15, ((0, T_LAT),), 128, "l1_conv", False)

    def ln_silu(z, lg, lb):
        mu = jnp.mean(z, axis=-1, keepdims=True)
        zc_ = z - mu
        var = jnp.mean(zc_ * zc_, axis=-1, keepdims=True)
        yv = zc_ * lax.rsqrt(var + EPS) * lg + lb
        return yv * _sigmoid(yv)

    def f_lnsilu_pw2(i, z, lg, lb, w):
        s = ln_silu(z, lg, lb).astype(bf16)
        return s, jnp.dot(s, w, preferred_element_type=f32)

    sb, out1 = _rowcall(f_lnsilu_pw2, "l1_ln_silu_pw2", T_LAT, FUSED_TM, [_rin(zc)],
                        [wts["conf_ln_g"], wts["conf_ln_b"], wts["conf_w_pw2"]], [(D, bf16), (D, f32)])
    par1 = mlp_params([g1[1], wts["conf_b_pw2"], ng[1, 1][None], sc2[1], sh2[1], g2[1]])
    x3, h3, r1, mo1, x4, _ = _mlp_forward(x2, 0, out1, par1, wts["mlp_w_in"], wts["mlp_w_out"], 1, "l1_mlp")

    def loss_fn(x4_, fg, tgt):
        err = _rms(x4_, fg) - tgt
        per_row = jnp.mean(err * err, axis=-1, keepdims=True)
        return 0.5 * jnp.sum(per_row, axis=0, keepdims=True)

    def f_head(i, x4_, tgt, fg):
        loss, vjp = jax.vjp(lambda a, e: loss_fn(a, e, tgt), x4_, fg)
        dx, dfg = vjp(jnp.ones((1, 1), f32))
        return dx, jnp.broadcast_to(loss, (1, 128)), dfg

    dx4, loss_acc, dfinal_g = _rowcall(f_head, "head", T_LAT, FUSED_TM, [_rin(x4), _rin(target)], [wts["final_g"]],
                                       [(D, f32)], [(1, 128), (1, D)])

    grads = {"final_g": dfinal_g}
    loss = reduce_loss(loss_acc[0, 0])

    dx3, dout1, dmo1, dhid1, acc1 = _mlp_backward(dx4, x3, r1, mo1, out1, par1, wts["mlp_w_in"], wts["mlp_w_out"], 1,
                                                  "l1_mlp_bwd", after=[loss.reshape(1, 1)])
    mlp_dw = _mlp_weight_grads(h3, dhid1, r1, dmo1, 1, None, "l1")
    dg1_1, db_pw2, dng11, dsc2_1, dsh2_1, dg2_1 = [acc1[k:k + 1] for k in range(6)]

    grads["conf_w_pw2"] = _mm(sb, dout1, "l1_pw2_dw", ta=True, out_dtype=bf16)
    grads["conf_b_pw2"] = db_pw2

    def f_pw2_lnsilu_bwd(i, z, dout, lg, lb, w):
        ds = lax.dot_general(dout, w, nt, preferred_element_type=f32)
        _, vjp = jax.vjp(ln_silu, z, lg, lb)
        return vjp(ds)

    dzc, dln_g, dln_b = _rowcall(f_pw2_lnsilu_bwd, "l1_pw2_ln_silu_bwd", T_LAT, FUSED_TM, [_rin(zc), _rin(dout1)],
                                 [wts["conf_ln_g"], wts["conf_ln_b"], wts["conf_w_pw2"]], [(D, f32)], [(1, D)] * 2)
    grads["conf_ln_g"], grads["conf_ln_b"] = dln_g, dln_b
    dzg, dconv_w, dconv_b = _dwconv_bwd([dzc], zg, 0, wts["conf_conv_w"], 31, 15, ((0, T_LAT),), 128,
                                        "l1_conv_bwd", f32)
    grads["conf_conv_w"], grads["conf_conv_b"] = dconv_w, dconv_b

    def f_glu_pw1_norm_bwd(i, p_, dz, x_, dxs, b1, g_, sc_, sh_, w):
        pf = p_.astype(f32)
        _, vjp = jax.vjp(glu, pf[:, :D], pf[:, D:], b1)
        da, db, db1 = vjp(dz)
        dp = jnp.concatenate([da, db], axis=1).astype(bf16)
        dh = lax.dot_general(dp, w, nt, preferred_element_type=f32)
        _, vjp = jax.vjp(_normmod, x_, g_, sc_, sh_)
        dx, dg, dsc, dsh = vjp(dh)
        return dp, dx + dxs, db1, dg, dsc, dsh

    dpw, dx2, db_pw1, dng10, dsc1_1, dsh1_1 = _rowcall(
        f_glu_pw1_norm_bwd, "l1_glu_pw1_normmod_bwd", T_LAT, FUSED_TM, [_rin(pw), _rin(dzg), _rin(x2), _rin(dx3)],
        [wts["conf_b_pw1"], ng[1, 0][None], sc1[1], sh1[1], wts["conf_w_pw1"]], [(2 * D, bf16), (D, f32)],
        [(1, 2 * D), (1, D), (1, D), (1, D)])
    grads["conf_b_pw1"] = db_pw1
    grads["conf_w_pw1"] = _mm(h2, dpw, "l1_pw1_dw", ta=True, out_dtype=bf16)
    sent = send_grads(["conf_w_pw2", "conf_w_pw1"], grads)

    dx1, dout0, dmo0, dhid0, acc0 = _mlp_backward(dx2, x1, r0, mo0, out0, par0, wts["mlp_w_in"], wts["mlp_w_out"], 0,
                                                  "l0_mlp_bwd", after=[sent])
    grads["mlp_w_in"], grads["mlp_w_out"] = _mlp_weight_grads(h1, dhid0, r0, dmo0, 0, mlp_dw, "l0")
    sent = send_grads(["mlp_w_in", "mlp_w_out"], grads)
    dg1_0, _, dng01, dsc2_0, dsh2_0, dg2_0 = [acc0[k:k + 1] for k in range(6)]

    grads["rec_w_out"] = _mm(zb, dout0, "l0_out_proj_dw", ta=True, out_dtype=bf16, after=[sent])
    sent = send_grads(["rec_w_out"], grads)

    def f_out_gate_bwd(i, gp, y0_, y1_, dout, w):
        lat = jnp.where(i < ctx_tiles, 0.0, 1.0)
        dz = lax.dot_general(dout, w, nt, preferred_element_type=f32)
        _, vjp = jax.vjp(lambda a, b: _gelu(a) * b, gp, y0_ + y1_)
        dgp, dy = vjp(dz)
        return dgp * lat, dy * lat

    dgp, dy = _rowcall(f_out_gate_bwd, "l0_out_proj_gate_bwd", T_ALL, FUSED_TM,
                       [_rin(gr, R, 0), _rin(y0), _rin(y1), _rin(dout0, None, 0, -ctx_tiles)], [wts["rec_w_out"]],
                       [(R, bf16), (R, f32)], after=[sent])
    (dh_f,) = _scan_call(a0, dy, SCAN_FWD_BWD, True, "l0_scan_fwd_bwd", True)
    (dh_r,) = _scan_call(a1, dy, SCAN_REV_BWD, False, "l0_scan_rev_bwd", True)

    dpre, du_f, *dpar_f = _gates_coeff_bwd(ub, u, dh_f, yp0, *gate_args, 0, None)
    dpre, du_r, *dpar_r = _gates_coeff_bwd(ub, u, dh_r, yp1, *gate_args, 1, dpre)
    grads["rec_b_a"], grads["rec_b_x"], grads["rec_lambda"] = [
        jnp.concatenate([f.reshape(-1), r_.reshape(-1)]).reshape(2, R) for f, r_ in zip(dpar_f, dpar_r)]
    grads["gates"] = _gates_dw(ub, dpre)
    sent = send_grads(["replicated"], grads)
    du_gates = _gates_dx(dpre, wts["gates"], after=[sent])
    drec, dconv4_w, dconv4_b = _dwconv_bwd([du_f, du_r, du_gates], gr, R // 256, wts["rec_conv_w"], 4, 1,
                                           CONV_SEGMENTS, 256, "l0_conv_bwd", bf16)
    grads["rec_conv_w"], grads["rec_conv_b"] = dconv4_w, dconv4_b
    dgr = jnp.concatenate([dgp, drec], axis=1)
    grads["rec_w_in"] = _mm(h0, dgr, "l0_in_proj_dw", ta=True, out_dtype=bf16)
    sent = send_grads(["rec_w_in"], grads)

    def f_pre0_bwd(i, x0, dgr_, dxs, g, scp_, shp_, w):
        lat = jnp.where(i < ctx_tiles, 0.0, 1.0)
        dh = lax.dot_general(dgr_, w, nt, preferred_element_type=f32)
        _, vjp = jax.vjp(lambda a, b, c, e: _normmod(a, b, blend(i, c), blend(i, e)), x0, g, scp_, shp_)
        dx, dg, dscp, dshp = vjp(dh)
        return dx + lat * dxs, dg, dscp, dshp

    dx0cat, dng00, dscp, dshp = _rowcall(
        f_pre0_bwd, "l0_in_proj_prenorm_bwd", T_ALL, FUSED_TM,
        [_rin(x0cat), _rin(dgr), _rin(dx1, None, 0, -ctx_tiles)], [ng[0, 0][None], scp, shp, wts["rec_w_in"]],
        [(D, f32)], [(1, D), (2, D), (2, D)], after=[sent])

    grads["norm_g"] = jnp.stack([jnp.concatenate([dng00, dng01], 0), jnp.concatenate([dng10, dng11], 0)])
    dmods = jnp.stack([
        jnp.concatenate([dshp[1:2], dscp[1:2], dg1_0, dsh2_0, dsc2_0, dg2_0], axis=0),
        jnp.concatenate([dsh1_1, dsc1_1, dg1_1, dsh2_1, dsc2_1, dg2_1], axis=0)])
    dcmod = jnp.concatenate([dshp[0:1], dscp[0:1]], axis=0)
    return loss, dx0cat[T_CTX:], dmods, dcmod, grads


def _unshard_cols(g):
    g = jnp.moveaxis(g, 0, -2)
    return g.reshape(g.shape[:-2] + (g.shape[-2] * g.shape[-1],))


def _shard_cols(w):
    w = w.reshape(w.shape[:-1] + (N_DEV, w.shape[-1] // N_DEV))
    return jnp.moveaxis(w, -2, 0)


def _shard_rows(w):
    return w.reshape((N_DEV, w.shape[0] // N_DEV) + w.shape[1:])


SMALL_PACK_ROWS = 64
REPL_FINAL_G_ROWS = -(-D // BLK)
REPL_ROWS = -(-(2 * 2 * N_BLK * BLK + 2 * 2 * N_BLK + REPL_FINAL_G_ROWS) // 16) * 16


def kernel(x, c, ctx, c_ctx, w_ada, b_ada, norm_g, rec_w_in, rec_conv_w, rec_conv_b, rec_lambda, rec_w_a, rec_b_a, rec_w_x, rec_b_x, rec_w_out, conf_w_pw1, conf_b_pw1, conf_conv_w, conf_conv_b, conf_ln_g, conf_ln_b, conf_w_pw2, conf_b_pw2, mlp_w_in, mlp_w_out, final_g, loss_target, m_c_ctx, m_w_ada, m_b_ada, m_norm_g, m_rec_w_in, m_rec_conv_w, m_rec_conv_b, m_rec_lambda, m_rec_w_a, m_rec_b_a, m_rec_w_x, m_rec_b_x, m_rec_w_out, m_conf_w_pw1, m_conf_b_pw1, m_conf_conv_w, m_conf_conv_b, m_conf_ln_g, m_conf_ln_b, m_conf_w_pw2, m_conf_b_pw2, m_mlp_w_in, m_mlp_w_out, m_final_g, v_c_ctx, v_w_ada, v_b_ada, v_norm_g, v_rec_w_in, v_rec_conv_w, v_rec_conv_b, v_rec_lambda, v_rec_w_a, v_rec_b_a, v_rec_w_x, v_rec_b_x, v_rec_w_out, v_conf_w_pw1, v_conf_b_pw1, v_conf_conv_w, v_conf_conv_b, v_conf_ln_g, v_conf_ln_b, v_conf_w_pw2, v_conf_b_pw2, v_mlp_w_in, v_mlp_w_out, v_final_g):
    me = 4 * lax.axis_index("x") + 2 * lax.axis_index("y") + lax.axis_index("c")
    weights = dict(c_ctx=c_ctx, w_ada=w_ada, b_ada=b_ada, norm_g=norm_g, rec_w_in=rec_w_in, rec_conv_w=rec_conv_w,
                   rec_conv_b=rec_conv_b, rec_lambda=rec_lambda, rec_w_a=rec_w_a, rec_b_a=rec_b_a, rec_w_x=rec_w_x,
                   rec_b_x=rec_b_x, rec_w_out=rec_w_out, conf_w_pw1=conf_w_pw1, conf_b_pw1=conf_b_pw1,
                   conf_conv_w=conf_conv_w, conf_conv_b=conf_conv_b, conf_ln_g=conf_ln_g, conf_ln_b=conf_ln_b,
                   conf_w_pw2=conf_w_pw2, conf_b_pw2=conf_b_pw2, mlp_w_in=mlp_w_in, mlp_w_out=mlp_w_out, final_g=final_g)
    m_in = dict(c_ctx=m_c_ctx, w_ada=m_w_ada, b_ada=m_b_ada, norm_g=m_norm_g, rec_w_in=m_rec_w_in, rec_conv_w=m_rec_conv_w,
                rec_conv_b=m_rec_conv_b, rec_lambda=m_rec_lambda, rec_w_a=m_rec_w_a, rec_b_a=m_rec_b_a, rec_w_x=m_rec_w_x,
                rec_b_x=m_rec_b_x, rec_w_out=m_rec_w_out, conf_w_pw1=m_conf_w_pw1, conf_b_pw1=m_conf_b_pw1,
                conf_conv_w=m_conf_conv_w, conf_conv_b=m_conf_conv_b, conf_ln_g=m_conf_ln_g, conf_ln_b=m_conf_ln_b,
                conf_w_pw2=m_conf_w_pw2, conf_b_pw2=m_conf_b_pw2, mlp_w_in=m_mlp_w_in, mlp_w_out=m_mlp_w_out,
                final_g=m_final_g)
    v_in = dict(c_ctx=v_c_ctx, w_ada=v_w_ada, b_ada=v_b_ada, norm_g=v_norm_g, rec_w_in=v_rec_w_in, rec_conv_w=v_rec_conv_w,
                rec_conv_b=v_rec_conv_b, rec_lambda=v_rec_lambda, rec_w_a=v_rec_w_a, rec_b_a=v_rec_b_a, rec_w_x=v_rec_w_x,
                rec_b_x=v_rec_b_x, rec_w_out=v_rec_w_out, conf_w_pw1=v_conf_w_pw1, conf_b_pw1=v_conf_b_pw1,
                conf_conv_w=v_conf_conv_w, conf_conv_b=v_conf_conv_b, conf_ln_g=v_conf_ln_g, conf_ln_b=v_conf_ln_b,
                conf_w_pw2=v_conf_w_pw2, conf_b_pw2=v_conf_b_pw2, mlp_w_in=v_mlp_w_in, mlp_w_out=v_mlp_w_out,
                final_g=v_final_g)
    names = list(weights)

    small_items = [c, norm_g, rec_conv_w, rec_lambda, conf_b_pw1, conf_conv_w, conf_conv_b, conf_ln_g, conf_ln_b,
                   conf_b_pw2]
    flat = jnp.concatenate([a.reshape(-1) for a in small_items])
    flat = jnp.pad(flat, (0, SMALL_PACK_ROWS * 128 - flat.shape[0])).reshape(SMALL_PACK_ROWS, 128)
    as_shard = lambda a: a.astype(bf16).reshape(-1, a.shape[-1])
    early_srcs = [flat, as_shard(rec_w_in[0])]
    early_handle, started = _exchange_start(early_srcs, [_own_block_filled(s, me) for s in early_srcs],
                                            "gather_early_start", False)
    zero = started[0, 0]
    gates = _gate_matrix(rec_w_a[0] + zero, rec_w_x[0] + zero)
    late_items = {"mlp": [rec_w_out[0], mlp_w_in, mlp_w_out], "conf": [conf_w_pw1[0], conf_w_pw2[0]]}
    late_shards = {g: [as_shard(a + zero) for a in items] for g, items in late_items.items()}
    late_lands = {g: [_own_block_filled(s, me) for s in shards] for g, shards in late_shards.items()}
    xcat, poscat = _token_rows(x[0] + zero, ctx[0])
    small_all, early = _exchange_wait(early_handle, [gates, xcat, poscat] + late_lands["mlp"] + late_lands["conf"],
                                      "gather_early_wait", False)

    small_all = small_all.reshape(N_DEV, -1)
    off = 0
    small = []
    for a in small_items:
        small.append(small_all[:, off:off + a.size].reshape((N_DEV,) + a.shape))
        off += a.size
    c_all, ng_all, rcw_all, lam_all, bpw1_all, ccw_all, ccb_all, lng_all, lnb_all, bpw2_all = small
    wts = {
        "norm_g": _unshard_cols(ng_all),
        "rec_conv_w": _unshard_cols(rcw_all)[0],
        "rec_lambda": _unshard_cols(lam_all)[0],
        "conf_b_pw1": _unshard_cols(bpw1_all),
        "conf_conv_w": _unshard_cols(ccw_all)[0],
        "conf_conv_b": _unshard_cols(ccb_all),
        "conf_ln_g": _unshard_cols(lng_all),
        "conf_ln_b": _unshard_cols(lnb_all),
        "conf_b_pw2": _unshard_cols(bpw2_all),
        "rec_conv_b": rec_conv_b,
        "rec_b_a": rec_b_a[0].reshape(2, R),
        "rec_b_x": rec_b_x[0].reshape(2, R),
        "final_g": final_g[None],
        "gates": gates,
    }

    c16 = jnp.concatenate([c_all[:, 0], jnp.broadcast_to(c_ctx[None], (8, D))], axis=0)
    b_loc = lax.dynamic_slice_in_dim(b_ada, me * ADA_SHARD, ADA_SHARD, axis=1)[:, None]
    (mods_gathered,) = _all_gather([_ada_forward(c16, w_ada, b_loc)], "gather_mods")
    mods_all = _unshard_cols(mods_gathered)
    mods = lax.dynamic_index_in_dim(mods_all, me, axis=1, keepdims=False).reshape(2, N_MOD, D)
    cmod = mods_all[0, 8, :2 * D].reshape(2, D)

    late_handles = {}
    late_handles["mlp"], token = _gather2_start(late_shards["mlp"], late_lands["mlp"], "gather_mlp_start",
                                                [early, mods_gathered])
    order = [token]
    wts["rec_w_in"] = _unshard_cols(early + token[0, 0].astype(bf16))

    def late_weights(group, after):
        if group == "mlp_halfway":
            late_handles["mlp"] = _gather2_forward1(late_handles["mlp"], after, "gather_mlp_forward1")
            return late_handles["mlp"][2][0]
        if group == "mlp":
            passed = _gather2_forward2(late_handles["mlp"], after, "gather_mlp_forward2")
            late_handles["conf"], started = _exchange_start(late_shards["conf"], late_lands["conf"], "gather_conf_start",
                                                            False, after=[passed[2][0]])
            got = _gather2_wait(passed, started, "gather_mlp_wait")
        else:
            got = _exchange_wait(late_handles[group], after, "gather_conf_wait", False)
        got = [g.reshape((N_DEV,) + a.shape) for g, a in zip(got, late_items[group])]
        if group == "mlp":
            return {"rec_w_out": got[0].reshape(R, D), "mlp_w_in": got[1], "mlp_w_out": got[2]}
        return {"conf_w_pw1": _unshard_cols(got[0]), "conf_w_pw2": got[1].reshape(D, D)}

    to_blocks = {"rec_w_in": _shard_cols, "conf_w_pw1": _shard_cols, "rec_w_out": _shard_rows, "conf_w_pw2": _shard_rows,
                 "mlp_w_in": lambda g: g, "mlp_w_out": lambda g: g}
    grad_handles = []

    repl_names = ["rec_w_a", "rec_w_x", "rec_b_a", "rec_b_x", "final_g"]

    def send_replicated(grads):
        dwg = grads["gates"]
        repl = {"rec_w_a": jnp.stack([_gate_blocks(dwg, 0), _gate_blocks(dwg, 2)]),
                "rec_w_x": jnp.stack([_gate_blocks(dwg, 1), _gate_blocks(dwg, 3)]),
                "rec_b_a": grads["rec_b_a"], "rec_b_x": grads["rec_b_x"],
                "final_g": jnp.pad(grads["final_g"], ((0, 0), (0, REPL_FINAL_G_ROWS * BLK - D)))}
        flat = jnp.concatenate([repl[n].reshape(-1, BLK) for n in repl_names], axis=0)
        flat = jnp.pad(flat, ((0, REPL_ROWS - flat.shape[0]), (0, 0))).astype(bf16)
        flat = flat.reshape(REPL_ROWS // 8, 8 * BLK)
        handle, sent = _exchange_start([flat], [_own_block_filled(flat, me)], "gather_replicated_start", False)
        grad_handles.append((["replicated"], handle))
        return sent

    def send_grads(group, grads):
        if group == ["replicated"]:
            return send_replicated(grads)
        blocks = [to_blocks[n](grads[n]) for n in group]
        blocks = [g.reshape(N_DEV, -1, g.shape[-1]) for g in blocks]
        lands = [_own_block_filled(lax.dynamic_index_in_dim(g, me, 0, keepdims=False), me) for g in blocks]
        handle, sent = _exchange_start(blocks, lands, "scatter_start_" + group[0], True)
        grad_handles.append((group, handle))
        return sent

    loss, grad_x, dmods, dcmod, grads = _local_step(
        xcat, poscat, loss_target[0], mods, cmod, wts, late_weights, send_grads,
        lambda partial: lax.psum(partial, ("x", "y", "c")), start_after=order)

    def as2d(shape):
        rows = 1
        for s in shape[:-1]:
            rows *= s
        return (rows, shape[-1])

    def whole(arr, shape):
        arr = arr.reshape((-1,) + as2d(shape))
        return (arr, arr.shape[0])

    shard_shapes = {n: weights[n].shape for n in names}
    g_out, d_out, m_out, v_out = {}, {}, {}, {}

    def adamw(n, pieces, after):
        shape = shard_shapes[n]
        r2, c2 = as2d(shape)
        g, dl, nm, nv = _adamw(pieces, weights[n].reshape(r2, c2), m_in[n].reshape(r2, c2), v_in[n].reshape(r2, c2),
                               "adamw_" + n, after=after)
        g_out[n], d_out[n], m_out[n], v_out[n] = (t.reshape(shape) for t in (g, dl, nm, nv))
        return g

    small_sharded = ["norm_g", "rec_conv_w", "rec_lambda", "conf_b_pw1", "conf_conv_w", "conf_conv_b", "conf_ln_g",
                     "conf_ln_b", "conf_b_pw2"]
    pack = jnp.concatenate([_shard_cols(grads[n]).reshape(N_DEV, -1) for n in small_sharded], axis=1)
    pack = jnp.pad(pack, ((0, 0), (0, SMALL_PACK_ROWS * 128 - pack.shape[1]))).reshape(N_DEV, SMALL_PACK_ROWS, 128)
    small_handle, token = _exchange_start(
        [pack], [_own_block_filled(lax.dynamic_index_in_dim(pack, me, 0, keepdims=False), me)], "scatter_small_start",
        True, after=[grad_x])
    dm_flat = jnp.concatenate([dmods.reshape(-1), dcmod.reshape(-1), grads["rec_conv_b"].reshape(-1)])
    dm_len = dm_flat.shape[0]
    dm_flat = jnp.pad(dm_flat, (0, 128 * 128 - dm_len)).reshape(128, 128)
    dm_handle, token = _exchange_start([dm_flat], [_own_block_filled(dm_flat, me)], "gather_dmods_start", False,
                                       after=[token])

    done = token
    for group, handle in grad_handles:
        if group == ["replicated"]:
            repl_all = _exchange_wait(handle, done, "gather_replicated_wait", False)[0]
            repl_all = repl_all.reshape(N_DEV, REPL_ROWS, BLK)
            row = 0
            for n in repl_names:
                n_rows = -(-weights[n].size // BLK)
                if as2d(shard_shapes[n]) == (n_rows, BLK) and row % 256 == 0:
                    done = adamw(n, [(repl_all, N_DEV, row)], [done])
                else:
                    got = repl_all[:, row:row + n_rows].reshape(N_DEV, -1)[:, :weights[n].size]
                    done = adamw(n, [whole(got, shard_shapes[n])], [done])
                row += n_rows
            continue
        for n, got in zip(group, _exchange_wait(handle, done, "scatter_wait_" + group[0], True)):
            done = adamw(n, [(got, N_DEV)], [done])

    dm_all = _exchange_wait(dm_handle, done, "gather_dmods_wait", False)[0].reshape(N_DEV, -1)
    dmods_all = dm_all[:, :2 * N_MOD * D].reshape(N_DEV, 2, N_MOD * D)
    dcmod_all = jnp.pad(dm_all[:, 2 * N_MOD * D:2 * N_MOD * D + 2 * D], ((0, 0), (0, (N_MOD - 2) * D)))
    g16_full = jnp.stack([jnp.concatenate([dmods_all[:, 0], dcmod_all], axis=0),
                          jnp.concatenate([dmods_all[:, 1], jnp.zeros_like(dcmod_all)], axis=0)])
    g16 = lax.dynamic_slice_in_dim(g16_full, me * ADA_SHARD, ADA_SHARD, axis=2)
    dw_ada, ds_part = _ada_backward(c16, g16, w_ada)
    ds_handle, token = _exchange_start([ds_part[0]], [_own_block_filled(ds_part[0], me)], "gather_dsilu_start", False)
    done = adamw("w_ada", [whole(dw_ada, shard_shapes["w_ada"])], [token])
    done = adamw("rec_conv_b", [whole(dm_all[:, dm_len - R:dm_len], shard_shapes["rec_conv_b"])], [done])
    db_terms = jnp.concatenate([dmods_all, jnp.stack([dcmod_all, jnp.zeros_like(dcmod_all)], axis=1)], axis=0)
    done = adamw("b_ada", [whole(db_terms, shard_shapes["b_ada"])], [done])
    pack_recv = _exchange_wait(small_handle, done, "scatter_small_wait", True)[0].reshape(N_DEV, -1)
    off = 0
    for n in small_sharded:
        size = weights[n].size
        done = adamw(n, [whole(pack_recv[:, off:off + size], shard_shapes[n])], [done])
        off += size
    ds_all = _exchange_wait(ds_handle, done, "gather_dsilu_wait", False)[0]
    adamw("c_ctx", [whole(ds_all[:, 0], shard_shapes["c_ctx"])], [])

    return (loss, grad_x[None], *[g_out[n] for n in names], *[d_out[n] for n in names],
            *[m_out[n] for n in names], *[v_out[n] for n in names])
```

```python
import functools

import jax
import jax.numpy as jnp
from jax import lax
from jax.experimental import pallas as pl
from jax.experimental.pallas import tpu as pltpu

f32 = jnp.float32
bf16 = jnp.bfloat16

N_DEV = 8
D = 1024
T_LAT = 2048
T_CTX = 256
T_ALL = T_CTX + T_LAT
R = 1280
N_BLK = 16
BLK = R // N_BLK
F = 4096
GRID_W = 64
RG_C = 8.0
EPS = 1e-6
POS_BASE = 10000.0
N_MOD = 6
ADA_SHARD = N_MOD * D // N_DEV

ADAM_LR = 0.001
ADAM_B1 = 0.9
ADAM_B2 = 0.999
ADAM_EPS = 1e-08
ADAM_WD = 0.01
ADAM_STEP = 10

VMEM_LIMIT_V7X = 56 * 1024 * 1024
HALO = 16
MESH = pl.DeviceIdType.MESH


def _cparams(*sem):
    return pltpu.CompilerParams(dimension_semantics=sem, vmem_limit_bytes=VMEM_LIMIT_V7X)


def _pick(n, cands):
    for c in cands:
        if n % c == 0:
            return c
    raise ValueError(f"no block size for {n}")


def _position():
    x, y, c = lax.axis_index("x"), lax.axis_index("y"), lax.axis_index("c")
    return x, y, c, 4 * x + 2 * y + c


def _peer(x, y, c, k):
    px = (1 - x) if (k >> 2) & 1 else x
    py = (1 - y) if (k >> 1) & 1 else y
    pc = (1 - c) if k & 1 else c
    return (px, py, pc), 4 * px + 2 * py + pc


def _exchange(arrs, name, scatter):
    n = len(arrs)

    def body(*refs):
        ins, outs = refs[:n], refs[n:2 * n]
        send_sems, recv_sems, local_sems = refs[2 * n:]
        x, y, c, me = _position()
        local = []
        for a in range(n):
            src = ins[a].at[me] if scatter else ins[a]
            cp = pltpu.make_async_copy(src, outs[a].at[me], local_sems.at[a])
            cp.start()
            local.append(cp)
        sends, recvs = [], []
        for a in range(n):
            for k in range(1, N_DEV):
                peer, peer_lin = _peer(x, y, c, k)
                src = ins[a].at[peer_lin] if scatter else ins[a]
                cp = pltpu.make_async_remote_copy(
                    src_ref=src, dst_ref=outs[a].at[me], send_sem=send_sems.at[a, k - 1],
                    recv_sem=recv_sems.at[a, k - 1], device_id=peer, device_id_type=MESH)
                cp.start()
                sends.append(cp)
                recvs.append(pltpu.make_async_remote_copy(
                    src_ref=src, dst_ref=outs[a].at[peer_lin], send_sem=send_sems.at[a, k - 1],
                    recv_sem=recv_sems.at[a, k - 1], device_id=peer, device_id_type=MESH))
        for cp in recvs:
            cp.wait_recv()
        for cp in sends:
            cp.wait_send()
        for cp in local:
            cp.wait()

    if scatter:
        out_shape = [jax.ShapeDtypeStruct(a.shape, a.dtype) for a in arrs]
    else:
        out_shape = [jax.ShapeDtypeStruct((N_DEV,) + a.shape, a.dtype) for a in arrs]
    any_spec = pl.BlockSpec(memory_space=pl.ANY)
    return pl.pallas_call(
        body, name=name, out_shape=out_shape,
        in_specs=[any_spec] * n, out_specs=[any_spec] * n,
        scratch_shapes=[pltpu.SemaphoreType.DMA((n, N_DEV - 1)), pltpu.SemaphoreType.DMA((n, N_DEV - 1)),
                        pltpu.SemaphoreType.DMA((n,))],
    )(*arrs)


def _all_gather(arrs, name):
    return _exchange(arrs, name, scatter=False)


def _lin(p):
    return 4 * p[0] + 2 * p[1] + p[2]


HBM_SPEC = pl.BlockSpec(memory_space=pltpu.HBM)
SEM_SPEC = pl.BlockSpec(memory_space=pltpu.SEMAPHORE)
DATAFLOW_EFFECT = pltpu.SideEffectType.DATAFLOW_SIDE_EFFECTING


def _split_copies(srcs, lands, send_sems, recv_sems, scatter):
    x, y, c, me = _position()
    out = []
    for a in range(len(srcs)):
        for k in range(1, N_DEV):
            peer, peer_lin = _peer(x, y, c, k)
            src = srcs[a].at[peer_lin] if scatter else srcs[a]
            mk = lambda slot: pltpu.make_async_remote_copy(
                src_ref=src, dst_ref=lands[a].at[slot], send_sem=send_sems.at[a * (N_DEV - 1) + k - 1],
                recv_sem=recv_sems.at[a * (N_DEV - 1) + k - 1], device_id=peer, device_id_type=MESH)
            out.append((mk(me), mk(peer_lin)))
    return out


def _exchange_start(srcs, lands, name, scatter, after=()):
    n = len(srcs)
    n_after = len(after)

    def body(*refs):
        srcs_r, lands_r = refs[:n], refs[n:2 * n]
        send_sems, recv_sems = refs[2 * n + n_after], refs[2 * n + n_after + 1]
        token = refs[-1]
        for outgoing, _ in _split_copies(srcs_r, lands_r, send_sems, recv_sems, scatter):
            outgoing.start()
        token[...] = jnp.zeros_like(token)

    hbm = lambda a: pltpu.HBM(a.shape, a.dtype)
    res = pl.pallas_call(
        body, name=name,
        out_shape=(pltpu.SemaphoreType.DMA((n * (N_DEV - 1),)), pltpu.SemaphoreType.DMA((n * (N_DEV - 1),)),
                   *[hbm(a) for a in srcs], *[hbm(a) for a in lands], jax.ShapeDtypeStruct((8, 128), f32)),
        in_specs=[HBM_SPEC] * (2 * n) + [pl.BlockSpec(memory_space=pl.ANY)] * n_after,
        out_specs=(SEM_SPEC, SEM_SPEC, *[HBM_SPEC] * (2 * n), pl.BlockSpec(memory_space=pltpu.VMEM)),
        input_output_aliases={i: 2 + i for i in range(2 * n)},
        compiler_params=pltpu.CompilerParams(has_side_effects=DATAFLOW_EFFECT),
    )(*[pltpu.with_memory_space_constraint(a, pltpu.HBM) for a in list(srcs) + list(lands)], *after)
    return (res[0], res[1], list(res[2:2 + n]), list(res[2 + n:2 + 2 * n])), res[-1]


def _exchange_wait(handle, after, name, scatter):
    send_sems, recv_sems, srcs, lands = handle
    n = len(srcs)
    after = list(after) if isinstance(after, (list, tuple)) else [after]

    def body(*refs):
        srcs_r, lands_r = refs[:n], refs[n:2 * n]
        send_s, recv_s = refs[2 * n], refs[2 * n + 1]
        for outgoing, incoming in _split_copies(srcs_r, lands_r, send_s, recv_s, scatter):
            outgoing.wait_send()
            incoming.wait_recv()

    hbm = lambda a: pltpu.HBM(a.shape, a.dtype)
    res = pl.pallas_call(
        body, name=name, out_shape=tuple(hbm(a) for a in list(srcs) + list(lands)),
        in_specs=[HBM_SPEC] * (2 * n) + [SEM_SPEC, SEM_SPEC] + [pl.BlockSpec(memory_space=pl.ANY)] * len(after),
        out_specs=tuple([HBM_SPEC] * (2 * n)),
        input_output_aliases={i: i for i in range(2 * n)},
        compiler_params=pltpu.CompilerParams(has_side_effects=DATAFLOW_EFFECT),
    )(*srcs, *lands, send_sems, recv_sems, *after)
    return list(res[n:])


def _split_call(body, name, hbm_ins, kept, in_sems, n_new_sems, after, with_token):
    n_in, n_sem = len(hbm_ins), len(in_sems)
    out_shape, out_specs = [], []
    if n_new_sems:
        out_shape += [pltpu.SemaphoreType.DMA((n_new_sems,))] * 2
        out_specs += [SEM_SPEC] * 2
    first_kept = len(out_shape)
    out_shape += [pltpu.HBM(hbm_ins[i].shape, hbm_ins[i].dtype) for i in kept]
    out_specs += [HBM_SPEC] * len(kept)
    if with_token:
        out_shape.append(jax.ShapeDtypeStruct((8, 128), f32))
        out_specs.append(pl.BlockSpec(memory_space=pltpu.VMEM))

    def wrapped(*refs):
        outs = refs[n_in + n_sem + len(after):]
        body(refs[:n_in], refs[n_in:n_in + n_sem], outs[:2] if n_new_sems else ())
        if with_token:
            outs[-1][...] = jnp.zeros_like(outs[-1])

    return pl.pallas_call(
        wrapped, name=name, out_shape=tuple(out_shape),
        in_specs=[HBM_SPEC] * n_in + [SEM_SPEC] * n_sem + [pl.BlockSpec(memory_space=pl.ANY)] * len(after),
        out_specs=tuple(out_specs), input_output_aliases={i: first_kept + j for j, i in enumerate(kept)},
        compiler_params=pltpu.CompilerParams(has_side_effects=DATAFLOW_EFFECT),
    )(*[pltpu.with_memory_space_constraint(a, pltpu.HBM) for a in hbm_ins], *in_sems, *after)


def _rcopy(src, dst, sems, k, to):
    return pltpu.make_async_remote_copy(src_ref=src, dst_ref=dst, send_sem=sems[0].at[k], recv_sem=sems[1].at[k],
                                        device_id=to, device_id_type=MESH)


def _gather2_start(shards, lands, name, after):
    n = len(shards)

    def body(ins, sems_in, sems_out):
        x, y, c, me = _position()
        for a in range(n):
            for k, to in enumerate(((x, y, 1 - c), (1 - x, y, c), (x, 1 - y, c))):
                _rcopy(ins[a], ins[n + a].at[me], sems_out, 3 * a + k, to).start()

    res = _split_call(body, name, list(shards) + list(lands), range(2 * n), (), 3 * n, after, True)
    return (res[0], res[1], list(res[2:2 + n]), list(res[2 + n:2 + 2 * n])), res[-1]


def _gather2_forward1(handle, after, name):
    send_sems, recv_sems, srcs, lands = handle
    n = len(srcs)

    def body(ins, sems_in, sems_out):
        x, y, c, me = _position()
        sib, xn, yn = (x, y, 1 - c), (1 - x, y, c), (x, 1 - y, c)
        for a in range(n):
            for k, peer in enumerate((sib, xn, yn)):
                _rcopy(ins[a], ins[n + a].at[me], sems_in, 3 * a + k, peer).wait_send()
                _rcopy(ins[a], ins[n + a].at[_lin(peer)], sems_in, 3 * a + k, peer).wait_recv()
        for a in range(n):
            land = ins[n + a]
            _rcopy(land.at[_lin(xn)], land.at[_lin(xn)], sems_out, 3 * a, sib).start()
            _rcopy(land.at[_lin(yn)], land.at[_lin(yn)], sems_out, 3 * a + 1, sib).start()

            @pl.when(c == 0)
            def _():
                _rcopy(land.at[_lin(xn)], land.at[_lin(xn)], sems_out, 3 * a + 2, yn).start()

            @pl.when(c == 1)
            def _():
                _rcopy(land.at[_lin(yn)], land.at[_lin(yn)], sems_out, 3 * a + 2, xn).start()

    res = _split_call(body, name, list(srcs) + list(lands), range(n, 2 * n), (send_sems, recv_sems), 3 * n, [after], False)
    return (res[0], res[1], list(res[2:]))


def _gather2_forward2(handle, after, name):
    send_sems, recv_sems, lands = handle
    n = len(lands)

    def body(ins, sems_in, sems_out):
        x, y, c, me = _position()
        sib, dg = (x, y, 1 - c), _lin((1 - x, 1 - y, c))
        for a in range(n):
            for k, slot in enumerate((_lin((1 - x, y, 1 - c)), _lin((x, 1 - y, 1 - c)), dg)):
                done = _rcopy(ins[a].at[slot], ins[a].at[slot], sems_in, 3 * a + k, sib)
                done.wait_send()
                done.wait_recv()
        for a in range(n):
            _rcopy(ins[a].at[dg], ins[a].at[dg], sems_out, a, sib).start()

    res = _split_call(body, name, list(lands), range(n), (send_sems, recv_sems), n, [after], False)
    return (res[0], res[1], list(res[2:]))


def _gather2_wait(handle, after, name):
    send_sems, recv_sems, lands = handle
    n = len(lands)

    def body(ins, sems_in, sems_out):
        x, y, c, me = _position()
        slot = _lin((1 - x, 1 - y, 1 - c))
        for a in range(n):
            done = _rcopy(ins[a].at[slot], ins[a].at[slot], sems_in, a, (x, y, 1 - c))
            done.wait_send()
            done.wait_recv()

    return list(_split_call(body, name, list(lands), range(n), (send_sems, recv_sems), 0, [after], False))


def _own_block_filled(block, me):
    land = lax.empty((N_DEV,) + block.shape, block.dtype)
    return lax.dynamic_update_index_in_dim(land, block, me, 0)


ANY_SPEC = pl.BlockSpec(memory_space=pl.ANY)


def _mm(a, b, name, ta=False, tb=False, out_dtype=f32, after=()):
    if ta:
        k_dim, m_dim = a.shape
    else:
        m_dim, k_dim = a.shape
    if tb:
        n_dim, k2 = b.shape
    else:
        k2, n_dim = b.shape
    assert k_dim == k2, (a.shape, b.shape)
    assert a.dtype == bf16 and b.dtype == bf16
    bm = _pick(m_dim, (512, 768, 640, 256, 128))
    bn = _pick(n_dim, (512, 640, 256, 128))
    bk = k_dim if k_dim <= 2560 else _pick(k_dim, (1024, 1280, 768, 512))
    nk = k_dim // bk
    a_spec = (pl.BlockSpec((bk, bm), lambda i, j, k: (k, i)) if ta
              else pl.BlockSpec((bm, bk), lambda i, j, k: (i, k)))
    b_spec = (pl.BlockSpec((bn, bk), lambda i, j, k: (j, k)) if tb
              else pl.BlockSpec((bk, bn), lambda i, j, k: (k, j)))
    dims = (((0 if ta else 1,), (1 if tb else 0,)), ((), ()))

    n_after = len(after)

    def body_single(a_ref, b_ref, *rest):
        o_ref = rest[n_after]
        o_ref[...] = lax.dot_general(a_ref[...], b_ref[...], dims, preferred_element_type=f32).astype(o_ref.dtype)

    def body(a_ref, b_ref, *rest):
        o_ref, acc_ref = rest[n_after:]
        k = pl.program_id(2)

        @pl.when(k == 0)
        def _():
            acc_ref[...] = jnp.zeros_like(acc_ref)

        acc_ref[...] += lax.dot_general(a_ref[...], b_ref[...], dims, preferred_element_type=f32)

        @pl.when(k == nk - 1)
        def _():
            o_ref[...] = acc_ref[...].astype(o_ref.dtype)

    return pl.pallas_call(
        body_single if nk == 1 else body, name=name, out_shape=jax.ShapeDtypeStruct((m_dim, n_dim), out_dtype),
        grid=(m_dim // bm, n_dim // bn, nk), in_specs=[a_spec, b_spec] + [ANY_SPEC] * n_after,
        out_specs=pl.BlockSpec((bm, bn), lambda i, j, k: (i, j)),
        scratch_shapes=[] if nk == 1 else [pltpu.VMEM((bm, bn), f32)],
        compiler_params=_cparams("parallel", "parallel", "arbitrary"),
    )(a, b, *after)


def _rin(arr, width=None, cb=0, roff=0):
    return (arr, arr.shape[1] if width is None else width, cb, roff)


def _rowcall(fn, name, rows, tm, row_ins, par_ins, row_outs, acc_outs=(), after=()):
    nr, npar, nro, n_after = len(row_ins), len(par_ins), len(row_outs), len(after)
    in_specs, args = [], []
    for arr, width, cb, roff in row_ins:
        if roff >= 0:
            imap = lambda i, cb=cb, roff=roff: (i + roff, cb)
        else:
            imap = lambda i, cb=cb, roff=roff: (jnp.maximum(i + roff, 0), cb)
        in_specs.append(pl.BlockSpec((tm, width), imap))
        args.append(arr)
    for p in par_ins:
        in_specs.append(pl.BlockSpec(p.shape, lambda i: (0, 0)))
        args.append(p)
    out_shape, out_specs = [], []
    for width, dt in row_outs:
        out_shape.append(jax.ShapeDtypeStruct((rows, width), dt))
        out_specs.append(pl.BlockSpec((tm, width), lambda i: (i, 0)))
    for p, width in acc_outs:
        out_shape.append(jax.ShapeDtypeStruct((p, width), f32))
        out_specs.append(pl.BlockSpec((p, width), lambda i: (0, 0)))

    def body(*refs):
        i = pl.program_id(0)
        res = fn(i, *[r[...] for r in refs[:nr + npar]])
        outs = refs[nr + npar + n_after:]
        for o, v in zip(outs[:nro], res[:nro]):
            o[...] = v.astype(o.dtype)
        if acc_outs:
            @pl.when(i == 0)
            def _():
                for o in outs[nro:]:
                    o[...] = jnp.zeros_like(o)

            for o, v in zip(outs[nro:], res[nro:]):
                o[...] += v

    return pl.pallas_call(
        body, name=name, out_shape=out_shape, grid=(rows // tm,), in_specs=in_specs + [ANY_SPEC] * n_after,
        out_specs=out_specs, compiler_params=_cparams("arbitrary"),
    )(*args, *after)


def _rms(x, g):
    return x * lax.rsqrt(jnp.mean(x * x, axis=-1, keepdims=True) + EPS) * g


def _normmod(x, g, sc, sh):
    return _rms(x, g) * (1.0 + sc) + sh


def _gelu(x):
    return 0.5 * x * (1.0 + jnp.tanh(0.7978845608028654 * (x + 0.044715 * (x * x * x))))


def _sigmoid(x):
    return 0.5 * (jnp.tanh(0.5 * x) + 1.0)


def _coeff_parts(pre_a, pre_x, ba, bx, lam):
    r = _sigmoid(pre_a + ba)
    ig = _sigmoid(pre_x + bx)
    nl = -lam
    sp = jnp.maximum(nl, 0.0) + jnp.log(1.0 + jnp.exp(-jnp.abs(nl)))
    la = -RG_C * r * sp
    a = jnp.exp(la)
    one_minus_a2 = -jnp.tanh(la) * (a * a + 1.0)
    inv_m = lax.rsqrt(one_minus_a2)
    return r, ig, sp, a, one_minus_a2 * inv_m, inv_m


def _coeff(pre_a, pre_x, u, ba, bx, lam):
    _, ig, _, a, m, _ = _coeff_parts(pre_a, pre_x, ba, bx, lam)
    return a, m * (ig * u)


def _coeff_bwd(pre_a, pre_x, u, ba, bx, lam, da, db):
    r, ig, sp, a, m, inv_m = _coeff_parts(pre_a, pre_x, ba, bx, lam)
    dbu = db * u
    dig = dbu * m
    dm = dbu * ig
    dla = a * (da - dm * a * inv_m)
    dpa = dla * (-RG_C * sp) * (r * (1.0 - r))
    dpx = dig * (ig * (1.0 - ig))
    dsp = jnp.sum(dla * (-RG_C * r), axis=0, keepdims=True)
    dlam = -dsp * _sigmoid(-lam)
    return (dpa, dpx, db * m * ig, jnp.sum(dpa, axis=0, keepdims=True), jnp.sum(dpx, axis=0, keepdims=True), dlam)


SCAN_CHUNK = 256


def _scan_call(a, v, chunk_of, reverse, name, backward, after=()):
    rows, width = a.shape
    n_out = 1 if backward else 2
    nt = SCAN_CHUNK // 8

    def body(a_ref, v_ref, *rest):
        outs, state_ref = rest[len(after):-1], rest[-1]

        @pl.when(pl.program_id(0) == 0)
        def _():
            state_ref[...] = jnp.zeros_like(state_ref)

        rid = lax.broadcasted_iota(jnp.int32, (8, width), 0)
        last_row = 0 if reverse else 7

        def shift(x, s, fill):
            rolled = pltpu.roll(x, (8 - s) if reverse else s, axis=0)
            return jnp.where((rid >= 8 - s) if reverse else (rid < s), fill, rolled)

        def tile(j, st):
            t0 = pl.multiple_of((nt - 1 - j if reverse else j) * 8, 8)
            at = a_ref[pl.ds(t0, 8), :]
            coef = shift(at, 1, 1.0) if backward else at
            acc = v_ref[pl.ds(t0, 8), :]
            for s in (1, 2, 4):
                acc = coef * shift(acc, s, 0.0) + acc
                coef = coef * shift(coef, s, 1.0)
            out = coef * st + acc
            outs[0][pl.ds(t0, 8), :] = out
            last = out[last_row:last_row + 1]
            if backward:
                return at[last_row:last_row + 1] * last
            outs[1][pl.ds(t0, 8), :] = shift(out, 1, st)
            return last

        state_ref[0:1, :] = lax.fori_loop(0, nt, tile, state_ref[0:1, :])

    spec = pl.BlockSpec((SCAN_CHUNK, width), lambda t: (chunk_of(t), 0))
    return pl.pallas_call(
        body, name=name, out_shape=[jax.ShapeDtypeStruct((rows, width), f32)] * n_out,
        grid=(rows // SCAN_CHUNK,), in_specs=[spec, spec] + [ANY_SPEC] * len(after), out_specs=[spec] * n_out,
        scratch_shapes=[pltpu.VMEM((8, width), f32)],
        compiler_params=_cparams("arbitrary"),
    )(a, v, *after)


CONV_CHUNK = 256


def _fill_padded(pad_ref, src_ref, start, n):
    cb = pad_ref.shape[1]
    pad_ref[pl.ds(0, HALO), :] = jnp.zeros((HALO, cb), f32)
    pad_ref[pl.ds(HALO, n), :] = src_ref[pl.ds(start, n), :].astype(f32)
    pad_ref[pl.ds(HALO + n, HALO), :] = jnp.zeros((HALO, cb), f32)


def _dwconv_fwd(x, x_cb0, w, b, taps, pad_left, segments, cb, name, emit_bf16):
    rows = x.shape[0]
    width = w.shape[1]

    def body(x_ref, w_ref, b_ref, *rest):
        outs, xp = rest[:-1], rest[-1]
        for start, n in segments:
            _fill_padded(xp, x_ref, start, n)
            for c0 in range(0, n, CONV_CHUNK):
                acc = jnp.zeros((CONV_CHUNK, cb), f32) + b_ref[...]
                for k in range(taps):
                    acc = acc + w_ref[k:k + 1, :] * xp[pl.ds(HALO + c0 + k - pad_left, CONV_CHUNK), :]
                for o in outs:
                    o[pl.ds(start + c0, CONV_CHUNK), :] = acc.astype(o.dtype)

    out_dtypes = [f32, bf16] if emit_bf16 else [f32]
    return pl.pallas_call(
        body, name=name, out_shape=[jax.ShapeDtypeStruct((rows, width), dt) for dt in out_dtypes],
        grid=(width // cb,),
        in_specs=[pl.BlockSpec((rows, cb), lambda j: (0, j + x_cb0)), pl.BlockSpec((taps, cb), lambda j: (0, j)),
                  pl.BlockSpec((1, cb), lambda j: (0, j))],
        out_specs=[pl.BlockSpec((rows, cb), lambda j: (0, j))] * len(out_dtypes),
        scratch_shapes=[pltpu.VMEM((rows + 2 * HALO, cb), f32)],
        compiler_params=_cparams("parallel"),
    )(x, w, b)


def _dwconv_bwd(douts, x, x_cb0, w, taps, pad_left, segments, cb, name, dx_dtype):
    rows = x.shape[0]
    width = w.shape[1]
    nd = len(douts)

    def body(*refs):
        d_refs, x_ref, w_ref = refs[:nd], refs[nd], refs[nd + 1]
        dx_ref, dw_ref, db_ref, dp, dsum = refs[nd + 2:]
        dw_ref[...] = jnp.zeros_like(dw_ref)
        db_ref[...] = jnp.zeros_like(db_ref)
        if nd > 1:
            total = d_refs[0][...]
            for r in d_refs[1:]:
                total = total + r[...]
            dsum[...] = total
            d_ref = dsum
        else:
            d_ref = d_refs[0]
        for start, n in segments:
            _fill_padded(dp, d_ref, start, n)
            for c0 in range(0, n, CONV_CHUNK):
                db_ref[...] += jnp.sum(dp[pl.ds(HALO + c0, CONV_CHUNK), :], axis=0, keepdims=True)
                xchunk = x_ref[pl.ds(start + c0, CONV_CHUNK), :].astype(f32)
                acc = jnp.zeros((CONV_CHUNK, cb), f32)
                for k in range(taps):
                    shifted = dp[pl.ds(HALO + c0 + pad_left - k, CONV_CHUNK), :]
                    acc = acc + w_ref[k:k + 1, :] * shifted
                    dw_ref[k:k + 1, :] += jnp.sum(shifted * xchunk, axis=0, keepdims=True)
                dx_ref[pl.ds(start + c0, CONV_CHUNK), :] = acc.astype(dx_ref.dtype)

    dspec = pl.BlockSpec((rows, cb), lambda j: (0, j))
    return pl.pallas_call(
        body, name=name,
        out_shape=[jax.ShapeDtypeStruct((rows, width), dx_dtype), jax.ShapeDtypeStruct((taps, width), f32),
                   jax.ShapeDtypeStruct((1, width), f32)],
        grid=(width // cb,),
        in_specs=[dspec] * nd + [pl.BlockSpec((rows, cb), lambda j: (0, j + x_cb0)),
                                 pl.BlockSpec((taps, cb), lambda j: (0, j))],
        out_specs=[dspec, pl.BlockSpec((taps, cb), lambda j: (0, j)), pl.BlockSpec((1, cb), lambda j: (0, j))],
        scratch_shapes=[pltpu.VMEM((rows + 2 * HALO, cb), f32), pltpu.VMEM((rows, cb), f32)],
        compiler_params=_cparams("parallel"),
    )(*douts, x, w)


def _ada_forward(c16, w_ada, b_loc):
    def body(c_ref, w_ref, b_ref, o_ref):
        cv = c_ref[...]
        s = (cv * _sigmoid(cv)).astype(bf16)
        o_ref[0] = jnp.dot(s, w_ref[0].astype(bf16), preferred_element_type=f32) + b_ref[0]

    return pl.pallas_call(
        body, name="ada_forward", out_shape=jax.ShapeDtypeStruct((2, 16, ADA_SHARD), f32), grid=(2,),
        in_specs=[pl.BlockSpec((16, D), lambda l: (0, 0)), pl.BlockSpec((1, D, ADA_SHARD), lambda l: (l, 0, 0)),
                  pl.BlockSpec((1, 1, ADA_SHARD), lambda l: (l, 0, 0))],
        out_specs=pl.BlockSpec((1, 16, ADA_SHARD), lambda l: (l, 0, 0)),
        compiler_params=_cparams("parallel"),
    )(c16, w_ada, b_loc)


def _ada_backward(c16, g16, w_ada):
    def body(c_ref, g_ref, w_ref, dw_ref, ds_ref):
        cv = c_ref[...]
        s = (cv * _sigmoid(cv)).astype(bf16)
        g = g_ref[0].astype(bf16)
        dw_ref[0] = lax.dot_general(s, g, (((0,), (0,)), ((), ())), preferred_element_type=f32)
        ds = lax.dot_general(g, w_ref[0].astype(bf16), (((1,), (1,)), ((), ())), preferred_element_type=f32)
        cc = cv[8:9]
        sg = _sigmoid(cc)
        dsilu = sg * (1.0 + cc * (1.0 - sg))
        ds_ref[0] = jnp.zeros((8, D), f32) + jnp.sum(ds[8:16], axis=0, keepdims=True) * dsilu

    return pl.pallas_call(
        body, name="ada_backward",
        out_shape=[jax.ShapeDtypeStruct((2, D, ADA_SHARD), f32), jax.ShapeDtypeStruct((2, 8, D), f32)], grid=(2,),
        in_specs=[pl.BlockSpec((16, D), lambda l: (0, 0)), pl.BlockSpec((1, 16, ADA_SHARD), lambda l: (l, 0, 0)),
                  pl.BlockSpec((1, D, ADA_SHARD), lambda l: (l, 0, 0))],
        out_specs=[pl.BlockSpec((1, D, ADA_SHARD), lambda l: (l, 0, 0)), pl.BlockSpec((1, 8, D), lambda l: (l, 0, 0))],
        compiler_params=_cparams("parallel"),
    )(c16, g16, w_ada)


def _adamw(pieces, w, m, v, name, after=()):
    rows, cols = w.shape
    n_arr, n_after = len(pieces), len(after)
    tm = 256 if (rows % 256 == 0 and rows > 256) else rows
    counts = [p[1] for p in pieces]
    first_tiles = [(p[2] if len(p) > 2 else 0) // tm for p in pieces]
    pieces = [p[0] for p in pieces]

    def body(*refs):
        p_refs = refs[:n_arr]
        w_ref, m_ref, v_ref = refs[n_arr:n_arr + 3]
        g_ref, d_ref, nm_ref, nv_ref = refs[n_arr + 3 + n_after:]
        g = None
        for p_ref in p_refs:
            for j in range(p_ref.shape[0]):
                term = p_ref[j].astype(f32)
                g = term if g is None else g + term
        m2 = ADAM_B1 * m_ref[...] + (1.0 - ADAM_B1) * g
        v2 = ADAM_B2 * v_ref[...] + (1.0 - ADAM_B2) * (g * g)
        m_hat = m2 / (1.0 - ADAM_B1 ** ADAM_STEP)
        v_hat = v2 / (1.0 - ADAM_B2 ** ADAM_STEP)
        g_ref[...] = g
        d_ref[...] = -ADAM_LR * (m_hat / (jnp.sqrt(v_hat) + ADAM_EPS) + ADAM_WD * w_ref[...])
        nm_ref[...] = m2
        nv_ref[...] = v2

    spec = pl.BlockSpec((tm, cols), lambda i: (i, 0))
    return pl.pallas_call(
        body, name=name, out_shape=[jax.ShapeDtypeStruct((rows, cols), f32)] * 4, grid=(rows // tm,),
        in_specs=[pl.BlockSpec((cnt, tm, cols), lambda i, t=t: (0, i + t, 0)) for cnt, t in zip(counts, first_tiles)]
        + [spec, spec, spec]
        + [ANY_SPEC] * n_after,
        out_specs=[spec] * 4, compiler_params=_cparams("parallel"),
    )(*pieces, w, m, v, *after)


MLP_TM = 256
FB = F // N_DEV


def _stack_rows(vals, n):
    cols = vals[0].shape[1]
    rid = lax.broadcasted_iota(jnp.int32, (n, cols), 0)
    out = jnp.zeros((n, cols), f32)
    for k, v in enumerate(vals):
        out = jnp.where(rid == k, v, out)
    return out


N_MLP_PARAMS = 9


class _ParamRows:
    def __init__(self, ref):
        self.ref = ref

    def __getitem__(self, sl):
        return self.ref[8 * sl.start:8 * sl.start + 1, :]


def _resident(shape, imap):
    return pl.BlockSpec(shape, imap, pipeline_mode=pl.Buffered(1))


def _mlp_forward(xa, xa_roff, out_prev, par, w_in, w_out, layer, name):
    def body(xa_ref, op_ref, par_ref, win_ref, wout_ref, x1_ref, h_ref, r_ref, mo_ref, x2_ref, hn_ref):
        p = _ParamRows(par_ref)
        x1 = xa_ref[...] + p[0:1] * (op_ref[...] + p[1:2])
        h = _normmod(x1, p[2:3], p[3:4], p[4:5]).astype(bf16)
        x1_ref[...] = x1
        h_ref[...] = h
        mo = jnp.zeros((MLP_TM, D), f32)
        for j in range(N_DEV):
            r = jnp.maximum(jnp.dot(h, win_ref[j], preferred_element_type=f32), 0.0)
            r_ref[:, j * FB:(j + 1) * FB] = r.astype(bf16)
            mo = mo + jnp.dot((r * r).astype(bf16), wout_ref[j], preferred_element_type=f32)
        mo_ref[...] = mo.astype(bf16)
        x2 = x1 + p[5:6] * mo
        x2_ref[...] = x2
        hn_ref[...] = _normmod(x2, p[6:7], p[7:8], p[8:9]).astype(bf16)

    row = lambda width: pl.BlockSpec((MLP_TM, width), lambda i: (i, 0))
    return pl.pallas_call(
        body, name=name, grid=(T_LAT // MLP_TM,),
        out_shape=[jax.ShapeDtypeStruct((T_LAT, D), f32), jax.ShapeDtypeStruct((T_LAT, D), bf16),
                   jax.ShapeDtypeStruct((T_LAT, F), bf16), jax.ShapeDtypeStruct((T_LAT, D), bf16),
                   jax.ShapeDtypeStruct((T_LAT, D), f32), jax.ShapeDtypeStruct((T_LAT, D), bf16)],
        in_specs=[pl.BlockSpec((MLP_TM, D), lambda i: (i + xa_roff, 0)), row(D), pl.BlockSpec((8 * N_MLP_PARAMS, D), lambda i: (0, 0)),
                  _resident((N_DEV, None, D, FB), lambda i: (0, layer, 0, 0)),
                  _resident((N_DEV, None, FB, D), lambda i: (0, layer, 0, 0))],
        out_specs=[row(D), row(D), row(F), row(D), row(D), row(D)],
        compiler_params=_cparams("parallel"),
    )(xa, out_prev, par, w_in, w_out)


def _mlp_backward(dx2, x1, r, mo, out_prev, par, w_in, w_out, layer, name, after=()):
    nt = (((1,), (1,)), ((), ()))

    n_after = len(after)

    def body(dx2_ref, x1_ref, r_ref, mo_ref, op_ref, par_ref, win_ref, wout_ref, *rest):
        dx1_ref, dop_ref, dmo_ref, dhid_ref, acc_ref = rest[n_after:]
        p = _ParamRows(par_ref)
        dx2v = dx2_ref[...]
        dmo = (p[5:6] * dx2v).astype(bf16)
        dmo_ref[...] = dmo
        dh = jnp.zeros((MLP_TM, D), f32)
        mo = mo_ref[...].astype(f32)
        for j in range(N_DEV):
            rf = r_ref[:, j * FB:(j + 1) * FB].astype(f32)
            dact = lax.dot_general(dmo, wout_ref[j], nt, preferred_element_type=f32)
            dhid = (dact * (2.0 * rf)).astype(bf16)
            dhid_ref[:, j * FB:(j + 1) * FB] = dhid
            dh = dh + lax.dot_general(dhid, win_ref[j], nt, preferred_element_type=f32)
        x1 = x1_ref[...]
        _, vjp = jax.vjp(_normmod, x1, p[2:3], p[3:4], p[4:5])
        dx, dng, dsc, dsh = vjp(dh)
        dx1 = dx2v + dx
        dx1_ref[...] = dx1
        dop_ref[...] = (p[0:1] * dx1).astype(bf16)
        sums = _stack_rows([jnp.sum(dx1 * (op_ref[...] + p[1:2]), axis=0, keepdims=True),
                            p[0:1] * jnp.sum(dx1, axis=0, keepdims=True), dng, dsc, dsh,
                            jnp.sum(dx2v * mo, axis=0, keepdims=True)], 8)

        @pl.when(pl.program_id(0) == 0)
        def _():
            acc_ref[...] = jnp.zeros_like(acc_ref)

        acc_ref[...] += sums

    row = lambda width: pl.BlockSpec((MLP_TM, width), lambda i: (i, 0))
    return pl.pallas_call(
        body, name=name, grid=(T_LAT // MLP_TM,),
        out_shape=[jax.ShapeDtypeStruct((T_LAT, D), f32), jax.ShapeDtypeStruct((T_LAT, D), bf16),
                   jax.ShapeDtypeStruct((T_LAT, D), bf16), jax.ShapeDtypeStruct((T_LAT, F), bf16),
                   jax.ShapeDtypeStruct((8, D), f32)],
        in_specs=[row(D), row(D), row(F), row(D), row(D), pl.BlockSpec((8 * N_MLP_PARAMS, D), lambda i: (0, 0)),
                  _resident((N_DEV, None, D, FB), lambda i: (0, layer, 0, 0)),
                  _resident((N_DEV, None, FB, D), lambda i: (0, layer, 0, 0))] + [ANY_SPEC] * n_after,
        out_specs=[row(D), row(D), row(D), row(F), pl.BlockSpec((8, D), lambda i: (0, 0))],
        compiler_params=_cparams("arbitrary"),
    )(dx2, x1, r, mo, out_prev, par, w_in, w_out, *after)


def _mlp_weight_grads(h, dhid, r, dmo, layer, other, tag):
    tn = (((0,), (0,)), ((), ()))

    def body_in(h_ref, dhid_ref, *rest):
        rest[-1][...] = lax.dot_general(h_ref[...], dhid_ref[...], tn, preferred_element_type=f32).astype(bf16)

    def body_out(r_ref, dmo_ref, *rest):
        rf = r_ref[...].astype(f32)
        rest[-1][...] = lax.dot_general((rf * rf).astype(bf16), dmo_ref[...], tn,
                                        preferred_element_type=f32).astype(bf16)

    def call(body, name, operands, specs, block, prev):
        extra = [] if prev is None else [prev]
        return pl.pallas_call(
            body, name=name, grid=(N_DEV,), out_shape=jax.ShapeDtypeStruct((N_DEV, 2) + block, bf16),
            in_specs=specs + [pl.BlockSpec(memory_space=pl.ANY)] * len(extra),
            out_specs=pl.BlockSpec((None, None) + block, lambda j: (j, layer, 0, 0)),
            input_output_aliases={} if prev is None else {2: 0},
            compiler_params=_cparams("parallel"),
        )(*operands, *extra)

    dw_in = call(body_in, tag + "_mlp_in_dw", [h, dhid],
                 [_resident((T_LAT, D), lambda j: (0, 0)), pl.BlockSpec((T_LAT, FB), lambda j: (0, j))], (D, FB),
                 None if other is None else other[0])
    dw_out = call(body_out, tag + "_mlp_out_dw", [r, dmo],
                  [pl.BlockSpec((T_LAT, FB), lambda j: (0, j)), _resident((T_LAT, D), lambda j: (0, 0))], (FB, D),
                  None if other is None else other[1])
    return dw_in, dw_out


def _pos_embed():
    n_rows = T_LAT // GRID_W
    q = D // 4
    omega = 1.0 / (POS_BASE ** (jnp.arange(q, dtype=f32) / q))
    er = jnp.arange(n_rows, dtype=jnp.int32).astype(f32)[:, None] * omega[None, :]
    ec = jnp.arange(GRID_W, dtype=jnp.int32).astype(f32)[:, None] * omega[None, :]
    by_row = jnp.concatenate([jnp.sin(er), jnp.cos(er)], axis=-1)[:, None, :]
    by_col = jnp.concatenate([jnp.sin(ec), jnp.cos(ec)], axis=-1)[None, :, :]
    full = jnp.concatenate([jnp.broadcast_to(by_row, (n_rows, GRID_W, D // 2)),
                            jnp.broadcast_to(by_col, (n_rows, GRID_W, D // 2))], axis=-1)
    return full.reshape(T_LAT, D)


HALF = R // 2
BLK_PER_HALF = N_BLK // 2
N_PARTS = 4


def _gate_matrix(w_a, w_x):
    eye = jnp.eye(BLK_PER_HALF, dtype=bf16)
    cols = []
    for h in range(2):
        for d in range(2):
            for w in (w_a, w_x):
                blocks = w[d, BLK_PER_HALF * h:BLK_PER_HALF * (h + 1)].astype(bf16)
                cols.append(jnp.einsum("hij,hg->higj", blocks, eye).reshape(HALF, HALF))
    return jnp.concatenate(cols, axis=1)


def _gate_blocks(dwg, part):
    out = []
    for h in range(2):
        blk = dwg[:, (N_PARTS * h + part) * HALF:(N_PARTS * h + part + 1) * HALF]
        blk = blk.reshape(BLK_PER_HALF, BLK, BLK_PER_HALF, BLK)
        out.append(jnp.moveaxis(jnp.diagonal(blk, axis1=0, axis2=2), -1, 0))
    return jnp.concatenate(out, axis=0)


GATE_BM = 768


def _gates_dx(dpre, wg, after=()):
    rows = dpre.shape[0]
    n_after = len(after)

    def body(d_ref, w_ref, *rest):
        rest[n_after][...] = lax.dot_general(d_ref[...], w_ref[...], (((1,), (1,)), ((), ())),
                                             preferred_element_type=f32)

    return pl.pallas_call(
        body, name="l0_gates_dx", grid=(rows // GATE_BM, 2), out_shape=jax.ShapeDtypeStruct((rows, R), f32),
        in_specs=[pl.BlockSpec((GATE_BM, N_PARTS * HALF), lambda i, h: (i, h)),
                  pl.BlockSpec((HALF, N_PARTS * HALF), lambda i, h: (0, h))] + [ANY_SPEC] * n_after,
        out_specs=pl.BlockSpec((GATE_BM, HALF), lambda i, h: (i, h)),
        compiler_params=_cparams("parallel", "parallel"),
    )(dpre, wg, *after)


COEFF_TM = 256


def _dir_params(d, *params):
    specs = [pl.BlockSpec((None, 1, HALF), lambda h, i: (d, 0, h))] * len(params)
    return specs, [p.reshape(2, 1, R) for p in params]


def _gates_coeff_fwd(ub, u, wg, ba, bx, lam, d):
    rows = u.shape[0]

    def body(ub_ref, u_ref, w_ref, ba_ref, bx_ref, lam_ref, a_ref, b_ref):
        pre = jnp.dot(ub_ref[...], w_ref[...], preferred_element_type=f32)
        a, b = _coeff(pre[:, :HALF], pre[:, HALF:], u_ref[...], ba_ref[...], bx_ref[...], lam_ref[...])
        a_ref[...] = a
        b_ref[...] = b

    tile = pl.BlockSpec((COEFF_TM, HALF), lambda h, i: (i, h))
    pspecs, pargs = _dir_params(d, ba, bx, lam)
    return pl.pallas_call(
        body, name=f"l0_gates_coeff_{d}", grid=(2, rows // COEFF_TM),
        out_shape=[jax.ShapeDtypeStruct((rows, R), f32)] * 2,
        in_specs=[tile, tile, pl.BlockSpec((HALF, 2 * HALF), lambda h, i: (0, 2 * h + d))] + pspecs,
        out_specs=[tile, tile], compiler_params=_cparams("parallel", "parallel"),
    )(ub, u, wg, *pargs)


def _gates_coeff_bwd(ub, u, dh, yp, wg, ba, bx, lam, d, dpre_prev):
    rows = u.shape[0]
    n_prev = 0 if dpre_prev is None else 1

    def body(ub_ref, u_ref, dh_ref, yp_ref, w_ref, ba_ref, bx_ref, lam_ref, *rest):
        dpre_ref, du_ref, dba_ref, dbx_ref, dlam_ref = rest[n_prev:]
        pre = jnp.dot(ub_ref[...], w_ref[...], preferred_element_type=f32)
        dhv = dh_ref[...]
        dpa, dpx, du, dba, dbx, dlam = _coeff_bwd(pre[:, :HALF], pre[:, HALF:], u_ref[...], ba_ref[...], bx_ref[...],
                                                  lam_ref[...], dhv * yp_ref[...], dhv)
        dpre_ref[:, :HALF] = dpa.astype(bf16)
        dpre_ref[:, HALF:] = dpx.astype(bf16)
        du_ref[...] = du

        @pl.when(pl.program_id(1) == 0)
        def _():
            dba_ref[...] = jnp.zeros_like(dba_ref)
            dbx_ref[...] = jnp.zeros_like(dbx_ref)
            dlam_ref[...] = jnp.zeros_like(dlam_ref)

        dba_ref[...] += dba
        dbx_ref[...] += dbx
        dlam_ref[...] += dlam

    tile = pl.BlockSpec((COEFF_TM, HALF), lambda h, i: (i, h))
    acc = pl.BlockSpec((1, HALF), lambda h, i: (0, h))
    pspecs, pargs = _dir_params(d, ba, bx, lam)
    extra = [] if dpre_prev is None else [dpre_prev]
    return pl.pallas_call(
        body, name=f"l0_gates_coeff_bwd_{d}", grid=(2, rows // COEFF_TM),
        out_shape=[jax.ShapeDtypeStruct((rows, 2 * N_PARTS * HALF), bf16), jax.ShapeDtypeStruct((rows, R), f32)]
        + [jax.ShapeDtypeStruct((1, R), f32)] * 3,
        in_specs=[tile] * 4 + [pl.BlockSpec((HALF, 2 * HALF), lambda h, i: (0, 2 * h + d))] + pspecs
        + [ANY_SPEC] * n_prev,
        out_specs=[pl.BlockSpec((COEFF_TM, 2 * HALF), lambda h, i: (i, 2 * h + d)), tile, acc, acc, acc],
        input_output_aliases={8: 0} if n_prev else {}, compiler_params=_cparams("parallel", "arbitrary"),
    )(ub, u, dh, yp, wg, *pargs, *extra)


def _gates_dw(u, dpre):
    rows = u.shape[0]

    def body(u_ref, d_ref, o_ref):
        o_ref[...] = lax.dot_general(u_ref[...], d_ref[...], (((0,), (0,)), ((), ())), preferred_element_type=f32)

    return pl.pallas_call(
        body, name="l0_gates_dw", grid=(2 * N_PARTS,), out_shape=jax.ShapeDtypeStruct((HALF, 2 * N_PARTS * HALF), f32),
        in_specs=[pl.BlockSpec((rows, HALF), lambda j: (0, j // N_PARTS)), pl.BlockSpec((rows, HALF), lambda j: (0, j))],
        out_specs=pl.BlockSpec((HALF, HALF), lambda j: (0, j)), compiler_params=_cparams("parallel"),
    )(u, dpre)


N_SCAN_CHUNKS = T_ALL // SCAN_CHUNK
SCAN_FWD = lambda t: t
SCAN_FWD_BWD = lambda t: N_SCAN_CHUNKS - 1 - t
SCAN_REV = lambda t: jnp.where(t == 0, 0, N_SCAN_CHUNKS - t)
SCAN_REV_BWD = lambda t: jnp.where(t == N_SCAN_CHUNKS - 1, 0, t + 1)
CONV_SEGMENTS = ((0, T_CTX), (T_CTX, T_LAT))
FUSED_TM = 256


def _token_rows(x, ctx):
    return (jnp.concatenate([ctx, x], axis=0),
            jnp.concatenate([jnp.zeros((T_CTX, D), f32), _pos_embed()], axis=0))


def _local_step(xcat, poscat, target, mods, cmod, wts, late_weights, send_grads, reduce_loss, start_after=()):
    sh1, sc1, g1, sh2, sc2, g2 = [[mods[l, i][None] for l in range(2)] for i in range(N_MOD)]
    ng = wts["norm_g"]
    scp = jnp.concatenate([cmod[1][None], sc1[0]], axis=0)
    shp = jnp.concatenate([cmod[0][None], sh1[0]], axis=0)

    ctx_tiles = T_CTX // FUSED_TM
    nt = (((1,), (1,)), ((), ()))

    def blend(i, p):
        sel = jnp.where(i < ctx_tiles, 1.0, 0.0)
        return sel * p[0:1] + (1.0 - sel) * p[1:2]

    def f_pre0(i, xc, pos, g, scp_, shp_, w):
        x0 = xc + pos
        h = _normmod(x0, g, blend(i, scp_), blend(i, shp_)).astype(bf16)
        return x0, h, jnp.dot(h, w, preferred_element_type=f32)

    x0cat, h0, gr = _rowcall(f_pre0, "l0_prenorm_in_proj", T_ALL, FUSED_TM, [_rin(xcat), _rin(poscat)],
                             [ng[0, 0][None], scp, shp, wts["rec_w_in"]], [(D, f32), (D, bf16), (2 * R, f32)],
                             after=start_after)
    u, ub = _dwconv_fwd(gr, R // 256, wts["rec_conv_w"], wts["rec_conv_b"], 4, 1, CONV_SEGMENTS, 256,
                        "l0_conv", True)
    gate_args = (wts["gates"], wts["rec_b_a"], wts["rec_b_x"], wts["rec_lambda"])
    a0, b0 = _gates_coeff_fwd(ub, u, *gate_args, 0)
    a1, b1 = _gates_coeff_fwd(ub, u, *gate_args, 1)
    halfway = late_weights("mlp_halfway", a1)
    y0, yp0 = _scan_call(a0, b0, SCAN_FWD, False, "l0_scan_fwd", False, after=[halfway])
    y1, yp1 = _scan_call(a1, b1, SCAN_REV, True, "l0_scan_rev", False)

    wts = dict(wts, **late_weights("mlp", y1))

    def f_gate_out(i, gp, y0_, y1_, w):
        z = (_gelu(gp) * (y0_ + y1_)).astype(bf16)
        return z, jnp.dot(z, w, preferred_element_type=f32)

    zb, out0 = _rowcall(f_gate_out, "l0_gate_out_proj", T_LAT, FUSED_TM,
                        [_rin(gr, R, 0, ctx_tiles), _rin(y0, None, 0, ctx_tiles), _rin(y1, None, 0, ctx_tiles)],
                        [wts["rec_w_out"]], [(R, bf16), (D, f32)])

    zero_d = jnp.zeros((1, D), f32)

    def mlp_params(rows):
        rows = rows + [zero_d] * (N_MLP_PARAMS - len(rows))
        return jnp.concatenate([jnp.broadcast_to(r, (8, D)) for r in rows], axis=0)

    par0 = mlp_params([g1[0], zero_d, ng[0, 1][None], sc2[0], sh2[0], g2[0], ng[1, 0][None], sc1[1], sh1[1]])
    x1, h1, r0, mo0, x2, h2 = _mlp_forward(x0cat, T_CTX // MLP_TM, out0, par0, wts["mlp_w_in"], wts["mlp_w_out"], 0,
                                           "l0_mlp")

    wts = dict(wts, **late_weights("conf", x2))
    def glu(pa, pb, b1):
        return (pa + b1[:, :D]) * _sigmoid(pb + b1[:, D:])

    def f_pw1_glu(i, h_, b1, w):
        p = jnp.dot(h_, w, preferred_element_type=f32)
        return glu(p[:, :D], p[:, D:], b1), p

    zg, pw = _rowcall(f_pw1_glu, "l1_pw1_glu", T_LAT, FUSED_TM, [_rin(h2)], [wts["conf_b_pw1"], wts["conf_w_pw1"]],
                      [(D, f32), (2 * D, bf16)])
    (zc,) = _dwconv_fwd(zg, 0, wts["conf_conv_w"], wts["conf_conv_b"], 31, 15, ((0, T_LAT),), 128, "l1_conv", False)

    def ln_silu(z, lg, lb):
        mu = jnp.mean(z, axis=-1, keepdims=True)
        zc_ = z - mu
        var = jnp.mean(zc_ * zc_, axis=-1, keepdims=True)
        yv = zc_ * lax.rsqrt(var + EPS) * lg + lb
        return yv * _sigmoid(yv)

    def f_lnsilu_pw2(i, z, lg, lb, w):
        s = ln_silu(z, lg, lb).astype(bf16)
        return s, jnp.dot(s, w, preferred_element_type=f32)

    sb, out1 = _rowcall(f_lnsilu_pw2, "l1_ln_silu_pw2", T_LAT, FUSED_TM, [_rin(zc)],
                        [wts["conf_ln_g"], wts["conf_ln_b"], wts["conf_w_pw2"]], [(D, bf16), (D, f32)])
    par1 = mlp_params([g1[1], wts["conf_b_pw2"], ng[1, 1][None], sc2[1], sh2[1], g2[1]])
    x3, h3, r1, mo1, x4, _ = _mlp_forward(x2, 0, out1, par1, wts["mlp_w_in"], wts["mlp_w_out"], 1, "l1_mlp")

    def loss_fn(x4_, fg, tgt):
        err = _rms(x4_, fg) - tgt
        per_row = jnp.mean(err * err, axis=-1, keepdims=True)
        return 0.5 * jnp.sum(per_row, axis=0, keepdims=True)

    def f_head(i, x4_, tgt, fg):
        loss, vjp = jax.vjp(lambda a, e: loss_fn(a, e, tgt), x4_, fg)
        dx, dfg = vjp(jnp.ones((1, 1), f32))
        return dx, jnp.broadcast_to(loss, (1, 128)), dfg

    dx4, loss_acc, dfinal_g = _rowcall(f_head, "head", T_LAT, FUSED_TM, [_rin(x4), _rin(target)], [wts["final_g"]],
                                       [(D, f32)], [(1, 128), (1, D)])

    grads = {"final_g": dfinal_g}
    loss = reduce_loss(loss_acc[0, 0])

    dx3, dout1, dmo1, dhid1, acc1 = _mlp_backward(dx4, x3, r1, mo1, out1, par1, wts["mlp_w_in"], wts["mlp_w_out"], 1,
                                                  "l1_mlp_bwd", after=[loss.reshape(1, 1)])
    mlp_dw = _mlp_weight_grads(h3, dhid1, r1, dmo1, 1, None, "l1")
    dg1_1, db_pw2, dng11, dsc2_1, dsh2_1, dg2_1 = [acc1[k:k + 1] for k in range(6)]

    grads["conf_w_pw2"] = _mm(sb, dout1, "l1_pw2_dw", ta=True, out_dtype=bf16)
    grads["conf_b_pw2"] = db_pw2

    def f_pw2_lnsilu_bwd(i, z, dout, lg, lb, w):
        ds = lax.dot_general(dout, w, nt, preferred_element_type=f32)
        _, vjp = jax.vjp(ln_silu, z, lg, lb)
        return vjp(ds)

    dzc, dln_g, dln_b = _rowcall(f_pw2_lnsilu_bwd, "l1_pw2_ln_silu_bwd", T_LAT, FUSED_TM, [_rin(zc), _rin(dout1)],
                                 [wts["conf_ln_g"], wts["conf_ln_b"], wts["conf_w_pw2"]], [(D, f32)], [(1, D)] * 2)
    grads["conf_ln_g"], grads["conf_ln_b"] = dln_g, dln_b
    dzg, dconv_w, dconv_b = _dwconv_bwd([dzc], zg, 0, wts["conf_conv_w"], 31, 15, ((0, T_LAT),), 128,
                                        "l1_conv_bwd", f32)
    grads["conf_conv_w"], grads["conf_conv_b"] = dconv_w, dconv_b

    def f_glu_pw1_norm_bwd(i, p_, dz, x_, dxs, b1, g_, sc_, sh_, w):
        pf = p_.astype(f32)
        _, vjp = jax.vjp(glu, pf[:, :D], pf[:, D:], b1)
        da, db, db1 = vjp(dz)
        dp = jnp.concatenate([da, db], axis=1).astype(bf16)
        dh = lax.dot_general(dp, w, nt, preferred_element_type=f32)
        _, vjp = jax.vjp(_normmod, x_, g_, sc_, sh_)
        dx, dg, dsc, dsh = vjp(dh)
        return dp, dx + dxs, db1, dg, dsc, dsh

    dpw, dx2, db_pw1, dng10, dsc1_1, dsh1_1 = _rowcall(
        f_glu_pw1_norm_bwd, "l1_glu_pw1_normmod_bwd", T_LAT, FUSED_TM, [_rin(pw), _rin(dzg), _rin(x2), _rin(dx3)],
        [wts["conf_b_pw1"], ng[1, 0][None], sc1[1], sh1[1], wts["conf_w_pw1"]], [(2 * D, bf16), (D, f32)],
        [(1, 2 * D), (1, D), (1, D), (1, D)])
    grads["conf_b_pw1"] = db_pw1
    grads["conf_w_pw1"] = _mm(h2, dpw, "l1_pw1_dw", ta=True, out_dtype=bf16)
    sent = send_grads(["conf_w_pw2", "conf_w_pw1"], grads)

    dx1, dout0, dmo0, dhid0, acc0 = _mlp_backward(dx2, x1, r0, mo0, out0, par0, wts["mlp_w_in"], wts["mlp_w_out"], 0,
                                                  "l0_mlp_bwd", after=[sent])
    grads["mlp_w_in"], grads["mlp_w_out"] = _mlp_weight_grads(h1, dhid0, r0, dmo0, 0, mlp_dw, "l0")
    sent = send_grads(["mlp_w_in", "mlp_w_out"], grads)
    dg1_0, _, dng01, dsc2_0, dsh2_0, dg2_0 = [acc0[k:k + 1] for k in range(6)]

    grads["rec_w_out"] = _mm(zb, dout0, "l0_out_proj_dw", ta=True, out_dtype=bf16, after=[sent])
    sent = send_grads(["rec_w_out"], grads)

    def f_out_gate_bwd(i, gp, y0_, y1_, dout, w):
        lat = jnp.where(i < ctx_tiles, 0.0, 1.0)
        dz = lax.dot_general(dout, w, nt, preferred_element_type=f32)
        _, vjp = jax.vjp(lambda a, b: _gelu(a) * b, gp, y0_ + y1_)
        dgp, dy = vjp(dz)
        return dgp * lat, dy * lat

    dgp, dy = _rowcall(f_out_gate_bwd, "l0_out_proj_gate_bwd", T_ALL, FUSED_TM,
                       [_rin(gr, R, 0), _rin(y0), _rin(y1), _rin(dout0, None, 0, -ctx_tiles)], [wts["rec_w_out"]],
                       [(R, bf16), (R, f32)], after=[sent])
    (dh_f,) = _scan_call(a0, dy, SCAN_FWD_BWD, True, "l0_scan_fwd_bwd", True)
    (dh_r,) = _scan_call(a1, dy, SCAN_REV_BWD, False, "l0_scan_rev_bwd", True)

    dpre, du_f, *dpar_f = _gates_coeff_bwd(ub, u, dh_f, yp0, *gate_args, 0, None)
    dpre, du_r, *dpar_r = _gates_coeff_bwd(ub, u, dh_r, yp1, *gate_args, 1, dpre)
    grads["rec_b_a"], grads["rec_b_x"], grads["rec_lambda"] = [
        jnp.concatenate([f.reshape(-1), r_.reshape(-1)]).reshape(2, R) for f, r_ in zip(dpar_f, dpar_r)]
    grads["gates"] = _gates_dw(ub, dpre)
    sent = send_grads(["replicated"], grads)
    du_gates = _gates_dx(dpre, wts["gates"], after=[sent])
    drec, dconv4_w, dconv4_b = _dwconv_bwd([du_f, du_r, du_gates], gr, R // 256, wts["rec_conv_w"], 4, 1,
                                           CONV_SEGMENTS, 256, "l0_conv_bwd", bf16)
    grads["rec_conv_w"], grads["rec_conv_b"] = dconv4_w, dconv4_b
    dgr = jnp.concatenate([dgp, drec], axis=1)
    grads["rec_w_in"] = _mm(h0, dgr, "l0_in_proj_dw", ta=True, out_dtype=bf16)
    sent = send_grads(["rec_w_in"], grads)

    def f_pre0_bwd(i, x0, dgr_, dxs, g, scp_, shp_, w):
        lat = jnp.where(i < ctx_tiles, 0.0, 1.0)
        dh = lax.dot_general(dgr_, w, nt, preferred_element_type=f32)
        _, vjp = jax.vjp(lambda a, b, c, e: _normmod(a, b, blend(i, c), blend(i, e)), x0, g, scp_, shp_)
        dx, dg, dscp, dshp = vjp(dh)
        return dx + lat * dxs, dg, dscp, dshp

    dx0cat, dng00, dscp, dshp = _rowcall(
        f_pre0_bwd, "l0_in_proj_prenorm_bwd", T_ALL, FUSED_TM,
        [_rin(x0cat), _rin(dgr), _rin(dx1, None, 0, -ctx_tiles)], [ng[0, 0][None], scp, shp, wts["rec_w_in"]],
        [(D, f32)], [(1, D), (2, D), (2, D)], after=[sent])

    grads["norm_g"] = jnp.stack([jnp.concatenate([dng00, dng01], 0), jnp.concatenate([dng10, dng11], 0)])
    dmods = jnp.stack([
        jnp.concatenate([dshp[1:2], dscp[1:2], dg1_0, dsh2_0, dsc2_0, dg2_0], axis=0),
        jnp.concatenate([dsh1_1, dsc1_1, dg1_1, dsh2_1, dsc2_1, dg2_1], axis=0)])
    dcmod = jnp.concatenate([dshp[0:1], dscp[0:1]], axis=0)
    return loss, dx0cat[T_CTX:], dmods, dcmod, grads


def _unshard_cols(g):
    g = jnp.moveaxis(g, 0, -2)
    return g.reshape(g.shape[:-2] + (g.shape[-2] * g.shape[-1],))


def _shard_cols(w):
    w = w.reshape(w.shape[:-1] + (N_DEV, w.shape[-1] // N_DEV))
    return jnp.moveaxis(w, -2, 0)


def _shard_rows(w):
    return w.reshape((N_DEV, w.shape[0] // N_DEV) + w.shape[1:])


SMALL_PACK_ROWS = 64
WIRE_COLS = 512
REPL_FINAL_G_ROWS = -(-D // BLK)
REPL_ROWS = -(-(2 * 2 * N_BLK * BLK + 2 * 2 * N_BLK + REPL_FINAL_G_ROWS) // 16) * 16


def kernel(x, c, ctx, c_ctx, w_ada, b_ada, norm_g, rec_w_in, rec_conv_w, rec_conv_b, rec_lambda, rec_w_a, rec_b_a, rec_w_x, rec_b_x, rec_w_out, conf_w_pw1, conf_b_pw1, conf_conv_w, conf_conv_b, conf_ln_g, conf_ln_b, conf_w_pw2, conf_b_pw2, mlp_w_in, mlp_w_out, final_g, loss_target, m_c_ctx, m_w_ada, m_b_ada, m_norm_g, m_rec_w_in, m_rec_conv_w, m_rec_conv_b, m_rec_lambda, m_rec_w_a, m_rec_b_a, m_rec_w_x, m_rec_b_x, m_rec_w_out, m_conf_w_pw1, m_conf_b_pw1, m_conf_conv_w, m_conf_conv_b, m_conf_ln_g, m_conf_ln_b, m_conf_w_pw2, m_conf_b_pw2, m_mlp_w_in, m_mlp_w_out, m_final_g, v_c_ctx, v_w_ada, v_b_ada, v_norm_g, v_rec_w_in, v_rec_conv_w, v_rec_conv_b, v_rec_lambda, v_rec_w_a, v_rec_b_a, v_rec_w_x, v_rec_b_x, v_rec_w_out, v_conf_w_pw1, v_conf_b_pw1, v_conf_conv_w, v_conf_conv_b, v_conf_ln_g, v_conf_ln_b, v_conf_w_pw2, v_conf_b_pw2, v_mlp_w_in, v_mlp_w_out, v_final_g):
    me = 4 * lax.axis_index("x") + 2 * lax.axis_index("y") + lax.axis_index("c")
    weights = dict(c_ctx=c_ctx, w_ada=w_ada, b_ada=b_ada, norm_g=norm_g, rec_w_in=rec_w_in, rec_conv_w=rec_conv_w,
                   rec_conv_b=rec_conv_b, rec_lambda=rec_lambda, rec_w_a=rec_w_a, rec_b_a=rec_b_a, rec_w_x=rec_w_x,
                   rec_b_x=rec_b_x, rec_w_out=rec_w_out, conf_w_pw1=conf_w_pw1, conf_b_pw1=conf_b_pw1,
                   conf_conv_w=conf_conv_w, conf_conv_b=conf_conv_b, conf_ln_g=conf_ln_g, conf_ln_b=conf_ln_b,
                   conf_w_pw2=conf_w_pw2, conf_b_pw2=conf_b_pw2, mlp_w_in=mlp_w_in, mlp_w_out=mlp_w_out, final_g=final_g)
    m_in = dict(c_ctx=m_c_ctx, w_ada=m_w_ada, b_ada=m_b_ada, norm_g=m_norm_g, rec_w_in=m_rec_w_in, rec_conv_w=m_rec_conv_w,
                rec_conv_b=m_rec_conv_b, rec_lambda=m_rec_lambda, rec_w_a=m_rec_w_a, rec_b_a=m_rec_b_a, rec_w_x=m_rec_w_x,
                rec_b_x=m_rec_b_x, rec_w_out=m_rec_w_out, conf_w_pw1=m_conf_w_pw1, conf_b_pw1=m_conf_b_pw1,
                conf_conv_w=m_conf_conv_w, conf_conv_b=m_conf_conv_b, conf_ln_g=m_conf_ln_g, conf_ln_b=m_conf_ln_b,
                conf_w_pw2=m_conf_w_pw2, conf_b_pw2=m_conf_b_pw2, mlp_w_in=m_mlp_w_in, mlp_w_out=m_mlp_w_out,
                final_g=m_final_g)
    v_in = dict(c_ctx=v_c_ctx, w_ada=v_w_ada, b_ada=v_b_ada, norm_g=v_norm_g, rec_w_in=v_rec_w_in, rec_conv_w=v_rec_conv_w,
                rec_conv_b=v_rec_conv_b, rec_lambda=v_rec_lambda, rec_w_a=v_rec_w_a, rec_b_a=v_rec_b_a, rec_w_x=v_rec_w_x,
                rec_b_x=v_rec_b_x, rec_w_out=v_rec_w_out, conf_w_pw1=v_conf_w_pw1, conf_b_pw1=v_conf_b_pw1,
                conf_conv_w=v_conf_conv_w, conf_conv_b=v_conf_conv_b, conf_ln_g=v_conf_ln_g, conf_ln_b=v_conf_ln_b,
                conf_w_pw2=v_conf_w_pw2, conf_b_pw2=v_conf_b_pw2, mlp_w_in=v_mlp_w_in, mlp_w_out=v_mlp_w_out,
                final_g=v_final_g)
    names = list(weights)

    small_items = [c, norm_g, rec_conv_w, rec_lambda, conf_b_pw1, conf_conv_w, conf_conv_b, conf_ln_g, conf_ln_b,
                   conf_b_pw2]
    flat = jnp.concatenate([a.reshape(-1) for a in small_items])
    flat = jnp.pad(flat, (0, SMALL_PACK_ROWS * 128 - flat.shape[0])).reshape(SMALL_PACK_ROWS, 128)
    as_shard = lambda a: a.astype(bf16).reshape(-1, a.shape[-1])
    early_srcs = [flat, as_shard(rec_w_in[0]).reshape(-1, WIRE_COLS)]
    early_handle, started = _exchange_start(early_srcs, [_own_block_filled(s, me) for s in early_srcs],
                                            "gather_early_start", False)
    zero = started[0, 0]
    gates = _gate_matrix(rec_w_a[0] + zero, rec_w_x[0] + zero)
    late_items = {"mlp": [rec_w_out[0], mlp_w_in, mlp_w_out], "conf": [conf_w_pw1[0], conf_w_pw2[0]]}
    late_shards = {g: [as_shard(a + zero) for a in items] for g, items in late_items.items()}
    late_lands = {g: [_own_block_filled(s, me) for s in shards] for g, shards in late_shards.items()}
    xcat, poscat = _token_rows(x[0] + zero, ctx[0])
    small_all, early = _exchange_wait(early_handle, [gates, xcat, poscat] + late_lands["mlp"] + late_lands["conf"],
                                      "gather_early_wait", False)

    small_all = small_all.reshape(N_DEV, -1)
    off = 0
    small = []
    for a in small_items:
        small.append(small_all[:, off:off + a.size].reshape((N_DEV,) + a.shape))
        off += a.size
    c_all, ng_all, rcw_all, lam_all, bpw1_all, ccw_all, ccb_all, lng_all, lnb_all, bpw2_all = small
    wts = {
        "norm_g": _unshard_cols(ng_all),
        "rec_conv_w": _unshard_cols(rcw_all)[0],
        "rec_lambda": _unshard_cols(lam_all)[0],
        "conf_b_pw1": _unshard_cols(bpw1_all),
        "conf_conv_w": _unshard_cols(ccw_all)[0],
        "conf_conv_b": _unshard_cols(ccb_all),
        "conf_ln_g": _unshard_cols(lng_all),
        "conf_ln_b": _unshard_cols(lnb_all),
        "conf_b_pw2": _unshard_cols(bpw2_all),
        "rec_conv_b": rec_conv_b,
        "rec_b_a": rec_b_a[0].reshape(2, R),
        "rec_b_x": rec_b_x[0].reshape(2, R),
        "final_g": final_g[None],
        "gates": gates,
    }

    c16 = jnp.concatenate([c_all[:, 0], jnp.broadcast_to(c_ctx[None], (8, D))], axis=0)
    b_loc = lax.dynamic_slice_in_dim(b_ada, me * ADA_SHARD, ADA_SHARD, axis=1)[:, None]
    (mods_gathered,) = _all_gather([_ada_forward(c16, w_ada, b_loc)], "gather_mods")
    mods_all = _unshard_cols(mods_gathered)
    mods = lax.dynamic_index_in_dim(mods_all, me, axis=1, keepdims=False).reshape(2, N_MOD, D)
    cmod = mods_all[0, 8, :2 * D].reshape(2, D)

    late_handles = {}
    late_handles["mlp"], token = _gather2_start(late_shards["mlp"], late_lands["mlp"], "gather_mlp_start",
                                                [early, mods_gathered])
    order = [token]
    early = early.reshape((N_DEV,) + rec_w_in.shape[1:]) + token[0, 0].astype(bf16)
    wts["rec_w_in"] = _unshard_cols(early)

    def late_weights(group, after):
        if group == "mlp_halfway":
            late_handles["mlp"] = _gather2_forward1(late_handles["mlp"], after, "gather_mlp_forward1")
            return late_handles["mlp"][2][0]
        if group == "mlp":
            passed = _gather2_forward2(late_handles["mlp"], after, "gather_mlp_forward2")
            late_handles["conf"], started = _exchange_start(late_shards["conf"], late_lands["conf"], "gather_conf_start",
                                                            False, after=[passed[2][0]])
            got = _gather2_wait(passed, started, "gather_mlp_wait")
        else:
            got = _exchange_wait(late_handles[group], after, "gather_conf_wait", False)
        got = [g.reshape((N_DEV,) + a.shape) for g, a in zip(got, late_items[group])]
        if group == "mlp":
            return {"rec_w_out": got[0].reshape(R, D), "mlp_w_in": got[1], "mlp_w_out": got[2]}
        return {"conf_w_pw1": _unshard_cols(got[0]), "conf_w_pw2": got[1].reshape(D, D)}

    to_blocks = {"rec_w_in": lambda g: _shard_cols(g).reshape(N_DEV, -1, WIRE_COLS), "conf_w_pw1": _shard_cols, "rec_w_out": _shard_rows, "conf_w_pw2": _shard_rows,
                 "mlp_w_in": lambda g: g, "mlp_w_out": lambda g: g}
    grad_handles = []

    repl_names = ["rec_w_a", "rec_w_x", "rec_b_a", "rec_b_x", "final_g"]

    def send_replicated(grads):
        dwg = grads["gates"]
        repl = {"rec_w_a": jnp.stack([_gate_blocks(dwg, 0), _gate_blocks(dwg, 2)]),
                "rec_w_x": jnp.stack([_gate_blocks(dwg, 1), _gate_blocks(dwg, 3)]),
                "rec_b_a": grads["rec_b_a"], "rec_b_x": grads["rec_b_x"],
                "final_g": jnp.pad(grads["final_g"], ((0, 0), (0, REPL_FINAL_G_ROWS * BLK - D)))}
        flat = jnp.concatenate([repl[n].reshape(-1, BLK) for n in repl_names], axis=0)
        flat = jnp.pad(flat, ((0, REPL_ROWS - flat.shape[0]), (0, 0))).astype(bf16)
        flat = flat.reshape(REPL_ROWS // 8, 8 * BLK)
        handle, sent = _exchange_start([flat], [_own_block_filled(flat, me)], "gather_replicated_start", False)
        grad_handles.append((["replicated"], handle))
        return sent

    def send_grads(group, grads):
        if group == ["replicated"]:
            return send_replicated(grads)
        blocks = [to_blocks[n](grads[n]) for n in group]
        blocks = [g.reshape(N_DEV, -1, g.shape[-1]) for g in blocks]
        lands = [_own_block_filled(lax.dynamic_index_in_dim(g, me, 0, keepdims=False), me) for g in blocks]
        handle, sent = _exchange_start(blocks, lands, "scatter_start_" + group[0], True)
        grad_handles.append((group, handle))
        return sent

    loss, grad_x, dmods, dcmod, grads = _local_step(
        xcat, poscat, loss_target[0], mods, cmod, wts, late_weights, send_grads,
        lambda partial: lax.psum(partial, ("x", "y", "c")), start_after=order)

    def as2d(shape):
        rows = 1
        for s in shape[:-1]:
            rows *= s
        return (rows, shape[-1])

    def whole(arr, shape):
        arr = arr.reshape((-1,) + as2d(shape))
        return (arr, arr.shape[0])

    shard_shapes = {n: weights[n].shape for n in names}
    g_out, d_out, m_out, v_out = {}, {}, {}, {}

    def adamw(n, pieces, after):
        shape = shard_shapes[n]
        r2, c2 = as2d(shape)
        g, dl, nm, nv = _adamw(pieces, weights[n].reshape(r2, c2), m_in[n].reshape(r2, c2), v_in[n].reshape(r2, c2),
                               "adamw_" + n, after=after)
        g_out[n], d_out[n], m_out[n], v_out[n] = (t.reshape(shape) for t in (g, dl, nm, nv))
        return g

    small_sharded = ["norm_g", "rec_conv_w", "rec_lambda", "conf_b_pw1", "conf_conv_w", "conf_conv_b", "conf_ln_g",
                     "conf_ln_b", "conf_b_pw2"]
    pack = jnp.concatenate([_shard_cols(grads[n]).reshape(N_DEV, -1) for n in small_sharded], axis=1)
    pack = jnp.pad(pack, ((0, 0), (0, SMALL_PACK_ROWS * 128 - pack.shape[1]))).reshape(N_DEV, SMALL_PACK_ROWS, 128)
    small_handle, token = _exchange_start(
        [pack], [_own_block_filled(lax.dynamic_index_in_dim(pack, me, 0, keepdims=False), me)], "scatter_small_start",
        True, after=[grad_x])
    dm_flat = jnp.concatenate([dmods.reshape(-1), dcmod.reshape(-1), grads["rec_conv_b"].reshape(-1)])
    dm_len = dm_flat.shape[0]
    dm_flat = jnp.pad(dm_flat, (0, 128 * 128 - dm_len)).reshape(128, 128)
    dm_handle, token = _exchange_start([dm_flat], [_own_block_filled(dm_flat, me)], "gather_dmods_start", False,
                                       after=[token])

    done = token
    for group, handle in grad_handles:
        if group == ["replicated"]:
            repl_all = _exchange_wait(handle, done, "gather_replicated_wait", False)[0]
            repl_all = repl_all.reshape(N_DEV, REPL_ROWS, BLK)
            row = 0
            for n in repl_names:
                n_rows = -(-weights[n].size // BLK)
                if as2d(shard_shapes[n]) == (n_rows, BLK) and row % 256 == 0:
                    done = adamw(n, [(repl_all, N_DEV, row)], [done])
                else:
                    got = repl_all[:, row:row + n_rows].reshape(N_DEV, -1)[:, :weights[n].size]
                    done = adamw(n, [whole(got, shard_shapes[n])], [done])
                row += n_rows
            continue
        for n, got in zip(group, _exchange_wait(handle, done, "scatter_wait_" + group[0], True)):
            got = got.reshape((N_DEV,) + as2d(shard_shapes[n]))
            done = adamw(n, [(got, N_DEV)], [done])

    dm_all = _exchange_wait(dm_handle, done, "gather_dmods_wait", False)[0].reshape(N_DEV, -1)
    dmods_all = dm_all[:, :2 * N_MOD * D].reshape(N_DEV, 2, N_MOD * D)
    dcmod_all = jnp.pad(dm_all[:, 2 * N_MOD * D:2 * N_MOD * D + 2 * D], ((0, 0), (0, (N_MOD - 2) * D)))
    g16_full = jnp.stack([jnp.concatenate([dmods_all[:, 0], dcmod_all], axis=0),
                          jnp.concatenate([dmods_all[:, 1], jnp.zeros_like(dcmod_all)], axis=0)])
    g16 = lax.dynamic_slice_in_dim(g16_full, me * ADA_SHARD, ADA_SHARD, axis=2)
    dw_ada, ds_part = _ada_backward(c16, g16, w_ada)
    ds_handle, token = _exchange_start([ds_part[0]], [_own_block_filled(ds_part[0], me)], "gather_dsilu_start", False)
    done = adamw("w_ada", [whole(dw_ada, shard_shapes["w_ada"])], [token])
    done = adamw("rec_conv_b", [whole(dm_all[:, dm_len - R:dm_len], shard_shapes["rec_conv_b"])], [done])
    db_terms = jnp.concatenate([dmods_all, jnp.stack([dcmod_all, jnp.zeros_like(dcmod_all)], axis=1)], axis=0)
    done = adamw("b_ada", [whole(db_terms, shard_shapes["b_ada"])], [done])
    pack_recv = _exchange_wait(small_handle, done, "scatter_small_wait", True)[0].reshape(N_DEV, -1)
    off = 0
    for n in small_sharded:
        size = weights[n].size
        done = adamw(n, [whole(pack_recv[:, off:off + size], shard_shapes[n])], [done])
        off += size
    ds_all = _exchange_wait(ds_handle, done, "gather_dsilu_wait", False)[0]
    adamw("c_ctx", [whole(ds_all[:, 0], shard_shapes["c_ctx"])], [])

    return (loss, grad_x[None], *[g_out[n] for n in names], *[d_out[n] for n in names],
            *[m_out[n] for n in names], *[v_out[n] for n in names])
```

```python
import functools

import jax
import jax.numpy as jnp
from jax import lax
from jax.experimental import pallas as pl
from jax.experimental.pallas import tpu as pltpu

f32 = jnp.float32
bf16 = jnp.bfloat16

N_DEV = 8
D = 1024
T_LAT = 2048
T_CTX = 256
T_ALL = T_CTX + T_LAT
R = 1280
N_BLK = 16
BLK = R // N_BLK
F = 4096
GRID_W = 64
RG_C = 8.0
EPS = 1e-6
POS_BASE = 10000.0
N_MOD = 6
ADA_SHARD = N_MOD * D // N_DEV

ADAM_LR = 0.001
ADAM_B1 = 0.9
ADAM_B2 = 0.999
ADAM_EPS = 1e-08
ADAM_WD = 0.01
ADAM_STEP = 10

VMEM_LIMIT_V7X = 56 * 1024 * 1024
HALO = 16
MESH = pl.DeviceIdType.MESH


def _cparams(*sem):
    return pltpu.CompilerParams(dimension_semantics=sem, vmem_limit_bytes=VMEM_LIMIT_V7X)


def _pick(n, cands):
    for c in cands:
        if n % c == 0:
            return c
    raise ValueError(f"no block size for {n}")


def _position():
    x, y, c = lax.axis_index("x"), lax.axis_index("y"), lax.axis_index("c")
    return x, y, c, 4 * x + 2 * y + c


def _peer(x, y, c, k):
    px = (1 - x) if (k >> 2) & 1 else x
    py = (1 - y) if (k >> 1) & 1 else y
    pc = (1 - c) if k & 1 else c
    return (px, py, pc), 4 * px + 2 * py + pc


def _exchange(arrs, name, scatter):
    n = len(arrs)

    def body(*refs):
        ins, outs = refs[:n], refs[n:2 * n]
        send_sems, recv_sems, local_sems = refs[2 * n:]
        x, y, c, me = _position()
        local = []
        for a in range(n):
            src = ins[a].at[me] if scatter else ins[a]
            cp = pltpu.make_async_copy(src, outs[a].at[me], local_sems.at[a])
            cp.start()
            local.append(cp)
        sends, recvs = [], []
        for a in range(n):
            for k in range(1, N_DEV):
                peer, peer_lin = _peer(x, y, c, k)
                src = ins[a].at[peer_lin] if scatter else ins[a]
                cp = pltpu.make_async_remote_copy(
                    src_ref=src, dst_ref=outs[a].at[me], send_sem=send_sems.at[a, k - 1],
                    recv_sem=recv_sems.at[a, k - 1], device_id=peer, device_id_type=MESH)
                cp.start()
                sends.append(cp)
                recvs.append(pltpu.make_async_remote_copy(
                    src_ref=src, dst_ref=outs[a].at[peer_lin], send_sem=send_sems.at[a, k - 1],
                    recv_sem=recv_sems.at[a, k - 1], device_id=peer, device_id_type=MESH))
        for cp in recvs:
            cp.wait_recv()
        for cp in sends:
            cp.wait_send()
        for cp in local:
            cp.wait()

    if scatter:
        out_shape = [jax.ShapeDtypeStruct(a.shape, a.dtype) for a in arrs]
    else:
        out_shape = [jax.ShapeDtypeStruct((N_DEV,) + a.shape, a.dtype) for a in arrs]
    any_spec = pl.BlockSpec(memory_space=pl.ANY)
    return pl.pallas_call(
        body, name=name, out_shape=out_shape,
        in_specs=[any_spec] * n, out_specs=[any_spec] * n,
        scratch_shapes=[pltpu.SemaphoreType.DMA((n, N_DEV - 1)), pltpu.SemaphoreType.DMA((n, N_DEV - 1)),
                        pltpu.SemaphoreType.DMA((n,))],
    )(*arrs)


def _all_gather(arrs, name):
    return _exchange(arrs, name, scatter=False)


def _lin(p):
    return 4 * p[0] + 2 * p[1] + p[2]


HBM_SPEC = pl.BlockSpec(memory_space=pltpu.HBM)
SEM_SPEC = pl.BlockSpec(memory_space=pltpu.SEMAPHORE)
DATAFLOW_EFFECT = pltpu.SideEffectType.DATAFLOW_SIDE_EFFECTING


def _split_copies(srcs, lands, send_sems, recv_sems, scatter):
    x, y, c, me = _position()
    out = []
    for a in range(len(srcs)):
        for k in range(1, N_DEV):
            peer, peer_lin = _peer(x, y, c, k)
            src = srcs[a].at[peer_lin] if scatter else srcs[a]
            mk = lambda slot: pltpu.make_async_remote_copy(
                src_ref=src, dst_ref=lands[a].at[slot], send_sem=send_sems.at[a * (N_DEV - 1) + k - 1],
                recv_sem=recv_sems.at[a * (N_DEV - 1) + k - 1], device_id=peer, device_id_type=MESH)
            out.append((mk(me), mk(peer_lin)))
    return out


def _exchange_start(srcs, lands, name, scatter, after=()):
    n = len(srcs)
    n_after = len(after)

    def body(*refs):
        srcs_r, lands_r = refs[:n], refs[n:2 * n]
        send_sems, recv_sems = refs[2 * n + n_after], refs[2 * n + n_after + 1]
        token = refs[-1]
        for outgoing, _ in _split_copies(srcs_r, lands_r, send_sems, recv_sems, scatter):
            outgoing.start()
        token[...] = jnp.zeros_like(token)

    hbm = lambda a: pltpu.HBM(a.shape, a.dtype)
    res = pl.pallas_call(
        body, name=name,
        out_shape=(pltpu.SemaphoreType.DMA((n * (N_DEV - 1),)), pltpu.SemaphoreType.DMA((n * (N_DEV - 1),)),
                   *[hbm(a) for a in srcs], *[hbm(a) for a in lands], jax.ShapeDtypeStruct((8, 128), f32)),
        in_specs=[HBM_SPEC] * (2 * n) + [pl.BlockSpec(memory_space=pl.ANY)] * n_after,
        out_specs=(SEM_SPEC, SEM_SPEC, *[HBM_SPEC] * (2 * n), pl.BlockSpec(memory_space=pltpu.VMEM)),
        input_output_aliases={i: 2 + i for i in range(2 * n)},
        compiler_params=pltpu.CompilerParams(has_side_effects=DATAFLOW_EFFECT),
    )(*[pltpu.with_memory_space_constraint(a, pltpu.HBM) for a in list(srcs) + list(lands)], *after)
    return (res[0], res[1], list(res[2:2 + n]), list(res[2 + n:2 + 2 * n])), res[-1]


def _exchange_wait(handle, after, name, scatter):
    send_sems, recv_sems, srcs, lands = handle
    n = len(srcs)
    after = list(after) if isinstance(after, (list, tuple)) else [after]

    def body(*refs):
        srcs_r, lands_r = refs[:n], refs[n:2 * n]
        send_s, recv_s = refs[2 * n], refs[2 * n + 1]
        for outgoing, incoming in _split_copies(srcs_r, lands_r, send_s, recv_s, scatter):
            outgoing.wait_send()
            incoming.wait_recv()

    hbm = lambda a: pltpu.HBM(a.shape, a.dtype)
    res = pl.pallas_call(
        body, name=name, out_shape=tuple(hbm(a) for a in list(srcs) + list(lands)),
        in_specs=[HBM_SPEC] * (2 * n) + [SEM_SPEC, SEM_SPEC] + [pl.BlockSpec(memory_space=pl.ANY)] * len(after),
        out_specs=tuple([HBM_SPEC] * (2 * n)),
        input_output_aliases={i: i for i in range(2 * n)},
        compiler_params=pltpu.CompilerParams(has_side_effects=DATAFLOW_EFFECT),
    )(*srcs, *lands, send_sems, recv_sems, *after)
    return list(res[n:])


def _split_call(body, name, hbm_ins, kept, in_sems, n_new_sems, after, with_token):
    n_in, n_sem = len(hbm_ins), len(in_sems)
    out_shape, out_specs = [], []
    if n_new_sems:
        out_shape += [pltpu.SemaphoreType.DMA((n_new_sems,))] * 2
        out_specs += [SEM_SPEC] * 2
    first_kept = len(out_shape)
    out_shape += [pltpu.HBM(hbm_ins[i].shape, hbm_ins[i].dtype) for i in kept]
    out_specs += [HBM_SPEC] * len(kept)
    if with_token:
        out_shape.append(jax.ShapeDtypeStruct((8, 128), f32))
        out_specs.append(pl.BlockSpec(memory_space=pltpu.VMEM))

    def wrapped(*refs):
        outs = refs[n_in + n_sem + len(after):]
        body(refs[:n_in], refs[n_in:n_in + n_sem], outs[:2] if n_new_sems else ())
        if with_token:
            outs[-1][...] = jnp.zeros_like(outs[-1])

    return pl.pallas_call(
        wrapped, name=name, out_shape=tuple(out_shape),
        in_specs=[HBM_SPEC] * n_in + [SEM_SPEC] * n_sem + [pl.BlockSpec(memory_space=pl.ANY)] * len(after),
        out_specs=tuple(out_specs), input_output_aliases={i: first_kept + j for j, i in enumerate(kept)},
        compiler_params=pltpu.CompilerParams(has_side_effects=DATAFLOW_EFFECT),
    )(*[pltpu.with_memory_space_constraint(a, pltpu.HBM) for a in hbm_ins], *in_sems, *after)


def _rcopy(src, dst, sems, k, to):
    return pltpu.make_async_remote_copy(src_ref=src, dst_ref=dst, send_sem=sems[0].at[k], recv_sem=sems[1].at[k],
                                        device_id=to, device_id_type=MESH)


def _gather2_start(shards, lands, name, after):
    n = len(shards)

    def body(ins, sems_in, sems_out):
        x, y, c, me = _position()
        for a in range(n):
            for k, to in enumerate(((x, y, 1 - c), (1 - x, y, c), (x, 1 - y, c))):
                _rcopy(ins[a], ins[n + a].at[me], sems_out, 3 * a + k, to).start()

    res = _split_call(body, name, list(shards) + list(lands), range(2 * n), (), 3 * n, after, True)
    return (res[0], res[1], list(res[2:2 + n]), list(res[2 + n:2 + 2 * n])), res[-1]


def _gather2_forward1(handle, after, name):
    send_sems, recv_sems, srcs, lands = handle
    n = len(srcs)

    def body(ins, sems_in, sems_out):
        x, y, c, me = _position()
        sib, xn, yn = (x, y, 1 - c), (1 - x, y, c), (x, 1 - y, c)
        for a in range(n):
            for k, peer in enumerate((sib, xn, yn)):
                _rcopy(ins[a], ins[n + a].at[me], sems_in, 3 * a + k, peer).wait_send()
                _rcopy(ins[a], ins[n + a].at[_lin(peer)], sems_in, 3 * a + k, peer).wait_recv()
        for a in range(n):
            land = ins[n + a]
            _rcopy(land.at[_lin(xn)], land.at[_lin(xn)], sems_out, 3 * a, sib).start()
            _rcopy(land.at[_lin(yn)], land.at[_lin(yn)], sems_out, 3 * a + 1, sib).start()

            @pl.when(c == 0)
            def _():
                _rcopy(land.at[_lin(xn)], land.at[_lin(xn)], sems_out, 3 * a + 2, yn).start()

            @pl.when(c == 1)
            def _():
                _rcopy(land.at[_lin(yn)], land.at[_lin(yn)], sems_out, 3 * a + 2, xn).start()

    res = _split_call(body, name, list(srcs) + list(lands), range(n, 2 * n), (send_sems, recv_sems), 3 * n, [after], False)
    return (res[0], res[1], list(res[2:]))


def _gather2_forward2(handle, after, name):
    send_sems, recv_sems, lands = handle
    n = len(lands)

    def body(ins, sems_in, sems_out):
        x, y, c, me = _position()
        sib, dg = (x, y, 1 - c), _lin((1 - x, 1 - y, c))
        for a in range(n):
            for k, slot in enumerate((_lin((1 - x, y, 1 - c)), _lin((x, 1 - y, 1 - c)), dg)):
                done = _rcopy(ins[a].at[slot], ins[a].at[slot], sems_in, 3 * a + k, sib)
                done.wait_send()
                done.wait_recv()
        for a in range(n):
            _rcopy(ins[a].at[dg], ins[a].at[dg], sems_out, a, sib).start()

    res = _split_call(body, name, list(lands), range(n), (send_sems, recv_sems), n, [after], False)
    return (res[0], res[1], list(res[2:]))


def _gather2_wait(handle, after, name):
    send_sems, recv_sems, lands = handle
    n = len(lands)

    def body(ins, sems_in, sems_out):
        x, y, c, me = _position()
        slot = _lin((1 - x, 1 - y, 1 - c))
        for a in range(n):
            done = _rcopy(ins[a].at[slot], ins[a].at[slot], sems_in, a, (x, y, 1 - c))
            done.wait_send()
            done.wait_recv()

    return list(_split_call(body, name, list(lands), range(n), (send_sems, recv_sems), 0, [after], False))


def _own_block_filled(block, me):
    land = lax.empty((N_DEV,) + block.shape, block.dtype)
    return lax.dynamic_update_index_in_dim(land, block, me, 0)


ANY_SPEC = pl.BlockSpec(memory_space=pl.ANY)


def _mm(a, b, name, ta=False, tb=False, out_dtype=f32, after=()):
    if ta:
        k_dim, m_dim = a.shape
    else:
        m_dim, k_dim = a.shape
    if tb:
        n_dim, k2 = b.shape
    else:
        k2, n_dim = b.shape
    assert k_dim == k2, (a.shape, b.shape)
    assert a.dtype == bf16 and b.dtype == bf16
    bm = _pick(m_dim, (512, 768, 640, 256, 128))
    bn = _pick(n_dim, (512, 640, 256, 128))
    bk = k_dim if k_dim <= 2560 else _pick(k_dim, (1024, 1280, 768, 512))
    nk = k_dim // bk
    a_spec = (pl.BlockSpec((bk, bm), lambda i, j, k: (k, i)) if ta
              else pl.BlockSpec((bm, bk), lambda i, j, k: (i, k)))
    b_spec = (pl.BlockSpec((bn, bk), lambda i, j, k: (j, k)) if tb
              else pl.BlockSpec((bk, bn), lambda i, j, k: (k, j)))
    dims = (((0 if ta else 1,), (1 if tb else 0,)), ((), ()))

    n_after = len(after)

    def body_single(a_ref, b_ref, *rest):
        o_ref = rest[n_after]
        o_ref[...] = lax.dot_general(a_ref[...], b_ref[...], dims, preferred_element_type=f32).astype(o_ref.dtype)

    def body(a_ref, b_ref, *rest):
        o_ref, acc_ref = rest[n_after:]
        k = pl.program_id(2)

        @pl.when(k == 0)
        def _():
            acc_ref[...] = jnp.zeros_like(acc_ref)

        acc_ref[...] += lax.dot_general(a_ref[...], b_ref[...], dims, preferred_element_type=f32)

        @pl.when(k == nk - 1)
        def _():
            o_ref[...] = acc_ref[...].astype(o_ref.dtype)

    return pl.pallas_call(
        body_single if nk == 1 else body, name=name, out_shape=jax.ShapeDtypeStruct((m_dim, n_dim), out_dtype),
        grid=(m_dim // bm, n_dim // bn, nk), in_specs=[a_spec, b_spec] + [ANY_SPEC] * n_after,
        out_specs=pl.BlockSpec((bm, bn), lambda i, j, k: (i, j)),
        scratch_shapes=[] if nk == 1 else [pltpu.VMEM((bm, bn), f32)],
        compiler_params=_cparams("parallel", "parallel", "arbitrary"),
    )(a, b, *after)


def _rin(arr, width=None, cb=0, roff=0):
    return (arr, arr.shape[1] if width is None else width, cb, roff)


def _rowcall(fn, name, rows, tm, row_ins, par_ins, row_outs, acc_outs=(), after=()):
    nr, npar, nro, n_after = len(row_ins), len(par_ins), len(row_outs), len(after)
    in_specs, args = [], []
    for arr, width, cb, roff in row_ins:
        if roff >= 0:
            imap = lambda i, cb=cb, roff=roff: (i + roff, cb)
        else:
            imap = lambda i, cb=cb, roff=roff: (jnp.maximum(i + roff, 0), cb)
        in_specs.append(pl.BlockSpec((tm, width), imap))
        args.append(arr)
    for p in par_ins:
        in_specs.append(pl.BlockSpec(p.shape, lambda i: (0, 0)))
        args.append(p)
    out_shape, out_specs = [], []
    for width, dt in row_outs:
        out_shape.append(jax.ShapeDtypeStruct((rows, width), dt))
        out_specs.append(pl.BlockSpec((tm, width), lambda i: (i, 0)))
    for p, width in acc_outs:
        out_shape.append(jax.ShapeDtypeStruct((p, width), f32))
        out_specs.append(pl.BlockSpec((p, width), lambda i: (0, 0)))

    def body(*refs):
        i = pl.program_id(0)
        res = fn(i, *[r[...] for r in refs[:nr + npar]])
        outs = refs[nr + npar + n_after:]
        for o, v in zip(outs[:nro], res[:nro]):
            o[...] = v.astype(o.dtype)
        if acc_outs:
            @pl.when(i == 0)
            def _():
                for o in outs[nro:]:
                    o[...] = jnp.zeros_like(o)

            for o, v in zip(outs[nro:], res[nro:]):
                o[...] += v

    return pl.pallas_call(
        body, name=name, out_shape=out_shape, grid=(rows // tm,), in_specs=in_specs + [ANY_SPEC] * n_after,
        out_specs=out_specs, compiler_params=_cparams("arbitrary"),
    )(*args, *after)


def _rms(x, g):
    return x * lax.rsqrt(jnp.mean(x * x, axis=-1, keepdims=True) + EPS) * g


def _normmod(x, g, sc, sh):
    return _rms(x, g) * (1.0 + sc) + sh


def _gelu(x):
    return 0.5 * x * (1.0 + jnp.tanh(0.7978845608028654 * (x + 0.044715 * (x * x * x))))


def _sigmoid(x):
    return 0.5 * (jnp.tanh(0.5 * x) + 1.0)


def _coeff_parts(pre_a, pre_x, ba, bx, lam):
    r = _sigmoid(pre_a + ba)
    ig = _sigmoid(pre_x + bx)
    nl = -lam
    sp = jnp.maximum(nl, 0.0) + jnp.log(1.0 + jnp.exp(-jnp.abs(nl)))
    la = -RG_C * r * sp
    a = jnp.exp(la)
    one_minus_a2 = -jnp.tanh(la) * (a * a + 1.0)
    inv_m = lax.rsqrt(one_minus_a2)
    return r, ig, sp, a, one_minus_a2 * inv_m, inv_m


def _coeff(pre_a, pre_x, u, ba, bx, lam):
    _, ig, _, a, m, _ = _coeff_parts(pre_a, pre_x, ba, bx, lam)
    return a, m * (ig * u)


def _coeff_bwd(pre_a, pre_x, u, ba, bx, lam, da, db):
    r, ig, sp, a, m, inv_m = _coeff_parts(pre_a, pre_x, ba, bx, lam)
    dbu = db * u
    dig = dbu * m
    dm = dbu * ig
    dla = a * (da - dm * a * inv_m)
    dpa = dla * (-RG_C * sp) * (r * (1.0 - r))
    dpx = dig * (ig * (1.0 - ig))
    dsp = jnp.sum(dla * (-RG_C * r), axis=0, keepdims=True)
    dlam = -dsp * _sigmoid(-lam)
    return (dpa, dpx, db * m * ig, jnp.sum(dpa, axis=0, keepdims=True), jnp.sum(dpx, axis=0, keepdims=True), dlam)


SCAN_CHUNK = 256


def _scan_call(a, v, chunk_of, reverse, name, backward, after=()):
    rows, width = a.shape
    n_out = 1 if backward else 2
    nt = SCAN_CHUNK // 8

    def body(a_ref, v_ref, *rest):
        outs, state_ref = rest[len(after):-1], rest[-1]

        @pl.when(pl.program_id(0) == 0)
        def _():
            state_ref[...] = jnp.zeros_like(state_ref)

        rid = lax.broadcasted_iota(jnp.int32, (8, width), 0)
        last_row = 0 if reverse else 7

        def shift(x, s, fill):
            rolled = pltpu.roll(x, (8 - s) if reverse else s, axis=0)
            return jnp.where((rid >= 8 - s) if reverse else (rid < s), fill, rolled)

        def tile(j, st):
            t0 = pl.multiple_of((nt - 1 - j if reverse else j) * 8, 8)
            at = a_ref[pl.ds(t0, 8), :]
            coef = shift(at, 1, 1.0) if backward else at
            acc = v_ref[pl.ds(t0, 8), :]
            for s in (1, 2, 4):
                acc = coef * shift(acc, s, 0.0) + acc
                coef = coef * shift(coef, s, 1.0)
            out = coef * st + acc
            outs[0][pl.ds(t0, 8), :] = out
            last = out[last_row:last_row + 1]
            if backward:
                return at[last_row:last_row + 1] * last
            outs[1][pl.ds(t0, 8), :] = shift(out, 1, st)
            return last

        state_ref[0:1, :] = lax.fori_loop(0, nt, tile, state_ref[0:1, :])

    spec = pl.BlockSpec((SCAN_CHUNK, width), lambda t: (chunk_of(t), 0))
    return pl.pallas_call(
        body, name=name, out_shape=[jax.ShapeDtypeStruct((rows, width), f32)] * n_out,
        grid=(rows // SCAN_CHUNK,), in_specs=[spec, spec] + [ANY_SPEC] * len(after), out_specs=[spec] * n_out,
        scratch_shapes=[pltpu.VMEM((8, width), f32)],
        compiler_params=_cparams("arbitrary"),
    )(a, v, *after)


CONV_CHUNK = 256


def _fill_padded(pad_ref, src_ref, start, n):
    cb = pad_ref.shape[1]
    pad_ref[pl.ds(0, HALO), :] = jnp.zeros((HALO, cb), f32)
    pad_ref[pl.ds(HALO, n), :] = src_ref[pl.ds(start, n), :].astype(f32)
    pad_ref[pl.ds(HALO + n, HALO), :] = jnp.zeros((HALO, cb), f32)


def _dwconv_fwd(x, x_cb0, w, b, taps, pad_left, segments, cb, name, emit_bf16):
    rows = x.shape[0]
    width = w.shape[1]

    def body(x_ref, w_ref, b_ref, *rest):
        outs, xp = rest[:-1], rest[-1]
        for start, n in segments:
            _fill_padded(xp, x_ref, start, n)
            for c0 in range(0, n, CONV_CHUNK):
                acc = jnp.zeros((CONV_CHUNK, cb), f32) + b_ref[...]
                for k in range(taps):
                    acc = acc + w_ref[k:k + 1, :] * xp[pl.ds(HALO + c0 + k - pad_left, CONV_CHUNK), :]
                for o in outs:
                    o[pl.ds(start + c0, CONV_CHUNK), :] = acc.astype(o.dtype)

    out_dtypes = [f32, bf16] if emit_bf16 else [f32]
    return pl.pallas_call(
        body, name=name, out_shape=[jax.ShapeDtypeStruct((rows, width), dt) for dt in out_dtypes],
        grid=(width // cb,),
        in_specs=[pl.BlockSpec((rows, cb), lambda j: (0, j + x_cb0)), pl.BlockSpec((taps, cb), lambda j: (0, j)),
                  pl.BlockSpec((1, cb), lambda j: (0, j))],
        out_specs=[pl.BlockSpec((rows, cb), lambda j: (0, j))] * len(out_dtypes),
        scratch_shapes=[pltpu.VMEM((rows + 2 * HALO, cb), f32)],
        compiler_params=_cparams("parallel"),
    )(x, w, b)


def _dwconv_bwd(douts, x, x_cb0, w, taps, pad_left, segments, cb, name, dx_dtype):
    rows = x.shape[0]
    width = w.shape[1]
    nd = len(douts)

    def body(*refs):
        d_refs, x_ref, w_ref = refs[:nd], refs[nd], refs[nd + 1]
        dx_ref, dw_ref, db_ref, dp, dsum = refs[nd + 2:]
        dw_ref[...] = jnp.zeros_like(dw_ref)
        db_ref[...] = jnp.zeros_like(db_ref)
        if nd > 1:
            total = d_refs[0][...]
            for r in d_refs[1:]:
                total = total + r[...]
            dsum[...] = total
            d_ref = dsum
        else:
            d_ref = d_refs[0]
        for start, n in segments:
            _fill_padded(dp, d_ref, start, n)
            for c0 in range(0, n, CONV_CHUNK):
                db_ref[...] += jnp.sum(dp[pl.ds(HALO + c0, CONV_CHUNK), :], axis=0, keepdims=True)
                xchunk = x_ref[pl.ds(start + c0, CONV_CHUNK), :].astype(f32)
                acc = jnp.zeros((CONV_CHUNK, cb), f32)
                for k in range(taps):
                    shifted = dp[pl.ds(HALO + c0 + pad_left - k, CONV_CHUNK), :]
                    acc = acc + w_ref[k:k + 1, :] * shifted
                    dw_ref[k:k + 1, :] += jnp.sum(shifted * xchunk, axis=0, keepdims=True)
                dx_ref[pl.ds(start + c0, CONV_CHUNK), :] = acc.astype(dx_ref.dtype)

    dspec = pl.BlockSpec((rows, cb), lambda j: (0, j))
    return pl.pallas_call(
        body, name=name,
        out_shape=[jax.ShapeDtypeStruct((rows, width), dx_dtype), jax.ShapeDtypeStruct((taps, width), f32),
                   jax.ShapeDtypeStruct((1, width), f32)],
        grid=(width // cb,),
        in_specs=[dspec] * nd + [pl.BlockSpec((rows, cb), lambda j: (0, j + x_cb0)),
                                 pl.BlockSpec((taps, cb), lambda j: (0, j))],
        out_specs=[dspec, pl.BlockSpec((taps, cb), lambda j: (0, j)), pl.BlockSpec((1, cb), lambda j: (0, j))],
        scratch_shapes=[pltpu.VMEM((rows + 2 * HALO, cb), f32), pltpu.VMEM((rows, cb), f32)],
        compiler_params=_cparams("parallel"),
    )(*douts, x, w)


def _ada_forward(c16, w_ada, b_loc):
    def body(c_ref, w_ref, b_ref, o_ref):
        cv = c_ref[...]
        s = (cv * _sigmoid(cv)).astype(bf16)
        o_ref[0] = jnp.dot(s, w_ref[0].astype(bf16), preferred_element_type=f32) + b_ref[0]

    return pl.pallas_call(
        body, name="ada_forward", out_shape=jax.ShapeDtypeStruct((2, 16, ADA_SHARD), f32), grid=(2,),
        in_specs=[pl.BlockSpec((16, D), lambda l: (0, 0)), pl.BlockSpec((1, D, ADA_SHARD), lambda l: (l, 0, 0)),
                  pl.BlockSpec((1, 1, ADA_SHARD), lambda l: (l, 0, 0))],
        out_specs=pl.BlockSpec((1, 16, ADA_SHARD), lambda l: (l, 0, 0)),
        compiler_params=_cparams("parallel"),
    )(c16, w_ada, b_loc)


def _ada_backward(c16, g16, w_ada):
    def body(c_ref, g_ref, w_ref, dw_ref, ds_ref):
        cv = c_ref[...]
        s = (cv * _sigmoid(cv)).astype(bf16)
        g = g_ref[0].astype(bf16)
        dw_ref[0] = lax.dot_general(s, g, (((0,), (0,)), ((), ())), preferred_element_type=f32)
        ds = lax.dot_general(g, w_ref[0].astype(bf16), (((1,), (1,)), ((), ())), preferred_element_type=f32)
        cc = cv[8:9]
        sg = _sigmoid(cc)
        dsilu = sg * (1.0 + cc * (1.0 - sg))
        ds_ref[0] = jnp.zeros((8, D), f32) + jnp.sum(ds[8:16], axis=0, keepdims=True) * dsilu

    return pl.pallas_call(
        body, name="ada_backward",
        out_shape=[jax.ShapeDtypeStruct((2, D, ADA_SHARD), f32), jax.ShapeDtypeStruct((2, 8, D), f32)], grid=(2,),
        in_specs=[pl.BlockSpec((16, D), lambda l: (0, 0)), pl.BlockSpec((1, 16, ADA_SHARD), lambda l: (l, 0, 0)),
                  pl.BlockSpec((1, D, ADA_SHARD), lambda l: (l, 0, 0))],
        out_specs=[pl.BlockSpec((1, D, ADA_SHARD), lambda l: (l, 0, 0)), pl.BlockSpec((1, 8, D), lambda l: (l, 0, 0))],
        compiler_params=_cparams("parallel"),
    )(c16, g16, w_ada)


def _adamw(pieces, w, m, v, name, after=()):
    rows, cols = w.shape
    n_arr, n_after = len(pieces), len(after)
    tm = 256 if (rows % 256 == 0 and rows > 256) else rows
    counts = [p[1] for p in pieces]
    first_tiles = [(p[2] if len(p) > 2 else 0) // tm for p in pieces]
    pieces = [p[0] for p in pieces]

    def body(*refs):
        p_refs = refs[:n_arr]
        w_ref, m_ref, v_ref = refs[n_arr:n_arr + 3]
        g_ref, d_ref, nm_ref, nv_ref = refs[n_arr + 3 + n_after:]
        g = None
        for p_ref in p_refs:
            for j in range(p_ref.shape[0]):
                term = p_ref[j].astype(f32)
                g = term if g is None else g + term
        m2 = ADAM_B1 * m_ref[...] + (1.0 - ADAM_B1) * g
        v2 = ADAM_B2 * v_ref[...] + (1.0 - ADAM_B2) * (g * g)
        m_hat = m2 / (1.0 - ADAM_B1 ** ADAM_STEP)
        v_hat = v2 / (1.0 - ADAM_B2 ** ADAM_STEP)
        g_ref[...] = g
        d_ref[...] = -ADAM_LR * (m_hat / (jnp.sqrt(v_hat) + ADAM_EPS) + ADAM_WD * w_ref[...])
        nm_ref[...] = m2
        nv_ref[...] = v2

    spec = pl.BlockSpec((tm, cols), lambda i: (i, 0))
    return pl.pallas_call(
        body, name=name, out_shape=[jax.ShapeDtypeStruct((rows, cols), f32)] * 4, grid=(rows // tm,),
        in_specs=[pl.BlockSpec((cnt, tm, cols), lambda i, t=t: (0, i + t, 0)) for cnt, t in zip(counts, first_tiles)]
        + [spec, spec, spec]
        + [ANY_SPEC] * n_after,
        out_specs=[spec] * 4, compiler_params=_cparams("parallel"),
    )(*pieces, w, m, v, *after)


MLP_TM = 256
FB = F // N_DEV


def _stack_rows(vals, n):
    cols = vals[0].shape[1]
    rid = lax.broadcasted_iota(jnp.int32, (n, cols), 0)
    out = jnp.zeros((n, cols), f32)
    for k, v in enumerate(vals):
        out = jnp.where(rid == k, v, out)
    return out


N_MLP_PARAMS = 9


class _ParamRows:
    def __init__(self, ref):
        self.ref = ref

    def __getitem__(self, sl):
        return self.ref[8 * sl.start:8 * sl.start + 1, :]


def _resident(shape, imap):
    return pl.BlockSpec(shape, imap, pipeline_mode=pl.Buffered(1))


def _mlp_forward(xa, xa_roff, out_prev, par, w_in, w_out, layer, name):
    def body(xa_ref, op_ref, par_ref, win_ref, wout_ref, x1_ref, h_ref, r_ref, mo_ref, x2_ref, hn_ref):
        p = _ParamRows(par_ref)
        x1 = xa_ref[...] + p[0:1] * (op_ref[...] + p[1:2])
        h = _normmod(x1, p[2:3], p[3:4], p[4:5]).astype(bf16)
        x1_ref[...] = x1
        h_ref[...] = h
        mo = jnp.zeros((MLP_TM, D), f32)
        for j in range(N_DEV):
            r = jnp.maximum(jnp.dot(h, win_ref[j], preferred_element_type=f32), 0.0)
            r_ref[:, j * FB:(j + 1) * FB] = r.astype(bf16)
            mo = mo + jnp.dot((r * r).astype(bf16), wout_ref[j], preferred_element_type=f32)
        mo_ref[...] = mo.astype(bf16)
        x2 = x1 + p[5:6] * mo
        x2_ref[...] = x2
        hn_ref[...] = _normmod(x2, p[6:7], p[7:8], p[8:9]).astype(bf16)

    row = lambda width: pl.BlockSpec((MLP_TM, width), lambda i: (i, 0))
    return pl.pallas_call(
        body, name=name, grid=(T_LAT // MLP_TM,),
        out_shape=[jax.ShapeDtypeStruct((T_LAT, D), f32), jax.ShapeDtypeStruct((T_LAT, D), bf16),
                   jax.ShapeDtypeStruct((T_LAT, F), bf16), jax.ShapeDtypeStruct((T_LAT, D), bf16),
                   jax.ShapeDtypeStruct((T_LAT, D), f32), jax.ShapeDtypeStruct((T_LAT, D), bf16)],
        in_specs=[pl.BlockSpec((MLP_TM, D), lambda i: (i + xa_roff, 0)), row(D), pl.BlockSpec((8 * N_MLP_PARAMS, D), lambda i: (0, 0)),
                  _resident((N_DEV, None, D, FB), lambda i: (0, layer, 0, 0)),
                  _resident((N_DEV, None, FB, D), lambda i: (0, layer, 0, 0))],
        out_specs=[row(D), row(D), row(F), row(D), row(D), row(D)],
        compiler_params=_cparams("parallel"),
    )(xa, out_prev, par, w_in, w_out)


def _mlp_backward(dx2, x1, r, mo, out_prev, par, w_in, w_out, layer, name, after=()):
    nt = (((1,), (1,)), ((), ()))

    n_after = len(after)

    def body(dx2_ref, x1_ref, r_ref, mo_ref, op_ref, par_ref, win_ref, wout_ref, *rest):
        dx1_ref, dop_ref, dmo_ref, dhid_ref, acc_ref = rest[n_after:]
        p = _ParamRows(par_ref)
        dx2v = dx2_ref[...]
        dmo = (p[5:6] * dx2v).astype(bf16)
        dmo_ref[...] = dmo
        dh = jnp.zeros((MLP_TM, D), f32)
        mo = mo_ref[...].astype(f32)
        for j in range(N_DEV):
            rf = r_ref[:, j * FB:(j + 1) * FB].astype(f32)
            dact = lax.dot_general(dmo, wout_ref[j], nt, preferred_element_type=f32)
            dhid = (dact * (2.0 * rf)).astype(bf16)
            dhid_ref[:, j * FB:(j + 1) * FB] = dhid
            dh = dh + lax.dot_general(dhid, win_ref[j], nt, preferred_element_type=f32)
        x1 = x1_ref[...]
        _, vjp = jax.vjp(_normmod, x1, p[2:3], p[3:4], p[4:5])
        dx, dng, dsc, dsh = vjp(dh)
        dx1 = dx2v + dx
        dx1_ref[...] = dx1
        dop_ref[...] = (p[0:1] * dx1).astype(bf16)
        sums = _stack_rows([jnp.sum(dx1 * (op_ref[...] + p[1:2]), axis=0, keepdims=True),
                            p[0:1] * jnp.sum(dx1, axis=0, keepdims=True), dng, dsc, dsh,
                            jnp.sum(dx2v * mo, axis=0, keepdims=True)], 8)

        @pl.when(pl.program_id(0) == 0)
        def _():
            acc_ref[...] = jnp.zeros_like(acc_ref)

        acc_ref[...] += sums

    row = lambda width: pl.BlockSpec((MLP_TM, width), lambda i: (i, 0))
    return pl.pallas_call(
        body, name=name, grid=(T_LAT // MLP_TM,),
        out_shape=[jax.ShapeDtypeStruct((T_LAT, D), f32), jax.ShapeDtypeStruct((T_LAT, D), bf16),
                   jax.ShapeDtypeStruct((T_LAT, D), bf16), jax.ShapeDtypeStruct((T_LAT, F), bf16),
                   jax.ShapeDtypeStruct((8, D), f32)],
        in_specs=[row(D), row(D), row(F), row(D), row(D), pl.BlockSpec((8 * N_MLP_PARAMS, D), lambda i: (0, 0)),
                  _resident((N_DEV, None, D, FB), lambda i: (0, layer, 0, 0)),
                  _resident((N_DEV, None, FB, D), lambda i: (0, layer, 0, 0))] + [ANY_SPEC] * n_after,
        out_specs=[row(D), row(D), row(D), row(F), pl.BlockSpec((8, D), lambda i: (0, 0))],
        compiler_params=_cparams("arbitrary"),
    )(dx2, x1, r, mo, out_prev, par, w_in, w_out, *after)


def _mlp_weight_grads(h, dhid, r, dmo, layer, other, tag):
    tn = (((0,), (0,)), ((), ()))

    def body_in(h_ref, dhid_ref, *rest):
        rest[-1][...] = lax.dot_general(h_ref[...], dhid_ref[...], tn, preferred_element_type=f32).astype(bf16)

    def body_out(r_ref, dmo_ref, *rest):
        rf = r_ref[...].astype(f32)
        rest[-1][...] = lax.dot_general((rf * rf).astype(bf16), dmo_ref[...], tn,
                                        preferred_element_type=f32).astype(bf16)

    def call(body, name, operands, specs, block, prev):
        extra = [] if prev is None else [prev]
        return pl.pallas_call(
            body, name=name, grid=(N_DEV,), out_shape=jax.ShapeDtypeStruct((N_DEV, 2) + block, bf16),
            in_specs=specs + [pl.BlockSpec(memory_space=pl.ANY)] * len(extra),
            out_specs=pl.BlockSpec((None, None) + block, lambda j: (j, layer, 0, 0)),
            input_output_aliases={} if prev is None else {2: 0},
            compiler_params=_cparams("parallel"),
        )(*operands, *extra)

    dw_in = call(body_in, tag + "_mlp_in_dw", [h, dhid],
                 [_resident((T_LAT, D), lambda j: (0, 0)), pl.BlockSpec((T_LAT, FB), lambda j: (0, j))], (D, FB),
                 None if other is None else other[0])
    dw_out = call(body_out, tag + "_mlp_out_dw", [r, dmo],
                  [pl.BlockSpec((T_LAT, FB), lambda j: (0, j)), _resident((T_LAT, D), lambda j: (0, 0))], (FB, D),
                  None if other is None else other[1])
    return dw_in, dw_out


def _pos_embed():
    n_rows = T_LAT // GRID_W
    q = D // 4
    omega = 1.0 / (POS_BASE ** (jnp.arange(q, dtype=f32) / q))
    er = jnp.arange(n_rows, dtype=jnp.int32).astype(f32)[:, None] * omega[None, :]
    ec = jnp.arange(GRID_W, dtype=jnp.int32).astype(f32)[:, None] * omega[None, :]
    by_row = jnp.concatenate([jnp.sin(er), jnp.cos(er)], axis=-1)[:, None, :]
    by_col = jnp.concatenate([jnp.sin(ec), jnp.cos(ec)], axis=-1)[None, :, :]
    full = jnp.concatenate([jnp.broadcast_to(by_row, (n_rows, GRID_W, D // 2)),
                            jnp.broadcast_to(by_col, (n_rows, GRID_W, D // 2))], axis=-1)
    return full.reshape(T_LAT, D)


HALF = R // 2
BLK_PER_HALF = N_BLK // 2
N_PARTS = 4


def _gate_matrix(w_a, w_x):
    eye = jnp.eye(BLK_PER_HALF, dtype=bf16)
    cols = []
    for h in range(2):
        for d in range(2):
            for w in (w_a, w_x):
                blocks = w[d, BLK_PER_HALF * h:BLK_PER_HALF * (h + 1)].astype(bf16)
                cols.append(jnp.einsum("hij,hg->higj", blocks, eye).reshape(HALF, HALF))
    return jnp.concatenate(cols, axis=1)


def _gate_blocks(dwg, part):
    out = []
    for h in range(2):
        blk = dwg[:, (N_PARTS * h + part) * HALF:(N_PARTS * h + part + 1) * HALF]
        blk = blk.reshape(BLK_PER_HALF, BLK, BLK_PER_HALF, BLK)
        out.append(jnp.moveaxis(jnp.diagonal(blk, axis1=0, axis2=2), -1, 0))
    return jnp.concatenate(out, axis=0)


GATE_BM = 768


def _gates_dx(dpre, wg, after=()):
    rows = dpre.shape[0]
    n_after = len(after)

    def body(d_ref, w_ref, *rest):
        rest[n_after][...] = lax.dot_general(d_ref[...], w_ref[...], (((1,), (1,)), ((), ())),
                                             preferred_element_type=f32)

    return pl.pallas_call(
        body, name="l0_gates_dx", grid=(rows // GATE_BM, 2), out_shape=jax.ShapeDtypeStruct((rows, R), f32),
        in_specs=[pl.BlockSpec((GATE_BM, N_PARTS * HALF), lambda i, h: (i, h)),
                  pl.BlockSpec((HALF, N_PARTS * HALF), lambda i, h: (0, h))] + [ANY_SPEC] * n_after,
        out_specs=pl.BlockSpec((GATE_BM, HALF), lambda i, h: (i, h)),
        compiler_params=_cparams("parallel", "parallel"),
    )(dpre, wg, *after)


COEFF_TM = 256


def _dir_params(d, *params):
    specs = [pl.BlockSpec((None, 1, HALF), lambda h, i: (d, 0, h))] * len(params)
    return specs, [p.reshape(2, 1, R) for p in params]


def _gates_coeff_fwd(ub, u, wg, ba, bx, lam, d):
    rows = u.shape[0]

    def body(ub_ref, u_ref, w_ref, ba_ref, bx_ref, lam_ref, a_ref, b_ref):
        pre = jnp.dot(ub_ref[...], w_ref[...], preferred_element_type=f32)
        a, b = _coeff(pre[:, :HALF], pre[:, HALF:], u_ref[...], ba_ref[...], bx_ref[...], lam_ref[...])
        a_ref[...] = a
        b_ref[...] = b

    tile = pl.BlockSpec((COEFF_TM, HALF), lambda h, i: (i, h))
    pspecs, pargs = _dir_params(d, ba, bx, lam)
    return pl.pallas_call(
        body, name=f"l0_gates_coeff_{d}", grid=(2, rows // COEFF_TM),
        out_shape=[jax.ShapeDtypeStruct((rows, R), f32)] * 2,
        in_specs=[tile, tile, pl.BlockSpec((HALF, 2 * HALF), lambda h, i: (0, 2 * h + d))] + pspecs,
        out_specs=[tile, tile], compiler_params=_cparams("parallel", "parallel"),
    )(ub, u, wg, *pargs)


def _gates_coeff_bwd(ub, u, dh, yp, wg, ba, bx, lam, d, dpre_prev):
    rows = u.shape[0]
    n_prev = 0 if dpre_prev is None else 1

    def body(ub_ref, u_ref, dh_ref, yp_ref, w_ref, ba_ref, bx_ref, lam_ref, *rest):
        dpre_ref, du_ref, dba_ref, dbx_ref, dlam_ref = rest[n_prev:]
        pre = jnp.dot(ub_ref[...], w_ref[...], preferred_element_type=f32)
        dhv = dh_ref[...]
        dpa, dpx, du, dba, dbx, dlam = _coeff_bwd(pre[:, :HALF], pre[:, HALF:], u_ref[...], ba_ref[...], bx_ref[...],
                                                  lam_ref[...], dhv * yp_ref[...], dhv)
        dpre_ref[:, :HALF] = dpa.astype(bf16)
        dpre_ref[:, HALF:] = dpx.astype(bf16)
        du_ref[...] = du

        @pl.when(pl.program_id(1) == 0)
        def _():
            dba_ref[...] = jnp.zeros_like(dba_ref)
            dbx_ref[...] = jnp.zeros_like(dbx_ref)
            dlam_ref[...] = jnp.zeros_like(dlam_ref)

        dba_ref[...] += dba
        dbx_ref[...] += dbx
        dlam_ref[...] += dlam

    tile = pl.BlockSpec((COEFF_TM, HALF), lambda h, i: (i, h))
    acc = pl.BlockSpec((1, HALF), lambda h, i: (0, h))
    pspecs, pargs = _dir_params(d, ba, bx, lam)
    extra = [] if dpre_prev is None else [dpre_prev]
    return pl.pallas_call(
        body, name=f"l0_gates_coeff_bwd_{d}", grid=(2, rows // COEFF_TM),
        out_shape=[jax.ShapeDtypeStruct((rows, 2 * N_PARTS * HALF), bf16), jax.ShapeDtypeStruct((rows, R), f32)]
        + [jax.ShapeDtypeStruct((1, R), f32)] * 3,
        in_specs=[tile] * 4 + [pl.BlockSpec((HALF, 2 * HALF), lambda h, i: (0, 2 * h + d))] + pspecs
        + [ANY_SPEC] * n_prev,
        out_specs=[pl.BlockSpec((COEFF_TM, 2 * HALF), lambda h, i: (i, 2 * h + d)), tile, acc, acc, acc],
        input_output_aliases={8: 0} if n_prev else {}, compiler_params=_cparams("parallel", "arbitrary"),
    )(ub, u, dh, yp, wg, *pargs, *extra)


def _gates_dw(u, dpre):
    rows = u.shape[0]

    def body(u_ref, d_ref, o_ref):
        o_ref[...] = lax.dot_general(u_ref[...], d_ref[...], (((0,), (0,)), ((), ())), preferred_element_type=f32)

    return pl.pallas_call(
        body, name="l0_gates_dw", grid=(2 * N_PARTS,), out_shape=jax.ShapeDtypeStruct((HALF, 2 * N_PARTS * HALF), f32),
        in_specs=[pl.BlockSpec((rows, HALF), lambda j: (0, j // N_PARTS)), pl.BlockSpec((rows, HALF), lambda j: (0, j))],
        out_specs=pl.BlockSpec((HALF, HALF), lambda j: (0, j)), compiler_params=_cparams("parallel"),
    )(u, dpre)


N_SCAN_CHUNKS = T_ALL // SCAN_CHUNK
SCAN_FWD = lambda t: t
SCAN_FWD_BWD = lambda t: N_SCAN_CHUNKS - 1 - t
SCAN_REV = lambda t: jnp.where(t == 0, 0, N_SCAN_CHUNKS - t)
SCAN_REV_BWD = lambda t: jnp.where(t == N_SCAN_CHUNKS - 1, 0, t + 1)
CONV_SEGMENTS = ((0, T_CTX), (T_CTX, T_LAT))
FUSED_TM = 256


def _token_rows(x, ctx):
    return (jnp.concatenate([ctx, x], axis=0),
            jnp.concatenate([jnp.zeros((T_CTX, D), f32), _pos_embed()], axis=0))


def _local_step(xcat, poscat, target, mods, cmod, wts, late_weights, send_grads, reduce_loss, start_after=()):
    sh1, sc1, g1, sh2, sc2, g2 = [[mods[l, i][None] for l in range(2)] for i in range(N_MOD)]
    ng = wts["norm_g"]
    scp = jnp.concatenate([cmod[1][None], sc1[0]], axis=0)
    shp = jnp.concatenate([cmod[0][None], sh1[0]], axis=0)

    ctx_tiles = T_CTX // FUSED_TM
    nt = (((1,), (1,)), ((), ()))

    def blend(i, p):
        sel = jnp.where(i < ctx_tiles, 1.0, 0.0)
        return sel * p[0:1] + (1.0 - sel) * p[1:2]

    def f_pre0(i, xc, pos, g, scp_, shp_, w):
        x0 = xc + pos
        h = _normmod(x0, g, blend(i, scp_), blend(i, shp_)).astype(bf16)
        return x0, h, jnp.dot(h, w, preferred_element_type=f32)

    x0cat, h0, gr = _rowcall(f_pre0, "l0_prenorm_in_proj", T_ALL, FUSED_TM, [_rin(xcat), _rin(poscat)],
                             [ng[0, 0][None], scp, shp, wts["rec_w_in"]], [(D, f32), (D, bf16), (2 * R, f32)],
                             after=start_after)
    u, ub = _dwconv_fwd(gr, R // 256, wts["rec_conv_w"], wts["rec_conv_b"], 4, 1, CONV_SEGMENTS, 256,
                        "l0_conv", True)
    gate_args = (wts["gates"], wts["rec_b_a"], wts["rec_b_x"], wts["rec_lambda"])
    a0, b0 = _gates_coeff_fwd(ub, u, *gate_args, 0)
    a1, b1 = _gates_coeff_fwd(ub, u, *gate_args, 1)
    halfway = late_weights("mlp_halfway", a1)
    y0, yp0 = _scan_call(a0, b0, SCAN_FWD, False, "l0_scan_fwd", False, after=[halfway])
    y1, yp1 = _scan_call(a1, b1, SCAN_REV, True, "l0_scan_rev", False)

    wts = dict(wts, **late_weights("mlp", y1))

    def f_gate_out(i, gp, y0_, y1_, w):
        z = (_gelu(gp) * (y0_ + y1_)).astype(bf16)
        return z, jnp.dot(z, w, preferred_element_type=f32)

    zb, out0 = _rowcall(f_gate_out, "l0_gate_out_proj", T_LAT, FUSED_TM,
                        [_rin(gr, R, 0, ctx_tiles), _rin(y0, None, 0, ctx_tiles), _rin(y1, None, 0, ctx_tiles)],
                        [wts["rec_w_out"]], [(R, bf16), (D, f32)])

    zero_d = jnp.zeros((1, D), f32)

    def mlp_params(rows):
        rows = rows + [zero_d] * (N_MLP_PARAMS - len(rows))
        return jnp.concatenate([jnp.broadcast_to(r, (8, D)) for r in rows], axis=0)

    par0 = mlp_params([g1[0], zero_d, ng[0, 1][None], sc2[0], sh2[0], g2[0], ng[1, 0][None], sc1[1], sh1[1]])
    x1, h1, r0, mo0, x2, h2 = _mlp_forward(x0cat, T_CTX // MLP_TM, out0, par0, wts["mlp_w_in"], wts["mlp_w_out"], 0,
                                           "l0_mlp")

    wts = dict(wts, **late_weights("conf", x2))
    def glu(pa, pb, b1):
        return (pa + b1[:, :D]) * _sigmoid(pb + b1[:, D:])

    def f_pw1_glu(i, h_, b1, w):
        p = jnp.dot(h_, w, preferred_element_type=f32)
        return glu(p[:, :D], p[:, D:], b1), p

    zg, pw = _rowcall(f_pw1_glu, "l1_pw1_glu", T_LAT, FUSED_TM, [_rin(h2)], [wts["conf_b_pw1"], wts["conf_w_pw1"]],
                      [(D, f32), (2 * D, bf16)])
    (zc,) = _dwconv_fwd(zg, 0, wts["conf_conv_w"], wts["conf_conv_b"], 31, 15, ((0, T_LAT),), 128, "l1_conv", False)

    def ln_silu(z, lg, lb):
        mu = jnp.mean(z, axis=-1, keepdims=True)
        zc_ = z - mu
        var = jnp.mean(zc_ * zc_, axis=-1, keepdims=True)
        yv = zc_ * lax.rsqrt(var + EPS) * lg + lb
        return yv * _sigmoid(yv)

    def f_lnsilu_pw2(i, z, lg, lb, w):
        s = ln_silu(z, lg, lb).astype(bf16)
        return s, jnp.dot(s, w, preferred_element_type=f32)

    sb, out1 = _rowcall(f_lnsilu_pw2, "l1_ln_silu_pw2", T_LAT, FUSED_TM, [_rin(zc)],
                        [wts["conf_ln_g"], wts["conf_ln_b"], wts["conf_w_pw2"]], [(D, bf16), (D, f32)])
    par1 = mlp_params([g1[1], wts["conf_b_pw2"], ng[1, 1][None], sc2[1], sh2[1], g2[1]])
    x3, h3, r1, mo1, x4, _ = _mlp_forward(x2, 0, out1, par1, wts["mlp_w_in"], wts["mlp_w_out"], 1, "l1_mlp")

    def loss_fn(x4_, fg, tgt):
        err = _rms(x4_, fg) - tgt
        per_row = jnp.mean(err * err, axis=-1, keepdims=True)
        return 0.5 * jnp.sum(per_row, axis=0, keepdims=True)

    def f_head(i, x4_, tgt, fg):
        loss, vjp = jax.vjp(lambda a, e: loss_fn(a, e, tgt), x4_, fg)
        dx, dfg = vjp(jnp.ones((1, 1), f32))
        return dx, jnp.broadcast_to(loss, (1, 128)), dfg

    dx4, loss_acc, dfinal_g = _rowcall(f_head, "head", T_LAT, FUSED_TM, [_rin(x4), _rin(target)], [wts["final_g"]],
                                       [(D, f32)], [(1, 128), (1, D)])

    grads = {"final_g": dfinal_g}
    loss = reduce_loss(loss_acc[0, 0])

    dx3, dout1, dmo1, dhid1, acc1 = _mlp_backward(dx4, x3, r1, mo1, out1, par1, wts["mlp_w_in"], wts["mlp_w_out"], 1,
                                                  "l1_mlp_bwd", after=[loss.reshape(1, 1)])
    mlp_dw = _mlp_weight_grads(h3, dhid1, r1, dmo1, 1, None, "l1")
    dg1_1, db_pw2, dng11, dsc2_1, dsh2_1, dg2_1 = [acc1[k:k + 1] for k in range(6)]

    grads["conf_w_pw2"] = _mm(sb, dout1, "l1_pw2_dw", ta=True, out_dtype=bf16)
    grads["conf_b_pw2"] = db_pw2

    def f_pw2_lnsilu_bwd(i, z, dout, lg, lb, w):
        ds = lax.dot_general(dout, w, nt, preferred_element_type=f32)
        _, vjp = jax.vjp(ln_silu, z, lg, lb)
        return vjp(ds)

    dzc, dln_g, dln_b = _rowcall(f_pw2_lnsilu_bwd, "l1_pw2_ln_silu_bwd", T_LAT, FUSED_TM, [_rin(zc), _rin(dout1)],
                                 [wts["conf_ln_g"], wts["conf_ln_b"], wts["conf_w_pw2"]], [(D, f32)], [(1, D)] * 2)
    grads["conf_ln_g"], grads["conf_ln_b"] = dln_g, dln_b
    dzg, dconv_w, dconv_b = _dwconv_bwd([dzc], zg, 0, wts["conf_conv_w"], 31, 15, ((0, T_LAT),), 128,
                                        "l1_conv_bwd", f32)
    grads["conf_conv_w"], grads["conf_conv_b"] = dconv_w, dconv_b

    def f_glu_pw1_norm_bwd(i, p_, dz, x_, dxs, b1, g_, sc_, sh_, w):
        pf = p_.astype(f32)
        _, vjp = jax.vjp(glu, pf[:, :D], pf[:, D:], b1)
        da, db, db1 = vjp(dz)
        dp = jnp.concatenate([da, db], axis=1).astype(bf16)
        dh = lax.dot_general(dp, w, nt, preferred_element_type=f32)
        _, vjp = jax.vjp(_normmod, x_, g_, sc_, sh_)
        dx, dg, dsc, dsh = vjp(dh)
        return dp, dx + dxs, db1, dg, dsc, dsh

    dpw, dx2, db_pw1, dng10, dsc1_1, dsh1_1 = _rowcall(
        f_glu_pw1_norm_bwd, "l1_glu_pw1_normmod_bwd", T_LAT, FUSED_TM, [_rin(pw), _rin(dzg), _rin(x2), _rin(dx3)],
        [wts["conf_b_pw1"], ng[1, 0][None], sc1[1], sh1[1], wts["conf_w_pw1"]], [(2 * D, bf16), (D, f32)],
        [(1, 2 * D), (1, D), (1, D), (1, D)])
    grads["conf_b_pw1"] = db_pw1
    grads["conf_w_pw1"] = _mm(h2, dpw, "l1_pw1_dw", ta=True, out_dtype=bf16)
    sent = send_grads(["conf_w_pw2", "conf_w_pw1"], grads)

    dx1, dout0, dmo0, dhid0, acc0 = _mlp_backward(dx2, x1, r0, mo0, out0, par0, wts["mlp_w_in"], wts["mlp_w_out"], 0,
                                                  "l0_mlp_bwd", after=[sent])
    grads["mlp_w_in"], grads["mlp_w_out"] = _mlp_weight_grads(h1, dhid0, r0, dmo0, 0, mlp_dw, "l0")
    sent = send_grads(["mlp_w_in", "mlp_w_out"], grads)
    dg1_0, _, dng01, dsc2_0, dsh2_0, dg2_0 = [acc0[k:k + 1] for k in range(6)]

    grads["rec_w_out"] = _mm(zb, dout0, "l0_out_proj_dw", ta=True, out_dtype=bf16, after=[sent])
    sent = send_grads(["rec_w_out"], grads)

    def f_out_gate_bwd(i, gp, y0_, y1_, dout, w):
        lat = jnp.where(i < ctx_tiles, 0.0, 1.0)
        dz = lax.dot_general(dout, w, nt, preferred_element_type=f32)
        _, vjp = jax.vjp(lambda a, b: _gelu(a) * b, gp, y0_ + y1_)
        dgp, dy = vjp(dz)
        return dgp * lat, dy * lat

    dgp, dy = _rowcall(f_out_gate_bwd, "l0_out_proj_gate_bwd", T_ALL, FUSED_TM,
                       [_rin(gr, R, 0), _rin(y0), _rin(y1), _rin(dout0, None, 0, -ctx_tiles)], [wts["rec_w_out"]],
                       [(R, bf16), (R, f32)], after=[sent])
    (dh_f,) = _scan_call(a0, dy, SCAN_FWD_BWD, True, "l0_scan_fwd_bwd", True)
    (dh_r,) = _scan_call(a1, dy, SCAN_REV_BWD, False, "l0_scan_rev_bwd", True)

    dpre, du_f, *dpar_f = _gates_coeff_bwd(ub, u, dh_f, yp0, *gate_args, 0, None)
    dpre, du_r, *dpar_r = _gates_coeff_bwd(ub, u, dh_r, yp1, *gate_args, 1, dpre)
    grads["rec_b_a"], grads["rec_b_x"], grads["rec_lambda"] = [
        jnp.concatenate([f.reshape(-1), r_.reshape(-1)]).reshape(2, R) for f, r_ in zip(dpar_f, dpar_r)]
    grads["gates"] = _gates_dw(ub, dpre)
    sent = send_grads(["replicated"], grads)
    du_gates = _gates_dx(dpre, wts["gates"], after=[sent])
    drec, dconv4_w, dconv4_b = _dwconv_bwd([du_f, du_r, du_gates], gr, R // 256, wts["rec_conv_w"], 4, 1,
                                           CONV_SEGMENTS, 256, "l0_conv_bwd", bf16)
    grads["rec_conv_w"], grads["rec_conv_b"] = dconv4_w, dconv4_b
    dgr = jnp.concatenate([dgp, drec], axis=1)
    sent = send_grads(["replicated_halfway"], grads)
    grads["rec_w_in"] = _mm(h0, dgr, "l0_in_proj_dw", ta=True, out_dtype=bf16, after=[sent])
    sent = send_grads(["rec_w_in"], grads)

    def f_pre0_bwd(i, x0, dgr_, dxs, g, scp_, shp_, w):
        lat = jnp.where(i < ctx_tiles, 0.0, 1.0)
        dh = lax.dot_general(dgr_, w, nt, preferred_element_type=f32)
        _, vjp = jax.vjp(lambda a, b, c, e: _normmod(a, b, blend(i, c), blend(i, e)), x0, g, scp_, shp_)
        dx, dg, dscp, dshp = vjp(dh)
        return dx + lat * dxs, dg, dscp, dshp

    dx0cat, dng00, dscp, dshp = _rowcall(
        f_pre0_bwd, "l0_in_proj_prenorm_bwd", T_ALL, FUSED_TM,
        [_rin(x0cat), _rin(dgr), _rin(dx1, None, 0, -ctx_tiles)], [ng[0, 0][None], scp, shp, wts["rec_w_in"]],
        [(D, f32)], [(1, D), (2, D), (2, D)], after=[sent])

    grads["norm_g"] = jnp.stack([jnp.concatenate([dng00, dng01], 0), jnp.concatenate([dng10, dng11], 0)])
    dmods = jnp.stack([
        jnp.concatenate([dshp[1:2], dscp[1:2], dg1_0, dsh2_0, dsc2_0, dg2_0], axis=0),
        jnp.concatenate([dsh1_1, dsc1_1, dg1_1, dsh2_1, dsc2_1, dg2_1], axis=0)])
    dcmod = jnp.concatenate([dshp[0:1], dscp[0:1]], axis=0)
    return loss, dx0cat[T_CTX:], dmods, dcmod, grads


def _unshard_cols(g):
    g = jnp.moveaxis(g, 0, -2)
    return g.reshape(g.shape[:-2] + (g.shape[-2] * g.shape[-1],))


def _shard_cols(w):
    w = w.reshape(w.shape[:-1] + (N_DEV, w.shape[-1] // N_DEV))
    return jnp.moveaxis(w, -2, 0)


def _shard_rows(w):
    return w.reshape((N_DEV, w.shape[0] // N_DEV) + w.shape[1:])


SMALL_PACK_ROWS = 64
REPL_FINAL_G_ROWS = -(-D // BLK)
REPL_ROWS = -(-(2 * 2 * N_BLK * BLK + 2 * 2 * N_BLK + REPL_FINAL_G_ROWS) // 16) * 16


def kernel(x, c, ctx, c_ctx, w_ada, b_ada, norm_g, rec_w_in, rec_conv_w, rec_conv_b, rec_lambda, rec_w_a, rec_b_a, rec_w_x, rec_b_x, rec_w_out, conf_w_pw1, conf_b_pw1, conf_conv_w, conf_conv_b, conf_ln_g, conf_ln_b, conf_w_pw2, conf_b_pw2, mlp_w_in, mlp_w_out, final_g, loss_target, m_c_ctx, m_w_ada, m_b_ada, m_norm_g, m_rec_w_in, m_rec_conv_w, m_rec_conv_b, m_rec_lambda, m_rec_w_a, m_rec_b_a, m_rec_w_x, m_rec_b_x, m_rec_w_out, m_conf_w_pw1, m_conf_b_pw1, m_conf_conv_w, m_conf_conv_b, m_conf_ln_g, m_conf_ln_b, m_conf_w_pw2, m_conf_b_pw2, m_mlp_w_in, m_mlp_w_out, m_final_g, v_c_ctx, v_w_ada, v_b_ada, v_norm_g, v_rec_w_in, v_rec_conv_w, v_rec_conv_b, v_rec_lambda, v_rec_w_a, v_rec_b_a, v_rec_w_x, v_rec_b_x, v_rec_w_out, v_conf_w_pw1, v_conf_b_pw1, v_conf_conv_w, v_conf_conv_b, v_conf_ln_g, v_conf_ln_b, v_conf_w_pw2, v_conf_b_pw2, v_mlp_w_in, v_mlp_w_out, v_final_g):
    me = 4 * lax.axis_index("x") + 2 * lax.axis_index("y") + lax.axis_index("c")
    weights = dict(c_ctx=c_ctx, w_ada=w_ada, b_ada=b_ada, norm_g=norm_g, rec_w_in=rec_w_in, rec_conv_w=rec_conv_w,
                   rec_conv_b=rec_conv_b, rec_lambda=rec_lambda, rec_w_a=rec_w_a, rec_b_a=rec_b_a, rec_w_x=rec_w_x,
                   rec_b_x=rec_b_x, rec_w_out=rec_w_out, conf_w_pw1=conf_w_pw1, conf_b_pw1=conf_b_pw1,
                   conf_conv_w=conf_conv_w, conf_conv_b=conf_conv_b, conf_ln_g=conf_ln_g, conf_ln_b=conf_ln_b,
                   conf_w_pw2=conf_w_pw2, conf_b_pw2=conf_b_pw2, mlp_w_in=mlp_w_in, mlp_w_out=mlp_w_out, final_g=final_g)
    m_in = dict(c_ctx=m_c_ctx, w_ada=m_w_ada, b_ada=m_b_ada, norm_g=m_norm_g, rec_w_in=m_rec_w_in, rec_conv_w=m_rec_conv_w,
                rec_conv_b=m_rec_conv_b, rec_lambda=m_rec_lambda, rec_w_a=m_rec_w_a, rec_b_a=m_rec_b_a, rec_w_x=m_rec_w_x,
                rec_b_x=m_rec_b_x, rec_w_out=m_rec_w_out, conf_w_pw1=m_conf_w_pw1, conf_b_pw1=m_conf_b_pw1,
                conf_conv_w=m_conf_conv_w, conf_conv_b=m_conf_conv_b, conf_ln_g=m_conf_ln_g, conf_ln_b=m_conf_ln_b,
                conf_w_pw2=m_conf_w_pw2, conf_b_pw2=m_conf_b_pw2, mlp_w_in=m_mlp_w_in, mlp_w_out=m_mlp_w_out,
                final_g=m_final_g)
    v_in = dict(c_ctx=v_c_ctx, w_ada=v_w_ada, b_ada=v_b_ada, norm_g=v_norm_g, rec_w_in=v_rec_w_in, rec_conv_w=v_rec_conv_w,
                rec_conv_b=v_rec_conv_b, rec_lambda=v_rec_lambda, rec_w_a=v_rec_w_a, rec_b_a=v_rec_b_a, rec_w_x=v_rec_w_x,
                rec_b_x=v_rec_b_x, rec_w_out=v_rec_w_out, conf_w_pw1=v_conf_w_pw1, conf_b_pw1=v_conf_b_pw1,
                conf_conv_w=v_conf_conv_w, conf_conv_b=v_conf_conv_b, conf_ln_g=v_conf_ln_g, conf_ln_b=v_conf_ln_b,
                conf_w_pw2=v_conf_w_pw2, conf_b_pw2=v_conf_b_pw2, mlp_w_in=v_mlp_w_in, mlp_w_out=v_mlp_w_out,
                final_g=v_final_g)
    names = list(weights)

    small_items = [c, norm_g, rec_conv_w, rec_lambda, conf_b_pw1, conf_conv_w, conf_conv_b, conf_ln_g, conf_ln_b,
                   conf_b_pw2]
    flat = jnp.concatenate([a.reshape(-1) for a in small_items])
    flat = jnp.pad(flat, (0, SMALL_PACK_ROWS * 128 - flat.shape[0])).reshape(SMALL_PACK_ROWS, 128)
    as_shard = lambda a: a.astype(bf16).reshape(-1, a.shape[-1])
    early_srcs = [flat, as_shard(rec_w_in[0])]
    early_handle, started = _exchange_start(early_srcs, [_own_block_filled(s, me) for s in early_srcs],
                                            "gather_early_start", False)
    zero = started[0, 0]
    gates = _gate_matrix(rec_w_a[0] + zero, rec_w_x[0] + zero)
    late_items = {"mlp": [rec_w_out[0], mlp_w_in, mlp_w_out], "conf": [conf_w_pw1[0], conf_w_pw2[0]]}
    late_shards = {g: [as_shard(a + zero) for a in items] for g, items in late_items.items()}
    late_lands = {g: [_own_block_filled(s, me) for s in shards] for g, shards in late_shards.items()}
    xcat, poscat = _token_rows(x[0] + zero, ctx[0])
    small_all, early = _exchange_wait(early_handle, [gates, xcat, poscat] + late_lands["mlp"] + late_lands["conf"],
                                      "gather_early_wait", False)

    small_all = small_all.reshape(N_DEV, -1)
    off = 0
    small = []
    for a in small_items:
        small.append(small_all[:, off:off + a.size].reshape((N_DEV,) + a.shape))
        off += a.size
    c_all, ng_all, rcw_all, lam_all, bpw1_all, ccw_all, ccb_all, lng_all, lnb_all, bpw2_all = small
    wts = {
        "norm_g": _unshard_cols(ng_all),
        "rec_conv_w": _unshard_cols(rcw_all)[0],
        "rec_lambda": _unshard_cols(lam_all)[0],
        "conf_b_pw1": _unshard_cols(bpw1_all),
        "conf_conv_w": _unshard_cols(ccw_all)[0],
        "conf_conv_b": _unshard_cols(ccb_all),
        "conf_ln_g": _unshard_cols(lng_all),
        "conf_ln_b": _unshard_cols(lnb_all),
        "conf_b_pw2": _unshard_cols(bpw2_all),
        "rec_conv_b": rec_conv_b,
        "rec_b_a": rec_b_a[0].reshape(2, R),
        "rec_b_x": rec_b_x[0].reshape(2, R),
        "final_g": final_g[None],
        "gates": gates,
    }

    c16 = jnp.concatenate([c_all[:, 0], jnp.broadcast_to(c_ctx[None], (8, D))], axis=0)
    b_loc = lax.dynamic_slice_in_dim(b_ada, me * ADA_SHARD, ADA_SHARD, axis=1)[:, None]
    (mods_gathered,) = _all_gather([_ada_forward(c16, w_ada, b_loc)], "gather_mods")
    mods_all = _unshard_cols(mods_gathered)
    mods = lax.dynamic_index_in_dim(mods_all, me, axis=1, keepdims=False).reshape(2, N_MOD, D)
    cmod = mods_all[0, 8, :2 * D].reshape(2, D)

    late_handles = {}
    late_handles["mlp"], token = _gather2_start(late_shards["mlp"], late_lands["mlp"], "gather_mlp_start",
                                                [early, mods_gathered])
    order = [token]
    wts["rec_w_in"] = _unshard_cols(early + token[0, 0].astype(bf16))

    def late_weights(group, after):
        if group == "mlp_halfway":
            late_handles["mlp"] = _gather2_forward1(late_handles["mlp"], after, "gather_mlp_forward1")
            return late_handles["mlp"][2][0]
        if group == "mlp":
            passed = _gather2_forward2(late_handles["mlp"], after, "gather_mlp_forward2")
            late_handles["conf"], started = _exchange_start(late_shards["conf"], late_lands["conf"], "gather_conf_start",
                                                            False, after=[passed[2][0]])
            got = _gather2_wait(passed, started, "gather_mlp_wait")
        else:
            got = _exchange_wait(late_handles[group], after, "gather_conf_wait", False)
        got = [g.reshape((N_DEV,) + a.shape) for g, a in zip(got, late_items[group])]
        if group == "mlp":
            return {"rec_w_out": got[0].reshape(R, D), "mlp_w_in": got[1], "mlp_w_out": got[2]}
        return {"conf_w_pw1": _unshard_cols(got[0]), "conf_w_pw2": got[1].reshape(D, D)}

    to_blocks = {"rec_w_in": _shard_cols, "conf_w_pw1": _shard_cols, "rec_w_out": _shard_rows, "conf_w_pw2": _shard_rows,
                 "mlp_w_in": lambda g: g, "mlp_w_out": lambda g: g}
    grad_handles = []

    repl_names = ["rec_w_a", "rec_w_x", "rec_b_a", "rec_b_x", "final_g"]

    def send_replicated(grads):
        dwg = grads["gates"]
        repl = {"rec_w_a": jnp.stack([_gate_blocks(dwg, 0), _gate_blocks(dwg, 2)]),
                "rec_w_x": jnp.stack([_gate_blocks(dwg, 1), _gate_blocks(dwg, 3)]),
                "rec_b_a": grads["rec_b_a"], "rec_b_x": grads["rec_b_x"],
                "final_g": jnp.pad(grads["final_g"], ((0, 0), (0, REPL_FINAL_G_ROWS * BLK - D)))}
        flat = jnp.concatenate([repl[n].reshape(-1, BLK) for n in repl_names], axis=0)
        flat = jnp.pad(flat, ((0, REPL_ROWS - flat.shape[0]), (0, 0))).astype(bf16)
        flat = flat.reshape(REPL_ROWS // 8, 8 * BLK)
        handle, sent = _gather2_start([flat], [_own_block_filled(flat, me)], "gather_replicated_start", [])
        grad_handles.append((["replicated"], handle))
        return sent

    def send_grads(group, grads):
        if group == ["replicated"]:
            return send_replicated(grads)
        if group == ["replicated_halfway"]:
            slot = [g for g, _ in grad_handles].index(["replicated"])
            passed = _gather2_forward1(grad_handles[slot][1], grads["rec_conv_b"], "gather_replicated_forward1")
            grad_handles[slot] = (["replicated"], passed)
            return passed[2][0]
        blocks = [to_blocks[n](grads[n]) for n in group]
        blocks = [g.reshape(N_DEV, -1, g.shape[-1]) for g in blocks]
        lands = [_own_block_filled(lax.dynamic_index_in_dim(g, me, 0, keepdims=False), me) for g in blocks]
        handle, sent = _exchange_start(blocks, lands, "scatter_start_" + group[0], True)
        grad_handles.append((group, handle))
        return sent

    loss, grad_x, dmods, dcmod, grads = _local_step(
        xcat, poscat, loss_target[0], mods, cmod, wts, late_weights, send_grads,
        lambda partial: lax.psum(partial, ("x", "y", "c")), start_after=order)

    def as2d(shape):
        rows = 1
        for s in shape[:-1]:
            rows *= s
        return (rows, shape[-1])

    def whole(arr, shape):
        arr = arr.reshape((-1,) + as2d(shape))
        return (arr, arr.shape[0])

    shard_shapes = {n: weights[n].shape for n in names}
    g_out, d_out, m_out, v_out = {}, {}, {}, {}

    def adamw(n, pieces, after):
        shape = shard_shapes[n]
        r2, c2 = as2d(shape)
        g, dl, nm, nv = _adamw(pieces, weights[n].reshape(r2, c2), m_in[n].reshape(r2, c2), v_in[n].reshape(r2, c2),
                               "adamw_" + n, after=after)
        g_out[n], d_out[n], m_out[n], v_out[n] = (t.reshape(shape) for t in (g, dl, nm, nv))
        return g

    small_sharded = ["norm_g", "rec_conv_w", "rec_lambda", "conf_b_pw1", "conf_conv_w", "conf_conv_b", "conf_ln_g",
                     "conf_ln_b", "conf_b_pw2"]
    pack = jnp.concatenate([_shard_cols(grads[n]).reshape(N_DEV, -1) for n in small_sharded], axis=1)
    pack = jnp.pad(pack, ((0, 0), (0, SMALL_PACK_ROWS * 128 - pack.shape[1]))).reshape(N_DEV, SMALL_PACK_ROWS, 128)
    small_handle, token = _exchange_start(
        [pack], [_own_block_filled(lax.dynamic_index_in_dim(pack, me, 0, keepdims=False), me)], "scatter_small_start",
        True, after=[grad_x])
    dm_flat = jnp.concatenate([dmods.reshape(-1), dcmod.reshape(-1), grads["rec_conv_b"].reshape(-1)])
    dm_len = dm_flat.shape[0]
    dm_flat = jnp.pad(dm_flat, (0, 128 * 128 - dm_len)).reshape(128, 128)
    dm_handle, token = _exchange_start([dm_flat], [_own_block_filled(dm_flat, me)], "gather_dmods_start", False,
                                       after=[token])

    done = token
    for group, handle in grad_handles:
        if group == ["replicated"]:
            passed = _gather2_forward2(handle, done, "gather_replicated_forward2")
            repl_all = _gather2_wait(passed, done, "gather_replicated_wait")[0]
            repl_all = repl_all.reshape(N_DEV, REPL_ROWS, BLK)
            row = 0
            for n in repl_names:
                n_rows = -(-weights[n].size // BLK)
                if as2d(shard_shapes[n]) == (n_rows, BLK) and row % 256 == 0:
                    done = adamw(n, [(repl_all, N_DEV, row)], [done])
                else:
                    got = repl_all[:, row:row + n_rows].reshape(N_DEV, -1)[:, :weights[n].size]
                    done = adamw(n, [whole(got, shard_shapes[n])], [done])
                row += n_rows
            continue
        for n, got in zip(group, _exchange_wait(handle, done, "scatter_wait_" + group[0], True)):
            done = adamw(n, [(got, N_DEV)], [done])

    dm_all = _exchange_wait(dm_handle, done, "gather_dmods_wait", False)[0].reshape(N_DEV, -1)
    dmods_all = dm_all[:, :2 * N_MOD * D].reshape(N_DEV, 2, N_MOD * D)
    dcmod_all = jnp.pad(dm_all[:, 2 * N_MOD * D:2 * N_MOD * D + 2 * D], ((0, 0), (0, (N_MOD - 2) * D)))
    g16_full = jnp.stack([jnp.concatenate([dmods_all[:, 0], dcmod_all], axis=0),
                          jnp.concatenate([dmods_all[:, 1], jnp.zeros_like(dcmod_all)], axis=0)])
    g16 = lax.dynamic_slice_in_dim(g16_full, me * ADA_SHARD, ADA_SHARD, axis=2)
    dw_ada, ds_part = _ada_backward(c16, g16, w_ada)
    ds_handle, token = _exchange_start([ds_part[0]], [_own_block_filled(ds_part[0], me)], "gather_dsilu_start", False)
    done = adamw("w_ada", [whole(dw_ada, shard_shapes["w_ada"])], [token])
    done = adamw("rec_conv_b", [whole(dm_all[:, dm_len - R:dm_len], shard_shapes["rec_conv_b"])], [done])
    db_terms = jnp.concatenate([dmods_all, jnp.stack([dcmod_all, jnp.zeros_like(dcmod_all)], axis=1)], axis=0)
    done = adamw("b_ada", [whole(db_terms, shard_shapes["b_ada"])], [done])
    pack_recv = _exchange_wait(small_handle, done, "scatter_small_wait", True)[0].reshape(N_DEV, -1)
    off = 0
    for n in small_sharded:
        size = weights[n].size
        done = adamw(n, [whole(pack_recv[:, off:off + size], shard_shapes[n])], [done])
        off += size
    ds_all = _exchange_wait(ds_handle, done, "gather_dsilu_wait", False)[0]
    adamw("c_ctx", [whole(ds_all[:, 0], shard_shapes["c_ctx"])], [])

    return (loss, grad_x[None], *[g_out[n] for n in names], *[d_out[n] for n in names],
            *[m_out[n] for n in names], *[v_out[n] for n in names])
```

```python
import functools

import jax
import jax.numpy as jnp
from jax import lax
from jax.experimental import pallas as pl
from jax.experimental.pallas import tpu as pltpu

f32 = jnp.float32
bf16 = jnp.bfloat16

N_DEV = 8
D = 1024
T_LAT = 2048
T_CTX = 256
T_ALL = T_CTX + T_LAT
R = 1280
N_BLK = 16
BLK = R // N_BLK
F = 4096
GRID_W = 64
RG_C = 8.0
EPS = 1e-6
POS_BASE = 10000.0
N_MOD = 6
ADA_SHARD = N_MOD * D // N_DEV

ADAM_LR = 0.001
ADAM_B1 = 0.9
ADAM_B2 = 0.999
ADAM_EPS = 1e-08
ADAM_WD = 0.01
ADAM_STEP = 10

VMEM_LIMIT_V7X = 56 * 1024 * 1024
HALO = 16
MESH = pl.DeviceIdType.MESH


def _cparams(*sem):
    return pltpu.CompilerParams(dimension_semantics=sem, vmem_limit_bytes=VMEM_LIMIT_V7X)


def _pick(n, cands):
    for c in cands:
        if n % c == 0:
            return c
    raise ValueError(f"no block size for {n}")


def _position():
    x, y, c = lax.axis_index("x"), lax.axis_index("y"), lax.axis_index("c")
    return x, y, c, 4 * x + 2 * y + c


def _peer(x, y, c, k):
    px = (1 - x) if (k >> 2) & 1 else x
    py = (1 - y) if (k >> 1) & 1 else y
    pc = (1 - c) if k & 1 else c
    return (px, py, pc), 4 * px + 2 * py + pc


def _exchange(arrs, name, scatter):
    n = len(arrs)

    def body(*refs):
        ins, outs = refs[:n], refs[n:2 * n]
        send_sems, recv_sems, local_sems = refs[2 * n:]
        x, y, c, me = _position()
        local = []
        for a in range(n):
            src = ins[a].at[me] if scatter else ins[a]
            cp = pltpu.make_async_copy(src, outs[a].at[me], local_sems.at[a])
            cp.start()
            local.append(cp)
        sends, recvs = [], []
        for a in range(n):
            for k in range(1, N_DEV):
                peer, peer_lin = _peer(x, y, c, k)
                src = ins[a].at[peer_lin] if scatter else ins[a]
                cp = pltpu.make_async_remote_copy(
                    src_ref=src, dst_ref=outs[a].at[me], send_sem=send_sems.at[a, k - 1],
                    recv_sem=recv_sems.at[a, k - 1], device_id=peer, device_id_type=MESH)
                cp.start()
                sends.append(cp)
                recvs.append(pltpu.make_async_remote_copy(
                    src_ref=src, dst_ref=outs[a].at[peer_lin], send_sem=send_sems.at[a, k - 1],
                    recv_sem=recv_sems.at[a, k - 1], device_id=peer, device_id_type=MESH))
        for cp in recvs:
            cp.wait_recv()
        for cp in sends:
            cp.wait_send()
        for cp in local:
            cp.wait()

    if scatter:
        out_shape = [jax.ShapeDtypeStruct(a.shape, a.dtype) for a in arrs]
    else:
        out_shape = [jax.ShapeDtypeStruct((N_DEV,) + a.shape, a.dtype) for a in arrs]
    any_spec = pl.BlockSpec(memory_space=pl.ANY)
    return pl.pallas_call(
        body, name=name, out_shape=out_shape,
        in_specs=[any_spec] * n, out_specs=[any_spec] * n,
        scratch_shapes=[pltpu.SemaphoreType.DMA((n, N_DEV - 1)), pltpu.SemaphoreType.DMA((n, N_DEV - 1)),
                        pltpu.SemaphoreType.DMA((n,))],
    )(*arrs)


def _all_gather(arrs, name):
    return _exchange(arrs, name, scatter=False)


def _lin(p):
    return 4 * p[0] + 2 * p[1] + p[2]


HBM_SPEC = pl.BlockSpec(memory_space=pltpu.HBM)
SEM_SPEC = pl.BlockSpec(memory_space=pltpu.SEMAPHORE)
DATAFLOW_EFFECT = pltpu.SideEffectType.DATAFLOW_SIDE_EFFECTING


def _split_copies(srcs, lands, send_sems, recv_sems, scatter):
    x, y, c, me = _position()
    out = []
    for a in range(len(srcs)):
        for k in range(1, N_DEV):
            peer, peer_lin = _peer(x, y, c, k)
            src = srcs[a].at[peer_lin] if scatter else srcs[a]
            mk = lambda slot: pltpu.make_async_remote_copy(
                src_ref=src, dst_ref=lands[a].at[slot], send_sem=send_sems.at[a * (N_DEV - 1) + k - 1],
                recv_sem=recv_sems.at[a * (N_DEV - 1) + k - 1], device_id=peer, device_id_type=MESH)
            out.append((mk(me), mk(peer_lin)))
    return out


def _exchange_start(srcs, lands, name, scatter, after=()):
    n = len(srcs)
    n_after = len(after)

    def body(*refs):
        srcs_r, lands_r = refs[:n], refs[n:2 * n]
        send_sems, recv_sems = refs[2 * n + n_after], refs[2 * n + n_after + 1]
        token = refs[-1]
        for outgoing, _ in _split_copies(srcs_r, lands_r, send_sems, recv_sems, scatter):
            outgoing.start()
        token[...] = jnp.zeros_like(token)

    hbm = lambda a: pltpu.HBM(a.shape, a.dtype)
    res = pl.pallas_call(
        body, name=name,
        out_shape=(pltpu.SemaphoreType.DMA((n * (N_DEV - 1),)), pltpu.SemaphoreType.DMA((n * (N_DEV - 1),)),
                   *[hbm(a) for a in srcs], *[hbm(a) for a in lands], jax.ShapeDtypeStruct((8, 128), f32)),
        in_specs=[HBM_SPEC] * (2 * n) + [pl.BlockSpec(memory_space=pl.ANY)] * n_after,
        out_specs=(SEM_SPEC, SEM_SPEC, *[HBM_SPEC] * (2 * n), pl.BlockSpec(memory_space=pltpu.VMEM)),
        input_output_aliases={i: 2 + i for i in range(2 * n)},
        compiler_params=pltpu.CompilerParams(has_side_effects=DATAFLOW_EFFECT),
    )(*[pltpu.with_memory_space_constraint(a, pltpu.HBM) for a in list(srcs) + list(lands)], *after)
    return (res[0], res[1], list(res[2:2 + n]), list(res[2 + n:2 + 2 * n])), res[-1]


def _exchange_wait(handle, after, name, scatter):
    send_sems, recv_sems, srcs, lands = handle
    n = len(srcs)
    after = list(after) if isinstance(after, (list, tuple)) else [after]

    def body(*refs):
        srcs_r, lands_r = refs[:n], refs[n:2 * n]
        send_s, recv_s = refs[2 * n], refs[2 * n + 1]
        for outgoing, incoming in _split_copies(srcs_r, lands_r, send_s, recv_s, scatter):
            outgoing.wait_send()
            incoming.wait_recv()

    hbm = lambda a: pltpu.HBM(a.shape, a.dtype)
    res = pl.pallas_call(
        body, name=name, out_shape=tuple(hbm(a) for a in list(srcs) + list(lands)),
        in_specs=[HBM_SPEC] * (2 * n) + [SEM_SPEC, SEM_SPEC] + [pl.BlockSpec(memory_space=pl.ANY)] * len(after),
        out_specs=tuple([HBM_SPEC] * (2 * n)),
        input_output_aliases={i: i for i in range(2 * n)},
        compiler_params=pltpu.CompilerParams(has_side_effects=DATAFLOW_EFFECT),
    )(*srcs, *lands, send_sems, recv_sems, *after)
    return list(res[n:])


def _split_call(body, name, hbm_ins, kept, in_sems, n_new_sems, after, with_token):
    n_in, n_sem = len(hbm_ins), len(in_sems)
    out_shape, out_specs = [], []
    if n_new_sems:
        out_shape += [pltpu.SemaphoreType.DMA((n_new_sems,))] * 2
        out_specs += [SEM_SPEC] * 2
    first_kept = len(out_shape)
    out_shape += [pltpu.HBM(hbm_ins[i].shape, hbm_ins[i].dtype) for i in kept]
    out_specs += [HBM_SPEC] * len(kept)
    if with_token:
        out_shape.append(jax.ShapeDtypeStruct((8, 128), f32))
        out_specs.append(pl.BlockSpec(memory_space=pltpu.VMEM))

    def wrapped(*refs):
        outs = refs[n_in + n_sem + len(after):]
        body(refs[:n_in], refs[n_in:n_in + n_sem], outs[:2] if n_new_sems else ())
        if with_token:
            outs[-1][...] = jnp.zeros_like(outs[-1])

    return pl.pallas_call(
        wrapped, name=name, out_shape=tuple(out_shape),
        in_specs=[HBM_SPEC] * n_in + [SEM_SPEC] * n_sem + [pl.BlockSpec(memory_space=pl.ANY)] * len(after),
        out_specs=tuple(out_specs), input_output_aliases={i: first_kept + j for j, i in enumerate(kept)},
        compiler_params=pltpu.CompilerParams(has_side_effects=DATAFLOW_EFFECT),
    )(*[pltpu.with_memory_space_constraint(a, pltpu.HBM) for a in hbm_ins], *in_sems, *after)


def _rcopy(src, dst, sems, k, to):
    return pltpu.make_async_remote_copy(src_ref=src, dst_ref=dst, send_sem=sems[0].at[k], recv_sem=sems[1].at[k],
                                        device_id=to, device_id_type=MESH)


def _gather2_start(shards, lands, name, after):
    n = len(shards)

    def body(ins, sems_in, sems_out):
        x, y, c, me = _position()
        for a in range(n):
            for k, to in enumerate(((x, y, 1 - c), (1 - x, y, c), (x, 1 - y, c))):
                _rcopy(ins[a], ins[n + a].at[me], sems_out, 3 * a + k, to).start()

    res = _split_call(body, name, list(shards) + list(lands), range(2 * n), (), 3 * n, after, True)
    return (res[0], res[1], list(res[2:2 + n]), list(res[2 + n:2 + 2 * n])), res[-1]


def _gather2_forward1(handle, after, name):
    send_sems, recv_sems, srcs, lands = handle
    n = len(srcs)

    def body(ins, sems_in, sems_out):
        x, y, c, me = _position()
        sib, xn, yn = (x, y, 1 - c), (1 - x, y, c), (x, 1 - y, c)
        for a in range(n):
            for k, peer in enumerate((sib, xn, yn)):
                _rcopy(ins[a], ins[n + a].at[me], sems_in, 3 * a + k, peer).wait_send()
                _rcopy(ins[a], ins[n + a].at[_lin(peer)], sems_in, 3 * a + k, peer).wait_recv()
        for a in range(n):
            land = ins[n + a]
            _rcopy(land.at[_lin(xn)], land.at[_lin(xn)], sems_out, 3 * a, sib).start()
            _rcopy(land.at[_lin(yn)], land.at[_lin(yn)], sems_out, 3 * a + 1, sib).start()

            @pl.when(c == 0)
            def _():
                _rcopy(land.at[_lin(xn)], land.at[_lin(xn)], sems_out, 3 * a + 2, yn).start()

            @pl.when(c == 1)
            def _():
                _rcopy(land.at[_lin(yn)], land.at[_lin(yn)], sems_out, 3 * a + 2, xn).start()

    res = _split_call(body, name, list(srcs) + list(lands), range(n, 2 * n), (send_sems, recv_sems), 3 * n, [after], False)
    return (res[0], res[1], list(res[2:]))


def _gather2_forward2(handle, after, name):
    send_sems, recv_sems, lands = handle
    n = len(lands)

    def body(ins, sems_in, sems_out):
        x, y, c, me = _position()
        sib, dg = (x, y, 1 - c), _lin((1 - x, 1 - y, c))
        for a in range(n):
            for k, slot in enumerate((_lin((1 - x, y, 1 - c)), _lin((x, 1 - y, 1 - c)), dg)):
                done = _rcopy(ins[a].at[slot], ins[a].at[slot], sems_in, 3 * a + k, sib)
                done.wait_send()
                done.wait_recv()
        for a in range(n):
            _rcopy(ins[a].at[dg], ins[a].at[dg], sems_out, a, sib).start()

    res = _split_call(body, name, list(lands), range(n), (send_sems, recv_sems), n, [after], False)
    return (res[0], res[1], list(res[2:]))


def _gather2_wait(handle, after, name):
    send_sems, recv_sems, lands = handle
    n = len(lands)

    def body(ins, sems_in, sems_out):
        x, y, c, me = _position()
        slot = _lin((1 - x, 1 - y, 1 - c))
        for a in range(n):
            done = _rcopy(ins[a].at[slot], ins[a].at[slot], sems_in, a, (x, y, 1 - c))
            done.wait_send()
            done.wait_recv()

    return list(_split_call(body, name, list(lands), range(n), (send_sems, recv_sems), 0, [after], False))


def _own_block_filled(block, me):
    land = lax.empty((N_DEV,) + block.shape, block.dtype)
    return lax.dynamic_update_index_in_dim(land, block, me, 0)


ANY_SPEC = pl.BlockSpec(memory_space=pl.ANY)


def _mm(a, b, name, ta=False, tb=False, out_dtype=f32, after=()):
    if ta:
        k_dim, m_dim = a.shape
    else:
        m_dim, k_dim = a.shape
    if tb:
        n_dim, k2 = b.shape
    else:
        k2, n_dim = b.shape
    assert k_dim == k2, (a.shape, b.shape)
    assert a.dtype == bf16 and b.dtype == bf16
    bm = _pick(m_dim, (512, 768, 640, 256, 128))
    bn = _pick(n_dim, (512, 640, 256, 128))
    bk = k_dim if k_dim <= 2560 else _pick(k_dim, (1024, 1280, 768, 512))
    nk = k_dim // bk
    a_spec = (pl.BlockSpec((bk, bm), lambda i, j, k: (k, i)) if ta
              else pl.BlockSpec((bm, bk), lambda i, j, k: (i, k)))
    b_spec = (pl.BlockSpec((bn, bk), lambda i, j, k: (j, k)) if tb
              else pl.BlockSpec((bk, bn), lambda i, j, k: (k, j)))
    dims = (((0 if ta else 1,), (1 if tb else 0,)), ((), ()))

    n_after = len(after)

    def body_single(a_ref, b_ref, *rest):
        o_ref = rest[n_after]
        o_ref[...] = lax.dot_general(a_ref[...], b_ref[...], dims, preferred_element_type=f32).astype(o_ref.dtype)

    def body(a_ref, b_ref, *rest):
        o_ref, acc_ref = rest[n_after:]
        k = pl.program_id(2)

        @pl.when(k == 0)
        def _():
            acc_ref[...] = jnp.zeros_like(acc_ref)

        acc_ref[...] += lax.dot_general(a_ref[...], b_ref[...], dims, preferred_element_type=f32)

        @pl.when(k == nk - 1)
        def _():
            o_ref[...] = acc_ref[...].astype(o_ref.dtype)

    return pl.pallas_call(
        body_single if nk == 1 else body, name=name, out_shape=jax.ShapeDtypeStruct((m_dim, n_dim), out_dtype),
        grid=(m_dim // bm, n_dim // bn, nk), in_specs=[a_spec, b_spec] + [ANY_SPEC] * n_after,
        out_specs=pl.BlockSpec((bm, bn), lambda i, j, k: (i, j)),
        scratch_shapes=[] if nk == 1 else [pltpu.VMEM((bm, bn), f32)],
        compiler_params=_cparams("parallel", "parallel", "arbitrary"),
    )(a, b, *after)


def _rin(arr, width=None, cb=0, roff=0):
    return (arr, arr.shape[1] if width is None else width, cb, roff)


def _rowcall(fn, name, rows, tm, row_ins, par_ins, row_outs, acc_outs=(), after=()):
    nr, npar, nro, n_after = len(row_ins), len(par_ins), len(row_outs), len(after)
    in_specs, args = [], []
    for arr, width, cb, roff in row_ins:
        if roff >= 0:
            imap = lambda i, cb=cb, roff=roff: (i + roff, cb)
        else:
            imap = lambda i, cb=cb, roff=roff: (jnp.maximum(i + roff, 0), cb)
        in_specs.append(pl.BlockSpec((tm, width), imap))
        args.append(arr)
    for p in par_ins:
        in_specs.append(pl.BlockSpec(p.shape, lambda i: (0, 0)))
        args.append(p)
    out_shape, out_specs = [], []
    for width, dt in row_outs:
        out_shape.append(jax.ShapeDtypeStruct((rows, width), dt))
        out_specs.append(pl.BlockSpec((tm, width), lambda i: (i, 0)))
    for p, width in acc_outs:
        out_shape.append(jax.ShapeDtypeStruct((p, width), f32))
        out_specs.append(pl.BlockSpec((p, width), lambda i: (0, 0)))

    def body(*refs):
        i = pl.program_id(0)
        res = fn(i, *[r[...] for r in refs[:nr + npar]])
        outs = refs[nr + npar + n_after:]
        for o, v in zip(outs[:nro], res[:nro]):
            o[...] = v.astype(o.dtype)
        if acc_outs:
            @pl.when(i == 0)
            def _():
                for o in outs[nro:]:
                    o[...] = jnp.zeros_like(o)

            for o, v in zip(outs[nro:], res[nro:]):
                o[...] += v

    return pl.pallas_call(
        body, name=name, out_shape=out_shape, grid=(rows // tm,), in_specs=in_specs + [ANY_SPEC] * n_after,
        out_specs=out_specs, compiler_params=_cparams("arbitrary"),
    )(*args, *after)


def _rms(x, g):
    return x * lax.rsqrt(jnp.mean(x * x, axis=-1, keepdims=True) + EPS) * g


def _normmod(x, g, sc, sh):
    return _rms(x, g) * (1.0 + sc) + sh


def _gelu(x):
    return 0.5 * x * (1.0 + jnp.tanh(0.7978845608028654 * (x + 0.044715 * (x * x * x))))


def _sigmoid(x):
    return 0.5 * (jnp.tanh(0.5 * x) + 1.0)


def _coeff_parts(pre_a, pre_x, ba, bx, lam):
    r = _sigmoid(pre_a + ba)
    ig = _sigmoid(pre_x + bx)
    nl = -lam
    sp = jnp.maximum(nl, 0.0) + jnp.log(1.0 + jnp.exp(-jnp.abs(nl)))
    la = -RG_C * r * sp
    a = jnp.exp(la)
    one_minus_a2 = -jnp.tanh(la) * (a * a + 1.0)
    inv_m = lax.rsqrt(one_minus_a2)
    return r, ig, sp, a, one_minus_a2 * inv_m, inv_m


def _coeff(pre_a, pre_x, u, ba, bx, lam):
    _, ig, _, a, m, _ = _coeff_parts(pre_a, pre_x, ba, bx, lam)
    return a, m * (ig * u)


def _coeff_bwd(pre_a, pre_x, u, ba, bx, lam, da, db):
    r, ig, sp, a, m, inv_m = _coeff_parts(pre_a, pre_x, ba, bx, lam)
    dbu = db * u
    dig = dbu * m
    dm = dbu * ig
    dla = a * (da - dm * a * inv_m)
    dpa = dla * (-RG_C * sp) * (r * (1.0 - r))
    dpx = dig * (ig * (1.0 - ig))
    dsp = jnp.sum(dla * (-RG_C * r), axis=0, keepdims=True)
    dlam = -dsp * _sigmoid(-lam)
    return (dpa, dpx, db * m * ig, jnp.sum(dpa, axis=0, keepdims=True), jnp.sum(dpx, axis=0, keepdims=True), dlam)


SCAN_CHUNK = 256


def _scan_call(a, v, chunk_of, reverse, name, backward, after=()):
    rows, width = a.shape
    n_out = 1 if backward else 2
    nt = SCAN_CHUNK // 8

    def body(a_ref, v_ref, *rest):
        outs, state_ref = rest[len(after):-1], rest[-1]

        @pl.when(pl.program_id(0) == 0)
        def _():
            state_ref[...] = jnp.zeros_like(state_ref)

        rid = lax.broadcasted_iota(jnp.int32, (8, width), 0)
        last_row = 0 if reverse else 7

        def shift(x, s, fill):
            rolled = pltpu.roll(x, (8 - s) if reverse else s, axis=0)
            return jnp.where((rid >= 8 - s) if reverse else (rid < s), fill, rolled)

        def tile(j, st):
            t0 = pl.multiple_of((nt - 1 - j if reverse else j) * 8, 8)
            at = a_ref[pl.ds(t0, 8), :]
            coef = shift(at, 1, 1.0) if backward else at
            acc = v_ref[pl.ds(t0, 8), :]
            for s in (1, 2, 4):
                acc = coef * shift(acc, s, 0.0) + acc
                coef = coef * shift(coef, s, 1.0)
            out = coef * st + acc
            outs[0][pl.ds(t0, 8), :] = out
            last = out[last_row:last_row + 1]
            if backward:
                return at[last_row:last_row + 1] * last
            outs[1][pl.ds(t0, 8), :] = shift(out, 1, st)
            return last

        state_ref[0:1, :] = lax.fori_loop(0, nt, tile, state_ref[0:1, :])

    spec = pl.BlockSpec((SCAN_CHUNK, width), lambda t: (chunk_of(t), 0))
    return pl.pallas_call(
        body, name=name, out_shape=[jax.ShapeDtypeStruct((rows, width), f32)] * n_out,
        grid=(rows // SCAN_CHUNK,), in_specs=[spec, spec] + [ANY_SPEC] * len(after), out_specs=[spec] * n_out,
        scratch_shapes=[pltpu.VMEM((8, width), f32)],
        compiler_params=_cparams("arbitrary"),
    )(a, v, *after)


CONV_CHUNK = 256


def _fill_padded(pad_ref, src_ref, start, n):
    cb = pad_ref.shape[1]
    pad_ref[pl.ds(0, HALO), :] = jnp.zeros((HALO, cb), f32)
    pad_ref[pl.ds(HALO, n), :] = src_ref[pl.ds(start, n), :].astype(f32)
    pad_ref[pl.ds(HALO + n, HALO), :] = jnp.zeros((HALO, cb), f32)


def _dwconv_fwd(x, x_cb0, w, b, taps, pad_left, segments, cb, name, emit_bf16):
    rows = x.shape[0]
    width = w.shape[1]

    def body(x_ref, w_ref, b_ref, *rest):
        outs, xp = rest[:-1], rest[-1]
        for start, n in segments:
            _fill_padded(xp, x_ref, start, n)
            for c0 in range(0, n, CONV_CHUNK):
                acc = jnp.zeros((CONV_CHUNK, cb), f32) + b_ref[...]
                for k in range(taps):
                    acc = acc + w_ref[k:k + 1, :] * xp[pl.ds(HALO + c0 + k - pad_left, CONV_CHUNK), :]
                for o in outs:
                    o[pl.ds(start + c0, CONV_CHUNK), :] = acc.astype(o.dtype)

    out_dtypes = [f32, bf16] if emit_bf16 else [f32]
    return pl.pallas_call(
        body, name=name, out_shape=[jax.ShapeDtypeStruct((rows, width), dt) for dt in out_dtypes],
        grid=(width // cb,),
        in_specs=[pl.BlockSpec((rows, cb), lambda j: (0, j + x_cb0)), pl.BlockSpec((taps, cb), lambda j: (0, j)),
                  pl.BlockSpec((1, cb), lambda j: (0, j))],
        out_specs=[pl.BlockSpec((rows, cb), lambda j: (0, j))] * len(out_dtypes),
        scratch_shapes=[pltpu.VMEM((rows + 2 * HALO, cb), f32)],
        compiler_params=_cparams("parallel"),
    )(x, w, b)


def _dwconv_bwd(douts, x, x_cb0, w, taps, pad_left, segments, cb, name, dx_dtype):
    rows = x.shape[0]
    width = w.shape[1]
    nd = len(douts)

    def body(*refs):
        d_refs, x_ref, w_ref = refs[:nd], refs[nd], refs[nd + 1]
        dx_ref, dw_ref, db_ref, dp, dsum = refs[nd + 2:]
        dw_ref[...] = jnp.zeros_like(dw_ref)
        db_ref[...] = jnp.zeros_like(db_ref)
        if nd > 1:
            total = d_refs[0][...]
            for r in d_refs[1:]:
                total = total + r[...]
            dsum[...] = total
            d_ref = dsum
        else:
            d_ref = d_refs[0]
        for start, n in segments:
            _fill_padded(dp, d_ref, start, n)
            for c0 in range(0, n, CONV_CHUNK):
                db_ref[...] += jnp.sum(dp[pl.ds(HALO + c0, CONV_CHUNK), :], axis=0, keepdims=True)
                xchunk = x_ref[pl.ds(start + c0, CONV_CHUNK), :].astype(f32)
                acc = jnp.zeros((CONV_CHUNK, cb), f32)
                for k in range(taps):
                    shifted = dp[pl.ds(HALO + c0 + pad_left - k, CONV_CHUNK), :]
                    acc = acc + w_ref[k:k + 1, :] * shifted
                    dw_ref[k:k + 1, :] += jnp.sum(shifted * xchunk, axis=0, keepdims=True)
                dx_ref[pl.ds(start + c0, CONV_CHUNK), :] = acc.astype(dx_ref.dtype)

    dspec = pl.BlockSpec((rows, cb), lambda j: (0, j))
    return pl.pallas_call(
        body, name=name,
        out_shape=[jax.ShapeDtypeStruct((rows, width), dx_dtype), jax.ShapeDtypeStruct((taps, width), f32),
                   jax.ShapeDtypeStruct((1, width), f32)],
        grid=(width // cb,),
        in_specs=[dspec] * nd + [pl.BlockSpec((rows, cb), lambda j: (0, j + x_cb0)),
                                 pl.BlockSpec((taps, cb), lambda j: (0, j))],
        out_specs=[dspec, pl.BlockSpec((taps, cb), lambda j: (0, j)), pl.BlockSpec((1, cb), lambda j: (0, j))],
        scratch_shapes=[pltpu.VMEM((rows + 2 * HALO, cb), f32), pltpu.VMEM((rows, cb), f32)],
        compiler_params=_cparams("parallel"),
    )(*douts, x, w)


def _ada_forward(c16, w_ada, b_loc):
    def body(c_ref, w_ref, b_ref, o_ref):
        cv = c_ref[...]
        s = (cv * _sigmoid(cv)).astype(bf16)
        o_ref[0] = jnp.dot(s, w_ref[0].astype(bf16), preferred_element_type=f32) + b_ref[0]

    return pl.pallas_call(
        body, name="ada_forward", out_shape=jax.ShapeDtypeStruct((2, 16, ADA_SHARD), f32), grid=(2,),
        in_specs=[pl.BlockSpec((16, D), lambda l: (0, 0)), pl.BlockSpec((1, D, ADA_SHARD), lambda l: (l, 0, 0)),
                  pl.BlockSpec((1, 1, ADA_SHARD), lambda l: (l, 0, 0))],
        out_specs=pl.BlockSpec((1, 16, ADA_SHARD), lambda l: (l, 0, 0)),
        compiler_params=_cparams("parallel"),
    )(c16, w_ada, b_loc)


def _ada_backward(c16, g16, w_ada):
    def body(c_ref, g_ref, w_ref, dw_ref, ds_ref):
        cv = c_ref[...]
        s = (cv * _sigmoid(cv)).astype(bf16)
        g = g_ref[0].astype(bf16)
        dw_ref[0] = lax.dot_general(s, g, (((0,), (0,)), ((), ())), preferred_element_type=f32)
        ds = lax.dot_general(g, w_ref[0].astype(bf16), (((1,), (1,)), ((), ())), preferred_element_type=f32)
        cc = cv[8:9]
        sg = _sigmoid(cc)
        dsilu = sg * (1.0 + cc * (1.0 - sg))
        ds_ref[0] = jnp.zeros((8, D), f32) + jnp.sum(ds[8:16], axis=0, keepdims=True) * dsilu

    return pl.pallas_call(
        body, name="ada_backward",
        out_shape=[jax.ShapeDtypeStruct((2, D, ADA_SHARD), f32), jax.ShapeDtypeStruct((2, 8, D), f32)], grid=(2,),
        in_specs=[pl.BlockSpec((16, D), lambda l: (0, 0)), pl.BlockSpec((1, 16, ADA_SHARD), lambda l: (l, 0, 0)),
                  pl.BlockSpec((1, D, ADA_SHARD), lambda l: (l, 0, 0))],
        out_specs=[pl.BlockSpec((1, D, ADA_SHARD), lambda l: (l, 0, 0)), pl.BlockSpec((1, 8, D), lambda l: (l, 0, 0))],
        compiler_params=_cparams("parallel"),
    )(c16, g16, w_ada)


def _adamw(pieces, w, m, v, name, after=()):
    rows, cols = w.shape
    n_arr, n_after = len(pieces), len(after)
    tm = 256 if (rows % 256 == 0 and rows > 256) else rows
    counts = [p[1] for p in pieces]
    first_tiles = [(p[2] if len(p) > 2 else 0) // tm for p in pieces]
    pieces = [p[0] for p in pieces]

    def body(*refs):
        p_refs = refs[:n_arr]
        w_ref, m_ref, v_ref = refs[n_arr:n_arr + 3]
        g_ref, d_ref, nm_ref, nv_ref = refs[n_arr + 3 + n_after:]
        g = None
        for p_ref in p_refs:
            for j in range(p_ref.shape[0]):
                term = p_ref[j].astype(f32)
                g = term if g is None else g + term
        m2 = ADAM_B1 * m_ref[...] + (1.0 - ADAM_B1) * g
        v2 = ADAM_B2 * v_ref[...] + (1.0 - ADAM_B2) * (g * g)
        m_hat = m2 / (1.0 - ADAM_B1 ** ADAM_STEP)
        v_hat = v2 / (1.0 - ADAM_B2 ** ADAM_STEP)
        g_ref[...] = g
        d_ref[...] = -ADAM_LR * (m_hat / (jnp.sqrt(v_hat) + ADAM_EPS) + ADAM_WD * w_ref[...])
        nm_ref[...] = m2
        nv_ref[...] = v2

    spec = pl.BlockSpec((tm, cols), lambda i: (i, 0))
    return pl.pallas_call(
        body, name=name, out_shape=[jax.ShapeDtypeStruct((rows, cols), f32)] * 4, grid=(rows // tm,),
        in_specs=[pl.BlockSpec((cnt, tm, cols), lambda i, t=t: (0, i + t, 0)) for cnt, t in zip(counts, first_tiles)]
        + [spec, spec, spec]
        + [ANY_SPEC] * n_after,
        out_specs=[spec] * 4, compiler_params=_cparams("parallel"),
    )(*pieces, w, m, v, *after)


MLP_TM = 256
FB = F // N_DEV


def _stack_rows(vals, n):
    cols = vals[0].shape[1]
    rid = lax.broadcasted_iota(jnp.int32, (n, cols), 0)
    out = jnp.zeros((n, cols), f32)
    for k, v in enumerate(vals):
        out = jnp.where(rid == k, v, out)
    return out


N_MLP_PARAMS = 9


class _ParamRows:
    def __init__(self, ref):
        self.ref = ref

    def __getitem__(self, sl):
        return self.ref[8 * sl.start:8 * sl.start + 1, :]


def _resident(shape, imap):
    return pl.BlockSpec(shape, imap, pipeline_mode=pl.Buffered(1))


def _mlp_forward(xa, xa_roff, out_prev, par, w_in, w_out, layer, name):
    def body(xa_ref, op_ref, par_ref, win_ref, wout_ref, x1_ref, h_ref, r_ref, mo_ref, x2_ref, hn_ref):
        p = _ParamRows(par_ref)
        x1 = xa_ref[...] + p[0:1] * (op_ref[...] + p[1:2])
        h = _normmod(x1, p[2:3], p[3:4], p[4:5]).astype(bf16)
        x1_ref[...] = x1
        h_ref[...] = h
        mo = jnp.zeros((MLP_TM, D), f32)
        for j in range(N_DEV):
            r = jnp.maximum(jnp.dot(h, win_ref[j], preferred_element_type=f32), 0.0)
            r_ref[:, j * FB:(j + 1) * FB] = r.astype(bf16)
            mo = mo + jnp.dot((r * r).astype(bf16), wout_ref[j], preferred_element_type=f32)
        mo_ref[...] = mo.astype(bf16)
        x2 = x1 + p[5:6] * mo
        x2_ref[...] = x2
        hn_ref[...] = _normmod(x2, p[6:7], p[7:8], p[8:9]).astype(bf16)

    row = lambda width: pl.BlockSpec((MLP_TM, width), lambda i: (i, 0))
    return pl.pallas_call(
        body, name=name, grid=(T_LAT // MLP_TM,),
        out_shape=[jax.ShapeDtypeStruct((T_LAT, D), f32), jax.ShapeDtypeStruct((T_LAT, D), bf16),
                   jax.ShapeDtypeStruct((T_LAT, F), bf16), jax.ShapeDtypeStruct((T_LAT, D), bf16),
                   jax.ShapeDtypeStruct((T_LAT, D), f32), jax.ShapeDtypeStruct((T_LAT, D), bf16)],
        in_specs=[pl.BlockSpec((MLP_TM, D), lambda i: (i + xa_roff, 0)), row(D), pl.BlockSpec((8 * N_MLP_PARAMS, D), lambda i: (0, 0)),
                  _resident((N_DEV, None, D, FB), lambda i: (0, layer, 0, 0)),
                  _resident((N_DEV, None, FB, D), lambda i: (0, layer, 0, 0))],
        out_specs=[row(D), row(D), row(F), row(D), row(D), row(D)],
        compiler_params=_cparams("parallel"),
    )(xa, out_prev, par, w_in, w_out)


def _mlp_backward(dx2, x1, r, mo, out_prev, par, w_in, w_out, layer, name, after=()):
    nt = (((1,), (1,)), ((), ()))

    n_after = len(after)

    def body(dx2_ref, x1_ref, r_ref, mo_ref, op_ref, par_ref, win_ref, wout_ref, *rest):
        dx1_ref, dop_ref, dmo_ref, dhid_ref, acc_ref = rest[n_after:]
        p = _ParamRows(par_ref)
        dx2v = dx2_ref[...]
        dmo = (p[5:6] * dx2v).astype(bf16)
        dmo_ref[...] = dmo
        dh = jnp.zeros((MLP_TM, D), f32)
        mo = mo_ref[...].astype(f32)
        for j in range(N_DEV):
            rf = r_ref[:, j * FB:(j + 1) * FB].astype(f32)
            dact = lax.dot_general(dmo, wout_ref[j], nt, preferred_element_type=f32)
            dhid = (dact * (2.0 * rf)).astype(bf16)
            dhid_ref[:, j * FB:(j + 1) * FB] = dhid
            dh = dh + lax.dot_general(dhid, win_ref[j], nt, preferred_element_type=f32)
        x1 = x1_ref[...]
        _, vjp = jax.vjp(_normmod, x1, p[2:3], p[3:4], p[4:5])
        dx, dng, dsc, dsh = vjp(dh)
        dx1 = dx2v + dx
        dx1_ref[...] = dx1
        dop_ref[...] = (p[0:1] * dx1).astype(bf16)
        sums = _stack_rows([jnp.sum(dx1 * (op_ref[...] + p[1:2]), axis=0, keepdims=True),
                            p[0:1] * jnp.sum(dx1, axis=0, keepdims=True), dng, dsc, dsh,
                            jnp.sum(dx2v * mo, axis=0, keepdims=True)], 8)

        @pl.when(pl.program_id(0) == 0)
        def _():
            acc_ref[...] = jnp.zeros_like(acc_ref)

        acc_ref[...] += sums

    row = lambda width: pl.BlockSpec((MLP_TM, width), lambda i: (i, 0))
    return pl.pallas_call(
        body, name=name, grid=(T_LAT // MLP_TM,),
        out_shape=[jax.ShapeDtypeStruct((T_LAT, D), f32), jax.ShapeDtypeStruct((T_LAT, D), bf16),
                   jax.ShapeDtypeStruct((T_LAT, D), bf16), jax.ShapeDtypeStruct((T_LAT, F), bf16),
                   jax.ShapeDtypeStruct((8, D), f32)],
        in_specs=[row(D), row(D), row(F), row(D), row(D), pl.BlockSpec((8 * N_MLP_PARAMS, D), lambda i: (0, 0)),
                  _resident((N_DEV, None, D, FB), lambda i: (0, layer, 0, 0)),
                  _resident((N_DEV, None, FB, D), lambda i: (0, layer, 0, 0))] + [ANY_SPEC] * n_after,
        out_specs=[row(D), row(D), row(D), row(F), pl.BlockSpec((8, D), lambda i: (0, 0))],
        compiler_params=_cparams("arbitrary"),
    )(dx2, x1, r, mo, out_prev, par, w_in, w_out, *after)


def _mlp_weight_grads(h, dhid, r, dmo, layer, other, tag):
    tn = (((0,), (0,)), ((), ()))

    def body_in(h_ref, dhid_ref, *rest):
        rest[-1][...] = lax.dot_general(h_ref[...], dhid_ref[...], tn, preferred_element_type=f32).astype(bf16)

    def body_out(r_ref, dmo_ref, *rest):
        rf = r_ref[...].astype(f32)
        rest[-1][...] = lax.dot_general((rf * rf).astype(bf16), dmo_ref[...], tn,
                                        preferred_element_type=f32).astype(bf16)

    def call(body, name, operands, specs, block, prev):
        extra = [] if prev is None else [prev]
        return pl.pallas_call(
            body, name=name, grid=(N_DEV,), out_shape=jax.ShapeDtypeStruct((N_DEV, 2) + block, bf16),
            in_specs=specs + [pl.BlockSpec(memory_space=pl.ANY)] * len(extra),
            out_specs=pl.BlockSpec((None, None) + block, lambda j: (j, layer, 0, 0)),
            input_output_aliases={} if prev is None else {2: 0},
            compiler_params=_cparams("parallel"),
        )(*operands, *extra)

    dw_in = call(body_in, tag + "_mlp_in_dw", [h, dhid],
                 [_resident((T_LAT, D), lambda j: (0, 0)), pl.BlockSpec((T_LAT, FB), lambda j: (0, j))], (D, FB),
                 None if other is None else other[0])
    dw_out = call(body_out, tag + "_mlp_out_dw", [r, dmo],
                  [pl.BlockSpec((T_LAT, FB), lambda j: (0, j)), _resident((T_LAT, D), lambda j: (0, 0))], (FB, D),
                  None if other is None else other[1])
    return dw_in, dw_out


def _pos_embed():
    n_rows = T_LAT // GRID_W
    q = D // 4
    omega = 1.0 / (POS_BASE ** (jnp.arange(q, dtype=f32) / q))
    er = jnp.arange(n_rows, dtype=jnp.int32).astype(f32)[:, None] * omega[None, :]
    ec = jnp.arange(GRID_W, dtype=jnp.int32).astype(f32)[:, None] * omega[None, :]
    by_row = jnp.concatenate([jnp.sin(er), jnp.cos(er)], axis=-1)[:, None, :]
    by_col = jnp.concatenate([jnp.sin(ec), jnp.cos(ec)], axis=-1)[None, :, :]
    full = jnp.concatenate([jnp.broadcast_to(by_row, (n_rows, GRID_W, D // 2)),
                            jnp.broadcast_to(by_col, (n_rows, GRID_W, D // 2))], axis=-1)
    return full.reshape(T_LAT, D)


HALF = R // 2
BLK_PER_HALF = N_BLK // 2
N_PARTS = 4


def _gate_matrix(w_a, w_x):
    eye = jnp.eye(BLK_PER_HALF, dtype=bf16)
    cols = []
    for h in range(2):
        for d in range(2):
            for w in (w_a, w_x):
                blocks = w[d, BLK_PER_HALF * h:BLK_PER_HALF * (h + 1)].astype(bf16)
                cols.append(jnp.einsum("hij,hg->higj", blocks, eye).reshape(HALF, HALF))
    return jnp.concatenate(cols, axis=1)


def _gate_blocks(dwg, part):
    out = []
    for h in range(2):
        blk = dwg[:, (N_PARTS * h + part) * HALF:(N_PARTS * h + part + 1) * HALF]
        blk = blk.reshape(BLK_PER_HALF, BLK, BLK_PER_HALF, BLK)
        out.append(jnp.moveaxis(jnp.diagonal(blk, axis1=0, axis2=2), -1, 0))
    return jnp.concatenate(out, axis=0)


GATE_BM = 768


def _gates_dx(dpre, wg, after=()):
    rows = dpre.shape[0]
    n_after = len(after)

    def body(d_ref, w_ref, *rest):
        rest[n_after][...] = lax.dot_general(d_ref[...], w_ref[...], (((1,), (1,)), ((), ())),
                                             preferred_element_type=f32)

    return pl.pallas_call(
        body, name="l0_gates_dx", grid=(rows // GATE_BM, 2), out_shape=jax.ShapeDtypeStruct((rows, R), f32),
        in_specs=[pl.BlockSpec((GATE_BM, N_PARTS * HALF), lambda i, h: (i, h)),
                  pl.BlockSpec((HALF, N_PARTS * HALF), lambda i, h: (0, h))] + [ANY_SPEC] * n_after,
        out_specs=pl.BlockSpec((GATE_BM, HALF), lambda i, h: (i, h)),
        compiler_params=_cparams("parallel", "parallel"),
    )(dpre, wg, *after)


COEFF_TM = 768


def _dir_params(d, *params):
    specs = [pl.BlockSpec((None, 1, HALF), lambda h, i: (d, 0, h))] * len(params)
    return specs, [p.reshape(2, 1, R) for p in params]


def _gates_coeff_fwd(ub, u, wg, ba, bx, lam, d):
    rows = u.shape[0]

    def body(ub_ref, u_ref, w_ref, ba_ref, bx_ref, lam_ref, a_ref, b_ref):
        pre = jnp.dot(ub_ref[...], w_ref[...], preferred_element_type=f32)
        a, b = _coeff(pre[:, :HALF], pre[:, HALF:], u_ref[...], ba_ref[...], bx_ref[...], lam_ref[...])
        a_ref[...] = a
        b_ref[...] = b

    tile = pl.BlockSpec((COEFF_TM, HALF), lambda h, i: (i, h))
    pspecs, pargs = _dir_params(d, ba, bx, lam)
    return pl.pallas_call(
        body, name=f"l0_gates_coeff_{d}", grid=(2, rows // COEFF_TM),
        out_shape=[jax.ShapeDtypeStruct((rows, R), f32)] * 2,
        in_specs=[tile, tile, pl.BlockSpec((HALF, 2 * HALF), lambda h, i: (0, 2 * h + d))] + pspecs,
        out_specs=[tile, tile], compiler_params=_cparams("parallel", "parallel"),
    )(ub, u, wg, *pargs)


def _gates_coeff_bwd(ub, u, dh, yp, wg, ba, bx, lam, d, dpre_prev):
    rows = u.shape[0]
    n_prev = 0 if dpre_prev is None else 1

    def body(ub_ref, u_ref, dh_ref, yp_ref, w_ref, ba_ref, bx_ref, lam_ref, *rest):
        dpre_ref, du_ref, dba_ref, dbx_ref, dlam_ref = rest[n_prev:]
        pre = jnp.dot(ub_ref[...], w_ref[...], preferred_element_type=f32)
        dhv = dh_ref[...]
        dpa, dpx, du, dba, dbx, dlam = _coeff_bwd(pre[:, :HALF], pre[:, HALF:], u_ref[...], ba_ref[...], bx_ref[...],
                                                  lam_ref[...], dhv * yp_ref[...], dhv)
        dpre_ref[:, :HALF] = dpa.astype(bf16)
        dpre_ref[:, HALF:] = dpx.astype(bf16)
        du_ref[...] = du

        @pl.when(pl.program_id(1) == 0)
        def _():
            dba_ref[...] = jnp.zeros_like(dba_ref)
            dbx_ref[...] = jnp.zeros_like(dbx_ref)
            dlam_ref[...] = jnp.zeros_like(dlam_ref)

        dba_ref[...] += dba
        dbx_ref[...] += dbx
        dlam_ref[...] += dlam

    tile = pl.BlockSpec((COEFF_TM, HALF), lambda h, i: (i, h))
    acc = pl.BlockSpec((1, HALF), lambda h, i: (0, h))
    pspecs, pargs = _dir_params(d, ba, bx, lam)
    extra = [] if dpre_prev is None else [dpre_prev]
    return pl.pallas_call(
        body, name=f"l0_gates_coeff_bwd_{d}", grid=(2, rows // COEFF_TM),
        out_shape=[jax.ShapeDtypeStruct((rows, 2 * N_PARTS * HALF), bf16), jax.ShapeDtypeStruct((rows, R), f32)]
        + [jax.ShapeDtypeStruct((1, R), f32)] * 3,
        in_specs=[tile] * 4 + [pl.BlockSpec((HALF, 2 * HALF), lambda h, i: (0, 2 * h + d))] + pspecs
        + [ANY_SPEC] * n_prev,
        out_specs=[pl.BlockSpec((COEFF_TM, 2 * HALF), lambda h, i: (i, 2 * h + d)), tile, acc, acc, acc],
        input_output_aliases={8: 0} if n_prev else {}, compiler_params=_cparams("parallel", "arbitrary"),
    )(ub, u, dh, yp, wg, *pargs, *extra)


def _gates_dw(u, dpre):
    rows = u.shape[0]

    def body(u_ref, d_ref, o_ref):
        o_ref[...] = lax.dot_general(u_ref[...], d_ref[...], (((0,), (0,)), ((), ())), preferred_element_type=f32)

    return pl.pallas_call(
        body, name="l0_gates_dw", grid=(2 * N_PARTS,), out_shape=jax.ShapeDtypeStruct((HALF, 2 * N_PARTS * HALF), f32),
        in_specs=[pl.BlockSpec((rows, HALF), lambda j: (0, j // N_PARTS)), pl.BlockSpec((rows, HALF), lambda j: (0, j))],
        out_specs=pl.BlockSpec((HALF, HALF), lambda j: (0, j)), compiler_params=_cparams("parallel"),
    )(u, dpre)


N_SCAN_CHUNKS = T_ALL // SCAN_CHUNK
SCAN_FWD = lambda t: t
SCAN_FWD_BWD = lambda t: N_SCAN_CHUNKS - 1 - t
SCAN_REV = lambda t: jnp.where(t == 0, 0, N_SCAN_CHUNKS - t)
SCAN_REV_BWD = lambda t: jnp.where(t == N_SCAN_CHUNKS - 1, 0, t + 1)
CONV_SEGMENTS = ((0, T_CTX), (T_CTX, T_LAT))
FUSED_TM = 256


def _token_rows(x, ctx):
    return (jnp.concatenate([ctx, x], axis=0),
            jnp.concatenate([jnp.zeros((T_CTX, D), f32), _pos_embed()], axis=0))


def _local_step(xcat, poscat, target, mods, cmod, wts, late_weights, send_grads, reduce_loss, start_after=()):
    sh1, sc1, g1, sh2, sc2, g2 = [[mods[l, i][None] for l in range(2)] for i in range(N_MOD)]
    ng = wts["norm_g"]
    scp = jnp.concatenate([cmod[1][None], sc1[0]], axis=0)
    shp = jnp.concatenate([cmod[0][None], sh1[0]], axis=0)

    ctx_tiles = T_CTX // FUSED_TM
    nt = (((1,), (1,)), ((), ()))

    def blend(i, p):
        sel = jnp.where(i < ctx_tiles, 1.0, 0.0)
        return sel * p[0:1] + (1.0 - sel) * p[1:2]

    def f_pre0(i, xc, pos, g, scp_, shp_, w):
        x0 = xc + pos
        h = _normmod(x0, g, blend(i, scp_), blend(i, shp_)).astype(bf16)
        return x0, h, jnp.dot(h, w, preferred_element_type=f32)

    x0cat, h0, gr = _rowcall(f_pre0, "l0_prenorm_in_proj", T_ALL, FUSED_TM, [_rin(xcat), _rin(poscat)],
                             [ng[0, 0][None], scp, shp, wts["rec_w_in"]], [(D, f32), (D, bf16), (2 * R, f32)],
                             after=start_after)
    u, ub = _dwconv_fwd(gr, R // 256, wts["rec_conv_w"], wts["rec_conv_b"], 4, 1, CONV_SEGMENTS, 256,
                        "l0_conv", True)
    gate_args = (wts["gates"], wts["rec_b_a"], wts["rec_b_x"], wts["rec_lambda"])
    a0, b0 = _gates_coeff_fwd(ub, u, *gate_args, 0)
    a1, b1 = _gates_coeff_fwd(ub, u, *gate_args, 1)
    halfway = late_weights("mlp_halfway", a1)
    y0, yp0 = _scan_call(a0, b0, SCAN_FWD, False, "l0_scan_fwd", False, after=[halfway])
    y1, yp1 = _scan_call(a1, b1, SCAN_REV, True, "l0_scan_rev", False)

    wts = dict(wts, **late_weights("mlp", y1))

    def f_gate_out(i, gp, y0_, y1_, w):
        z = (_gelu(gp) * (y0_ + y1_)).astype(bf16)
        return z, jnp.dot(z, w, preferred_element_type=f32)

    zb, out0 = _rowcall(f_gate_out, "l0_gate_out_proj", T_LAT, FUSED_TM,
                        [_rin(gr, R, 0, ctx_tiles), _rin(y0, None, 0, ctx_tiles), _rin(y1, None, 0, ctx_tiles)],
                        [wts["rec_w_out"]], [(R, bf16), (D, f32)])

    zero_d = jnp.zeros((1, D), f32)

    def mlp_params(rows):
        rows = rows + [zero_d] * (N_MLP_PARAMS - len(rows))
        return jnp.concatenate([jnp.broadcast_to(r, (8, D)) for r in rows], axis=0)

    par0 = mlp_params([g1[0], zero_d, ng[0, 1][None], sc2[0], sh2[0], g2[0], ng[1, 0][None], sc1[1], sh1[1]])
    x1, h1, r0, mo0, x2, h2 = _mlp_forward(x0cat, T_CTX // MLP_TM, out0, par0, wts["mlp_w_in"], wts["mlp_w_out"], 0,
                                           "l0_mlp")

    wts = dict(wts, **late_weights("conf", x2))
    def glu(pa, pb, b1):
        return (pa + b1[:, :D]) * _sigmoid(pb + b1[:, D:])

    def f_pw1_glu(i, h_, b1, w):
        p = jnp.dot(h_, w, preferred_element_type=f32)
        return glu(p[:, :D], p[:, D:], b1), p

    zg, pw = _rowcall(f_pw1_glu, "l1_pw1_glu", T_LAT, FUSED_TM, [_rin(h2)], [wts["conf_b_pw1"], wts["conf_w_pw1"]],
                      [(D, f32), (2 * D, bf16)])
    (zc,) = _dwconv_fwd(zg, 0, wts["conf_conv_w"], wts["conf_conv_b"], 31, 15, ((0, T_LAT),), 128, "l1_conv", False)

    def ln_silu(z, lg, lb):
        mu = jnp.mean(z, axis=-1, keepdims=True)
        zc_ = z - mu
        var = jnp.mean(zc_ * zc_, axis=-1, keepdims=True)
        yv = zc_ * lax.rsqrt(var + EPS) * lg + lb
        return yv * _sigmoid(yv)

    def f_lnsilu_pw2(i, z, lg, lb, w):
        s = ln_silu(z, lg, lb).astype(bf16)
        return s, jnp.dot(s, w, preferred_element_type=f32)

    sb, out1 = _rowcall(f_lnsilu_pw2, "l1_ln_silu_pw2", T_LAT, FUSED_TM, [_rin(zc)],
                        [wts["conf_ln_g"], wts["conf_ln_b"], wts["conf_w_pw2"]], [(D, bf16), (D, f32)])
    par1 = mlp_params([g1[1], wts["conf_b_pw2"], ng[1, 1][None], sc2[1], sh2[1], g2[1]])
    x3, h3, r1, mo1, x4, _ = _mlp_forward(x2, 0, out1, par1, wts["mlp_w_in"], wts["mlp_w_out"], 1, "l1_mlp")

    def loss_fn(x4_, fg, tgt):
        err = _rms(x4_, fg) - tgt
        per_row = jnp.mean(err * err, axis=-1, keepdims=True)
        return 0.5 * jnp.sum(per_row, axis=0, keepdims=True)

    def f_head(i, x4_, tgt, fg):
        loss, vjp = jax.vjp(lambda a, e: loss_fn(a, e, tgt), x4_, fg)
        dx, dfg = vjp(jnp.ones((1, 1), f32))
        return dx, jnp.broadcast_to(loss, (1, 128)), dfg

    dx4, loss_acc, dfinal_g = _rowcall(f_head, "head", T_LAT, FUSED_TM, [_rin(x4), _rin(target)], [wts["final_g"]],
                                       [(D, f32)], [(1, 128), (1, D)])

    grads = {"final_g": dfinal_g}
    loss = reduce_loss(loss_acc[0, 0])

    dx3, dout1, dmo1, dhid1, acc1 = _mlp_backward(dx4, x3, r1, mo1, out1, par1, wts["mlp_w_in"], wts["mlp_w_out"], 1,
                                                  "l1_mlp_bwd", after=[loss.reshape(1, 1)])
    mlp_dw = _mlp_weight_grads(h3, dhid1, r1, dmo1, 1, None, "l1")
    dg1_1, db_pw2, dng11, dsc2_1, dsh2_1, dg2_1 = [acc1[k:k + 1] for k in range(6)]

    grads["conf_w_pw2"] = _mm(sb, dout1, "l1_pw2_dw", ta=True, out_dtype=bf16)
    grads["conf_b_pw2"] = db_pw2

    def f_pw2_lnsilu_bwd(i, z, dout, lg, lb, w):
        ds = lax.dot_general(dout, w, nt, preferred_element_type=f32)
        _, vjp = jax.vjp(ln_silu, z, lg, lb)
        return vjp(ds)

    dzc, dln_g, dln_b = _rowcall(f_pw2_lnsilu_bwd, "l1_pw2_ln_silu_bwd", T_LAT, FUSED_TM, [_rin(zc), _rin(dout1)],
                                 [wts["conf_ln_g"], wts["conf_ln_b"], wts["conf_w_pw2"]], [(D, f32)], [(1, D)] * 2)
    grads["conf_ln_g"], grads["conf_ln_b"] = dln_g, dln_b
    dzg, dconv_w, dconv_b = _dwconv_bwd([dzc], zg, 0, wts["conf_conv_w"], 31, 15, ((0, T_LAT),), 128,
                                        "l1_conv_bwd", f32)
    grads["conf_conv_w"], grads["conf_conv_b"] = dconv_w, dconv_b

    def f_glu_pw1_norm_bwd(i, p_, dz, x_, dxs, b1, g_, sc_, sh_, w):
        pf = p_.astype(f32)
        _, vjp = jax.vjp(glu, pf[:, :D], pf[:, D:], b1)
        da, db, db1 = vjp(dz)
        dp = jnp.concatenate([da, db], axis=1).astype(bf16)
        dh = lax.dot_general(dp, w, nt, preferred_element_type=f32)
        _, vjp = jax.vjp(_normmod, x_, g_, sc_, sh_)
        dx, dg, dsc, dsh = vjp(dh)
        return dp, dx + dxs, db1, dg, dsc, dsh

    dpw, dx2, db_pw1, dng10, dsc1_1, dsh1_1 = _rowcall(
        f_glu_pw1_norm_bwd, "l1_glu_pw1_normmod_bwd", T_LAT, FUSED_TM, [_rin(pw), _rin(dzg), _rin(x2), _rin(dx3)],
        [wts["conf_b_pw1"], ng[1, 0][None], sc1[1], sh1[1], wts["conf_w_pw1"]], [(2 * D, bf16), (D, f32)],
        [(1, 2 * D), (1, D), (1, D), (1, D)])
    grads["conf_b_pw1"] = db_pw1
    grads["conf_w_pw1"] = _mm(h2, dpw, "l1_pw1_dw", ta=True, out_dtype=bf16)
    sent = send_grads(["conf_w_pw2", "conf_w_pw1"], grads)

    dx1, dout0, dmo0, dhid0, acc0 = _mlp_backward(dx2, x1, r0, mo0, out0, par0, wts["mlp_w_in"], wts["mlp_w_out"], 0,
                                                  "l0_mlp_bwd", after=[sent])
    grads["mlp_w_in"], grads["mlp_w_out"] = _mlp_weight_grads(h1, dhid0, r0, dmo0, 0, mlp_dw, "l0")
    sent = send_grads(["mlp_w_in", "mlp_w_out"], grads)
    dg1_0, _, dng01, dsc2_0, dsh2_0, dg2_0 = [acc0[k:k + 1] for k in range(6)]

    grads["rec_w_out"] = _mm(zb, dout0, "l0_out_proj_dw", ta=True, out_dtype=bf16, after=[sent])
    sent = send_grads(["rec_w_out"], grads)

    def f_out_gate_bwd(i, gp, y0_, y1_, dout, w):
        lat = jnp.where(i < ctx_tiles, 0.0, 1.0)
        dz = lax.dot_general(dout, w, nt, preferred_element_type=f32)
        _, vjp = jax.vjp(lambda a, b: _gelu(a) * b, gp, y0_ + y1_)
        dgp, dy = vjp(dz)
        return dgp * lat, dy * lat

    dgp, dy = _rowcall(f_out_gate_bwd, "l0_out_proj_gate_bwd", T_ALL, FUSED_TM,
                       [_rin(gr, R, 0), _rin(y0), _rin(y1), _rin(dout0, None, 0, -ctx_tiles)], [wts["rec_w_out"]],
                       [(R, bf16), (R, f32)], after=[sent])
    (dh_f,) = _scan_call(a0, dy, SCAN_FWD_BWD, True, "l0_scan_fwd_bwd", True)
    (dh_r,) = _scan_call(a1, dy, SCAN_REV_BWD, False, "l0_scan_rev_bwd", True)

    dpre, du_f, *dpar_f = _gates_coeff_bwd(ub, u, dh_f, yp0, *gate_args, 0, None)
    dpre, du_r, *dpar_r = _gates_coeff_bwd(ub, u, dh_r, yp1, *gate_args, 1, dpre)
    grads["rec_b_a"], grads["rec_b_x"], grads["rec_lambda"] = [
        jnp.concatenate([f.reshape(-1), r_.reshape(-1)]).reshape(2, R) for f, r_ in zip(dpar_f, dpar_r)]
    grads["gates"] = _gates_dw(ub, dpre)
    sent = send_grads(["replicated"], grads)
    du_gates = _gates_dx(dpre, wts["gates"], after=[sent])
    drec, dconv4_w, dconv4_b = _dwconv_bwd([du_f, du_r, du_gates], gr, R // 256, wts["rec_conv_w"], 4, 1,
                                           CONV_SEGMENTS, 256, "l0_conv_bwd", bf16)
    grads["rec_conv_w"], grads["rec_conv_b"] = dconv4_w, dconv4_b
    dgr = jnp.concatenate([dgp, drec], axis=1)
    grads["rec_w_in"] = _mm(h0, dgr, "l0_in_proj_dw", ta=True, out_dtype=bf16)
    sent = send_grads(["rec_w_in"], grads)

    def f_pre0_bwd(i, x0, dgr_, dxs, g, scp_, shp_, w):
        lat = jnp.where(i < ctx_tiles, 0.0, 1.0)
        dh = lax.dot_general(dgr_, w, nt, preferred_element_type=f32)
        _, vjp = jax.vjp(lambda a, b, c, e: _normmod(a, b, blend(i, c), blend(i, e)), x0, g, scp_, shp_)
        dx, dg, dscp, dshp = vjp(dh)
        return dx + lat * dxs, dg, dscp, dshp

    dx0cat, dng00, dscp, dshp = _rowcall(
        f_pre0_bwd, "l0_in_proj_prenorm_bwd", T_ALL, FUSED_TM,
        [_rin(x0cat), _rin(dgr), _rin(dx1, None, 0, -ctx_tiles)], [ng[0, 0][None], scp, shp, wts["rec_w_in"]],
        [(D, f32)], [(1, D), (2, D), (2, D)], after=[sent])

    grads["norm_g"] = jnp.stack([jnp.concatenate([dng00, dng01], 0), jnp.concatenate([dng10, dng11], 0)])
    dmods = jnp.stack([
        jnp.concatenate([dshp[1:2], dscp[1:2], dg1_0, dsh2_0, dsc2_0, dg2_0], axis=0),
        jnp.concatenate([dsh1_1, dsc1_1, dg1_1, dsh2_1, dsc2_1, dg2_1], axis=0)])
    dcmod = jnp.concatenate([dshp[0:1], dscp[0:1]], axis=0)
    return loss, dx0cat[T_CTX:], dmods, dcmod, grads


def _unshard_cols(g):
    g = jnp.moveaxis(g, 0, -2)
    return g.reshape(g.shape[:-2] + (g.shape[-2] * g.shape[-1],))


def _shard_cols(w):
    w = w.reshape(w.shape[:-1] + (N_DEV, w.shape[-1] // N_DEV))
    return jnp.moveaxis(w, -2, 0)


def _shard_rows(w):
    return w.reshape((N_DEV, w.shape[0] // N_DEV) + w.shape[1:])


SMALL_PACK_ROWS = 64
REPL_FINAL_G_ROWS = -(-D // BLK)
REPL_ROWS = -(-(2 * 2 * N_BLK * BLK + 2 * 2 * N_BLK + REPL_FINAL_G_ROWS) // 16) * 16


def kernel(x, c, ctx, c_ctx, w_ada, b_ada, norm_g, rec_w_in, rec_conv_w, rec_conv_b, rec_lambda, rec_w_a, rec_b_a, rec_w_x, rec_b_x, rec_w_out, conf_w_pw1, conf_b_pw1, conf_conv_w, conf_conv_b, conf_ln_g, conf_ln_b, conf_w_pw2, conf_b_pw2, mlp_w_in, mlp_w_out, final_g, loss_target, m_c_ctx, m_w_ada, m_b_ada, m_norm_g, m_rec_w_in, m_rec_conv_w, m_rec_conv_b, m_rec_lambda, m_rec_w_a, m_rec_b_a, m_rec_w_x, m_rec_b_x, m_rec_w_out, m_conf_w_pw1, m_conf_b_pw1, m_conf_conv_w, m_conf_conv_b, m_conf_ln_g, m_conf_ln_b, m_conf_w_pw2, m_conf_b_pw2, m_mlp_w_in, m_mlp_w_out, m_final_g, v_c_ctx, v_w_ada, v_b_ada, v_norm_g, v_rec_w_in, v_rec_conv_w, v_rec_conv_b, v_rec_lambda, v_rec_w_a, v_rec_b_a, v_rec_w_x, v_rec_b_x, v_rec_w_out, v_conf_w_pw1, v_conf_b_pw1, v_conf_conv_w, v_conf_conv_b, v_conf_ln_g, v_conf_ln_b, v_conf_w_pw2, v_conf_b_pw2, v_mlp_w_in, v_mlp_w_out, v_final_g):
    me = 4 * lax.axis_index("x") + 2 * lax.axis_index("y") + lax.axis_index("c")
    weights = dict(c_ctx=c_ctx, w_ada=w_ada, b_ada=b_ada, norm_g=norm_g, rec_w_in=rec_w_in, rec_conv_w=rec_conv_w,
                   rec_conv_b=rec_conv_b, rec_lambda=rec_lambda, rec_w_a=rec_w_a, rec_b_a=rec_b_a, rec_w_x=rec_w_x,
                   rec_b_x=rec_b_x, rec_w_out=rec_w_out, conf_w_pw1=conf_w_pw1, conf_b_pw1=conf_b_pw1,
                   conf_conv_w=conf_conv_w, conf_conv_b=conf_conv_b, conf_ln_g=conf_ln_g, conf_ln_b=conf_ln_b,
                   conf_w_pw2=conf_w_pw2, conf_b_pw2=conf_b_pw2, mlp_w_in=mlp_w_in, mlp_w_out=mlp_w_out, final_g=final_g)
    m_in = dict(c_ctx=m_c_ctx, w_ada=m_w_ada, b_ada=m_b_ada, norm_g=m_norm_g, rec_w_in=m_rec_w_in, rec_conv_w=m_rec_conv_w,
                rec_conv_b=m_rec_conv_b, rec_lambda=m_rec_lambda, rec_w_a=m_rec_w_a, rec_b_a=m_rec_b_a, rec_w_x=m_rec_w_x,
                rec_b_x=m_rec_b_x, rec_w_out=m_rec_w_out, conf_w_pw1=m_conf_w_pw1, conf_b_pw1=m_conf_b_pw1,
                conf_conv_w=m_conf_conv_w, conf_conv_b=m_conf_conv_b, conf_ln_g=m_conf_ln_g, conf_ln_b=m_conf_ln_b,
                conf_w_pw2=m_conf_w_pw2, conf_b_pw2=m_conf_b_pw2, mlp_w_in=m_mlp_w_in, mlp_w_out=m_mlp_w_out,
                final_g=m_final_g)
    v_in = dict(c_ctx=v_c_ctx, w_ada=v_w_ada, b_ada=v_b_ada, norm_g=v_norm_g, rec_w_in=v_rec_w_in, rec_conv_w=v_rec_conv_w,
                rec_conv_b=v_rec_conv_b, rec_lambda=v_rec_lambda, rec_w_a=v_rec_w_a, rec_b_a=v_rec_b_a, rec_w_x=v_rec_w_x,
                rec_b_x=v_rec_b_x, rec_w_out=v_rec_w_out, conf_w_pw1=v_conf_w_pw1, conf_b_pw1=v_conf_b_pw1,
                conf_conv_w=v_conf_conv_w, conf_conv_b=v_conf_conv_b, conf_ln_g=v_conf_ln_g, conf_ln_b=v_conf_ln_b,
                conf_w_pw2=v_conf_w_pw2, conf_b_pw2=v_conf_b_pw2, mlp_w_in=v_mlp_w_in, mlp_w_out=v_mlp_w_out,
                final_g=v_final_g)
    names = list(weights)

    small_items = [c, norm_g, rec_conv_w, rec_lambda, conf_b_pw1, conf_conv_w, conf_conv_b, conf_ln_g, conf_ln_b,
                   conf_b_pw2]
    flat = jnp.concatenate([a.reshape(-1) for a in small_items])
    flat = jnp.pad(flat, (0, SMALL_PACK_ROWS * 128 - flat.shape[0])).reshape(SMALL_PACK_ROWS, 128)
    as_shard = lambda a: a.astype(bf16).reshape(-1, a.shape[-1])
    early_srcs = [flat, as_shard(rec_w_in[0])]
    early_handle, started = _exchange_start(early_srcs, [_own_block_filled(s, me) for s in early_srcs],
                                            "gather_early_start", False)
    zero = started[0, 0]
    gates = _gate_matrix(rec_w_a[0] + zero, rec_w_x[0] + zero)
    late_items = {"mlp": [rec_w_out[0], mlp_w_in, mlp_w_out], "conf": [conf_w_pw1[0], conf_w_pw2[0]]}
    late_shards = {g: [as_shard(a + zero) for a in items] for g, items in late_items.items()}
    late_lands = {g: [_own_block_filled(s, me) for s in shards] for g, shards in late_shards.items()}
    xcat, poscat = _token_rows(x[0] + zero, ctx[0])
    small_all, early = _exchange_wait(early_handle, [gates, xcat, poscat] + late_lands["mlp"] + late_lands["conf"],
                                      "gather_early_wait", False)

    small_all = small_all.reshape(N_DEV, -1)
    off = 0
    small = []
    for a in small_items:
        small.append(small_all[:, off:off + a.size].reshape((N_DEV,) + a.shape))
        off += a.size
    c_all, ng_all, rcw_all, lam_all, bpw1_all, ccw_all, ccb_all, lng_all, lnb_all, bpw2_all = small
    wts = {
        "norm_g": _unshard_cols(ng_all),
        "rec_conv_w": _unshard_cols(rcw_all)[0],
        "rec_lambda": _unshard_cols(lam_all)[0],
        "conf_b_pw1": _unshard_cols(bpw1_all),
        "conf_conv_w": _unshard_cols(ccw_all)[0],
        "conf_conv_b": _unshard_cols(ccb_all),
        "conf_ln_g": _unshard_cols(lng_all),
        "conf_ln_b": _unshard_cols(lnb_all),
        "conf_b_pw2": _unshard_cols(bpw2_all),
        "rec_conv_b": rec_conv_b,
        "rec_b_a": rec_b_a[0].reshape(2, R),
        "rec_b_x": rec_b_x[0].reshape(2, R),
        "final_g": final_g[None],
        "gates": gates,
    }

    c16 = jnp.concatenate([c_all[:, 0], jnp.broadcast_to(c_ctx[None], (8, D))], axis=0)
    b_loc = lax.dynamic_slice_in_dim(b_ada, me * ADA_SHARD, ADA_SHARD, axis=1)[:, None]
    (mods_gathered,) = _all_gather([_ada_forward(c16, w_ada, b_loc)], "gather_mods")
    mods_all = _unshard_cols(mods_gathered)
    mods = lax.dynamic_index_in_dim(mods_all, me, axis=1, keepdims=False).reshape(2, N_MOD, D)
    cmod = mods_all[0, 8, :2 * D].reshape(2, D)

    late_handles = {}
    late_handles["mlp"], token = _gather2_start(late_shards["mlp"], late_lands["mlp"], "gather_mlp_start",
                                                [early, mods_gathered])
    order = [token]
    wts["rec_w_in"] = _unshard_cols(early + token[0, 0].astype(bf16))

    def late_weights(group, after):
        if group == "mlp_halfway":
            late_handles["mlp"] = _gather2_forward1(late_handles["mlp"], after, "gather_mlp_forward1")
            return late_handles["mlp"][2][0]
        if group == "mlp":
            passed = _gather2_forward2(late_handles["mlp"], after, "gather_mlp_forward2")
            late_handles["conf"], started = _exchange_start(late_shards["conf"], late_lands["conf"], "gather_conf_start",
                                                            False, after=[passed[2][0]])
            got = _gather2_wait(passed, started, "gather_mlp_wait")
        else:
            got = _exchange_wait(late_handles[group], after, "gather_conf_wait", False)
        got = [g.reshape((N_DEV,) + a.shape) for g, a in zip(got, late_items[group])]
        if group == "mlp":
            return {"rec_w_out": got[0].reshape(R, D), "mlp_w_in": got[1], "mlp_w_out": got[2]}
        return {"conf_w_pw1": _unshard_cols(got[0]), "conf_w_pw2": got[1].reshape(D, D)}

    to_blocks = {"rec_w_in": _shard_cols, "conf_w_pw1": _shard_cols, "rec_w_out": _shard_rows, "conf_w_pw2": _shard_rows,
                 "mlp_w_in": lambda g: g, "mlp_w_out": lambda g: g}
    grad_handles = []

    repl_names = ["rec_w_a", "rec_w_x", "rec_b_a", "rec_b_x", "final_g"]

    def send_replicated(grads):
        dwg = grads["gates"]
        repl = {"rec_w_a": jnp.stack([_gate_blocks(dwg, 0), _gate_blocks(dwg, 2)]),
                "rec_w_x": jnp.stack([_gate_blocks(dwg, 1), _gate_blocks(dwg, 3)]),
                "rec_b_a": grads["rec_b_a"], "rec_b_x": grads["rec_b_x"],
                "final_g": jnp.pad(grads["final_g"], ((0, 0), (0, REPL_FINAL_G_ROWS * BLK - D)))}
        flat = jnp.concatenate([repl[n].reshape(-1, BLK) for n in repl_names], axis=0)
        flat = jnp.pad(flat, ((0, REPL_ROWS - flat.shape[0]), (0, 0))).astype(bf16)
        flat = flat.reshape(REPL_ROWS // 8, 8 * BLK)
        handle, sent = _exchange_start([flat], [_own_block_filled(flat, me)], "gather_replicated_start", False)
        grad_handles.append((["replicated"], handle))
        return sent

    def send_grads(group, grads):
        if group == ["replicated"]:
            return send_replicated(grads)
        blocks = [to_blocks[n](grads[n]) for n in group]
        blocks = [g.reshape(N_DEV, -1, g.shape[-1]) for g in blocks]
        lands = [_own_block_filled(lax.dynamic_index_in_dim(g, me, 0, keepdims=False), me) for g in blocks]
        handle, sent = _exchange_start(blocks, lands, "scatter_start_" + group[0], True)
        grad_handles.append((group, handle))
        return sent

    loss, grad_x, dmods, dcmod, grads = _local_step(
        xcat, poscat, loss_target[0], mods, cmod, wts, late_weights, send_grads,
        lambda partial: lax.psum(partial, ("x", "y", "c")), start_after=order)

    def as2d(shape):
        rows = 1
        for s in shape[:-1]:
            rows *= s
        return (rows, shape[-1])

    def whole(arr, shape):
        arr = arr.reshape((-1,) + as2d(shape))
        return (arr, arr.shape[0])

    shard_shapes = {n: weights[n].shape for n in names}
    g_out, d_out, m_out, v_out = {}, {}, {}, {}

    def adamw(n, pieces, after):
        shape = shard_shapes[n]
        r2, c2 = as2d(shape)
        g, dl, nm, nv = _adamw(pieces, weights[n].reshape(r2, c2), m_in[n].reshape(r2, c2), v_in[n].reshape(r2, c2),
                               "adamw_" + n, after=after)
        g_out[n], d_out[n], m_out[n], v_out[n] = (t.reshape(shape) for t in (g, dl, nm, nv))
        return g

    small_sharded = ["norm_g", "rec_conv_w", "rec_lambda", "conf_b_pw1", "conf_conv_w", "conf_conv_b", "conf_ln_g",
                     "conf_ln_b", "conf_b_pw2"]
    pack = jnp.concatenate([_shard_cols(grads[n]).reshape(N_DEV, -1) for n in small_sharded], axis=1)
    pack = jnp.pad(pack, ((0, 0), (0, SMALL_PACK_ROWS * 128 - pack.shape[1]))).reshape(N_DEV, SMALL_PACK_ROWS, 128)
    small_handle, token = _exchange_start(
        [pack], [_own_block_filled(lax.dynamic_index_in_dim(pack, me, 0, keepdims=False), me)], "scatter_small_start",
        True, after=[grad_x])
    dm_flat = jnp.concatenate([dmods.reshape(-1), dcmod.reshape(-1), grads["rec_conv_b"].reshape(-1)])
    dm_len = dm_flat.shape[0]
    dm_flat = jnp.pad(dm_flat, (0, 128 * 128 - dm_len)).reshape(128, 128)
    dm_handle, token = _exchange_start([dm_flat], [_own_block_filled(dm_flat, me)], "gather_dmods_start", False,
                                       after=[token])

    done = token
    for group, handle in grad_handles:
        if group == ["replicated"]:
            repl_all = _exchange_wait(handle, done, "gather_replicated_wait", False)[0]
            repl_all = repl_all.reshape(N_DEV, REPL_ROWS, BLK)
            row = 0
            for n in repl_names:
                n_rows = -(-weights[n].size // BLK)
                if as2d(shard_shapes[n]) == (n_rows, BLK) and row % 256 == 0:
                    done = adamw(n, [(repl_all, N_DEV, row)], [done])
                else:
                    got = repl_all[:, row:row + n_rows].reshape(N_DEV, -1)[:, :weights[n].size]
                    done = adamw(n, [whole(got, shard_shapes[n])], [done])
                row += n_rows
            continue
        for n, got in zip(group, _exchange_wait(handle, done, "scatter_wait_" + group[0], True)):
            done = adamw(n, [(got, N_DEV)], [done])

    dm_all = _exchange_wait(dm_handle, done, "gather_dmods_wait", False)[0].reshape(N_DEV, -1)
    dmods_all = dm_all[:, :2 * N_MOD * D].reshape(N_DEV, 2, N_MOD * D)
    dcmod_all = jnp.pad(dm_all[:, 2 * N_MOD * D:2 * N_MOD * D + 2 * D], ((0, 0), (0, (N_MOD - 2) * D)))
    g16_full = jnp.stack([jnp.concatenate([dmods_all[:, 0], dcmod_all], axis=0),
                          jnp.concatenate([dmods_all[:, 1], jnp.zeros_like(dcmod_all)], axis=0)])
    g16 = lax.dynamic_slice_in_dim(g16_full, me * ADA_SHARD, ADA_SHARD, axis=2)
    dw_ada, ds_part = _ada_backward(c16, g16, w_ada)
    ds_handle, token = _exchange_start([ds_part[0]], [_own_block_filled(ds_part[0], me)], "gather_dsilu_start", False)
    done = adamw("w_ada", [whole(dw_ada, shard_shapes["w_ada"])], [token])
    done = adamw("rec_conv_b", [whole(dm_all[:, dm_len - R:dm_len], shard_shapes["rec_conv_b"])], [done])
    db_terms = jnp.concatenate([dmods_all, jnp.stack([dcmod_all, jnp.zeros_like(dcmod_all)], axis=1)], axis=0)
    done = adamw("b_ada", [whole(db_terms, shard_shapes["b_ada"])], [done])
    pack_recv = _exchange_wait(small_handle, done, "scatter_small_wait", True)[0].reshape(N_DEV, -1)
    off = 0
    for n in small_sharded:
        size = weights[n].size
        done = adamw(n, [whole(pack_recv[:, off:off + size], shard_shapes[n])], [done])
        off += size
    ds_all = _exchange_wait(ds_handle, done, "gather_dsilu_wait", False)[0]
    adamw("c_ctx", [whole(ds_all[:, 0], shard_shapes["c_ctx"])], [])

    return (loss, grad_x[None], *[g_out[n] for n in names], *[d_out[n] for n in names],
            *[m_out[n] for n in names], *[v_out[n] for n in names])
```

```python
import functools

import jax
import jax.numpy as jnp
from jax import lax
from jax.experimental import pallas as pl
from jax.experimental.pallas import tpu as pltpu

f32 = jnp.float32
bf16 = jnp.bfloat16

N_DEV = 8
D = 1024
T_LAT = 2048
T_CTX = 256
T_ALL = T_CTX + T_LAT
R = 1280
N_BLK = 16
BLK = R // N_BLK
F = 4096
GRID_W = 64
RG_C = 8.0
EPS = 1e-6
POS_BASE = 10000.0
N_MOD = 6
ADA_SHARD = N_MOD * D // N_DEV

ADAM_LR = 0.001
ADAM_B1 = 0.9
ADAM_B2 = 0.999
ADAM_EPS = 1e-08
ADAM_WD = 0.01
ADAM_STEP = 10

VMEM_LIMIT_V7X = 56 * 1024 * 1024
HALO = 16
MESH = pl.DeviceIdType.MESH


def _cparams(*sem):
    return pltpu.CompilerParams(dimension_semantics=sem, vmem_limit_bytes=VMEM_LIMIT_V7X)


def _pick(n, cands):
    for c in cands:
        if n % c == 0:
            return c
    raise ValueError(f"no block size for {n}")


def _position():
    x, y, c = lax.axis_index("x"), lax.axis_index("y"), lax.axis_index("c")
    return x, y, c, 4 * x + 2 * y + c


def _peer(x, y, c, k):
    px = (1 - x) if (k >> 2) & 1 else x
    py = (1 - y) if (k >> 1) & 1 else y
    pc = (1 - c) if k & 1 else c
    return (px, py, pc), 4 * px + 2 * py + pc


def _exchange(arrs, name, scatter):
    n = len(arrs)

    def body(*refs):
        ins, outs = refs[:n], refs[n:2 * n]
        send_sems, recv_sems, local_sems = refs[2 * n:]
        x, y, c, me = _position()
        local = []
        for a in range(n):
            src = ins[a].at[me] if scatter else ins[a]
            cp = pltpu.make_async_copy(src, outs[a].at[me], local_sems.at[a])
            cp.start()
            local.append(cp)
        sends, recvs = [], []
        for a in range(n):
            for k in range(1, N_DEV):
                peer, peer_lin = _peer(x, y, c, k)
                src = ins[a].at[peer_lin] if scatter else ins[a]
                cp = pltpu.make_async_remote_copy(
                    src_ref=src, dst_ref=outs[a].at[me], send_sem=send_sems.at[a, k - 1],
                    recv_sem=recv_sems.at[a, k - 1], device_id=peer, device_id_type=MESH)
                cp.start()
                sends.append(cp)
                recvs.append(pltpu.make_async_remote_copy(
                    src_ref=src, dst_ref=outs[a].at[peer_lin], send_sem=send_sems.at[a, k - 1],
                    recv_sem=recv_sems.at[a, k - 1], device_id=peer, device_id_type=MESH))
        for cp in recvs:
            cp.wait_recv()
        for cp in sends:
            cp.wait_send()
        for cp in local:
            cp.wait()

    if scatter:
        out_shape = [jax.ShapeDtypeStruct(a.shape, a.dtype) for a in arrs]
    else:
        out_shape = [jax.ShapeDtypeStruct((N_DEV,) + a.shape, a.dtype) for a in arrs]
    any_spec = pl.BlockSpec(memory_space=pl.ANY)
    return pl.pallas_call(
        body, name=name, out_shape=out_shape,
        in_specs=[any_spec] * n, out_specs=[any_spec] * n,
        scratch_shapes=[pltpu.SemaphoreType.DMA((n, N_DEV - 1)), pltpu.SemaphoreType.DMA((n, N_DEV - 1)),
                        pltpu.SemaphoreType.DMA((n,))],
    )(*arrs)


def _all_gather(arrs, name):
    return _exchange(arrs, name, scatter=False)


def _lin(p):
    return 4 * p[0] + 2 * p[1] + p[2]


HBM_SPEC = pl.BlockSpec(memory_space=pltpu.HBM)
SEM_SPEC = pl.BlockSpec(memory_space=pltpu.SEMAPHORE)
DATAFLOW_EFFECT = pltpu.SideEffectType.DATAFLOW_SIDE_EFFECTING


def _split_copies(srcs, lands, send_sems, recv_sems, scatter):
    x, y, c, me = _position()
    out = []
    for a in range(len(srcs)):
        for k in range(1, N_DEV):
            peer, peer_lin = _peer(x, y, c, k)
            src = srcs[a].at[peer_lin] if scatter else srcs[a]
            mk = lambda slot: pltpu.make_async_remote_copy(
                src_ref=src, dst_ref=lands[a].at[slot], send_sem=send_sems.at[a * (N_DEV - 1) + k - 1],
                recv_sem=recv_sems.at[a * (N_DEV - 1) + k - 1], device_id=peer, device_id_type=MESH)
            out.append((mk(me), mk(peer_lin)))
    return out


def _exchange_start(srcs, lands, name, scatter, after=()):
    n = len(srcs)
    n_after = len(after)

    def body(*refs):
        srcs_r, lands_r = refs[:n], refs[n:2 * n]
        send_sems, recv_sems = refs[2 * n + n_after], refs[2 * n + n_after + 1]
        token = refs[-1]
        for outgoing, _ in _split_copies(srcs_r, lands_r, send_sems, recv_sems, scatter):
            outgoing.start()
        token[...] = jnp.zeros_like(token)

    hbm = lambda a: pltpu.HBM(a.shape, a.dtype)
    res = pl.pallas_call(
        body, name=name,
        out_shape=(pltpu.SemaphoreType.DMA((n * (N_DEV - 1),)), pltpu.SemaphoreType.DMA((n * (N_DEV - 1),)),
                   *[hbm(a) for a in srcs], *[hbm(a) for a in lands], jax.ShapeDtypeStruct((8, 128), f32)),
        in_specs=[HBM_SPEC] * (2 * n) + [pl.BlockSpec(memory_space=pl.ANY)] * n_after,
        out_specs=(SEM_SPEC, SEM_SPEC, *[HBM_SPEC] * (2 * n), pl.BlockSpec(memory_space=pltpu.VMEM)),
        input_output_aliases={i: 2 + i for i in range(2 * n)},
        compiler_params=pltpu.CompilerParams(has_side_effects=DATAFLOW_EFFECT),
    )(*[pltpu.with_memory_space_constraint(a, pltpu.HBM) for a in list(srcs) + list(lands)], *after)
    return (res[0], res[1], list(res[2:2 + n]), list(res[2 + n:2 + 2 * n])), res[-1]


def _exchange_wait(handle, after, name, scatter):
    send_sems, recv_sems, srcs, lands = handle
    n = len(srcs)
    after = list(after) if isinstance(after, (list, tuple)) else [after]

    def body(*refs):
        srcs_r, lands_r = refs[:n], refs[n:2 * n]
        send_s, recv_s = refs[2 * n], refs[2 * n + 1]
        for outgoing, incoming in _split_copies(srcs_r, lands_r, send_s, recv_s, scatter):
            outgoing.wait_send()
            incoming.wait_recv()

    hbm = lambda a: pltpu.HBM(a.shape, a.dtype)
    res = pl.pallas_call(
        body, name=name, out_shape=tuple(hbm(a) for a in list(srcs) + list(lands)),
        in_specs=[HBM_SPEC] * (2 * n) + [SEM_SPEC, SEM_SPEC] + [pl.BlockSpec(memory_space=pl.ANY)] * len(after),
        out_specs=tuple([HBM_SPEC] * (2 * n)),
        input_output_aliases={i: i for i in range(2 * n)},
        compiler_params=pltpu.CompilerParams(has_side_effects=DATAFLOW_EFFECT),
    )(*srcs, *lands, send_sems, recv_sems, *after)
    return list(res[n:])


def _split_call(body, name, hbm_ins, kept, in_sems, n_new_sems, after, with_token):
    n_in, n_sem = len(hbm_ins), len(in_sems)
    out_shape, out_specs = [], []
    if n_new_sems:
        out_shape += [pltpu.SemaphoreType.DMA((n_new_sems,))] * 2
        out_specs += [SEM_SPEC] * 2
    first_kept = len(out_shape)
    out_shape += [pltpu.HBM(hbm_ins[i].shape, hbm_ins[i].dtype) for i in kept]
    out_specs += [HBM_SPEC] * len(kept)
    if with_token:
        out_shape.append(jax.ShapeDtypeStruct((8, 128), f32))
        out_specs.append(pl.BlockSpec(memory_space=pltpu.VMEM))

    def wrapped(*refs):
        outs = refs[n_in + n_sem + len(after):]
        body(refs[:n_in], refs[n_in:n_in + n_sem], outs[:2] if n_new_sems else ())
        if with_token:
            outs[-1][...] = jnp.zeros_like(outs[-1])

    return pl.pallas_call(
        wrapped, name=name, out_shape=tuple(out_shape),
        in_specs=[HBM_SPEC] * n_in + [SEM_SPEC] * n_sem + [pl.BlockSpec(memory_space=pl.ANY)] * len(after),
        out_specs=tuple(out_specs), input_output_aliases={i: first_kept + j for j, i in enumerate(kept)},
        compiler_params=pltpu.CompilerParams(has_side_effects=DATAFLOW_EFFECT),
    )(*[pltpu.with_memory_space_constraint(a, pltpu.HBM) for a in hbm_ins], *in_sems, *after)


def _rcopy(src, dst, sems, k, to):
    return pltpu.make_async_remote_copy(src_ref=src, dst_ref=dst, send_sem=sems[0].at[k], recv_sem=sems[1].at[k],
                                        device_id=to, device_id_type=MESH)


def _gather2_start(shards, lands, name, after):
    n = len(shards)

    def body(ins, sems_in, sems_out):
        x, y, c, me = _position()
        for a in range(n):
            for k, to in enumerate(((x, y, 1 - c), (1 - x, y, c), (x, 1 - y, c))):
                _rcopy(ins[a], ins[n + a].at[me], sems_out, 3 * a + k, to).start()

    res = _split_call(body, name, list(shards) + list(lands), range(2 * n), (), 3 * n, after, True)
    return (res[0], res[1], list(res[2:2 + n]), list(res[2 + n:2 + 2 * n])), res[-1]


def _gather2_forward1(handle, after, name):
    send_sems, recv_sems, srcs, lands = handle
    n = len(srcs)

    def body(ins, sems_in, sems_out):
        x, y, c, me = _position()
        sib, xn, yn = (x, y, 1 - c), (1 - x, y, c), (x, 1 - y, c)
        for a in range(n):
            for k, peer in enumerate((sib, xn, yn)):
                _rcopy(ins[a], ins[n + a].at[me], sems_in, 3 * a + k, peer).wait_send()
                _rcopy(ins[a], ins[n + a].at[_lin(peer)], sems_in, 3 * a + k, peer).wait_recv()
        for a in range(n):
            land = ins[n + a]
            _rcopy(land.at[_lin(xn)], land.at[_lin(xn)], sems_out, 3 * a, sib).start()
            _rcopy(land.at[_lin(yn)], land.at[_lin(yn)], sems_out, 3 * a + 1, sib).start()

            @pl.when(c == 0)
            def _():
                _rcopy(land.at[_lin(xn)], land.at[_lin(xn)], sems_out, 3 * a + 2, yn).start()

            @pl.when(c == 1)
            def _():
                _rcopy(land.at[_lin(yn)], land.at[_lin(yn)], sems_out, 3 * a + 2, xn).start()

    res = _split_call(body, name, list(srcs) + list(lands), range(n, 2 * n), (send_sems, recv_sems), 3 * n, [after], False)
    return (res[0], res[1], list(res[2:]))


def _gather2_forward2(handle, after, name):
    send_sems, recv_sems, lands = handle
    n = len(lands)

    def body(ins, sems_in, sems_out):
        x, y, c, me = _position()
        sib, dg = (x, y, 1 - c), _lin((1 - x, 1 - y, c))
        for a in range(n):
            for k, slot in enumerate((_lin((1 - x, y, 1 - c)), _lin((x, 1 - y, 1 - c)), dg)):
                done = _rcopy(ins[a].at[slot], ins[a].at[slot], sems_in, 3 * a + k, sib)
                done.wait_send()
                done.wait_recv()
        for a in range(n):
            _rcopy(ins[a].at[dg], ins[a].at[dg], sems_out, a, sib).start()

    res = _split_call(body, name, list(lands), range(n), (send_sems, recv_sems), n, [after], False)
    return (res[0], res[1], list(res[2:]))


def _gather2_wait(handle, after, name):
    send_sems, recv_sems, lands = handle
    n = len(lands)

    def body(ins, sems_in, sems_out):
        x, y, c, me = _position()
        slot = _lin((1 - x, 1 - y, 1 - c))
        for a in range(n):
            done = _rcopy(ins[a].at[slot], ins[a].at[slot], sems_in, a, (x, y, 1 - c))
            done.wait_send()
            done.wait_recv()

    return list(_split_call(body, name, list(lands), range(n), (send_sems, recv_sems), 0, [after], False))


def _own_block_filled(block, me):
    land = lax.empty((N_DEV,) + block.shape, block.dtype)
    return lax.dynamic_update_index_in_dim(land, block, me, 0)


ANY_SPEC = pl.BlockSpec(memory_space=pl.ANY)


def _mm(a, b, name, ta=False, tb=False, out_dtype=f32, after=()):
    if ta:
        k_dim, m_dim = a.shape
    else:
        m_dim, k_dim = a.shape
    if tb:
        n_dim, k2 = b.shape
    else:
        k2, n_dim = b.shape
    assert k_dim == k2, (a.shape, b.shape)
    assert a.dtype == bf16 and b.dtype == bf16
    bm = _pick(m_dim, (512, 768, 640, 256, 128))
    bn = _pick(n_dim, (512, 640, 256, 128))
    bk = k_dim if k_dim <= 2560 else _pick(k_dim, (1024, 1280, 768, 512))
    nk = k_dim // bk
    a_spec = (pl.BlockSpec((bk, bm), lambda i, j, k: (k, i)) if ta
              else pl.BlockSpec((bm, bk), lambda i, j, k: (i, k)))
    b_spec = (pl.BlockSpec((bn, bk), lambda i, j, k: (j, k)) if tb
              else pl.BlockSpec((bk, bn), lambda i, j, k: (k, j)))
    dims = (((0 if ta else 1,), (1 if tb else 0,)), ((), ()))

    n_after = len(after)

    def body_single(a_ref, b_ref, *rest):
        o_ref = rest[n_after]
        o_ref[...] = lax.dot_general(a_ref[...], b_ref[...], dims, preferred_element_type=f32).astype(o_ref.dtype)

    def body(a_ref, b_ref, *rest):
        o_ref, acc_ref = rest[n_after:]
        k = pl.program_id(2)

        @pl.when(k == 0)
        def _():
            acc_ref[...] = jnp.zeros_like(acc_ref)

        acc_ref[...] += lax.dot_general(a_ref[...], b_ref[...], dims, preferred_element_type=f32)

        @pl.when(k == nk - 1)
        def _():
            o_ref[...] = acc_ref[...].astype(o_ref.dtype)

    return pl.pallas_call(
        body_single if nk == 1 else body, name=name, out_shape=jax.ShapeDtypeStruct((m_dim, n_dim), out_dtype),
        grid=(m_dim // bm, n_dim // bn, nk), in_specs=[a_spec, b_spec] + [ANY_SPEC] * n_after,
        out_specs=pl.BlockSpec((bm, bn), lambda i, j, k: (i, j)),
        scratch_shapes=[] if nk == 1 else [pltpu.VMEM((bm, bn), f32)],
        compiler_params=_cparams("parallel", "parallel", "arbitrary"),
    )(a, b, *after)


def _rin(arr, width=None, cb=0, roff=0):
    return (arr, arr.shape[1] if width is None else width, cb, roff)


def _rowcall(fn, name, rows, tm, row_ins, par_ins, row_outs, acc_outs=(), after=()):
    nr, npar, nro, n_after = len(row_ins), len(par_ins), len(row_outs), len(after)
    in_specs, args = [], []
    for arr, width, cb, roff in row_ins:
        if roff >= 0:
            imap = lambda i, cb=cb, roff=roff: (i + roff, cb)
        else:
            imap = lambda i, cb=cb, roff=roff: (jnp.maximum(i + roff, 0), cb)
        in_specs.append(pl.BlockSpec((tm, width), imap))
        args.append(arr)
    for p in par_ins:
        in_specs.append(pl.BlockSpec(p.shape, lambda i: (0, 0)))
        args.append(p)
    out_shape, out_specs = [], []
    for width, dt in row_outs:
        out_shape.append(jax.ShapeDtypeStruct((rows, width), dt))
        out_specs.append(pl.BlockSpec((tm, width), lambda i: (i, 0)))
    for p, width in acc_outs:
        out_shape.append(jax.ShapeDtypeStruct((p, width), f32))
        out_specs.append(pl.BlockSpec((p, width), lambda i: (0, 0)))

    def body(*refs):
        i = pl.program_id(0)
        res = fn(i, *[r[...] for r in refs[:nr + npar]])
        outs = refs[nr + npar + n_after:]
        for o, v in zip(outs[:nro], res[:nro]):
            o[...] = v.astype(o.dtype)
        if acc_outs:
            @pl.when(i == 0)
            def _():
                for o in outs[nro:]:
                    o[...] = jnp.zeros_like(o)

            for o, v in zip(outs[nro:], res[nro:]):
                o[...] += v

    return pl.pallas_call(
        body, name=name, out_shape=out_shape, grid=(rows // tm,), in_specs=in_specs + [ANY_SPEC] * n_after,
        out_specs=out_specs, compiler_params=_cparams("arbitrary"),
    )(*args, *after)


def _rms(x, g):
    return x * lax.rsqrt(jnp.mean(x * x, axis=-1, keepdims=True) + EPS) * g


def _normmod(x, g, sc, sh):
    return _rms(x, g) * (1.0 + sc) + sh


def _gelu(x):
    return 0.5 * x * (1.0 + jnp.tanh(0.7978845608028654 * (x + 0.044715 * (x * x * x))))


def _sigmoid(x):
    return 0.5 * (jnp.tanh(0.5 * x) + 1.0)


def _coeff_parts(pre_a, pre_x, ba, bx, lam):
    r = _sigmoid(pre_a + ba)
    ig = _sigmoid(pre_x + bx)
    nl = -lam
    sp = jnp.maximum(nl, 0.0) + jnp.log(1.0 + jnp.exp(-jnp.abs(nl)))
    la = -RG_C * r * sp
    a = jnp.exp(la)
    one_minus_a2 = -jnp.tanh(la) * (a * a + 1.0)
    inv_m = lax.rsqrt(one_minus_a2)
    return r, ig, sp, a, one_minus_a2 * inv_m, inv_m


def _coeff(pre_a, pre_x, u, ba, bx, lam):
    _, ig, _, a, m, _ = _coeff_parts(pre_a, pre_x, ba, bx, lam)
    return a, m * (ig * u)


def _coeff_bwd(pre_a, pre_x, u, ba, bx, lam, da, db):
    r, ig, sp, a, m, inv_m = _coeff_parts(pre_a, pre_x, ba, bx, lam)
    dbu = db * u
    dig = dbu * m
    dm = dbu * ig
    dla = a * (da - dm * a * inv_m)
    dpa = dla * (-RG_C * sp) * (r * (1.0 - r))
    dpx = dig * (ig * (1.0 - ig))
    dsp = jnp.sum(dla * (-RG_C * r), axis=0, keepdims=True)
    dlam = -dsp * _sigmoid(-lam)
    return (dpa, dpx, db * m * ig, jnp.sum(dpa, axis=0, keepdims=True), jnp.sum(dpx, axis=0, keepdims=True), dlam)


SCAN_CHUNK = 256


def _scan_call(a, v, chunk_of, reverse, name, backward, after=()):
    rows, width = a.shape
    n_out = 1 if backward else 2
    nt = SCAN_CHUNK // 8

    def body(a_ref, v_ref, *rest):
        outs, state_ref = rest[len(after):-1], rest[-1]

        @pl.when(pl.program_id(0) == 0)
        def _():
            state_ref[...] = jnp.zeros_like(state_ref)

        rid = lax.broadcasted_iota(jnp.int32, (8, width), 0)
        last_row = 0 if reverse else 7

        def shift(x, s, fill):
            rolled = pltpu.roll(x, (8 - s) if reverse else s, axis=0)
            return jnp.where((rid >= 8 - s) if reverse else (rid < s), fill, rolled)

        def tile(j, st):
            t0 = pl.multiple_of((nt - 1 - j if reverse else j) * 8, 8)
            at = a_ref[pl.ds(t0, 8), :]
            coef = shift(at, 1, 1.0) if backward else at
            acc = v_ref[pl.ds(t0, 8), :]
            for s in (1, 2, 4):
                acc = coef * shift(acc, s, 0.0) + acc
                coef = coef * shift(coef, s, 1.0)
            out = coef * st + acc
            outs[0][pl.ds(t0, 8), :] = out
            last = out[last_row:last_row + 1]
            if backward:
                return at[last_row:last_row + 1] * last
            outs[1][pl.ds(t0, 8), :] = shift(out, 1, st)
            return last

        state_ref[0:1, :] = lax.fori_loop(0, nt, tile, state_ref[0:1, :])

    spec = pl.BlockSpec((SCAN_CHUNK, width), lambda t: (chunk_of(t), 0))
    return pl.pallas_call(
        body, name=name, out_shape=[jax.ShapeDtypeStruct((rows, width), f32)] * n_out,
        grid=(rows // SCAN_CHUNK,), in_specs=[spec, spec] + [ANY_SPEC] * len(after), out_specs=[spec] * n_out,
        scratch_shapes=[pltpu.VMEM((8, width), f32)],
        compiler_params=_cparams("arbitrary"),
    )(a, v, *after)


CONV_CHUNK = 256


def _fill_padded(pad_ref, src_ref, start, n):
    cb = pad_ref.shape[1]
    pad_ref[pl.ds(0, HALO), :] = jnp.zeros((HALO, cb), f32)
    pad_ref[pl.ds(HALO, n), :] = src_ref[pl.ds(start, n), :].astype(f32)
    pad_ref[pl.ds(HALO + n, HALO), :] = jnp.zeros((HALO, cb), f32)


def _dwconv_fwd(x, x_cb0, w, b, taps, pad_left, segments, cb, name, emit_bf16):
    rows = x.shape[0]
    width = w.shape[1]

    def body(x_ref, w_ref, b_ref, *rest):
        outs, xp = rest[:-1], rest[-1]
        for start, n in segments:
            _fill_padded(xp, x_ref, start, n)
            for c0 in range(0, n, CONV_CHUNK):
                acc = jnp.zeros((CONV_CHUNK, cb), f32) + b_ref[...]
                for k in range(taps):
                    acc = acc + w_ref[k:k + 1, :] * xp[pl.ds(HALO + c0 + k - pad_left, CONV_CHUNK), :]
                for o in outs:
                    o[pl.ds(start + c0, CONV_CHUNK), :] = acc.astype(o.dtype)

    out_dtypes = [f32, bf16] if emit_bf16 else [f32]
    return pl.pallas_call(
        body, name=name, out_shape=[jax.ShapeDtypeStruct((rows, width), dt) for dt in out_dtypes],
        grid=(width // cb,),
        in_specs=[pl.BlockSpec((rows, cb), lambda j: (0, j + x_cb0)), pl.BlockSpec((taps, cb), lambda j: (0, j)),
                  pl.BlockSpec((1, cb), lambda j: (0, j))],
        out_specs=[pl.BlockSpec((rows, cb), lambda j: (0, j))] * len(out_dtypes),
        scratch_shapes=[pltpu.VMEM((rows + 2 * HALO, cb), f32)],
        compiler_params=_cparams("parallel"),
    )(x, w, b)


def _dwconv_bwd(douts, x, x_cb0, w, taps, pad_left, segments, cb, name, dx_dtype):
    rows = x.shape[0]
    width = w.shape[1]
    nd = len(douts)

    def body(*refs):
        d_refs, x_ref, w_ref = refs[:nd], refs[nd], refs[nd + 1]
        dx_ref, dw_ref, db_ref, dp, dsum = refs[nd + 2:]
        dw_ref[...] = jnp.zeros_like(dw_ref)
        db_ref[...] = jnp.zeros_like(db_ref)
        if nd > 1:
            total = d_refs[0][...]
            for r in d_refs[1:]:
                total = total + r[...]
            dsum[...] = total
            d_ref = dsum
        else:
            d_ref = d_refs[0]
        for start, n in segments:
            _fill_padded(dp, d_ref, start, n)
            for c0 in range(0, n, CONV_CHUNK):
                db_ref[...] += jnp.sum(dp[pl.ds(HALO + c0, CONV_CHUNK), :], axis=0, keepdims=True)
                xchunk = x_ref[pl.ds(start + c0, CONV_CHUNK), :].astype(f32)
                acc = jnp.zeros((CONV_CHUNK, cb), f32)
                for k in range(taps):
                    shifted = dp[pl.ds(HALO + c0 + pad_left - k, CONV_CHUNK), :]
                    acc = acc + w_ref[k:k + 1, :] * shifted
                    dw_ref[k:k + 1, :] += jnp.sum(shifted * xchunk, axis=0, keepdims=True)
                dx_ref[pl.ds(start + c0, CONV_CHUNK), :] = acc.astype(dx_ref.dtype)

    dspec = pl.BlockSpec((rows, cb), lambda j: (0, j))
    return pl.pallas_call(
        body, name=name,
        out_shape=[jax.ShapeDtypeStruct((rows, width), dx_dtype), jax.ShapeDtypeStruct((taps, width), f32),
                   jax.ShapeDtypeStruct((1, width), f32)],
        grid=(width // cb,),
        in_specs=[dspec] * nd + [pl.BlockSpec((rows, cb), lambda j: (0, j + x_cb0)),
                                 pl.BlockSpec((taps, cb), lambda j: (0, j))],
        out_specs=[dspec, pl.BlockSpec((taps, cb), lambda j: (0, j)), pl.BlockSpec((1, cb), lambda j: (0, j))],
        scratch_shapes=[pltpu.VMEM((rows + 2 * HALO, cb), f32), pltpu.VMEM((rows, cb), f32)],
        compiler_params=_cparams("parallel"),
    )(*douts, x, w)


def _ada_forward(c16, w_ada, b_loc):
    def body(c_ref, w_ref, b_ref, o_ref):
        cv = c_ref[...]
        s = (cv * _sigmoid(cv)).astype(bf16)
        o_ref[0] = jnp.dot(s, w_ref[0].astype(bf16), preferred_element_type=f32) + b_ref[0]

    return pl.pallas_call(
        body, name="ada_forward", out_shape=jax.ShapeDtypeStruct((2, 16, ADA_SHARD), f32), grid=(2,),
        in_specs=[pl.BlockSpec((16, D), lambda l: (0, 0)), pl.BlockSpec((1, D, ADA_SHARD), lambda l: (l, 0, 0)),
                  pl.BlockSpec((1, 1, ADA_SHARD), lambda l: (l, 0, 0))],
        out_specs=pl.BlockSpec((1, 16, ADA_SHARD), lambda l: (l, 0, 0)),
        compiler_params=_cparams("parallel"),
    )(c16, w_ada, b_loc)


def _ada_backward(c16, g16, w_ada):
    def body(c_ref, g_ref, w_ref, dw_ref, ds_ref):
        cv = c_ref[...]
        s = (cv * _sigmoid(cv)).astype(bf16)
        g = g_ref[0].astype(bf16)
        dw_ref[0] = lax.dot_general(s, g, (((0,), (0,)), ((), ())), preferred_element_type=f32)
        ds = lax.dot_general(g, w_ref[0].astype(bf16), (((1,), (1,)), ((), ())), preferred_element_type=f32)
        cc = cv[8:9]
        sg = _sigmoid(cc)
        dsilu = sg * (1.0 + cc * (1.0 - sg))
        ds_ref[0] = jnp.zeros((8, D), f32) + jnp.sum(ds[8:16], axis=0, keepdims=True) * dsilu

    return pl.pallas_call(
        body, name="ada_backward",
        out_shape=[jax.ShapeDtypeStruct((2, D, ADA_SHARD), f32), jax.ShapeDtypeStruct((2, 8, D), f32)], grid=(2,),
        in_specs=[pl.BlockSpec((16, D), lambda l: (0, 0)), pl.BlockSpec((1, 16, ADA_SHARD), lambda l: (l, 0, 0)),
                  pl.BlockSpec((1, D, ADA_SHARD), lambda l: (l, 0, 0))],
        out_specs=[pl.BlockSpec((1, D, ADA_SHARD), lambda l: (l, 0, 0)), pl.BlockSpec((1, 8, D), lambda l: (l, 0, 0))],
        compiler_params=_cparams("parallel"),
    )(c16, g16, w_ada)


def _adamw(pieces, w, m, v, name, after=()):
    rows, cols = w.shape
    n_arr, n_after = len(pieces), len(after)
    tm = 256 if (rows % 256 == 0 and rows > 256) else rows
    counts = [p[1] for p in pieces]
    first_tiles = [(p[2] if len(p) > 2 else 0) // tm for p in pieces]
    pieces = [p[0] for p in pieces]

    def body(*refs):
        p_refs = refs[:n_arr]
        w_ref, m_ref, v_ref = refs[n_arr:n_arr + 3]
        g_ref, d_ref, nm_ref, nv_ref = refs[n_arr + 3 + n_after:]
        g = None
        for p_ref in p_refs:
            for j in range(p_ref.shape[0]):
                term = p_ref[j].astype(f32)
                g = term if g is None else g + term
        m2 = ADAM_B1 * m_ref[...] + (1.0 - ADAM_B1) * g
        v2 = ADAM_B2 * v_ref[...] + (1.0 - ADAM_B2) * (g * g)
        m_hat = m2 / (1.0 - ADAM_B1 ** ADAM_STEP)
        v_hat = v2 / (1.0 - ADAM_B2 ** ADAM_STEP)
        g_ref[...] = g
        d_ref[...] = -ADAM_LR * (m_hat / (jnp.sqrt(v_hat) + ADAM_EPS) + ADAM_WD * w_ref[...])
        nm_ref[...] = m2
        nv_ref[...] = v2

    spec = pl.BlockSpec((tm, cols), lambda i: (i, 0))
    return pl.pallas_call(
        body, name=name, out_shape=[jax.ShapeDtypeStruct((rows, cols), f32)] * 4, grid=(rows // tm,),
        in_specs=[pl.BlockSpec((cnt, tm, cols), lambda i, t=t: (0, i + t, 0)) for cnt, t in zip(counts, first_tiles)]
        + [spec, spec, spec]
        + [ANY_SPEC] * n_after,
        out_specs=[spec] * 4, compiler_params=_cparams("parallel"),
    )(*pieces, w, m, v, *after)


MLP_TM = 256
FB = F // N_DEV


def _stack_rows(vals, n):
    cols = vals[0].shape[1]
    rid = lax.broadcasted_iota(jnp.int32, (n, cols), 0)
    out = jnp.zeros((n, cols), f32)
    for k, v in enumerate(vals):
        out = jnp.where(rid == k, v, out)
    return out


N_MLP_PARAMS = 9


class _ParamRows:
    def __init__(self, ref):
        self.ref = ref

    def __getitem__(self, sl):
        return self.ref[8 * sl.start:8 * sl.start + 1, :]


def _resident(shape, imap):
    return pl.BlockSpec(shape, imap, pipeline_mode=pl.Buffered(1))


def _mlp_forward(xa, xa_roff, out_prev, par, w_in, w_out, layer, name):
    def body(xa_ref, op_ref, par_ref, win_ref, wout_ref, x1_ref, h_ref, r_ref, mo_ref, x2_ref, hn_ref):
        p = _ParamRows(par_ref)
        x1 = xa_ref[...] + p[0:1] * (op_ref[...] + p[1:2])
        h = _normmod(x1, p[2:3], p[3:4], p[4:5]).astype(bf16)
        x1_ref[...] = x1
        h_ref[...] = h
        mo = jnp.zeros((MLP_TM, D), f32)
        for j in range(N_DEV):
            r = jnp.maximum(jnp.dot(h, win_ref[j], preferred_element_type=f32), 0.0)
            r_ref[:, j * FB:(j + 1) * FB] = r.astype(bf16)
            mo = mo + jnp.dot((r * r).astype(bf16), wout_ref[j], preferred_element_type=f32)
        mo_ref[...] = mo.astype(bf16)
        x2 = x1 + p[5:6] * mo
        x2_ref[...] = x2
        hn_ref[...] = _normmod(x2, p[6:7], p[7:8], p[8:9]).astype(bf16)

    row = lambda width: pl.BlockSpec((MLP_TM, width), lambda i: (i, 0))
    return pl.pallas_call(
        body, name=name, grid=(T_LAT // MLP_TM,),
        out_shape=[jax.ShapeDtypeStruct((T_LAT, D), f32), jax.ShapeDtypeStruct((T_LAT, D), bf16),
                   jax.ShapeDtypeStruct((T_LAT, F), bf16), jax.ShapeDtypeStruct((T_LAT, D), bf16),
                   jax.ShapeDtypeStruct((T_LAT, D), f32), jax.ShapeDtypeStruct((T_LAT, D), bf16)],
        in_specs=[pl.BlockSpec((MLP_TM, D), lambda i: (i + xa_roff, 0)), row(D), pl.BlockSpec((8 * N_MLP_PARAMS, D), lambda i: (0, 0)),
                  _resident((N_DEV, None, D, FB), lambda i: (0, layer, 0, 0)),
                  _resident((N_DEV, None, FB, D), lambda i: (0, layer, 0, 0))],
        out_specs=[row(D), row(D), row(F), row(D), row(D), row(D)],
        compiler_params=_cparams("parallel"),
    )(xa, out_prev, par, w_in, w_out)


def _mlp_backward(dx2, x1, r, mo, out_prev, par, w_in, w_out, layer, name, after=()):
    nt = (((1,), (1,)), ((), ()))

    n_after = len(after)

    def body(dx2_ref, x1_ref, r_ref, mo_ref, op_ref, par_ref, win_ref, wout_ref, *rest):
        dx1_ref, dop_ref, dmo_ref, dhid_ref, acc_ref = rest[n_after:]
        p = _ParamRows(par_ref)
        dx2v = dx2_ref[...]
        dmo = (p[5:6] * dx2v).astype(bf16)
        dmo_ref[...] = dmo
        dh = jnp.zeros((MLP_TM, D), f32)
        mo = mo_ref[...].astype(f32)
        for j in range(N_DEV):
            rf = r_ref[:, j * FB:(j + 1) * FB].astype(f32)
            dact = lax.dot_general(dmo, wout_ref[j], nt, preferred_element_type=f32)
            dhid = (dact * (2.0 * rf)).astype(bf16)
            dhid_ref[:, j * FB:(j + 1) * FB] = dhid
            dh = dh + lax.dot_general(dhid, win_ref[j], nt, preferred_element_type=f32)
        x1 = x1_ref[...]
        _, vjp = jax.vjp(_normmod, x1, p[2:3], p[3:4], p[4:5])
        dx, dng, dsc, dsh = vjp(dh)
        dx1 = dx2v + dx
        dx1_ref[...] = dx1
        dop_ref[...] = (p[0:1] * dx1).astype(bf16)
        sums = _stack_rows([jnp.sum(dx1 * (op_ref[...] + p[1:2]), axis=0, keepdims=True),
                            p[0:1] * jnp.sum(dx1, axis=0, keepdims=True), dng, dsc, dsh,
                            jnp.sum(dx2v * mo, axis=0, keepdims=True)], 8)

        @pl.when(pl.program_id(0) == 0)
        def _():
            acc_ref[...] = jnp.zeros_like(acc_ref)

        acc_ref[...] += sums

    row = lambda width: pl.BlockSpec((MLP_TM, width), lambda i: (i, 0))
    return pl.pallas_call(
        body, name=name, grid=(T_LAT // MLP_TM,),
        out_shape=[jax.ShapeDtypeStruct((T_LAT, D), f32), jax.ShapeDtypeStruct((T_LAT, D), bf16),
                   jax.ShapeDtypeStruct((T_LAT, D), bf16), jax.ShapeDtypeStruct((T_LAT, F), bf16),
                   jax.ShapeDtypeStruct((8, D), f32)],
        in_specs=[row(D), row(D), row(F), row(D), row(D), pl.BlockSpec((8 * N_MLP_PARAMS, D), lambda i: (0, 0)),
                  _resident((N_DEV, None, D, FB), lambda i: (0, layer, 0, 0)),
                  _resident((N_DEV, None, FB, D), lambda i: (0, layer, 0, 0))] + [ANY_SPEC] * n_after,
        out_specs=[row(D), row(D), row(D), row(F), pl.BlockSpec((8, D), lambda i: (0, 0))],
        compiler_params=_cparams("arbitrary"),
    )(dx2, x1, r, mo, out_prev, par, w_in, w_out, *after)


def _mlp_weight_grads(h, dhid, r, dmo, layer, other, tag):
    tn = (((0,), (0,)), ((), ()))

    def body_in(h_ref, dhid_ref, *rest):
        rest[-1][...] = lax.dot_general(h_ref[...], dhid_ref[...], tn, preferred_element_type=f32).astype(bf16)

    def body_out(r_ref, dmo_ref, *rest):
        rf = r_ref[...].astype(f32)
        rest[-1][...] = lax.dot_general((rf * rf).astype(bf16), dmo_ref[...], tn,
                                        preferred_element_type=f32).astype(bf16)

    def call(body, name, operands, specs, block, prev):
        extra = [] if prev is None else [prev]
        return pl.pallas_call(
            body, name=name, grid=(N_DEV,), out_shape=jax.ShapeDtypeStruct((N_DEV, 2) + block, bf16),
            in_specs=specs + [pl.BlockSpec(memory_space=pl.ANY)] * len(extra),
            out_specs=pl.BlockSpec((None, None) + block, lambda j: (j, layer, 0, 0)),
            input_output_aliases={} if prev is None else {2: 0},
            compiler_params=_cparams("parallel"),
        )(*operands, *extra)

    dw_in = call(body_in, tag + "_mlp_in_dw", [h, dhid],
                 [_resident((T_LAT, D), lambda j: (0, 0)), pl.BlockSpec((T_LAT, FB), lambda j: (0, j))], (D, FB),
                 None if other is None else other[0])
    dw_out = call(body_out, tag + "_mlp_out_dw", [r, dmo],
                  [pl.BlockSpec((T_LAT, FB), lambda j: (0, j)), _resident((T_LAT, D), lambda j: (0, 0))], (FB, D),
                  None if other is None else other[1])
    return dw_in, dw_out


def _pos_embed():
    n_rows = T_LAT // GRID_W
    q = D // 4
    omega = 1.0 / (POS_BASE ** (jnp.arange(q, dtype=f32) / q))
    er = jnp.arange(n_rows, dtype=jnp.int32).astype(f32)[:, None] * omega[None, :]
    ec = jnp.arange(GRID_W, dtype=jnp.int32).astype(f32)[:, None] * omega[None, :]
    by_row = jnp.concatenate([jnp.sin(er), jnp.cos(er)], axis=-1)[:, None, :]
    by_col = jnp.concatenate([jnp.sin(ec), jnp.cos(ec)], axis=-1)[None, :, :]
    full = jnp.concatenate([jnp.broadcast_to(by_row, (n_rows, GRID_W, D // 2)),
                            jnp.broadcast_to(by_col, (n_rows, GRID_W, D // 2))], axis=-1)
    return full.reshape(T_LAT, D)


HALF = R // 2
BLK_PER_HALF = N_BLK // 2
N_PARTS = 4


def _gate_matrix(w_a, w_x):
    eye = jnp.eye(BLK_PER_HALF, dtype=bf16)
    cols = []
    for h in range(2):
        for d in range(2):
            for w in (w_a, w_x):
                blocks = w[d, BLK_PER_HALF * h:BLK_PER_HALF * (h + 1)].astype(bf16)
                cols.append(jnp.einsum("hij,hg->higj", blocks, eye).reshape(HALF, HALF))
    return jnp.concatenate(cols, axis=1)


def _gate_blocks(dwg, part):
    out = []
    for h in range(2):
        blk = dwg[:, (N_PARTS * h + part) * HALF:(N_PARTS * h + part + 1) * HALF]
        blk = blk.reshape(BLK_PER_HALF, BLK, BLK_PER_HALF, BLK)
        out.append(jnp.moveaxis(jnp.diagonal(blk, axis1=0, axis2=2), -1, 0))
    return jnp.concatenate(out, axis=0)


GATE_BM = 768


def _gates_dx(dpre, wg, after=()):
    rows = dpre.shape[0]
    n_after = len(after)

    def body(d_ref, w_ref, *rest):
        rest[n_after][...] = lax.dot_general(d_ref[...], w_ref[...], (((1,), (1,)), ((), ())),
                                             preferred_element_type=f32)

    return pl.pallas_call(
        body, name="l0_gates_dx", grid=(rows // GATE_BM, 2), out_shape=jax.ShapeDtypeStruct((rows, R), f32),
        in_specs=[pl.BlockSpec((GATE_BM, N_PARTS * HALF), lambda i, h: (i, h)),
                  pl.BlockSpec((HALF, N_PARTS * HALF), lambda i, h: (0, h))] + [ANY_SPEC] * n_after,
        out_specs=pl.BlockSpec((GATE_BM, HALF), lambda i, h: (i, h)),
        compiler_params=_cparams("parallel", "parallel"),
    )(dpre, wg, *after)


COEFF_TM = 1152


def _dir_params(d, *params):
    specs = [pl.BlockSpec((None, 1, HALF), lambda h, i: (d, 0, h))] * len(params)
    return specs, [p.reshape(2, 1, R) for p in params]


def _gates_coeff_fwd(ub, u, wg, ba, bx, lam, d):
    rows = u.shape[0]

    def body(ub_ref, u_ref, w_ref, ba_ref, bx_ref, lam_ref, a_ref, b_ref):
        pre = jnp.dot(ub_ref[...], w_ref[...], preferred_element_type=f32)
        a, b = _coeff(pre[:, :HALF], pre[:, HALF:], u_ref[...], ba_ref[...], bx_ref[...], lam_ref[...])
        a_ref[...] = a
        b_ref[...] = b

    tile = pl.BlockSpec((COEFF_TM, HALF), lambda h, i: (i, h))
    pspecs, pargs = _dir_params(d, ba, bx, lam)
    return pl.pallas_call(
        body, name=f"l0_gates_coeff_{d}", grid=(2, rows // COEFF_TM),
        out_shape=[jax.ShapeDtypeStruct((rows, R), f32)] * 2,
        in_specs=[tile, tile, pl.BlockSpec((HALF, 2 * HALF), lambda h, i: (0, 2 * h + d))] + pspecs,
        out_specs=[tile, tile], compiler_params=_cparams("parallel", "parallel"),
    )(ub, u, wg, *pargs)


def _gates_coeff_bwd(ub, u, dh, yp, wg, ba, bx, lam, d, dpre_prev):
    rows = u.shape[0]
    n_prev = 0 if dpre_prev is None else 1

    def body(ub_ref, u_ref, dh_ref, yp_ref, w_ref, ba_ref, bx_ref, lam_ref, *rest):
        dpre_ref, du_ref, dba_ref, dbx_ref, dlam_ref = rest[n_prev:]
        pre = jnp.dot(ub_ref[...], w_ref[...], preferred_element_type=f32)
        dhv = dh_ref[...]
        dpa, dpx, du, dba, dbx, dlam = _coeff_bwd(pre[:, :HALF], pre[:, HALF:], u_ref[...], ba_ref[...], bx_ref[...],
                                                  lam_ref[...], dhv * yp_ref[...], dhv)
        dpre_ref[:, :HALF] = dpa.astype(bf16)
        dpre_ref[:, HALF:] = dpx.astype(bf16)
        du_ref[...] = du

        @pl.when(pl.program_id(1) == 0)
        def _():
            dba_ref[...] = jnp.zeros_like(dba_ref)
            dbx_ref[...] = jnp.zeros_like(dbx_ref)
            dlam_ref[...] = jnp.zeros_like(dlam_ref)

        dba_ref[...] += dba
        dbx_ref[...] += dbx
        dlam_ref[...] += dlam

    tile = pl.BlockSpec((COEFF_TM, HALF), lambda h, i: (i, h))
    acc = pl.BlockSpec((1, HALF), lambda h, i: (0, h))
    pspecs, pargs = _dir_params(d, ba, bx, lam)
    extra = [] if dpre_prev is None else [dpre_prev]
    return pl.pallas_call(
        body, name=f"l0_gates_coeff_bwd_{d}", grid=(2, rows // COEFF_TM),
        out_shape=[jax.ShapeDtypeStruct((rows, 2 * N_PARTS * HALF), bf16), jax.ShapeDtypeStruct((rows, R), f32)]
        + [jax.ShapeDtypeStruct((1, R), f32)] * 3,
        in_specs=[tile] * 4 + [pl.BlockSpec((HALF, 2 * HALF), lambda h, i: (0, 2 * h + d))] + pspecs
        + [ANY_SPEC] * n_prev,
        out_specs=[pl.BlockSpec((COEFF_TM, 2 * HALF), lambda h, i: (i, 2 * h + d)), tile, acc, acc, acc],
        input_output_aliases={8: 0} if n_prev else {}, compiler_params=_cparams("parallel", "arbitrary"),
    )(ub, u, dh, yp, wg, *pargs, *extra)


def _gates_dw(u, dpre):
    rows = u.shape[0]

    def body(u_ref, d_ref, o_ref):
        o_ref[...] = lax.dot_general(u_ref[...], d_ref[...], (((0,), (0,)), ((), ())), preferred_element_type=f32)

    return pl.pallas_call(
        body, name="l0_gates_dw", grid=(2 * N_PARTS,), out_shape=jax.ShapeDtypeStruct((HALF, 2 * N_PARTS * HALF), f32),
        in_specs=[pl.BlockSpec((rows, HALF), lambda j: (0, j // N_PARTS)), pl.BlockSpec((rows, HALF), lambda j: (0, j))],
        out_specs=pl.BlockSpec((HALF, HALF), lambda j: (0, j)), compiler_params=_cparams("parallel"),
    )(u, dpre)


N_SCAN_CHUNKS = T_ALL // SCAN_CHUNK
SCAN_FWD = lambda t: t
SCAN_FWD_BWD = lambda t: N_SCAN_CHUNKS - 1 - t
SCAN_REV = lambda t: jnp.where(t == 0, 0, N_SCAN_CHUNKS - t)
SCAN_REV_BWD = lambda t: jnp.where(t == N_SCAN_CHUNKS - 1, 0, t + 1)
CONV_SEGMENTS = ((0, T_CTX), (T_CTX, T_LAT))
FUSED_TM = 256


def _token_rows(x, ctx):
    return (jnp.concatenate([ctx, x], axis=0),
            jnp.concatenate([jnp.zeros((T_CTX, D), f32), _pos_embed()], axis=0))


def _local_step(xcat, poscat, target, mods, cmod, wts, late_weights, send_grads, reduce_loss, start_after=()):
    sh1, sc1, g1, sh2, sc2, g2 = [[mods[l, i][None] for l in range(2)] for i in range(N_MOD)]
    ng = wts["norm_g"]
    scp = jnp.concatenate([cmod[1][None], sc1[0]], axis=0)
    shp = jnp.concatenate([cmod[0][None], sh1[0]], axis=0)

    ctx_tiles = T_CTX // FUSED_TM
    nt = (((1,), (1,)), ((), ()))

    def blend(i, p):
        sel = jnp.where(i < ctx_tiles, 1.0, 0.0)
        return sel * p[0:1] + (1.0 - sel) * p[1:2]

    def f_pre0(i, xc, pos, g, scp_, shp_, w):
        x0 = xc + pos
        h = _normmod(x0, g, blend(i, scp_), blend(i, shp_)).astype(bf16)
        return x0, h, jnp.dot(h, w, preferred_element_type=f32)

    x0cat, h0, gr = _rowcall(f_pre0, "l0_prenorm_in_proj", T_ALL, FUSED_TM, [_rin(xcat), _rin(poscat)],
                             [ng[0, 0][None], scp, shp, wts["rec_w_in"]], [(D, f32), (D, bf16), (2 * R, f32)],
                             after=start_after)
    u, ub = _dwconv_fwd(gr, R // 256, wts["rec_conv_w"], wts["rec_conv_b"], 4, 1, CONV_SEGMENTS, 256,
                        "l0_conv", True)
    gate_args = (wts["gates"], wts["rec_b_a"], wts["rec_b_x"], wts["rec_lambda"])
    a0, b0 = _gates_coeff_fwd(ub, u, *gate_args, 0)
    a1, b1 = _gates_coeff_fwd(ub, u, *gate_args, 1)
    halfway = late_weights("mlp_halfway", a1)
    y0, yp0 = _scan_call(a0, b0, SCAN_FWD, False, "l0_scan_fwd", False, after=[halfway])
    y1, yp1 = _scan_call(a1, b1, SCAN_REV, True, "l0_scan_rev", False)

    wts = dict(wts, **late_weights("mlp", y1))

    def f_gate_out(i, gp, y0_, y1_, w):
        z = (_gelu(gp) * (y0_ + y1_)).astype(bf16)
        return z, jnp.dot(z, w, preferred_element_type=f32)

    zb, out0 = _rowcall(f_gate_out, "l0_gate_out_proj", T_LAT, FUSED_TM,
                        [_rin(gr, R, 0, ctx_tiles), _rin(y0, None, 0, ctx_tiles), _rin(y1, None, 0, ctx_tiles)],
                        [wts["rec_w_out"]], [(R, bf16), (D, f32)])

    zero_d = jnp.zeros((1, D), f32)

    def mlp_params(rows):
        rows = rows + [zero_d] * (N_MLP_PARAMS - len(rows))
        return jnp.concatenate([jnp.broadcast_to(r, (8, D)) for r in rows], axis=0)

    par0 = mlp_params([g1[0], zero_d, ng[0, 1][None], sc2[0], sh2[0], g2[0], ng[1, 0][None], sc1[1], sh1[1]])
    x1, h1, r0, mo0, x2, h2 = _mlp_forward(x0cat, T_CTX // MLP_TM, out0, par0, wts["mlp_w_in"], wts["mlp_w_out"], 0,
                                           "l0_mlp")

    wts = dict(wts, **late_weights("conf", x2))
    def glu(pa, pb, b1):
        return (pa + b1[:, :D]) * _sigmoid(pb + b1[:, D:])

    def f_pw1_glu(i, h_, b1, w):
        p = jnp.dot(h_, w, preferred_element_type=f32)
        return glu(p[:, :D], p[:, D:], b1), p

    zg, pw = _rowcall(f_pw1_glu, "l1_pw1_glu", T_LAT, FUSED_TM, [_rin(h2)], [wts["conf_b_pw1"], wts["conf_w_pw1"]],
                      [(D, f32), (2 * D, bf16)])
    (zc,) = _dwconv_fwd(zg, 0, wts["conf_conv_w"], wts["conf_conv_b"], 31, 15, ((0, T_LAT),), 128, "l1_conv", False)

    def ln_silu(z, lg, lb):
        mu = jnp.mean(z, axis=-1, keepdims=True)
        zc_ = z - mu
        var = jnp.mean(zc_ * zc_, axis=-1, keepdims=True)
        yv = zc_ * lax.rsqrt(var + EPS) * lg + lb
        return yv * _sigmoid(yv)

    def f_lnsilu_pw2(i, z, lg, lb, w):
        s = ln_silu(z, lg, lb).astype(bf16)
        return s, jnp.dot(s, w, preferred_element_type=f32)

    sb, out1 = _rowcall(f_lnsilu_pw2, "l1_ln_silu_pw2", T_LAT, FUSED_TM, [_rin(zc)],
                        [wts["conf_ln_g"], wts["conf_ln_b"], wts["conf_w_pw2"]], [(D, bf16), (D, f32)])
    par1 = mlp_params([g1[1], wts["conf_b_pw2"], ng[1, 1][None], sc2[1], sh2[1], g2[1]])
    x3, h3, r1, mo1, x4, _ = _mlp_forward(x2, 0, out1, par1, wts["mlp_w_in"], wts["mlp_w_out"], 1, "l1_mlp")

    def loss_fn(x4_, fg, tgt):
        err = _rms(x4_, fg) - tgt
        per_row = jnp.mean(err * err, axis=-1, keepdims=True)
        return 0.5 * jnp.sum(per_row, axis=0, keepdims=True)

    def f_head(i, x4_, tgt, fg):
        loss, vjp = jax.vjp(lambda a, e: loss_fn(a, e, tgt), x4_, fg)
        dx, dfg = vjp(jnp.ones((1, 1), f32))
        return dx, jnp.broadcast_to(loss, (1, 128)), dfg

    dx4, loss_acc, dfinal_g = _rowcall(f_head, "head", T_LAT, FUSED_TM, [_rin(x4), _rin(target)], [wts["final_g"]],
                                       [(D, f32)], [(1, 128), (1, D)])

    grads = {"final_g": dfinal_g}
    loss = reduce_loss(loss_acc[0, 0])

    dx3, dout1, dmo1, dhid1, acc1 = _mlp_backward(dx4, x3, r1, mo1, out1, par1, wts["mlp_w_in"], wts["mlp_w_out"], 1,
                                                  "l1_mlp_bwd", after=[loss.reshape(1, 1)])
    mlp_dw = _mlp_weight_grads(h3, dhid1, r1, dmo1, 1, None, "l1")
    dg1_1, db_pw2, dng11, dsc2_1, dsh2_1, dg2_1 = [acc1[k:k + 1] for k in range(6)]

    grads["conf_w_pw2"] = _mm(sb, dout1, "l1_pw2_dw", ta=True, out_dtype=bf16)
    grads["conf_b_pw2"] = db_pw2

    def f_pw2_lnsilu_bwd(i, z, dout, lg, lb, w):
        ds = lax.dot_general(dout, w, nt, preferred_element_type=f32)
        _, vjp = jax.vjp(ln_silu, z, lg, lb)
        return vjp(ds)

    dzc, dln_g, dln_b = _rowcall(f_pw2_lnsilu_bwd, "l1_pw2_ln_silu_bwd", T_LAT, FUSED_TM, [_rin(zc), _rin(dout1)],
                                 [wts["conf_ln_g"], wts["conf_ln_b"], wts["conf_w_pw2"]], [(D, f32)], [(1, D)] * 2)
    grads["conf_ln_g"], grads["conf_ln_b"] = dln_g, dln_b
    dzg, dconv_w, dconv_b = _dwconv_bwd([dzc], zg, 0, wts["conf_conv_w"], 31, 15, ((0, T_LAT),), 128,
                                        "l1_conv_bwd", f32)
    grads["conf_conv_w"], grads["conf_conv_b"] = dconv_w, dconv_b

    def f_glu_pw1_norm_bwd(i, p_, dz, x_, dxs, b1, g_, sc_, sh_, w):
        pf = p_.astype(f32)
        _, vjp = jax.vjp(glu, pf[:, :D], pf[:, D:], b1)
        da, db, db1 = vjp(dz)
        dp = jnp.concatenate([da, db], axis=1).astype(bf16)
        dh = lax.dot_general(dp, w, nt, preferred_element_type=f32)
        _, vjp = jax.vjp(_normmod, x_, g_, sc_, sh_)
        dx, dg, dsc, dsh = vjp(dh)
        return dp, dx + dxs, db1, dg, dsc, dsh

    dpw, dx2, db_pw1, dng10, dsc1_1, dsh1_1 = _rowcall(
        f_glu_pw1_norm_bwd, "l1_glu_pw1_normmod_bwd", T_LAT, FUSED_TM, [_rin(pw), _rin(dzg), _rin(x2), _rin(dx3)],
        [wts["conf_b_pw1"], ng[1, 0][None], sc1[1], sh1[1], wts["conf_w_pw1"]], [(2 * D, bf16), (D, f32)],
        [(1, 2 * D), (1, D), (1, D), (1, D)])
    grads["conf_b_pw1"] = db_pw1
    grads["conf_w_pw1"] = _mm(h2, dpw, "l1_pw1_dw", ta=True, out_dtype=bf16)
    sent = send_grads(["conf_w_pw2", "conf_w_pw1"], grads)

    dx1, dout0, dmo0, dhid0, acc0 = _mlp_backward(dx2, x1, r0, mo0, out0, par0, wts["mlp_w_in"], wts["mlp_w_out"], 0,
                                                  "l0_mlp_bwd", after=[sent])
    grads["mlp_w_in"], grads["mlp_w_out"] = _mlp_weight_grads(h1, dhid0, r0, dmo0, 0, mlp_dw, "l0")
    sent = send_grads(["mlp_w_in", "mlp_w_out"], grads)
    dg1_0, _, dng01, dsc2_0, dsh2_0, dg2_0 = [acc0[k:k + 1] for k in range(6)]

    grads["rec_w_out"] = _mm(zb, dout0, "l0_out_proj_dw", ta=True, out_dtype=bf16, after=[sent])
    sent = send_grads(["rec_w_out"], grads)

    def f_out_gate_bwd(i, gp, y0_, y1_, dout, w):
        lat = jnp.where(i < ctx_tiles, 0.0, 1.0)
        dz = lax.dot_general(dout, w, nt, preferred_element_type=f32)
        _, vjp = jax.vjp(lambda a, b: _gelu(a) * b, gp, y0_ + y1_)
        dgp, dy = vjp(dz)
        return dgp * lat, dy * lat

    dgp, dy = _rowcall(f_out_gate_bwd, "l0_out_proj_gate_bwd", T_ALL, FUSED_TM,
                       [_rin(gr, R, 0), _rin(y0), _rin(y1), _rin(dout0, None, 0, -ctx_tiles)], [wts["rec_w_out"]],
                       [(R, bf16), (R, f32)], after=[sent])
    (dh_f,) = _scan_call(a0, dy, SCAN_FWD_BWD, True, "l0_scan_fwd_bwd", True)
    (dh_r,) = _scan_call(a1, dy, SCAN_REV_BWD, False, "l0_scan_rev_bwd", True)

    dpre, du_f, *dpar_f = _gates_coeff_bwd(ub, u, dh_f, yp0, *gate_args, 0, None)
    dpre, du_r, *dpar_r = _gates_coeff_bwd(ub, u, dh_r, yp1, *gate_args, 1, dpre)
    grads["rec_b_a"], grads["rec_b_x"], grads["rec_lambda"] = [
        jnp.concatenate([f.reshape(-1), r_.reshape(-1)]).reshape(2, R) for f, r_ in zip(dpar_f, dpar_r)]
    grads["gates"] = _gates_dw(ub, dpre)
    sent = send_grads(["replicated"], grads)
    du_gates = _gates_dx(dpre, wts["gates"], after=[sent])
    drec, dconv4_w, dconv4_b = _dwconv_bwd([du_f, du_r, du_gates], gr, R // 256, wts["rec_conv_w"], 4, 1,
                                           CONV_SEGMENTS, 256, "l0_conv_bwd", bf16)
    grads["rec_conv_w"], grads["rec_conv_b"] = dconv4_w, dconv4_b
    dgr = jnp.concatenate([dgp, drec], axis=1)
    grads["rec_w_in"] = _mm(h0, dgr, "l0_in_proj_dw", ta=True, out_dtype=bf16)
    sent = send_grads(["rec_w_in"], grads)

    def f_pre0_bwd(i, x0, dgr_, dxs, g, scp_, shp_, w):
        lat = jnp.where(i < ctx_tiles, 0.0, 1.0)
        dh = lax.dot_general(dgr_, w, nt, preferred_element_type=f32)
        _, vjp = jax.vjp(lambda a, b, c, e: _normmod(a, b, blend(i, c), blend(i, e)), x0, g, scp_, shp_)
        dx, dg, dscp, dshp = vjp(dh)
        return dx + lat * dxs, dg, dscp, dshp

    dx0cat, dng00, dscp, dshp = _rowcall(
        f_pre0_bwd, "l0_in_proj_prenorm_bwd", T_ALL, FUSED_TM,
        [_rin(x0cat), _rin(dgr), _rin(dx1, None, 0, -ctx_tiles)], [ng[0, 0][None], scp, shp, wts["rec_w_in"]],
        [(D, f32)], [(1, D), (2, D), (2, D)], after=[sent])

    grads["norm_g"] = jnp.stack([jnp.concatenate([dng00, dng01], 0), jnp.concatenate([dng10, dng11], 0)])
    dmods = jnp.stack([
        jnp.concatenate([dshp[1:2], dscp[1:2], dg1_0, dsh2_0, dsc2_0, dg2_0], axis=0),
        jnp.concatenate([dsh1_1, dsc1_1, dg1_1, dsh2_1, dsc2_1, dg2_1], axis=0)])
    dcmod = jnp.concatenate([dshp[0:1], dscp[0:1]], axis=0)
    return loss, dx0cat[T_CTX:], dmods, dcmod, grads


def _unshard_cols(g):
    g = jnp.moveaxis(g, 0, -2)
    return g.reshape(g.shape[:-2] + (g.shape[-2] * g.shape[-1],))


def _shard_cols(w):
    w = w.reshape(w.shape[:-1] + (N_DEV, w.shape[-1] // N_DEV))
    return jnp.moveaxis(w, -2, 0)


def _shard_rows(w):
    return w.reshape((N_DEV, w.shape[0] // N_DEV) + w.shape[1:])


SMALL_PACK_ROWS = 64
REPL_FINAL_G_ROWS = -(-D // BLK)
REPL_ROWS = -(-(2 * 2 * N_BLK * BLK + 2 * 2 * N_BLK + REPL_FINAL_G_ROWS) // 16) * 16


def kernel(x, c, ctx, c_ctx, w_ada, b_ada, norm_g, rec_w_in, rec_conv_w, rec_conv_b, rec_lambda, rec_w_a, rec_b_a, rec_w_x, rec_b_x, rec_w_out, conf_w_pw1, conf_b_pw1, conf_conv_w, conf_conv_b, conf_ln_g, conf_ln_b, conf_w_pw2, conf_b_pw2, mlp_w_in, mlp_w_out, final_g, loss_target, m_c_ctx, m_w_ada, m_b_ada, m_norm_g, m_rec_w_in, m_rec_conv_w, m_rec_conv_b, m_rec_lambda, m_rec_w_a, m_rec_b_a, m_rec_w_x, m_rec_b_x, m_rec_w_out, m_conf_w_pw1, m_conf_b_pw1, m_conf_conv_w, m_conf_conv_b, m_conf_ln_g, m_conf_ln_b, m_conf_w_pw2, m_conf_b_pw2, m_mlp_w_in, m_mlp_w_out, m_final_g, v_c_ctx, v_w_ada, v_b_ada, v_norm_g, v_rec_w_in, v_rec_conv_w, v_rec_conv_b, v_rec_lambda, v_rec_w_a, v_rec_b_a, v_rec_w_x, v_rec_b_x, v_rec_w_out, v_conf_w_pw1, v_conf_b_pw1, v_conf_conv_w, v_conf_conv_b, v_conf_ln_g, v_conf_ln_b, v_conf_w_pw2, v_conf_b_pw2, v_mlp_w_in, v_mlp_w_out, v_final_g):
    me = 4 * lax.axis_index("x") + 2 * lax.axis_index("y") + lax.axis_index("c")
    weights = dict(c_ctx=c_ctx, w_ada=w_ada, b_ada=b_ada, norm_g=norm_g, rec_w_in=rec_w_in, rec_conv_w=rec_conv_w,
                   rec_conv_b=rec_conv_b, rec_lambda=rec_lambda, rec_w_a=rec_w_a, rec_b_a=rec_b_a, rec_w_x=rec_w_x,
                   rec_b_x=rec_b_x, rec_w_out=rec_w_out, conf_w_pw1=conf_w_pw1, conf_b_pw1=conf_b_pw1,
                   conf_conv_w=conf_conv_w, conf_conv_b=conf_conv_b, conf_ln_g=conf_ln_g, conf_ln_b=conf_ln_b,
                   conf_w_pw2=conf_w_pw2, conf_b_pw2=conf_b_pw2, mlp_w_in=mlp_w_in, mlp_w_out=mlp_w_out, final_g=final_g)
    m_in = dict(c_ctx=m_c_ctx, w_ada=m_w_ada, b_ada=m_b_ada, norm_g=m_norm_g, rec_w_in=m_rec_w_in, rec_conv_w=m_rec_conv_w,
                rec_conv_b=m_rec_conv_b, rec_lambda=m_rec_lambda, rec_w_a=m_rec_w_a, rec_b_a=m_rec_b_a, rec_w_x=m_rec_w_x,
                rec_b_x=m_rec_b_x, rec_w_out=m_rec_w_out, conf_w_pw1=m_conf_w_pw1, conf_b_pw1=m_conf_b_pw1,
                conf_conv_w=m_conf_conv_w, conf_conv_b=m_conf_conv_b, conf_ln_g=m_conf_ln_g, conf_ln_b=m_conf_ln_b,
                conf_w_pw2=m_conf_w_pw2, conf_b_pw2=m_conf_b_pw2, mlp_w_in=m_mlp_w_in, mlp_w_out=m_mlp_w_out,
                final_g=m_final_g)
    v_in = dict(c_ctx=v_c_ctx, w_ada=v_w_ada, b_ada=v_b_ada, norm_g=v_norm_g, rec_w_in=v_rec_w_in, rec_conv_w=v_rec_conv_w,
                rec_conv_b=v_rec_conv_b, rec_lambda=v_rec_lambda, rec_w_a=v_rec_w_a, rec_b_a=v_rec_b_a, rec_w_x=v_rec_w_x,
                rec_b_x=v_rec_b_x, rec_w_out=v_rec_w_out, conf_w_pw1=v_conf_w_pw1, conf_b_pw1=v_conf_b_pw1,
                conf_conv_w=v_conf_conv_w, conf_conv_b=v_conf_conv_b, conf_ln_g=v_conf_ln_g, conf_ln_b=v_conf_ln_b,
                conf_w_pw2=v_conf_w_pw2, conf_b_pw2=v_conf_b_pw2, mlp_w_in=v_mlp_w_in, mlp_w_out=v_mlp_w_out,
                final_g=v_final_g)
    names = list(weights)

    small_items = [c, norm_g, rec_conv_w, rec_lambda, conf_b_pw1, conf_conv_w, conf_conv_b, conf_ln_g, conf_ln_b,
                   conf_b_pw2]
    flat = jnp.concatenate([a.reshape(-1) for a in small_items])
    flat = jnp.pad(flat, (0, SMALL_PACK_ROWS * 128 - flat.shape[0])).reshape(SMALL_PACK_ROWS, 128)
    as_shard = lambda a: a.astype(bf16).reshape(-1, a.shape[-1])
    early_srcs = [flat, as_shard(rec_w_in[0])]
    early_handle, started = _exchange_start(early_srcs, [_own_block_filled(s, me) for s in early_srcs],
                                            "gather_early_start", False)
    zero = started[0, 0]
    gates = _gate_matrix(rec_w_a[0] + zero, rec_w_x[0] + zero)
    late_items = {"mlp": [rec_w_out[0], mlp_w_in, mlp_w_out], "conf": [conf_w_pw1[0], conf_w_pw2[0]]}
    late_shards = {g: [as_shard(a + zero) for a in items] for g, items in late_items.items()}
    late_lands = {g: [_own_block_filled(s, me) for s in shards] for g, shards in late_shards.items()}
    xcat, poscat = _token_rows(x[0] + zero, ctx[0])
    small_all, early = _exchange_wait(early_handle, [gates, xcat, poscat] + late_lands["mlp"] + late_lands["conf"],
                                      "gather_early_wait", False)

    small_all = small_all.reshape(N_DEV, -1)
    off = 0
    small = []
    for a in small_items:
        small.append(small_all[:, off:off + a.size].reshape((N_DEV,) + a.shape))
        off += a.size
    c_all, ng_all, rcw_all, lam_all, bpw1_all, ccw_all, ccb_all, lng_all, lnb_all, bpw2_all = small
    wts = {
        "norm_g": _unshard_cols(ng_all),
        "rec_conv_w": _unshard_cols(rcw_all)[0],
        "rec_lambda": _unshard_cols(lam_all)[0],
        "conf_b_pw1": _unshard_cols(bpw1_all),
        "conf_conv_w": _unshard_cols(ccw_all)[0],
        "conf_conv_b": _unshard_cols(ccb_all),
        "conf_ln_g": _unshard_cols(lng_all),
        "conf_ln_b": _unshard_cols(lnb_all),
        "conf_b_pw2": _unshard_cols(bpw2_all),
        "rec_conv_b": rec_conv_b,
        "rec_b_a": rec_b_a[0].reshape(2, R),
        "rec_b_x": rec_b_x[0].reshape(2, R),
        "final_g": final_g[None],
        "gates": gates,
    }

    c16 = jnp.concatenate([c_all[:, 0], jnp.broadcast_to(c_ctx[None], (8, D))], axis=0)
    b_loc = lax.dynamic_slice_in_dim(b_ada, me * ADA_SHARD, ADA_SHARD, axis=1)[:, None]
    (mods_gathered,) = _all_gather([_ada_forward(c16, w_ada, b_loc)], "gather_mods")
    mods_all = _unshard_cols(mods_gathered)
    mods = lax.dynamic_index_in_dim(mods_all, me, axis=1, keepdims=False).reshape(2, N_MOD, D)
    cmod = mods_all[0, 8, :2 * D].reshape(2, D)

    late_handles = {}
    late_handles["mlp"], token = _gather2_start(late_shards["mlp"], late_lands["mlp"], "gather_mlp_start",
                                                [early, mods_gathered])
    order = [token]
    wts["rec_w_in"] = _unshard_cols(early + token[0, 0].astype(bf16))

    def late_weights(group, after):
        if group == "mlp_halfway":
            late_handles["mlp"] = _gather2_forward1(late_handles["mlp"], after, "gather_mlp_forward1")
            return late_handles["mlp"][2][0]
        if group == "mlp":
            passed = _gather2_forward2(late_handles["mlp"], after, "gather_mlp_forward2")
            late_handles["conf"], started = _exchange_start(late_shards["conf"], late_lands["conf"], "gather_conf_start",
                                                            False, after=[passed[2][0]])
            got = _gather2_wait(passed, started, "gather_mlp_wait")
        else:
            got = _exchange_wait(late_handles[group], after, "gather_conf_wait", False)
        got = [g.reshape((N_DEV,) + a.shape) for g, a in zip(got, late_items[group])]
        if group == "mlp":
            return {"rec_w_out": got[0].reshape(R, D), "mlp_w_in": got[1], "mlp_w_out": got[2]}
        return {"conf_w_pw1": _unshard_cols(got[0]), "conf_w_pw2": got[1].reshape(D, D)}

    to_blocks = {"rec_w_in": _shard_cols, "conf_w_pw1": _shard_cols, "rec_w_out": _shard_rows, "conf_w_pw2": _shard_rows,
                 "mlp_w_in": lambda g: g, "mlp_w_out": lambda g: g}
    grad_handles = []

    repl_names = ["rec_w_a", "rec_w_x", "rec_b_a", "rec_b_x", "final_g"]

    def send_replicated(grads):
        dwg = grads["gates"]
        repl = {"rec_w_a": jnp.stack([_gate_blocks(dwg, 0), _gate_blocks(dwg, 2)]),
                "rec_w_x": jnp.stack([_gate_blocks(dwg, 1), _gate_blocks(dwg, 3)]),
                "rec_b_a": grads["rec_b_a"], "rec_b_x": grads["rec_b_x"],
                "final_g": jnp.pad(grads["final_g"], ((0, 0), (0, REPL_FINAL_G_ROWS * BLK - D)))}
        flat = jnp.concatenate([repl[n].reshape(-1, BLK) for n in repl_names], axis=0)
        flat = jnp.pad(flat, ((0, REPL_ROWS - flat.shape[0]), (0, 0))).astype(bf16)
        flat = flat.reshape(REPL_ROWS // 8, 8 * BLK)
        handle, sent = _exchange_start([flat], [_own_block_filled(flat, me)], "gather_replicated_start", False)
        grad_handles.append((["replicated"], handle))
        return sent

    def send_grads(group, grads):
        if group == ["replicated"]:
            return send_replicated(grads)
        blocks = [to_blocks[n](grads[n]) for n in group]
        blocks = [g.reshape(N_DEV, -1, g.shape[-1]) for g in blocks]
        lands = [_own_block_filled(lax.dynamic_index_in_dim(g, me, 0, keepdims=False), me) for g in blocks]
        handle, sent = _exchange_start(blocks, lands, "scatter_start_" + group[0], True)
        grad_handles.append((group, handle))
        return sent

    loss, grad_x, dmods, dcmod, grads = _local_step(
        xcat, poscat, loss_target[0], mods, cmod, wts, late_weights, send_grads,
        lambda partial: lax.psum(partial, ("x", "y", "c")), start_after=order)

    def as2d(shape):
        rows = 1
        for s in shape[:-1]:
            rows *= s
        return (rows, shape[-1])

    def whole(arr, shape):
        arr = arr.reshape((-1,) + as2d(shape))
        return (arr, arr.shape[0])

    shard_shapes = {n: weights[n].shape for n in names}
    g_out, d_out, m_out, v_out = {}, {}, {}, {}

    def adamw(n, pieces, after):
        shape = shard_shapes[n]
        r2, c2 = as2d(shape)
        g, dl, nm, nv = _adamw(pieces, weights[n].reshape(r2, c2), m_in[n].reshape(r2, c2), v_in[n].reshape(r2, c2),
                               "adamw_" + n, after=after)
        g_out[n], d_out[n], m_out[n], v_out[n] = (t.reshape(shape) for t in (g, dl, nm, nv))
        return g

    small_sharded = ["norm_g", "rec_conv_w", "rec_lambda", "conf_b_pw1", "conf_conv_w", "conf_conv_b", "conf_ln_g",
                     "conf_ln_b", "conf_b_pw2"]
    pack = jnp.concatenate([_shard_cols(grads[n]).reshape(N_DEV, -1) for n in small_sharded], axis=1)
    pack = jnp.pad(pack, ((0, 0), (0, SMALL_PACK_ROWS * 128 - pack.shape[1]))).reshape(N_DEV, SMALL_PACK_ROWS, 128)
    small_handle, token = _exchange_start(
        [pack], [_own_block_filled(lax.dynamic_index_in_dim(pack, me, 0, keepdims=False), me)], "scatter_small_start",
        True, after=[grad_x])
    dm_flat = jnp.concatenate([dmods.reshape(-1), dcmod.reshape(-1), grads["rec_conv_b"].reshape(-1)])
    dm_len = dm_flat.shape[0]
    dm_flat = jnp.pad(dm_flat, (0, 128 * 128 - dm_len)).reshape(128, 128)
    dm_handle, token = _exchange_start([dm_flat], [_own_block_filled(dm_flat, me)], "gather_dmods_start", False,
                                       after=[token])

    done = token
    for group, handle in grad_handles:
        if group == ["replicated"]:
            repl_all = _exchange_wait(handle, done, "gather_replicated_wait", False)[0]
            repl_all = repl_all.reshape(N_DEV, REPL_ROWS, BLK)
            row = 0
            for n in repl_names:
                n_rows = -(-weights[n].size // BLK)
                if as2d(shard_shapes[n]) == (n_rows, BLK) and row % 256 == 0:
                    done = adamw(n, [(repl_all, N_DEV, row)], [done])
                else:
                    got = repl_all[:, row:row + n_rows].reshape(N_DEV, -1)[:, :weights[n].size]
                    done = adamw(n, [whole(got, shard_shapes[n])], [done])
                row += n_rows
            continue
        for n, got in zip(group, _exchange_wait(handle, done, "scatter_wait_" + group[0], True)):
            done = adamw(n, [(got, N_DEV)], [done])

    dm_all = _exchange_wait(dm_handle, done, "gather_dmods_wait", False)[0].reshape(N_DEV, -1)
    dmods_all = dm_all[:, :2 * N_MOD * D].reshape(N_DEV, 2, N_MOD * D)
    dcmod_all = jnp.pad(dm_all[:, 2 * N_MOD * D:2 * N_MOD * D + 2 * D], ((0, 0), (0, (N_MOD - 2) * D)))
    g16_full = jnp.stack([jnp.concatenate([dmods_all[:, 0], dcmod_all], axis=0),
                          jnp.concatenate([dmods_all[:, 1], jnp.zeros_like(dcmod_all)], axis=0)])
    g16 = lax.dynamic_slice_in_dim(g16_full, me * ADA_SHARD, ADA_SHARD, axis=2)
    dw_ada, ds_part = _ada_backward(c16, g16, w_ada)
    ds_handle, token = _exchange_start([ds_part[0]], [_own_block_filled(ds_part[0], me)], "gather_dsilu_start", False)
    done = adamw("w_ada", [whole(dw_ada, shard_shapes["w_ada"])], [token])
    done = adamw("rec_conv_b", [whole(dm_all[:, dm_len - R:dm_len], shard_shapes["rec_conv_b"])], [done])
    db_terms = jnp.concatenate([dmods_all, jnp.stack([dcmod_all, jnp.zeros_like(dcmod_all)], axis=1)], axis=0)
    done = adamw("b_ada", [whole(db_terms, shard_shapes["b_ada"])], [done])
    pack_recv = _exchange_wait(small_handle, done, "scatter_small_wait", True)[0].reshape(N_DEV, -1)
    off = 0
    for n in small_sharded:
        size = weights[n].size
        done = adamw(n, [whole(pack_recv[:, off:off + size], shard_shapes[n])], [done])
        off += size
    ds_all = _exchange_wait(ds_handle, done, "gather_dsilu_wait", False)[0]
    adamw("c_ctx", [whole(ds_all[:, 0], shard_shapes["c_ctx"])], [])

    return (loss, grad_x[None], *[g_out[n] for n in names], *[d_out[n] for n in names],
            *[m_out[n] for n in names], *[v_out[n] for n in names])
```

```python
import functools

import jax
import jax.numpy as jnp
from jax import lax
from jax.experimental import pallas as pl
from jax.experimental.pallas import tpu as pltpu

f32 = jnp.float32
bf16 = jnp.bfloat16

N_DEV = 8
D = 1024
T_LAT = 2048
T_CTX = 256
T_ALL = T_CTX + T_LAT
R = 1280
N_BLK = 16
BLK = R // N_BLK
F = 4096
GRID_W = 64
RG_C = 8.0
EPS = 1e-6
POS_BASE = 10000.0
N_MOD = 6
ADA_SHARD = N_MOD * D // N_DEV

ADAM_LR = 0.001
ADAM_B1 = 0.9
ADAM_B2 = 0.999
ADAM_EPS = 1e-08
ADAM_WD = 0.01
ADAM_STEP = 10

VMEM_LIMIT_V7X = 56 * 1024 * 1024
HALO = 16
MESH = pl.DeviceIdType.MESH


def _cparams(*sem):
    return pltpu.CompilerParams(dimension_semantics=sem, vmem_limit_bytes=VMEM_LIMIT_V7X)


def _pick(n, cands):
    for c in cands:
        if n % c == 0:
            return c
    raise ValueError(f"no block size for {n}")


def _position():
    x, y, c = lax.axis_index("x"), lax.axis_index("y"), lax.axis_index("c")
    return x, y, c, 4 * x + 2 * y + c


def _peer(x, y, c, k):
    px = (1 - x) if (k >> 2) & 1 else x
    py = (1 - y) if (k >> 1) & 1 else y
    pc = (1 - c) if k & 1 else c
    return (px, py, pc), 4 * px + 2 * py + pc


def _exchange(arrs, name, scatter):
    n = len(arrs)

    def body(*refs):
        ins, outs = refs[:n], refs[n:2 * n]
        send_sems, recv_sems, local_sems = refs[2 * n:]
        x, y, c, me = _position()
        local = []
        for a in range(n):
            src = ins[a].at[me] if scatter else ins[a]
            cp = pltpu.make_async_copy(src, outs[a].at[me], local_sems.at[a])
            cp.start()
            local.append(cp)
        sends, recvs = [], []
        for a in range(n):
            for k in range(1, N_DEV):
                peer, peer_lin = _peer(x, y, c, k)
                src = ins[a].at[peer_lin] if scatter else ins[a]
                cp = pltpu.make_async_remote_copy(
                    src_ref=src, dst_ref=outs[a].at[me], send_sem=send_sems.at[a, k - 1],
                    recv_sem=recv_sems.at[a, k - 1], device_id=peer, device_id_type=MESH)
                cp.start()
                sends.append(cp)
                recvs.append(pltpu.make_async_remote_copy(
                    src_ref=src, dst_ref=outs[a].at[peer_lin], send_sem=send_sems.at[a, k - 1],
                    recv_sem=recv_sems.at[a, k - 1], device_id=peer, device_id_type=MESH))
        for cp in recvs:
            cp.wait_recv()
        for cp in sends:
            cp.wait_send()
        for cp in local:
            cp.wait()

    if scatter:
        out_shape = [jax.ShapeDtypeStruct(a.shape, a.dtype) for a in arrs]
    else:
        out_shape = [jax.ShapeDtypeStruct((N_DEV,) + a.shape, a.dtype) for a in arrs]
    any_spec = pl.BlockSpec(memory_space=pl.ANY)
    return pl.pallas_call(
        body, name=name, out_shape=out_shape,
        in_specs=[any_spec] * n, out_specs=[any_spec] * n,
        scratch_shapes=[pltpu.SemaphoreType.DMA((n, N_DEV - 1)), pltpu.SemaphoreType.DMA((n, N_DEV - 1)),
                        pltpu.SemaphoreType.DMA((n,))],
    )(*arrs)


def _all_gather(arrs, name):
    return _exchange(arrs, name, scatter=False)


def _lin(p):
    return 4 * p[0] + 2 * p[1] + p[2]


HBM_SPEC = pl.BlockSpec(memory_space=pltpu.HBM)
SEM_SPEC = pl.BlockSpec(memory_space=pltpu.SEMAPHORE)
DATAFLOW_EFFECT = pltpu.SideEffectType.DATAFLOW_SIDE_EFFECTING


def _split_copies(srcs, lands, send_sems, recv_sems, scatter):
    x, y, c, me = _position()
    out = []
    for a in range(len(srcs)):
        for k in range(1, N_DEV):
            peer, peer_lin = _peer(x, y, c, k)
            src = srcs[a].at[peer_lin] if scatter else srcs[a]
            mk = lambda slot: pltpu.make_async_remote_copy(
                src_ref=src, dst_ref=lands[a].at[slot], send_sem=send_sems.at[a * (N_DEV - 1) + k - 1],
                recv_sem=recv_sems.at[a * (N_DEV - 1) + k - 1], device_id=peer, device_id_type=MESH)
            out.append((mk(me), mk(peer_lin)))
    return out


def _exchange_start(srcs, lands, name, scatter, after=()):
    n = len(srcs)
    n_after = len(after)

    def body(*refs):
        srcs_r, lands_r = refs[:n], refs[n:2 * n]
        send_sems, recv_sems = refs[2 * n + n_after], refs[2 * n + n_after + 1]
        token = refs[-1]
        for outgoing, _ in _split_copies(srcs_r, lands_r, send_sems, recv_sems, scatter):
            outgoing.start()
        token[...] = jnp.zeros_like(token)

    hbm = lambda a: pltpu.HBM(a.shape, a.dtype)
    res = pl.pallas_call(
        body, name=name,
        out_shape=(pltpu.SemaphoreType.DMA((n * (N_DEV - 1),)), pltpu.SemaphoreType.DMA((n * (N_DEV - 1),)),
                   *[hbm(a) for a in srcs], *[hbm(a) for a in lands], jax.ShapeDtypeStruct((8, 128), f32)),
        in_specs=[HBM_SPEC] * (2 * n) + [pl.BlockSpec(memory_space=pl.ANY)] * n_after,
        out_specs=(SEM_SPEC, SEM_SPEC, *[HBM_SPEC] * (2 * n), pl.BlockSpec(memory_space=pltpu.VMEM)),
        input_output_aliases={i: 2 + i for i in range(2 * n)},
        compiler_params=pltpu.CompilerParams(has_side_effects=DATAFLOW_EFFECT),
    )(*[pltpu.with_memory_space_constraint(a, pltpu.HBM) for a in list(srcs) + list(lands)], *after)
    return (res[0], res[1], list(res[2:2 + n]), list(res[2 + n:2 + 2 * n])), res[-1]


def _exchange_wait(handle, after, name, scatter):
    send_sems, recv_sems, srcs, lands = handle
    n = len(srcs)
    after = list(after) if isinstance(after, (list, tuple)) else [after]

    def body(*refs):
        srcs_r, lands_r = refs[:n], refs[n:2 * n]
        send_s, recv_s = refs[2 * n], refs[2 * n + 1]
        for outgoing, incoming in _split_copies(srcs_r, lands_r, send_s, recv_s, scatter):
            outgoing.wait_send()
            incoming.wait_recv()

    hbm = lambda a: pltpu.HBM(a.shape, a.dtype)
    res = pl.pallas_call(
        body, name=name, out_shape=tuple(hbm(a) for a in list(srcs) + list(lands)),
        in_specs=[HBM_SPEC] * (2 * n) + [SEM_SPEC, SEM_SPEC] + [pl.BlockSpec(memory_space=pl.ANY)] * len(after),
        out_specs=tuple([HBM_SPEC] * (2 * n)),
        input_output_aliases={i: i for i in range(2 * n)},
        compiler_params=pltpu.CompilerParams(has_side_effects=DATAFLOW_EFFECT),
    )(*srcs, *lands, send_sems, recv_sems, *after)
    return list(res[n:])


def _split_call(body, name, hbm_ins, kept, in_sems, n_new_sems, after, with_token):
    n_in, n_sem = len(hbm_ins), len(in_sems)
    out_shape, out_specs = [], []
    if n_new_sems:
        out_shape += [pltpu.SemaphoreType.DMA((n_new_sems,))] * 2
        out_specs += [SEM_SPEC] * 2
    first_kept = len(out_shape)
    out_shape += [pltpu.HBM(hbm_ins[i].shape, hbm_ins[i].dtype) for i in kept]
    out_specs += [HBM_SPEC] * len(kept)
    if with_token:
        out_shape.append(jax.ShapeDtypeStruct((8, 128), f32))
        out_specs.append(pl.BlockSpec(memory_space=pltpu.VMEM))

    def wrapped(*refs):
        outs = refs[n_in + n_sem + len(after):]
        body(refs[:n_in], refs[n_in:n_in + n_sem], outs[:2] if n_new_sems else ())
        if with_token:
            outs[-1][...] = jnp.zeros_like(outs[-1])

    return pl.pallas_call(
        wrapped, name=name, out_shape=tuple(out_shape),
        in_specs=[HBM_SPEC] * n_in + [SEM_SPEC] * n_sem + [pl.BlockSpec(memory_space=pl.ANY)] * len(after),
        out_specs=tuple(out_specs), input_output_aliases={i: first_kept + j for j, i in enumerate(kept)},
        compiler_params=pltpu.CompilerParams(has_side_effects=DATAFLOW_EFFECT),
    )(*[pltpu.with_memory_space_constraint(a, pltpu.HBM) for a in hbm_ins], *in_sems, *after)


def _rcopy(src, dst, sems, k, to):
    return pltpu.make_async_remote_copy(src_ref=src, dst_ref=dst, send_sem=sems[0].at[k], recv_sem=sems[1].at[k],
                                        device_id=to, device_id_type=MESH)


def _gather2_start(shards, lands, name, after):
    n = len(shards)

    def body(ins, sems_in, sems_out):
        x, y, c, me = _position()
        for a in range(n):
            for k, to in enumerate(((x, y, 1 - c), (1 - x, y, c), (x, 1 - y, c))):
                _rcopy(ins[a], ins[n + a].at[me], sems_out, 3 * a + k, to).start()

    res = _split_call(body, name, list(shards) + list(lands), range(2 * n), (), 3 * n, after, True)
    return (res[0], res[1], list(res[2:2 + n]), list(res[2 + n:2 + 2 * n])), res[-1]


def _gather2_forward1(handle, after, name):
    send_sems, recv_sems, srcs, lands = handle
    n = len(srcs)

    def body(ins, sems_in, sems_out):
        x, y, c, me = _position()
        sib, xn, yn = (x, y, 1 - c), (1 - x, y, c), (x, 1 - y, c)
        for a in range(n):
            for k, peer in enumerate((sib, xn, yn)):
                _rcopy(ins[a], ins[n + a].at[me], sems_in, 3 * a + k, peer).wait_send()
                _rcopy(ins[a], ins[n + a].at[_lin(peer)], sems_in, 3 * a + k, peer).wait_recv()
        for a in range(n):
            land = ins[n + a]
            _rcopy(land.at[_lin(xn)], land.at[_lin(xn)], sems_out, 3 * a, sib).start()
            _rcopy(land.at[_lin(yn)], land.at[_lin(yn)], sems_out, 3 * a + 1, sib).start()

            @pl.when(c == 0)
            def _():
                _rcopy(land.at[_lin(xn)], land.at[_lin(xn)], sems_out, 3 * a + 2, yn).start()

            @pl.when(c == 1)
            def _():
                _rcopy(land.at[_lin(yn)], land.at[_lin(yn)], sems_out, 3 * a + 2, xn).start()

    res = _split_call(body, name, list(srcs) + list(lands), range(n, 2 * n), (send_sems, recv_sems), 3 * n, [after], False)
    return (res[0], res[1], list(res[2:]))


def _gather2_forward2(handle, after, name):
    send_sems, recv_sems, lands = handle
    n = len(lands)

    def body(ins, sems_in, sems_out):
        x, y, c, me = _position()
        sib, dg = (x, y, 1 - c), _lin((1 - x, 1 - y, c))
        for a in range(n):
            for k, slot in enumerate((_lin((1 - x, y, 1 - c)), _lin((x, 1 - y, 1 - c)), dg)):
                done = _rcopy(ins[a].at[slot], ins[a].at[slot], sems_in, 3 * a + k, sib)
                done.wait_send()
                done.wait_recv()
        for a in range(n):
            _rcopy(ins[a].at[dg], ins[a].at[dg], sems_out, a, sib).start()

    res = _split_call(body, name, list(lands), range(n), (send_sems, recv_sems), n, [after], False)
    return (res[0], res[1], list(res[2:]))


def _gather2_wait(handle, after, name):
    send_sems, recv_sems, lands = handle
    n = len(lands)

    def body(ins, sems_in, sems_out):
        x, y, c, me = _position()
        slot = _lin((1 - x, 1 - y, 1 - c))
        for a in range(n):
            done = _rcopy(ins[a].at[slot], ins[a].at[slot], sems_in, a, (x, y, 1 - c))
            done.wait_send()
            done.wait_recv()

    return list(_split_call(body, name, list(lands), range(n), (send_sems, recv_sems), 0, [after], False))


def _own_block_filled(block, me):
    land = lax.empty((N_DEV,) + block.shape, block.dtype)
    return lax.dynamic_update_index_in_dim(land, block, me, 0)


ANY_SPEC = pl.BlockSpec(memory_space=pl.ANY)


def _mm(a, b, name, ta=False, tb=False, out_dtype=f32, after=()):
    if ta:
        k_dim, m_dim = a.shape
    else:
        m_dim, k_dim = a.shape
    if tb:
        n_dim, k2 = b.shape
    else:
        k2, n_dim = b.shape
    assert k_dim == k2, (a.shape, b.shape)
    assert a.dtype == bf16 and b.dtype == bf16
    bm = _pick(m_dim, (512, 768, 640, 256, 128))
    bn = _pick(n_dim, (512, 640, 256, 128))
    bk = k_dim if k_dim <= 2560 else _pick(k_dim, (1024, 1280, 768, 512))
    nk = k_dim // bk
    a_spec = (pl.BlockSpec((bk, bm), lambda i, j, k: (k, i)) if ta
              else pl.BlockSpec((bm, bk), lambda i, j, k: (i, k)))
    b_spec = (pl.BlockSpec((bn, bk), lambda i, j, k: (j, k)) if tb
              else pl.BlockSpec((bk, bn), lambda i, j, k: (k, j)))
    dims = (((0 if ta else 1,), (1 if tb else 0,)), ((), ()))

    n_after = len(after)

    def body_single(a_ref, b_ref, *rest):
        o_ref = rest[n_after]
        o_ref[...] = lax.dot_general(a_ref[...], b_ref[...], dims, preferred_element_type=f32).astype(o_ref.dtype)

    def body(a_ref, b_ref, *rest):
        o_ref, acc_ref = rest[n_after:]
        k = pl.program_id(2)

        @pl.when(k == 0)
        def _():
            acc_ref[...] = jnp.zeros_like(acc_ref)

        acc_ref[...] += lax.dot_general(a_ref[...], b_ref[...], dims, preferred_element_type=f32)

        @pl.when(k == nk - 1)
        def _():
            o_ref[...] = acc_ref[...].astype(o_ref.dtype)

    return pl.pallas_call(
        body_single if nk == 1 else body, name=name, out_shape=jax.ShapeDtypeStruct((m_dim, n_dim), out_dtype),
        grid=(m_dim // bm, n_dim // bn, nk), in_specs=[a_spec, b_spec] + [ANY_SPEC] * n_after,
        out_specs=pl.BlockSpec((bm, bn), lambda i, j, k: (i, j)),
        scratch_shapes=[] if nk == 1 else [pltpu.VMEM((bm, bn), f32)],
        compiler_params=_cparams("parallel", "parallel", "arbitrary"),
    )(a, b, *after)


def _rin(arr, width=None, cb=0, roff=0):
    return (arr, arr.shape[1] if width is None else width, cb, roff)


def _rowcall(fn, name, rows, tm, row_ins, par_ins, row_outs, acc_outs=(), after=()):
    nr, npar, nro, n_after = len(row_ins), len(par_ins), len(row_outs), len(after)
    in_specs, args = [], []
    for arr, width, cb, roff in row_ins:
        if roff >= 0:
            imap = lambda i, cb=cb, roff=roff: (i + roff, cb)
        else:
            imap = lambda i, cb=cb, roff=roff: (jnp.maximum(i + roff, 0), cb)
        in_specs.append(pl.BlockSpec((tm, width), imap))
        args.append(arr)
    for p in par_ins:
        in_specs.append(pl.BlockSpec(p.shape, lambda i: (0, 0)))
        args.append(p)
    out_shape, out_specs = [], []
    for width, dt in row_outs:
        out_shape.append(jax.ShapeDtypeStruct((rows, width), dt))
        out_specs.append(pl.BlockSpec((tm, width), lambda i: (i, 0)))
    for p, width in acc_outs:
        out_shape.append(jax.ShapeDtypeStruct((p, width), f32))
        out_specs.append(pl.BlockSpec((p, width), lambda i: (0, 0)))

    def body(*refs):
        i = pl.program_id(0)
        res = fn(i, *[r[...] for r in refs[:nr + npar]])
        outs = refs[nr + npar + n_after:]
        for o, v in zip(outs[:nro], res[:nro]):
            o[...] = v.astype(o.dtype)
        if acc_outs:
            @pl.when(i == 0)
            def _():
                for o in outs[nro:]:
                    o[...] = jnp.zeros_like(o)

            for o, v in zip(outs[nro:], res[nro:]):
                o[...] += v

    return pl.pallas_call(
        body, name=name, out_shape=out_shape, grid=(rows // tm,), in_specs=in_specs + [ANY_SPEC] * n_after,
        out_specs=out_specs, compiler_params=_cparams("arbitrary"),
    )(*args, *after)


def _rms(x, g):
    return x * lax.rsqrt(jnp.mean(x * x, axis=-1, keepdims=True) + EPS) * g


def _normmod(x, g, sc, sh):
    return _rms(x, g) * (1.0 + sc) + sh


def _gelu(x):
    return 0.5 * x * (1.0 + jnp.tanh(0.7978845608028654 * (x + 0.044715 * (x * x * x))))


def _sigmoid(x):
    return 0.5 * (jnp.tanh(0.5 * x) + 1.0)


def _coeff_parts(pre_a, pre_x, ba, bx, lam):
    r = _sigmoid(pre_a + ba)
    ig = _sigmoid(pre_x + bx)
    nl = -lam
    sp = jnp.maximum(nl, 0.0) + jnp.log(1.0 + jnp.exp(-jnp.abs(nl)))
    la = -RG_C * r * sp
    a = jnp.exp(la)
    one_minus_a2 = -jnp.tanh(la) * (a * a + 1.0)
    inv_m = lax.rsqrt(one_minus_a2)
    return r, ig, sp, a, one_minus_a2 * inv_m, inv_m


def _coeff(pre_a, pre_x, u, ba, bx, lam):
    _, ig, _, a, m, _ = _coeff_parts(pre_a, pre_x, ba, bx, lam)
    return a, m * (ig * u)


def _coeff_bwd(pre_a, pre_x, u, ba, bx, lam, da, db):
    r, ig, sp, a, m, inv_m = _coeff_parts(pre_a, pre_x, ba, bx, lam)
    dbu = db * u
    dig = dbu * m
    dm = dbu * ig
    dla = a * (da - dm * a * inv_m)
    dpa = dla * (-RG_C * sp) * (r * (1.0 - r))
    dpx = dig * (ig * (1.0 - ig))
    dsp = jnp.sum(dla * (-RG_C * r), axis=0, keepdims=True)
    dlam = -dsp * _sigmoid(-lam)
    return (dpa, dpx, db * m * ig, jnp.sum(dpa, axis=0, keepdims=True), jnp.sum(dpx, axis=0, keepdims=True), dlam)


SCAN_CHUNK = 256


def _scan_call(a, v, chunk_of, reverse, name, backward, after=()):
    rows, width = a.shape
    n_out = 1 if backward else 2
    nt = SCAN_CHUNK // 8

    def body(a_ref, v_ref, *rest):
        outs, state_ref = rest[len(after):-1], rest[-1]

        @pl.when(pl.program_id(0) == 0)
        def _():
            state_ref[...] = jnp.zeros_like(state_ref)

        rid = lax.broadcasted_iota(jnp.int32, (8, width), 0)
        last_row = 0 if reverse else 7

        def shift(x, s, fill):
            rolled = pltpu.roll(x, (8 - s) if reverse else s, axis=0)
            return jnp.where((rid >= 8 - s) if reverse else (rid < s), fill, rolled)

        def tile(j, st):
            t0 = pl.multiple_of((nt - 1 - j if reverse else j) * 8, 8)
            at = a_ref[pl.ds(t0, 8), :]
            coef = shift(at, 1, 1.0) if backward else at
            acc = v_ref[pl.ds(t0, 8), :]
            for s in (1, 2, 4):
                acc = coef * shift(acc, s, 0.0) + acc
                coef = coef * shift(coef, s, 1.0)
            out = coef * st + acc
            outs[0][pl.ds(t0, 8), :] = out
            last = out[last_row:last_row + 1]
            if backward:
                return at[last_row:last_row + 1] * last
            outs[1][pl.ds(t0, 8), :] = shift(out, 1, st)
            return last

        state_ref[0:1, :] = lax.fori_loop(0, nt, tile, state_ref[0:1, :])

    spec = pl.BlockSpec((SCAN_CHUNK, width), lambda t: (chunk_of(t), 0))
    return pl.pallas_call(
        body, name=name, out_shape=[jax.ShapeDtypeStruct((rows, width), f32)] * n_out,
        grid=(rows // SCAN_CHUNK,), in_specs=[spec, spec] + [ANY_SPEC] * len(after), out_specs=[spec] * n_out,
        scratch_shapes=[pltpu.VMEM((8, width), f32)],
        compiler_params=_cparams("arbitrary"),
    )(a, v, *after)


CONV_CHUNK = 256


def _fill_padded(pad_ref, src_ref, start, n):
    cb = pad_ref.shape[1]
    pad_ref[pl.ds(0, HALO), :] = jnp.zeros((HALO, cb), f32)
    pad_ref[pl.ds(HALO, n), :] = src_ref[pl.ds(start, n), :].astype(f32)
    pad_ref[pl.ds(HALO + n, HALO), :] = jnp.zeros((HALO, cb), f32)


def _dwconv_fwd(x, x_cb0, w, b, taps, pad_left, segments, cb, name, emit_bf16):
    rows = x.shape[0]
    width = w.shape[1]

    def body(x_ref, w_ref, b_ref, *rest):
        outs, xp = rest[:-1], rest[-1]
        for start, n in segments:
            _fill_padded(xp, x_ref, start, n)
            for c0 in range(0, n, CONV_CHUNK):
                acc = jnp.zeros((CONV_CHUNK, cb), f32) + b_ref[...]
                for k in range(taps):
                    acc = acc + w_ref[k:k + 1, :] * xp[pl.ds(HALO + c0 + k - pad_left, CONV_CHUNK), :]
                for o in outs:
                    o[pl.ds(start + c0, CONV_CHUNK), :] = acc.astype(o.dtype)

    out_dtypes = [f32, bf16] if emit_bf16 else [f32]
    return pl.pallas_call(
        body, name=name, out_shape=[jax.ShapeDtypeStruct((rows, width), dt) for dt in out_dtypes],
        grid=(width // cb,),
        in_specs=[pl.BlockSpec((rows, cb), lambda j: (0, j + x_cb0)), pl.BlockSpec((taps, cb), lambda j: (0, j)),
                  pl.BlockSpec((1, cb), lambda j: (0, j))],
        out_specs=[pl.BlockSpec((rows, cb), lambda j: (0, j))] * len(out_dtypes),
        scratch_shapes=[pltpu.VMEM((rows + 2 * HALO, cb), f32)],
        compiler_params=_cparams("parallel"),
    )(x, w, b)


def _dwconv_bwd(douts, x, x_cb0, w, taps, pad_left, segments, cb, name, dx_dtype):
    rows = x.shape[0]
    width = w.shape[1]
    nd = len(douts)

    def body(*refs):
        d_refs, x_ref, w_ref = refs[:nd], refs[nd], refs[nd + 1]
        dx_ref, dw_ref, db_ref, dp, dsum = refs[nd + 2:]
        dw_ref[...] = jnp.zeros_like(dw_ref)
        db_ref[...] = jnp.zeros_like(db_ref)
        if nd > 1:
            total = d_refs[0][...]
            for r in d_refs[1:]:
                total = total + r[...]
            dsum[...] = total
            d_ref = dsum
        else:
            d_ref = d_refs[0]
        for start, n in segments:
            _fill_padded(dp, d_ref, start, n)
            for c0 in range(0, n, CONV_CHUNK):
                db_ref[...] += jnp.sum(dp[pl.ds(HALO + c0, CONV_CHUNK), :], axis=0, keepdims=True)
                xchunk = x_ref[pl.ds(start + c0, CONV_CHUNK), :].astype(f32)
                acc = jnp.zeros((CONV_CHUNK, cb), f32)
                for k in range(taps):
                    shifted = dp[pl.ds(HALO + c0 + pad_left - k, CONV_CHUNK), :]
                    acc = acc + w_ref[k:k + 1, :] * shifted
                    dw_ref[k:k + 1, :] += jnp.sum(shifted * xchunk, axis=0, keepdims=True)
                dx_ref[pl.ds(start + c0, CONV_CHUNK), :] = acc.astype(dx_ref.dtype)

    dspec = pl.BlockSpec((rows, cb), lambda j: (0, j))
    return pl.pallas_call(
        body, name=name,
        out_shape=[jax.ShapeDtypeStruct((rows, width), dx_dtype), jax.ShapeDtypeStruct((taps, width), f32),
                   jax.ShapeDtypeStruct((1, width), f32)],
        grid=(width // cb,),
        in_specs=[dspec] * nd + [pl.BlockSpec((rows, cb), lambda j: (0, j + x_cb0)),
                                 pl.BlockSpec((taps, cb), lambda j: (0, j))],
        out_specs=[dspec, pl.BlockSpec((taps, cb), lambda j: (0, j)), pl.BlockSpec((1, cb), lambda j: (0, j))],
        scratch_shapes=[pltpu.VMEM((rows + 2 * HALO, cb), f32), pltpu.VMEM((rows, cb), f32)],
        compiler_params=_cparams("parallel"),
    )(*douts, x, w)


def _ada_forward(c16, w_ada, b_loc):
    def body(c_ref, w_ref, b_ref, o_ref):
        cv = c_ref[...]
        s = (cv * _sigmoid(cv)).astype(bf16)
        o_ref[0] = jnp.dot(s, w_ref[0].astype(bf16), preferred_element_type=f32) + b_ref[0]

    return pl.pallas_call(
        body, name="ada_forward", out_shape=jax.ShapeDtypeStruct((2, 16, ADA_SHARD), f32), grid=(2,),
        in_specs=[pl.BlockSpec((16, D), lambda l: (0, 0)), pl.BlockSpec((1, D, ADA_SHARD), lambda l: (l, 0, 0)),
                  pl.BlockSpec((1, 1, ADA_SHARD), lambda l: (l, 0, 0))],
        out_specs=pl.BlockSpec((1, 16, ADA_SHARD), lambda l: (l, 0, 0)),
        compiler_params=_cparams("parallel"),
    )(c16, w_ada, b_loc)


def _ada_backward(c16, g16, w_ada):
    def body(c_ref, g_ref, w_ref, dw_ref, ds_ref):
        cv = c_ref[...]
        s = (cv * _sigmoid(cv)).astype(bf16)
        g = g_ref[0].astype(bf16)
        dw_ref[0] = lax.dot_general(s, g, (((0,), (0,)), ((), ())), preferred_element_type=f32)
        ds = lax.dot_general(g, w_ref[0].astype(bf16), (((1,), (1,)), ((), ())), preferred_element_type=f32)
        cc = cv[8:9]
        sg = _sigmoid(cc)
        dsilu = sg * (1.0 + cc * (1.0 - sg))
        ds_ref[0] = jnp.zeros((8, D), f32) + jnp.sum(ds[8:16], axis=0, keepdims=True) * dsilu

    return pl.pallas_call(
        body, name="ada_backward",
        out_shape=[jax.ShapeDtypeStruct((2, D, ADA_SHARD), f32), jax.ShapeDtypeStruct((2, 8, D), f32)], grid=(2,),
        in_specs=[pl.BlockSpec((16, D), lambda l: (0, 0)), pl.BlockSpec((1, 16, ADA_SHARD), lambda l: (l, 0, 0)),
                  pl.BlockSpec((1, D, ADA_SHARD), lambda l: (l, 0, 0))],
        out_specs=[pl.BlockSpec((1, D, ADA_SHARD), lambda l: (l, 0, 0)), pl.BlockSpec((1, 8, D), lambda l: (l, 0, 0))],
        compiler_params=_cparams("parallel"),
    )(c16, g16, w_ada)


def _adamw(pieces, w, m, v, name, after=()):
    rows, cols = w.shape
    n_arr, n_after = len(pieces), len(after)
    tm = 256 if (rows % 256 == 0 and rows > 256) else rows
    counts = [p[1] for p in pieces]
    first_tiles = [(p[2] if len(p) > 2 else 0) // tm for p in pieces]
    pieces = [p[0] for p in pieces]

    def body(*refs):
        p_refs = refs[:n_arr]
        w_ref, m_ref, v_ref = refs[n_arr:n_arr + 3]
        g_ref, d_ref, nm_ref, nv_ref = refs[n_arr + 3 + n_after:]
        g = None
        for p_ref in p_refs:
            for j in range(p_ref.shape[0]):
                term = p_ref[j].astype(f32)
                g = term if g is None else g + term
        m2 = ADAM_B1 * m_ref[...] + (1.0 - ADAM_B1) * g
        v2 = ADAM_B2 * v_ref[...] + (1.0 - ADAM_B2) * (g * g)
        m_hat = m2 / (1.0 - ADAM_B1 ** ADAM_STEP)
        v_hat = v2 / (1.0 - ADAM_B2 ** ADAM_STEP)
        g_ref[...] = g
        d_ref[...] = -ADAM_LR * (m_hat / (jnp.sqrt(v_hat) + ADAM_EPS) + ADAM_WD * w_ref[...])
        nm_ref[...] = m2
        nv_ref[...] = v2

    spec = pl.BlockSpec((tm, cols), lambda i: (i, 0))
    return pl.pallas_call(
        body, name=name, out_shape=[jax.ShapeDtypeStruct((rows, cols), f32)] * 4, grid=(rows // tm,),
        in_specs=[pl.BlockSpec((cnt, tm, cols), lambda i, t=t: (0, i + t, 0)) for cnt, t in zip(counts, first_tiles)]
        + [spec, spec, spec]
        + [ANY_SPEC] * n_after,
        out_specs=[spec] * 4, compiler_params=_cparams("parallel"),
    )(*pieces, w, m, v, *after)


MLP_TM = 256
FB = F // N_DEV


def _stack_rows(vals, n):
    cols = vals[0].shape[1]
    rid = lax.broadcasted_iota(jnp.int32, (n, cols), 0)
    out = jnp.zeros((n, cols), f32)
    for k, v in enumerate(vals):
        out = jnp.where(rid == k, v, out)
    return out


N_MLP_PARAMS = 9


class _ParamRows:
    def __init__(self, ref):
        self.ref = ref

    def __getitem__(self, sl):
        return self.ref[8 * sl.start:8 * sl.start + 1, :]


def _resident(shape, imap):
    return pl.BlockSpec(shape, imap, pipeline_mode=pl.Buffered(1))


def _mlp_forward(xa, xa_roff, out_prev, par, w_in, w_out, layer, name):
    def body(xa_ref, op_ref, par_ref, win_ref, wout_ref, x1_ref, h_ref, r_ref, mo_ref, x2_ref, hn_ref):
        p = _ParamRows(par_ref)
        x1 = xa_ref[...] + p[0:1] * (op_ref[...] + p[1:2])
        h = _normmod(x1, p[2:3], p[3:4], p[4:5]).astype(bf16)
        x1_ref[...] = x1
        h_ref[...] = h
        mo = jnp.zeros((MLP_TM, D), f32)
        for j in range(N_DEV):
            r = jnp.maximum(jnp.dot(h, win_ref[j], preferred_element_type=f32), 0.0)
            r_ref[:, j * FB:(j + 1) * FB] = r.astype(bf16)
            mo = mo + jnp.dot((r * r).astype(bf16), wout_ref[j], preferred_element_type=f32)
        mo_ref[...] = mo.astype(bf16)
        x2 = x1 + p[5:6] * mo
        x2_ref[...] = x2
        hn_ref[...] = _normmod(x2, p[6:7], p[7:8], p[8:9]).astype(bf16)

    row = lambda width: pl.BlockSpec((MLP_TM, width), lambda i: (i, 0))
    return pl.pallas_call(
        body, name=name, grid=(T_LAT // MLP_TM,),
        out_shape=[jax.ShapeDtypeStruct((T_LAT, D), f32), jax.ShapeDtypeStruct((T_LAT, D), bf16),
                   jax.ShapeDtypeStruct((T_LAT, F), bf16), jax.ShapeDtypeStruct((T_LAT, D), bf16),
                   jax.ShapeDtypeStruct((T_LAT, D), f32), jax.ShapeDtypeStruct((T_LAT, D), bf16)],
        in_specs=[pl.BlockSpec((MLP_TM, D), lambda i: (i + xa_roff, 0)), row(D), pl.BlockSpec((8 * N_MLP_PARAMS, D), lambda i: (0, 0)),
                  _resident((N_DEV, None, D, FB), lambda i: (0, layer, 0, 0)),
                  _resident((N_DEV, None, FB, D), lambda i: (0, layer, 0, 0))],
        out_specs=[row(D), row(D), row(F), row(D), row(D), row(D)],
        compiler_params=_cparams("parallel"),
    )(xa, out_prev, par, w_in, w_out)


def _mlp_backward(dx2, x1, r, mo, out_prev, par, w_in, w_out, layer, name, after=()):
    nt = (((1,), (1,)), ((), ()))

    n_after = len(after)

    def body(dx2_ref, x1_ref, r_ref, mo_ref, op_ref, par_ref, win_ref, wout_ref, *rest):
        dx1_ref, dop_ref, dmo_ref, dhid_ref, acc_ref = rest[n_after:]
        p = _ParamRows(par_ref)
        dx2v = dx2_ref[...]
        dmo = (p[5:6] * dx2v).astype(bf16)
        dmo_ref[...] = dmo
        dh = jnp.zeros((MLP_TM, D), f32)
        mo = mo_ref[...].astype(f32)
        for j in range(N_DEV):
            rf = r_ref[:, j * FB:(j + 1) * FB].astype(f32)
            dact = lax.dot_general(dmo, wout_ref[j], nt, preferred_element_type=f32)
            dhid = (dact * (2.0 * rf)).astype(bf16)
            dhid_ref[:, j * FB:(j + 1) * FB] = dhid
            dh = dh + lax.dot_general(dhid, win_ref[j], nt, preferred_element_type=f32)
        x1 = x1_ref[...]
        _, vjp = jax.vjp(_normmod, x1, p[2:3], p[3:4], p[4:5])
        dx, dng, dsc, dsh = vjp(dh)
        dx1 = dx2v + dx
        dx1_ref[...] = dx1
        dop_ref[...] = (p[0:1] * dx1).astype(bf16)
        sums = _stack_rows([jnp.sum(dx1 * (op_ref[...] + p[1:2]), axis=0, keepdims=True),
                            p[0:1] * jnp.sum(dx1, axis=0, keepdims=True), dng, dsc, dsh,
                            jnp.sum(dx2v * mo, axis=0, keepdims=True)], 8)

        @pl.when(pl.program_id(0) == 0)
        def _():
            acc_ref[...] = jnp.zeros_like(acc_ref)

        acc_ref[...] += sums

    row = lambda width: pl.BlockSpec((MLP_TM, width), lambda i: (i, 0))
    return pl.pallas_call(
        body, name=name, grid=(T_LAT // MLP_TM,),
        out_shape=[jax.ShapeDtypeStruct((T_LAT, D), f32), jax.ShapeDtypeStruct((T_LAT, D), bf16),
                   jax.ShapeDtypeStruct((T_LAT, D), bf16), jax.ShapeDtypeStruct((T_LAT, F), bf16),
                   jax.ShapeDtypeStruct((8, D), f32)],
        in_specs=[row(D), row(D), row(F), row(D), row(D), pl.BlockSpec((8 * N_MLP_PARAMS, D), lambda i: (0, 0)),
                  _resident((N_DEV, None, D, FB), lambda i: (0, layer, 0, 0)),
                  _resident((N_DEV, None, FB, D), lambda i: (0, layer, 0, 0))] + [ANY_SPEC] * n_after,
        out_specs=[row(D), row(D), row(D), row(F), pl.BlockSpec((8, D), lambda i: (0, 0))],
        compiler_params=_cparams("arbitrary"),
    )(dx2, x1, r, mo, out_prev, par, w_in, w_out, *after)


def _mlp_weight_grads(h, dhid, r, dmo, layer, other, tag):
    tn = (((0,), (0,)), ((), ()))

    def body_in(h_ref, dhid_ref, *rest):
        rest[-1][...] = lax.dot_general(h_ref[...], dhid_ref[...], tn, preferred_element_type=f32).astype(bf16)

    def body_out(r_ref, dmo_ref, *rest):
        rf = r_ref[...].astype(f32)
        rest[-1][...] = lax.dot_general((rf * rf).astype(bf16), dmo_ref[...], tn,
                                        preferred_element_type=f32).astype(bf16)

    def call(body, name, operands, specs, block, prev):
        extra = [] if prev is None else [prev]
        return pl.pallas_call(
            body, name=name, grid=(N_DEV,), out_shape=jax.ShapeDtypeStruct((N_DEV, 2) + block, bf16),
            in_specs=specs + [pl.BlockSpec(memory_space=pl.ANY)] * len(extra),
            out_specs=pl.BlockSpec((None, None) + block, lambda j: (j, layer, 0, 0)),
            input_output_aliases={} if prev is None else {2: 0},
            compiler_params=_cparams("parallel"),
        )(*operands, *extra)

    dw_in = call(body_in, tag + "_mlp_in_dw", [h, dhid],
                 [_resident((T_LAT, D), lambda j: (0, 0)), pl.BlockSpec((T_LAT, FB), lambda j: (0, j))], (D, FB),
                 None if other is None else other[0])
    dw_out = call(body_out, tag + "_mlp_out_dw", [r, dmo],
                  [pl.BlockSpec((T_LAT, FB), lambda j: (0, j)), _resident((T_LAT, D), lambda j: (0, 0))], (FB, D),
                  None if other is None else other[1])
    return dw_in, dw_out


def _pos_embed():
    n_rows = T_LAT // GRID_W
    q = D // 4
    omega = 1.0 / (POS_BASE ** (jnp.arange(q, dtype=f32) / q))
    er = jnp.arange(n_rows, dtype=jnp.int32).astype(f32)[:, None] * omega[None, :]
    ec = jnp.arange(GRID_W, dtype=jnp.int32).astype(f32)[:, None] * omega[None, :]
    by_row = jnp.concatenate([jnp.sin(er), jnp.cos(er)], axis=-1)[:, None, :]
    by_col = jnp.concatenate([jnp.sin(ec), jnp.cos(ec)], axis=-1)[None, :, :]
    full = jnp.concatenate([jnp.broadcast_to(by_row, (n_rows, GRID_W, D // 2)),
                            jnp.broadcast_to(by_col, (n_rows, GRID_W, D // 2))], axis=-1)
    return full.reshape(T_LAT, D)


HALF = R // 2
BLK_PER_HALF = N_BLK // 2
N_PARTS = 4


def _gate_matrix(w_a, w_x):
    eye = jnp.eye(BLK_PER_HALF, dtype=bf16)
    cols = []
    for h in range(2):
        for d in range(2):
            for w in (w_a, w_x):
                blocks = w[d, BLK_PER_HALF * h:BLK_PER_HALF * (h + 1)].astype(bf16)
                cols.append(jnp.einsum("hij,hg->higj", blocks, eye).reshape(HALF, HALF))
    return jnp.concatenate(cols, axis=1)


def _gate_blocks(dwg, part):
    out = []
    for h in range(2):
        blk = dwg[:, (N_PARTS * h + part) * HALF:(N_PARTS * h + part + 1) * HALF]
        blk = blk.reshape(BLK_PER_HALF, BLK, BLK_PER_HALF, BLK)
        out.append(jnp.moveaxis(jnp.diagonal(blk, axis1=0, axis2=2), -1, 0))
    return jnp.concatenate(out, axis=0)


GATE_BM = 768


def _gates_dx(dpre, wg, after=()):
    rows = dpre.shape[0]
    n_after = len(after)

    def body(d_ref, w_ref, *rest):
        rest[n_after][...] = lax.dot_general(d_ref[...], w_ref[...], (((1,), (1,)), ((), ())),
                                             preferred_element_type=f32)

    return pl.pallas_call(
        body, name="l0_gates_dx", grid=(rows // GATE_BM, 2), out_shape=jax.ShapeDtypeStruct((rows, R), f32),
        in_specs=[pl.BlockSpec((GATE_BM, N_PARTS * HALF), lambda i, h: (i, h)),
                  pl.BlockSpec((HALF, N_PARTS * HALF), lambda i, h: (0, h))] + [ANY_SPEC] * n_after,
        out_specs=pl.BlockSpec((GATE_BM, HALF), lambda i, h: (i, h)),
        compiler_params=_cparams("parallel", "parallel"),
    )(dpre, wg, *after)


COEFF_TM = 768


def _dir_params(d, *params):
    specs = [pl.BlockSpec((None, 1, HALF), lambda h, i: (d, 0, h))] * len(params)
    return specs, [p.reshape(2, 1, R) for p in params]


def _gates_coeff_fwd(ub, u, wg, ba, bx, lam, d):
    rows = u.shape[0]

    def body(ub_ref, u_ref, w_ref, ba_ref, bx_ref, lam_ref, a_ref, b_ref):
        pre = jnp.dot(ub_ref[...], w_ref[...], preferred_element_type=f32)
        a, b = _coeff(pre[:, :HALF], pre[:, HALF:], u_ref[...], ba_ref[...], bx_ref[...], lam_ref[...])
        a_ref[...] = a
        b_ref[...] = b

    tile = pl.BlockSpec((COEFF_TM, HALF), lambda h, i: (i, h))
    pspecs, pargs = _dir_params(d, ba, bx, lam)
    return pl.pallas_call(
        body, name=f"l0_gates_coeff_{d}", grid=(2, rows // COEFF_TM),
        out_shape=[jax.ShapeDtypeStruct((rows, R), f32)] * 2,
        in_specs=[tile, tile, pl.BlockSpec((HALF, 2 * HALF), lambda h, i: (0, 2 * h + d))] + pspecs,
        out_specs=[tile, tile], compiler_params=_cparams("parallel", "parallel"),
    )(ub, u, wg, *pargs)


def _gates_coeff_bwd(ub, u, dh, yp, wg, ba, bx, lam, d, dpre_prev):
    rows = u.shape[0]
    n_prev = 0 if dpre_prev is None else 1

    def body(ub_ref, u_ref, dh_ref, yp_ref, w_ref, ba_ref, bx_ref, lam_ref, *rest):
        dpre_ref, du_ref, dba_ref, dbx_ref, dlam_ref = rest[n_prev:]
        pre = jnp.dot(ub_ref[...], w_ref[...], preferred_element_type=f32)
        dhv = dh_ref[...]
        dpa, dpx, du, dba, dbx, dlam = _coeff_bwd(pre[:, :HALF], pre[:, HALF:], u_ref[...], ba_ref[...], bx_ref[...],
                                                  lam_ref[...], dhv * yp_ref[...], dhv)
        dpre_ref[:, :HALF] = dpa.astype(bf16)
        dpre_ref[:, HALF:] = dpx.astype(bf16)
        du_ref[...] = du

        @pl.when(pl.program_id(1) == 0)
        def _():
            dba_ref[...] = jnp.zeros_like(dba_ref)
            dbx_ref[...] = jnp.zeros_like(dbx_ref)
            dlam_ref[...] = jnp.zeros_like(dlam_ref)

        dba_ref[...] += dba
        dbx_ref[...] += dbx
        dlam_ref[...] += dlam

    tile = pl.BlockSpec((COEFF_TM, HALF), lambda h, i: (i, h))
    acc = pl.BlockSpec((1, HALF), lambda h, i: (0, h))
    pspecs, pargs = _dir_params(d, ba, bx, lam)
    extra = [] if dpre_prev is None else [dpre_prev]
    return pl.pallas_call(
        body, name=f"l0_gates_coeff_bwd_{d}", grid=(2, rows // COEFF_TM),
        out_shape=[jax.ShapeDtypeStruct((rows, 2 * N_PARTS * HALF), bf16), jax.ShapeDtypeStruct((rows, R), f32)]
        + [jax.ShapeDtypeStruct((1, R), f32)] * 3,
        in_specs=[tile] * 4 + [pl.BlockSpec((HALF, 2 * HALF), lambda h, i: (0, 2 * h + d))] + pspecs
        + [ANY_SPEC] * n_prev,
        out_specs=[pl.BlockSpec((COEFF_TM, 2 * HALF), lambda h, i: (i, 2 * h + d)), tile, acc, acc, acc],
        input_output_aliases={8: 0} if n_prev else {}, compiler_params=_cparams("parallel", "arbitrary"),
    )(ub, u, dh, yp, wg, *pargs, *extra)


def _gates_dw(u, dpre):
    rows = u.shape[0]

    def body(u_ref, d_ref, o_ref):
        o_ref[...] = lax.dot_general(u_ref[...], d_ref[...], (((0,), (0,)), ((), ())), preferred_element_type=f32)

    return pl.pallas_call(
        body, name="l0_gates_dw", grid=(2 * N_PARTS,), out_shape=jax.ShapeDtypeStruct((HALF, 2 * N_PARTS * HALF), f32),
        in_specs=[pl.BlockSpec((rows, HALF), lambda j: (0, j // N_PARTS)), pl.BlockSpec((rows, HALF), lambda j: (0, j))],
        out_specs=pl.BlockSpec((HALF, HALF), lambda j: (0, j)), compiler_params=_cparams("parallel"),
    )(u, dpre)


N_SCAN_CHUNKS = T_ALL // SCAN_CHUNK
SCAN_FWD = lambda t: t
SCAN_FWD_BWD = lambda t: N_SCAN_CHUNKS - 1 - t
SCAN_REV = lambda t: jnp.where(t == 0, 0, N_SCAN_CHUNKS - t)
SCAN_REV_BWD = lambda t: jnp.where(t == N_SCAN_CHUNKS - 1, 0, t + 1)
CONV_SEGMENTS = ((0, T_CTX), (T_CTX, T_LAT))
FUSED_TM = 256


def _token_rows(x, ctx):
    return (jnp.concatenate([ctx, x], axis=0),
            jnp.concatenate([jnp.zeros((T_CTX, D), f32), _pos_embed()], axis=0))


def _local_step(xcat, poscat, target, mods, cmod, wts, late_weights, send_grads, reduce_loss, start_after=()):
    sh1, sc1, g1, sh2, sc2, g2 = [[mods[l, i][None] for l in range(2)] for i in range(N_MOD)]
    ng = wts["norm_g"]
    scp = jnp.concatenate([cmod[1][None], sc1[0]], axis=0)
    shp = jnp.concatenate([cmod[0][None], sh1[0]], axis=0)

    ctx_tiles = T_CTX // FUSED_TM
    nt = (((1,), (1,)), ((), ()))

    def blend(i, p):
        sel = jnp.where(i < ctx_tiles, 1.0, 0.0)
        return sel * p[0:1] + (1.0 - sel) * p[1:2]

    def f_pre0(i, xc, pos, g, scp_, shp_, w):
        x0 = xc + pos
        h = _normmod(x0, g, blend(i, scp_), blend(i, shp_)).astype(bf16)
        return x0, h, jnp.dot(h, w, preferred_element_type=f32)

    x0cat, h0, gr = _rowcall(f_pre0, "l0_prenorm_in_proj", T_ALL, FUSED_TM, [_rin(xcat), _rin(poscat)],
                             [ng[0, 0][None], scp, shp, wts["rec_w_in"]], [(D, f32), (D, bf16), (2 * R, f32)],
                             after=start_after)
    u, ub = _dwconv_fwd(gr, R // 256, wts["rec_conv_w"], wts["rec_conv_b"], 4, 1, CONV_SEGMENTS, 256,
                        "l0_conv", True)
    gate_args = (wts["gates"], wts["rec_b_a"], wts["rec_b_x"], wts["rec_lambda"])
    a0, b0 = _gates_coeff_fwd(ub, u, *gate_args, 0)
    a1, b1 = _gates_coeff_fwd(ub, u, *gate_args, 1)
    halfway = late_weights("mlp_halfway", a1)
    y0, yp0 = _scan_call(a0, b0, SCAN_FWD, False, "l0_scan_fwd", False, after=[halfway])
    y1, yp1 = _scan_call(a1, b1, SCAN_REV, True, "l0_scan_rev", False)

    wts = dict(wts, **late_weights("mlp", y1))

    def f_gate_out(i, gp, y0_, y1_, w):
        z = (_gelu(gp) * (y0_ + y1_)).astype(bf16)
        return z, jnp.dot(z, w, preferred_element_type=f32)

    zb, out0 = _rowcall(f_gate_out, "l0_gate_out_proj", T_LAT, FUSED_TM,
                        [_rin(gr, R, 0, ctx_tiles), _rin(y0, None, 0, ctx_tiles), _rin(y1, None, 0, ctx_tiles)],
                        [wts["rec_w_out"]], [(R, bf16), (D, f32)])

    zero_d = jnp.zeros((1, D), f32)

    def mlp_params(rows):
        rows = rows + [zero_d] * (N_MLP_PARAMS - len(rows))
        return jnp.concatenate([jnp.broadcast_to(r, (8, D)) for r in rows], axis=0)

    par0 = mlp_params([g1[0], zero_d, ng[0, 1][None], sc2[0], sh2[0], g2[0], ng[1, 0][None], sc1[1], sh1[1]])
    x1, h1, r0, mo0, x2, h2 = _mlp_forward(x0cat, T_CTX // MLP_TM, out0, par0, wts["mlp_w_in"], wts["mlp_w_out"], 0,
                                           "l0_mlp")

    wts = dict(wts, **late_weights("conf", x2))
    def glu(pa, pb, b1):
        return (pa + b1[:, :D]) * _sigmoid(pb + b1[:, D:])

    def f_pw1_glu(i, h_, b1, w):
        p = jnp.dot(h_, w, preferred_element_type=f32)
        return glu(p[:, :D], p[:, D:], b1), p

    zg, pw = _rowcall(f_pw1_glu, "l1_pw1_glu", T_LAT, FUSED_TM, [_rin(h2)], [wts["conf_b_pw1"], wts["conf_w_pw1"]],
                      [(D, f32), (2 * D, bf16)])
    (zc,) = _dwconv_fwd(zg, 0, wts["conf_conv_w"], wts["conf_conv_b"], 31, 15, ((0, T_LAT),), 128, "l1_conv", False)

    def ln_silu(z, lg, lb):
        mu = jnp.mean(z, axis=-1, keepdims=True)
        zc_ = z - mu
        var = jnp.mean(zc_ * zc_, axis=-1, keepdims=True)
        yv = zc_ * lax.rsqrt(var + EPS) * lg + lb
        return yv * _sigmoid(yv)

    def f_lnsilu_pw2(i, z, lg, lb, w):
        s = ln_silu(z, lg, lb).astype(bf16)
        return s, jnp.dot(s, w, preferred_element_type=f32)

    sb, out1 = _rowcall(f_lnsilu_pw2, "l1_ln_silu_pw2", T_LAT, FUSED_TM, [_rin(zc)],
                        [wts["conf_ln_g"], wts["conf_ln_b"], wts["conf_w_pw2"]], [(D, bf16), (D, f32)])
    par1 = mlp_params([g1[1], wts["conf_b_pw2"], ng[1, 1][None], sc2[1], sh2[1], g2[1]])
    x3, h3, r1, mo1, x4, _ = _mlp_forward(x2, 0, out1, par1, wts["mlp_w_in"], wts["mlp_w_out"], 1, "l1_mlp")

    def loss_fn(x4_, fg, tgt):
        err = _rms(x4_, fg) - tgt
        per_row = jnp.mean(err * err, axis=-1, keepdims=True)
        return 0.5 * jnp.sum(per_row, axis=0, keepdims=True)

    def f_head(i, x4_, tgt, fg):
        loss, vjp = jax.vjp(lambda a, e: loss_fn(a, e, tgt), x4_, fg)
        dx, dfg = vjp(jnp.ones((1, 1), f32))
        return dx, jnp.broadcast_to(loss, (1, 128)), dfg

    dx4, loss_acc, dfinal_g = _rowcall(f_head, "head", T_LAT, FUSED_TM, [_rin(x4), _rin(target)], [wts["final_g"]],
                                       [(D, f32)], [(1, 128), (1, D)])

    grads = {"final_g": dfinal_g}
    loss = reduce_loss(loss_acc[0, 0])

    dx3, dout1, dmo1, dhid1, acc1 = _mlp_backward(dx4, x3, r1, mo1, out1, par1, wts["mlp_w_in"], wts["mlp_w_out"], 1,
                                                  "l1_mlp_bwd", after=[loss.reshape(1, 1)])
    mlp_dw = _mlp_weight_grads(h3, dhid1, r1, dmo1, 1, None, "l1")
    dg1_1, db_pw2, dng11, dsc2_1, dsh2_1, dg2_1 = [acc1[k:k + 1] for k in range(6)]

    grads["conf_w_pw2"] = _mm(sb, dout1, "l1_pw2_dw", ta=True, out_dtype=bf16)
    grads["conf_b_pw2"] = db_pw2

    def f_pw2_lnsilu_bwd(i, z, dout, lg, lb, w):
        ds = lax.dot_general(dout, w, nt, preferred_element_type=f32)
        _, vjp = jax.vjp(ln_silu, z, lg, lb)
        return vjp(ds)

    dzc, dln_g, dln_b = _rowcall(f_pw2_lnsilu_bwd, "l1_pw2_ln_silu_bwd", T_LAT, FUSED_TM, [_rin(zc), _rin(dout1)],
                                 [wts["conf_ln_g"], wts["conf_ln_b"], wts["conf_w_pw2"]], [(D, f32)], [(1, D)] * 2)
    grads["conf_ln_g"], grads["conf_ln_b"] = dln_g, dln_b
    dzg, dconv_w, dconv_b = _dwconv_bwd([dzc], zg, 0, wts["conf_conv_w"], 31, 15, ((0, T_LAT),), 128,
                                        "l1_conv_bwd", f32)
    grads["conf_conv_w"], grads["conf_conv_b"] = dconv_w, dconv_b

    def f_glu_pw1_norm_bwd(i, p_, dz, x_, dxs, b1, g_, sc_, sh_, w):
        pf = p_.astype(f32)
        _, vjp = jax.vjp(glu, pf[:, :D], pf[:, D:], b1)
        da, db, db1 = vjp(dz)
        dp = jnp.concatenate([da, db], axis=1).astype(bf16)
        dh = lax.dot_general(dp, w, nt, preferred_element_type=f32)
        _, vjp = jax.vjp(_normmod, x_, g_, sc_, sh_)
        dx, dg, dsc, dsh = vjp(dh)
        return dp, dx + dxs, db1, dg, dsc, dsh

    dpw, dx2, db_pw1, dng10, dsc1_1, dsh1_1 = _rowcall(
        f_glu_pw1_norm_bwd, "l1_glu_pw1_normmod_bwd", T_LAT, FUSED_TM, [_rin(pw), _rin(dzg), _rin(x2), _rin(dx3)],
        [wts["conf_b_pw1"], ng[1, 0][None], sc1[1], sh1[1], wts["conf_w_pw1"]], [(2 * D, bf16), (D, f32)],
        [(1, 2 * D), (1, D), (1, D), (1, D)])
    grads["conf_b_pw1"] = db_pw1
    grads["conf_w_pw1"] = _mm(h2, dpw, "l1_pw1_dw", ta=True, out_dtype=bf16)
    sent = send_grads(["conf_w_pw2", "conf_w_pw1"], grads)

    dx1, dout0, dmo0, dhid0, acc0 = _mlp_backward(dx2, x1, r0, mo0, out0, par0, wts["mlp_w_in"], wts["mlp_w_out"], 0,
                                                  "l0_mlp_bwd", after=[sent])
    grads["mlp_w_in"], grads["mlp_w_out"] = _mlp_weight_grads(h1, dhid0, r0, dmo0, 0, mlp_dw, "l0")
    sent = send_grads(["mlp_w_in", "mlp_w_out"], grads)
    dg1_0, _, dng01, dsc2_0, dsh2_0, dg2_0 = [acc0[k:k + 1] for k in range(6)]

    grads["rec_w_out"] = _mm(zb, dout0, "l0_out_proj_dw", ta=True, out_dtype=bf16, after=[sent])
    sent = send_grads(["rec_w_out"], grads)

    def f_out_gate_bwd(i, gp, y0_, y1_, dout, w):
        lat = jnp.where(i < ctx_tiles, 0.0, 1.0)
        dz = lax.dot_general(dout, w, nt, preferred_element_type=f32)
        _, vjp = jax.vjp(lambda a, b: _gelu(a) * b, gp, y0_ + y1_)
        dgp, dy = vjp(dz)
        return dgp * lat, dy * lat

    dgp, dy = _rowcall(f_out_gate_bwd, "l0_out_proj_gate_bwd", T_ALL, FUSED_TM,
                       [_rin(gr, R, 0), _rin(y0), _rin(y1), _rin(dout0, None, 0, -ctx_tiles)], [wts["rec_w_out"]],
                       [(R, bf16), (R, f32)], after=[sent])
    (dh_f,) = _scan_call(a0, dy, SCAN_FWD_BWD, True, "l0_scan_fwd_bwd", True)
    (dh_r,) = _scan_call(a1, dy, SCAN_REV_BWD, False, "l0_scan_rev_bwd", True)

    dpre, du_f, *dpar_f = _gates_coeff_bwd(ub, u, dh_f, yp0, *gate_args, 0, None)
    dpre, du_r, *dpar_r = _gates_coeff_bwd(ub, u, dh_r, yp1, *gate_args, 1, dpre)
    grads["rec_b_a"], grads["rec_b_x"], grads["rec_lambda"] = [
        jnp.concatenate([f.reshape(-1), r_.reshape(-1)]).reshape(2, R) for f, r_ in zip(dpar_f, dpar_r)]
    grads["gates"] = _gates_dw(ub, dpre)
    sent = send_grads(["replicated"], grads)
    du_gates = _gates_dx(dpre, wts["gates"], after=[sent])
    drec, dconv4_w, dconv4_b = _dwconv_bwd([du_f, du_r, du_gates], gr, R // 256, wts["rec_conv_w"], 4, 1,
                                           CONV_SEGMENTS, 256, "l0_conv_bwd", bf16)
    grads["rec_conv_w"], grads["rec_conv_b"] = dconv4_w, dconv4_b
    dgr = jnp.concatenate([dgp, drec], axis=1)
    grads["rec_w_in"] = _mm(h0, dgr, "l0_in_proj_dw", ta=True, out_dtype=bf16)
    sent = send_grads(["rec_w_in"], grads)

    def f_pre0_bwd(i, x0, dgr_, dxs, g, scp_, shp_, w):
        lat = jnp.where(i < ctx_tiles, 0.0, 1.0)
        dh = lax.dot_general(dgr_, w, nt, preferred_element_type=f32)
        _, vjp = jax.vjp(lambda a, b, c, e: _normmod(a, b, blend(i, c), blend(i, e)), x0, g, scp_, shp_)
        dx, dg, dscp, dshp = vjp(dh)
        return dx + lat * dxs, dg, dscp, dshp

    dx0cat, dng00, dscp, dshp = _rowcall(
        f_pre0_bwd, "l0_in_proj_prenorm_bwd", T_ALL, FUSED_TM,
        [_rin(x0cat), _rin(dgr), _rin(dx1, None, 0, -ctx_tiles)], [ng[0, 0][None], scp, shp, wts["rec_w_in"]],
        [(D, f32)], [(1, D), (2, D), (2, D)], after=[sent])

    grads["norm_g"] = jnp.stack([jnp.concatenate([dng00, dng01], 0), jnp.concatenate([dng10, dng11], 0)])
    dmods = jnp.stack([
        jnp.concatenate([dshp[1:2], dscp[1:2], dg1_0, dsh2_0, dsc2_0, dg2_0], axis=0),
        jnp.concatenate([dsh1_1, dsc1_1, dg1_1, dsh2_1, dsc2_1, dg2_1], axis=0)])
    dcmod = jnp.concatenate([dshp[0:1], dscp[0:1]], axis=0)
    return loss, dx0cat[T_CTX:], dmods, dcmod, grads


def _unshard_cols(g):
    g = jnp.moveaxis(g, 0, -2)
    return g.reshape(g.shape[:-2] + (g.shape[-2] * g.shape[-1],))


def _shard_cols(w):
    w = w.reshape(w.shape[:-1] + (N_DEV, w.shape[-1] // N_DEV))
    return jnp.moveaxis(w, -2, 0)


def _shard_rows(w):
    return w.reshape((N_DEV, w.shape[0] // N_DEV) + w.shape[1:])


SMALL_PACK_ROWS = 64
REPL_FINAL_G_ROWS = -(-D // BLK)
REPL_ROWS = -(-(2 * 2 * N_BLK * BLK + 2 * 2 * N_BLK + REPL_FINAL_G_ROWS) // 16) * 16


def kernel(x, c, ctx, c_ctx, w_ada, b_ada, norm_g, rec_w_in, rec_conv_w, rec_conv_b, rec_lambda, rec_w_a, rec_b_a, rec_w_x, rec_b_x, rec_w_out, conf_w_pw1, conf_b_pw1, conf_conv_w, conf_conv_b, conf_ln_g, conf_ln_b, conf_w_pw2, conf_b_pw2, mlp_w_in, mlp_w_out, final_g, loss_target, m_c_ctx, m_w_ada, m_b_ada, m_norm_g, m_rec_w_in, m_rec_conv_w, m_rec_conv_b, m_rec_lambda, m_rec_w_a, m_rec_b_a, m_rec_w_x, m_rec_b_x, m_rec_w_out, m_conf_w_pw1, m_conf_b_pw1, m_conf_conv_w, m_conf_conv_b, m_conf_ln_g, m_conf_ln_b, m_conf_w_pw2, m_conf_b_pw2, m_mlp_w_in, m_mlp_w_out, m_final_g, v_c_ctx, v_w_ada, v_b_ada, v_norm_g, v_rec_w_in, v_rec_conv_w, v_rec_conv_b, v_rec_lambda, v_rec_w_a, v_rec_b_a, v_rec_w_x, v_rec_b_x, v_rec_w_out, v_conf_w_pw1, v_conf_b_pw1, v_conf_conv_w, v_conf_conv_b, v_conf_ln_g, v_conf_ln_b, v_conf_w_pw2, v_conf_b_pw2, v_mlp_w_in, v_mlp_w_out, v_final_g):
    me = 4 * lax.axis_index("x") + 2 * lax.axis_index("y") + lax.axis_index("c")
    weights = dict(c_ctx=c_ctx, w_ada=w_ada, b_ada=b_ada, norm_g=norm_g, rec_w_in=rec_w_in, rec_conv_w=rec_conv_w,
                   rec_conv_b=rec_conv_b, rec_lambda=rec_lambda, rec_w_a=rec_w_a, rec_b_a=rec_b_a, rec_w_x=rec_w_x,
                   rec_b_x=rec_b_x, rec_w_out=rec_w_out, conf_w_pw1=conf_w_pw1, conf_b_pw1=conf_b_pw1,
                   conf_conv_w=conf_conv_w, conf_conv_b=conf_conv_b, conf_ln_g=conf_ln_g, conf_ln_b=conf_ln_b,
                   conf_w_pw2=conf_w_pw2, conf_b_pw2=conf_b_pw2, mlp_w_in=mlp_w_in, mlp_w_out=mlp_w_out, final_g=final_g)
    m_in = dict(c_ctx=m_c_ctx, w_ada=m_w_ada, b_ada=m_b_ada, norm_g=m_norm_g, rec_w_in=m_rec_w_in, rec_conv_w=m_rec_conv_w,
                rec_conv_b=m_rec_conv_b, rec_lambda=m_rec_lambda, rec_w_a=m_rec_w_a, rec_b_a=m_rec_b_a, rec_w_x=m_rec_w_x,
                rec_b_x=m_rec_b_x, rec_w_out=m_rec_w_out, conf_w_pw1=m_conf_w_pw1, conf_b_pw1=m_conf_b_pw1,
                conf_conv_w=m_conf_conv_w, conf_conv_b=m_conf_conv_b, conf_ln_g=m_conf_ln_g, conf_ln_b=m_conf_ln_b,
                conf_w_pw2=m_conf_w_pw2, conf_b_pw2=m_conf_b_pw2, mlp_w_in=m_mlp_w_in, mlp_w_out=m_mlp_w_out,
                final_g=m_final_g)
    v_in = dict(c_ctx=v_c_ctx, w_ada=v_w_ada, b_ada=v_b_ada, norm_g=v_norm_g, rec_w_in=v_rec_w_in, rec_conv_w=v_rec_conv_w,
                rec_conv_b=v_rec_conv_b, rec_lambda=v_rec_lambda, rec_w_a=v_rec_w_a, rec_b_a=v_rec_b_a, rec_w_x=v_rec_w_x,
                rec_b_x=v_rec_b_x, rec_w_out=v_rec_w_out, conf_w_pw1=v_conf_w_pw1, conf_b_pw1=v_conf_b_pw1,
                conf_conv_w=v_conf_conv_w, conf_conv_b=v_conf_conv_b, conf_ln_g=v_conf_ln_g, conf_ln_b=v_conf_ln_b,
                conf_w_pw2=v_conf_w_pw2, conf_b_pw2=v_conf_b_pw2, mlp_w_in=v_mlp_w_in, mlp_w_out=v_mlp_w_out,
                final_g=v_final_g)
    names = list(weights)

    small_items = [c, norm_g, rec_conv_w, rec_lambda, conf_b_pw1, conf_conv_w, conf_conv_b, conf_ln_g, conf_ln_b,
                   conf_b_pw2]
    flat = jnp.concatenate([a.reshape(-1) for a in small_items])
    flat = jnp.pad(flat, (0, SMALL_PACK_ROWS * 128 - flat.shape[0])).reshape(SMALL_PACK_ROWS, 128)
    as_shard = lambda a: a.astype(bf16).reshape(-1, a.shape[-1])
    early_srcs = [flat, as_shard(rec_w_in[0])]
    early_handle, started = _exchange_start(early_srcs, [_own_block_filled(s, me) for s in early_srcs],
                                            "gather_early_start", False)
    zero = started[0, 0]
    gates = _gate_matrix(rec_w_a[0] + zero, rec_w_x[0] + zero)
    late_items = {"mlp": [rec_w_out[0], mlp_w_in, mlp_w_out], "conf": [conf_w_pw1[0], conf_w_pw2[0]]}
    late_shards = {g: [as_shard(a + zero) for a in items] for g, items in late_items.items()}
    late_lands = {g: [_own_block_filled(s, me) for s in shards] for g, shards in late_shards.items()}
    xcat, poscat = _token_rows(x[0] + zero, ctx[0])
    small_all, early = _exchange_wait(early_handle, [gates, xcat, poscat] + late_lands["mlp"] + late_lands["conf"],
                                      "gather_early_wait", False)

    small_all = small_all.reshape(N_DEV, -1)
    off = 0
    small = []
    for a in small_items:
        small.append(small_all[:, off:off + a.size].reshape((N_DEV,) + a.shape))
        off += a.size
    c_all, ng_all, rcw_all, lam_all, bpw1_all, ccw_all, ccb_all, lng_all, lnb_all, bpw2_all = small
    wts = {
        "norm_g": _unshard_cols(ng_all),
        "rec_conv_w": _unshard_cols(rcw_all)[0],
        "rec_lambda": _unshard_cols(lam_all)[0],
        "conf_b_pw1": _unshard_cols(bpw1_all),
        "conf_conv_w": _unshard_cols(ccw_all)[0],
        "conf_conv_b": _unshard_cols(ccb_all),
        "conf_ln_g": _unshard_cols(lng_all),
        "conf_ln_b": _unshard_cols(lnb_all),
        "conf_b_pw2": _unshard_cols(bpw2_all),
        "rec_conv_b": rec_conv_b,
        "rec_b_a": rec_b_a[0].reshape(2, R),
        "rec_b_x": rec_b_x[0].reshape(2, R),
        "final_g": final_g[None],
        "gates": gates,
    }

    c16 = jnp.concatenate([c_all[:, 0], jnp.broadcast_to(c_ctx[None], (8, D))], axis=0)
    b_loc = lax.dynamic_slice_in_dim(b_ada, me * ADA_SHARD, ADA_SHARD, axis=1)[:, None]
    ada_loc = _ada_forward(c16, w_ada, b_loc).reshape(2 * 16, ADA_SHARD)
    mods_handle, mods_started = _exchange_start([ada_loc], [_own_block_filled(ada_loc, me)], "gather_mods_start", False,
                                                [early])
    wts["rec_w_in"] = _unshard_cols(early + mods_started[0, 0].astype(bf16))
    (mods_gathered,) = _exchange_wait(mods_handle, [wts["rec_w_in"]], "gather_mods_wait", False)
    mods_gathered = mods_gathered.reshape(N_DEV, 2, 16, ADA_SHARD)
    mods_all = _unshard_cols(mods_gathered)
    mods = lax.dynamic_index_in_dim(mods_all, me, axis=1, keepdims=False).reshape(2, N_MOD, D)
    cmod = mods_all[0, 8, :2 * D].reshape(2, D)

    late_handles = {}
    late_handles["mlp"], token = _gather2_start(late_shards["mlp"], late_lands["mlp"], "gather_mlp_start",
                                                [early, mods_gathered])
    order = [token]

    def late_weights(group, after):
        if group == "mlp_halfway":
            late_handles["mlp"] = _gather2_forward1(late_handles["mlp"], after, "gather_mlp_forward1")
            return late_handles["mlp"][2][0]
        if group == "mlp":
            passed = _gather2_forward2(late_handles["mlp"], after, "gather_mlp_forward2")
            late_handles["conf"], started = _exchange_start(late_shards["conf"], late_lands["conf"], "gather_conf_start",
                                                            False, after=[passed[2][0]])
            got = _gather2_wait(passed, started, "gather_mlp_wait")
        else:
            got = _exchange_wait(late_handles[group], after, "gather_conf_wait", False)
        got = [g.reshape((N_DEV,) + a.shape) for g, a in zip(got, late_items[group])]
        if group == "mlp":
            return {"rec_w_out": got[0].reshape(R, D), "mlp_w_in": got[1], "mlp_w_out": got[2]}
        return {"conf_w_pw1": _unshard_cols(got[0]), "conf_w_pw2": got[1].reshape(D, D)}

    to_blocks = {"rec_w_in": _shard_cols, "conf_w_pw1": _shard_cols, "rec_w_out": _shard_rows, "conf_w_pw2": _shard_rows,
                 "mlp_w_in": lambda g: g, "mlp_w_out": lambda g: g}
    grad_handles = []

    repl_names = ["rec_w_a", "rec_w_x", "rec_b_a", "rec_b_x", "final_g"]

    def send_replicated(grads):
        dwg = grads["gates"]
        repl = {"rec_w_a": jnp.stack([_gate_blocks(dwg, 0), _gate_blocks(dwg, 2)]),
                "rec_w_x": jnp.stack([_gate_blocks(dwg, 1), _gate_blocks(dwg, 3)]),
                "rec_b_a": grads["rec_b_a"], "rec_b_x": grads["rec_b_x"],
                "final_g": jnp.pad(grads["final_g"], ((0, 0), (0, REPL_FINAL_G_ROWS * BLK - D)))}
        flat = jnp.concatenate([repl[n].reshape(-1, BLK) for n in repl_names], axis=0)
        flat = jnp.pad(flat, ((0, REPL_ROWS - flat.shape[0]), (0, 0))).astype(bf16)
        flat = flat.reshape(REPL_ROWS // 8, 8 * BLK)
        handle, sent = _exchange_start([flat], [_own_block_filled(flat, me)], "gather_replicated_start", False)
        grad_handles.append((["replicated"], handle))
        return sent

    def send_grads(group, grads):
        if group == ["replicated"]:
            return send_replicated(grads)
        blocks = [to_blocks[n](grads[n]) for n in group]
        blocks = [g.reshape(N_DEV, -1, g.shape[-1]) for g in blocks]
        lands = [_own_block_filled(lax.dynamic_index_in_dim(g, me, 0, keepdims=False), me) for g in blocks]
        handle, sent = _exchange_start(blocks, lands, "scatter_start_" + group[0], True)
        grad_handles.append((group, handle))
        return sent

    loss, grad_x, dmods, dcmod, grads = _local_step(
        xcat, poscat, loss_target[0], mods, cmod, wts, late_weights, send_grads,
        lambda partial: lax.psum(partial, ("x", "y", "c")), start_after=order)

    def as2d(shape):
        rows = 1
        for s in shape[:-1]:
            rows *= s
        return (rows, shape[-1])

    def whole(arr, shape):
        arr = arr.reshape((-1,) + as2d(shape))
        return (arr, arr.shape[0])

    shard_shapes = {n: weights[n].shape for n in names}
    g_out, d_out, m_out, v_out = {}, {}, {}, {}

    def adamw(n, pieces, after):
        shape = shard_shapes[n]
        r2, c2 = as2d(shape)
        g, dl, nm, nv = _adamw(pieces, weights[n].reshape(r2, c2), m_in[n].reshape(r2, c2), v_in[n].reshape(r2, c2),
                               "adamw_" + n, after=after)
        g_out[n], d_out[n], m_out[n], v_out[n] = (t.reshape(shape) for t in (g, dl, nm, nv))
        return g

    small_sharded = ["norm_g", "rec_conv_w", "rec_lambda", "conf_b_pw1", "conf_conv_w", "conf_conv_b", "conf_ln_g",
                     "conf_ln_b", "conf_b_pw2"]
    pack = jnp.concatenate([_shard_cols(grads[n]).reshape(N_DEV, -1) for n in small_sharded], axis=1)
    pack = jnp.pad(pack, ((0, 0), (0, SMALL_PACK_ROWS * 128 - pack.shape[1]))).reshape(N_DEV, SMALL_PACK_ROWS, 128)
    small_handle, token = _exchange_start(
        [pack], [_own_block_filled(lax.dynamic_index_in_dim(pack, me, 0, keepdims=False), me)], "scatter_small_start",
        True, after=[grad_x])
    dm_flat = jnp.concatenate([dmods.reshape(-1), dcmod.reshape(-1), grads["rec_conv_b"].reshape(-1)])
    dm_len = dm_flat.shape[0]
    dm_flat = jnp.pad(dm_flat, (0, 128 * 128 - dm_len)).reshape(128, 128)
    dm_handle, token = _exchange_start([dm_flat], [_own_block_filled(dm_flat, me)], "gather_dmods_start", False,
                                       after=[token])

    done = token
    for group, handle in grad_handles:
        if group == ["replicated"]:
            repl_all = _exchange_wait(handle, done, "gather_replicated_wait", False)[0]
            repl_all = repl_all.reshape(N_DEV, REPL_ROWS, BLK)
            row = 0
            for n in repl_names:
                n_rows = -(-weights[n].size // BLK)
                if as2d(shard_shapes[n]) == (n_rows, BLK) and row % 256 == 0:
                    done = adamw(n, [(repl_all, N_DEV, row)], [done])
                else:
                    got = repl_all[:, row:row + n_rows].reshape(N_DEV, -1)[:, :weights[n].size]
                    done = adamw(n, [whole(got, shard_shapes[n])], [done])
                row += n_rows
            continue
        for n, got in zip(group, _exchange_wait(handle, done, "scatter_wait_" + group[0], True)):
            done = adamw(n, [(got, N_DEV)], [done])

    dm_all = _exchange_wait(dm_handle, done, "gather_dmods_wait", False)[0].reshape(N_DEV, -1)
    dmods_all = dm_all[:, :2 * N_MOD * D].reshape(N_DEV, 2, N_MOD * D)
    dcmod_all = jnp.pad(dm_all[:, 2 * N_MOD * D:2 * N_MOD * D + 2 * D], ((0, 0), (0, (N_MOD - 2) * D)))
    g16_full = jnp.stack([jnp.concatenate([dmods_all[:, 0], dcmod_all], axis=0),
                          jnp.concatenate([dmods_all[:, 1], jnp.zeros_like(dcmod_all)], axis=0)])
    g16 = lax.dynamic_slice_in_dim(g16_full, me * ADA_SHARD, ADA_SHARD, axis=2)
    dw_ada, ds_part = _ada_backward(c16, g16, w_ada)
    ds_handle, token = _exchange_start([ds_part[0]], [_own_block_filled(ds_part[0], me)], "gather_dsilu_start", False)
    done = adamw("w_ada", [whole(dw_ada, shard_shapes["w_ada"])], [token])
    done = adamw("rec_conv_b", [whole(dm_all[:, dm_len - R:dm_len], shard_shapes["rec_conv_b"])], [done])
    db_terms = jnp.concatenate([dmods_all, jnp.stack([dcmod_all, jnp.zeros_like(dcmod_all)], axis=1)], axis=0)
    done = adamw("b_ada", [whole(db_terms, shard_shapes["b_ada"])], [done])
    pack_recv = _exchange_wait(small_handle, done, "scatter_small_wait", True)[0].reshape(N_DEV, -1)
    off = 0
    for n in small_sharded:
        size = weights[n].size
        done = adamw(n, [whole(pack_recv[:, off:off + size], shard_shapes[n])], [done])
        off += size
    ds_all = _exchange_wait(ds_handle, done, "gather_dsilu_wait", False)[0]
    adamw("c_ctx", [whole(ds_all[:, 0], shard_shapes["c_ctx"])], [])

    return (loss, grad_x[None], *[g_out[n] for n in names], *[d_out[n] for n in names],
            *[m_out[n] for n in names], *[v_out[n] for n in names])
```
